```python
import jax, jax.numpy as jnp
from jax import lax
import numpy as np

D_MODEL = 1024
BATCH = 8
SEQ = 4096
DEPTH = 1

CHUNK = 64
LRU_WIDTH = 512
LRU_HEADS = 8
LRU_HEAD_DIM = LRU_WIDTH // LRU_HEADS
LRU_CONV_WIDTH = 4
LRU_C = 8.0
GMLP_WIDTH = 512
GMLP_GROUPS = 4
GMLP_GROUP_DIM = GMLP_WIDTH // GMLP_GROUPS
GMLP_BLOCK = 128
MIX_WIDTH = LRU_WIDTH + GMLP_WIDTH
IN_COLS = 2 * LRU_WIDTH + 2 * GMLP_WIDTH
D_FF = 3 * D_MODEL
FFN_CONV_WIDTH = 3
N_MOD = 6
EPS = 1e-6

kernel_name = "hybrid_rglru_gmlp_convffn_block"


def rmsnorm(x, g):
    xf = x.astype(jnp.float32)
    y = xf * lax.rsqrt(jnp.mean(xf * xf, axis=-1, keepdims=True) + EPS)
    return (y * g.astype(jnp.float32)).astype(x.dtype)


def layernorm(x, g, b):
    xf = x.astype(jnp.float32)
    mu = jnp.mean(xf, axis=-1, keepdims=True)
    var = jnp.mean(jnp.square(xf - mu), axis=-1, keepdims=True)
    y = (xf - mu) * lax.rsqrt(var + EPS)
    return (y * g.astype(jnp.float32) + b.astype(jnp.float32)).astype(x.dtype)


def causal_depthwise_conv(x, w, b):
    k_width = w.shape[0]
    s = x.shape[1]
    xp = jnp.pad(x, ((0, 0), (k_width - 1, 0), (0, 0)))
    out = xp[:, 0:s] * w[0]
    for k in range(1, k_width):
        out = out + xp[:, k:k + s] * w[k]
    return out + b


def _lin_rec_combine(left, right):
    a1, b1 = left
    a2, b2 = right
    return a1 * a2, a2 * b1 + b2


def rg_lru_group(x_raw, gate_raw, conv_w, conv_b, w_rgate, b_rgate, w_igate, b_igate, lru_a):
    bsz, s, _ = x_raw.shape
    xc = causal_depthwise_conv(x_raw, conv_w, conv_b)
    xh = xc.reshape(bsz, s, LRU_HEADS, LRU_HEAD_DIM)
    r = jax.nn.sigmoid(jnp.einsum('bshi,hij->bshj', xh, w_rgate) + b_rgate).reshape(bsz, s, LRU_WIDTH)
    i = jax.nn.sigmoid(jnp.einsum('bshi,hij->bshj', xh, w_igate) + b_igate).reshape(bsz, s, LRU_WIDTH)
    log_a = -LRU_C * r.astype(jnp.float32) * jax.nn.softplus(-lru_a.astype(jnp.float32))
    a = jnp.exp(log_a)
    mult = jnp.sqrt(-jnp.expm1(2.0 * log_a))
    bx = mult * (i * xc).astype(jnp.float32)
    _, h = lax.associative_scan(_lin_rec_combine, (a, bx), axis=1)
    return h.astype(x_raw.dtype) * jax.nn.gelu(gate_raw)


def gmlp_group(u_raw, v_raw, v_norm_g, v_norm_b, w_spatial, b_spatial):
    bsz, s, _ = u_raw.shape
    u = jax.nn.gelu(u_raw)
    v = layernorm(jax.nn.gelu(v_raw), v_norm_g, v_norm_b)
    vb = v.reshape(bsz, s // GMLP_BLOCK, GMLP_BLOCK, GMLP_GROUPS, GMLP_GROUP_DIM)
    pos = jnp.arange(GMLP_BLOCK)
    mask = (pos[None, :] // CHUNK) <= (pos[:, None] // CHUNK)
    ws = jnp.where(mask[None], w_spatial, jnp.zeros_like(w_spatial))
    sp = jnp.einsum('gij,bnjgc->bnigc', ws, vb) + b_spatial.T[None, None, :, :, None]
    return u * sp.reshape(bsz, s, GMLP_WIDTH)


def _fwd_setup_inputs(seed: int = 0) -> dict:
    key = jax.random.key(seed)
    ks = jax.random.split(key, 32)
    L = DEPTH

    def nrm(k, shape, scale):
        return jax.random.normal(k, shape, jnp.float32) * scale

    x = nrm(ks[0], (BATCH, SEQ, D_MODEL), 1.0)
    c = nrm(ks[1], (BATCH, D_MODEL), 1.0)
    w_ada = nrm(ks[2], (L, D_MODEL, N_MOD * D_MODEL), 0.5 * D_MODEL ** -0.5)
    b_ada = nrm(ks[3], (L, N_MOD * D_MODEL), 0.02)
    g_mix_pre = 1.0 + nrm(ks[4], (L, D_MODEL), 0.02)
    g_mix_post = 1.0 + nrm(ks[5], (L, D_MODEL), 0.02)
    w_in = nrm(ks[6], (L, D_MODEL, IN_COLS), D_MODEL ** -0.5)
    conv_w = nrm(ks[7], (L, LRU_CONV_WIDTH, LRU_WIDTH), LRU_CONV_WIDTH ** -0.5)
    conv_b = nrm(ks[8], (L, LRU_WIDTH), 0.02)
    w_rgate = nrm(ks[9], (L, LRU_HEADS, LRU_HEAD_DIM, LRU_HEAD_DIM), LRU_HEAD_DIM ** -0.5)
    b_rgate = nrm(ks[10], (L, LRU_HEADS, LRU_HEAD_DIM), 0.02)
    w_igate = nrm(ks[11], (L, LRU_HEADS, LRU_HEAD_DIM, LRU_HEAD_DIM), LRU_HEAD_DIM ** -0.5)
    b_igate = nrm(ks[12], (L, LRU_HEADS, LRU_HEAD_DIM), 0.02)
    a_c = jax.random.uniform(ks[13], (L, LRU_WIDTH), jnp.float32, 0.9, 0.999)
    p = a_c ** (1.0 / LRU_C)
    lru_a = jnp.log(p) - jnp.log1p(-p)
    v_norm_g = 1.0 + nrm(ks[14], (L, GMLP_WIDTH), 0.02)
    v_norm_b = nrm(ks[15], (L, GMLP_WIDTH), 0.02)
    w_spatial = nrm(ks[16], (L, GMLP_GROUPS, GMLP_BLOCK, GMLP_BLOCK), GMLP_BLOCK ** -0.5)
    b_spatial = 1.0 + nrm(ks[17], (L, GMLP_GROUPS, GMLP_BLOCK), 0.02)
    g_lru_out = 1.0 + nrm(ks[18], (L, LRU_WIDTH), 0.02)
    g_gmlp_out = 1.0 + nrm(ks[19], (L, GMLP_WIDTH), 0.02)
    w_out = nrm(ks[20], (L, MIX_WIDTH, D_MODEL), MIX_WIDTH ** -0.5)
    g_ffn_pre = 1.0 + nrm(ks[21], (L, D_MODEL), 0.02)
    g_ffn_post = 1.0 + nrm(ks[22], (L, D_MODEL), 0.02)
    w_up = nrm(ks[23], (L, D_MODEL, 2 * D_FF), D_MODEL ** -0.5)
    ffn_conv_w = nrm(ks[24], (L, FFN_CONV_WIDTH, 2 * D_FF), FFN_CONV_WIDTH ** -0.5)
    ffn_conv_b = nrm(ks[25], (L, 2 * D_FF), 0.02)
    w_down = nrm(ks[26], (L, D_FF, D_MODEL), D_FF ** -0.5)
    return {"x": x, "c": c, "w_ada": w_ada, "b_ada": b_ada,
            "g_mix_pre": g_mix_pre, "g_mix_post": g_mix_post, "w_in": w_in,
            "conv_w": conv_w, "conv_b": conv_b, "w_rgate": w_rgate, "b_rgate": b_rgate,
            "w_igate": w_igate, "b_igate": b_igate, "lru_a": lru_a,
            "v_norm_g": v_norm_g, "v_norm_b": v_norm_b, "w_spatial": w_spatial, "b_spatial": b_spatial,
            "g_lru_out": g_lru_out, "g_gmlp_out": g_gmlp_out, "w_out": w_out,
            "g_ffn_pre": g_ffn_pre, "g_ffn_post": g_ffn_post, "w_up": w_up,
            "ffn_conv_w": ffn_conv_w, "ffn_conv_b": ffn_conv_b, "w_down": w_down}


def _fwd_reference(x, c, w_ada, b_ada, g_mix_pre, g_mix_post, w_in, conv_w, conv_b,
              w_rgate, b_rgate, w_igate, b_igate, lru_a, v_norm_g, v_norm_b,
              w_spatial, b_spatial, g_lru_out, g_gmlp_out, w_out,
              g_ffn_pre, g_ffn_post, w_up, ffn_conv_w, ffn_conv_b, w_down):
    c_act = jax.nn.silu(c)
    for l in range(DEPTH):
        mod = c_act @ w_ada[l] + b_ada[l]
        sh_m, sc_m, gt_m, sh_f, sc_f, gt_f = [m[:, None, :] for m in jnp.split(mod, N_MOD, axis=-1)]

        h = rmsnorm(x, g_mix_pre[l]) * (1.0 + sc_m) + sh_m
        z = h @ w_in[l]
        lru_x, lru_gate, g_u, g_v = jnp.split(
            z, [LRU_WIDTH, 2 * LRU_WIDTH, 2 * LRU_WIDTH + GMLP_WIDTH], axis=-1)
        y_lru = rg_lru_group(lru_x, lru_gate, conv_w[l], conv_b[l], w_rgate[l], b_rgate[l],
                             w_igate[l], b_igate[l], lru_a[l])
        y_gmlp = gmlp_group(g_u, g_v, v_norm_g[l], v_norm_b[l], w_spatial[l], b_spatial[l])
        y = jnp.concatenate([rmsnorm(y_lru, g_lru_out[l]), rmsnorm(y_gmlp, g_gmlp_out[l])], axis=-1)
        y = y @ w_out[l]
        x = x + gt_m * rmsnorm(y, g_mix_post[l])

        h = rmsnorm(x, g_ffn_pre[l]) * (1.0 + sc_f) + sh_f
        up = causal_depthwise_conv(h @ w_up[l], ffn_conv_w[l], ffn_conv_b[l])
        g_ff, v_ff = jnp.split(up, 2, axis=-1)
        y = (jax.nn.gelu(g_ff) * v_ff) @ w_down[l]
        x = x + gt_f * rmsnorm(y, g_ffn_post[l])
    return x


import jax as _jax
import jax.numpy as _jnp

TWIN_FORMAT = 'train_step'
FWD_PARAMS = ['x', 'c', 'w_ada', 'b_ada', 'g_mix_pre', 'g_mix_post', 'w_in', 'conv_w', 'conv_b', 'w_rgate', 'b_rgate', 'w_igate', 'b_igate', 'lru_a', 'v_norm_g', 'v_norm_b', 'w_spatial', 'b_spatial', 'g_lru_out', 'g_gmlp_out', 'w_out', 'g_ffn_pre', 'g_ffn_post', 'w_up', 'ffn_conv_w', 'ffn_conv_b', 'w_down']
TWIN_WEIGHTS = ['w_ada', 'b_ada', 'g_mix_pre', 'g_mix_post', 'w_in', 'conv_w', 'conv_b', 'w_rgate', 'b_rgate', 'w_igate', 'b_igate', 'lru_a', 'v_norm_g', 'v_norm_b', 'w_spatial', 'b_spatial', 'g_lru_out', 'g_gmlp_out', 'w_out', 'g_ffn_pre', 'g_ffn_post', 'w_up', 'ffn_conv_w', 'ffn_conv_b', 'w_down']
TWIN_DIFF_INPUT = 'x'
TWIN_INPUTS = ['x', 'c', 'w_ada', 'b_ada', 'g_mix_pre', 'g_mix_post', 'w_in', 'conv_w', 'conv_b', 'w_rgate', 'b_rgate', 'w_igate', 'b_igate', 'lru_a', 'v_norm_g', 'v_norm_b', 'w_spatial', 'b_spatial', 'g_lru_out', 'g_gmlp_out', 'w_out', 'g_ffn_pre', 'g_ffn_post', 'w_up', 'ffn_conv_w', 'ffn_conv_b', 'w_down', 'loss_target', 'm_w_ada', 'm_b_ada', 'm_g_mix_pre', 'm_g_mix_post', 'm_w_in', 'm_conv_w', 'm_conv_b', 'm_w_rgate', 'm_b_rgate', 'm_w_igate', 'm_b_igate', 'm_lru_a', 'm_v_norm_g', 'm_v_norm_b', 'm_w_spatial', 'm_b_spatial', 'm_g_lru_out', 'm_g_gmlp_out', 'm_w_out', 'm_g_ffn_pre', 'm_g_ffn_post', 'm_w_up', 'm_ffn_conv_w', 'm_ffn_conv_b', 'm_w_down', 'v_w_ada', 'v_b_ada', 'v_g_mix_pre', 'v_g_mix_post', 'v_w_in', 'v_conv_w', 'v_conv_b', 'v_w_rgate', 'v_b_rgate', 'v_w_igate', 'v_b_igate', 'v_lru_a', 'v_v_norm_g', 'v_v_norm_b', 'v_w_spatial', 'v_b_spatial', 'v_g_lru_out', 'v_g_gmlp_out', 'v_w_out', 'v_g_ffn_pre', 'v_g_ffn_post', 'v_w_up', 'v_ffn_conv_w', 'v_ffn_conv_b', 'v_w_down']
TWIN_OUTPUTS = ['loss', 'grad_x', 'grad_w_ada', 'grad_b_ada', 'grad_g_mix_pre', 'grad_g_mix_post', 'grad_w_in', 'grad_conv_w', 'grad_conv_b', 'grad_w_rgate', 'grad_b_rgate', 'grad_w_igate', 'grad_b_igate', 'grad_lru_a', 'grad_v_norm_g', 'grad_v_norm_b', 'grad_w_spatial', 'grad_b_spatial', 'grad_g_lru_out', 'grad_g_gmlp_out', 'grad_w_out', 'grad_g_ffn_pre', 'grad_g_ffn_post', 'grad_w_up', 'grad_ffn_conv_w', 'grad_ffn_conv_b', 'grad_w_down', 'delta_w_ada', 'delta_b_ada', 'delta_g_mix_pre', 'delta_g_mix_post', 'delta_w_in', 'delta_conv_w', 'delta_conv_b', 'delta_w_rgate', 'delta_b_rgate', 'delta_w_igate', 'delta_b_igate', 'delta_lru_a', 'delta_v_norm_g', 'delta_v_norm_b', 'delta_w_spatial', 'delta_b_spatial', 'delta_g_lru_out', 'delta_g_gmlp_out', 'delta_w_out', 'delta_g_ffn_pre', 'delta_g_ffn_post', 'delta_w_up', 'delta_ffn_conv_w', 'delta_ffn_conv_b', 'delta_w_down', 'new_m_w_ada', 'new_m_b_ada', 'new_m_g_mix_pre', 'new_m_g_mix_post', 'new_m_w_in', 'new_m_conv_w', 'new_m_conv_b', 'new_m_w_rgate', 'new_m_b_rgate', 'new_m_w_igate', 'new_m_b_igate', 'new_m_lru_a', 'new_m_v_norm_g', 'new_m_v_norm_b', 'new_m_w_spatial', 'new_m_b_spatial', 'new_m_g_lru_out', 'new_m_g_gmlp_out', 'new_m_w_out', 'new_m_g_ffn_pre', 'new_m_g_ffn_post', 'new_m_w_up', 'new_m_ffn_conv_w', 'new_m_ffn_conv_b', 'new_m_w_down', 'new_v_w_ada', 'new_v_b_ada', 'new_v_g_mix_pre', 'new_v_g_mix_post', 'new_v_w_in', 'new_v_conv_w', 'new_v_conv_b', 'new_v_w_rgate', 'new_v_b_rgate', 'new_v_w_igate', 'new_v_b_igate', 'new_v_lru_a', 'new_v_v_norm_g', 'new_v_v_norm_b', 'new_v_w_spatial', 'new_v_b_spatial', 'new_v_g_lru_out', 'new_v_g_gmlp_out', 'new_v_w_out', 'new_v_g_ffn_pre', 'new_v_g_ffn_post', 'new_v_w_up', 'new_v_ffn_conv_w', 'new_v_ffn_conv_b', 'new_v_w_down']
TWIN_LEAF_KINDS = {'loss': 'loss', 'grad_x': 'grad_x', 'grad_w_ada': 'grad_w', 'grad_b_ada': 'grad_w', 'grad_g_mix_pre': 'grad_w', 'grad_g_mix_post': 'grad_w', 'grad_w_in': 'grad_w', 'grad_conv_w': 'grad_w', 'grad_conv_b': 'grad_w', 'grad_w_rgate': 'grad_w', 'grad_b_rgate': 'grad_w', 'grad_w_igate': 'grad_w', 'grad_b_igate': 'grad_w', 'grad_lru_a': 'grad_w', 'grad_v_norm_g': 'grad_w', 'grad_v_norm_b': 'grad_w', 'grad_w_spatial': 'grad_w', 'grad_b_spatial': 'grad_w', 'grad_g_lru_out': 'grad_w', 'grad_g_gmlp_out': 'grad_w', 'grad_w_out': 'grad_w', 'grad_g_ffn_pre': 'grad_w', 'grad_g_ffn_post': 'grad_w', 'grad_w_up': 'grad_w', 'grad_ffn_conv_w': 'grad_w', 'grad_ffn_conv_b': 'grad_w', 'grad_w_down': 'grad_w', 'delta_w_ada': 'delta_w', 'delta_b_ada': 'delta_w', 'delta_g_mix_pre': 'delta_w', 'delta_g_mix_post': 'delta_w', 'delta_w_in': 'delta_w', 'delta_conv_w': 'delta_w', 'delta_conv_b': 'delta_w', 'delta_w_rgate': 'delta_w', 'delta_b_rgate': 'delta_w', 'delta_w_igate': 'delta_w', 'delta_b_igate': 'delta_w', 'delta_lru_a': 'delta_w', 'delta_v_norm_g': 'delta_w', 'delta_v_norm_b': 'delta_w', 'delta_w_spatial': 'delta_w', 'delta_b_spatial': 'delta_w', 'delta_g_lru_out': 'delta_w', 'delta_g_gmlp_out': 'delta_w', 'delta_w_out': 'delta_w', 'delta_g_ffn_pre': 'delta_w', 'delta_g_ffn_post': 'delta_w', 'delta_w_up': 'delta_w', 'delta_ffn_conv_w': 'delta_w', 'delta_ffn_conv_b': 'delta_w', 'delta_w_down': 'delta_w', 'new_m_w_ada': 'new_m', 'new_m_b_ada': 'new_m', 'new_m_g_mix_pre': 'new_m', 'new_m_g_mix_post': 'new_m', 'new_m_w_in': 'new_m', 'new_m_conv_w': 'new_m', 'new_m_conv_b': 'new_m', 'new_m_w_rgate': 'new_m', 'new_m_b_rgate': 'new_m', 'new_m_w_igate': 'new_m', 'new_m_b_igate': 'new_m', 'new_m_lru_a': 'new_m', 'new_m_v_norm_g': 'new_m', 'new_m_v_norm_b': 'new_m', 'new_m_w_spatial': 'new_m', 'new_m_b_spatial': 'new_m', 'new_m_g_lru_out': 'new_m', 'new_m_g_gmlp_out': 'new_m', 'new_m_w_out': 'new_m', 'new_m_g_ffn_pre': 'new_m', 'new_m_g_ffn_post': 'new_m', 'new_m_w_up': 'new_m', 'new_m_ffn_conv_w': 'new_m', 'new_m_ffn_conv_b': 'new_m', 'new_m_w_down': 'new_m', 'new_v_w_ada': 'new_v', 'new_v_b_ada': 'new_v', 'new_v_g_mix_pre': 'new_v', 'new_v_g_mix_post': 'new_v', 'new_v_w_in': 'new_v', 'new_v_conv_w': 'new_v', 'new_v_conv_b': 'new_v', 'new_v_w_rgate': 'new_v', 'new_v_b_rgate': 'new_v', 'new_v_w_igate': 'new_v', 'new_v_b_igate': 'new_v', 'new_v_lru_a': 'new_v', 'new_v_v_norm_g': 'new_v', 'new_v_v_norm_b': 'new_v', 'new_v_w_spatial': 'new_v', 'new_v_b_spatial': 'new_v', 'new_v_g_lru_out': 'new_v', 'new_v_g_gmlp_out': 'new_v', 'new_v_w_out': 'new_v', 'new_v_g_ffn_pre': 'new_v', 'new_v_g_ffn_post': 'new_v', 'new_v_w_up': 'new_v', 'new_v_ffn_conv_w': 'new_v', 'new_v_ffn_conv_b': 'new_v', 'new_v_w_down': 'new_v'}


def _forward(args):
    return _fwd_reference(*[args[k] for k in FWD_PARAMS])


def _output_shape():
    def fwd():
        inp = _fwd_setup_inputs(0)
        return _fwd_reference(*[inp[k] for k in FWD_PARAMS])
    out = _jax.eval_shape(fwd)
    return out.shape, out.dtype

N_MICROBATCH = 1
ADAM_LR = 0.001
ADAM_B1 = 0.9
ADAM_B2 = 0.999
ADAM_EPS = 1e-08
ADAM_WD = 0.01
ADAM_STEP = 10
PER_EXAMPLE_BATCH_AXIS = {'x': 0, 'c': 0, 'loss_target': 0}
SHARED_INPUTS = []
_WEIGHT_DTYPES = {'w_ada': _jnp.float32, 'b_ada': _jnp.float32, 'g_mix_pre': _jnp.float32, 'g_mix_post': _jnp.float32, 'w_in': _jnp.float32, 'conv_w': _jnp.float32, 'conv_b': _jnp.float32, 'w_rgate': _jnp.float32, 'b_rgate': _jnp.float32, 'w_igate': _jnp.float32, 'b_igate': _jnp.float32, 'lru_a': _jnp.float32, 'v_norm_g': _jnp.float32, 'v_norm_b': _jnp.float32, 'w_spatial': _jnp.float32, 'b_spatial': _jnp.float32, 'g_lru_out': _jnp.float32, 'g_gmlp_out': _jnp.float32, 'w_out': _jnp.float32, 'g_ffn_pre': _jnp.float32, 'g_ffn_post': _jnp.float32, 'w_up': _jnp.float32, 'ffn_conv_w': _jnp.float32, 'ffn_conv_b': _jnp.float32, 'w_down': _jnp.float32}
MOMENT_SCALE = {'w_ada': 1.290081e+00, 'b_ada': 2.799277e+00, 'g_mix_pre': 1.118253e-01, 'g_mix_post': 3.443363e+00, 'w_in': 1.762813e-01, 'conv_w': 3.088939e-01, 'conv_b': 1.089323e+00, 'w_rgate': 4.585025e-02, 'b_rgate': 4.452789e-02, 'w_igate': 8.631447e-02, 'b_igate': 1.161979e-01, 'lru_a': 1.096933e-01, 'v_norm_g': 6.472886e-02, 'v_norm_b': 5.458499e-02, 'w_spatial': 5.719889e-02, 'b_spatial': 6.185867e-02, 'g_lru_out': 3.560307e-01, 'g_gmlp_out': 2.828816e-01, 'w_out': 2.922398e-01, 'g_ffn_pre': 1.253795e-01, 'g_ffn_post': 3.375510e+00, 'w_up': 5.696696e-02, 'ffn_conv_w': 6.037852e-02, 'ffn_conv_b': 1.034610e-01, 'w_down': 1.159122e-01}


def _to_microbatches(a, axis):
    t = _jnp.moveaxis(a, axis, 0)
    t = t.reshape((N_MICROBATCH, t.shape[0] // N_MICROBATCH) + t.shape[1:])
    return _jnp.moveaxis(t, 1, axis + 1)


def setup_inputs(seed: int = 0) -> dict:
    inp = _fwd_setup_inputs(seed)
    key = _jax.random.fold_in(_jax.random.key(seed), 7919)
    shape, _ = _output_shape()
    out = dict(inp)
    out["loss_target"] = _jax.random.normal(_jax.random.fold_in(key, 0), shape, _jnp.float32)
    for i, name in enumerate(TWIN_WEIGHTS):
        w = inp[name].astype(_jnp.float32)
        if MOMENT_SCALE is None:
            s = _jnp.sqrt(_jnp.mean(_jnp.square(w)) + 1e-30)
        else:
            s = MOMENT_SCALE[name]
        km, kv = _jax.random.split(_jax.random.fold_in(key, i + 1))
        out[name] = w
        out["m_" + name] = s * _jax.random.normal(km, w.shape, _jnp.float32)
        out["v_" + name] = (s * s) * _jax.random.uniform(kv, w.shape, _jnp.float32, 0.5, 1.5)
    if N_MICROBATCH > 1:
        for name, axis in PER_EXAMPLE_BATCH_AXIS.items():
            out[name] = _to_microbatches(out[name], axis)
    return {'x': out['x'], 'c': out['c'], 'w_ada': out['w_ada'], 'b_ada': out['b_ada'], 'g_mix_pre': out['g_mix_pre'], 'g_mix_post': out['g_mix_post'], 'w_in': out['w_in'], 'conv_w': out['conv_w'], 'conv_b': out['conv_b'], 'w_rgate': out['w_rgate'], 'b_rgate': out['b_rgate'], 'w_igate': out['w_igate'], 'b_igate': out['b_igate'], 'lru_a': out['lru_a'], 'v_norm_g': out['v_norm_g'], 'v_norm_b': out['v_norm_b'], 'w_spatial': out['w_spatial'], 'b_spatial': out['b_spatial'], 'g_lru_out': out['g_lru_out'], 'g_gmlp_out': out['g_gmlp_out'], 'w_out': out['w_out'], 'g_ffn_pre': out['g_ffn_pre'], 'g_ffn_post': out['g_ffn_post'], 'w_up': out['w_up'], 'ffn_conv_w': out['ffn_conv_w'], 'ffn_conv_b': out['ffn_conv_b'], 'w_down': out['w_down'], 'loss_target': out['loss_target'], 'm_w_ada': out['m_w_ada'], 'm_b_ada': out['m_b_ada'], 'm_g_mix_pre': out['m_g_mix_pre'], 'm_g_mix_post': out['m_g_mix_post'], 'm_w_in': out['m_w_in'], 'm_conv_w': out['m_conv_w'], 'm_conv_b': out['m_conv_b'], 'm_w_rgate': out['m_w_rgate'], 'm_b_rgate': out['m_b_rgate'], 'm_w_igate': out['m_w_igate'], 'm_b_igate': out['m_b_igate'], 'm_lru_a': out['m_lru_a'], 'm_v_norm_g': out['m_v_norm_g'], 'm_v_norm_b': out['m_v_norm_b'], 'm_w_spatial': out['m_w_spatial'], 'm_b_spatial': out['m_b_spatial'], 'm_g_lru_out': out['m_g_lru_out'], 'm_g_gmlp_out': out['m_g_gmlp_out'], 'm_w_out': out['m_w_out'], 'm_g_ffn_pre': out['m_g_ffn_pre'], 'm_g_ffn_post': out['m_g_ffn_post'], 'm_w_up': out['m_w_up'], 'm_ffn_conv_w': out['m_ffn_conv_w'], 'm_ffn_conv_b': out['m_ffn_conv_b'], 'm_w_down': out['m_w_down'], 'v_w_ada': out['v_w_ada'], 'v_b_ada': out['v_b_ada'], 'v_g_mix_pre': out['v_g_mix_pre'], 'v_g_mix_post': out['v_g_mix_post'], 'v_w_in': out['v_w_in'], 'v_conv_w': out['v_conv_w'], 'v_conv_b': out['v_conv_b'], 'v_w_rgate': out['v_w_rgate'], 'v_b_rgate': out['v_b_rgate'], 'v_w_igate': out['v_w_igate'], 'v_b_igate': out['v_b_igate'], 'v_lru_a': out['v_lru_a'], 'v_v_norm_g': out['v_v_norm_g'], 'v_v_norm_b': out['v_v_norm_b'], 'v_w_spatial': out['v_w_spatial'], 'v_b_spatial': out['v_b_spatial'], 'v_g_lru_out': out['v_g_lru_out'], 'v_g_gmlp_out': out['v_g_gmlp_out'], 'v_w_out': out['v_w_out'], 'v_g_ffn_pre': out['v_g_ffn_pre'], 'v_g_ffn_post': out['v_g_ffn_post'], 'v_w_up': out['v_w_up'], 'v_ffn_conv_w': out['v_ffn_conv_w'], 'v_ffn_conv_b': out['v_ffn_conv_b'], 'v_w_down': out['v_w_down']}


def _loss(weights, diff, rest, loss_target):
    with _jax.named_scope("forward"):
        args = {**rest, TWIN_DIFF_INPUT: diff, **{k: w.astype(_WEIGHT_DTYPES[k]) for k, w in weights.items()}}
        y = _forward(args)
    with _jax.named_scope("loss_head"):
        err = _jnp.square(y.astype(_jnp.float32) - loss_target)
        return 0.5 * _jnp.sum(_jnp.mean(err, axis=-1)) if err.ndim else 0.5 * err


def _adamw(w, g, m, v):
    m = ADAM_B1 * m + (1.0 - ADAM_B1) * g
    v = ADAM_B2 * v + (1.0 - ADAM_B2) * _jnp.square(g)
    m_hat = m / (1.0 - ADAM_B1 ** ADAM_STEP)
    v_hat = v / (1.0 - ADAM_B2 ** ADAM_STEP)
    delta = -ADAM_LR * (m_hat / (_jnp.sqrt(v_hat) + ADAM_EPS) + ADAM_WD * w)
    return delta, m, v


def reference(x, c, w_ada, b_ada, g_mix_pre, g_mix_post, w_in, conv_w, conv_b, w_rgate, b_rgate, w_igate, b_igate, lru_a, v_norm_g, v_norm_b, w_spatial, b_spatial, g_lru_out, g_gmlp_out, w_out, g_ffn_pre, g_ffn_post, w_up, ffn_conv_w, ffn_conv_b, w_down, loss_target, m_w_ada, m_b_ada, m_g_mix_pre, m_g_mix_post, m_w_in, m_conv_w, m_conv_b, m_w_rgate, m_b_rgate, m_w_igate, m_b_igate, m_lru_a, m_v_norm_g, m_v_norm_b, m_w_spatial, m_b_spatial, m_g_lru_out, m_g_gmlp_out, m_w_out, m_g_ffn_pre, m_g_ffn_post, m_w_up, m_ffn_conv_w, m_ffn_conv_b, m_w_down, v_w_ada, v_b_ada, v_g_mix_pre, v_g_mix_post, v_w_in, v_conv_w, v_conv_b, v_w_rgate, v_b_rgate, v_w_igate, v_b_igate, v_lru_a, v_v_norm_g, v_v_norm_b, v_w_spatial, v_b_spatial, v_g_lru_out, v_g_gmlp_out, v_w_out, v_g_ffn_pre, v_g_ffn_post, v_w_up, v_ffn_conv_w, v_ffn_conv_b, v_w_down):
    given = dict(x=x, c=c, w_ada=w_ada, b_ada=b_ada, g_mix_pre=g_mix_pre, g_mix_post=g_mix_post, w_in=w_in, conv_w=conv_w, conv_b=conv_b, w_rgate=w_rgate, b_rgate=b_rgate, w_igate=w_igate, b_igate=b_igate, lru_a=lru_a, v_norm_g=v_norm_g, v_norm_b=v_norm_b, w_spatial=w_spatial, b_spatial=b_spatial, g_lru_out=g_lru_out, g_gmlp_out=g_gmlp_out, w_out=w_out, g_ffn_pre=g_ffn_pre, g_ffn_post=g_ffn_post, w_up=w_up, ffn_conv_w=ffn_conv_w, ffn_conv_b=ffn_conv_b, w_down=w_down, loss_target=loss_target, m_w_ada=m_w_ada, m_b_ada=m_b_ada, m_g_mix_pre=m_g_mix_pre, m_g_mix_post=m_g_mix_post, m_w_in=m_w_in, m_conv_w=m_conv_w, m_conv_b=m_conv_b, m_w_rgate=m_w_rgate, m_b_rgate=m_b_rgate, m_w_igate=m_w_igate, m_b_igate=m_b_igate, m_lru_a=m_lru_a, m_v_norm_g=m_v_norm_g, m_v_norm_b=m_v_norm_b, m_w_spatial=m_w_spatial, m_b_spatial=m_b_spatial, m_g_lru_out=m_g_lru_out, m_g_gmlp_out=m_g_gmlp_out, m_w_out=m_w_out, m_g_ffn_pre=m_g_ffn_pre, m_g_ffn_post=m_g_ffn_post, m_w_up=m_w_up, m_ffn_conv_w=m_ffn_conv_w, m_ffn_conv_b=m_ffn_conv_b, m_w_down=m_w_down, v_w_ada=v_w_ada, v_b_ada=v_b_ada, v_g_mix_pre=v_g_mix_pre, v_g_mix_post=v_g_mix_post, v_w_in=v_w_in, v_conv_w=v_conv_w, v_conv_b=v_conv_b, v_w_rgate=v_w_rgate, v_b_rgate=v_b_rgate, v_w_igate=v_w_igate, v_b_igate=v_b_igate, v_lru_a=v_lru_a, v_v_norm_g=v_v_norm_g, v_v_norm_b=v_v_norm_b, v_w_spatial=v_w_spatial, v_b_spatial=v_b_spatial, v_g_lru_out=v_g_lru_out, v_g_gmlp_out=v_g_gmlp_out, v_w_out=v_w_out, v_g_ffn_pre=v_g_ffn_pre, v_g_ffn_post=v_g_ffn_post, v_w_up=v_w_up, v_ffn_conv_w=v_ffn_conv_w, v_ffn_conv_b=v_ffn_conv_b, v_w_down=v_w_down)
    weights = {n: given[n] for n in TWIN_WEIGHTS}
    shared = {n: given[n] for n in SHARED_INPUTS}
    per_example = {n: given[n] for n in ['x', 'c']}
    grad_fn = _jax.value_and_grad(_loss, argnums=(0, 1))

    def one_microbatch(ex, loss_target):
        ex = dict(ex)
        diff = ex.pop(TWIN_DIFF_INPUT)
        return grad_fn(weights, diff, {**shared, **ex}, loss_target)

    if N_MICROBATCH == 1:
        loss, (grad_w, grad_x) = one_microbatch(per_example, given["loss_target"])
    else:
        def body(carry, xs):
            loss_sum, grad_sum = carry
            l_k, (gw_k, gx_k) = one_microbatch(xs[0], xs[1])
            with _jax.named_scope("update"):
                return (loss_sum + l_k, _jax.tree.map(_jnp.add, grad_sum, gw_k)), gx_k

        init = (_jnp.zeros((), _jnp.float32), _jax.tree.map(_jnp.zeros_like, weights))
        (loss, grad_w), grad_x = _jax.lax.scan(body, init, (per_example, given["loss_target"]))
    with _jax.named_scope("update"):
        delta_w, new_m, new_v = {}, {}, {}
        for n in TWIN_WEIGHTS:
            delta_w[n], new_m[n], new_v[n] = _adamw(weights[n], grad_w[n], given["m_" + n], given["v_" + n])
    return (loss, grad_x, *[grad_w[n] for n in TWIN_WEIGHTS], *[delta_w[n] for n in TWIN_WEIGHTS],
            *[new_m[n] for n in TWIN_WEIGHTS], *[new_v[n] for n in TWIN_WEIGHTS])
```

```python
import functools

import jax
import jax.numpy as jnp
from jax import lax
from jax.experimental import pallas as pl
from jax.experimental.pallas import tpu as pltpu

F32 = jnp.float32
BF16 = jnp.bfloat16

D_MODEL = 1024
LRU_W = 512
GMLP_W = 512
N_HEADS = 8
HEAD_DIM = 64
N_GROUPS = 4
POS_BLOCK = 128
CHUNK = 64
IN_COLS = 2048
D_FF = 3072
N_MOD = 6
N_DEV = 8
EPS = 1e-6
LRU_C = 8.0
LRU_CONV_K = 4
FFN_CONV_K = 3

ADAM_LR = 0.001
ADAM_B1 = 0.9
ADAM_B2 = 0.999
ADAM_EPS = 1e-08
ADAM_WD = 0.01
ADAM_STEP = 10

LANES = 128
SUBLANES = 8
TT_BIG = 512
TT_MIX = 256
TT_WG = 1024
FF_CW = 512
VMEM_LIMIT = 56 * 1024 * 1024

MESH = pl.DeviceIdType.MESH


def _sds(shape, dtype):
    return jax.ShapeDtypeStruct(shape, dtype)


def _cparams(sem=None):
    return pltpu.CompilerParams(dimension_semantics=sem, vmem_limit_bytes=VMEM_LIMIT)


def _whole():
    return pl.BlockSpec(memory_space=pltpu.VMEM)


def _const(shape):
    nd = len(shape)
    return pl.BlockSpec(shape, lambda *_: (0,) * nd)


def _gelu(x):
    u = 0.7978845608028654 * (x + 0.044715 * x * x * x)
    return 0.5 * x * (1.0 + jnp.tanh(u))


def _gelu_and_grad(x):
    x2 = x * x
    u = 0.7978845608028654 * (x + 0.044715 * x * x2)
    t = jnp.tanh(u)
    g = 0.5 * x * (1.0 + t)
    dg = 0.5 * (1.0 + t) + 0.5 * x * (1.0 - t * t) * 0.7978845608028654 * (1.0 + 3.0 * 0.044715 * x2)
    return g, dg


def _sigmoid(x):
    return 1.0 / (1.0 + jnp.exp(-x))


def _softplus(x):
    return jnp.maximum(x, 0.0) + jnp.log1p(jnp.exp(-jnp.abs(x)))


def _neg_expm1(x):
    series = -x * (1.0 + x * (0.5 + x * (1.0 / 6.0 + x * (1.0 / 24.0 + x * (1.0 / 120.0)))))
    return jnp.where(x > -0.1, series, 1.0 - jnp.exp(x))


def _dot(a, b):
    return jnp.dot(a.astype(BF16), b.astype(BF16), preferred_element_type=F32)


def _dot_nt(a, b):
    return lax.dot_general(a.astype(BF16), b.astype(BF16), (((1,), (1,)), ((), ())), preferred_element_type=F32)


def _dot_tn(a, b):
    return lax.dot_general(a.astype(BF16), b.astype(BF16), (((0,), (0,)), ((), ())), preferred_element_type=F32)


def _rows(shape):
    return lax.broadcasted_iota(jnp.int32, shape, 0)


def _shift_down(cur, prev8, s):
    if s == 0:
        return cur
    n = cur.shape[0]
    r = pltpu.roll(cur, s, 0)
    p = pltpu.roll(prev8, s, 0)
    top = jnp.where(_rows(p.shape) < s, p, r[0:SUBLANES])
    if n == SUBLANES:
        return top
    return jnp.concatenate([top, r[SUBLANES:]], axis=0)


def _shift_up(cur, next8, s):
    if s == 0:
        return cur
    n = cur.shape[0]
    r = pltpu.roll(cur, n - s, 0)
    q = pltpu.roll(next8, SUBLANES - s, 0)
    bot = jnp.where(_rows(q.shape) >= SUBLANES - s, q, r[n - SUBLANES:])
    if n == SUBLANES:
        return bot
    return jnp.concatenate([r[:n - SUBLANES], bot], axis=0)


def _scan_fwd(a, b):
    n = a.shape[0]
    rows = _rows(a.shape)
    s = 1
    while s < n:
        a_s = pltpu.roll(a, s, 0)
        b_s = pltpu.roll(b, s, 0)
        m = rows >= s
        b = jnp.where(m, a * b_s + b, b)
        a = jnp.where(m, a * a_s, a)
        s *= 2
    return a, b


def _scan_rev(a, b):
    n = a.shape[0]
    rows = _rows(a.shape)
    s = 1
    while s < n:
        a_s = pltpu.roll(a, n - s, 0)
        b_s = pltpu.roll(b, n - s, 0)
        m = rows < n - s
        b = jnp.where(m, b + a * b_s, b)
        a = jnp.where(m, a * a_s, a)
        s *= 2
    return a, b


def _rms(x):
    r = lax.rsqrt(jnp.mean(x * x, axis=-1, keepdims=True) + EPS)
    return x * r, r


def _rms_bwd(d_n, n, r):
    return r * (d_n - n * jnp.mean(d_n * n, axis=-1, keepdims=True))


def _colsum(x):
    return jnp.sum(x, axis=0, keepdims=True)


def _in_fwd(x, sh, sc, g, w_in):
    s_len = x.shape[0]
    tt = min(TT_BIG, s_len)

    def body(x_ref, sh_ref, sc_ref, g_ref, w_ref, z_ref, h_ref):
        n, _ = _rms(x_ref[...])
        h = (n * g_ref[...] * (1.0 + sc_ref[...]) + sh_ref[...]).astype(BF16)
        h_ref[...] = h
        z_ref[...] = jnp.dot(h, w_ref[...], preferred_element_type=F32)

    row = lambda c: pl.BlockSpec((tt, c), lambda i: (i, 0))
    vec = _const((1, D_MODEL))
    return pl.pallas_call(
        body, name="in_fwd", grid=(s_len // tt,),
        in_specs=[row(D_MODEL), vec, vec, vec, _whole()],
        out_specs=[row(IN_COLS), row(D_MODEL)],
        out_shape=[_sds((s_len, IN_COLS), F32), _sds((s_len, D_MODEL), BF16)],
        compiler_params=_cparams(("arbitrary",)),
    )(x, sh, sc, g, w_in)


def _lru_gates(xc, wr_ref, wi_ref, br, bi, sp_a):
    r = _sigmoid(_dot(xc, wr_ref[...]) + br)
    i = _sigmoid(_dot(xc, wi_ref[...]) + bi)
    la = -LRU_C * r * sp_a
    a = jnp.exp(la)
    mult = jnp.sqrt(_neg_expm1(2.0 * la))
    return r, i, a, mult


def _lru_conv(lx, prev8, cw_ref, cb):
    xc = cb + cw_ref[LRU_CONV_K - 1:LRU_CONV_K, :] * lx
    taps = []
    for k in range(LRU_CONV_K - 1):
        tap = _shift_down(lx, prev8, LRU_CONV_K - 1 - k)
        taps.append(tap)
        xc = xc + cw_ref[k:k + 1, :] * tap
    return xc, taps


def _ws_mask(transposed=False):
    i = lax.broadcasted_iota(jnp.int32, (POS_BLOCK, POS_BLOCK), 0)
    j = lax.broadcasted_iota(jnp.int32, (POS_BLOCK, POS_BLOCK), 1)
    if transposed:
        i, j = j, i
    return (j // CHUNK) <= (i // CHUNK)


def _gmlp_v(gv, vg, vb):
    av, dav = _gelu_and_grad(gv)
    mu = jnp.mean(av, axis=-1, keepdims=True)
    cen = av - mu
    rs = lax.rsqrt(jnp.mean(cen * cen, axis=-1, keepdims=True) + EPS)
    vhat = cen * rs
    return vhat * vg + vb, vhat, rs, dav


def _mix_fwd(z, conv_w, conv_b, wr_bd, wi_bd, b_r, b_i, lru_a, vn_g, vn_b, w_sp, b_sp_t, g_lru, g_gmlp):
    s_len = z.shape[0]
    tt = min(TT_MIX, s_len)
    nblk = tt // POS_BLOCK

    def body(z_ref, cw_ref, cb_ref, wr_ref, wi_ref, br_ref, bi_ref, la_ref, vg_ref, vb_ref, ws_ref, bst_ref,
             gl_ref, gg_ref, y_ref, hl_ref, prev8, hcar):
        i = pl.program_id(0)

        @pl.when(i == 0)
        def _():
            prev8[...] = jnp.zeros_like(prev8)
            hcar[...] = jnp.zeros_like(hcar)

        lx = z_ref[:, 0:LRU_W]
        gate = z_ref[:, LRU_W:2 * LRU_W]
        gu = z_ref[:, 2 * LRU_W:2 * LRU_W + GMLP_W]
        gv = z_ref[:, 2 * LRU_W + GMLP_W:]

        xc, _ = _lru_conv(lx, prev8[...], cw_ref, cb_ref[...])
        prev8[...] = lx[tt - SUBLANES:]
        sp_a = _softplus(-la_ref[...])
        _, ig, a, mult = _lru_gates(xc, wr_ref, wi_ref, br_ref[...], bi_ref[...], sp_a)
        bx = mult * (ig * xc)
        a_cum, b_cum = _scan_fwd(a, bx)
        hl = a_cum * hcar[0:1, :] + b_cum
        hcar[...] = jnp.broadcast_to(hl[tt - 1:tt, :], hcar.shape)
        hl_ref[...] = hl
        y_lru = hl * _gelu(gate)
        n_l, _ = _rms(y_lru)
        y_ref[:, 0:LRU_W] = (n_l * gl_ref[...]).astype(BF16)

        u = _gelu(gu)
        v, _, _, _ = _gmlp_v(gv, vg_ref[...], vb_ref[...])
        mask = _ws_mask()
        sp_parts = []
        for nb in range(nblk):
            row = []
            for g in range(N_GROUPS):
                wsm = jnp.where(mask, ws_ref[g], 0.0)
                vblk = v[nb * POS_BLOCK:(nb + 1) * POS_BLOCK, g * LANES:(g + 1) * LANES]
                row.append(_dot(wsm, vblk) + bst_ref[:, g:g + 1])
            sp_parts.append(jnp.concatenate(row, axis=1))
        sp = jnp.concatenate(sp_parts, axis=0) if nblk > 1 else sp_parts[0]
        n_g, _ = _rms(u * sp)
        y_ref[:, LRU_W:] = (n_g * gg_ref[...]).astype(BF16)

    row = lambda c: pl.BlockSpec((tt, c), lambda i: (i, 0))
    v512 = _const((1, LRU_W))
    return pl.pallas_call(
        body, name="mix_fwd", grid=(s_len // tt,),
        in_specs=[row(IN_COLS), _const((LRU_CONV_K, LRU_W)), v512, _whole(), _whole(), v512, v512, v512, v512, v512,
                  _whole(), _whole(), v512, v512],
        out_specs=[row(LRU_W + GMLP_W), row(LRU_W)],
        out_shape=[_sds((s_len, LRU_W + GMLP_W), BF16), _sds((s_len, LRU_W), F32)],
        scratch_shapes=[pltpu.VMEM((SUBLANES, LRU_W), F32), pltpu.VMEM((SUBLANES, LRU_W), F32)],
        compiler_params=_cparams(("arbitrary",)),
    )(z, conv_w, conv_b, wr_bd, wi_bd, b_r, b_i, lru_a, vn_g, vn_b, w_sp, b_sp_t, g_lru, g_gmlp)


def _out_up_fwd(ycat, x, w_out, g_post, gt_m, g_pre, sc_f, sh_f, w_up):
    s_len = x.shape[0]
    tt = min(TT_BIG, s_len)

    def body(yc_ref, x_ref, wo_ref, gp_ref, gt_ref, g2_ref, sc_ref, sh_ref, wu_ref, y_ref, x1_ref, h2_ref, up_ref):
        y = jnp.dot(yc_ref[...], wo_ref[...], preferred_element_type=F32)
        y_ref[...] = y
        n_y, _ = _rms(y)
        x1 = x_ref[...] + gt_ref[...] * (n_y * gp_ref[...])
        x1_ref[...] = x1
        n1, _ = _rms(x1)
        h2 = (n1 * g2_ref[...] * (1.0 + sc_ref[...]) + sh_ref[...]).astype(BF16)
        h2_ref[...] = h2
        up_ref[0] = jnp.dot(h2, wu_ref[:, 0:D_FF], preferred_element_type=F32)
        up_ref[1] = jnp.dot(h2, wu_ref[:, D_FF:], preferred_element_type=F32)

    row = lambda c: pl.BlockSpec((tt, c), lambda i: (i, 0))
    vec = _const((1, D_MODEL))
    return pl.pallas_call(
        body, name="out_up_fwd", grid=(s_len // tt,),
        in_specs=[row(D_MODEL), row(D_MODEL), _whole(), vec, vec, vec, vec, vec, _whole()],
        out_specs=[row(D_MODEL), row(D_MODEL), row(D_MODEL), pl.BlockSpec((2, tt, D_FF), lambda i: (0, i, 0))],
        out_shape=[_sds((s_len, D_MODEL), F32), _sds((s_len, D_MODEL), F32), _sds((s_len, D_MODEL), BF16),
                   _sds((2, s_len, D_FF), F32)],
        compiler_params=_cparams(("arbitrary",)),
    )(ycat, x, w_out, g_post, gt_m, g_pre, sc_f, sh_f, w_up)


def _ffn_conv(up_pre, prev8, cw_ref, cb):
    up = cb + cw_ref[FFN_CONV_K - 1:FFN_CONV_K, :] * up_pre
    taps = []
    for k in range(FFN_CONV_K - 1):
        tap = _shift_down(up_pre, prev8, FFN_CONV_K - 1 - k)
        taps.append(tap)
        up = up + cw_ref[k:k + 1, :] * tap
    return up, taps


def _ffn_fwd(up_pre, ffn_cw, ffn_cb, w_down, x1, gt_f, g_post, target):
    s_len = x1.shape[0]
    tt = min(TT_BIG, s_len)
    cw = FF_CW
    nc = D_FF // cw

    def body(up_ref, cwg_ref, cwv_ref, cbg_ref, cbv_ref, wd_ref, x1_ref, gt_ref, gp_ref, tg_ref,
             act_ref, y2_ref, dout_ref, loss_ref, prev, acc):
        i = pl.program_id(0)
        c = pl.program_id(1)

        @pl.when(i == 0)
        def _():
            prev[c] = jnp.zeros((2, SUBLANES, cw), F32)

        @pl.when((i == 0) & (c == 0))
        def _():
            loss_ref[...] = jnp.zeros_like(loss_ref)

        ug, _ = _ffn_conv(up_ref[0], prev[c, 0], cwg_ref, cbg_ref[...])
        uv, _ = _ffn_conv(up_ref[1], prev[c, 1], cwv_ref, cbv_ref[...])
        prev[c, 0] = up_ref[0, tt - SUBLANES:, :]
        prev[c, 1] = up_ref[1, tt - SUBLANES:, :]
        act = (_gelu(ug) * uv).astype(BF16)
        act_ref[...] = act
        part = jnp.dot(act, wd_ref[...], preferred_element_type=F32)

        @pl.when(c == 0)
        def _():
            acc[...] = part

        @pl.when(c > 0)
        def _():
            acc[...] += part

        @pl.when(c == nc - 1)
        def _():
            y2 = acc[...]
            y2_ref[...] = y2
            n2, _ = _rms(y2)
            out = x1_ref[...] + gt_ref[...] * (n2 * gp_ref[...])
            err = out - tg_ref[...]
            dout_ref[...] = err * (1.0 / D_MODEL)
            loss_ref[...] += jnp.broadcast_to(0.5 * jnp.sum(err * err, keepdims=True) * (1.0 / D_MODEL), loss_ref.shape)

    row = pl.BlockSpec((tt, D_MODEL), lambda i, c: (i, 0))
    vec = _const((1, D_MODEL))
    ffn_cb2 = ffn_cb.reshape(1, 2 * D_FF)
    return pl.pallas_call(
        body, name="ffn_fwd", grid=(s_len // tt, nc),
        in_specs=[pl.BlockSpec((2, tt, cw), lambda i, c: (0, i, c)),
                  pl.BlockSpec((FFN_CONV_K, cw), lambda i, c: (0, c)),
                  pl.BlockSpec((FFN_CONV_K, cw), lambda i, c: (0, c + nc)),
                  pl.BlockSpec((1, cw), lambda i, c: (0, c)),
                  pl.BlockSpec((1, cw), lambda i, c: (0, c + nc)),
                  pl.BlockSpec((cw, D_MODEL), lambda i, c: (c, 0)),
                  row, vec, vec, row],
        out_specs=[pl.BlockSpec((tt, cw), lambda i, c: (i, c)), row, row, _const((SUBLANES, LANES))],
        out_shape=[_sds((s_len, D_FF), BF16), _sds((s_len, D_MODEL), F32), _sds((s_len, D_MODEL), F32),
                   _sds((SUBLANES, LANES), F32)],
        scratch_shapes=[pltpu.VMEM((nc, 2, SUBLANES, cw), F32), pltpu.VMEM((tt, D_MODEL), F32)],
        compiler_params=_cparams(("arbitrary", "arbitrary")),
    )(up_pre, ffn_cw, ffn_cw, ffn_cb2, ffn_cb2, w_down, x1, gt_f, g_post, target)


def _ffn_bwd(dout, y2, up_pre, ffn_cw, ffn_cb, w_down, gt_f, g_post):
    s_len = dout.shape[0]
    tt = min(TT_BIG, s_len)
    nt = s_len // tt
    cw = FF_CW
    nc = D_FF // cw
    hb = tt // SUBLANES

    def body(do_ref, y2_ref, up_ref, halo_ref, cwg_ref, cwv_ref, cbg_ref, cbv_ref, wd_ref, gt_ref, gp_ref,
             dy2_ref, dup_ref, vs_ref, cs_ref, nxt, dy2s):
        i = pl.program_id(0)
        c = pl.program_id(1)
        first_tile = i == nt - 1

        @pl.when(i == 0)
        def _():
            nxt[c] = jnp.zeros((2, SUBLANES, cw), F32)
            cs_ref[c] = jnp.zeros((2, SUBLANES, cw), F32)

        @pl.when((i == 0) & (c == 0))
        def _():
            vs_ref[...] = jnp.zeros_like(vs_ref)

        @pl.when(c == 0)
        def _():
            n2, r2 = _rms(y2_ref[...])
            do = do_ref[...]
            vs_ref[0:1, :] += _colsum(do * n2 * gp_ref[...])
            vs_ref[1:2, :] += _colsum(do * gt_ref[...] * n2)
            dy2 = _rms_bwd(do * gt_ref[...] * gp_ref[...], n2, r2).astype(BF16)
            dy2s[...] = dy2
            dy2_ref[...] = dy2

        d_act = _dot_nt(dy2s[...], wd_ref[...])
        halo_g = jnp.where(first_tile, 0.0, halo_ref[0])
        halo_v = jnp.where(first_tile, 0.0, halo_ref[1])
        ug, taps_g = _ffn_conv(up_ref[0], halo_g, cwg_ref, cbg_ref[...])
        uv, taps_v = _ffn_conv(up_ref[1], halo_v, cwv_ref, cbv_ref[...])
        gl, dgl = _gelu_and_grad(ug)
        d_ug = d_act * uv * dgl
        d_uv = d_act * gl
        for half, (d_u, taps, cw_ref) in enumerate(((d_ug, taps_g, cwg_ref), (d_uv, taps_v, cwv_ref))):
            nx = nxt[c, half]
            d_pre = cw_ref[FFN_CONV_K - 1:FFN_CONV_K, :] * d_u
            sums = [None] * (FFN_CONV_K + 1)
            sums[FFN_CONV_K - 1] = _colsum(d_u * up_ref[half])
            for k in range(FFN_CONV_K - 1):
                d_pre = d_pre + cw_ref[k:k + 1, :] * _shift_up(d_u, nx, FFN_CONV_K - 1 - k)
                sums[k] = _colsum(d_u * taps[k])
            sums[FFN_CONV_K] = _colsum(d_u)
            pad = jnp.zeros((SUBLANES - FFN_CONV_K - 1, cw), F32)
            cs_ref[c, half] += jnp.concatenate(sums + [pad], axis=0)
            nxt[c, half] = d_u[0:SUBLANES]
            dup_ref[half] = d_pre.astype(BF16)

    rev = lambda i, c: (nt - 1 - i, 0)
    row = pl.BlockSpec((tt, D_MODEL), rev)
    vec = _const((1, D_MODEL))
    ffn_cb2 = ffn_cb.reshape(1, 2 * D_FF)
    outs = pl.pallas_call(
        body, name="ffn_bwd", grid=(nt, nc),
        in_specs=[row, row,
                  pl.BlockSpec((2, tt, cw), lambda i, c: (0, nt - 1 - i, c)),
                  pl.BlockSpec((2, SUBLANES, cw), lambda i, c: (0, jnp.maximum((nt - 1 - i) * hb - 1, 0), c)),
                  pl.BlockSpec((FFN_CONV_K, cw), lambda i, c: (0, c)),
                  pl.BlockSpec((FFN_CONV_K, cw), lambda i, c: (0, c + nc)),
                  pl.BlockSpec((1, cw), lambda i, c: (0, c)),
                  pl.BlockSpec((1, cw), lambda i, c: (0, c + nc)),
                  pl.BlockSpec((cw, D_MODEL), lambda i, c: (c, 0)),
                  vec, vec],
        out_specs=[row, pl.BlockSpec((2, tt, cw), lambda i, c: (0, nt - 1 - i, c)),
                   _const((SUBLANES, D_MODEL)), _const((nc, 2, SUBLANES, cw))],
        out_shape=[_sds((s_len, D_MODEL), BF16), _sds((2, s_len, D_FF), BF16), _sds((SUBLANES, D_MODEL), F32),
                   _sds((nc, 2, SUBLANES, cw), F32)],
        scratch_shapes=[pltpu.VMEM((nc, 2, SUBLANES, cw), F32), pltpu.VMEM((tt, D_MODEL), BF16)],
        compiler_params=_cparams(("arbitrary", "arbitrary")),
    )(dout, y2, up_pre, up_pre, ffn_cw, ffn_cw, ffn_cb2, ffn_cb2, w_down, gt_f, g_post)
    d_y2, d_up, vsum, cs = outs
    cs = jnp.transpose(cs, (2, 1, 0, 3)).reshape(SUBLANES, 2 * D_FF)
    return d_y2, d_up, vsum, cs


def _up_bwd(d_up, w_up, x1, dout, y, w_out, g_pre, sc_f, g_post, gt_m):
    s_len = x1.shape[0]
    tt = min(TT_BIG, s_len)

    def body(du_ref, wu_ref, x1_ref, do_ref, y_ref, wo_ref, g2_ref, sc_ref, gp_ref, gt_ref,
             dx1_ref, dy_ref, dyc_ref, vs_ref):
        @pl.when(pl.program_id(0) == 0)
        def _():
            vs_ref[...] = jnp.zeros_like(vs_ref)

        d_h2 = _dot_nt(du_ref[0], wu_ref[:, 0:D_FF]) + _dot_nt(du_ref[1], wu_ref[:, D_FF:])
        n1, r1 = _rms(x1_ref[...])
        ng = n1 * g2_ref[...]
        vs_ref[0:1, :] += _colsum(d_h2)
        vs_ref[1:2, :] += _colsum(d_h2 * ng)
        d_ng = d_h2 * (1.0 + sc_ref[...])
        vs_ref[2:3, :] += _colsum(d_ng * n1)
        d_x1 = do_ref[...] + _rms_bwd(d_ng * g2_ref[...], n1, r1)
        dx1_ref[...] = d_x1
        n_y, r_y = _rms(y_ref[...])
        vs_ref[3:4, :] += _colsum(d_x1 * n_y * gp_ref[...])
        d_on = d_x1 * gt_ref[...]
        vs_ref[4:5, :] += _colsum(d_on * n_y)
        d_y = _rms_bwd(d_on * gp_ref[...], n_y, r_y).astype(BF16)
        dy_ref[...] = d_y
        dyc_ref[...] = _dot_nt(d_y, wo_ref[...])

    row = lambda c: pl.BlockSpec((tt, c), lambda i: (i, 0))
    vec = _const((1, D_MODEL))
    return pl.pallas_call(
        body, name="up_bwd", grid=(s_len // tt,),
        in_specs=[pl.BlockSpec((2, tt, D_FF), lambda i: (0, i, 0)), _whole(), row(D_MODEL), row(D_MODEL), row(D_MODEL),
                  _whole(), vec, vec, vec, vec],
        out_specs=[row(D_MODEL), row(D_MODEL), row(LRU_W + GMLP_W), _const((SUBLANES, D_MODEL))],
        out_shape=[_sds((s_len, D_MODEL), F32), _sds((s_len, D_MODEL), BF16), _sds((s_len, LRU_W + GMLP_W), F32),
                   _sds((SUBLANES, D_MODEL), F32)],
        compiler_params=_cparams(("arbitrary",)),
    )(d_up, w_up, x1, dout, y, w_out, g_pre, sc_f, g_post, gt_m)


def _mix_bwd(d_ycat, z, hl, conv_w, conv_b, wr_bd, wi_bd, b_r, b_i, lru_a, vn_g, vn_b, w_sp, w_sp_t, b_sp_t,
             g_lru, g_gmlp):
    s_len = z.shape[0]
    tt = min(TT_MIX, s_len)
    nt = s_len // tt
    nblk = tt // POS_BLOCK
    hb = tt // SUBLANES

    def body(dyc_ref, z_ref, zh_ref, hl_ref, hh_ref, cw_ref, cb_ref, wr_ref, wi_ref, br_ref, bi_ref, la_ref,
             vg_ref, vb_ref, ws_ref, wst_ref, bst_ref, gl_ref, gg_ref,
             dz_ref, vs_ref, dcw_ref, dwr_ref, dwi_ref, dws_ref, dbs_ref, nxt_dxc, nxt_a, nxt_lam):
        i = pl.program_id(0)
        first_tile = i == nt - 1

        @pl.when(i == 0)
        def _():
            for ref in (vs_ref, dcw_ref, dwr_ref, dwi_ref, dws_ref, dbs_ref, nxt_dxc, nxt_a, nxt_lam):
                ref[...] = jnp.zeros_like(ref)

        lx = z_ref[:, 0:LRU_W]
        gate = z_ref[:, LRU_W:2 * LRU_W]
        gu = z_ref[:, 2 * LRU_W:2 * LRU_W + GMLP_W]
        gv = z_ref[:, 2 * LRU_W + GMLP_W:]
        prev8 = jnp.where(first_tile, 0.0, zh_ref[...])
        hprev8 = jnp.where(first_tile, 0.0, hh_ref[...])

        xc, taps = _lru_conv(lx, prev8, cw_ref, cb_ref[...])
        a_par = la_ref[...]
        sp_a = _softplus(-a_par)
        r, ig, a, mult = _lru_gates(xc, wr_ref, wi_ref, br_ref[...], bi_ref[...], sp_a)
        hl = hl_ref[...]
        h_prev = _shift_down(hl, hprev8, 1)
        ggate, dggate = _gelu_and_grad(gate)
        y_lru = hl * ggate
        n_l, r_l = _rms(y_lru)
        d_nl = dyc_ref[:, 0:LRU_W]
        vs_ref[6:7, :] += _colsum(d_nl * n_l)
        d_yl = _rms_bwd(d_nl * gl_ref[...], n_l, r_l)
        d_hl = d_yl * ggate
        d_gate = d_yl * hl * dggate
        a_up = _shift_up(a, nxt_a[...], 1)
        a_cum, b_cum = _scan_rev(a_up, d_hl)
        lam = b_cum + a_cum * nxt_lam[0:1, :]
        nxt_a[...] = jnp.broadcast_to(a[0:1, :], nxt_a.shape)
        nxt_lam[...] = jnp.broadcast_to(lam[0:1, :], nxt_lam.shape)
        ixc = ig * xc
        d_la = lam * h_prev * a - lam * ixc * (a * a) / mult
        d_i = lam * mult * xc
        d_xc = lam * mult * ig
        vs_ref[3:4, :] += _colsum(d_la * r) * (LRU_C * _sigmoid(-a_par))
        d_pr = d_la * (-LRU_C * sp_a) * r * (1.0 - r)
        d_pi = d_i * ig * (1.0 - ig)
        vs_ref[1:2, :] += _colsum(d_pr)
        vs_ref[2:3, :] += _colsum(d_pi)
        dwr_ref[...] += _dot_tn(xc, d_pr)
        dwi_ref[...] += _dot_tn(xc, d_pi)
        d_xc = d_xc + _dot_nt(d_pr, wr_ref[...]) + _dot_nt(d_pi, wi_ref[...])
        vs_ref[0:1, :] += _colsum(d_xc)
        nx = nxt_dxc[...]
        d_lx = cw_ref[LRU_CONV_K - 1:LRU_CONV_K, :] * d_xc
        dcw_ref[LRU_CONV_K - 1:LRU_CONV_K, :] += _colsum(d_xc * lx)
        for k in range(LRU_CONV_K - 1):
            d_lx = d_lx + cw_ref[k:k + 1, :] * _shift_up(d_xc, nx, LRU_CONV_K - 1 - k)
            dcw_ref[k:k + 1, :] += _colsum(d_xc * taps[k])
        nxt_dxc[...] = d_xc[0:SUBLANES]
        dz_ref[:, 0:LRU_W] = d_lx.astype(BF16)
        dz_ref[:, LRU_W:2 * LRU_W] = d_gate.astype(BF16)

        u, du = _gelu_and_grad(gu)
        v, vhat, rs, dav = _gmlp_v(gv, vg_ref[...], vb_ref[...])
        mask = _ws_mask()
        sp_parts = []
        for nb in range(nblk):
            rowp = []
            for g in range(N_GROUPS):
                wsm = jnp.where(mask, ws_ref[g], 0.0)
                vblk = v[nb * POS_BLOCK:(nb + 1) * POS_BLOCK, g * LANES:(g + 1) * LANES]
                rowp.append(_dot(wsm, vblk) + bst_ref[:, g:g + 1])
            sp_parts.append(jnp.concatenate(rowp, axis=1))
        sp = jnp.concatenate(sp_parts, axis=0) if nblk > 1 else sp_parts[0]
        y_g = u * sp
        n_g, r_g = _rms(y_g)
        d_ng = dyc_ref[:, LRU_W:]
        vs_ref[7:8, :] += _colsum(d_ng * n_g)
        d_yg = _rms_bwd(d_ng * gg_ref[...], n_g, r_g)
        d_gu = d_yg * sp * du
        d_sp = d_yg * u
        mask_t = _ws_mask(transposed=True)
        dv_parts = []
        for nb in range(nblk):
            rowp = []
            for g in range(N_GROUPS):
                rs_, cs_ = slice(nb * POS_BLOCK, (nb + 1) * POS_BLOCK), slice(g * LANES, (g + 1) * LANES)
                dsp_blk = d_sp[rs_, cs_]
                dbs_ref[:, g:g + 1] += jnp.sum(dsp_blk, axis=1, keepdims=True)
                dws_ref[g] += _dot_nt(dsp_blk, v[rs_, cs_])
                wsm_t = jnp.where(mask_t, wst_ref[g], 0.0)
                rowp.append(_dot(wsm_t, dsp_blk))
            dv_parts.append(jnp.concatenate(rowp, axis=1))
        d_v = jnp.concatenate(dv_parts, axis=0) if nblk > 1 else dv_parts[0]
        vs_ref[4:5, :] += _colsum(d_v * vhat)
        vs_ref[5:6, :] += _colsum(d_v)
        d_vh = d_v * vg_ref[...]
        d_av = rs * (d_vh - jnp.mean(d_vh, axis=-1, keepdims=True)
                     - vhat * jnp.mean(d_vh * vhat, axis=-1, keepdims=True))
        dz_ref[:, 2 * LRU_W:2 * LRU_W + GMLP_W] = d_gu.astype(BF16)
        dz_ref[:, 2 * LRU_W + GMLP_W:] = (d_av * dav).astype(BF16)

    rev = lambda c: pl.BlockSpec((tt, c), lambda i: (nt - 1 - i, 0))
    halo = pl.BlockSpec((SUBLANES, LRU_W), lambda i: (jnp.maximum((nt - 1 - i) * hb - 1, 0), 0))
    v512 = _const((1, LRU_W))
    return pl.pallas_call(
        body, name="mix_bwd", grid=(nt,),
        in_specs=[rev(LRU_W + GMLP_W), rev(IN_COLS), halo, rev(LRU_W), halo,
                  _const((LRU_CONV_K, LRU_W)), v512, _whole(), _whole(), v512, v512, v512, v512, v512,
                  _whole(), _whole(), _whole(), v512, v512],
        out_specs=[rev(IN_COLS), _const((SUBLANES, LRU_W)), _const((SUBLANES, LRU_W)),
                   _const((LRU_W, LRU_W)), _const((LRU_W, LRU_W)),
                   _const((N_GROUPS, POS_BLOCK, POS_BLOCK)), _const((POS_BLOCK, N_GROUPS))],
        out_shape=[_sds((s_len, IN_COLS), BF16), _sds((SUBLANES, LRU_W), F32), _sds((SUBLANES, LRU_W), F32),
                   _sds((LRU_W, LRU_W), F32), _sds((LRU_W, LRU_W), F32),
                   _sds((N_GROUPS, POS_BLOCK, POS_BLOCK), F32), _sds((POS_BLOCK, N_GROUPS), F32)],
        scratch_shapes=[pltpu.VMEM((SUBLANES, LRU_W), F32), pltpu.VMEM((SUBLANES, LRU_W), F32),
                        pltpu.VMEM((SUBLANES, LRU_W), F32)],
        compiler_params=_cparams(("arbitrary",)),
    )(d_ycat, z, z, hl, hl, conv_w, conv_b, wr_bd, wi_bd, b_r, b_i, lru_a, vn_g, vn_b, w_sp, w_sp_t, b_sp_t,
      g_lru, g_gmlp)


def _in_bwd(d_z, w_in, x, d_x1, g, sc):
    s_len = x.shape[0]
    tt = min(TT_BIG, s_len)

    def body(dz_ref, w_ref, x_ref, dx1_ref, g_ref, sc_ref, gx_ref, vs_ref):
        @pl.when(pl.program_id(0) == 0)
        def _():
            vs_ref[...] = jnp.zeros_like(vs_ref)

        d_h = _dot_nt(dz_ref[...], w_ref[...])
        n, r = _rms(x_ref[...])
        vs_ref[0:1, :] += _colsum(d_h)
        vs_ref[1:2, :] += _colsum(d_h * n * g_ref[...])
        d_ng = d_h * (1.0 + sc_ref[...])
        vs_ref[2:3, :] += _colsum(d_ng * n)
        gx_ref[...] = dx1_ref[...] + _rms_bwd(d_ng * g_ref[...], n, r)

    row = lambda c: pl.BlockSpec((tt, c), lambda i: (i, 0))
    vec = _const((1, D_MODEL))
    return pl.pallas_call(
        body, name="in_bwd", grid=(s_len // tt,),
        in_specs=[row(IN_COLS), _whole(), row(D_MODEL), row(D_MODEL), vec, vec],
        out_specs=[row(D_MODEL), _const((SUBLANES, D_MODEL))],
        out_shape=[_sds((s_len, D_MODEL), F32), _sds((SUBLANES, D_MODEL), F32)],
        compiler_params=_cparams(("arbitrary",)),
    )(d_z, w_in, x, d_x1, g, sc)


def _wgrad(a, b, tn, name):
    s_len, k_dim = a.shape
    halves = b.ndim == 3
    n_dim = b.shape[-1] * (2 if halves else 1)
    ts = min(TT_WG, s_len)
    nj = n_dim // tn
    nt = s_len // ts

    def body(a_ref, b_ref, o_ref, ob_ref):
        t = pl.program_id(1)
        part = _dot_tn(a_ref[...], b_ref[0] if halves else b_ref[...])

        @pl.when(t == 0)
        def _():
            o_ref[...] = part

        @pl.when(t > 0)
        def _():
            o_ref[...] += part

        @pl.when(t == nt - 1)
        def _():
            ob_ref[...] = o_ref[...].astype(BF16)

    if halves:
        per_half = nj // 2
        b_spec = pl.BlockSpec((1, ts, tn), lambda j, t: (j // per_half, t, j % per_half))
    else:
        b_spec = pl.BlockSpec((ts, tn), lambda j, t: (t, j))
    o_spec = pl.BlockSpec((k_dim, tn), lambda j, t: (0, j))
    return pl.pallas_call(
        body, name=name, grid=(nj, nt),
        in_specs=[pl.BlockSpec((ts, k_dim), lambda j, t: (t, 0)), b_spec],
        out_specs=[o_spec, o_spec],
        out_shape=[_sds((k_dim, n_dim), F32), _sds((k_dim, n_dim), BF16)],
        compiler_params=_cparams(("arbitrary", "arbitrary")),
    )(a, b)


def _adam_math(w, g, m, v):
    m = ADAM_B1 * m + (1.0 - ADAM_B1) * g
    v = ADAM_B2 * v + (1.0 - ADAM_B2) * (g * g)
    m_hat = m / (1.0 - ADAM_B1 ** ADAM_STEP)
    v_hat = v / (1.0 - ADAM_B2 ** ADAM_STEP)
    delta = -ADAM_LR * (m_hat / (jnp.sqrt(v_hat) + ADAM_EPS) + ADAM_WD * w)
    return delta, m, v


def _row_tile(rows, cols, n_f32_arrays):
    budget = VMEM_LIMIT // 2
    tr = rows
    while tr % 2 == 0 and tr // 2 >= SUBLANES and (tr // 2) % SUBLANES == 0 and tr * cols * 4 * n_f32_arrays * 2 > budget:
        tr //= 2
    return tr


def _adamw_sum(w, g_own, recv, m, v, name):
    rows, cols = w.shape
    n_recv = recv.shape[0]
    tr = _row_tile(rows, cols, 10)

    def body(w_ref, g_ref, r_ref, m_ref, v_ref, go_ref, d_ref, mo_ref, vo_ref):
        g = g_ref[...]
        for k in range(n_recv):
            g = g + r_ref[k].astype(F32)
        go_ref[...] = g
        d_ref[...], mo_ref[...], vo_ref[...] = _adam_math(w_ref[...], g, m_ref[...], v_ref[...])

    blk = pl.BlockSpec((tr, cols), lambda i: (i, 0))
    return pl.pallas_call(
        body, name=name, grid=(rows // tr,),
        in_specs=[blk, blk, pl.BlockSpec((n_recv, tr, cols), lambda i: (0, i, 0)), blk, blk],
        out_specs=[blk] * 4, out_shape=[_sds((rows, cols), F32)] * 4,
        compiler_params=_cparams(("arbitrary",)),
    )(w, g_own, recv, m, v)


def _adamw_gathered(w, parts, m, v, name):
    rows, cols = w.shape

    def body(w_ref, p_ref, m_ref, v_ref, go_ref, d_ref, mo_ref, vo_ref):
        g = p_ref[0]
        for k in range(1, N_DEV):
            g = g + p_ref[k]
        go_ref[...] = g
        d_ref[...], mo_ref[...], vo_ref[...] = _adam_math(w_ref[...], g, m_ref[...], v_ref[...])

    return pl.pallas_call(
        body, name=name, out_shape=[_sds((rows, cols), F32)] * 4,
        in_specs=[_whole()] * 4, out_specs=[_whole()] * 4,
        compiler_params=_cparams(),
    )(w, parts, m, v)


def _adamw_plain(w, g, m, v, name):
    rows, cols = w.shape

    def body(w_ref, g_ref, m_ref, v_ref, d_ref, mo_ref, vo_ref):
        d_ref[...], mo_ref[...], vo_ref[...] = _adam_math(w_ref[...], g_ref[...], m_ref[...], v_ref[...])

    return pl.pallas_call(
        body, name=name, out_shape=[_sds((rows, cols), F32)] * 3,
        in_specs=[_whole()] * 4, out_specs=[_whole()] * 3,
        compiler_params=_cparams(),
    )(w, g, m, v)


def _adamw_wada(c_all_t, dmod_cols, w, m, v):
    rows, cols = w.shape

    def body(ct_ref, dm_ref, w_ref, m_ref, v_ref, go_ref, d_ref, mo_ref, vo_ref):
        ct = ct_ref[...]
        ca = ct * _sigmoid(ct)
        g = jnp.dot(ca, dm_ref[...], preferred_element_type=F32, precision=lax.Precision.HIGHEST)
        go_ref[...] = g
        d_ref[...], mo_ref[...], vo_ref[...] = _adam_math(w_ref[...], g, m_ref[...], v_ref[...])

    return pl.pallas_call(
        body, name="adamw_w_ada", out_shape=[_sds((rows, cols), F32)] * 4,
        in_specs=[_whole()] * 5, out_specs=[_whole()] * 4,
        compiler_params=_cparams(),
    )(c_all_t, dmod_cols, w, m, v)


def _mod_part(c_all, w_ada, b_cols):
    def body(c_ref, w_ref, b_ref, o_ref):
        cv = c_ref[...]
        ca = cv * _sigmoid(cv)
        o_ref[...] = jnp.dot(ca, w_ref[...], preferred_element_type=F32, precision=lax.Precision.HIGHEST) + b_ref[...]

    return pl.pallas_call(
        body, name="mod_part", out_shape=_sds((N_DEV, w_ada.shape[1]), F32),
        in_specs=[_whole()] * 3, out_specs=_whole(), compiler_params=_cparams(),
    )(c_all, w_ada, b_cols)


def _my_pos():
    return lax.axis_index("x"), lax.axis_index("y"), lax.axis_index("c")


def _flip(pos, k):
    x, y, c = pos
    return (1 - x if k & 4 else x, 1 - y if k & 2 else y, 1 - c if k & 1 else c)


def _dev_index(pos):
    x, y, c = pos
    return 4 * x + 2 * y + c


def _gather_small(v, name):
    rows, cols = v.shape

    def body(v_ref, out_ref, send_sems, recv_sems):
        me = _my_pos()

        def slot(pos):
            return out_ref.at[pl.ds(pl.multiple_of(_dev_index(pos) * rows, SUBLANES), rows), :]

        def copy(k):
            return pltpu.make_async_remote_copy(
                src_ref=v_ref, dst_ref=slot(me), send_sem=send_sems.at[k - 1], recv_sem=recv_sems.at[k - 1],
                device_id=_flip(me, k), device_id_type=MESH)

        sends = [copy(k) for k in range(1, N_DEV)]
        for cp in sends:
            cp.start()
        out_ref[pl.ds(pl.multiple_of(_dev_index(me) * rows, SUBLANES), rows), :] = v_ref[...]
        for k in range(1, N_DEV):
            pltpu.make_async_remote_copy(
                src_ref=v_ref, dst_ref=slot(_flip(me, k)), send_sem=send_sems.at[k - 1], recv_sem=recv_sems.at[k - 1],
                device_id=_flip(me, k), device_id_type=MESH).wait_recv()
        for cp in sends:
            cp.wait_send()

    return pl.pallas_call(
        body, name=name, out_shape=_sds((N_DEV * rows, cols), F32),
        in_specs=[_whole()], out_specs=_whole(),
        scratch_shapes=[pltpu.SemaphoreType.DMA((N_DEV - 1,)), pltpu.SemaphoreType.DMA((N_DEV - 1,))],
        compiler_params=pltpu.CompilerParams(vmem_limit_bytes=VMEM_LIMIT),
    )(v)


def _gather_weights(w_in_s, w_out_s, w_up_s, w_down_s):
    shards = (w_in_s, w_out_s, w_up_s, w_down_s)
    col_sharded = (True, False, True, False)
    n_w = len(shards)
    full_shapes = [(s.shape[0], s.shape[1] * N_DEV) if cs else (s.shape[0] * N_DEV, s.shape[1])
                   for s, cs in zip(shards, col_sharded)]

    def body(*refs):
        in_refs = refs[:n_w]
        out_refs = refs[n_w:2 * n_w]
        stage = refs[2 * n_w:3 * n_w]
        send_sems, recv_sems, local_sems = refs[3 * n_w:]
        me = _my_pos()
        x, y, c = me
        sibling = (x, y, 1 - c)
        chips = [(1 - x, y), (x, 1 - y), (1 - x, 1 - y)]

        def region(w, pos):
            r, cdim = shards[w].shape
            d = _dev_index(pos)
            if col_sharded[w]:
                return out_refs[w].at[:, pl.ds(pl.multiple_of(d * cdim, LANES), cdim)]
            return out_refs[w].at[pl.ds(pl.multiple_of(d * r, 2 * SUBLANES), r), :]

        def copy(w, k, block, to, src=None):
            return pltpu.make_async_remote_copy(
                src_ref=region(w, block) if src is None else src, dst_ref=region(w, block),
                send_sem=send_sems.at[w, k], recv_sem=recv_sems.at[w, k], device_id=to, device_id_type=MESH)

        first, mine = [], []
        for w in range(n_w):
            stage[w][...] = in_refs[w][...].astype(BF16)
            first.append(copy(w, 0, me, sibling, src=stage[w]))
            first += [copy(w, 1 + j, me, (*chip, c), src=stage[w]) for j, chip in enumerate(chips)]
            mine.append(pltpu.make_async_copy(stage[w], region(w, me), local_sems.at[w]))
        for cp in first + mine:
            cp.start()
        passed = []
        for w in range(n_w):
            for j, chip in enumerate(chips):
                copy(w, 1 + j, (*chip, c), me).wait_recv()
                fwd = copy(w, 4 + j, (*chip, c), sibling)
                fwd.start()
                passed.append(fwd)
        for w in range(n_w):
            copy(w, 0, sibling, me).wait_recv()
            for j, chip in enumerate(chips):
                copy(w, 4 + j, (*chip, 1 - c), me).wait_recv()
        for cp in first + passed:
            cp.wait_send()
        for cp in mine:
            cp.wait()

    return pl.pallas_call(
        body, name="gather_weights", out_shape=[_sds(s, BF16) for s in full_shapes],
        in_specs=[_whole()] * n_w, out_specs=[pl.BlockSpec(memory_space=pl.ANY)] * n_w,
        scratch_shapes=[pltpu.VMEM(s.shape, BF16) for s in shards]
        + [pltpu.SemaphoreType.DMA((n_w, N_DEV - 1)), pltpu.SemaphoreType.DMA((n_w, N_DEV - 1)),
           pltpu.SemaphoreType.DMA((n_w,))],
        compiler_params=pltpu.CompilerParams(vmem_limit_bytes=VMEM_LIMIT),
    )(*shards)


def _scatter_grads(grads_bf, shard_shapes, col_sharded):
    n_w = len(grads_bf)

    def body(*refs):
        g_refs = refs[:n_w]
        r_refs = refs[n_w:2 * n_w]
        send_sems, recv_sems = refs[2 * n_w:]
        me = _my_pos()

        def region(w, pos):
            r, cdim = shard_shapes[w]
            d = _dev_index(pos)
            if col_sharded[w]:
                return g_refs[w].at[:, pl.ds(pl.multiple_of(d * cdim, LANES), cdim)]
            return g_refs[w].at[pl.ds(pl.multiple_of(d * r, 2 * SUBLANES), r), :]

        def copy(w, k):
            peer = _flip(me, k)
            return pltpu.make_async_remote_copy(
                src_ref=region(w, peer), dst_ref=r_refs[w].at[k - 1],
                send_sem=send_sems.at[w, k - 1], recv_sem=recv_sems.at[w, k - 1], device_id=peer, device_id_type=MESH)

        copies = [copy(w, k) for w in range(n_w) for k in range(1, N_DEV)]
        for cp in copies:
            cp.start()
        for cp in copies:
            cp.wait_recv()
        for cp in copies:
            cp.wait_send()

    return pl.pallas_call(
        body, name="scatter_grads", out_shape=[_sds((N_DEV - 1,) + tuple(s), BF16) for s in shard_shapes],
        in_specs=[pl.BlockSpec(memory_space=pl.ANY)] * n_w, out_specs=[pl.BlockSpec(memory_space=pl.ANY)] * n_w,
        scratch_shapes=[pltpu.SemaphoreType.DMA((n_w, N_DEV - 1)), pltpu.SemaphoreType.DMA((n_w, N_DEV - 1))],
        compiler_params=pltpu.CompilerParams(vmem_limit_bytes=VMEM_LIMIT),
    )(*grads_bf)


def _pack(arrays):
    flat = jnp.concatenate([a.reshape(-1) for a in arrays])
    pad = (-flat.shape[0]) % (SUBLANES * LANES)
    flat = jnp.pad(flat, (0, pad))
    return flat.reshape(-1, LANES)


def _unpack(packed, shapes):
    flat = packed.reshape(-1)
    out, off = [], 0
    for shp in shapes:
        n = 1
        for d in shp:
            n *= d
        out.append(flat[off:off + n].reshape(shp))
        off += n
    return out


def _block_diag(w):
    eye = jnp.eye(N_HEADS, dtype=w.dtype)
    return (eye[:, None, :, None] * w[:, :, None, :]).reshape(N_HEADS * HEAD_DIM, N_HEADS * HEAD_DIM)


def _diag_blocks(dense):
    return jnp.stack([dense[h * HEAD_DIM:(h + 1) * HEAD_DIM, h * HEAD_DIM:(h + 1) * HEAD_DIM] for h in range(N_HEADS)])


def _local_step(x2, target, mod, w_in_f, w_out_f, w_up_f, w_down_f, conv_w_full, ffn_cw_full,
                g_mix_pre, g_mix_post, conv_b, w_rgate, b_rgate, w_igate, b_igate, lru_a, v_norm_g, v_norm_b,
                w_spatial, b_spatial, g_lru_out, g_gmlp_out, g_ffn_pre, g_ffn_post, ffn_conv_b):
    sh_m, sc_m, gt_m, sh_f, sc_f, gt_f = [mod[k] for k in range(N_MOD)]
    wr_bd = _block_diag(w_rgate[0]).astype(BF16)
    wi_bd = _block_diag(w_igate[0]).astype(BF16)
    b_r = b_rgate.reshape(1, LRU_W)
    b_i = b_igate.reshape(1, LRU_W)
    b_sp_t = b_spatial[0].T
    w_sp_t = jnp.swapaxes(w_spatial[0], 1, 2)

    z, h = _in_fwd(x2, sh_m, sc_m, g_mix_pre, w_in_f)
    mix_params = (conv_w_full, conv_b, wr_bd, wi_bd, b_r, b_i, lru_a, v_norm_g, v_norm_b)
    ycat, hl = _mix_fwd(z, *mix_params, w_spatial[0], b_sp_t, g_lru_out, g_gmlp_out)
    y, x1, h2, up_pre = _out_up_fwd(ycat, x2, w_out_f, g_mix_post, gt_m, g_ffn_pre, sc_f, sh_f, w_up_f)
    act, y2, dout, loss_acc = _ffn_fwd(up_pre, ffn_cw_full, ffn_conv_b, w_down_f, x1, gt_f, g_ffn_post, target)

    d_y2, d_up, vs_ffn, cs_ffn = _ffn_bwd(dout, y2, up_pre, ffn_cw_full, ffn_conv_b, w_down_f, gt_f, g_ffn_post)
    gw_down = _wgrad(act, d_y2, D_MODEL // 2, "wgrad_down")
    d_x1, d_y, d_ycat, vs_up = _up_bwd(d_up, w_up_f, x1, dout, y, w_out_f, g_ffn_pre, sc_f, g_mix_post, gt_m)
    gw_up = _wgrad(h2, d_up, 2 * D_FF // N_DEV, "wgrad_up")
    gw_out = _wgrad(ycat, d_y, D_MODEL, "wgrad_out")
    d_z, vs_mix, dcw, d_wr, d_wi, d_ws, d_bs_t = _mix_bwd(
        d_ycat, z, hl, *mix_params, w_spatial[0], w_sp_t, b_sp_t, g_lru_out, g_gmlp_out)
    grad_x, vs_in = _in_bwd(d_z, w_in_f, x2, d_x1, g_mix_pre, sc_m)
    gw_in = _wgrad(h, d_z, IN_COLS // N_DEV, "wgrad_in")

    dmod = jnp.concatenate([vs_in[0], vs_in[1], vs_up[3], vs_up[0], vs_up[1], vs_ffn[0]])
    mask = ((jnp.arange(POS_BLOCK)[None, :] // CHUNK) <= (jnp.arange(POS_BLOCK)[:, None] // CHUNK))
    small_g = dict(b_ada=dmod, g_mix_pre=vs_in[2], g_mix_post=vs_up[4], conv_b=vs_mix[0], w_rgate=_diag_blocks(d_wr),
                   b_rgate=vs_mix[1], w_igate=_diag_blocks(d_wi), b_igate=vs_mix[2], lru_a=vs_mix[3],
                   v_norm_g=vs_mix[4], v_norm_b=vs_mix[5], w_spatial=jnp.where(mask[None], d_ws, 0.0),
                   b_spatial=d_bs_t.T, g_lru_out=vs_mix[6], g_gmlp_out=vs_mix[7], g_ffn_pre=vs_up[2],
                   g_ffn_post=vs_ffn[1], ffn_conv_b=cs_ffn[FFN_CONV_K])
    extra_g = [dcw[0:LRU_CONV_K], cs_ffn[0:FFN_CONV_K]]
    return dict(loss_acc=loss_acc[0, 0], grad_x=grad_x, small_g=small_g, extra_g=extra_g,
                w_in=gw_in, w_out=gw_out, w_up=gw_up, w_down=gw_down)


def kernel(x, c, w_ada, b_ada, g_mix_pre, g_mix_post, w_in, conv_w, conv_b, w_rgate, b_rgate, w_igate, b_igate, lru_a, v_norm_g, v_norm_b, w_spatial, b_spatial, g_lru_out, g_gmlp_out, w_out, g_ffn_pre, g_ffn_post, w_up, ffn_conv_w, ffn_conv_b, w_down, loss_target, m_w_ada, m_b_ada, m_g_mix_pre, m_g_mix_post, m_w_in, m_conv_w, m_conv_b, m_w_rgate, m_b_rgate, m_w_igate, m_b_igate, m_lru_a, m_v_norm_g, m_v_norm_b, m_w_spatial, m_b_spatial, m_g_lru_out, m_g_gmlp_out, m_w_out, m_g_ffn_pre, m_g_ffn_post, m_w_up, m_ffn_conv_w, m_ffn_conv_b, m_w_down, v_w_ada, v_b_ada, v_g_mix_pre, v_g_mix_post, v_w_in, v_conv_w, v_conv_b, v_w_rgate, v_b_rgate, v_w_igate, v_b_igate, v_lru_a, v_v_norm_g, v_v_norm_b, v_w_spatial, v_b_spatial, v_g_lru_out, v_g_gmlp_out, v_w_out, v_g_ffn_pre, v_g_ffn_post, v_w_up, v_ffn_conv_w, v_ffn_conv_b, v_w_down):
    me = _dev_index(_my_pos())
    ada_cols = w_ada.shape[-1]
    cw_cols = conv_w.shape[-1]
    fcw_cols = ffn_conv_w.shape[-1]

    start_shapes = [(D_MODEL,), (LRU_CONV_K, cw_cols), (FFN_CONV_K, fcw_cols)]
    gathered = _gather_small(_pack([c[0], conv_w[0], ffn_conv_w[0]]), "gather_start")
    per_dev = [_unpack(blk, start_shapes) for blk in jnp.split(gathered, N_DEV, axis=0)]
    c_all = jnp.stack([p[0] for p in per_dev])
    conv_w_full = jnp.concatenate([p[1] for p in per_dev], axis=1)
    ffn_cw_full = jnp.concatenate([p[2] for p in per_dev], axis=1)
    b_cols = lax.dynamic_slice_in_dim(b_ada, me * ada_cols, ada_cols, axis=1)
    mod_mine = _mod_part(c_all, w_ada[0], b_cols)
    mod_all = _gather_small(_pack([mod_mine]), "gather_mod")
    mod_all = mod_all.reshape(N_DEV, N_DEV, ada_cols)
    mod = lax.dynamic_index_in_dim(mod_all, me, axis=1, keepdims=False).reshape(N_MOD, 1, D_MODEL)

    w_in_f, w_out_f, w_up_f, w_down_f = _gather_weights(w_in[0], w_out[0], w_up[0], w_down[0])

    loc = _local_step(x[0], loss_target[0], mod, w_in_f, w_out_f, w_up_f, w_down_f, conv_w_full, ffn_cw_full,
                      g_mix_pre, g_mix_post, conv_b, w_rgate, b_rgate, w_igate, b_igate, lru_a, v_norm_g, v_norm_b,
                      w_spatial, b_spatial, g_lru_out, g_gmlp_out, g_ffn_pre, g_ffn_post, ffn_conv_b)
    loss = lax.psum(loc["loss_acc"], ("x", "y", "c"))
    grad_x = loc["grad_x"]
    small_g, extra_g = loc["small_g"], loc["extra_g"]
    gw_in, gw_in_bf = loc["w_in"]
    gw_out, gw_out_bf = loc["w_out"]
    gw_up, gw_up_bf = loc["w_up"]
    gw_down, gw_down_bf = loc["w_down"]

    shard_shapes = [w_in.shape[1:], w_out.shape[1:], w_up.shape[1:], w_down.shape[1:]]
    col_sharded = (True, False, True, False)
    recv = _scatter_grads([gw_in_bf, gw_out_bf, gw_up_bf, gw_down_bf], shard_shapes, col_sharded)

    def own(gfull, shp, cs):
        if cs:
            return lax.dynamic_slice_in_dim(gfull, me * shp[1], shp[1], axis=1)
        return lax.dynamic_slice_in_dim(gfull, me * shp[0], shp[0], axis=0)

    big = {}
    for name, gfull, shp, cs, rv, w_, m_, v_ in (
            ("w_in", gw_in, shard_shapes[0], True, recv[0], w_in, m_w_in, v_w_in),
            ("w_out", gw_out, shard_shapes[1], False, recv[1], w_out, m_w_out, v_w_out),
            ("w_up", gw_up, shard_shapes[2], True, recv[2], w_up, m_w_up, v_w_up),
            ("w_down", gw_down, shard_shapes[3], False, recv[3], w_down, m_w_down, v_w_down)):
        outs = _adamw_sum(w_[0], own(gfull, shp, cs), rv, m_[0], v_[0], "adamw_" + name)
        big[name] = [o[None] for o in outs]

    small_names = ["b_ada", "g_mix_pre", "g_mix_post", "conv_b", "w_rgate", "b_rgate", "w_igate", "b_igate", "lru_a",
                   "v_norm_g", "v_norm_b", "w_spatial", "b_spatial", "g_lru_out", "g_gmlp_out", "g_ffn_pre", "g_ffn_post",
                   "ffn_conv_b"]
    small_w = dict(b_ada=b_ada, g_mix_pre=g_mix_pre, g_mix_post=g_mix_post, conv_b=conv_b, w_rgate=w_rgate,
                   b_rgate=b_rgate, w_igate=w_igate, b_igate=b_igate, lru_a=lru_a, v_norm_g=v_norm_g, v_norm_b=v_norm_b,
                   w_spatial=w_spatial, b_spatial=b_spatial, g_lru_out=g_lru_out, g_gmlp_out=g_gmlp_out,
                   g_ffn_pre=g_ffn_pre, g_ffn_post=g_ffn_post, ffn_conv_b=ffn_conv_b)
    small_m = dict(b_ada=m_b_ada, g_mix_pre=m_g_mix_pre, g_mix_post=m_g_mix_post, conv_b=m_conv_b, w_rgate=m_w_rgate,
                   b_rgate=m_b_rgate, w_igate=m_w_igate, b_igate=m_b_igate, lru_a=m_lru_a, v_norm_g=m_v_norm_g,
                   v_norm_b=m_v_norm_b, w_spatial=m_w_spatial, b_spatial=m_b_spatial, g_lru_out=m_g_lru_out,
                   g_gmlp_out=m_g_gmlp_out, g_ffn_pre=m_g_ffn_pre, g_ffn_post=m_g_ffn_post, ffn_conv_b=m_ffn_conv_b)
    small_v = dict(b_ada=v_b_ada, g_mix_pre=v_g_mix_pre, g_mix_post=v_g_mix_post, conv_b=v_conv_b, w_rgate=v_w_rgate,
                   b_rgate=v_b_rgate, w_igate=v_w_igate, b_igate=v_b_igate, lru_a=v_lru_a, v_norm_g=v_v_norm_g,
                   v_norm_b=v_v_norm_b, w_spatial=v_w_spatial, b_spatial=v_b_spatial, g_lru_out=v_g_lru_out,
                   g_gmlp_out=v_g_gmlp_out, g_ffn_pre=v_g_ffn_pre, g_ffn_post=v_g_ffn_post, ffn_conv_b=v_ffn_conv_b)
    small_shapes = [small_w[n].shape for n in small_names]
    pack_small = _pack([small_g[n] for n in small_names])
    n_small_rows = pack_small.shape[0]
    pack_g = jnp.concatenate([pack_small, _pack(extra_g)], axis=0)
    parts = _gather_small(pack_g, "gather_grads").reshape(N_DEV, -1, LANES)
    pw, pm, pv = [_pack([dct[n] for n in small_names]) for dct in (small_w, small_m, small_v)]
    g_s, d_s, m_s, v_s = _adamw_gathered(pw, parts[:, :n_small_rows], pm, pv, "adamw_small")
    small_out = {n: vals for n, vals in zip(small_names, zip(*[_unpack(p, small_shapes) for p in (g_s, d_s, m_s, v_s)]))}

    flat_parts = parts[:, n_small_rows:].reshape(N_DEV, -1)
    cwp = flat_parts[:, :LRU_CONV_K * LRU_W].reshape(N_DEV, LRU_CONV_K, LRU_W)
    fwp = flat_parts[:, LRU_CONV_K * LRU_W:LRU_CONV_K * LRU_W + FFN_CONV_K * 2 * D_FF].reshape(N_DEV, FFN_CONV_K, 2 * D_FF)
    cwp = lax.dynamic_slice_in_dim(cwp, me * cw_cols, cw_cols, axis=2)
    fwp = lax.dynamic_slice_in_dim(fwp, me * fcw_cols, fcw_cols, axis=2)
    shard_conv_shapes = [conv_w.shape, ffn_conv_w.shape]
    parts_c = jnp.stack([_pack([cwp[d], fwp[d]]) for d in range(N_DEV)])
    g_c, d_c, m_c, v_c = _adamw_gathered(_pack([conv_w, ffn_conv_w]), parts_c, _pack([m_conv_w, m_ffn_conv_w]),
                                         _pack([v_conv_w, v_ffn_conv_w]), "adamw_conv")
    conv_out = {n: vals for n, vals in zip(("conv_w", "ffn_conv_w"),
                                           zip(*[_unpack(p, shard_conv_shapes) for p in (g_c, d_c, m_c, v_c)]))}

    dmod_all = parts.reshape(N_DEV, -1)[:, :N_MOD * D_MODEL]
    dmod_cols = lax.dynamic_slice_in_dim(dmod_all, me * ada_cols, ada_cols, axis=1)
    ada_out = [o[None] for o in _adamw_wada(c_all.T, dmod_cols, w_ada[0], m_w_ada[0], v_w_ada[0])]

    order = ["w_ada", "b_ada", "g_mix_pre", "g_mix_post", "w_in", "conv_w", "conv_b", "w_rgate", "b_rgate", "w_igate",
             "b_igate", "lru_a", "v_norm_g", "v_norm_b", "w_spatial", "b_spatial", "g_lru_out", "g_gmlp_out", "w_out",
             "g_ffn_pre", "g_ffn_post", "w_up", "ffn_conv_w", "ffn_conv_b", "w_down"]
    results = {"w_ada": ada_out, **big, **small_out, **conv_out}
    outs = [loss, grad_x[None]]
    for kind in range(4):
        outs += [results[n][kind] for n in order]
    return tuple(outs)
```

```python
import functools

import jax
import jax.numpy as jnp
from jax import lax
from jax.experimental import pallas as pl
from jax.experimental.pallas import tpu as pltpu

F32 = jnp.float32
BF16 = jnp.bfloat16

D_MODEL = 1024
LRU_W = 512
GMLP_W = 512
N_HEADS = 8
HEAD_DIM = 64
N_GROUPS = 4
POS_BLOCK = 128
CHUNK = 64
IN_COLS = 2048
D_FF = 3072
N_MOD = 6
N_DEV = 8
EPS = 1e-6
LRU_C = 8.0
LRU_CONV_K = 4
FFN_CONV_K = 3

ADAM_LR = 0.001
ADAM_B1 = 0.9
ADAM_B2 = 0.999
ADAM_EPS = 1e-08
ADAM_WD = 0.01
ADAM_STEP = 10

LANES = 128
SUBLANES = 8
TT_BIG = 512
TT_MIX = 256
TT_WG = 1024
FF_CW = 512
VMEM_LIMIT = 56 * 1024 * 1024

MESH = pl.DeviceIdType.MESH


def _sds(shape, dtype):
    return jax.ShapeDtypeStruct(shape, dtype)


def _cparams(sem=None):
    return pltpu.CompilerParams(dimension_semantics=sem, vmem_limit_bytes=VMEM_LIMIT)


def _whole():
    return pl.BlockSpec(memory_space=pltpu.VMEM)


def _const(shape):
    nd = len(shape)
    return pl.BlockSpec(shape, lambda *_: (0,) * nd)


def _any():
    return pl.BlockSpec(memory_space=pl.ANY)


class _Carry:
    def __init__(self, inputs, in_specs, out_shape, out_specs, scratch, start, finish):
        self.inputs, self.in_specs, self.out_shape, self.out_specs = inputs, in_specs, out_shape, out_specs
        self.scratch, self.start, self.finish = scratch, start, finish


def _call(body, name, grid, in_specs, out_specs, out_shape, scratch, args, carry=None):
    n_in, n_out, n_scr = len(in_specs), len(out_specs), len(scratch)
    c_in = len(carry.in_specs) if carry else 0
    c_out = len(carry.out_specs) if carry else 0

    def full_body(*refs):
        ins = refs[:n_in]
        c_ins = refs[n_in:n_in + c_in]
        outs = refs[n_in + c_in:n_in + c_in + n_out]
        c_outs = refs[n_in + c_in + n_out:n_in + c_in + n_out + c_out]
        scr = refs[n_in + c_in + n_out + c_out:n_in + c_in + n_out + c_out + n_scr]
        c_scr = refs[n_in + c_in + n_out + c_out + n_scr:]
        if carry:
            first = functools.reduce(lambda a, b: a & b, [pl.program_id(d) == 0 for d in range(len(grid))])
            last = functools.reduce(lambda a, b: a & b, [pl.program_id(d) == g - 1 for d, g in enumerate(grid)])

            @pl.when(first)
            def _():
                carry.start(c_ins, c_outs, c_scr)

        body(*ins, *outs, *scr)
        if carry:
            @pl.when(last)
            def _():
                carry.finish(c_ins, c_outs, c_scr)

    res = pl.pallas_call(
        full_body, name=name, grid=grid,
        in_specs=list(in_specs) + (list(carry.in_specs) if carry else []),
        out_specs=list(out_specs) + (list(carry.out_specs) if carry else []),
        out_shape=list(out_shape) + (list(carry.out_shape) if carry else []),
        scratch_shapes=list(scratch) + (list(carry.scratch) if carry else []),
        compiler_params=_cparams(("arbitrary",) * len(grid)),
    )(*args, *(carry.inputs if carry else []))
    return res[:n_out], res[n_out:]


def _gelu(x):
    u = 0.7978845608028654 * (x + 0.044715 * x * x * x)
    return 0.5 * x * (1.0 + jnp.tanh(u))


def _gelu_and_grad(x):
    x2 = x * x
    u = 0.7978845608028654 * (x + 0.044715 * x * x2)
    t = jnp.tanh(u)
    g = 0.5 * x * (1.0 + t)
    dg = 0.5 * (1.0 + t) + 0.5 * x * (1.0 - t * t) * 0.7978845608028654 * (1.0 + 3.0 * 0.044715 * x2)
    return g, dg


def _sigmoid(x):
    return 1.0 / (1.0 + jnp.exp(-x))


def _softplus(x):
    return jnp.maximum(x, 0.0) + jnp.log1p(jnp.exp(-jnp.abs(x)))


def _neg_expm1(x):
    series = -x * (1.0 + x * (0.5 + x * (1.0 / 6.0 + x * (1.0 / 24.0 + x * (1.0 / 120.0)))))
    return jnp.where(x > -0.1, series, 1.0 - jnp.exp(x))


def _dot(a, b):
    return jnp.dot(a.astype(BF16), b.astype(BF16), preferred_element_type=F32)


def _dot_nt(a, b):
    return lax.dot_general(a.astype(BF16), b.astype(BF16), (((1,), (1,)), ((), ())), preferred_element_type=F32)


def _dot_tn(a, b):
    return lax.dot_general(a.astype(BF16), b.astype(BF16), (((0,), (0,)), ((), ())), preferred_element_type=F32)


def _rows(shape):
    return lax.broadcasted_iota(jnp.int32, shape, 0)


def _shift_down(cur, prev8, s):
    if s == 0:
        return cur
    n = cur.shape[0]
    r = pltpu.roll(cur, s, 0)
    p = pltpu.roll(prev8, s, 0)
    top = jnp.where(_rows(p.shape) < s, p, r[0:SUBLANES])
    if n == SUBLANES:
        return top
    return jnp.concatenate([top, r[SUBLANES:]], axis=0)


def _shift_up(cur, next8, s):
    if s == 0:
        return cur
    n = cur.shape[0]
    r = pltpu.roll(cur, n - s, 0)
    q = pltpu.roll(next8, SUBLANES - s, 0)
    bot = jnp.where(_rows(q.shape) >= SUBLANES - s, q, r[n - SUBLANES:])
    if n == SUBLANES:
        return bot
    return jnp.concatenate([r[:n - SUBLANES], bot], axis=0)


def _scan_fwd(a, b):
    n = a.shape[0]
    rows = _rows(a.shape)
    s = 1
    while s < n:
        a_s = pltpu.roll(a, s, 0)
        b_s = pltpu.roll(b, s, 0)
        m = rows >= s
        b = jnp.where(m, a * b_s + b, b)
        a = jnp.where(m, a * a_s, a)
        s *= 2
    return a, b


def _scan_rev(a, b):
    n = a.shape[0]
    rows = _rows(a.shape)
    s = 1
    while s < n:
        a_s = pltpu.roll(a, n - s, 0)
        b_s = pltpu.roll(b, n - s, 0)
        m = rows < n - s
        b = jnp.where(m, b + a * b_s, b)
        a = jnp.where(m, a * a_s, a)
        s *= 2
    return a, b


def _rms(x):
    r = lax.rsqrt(jnp.mean(x * x, axis=-1, keepdims=True) + EPS)
    return x * r, r


def _rms_bwd(d_n, n, r):
    return r * (d_n - n * jnp.mean(d_n * n, axis=-1, keepdims=True))


def _colsum(x):
    return jnp.sum(x, axis=0, keepdims=True)


def _in_fwd(x, sh, sc, g, w_in, carry=None):
    s_len = x.shape[0]
    tt = min(TT_BIG, s_len)

    def body(x_ref, sh_ref, sc_ref, g_ref, w_ref, z_ref, h_ref):
        n, _ = _rms(x_ref[...])
        h = (n * g_ref[...] * (1.0 + sc_ref[...]) + sh_ref[...]).astype(BF16)
        h_ref[...] = h
        z_ref[...] = jnp.dot(h, w_ref[...], preferred_element_type=F32)

    row = lambda c: pl.BlockSpec((tt, c), lambda i: (i, 0))
    vec = _const((1, D_MODEL))
    return _call(
        body, "in_fwd", (s_len // tt,),
        in_specs=[row(D_MODEL), vec, vec, vec, _whole()],
        out_specs=[row(IN_COLS), row(D_MODEL)],
        out_shape=[_sds((s_len, IN_COLS), F32), _sds((s_len, D_MODEL), BF16)],
        scratch=[], args=(x, sh, sc, g, w_in), carry=carry)


def _lru_gates(xc, wr_ref, wi_ref, br, bi, sp_a):
    r = _sigmoid(_dot(xc, wr_ref[...]) + br)
    i = _sigmoid(_dot(xc, wi_ref[...]) + bi)
    la = -LRU_C * r * sp_a
    a = jnp.exp(la)
    mult = jnp.sqrt(_neg_expm1(2.0 * la))
    return r, i, a, mult


def _lru_conv(lx, prev8, cw_ref, cb):
    xc = cb + cw_ref[LRU_CONV_K - 1:LRU_CONV_K, :] * lx
    taps = []
    for k in range(LRU_CONV_K - 1):
        tap = _shift_down(lx, prev8, LRU_CONV_K - 1 - k)
        taps.append(tap)
        xc = xc + cw_ref[k:k + 1, :] * tap
    return xc, taps


def _ws_mask(transposed=False):
    i = lax.broadcasted_iota(jnp.int32, (POS_BLOCK, POS_BLOCK), 0)
    j = lax.broadcasted_iota(jnp.int32, (POS_BLOCK, POS_BLOCK), 1)
    if transposed:
        i, j = j, i
    return (j // CHUNK) <= (i // CHUNK)


def _gmlp_v(gv, vg, vb):
    av, dav = _gelu_and_grad(gv)
    mu = jnp.mean(av, axis=-1, keepdims=True)
    cen = av - mu
    rs = lax.rsqrt(jnp.mean(cen * cen, axis=-1, keepdims=True) + EPS)
    vhat = cen * rs
    return vhat * vg + vb, vhat, rs, dav


def _mix_fwd(z, conv_w, conv_b, wr_bd, wi_bd, b_r, b_i, lru_a, vn_g, vn_b, w_sp, b_sp_t, g_lru, g_gmlp, carry=None):
    s_len = z.shape[0]
    tt = min(TT_MIX, s_len)
    nblk = tt // POS_BLOCK

    def body(z_ref, cw_ref, cb_ref, wr_ref, wi_ref, br_ref, bi_ref, la_ref, vg_ref, vb_ref, ws_ref, bst_ref,
             gl_ref, gg_ref, y_ref, hl_ref, prev8, hcar):
        i = pl.program_id(0)

        @pl.when(i == 0)
        def _():
            prev8[...] = jnp.zeros_like(prev8)
            hcar[...] = jnp.zeros_like(hcar)

        lx = z_ref[:, 0:LRU_W]
        gate = z_ref[:, LRU_W:2 * LRU_W]
        gu = z_ref[:, 2 * LRU_W:2 * LRU_W + GMLP_W]
        gv = z_ref[:, 2 * LRU_W + GMLP_W:]

        xc, _ = _lru_conv(lx, prev8[...], cw_ref, cb_ref[...])
        prev8[...] = lx[tt - SUBLANES:]
        sp_a = _softplus(-la_ref[...])
        _, ig, a, mult = _lru_gates(xc, wr_ref, wi_ref, br_ref[...], bi_ref[...], sp_a)
        bx = mult * (ig * xc)
        a_cum, b_cum = _scan_fwd(a, bx)
        hl = a_cum * hcar[0:1, :] + b_cum
        hcar[...] = jnp.broadcast_to(hl[tt - 1:tt, :], hcar.shape)
        hl_ref[...] = hl
        y_lru = hl * _gelu(gate)
        n_l, _ = _rms(y_lru)
        y_ref[:, 0:LRU_W] = (n_l * gl_ref[...]).astype(BF16)

        u = _gelu(gu)
        v, _, _, _ = _gmlp_v(gv, vg_ref[...], vb_ref[...])
        mask = _ws_mask()
        sp_parts = []
        for nb in range(nblk):
            row = []
            for g in range(N_GROUPS):
                wsm = jnp.where(mask, ws_ref[g], 0.0)
                vblk = v[nb * POS_BLOCK:(nb + 1) * POS_BLOCK, g * LANES:(g + 1) * LANES]
                row.append(_dot(wsm, vblk) + bst_ref[:, g:g + 1])
            sp_parts.append(jnp.concatenate(row, axis=1))
        sp = jnp.concatenate(sp_parts, axis=0) if nblk > 1 else sp_parts[0]
        n_g, _ = _rms(u * sp)
        y_ref[:, LRU_W:] = (n_g * gg_ref[...]).astype(BF16)

    row = lambda c: pl.BlockSpec((tt, c), lambda i: (i, 0))
    v512 = _const((1, LRU_W))
    return _call(
        body, "mix_fwd", (s_len // tt,),
        in_specs=[row(IN_COLS), _const((LRU_CONV_K, LRU_W)), v512, _whole(), _whole(), v512, v512, v512, v512, v512,
                  _whole(), _whole(), v512, v512],
        out_specs=[row(LRU_W + GMLP_W), row(LRU_W)],
        out_shape=[_sds((s_len, LRU_W + GMLP_W), BF16), _sds((s_len, LRU_W), F32)],
        scratch=[pltpu.VMEM((SUBLANES, LRU_W), F32), pltpu.VMEM((SUBLANES, LRU_W), F32)],
        args=(z, conv_w, conv_b, wr_bd, wi_bd, b_r, b_i, lru_a, vn_g, vn_b, w_sp, b_sp_t, g_lru, g_gmlp), carry=carry)


def _out_up_fwd(ycat, x, w_out, g_post, gt_m, g_pre, sc_f, sh_f, w_up, carry=None):
    s_len = x.shape[0]
    tt = min(TT_MIX, s_len)

    def body(yc_ref, x_ref, wo_ref, gp_ref, gt_ref, g2_ref, sc_ref, sh_ref, wu_ref, y_ref, x1_ref, h2_ref, up_ref):
        y = jnp.dot(yc_ref[...], wo_ref[...], preferred_element_type=F32)
        y_ref[...] = y
        n_y, _ = _rms(y)
        x1 = x_ref[...] + gt_ref[...] * (n_y * gp_ref[...])
        x1_ref[...] = x1
        n1, _ = _rms(x1)
        h2 = (n1 * g2_ref[...] * (1.0 + sc_ref[...]) + sh_ref[...]).astype(BF16)
        h2_ref[...] = h2
        up_ref[0] = jnp.dot(h2, wu_ref[:, 0:D_FF], preferred_element_type=F32)
        up_ref[1] = jnp.dot(h2, wu_ref[:, D_FF:], preferred_element_type=F32)

    row = lambda c: pl.BlockSpec((tt, c), lambda i: (i, 0))
    vec = _const((1, D_MODEL))
    return _call(
        body, "out_up_fwd", (s_len // tt,),
        in_specs=[row(D_MODEL), row(D_MODEL), _whole(), vec, vec, vec, vec, vec, _whole()],
        out_specs=[row(D_MODEL), row(D_MODEL), row(D_MODEL), pl.BlockSpec((2, tt, D_FF), lambda i: (0, i, 0))],
        out_shape=[_sds((s_len, D_MODEL), F32), _sds((s_len, D_MODEL), F32), _sds((s_len, D_MODEL), BF16),
                   _sds((2, s_len, D_FF), F32)],
        scratch=[], args=(ycat, x, w_out, g_post, gt_m, g_pre, sc_f, sh_f, w_up), carry=carry)


def _ffn_conv(up_pre, prev8, cw_ref, cb):
    up = cb + cw_ref[FFN_CONV_K - 1:FFN_CONV_K, :] * up_pre
    taps = []
    for k in range(FFN_CONV_K - 1):
        tap = _shift_down(up_pre, prev8, FFN_CONV_K - 1 - k)
        taps.append(tap)
        up = up + cw_ref[k:k + 1, :] * tap
    return up, taps


def _ffn_fwd(up_pre, ffn_cw, ffn_cb, w_down, x1, gt_f, g_post, target):
    s_len = x1.shape[0]
    tt = min(TT_BIG, s_len)
    cw = FF_CW
    nc = D_FF // cw

    def body(up_ref, cwg_ref, cwv_ref, cbg_ref, cbv_ref, wd_ref, x1_ref, gt_ref, gp_ref, tg_ref,
             act_ref, y2_ref, dout_ref, loss_ref, prev, acc):
        i = pl.program_id(0)
        c = pl.program_id(1)

        @pl.when(i == 0)
        def _():
            prev[c] = jnp.zeros((2, SUBLANES, cw), F32)

        @pl.when((i == 0) & (c == 0))
        def _():
            loss_ref[...] = jnp.zeros_like(loss_ref)

        ug, _ = _ffn_conv(up_ref[0], prev[c, 0], cwg_ref, cbg_ref[...])
        uv, _ = _ffn_conv(up_ref[1], prev[c, 1], cwv_ref, cbv_ref[...])
        prev[c, 0] = up_ref[0, tt - SUBLANES:, :]
        prev[c, 1] = up_ref[1, tt - SUBLANES:, :]
        act = (_gelu(ug) * uv).astype(BF16)
        act_ref[...] = act
        part = jnp.dot(act, wd_ref[...], preferred_element_type=F32)

        @pl.when(c == 0)
        def _():
            acc[...] = part

        @pl.when(c > 0)
        def _():
            acc[...] += part

        @pl.when(c == nc - 1)
        def _():
            y2 = acc[...]
            y2_ref[...] = y2
            n2, _ = _rms(y2)
            out = x1_ref[...] + gt_ref[...] * (n2 * gp_ref[...])
            err = out - tg_ref[...]
            dout_ref[...] = err * (1.0 / D_MODEL)
            loss_ref[...] += jnp.broadcast_to(0.5 * jnp.sum(err * err, keepdims=True) * (1.0 / D_MODEL), loss_ref.shape)

    row = pl.BlockSpec((tt, D_MODEL), lambda i, c: (i, 0))
    vec = _const((1, D_MODEL))
    ffn_cb2 = ffn_cb.reshape(1, 2 * D_FF)
    return pl.pallas_call(
        body, name="ffn_fwd", grid=(s_len // tt, nc),
        in_specs=[pl.BlockSpec((2, tt, cw), lambda i, c: (0, i, c)),
                  pl.BlockSpec((FFN_CONV_K, cw), lambda i, c: (0, c)),
                  pl.BlockSpec((FFN_CONV_K, cw), lambda i, c: (0, c + nc)),
                  pl.BlockSpec((1, cw), lambda i, c: (0, c)),
                  pl.BlockSpec((1, cw), lambda i, c: (0, c + nc)),
                  pl.BlockSpec((cw, D_MODEL), lambda i, c: (c, 0)),
                  row, vec, vec, row],
        out_specs=[pl.BlockSpec((tt, cw), lambda i, c: (i, c)), row, row, _const((SUBLANES, LANES))],
        out_shape=[_sds((s_len, D_FF), BF16), _sds((s_len, D_MODEL), F32), _sds((s_len, D_MODEL), F32),
                   _sds((SUBLANES, LANES), F32)],
        scratch_shapes=[pltpu.VMEM((nc, 2, SUBLANES, cw), F32), pltpu.VMEM((tt, D_MODEL), F32)],
        compiler_params=_cparams(("arbitrary", "arbitrary")),
    )(up_pre, ffn_cw, ffn_cw, ffn_cb2, ffn_cb2, w_down, x1, gt_f, g_post, target)


def _ffn_bwd(dout, y2, up_pre, ffn_cw, ffn_cb, w_down, gt_f, g_post):
    s_len = dout.shape[0]
    tt = min(TT_BIG, s_len)
    nt = s_len // tt
    cw = FF_CW
    nc = D_FF // cw
    hb = tt // SUBLANES

    def body(do_ref, y2_ref, up_ref, halo_ref, cwg_ref, cwv_ref, cbg_ref, cbv_ref, wd_ref, gt_ref, gp_ref,
             dy2_ref, dup_ref, vs_ref, cs_ref, nxt, dy2s):
        i = pl.program_id(0)
        c = pl.program_id(1)
        first_tile = i == nt - 1

        @pl.when(i == 0)
        def _():
            nxt[c] = jnp.zeros((2, SUBLANES, cw), F32)
            cs_ref[c] = jnp.zeros((2, SUBLANES, cw), F32)

        @pl.when((i == 0) & (c == 0))
        def _():
            vs_ref[...] = jnp.zeros_like(vs_ref)

        @pl.when(c == 0)
        def _():
            n2, r2 = _rms(y2_ref[...])
            do = do_ref[...]
            vs_ref[0:1, :] += _colsum(do * n2 * gp_ref[...])
            vs_ref[1:2, :] += _colsum(do * gt_ref[...] * n2)
            dy2 = _rms_bwd(do * gt_ref[...] * gp_ref[...], n2, r2).astype(BF16)
            dy2s[...] = dy2
            dy2_ref[...] = dy2

        d_act = _dot_nt(dy2s[...], wd_ref[...])
        halo_g = jnp.where(first_tile, 0.0, halo_ref[0])
        halo_v = jnp.where(first_tile, 0.0, halo_ref[1])
        ug, taps_g = _ffn_conv(up_ref[0], halo_g, cwg_ref, cbg_ref[...])
        uv, taps_v = _ffn_conv(up_ref[1], halo_v, cwv_ref, cbv_ref[...])
        gl, dgl = _gelu_and_grad(ug)
        d_ug = d_act * uv * dgl
        d_uv = d_act * gl
        for half, (d_u, taps, cw_ref) in enumerate(((d_ug, taps_g, cwg_ref), (d_uv, taps_v, cwv_ref))):
            nx = nxt[c, half]
            d_pre = cw_ref[FFN_CONV_K - 1:FFN_CONV_K, :] * d_u
            sums = [None] * (FFN_CONV_K + 1)
            sums[FFN_CONV_K - 1] = _colsum(d_u * up_ref[half])
            for k in range(FFN_CONV_K - 1):
                d_pre = d_pre + cw_ref[k:k + 1, :] * _shift_up(d_u, nx, FFN_CONV_K - 1 - k)
                sums[k] = _colsum(d_u * taps[k])
            sums[FFN_CONV_K] = _colsum(d_u)
            pad = jnp.zeros((SUBLANES - FFN_CONV_K - 1, cw), F32)
            cs_ref[c, half] += jnp.concatenate(sums + [pad], axis=0)
            nxt[c, half] = d_u[0:SUBLANES]
            dup_ref[half] = d_pre.astype(BF16)

    rev = lambda i, c: (nt - 1 - i, 0)
    row = pl.BlockSpec((tt, D_MODEL), rev)
    vec = _const((1, D_MODEL))
    ffn_cb2 = ffn_cb.reshape(1, 2 * D_FF)
    outs = pl.pallas_call(
        body, name="ffn_bwd", grid=(nt, nc),
        in_specs=[row, row,
                  pl.BlockSpec((2, tt, cw), lambda i, c: (0, nt - 1 - i, c)),
                  pl.BlockSpec((2, SUBLANES, cw), lambda i, c: (0, jnp.maximum((nt - 1 - i) * hb - 1, 0), c)),
                  pl.BlockSpec((FFN_CONV_K, cw), lambda i, c: (0, c)),
                  pl.BlockSpec((FFN_CONV_K, cw), lambda i, c: (0, c + nc)),
                  pl.BlockSpec((1, cw), lambda i, c: (0, c)),
                  pl.BlockSpec((1, cw), lambda i, c: (0, c + nc)),
                  pl.BlockSpec((cw, D_MODEL), lambda i, c: (c, 0)),
                  vec, vec],
        out_specs=[row, pl.BlockSpec((2, tt, cw), lambda i, c: (0, nt - 1 - i, c)),
                   _const((SUBLANES, D_MODEL)), _const((nc, 2, SUBLANES, cw))],
        out_shape=[_sds((s_len, D_MODEL), BF16), _sds((2, s_len, D_FF), BF16), _sds((SUBLANES, D_MODEL), F32),
                   _sds((nc, 2, SUBLANES, cw), F32)],
        scratch_shapes=[pltpu.VMEM((nc, 2, SUBLANES, cw), F32), pltpu.VMEM((tt, D_MODEL), BF16)],
        compiler_params=_cparams(("arbitrary", "arbitrary")),
    )(dout, y2, up_pre, up_pre, ffn_cw, ffn_cw, ffn_cb2, ffn_cb2, w_down, gt_f, g_post)
    d_y2, d_up, vsum, cs = outs
    cs = jnp.transpose(cs, (2, 1, 0, 3)).reshape(SUBLANES, 2 * D_FF)
    return d_y2, d_up, vsum, cs


def _up_bwd(d_up, w_up, x1, dout, y, w_out, g_pre, sc_f, g_post, gt_m, carry=None):
    s_len = x1.shape[0]
    tt = min(TT_BIG, s_len)

    def body(du_ref, wu_ref, x1_ref, do_ref, y_ref, wo_ref, g2_ref, sc_ref, gp_ref, gt_ref,
             dx1_ref, dy_ref, dyc_ref, vs_ref):
        @pl.when(pl.program_id(0) == 0)
        def _():
            vs_ref[...] = jnp.zeros_like(vs_ref)

        d_h2 = _dot_nt(du_ref[0], wu_ref[:, 0:D_FF]) + _dot_nt(du_ref[1], wu_ref[:, D_FF:])
        n1, r1 = _rms(x1_ref[...])
        ng = n1 * g2_ref[...]
        vs_ref[0:1, :] += _colsum(d_h2)
        vs_ref[1:2, :] += _colsum(d_h2 * ng)
        d_ng = d_h2 * (1.0 + sc_ref[...])
        vs_ref[2:3, :] += _colsum(d_ng * n1)
        d_x1 = do_ref[...] + _rms_bwd(d_ng * g2_ref[...], n1, r1)
        dx1_ref[...] = d_x1
        n_y, r_y = _rms(y_ref[...])
        vs_ref[3:4, :] += _colsum(d_x1 * n_y * gp_ref[...])
        d_on = d_x1 * gt_ref[...]
        vs_ref[4:5, :] += _colsum(d_on * n_y)
        d_y = _rms_bwd(d_on * gp_ref[...], n_y, r_y).astype(BF16)
        dy_ref[...] = d_y
        dyc_ref[...] = _dot_nt(d_y, wo_ref[...])

    row = lambda c: pl.BlockSpec((tt, c), lambda i: (i, 0))
    vec = _const((1, D_MODEL))
    return _call(
        body, "up_bwd", (s_len // tt,),
        in_specs=[pl.BlockSpec((2, tt, D_FF), lambda i: (0, i, 0)), _whole(), row(D_MODEL), row(D_MODEL), row(D_MODEL),
                  _whole(), vec, vec, vec, vec],
        out_specs=[row(D_MODEL), row(D_MODEL), row(LRU_W + GMLP_W), _const((SUBLANES, D_MODEL))],
        out_shape=[_sds((s_len, D_MODEL), F32), _sds((s_len, D_MODEL), BF16), _sds((s_len, LRU_W + GMLP_W), F32),
                   _sds((SUBLANES, D_MODEL), F32)],
        scratch=[], args=(d_up, w_up, x1, dout, y, w_out, g_pre, sc_f, g_post, gt_m), carry=carry)


def _mix_bwd(d_ycat, z, hl, conv_w, conv_b, wr_bd, wi_bd, b_r, b_i, lru_a, vn_g, vn_b, w_sp, w_sp_t, b_sp_t,
             g_lru, g_gmlp, carry=None):
    s_len = z.shape[0]
    tt = min(TT_MIX, s_len)
    nt = s_len // tt
    nblk = tt // POS_BLOCK
    hb = tt // SUBLANES

    def body(dyc_ref, z_ref, zh_ref, hl_ref, hh_ref, cw_ref, cb_ref, wr_ref, wi_ref, br_ref, bi_ref, la_ref,
             vg_ref, vb_ref, ws_ref, wst_ref, bst_ref, gl_ref, gg_ref,
             dz_ref, vs_ref, dcw_ref, dwr_ref, dwi_ref, dws_ref, dbs_ref, nxt_dxc, nxt_a, nxt_lam):
        i = pl.program_id(0)
        first_tile = i == nt - 1

        @pl.when(i == 0)
        def _():
            for ref in (vs_ref, dcw_ref, dwr_ref, dwi_ref, dws_ref, dbs_ref, nxt_dxc, nxt_a, nxt_lam):
                ref[...] = jnp.zeros_like(ref)

        lx = z_ref[:, 0:LRU_W]
        gate = z_ref[:, LRU_W:2 * LRU_W]
        gu = z_ref[:, 2 * LRU_W:2 * LRU_W + GMLP_W]
        gv = z_ref[:, 2 * LRU_W + GMLP_W:]
        prev8 = jnp.where(first_tile, 0.0, zh_ref[...])
        hprev8 = jnp.where(first_tile, 0.0, hh_ref[...])

        xc, taps = _lru_conv(lx, prev8, cw_ref, cb_ref[...])
        a_par = la_ref[...]
        sp_a = _softplus(-a_par)
        r, ig, a, mult = _lru_gates(xc, wr_ref, wi_ref, br_ref[...], bi_ref[...], sp_a)
        hl = hl_ref[...]
        h_prev = _shift_down(hl, hprev8, 1)
        ggate, dggate = _gelu_and_grad(gate)
        y_lru = hl * ggate
        n_l, r_l = _rms(y_lru)
        d_nl = dyc_ref[:, 0:LRU_W]
        vs_ref[6:7, :] += _colsum(d_nl * n_l)
        d_yl = _rms_bwd(d_nl * gl_ref[...], n_l, r_l)
        d_hl = d_yl * ggate
        d_gate = d_yl * hl * dggate
        a_up = _shift_up(a, nxt_a[...], 1)
        a_cum, b_cum = _scan_rev(a_up, d_hl)
        lam = b_cum + a_cum * nxt_lam[0:1, :]
        nxt_a[...] = jnp.broadcast_to(a[0:1, :], nxt_a.shape)
        nxt_lam[...] = jnp.broadcast_to(lam[0:1, :], nxt_lam.shape)
        ixc = ig * xc
        d_la = lam * h_prev * a - lam * ixc * (a * a) / mult
        d_i = lam * mult * xc
        d_xc = lam * mult * ig
        vs_ref[3:4, :] += _colsum(d_la * r) * (LRU_C * _sigmoid(-a_par))
        d_pr = d_la * (-LRU_C * sp_a) * r * (1.0 - r)
        d_pi = d_i * ig * (1.0 - ig)
        vs_ref[1:2, :] += _colsum(d_pr)
        vs_ref[2:3, :] += _colsum(d_pi)
        dwr_ref[...] += _dot_tn(xc, d_pr)
        dwi_ref[...] += _dot_tn(xc, d_pi)
        d_xc = d_xc + _dot_nt(d_pr, wr_ref[...]) + _dot_nt(d_pi, wi_ref[...])
        vs_ref[0:1, :] += _colsum(d_xc)
        nx = nxt_dxc[...]
        d_lx = cw_ref[LRU_CONV_K - 1:LRU_CONV_K, :] * d_xc
        dcw_ref[LRU_CONV_K - 1:LRU_CONV_K, :] += _colsum(d_xc * lx)
        for k in range(LRU_CONV_K - 1):
            d_lx = d_lx + cw_ref[k:k + 1, :] * _shift_up(d_xc, nx, LRU_CONV_K - 1 - k)
            dcw_ref[k:k + 1, :] += _colsum(d_xc * taps[k])
        nxt_dxc[...] = d_xc[0:SUBLANES]
        dz_ref[:, 0:LRU_W] = d_lx.astype(BF16)
        dz_ref[:, LRU_W:2 * LRU_W] = d_gate.astype(BF16)

        u, du = _gelu_and_grad(gu)
        v, vhat, rs, dav = _gmlp_v(gv, vg_ref[...], vb_ref[...])
        mask = _ws_mask()
        sp_parts = []
        for nb in range(nblk):
            rowp = []
            for g in range(N_GROUPS):
                wsm = jnp.where(mask, ws_ref[g], 0.0)
                vblk = v[nb * POS_BLOCK:(nb + 1) * POS_BLOCK, g * LANES:(g + 1) * LANES]
                rowp.append(_dot(wsm, vblk) + bst_ref[:, g:g + 1])
            sp_parts.append(jnp.concatenate(rowp, axis=1))
        sp = jnp.concatenate(sp_parts, axis=0) if nblk > 1 else sp_parts[0]
        y_g = u * sp
        n_g, r_g = _rms(y_g)
        d_ng = dyc_ref[:, LRU_W:]
        vs_ref[7:8, :] += _colsum(d_ng * n_g)
        d_yg = _rms_bwd(d_ng * gg_ref[...], n_g, r_g)
        d_gu = d_yg * sp * du
        d_sp = d_yg * u
        mask_t = _ws_mask(transposed=True)
        dv_parts = []
        for nb in range(nblk):
            rowp = []
            for g in range(N_GROUPS):
                rs_, cs_ = slice(nb * POS_BLOCK, (nb + 1) * POS_BLOCK), slice(g * LANES, (g + 1) * LANES)
                dsp_blk = d_sp[rs_, cs_]
                dbs_ref[:, g:g + 1] += jnp.sum(dsp_blk, axis=1, keepdims=True)
                dws_ref[g] += _dot_nt(dsp_blk, v[rs_, cs_])
                wsm_t = jnp.where(mask_t, wst_ref[g], 0.0)
                rowp.append(_dot(wsm_t, dsp_blk))
            dv_parts.append(jnp.concatenate(rowp, axis=1))
        d_v = jnp.concatenate(dv_parts, axis=0) if nblk > 1 else dv_parts[0]
        vs_ref[4:5, :] += _colsum(d_v * vhat)
        vs_ref[5:6, :] += _colsum(d_v)
        d_vh = d_v * vg_ref[...]
        d_av = rs * (d_vh - jnp.mean(d_vh, axis=-1, keepdims=True)
                     - vhat * jnp.mean(d_vh * vhat, axis=-1, keepdims=True))
        dz_ref[:, 2 * LRU_W:2 * LRU_W + GMLP_W] = d_gu.astype(BF16)
        dz_ref[:, 2 * LRU_W + GMLP_W:] = (d_av * dav).astype(BF16)

    rev = lambda c: pl.BlockSpec((tt, c), lambda i: (nt - 1 - i, 0))
    halo = pl.BlockSpec((SUBLANES, LRU_W), lambda i: (jnp.maximum((nt - 1 - i) * hb - 1, 0), 0))
    v512 = _const((1, LRU_W))
    return _call(
        body, "mix_bwd", (nt,),
        in_specs=[rev(LRU_W + GMLP_W), rev(IN_COLS), halo, rev(LRU_W), halo,
                  _const((LRU_CONV_K, LRU_W)), v512, _whole(), _whole(), v512, v512, v512, v512, v512,
                  _whole(), _whole(), _whole(), v512, v512],
        out_specs=[rev(IN_COLS), _const((SUBLANES, LRU_W)), _const((SUBLANES, LRU_W)),
                   _const((LRU_W, LRU_W)), _const((LRU_W, LRU_W)),
                   _const((N_GROUPS, POS_BLOCK, POS_BLOCK)), _const((POS_BLOCK, N_GROUPS))],
        out_shape=[_sds((s_len, IN_COLS), BF16), _sds((SUBLANES, LRU_W), F32), _sds((SUBLANES, LRU_W), F32),
                   _sds((LRU_W, LRU_W), F32), _sds((LRU_W, LRU_W), F32),
                   _sds((N_GROUPS, POS_BLOCK, POS_BLOCK), F32), _sds((POS_BLOCK, N_GROUPS), F32)],
        scratch=[pltpu.VMEM((SUBLANES, LRU_W), F32), pltpu.VMEM((SUBLANES, LRU_W), F32),
                 pltpu.VMEM((SUBLANES, LRU_W), F32)],
        args=(d_ycat, z, z, hl, hl, conv_w, conv_b, wr_bd, wi_bd, b_r, b_i, lru_a, vn_g, vn_b, w_sp, w_sp_t, b_sp_t,
              g_lru, g_gmlp), carry=carry)


def _in_bwd(d_z, w_in, x, d_x1, g, sc, carry=None):
    s_len = x.shape[0]
    tt = min(TT_BIG, s_len)

    def body(dz_ref, w_ref, x_ref, dx1_ref, g_ref, sc_ref, gx_ref, vs_ref):
        @pl.when(pl.program_id(0) == 0)
        def _():
            vs_ref[...] = jnp.zeros_like(vs_ref)

        d_h = _dot_nt(dz_ref[...], w_ref[...])
        n, r = _rms(x_ref[...])
        vs_ref[0:1, :] += _colsum(d_h)
        vs_ref[1:2, :] += _colsum(d_h * n * g_ref[...])
        d_ng = d_h * (1.0 + sc_ref[...])
        vs_ref[2:3, :] += _colsum(d_ng * n)
        gx_ref[...] = dx1_ref[...] + _rms_bwd(d_ng * g_ref[...], n, r)

    row = lambda c: pl.BlockSpec((tt, c), lambda i: (i, 0))
    vec = _const((1, D_MODEL))
    return _call(
        body, "in_bwd", (s_len // tt,),
        in_specs=[row(IN_COLS), _whole(), row(D_MODEL), row(D_MODEL), vec, vec],
        out_specs=[row(D_MODEL), _const((SUBLANES, D_MODEL))],
        out_shape=[_sds((s_len, D_MODEL), F32), _sds((SUBLANES, D_MODEL), F32)],
        scratch=[], args=(d_z, w_in, x, d_x1, g, sc), carry=carry)


def _wgrad(a, b, tn, name, carry=None):
    s_len, k_dim = a.shape
    halves = b.ndim == 3
    n_dim = b.shape[-1] * (2 if halves else 1)
    ts = min(TT_WG, s_len)
    nj = n_dim // tn
    nt = s_len // ts

    def body(a_ref, b_ref, o_ref, ob_ref):
        t = pl.program_id(1)
        part = _dot_tn(a_ref[...], b_ref[0] if halves else b_ref[...])

        @pl.when(t == 0)
        def _():
            o_ref[...] = part

        @pl.when(t > 0)
        def _():
            o_ref[...] += part

        @pl.when(t == nt - 1)
        def _():
            ob_ref[...] = o_ref[...].astype(BF16)

    if halves:
        per_half = nj // 2
        b_spec = pl.BlockSpec((1, ts, tn), lambda j, t: (j // per_half, t, j % per_half))
    else:
        b_spec = pl.BlockSpec((ts, tn), lambda j, t: (t, j))
    o_spec = pl.BlockSpec((k_dim, tn), lambda j, t: (0, j))
    return _call(
        body, name, (nj, nt),
        in_specs=[pl.BlockSpec((ts, k_dim), lambda j, t: (t, 0)), b_spec],
        out_specs=[o_spec, o_spec],
        out_shape=[_sds((k_dim, n_dim), F32), _sds((k_dim, n_dim), BF16)],
        scratch=[], args=(a, b), carry=carry)


def _adam_math(w, g, m, v):
    m = ADAM_B1 * m + (1.0 - ADAM_B1) * g
    v = ADAM_B2 * v + (1.0 - ADAM_B2) * (g * g)
    m_hat = m / (1.0 - ADAM_B1 ** ADAM_STEP)
    v_hat = v / (1.0 - ADAM_B2 ** ADAM_STEP)
    delta = -ADAM_LR * (m_hat / (jnp.sqrt(v_hat) + ADAM_EPS) + ADAM_WD * w)
    return delta, m, v


def _row_tile(rows, cols, n_f32_arrays):
    budget = VMEM_LIMIT // 2
    tr = rows
    while tr % 2 == 0 and tr // 2 >= SUBLANES and (tr // 2) % SUBLANES == 0 and tr * cols * 4 * n_f32_arrays * 2 > budget:
        tr //= 2
    return tr


def _adamw_sum(w, g_own, recv, m, v, name):
    rows, cols = w.shape
    n_recv = recv.shape[0]
    tr = _row_tile(rows, cols, 10)

    def body(w_ref, g_ref, r_ref, m_ref, v_ref, go_ref, d_ref, mo_ref, vo_ref):
        g = g_ref[...]
        for k in range(n_recv):
            g = g + r_ref[k].astype(F32)
        go_ref[...] = g
        d_ref[...], mo_ref[...], vo_ref[...] = _adam_math(w_ref[...], g, m_ref[...], v_ref[...])

    blk = pl.BlockSpec((tr, cols), lambda i: (i, 0))
    return pl.pallas_call(
        body, name=name, grid=(rows // tr,),
        in_specs=[blk, blk, pl.BlockSpec((n_recv, tr, cols), lambda i: (0, i, 0)), blk, blk],
        out_specs=[blk] * 4, out_shape=[_sds((rows, cols), F32)] * 4,
        compiler_params=_cparams(("arbitrary",)),
    )(w, g_own, recv, m, v)


def _adamw_gathered(w, parts, m, v, name):
    rows, cols = w.shape

    def body(w_ref, p_ref, m_ref, v_ref, go_ref, d_ref, mo_ref, vo_ref):
        g = p_ref[0]
        for k in range(1, N_DEV):
            g = g + p_ref[k]
        go_ref[...] = g
        d_ref[...], mo_ref[...], vo_ref[...] = _adam_math(w_ref[...], g, m_ref[...], v_ref[...])

    return pl.pallas_call(
        body, name=name, out_shape=[_sds((rows, cols), F32)] * 4,
        in_specs=[_whole()] * 4, out_specs=[_whole()] * 4,
        compiler_params=_cparams(),
    )(w, parts, m, v)


def _adamw_plain(w, g, m, v, name):
    rows, cols = w.shape

    def body(w_ref, g_ref, m_ref, v_ref, d_ref, mo_ref, vo_ref):
        d_ref[...], mo_ref[...], vo_ref[...] = _adam_math(w_ref[...], g_ref[...], m_ref[...], v_ref[...])

    return pl.pallas_call(
        body, name=name, out_shape=[_sds((rows, cols), F32)] * 3,
        in_specs=[_whole()] * 4, out_specs=[_whole()] * 3,
        compiler_params=_cparams(),
    )(w, g, m, v)


def _adamw_wada(c_all_t, dmod_cols, w, m, v):
    rows, cols = w.shape

    def body(ct_ref, dm_ref, w_ref, m_ref, v_ref, go_ref, d_ref, mo_ref, vo_ref):
        ct = ct_ref[...]
        ca = ct * _sigmoid(ct)
        g = jnp.dot(ca, dm_ref[...], preferred_element_type=F32, precision=lax.Precision.HIGHEST)
        go_ref[...] = g
        d_ref[...], mo_ref[...], vo_ref[...] = _adam_math(w_ref[...], g, m_ref[...], v_ref[...])

    return pl.pallas_call(
        body, name="adamw_w_ada", out_shape=[_sds((rows, cols), F32)] * 4,
        in_specs=[_whole()] * 5, out_specs=[_whole()] * 4,
        compiler_params=_cparams(),
    )(c_all_t, dmod_cols, w, m, v)


def _mod_part(c_all, w_ada, b_cols):
    def body(c_ref, w_ref, b_ref, o_ref):
        cv = c_ref[...]
        ca = cv * _sigmoid(cv)
        o_ref[...] = jnp.dot(ca, w_ref[...], preferred_element_type=F32, precision=lax.Precision.HIGHEST) + b_ref[...]

    return pl.pallas_call(
        body, name="mod_part", out_shape=_sds((N_DEV, w_ada.shape[1]), F32),
        in_specs=[_whole()] * 3, out_specs=_whole(), compiler_params=_cparams(),
    )(c_all, w_ada, b_cols)


def _my_pos():
    return lax.axis_index("x"), lax.axis_index("y"), lax.axis_index("c")


def _flip(pos, k):
    x, y, c = pos
    return (1 - x if k & 4 else x, 1 - y if k & 2 else y, 1 - c if k & 1 else c)


def _dev_index(pos):
    x, y, c = pos
    return 4 * x + 2 * y + c


def _gather_small(v, name):
    rows, cols = v.shape

    def body(v_ref, out_ref, send_sems, recv_sems):
        me = _my_pos()

        def slot(pos):
            return out_ref.at[pl.ds(pl.multiple_of(_dev_index(pos) * rows, SUBLANES), rows), :]

        def copy(k):
            return pltpu.make_async_remote_copy(
                src_ref=v_ref, dst_ref=slot(me), send_sem=send_sems.at[k - 1], recv_sem=recv_sems.at[k - 1],
                device_id=_flip(me, k), device_id_type=MESH)

        sends = [copy(k) for k in range(1, N_DEV)]
        for cp in sends:
            cp.start()
        out_ref[pl.ds(pl.multiple_of(_dev_index(me) * rows, SUBLANES), rows), :] = v_ref[...]
        for k in range(1, N_DEV):
            pltpu.make_async_remote_copy(
                src_ref=v_ref, dst_ref=slot(_flip(me, k)), send_sem=send_sems.at[k - 1], recv_sem=recv_sems.at[k - 1],
                device_id=_flip(me, k), device_id_type=MESH).wait_recv()
        for cp in sends:
            cp.wait_send()

    return pl.pallas_call(
        body, name=name, out_shape=_sds((N_DEV * rows, cols), F32),
        in_specs=[_whole()], out_specs=_whole(),
        scratch_shapes=[pltpu.SemaphoreType.DMA((N_DEV - 1,)), pltpu.SemaphoreType.DMA((N_DEV - 1,))],
        compiler_params=pltpu.CompilerParams(vmem_limit_bytes=VMEM_LIMIT),
    )(v)


def _region(ref, shard_shape, col_sharded, pos):
    r, cdim = shard_shape
    d = _dev_index(pos)
    if col_sharded:
        return ref.at[:, pl.ds(pl.multiple_of(d * cdim, LANES), cdim)]
    return ref.at[pl.ds(pl.multiple_of(d * r, 2 * SUBLANES), r), :]


def _gather_carry(shards, col_sharded):
    n_w = len(shards)
    shapes = [tuple(s.shape) for s in shards]
    full_shapes = [(s[0], s[1] * N_DEV) if cs else (s[0] * N_DEV, s[1]) for s, cs in zip(shapes, col_sharded)]

    def tools(out_refs, scr):
        send_sems, recv_sems = scr[n_w], scr[n_w + 1]
        me = _my_pos()
        x, y, c = me
        sibling = (x, y, 1 - c)
        chips = [(1 - x, y), (x, 1 - y), (1 - x, 1 - y)]

        def region(w, pos):
            return _region(out_refs[w], shapes[w], col_sharded[w], pos)

        def copy(w, k, block, to, src=None):
            return pltpu.make_async_remote_copy(
                src_ref=region(w, block) if src is None else src, dst_ref=region(w, block),
                send_sem=send_sems.at[w, k], recv_sem=recv_sems.at[w, k], device_id=to, device_id_type=MESH)

        def first(w):
            return [copy(w, 0, me, sibling, src=scr[w])] + [
                copy(w, 1 + j, me, (*chip, c), src=scr[w]) for j, chip in enumerate(chips)]

        def mine(w):
            return pltpu.make_async_copy(scr[w], region(w, me), scr[n_w + 2].at[w])

        return me, c, sibling, chips, copy, first, mine

    def start(ins, outs, scr):
        _, _, _, _, _, first, mine = tools(outs, scr)
        for w in range(n_w):
            scr[w][...] = ins[w][...].astype(BF16)
            for cp in first(w) + [mine(w)]:
                cp.start()

    def finish(ins, outs, scr):
        me, c, sibling, chips, copy, first, mine = tools(outs, scr)
        passed = []
        for w in range(n_w):
            for j, chip in enumerate(chips):
                copy(w, 1 + j, (*chip, c), me).wait_recv()
                fwd = copy(w, 4 + j, (*chip, c), sibling)
                fwd.start()
                passed.append(fwd)
        for w in range(n_w):
            copy(w, 0, sibling, me).wait_recv()
            for j, chip in enumerate(chips):
                copy(w, 4 + j, (*chip, 1 - c), me).wait_recv()
        for w in range(n_w):
            for cp in first(w):
                cp.wait_send()
            mine(w).wait()
        for cp in passed:
            cp.wait_send()

    return _Carry(
        inputs=list(shards), in_specs=[_whole()] * n_w,
        out_shape=[_sds(s, BF16) for s in full_shapes], out_specs=[_any()] * n_w,
        scratch=[pltpu.VMEM(s, BF16) for s in shapes]
        + [pltpu.SemaphoreType.DMA((n_w, N_DEV - 1)), pltpu.SemaphoreType.DMA((n_w, N_DEV - 1)),
           pltpu.SemaphoreType.DMA((n_w,))],
        start=start, finish=finish)


def _scatter_carry(grads_bf, shard_shapes, col_sharded):
    n_w = len(grads_bf)
    shapes = [tuple(s) for s in shard_shapes]

    def copies(ins, outs, scr):
        send_sems, recv_sems = scr
        me = _my_pos()
        out = []
        for w in range(n_w):
            for k in range(1, N_DEV):
                peer = _flip(me, k)
                out.append(pltpu.make_async_remote_copy(
                    src_ref=_region(ins[w], shapes[w], col_sharded[w], peer), dst_ref=outs[w].at[k - 1],
                    send_sem=send_sems.at[w, k - 1], recv_sem=recv_sems.at[w, k - 1],
                    device_id=peer, device_id_type=MESH))
        return out

    def start(ins, outs, scr):
        for cp in copies(ins, outs, scr):
            cp.start()

    def finish(ins, outs, scr):
        cps = copies(ins, outs, scr)
        for cp in cps:
            cp.wait_recv()
        for cp in cps:
            cp.wait_send()

    return _Carry(
        inputs=list(grads_bf), in_specs=[_any()] * n_w,
        out_shape=[_sds((N_DEV - 1,) + s, BF16) for s in shapes], out_specs=[_any()] * n_w,
        scratch=[pltpu.SemaphoreType.DMA((n_w, N_DEV - 1)), pltpu.SemaphoreType.DMA((n_w, N_DEV - 1))],
        start=start, finish=finish)


def _comm_only(carry, name):
    def body(o_ref):
        o_ref[...] = jnp.zeros_like(o_ref)

    _, outs = _call(body, name, (1,), in_specs=[], out_specs=[_const((SUBLANES, LANES))],
                    out_shape=[_sds((SUBLANES, LANES), F32)], scratch=[], args=(), carry=carry)
    return outs


def _pack(arrays):
    flat = jnp.concatenate([a.reshape(-1) for a in arrays])
    pad = (-flat.shape[0]) % (SUBLANES * LANES)
    flat = jnp.pad(flat, (0, pad))
    return flat.reshape(-1, LANES)


def _unpack(packed, shapes):
    flat = packed.reshape(-1)
    out, off = [], 0
    for shp in shapes:
        n = 1
        for d in shp:
            n *= d
        out.append(flat[off:off + n].reshape(shp))
        off += n
    return out


def _block_diag(w):
    eye = jnp.eye(N_HEADS, dtype=w.dtype)
    return (eye[:, None, :, None] * w[:, :, None, :]).reshape(N_HEADS * HEAD_DIM, N_HEADS * HEAD_DIM)


def _diag_blocks(dense):
    return jnp.stack([dense[h * HEAD_DIM:(h + 1) * HEAD_DIM, h * HEAD_DIM:(h + 1) * HEAD_DIM] for h in range(N_HEADS)])


def _local_step(x2, target, mod, w_in_f, w_full, conv_w_full, ffn_cw_full,
                g_mix_pre, g_mix_post, conv_b, w_rgate, b_rgate, w_igate, b_igate, lru_a, v_norm_g, v_norm_b,
                w_spatial, b_spatial, g_lru_out, g_gmlp_out, g_ffn_pre, g_ffn_post, ffn_conv_b,
                gather=None, scatter=None):
    sh_m, sc_m, gt_m, sh_f, sc_f, gt_f = [mod[k] for k in range(N_MOD)]
    wr_bd = _block_diag(w_rgate[0]).astype(BF16)
    wi_bd = _block_diag(w_igate[0]).astype(BF16)
    b_r = b_rgate.reshape(1, LRU_W)
    b_i = b_igate.reshape(1, LRU_W)
    b_sp_t = b_spatial[0].T
    w_sp_t = jnp.swapaxes(w_spatial[0], 1, 2)

    def arriving(name):
        return gather(name) if gather else None

    def leaving(name, grad_bf):
        return scatter(name, grad_bf) if scatter else None

    def landed(name, carried):
        return carried[0] if gather else w_full[name]

    (z, h), got = _in_fwd(x2, sh_m, sc_m, g_mix_pre, w_in_f, carry=arriving("w_out"))
    w_out_f = landed("w_out", got)
    mix_params = (conv_w_full, conv_b, wr_bd, wi_bd, b_r, b_i, lru_a, v_norm_g, v_norm_b)
    (ycat, hl), got = _mix_fwd(z, *mix_params, w_spatial[0], b_sp_t, g_lru_out, g_gmlp_out, carry=arriving("w_up"))
    w_up_f = landed("w_up", got)
    (y, x1, h2, up_pre), got = _out_up_fwd(ycat, x2, w_out_f, g_mix_post, gt_m, g_ffn_pre, sc_f, sh_f, w_up_f,
                                           carry=arriving("w_down"))
    w_down_f = landed("w_down", got)
    act, y2, dout, loss_acc = _ffn_fwd(up_pre, ffn_cw_full, ffn_conv_b, w_down_f, x1, gt_f, g_ffn_post, target)

    recv = {}
    d_y2, d_up, vs_ffn, cs_ffn = _ffn_bwd(dout, y2, up_pre, ffn_cw_full, ffn_conv_b, w_down_f, gt_f, g_ffn_post)
    gw_down, _ = _wgrad(act, d_y2, D_MODEL // 2, "wgrad_down")
    (d_x1, d_y, d_ycat, vs_up), recv["w_down"] = _up_bwd(
        d_up, w_up_f, x1, dout, y, w_out_f, g_ffn_pre, sc_f, g_mix_post, gt_m, carry=leaving("w_down", gw_down[1]))
    gw_up, _ = _wgrad(h2, d_up, 2 * D_FF // N_DEV, "wgrad_up")
    gw_out, _ = _wgrad(ycat, d_y, D_MODEL, "wgrad_out")
    (d_z, vs_mix, dcw, d_wr, d_wi, d_ws, d_bs_t), recv["w_up"] = _mix_bwd(
        d_ycat, z, hl, *mix_params, w_spatial[0], w_sp_t, b_sp_t, g_lru_out, g_gmlp_out,
        carry=leaving("w_up", gw_up[1]))
    gw_in, recv["w_out"] = _wgrad(h, d_z, IN_COLS // N_DEV, "wgrad_in", carry=leaving("w_out", gw_out[1]))
    (grad_x, vs_in), recv["w_in"] = _in_bwd(d_z, w_in_f, x2, d_x1, g_mix_pre, sc_m, carry=leaving("w_in", gw_in[1]))

    dmod = jnp.concatenate([vs_in[0], vs_in[1], vs_up[3], vs_up[0], vs_up[1], vs_ffn[0]])
    mask = ((jnp.arange(POS_BLOCK)[None, :] // CHUNK) <= (jnp.arange(POS_BLOCK)[:, None] // CHUNK))
    small_g = dict(b_ada=dmod, g_mix_pre=vs_in[2], g_mix_post=vs_up[4], conv_b=vs_mix[0], w_rgate=_diag_blocks(d_wr),
                   b_rgate=vs_mix[1], w_igate=_diag_blocks(d_wi), b_igate=vs_mix[2], lru_a=vs_mix[3],
                   v_norm_g=vs_mix[4], v_norm_b=vs_mix[5], w_spatial=jnp.where(mask[None], d_ws, 0.0),
                   b_spatial=d_bs_t.T, g_lru_out=vs_mix[6], g_gmlp_out=vs_mix[7], g_ffn_pre=vs_up[2],
                   g_ffn_post=vs_ffn[1], ffn_conv_b=cs_ffn[FFN_CONV_K])
    extra_g = [dcw[0:LRU_CONV_K], cs_ffn[0:FFN_CONV_K]]
    return dict(loss_acc=loss_acc[0, 0], grad_x=grad_x, small_g=small_g, extra_g=extra_g, recv=recv,
                w_in=gw_in, w_out=gw_out, w_up=gw_up, w_down=gw_down)


def kernel(x, c, w_ada, b_ada, g_mix_pre, g_mix_post, w_in, conv_w, conv_b, w_rgate, b_rgate, w_igate, b_igate, lru_a, v_norm_g, v_norm_b, w_spatial, b_spatial, g_lru_out, g_gmlp_out, w_out, g_ffn_pre, g_ffn_post, w_up, ffn_conv_w, ffn_conv_b, w_down, loss_target, m_w_ada, m_b_ada, m_g_mix_pre, m_g_mix_post, m_w_in, m_conv_w, m_conv_b, m_w_rgate, m_b_rgate, m_w_igate, m_b_igate, m_lru_a, m_v_norm_g, m_v_norm_b, m_w_spatial, m_b_spatial, m_g_lru_out, m_g_gmlp_out, m_w_out, m_g_ffn_pre, m_g_ffn_post, m_w_up, m_ffn_conv_w, m_ffn_conv_b, m_w_down, v_w_ada, v_b_ada, v_g_mix_pre, v_g_mix_post, v_w_in, v_conv_w, v_conv_b, v_w_rgate, v_b_rgate, v_w_igate, v_b_igate, v_lru_a, v_v_norm_g, v_v_norm_b, v_w_spatial, v_b_spatial, v_g_lru_out, v_g_gmlp_out, v_w_out, v_g_ffn_pre, v_g_ffn_post, v_w_up, v_ffn_conv_w, v_ffn_conv_b, v_w_down):
    me = _dev_index(_my_pos())
    ada_cols = w_ada.shape[-1]
    cw_cols = conv_w.shape[-1]
    fcw_cols = ffn_conv_w.shape[-1]

    start_shapes = [(D_MODEL,), (LRU_CONV_K, cw_cols), (FFN_CONV_K, fcw_cols)]
    gathered = _gather_small(_pack([c[0], conv_w[0], ffn_conv_w[0]]), "gather_start")
    per_dev = [_unpack(blk, start_shapes) for blk in jnp.split(gathered, N_DEV, axis=0)]
    c_all = jnp.stack([p[0] for p in per_dev])
    conv_w_full = jnp.concatenate([p[1] for p in per_dev], axis=1)
    ffn_cw_full = jnp.concatenate([p[2] for p in per_dev], axis=1)
    b_cols = lax.dynamic_slice_in_dim(b_ada, me * ada_cols, ada_cols, axis=1)
    mod_mine = _mod_part(c_all, w_ada[0], b_cols)
    mod_all = _gather_small(_pack([mod_mine]), "gather_mod")
    mod_all = mod_all.reshape(N_DEV, N_DEV, ada_cols)
    mod = lax.dynamic_index_in_dim(mod_all, me, axis=1, keepdims=False).reshape(N_MOD, 1, D_MODEL)

    big_w = dict(w_in=(w_in, m_w_in, v_w_in, True), w_out=(w_out, m_w_out, v_w_out, False),
                 w_up=(w_up, m_w_up, v_w_up, True), w_down=(w_down, m_w_down, v_w_down, False))

    def gather(name):
        return _gather_carry([big_w[name][0][0]], [big_w[name][3]])

    def scatter(name, grad_bf):
        return _scatter_carry([grad_bf], [big_w[name][0].shape[1:]], [big_w[name][3]])

    (w_in_f,) = _comm_only(gather("w_in"), "gather_w_in")

    loc = _local_step(x[0], loss_target[0], mod, w_in_f, None, conv_w_full, ffn_cw_full,
                      g_mix_pre, g_mix_post, conv_b, w_rgate, b_rgate, w_igate, b_igate, lru_a, v_norm_g, v_norm_b,
                      w_spatial, b_spatial, g_lru_out, g_gmlp_out, g_ffn_pre, g_ffn_post, ffn_conv_b,
                      gather=gather, scatter=scatter)
    loss = lax.psum(loc["loss_acc"], ("x", "y", "c"))
    grad_x = loc["grad_x"]
    small_g, extra_g = loc["small_g"], loc["extra_g"]

    big = {}
    for name, (w_, m_, v_, cs) in big_w.items():
        shp = w_.shape[1:]
        own = lax.dynamic_slice_in_dim(loc[name][0], me * shp[1 if cs else 0], shp[1 if cs else 0], axis=1 if cs else 0)
        outs = _adamw_sum(w_[0], own, loc["recv"][name][0], m_[0], v_[0], "adamw_" + name)
        big[name] = [o[None] for o in outs]

    small_names = ["b_ada", "g_mix_pre", "g_mix_post", "conv_b", "w_rgate", "b_rgate", "w_igate", "b_igate", "lru_a",
                   "v_norm_g", "v_norm_b", "w_spatial", "b_spatial", "g_lru_out", "g_gmlp_out", "g_ffn_pre", "g_ffn_post",
                   "ffn_conv_b"]
    small_w = dict(b_ada=b_ada, g_mix_pre=g_mix_pre, g_mix_post=g_mix_post, conv_b=conv_b, w_rgate=w_rgate,
                   b_rgate=b_rgate, w_igate=w_igate, b_igate=b_igate, lru_a=lru_a, v_norm_g=v_norm_g, v_norm_b=v_norm_b,
                   w_spatial=w_spatial, b_spatial=b_spatial, g_lru_out=g_lru_out, g_gmlp_out=g_gmlp_out,
                   g_ffn_pre=g_ffn_pre, g_ffn_post=g_ffn_post, ffn_conv_b=ffn_conv_b)
    small_m = dict(b_ada=m_b_ada, g_mix_pre=m_g_mix_pre, g_mix_post=m_g_mix_post, conv_b=m_conv_b, w_rgate=m_w_rgate,
                   b_rgate=m_b_rgate, w_igate=m_w_igate, b_igate=m_b_igate, lru_a=m_lru_a, v_norm_g=m_v_norm_g,
                   v_norm_b=m_v_norm_b, w_spatial=m_w_spatial, b_spatial=m_b_spatial, g_lru_out=m_g_lru_out,
                   g_gmlp_out=m_g_gmlp_out, g_ffn_pre=m_g_ffn_pre, g_ffn_post=m_g_ffn_post, ffn_conv_b=m_ffn_conv_b)
    small_v = dict(b_ada=v_b_ada, g_mix_pre=v_g_mix_pre, g_mix_post=v_g_mix_post, conv_b=v_conv_b, w_rgate=v_w_rgate,
                   b_rgate=v_b_rgate, w_igate=v_w_igate, b_igate=v_b_igate, lru_a=v_lru_a, v_norm_g=v_v_norm_g,
                   v_norm_b=v_v_norm_b, w_spatial=v_w_spatial, b_spatial=v_b_spatial, g_lru_out=v_g_lru_out,
                   g_gmlp_out=v_g_gmlp_out, g_ffn_pre=v_g_ffn_pre, g_ffn_post=v_g_ffn_post, ffn_conv_b=v_ffn_conv_b)
    small_shapes = [small_w[n].shape for n in small_names]
    pack_small = _pack([small_g[n] for n in small_names])
    n_small_rows = pack_small.shape[0]
    pack_g = jnp.concatenate([pack_small, _pack(extra_g)], axis=0)
    parts = _gather_small(pack_g, "gather_grads").reshape(N_DEV, -1, LANES)
    pw, pm, pv = [_pack([dct[n] for n in small_names]) for dct in (small_w, small_m, small_v)]
    g_s, d_s, m_s, v_s = _adamw_gathered(pw, parts[:, :n_small_rows], pm, pv, "adamw_small")
    small_out = {n: vals for n, vals in zip(small_names, zip(*[_unpack(p, small_shapes) for p in (g_s, d_s, m_s, v_s)]))}

    flat_parts = parts[:, n_small_rows:].reshape(N_DEV, -1)
    cwp = flat_parts[:, :LRU_CONV_K * LRU_W].reshape(N_DEV, LRU_CONV_K, LRU_W)
    fwp = flat_parts[:, LRU_CONV_K * LRU_W:LRU_CONV_K * LRU_W + FFN_CONV_K * 2 * D_FF].reshape(N_DEV, FFN_CONV_K, 2 * D_FF)
    cwp = lax.dynamic_slice_in_dim(cwp, me * cw_cols, cw_cols, axis=2)
    fwp = lax.dynamic_slice_in_dim(fwp, me * fcw_cols, fcw_cols, axis=2)
    shard_conv_shapes = [conv_w.shape, ffn_conv_w.shape]
    parts_c = jnp.stack([_pack([cwp[d], fwp[d]]) for d in range(N_DEV)])
    g_c, d_c, m_c, v_c = _adamw_gathered(_pack([conv_w, ffn_conv_w]), parts_c, _pack([m_conv_w, m_ffn_conv_w]),
                                         _pack([v_conv_w, v_ffn_conv_w]), "adamw_conv")
    conv_out = {n: vals for n, vals in zip(("conv_w", "ffn_conv_w"),
                                           zip(*[_unpack(p, shard_conv_shapes) for p in (g_c, d_c, m_c, v_c)]))}

    dmod_all = parts.reshape(N_DEV, -1)[:, :N_MOD * D_MODEL]
    dmod_cols = lax.dynamic_slice_in_dim(dmod_all, me * ada_cols, ada_cols, axis=1)
    ada_out = [o[None] for o in _adamw_wada(c_all.T, dmod_cols, w_ada[0], m_w_ada[0], v_w_ada[0])]

    order = ["w_ada", "b_ada", "g_mix_pre", "g_mix_post", "w_in", "conv_w", "conv_b", "w_rgate", "b_rgate", "w_igate",
             "b_igate", "lru_a", "v_norm_g", "v_norm_b", "w_spatial", "b_spatial", "g_lru_out", "g_gmlp_out", "w_out",
             "g_ffn_pre", "g_ffn_post", "w_up", "ffn_conv_w", "ffn_conv_b", "w_down"]
    results = {"w_ada": ada_out, **big, **small_out, **conv_out}
    outs = [loss, grad_x[None]]
    for kind in range(4):
        outs += [results[n][kind] for n in order]
    return tuple(outs)
```

```python
import functools

import jax
import jax.numpy as jnp
from jax import lax
from jax.experimental import pallas as pl
from jax.experimental.pallas import tpu as pltpu

F32 = jnp.float32
BF16 = jnp.bfloat16

D_MODEL = 1024
LRU_W = 512
GMLP_W = 512
N_HEADS = 8
HEAD_DIM = 64
N_GROUPS = 4
POS_BLOCK = 128
CHUNK = 64
IN_COLS = 2048
D_FF = 3072
N_MOD = 6
N_DEV = 8
EPS = 1e-6
LRU_C = 8.0
LRU_CONV_K = 4
FFN_CONV_K = 3

ADAM_LR = 0.001
ADAM_B1 = 0.9
ADAM_B2 = 0.999
ADAM_EPS = 1e-08
ADAM_WD = 0.01
ADAM_STEP = 10

LANES = 128
SUBLANES = 8
TT_BIG = 512
TT_MIX = 256
TT_WG = 1024
FF_CW = 512
VMEM_LIMIT = 56 * 1024 * 1024

MESH = pl.DeviceIdType.MESH


def _sds(shape, dtype):
    return jax.ShapeDtypeStruct(shape, dtype)


def _cparams(sem=None):
    return pltpu.CompilerParams(dimension_semantics=sem, vmem_limit_bytes=VMEM_LIMIT)


def _whole():
    return pl.BlockSpec(memory_space=pltpu.VMEM)


def _const(shape):
    nd = len(shape)
    return pl.BlockSpec(shape, lambda *_: (0,) * nd)


def _any():
    return pl.BlockSpec(memory_space=pl.ANY)


class _Carry:
    def __init__(self, inputs, in_specs, out_shape, out_specs, scratch, start, finish):
        self.inputs, self.in_specs, self.out_shape, self.out_specs = inputs, in_specs, out_shape, out_specs
        self.scratch, self.start, self.finish = scratch, start, finish


def _call(body, name, grid, in_specs, out_specs, out_shape, scratch, args, carry=None):
    n_in, n_out, n_scr = len(in_specs), len(out_specs), len(scratch)
    c_in = len(carry.in_specs) if carry else 0
    c_out = len(carry.out_specs) if carry else 0

    def full_body(*refs):
        ins = refs[:n_in]
        c_ins = refs[n_in:n_in + c_in]
        outs = refs[n_in + c_in:n_in + c_in + n_out]
        c_outs = refs[n_in + c_in + n_out:n_in + c_in + n_out + c_out]
        scr = refs[n_in + c_in + n_out + c_out:n_in + c_in + n_out + c_out + n_scr]
        c_scr = refs[n_in + c_in + n_out + c_out + n_scr:]
        if carry:
            first = functools.reduce(lambda a, b: a & b, [pl.program_id(d) == 0 for d in range(len(grid))])
            last = functools.reduce(lambda a, b: a & b, [pl.program_id(d) == g - 1 for d, g in enumerate(grid)])

            @pl.when(first)
            def _():
                carry.start(c_ins, c_outs, c_scr)

        body(*ins, *outs, *scr)
        if carry:
            @pl.when(last)
            def _():
                carry.finish(c_ins, c_outs, c_scr)

    res = pl.pallas_call(
        full_body, name=name, grid=grid,
        in_specs=list(in_specs) + (list(carry.in_specs) if carry else []),
        out_specs=list(out_specs) + (list(carry.out_specs) if carry else []),
        out_shape=list(out_shape) + (list(carry.out_shape) if carry else []),
        scratch_shapes=list(scratch) + (list(carry.scratch) if carry else []),
        compiler_params=_cparams(("arbitrary",) * len(grid)),
    )(*args, *(carry.inputs if carry else []))
    return res[:n_out], res[n_out:]


def _gelu(x):
    u = 0.7978845608028654 * (x + 0.044715 * x * x * x)
    return 0.5 * x * (1.0 + jnp.tanh(u))


def _gelu_and_grad(x):
    x2 = x * x
    u = 0.7978845608028654 * (x + 0.044715 * x * x2)
    t = jnp.tanh(u)
    g = 0.5 * x * (1.0 + t)
    dg = 0.5 * (1.0 + t) + 0.5 * x * (1.0 - t * t) * 0.7978845608028654 * (1.0 + 3.0 * 0.044715 * x2)
    return g, dg


def _sigmoid(x):
    return 1.0 / (1.0 + jnp.exp(-x))


def _softplus(x):
    return jnp.maximum(x, 0.0) + jnp.log1p(jnp.exp(-jnp.abs(x)))


def _neg_expm1(x):
    series = -x * (1.0 + x * (0.5 + x * (1.0 / 6.0 + x * (1.0 / 24.0 + x * (1.0 / 120.0)))))
    return jnp.where(x > -0.1, series, 1.0 - jnp.exp(x))


def _dot(a, b):
    return jnp.dot(a.astype(BF16), b.astype(BF16), preferred_element_type=F32)


def _dot_nt(a, b):
    return lax.dot_general(a.astype(BF16), b.astype(BF16), (((1,), (1,)), ((), ())), preferred_element_type=F32)


def _dot_tn(a, b):
    return lax.dot_general(a.astype(BF16), b.astype(BF16), (((0,), (0,)), ((), ())), preferred_element_type=F32)


def _rows(shape):
    return lax.broadcasted_iota(jnp.int32, shape, 0)


def _shift_down(cur, prev8, s):
    if s == 0:
        return cur
    n = cur.shape[0]
    r = pltpu.roll(cur, s, 0)
    p = pltpu.roll(prev8, s, 0)
    top = jnp.where(_rows(p.shape) < s, p, r[0:SUBLANES])
    if n == SUBLANES:
        return top
    return jnp.concatenate([top, r[SUBLANES:]], axis=0)


def _shift_up(cur, next8, s):
    if s == 0:
        return cur
    n = cur.shape[0]
    r = pltpu.roll(cur, n - s, 0)
    q = pltpu.roll(next8, SUBLANES - s, 0)
    bot = jnp.where(_rows(q.shape) >= SUBLANES - s, q, r[n - SUBLANES:])
    if n == SUBLANES:
        return bot
    return jnp.concatenate([r[:n - SUBLANES], bot], axis=0)


def _scan_fwd(a, b):
    n = a.shape[0]
    rows = _rows(a.shape)
    s = 1
    while s < n:
        a_s = pltpu.roll(a, s, 0)
        b_s = pltpu.roll(b, s, 0)
        m = rows >= s
        b = jnp.where(m, a * b_s + b, b)
        a = jnp.where(m, a * a_s, a)
        s *= 2
    return a, b


def _scan_rev(a, b):
    n = a.shape[0]
    rows = _rows(a.shape)
    s = 1
    while s < n:
        a_s = pltpu.roll(a, n - s, 0)
        b_s = pltpu.roll(b, n - s, 0)
        m = rows < n - s
        b = jnp.where(m, b + a * b_s, b)
        a = jnp.where(m, a * a_s, a)
        s *= 2
    return a, b


def _rms(x):
    r = lax.rsqrt(jnp.mean(x * x, axis=-1, keepdims=True) + EPS)
    return x * r, r


def _rms_bwd(d_n, n, r):
    return r * (d_n - n * jnp.mean(d_n * n, axis=-1, keepdims=True))


def _colsum(x):
    return jnp.sum(x, axis=0, keepdims=True)


def _in_fwd(x, sh, sc, g, w_in, carry=None):
    s_len = x.shape[0]
    tt = min(TT_BIG, s_len)

    def body(x_ref, sh_ref, sc_ref, g_ref, w_ref, z_ref, h_ref):
        n, _ = _rms(x_ref[...])
        h = (n * g_ref[...] * (1.0 + sc_ref[...]) + sh_ref[...]).astype(BF16)
        h_ref[...] = h
        z_ref[...] = jnp.dot(h, w_ref[...], preferred_element_type=F32)

    row = lambda c: pl.BlockSpec((tt, c), lambda i: (i, 0))
    vec = _const((1, D_MODEL))
    return _call(
        body, "in_fwd", (s_len // tt,),
        in_specs=[row(D_MODEL), vec, vec, vec, _whole()],
        out_specs=[row(IN_COLS), row(D_MODEL)],
        out_shape=[_sds((s_len, IN_COLS), F32), _sds((s_len, D_MODEL), BF16)],
        scratch=[], args=(x, sh, sc, g, w_in), carry=carry)


def _lru_gates(xc, wr_ref, wi_ref, br, bi, sp_a):
    r = _sigmoid(_dot(xc, wr_ref[...]) + br)
    i = _sigmoid(_dot(xc, wi_ref[...]) + bi)
    la = -LRU_C * r * sp_a
    a = jnp.exp(la)
    mult = jnp.sqrt(_neg_expm1(2.0 * la))
    return r, i, a, mult


def _lru_conv(lx, prev8, cw_ref, cb):
    xc = cb + cw_ref[LRU_CONV_K - 1:LRU_CONV_K, :] * lx
    taps = []
    for k in range(LRU_CONV_K - 1):
        tap = _shift_down(lx, prev8, LRU_CONV_K - 1 - k)
        taps.append(tap)
        xc = xc + cw_ref[k:k + 1, :] * tap
    return xc, taps


def _ws_mask(transposed=False):
    i = lax.broadcasted_iota(jnp.int32, (POS_BLOCK, POS_BLOCK), 0)
    j = lax.broadcasted_iota(jnp.int32, (POS_BLOCK, POS_BLOCK), 1)
    if transposed:
        i, j = j, i
    return (j // CHUNK) <= (i // CHUNK)


def _gmlp_v(gv, vg, vb):
    av, dav = _gelu_and_grad(gv)
    mu = jnp.mean(av, axis=-1, keepdims=True)
    cen = av - mu
    rs = lax.rsqrt(jnp.mean(cen * cen, axis=-1, keepdims=True) + EPS)
    vhat = cen * rs
    return vhat * vg + vb, vhat, rs, dav


def _mix_fwd(z, conv_w, conv_b, wr_bd, wi_bd, b_r, b_i, lru_a, vn_g, vn_b, w_sp, b_sp_t, g_lru, g_gmlp, carry=None):
    s_len = z.shape[0]
    tt = min(TT_MIX, s_len)
    nblk = tt // POS_BLOCK

    def body(z_ref, cw_ref, cb_ref, wr_ref, wi_ref, br_ref, bi_ref, la_ref, vg_ref, vb_ref, ws_ref, bst_ref,
             gl_ref, gg_ref, y_ref, hl_ref, prev8, hcar):
        i = pl.program_id(0)

        @pl.when(i == 0)
        def _():
            prev8[...] = jnp.zeros_like(prev8)
            hcar[...] = jnp.zeros_like(hcar)

        lx = z_ref[:, 0:LRU_W]
        gate = z_ref[:, LRU_W:2 * LRU_W]
        gu = z_ref[:, 2 * LRU_W:2 * LRU_W + GMLP_W]
        gv = z_ref[:, 2 * LRU_W + GMLP_W:]

        xc, _ = _lru_conv(lx, prev8[...], cw_ref, cb_ref[...])
        prev8[...] = lx[tt - SUBLANES:]
        sp_a = _softplus(-la_ref[...])
        _, ig, a, mult = _lru_gates(xc, wr_ref, wi_ref, br_ref[...], bi_ref[...], sp_a)
        bx = mult * (ig * xc)
        a_cum, b_cum = _scan_fwd(a, bx)
        hl = a_cum * hcar[0:1, :] + b_cum
        hcar[...] = jnp.broadcast_to(hl[tt - 1:tt, :], hcar.shape)
        hl_ref[...] = hl
        y_lru = hl * _gelu(gate)
        n_l, _ = _rms(y_lru)
        y_ref[:, 0:LRU_W] = (n_l * gl_ref[...]).astype(BF16)

        u = _gelu(gu)
        v, _, _, _ = _gmlp_v(gv, vg_ref[...], vb_ref[...])
        mask = _ws_mask()
        sp_parts = []
        for nb in range(nblk):
            row = []
            for g in range(N_GROUPS):
                wsm = jnp.where(mask, ws_ref[g], 0.0)
                vblk = v[nb * POS_BLOCK:(nb + 1) * POS_BLOCK, g * LANES:(g + 1) * LANES]
                row.append(_dot(wsm, vblk) + bst_ref[:, g:g + 1])
            sp_parts.append(jnp.concatenate(row, axis=1))
        sp = jnp.concatenate(sp_parts, axis=0) if nblk > 1 else sp_parts[0]
        n_g, _ = _rms(u * sp)
        y_ref[:, LRU_W:] = (n_g * gg_ref[...]).astype(BF16)

    row = lambda c: pl.BlockSpec((tt, c), lambda i: (i, 0))
    v512 = _const((1, LRU_W))
    return _call(
        body, "mix_fwd", (s_len // tt,),
        in_specs=[row(IN_COLS), _const((LRU_CONV_K, LRU_W)), v512, _whole(), _whole(), v512, v512, v512, v512, v512,
                  _whole(), _whole(), v512, v512],
        out_specs=[row(LRU_W + GMLP_W), row(LRU_W)],
        out_shape=[_sds((s_len, LRU_W + GMLP_W), BF16), _sds((s_len, LRU_W), F32)],
        scratch=[pltpu.VMEM((SUBLANES, LRU_W), F32), pltpu.VMEM((SUBLANES, LRU_W), F32)],
        args=(z, conv_w, conv_b, wr_bd, wi_bd, b_r, b_i, lru_a, vn_g, vn_b, w_sp, b_sp_t, g_lru, g_gmlp), carry=carry)


def _out_up_fwd(ycat, x, w_out, g_post, gt_m, g_pre, sc_f, sh_f, w_up, carry=None):
    s_len = x.shape[0]
    tt = min(TT_MIX, s_len)

    def body(yc_ref, x_ref, wo_ref, gp_ref, gt_ref, g2_ref, sc_ref, sh_ref, wu_ref, y_ref, x1_ref, h2_ref, up_ref):
        y = jnp.dot(yc_ref[...], wo_ref[...], preferred_element_type=F32)
        y_ref[...] = y
        n_y, _ = _rms(y)
        x1 = x_ref[...] + gt_ref[...] * (n_y * gp_ref[...])
        x1_ref[...] = x1
        n1, _ = _rms(x1)
        h2 = (n1 * g2_ref[...] * (1.0 + sc_ref[...]) + sh_ref[...]).astype(BF16)
        h2_ref[...] = h2
        up_ref[0] = jnp.dot(h2, wu_ref[:, 0:D_FF], preferred_element_type=F32)
        up_ref[1] = jnp.dot(h2, wu_ref[:, D_FF:], preferred_element_type=F32)

    row = lambda c: pl.BlockSpec((tt, c), lambda i: (i, 0))
    vec = _const((1, D_MODEL))
    return _call(
        body, "out_up_fwd", (s_len // tt,),
        in_specs=[row(D_MODEL), row(D_MODEL), _whole(), vec, vec, vec, vec, vec, _whole()],
        out_specs=[row(D_MODEL), row(D_MODEL), row(D_MODEL), pl.BlockSpec((2, tt, D_FF), lambda i: (0, i, 0))],
        out_shape=[_sds((s_len, D_MODEL), F32), _sds((s_len, D_MODEL), F32), _sds((s_len, D_MODEL), BF16),
                   _sds((2, s_len, D_FF), F32)],
        scratch=[], args=(ycat, x, w_out, g_post, gt_m, g_pre, sc_f, sh_f, w_up), carry=carry)


def _ffn_conv(up_pre, prev8, cw_ref, cb):
    up = cb + cw_ref[FFN_CONV_K - 1:FFN_CONV_K, :] * up_pre
    taps = []
    for k in range(FFN_CONV_K - 1):
        tap = _shift_down(up_pre, prev8, FFN_CONV_K - 1 - k)
        taps.append(tap)
        up = up + cw_ref[k:k + 1, :] * tap
    return up, taps


def _ffn_fwd(up_pre, ffn_cw, ffn_cb, w_down, x1, gt_f, g_post, target):
    s_len = x1.shape[0]
    tt = min(TT_BIG, s_len)
    cw = FF_CW
    nc = D_FF // cw

    def body(up_ref, cwg_ref, cwv_ref, cbg_ref, cbv_ref, wd_ref, x1_ref, gt_ref, gp_ref, tg_ref,
             act_ref, y2_ref, dout_ref, loss_ref, upc_ref, prev, acc):
        i = pl.program_id(0)
        c = pl.program_id(1)

        @pl.when(i == 0)
        def _():
            prev[c] = jnp.zeros((2, SUBLANES, cw), F32)

        @pl.when((i == 0) & (c == 0))
        def _():
            loss_ref[...] = jnp.zeros_like(loss_ref)

        ug, _ = _ffn_conv(up_ref[0], prev[c, 0], cwg_ref, cbg_ref[...])
        uv, _ = _ffn_conv(up_ref[1], prev[c, 1], cwv_ref, cbv_ref[...])
        prev[c, 0] = up_ref[0, tt - SUBLANES:, :]
        prev[c, 1] = up_ref[1, tt - SUBLANES:, :]
        upc_ref[0] = ug
        upc_ref[1] = uv
        act = (_gelu(ug) * uv).astype(BF16)
        act_ref[...] = act
        part = jnp.dot(act, wd_ref[...], preferred_element_type=F32)

        @pl.when(c == 0)
        def _():
            acc[...] = part

        @pl.when(c > 0)
        def _():
            acc[...] += part

        @pl.when(c == nc - 1)
        def _():
            y2 = acc[...]
            y2_ref[...] = y2
            n2, _ = _rms(y2)
            out = x1_ref[...] + gt_ref[...] * (n2 * gp_ref[...])
            err = out - tg_ref[...]
            dout_ref[...] = err * (1.0 / D_MODEL)
            loss_ref[...] += jnp.broadcast_to(0.5 * jnp.sum(err * err, keepdims=True) * (1.0 / D_MODEL), loss_ref.shape)

    row = pl.BlockSpec((tt, D_MODEL), lambda i, c: (i, 0))
    vec = _const((1, D_MODEL))
    ffn_cb2 = ffn_cb.reshape(1, 2 * D_FF)
    return pl.pallas_call(
        body, name="ffn_fwd", grid=(s_len // tt, nc),
        in_specs=[pl.BlockSpec((2, tt, cw), lambda i, c: (0, i, c)),
                  pl.BlockSpec((FFN_CONV_K, cw), lambda i, c: (0, c)),
                  pl.BlockSpec((FFN_CONV_K, cw), lambda i, c: (0, c + nc)),
                  pl.BlockSpec((1, cw), lambda i, c: (0, c)),
                  pl.BlockSpec((1, cw), lambda i, c: (0, c + nc)),
                  pl.BlockSpec((cw, D_MODEL), lambda i, c: (c, 0)),
                  row, vec, vec, row],
        out_specs=[pl.BlockSpec((tt, cw), lambda i, c: (i, c)), row, row, _const((SUBLANES, LANES)),
                   pl.BlockSpec((2, tt, cw), lambda i, c: (0, i, c))],
        out_shape=[_sds((s_len, D_FF), BF16), _sds((s_len, D_MODEL), F32), _sds((s_len, D_MODEL), F32),
                   _sds((SUBLANES, LANES), F32), _sds((2, s_len, D_FF), F32)],
        scratch_shapes=[pltpu.VMEM((nc, 2, SUBLANES, cw), F32), pltpu.VMEM((tt, D_MODEL), F32)],
        compiler_params=_cparams(("arbitrary", "arbitrary")),
    )(up_pre, ffn_cw, ffn_cw, ffn_cb2, ffn_cb2, w_down, x1, gt_f, g_post, target)


def _ffn_bwd(dout, y2, up_pre, up, ffn_cw, w_down, gt_f, g_post):
    s_len = dout.shape[0]
    tt = min(TT_BIG, s_len)
    nt = s_len // tt
    cw = FF_CW
    nc = D_FF // cw

    def body(do_ref, y2_ref, up_ref, upc_ref, cwg_ref, cwv_ref, wd_ref, gt_ref, gp_ref,
             dy2_ref, dup_ref, vs_ref, cs_ref, nxt, dy2s, cs_acc):
        i = pl.program_id(0)
        c = pl.program_id(1)

        @pl.when(i == 0)
        def _():
            nxt[c] = jnp.zeros((2, SUBLANES, cw), F32)
            cs_acc[c] = jnp.zeros((2, SUBLANES, cw), F32)

        @pl.when((i == 0) & (c == 0))
        def _():
            vs_ref[...] = jnp.zeros_like(vs_ref)

        @pl.when(c == 0)
        def _():
            n2, r2 = _rms(y2_ref[...])
            do = do_ref[...]
            vs_ref[0:1, :] += _colsum(do * n2 * gp_ref[...])
            vs_ref[1:2, :] += _colsum(do * gt_ref[...] * n2)
            dy2 = _rms_bwd(do * gt_ref[...] * gp_ref[...], n2, r2).astype(BF16)
            dy2s[...] = dy2
            dy2_ref[...] = dy2

        d_act = _dot_nt(dy2s[...], wd_ref[...])
        uv = upc_ref[1]
        gl, dgl = _gelu_and_grad(upc_ref[0])
        d_ug = d_act * uv * dgl
        d_uv = d_act * gl
        for half, (d_u, cw_ref) in enumerate(((d_ug, cwg_ref), (d_uv, cwv_ref))):
            nx = nxt[c, half]
            x_in = up_ref[half]
            d_pre = cw_ref[FFN_CONV_K - 1:FFN_CONV_K, :] * d_u
            sums = [None] * (FFN_CONV_K + 1)
            sums[FFN_CONV_K - 1] = _colsum(d_u * x_in)
            for k in range(FFN_CONV_K - 1):
                ahead = _shift_up(d_u, nx, FFN_CONV_K - 1 - k)
                d_pre = d_pre + cw_ref[k:k + 1, :] * ahead
                sums[k] = _colsum(ahead * x_in)
            sums[FFN_CONV_K] = _colsum(d_u)
            pad = jnp.zeros((SUBLANES - FFN_CONV_K - 1, cw), F32)
            cs_acc[c, half] += jnp.concatenate(sums + [pad], axis=0)
            nxt[c, half] = d_u[0:SUBLANES]
            dup_ref[half] = d_pre.astype(BF16)

        for cc in range(nc):
            @pl.when((i == nt - 1) & (c == cc))
            def _():
                cs_ref[:, cc * cw:(cc + 1) * cw] = cs_acc[cc, 0]
                cs_ref[:, D_FF + cc * cw:D_FF + (cc + 1) * cw] = cs_acc[cc, 1]

    rev = lambda i, c: (nt - 1 - i, 0)
    row = pl.BlockSpec((tt, D_MODEL), rev)
    vec = _const((1, D_MODEL))
    blk = pl.BlockSpec((2, tt, cw), lambda i, c: (0, nt - 1 - i, c))
    return pl.pallas_call(
        body, name="ffn_bwd", grid=(nt, nc),
        in_specs=[row, row, blk, blk,
                  pl.BlockSpec((FFN_CONV_K, cw), lambda i, c: (0, c)),
                  pl.BlockSpec((FFN_CONV_K, cw), lambda i, c: (0, c + nc)),
                  pl.BlockSpec((cw, D_MODEL), lambda i, c: (c, 0)),
                  vec, vec],
        out_specs=[row, blk, _const((SUBLANES, D_MODEL)), _const((SUBLANES, 2 * D_FF))],
        out_shape=[_sds((s_len, D_MODEL), BF16), _sds((2, s_len, D_FF), BF16), _sds((SUBLANES, D_MODEL), F32),
                   _sds((SUBLANES, 2 * D_FF), F32)],
        scratch_shapes=[pltpu.VMEM((nc, 2, SUBLANES, cw), F32), pltpu.VMEM((tt, D_MODEL), BF16),
                        pltpu.VMEM((nc, 2, SUBLANES, cw), F32)],
        compiler_params=_cparams(("arbitrary", "arbitrary")),
    )(dout, y2, up_pre, up, ffn_cw, ffn_cw, w_down, gt_f, g_post)


def _up_bwd(d_up, w_up, x1, dout, y, w_out, g_pre, sc_f, g_post, gt_m, carry=None):
    s_len = x1.shape[0]
    tt = min(TT_BIG, s_len)

    def body(du_ref, wu_ref, x1_ref, do_ref, y_ref, wo_ref, g2_ref, sc_ref, gp_ref, gt_ref,
             dx1_ref, dy_ref, dyc_ref, vs_ref):
        @pl.when(pl.program_id(0) == 0)
        def _():
            vs_ref[...] = jnp.zeros_like(vs_ref)

        d_h2 = _dot_nt(du_ref[0], wu_ref[:, 0:D_FF]) + _dot_nt(du_ref[1], wu_ref[:, D_FF:])
        n1, r1 = _rms(x1_ref[...])
        ng = n1 * g2_ref[...]
        vs_ref[0:1, :] += _colsum(d_h2)
        vs_ref[1:2, :] += _colsum(d_h2 * ng)
        d_ng = d_h2 * (1.0 + sc_ref[...])
        vs_ref[2:3, :] += _colsum(d_ng * n1)
        d_x1 = do_ref[...] + _rms_bwd(d_ng * g2_ref[...], n1, r1)
        dx1_ref[...] = d_x1
        n_y, r_y = _rms(y_ref[...])
        vs_ref[3:4, :] += _colsum(d_x1 * n_y * gp_ref[...])
        d_on = d_x1 * gt_ref[...]
        vs_ref[4:5, :] += _colsum(d_on * n_y)
        d_y = _rms_bwd(d_on * gp_ref[...], n_y, r_y).astype(BF16)
        dy_ref[...] = d_y
        dyc_ref[...] = _dot_nt(d_y, wo_ref[...])

    row = lambda c: pl.BlockSpec((tt, c), lambda i: (i, 0))
    vec = _const((1, D_MODEL))
    return _call(
        body, "up_bwd", (s_len // tt,),
        in_specs=[pl.BlockSpec((2, tt, D_FF), lambda i: (0, i, 0)), _whole(), row(D_MODEL), row(D_MODEL), row(D_MODEL),
                  _whole(), vec, vec, vec, vec],
        out_specs=[row(D_MODEL), row(D_MODEL), row(LRU_W + GMLP_W), _const((SUBLANES, D_MODEL))],
        out_shape=[_sds((s_len, D_MODEL), F32), _sds((s_len, D_MODEL), BF16), _sds((s_len, LRU_W + GMLP_W), F32),
                   _sds((SUBLANES, D_MODEL), F32)],
        scratch=[], args=(d_up, w_up, x1, dout, y, w_out, g_pre, sc_f, g_post, gt_m), carry=carry)


def _mix_bwd(d_ycat, z, hl, conv_w, conv_b, wr_bd, wi_bd, b_r, b_i, lru_a, vn_g, vn_b, w_sp, w_sp_t, b_sp_t,
             g_lru, g_gmlp, carry=None):
    s_len = z.shape[0]
    tt = min(TT_MIX, s_len)
    nt = s_len // tt
    nblk = tt // POS_BLOCK
    hb = tt // SUBLANES

    def body(dyc_ref, z_ref, zh_ref, hl_ref, hh_ref, cw_ref, cb_ref, wr_ref, wi_ref, br_ref, bi_ref, la_ref,
             vg_ref, vb_ref, ws_ref, wst_ref, bst_ref, gl_ref, gg_ref,
             dz_ref, vs_ref, dcw_ref, dwrb_ref, dwib_ref, dws_ref, dbs_ref, nxt_dxc, nxt_a, nxt_lam, dwr_ref, dwi_ref):
        i = pl.program_id(0)
        first_tile = i == nt - 1

        @pl.when(i == 0)
        def _():
            for ref in (vs_ref, dcw_ref, dwr_ref, dwi_ref, dws_ref, dbs_ref, nxt_dxc, nxt_a, nxt_lam):
                ref[...] = jnp.zeros_like(ref)

        lx = z_ref[:, 0:LRU_W]
        gate = z_ref[:, LRU_W:2 * LRU_W]
        gu = z_ref[:, 2 * LRU_W:2 * LRU_W + GMLP_W]
        gv = z_ref[:, 2 * LRU_W + GMLP_W:]
        prev8 = jnp.where(first_tile, 0.0, zh_ref[...])
        hprev8 = jnp.where(first_tile, 0.0, hh_ref[...])

        xc, taps = _lru_conv(lx, prev8, cw_ref, cb_ref[...])
        a_par = la_ref[...]
        sp_a = _softplus(-a_par)
        r, ig, a, mult = _lru_gates(xc, wr_ref, wi_ref, br_ref[...], bi_ref[...], sp_a)
        hl = hl_ref[...]
        h_prev = _shift_down(hl, hprev8, 1)
        ggate, dggate = _gelu_and_grad(gate)
        y_lru = hl * ggate
        n_l, r_l = _rms(y_lru)
        d_nl = dyc_ref[:, 0:LRU_W]
        vs_ref[6:7, :] += _colsum(d_nl * n_l)
        d_yl = _rms_bwd(d_nl * gl_ref[...], n_l, r_l)
        d_hl = d_yl * ggate
        d_gate = d_yl * hl * dggate
        a_up = _shift_up(a, nxt_a[...], 1)
        a_cum, b_cum = _scan_rev(a_up, d_hl)
        lam = b_cum + a_cum * nxt_lam[0:1, :]
        nxt_a[...] = jnp.broadcast_to(a[0:1, :], nxt_a.shape)
        nxt_lam[...] = jnp.broadcast_to(lam[0:1, :], nxt_lam.shape)
        ixc = ig * xc
        d_la = lam * h_prev * a - lam * ixc * (a * a) / mult
        d_i = lam * mult * xc
        d_xc = lam * mult * ig
        vs_ref[3:4, :] += _colsum(d_la * r) * (LRU_C * _sigmoid(-a_par))
        d_pr = d_la * (-LRU_C * sp_a) * r * (1.0 - r)
        d_pi = d_i * ig * (1.0 - ig)
        vs_ref[1:2, :] += _colsum(d_pr)
        vs_ref[2:3, :] += _colsum(d_pi)
        dwr_ref[...] += _dot_tn(xc, d_pr)
        dwi_ref[...] += _dot_tn(xc, d_pi)
        d_xc = d_xc + _dot_nt(d_pr, wr_ref[...]) + _dot_nt(d_pi, wi_ref[...])
        vs_ref[0:1, :] += _colsum(d_xc)
        nx = nxt_dxc[...]
        d_lx = cw_ref[LRU_CONV_K - 1:LRU_CONV_K, :] * d_xc
        dcw_ref[LRU_CONV_K - 1:LRU_CONV_K, :] += _colsum(d_xc * lx)
        for k in range(LRU_CONV_K - 1):
            d_lx = d_lx + cw_ref[k:k + 1, :] * _shift_up(d_xc, nx, LRU_CONV_K - 1 - k)
            dcw_ref[k:k + 1, :] += _colsum(d_xc * taps[k])
        nxt_dxc[...] = d_xc[0:SUBLANES]
        dz_ref[:, 0:LRU_W] = d_lx.astype(BF16)
        dz_ref[:, LRU_W:2 * LRU_W] = d_gate.astype(BF16)

        u, du = _gelu_and_grad(gu)
        v, vhat, rs, dav = _gmlp_v(gv, vg_ref[...], vb_ref[...])
        mask = _ws_mask()
        sp_parts = []
        for nb in range(nblk):
            rowp = []
            for g in range(N_GROUPS):
                wsm = jnp.where(mask, ws_ref[g], 0.0)
                vblk = v[nb * POS_BLOCK:(nb + 1) * POS_BLOCK, g * LANES:(g + 1) * LANES]
                rowp.append(_dot(wsm, vblk) + bst_ref[:, g:g + 1])
            sp_parts.append(jnp.concatenate(rowp, axis=1))
        sp = jnp.concatenate(sp_parts, axis=0) if nblk > 1 else sp_parts[0]
        y_g = u * sp
        n_g, r_g = _rms(y_g)
        d_ng = dyc_ref[:, LRU_W:]
        vs_ref[7:8, :] += _colsum(d_ng * n_g)
        d_yg = _rms_bwd(d_ng * gg_ref[...], n_g, r_g)
        d_gu = d_yg * sp * du
        d_sp = d_yg * u
        mask_t = _ws_mask(transposed=True)
        ones8 = jnp.ones((SUBLANES, LANES), F32)
        dv_parts = []
        for nb in range(nblk):
            rowp = []
            for g in range(N_GROUPS):
                rs_, cs_ = slice(nb * POS_BLOCK, (nb + 1) * POS_BLOCK), slice(g * LANES, (g + 1) * LANES)
                dsp_blk = d_sp[rs_, cs_]
                dbs_ref[g:g + 1, :] += lax.dot_general(
                    ones8, dsp_blk, (((1,), (1,)), ((), ())), preferred_element_type=F32,
                    precision=lax.Precision.HIGHEST)[0:1, :]
                dws_ref[g] += _dot_nt(dsp_blk, v[rs_, cs_])
                wsm_t = jnp.where(mask_t, wst_ref[g], 0.0)
                rowp.append(_dot(wsm_t, dsp_blk))
            dv_parts.append(jnp.concatenate(rowp, axis=1))
        d_v = jnp.concatenate(dv_parts, axis=0) if nblk > 1 else dv_parts[0]
        vs_ref[4:5, :] += _colsum(d_v * vhat)
        vs_ref[5:6, :] += _colsum(d_v)
        d_vh = d_v * vg_ref[...]
        d_av = rs * (d_vh - jnp.mean(d_vh, axis=-1, keepdims=True)
                     - vhat * jnp.mean(d_vh * vhat, axis=-1, keepdims=True))
        dz_ref[:, 2 * LRU_W:2 * LRU_W + GMLP_W] = d_gu.astype(BF16)
        dz_ref[:, 2 * LRU_W + GMLP_W:] = (d_av * dav).astype(BF16)

        @pl.when(i == nt - 1)
        def _():
            for hd in range(N_HEADS):
                blk = slice(hd * HEAD_DIM, (hd + 1) * HEAD_DIM)
                dwrb_ref[blk, :] = dwr_ref[blk, blk]
                dwib_ref[blk, :] = dwi_ref[blk, blk]
            for g in range(N_GROUPS):
                dws_ref[g] = jnp.where(mask, dws_ref[g], 0.0)

    rev = lambda c: pl.BlockSpec((tt, c), lambda i: (nt - 1 - i, 0))
    halo = pl.BlockSpec((SUBLANES, LRU_W), lambda i: (jnp.maximum((nt - 1 - i) * hb - 1, 0), 0))
    v512 = _const((1, LRU_W))
    return _call(
        body, "mix_bwd", (nt,),
        in_specs=[rev(LRU_W + GMLP_W), rev(IN_COLS), halo, rev(LRU_W), halo,
                  _const((LRU_CONV_K, LRU_W)), v512, _whole(), _whole(), v512, v512, v512, v512, v512,
                  _whole(), _whole(), _whole(), v512, v512],
        out_specs=[rev(IN_COLS), _const((SUBLANES, LRU_W)), _const((SUBLANES, LRU_W)),
                   _const((LRU_W, HEAD_DIM)), _const((LRU_W, HEAD_DIM)),
                   _const((N_GROUPS, POS_BLOCK, POS_BLOCK)), _const((SUBLANES, POS_BLOCK))],
        out_shape=[_sds((s_len, IN_COLS), BF16), _sds((SUBLANES, LRU_W), F32), _sds((SUBLANES, LRU_W), F32),
                   _sds((LRU_W, HEAD_DIM), F32), _sds((LRU_W, HEAD_DIM), F32),
                   _sds((N_GROUPS, POS_BLOCK, POS_BLOCK), F32), _sds((SUBLANES, POS_BLOCK), F32)],
        scratch=[pltpu.VMEM((SUBLANES, LRU_W), F32), pltpu.VMEM((SUBLANES, LRU_W), F32),
                 pltpu.VMEM((SUBLANES, LRU_W), F32), pltpu.VMEM((LRU_W, LRU_W), F32), pltpu.VMEM((LRU_W, LRU_W), F32)],
        args=(d_ycat, z, z, hl, hl, conv_w, conv_b, wr_bd, wi_bd, b_r, b_i, lru_a, vn_g, vn_b, w_sp, w_sp_t, b_sp_t,
              g_lru, g_gmlp), carry=carry)


def _in_bwd(d_z, w_in, x, d_x1, g, sc, carry=None):
    s_len = x.shape[0]
    tt = min(TT_BIG, s_len)

    def body(dz_ref, w_ref, x_ref, dx1_ref, g_ref, sc_ref, gx_ref, vs_ref):
        @pl.when(pl.program_id(0) == 0)
        def _():
            vs_ref[...] = jnp.zeros_like(vs_ref)

        d_h = _dot_nt(dz_ref[...], w_ref[...])
        n, r = _rms(x_ref[...])
        vs_ref[0:1, :] += _colsum(d_h)
        vs_ref[1:2, :] += _colsum(d_h * n * g_ref[...])
        d_ng = d_h * (1.0 + sc_ref[...])
        vs_ref[2:3, :] += _colsum(d_ng * n)
        gx_ref[...] = dx1_ref[...] + _rms_bwd(d_ng * g_ref[...], n, r)

    row = lambda c: pl.BlockSpec((tt, c), lambda i: (i, 0))
    vec = _const((1, D_MODEL))
    return _call(
        body, "in_bwd", (s_len // tt,),
        in_specs=[row(IN_COLS), _whole(), row(D_MODEL), row(D_MODEL), vec, vec],
        out_specs=[row(D_MODEL), _const((SUBLANES, D_MODEL))],
        out_shape=[_sds((s_len, D_MODEL), F32), _sds((SUBLANES, D_MODEL), F32)],
        scratch=[], args=(d_z, w_in, x, d_x1, g, sc), carry=carry)


def _wgrad(a, b, tn, name, carry=None):
    s_len, k_dim = a.shape
    halves = b.ndim == 3
    n_dim = b.shape[-1] * (2 if halves else 1)
    ts = min(TT_WG, s_len)
    nj = n_dim // tn
    nt = s_len // ts

    def body(a_ref, b_ref, o_ref, ob_ref):
        t = pl.program_id(1)
        part = _dot_tn(a_ref[...], b_ref[0] if halves else b_ref[...])

        @pl.when(t == 0)
        def _():
            o_ref[...] = part

        @pl.when(t > 0)
        def _():
            o_ref[...] += part

        @pl.when(t == nt - 1)
        def _():
            ob_ref[...] = o_ref[...].astype(BF16)

    if halves:
        per_half = nj // 2
        b_spec = pl.BlockSpec((1, ts, tn), lambda j, t: (j // per_half, t, j % per_half))
    else:
        b_spec = pl.BlockSpec((ts, tn), lambda j, t: (t, j))
    o_spec = pl.BlockSpec((k_dim, tn), lambda j, t: (0, j))
    return _call(
        body, name, (nj, nt),
        in_specs=[pl.BlockSpec((ts, k_dim), lambda j, t: (t, 0)), b_spec],
        out_specs=[o_spec, o_spec],
        out_shape=[_sds((k_dim, n_dim), F32), _sds((k_dim, n_dim), BF16)],
        scratch=[], args=(a, b), carry=carry)


def _adam_math(w, g, m, v):
    m = ADAM_B1 * m + (1.0 - ADAM_B1) * g
    v = ADAM_B2 * v + (1.0 - ADAM_B2) * (g * g)
    m_hat = m / (1.0 - ADAM_B1 ** ADAM_STEP)
    v_hat = v / (1.0 - ADAM_B2 ** ADAM_STEP)
    delta = -ADAM_LR * (m_hat / (jnp.sqrt(v_hat) + ADAM_EPS) + ADAM_WD * w)
    return delta, m, v


def _row_tile(rows, cols, n_f32_arrays):
    budget = VMEM_LIMIT // 2
    tr = rows
    while tr % 2 == 0 and tr // 2 >= SUBLANES and (tr // 2) % SUBLANES == 0 and tr * cols * 4 * n_f32_arrays * 2 > budget:
        tr //= 2
    return tr


def _adamw_sum(w, g_full, recv, m, v, col_sharded, name):
    _, rows, cols = w.shape
    n_recv = recv.shape[0]
    tr = _row_tile(rows, cols, 10)
    nb = rows // tr

    def body(me_ref, w_ref, g_ref, r_ref, m_ref, v_ref, go_ref, d_ref, mo_ref, vo_ref):
        g = g_ref[...]
        for k in range(n_recv):
            g = g + r_ref[k].astype(F32)
        go_ref[0] = g
        d_ref[0], mo_ref[0], vo_ref[0] = _adam_math(w_ref[0], g, m_ref[0], v_ref[0])

    if col_sharded:
        own = pl.BlockSpec((tr, cols), lambda i, me: (i, me[0]))
    else:
        own = pl.BlockSpec((tr, cols), lambda i, me: (me[0] * nb + i, 0))
    blk = pl.BlockSpec((1, tr, cols), lambda i, me: (0, i, 0))
    return pl.pallas_call(
        body, name=name,
        grid_spec=pltpu.PrefetchScalarGridSpec(
            num_scalar_prefetch=1, grid=(nb,),
            in_specs=[blk, own, pl.BlockSpec((n_recv, tr, cols), lambda i, me: (0, i, 0)), blk, blk],
            out_specs=[blk] * 4),
        out_shape=[_sds((1, rows, cols), F32)] * 4,
        compiler_params=_cparams(("arbitrary",)),
    )(jnp.reshape(_dev_index(_my_pos()), (1,)).astype(jnp.int32), w, g_full, recv, m, v)


def _row_of_each(ref, row):
    cols = ref.shape[1]
    rows = _rows((N_DEV, cols))
    out = jnp.zeros((N_DEV, cols), F32)
    for d in range(N_DEV):
        picked = ref[d * SUBLANES + row:d * SUBLANES + row + 1, :]
        out = jnp.where(rows == d, jnp.broadcast_to(picked, (N_DEV, cols)), out)
    return out


def _my_columns(full, width, me):
    out = jnp.zeros(full.shape[:-1] + (width,), F32)
    for d in range(N_DEV):
        out = out + jnp.where(me == d, full[:, d * width:(d + 1) * width], 0.0)
    return out


def _adamw_wada(c_all, vs_in_all, vs_up_all, vs_ffn_all, w, m, v):
    _, rows, cols = w.shape

    def body(c_ref, vi_ref, vu_ref, vf_ref, w_ref, m_ref, v_ref, go_ref, d_ref, mo_ref, vo_ref):
        me = _dev_index(_my_pos())
        cv = _row_of_each(c_ref, 0)
        ca = cv * _sigmoid(cv)
        dmod = jnp.concatenate([_row_of_each(vi_ref, 0), _row_of_each(vi_ref, 1), _row_of_each(vu_ref, 3),
                                _row_of_each(vu_ref, 0), _row_of_each(vu_ref, 1), _row_of_each(vf_ref, 0)], axis=1)
        dm = _my_columns(dmod, cols, me)
        g = lax.dot_general(ca, dm, (((0,), (0,)), ((), ())), preferred_element_type=F32,
                            precision=lax.Precision.HIGHEST)
        go_ref[0] = g
        d_ref[0], mo_ref[0], vo_ref[0] = _adam_math(w_ref[0], g, m_ref[0], v_ref[0])

    return pl.pallas_call(
        body, name="adamw_w_ada", out_shape=[_sds((1, rows, cols), F32)] * 4,
        in_specs=[_whole()] * 7, out_specs=[_whole()] * 4,
        compiler_params=_cparams(),
    )(c_all, vs_in_all, vs_up_all, vs_ffn_all, w, m, v)


def _adamw_small(gathered, params, conv_params):
    names = list(params) + list(conv_params)
    allp = {**params, **conv_params}
    n_g = len(gathered)

    def body(*refs):
        g_refs = refs[:n_g]
        p_refs = refs[n_g:n_g + 3 * len(names)]
        o_refs = refs[n_g + 3 * len(names):]
        me = _dev_index(_my_pos())

        def total(ref):
            r = ref.shape[0] // N_DEV
            s = ref[0:r, :]
            for d in range(1, N_DEV):
                s = s + ref[d * r:(d + 1) * r, :]
            return s

        vs_in, vs_up, vs_ffn, cs, vs_mix, dcw, dwr, dwi, dws, dbs = [total(r) for r in g_refs]
        mine = lambda full, width: _my_columns(full, width, me)

        all_ = (slice(None), slice(None))
        heads = lambda row: [((0, slice(h, h + 1), slice(None)), row[:, h * HEAD_DIM:(h + 1) * HEAD_DIM])
                             for h in range(N_HEADS)]
        blocks = lambda dense: [((0, h), dense[h * HEAD_DIM:(h + 1) * HEAD_DIM, :]) for h in range(N_HEADS)]
        pieces = {
            "b_ada": [((slice(None), slice(k * D_MODEL, (k + 1) * D_MODEL)), row) for k, row in enumerate(
                (vs_in[0:1], vs_in[1:2], vs_up[3:4], vs_up[0:1], vs_up[1:2], vs_ffn[0:1]))],
            "g_mix_pre": [(all_, vs_in[2:3])], "g_mix_post": [(all_, vs_up[4:5])],
            "g_ffn_pre": [(all_, vs_up[2:3])], "g_ffn_post": [(all_, vs_ffn[1:2])],
            "conv_b": [(all_, vs_mix[0:1])], "b_rgate": heads(vs_mix[1:2]), "b_igate": heads(vs_mix[2:3]),
            "lru_a": [(all_, vs_mix[3:4])], "v_norm_g": [(all_, vs_mix[4:5])], "v_norm_b": [(all_, vs_mix[5:6])],
            "g_lru_out": [(all_, vs_mix[6:7])], "g_gmlp_out": [(all_, vs_mix[7:8])],
            "w_rgate": blocks(dwr), "w_igate": blocks(dwi),
            "w_spatial": [((0, g), dws[g * POS_BLOCK:(g + 1) * POS_BLOCK, :]) for g in range(N_GROUPS)],
            "b_spatial": [((0,), dbs[0:N_GROUPS])],
            "ffn_conv_b": [(all_, cs[FFN_CONV_K:FFN_CONV_K + 1])],
            "conv_w": [((0,), mine(dcw[0:LRU_CONV_K], LRU_W // N_DEV))],
            "ffn_conv_w": [((0,), mine(cs[0:FFN_CONV_K], 2 * D_FF // N_DEV))],
        }
        for n_i, name in enumerate(names):
            w_ref, m_ref, v_ref = p_refs[3 * n_i:3 * n_i + 3]
            go_ref, d_ref, mo_ref, vo_ref = o_refs[4 * n_i:4 * n_i + 4]
            for idx, g in pieces[name]:
                go_ref[idx] = g
                d_ref[idx], mo_ref[idx], vo_ref[idx] = _adam_math(w_ref[idx], g, m_ref[idx], v_ref[idx])

    flat_params = [a for n in names for a in allp[n]]
    out_shape = [_sds(allp[n][0].shape, F32) for n in names for _ in range(4)]
    outs = pl.pallas_call(
        body, name="adamw_small", out_shape=out_shape,
        in_specs=[_whole()] * (n_g + len(flat_params)), out_specs=[_whole()] * len(out_shape),
        compiler_params=_cparams(),
    )(*gathered, *flat_params)
    return {n: outs[4 * i:4 * i + 4] for i, n in enumerate(names)}


def _mod_part(c_all, w_ada, b_ada):
    cols = w_ada.shape[1]

    def body(c_ref, w_ref, b_ref, o_ref):
        cv = _row_of_each(c_ref, 0)
        ca = cv * _sigmoid(cv)
        b_cols = _my_columns(b_ref[...], cols, _dev_index(_my_pos()))
        o_ref[...] = jnp.dot(ca, w_ref[...], preferred_element_type=F32, precision=lax.Precision.HIGHEST) + b_cols

    return pl.pallas_call(
        body, name="mod_part", out_shape=_sds((N_DEV, cols), F32),
        in_specs=[_whole()] * 3, out_specs=_whole(), compiler_params=_cparams(),
    )(c_all, w_ada, b_ada)


def _my_pos():
    return lax.axis_index("x"), lax.axis_index("y"), lax.axis_index("c")


def _flip(pos, k):
    x, y, c = pos
    return (1 - x if k & 4 else x, 1 - y if k & 2 else y, 1 - c if k & 1 else c)


def _dev_index(pos):
    x, y, c = pos
    return 4 * x + 2 * y + c


def _gather_multi(arrays, name):
    n = len(arrays)

    def body(*refs):
        ins, outs = refs[:n], refs[n:2 * n]
        send_sems, recv_sems = refs[2 * n:]
        me = _my_pos()

        def slot(a, pos):
            rows = ins[a].shape[0]
            return outs[a].at[pl.ds(pl.multiple_of(_dev_index(pos) * rows, SUBLANES), rows), :]

        def copy(a, k):
            return pltpu.make_async_remote_copy(
                src_ref=ins[a], dst_ref=slot(a, me), send_sem=send_sems.at[a, k - 1], recv_sem=recv_sems.at[a, k - 1],
                device_id=_flip(me, k), device_id_type=MESH)

        sends = [copy(a, k) for a in range(n) for k in range(1, N_DEV)]
        for cp in sends:
            cp.start()
        for a in range(n):
            rows = ins[a].shape[0]
            outs[a][pl.ds(pl.multiple_of(_dev_index(me) * rows, SUBLANES), rows), :] = ins[a][...]
        for a in range(n):
            for k in range(1, N_DEV):
                pltpu.make_async_remote_copy(
                    src_ref=ins[a], dst_ref=slot(a, _flip(me, k)), send_sem=send_sems.at[a, k - 1],
                    recv_sem=recv_sems.at[a, k - 1], device_id=_flip(me, k), device_id_type=MESH).wait_recv()
        for cp in sends:
            cp.wait_send()

    return pl.pallas_call(
        body, name=name, out_shape=[_sds((N_DEV * a.shape[0], a.shape[1]), F32) for a in arrays],
        in_specs=[_whole()] * n, out_specs=[_whole()] * n,
        scratch_shapes=[pltpu.SemaphoreType.DMA((n, N_DEV - 1)), pltpu.SemaphoreType.DMA((n, N_DEV - 1))],
        compiler_params=pltpu.CompilerParams(vmem_limit_bytes=VMEM_LIMIT),
    )(*arrays)


def _region(ref, shard_shape, col_sharded, pos):
    r, cdim = shard_shape
    d = _dev_index(pos)
    if col_sharded:
        return ref.at[:, pl.ds(pl.multiple_of(d * cdim, LANES), cdim)]
    return ref.at[pl.ds(pl.multiple_of(d * r, 2 * SUBLANES), r), :]


def _gather_carry(shards, col_sharded):
    n_w = len(shards)
    shapes = [tuple(s.shape) for s in shards]
    full_shapes = [(s[0], s[1] * N_DEV) if cs else (s[0] * N_DEV, s[1]) for s, cs in zip(shapes, col_sharded)]

    def tools(out_refs, scr):
        send_sems, recv_sems = scr[n_w], scr[n_w + 1]
        me = _my_pos()
        x, y, c = me
        sibling = (x, y, 1 - c)
        chips = [(1 - x, y), (x, 1 - y), (1 - x, 1 - y)]

        def region(w, pos):
            return _region(out_refs[w], shapes[w], col_sharded[w], pos)

        def copy(w, k, block, to, src=None):
            return pltpu.make_async_remote_copy(
                src_ref=region(w, block) if src is None else src, dst_ref=region(w, block),
                send_sem=send_sems.at[w, k], recv_sem=recv_sems.at[w, k], device_id=to, device_id_type=MESH)

        def first(w):
            return [copy(w, 0, me, sibling, src=scr[w])] + [
                copy(w, 1 + j, me, (*chip, c), src=scr[w]) for j, chip in enumerate(chips)]

        def mine(w):
            return pltpu.make_async_copy(scr[w], region(w, me), scr[n_w + 2].at[w])

        return me, c, sibling, chips, copy, first, mine

    def start(ins, outs, scr):
        _, _, _, _, _, first, mine = tools(outs, scr)
        for w in range(n_w):
            scr[w][...] = ins[w][...].astype(BF16)
            for cp in first(w) + [mine(w)]:
                cp.start()

    def finish(ins, outs, scr):
        me, c, sibling, chips, copy, first, mine = tools(outs, scr)
        passed = []
        for w in range(n_w):
            for j, chip in enumerate(chips):
                copy(w, 1 + j, (*chip, c), me).wait_recv()
                fwd = copy(w, 4 + j, (*chip, c), sibling)
                fwd.start()
                passed.append(fwd)
        for w in range(n_w):
            copy(w, 0, sibling, me).wait_recv()
            for j, chip in enumerate(chips):
                copy(w, 4 + j, (*chip, 1 - c), me).wait_recv()
        for w in range(n_w):
            for cp in first(w):
                cp.wait_send()
            mine(w).wait()
        for cp in passed:
            cp.wait_send()

    return _Carry(
        inputs=list(shards), in_specs=[_whole()] * n_w,
        out_shape=[_sds(s, BF16) for s in full_shapes], out_specs=[_any()] * n_w,
        scratch=[pltpu.VMEM(s, BF16) for s in shapes]
        + [pltpu.SemaphoreType.DMA((n_w, N_DEV - 1)), pltpu.SemaphoreType.DMA((n_w, N_DEV - 1)),
           pltpu.SemaphoreType.DMA((n_w,))],
        start=start, finish=finish)


def _scatter_carry(grads_bf, shard_shapes, col_sharded):
    n_w = len(grads_bf)
    shapes = [tuple(s) for s in shard_shapes]

    def copies(ins, outs, scr):
        send_sems, recv_sems = scr
        me = _my_pos()
        out = []
        for w in range(n_w):
            for k in range(1, N_DEV):
                peer = _flip(me, k)
                out.append(pltpu.make_async_remote_copy(
                    src_ref=_region(ins[w], shapes[w], col_sharded[w], peer), dst_ref=outs[w].at[k - 1],
                    send_sem=send_sems.at[w, k - 1], recv_sem=recv_sems.at[w, k - 1],
                    device_id=peer, device_id_type=MESH))
        return out

    def start(ins, outs, scr):
        for cp in copies(ins, outs, scr):
            cp.start()

    def finish(ins, outs, scr):
        cps = copies(ins, outs, scr)
        for cp in cps:
            cp.wait_recv()
        for cp in cps:
            cp.wait_send()

    return _Carry(
        inputs=list(grads_bf), in_specs=[_any()] * n_w,
        out_shape=[_sds((N_DEV - 1,) + s, BF16) for s in shapes], out_specs=[_any()] * n_w,
        scratch=[pltpu.SemaphoreType.DMA((n_w, N_DEV - 1)), pltpu.SemaphoreType.DMA((n_w, N_DEV - 1))],
        start=start, finish=finish)


def _comm_only(carry, name):
    def body(o_ref):
        o_ref[...] = jnp.zeros_like(o_ref)

    _, outs = _call(body, name, (1,), in_specs=[], out_specs=[_const((SUBLANES, LANES))],
                    out_shape=[_sds((SUBLANES, LANES), F32)], scratch=[], args=(), carry=carry)
    return outs


def _block_diag(w):
    eye = jnp.eye(N_HEADS, dtype=w.dtype)
    return (eye[:, None, :, None] * w[:, :, None, :]).reshape(N_HEADS * HEAD_DIM, N_HEADS * HEAD_DIM)


def _pad_rows(a):
    return jnp.pad(a, ((0, SUBLANES - a.shape[0]), (0, 0)))


def _columns_from_devices(gathered, rows):
    w = gathered.shape[1]
    return gathered.reshape(N_DEV, SUBLANES, w)[:, :rows].transpose(1, 0, 2).reshape(rows, N_DEV * w)


def _local_step(x2, target, mod, w_in_f, w_full, conv_w_full, ffn_cw_full,
                g_mix_pre, g_mix_post, conv_b, w_rgate, b_rgate, w_igate, b_igate, lru_a, v_norm_g, v_norm_b,
                w_spatial, b_spatial, g_lru_out, g_gmlp_out, g_ffn_pre, g_ffn_post, ffn_conv_b,
                gather=None, scatter=None):
    sh_m, sc_m, gt_m, sh_f, sc_f, gt_f = [mod[k] for k in range(N_MOD)]
    wr_bd = _block_diag(w_rgate[0]).astype(BF16)
    wi_bd = _block_diag(w_igate[0]).astype(BF16)
    b_r = b_rgate.reshape(1, LRU_W)
    b_i = b_igate.reshape(1, LRU_W)
    b_sp_t = b_spatial[0].T
    w_sp_t = jnp.swapaxes(w_spatial[0], 1, 2)

    def arriving(name):
        return gather(name) if gather else None

    def leaving(name, grad_bf):
        return scatter(name, grad_bf) if scatter else None

    def landed(name, carried):
        return carried[0] if gather else w_full[name]

    (z, h), got = _in_fwd(x2, sh_m, sc_m, g_mix_pre, w_in_f, carry=arriving("w_out"))
    w_out_f = landed("w_out", got)
    mix_params = (conv_w_full, conv_b, wr_bd, wi_bd, b_r, b_i, lru_a, v_norm_g, v_norm_b)
    (ycat, hl), got = _mix_fwd(z, *mix_params, w_spatial[0], b_sp_t, g_lru_out, g_gmlp_out, carry=arriving("w_up"))
    w_up_f = landed("w_up", got)
    (y, x1, h2, up_pre), got = _out_up_fwd(ycat, x2, w_out_f, g_mix_post, gt_m, g_ffn_pre, sc_f, sh_f, w_up_f,
                                           carry=arriving("w_down"))
    w_down_f = landed("w_down", got)
    act, y2, dout, loss_acc, up = _ffn_fwd(up_pre, ffn_cw_full, ffn_conv_b, w_down_f, x1, gt_f, g_ffn_post, target)

    recv = {}
    d_y2, d_up, vs_ffn, cs_ffn = _ffn_bwd(dout, y2, up_pre, up, ffn_cw_full, w_down_f, gt_f, g_ffn_post)
    gw_down, _ = _wgrad(act, d_y2, D_MODEL // 2, "wgrad_down")
    (d_x1, d_y, d_ycat, vs_up), recv["w_down"] = _up_bwd(
        d_up, w_up_f, x1, dout, y, w_out_f, g_ffn_pre, sc_f, g_mix_post, gt_m, carry=leaving("w_down", gw_down[1]))
    gw_up, _ = _wgrad(h2, d_up, D_FF // 2, "wgrad_up")
    gw_out, _ = _wgrad(ycat, d_y, D_MODEL, "wgrad_out")
    (d_z, vs_mix, dcw, d_wr, d_wi, d_ws, d_bs), recv["w_up"] = _mix_bwd(
        d_ycat, z, hl, *mix_params, w_spatial[0], w_sp_t, b_sp_t, g_lru_out, g_gmlp_out,
        carry=leaving("w_up", gw_up[1]))
    gw_in, recv["w_out"] = _wgrad(h, d_z, IN_COLS // 2, "wgrad_in", carry=leaving("w_out", gw_out[1]))
    (grad_x, vs_in), recv["w_in"] = _in_bwd(d_z, w_in_f, x2, d_x1, g_mix_pre, sc_m, carry=leaving("w_in", gw_in[1]))

    small = [vs_in, vs_up, vs_ffn, cs_ffn, vs_mix, dcw, d_wr, d_wi, d_ws.reshape(N_GROUPS * POS_BLOCK, POS_BLOCK), d_bs]
    return dict(loss_acc=loss_acc[0, 0], grad_x=grad_x, small=small, recv=recv,
                w_in=gw_in, w_out=gw_out, w_up=gw_up, w_down=gw_down)


def kernel(x, c, w_ada, b_ada, g_mix_pre, g_mix_post, w_in, conv_w, conv_b, w_rgate, b_rgate, w_igate, b_igate, lru_a, v_norm_g, v_norm_b, w_spatial, b_spatial, g_lru_out, g_gmlp_out, w_out, g_ffn_pre, g_ffn_post, w_up, ffn_conv_w, ffn_conv_b, w_down, loss_target, m_w_ada, m_b_ada, m_g_mix_pre, m_g_mix_post, m_w_in, m_conv_w, m_conv_b, m_w_rgate, m_b_rgate, m_w_igate, m_b_igate, m_lru_a, m_v_norm_g, m_v_norm_b, m_w_spatial, m_b_spatial, m_g_lru_out, m_g_gmlp_out, m_w_out, m_g_ffn_pre, m_g_ffn_post, m_w_up, m_ffn_conv_w, m_ffn_conv_b, m_w_down, v_w_ada, v_b_ada, v_g_mix_pre, v_g_mix_post, v_w_in, v_conv_w, v_conv_b, v_w_rgate, v_b_rgate, v_w_igate, v_b_igate, v_lru_a, v_v_norm_g, v_v_norm_b, v_w_spatial, v_b_spatial, v_g_lru_out, v_g_gmlp_out, v_w_out, v_g_ffn_pre, v_g_ffn_post, v_w_up, v_ffn_conv_w, v_ffn_conv_b, v_w_down):
    me = _dev_index(_my_pos())
    ada_cols = w_ada.shape[-1]

    c_all, cw_all, fcw_all = _gather_multi(
        [jnp.broadcast_to(c, (SUBLANES, D_MODEL)), _pad_rows(conv_w[0]), _pad_rows(ffn_conv_w[0])], "gather_start")
    conv_w_full = _columns_from_devices(cw_all, LRU_CONV_K)
    ffn_cw_full = _columns_from_devices(fcw_all, FFN_CONV_K)
    mod_mine = _mod_part(c_all, w_ada[0], b_ada)
    (mod_all,) = _gather_multi([mod_mine], "gather_mod")
    mod = lax.dynamic_index_in_dim(mod_all.reshape(N_DEV, N_DEV, ada_cols), me, axis=1, keepdims=False)
    mod = mod.reshape(N_MOD, 1, D_MODEL)

    big_w = dict(w_in=(w_in, m_w_in, v_w_in, True), w_out=(w_out, m_w_out, v_w_out, False),
                 w_up=(w_up, m_w_up, v_w_up, True), w_down=(w_down, m_w_down, v_w_down, False))

    def gather(name):
        return _gather_carry([big_w[name][0][0]], [big_w[name][3]])

    def scatter(name, grad_bf):
        return _scatter_carry([grad_bf], [big_w[name][0].shape[1:]], [big_w[name][3]])

    (w_in_f,) = _comm_only(gather("w_in"), "gather_w_in")

    loc = _local_step(x[0], loss_target[0], mod, w_in_f, None, conv_w_full, ffn_cw_full,
                      g_mix_pre, g_mix_post, conv_b, w_rgate, b_rgate, w_igate, b_igate, lru_a, v_norm_g, v_norm_b,
                      w_spatial, b_spatial, g_lru_out, g_gmlp_out, g_ffn_pre, g_ffn_post, ffn_conv_b,
                      gather=gather, scatter=scatter)
    loss = lax.psum(loc["loss_acc"], ("x", "y", "c"))
    grad_x = loc["grad_x"]

    results = {}
    for name, (w_, m_, v_, cs) in big_w.items():
        results[name] = _adamw_sum(w_, loc[name][0], loc["recv"][name][0], m_, v_, cs, "adamw_" + name)

    gathered = _gather_multi(loc["small"], "gather_grads")
    params = dict(
        b_ada=(b_ada, m_b_ada, v_b_ada), g_mix_pre=(g_mix_pre, m_g_mix_pre, v_g_mix_pre),
        g_mix_post=(g_mix_post, m_g_mix_post, v_g_mix_post), conv_b=(conv_b, m_conv_b, v_conv_b),
        w_rgate=(w_rgate, m_w_rgate, v_w_rgate), b_rgate=(b_rgate, m_b_rgate, v_b_rgate),
        w_igate=(w_igate, m_w_igate, v_w_igate), b_igate=(b_igate, m_b_igate, v_b_igate),
        lru_a=(lru_a, m_lru_a, v_lru_a), v_norm_g=(v_norm_g, m_v_norm_g, v_v_norm_g),
        v_norm_b=(v_norm_b, m_v_norm_b, v_v_norm_b), w_spatial=(w_spatial, m_w_spatial, v_w_spatial),
        b_spatial=(b_spatial, m_b_spatial, v_b_spatial), g_lru_out=(g_lru_out, m_g_lru_out, v_g_lru_out),
        g_gmlp_out=(g_gmlp_out, m_g_gmlp_out, v_g_gmlp_out), g_ffn_pre=(g_ffn_pre, m_g_ffn_pre, v_g_ffn_pre),
        g_ffn_post=(g_ffn_post, m_g_ffn_post, v_g_ffn_post), ffn_conv_b=(ffn_conv_b, m_ffn_conv_b, v_ffn_conv_b))
    conv_params = dict(conv_w=(conv_w, m_conv_w, v_conv_w), ffn_conv_w=(ffn_conv_w, m_ffn_conv_w, v_ffn_conv_w))
    results.update(_adamw_small(gathered, params, conv_params))

    results["w_ada"] = _adamw_wada(c_all, gathered[0], gathered[1], gathered[2], w_ada, m_w_ada, v_w_ada)

    order = ["w_ada", "b_ada", "g_mix_pre", "g_mix_post", "w_in", "conv_w", "conv_b", "w_rgate", "b_rgate", "w_igate",
             "b_igate", "lru_a", "v_norm_g", "v_norm_b", "w_spatial", "b_spatial", "g_lru_out", "g_gmlp_out", "w_out",
             "g_ffn_pre", "g_ffn_post", "w_up", "ffn_conv_w", "ffn_conv_b", "w_down"]
    outs = [loss, grad_x[None]]
    for kind in range(4):
        outs += [results[n][kind] for n in order]
    return tuple(outs)
```

```python
import functools

import jax
import jax.numpy as jnp
from jax import lax
from jax.experimental import pallas as pl
from jax.experimental.pallas import tpu as pltpu

F32 = jnp.float32
BF16 = jnp.bfloat16

D_MODEL = 1024
LRU_W = 512
GMLP_W = 512
N_HEADS = 8
HEAD_DIM = 64
N_GROUPS = 4
POS_BLOCK = 128
CHUNK = 64
IN_COLS = 2048
D_FF = 3072
N_MOD = 6
N_DEV = 8
EPS = 1e-6
LRU_C = 8.0
LRU_CONV_K = 4
FFN_CONV_K = 3

ADAM_LR = 0.001
ADAM_B1 = 0.9
ADAM_B2 = 0.999
ADAM_EPS = 1e-08
ADAM_WD = 0.01
ADAM_STEP = 10

LANES = 128
SUBLANES = 8
TT_BIG = 512
TT_MIX = 256
TT_WG = 1024
FF_CW = 512
VMEM_LIMIT = 56 * 1024 * 1024

MESH = pl.DeviceIdType.MESH


def _sds(shape, dtype):
    return jax.ShapeDtypeStruct(shape, dtype)


def _cparams(sem=None):
    return pltpu.CompilerParams(dimension_semantics=sem, vmem_limit_bytes=VMEM_LIMIT)


def _whole():
    return pl.BlockSpec(memory_space=pltpu.VMEM)


def _const(shape):
    nd = len(shape)
    return pl.BlockSpec(shape, lambda *_: (0,) * nd)


def _any():
    return pl.BlockSpec(memory_space=pl.ANY)


class _Carry:
    def __init__(self, inputs, in_specs, out_shape, out_specs, scratch, start, finish):
        self.inputs, self.in_specs, self.out_shape, self.out_specs = inputs, in_specs, out_shape, out_specs
        self.scratch, self.start, self.finish = scratch, start, finish


def _call(body, name, grid, in_specs, out_specs, out_shape, scratch, args, carry=None):
    n_in, n_out, n_scr = len(in_specs), len(out_specs), len(scratch)
    c_in = len(carry.in_specs) if carry else 0
    c_out = len(carry.out_specs) if carry else 0

    def full_body(*refs):
        ins = refs[:n_in]
        c_ins = refs[n_in:n_in + c_in]
        outs = refs[n_in + c_in:n_in + c_in + n_out]
        c_outs = refs[n_in + c_in + n_out:n_in + c_in + n_out + c_out]
        scr = refs[n_in + c_in + n_out + c_out:n_in + c_in + n_out + c_out + n_scr]
        c_scr = refs[n_in + c_in + n_out + c_out + n_scr:]
        if carry:
            first = functools.reduce(lambda a, b: a & b, [pl.program_id(d) == 0 for d in range(len(grid))])
            last = functools.reduce(lambda a, b: a & b, [pl.program_id(d) == g - 1 for d, g in enumerate(grid)])

            @pl.when(first)
            def _():
                carry.start(c_ins, c_outs, c_scr)

        body(*ins, *outs, *scr)
        if carry:
            @pl.when(last)
            def _():
                carry.finish(c_ins, c_outs, c_scr)

    res = pl.pallas_call(
        full_body, name=name, grid=grid,
        in_specs=list(in_specs) + (list(carry.in_specs) if carry else []),
        out_specs=list(out_specs) + (list(carry.out_specs) if carry else []),
        out_shape=list(out_shape) + (list(carry.out_shape) if carry else []),
        scratch_shapes=list(scratch) + (list(carry.scratch) if carry else []),
        compiler_params=_cparams(("arbitrary",) * len(grid)),
    )(*args, *(carry.inputs if carry else []))
    return res[:n_out], res[n_out:]


def _gelu(x):
    u = 0.7978845608028654 * (x + 0.044715 * x * x * x)
    return 0.5 * x * (1.0 + jnp.tanh(u))


def _gelu_and_grad(x):
    x2 = x * x
    u = 0.7978845608028654 * (x + 0.044715 * x * x2)
    t = jnp.tanh(u)
    g = 0.5 * x * (1.0 + t)
    dg = 0.5 * (1.0 + t) + 0.5 * x * (1.0 - t * t) * 0.7978845608028654 * (1.0 + 3.0 * 0.044715 * x2)
    return g, dg


def _sigmoid(x):
    return 1.0 / (1.0 + jnp.exp(-x))


def _softplus(x):
    return jnp.maximum(x, 0.0) + jnp.log1p(jnp.exp(-jnp.abs(x)))


def _neg_expm1(x):
    series = -x * (1.0 + x * (0.5 + x * (1.0 / 6.0 + x * (1.0 / 24.0 + x * (1.0 / 120.0)))))
    return jnp.where(x > -0.1, series, 1.0 - jnp.exp(x))


def _dot(a, b):
    return jnp.dot(a.astype(BF16), b.astype(BF16), preferred_element_type=F32)


def _dot_nt(a, b):
    return lax.dot_general(a.astype(BF16), b.astype(BF16), (((1,), (1,)), ((), ())), preferred_element_type=F32)


def _dot_tn(a, b):
    return lax.dot_general(a.astype(BF16), b.astype(BF16), (((0,), (0,)), ((), ())), preferred_element_type=F32)


def _rows(shape):
    return lax.broadcasted_iota(jnp.int32, shape, 0)


def _shift_down(cur, prev8, s):
    if s == 0:
        return cur
    n = cur.shape[0]
    r = pltpu.roll(cur, s, 0)
    p = pltpu.roll(prev8, s, 0)
    top = jnp.where(_rows(p.shape) < s, p, r[0:SUBLANES])
    if n == SUBLANES:
        return top
    return jnp.concatenate([top, r[SUBLANES:]], axis=0)


def _shift_up(cur, next8, s):
    if s == 0:
        return cur
    n = cur.shape[0]
    r = pltpu.roll(cur, n - s, 0)
    q = pltpu.roll(next8, SUBLANES - s, 0)
    bot = jnp.where(_rows(q.shape) >= SUBLANES - s, q, r[n - SUBLANES:])
    if n == SUBLANES:
        return bot
    return jnp.concatenate([r[:n - SUBLANES], bot], axis=0)


def _scan_fwd(a, b):
    n = a.shape[0]
    rows = _rows(a.shape)
    s = 1
    while s < n:
        a_s = pltpu.roll(a, s, 0)
        b_s = pltpu.roll(b, s, 0)
        m = rows >= s
        b = jnp.where(m, a * b_s + b, b)
        a = jnp.where(m, a * a_s, a)
        s *= 2
    return a, b


def _scan_rev(a, b):
    n = a.shape[0]
    rows = _rows(a.shape)
    s = 1
    while s < n:
        a_s = pltpu.roll(a, n - s, 0)
        b_s = pltpu.roll(b, n - s, 0)
        m = rows < n - s
        b = jnp.where(m, b + a * b_s, b)
        a = jnp.where(m, a * a_s, a)
        s *= 2
    return a, b


def _rms(x):
    r = lax.rsqrt(jnp.mean(x * x, axis=-1, keepdims=True) + EPS)
    return x * r, r


def _rms_bwd(d_n, n, r):
    return r * (d_n - n * jnp.mean(d_n * n, axis=-1, keepdims=True))


def _colsum(x):
    return jnp.sum(x, axis=0, keepdims=True)


def _lru_gates(xc, wr_ref, wi_ref, br, bi, sp_a):
    r = _sigmoid(_dot(xc, wr_ref[...]) + br)
    i = _sigmoid(_dot(xc, wi_ref[...]) + bi)
    la = -LRU_C * r * sp_a
    a = jnp.exp(la)
    mult = jnp.sqrt(_neg_expm1(2.0 * la))
    return r, i, a, mult


def _lru_conv(lx, prev8, cw_ref, cb):
    xc = cb + cw_ref[LRU_CONV_K - 1:LRU_CONV_K, :] * lx
    taps = []
    for k in range(LRU_CONV_K - 1):
        tap = _shift_down(lx, prev8, LRU_CONV_K - 1 - k)
        taps.append(tap)
        xc = xc + cw_ref[k:k + 1, :] * tap
    return xc, taps


def _ws_mask(transposed=False):
    i = lax.broadcasted_iota(jnp.int32, (POS_BLOCK, POS_BLOCK), 0)
    j = lax.broadcasted_iota(jnp.int32, (POS_BLOCK, POS_BLOCK), 1)
    if transposed:
        i, j = j, i
    return (j // CHUNK) <= (i // CHUNK)


def _gmlp_v(gv, vg, vb):
    av, dav = _gelu_and_grad(gv)
    mu = jnp.mean(av, axis=-1, keepdims=True)
    cen = av - mu
    rs = lax.rsqrt(jnp.mean(cen * cen, axis=-1, keepdims=True) + EPS)
    vhat = cen * rs
    return vhat * vg + vb, vhat, rs, dav


def _mix_fwd(x, sh, sc, g_pre, w_in, conv_w, conv_b, wr_bd, wi_bd, b_r, b_i, lru_a, vn_g, vn_b, w_sp, b_sp_t,
             g_lru, g_gmlp, carry=None):
    s_len = x.shape[0]
    tt = min(TT_MIX, s_len)
    nblk = tt // POS_BLOCK

    def body(x_ref, sh_ref, sc_ref, g_ref, w_ref, cw_ref, cb_ref, wr_ref, wi_ref, br_ref, bi_ref, la_ref, vg_ref,
             vb_ref, ws_ref, bst_ref, gl_ref, gg_ref, z_ref, h_ref, y_ref, hl_ref, prev8, hcar):
        i = pl.program_id(0)

        @pl.when(i == 0)
        def _():
            prev8[...] = jnp.zeros_like(prev8)
            hcar[...] = jnp.zeros_like(hcar)

        n_x, _ = _rms(x_ref[...])
        h = (n_x * g_ref[...] * (1.0 + sc_ref[...]) + sh_ref[...]).astype(BF16)
        h_ref[...] = h
        z_ref[...] = jnp.dot(h, w_ref[...], preferred_element_type=F32)

        lx = z_ref[:, 0:LRU_W]
        gate = z_ref[:, LRU_W:2 * LRU_W]
        gu = z_ref[:, 2 * LRU_W:2 * LRU_W + GMLP_W]
        gv = z_ref[:, 2 * LRU_W + GMLP_W:]

        xc, _ = _lru_conv(lx, prev8[...], cw_ref, cb_ref[...])
        prev8[...] = lx[tt - SUBLANES:]
        sp_a = _softplus(-la_ref[...])
        _, ig, a, mult = _lru_gates(xc, wr_ref, wi_ref, br_ref[...], bi_ref[...], sp_a)
        bx = mult * (ig * xc)
        a_cum, b_cum = _scan_fwd(a, bx)
        hl = a_cum * hcar[0:1, :] + b_cum
        hcar[...] = jnp.broadcast_to(hl[tt - 1:tt, :], hcar.shape)
        hl_ref[...] = hl
        y_lru = hl * _gelu(gate)
        n_l, _ = _rms(y_lru)
        y_ref[:, 0:LRU_W] = (n_l * gl_ref[...]).astype(BF16)

        u = _gelu(gu)
        v, _, _, _ = _gmlp_v(gv, vg_ref[...], vb_ref[...])
        mask = _ws_mask()
        sp_parts = []
        for nb in range(nblk):
            row = []
            for g in range(N_GROUPS):
                wsm = jnp.where(mask, ws_ref[g], 0.0)
                vblk = v[nb * POS_BLOCK:(nb + 1) * POS_BLOCK, g * LANES:(g + 1) * LANES]
                row.append(_dot(wsm, vblk) + bst_ref[:, g:g + 1])
            sp_parts.append(jnp.concatenate(row, axis=1))
        sp = jnp.concatenate(sp_parts, axis=0) if nblk > 1 else sp_parts[0]
        n_g, _ = _rms(u * sp)
        y_ref[:, LRU_W:] = (n_g * gg_ref[...]).astype(BF16)

    row = lambda c: pl.BlockSpec((tt, c), lambda i: (i, 0))
    v512 = _const((1, LRU_W))
    vec = _const((1, D_MODEL))
    return _call(
        body, "mix_fwd", (s_len // tt,),
        in_specs=[row(D_MODEL), vec, vec, vec, _whole(),
                  _const((LRU_CONV_K, LRU_W)), v512, _whole(), _whole(), v512, v512, v512, v512, v512,
                  _whole(), _whole(), v512, v512],
        out_specs=[row(IN_COLS), row(D_MODEL), row(LRU_W + GMLP_W), row(LRU_W)],
        out_shape=[_sds((s_len, IN_COLS), F32), _sds((s_len, D_MODEL), BF16),
                   _sds((s_len, LRU_W + GMLP_W), BF16), _sds((s_len, LRU_W), F32)],
        scratch=[pltpu.VMEM((SUBLANES, LRU_W), F32), pltpu.VMEM((SUBLANES, LRU_W), F32)],
        args=(x, sh, sc, g_pre, w_in, conv_w, conv_b, wr_bd, wi_bd, b_r, b_i, lru_a, vn_g, vn_b, w_sp, b_sp_t,
              g_lru, g_gmlp), carry=carry)


def _out_up_fwd(ycat, x, w_out, g_post, gt_m, g_pre, sc_f, sh_f, w_up, carry=None):
    s_len = x.shape[0]
    tt = min(TT_MIX, s_len)

    def body(yc_ref, x_ref, wo_ref, gp_ref, gt_ref, g2_ref, sc_ref, sh_ref, wu_ref, y_ref, x1_ref, h2_ref, up_ref):
        y = jnp.dot(yc_ref[...], wo_ref[...], preferred_element_type=F32)
        y_ref[...] = y
        n_y, _ = _rms(y)
        x1 = x_ref[...] + gt_ref[...] * (n_y * gp_ref[...])
        x1_ref[...] = x1
        n1, _ = _rms(x1)
        h2 = (n1 * g2_ref[...] * (1.0 + sc_ref[...]) + sh_ref[...]).astype(BF16)
        h2_ref[...] = h2
        up_ref[0] = jnp.dot(h2, wu_ref[:, 0:D_FF], preferred_element_type=F32)
        up_ref[1] = jnp.dot(h2, wu_ref[:, D_FF:], preferred_element_type=F32)

    row = lambda c: pl.BlockSpec((tt, c), lambda i: (i, 0))
    vec = _const((1, D_MODEL))
    return _call(
        body, "out_up_fwd", (s_len // tt,),
        in_specs=[row(D_MODEL), row(D_MODEL), _whole(), vec, vec, vec, vec, vec, _whole()],
        out_specs=[row(D_MODEL), row(D_MODEL), row(D_MODEL), pl.BlockSpec((2, tt, D_FF), lambda i: (0, i, 0))],
        out_shape=[_sds((s_len, D_MODEL), F32), _sds((s_len, D_MODEL), F32), _sds((s_len, D_MODEL), BF16),
                   _sds((2, s_len, D_FF), F32)],
        scratch=[], args=(ycat, x, w_out, g_post, gt_m, g_pre, sc_f, sh_f, w_up), carry=carry)


def _ffn_conv(up_pre, prev8, cw_ref, cb):
    up = cb + cw_ref[FFN_CONV_K - 1:FFN_CONV_K, :] * up_pre
    taps = []
    for k in range(FFN_CONV_K - 1):
        tap = _shift_down(up_pre, prev8, FFN_CONV_K - 1 - k)
        taps.append(tap)
        up = up + cw_ref[k:k + 1, :] * tap
    return up, taps


def _ffn_fwd(up_pre, ffn_cw, ffn_cb, w_down, x1, gt_f, g_post, target):
    s_len = x1.shape[0]
    tt = min(TT_BIG, s_len)
    cw = FF_CW
    nc = D_FF // cw

    def body(up_ref, cwg_ref, cwv_ref, cbg_ref, cbv_ref, wd_ref, x1_ref, gt_ref, gp_ref, tg_ref,
             act_ref, y2_ref, dout_ref, loss_ref, upc_ref, prev, acc):
        i = pl.program_id(0)
        c = pl.program_id(1)

        @pl.when(i == 0)
        def _():
            prev[c] = jnp.zeros((2, SUBLANES, cw), F32)

        @pl.when((i == 0) & (c == 0))
        def _():
            loss_ref[...] = jnp.zeros_like(loss_ref)

        ug, _ = _ffn_conv(up_ref[0], prev[c, 0], cwg_ref, cbg_ref[...])
        uv, _ = _ffn_conv(up_ref[1], prev[c, 1], cwv_ref, cbv_ref[...])
        prev[c, 0] = up_ref[0, tt - SUBLANES:, :]
        prev[c, 1] = up_ref[1, tt - SUBLANES:, :]
        upc_ref[0] = ug
        upc_ref[1] = uv
        act = (_gelu(ug) * uv).astype(BF16)
        act_ref[...] = act
        part = jnp.dot(act, wd_ref[...], preferred_element_type=F32)

        @pl.when(c == 0)
        def _():
            acc[...] = part

        @pl.when(c > 0)
        def _():
            acc[...] += part

        @pl.when(c == nc - 1)
        def _():
            y2 = acc[...]
            y2_ref[...] = y2
            n2, _ = _rms(y2)
            out = x1_ref[...] + gt_ref[...] * (n2 * gp_ref[...])
            err = out - tg_ref[...]
            dout_ref[...] = err * (1.0 / D_MODEL)
            loss_ref[...] += jnp.broadcast_to(0.5 * jnp.sum(err * err, keepdims=True) * (1.0 / D_MODEL), loss_ref.shape)

    row = pl.BlockSpec((tt, D_MODEL), lambda i, c: (i, 0))
    vec = _const((1, D_MODEL))
    ffn_cb2 = ffn_cb.reshape(1, 2 * D_FF)
    return pl.pallas_call(
        body, name="ffn_fwd", grid=(s_len // tt, nc),
        in_specs=[pl.BlockSpec((2, tt, cw), lambda i, c: (0, i, c)),
                  pl.BlockSpec((FFN_CONV_K, cw), lambda i, c: (0, c)),
                  pl.BlockSpec((FFN_CONV_K, cw), lambda i, c: (0, c + nc)),
                  pl.BlockSpec((1, cw), lambda i, c: (0, c)),
                  pl.BlockSpec((1, cw), lambda i, c: (0, c + nc)),
                  pl.BlockSpec((cw, D_MODEL), lambda i, c: (c, 0)),
                  row, vec, vec, row],
        out_specs=[pl.BlockSpec((tt, cw), lambda i, c: (i, c)), row, row, _const((SUBLANES, LANES)),
                   pl.BlockSpec((2, tt, cw), lambda i, c: (0, i, c))],
        out_shape=[_sds((s_len, D_FF), BF16), _sds((s_len, D_MODEL), F32), _sds((s_len, D_MODEL), F32),
                   _sds((SUBLANES, LANES), F32), _sds((2, s_len, D_FF), F32)],
        scratch_shapes=[pltpu.VMEM((nc, 2, SUBLANES, cw), F32), pltpu.VMEM((tt, D_MODEL), F32)],
        compiler_params=_cparams(("arbitrary", "arbitrary")),
    )(up_pre, ffn_cw, ffn_cw, ffn_cb2, ffn_cb2, w_down, x1, gt_f, g_post, target)


def _ffn_bwd(dout, y2, up_pre, up, ffn_cw, w_down, gt_f, g_post):
    s_len = dout.shape[0]
    tt = min(TT_BIG, s_len)
    nt = s_len // tt
    cw = FF_CW
    nc = D_FF // cw

    def body(do_ref, y2_ref, up_ref, upc_ref, cwg_ref, cwv_ref, wd_ref, gt_ref, gp_ref,
             dy2_ref, dup_ref, vs_ref, cs_ref, nxt, dy2s, cs_acc):
        i = pl.program_id(0)
        c = pl.program_id(1)

        @pl.when(i == 0)
        def _():
            nxt[c] = jnp.zeros((2, SUBLANES, cw), F32)
            cs_acc[c] = jnp.zeros((2, SUBLANES, cw), F32)

        @pl.when((i == 0) & (c == 0))
        def _():
            vs_ref[...] = jnp.zeros_like(vs_ref)

        @pl.when(c == 0)
        def _():
            n2, r2 = _rms(y2_ref[...])
            do = do_ref[...]
            vs_ref[0:1, :] += _colsum(do * n2 * gp_ref[...])
            vs_ref[1:2, :] += _colsum(do * gt_ref[...] * n2)
            dy2 = _rms_bwd(do * gt_ref[...] * gp_ref[...], n2, r2).astype(BF16)
            dy2s[...] = dy2
            dy2_ref[...] = dy2

        d_act = _dot_nt(dy2s[...], wd_ref[...])
        uv = upc_ref[1]
        gl, dgl = _gelu_and_grad(upc_ref[0])
        d_ug = d_act * uv * dgl
        d_uv = d_act * gl
        for half, (d_u, cw_ref) in enumerate(((d_ug, cwg_ref), (d_uv, cwv_ref))):
            nx = nxt[c, half]
            x_in = up_ref[half]
            d_pre = cw_ref[FFN_CONV_K - 1:FFN_CONV_K, :] * d_u
            sums = [None] * (FFN_CONV_K + 1)
            sums[FFN_CONV_K - 1] = _colsum(d_u * x_in)
            for k in range(FFN_CONV_K - 1):
                ahead = _shift_up(d_u, nx, FFN_CONV_K - 1 - k)
                d_pre = d_pre + cw_ref[k:k + 1, :] * ahead
                sums[k] = _colsum(ahead * x_in)
            sums[FFN_CONV_K] = _colsum(d_u)
            pad = jnp.zeros((SUBLANES - FFN_CONV_K - 1, cw), F32)
            cs_acc[c, half] += jnp.concatenate(sums + [pad], axis=0)
            nxt[c, half] = d_u[0:SUBLANES]
            dup_ref[half] = d_pre.astype(BF16)

        for cc in range(nc):
            @pl.when((i == nt - 1) & (c == cc))
            def _():
                cs_ref[:, cc * cw:(cc + 1) * cw] = cs_acc[cc, 0]
                cs_ref[:, D_FF + cc * cw:D_FF + (cc + 1) * cw] = cs_acc[cc, 1]

    rev = lambda i, c: (nt - 1 - i, 0)
    row = pl.BlockSpec((tt, D_MODEL), rev)
    vec = _const((1, D_MODEL))
    blk = pl.BlockSpec((2, tt, cw), lambda i, c: (0, nt - 1 - i, c))
    return pl.pallas_call(
        body, name="ffn_bwd", grid=(nt, nc),
        in_specs=[row, row, blk, blk,
                  pl.BlockSpec((FFN_CONV_K, cw), lambda i, c: (0, c)),
                  pl.BlockSpec((FFN_CONV_K, cw), lambda i, c: (0, c + nc)),
                  pl.BlockSpec((cw, D_MODEL), lambda i, c: (c, 0)),
                  vec, vec],
        out_specs=[row, blk, _const((SUBLANES, D_MODEL)), _const((SUBLANES, 2 * D_FF))],
        out_shape=[_sds((s_len, D_MODEL), BF16), _sds((2, s_len, D_FF), BF16), _sds((SUBLANES, D_MODEL), F32),
                   _sds((SUBLANES, 2 * D_FF), F32)],
        scratch_shapes=[pltpu.VMEM((nc, 2, SUBLANES, cw), F32), pltpu.VMEM((tt, D_MODEL), BF16),
                        pltpu.VMEM((nc, 2, SUBLANES, cw), F32)],
        compiler_params=_cparams(("arbitrary", "arbitrary")),
    )(dout, y2, up_pre, up, ffn_cw, ffn_cw, w_down, gt_f, g_post)


def _up_bwd(d_up, w_up, x1, dout, y, w_out, g_pre, sc_f, g_post, gt_m, carry=None):
    s_len = x1.shape[0]
    tt = min(TT_BIG, s_len)

    def body(du_ref, wu_ref, x1_ref, do_ref, y_ref, wo_ref, g2_ref, sc_ref, gp_ref, gt_ref,
             dx1_ref, dy_ref, dyc_ref, vs_ref):
        @pl.when(pl.program_id(0) == 0)
        def _():
            vs_ref[...] = jnp.zeros_like(vs_ref)

        d_h2 = _dot_nt(du_ref[0], wu_ref[:, 0:D_FF]) + _dot_nt(du_ref[1], wu_ref[:, D_FF:])
        n1, r1 = _rms(x1_ref[...])
        ng = n1 * g2_ref[...]
        vs_ref[0:1, :] += _colsum(d_h2)
        vs_ref[1:2, :] += _colsum(d_h2 * ng)
        d_ng = d_h2 * (1.0 + sc_ref[...])
        vs_ref[2:3, :] += _colsum(d_ng * n1)
        d_x1 = do_ref[...] + _rms_bwd(d_ng * g2_ref[...], n1, r1)
        dx1_ref[...] = d_x1
        n_y, r_y = _rms(y_ref[...])
        vs_ref[3:4, :] += _colsum(d_x1 * n_y * gp_ref[...])
        d_on = d_x1 * gt_ref[...]
        vs_ref[4:5, :] += _colsum(d_on * n_y)
        d_y = _rms_bwd(d_on * gp_ref[...], n_y, r_y).astype(BF16)
        dy_ref[...] = d_y
        dyc_ref[...] = _dot_nt(d_y, wo_ref[...])

    row = lambda c: pl.BlockSpec((tt, c), lambda i: (i, 0))
    vec = _const((1, D_MODEL))
    return _call(
        body, "up_bwd", (s_len // tt,),
        in_specs=[pl.BlockSpec((2, tt, D_FF), lambda i: (0, i, 0)), _whole(), row(D_MODEL), row(D_MODEL), row(D_MODEL),
                  _whole(), vec, vec, vec, vec],
        out_specs=[row(D_MODEL), row(D_MODEL), row(LRU_W + GMLP_W), _const((SUBLANES, D_MODEL))],
        out_shape=[_sds((s_len, D_MODEL), F32), _sds((s_len, D_MODEL), BF16), _sds((s_len, LRU_W + GMLP_W), F32),
                   _sds((SUBLANES, D_MODEL), F32)],
        scratch=[], args=(d_up, w_up, x1, dout, y, w_out, g_pre, sc_f, g_post, gt_m), carry=carry)


def _head_pair_block(hd):
    return (slice((hd // 2) * HEAD_DIM, (hd // 2 + 1) * HEAD_DIM), slice((hd % 2) * HEAD_DIM, (hd % 2 + 1) * HEAD_DIM))


def _mix_bwd(d_ycat, z, hl, conv_w, conv_b, wr_bd, wi_bd, b_r, b_i, lru_a, vn_g, vn_b, w_sp, w_sp_t, b_sp_t,
             g_lru, g_gmlp, carry=None):
    s_len = z.shape[0]
    tt = min(TT_MIX, s_len)
    nt = s_len // tt
    nblk = tt // POS_BLOCK
    hb = tt // SUBLANES

    def body(dyc_ref, z_ref, zh_ref, hl_ref, hh_ref, cw_ref, cb_ref, wr_ref, wi_ref, br_ref, bi_ref, la_ref,
             vg_ref, vb_ref, ws_ref, wst_ref, bst_ref, gl_ref, gg_ref,
             dz_ref, vs_ref, dcw_ref, dwrb_ref, dwib_ref, dws_ref, dbs_ref, nxt_dxc, nxt_a, nxt_lam, dwr_ref, dwi_ref):
        i = pl.program_id(0)
        first_tile = i == nt - 1

        @pl.when(i == 0)
        def _():
            for ref in (vs_ref, dcw_ref, dwr_ref, dwi_ref, dws_ref, dbs_ref, nxt_dxc, nxt_a, nxt_lam):
                ref[...] = jnp.zeros_like(ref)

        lx = z_ref[:, 0:LRU_W]
        gate = z_ref[:, LRU_W:2 * LRU_W]
        gu = z_ref[:, 2 * LRU_W:2 * LRU_W + GMLP_W]
        gv = z_ref[:, 2 * LRU_W + GMLP_W:]
        prev8 = jnp.where(first_tile, 0.0, zh_ref[...])
        hprev8 = jnp.where(first_tile, 0.0, hh_ref[...])

        xc, taps = _lru_conv(lx, prev8, cw_ref, cb_ref[...])
        a_par = la_ref[...]
        sp_a = _softplus(-a_par)
        r, ig, a, mult = _lru_gates(xc, wr_ref, wi_ref, br_ref[...], bi_ref[...], sp_a)
        hl = hl_ref[...]
        h_prev = _shift_down(hl, hprev8, 1)
        ggate, dggate = _gelu_and_grad(gate)
        y_lru = hl * ggate
        n_l, r_l = _rms(y_lru)
        d_nl = dyc_ref[:, 0:LRU_W]
        vs_ref[6:7, :] += _colsum(d_nl * n_l)
        d_yl = _rms_bwd(d_nl * gl_ref[...], n_l, r_l)
        d_hl = d_yl * ggate
        d_gate = d_yl * hl * dggate
        a_up = _shift_up(a, nxt_a[...], 1)
        a_cum, b_cum = _scan_rev(a_up, d_hl)
        lam = b_cum + a_cum * nxt_lam[0:1, :]
        nxt_a[...] = jnp.broadcast_to(a[0:1, :], nxt_a.shape)
        nxt_lam[...] = jnp.broadcast_to(lam[0:1, :], nxt_lam.shape)
        ixc = ig * xc
        d_la = lam * h_prev * a - lam * ixc * (a * a) / mult
        d_i = lam * mult * xc
        d_xc = lam * mult * ig
        vs_ref[3:4, :] += _colsum(d_la * r) * (LRU_C * _sigmoid(-a_par))
        d_pr = d_la * (-LRU_C * sp_a) * r * (1.0 - r)
        d_pi = d_i * ig * (1.0 - ig)
        vs_ref[1:2, :] += _colsum(d_pr)
        vs_ref[2:3, :] += _colsum(d_pi)
        dwr_ref[...] += _dot_tn(xc, d_pr)
        dwi_ref[...] += _dot_tn(xc, d_pi)
        d_xc = d_xc + _dot_nt(d_pr, wr_ref[...]) + _dot_nt(d_pi, wi_ref[...])
        vs_ref[0:1, :] += _colsum(d_xc)
        nx = nxt_dxc[...]
        d_lx = cw_ref[LRU_CONV_K - 1:LRU_CONV_K, :] * d_xc
        dcw_ref[LRU_CONV_K - 1:LRU_CONV_K, :] += _colsum(d_xc * lx)
        for k in range(LRU_CONV_K - 1):
            d_lx = d_lx + cw_ref[k:k + 1, :] * _shift_up(d_xc, nx, LRU_CONV_K - 1 - k)
            dcw_ref[k:k + 1, :] += _colsum(d_xc * taps[k])
        nxt_dxc[...] = d_xc[0:SUBLANES]
        dz_ref[:, 0:LRU_W] = d_lx.astype(BF16)
        dz_ref[:, LRU_W:2 * LRU_W] = d_gate.astype(BF16)

        u, du = _gelu_and_grad(gu)
        v, vhat, rs, dav = _gmlp_v(gv, vg_ref[...], vb_ref[...])
        mask = _ws_mask()
        sp_parts = []
        for nb in range(nblk):
            rowp = []
            for g in range(N_GROUPS):
                wsm = jnp.where(mask, ws_ref[g], 0.0)
                vblk = v[nb * POS_BLOCK:(nb + 1) * POS_BLOCK, g * LANES:(g + 1) * LANES]
                rowp.append(_dot(wsm, vblk) + bst_ref[:, g:g + 1])
            sp_parts.append(jnp.concatenate(rowp, axis=1))
        sp = jnp.concatenate(sp_parts, axis=0) if nblk > 1 else sp_parts[0]
        y_g = u * sp
        n_g, r_g = _rms(y_g)
        d_ng = dyc_ref[:, LRU_W:]
        vs_ref[7:8, :] += _colsum(d_ng * n_g)
        d_yg = _rms_bwd(d_ng * gg_ref[...], n_g, r_g)
        d_gu = d_yg * sp * du
        d_sp = d_yg * u
        mask_t = _ws_mask(transposed=True)
        ones8 = jnp.ones((SUBLANES, LANES), F32)
        dv_parts = []
        for nb in range(nblk):
            rowp = []
            for g in range(N_GROUPS):
                rs_, cs_ = slice(nb * POS_BLOCK, (nb + 1) * POS_BLOCK), slice(g * LANES, (g + 1) * LANES)
                dsp_blk = d_sp[rs_, cs_]
                dbs_ref[g:g + 1, :] += lax.dot_general(
                    ones8, dsp_blk, (((1,), (1,)), ((), ())), preferred_element_type=F32,
                    precision=lax.Precision.HIGHEST)[0:1, :]
                dws_ref[g] += _dot_nt(dsp_blk, v[rs_, cs_])
                wsm_t = jnp.where(mask_t, wst_ref[g], 0.0)
                rowp.append(_dot(wsm_t, dsp_blk))
            dv_parts.append(jnp.concatenate(rowp, axis=1))
        d_v = jnp.concatenate(dv_parts, axis=0) if nblk > 1 else dv_parts[0]
        vs_ref[4:5, :] += _colsum(d_v * vhat)
        vs_ref[5:6, :] += _colsum(d_v)
        d_vh = d_v * vg_ref[...]
        d_av = rs * (d_vh - jnp.mean(d_vh, axis=-1, keepdims=True)
                     - vhat * jnp.mean(d_vh * vhat, axis=-1, keepdims=True))
        dz_ref[:, 2 * LRU_W:2 * LRU_W + GMLP_W] = d_gu.astype(BF16)
        dz_ref[:, 2 * LRU_W + GMLP_W:] = (d_av * dav).astype(BF16)

        @pl.when(i == nt - 1)
        def _():
            for hd in range(N_HEADS):
                blk = slice(hd * HEAD_DIM, (hd + 1) * HEAD_DIM)
                dwrb_ref[_head_pair_block(hd)] = dwr_ref[blk, blk]
                dwib_ref[_head_pair_block(hd)] = dwi_ref[blk, blk]
            for g in range(N_GROUPS):
                dws_ref[g] = jnp.where(mask, dws_ref[g], 0.0)

    rev = lambda c: pl.BlockSpec((tt, c), lambda i: (nt - 1 - i, 0))
    halo = pl.BlockSpec((SUBLANES, LRU_W), lambda i: (jnp.maximum((nt - 1 - i) * hb - 1, 0), 0))
    v512 = _const((1, LRU_W))
    return _call(
        body, "mix_bwd", (nt,),
        in_specs=[rev(LRU_W + GMLP_W), rev(IN_COLS), halo, rev(LRU_W), halo,
                  _const((LRU_CONV_K, LRU_W)), v512, _whole(), _whole(), v512, v512, v512, v512, v512,
                  _whole(), _whole(), _whole(), v512, v512],
        out_specs=[rev(IN_COLS), _const((SUBLANES, LRU_W)), _const((SUBLANES, LRU_W)),
                   _const((LRU_W // 2, 2 * HEAD_DIM)), _const((LRU_W // 2, 2 * HEAD_DIM)),
                   _const((N_GROUPS, POS_BLOCK, POS_BLOCK)), _const((SUBLANES, POS_BLOCK))],
        out_shape=[_sds((s_len, IN_COLS), BF16), _sds((SUBLANES, LRU_W), F32), _sds((SUBLANES, LRU_W), F32),
                   _sds((LRU_W // 2, 2 * HEAD_DIM), F32), _sds((LRU_W // 2, 2 * HEAD_DIM), F32),
                   _sds((N_GROUPS, POS_BLOCK, POS_BLOCK), F32), _sds((SUBLANES, POS_BLOCK), F32)],
        scratch=[pltpu.VMEM((SUBLANES, LRU_W), F32), pltpu.VMEM((SUBLANES, LRU_W), F32),
                 pltpu.VMEM((SUBLANES, LRU_W), F32), pltpu.VMEM((LRU_W, LRU_W), F32), pltpu.VMEM((LRU_W, LRU_W), F32)],
        args=(d_ycat, z, z, hl, hl, conv_w, conv_b, wr_bd, wi_bd, b_r, b_i, lru_a, vn_g, vn_b, w_sp, w_sp_t, b_sp_t,
              g_lru, g_gmlp), carry=carry)


def _in_bwd(d_z, w_in, x, d_x1, g, sc, carry=None):
    s_len = x.shape[0]
    tt = min(TT_BIG, s_len)

    def body(dz_ref, w_ref, x_ref, dx1_ref, g_ref, sc_ref, gx_ref, vs_ref):
        @pl.when(pl.program_id(0) == 0)
        def _():
            vs_ref[...] = jnp.zeros_like(vs_ref)

        d_h = _dot_nt(dz_ref[...], w_ref[...])
        n, r = _rms(x_ref[...])
        vs_ref[0:1, :] += _colsum(d_h)
        vs_ref[1:2, :] += _colsum(d_h * n * g_ref[...])
        d_ng = d_h * (1.0 + sc_ref[...])
        vs_ref[2:3, :] += _colsum(d_ng * n)
        gx_ref[...] = dx1_ref[...] + _rms_bwd(d_ng * g_ref[...], n, r)

    row = lambda c: pl.BlockSpec((tt, c), lambda i: (i, 0))
    vec = _const((1, D_MODEL))
    return _call(
        body, "in_bwd", (s_len // tt,),
        in_specs=[row(IN_COLS), _whole(), row(D_MODEL), row(D_MODEL), vec, vec],
        out_specs=[row(D_MODEL), _const((SUBLANES, D_MODEL))],
        out_shape=[_sds((s_len, D_MODEL), F32), _sds((SUBLANES, D_MODEL), F32)],
        scratch=[], args=(d_z, w_in, x, d_x1, g, sc), carry=carry)


def _wgrad(a, b, tn, name, carry=None):
    s_len, k_dim = a.shape
    halves = b.ndim == 3
    n_dim = b.shape[-1] * (2 if halves else 1)
    ts = min(TT_WG, s_len)
    nj = n_dim // tn
    nt = s_len // ts

    def body(a_ref, b_ref, o_ref, ob_ref):
        t = pl.program_id(1)
        part = _dot_tn(a_ref[...], b_ref[0] if halves else b_ref[...])

        @pl.when(t == 0)
        def _():
            o_ref[...] = part

        @pl.when(t > 0)
        def _():
            o_ref[...] += part

        @pl.when(t == nt - 1)
        def _():
            ob_ref[...] = o_ref[...].astype(BF16)

    if halves:
        per_half = nj // 2
        b_spec = pl.BlockSpec((1, ts, tn), lambda j, t: (j // per_half, t, j % per_half))
    else:
        b_spec = pl.BlockSpec((ts, tn), lambda j, t: (t, j))
    o_spec = pl.BlockSpec((k_dim, tn), lambda j, t: (0, j))
    return _call(
        body, name, (nj, nt),
        in_specs=[pl.BlockSpec((ts, k_dim), lambda j, t: (t, 0)), b_spec],
        out_specs=[o_spec, o_spec],
        out_shape=[_sds((k_dim, n_dim), F32), _sds((k_dim, n_dim), BF16)],
        scratch=[], args=(a, b), carry=carry)


def _adam_math(w, g, m, v):
    m = ADAM_B1 * m + (1.0 - ADAM_B1) * g
    v = ADAM_B2 * v + (1.0 - ADAM_B2) * (g * g)
    m_hat = m / (1.0 - ADAM_B1 ** ADAM_STEP)
    v_hat = v / (1.0 - ADAM_B2 ** ADAM_STEP)
    delta = -ADAM_LR * (m_hat / (jnp.sqrt(v_hat) + ADAM_EPS) + ADAM_WD * w)
    return delta, m, v


def _row_tile(rows, cols, n_f32_arrays):
    budget = VMEM_LIMIT // 2
    tr = rows
    while tr % 2 == 0 and tr // 2 >= SUBLANES and (tr // 2) % SUBLANES == 0 and tr * cols * 4 * n_f32_arrays * 2 > budget:
        tr //= 2
    return tr


def _adamw_sum(w, g_full, recv, m, v, col_sharded, name):
    _, rows, cols = w.shape
    n_recv = len(recv)
    tr = _row_tile(rows, cols, 10)
    nb = rows // tr

    def body(me_ref, w_ref, g_ref, *rest):
        r_refs = rest[:n_recv]
        m_ref, v_ref, go_ref, d_ref, mo_ref, vo_ref = rest[n_recv:]
        g = g_ref[...]
        for r_ref in r_refs:
            for k in range(r_ref.shape[0]):
                g = g + r_ref[k].astype(F32)
        go_ref[0] = g
        d_ref[0], mo_ref[0], vo_ref[0] = _adam_math(w_ref[0], g, m_ref[0], v_ref[0])

    if col_sharded:
        own = pl.BlockSpec((tr, cols), lambda i, me: (i, me[0]))
    else:
        own = pl.BlockSpec((tr, cols), lambda i, me: (me[0] * nb + i, 0))
    blk = pl.BlockSpec((1, tr, cols), lambda i, me: (0, i, 0))
    return pl.pallas_call(
        body, name=name,
        grid_spec=pltpu.PrefetchScalarGridSpec(
            num_scalar_prefetch=1, grid=(nb,),
            in_specs=[blk, own] + [pl.BlockSpec((r.shape[0], tr, cols), lambda i, me: (0, i, 0)) for r in recv]
            + [blk, blk],
            out_specs=[blk] * 4),
        out_shape=[_sds((1, rows, cols), F32)] * 4,
        compiler_params=_cparams(("arbitrary",)),
    )(jnp.reshape(_dev_index(_my_pos()), (1,)).astype(jnp.int32), w, g_full, *recv, m, v)


def _row_of_each(ref, row):
    cols = ref.shape[1]
    rows = _rows((N_DEV, cols))
    out = jnp.zeros((N_DEV, cols), F32)
    for d in range(N_DEV):
        picked = ref[d * SUBLANES + row:d * SUBLANES + row + 1, :]
        out = jnp.where(rows == d, jnp.broadcast_to(picked, (N_DEV, cols)), out)
    return out


def _my_columns(full, width, me):
    out = jnp.zeros(full.shape[:-1] + (width,), F32)
    for d in range(N_DEV):
        out = out + jnp.where(me == d, full[:, d * width:(d + 1) * width], 0.0)
    return out


def _adamw_wada(c_all, vs_in_all, vs_up_all, vs_ffn_all, w, m, v):
    _, rows, cols = w.shape

    def body(c_ref, vi_ref, vu_ref, vf_ref, w_ref, m_ref, v_ref, go_ref, d_ref, mo_ref, vo_ref):
        me = _dev_index(_my_pos())
        cv = _row_of_each(c_ref, 0)
        ca = cv * _sigmoid(cv)
        dmod = jnp.concatenate([_row_of_each(vi_ref, 0), _row_of_each(vi_ref, 1), _row_of_each(vu_ref, 3),
                                _row_of_each(vu_ref, 0), _row_of_each(vu_ref, 1), _row_of_each(vf_ref, 0)], axis=1)
        dm = _my_columns(dmod, cols, me)
        g = lax.dot_general(ca, dm, (((0,), (0,)), ((), ())), preferred_element_type=F32,
                            precision=lax.Precision.HIGHEST)
        go_ref[0] = g
        d_ref[0], mo_ref[0], vo_ref[0] = _adam_math(w_ref[0], g, m_ref[0], v_ref[0])

    return pl.pallas_call(
        body, name="adamw_w_ada", out_shape=[_sds((1, rows, cols), F32)] * 4,
        in_specs=[_whole()] * 7, out_specs=[_whole()] * 4,
        compiler_params=_cparams(),
    )(c_all, vs_in_all, vs_up_all, vs_ffn_all, w, m, v)


def _adamw_small(gathered, reduced, params, conv_params):
    names = list(params) + list(conv_params)
    allp = {**params, **conv_params}
    n_g = len(gathered) + len(reduced)

    def body(*refs):
        g_refs = refs[:n_g]
        p_refs = refs[n_g:n_g + 3 * len(names)]
        o_refs = refs[n_g + 3 * len(names):]
        me = _dev_index(_my_pos())

        def total(ref):
            s = ref[0:SUBLANES, :]
            for d in range(1, N_DEV):
                s = s + ref[d * SUBLANES:(d + 1) * SUBLANES, :]
            return s

        vs_in, vs_up, vs_ffn, loss = [total(r) for r in g_refs[:4]]
        cs, vs_mix, dcw, dwr, dwi, dws, dbs = [r[...] for r in g_refs[4:]]
        o_refs[-1][...] = loss[0:1, 0:1]
        mine = lambda full, width: _my_columns(full, width, me)

        all_ = (slice(None), slice(None))
        heads = lambda row: [((0, slice(h, h + 1), slice(None)), row[:, h * HEAD_DIM:(h + 1) * HEAD_DIM])
                             for h in range(N_HEADS)]
        blocks = lambda pairs: [((0, h), pairs[_head_pair_block(h)]) for h in range(N_HEADS)]
        pieces = {
            "b_ada": [((slice(None), slice(k * D_MODEL, (k + 1) * D_MODEL)), row) for k, row in enumerate(
                (vs_in[0:1], vs_in[1:2], vs_up[3:4], vs_up[0:1], vs_up[1:2], vs_ffn[0:1]))],
            "g_mix_pre": [(all_, vs_in[2:3])], "g_mix_post": [(all_, vs_up[4:5])],
            "g_ffn_pre": [(all_, vs_up[2:3])], "g_ffn_post": [(all_, vs_ffn[1:2])],
            "conv_b": [(all_, vs_mix[0:1])], "b_rgate": heads(vs_mix[1:2]), "b_igate": heads(vs_mix[2:3]),
            "lru_a": [(all_, vs_mix[3:4])], "v_norm_g": [(all_, vs_mix[4:5])], "v_norm_b": [(all_, vs_mix[5:6])],
            "g_lru_out": [(all_, vs_mix[6:7])], "g_gmlp_out": [(all_, vs_mix[7:8])],
            "w_rgate": blocks(dwr), "w_igate": blocks(dwi),
            "w_spatial": [((0, g), dws[g * POS_BLOCK:(g + 1) * POS_BLOCK, :]) for g in range(N_GROUPS)],
            "b_spatial": [((0,), dbs[0:N_GROUPS])],
            "ffn_conv_b": [(all_, cs[FFN_CONV_K:FFN_CONV_K + 1])],
            "conv_w": [((0,), mine(dcw[0:LRU_CONV_K], LRU_W // N_DEV))],
            "ffn_conv_w": [((0,), mine(cs[0:FFN_CONV_K], 2 * D_FF // N_DEV))],
        }
        for n_i, name in enumerate(names):
            w_ref, m_ref, v_ref = p_refs[3 * n_i:3 * n_i + 3]
            go_ref, d_ref, mo_ref, vo_ref = o_refs[4 * n_i:4 * n_i + 4]
            for idx, g in pieces[name]:
                go_ref[idx] = g
                d_ref[idx], mo_ref[idx], vo_ref[idx] = _adam_math(w_ref[idx], g, m_ref[idx], v_ref[idx])

    flat_params = [a for n in names for a in allp[n]]
    out_shape = [_sds(allp[n][0].shape, F32) for n in names for _ in range(4)] + [_sds((1, 1), F32)]
    outs = pl.pallas_call(
        body, name="adamw_small", out_shape=out_shape,
        in_specs=[_whole()] * (n_g + len(flat_params)), out_specs=[_whole()] * len(out_shape),
        compiler_params=_cparams(),
    )(*gathered, *reduced, *flat_params)
    return {n: outs[4 * i:4 * i + 4] for i, n in enumerate(names)}, outs[-1]


def _mod_part(c_all, w_ada, b_ada):
    cols = w_ada.shape[1]

    def body(c_ref, w_ref, b_ref, o_ref):
        cv = _row_of_each(c_ref, 0)
        ca = cv * _sigmoid(cv)
        b_cols = _my_columns(b_ref[...], cols, _dev_index(_my_pos()))
        o_ref[...] = jnp.dot(ca, w_ref[...], preferred_element_type=F32, precision=lax.Precision.HIGHEST) + b_cols

    return pl.pallas_call(
        body, name="mod_part", out_shape=_sds((N_DEV, cols), F32),
        in_specs=[_whole()] * 3, out_specs=_whole(), compiler_params=_cparams(),
    )(c_all, w_ada, b_ada)


def _my_pos():
    return lax.axis_index("x"), lax.axis_index("y"), lax.axis_index("c")


def _flip(pos, k):
    x, y, c = pos
    return (1 - x if k & 4 else x, 1 - y if k & 2 else y, 1 - c if k & 1 else c)


def _dev_index(pos):
    x, y, c = pos
    return 4 * x + 2 * y + c


def _gather_multi(arrays, name, carry=None):
    n = len(arrays)

    def body(*refs):
        ins, outs = refs[:n], refs[n:2 * n]
        send_sems, recv_sems = refs[2 * n:]
        me = _my_pos()

        def slot(a, pos):
            rows = ins[a].shape[0]
            return outs[a].at[pl.ds(pl.multiple_of(_dev_index(pos) * rows, SUBLANES), rows), :]

        def copy(a, k):
            return pltpu.make_async_remote_copy(
                src_ref=ins[a], dst_ref=slot(a, me), send_sem=send_sems.at[a, k - 1], recv_sem=recv_sems.at[a, k - 1],
                device_id=_flip(me, k), device_id_type=MESH)

        sends = [copy(a, k) for a in range(n) for k in range(1, N_DEV)]
        for cp in sends:
            cp.start()
        for a in range(n):
            rows = ins[a].shape[0]
            outs[a][pl.ds(pl.multiple_of(_dev_index(me) * rows, SUBLANES), rows), :] = ins[a][...]
        for a in range(n):
            for k in range(1, N_DEV):
                pltpu.make_async_remote_copy(
                    src_ref=ins[a], dst_ref=slot(a, _flip(me, k)), send_sem=send_sems.at[a, k - 1],
                    recv_sem=recv_sems.at[a, k - 1], device_id=_flip(me, k), device_id_type=MESH).wait_recv()
        for cp in sends:
            cp.wait_send()

    return _call(
        body, name, (1,), in_specs=[_whole()] * n, out_specs=[_whole()] * n,
        out_shape=[_sds((N_DEV * a.shape[0], a.shape[1]), F32) for a in arrays],
        scratch=[pltpu.SemaphoreType.DMA((n, N_DEV - 1)), pltpu.SemaphoreType.DMA((n, N_DEV - 1))],
        args=tuple(arrays), carry=carry)


def _reduce_small(gath, red):
    n_g, n_r = len(gath), len(red)
    chip_flips = (4, 2, 6)

    def body(*refs):
        g_in, r_in = refs[:n_g], refs[n_g:n_g + n_r]
        g_out, r_out = refs[n_g + n_r:2 * n_g + n_r], refs[2 * n_g + n_r:2 * (n_g + n_r)]
        scr = refs[2 * (n_g + n_r):]
        sib, land = scr[:n_r], scr[n_r:2 * n_r]
        g_send, g_recv, s_send, s_recv, i_send, i_recv, f_send, f_recv = scr[2 * n_r:]
        me = _my_pos()
        c = me[2]
        sibling = _flip(me, 1)

        def slot(a, pos):
            return g_out[a].at[pl.ds(pl.multiple_of(_dev_index(pos) * SUBLANES, SUBLANES), SUBLANES), :]

        def gcopy(a, k):
            return pltpu.make_async_remote_copy(
                src_ref=g_in[a], dst_ref=slot(a, me), send_sem=g_send.at[a, k - 1], recv_sem=g_recv.at[a, k - 1],
                device_id=_flip(me, k), device_id_type=MESH)

        def scopy(a):
            return pltpu.make_async_remote_copy(
                src_ref=r_in[a], dst_ref=sib[a], send_sem=s_send.at[a], recv_sem=s_recv.at[a],
                device_id=sibling, device_id_type=MESH)

        def icopy(a, j):
            return pltpu.make_async_remote_copy(
                src_ref=r_out[a], dst_ref=land[a].at[j], send_sem=i_send.at[a, j], recv_sem=i_recv.at[a, j],
                device_id=_flip(me, chip_flips[j]), device_id_type=MESH)

        def fcopy(a, j):
            return pltpu.make_async_remote_copy(
                src_ref=land[a].at[j], dst_ref=land[a].at[j], send_sem=f_send.at[a, j], recv_sem=f_recv.at[a, j],
                device_id=sibling, device_id_type=MESH)

        gathers = [gcopy(a, k) for a in range(n_g) for k in range(1, N_DEV)]
        swaps = [scopy(a) for a in range(n_r)]
        for cp in gathers + swaps:
            cp.start()
        for a in range(n_g):
            g_out[a][pl.ds(pl.multiple_of(_dev_index(me) * SUBLANES, SUBLANES), SUBLANES), :] = g_in[a][...]
        for a in range(n_r):
            swaps[a].wait_recv()
            r_out[a][...] = r_in[a][...] + sib[a][...]

        for core in range(2):
            mine = [a for a in range(n_r) if a % 2 == core]
            theirs = [a for a in range(n_r) if a % 2 != core]

            @pl.when(c == core)
            def _():
                out = [icopy(a, j) for a in mine for j in range(3)]
                for cp in out:
                    cp.start()
                fwd = []
                for a in mine:
                    for j in range(3):
                        icopy(a, j).wait_recv()
                        cp = fcopy(a, j)
                        cp.start()
                        fwd.append(cp)
                for a in theirs:
                    for j in range(3):
                        fcopy(a, j).wait_recv()
                for cp in out + fwd:
                    cp.wait_send()

        for a in range(n_r):
            r_out[a][...] = (r_out[a][...] + land[a][1]) + (land[a][0] + land[a][2])
        for a in range(n_g):
            for k in range(1, N_DEV):
                pltpu.make_async_remote_copy(
                    src_ref=g_in[a], dst_ref=slot(a, _flip(me, k)), send_sem=g_send.at[a, k - 1],
                    recv_sem=g_recv.at[a, k - 1], device_id=_flip(me, k), device_id_type=MESH).wait_recv()
        for cp in gathers + swaps:
            cp.wait_send()

    shapes = [tuple(a.shape) for a in red]
    outs = pl.pallas_call(
        body, name="reduce_small",
        out_shape=[_sds((N_DEV * SUBLANES, a.shape[1]), F32) for a in gath] + [_sds(s, F32) for s in shapes],
        in_specs=[_whole()] * (n_g + n_r), out_specs=[_whole()] * (n_g + n_r),
        scratch_shapes=[pltpu.VMEM(s, F32) for s in shapes] + [pltpu.VMEM((3,) + s, F32) for s in shapes]
        + [pltpu.SemaphoreType.DMA((n_g, N_DEV - 1)), pltpu.SemaphoreType.DMA((n_g, N_DEV - 1)),
           pltpu.SemaphoreType.DMA((n_r,)), pltpu.SemaphoreType.DMA((n_r,)),
           pltpu.SemaphoreType.DMA((n_r, 3)), pltpu.SemaphoreType.DMA((n_r, 3)),
           pltpu.SemaphoreType.DMA((n_r, 3)), pltpu.SemaphoreType.DMA((n_r, 3))],
        compiler_params=pltpu.CompilerParams(vmem_limit_bytes=VMEM_LIMIT),
    )(*gath, *red)
    return outs[:n_g], outs[n_g:]


def _region(ref, shard_shape, col_sharded, pos):
    r, cdim = shard_shape
    d = _dev_index(pos)
    if col_sharded:
        return ref.at[:, pl.ds(pl.multiple_of(d * cdim, LANES), cdim)]
    return ref.at[pl.ds(pl.multiple_of(d * r, 2 * SUBLANES), r), :]


def _gather_carry(shards, col_sharded):
    n_w = len(shards)
    shapes = [tuple(s.shape) for s in shards]
    full_shapes = [(s[0], s[1] * N_DEV) if cs else (s[0] * N_DEV, s[1]) for s, cs in zip(shapes, col_sharded)]

    def tools(out_refs, scr):
        send_sems, recv_sems = scr[n_w], scr[n_w + 1]
        me = _my_pos()
        x, y, c = me
        sibling = (x, y, 1 - c)
        chips = [(1 - x, y), (x, 1 - y), (1 - x, 1 - y)]

        def region(w, pos):
            return _region(out_refs[w], shapes[w], col_sharded[w], pos)

        def copy(w, k, block, to, src=None):
            return pltpu.make_async_remote_copy(
                src_ref=region(w, block) if src is None else src, dst_ref=region(w, block),
                send_sem=send_sems.at[w, k], recv_sem=recv_sems.at[w, k], device_id=to, device_id_type=MESH)

        def first(w):
            return [copy(w, 0, me, sibling, src=scr[w])] + [
                copy(w, 1 + j, me, (*chip, c), src=scr[w]) for j, chip in enumerate(chips)]

        def mine(w):
            return pltpu.make_async_copy(scr[w], region(w, me), scr[n_w + 2].at[w])

        return me, c, sibling, chips, copy, first, mine

    def start(ins, outs, scr):
        _, _, _, _, _, first, mine = tools(outs, scr)
        for w in range(n_w):
            scr[w][...] = ins[w][...].astype(BF16)
            for cp in first(w) + [mine(w)]:
                cp.start()

    def finish(ins, outs, scr):
        me, c, sibling, chips, copy, first, mine = tools(outs, scr)
        passed = []
        for w in range(n_w):
            for j, chip in enumerate(chips):
                copy(w, 1 + j, (*chip, c), me).wait_recv()
                fwd = copy(w, 4 + j, (*chip, c), sibling)
                fwd.start()
                passed.append(fwd)
        for w in range(n_w):
            copy(w, 0, sibling, me).wait_recv()
            for j, chip in enumerate(chips):
                copy(w, 4 + j, (*chip, 1 - c), me).wait_recv()
        for w in range(n_w):
            for cp in first(w):
                cp.wait_send()
            mine(w).wait()
        for cp in passed:
            cp.wait_send()

    return _Carry(
        inputs=list(shards), in_specs=[_whole()] * n_w,
        out_shape=[_sds(s, BF16) for s in full_shapes], out_specs=[_any()] * n_w,
        scratch=[pltpu.VMEM(s, BF16) for s in shapes]
        + [pltpu.SemaphoreType.DMA((n_w, N_DEV - 1)), pltpu.SemaphoreType.DMA((n_w, N_DEV - 1)),
           pltpu.SemaphoreType.DMA((n_w,))],
        start=start, finish=finish)


def _scatter_carry(grads_bf, shard_shapes, col_sharded, relations):
    n_w = len(grads_bf)
    shapes = [tuple(s) for s in shard_shapes]

    def copies(ins, outs, scr):
        send_sems, recv_sems = scr
        me = _my_pos()
        out = []
        for w in range(n_w):
            for i, k in enumerate(relations[w]):
                peer = _flip(me, k)
                out.append(pltpu.make_async_remote_copy(
                    src_ref=_region(ins[w], shapes[w], col_sharded[w], peer), dst_ref=outs[w].at[i],
                    send_sem=send_sems.at[w, i], recv_sem=recv_sems.at[w, i],
                    device_id=peer, device_id_type=MESH))
        return out

    def start(ins, outs, scr):
        for cp in copies(ins, outs, scr):
            cp.start()

    def finish(ins, outs, scr):
        cps = copies(ins, outs, scr)
        for cp in cps:
            cp.wait_recv()
        for cp in cps:
            cp.wait_send()

    return _Carry(
        inputs=list(grads_bf), in_specs=[_any()] * n_w,
        out_shape=[_sds((len(r),) + s, BF16) for r, s in zip(relations, shapes)], out_specs=[_any()] * n_w,
        scratch=[pltpu.SemaphoreType.DMA((n_w, N_DEV - 1)), pltpu.SemaphoreType.DMA((n_w, N_DEV - 1))],
        start=start, finish=finish)


def _block_diag(w):
    eye = jnp.eye(N_HEADS, dtype=w.dtype)
    return (eye[:, None, :, None] * w[:, :, None, :]).reshape(N_HEADS * HEAD_DIM, N_HEADS * HEAD_DIM)


def _pad_rows(a):
    return jnp.pad(a, ((0, SUBLANES - a.shape[0]), (0, 0)))


def _columns_from_devices(gathered, rows):
    w = gathered.shape[1]
    return gathered.reshape(N_DEV, SUBLANES, w)[:, :rows].transpose(1, 0, 2).reshape(rows, N_DEV * w)


def _local_step(x2, target, mod, w_in_f, w_full, conv_w_full, ffn_cw_full,
                g_mix_pre, g_mix_post, conv_b, w_rgate, b_rgate, w_igate, b_igate, lru_a, v_norm_g, v_norm_b,
                w_spatial, b_spatial, g_lru_out, g_gmlp_out, g_ffn_pre, g_ffn_post, ffn_conv_b,
                gather=None, scatter=None):
    sh_m, sc_m, gt_m, sh_f, sc_f, gt_f = [mod[k] for k in range(N_MOD)]
    wr_bd = _block_diag(w_rgate[0]).astype(BF16)
    wi_bd = _block_diag(w_igate[0]).astype(BF16)
    b_r = b_rgate.reshape(1, LRU_W)
    b_i = b_igate.reshape(1, LRU_W)
    b_sp_t = b_spatial[0].T
    w_sp_t = jnp.swapaxes(w_spatial[0], 1, 2)

    def arriving(name):
        return gather(name) if gather else None

    near, far = (1, 2, 3, 4, 5), (6, 7)

    def leaving(*parts):
        return scatter(parts) if scatter else None

    def received(recv, parts, outs):
        for (name, _, _), out in zip(parts, outs):
            recv.setdefault(name, []).append(out)

    def landed(name, carried):
        return carried[0] if gather else w_full[name]

    mix_params = (conv_w_full, conv_b, wr_bd, wi_bd, b_r, b_i, lru_a, v_norm_g, v_norm_b)
    (z, h, ycat, hl), got = _mix_fwd(x2, sh_m, sc_m, g_mix_pre, w_in_f, *mix_params, w_spatial[0], b_sp_t,
                                     g_lru_out, g_gmlp_out, carry=arriving("w_up"))
    w_up_f = landed("w_up", got)
    w_out_f = w_full["w_out"]
    (y, x1, h2, up_pre), got = _out_up_fwd(ycat, x2, w_out_f, g_mix_post, gt_m, g_ffn_pre, sc_f, sh_f, w_up_f,
                                           carry=arriving("w_down"))
    w_down_f = landed("w_down", got)
    act, y2, dout, loss_acc, up = _ffn_fwd(up_pre, ffn_cw_full, ffn_conv_b, w_down_f, x1, gt_f, g_ffn_post, target)

    recv = {}
    d_y2, d_up, vs_ffn, cs_ffn = _ffn_bwd(dout, y2, up_pre, up, ffn_cw_full, w_down_f, gt_f, g_ffn_post)
    gw_up, _ = _wgrad(h2, d_up, D_FF // 2, "wgrad_up")
    parts = [("w_up", gw_up[1], near)]
    gw_down, got = _wgrad(act, d_y2, D_MODEL // 2, "wgrad_down", carry=leaving(*parts))
    received(recv, parts, got)
    parts = [("w_up", gw_up[1], far), ("w_down", gw_down[1], near)]
    (d_x1, d_y, d_ycat, vs_up), got = _up_bwd(
        d_up, w_up_f, x1, dout, y, w_out_f, g_ffn_pre, sc_f, g_mix_post, gt_m, carry=leaving(*parts))
    received(recv, parts, got)
    gw_out, _ = _wgrad(ycat, d_y, D_MODEL, "wgrad_out")
    parts = [("w_down", gw_down[1], far), ("w_out", gw_out[1], near + far)]
    (d_z, vs_mix, dcw, d_wr, d_wi, d_ws, d_bs), got = _mix_bwd(
        d_ycat, z, hl, *mix_params, w_spatial[0], w_sp_t, b_sp_t, g_lru_out, g_gmlp_out, carry=leaving(*parts))
    received(recv, parts, got)
    gw_in, _ = _wgrad(h, d_z, IN_COLS // 2, "wgrad_in")
    parts = [("w_in", gw_in[1], near + far)]
    (grad_x, vs_in), got = _in_bwd(d_z, w_in_f, x2, d_x1, g_mix_pre, sc_m, carry=leaving(*parts))
    received(recv, parts, got)

    gath = [vs_in, vs_up, vs_ffn, loss_acc]
    red = [cs_ffn, vs_mix, dcw, d_wr, d_wi, d_ws.reshape(N_GROUPS * POS_BLOCK, POS_BLOCK), d_bs]
    return dict(grad_x=grad_x, gath=gath, red=red, recv=recv, w_in=gw_in, w_out=gw_out, w_up=gw_up, w_down=gw_down)


def kernel(x, c, w_ada, b_ada, g_mix_pre, g_mix_post, w_in, conv_w, conv_b, w_rgate, b_rgate, w_igate, b_igate, lru_a, v_norm_g, v_norm_b, w_spatial, b_spatial, g_lru_out, g_gmlp_out, w_out, g_ffn_pre, g_ffn_post, w_up, ffn_conv_w, ffn_conv_b, w_down, loss_target, m_w_ada, m_b_ada, m_g_mix_pre, m_g_mix_post, m_w_in, m_conv_w, m_conv_b, m_w_rgate, m_b_rgate, m_w_igate, m_b_igate, m_lru_a, m_v_norm_g, m_v_norm_b, m_w_spatial, m_b_spatial, m_g_lru_out, m_g_gmlp_out, m_w_out, m_g_ffn_pre, m_g_ffn_post, m_w_up, m_ffn_conv_w, m_ffn_conv_b, m_w_down, v_w_ada, v_b_ada, v_g_mix_pre, v_g_mix_post, v_w_in, v_conv_w, v_conv_b, v_w_rgate, v_b_rgate, v_w_igate, v_b_igate, v_lru_a, v_v_norm_g, v_v_norm_b, v_w_spatial, v_b_spatial, v_g_lru_out, v_g_gmlp_out, v_w_out, v_g_ffn_pre, v_g_ffn_post, v_w_up, v_ffn_conv_w, v_ffn_conv_b, v_w_down):
    me = _dev_index(_my_pos())
    ada_cols = w_ada.shape[-1]

    big_w = dict(w_in=(w_in, m_w_in, v_w_in, True), w_out=(w_out, m_w_out, v_w_out, False),
                 w_up=(w_up, m_w_up, v_w_up, True), w_down=(w_down, m_w_down, v_w_down, False))

    def gather(name):
        return _gather_carry([big_w[name][0][0]], [big_w[name][3]])

    def scatter(parts):
        return _scatter_carry([g for _, g, _ in parts], [big_w[n][0].shape[1:] for n, _, _ in parts],
                              [big_w[n][3] for n, _, _ in parts], [rel for _, _, rel in parts])

    (c_all, cw_all, fcw_all), (w_in_f,) = _gather_multi(
        [jnp.broadcast_to(c, (SUBLANES, D_MODEL)), _pad_rows(conv_w[0]), _pad_rows(ffn_conv_w[0])], "gather_start",
        carry=gather("w_in"))
    conv_w_full = _columns_from_devices(cw_all, LRU_CONV_K)
    ffn_cw_full = _columns_from_devices(fcw_all, FFN_CONV_K)
    mod_mine = _mod_part(c_all, w_ada[0], b_ada)
    (mod_all,), (w_out_f,) = _gather_multi([mod_mine], "gather_mod", carry=gather("w_out"))
    mod = lax.dynamic_index_in_dim(mod_all.reshape(N_DEV, N_DEV, ada_cols), me, axis=1, keepdims=False)
    mod = mod.reshape(N_MOD, 1, D_MODEL)

    loc = _local_step(x[0], loss_target[0], mod, w_in_f, dict(w_out=w_out_f), conv_w_full, ffn_cw_full,
                      g_mix_pre, g_mix_post, conv_b, w_rgate, b_rgate, w_igate, b_igate, lru_a, v_norm_g, v_norm_b,
                      w_spatial, b_spatial, g_lru_out, g_gmlp_out, g_ffn_pre, g_ffn_post, ffn_conv_b,
                      gather=gather, scatter=scatter)
    grad_x = loc["grad_x"]

    results = {}
    for name, (w_, m_, v_, cs) in big_w.items():
        results[name] = _adamw_sum(w_, loc[name][0], loc["recv"][name], m_, v_, cs, "adamw_" + name)

    gathered, reduced = _reduce_small(loc["gath"], loc["red"])
    params = dict(
        b_ada=(b_ada, m_b_ada, v_b_ada), g_mix_pre=(g_mix_pre, m_g_mix_pre, v_g_mix_pre),
        g_mix_post=(g_mix_post, m_g_mix_post, v_g_mix_post), conv_b=(conv_b, m_conv_b, v_conv_b),
        w_rgate=(w_rgate, m_w_rgate, v_w_rgate), b_rgate=(b_rgate, m_b_rgate, v_b_rgate),
        w_igate=(w_igate, m_w_igate, v_w_igate), b_igate=(b_igate, m_b_igate, v_b_igate),
        lru_a=(lru_a, m_lru_a, v_lru_a), v_norm_g=(v_norm_g, m_v_norm_g, v_v_norm_g),
        v_norm_b=(v_norm_b, m_v_norm_b, v_v_norm_b), w_spatial=(w_spatial, m_w_spatial, v_w_spatial),
        b_spatial=(b_spatial, m_b_spatial, v_b_spatial), g_lru_out=(g_lru_out, m_g_lru_out, v_g_lru_out),
        g_gmlp_out=(g_gmlp_out, m_g_gmlp_out, v_g_gmlp_out), g_ffn_pre=(g_ffn_pre, m_g_ffn_pre, v_g_ffn_pre),
        g_ffn_post=(g_ffn_post, m_g_ffn_post, v_g_ffn_post), ffn_conv_b=(ffn_conv_b, m_ffn_conv_b, v_ffn_conv_b))
    conv_params = dict(conv_w=(conv_w, m_conv_w, v_conv_w), ffn_conv_w=(ffn_conv_w, m_ffn_conv_w, v_ffn_conv_w))
    small_results, loss = _adamw_small(gathered, reduced, params, conv_params)
    results.update(small_results)
    loss = loss.reshape(())

    results["w_ada"] = _adamw_wada(c_all, gathered[0], gathered[1], gathered[2], w_ada, m_w_ada, v_w_ada)

    order = ["w_ada", "b_ada", "g_mix_pre", "g_mix_post", "w_in", "conv_w", "conv_b", "w_rgate", "b_rgate", "w_igate",
             "b_igate", "lru_a", "v_norm_g", "v_norm_b", "w_spatial", "b_spatial", "g_lru_out", "g_gmlp_out", "w_out",
             "g_ffn_pre", "g_ffn_post", "w_up", "ffn_conv_w", "ffn_conv_b", "w_down"]
    outs = [loss, grad_x[None]]
    for kind in range(4):
        outs += [results[n][kind] for n in order]
    return tuple(outs)
```

```python
import functools

import jax
import jax.numpy as jnp
from jax import lax
from jax.experimental import pallas as pl
from jax.experimental.pallas import tpu as pltpu

F32 = jnp.float32
BF16 = jnp.bfloat16

D_MODEL = 1024
LRU_W = 512
GMLP_W = 512
N_HEADS = 8
HEAD_DIM = 64
N_GROUPS = 4
POS_BLOCK = 128
CHUNK = 64
IN_COLS = 2048
D_FF = 3072
N_MOD = 6
N_DEV = 8
EPS = 1e-6
LRU_C = 8.0
LRU_CONV_K = 4
FFN_CONV_K = 3

ADAM_LR = 0.001
ADAM_B1 = 0.9
ADAM_B2 = 0.999
ADAM_EPS = 1e-08
ADAM_WD = 0.01
ADAM_STEP = 10

LANES = 128
SUBLANES = 8
TT_BIG = 512
TT_MIX = 256
TT_WG = 1024
FF_CW = 512
VMEM_LIMIT = 56 * 1024 * 1024

MESH = pl.DeviceIdType.MESH


def _sds(shape, dtype):
    return jax.ShapeDtypeStruct(shape, dtype)


def _cparams(sem=None):
    return pltpu.CompilerParams(dimension_semantics=sem, vmem_limit_bytes=VMEM_LIMIT)


def _whole():
    return pl.BlockSpec(memory_space=pltpu.VMEM)


def _const(shape):
    nd = len(shape)
    return pl.BlockSpec(shape, lambda *_: (0,) * nd)


def _any():
    return pl.BlockSpec(memory_space=pl.ANY)


class _Carry:
    def __init__(self, inputs, in_specs, out_shape, out_specs, scratch, start, finish):
        self.inputs, self.in_specs, self.out_shape, self.out_specs = inputs, in_specs, out_shape, out_specs
        self.scratch, self.start, self.finish = scratch, start, finish


def _call(body, name, grid, in_specs, out_specs, out_shape, scratch, args, carry=None):
    n_in, n_out, n_scr = len(in_specs), len(out_specs), len(scratch)
    c_in = len(carry.in_specs) if carry else 0
    c_out = len(carry.out_specs) if carry else 0

    def full_body(*refs):
        ins = refs[:n_in]
        c_ins = refs[n_in:n_in + c_in]
        outs = refs[n_in + c_in:n_in + c_in + n_out]
        c_outs = refs[n_in + c_in + n_out:n_in + c_in + n_out + c_out]
        scr = refs[n_in + c_in + n_out + c_out:n_in + c_in + n_out + c_out + n_scr]
        c_scr = refs[n_in + c_in + n_out + c_out + n_scr:]
        if carry:
            first = functools.reduce(lambda a, b: a & b, [pl.program_id(d) == 0 for d in range(len(grid))])
            last = functools.reduce(lambda a, b: a & b, [pl.program_id(d) == g - 1 for d, g in enumerate(grid)])

            @pl.when(first)
            def _():
                carry.start(c_ins, c_outs, c_scr)

        body(*ins, *outs, *scr)
        if carry:
            @pl.when(last)
            def _():
                carry.finish(c_ins, c_outs, c_scr)

    res = pl.pallas_call(
        full_body, name=name, grid=grid,
        in_specs=list(in_specs) + (list(carry.in_specs) if carry else []),
        out_specs=list(out_specs) + (list(carry.out_specs) if carry else []),
        out_shape=list(out_shape) + (list(carry.out_shape) if carry else []),
        scratch_shapes=list(scratch) + (list(carry.scratch) if carry else []),
        compiler_params=_cparams(("arbitrary",) * len(grid)),
    )(*args, *(carry.inputs if carry else []))
    return res[:n_out], res[n_out:]


def _gelu(x):
    u = 0.7978845608028654 * (x + 0.044715 * x * x * x)
    return 0.5 * x * (1.0 + jnp.tanh(u))


def _gelu_and_grad(x):
    x2 = x * x
    u = 0.7978845608028654 * (x + 0.044715 * x * x2)
    t = jnp.tanh(u)
    g = 0.5 * x * (1.0 + t)
    dg = 0.5 * (1.0 + t) + 0.5 * x * (1.0 - t * t) * 0.7978845608028654 * (1.0 + 3.0 * 0.044715 * x2)
    return g, dg


def _sigmoid(x):
    return 1.0 / (1.0 + jnp.exp(-x))


def _softplus(x):
    return jnp.maximum(x, 0.0) + jnp.log1p(jnp.exp(-jnp.abs(x)))


def _neg_expm1(x):
    series = -x * (1.0 + x * (0.5 + x * (1.0 / 6.0 + x * (1.0 / 24.0 + x * (1.0 / 120.0)))))
    return jnp.where(x > -0.1, series, 1.0 - jnp.exp(x))


def _dot(a, b):
    return jnp.dot(a.astype(BF16), b.astype(BF16), preferred_element_type=F32)


def _dot_nt(a, b):
    return lax.dot_general(a.astype(BF16), b.astype(BF16), (((1,), (1,)), ((), ())), preferred_element_type=F32)


def _dot_tn(a, b):
    return lax.dot_general(a.astype(BF16), b.astype(BF16), (((0,), (0,)), ((), ())), preferred_element_type=F32)


def _rows(shape):
    return lax.broadcasted_iota(jnp.int32, shape, 0)


def _shift_down(cur, prev8, s):
    if s == 0:
        return cur
    n = cur.shape[0]
    r = pltpu.roll(cur, s, 0)
    p = pltpu.roll(prev8, s, 0)
    top = jnp.where(_rows(p.shape) < s, p, r[0:SUBLANES])
    if n == SUBLANES:
        return top
    return jnp.concatenate([top, r[SUBLANES:]], axis=0)


def _shift_up(cur, next8, s):
    if s == 0:
        return cur
    n = cur.shape[0]
    r = pltpu.roll(cur, n - s, 0)
    q = pltpu.roll(next8, SUBLANES - s, 0)
    bot = jnp.where(_rows(q.shape) >= SUBLANES - s, q, r[n - SUBLANES:])
    if n == SUBLANES:
        return bot
    return jnp.concatenate([r[:n - SUBLANES], bot], axis=0)


def _scan_fwd(a, b):
    n = a.shape[0]
    rows = _rows(a.shape)
    s = 1
    while s < n:
        a_s = pltpu.roll(a, s, 0)
        b_s = pltpu.roll(b, s, 0)
        m = rows >= s
        b = jnp.where(m, a * b_s + b, b)
        a = jnp.where(m, a * a_s, a)
        s *= 2
    return a, b


def _scan_rev(a, b):
    n = a.shape[0]
    rows = _rows(a.shape)
    s = 1
    while s < n:
        a_s = pltpu.roll(a, n - s, 0)
        b_s = pltpu.roll(b, n - s, 0)
        m = rows < n - s
        b = jnp.where(m, b + a * b_s, b)
        a = jnp.where(m, a * a_s, a)
        s *= 2
    return a, b


def _rms(x):
    r = lax.rsqrt(jnp.mean(x * x, axis=-1, keepdims=True) + EPS)
    return x * r, r


def _rms_bwd(d_n, n, r):
    return r * (d_n - n * jnp.mean(d_n * n, axis=-1, keepdims=True))


def _colsum(x):
    return jnp.sum(x, axis=0, keepdims=True)


def _lru_gates(xc, wr_ref, wi_ref, br, bi, sp_a):
    r = _sigmoid(_dot(xc, wr_ref[...]) + br)
    i = _sigmoid(_dot(xc, wi_ref[...]) + bi)
    la = -LRU_C * r * sp_a
    a = jnp.exp(la)
    mult = jnp.sqrt(_neg_expm1(2.0 * la))
    return r, i, a, mult


def _lru_conv(lx, prev8, cw_ref, cb):
    xc = cb + cw_ref[LRU_CONV_K - 1:LRU_CONV_K, :] * lx
    taps = []
    for k in range(LRU_CONV_K - 1):
        tap = _shift_down(lx, prev8, LRU_CONV_K - 1 - k)
        taps.append(tap)
        xc = xc + cw_ref[k:k + 1, :] * tap
    return xc, taps


def _ws_mask(transposed=False):
    i = lax.broadcasted_iota(jnp.int32, (POS_BLOCK, POS_BLOCK), 0)
    j = lax.broadcasted_iota(jnp.int32, (POS_BLOCK, POS_BLOCK), 1)
    if transposed:
        i, j = j, i
    return (j // CHUNK) <= (i // CHUNK)


def _gmlp_v(gv, vg, vb):
    av, dav = _gelu_and_grad(gv)
    mu = jnp.mean(av, axis=-1, keepdims=True)
    cen = av - mu
    rs = lax.rsqrt(jnp.mean(cen * cen, axis=-1, keepdims=True) + EPS)
    vhat = cen * rs
    return vhat * vg + vb, vhat, rs, dav


def _mix_fwd(x, sh, sc, g_pre, w_in, conv_w, conv_b, wr_bd, wi_bd, b_r, b_i, lru_a, vn_g, vn_b, w_sp, b_sp_t,
             g_lru, g_gmlp, carry=None):
    s_len = x.shape[0]
    tt = min(TT_MIX, s_len)
    nblk = tt // POS_BLOCK

    def body(x_ref, sh_ref, sc_ref, g_ref, w_ref, cw_ref, cb_ref, wr_ref, wi_ref, br_ref, bi_ref, la_ref, vg_ref,
             vb_ref, ws_ref, bst_ref, gl_ref, gg_ref, z_ref, h_ref, y_ref, hl_ref, prev8, hcar):
        i = pl.program_id(0)

        @pl.when(i == 0)
        def _():
            prev8[...] = jnp.zeros_like(prev8)
            hcar[...] = jnp.zeros_like(hcar)

        n_x, _ = _rms(x_ref[...])
        h = (n_x * g_ref[...] * (1.0 + sc_ref[...]) + sh_ref[...]).astype(BF16)
        h_ref[...] = h
        z_ref[...] = jnp.dot(h, w_ref[...], preferred_element_type=F32)

        lx = z_ref[:, 0:LRU_W]
        gate = z_ref[:, LRU_W:2 * LRU_W]
        gu = z_ref[:, 2 * LRU_W:2 * LRU_W + GMLP_W]
        gv = z_ref[:, 2 * LRU_W + GMLP_W:]

        xc, _ = _lru_conv(lx, prev8[...], cw_ref, cb_ref[...])
        prev8[...] = lx[tt - SUBLANES:]
        sp_a = _softplus(-la_ref[...])
        _, ig, a, mult = _lru_gates(xc, wr_ref, wi_ref, br_ref[...], bi_ref[...], sp_a)
        bx = mult * (ig * xc)
        a_cum, b_cum = _scan_fwd(a, bx)
        hl = a_cum * hcar[0:1, :] + b_cum
        hcar[...] = jnp.broadcast_to(hl[tt - 1:tt, :], hcar.shape)
        hl_ref[...] = hl
        y_lru = hl * _gelu(gate)
        n_l, _ = _rms(y_lru)
        y_ref[:, 0:LRU_W] = (n_l * gl_ref[...]).astype(BF16)

        u = _gelu(gu)
        v, _, _, _ = _gmlp_v(gv, vg_ref[...], vb_ref[...])
        mask = _ws_mask()
        sp_parts = []
        for nb in range(nblk):
            row = []
            for g in range(N_GROUPS):
                wsm = jnp.where(mask, ws_ref[g], 0.0)
                vblk = v[nb * POS_BLOCK:(nb + 1) * POS_BLOCK, g * LANES:(g + 1) * LANES]
                row.append(_dot(wsm, vblk) + bst_ref[:, g:g + 1])
            sp_parts.append(jnp.concatenate(row, axis=1))
        sp = jnp.concatenate(sp_parts, axis=0) if nblk > 1 else sp_parts[0]
        n_g, _ = _rms(u * sp)
        y_ref[:, LRU_W:] = (n_g * gg_ref[...]).astype(BF16)

    row = lambda c: pl.BlockSpec((tt, c), lambda i: (i, 0))
    v512 = _const((1, LRU_W))
    vec = _const((1, D_MODEL))
    return _call(
        body, "mix_fwd", (s_len // tt,),
        in_specs=[row(D_MODEL), vec, vec, vec, _whole(),
                  _const((LRU_CONV_K, LRU_W)), v512, _whole(), _whole(), v512, v512, v512, v512, v512,
                  _whole(), _whole(), v512, v512],
        out_specs=[row(IN_COLS), row(D_MODEL), row(LRU_W + GMLP_W), row(LRU_W)],
        out_shape=[_sds((s_len, IN_COLS), F32), _sds((s_len, D_MODEL), BF16),
                   _sds((s_len, LRU_W + GMLP_W), BF16), _sds((s_len, LRU_W), F32)],
        scratch=[pltpu.VMEM((SUBLANES, LRU_W), F32), pltpu.VMEM((SUBLANES, LRU_W), F32)],
        args=(x, sh, sc, g_pre, w_in, conv_w, conv_b, wr_bd, wi_bd, b_r, b_i, lru_a, vn_g, vn_b, w_sp, b_sp_t,
              g_lru, g_gmlp), carry=carry)


def _out_up_fwd(ycat, x, w_out, g_post, gt_m, g_pre, sc_f, sh_f, w_up, carry=None):
    s_len = x.shape[0]
    tt = min(TT_MIX, s_len)

    def body(yc_ref, x_ref, wo_ref, gp_ref, gt_ref, g2_ref, sc_ref, sh_ref, wu_ref, y_ref, x1_ref, h2_ref, up_ref):
        y = jnp.dot(yc_ref[...], wo_ref[...], preferred_element_type=F32)
        y_ref[...] = y
        n_y, _ = _rms(y)
        x1 = x_ref[...] + gt_ref[...] * (n_y * gp_ref[...])
        x1_ref[...] = x1
        n1, _ = _rms(x1)
        h2 = (n1 * g2_ref[...] * (1.0 + sc_ref[...]) + sh_ref[...]).astype(BF16)
        h2_ref[...] = h2
        up_ref[0] = jnp.dot(h2, wu_ref[:, 0:D_FF], preferred_element_type=F32)
        up_ref[1] = jnp.dot(h2, wu_ref[:, D_FF:], preferred_element_type=F32)

    row = lambda c: pl.BlockSpec((tt, c), lambda i: (i, 0))
    vec = _const((1, D_MODEL))
    return _call(
        body, "out_up_fwd", (s_len // tt,),
        in_specs=[row(D_MODEL), row(D_MODEL), _whole(), vec, vec, vec, vec, vec, _whole()],
        out_specs=[row(D_MODEL), row(D_MODEL), row(D_MODEL), pl.BlockSpec((2, tt, D_FF), lambda i: (0, i, 0))],
        out_shape=[_sds((s_len, D_MODEL), F32), _sds((s_len, D_MODEL), F32), _sds((s_len, D_MODEL), BF16),
                   _sds((2, s_len, D_FF), F32)],
        scratch=[], args=(ycat, x, w_out, g_post, gt_m, g_pre, sc_f, sh_f, w_up), carry=carry)


def _ffn_conv(up_pre, prev8, cw_ref, cb):
    up = cb + cw_ref[FFN_CONV_K - 1:FFN_CONV_K, :] * up_pre
    taps = []
    for k in range(FFN_CONV_K - 1):
        tap = _shift_down(up_pre, prev8, FFN_CONV_K - 1 - k)
        taps.append(tap)
        up = up + cw_ref[k:k + 1, :] * tap
    return up, taps


def _ffn_fwd(up_pre, ffn_cw, ffn_cb, w_down, x1, gt_f, g_post, target):
    s_len = x1.shape[0]
    tt = min(TT_BIG, s_len)
    cw = FF_CW
    nc = D_FF // cw

    def body(up_ref, cwg_ref, cwv_ref, cbg_ref, cbv_ref, wd_ref, x1_ref, gt_ref, gp_ref, tg_ref,
             act_ref, dy2_ref, dout_ref, loss_ref, upc_ref, vs_ref, prev, acc):
        i = pl.program_id(0)
        c = pl.program_id(1)

        @pl.when(i == 0)
        def _():
            prev[c] = jnp.zeros((2, SUBLANES, cw), F32)

        @pl.when((i == 0) & (c == 0))
        def _():
            loss_ref[...] = jnp.zeros_like(loss_ref)
            vs_ref[...] = jnp.zeros_like(vs_ref)

        ug, _ = _ffn_conv(up_ref[0], prev[c, 0], cwg_ref, cbg_ref[...])
        uv, _ = _ffn_conv(up_ref[1], prev[c, 1], cwv_ref, cbv_ref[...])
        prev[c, 0] = up_ref[0, tt - SUBLANES:, :]
        prev[c, 1] = up_ref[1, tt - SUBLANES:, :]
        upc_ref[0] = ug
        upc_ref[1] = uv
        act = (_gelu(ug) * uv).astype(BF16)
        act_ref[...] = act
        part = jnp.dot(act, wd_ref[...], preferred_element_type=F32)

        @pl.when(c == 0)
        def _():
            acc[...] = part

        @pl.when(c > 0)
        def _():
            acc[...] += part

        @pl.when(c == nc - 1)
        def _():
            n2, r2 = _rms(acc[...])
            out = x1_ref[...] + gt_ref[...] * (n2 * gp_ref[...])
            err = out - tg_ref[...]
            do = err * (1.0 / D_MODEL)
            dout_ref[...] = do
            loss_ref[...] += jnp.broadcast_to(0.5 * jnp.sum(err * err, keepdims=True) * (1.0 / D_MODEL), loss_ref.shape)
            vs_ref[0:1, :] += _colsum(do * n2 * gp_ref[...])
            vs_ref[1:2, :] += _colsum(do * gt_ref[...] * n2)
            dy2_ref[...] = _rms_bwd(do * gt_ref[...] * gp_ref[...], n2, r2).astype(BF16)

    row = pl.BlockSpec((tt, D_MODEL), lambda i, c: (i, 0))
    vec = _const((1, D_MODEL))
    ffn_cb2 = ffn_cb.reshape(1, 2 * D_FF)
    return pl.pallas_call(
        body, name="ffn_fwd", grid=(s_len // tt, nc),
        in_specs=[pl.BlockSpec((2, tt, cw), lambda i, c: (0, i, c)),
                  pl.BlockSpec((FFN_CONV_K, cw), lambda i, c: (0, c)),
                  pl.BlockSpec((FFN_CONV_K, cw), lambda i, c: (0, c + nc)),
                  pl.BlockSpec((1, cw), lambda i, c: (0, c)),
                  pl.BlockSpec((1, cw), lambda i, c: (0, c + nc)),
                  pl.BlockSpec((cw, D_MODEL), lambda i, c: (c, 0)),
                  row, vec, vec, row],
        out_specs=[pl.BlockSpec((tt, cw), lambda i, c: (i, c)), row, row, _const((SUBLANES, LANES)),
                   pl.BlockSpec((2, tt, cw), lambda i, c: (0, i, c)), _const((SUBLANES, D_MODEL))],
        out_shape=[_sds((s_len, D_FF), BF16), _sds((s_len, D_MODEL), BF16), _sds((s_len, D_MODEL), F32),
                   _sds((SUBLANES, LANES), F32), _sds((2, s_len, D_FF), F32), _sds((SUBLANES, D_MODEL), F32)],
        scratch_shapes=[pltpu.VMEM((nc, 2, SUBLANES, cw), F32), pltpu.VMEM((tt, D_MODEL), F32)],
        compiler_params=_cparams(("arbitrary", "arbitrary")),
    )(up_pre, ffn_cw, ffn_cw, ffn_cb2, ffn_cb2, w_down, x1, gt_f, g_post, target)


def _ffn_bwd(d_y2, up_pre, up, ffn_cw, w_down, carry=None):
    s_len = d_y2.shape[0]
    tt = min(TT_BIG, s_len)
    nt = s_len // tt
    cw = FF_CW
    nc = D_FF // cw

    def body(dy2_ref, up_ref, upc_ref, cwg_ref, cwv_ref, wd_ref, dup_ref, cs_ref, nxt, cs_acc):
        i = pl.program_id(0)
        c = pl.program_id(1)

        @pl.when(i == 0)
        def _():
            nxt[c] = jnp.zeros((2, SUBLANES, cw), F32)
            cs_acc[c] = jnp.zeros((2, SUBLANES, cw), F32)

        d_act = _dot_nt(dy2_ref[...], wd_ref[...])
        uv = upc_ref[1]
        gl, dgl = _gelu_and_grad(upc_ref[0])
        d_ug = d_act * uv * dgl
        d_uv = d_act * gl
        for half, (d_u, cw_ref) in enumerate(((d_ug, cwg_ref), (d_uv, cwv_ref))):
            nx = nxt[c, half]
            x_in = up_ref[half]
            d_pre = cw_ref[FFN_CONV_K - 1:FFN_CONV_K, :] * d_u
            sums = [None] * (FFN_CONV_K + 1)
            sums[FFN_CONV_K - 1] = _colsum(d_u * x_in)
            for k in range(FFN_CONV_K - 1):
                ahead = _shift_up(d_u, nx, FFN_CONV_K - 1 - k)
                d_pre = d_pre + cw_ref[k:k + 1, :] * ahead
                sums[k] = _colsum(ahead * x_in)
            sums[FFN_CONV_K] = _colsum(d_u)
            pad = jnp.zeros((SUBLANES - FFN_CONV_K - 1, cw), F32)
            cs_acc[c, half] += jnp.concatenate(sums + [pad], axis=0)
            nxt[c, half] = d_u[0:SUBLANES]
            dup_ref[half] = d_pre.astype(BF16)

        for cc in range(nc):
            @pl.when((i == nt - 1) & (c == cc))
            def _():
                cs_ref[:, cc * cw:(cc + 1) * cw] = cs_acc[cc, 0]
                cs_ref[:, D_FF + cc * cw:D_FF + (cc + 1) * cw] = cs_acc[cc, 1]

    row = pl.BlockSpec((tt, D_MODEL), lambda i, c: (nt - 1 - i, 0))
    blk = pl.BlockSpec((2, tt, cw), lambda i, c: (0, nt - 1 - i, c))
    return _call(
        body, "ffn_bwd", (nt, nc),
        in_specs=[row, blk, blk,
                  pl.BlockSpec((FFN_CONV_K, cw), lambda i, c: (0, c)),
                  pl.BlockSpec((FFN_CONV_K, cw), lambda i, c: (0, c + nc)),
                  pl.BlockSpec((cw, D_MODEL), lambda i, c: (c, 0))],
        out_specs=[blk, _const((SUBLANES, 2 * D_FF))],
        out_shape=[_sds((2, s_len, D_FF), BF16), _sds((SUBLANES, 2 * D_FF), F32)],
        scratch=[pltpu.VMEM((nc, 2, SUBLANES, cw), F32), pltpu.VMEM((nc, 2, SUBLANES, cw), F32)],
        args=(d_y2, up_pre, up, ffn_cw, ffn_cw, w_down), carry=carry)


def _up_bwd(d_up, w_up, x1, dout, y, w_out, g_pre, sc_f, g_post, gt_m, carry=None):
    s_len = x1.shape[0]
    tt = min(TT_BIG, s_len)

    def body(du_ref, wu_ref, x1_ref, do_ref, y_ref, wo_ref, g2_ref, sc_ref, gp_ref, gt_ref,
             dx1_ref, dy_ref, dyc_ref, vs_ref):
        @pl.when(pl.program_id(0) == 0)
        def _():
            vs_ref[...] = jnp.zeros_like(vs_ref)

        d_h2 = _dot_nt(du_ref[0], wu_ref[:, 0:D_FF]) + _dot_nt(du_ref[1], wu_ref[:, D_FF:])
        n1, r1 = _rms(x1_ref[...])
        ng = n1 * g2_ref[...]
        vs_ref[0:1, :] += _colsum(d_h2)
        vs_ref[1:2, :] += _colsum(d_h2 * ng)
        d_ng = d_h2 * (1.0 + sc_ref[...])
        vs_ref[2:3, :] += _colsum(d_ng * n1)
        d_x1 = do_ref[...] + _rms_bwd(d_ng * g2_ref[...], n1, r1)
        dx1_ref[...] = d_x1
        n_y, r_y = _rms(y_ref[...])
        vs_ref[3:4, :] += _colsum(d_x1 * n_y * gp_ref[...])
        d_on = d_x1 * gt_ref[...]
        vs_ref[4:5, :] += _colsum(d_on * n_y)
        d_y = _rms_bwd(d_on * gp_ref[...], n_y, r_y).astype(BF16)
        dy_ref[...] = d_y
        dyc_ref[...] = _dot_nt(d_y, wo_ref[...])

    row = lambda c: pl.BlockSpec((tt, c), lambda i: (i, 0))
    vec = _const((1, D_MODEL))
    return _call(
        body, "up_bwd", (s_len // tt,),
        in_specs=[pl.BlockSpec((2, tt, D_FF), lambda i: (0, i, 0)), _whole(), row(D_MODEL), row(D_MODEL), row(D_MODEL),
                  _whole(), vec, vec, vec, vec],
        out_specs=[row(D_MODEL), row(D_MODEL), row(LRU_W + GMLP_W), _const((SUBLANES, D_MODEL))],
        out_shape=[_sds((s_len, D_MODEL), F32), _sds((s_len, D_MODEL), BF16), _sds((s_len, LRU_W + GMLP_W), F32),
                   _sds((SUBLANES, D_MODEL), F32)],
        scratch=[], args=(d_up, w_up, x1, dout, y, w_out, g_pre, sc_f, g_post, gt_m), carry=carry)


def _head_pair_block(hd):
    return (slice((hd // 2) * HEAD_DIM, (hd // 2 + 1) * HEAD_DIM), slice((hd % 2) * HEAD_DIM, (hd % 2 + 1) * HEAD_DIM))


def _mix_bwd(d_ycat, z, hl, conv_w, conv_b, wr_bd, wi_bd, b_r, b_i, lru_a, vn_g, vn_b, w_sp, w_sp_t, b_sp_t,
             g_lru, g_gmlp, carry=None):
    s_len = z.shape[0]
    tt = min(TT_MIX, s_len)
    nt = s_len // tt
    nblk = tt // POS_BLOCK
    hb = tt // SUBLANES

    def body(dyc_ref, z_ref, zh_ref, hl_ref, hh_ref, cw_ref, cb_ref, wr_ref, wi_ref, br_ref, bi_ref, la_ref,
             vg_ref, vb_ref, ws_ref, wst_ref, bst_ref, gl_ref, gg_ref,
             dz_ref, vs_ref, dcw_ref, dwrb_ref, dwib_ref, dws_ref, dbs_ref, nxt_dxc, nxt_a, nxt_lam, dwr_ref, dwi_ref):
        i = pl.program_id(0)
        first_tile = i == nt - 1

        @pl.when(i == 0)
        def _():
            for ref in (vs_ref, dcw_ref, dwr_ref, dwi_ref, dws_ref, dbs_ref, nxt_dxc, nxt_a, nxt_lam):
                ref[...] = jnp.zeros_like(ref)

        lx = z_ref[:, 0:LRU_W]
        gate = z_ref[:, LRU_W:2 * LRU_W]
        gu = z_ref[:, 2 * LRU_W:2 * LRU_W + GMLP_W]
        gv = z_ref[:, 2 * LRU_W + GMLP_W:]
        prev8 = jnp.where(first_tile, 0.0, zh_ref[...])
        hprev8 = jnp.where(first_tile, 0.0, hh_ref[...])

        xc, taps = _lru_conv(lx, prev8, cw_ref, cb_ref[...])
        a_par = la_ref[...]
        sp_a = _softplus(-a_par)
        r, ig, a, mult = _lru_gates(xc, wr_ref, wi_ref, br_ref[...], bi_ref[...], sp_a)
        hl = hl_ref[...]
        h_prev = _shift_down(hl, hprev8, 1)
        ggate, dggate = _gelu_and_grad(gate)
        y_lru = hl * ggate
        n_l, r_l = _rms(y_lru)
        d_nl = dyc_ref[:, 0:LRU_W]
        vs_ref[6:7, :] += _colsum(d_nl * n_l)
        d_yl = _rms_bwd(d_nl * gl_ref[...], n_l, r_l)
        d_hl = d_yl * ggate
        d_gate = d_yl * hl * dggate
        a_up = _shift_up(a, nxt_a[...], 1)
        a_cum, b_cum = _scan_rev(a_up, d_hl)
        lam = b_cum + a_cum * nxt_lam[0:1, :]
        nxt_a[...] = jnp.broadcast_to(a[0:1, :], nxt_a.shape)
        nxt_lam[...] = jnp.broadcast_to(lam[0:1, :], nxt_lam.shape)
        ixc = ig * xc
        d_la = lam * h_prev * a - lam * ixc * (a * a) / mult
        d_i = lam * mult * xc
        d_xc = lam * mult * ig
        vs_ref[3:4, :] += _colsum(d_la * r) * (LRU_C * _sigmoid(-a_par))
        d_pr = d_la * (-LRU_C * sp_a) * r * (1.0 - r)
        d_pi = d_i * ig * (1.0 - ig)
        vs_ref[1:2, :] += _colsum(d_pr)
        vs_ref[2:3, :] += _colsum(d_pi)
        dwr_ref[...] += _dot_tn(xc, d_pr)
        dwi_ref[...] += _dot_tn(xc, d_pi)
        d_xc = d_xc + _dot_nt(d_pr, wr_ref[...]) + _dot_nt(d_pi, wi_ref[...])
        vs_ref[0:1, :] += _colsum(d_xc)
        nx = nxt_dxc[...]
        d_lx = cw_ref[LRU_CONV_K - 1:LRU_CONV_K, :] * d_xc
        dcw_ref[LRU_CONV_K - 1:LRU_CONV_K, :] += _colsum(d_xc * lx)
        for k in range(LRU_CONV_K - 1):
            d_lx = d_lx + cw_ref[k:k + 1, :] * _shift_up(d_xc, nx, LRU_CONV_K - 1 - k)
            dcw_ref[k:k + 1, :] += _colsum(d_xc * taps[k])
        nxt_dxc[...] = d_xc[0:SUBLANES]
        dz_ref[:, 0:LRU_W] = d_lx.astype(BF16)
        dz_ref[:, LRU_W:2 * LRU_W] = d_gate.astype(BF16)

        u, du = _gelu_and_grad(gu)
        v, vhat, rs, dav = _gmlp_v(gv, vg_ref[...], vb_ref[...])
        mask = _ws_mask()
        sp_parts = []
        for nb in range(nblk):
            rowp = []
            for g in range(N_GROUPS):
                wsm = jnp.where(mask, ws_ref[g], 0.0)
                vblk = v[nb * POS_BLOCK:(nb + 1) * POS_BLOCK, g * LANES:(g + 1) * LANES]
                rowp.append(_dot(wsm, vblk) + bst_ref[:, g:g + 1])
            sp_parts.append(jnp.concatenate(rowp, axis=1))
        sp = jnp.concatenate(sp_parts, axis=0) if nblk > 1 else sp_parts[0]
        y_g = u * sp
        n_g, r_g = _rms(y_g)
        d_ng = dyc_ref[:, LRU_W:]
        vs_ref[7:8, :] += _colsum(d_ng * n_g)
        d_yg = _rms_bwd(d_ng * gg_ref[...], n_g, r_g)
        d_gu = d_yg * sp * du
        d_sp = d_yg * u
        mask_t = _ws_mask(transposed=True)
        ones8 = jnp.ones((SUBLANES, LANES), F32)
        dv_parts = []
        for nb in range(nblk):
            rowp = []
            for g in range(N_GROUPS):
                rs_, cs_ = slice(nb * POS_BLOCK, (nb + 1) * POS_BLOCK), slice(g * LANES, (g + 1) * LANES)
                dsp_blk = d_sp[rs_, cs_]
                dbs_ref[g:g + 1, :] += lax.dot_general(
                    ones8, dsp_blk, (((1,), (1,)), ((), ())), preferred_element_type=F32,
                    precision=lax.Precision.HIGHEST)[0:1, :]
                dws_ref[g] += _dot_nt(dsp_blk, v[rs_, cs_])
                wsm_t = jnp.where(mask_t, wst_ref[g], 0.0)
                rowp.append(_dot(wsm_t, dsp_blk))
            dv_parts.append(jnp.concatenate(rowp, axis=1))
        d_v = jnp.concatenate(dv_parts, axis=0) if nblk > 1 else dv_parts[0]
        vs_ref[4:5, :] += _colsum(d_v * vhat)
        vs_ref[5:6, :] += _colsum(d_v)
        d_vh = d_v * vg_ref[...]
        d_av = rs * (d_vh - jnp.mean(d_vh, axis=-1, keepdims=True)
                     - vhat * jnp.mean(d_vh * vhat, axis=-1, keepdims=True))
        dz_ref[:, 2 * LRU_W:2 * LRU_W + GMLP_W] = d_gu.astype(BF16)
        dz_ref[:, 2 * LRU_W + GMLP_W:] = (d_av * dav).astype(BF16)

        @pl.when(i == nt - 1)
        def _():
            for hd in range(N_HEADS):
                blk = slice(hd * HEAD_DIM, (hd + 1) * HEAD_DIM)
                dwrb_ref[_head_pair_block(hd)] = dwr_ref[blk, blk]
                dwib_ref[_head_pair_block(hd)] = dwi_ref[blk, blk]
            for g in range(N_GROUPS):
                dws_ref[g] = jnp.where(mask, dws_ref[g], 0.0)

    rev = lambda c: pl.BlockSpec((tt, c), lambda i: (nt - 1 - i, 0))
    halo = pl.BlockSpec((SUBLANES, LRU_W), lambda i: (jnp.maximum((nt - 1 - i) * hb - 1, 0), 0))
    v512 = _const((1, LRU_W))
    return _call(
        body, "mix_bwd", (nt,),
        in_specs=[rev(LRU_W + GMLP_W), rev(IN_COLS), halo, rev(LRU_W), halo,
                  _const((LRU_CONV_K, LRU_W)), v512, _whole(), _whole(), v512, v512, v512, v512, v512,
                  _whole(), _whole(), _whole(), v512, v512],
        out_specs=[rev(IN_COLS), _const((SUBLANES, LRU_W)), _const((SUBLANES, LRU_W)),
                   _const((LRU_W // 2, 2 * HEAD_DIM)), _const((LRU_W // 2, 2 * HEAD_DIM)),
                   _const((N_GROUPS, POS_BLOCK, POS_BLOCK)), _const((SUBLANES, POS_BLOCK))],
        out_shape=[_sds((s_len, IN_COLS), BF16), _sds((SUBLANES, LRU_W), F32), _sds((SUBLANES, LRU_W), F32),
                   _sds((LRU_W // 2, 2 * HEAD_DIM), F32), _sds((LRU_W // 2, 2 * HEAD_DIM), F32),
                   _sds((N_GROUPS, POS_BLOCK, POS_BLOCK), F32), _sds((SUBLANES, POS_BLOCK), F32)],
        scratch=[pltpu.VMEM((SUBLANES, LRU_W), F32), pltpu.VMEM((SUBLANES, LRU_W), F32),
                 pltpu.VMEM((SUBLANES, LRU_W), F32), pltpu.VMEM((LRU_W, LRU_W), F32), pltpu.VMEM((LRU_W, LRU_W), F32)],
        args=(d_ycat, z, z, hl, hl, conv_w, conv_b, wr_bd, wi_bd, b_r, b_i, lru_a, vn_g, vn_b, w_sp, w_sp_t, b_sp_t,
              g_lru, g_gmlp), carry=carry)


def _in_bwd(d_z, w_in, x, d_x1, g, sc, carry=None):
    s_len = x.shape[0]
    tt = min(TT_BIG, s_len)

    def body(dz_ref, w_ref, x_ref, dx1_ref, g_ref, sc_ref, gx_ref, vs_ref):
        @pl.when(pl.program_id(0) == 0)
        def _():
            vs_ref[...] = jnp.zeros_like(vs_ref)

        d_h = _dot_nt(dz_ref[...], w_ref[...])
        n, r = _rms(x_ref[...])
        vs_ref[0:1, :] += _colsum(d_h)
        vs_ref[1:2, :] += _colsum(d_h * n * g_ref[...])
        d_ng = d_h * (1.0 + sc_ref[...])
        vs_ref[2:3, :] += _colsum(d_ng * n)
        gx_ref[...] = dx1_ref[...] + _rms_bwd(d_ng * g_ref[...], n, r)

    row = lambda c: pl.BlockSpec((tt, c), lambda i: (i, 0))
    vec = _const((1, D_MODEL))
    return _call(
        body, "in_bwd", (s_len // tt,),
        in_specs=[row(IN_COLS), _whole(), row(D_MODEL), row(D_MODEL), vec, vec],
        out_specs=[row(D_MODEL), _const((SUBLANES, D_MODEL))],
        out_shape=[_sds((s_len, D_MODEL), F32), _sds((SUBLANES, D_MODEL), F32)],
        scratch=[], args=(d_z, w_in, x, d_x1, g, sc), carry=carry)


def _wgrad(a, b, tn, name, carry=None):
    s_len, k_dim = a.shape
    halves = b.ndim == 3
    n_dim = b.shape[-1] * (2 if halves else 1)
    ts = min(TT_WG, s_len)
    nj = n_dim // tn
    nt = s_len // ts

    def body(a_ref, b_ref, o_ref, ob_ref):
        t = pl.program_id(1)
        part = _dot_tn(a_ref[...], b_ref[0] if halves else b_ref[...])

        @pl.when(t == 0)
        def _():
            o_ref[...] = part

        @pl.when(t > 0)
        def _():
            o_ref[...] += part

        @pl.when(t == nt - 1)
        def _():
            ob_ref[...] = o_ref[...].astype(BF16)

    if halves:
        per_half = nj // 2
        b_spec = pl.BlockSpec((1, ts, tn), lambda j, t: (j // per_half, t, j % per_half))
    else:
        b_spec = pl.BlockSpec((ts, tn), lambda j, t: (t, j))
    o_spec = pl.BlockSpec((k_dim, tn), lambda j, t: (0, j))
    return _call(
        body, name, (nj, nt),
        in_specs=[pl.BlockSpec((ts, k_dim), lambda j, t: (t, 0)), b_spec],
        out_specs=[o_spec, o_spec],
        out_shape=[_sds((k_dim, n_dim), F32), _sds((k_dim, n_dim), BF16)],
        scratch=[], args=(a, b), carry=carry)


def _adam_math(w, g, m, v):
    m = ADAM_B1 * m + (1.0 - ADAM_B1) * g
    v = ADAM_B2 * v + (1.0 - ADAM_B2) * (g * g)
    m_hat = m / (1.0 - ADAM_B1 ** ADAM_STEP)
    v_hat = v / (1.0 - ADAM_B2 ** ADAM_STEP)
    delta = -ADAM_LR * (m_hat / (jnp.sqrt(v_hat) + ADAM_EPS) + ADAM_WD * w)
    return delta, m, v


def _row_tile(rows, cols, n_f32_arrays):
    budget = VMEM_LIMIT // 2
    tr = rows
    while tr % 2 == 0 and tr // 2 >= SUBLANES and (tr // 2) % SUBLANES == 0 and tr * cols * 4 * n_f32_arrays * 2 > budget:
        tr //= 2
    return tr


def _adamw_sum(w, g_full, recv, m, v, col_sharded, name):
    _, rows, cols = w.shape
    n_recv = len(recv)
    tr = _row_tile(rows, cols, 10)
    nb = rows // tr

    def body(me_ref, w_ref, g_ref, *rest):
        r_refs = rest[:n_recv]
        m_ref, v_ref, go_ref, d_ref, mo_ref, vo_ref = rest[n_recv:]
        g = g_ref[...]
        for r_ref in r_refs:
            for k in range(r_ref.shape[0]):
                g = g + r_ref[k].astype(F32)
        go_ref[0] = g
        d_ref[0], mo_ref[0], vo_ref[0] = _adam_math(w_ref[0], g, m_ref[0], v_ref[0])

    if col_sharded:
        own = pl.BlockSpec((tr, cols), lambda i, me: (i, me[0]))
    else:
        own = pl.BlockSpec((tr, cols), lambda i, me: (me[0] * nb + i, 0))
    blk = pl.BlockSpec((1, tr, cols), lambda i, me: (0, i, 0))
    return pl.pallas_call(
        body, name=name,
        grid_spec=pltpu.PrefetchScalarGridSpec(
            num_scalar_prefetch=1, grid=(nb,),
            in_specs=[blk, own] + [pl.BlockSpec((r.shape[0], tr, cols), lambda i, me: (0, i, 0)) for r in recv]
            + [blk, blk],
            out_specs=[blk] * 4),
        out_shape=[_sds((1, rows, cols), F32)] * 4,
        compiler_params=_cparams(("arbitrary",)),
    )(jnp.reshape(_dev_index(_my_pos()), (1,)).astype(jnp.int32), w, g_full, *recv, m, v)


def _row_of_each(ref, row):
    cols = ref.shape[1]
    rows = _rows((N_DEV, cols))
    out = jnp.zeros((N_DEV, cols), F32)
    for d in range(N_DEV):
        picked = ref[d * SUBLANES + row:d * SUBLANES + row + 1, :]
        out = jnp.where(rows == d, jnp.broadcast_to(picked, (N_DEV, cols)), out)
    return out


def _my_columns(full, width, me):
    out = jnp.zeros(full.shape[:-1] + (width,), F32)
    for d in range(N_DEV):
        out = out + jnp.where(me == d, full[:, d * width:(d + 1) * width], 0.0)
    return out


def _adamw_wada(c_all, vs_in_all, vs_up_all, vs_ffn_all, w, m, v):
    _, rows, cols = w.shape

    def body(c_ref, vi_ref, vu_ref, vf_ref, w_ref, m_ref, v_ref, go_ref, d_ref, mo_ref, vo_ref):
        me = _dev_index(_my_pos())
        cv = _row_of_each(c_ref, 0)
        ca = cv * _sigmoid(cv)
        dmod = jnp.concatenate([_row_of_each(vi_ref, 0), _row_of_each(vi_ref, 1), _row_of_each(vu_ref, 3),
                                _row_of_each(vu_ref, 0), _row_of_each(vu_ref, 1), _row_of_each(vf_ref, 0)], axis=1)
        dm = _my_columns(dmod, cols, me)
        g = lax.dot_general(ca, dm, (((0,), (0,)), ((), ())), preferred_element_type=F32,
                            precision=lax.Precision.HIGHEST)
        go_ref[0] = g
        d_ref[0], mo_ref[0], vo_ref[0] = _adam_math(w_ref[0], g, m_ref[0], v_ref[0])

    return pl.pallas_call(
        body, name="adamw_w_ada", out_shape=[_sds((1, rows, cols), F32)] * 4,
        in_specs=[_whole()] * 7, out_specs=[_whole()] * 4,
        compiler_params=_cparams(),
    )(c_all, vs_in_all, vs_up_all, vs_ffn_all, w, m, v)


def _adamw_small(gathered, reduced, params, conv_params):
    names = list(params) + list(conv_params)
    allp = {**params, **conv_params}
    n_g = len(gathered) + len(reduced)

    def body(*refs):
        g_refs = refs[:n_g]
        p_refs = refs[n_g:n_g + 3 * len(names)]
        o_refs = refs[n_g + 3 * len(names):]
        me = _dev_index(_my_pos())

        def total(ref):
            s = ref[0:SUBLANES, :]
            for d in range(1, N_DEV):
                s = s + ref[d * SUBLANES:(d + 1) * SUBLANES, :]
            return s

        vs_in, vs_up, vs_ffn, loss = [total(r) for r in g_refs[:4]]
        cs, vs_mix, dcw, dwr, dwi, dws, dbs = [r[...] for r in g_refs[4:]]
        o_refs[-1][...] = loss[0:1, 0:1]
        mine = lambda full, width: _my_columns(full, width, me)

        all_ = (slice(None), slice(None))
        heads = lambda row: [((0, slice(h, h + 1), slice(None)), row[:, h * HEAD_DIM:(h + 1) * HEAD_DIM])
                             for h in range(N_HEADS)]
        blocks = lambda pairs: [((0, h), pairs[_head_pair_block(h)]) for h in range(N_HEADS)]
        pieces = {
            "b_ada": [((slice(None), slice(k * D_MODEL, (k + 1) * D_MODEL)), row) for k, row in enumerate(
                (vs_in[0:1], vs_in[1:2], vs_up[3:4], vs_up[0:1], vs_up[1:2], vs_ffn[0:1]))],
            "g_mix_pre": [(all_, vs_in[2:3])], "g_mix_post": [(all_, vs_up[4:5])],
            "g_ffn_pre": [(all_, vs_up[2:3])], "g_ffn_post": [(all_, vs_ffn[1:2])],
            "conv_b": [(all_, vs_mix[0:1])], "b_rgate": heads(vs_mix[1:2]), "b_igate": heads(vs_mix[2:3]),
            "lru_a": [(all_, vs_mix[3:4])], "v_norm_g": [(all_, vs_mix[4:5])], "v_norm_b": [(all_, vs_mix[5:6])],
            "g_lru_out": [(all_, vs_mix[6:7])], "g_gmlp_out": [(all_, vs_mix[7:8])],
            "w_rgate": blocks(dwr), "w_igate": blocks(dwi),
            "w_spatial": [((0, g), dws[g * POS_BLOCK:(g + 1) * POS_BLOCK, :]) for g in range(N_GROUPS)],
            "b_spatial": [((0,), dbs[0:N_GROUPS])],
            "ffn_conv_b": [(all_, cs[FFN_CONV_K:FFN_CONV_K + 1])],
            "conv_w": [((0,), mine(dcw[0:LRU_CONV_K], LRU_W // N_DEV))],
            "ffn_conv_w": [((0,), mine(cs[0:FFN_CONV_K], 2 * D_FF // N_DEV))],
        }
        for n_i, name in enumerate(names):
            w_ref, m_ref, v_ref = p_refs[3 * n_i:3 * n_i + 3]
            go_ref, d_ref, mo_ref, vo_ref = o_refs[4 * n_i:4 * n_i + 4]
            for idx, g in pieces[name]:
                go_ref[idx] = g
                d_ref[idx], mo_ref[idx], vo_ref[idx] = _adam_math(w_ref[idx], g, m_ref[idx], v_ref[idx])

    flat_params = [a for n in names for a in allp[n]]
    out_shape = [_sds(allp[n][0].shape, F32) for n in names for _ in range(4)] + [_sds((1, 1), F32)]
    outs = pl.pallas_call(
        body, name="adamw_small", out_shape=out_shape,
        in_specs=[_whole()] * (n_g + len(flat_params)), out_specs=[_whole()] * len(out_shape),
        compiler_params=_cparams(),
    )(*gathered, *reduced, *flat_params)
    return {n: outs[4 * i:4 * i + 4] for i, n in enumerate(names)}, outs[-1]


def _mod_part(c_all, w_ada, b_ada):
    cols = w_ada.shape[1]

    def body(c_ref, w_ref, b_ref, o_ref):
        cv = _row_of_each(c_ref, 0)
        ca = cv * _sigmoid(cv)
        b_cols = _my_columns(b_ref[...], cols, _dev_index(_my_pos()))
        o_ref[...] = jnp.dot(ca, w_ref[...], preferred_element_type=F32, precision=lax.Precision.HIGHEST) + b_cols

    return pl.pallas_call(
        body, name="mod_part", out_shape=_sds((N_DEV, cols), F32),
        in_specs=[_whole()] * 3, out_specs=_whole(), compiler_params=_cparams(),
    )(c_all, w_ada, b_ada)


def _my_pos():
    return lax.axis_index("x"), lax.axis_index("y"), lax.axis_index("c")


def _flip(pos, k):
    x, y, c = pos
    return (1 - x if k & 4 else x, 1 - y if k & 2 else y, 1 - c if k & 1 else c)


def _dev_index(pos):
    x, y, c = pos
    return 4 * x + 2 * y + c


def _gather_multi(arrays, name, carry=None):
    n = len(arrays)

    def body(*refs):
        ins, outs = refs[:n], refs[n:2 * n]
        send_sems, recv_sems = refs[2 * n:]
        me = _my_pos()

        def slot(a, pos):
            rows = ins[a].shape[0]
            return outs[a].at[pl.ds(pl.multiple_of(_dev_index(pos) * rows, SUBLANES), rows), :]

        def copy(a, k):
            return pltpu.make_async_remote_copy(
                src_ref=ins[a], dst_ref=slot(a, me), send_sem=send_sems.at[a, k - 1], recv_sem=recv_sems.at[a, k - 1],
                device_id=_flip(me, k), device_id_type=MESH)

        sends = [copy(a, k) for a in range(n) for k in range(1, N_DEV)]
        for cp in sends:
            cp.start()
        for a in range(n):
            rows = ins[a].shape[0]
            outs[a][pl.ds(pl.multiple_of(_dev_index(me) * rows, SUBLANES), rows), :] = ins[a][...]
        for a in range(n):
            for k in range(1, N_DEV):
                pltpu.make_async_remote_copy(
                    src_ref=ins[a], dst_ref=slot(a, _flip(me, k)), send_sem=send_sems.at[a, k - 1],
                    recv_sem=recv_sems.at[a, k - 1], device_id=_flip(me, k), device_id_type=MESH).wait_recv()
        for cp in sends:
            cp.wait_send()

    return _call(
        body, name, (1,), in_specs=[_whole()] * n, out_specs=[_whole()] * n,
        out_shape=[_sds((N_DEV * a.shape[0], a.shape[1]), F32) for a in arrays],
        scratch=[pltpu.SemaphoreType.DMA((n, N_DEV - 1)), pltpu.SemaphoreType.DMA((n, N_DEV - 1))],
        args=tuple(arrays), carry=carry)


def _reduce_small(gath, red):
    n_g, n_r = len(gath), len(red)
    chip_flips = (4, 2, 6)

    def body(*refs):
        g_in, r_in = refs[:n_g], refs[n_g:n_g + n_r]
        g_out, r_out = refs[n_g + n_r:2 * n_g + n_r], refs[2 * n_g + n_r:2 * (n_g + n_r)]
        scr = refs[2 * (n_g + n_r):]
        sib, land = scr[:n_r], scr[n_r:2 * n_r]
        g_send, g_recv, s_send, s_recv, i_send, i_recv, f_send, f_recv = scr[2 * n_r:]
        me = _my_pos()
        c = me[2]
        sibling = _flip(me, 1)

        def slot(a, pos):
            return g_out[a].at[pl.ds(pl.multiple_of(_dev_index(pos) * SUBLANES, SUBLANES), SUBLANES), :]

        def gcopy(a, k):
            return pltpu.make_async_remote_copy(
                src_ref=g_in[a], dst_ref=slot(a, me), send_sem=g_send.at[a, k - 1], recv_sem=g_recv.at[a, k - 1],
                device_id=_flip(me, k), device_id_type=MESH)

        def scopy(a):
            return pltpu.make_async_remote_copy(
                src_ref=r_in[a], dst_ref=sib[a], send_sem=s_send.at[a], recv_sem=s_recv.at[a],
                device_id=sibling, device_id_type=MESH)

        def icopy(a, j):
            return pltpu.make_async_remote_copy(
                src_ref=r_out[a], dst_ref=land[a].at[j], send_sem=i_send.at[a, j], recv_sem=i_recv.at[a, j],
                device_id=_flip(me, chip_flips[j]), device_id_type=MESH)

        def fcopy(a, j):
            return pltpu.make_async_remote_copy(
                src_ref=land[a].at[j], dst_ref=land[a].at[j], send_sem=f_send.at[a, j], recv_sem=f_recv.at[a, j],
                device_id=sibling, device_id_type=MESH)

        gathers = [gcopy(a, k) for a in range(n_g) for k in range(1, N_DEV)]
        swaps = [scopy(a) for a in range(n_r)]
        for cp in gathers + swaps:
            cp.start()
        for a in range(n_g):
            g_out[a][pl.ds(pl.multiple_of(_dev_index(me) * SUBLANES, SUBLANES), SUBLANES), :] = g_in[a][...]
        for a in range(n_r):
            swaps[a].wait_recv()
            r_out[a][...] = r_in[a][...] + sib[a][...]

        for core in range(2):
            mine = [a for a in range(n_r) if a % 2 == core]
            theirs = [a for a in range(n_r) if a % 2 != core]

            @pl.when(c == core)
            def _():
                out = [icopy(a, j) for a in mine for j in range(3)]
                for cp in out:
                    cp.start()
                fwd = []
                for a in mine:
                    for j in range(3):
                        icopy(a, j).wait_recv()
                        cp = fcopy(a, j)
                        cp.start()
                        fwd.append(cp)
                for a in theirs:
                    for j in range(3):
                        fcopy(a, j).wait_recv()
                for cp in out + fwd:
                    cp.wait_send()

        for a in range(n_r):
            r_out[a][...] = (r_out[a][...] + land[a][1]) + (land[a][0] + land[a][2])
        for a in range(n_g):
            for k in range(1, N_DEV):
                pltpu.make_async_remote_copy(
                    src_ref=g_in[a], dst_ref=slot(a, _flip(me, k)), send_sem=g_send.at[a, k - 1],
                    recv_sem=g_recv.at[a, k - 1], device_id=_flip(me, k), device_id_type=MESH).wait_recv()
        for cp in gathers + swaps:
            cp.wait_send()

    shapes = [tuple(a.shape) for a in red]
    outs = pl.pallas_call(
        body, name="reduce_small",
        out_shape=[_sds((N_DEV * SUBLANES, a.shape[1]), F32) for a in gath] + [_sds(s, F32) for s in shapes],
        in_specs=[_whole()] * (n_g + n_r), out_specs=[_whole()] * (n_g + n_r),
        scratch_shapes=[pltpu.VMEM(s, F32) for s in shapes] + [pltpu.VMEM((3,) + s, F32) for s in shapes]
        + [pltpu.SemaphoreType.DMA((n_g, N_DEV - 1)), pltpu.SemaphoreType.DMA((n_g, N_DEV - 1)),
           pltpu.SemaphoreType.DMA((n_r,)), pltpu.SemaphoreType.DMA((n_r,)),
           pltpu.SemaphoreType.DMA((n_r, 3)), pltpu.SemaphoreType.DMA((n_r, 3)),
           pltpu.SemaphoreType.DMA((n_r, 3)), pltpu.SemaphoreType.DMA((n_r, 3))],
        compiler_params=pltpu.CompilerParams(vmem_limit_bytes=VMEM_LIMIT),
    )(*gath, *red)
    return outs[:n_g], outs[n_g:]


def _region(ref, shard_shape, col_sharded, pos):
    r, cdim = shard_shape
    d = _dev_index(pos)
    if col_sharded:
        return ref.at[:, pl.ds(pl.multiple_of(d * cdim, LANES), cdim)]
    return ref.at[pl.ds(pl.multiple_of(d * r, 2 * SUBLANES), r), :]


def _gather_carry(shards, col_sharded):
    n_w = len(shards)
    shapes = [tuple(s.shape) for s in shards]
    full_shapes = [(s[0], s[1] * N_DEV) if cs else (s[0] * N_DEV, s[1]) for s, cs in zip(shapes, col_sharded)]

    def tools(out_refs, scr):
        send_sems, recv_sems = scr[n_w], scr[n_w + 1]
        me = _my_pos()
        x, y, c = me
        sibling = (x, y, 1 - c)
        chips = [(1 - x, y), (x, 1 - y), (1 - x, 1 - y)]

        def region(w, pos):
            return _region(out_refs[w], shapes[w], col_sharded[w], pos)

        def copy(w, k, block, to, src=None):
            return pltpu.make_async_remote_copy(
                src_ref=region(w, block) if src is None else src, dst_ref=region(w, block),
                send_sem=send_sems.at[w, k], recv_sem=recv_sems.at[w, k], device_id=to, device_id_type=MESH)

        def first(w):
            return [copy(w, 0, me, sibling, src=scr[w])] + [
                copy(w, 1 + j, me, (*chip, c), src=scr[w]) for j, chip in enumerate(chips)]

        def mine(w):
            return pltpu.make_async_copy(scr[w], region(w, me), scr[n_w + 2].at[w])

        return me, c, sibling, chips, copy, first, mine

    def start(ins, outs, scr):
        _, _, _, _, _, first, mine = tools(outs, scr)
        for w in range(n_w):
            scr[w][...] = ins[w][...].astype(BF16)
            for cp in first(w) + [mine(w)]:
                cp.start()

    def finish(ins, outs, scr):
        me, c, sibling, chips, copy, first, mine = tools(outs, scr)
        passed = []
        for w in range(n_w):
            for j, chip in enumerate(chips):
                copy(w, 1 + j, (*chip, c), me).wait_recv()
                fwd = copy(w, 4 + j, (*chip, c), sibling)
                fwd.start()
                passed.append(fwd)
        for w in range(n_w):
            copy(w, 0, sibling, me).wait_recv()
            for j, chip in enumerate(chips):
                copy(w, 4 + j, (*chip, 1 - c), me).wait_recv()
        for w in range(n_w):
            for cp in first(w):
                cp.wait_send()
            mine(w).wait()
        for cp in passed:
            cp.wait_send()

    return _Carry(
        inputs=list(shards), in_specs=[_whole()] * n_w,
        out_shape=[_sds(s, BF16) for s in full_shapes], out_specs=[_any()] * n_w,
        scratch=[pltpu.VMEM(s, BF16) for s in shapes]
        + [pltpu.SemaphoreType.DMA((n_w, N_DEV - 1)), pltpu.SemaphoreType.DMA((n_w, N_DEV - 1)),
           pltpu.SemaphoreType.DMA((n_w,))],
        start=start, finish=finish)


def _scatter_carry(grads_bf, shard_shapes, col_sharded, relations):
    n_w = len(grads_bf)
    shapes = [tuple(s) for s in shard_shapes]

    def copies(ins, outs, scr):
        send_sems, recv_sems = scr
        me = _my_pos()
        out = []
        for w in range(n_w):
            for i, k in enumerate(relations[w]):
                peer = _flip(me, k)
                out.append(pltpu.make_async_remote_copy(
                    src_ref=_region(ins[w], shapes[w], col_sharded[w], peer), dst_ref=outs[w].at[i],
                    send_sem=send_sems.at[w, i], recv_sem=recv_sems.at[w, i],
                    device_id=peer, device_id_type=MESH))
        return out

    def start(ins, outs, scr):
        for cp in copies(ins, outs, scr):
            cp.start()

    def finish(ins, outs, scr):
        cps = copies(ins, outs, scr)
        for cp in cps:
            cp.wait_recv()
        for cp in cps:
            cp.wait_send()

    return _Carry(
        inputs=list(grads_bf), in_specs=[_any()] * n_w,
        out_shape=[_sds((len(r),) + s, BF16) for r, s in zip(relations, shapes)], out_specs=[_any()] * n_w,
        scratch=[pltpu.SemaphoreType.DMA((n_w, N_DEV - 1)), pltpu.SemaphoreType.DMA((n_w, N_DEV - 1))],
        start=start, finish=finish)


def _block_diag(w):
    eye = jnp.eye(N_HEADS, dtype=w.dtype)
    return (eye[:, None, :, None] * w[:, :, None, :]).reshape(N_HEADS * HEAD_DIM, N_HEADS * HEAD_DIM)


def _pad_rows(a):
    return jnp.pad(a, ((0, SUBLANES - a.shape[0]), (0, 0)))


def _columns_from_devices(gathered, rows):
    w = gathered.shape[1]
    return gathered.reshape(N_DEV, SUBLANES, w)[:, :rows].transpose(1, 0, 2).reshape(rows, N_DEV * w)


def _local_step(x2, target, mod, w_in_f, w_full, conv_w_full, ffn_cw_full,
                g_mix_pre, g_mix_post, conv_b, w_rgate, b_rgate, w_igate, b_igate, lru_a, v_norm_g, v_norm_b,
                w_spatial, b_spatial, g_lru_out, g_gmlp_out, g_ffn_pre, g_ffn_post, ffn_conv_b,
                gather=None, scatter=None):
    sh_m, sc_m, gt_m, sh_f, sc_f, gt_f = [mod[k] for k in range(N_MOD)]
    wr_bd = _block_diag(w_rgate[0]).astype(BF16)
    wi_bd = _block_diag(w_igate[0]).astype(BF16)
    b_r = b_rgate.reshape(1, LRU_W)
    b_i = b_igate.reshape(1, LRU_W)
    b_sp_t = b_spatial[0].T
    w_sp_t = jnp.swapaxes(w_spatial[0], 1, 2)

    def arriving(name):
        return gather(name) if gather else None

    near, far = (1, 2, 3, 4, 5), (6, 7)

    def leaving(*parts):
        return scatter(parts) if scatter else None

    def received(recv, parts, outs):
        for (name, _, _), out in zip(parts, outs):
            recv.setdefault(name, []).append(out)

    def landed(name, carried):
        return carried[0] if gather else w_full[name]

    mix_params = (conv_w_full, conv_b, wr_bd, wi_bd, b_r, b_i, lru_a, v_norm_g, v_norm_b)
    (z, h, ycat, hl), got = _mix_fwd(x2, sh_m, sc_m, g_mix_pre, w_in_f, *mix_params, w_spatial[0], b_sp_t,
                                     g_lru_out, g_gmlp_out, carry=arriving("w_up"))
    w_up_f = landed("w_up", got)
    w_out_f = w_full["w_out"]
    (y, x1, h2, up_pre), got = _out_up_fwd(ycat, x2, w_out_f, g_mix_post, gt_m, g_ffn_pre, sc_f, sh_f, w_up_f,
                                           carry=arriving("w_down"))
    w_down_f = landed("w_down", got)
    act, d_y2, dout, loss_acc, up, vs_ffn = _ffn_fwd(up_pre, ffn_cw_full, ffn_conv_b, w_down_f, x1, gt_f, g_ffn_post,
                                                      target)

    recv = {}
    gw_down, _ = _wgrad(act, d_y2, D_MODEL // 2, "wgrad_down")
    parts = [("w_down", gw_down[1], near + far)]
    (d_up, cs_ffn), got = _ffn_bwd(d_y2, up_pre, up, ffn_cw_full, w_down_f, carry=leaving(*parts))
    received(recv, parts, got)
    gw_up, _ = _wgrad(h2, d_up, D_FF // 2, "wgrad_up")
    parts = [("w_up", gw_up[1], near)]
    (d_x1, d_y, d_ycat, vs_up), got = _up_bwd(
        d_up, w_up_f, x1, dout, y, w_out_f, g_ffn_pre, sc_f, g_mix_post, gt_m, carry=leaving(*parts))
    received(recv, parts, got)
    gw_out, _ = _wgrad(ycat, d_y, D_MODEL, "wgrad_out")
    parts = [("w_up", gw_up[1], far), ("w_out", gw_out[1], near + far)]
    (d_z, vs_mix, dcw, d_wr, d_wi, d_ws, d_bs), got = _mix_bwd(
        d_ycat, z, hl, *mix_params, w_spatial[0], w_sp_t, b_sp_t, g_lru_out, g_gmlp_out, carry=leaving(*parts))
    received(recv, parts, got)
    gw_in, _ = _wgrad(h, d_z, IN_COLS // 2, "wgrad_in")
    parts = [("w_in", gw_in[1], near + far)]
    (grad_x, vs_in), got = _in_bwd(d_z, w_in_f, x2, d_x1, g_mix_pre, sc_m, carry=leaving(*parts))
    received(recv, parts, got)

    gath = [vs_in, vs_up, vs_ffn, loss_acc]
    red = [cs_ffn, vs_mix, dcw, d_wr, d_wi, d_ws.reshape(N_GROUPS * POS_BLOCK, POS_BLOCK), d_bs]
    return dict(grad_x=grad_x, gath=gath, red=red, recv=recv, w_in=gw_in, w_out=gw_out, w_up=gw_up, w_down=gw_down)


def kernel(x, c, w_ada, b_ada, g_mix_pre, g_mix_post, w_in, conv_w, conv_b, w_rgate, b_rgate, w_igate, b_igate, lru_a, v_norm_g, v_norm_b, w_spatial, b_spatial, g_lru_out, g_gmlp_out, w_out, g_ffn_pre, g_ffn_post, w_up, ffn_conv_w, ffn_conv_b, w_down, loss_target, m_w_ada, m_b_ada, m_g_mix_pre, m_g_mix_post, m_w_in, m_conv_w, m_conv_b, m_w_rgate, m_b_rgate, m_w_igate, m_b_igate, m_lru_a, m_v_norm_g, m_v_norm_b, m_w_spatial, m_b_spatial, m_g_lru_out, m_g_gmlp_out, m_w_out, m_g_ffn_pre, m_g_ffn_post, m_w_up, m_ffn_conv_w, m_ffn_conv_b, m_w_down, v_w_ada, v_b_ada, v_g_mix_pre, v_g_mix_post, v_w_in, v_conv_w, v_conv_b, v_w_rgate, v_b_rgate, v_w_igate, v_b_igate, v_lru_a, v_v_norm_g, v_v_norm_b, v_w_spatial, v_b_spatial, v_g_lru_out, v_g_gmlp_out, v_w_out, v_g_ffn_pre, v_g_ffn_post, v_w_up, v_ffn_conv_w, v_ffn_conv_b, v_w_down):
    me = _dev_index(_my_pos())
    ada_cols = w_ada.shape[-1]

    big_w = dict(w_in=(w_in, m_w_in, v_w_in, True), w_out=(w_out, m_w_out, v_w_out, False),
                 w_up=(w_up, m_w_up, v_w_up, True), w_down=(w_down, m_w_down, v_w_down, False))

    def gather(name):
        return _gather_carry([big_w[name][0][0]], [big_w[name][3]])

    def scatter(parts):
        return _scatter_carry([g for _, g, _ in parts], [big_w[n][0].shape[1:] for n, _, _ in parts],
                              [big_w[n][3] for n, _, _ in parts], [rel for _, _, rel in parts])

    (c_all, cw_all, fcw_all), (w_in_f,) = _gather_multi(
        [jnp.broadcast_to(c, (SUBLANES, D_MODEL)), _pad_rows(conv_w[0]), _pad_rows(ffn_conv_w[0])], "gather_start",
        carry=gather("w_in"))
    conv_w_full = _columns_from_devices(cw_all, LRU_CONV_K)
    ffn_cw_full = _columns_from_devices(fcw_all, FFN_CONV_K)
    mod_mine = _mod_part(c_all, w_ada[0], b_ada)
    (mod_all,), (w_out_f,) = _gather_multi([mod_mine], "gather_mod", carry=gather("w_out"))
    mod = lax.dynamic_index_in_dim(mod_all.reshape(N_DEV, N_DEV, ada_cols), me, axis=1, keepdims=False)
    mod = mod.reshape(N_MOD, 1, D_MODEL)

    loc = _local_step(x[0], loss_target[0], mod, w_in_f, dict(w_out=w_out_f), conv_w_full, ffn_cw_full,
                      g_mix_pre, g_mix_post, conv_b, w_rgate, b_rgate, w_igate, b_igate, lru_a, v_norm_g, v_norm_b,
                      w_spatial, b_spatial, g_lru_out, g_gmlp_out, g_ffn_pre, g_ffn_post, ffn_conv_b,
                      gather=gather, scatter=scatter)
    grad_x = loc["grad_x"]

    results = {}
    for name, (w_, m_, v_, cs) in big_w.items():
        results[name] = _adamw_sum(w_, loc[name][0], loc["recv"][name], m_, v_, cs, "adamw_" + name)

    gathered, reduced = _reduce_small(loc["gath"], loc["red"])
    params = dict(
        b_ada=(b_ada, m_b_ada, v_b_ada), g_mix_pre=(g_mix_pre, m_g_mix_pre, v_g_mix_pre),
        g_mix_post=(g_mix_post, m_g_mix_post, v_g_mix_post), conv_b=(conv_b, m_conv_b, v_conv_b),
        w_rgate=(w_rgate, m_w_rgate, v_w_rgate), b_rgate=(b_rgate, m_b_rgate, v_b_rgate),
        w_igate=(w_igate, m_w_igate, v_w_igate), b_igate=(b_igate, m_b_igate, v_b_igate),
        lru_a=(lru_a, m_lru_a, v_lru_a), v_norm_g=(v_norm_g, m_v_norm_g, v_v_norm_g),
        v_norm_b=(v_norm_b, m_v_norm_b, v_v_norm_b), w_spatial=(w_spatial, m_w_spatial, v_w_spatial),
        b_spatial=(b_spatial, m_b_spatial, v_b_spatial), g_lru_out=(g_lru_out, m_g_lru_out, v_g_lru_out),
        g_gmlp_out=(g_gmlp_out, m_g_gmlp_out, v_g_gmlp_out), g_ffn_pre=(g_ffn_pre, m_g_ffn_pre, v_g_ffn_pre),
        g_ffn_post=(g_ffn_post, m_g_ffn_post, v_g_ffn_post), ffn_conv_b=(ffn_conv_b, m_ffn_conv_b, v_ffn_conv_b))
    conv_params = dict(conv_w=(conv_w, m_conv_w, v_conv_w), ffn_conv_w=(ffn_conv_w, m_ffn_conv_w, v_ffn_conv_w))
    small_results, loss = _adamw_small(gathered, reduced, params, conv_params)
    results.update(small_results)
    loss = loss.reshape(())

    results["w_ada"] = _adamw_wada(c_all, gathered[0], gathered[1], gathered[2], w_ada, m_w_ada, v_w_ada)

    order = ["w_ada", "b_ada", "g_mix_pre", "g_mix_post", "w_in", "conv_w", "conv_b", "w_rgate", "b_rgate", "w_igate",
             "b_igate", "lru_a", "v_norm_g", "v_norm_b", "w_spatial", "b_spatial", "g_lru_out", "g_gmlp_out", "w_out",
             "g_ffn_pre", "g_ffn_post", "w_up", "ffn_conv_w", "ffn_conv_b", "w_down"]
    outs = [loss, grad_x[None]]
    for kind in range(4):
        outs += [results[n][kind] for n in order]
    return tuple(outs)
```

```python
import functools

import jax
import jax.numpy as jnp
from jax import lax
from jax.experimental import pallas as pl
from jax.experimental.pallas import tpu as pltpu

F32 = jnp.float32
BF16 = jnp.bfloat16

D_MODEL = 1024
LRU_W = 512
GMLP_W = 512
N_HEADS = 8
HEAD_DIM = 64
N_GROUPS = 4
POS_BLOCK = 128
CHUNK = 64
IN_COLS = 2048
D_FF = 3072
N_MOD = 6
N_DEV = 8
EPS = 1e-6
LRU_C = 8.0
LRU_CONV_K = 4
FFN_CONV_K = 3

ADAM_LR = 0.001
ADAM_B1 = 0.9
ADAM_B2 = 0.999
ADAM_EPS = 1e-08
ADAM_WD = 0.01
ADAM_STEP = 10

LANES = 128
SUBLANES = 8
TT_BIG = 512
TT_MIX = 256
TT_WG = 1024
FF_CW = 512
VMEM_LIMIT = 56 * 1024 * 1024

MESH = pl.DeviceIdType.MESH


def _sds(shape, dtype):
    return jax.ShapeDtypeStruct(shape, dtype)


def _cparams(sem=None):
    return pltpu.CompilerParams(dimension_semantics=sem, vmem_limit_bytes=VMEM_LIMIT)


def _whole():
    return pl.BlockSpec(memory_space=pltpu.VMEM)


def _const(shape):
    nd = len(shape)
    return pl.BlockSpec(shape, lambda *_: (0,) * nd)


def _any():
    return pl.BlockSpec(memory_space=pl.ANY)


class _Carry:
    def __init__(self, inputs, in_specs, out_shape, out_specs, scratch, start, finish):
        self.inputs, self.in_specs, self.out_shape, self.out_specs = inputs, in_specs, out_shape, out_specs
        self.scratch, self.start, self.finish = scratch, start, finish


def _call(body, name, grid, in_specs, out_specs, out_shape, scratch, args, carry=None):
    n_in, n_out, n_scr = len(in_specs), len(out_specs), len(scratch)
    c_in = len(carry.in_specs) if carry else 0
    c_out = len(carry.out_specs) if carry else 0

    def full_body(*refs):
        ins = refs[:n_in]
        c_ins = refs[n_in:n_in + c_in]
        outs = refs[n_in + c_in:n_in + c_in + n_out]
        c_outs = refs[n_in + c_in + n_out:n_in + c_in + n_out + c_out]
        scr = refs[n_in + c_in + n_out + c_out:n_in + c_in + n_out + c_out + n_scr]
        c_scr = refs[n_in + c_in + n_out + c_out + n_scr:]
        if carry:
            first = functools.reduce(lambda a, b: a & b, [pl.program_id(d) == 0 for d in range(len(grid))])
            last = functools.reduce(lambda a, b: a & b, [pl.program_id(d) == g - 1 for d, g in enumerate(grid)])

            @pl.when(first)
            def _():
                carry.start(c_ins, c_outs, c_scr)

        body(*ins, *outs, *scr)
        if carry:
            @pl.when(last)
            def _():
                carry.finish(c_ins, c_outs, c_scr)

    res = pl.pallas_call(
        full_body, name=name, grid=grid,
        in_specs=list(in_specs) + (list(carry.in_specs) if carry else []),
        out_specs=list(out_specs) + (list(carry.out_specs) if carry else []),
        out_shape=list(out_shape) + (list(carry.out_shape) if carry else []),
        scratch_shapes=list(scratch) + (list(carry.scratch) if carry else []),
        compiler_params=_cparams(("arbitrary",) * len(grid)),
    )(*args, *(carry.inputs if carry else []))
    return res[:n_out], res[n_out:]


def _gelu(x):
    u = 0.7978845608028654 * (x + 0.044715 * x * x * x)
    return 0.5 * x * (1.0 + jnp.tanh(u))


def _gelu_and_grad(x):
    x2 = x * x
    u = 0.7978845608028654 * (x + 0.044715 * x * x2)
    t = jnp.tanh(u)
    g = 0.5 * x * (1.0 + t)
    dg = 0.5 * (1.0 + t) + 0.5 * x * (1.0 - t * t) * 0.7978845608028654 * (1.0 + 3.0 * 0.044715 * x2)
    return g, dg


def _sigmoid(x):
    return 1.0 / (1.0 + jnp.exp(-x))


def _softplus(x):
    return jnp.maximum(x, 0.0) + jnp.log1p(jnp.exp(-jnp.abs(x)))


def _neg_expm1(x):
    series = -x * (1.0 + x * (0.5 + x * (1.0 / 6.0 + x * (1.0 / 24.0 + x * (1.0 / 120.0)))))
    return jnp.where(x > -0.1, series, 1.0 - jnp.exp(x))


def _dot(a, b):
    return jnp.dot(a.astype(BF16), b.astype(BF16), preferred_element_type=F32)


def _dot_nt(a, b):
    return lax.dot_general(a.astype(BF16), b.astype(BF16), (((1,), (1,)), ((), ())), preferred_element_type=F32)


def _dot_tn(a, b):
    return lax.dot_general(a.astype(BF16), b.astype(BF16), (((0,), (0,)), ((), ())), preferred_element_type=F32)


def _rows(shape):
    return lax.broadcasted_iota(jnp.int32, shape, 0)


def _shift_down(cur, prev8, s):
    if s == 0:
        return cur
    n = cur.shape[0]
    r = pltpu.roll(cur, s, 0)
    p = pltpu.roll(prev8, s, 0)
    top = jnp.where(_rows(p.shape) < s, p, r[0:SUBLANES])
    if n == SUBLANES:
        return top
    return jnp.concatenate([top, r[SUBLANES:]], axis=0)


def _shift_up(cur, next8, s):
    if s == 0:
        return cur
    n = cur.shape[0]
    r = pltpu.roll(cur, n - s, 0)
    q = pltpu.roll(next8, SUBLANES - s, 0)
    bot = jnp.where(_rows(q.shape) >= SUBLANES - s, q, r[n - SUBLANES:])
    if n == SUBLANES:
        return bot
    return jnp.concatenate([r[:n - SUBLANES], bot], axis=0)


def _scan_fwd(a, b):
    n = a.shape[0]
    rows = _rows(a.shape)
    s = 1
    while s < n:
        a_s = pltpu.roll(a, s, 0)
        b_s = pltpu.roll(b, s, 0)
        m = rows >= s
        b = jnp.where(m, a * b_s + b, b)
        a = jnp.where(m, a * a_s, a)
        s *= 2
    return a, b


def _scan_rev(a, b):
    n = a.shape[0]
    rows = _rows(a.shape)
    s = 1
    while s < n:
        a_s = pltpu.roll(a, n - s, 0)
        b_s = pltpu.roll(b, n - s, 0)
        m = rows < n - s
        b = jnp.where(m, b + a * b_s, b)
        a = jnp.where(m, a * a_s, a)
        s *= 2
    return a, b


def _rms(x):
    r = lax.rsqrt(jnp.mean(x * x, axis=-1, keepdims=True) + EPS)
    return x * r, r


def _rms_bwd(d_n, n, r):
    return r * (d_n - n * jnp.mean(d_n * n, axis=-1, keepdims=True))


def _colsum(x):
    return jnp.sum(x, axis=0, keepdims=True)


def _lru_gates(xc, wr_ref, wi_ref, br, bi, sp_a):
    r = _sigmoid(_dot(xc, wr_ref[...]) + br)
    i = _sigmoid(_dot(xc, wi_ref[...]) + bi)
    la = -LRU_C * r * sp_a
    a = jnp.exp(la)
    mult = jnp.sqrt(_neg_expm1(2.0 * la))
    return r, i, a, mult


def _lru_conv(lx, prev8, cw_ref, cb):
    xc = cb + cw_ref[LRU_CONV_K - 1:LRU_CONV_K, :] * lx
    taps = []
    for k in range(LRU_CONV_K - 1):
        tap = _shift_down(lx, prev8, LRU_CONV_K - 1 - k)
        taps.append(tap)
        xc = xc + cw_ref[k:k + 1, :] * tap
    return xc, taps


def _ws_mask(transposed=False):
    i = lax.broadcasted_iota(jnp.int32, (POS_BLOCK, POS_BLOCK), 0)
    j = lax.broadcasted_iota(jnp.int32, (POS_BLOCK, POS_BLOCK), 1)
    if transposed:
        i, j = j, i
    return (j // CHUNK) <= (i // CHUNK)


def _gmlp_v(gv, vg, vb):
    av, dav = _gelu_and_grad(gv)
    mu = jnp.mean(av, axis=-1, keepdims=True)
    cen = av - mu
    rs = lax.rsqrt(jnp.mean(cen * cen, axis=-1, keepdims=True) + EPS)
    vhat = cen * rs
    return vhat * vg + vb, vhat, rs, dav


def _mix_fwd(x, sh, sc, g_pre, w_in, conv_w, conv_b, wr_bd, wi_bd, b_r, b_i, lru_a, vn_g, vn_b, w_sp, b_sp_t,
             g_lru, g_gmlp, carry=None):
    s_len = x.shape[0]
    tt = min(TT_MIX, s_len)
    nblk = tt // POS_BLOCK

    def body(x_ref, sh_ref, sc_ref, g_ref, w_ref, cw_ref, cb_ref, wr_ref, wi_ref, br_ref, bi_ref, la_ref, vg_ref,
             vb_ref, ws_ref, bst_ref, gl_ref, gg_ref, z_ref, h_ref, y_ref, hl_ref, prev8, hcar):
        i = pl.program_id(0)

        @pl.when(i == 0)
        def _():
            prev8[...] = jnp.zeros_like(prev8)
            hcar[...] = jnp.zeros_like(hcar)

        n_x, _ = _rms(x_ref[...])
        h = (n_x * g_ref[...] * (1.0 + sc_ref[...]) + sh_ref[...]).astype(BF16)
        h_ref[...] = h
        z_ref[...] = jnp.dot(h, w_ref[...], preferred_element_type=F32)

        lx = z_ref[:, 0:LRU_W]
        gate = z_ref[:, LRU_W:2 * LRU_W]
        gu = z_ref[:, 2 * LRU_W:2 * LRU_W + GMLP_W]
        gv = z_ref[:, 2 * LRU_W + GMLP_W:]

        xc, _ = _lru_conv(lx, prev8[...], cw_ref, cb_ref[...])
        prev8[...] = lx[tt - SUBLANES:]
        sp_a = _softplus(-la_ref[...])
        _, ig, a, mult = _lru_gates(xc, wr_ref, wi_ref, br_ref[...], bi_ref[...], sp_a)
        bx = mult * (ig * xc)
        a_cum, b_cum = _scan_fwd(a, bx)
        hl = a_cum * hcar[0:1, :] + b_cum
        hcar[...] = jnp.broadcast_to(hl[tt - 1:tt, :], hcar.shape)
        hl_ref[...] = hl
        y_lru = hl * _gelu(gate)
        n_l, _ = _rms(y_lru)
        y_ref[:, 0:LRU_W] = (n_l * gl_ref[...]).astype(BF16)

        u = _gelu(gu)
        v, _, _, _ = _gmlp_v(gv, vg_ref[...], vb_ref[...])
        mask = _ws_mask()
        sp_parts = []
        for nb in range(nblk):
            row = []
            for g in range(N_GROUPS):
                wsm = jnp.where(mask, ws_ref[g], 0.0)
                vblk = v[nb * POS_BLOCK:(nb + 1) * POS_BLOCK, g * LANES:(g + 1) * LANES]
                row.append(_dot(wsm, vblk) + bst_ref[:, g:g + 1])
            sp_parts.append(jnp.concatenate(row, axis=1))
        sp = jnp.concatenate(sp_parts, axis=0) if nblk > 1 else sp_parts[0]
        n_g, _ = _rms(u * sp)
        y_ref[:, LRU_W:] = (n_g * gg_ref[...]).astype(BF16)

    row = lambda c: pl.BlockSpec((tt, c), lambda i: (i, 0))
    v512 = _const((1, LRU_W))
    vec = _const((1, D_MODEL))
    return _call(
        body, "mix_fwd", (s_len // tt,),
        in_specs=[row(D_MODEL), vec, vec, vec, _whole(),
                  _const((LRU_CONV_K, LRU_W)), v512, _whole(), _whole(), v512, v512, v512, v512, v512,
                  _whole(), _whole(), v512, v512],
        out_specs=[row(IN_COLS), row(D_MODEL), row(LRU_W + GMLP_W), row(LRU_W)],
        out_shape=[_sds((s_len, IN_COLS), F32), _sds((s_len, D_MODEL), BF16),
                   _sds((s_len, LRU_W + GMLP_W), BF16), _sds((s_len, LRU_W), F32)],
        scratch=[pltpu.VMEM((SUBLANES, LRU_W), F32), pltpu.VMEM((SUBLANES, LRU_W), F32)],
        args=(x, sh, sc, g_pre, w_in, conv_w, conv_b, wr_bd, wi_bd, b_r, b_i, lru_a, vn_g, vn_b, w_sp, b_sp_t,
              g_lru, g_gmlp), carry=carry)


def _out_up_fwd(ycat, x, w_out, g_post, gt_m, g_pre, sc_f, sh_f, w_up, carry=None):
    s_len = x.shape[0]
    tt = min(TT_MIX, s_len)

    def body(yc_ref, x_ref, wo_ref, gp_ref, gt_ref, g2_ref, sc_ref, sh_ref, wu_ref, y_ref, x1_ref, h2_ref, up_ref):
        y = jnp.dot(yc_ref[...], wo_ref[...], preferred_element_type=F32)
        y_ref[...] = y
        n_y, _ = _rms(y)
        x1 = x_ref[...] + gt_ref[...] * (n_y * gp_ref[...])
        x1_ref[...] = x1
        n1, _ = _rms(x1)
        h2 = (n1 * g2_ref[...] * (1.0 + sc_ref[...]) + sh_ref[...]).astype(BF16)
        h2_ref[...] = h2
        up_ref[0] = jnp.dot(h2, wu_ref[:, 0:D_FF], preferred_element_type=F32)
        up_ref[1] = jnp.dot(h2, wu_ref[:, D_FF:], preferred_element_type=F32)

    row = lambda c: pl.BlockSpec((tt, c), lambda i: (i, 0))
    vec = _const((1, D_MODEL))
    return _call(
        body, "out_up_fwd", (s_len // tt,),
        in_specs=[row(D_MODEL), row(D_MODEL), _whole(), vec, vec, vec, vec, vec, _whole()],
        out_specs=[row(D_MODEL), row(D_MODEL), row(D_MODEL), pl.BlockSpec((2, tt, D_FF), lambda i: (0, i, 0))],
        out_shape=[_sds((s_len, D_MODEL), F32), _sds((s_len, D_MODEL), F32), _sds((s_len, D_MODEL), BF16),
                   _sds((2, s_len, D_FF), F32)],
        scratch=[], args=(ycat, x, w_out, g_post, gt_m, g_pre, sc_f, sh_f, w_up), carry=carry)


def _ffn_conv(up_pre, prev8, cw_ref, cb):
    up = cb + cw_ref[FFN_CONV_K - 1:FFN_CONV_K, :] * up_pre
    taps = []
    for k in range(FFN_CONV_K - 1):
        tap = _shift_down(up_pre, prev8, FFN_CONV_K - 1 - k)
        taps.append(tap)
        up = up + cw_ref[k:k + 1, :] * tap
    return up, taps


def _ffn_fwd(up_pre, ffn_cw, ffn_cb, w_down, x1, gt_f, g_post, target):
    s_len = x1.shape[0]
    tt = min(TT_BIG, s_len)
    cw = FF_CW
    nc = D_FF // cw

    def body(up_ref, cwg_ref, cwv_ref, cbg_ref, cbv_ref, wd_ref, x1_ref, gt_ref, gp_ref, tg_ref,
             act_ref, dy2_ref, dout_ref, loss_ref, upc_ref, vs_ref, prev, acc):
        i = pl.program_id(0)
        c = pl.program_id(1)

        @pl.when(i == 0)
        def _():
            prev[c] = jnp.zeros((2, SUBLANES, cw), F32)

        @pl.when((i == 0) & (c == 0))
        def _():
            loss_ref[...] = jnp.zeros_like(loss_ref)
            vs_ref[...] = jnp.zeros_like(vs_ref)

        ug, _ = _ffn_conv(up_ref[0], prev[c, 0], cwg_ref, cbg_ref[...])
        uv, _ = _ffn_conv(up_ref[1], prev[c, 1], cwv_ref, cbv_ref[...])
        prev[c, 0] = up_ref[0, tt - SUBLANES:, :]
        prev[c, 1] = up_ref[1, tt - SUBLANES:, :]
        upc_ref[0] = ug
        upc_ref[1] = uv
        act = (_gelu(ug) * uv).astype(BF16)
        act_ref[...] = act
        part = jnp.dot(act, wd_ref[...], preferred_element_type=F32)

        @pl.when(c == 0)
        def _():
            acc[...] = part

        @pl.when(c > 0)
        def _():
            acc[...] += part

        @pl.when(c == nc - 1)
        def _():
            n2, r2 = _rms(acc[...])
            out = x1_ref[...] + gt_ref[...] * (n2 * gp_ref[...])
            err = out - tg_ref[...]
            do = err * (1.0 / D_MODEL)
            dout_ref[...] = do
            loss_ref[...] += jnp.broadcast_to(0.5 * jnp.sum(err * err, keepdims=True) * (1.0 / D_MODEL), loss_ref.shape)
            vs_ref[0:1, :] += _colsum(do * n2 * gp_ref[...])
            vs_ref[1:2, :] += _colsum(do * gt_ref[...] * n2)
            dy2_ref[...] = _rms_bwd(do * gt_ref[...] * gp_ref[...], n2, r2).astype(BF16)

    row = pl.BlockSpec((tt, D_MODEL), lambda i, c: (i, 0))
    vec = _const((1, D_MODEL))
    ffn_cb2 = ffn_cb.reshape(1, 2 * D_FF)
    return pl.pallas_call(
        body, name="ffn_fwd", grid=(s_len // tt, nc),
        in_specs=[pl.BlockSpec((2, tt, cw), lambda i, c: (0, i, c)),
                  pl.BlockSpec((FFN_CONV_K, cw), lambda i, c: (0, c)),
                  pl.BlockSpec((FFN_CONV_K, cw), lambda i, c: (0, c + nc)),
                  pl.BlockSpec((1, cw), lambda i, c: (0, c)),
                  pl.BlockSpec((1, cw), lambda i, c: (0, c + nc)),
                  pl.BlockSpec((cw, D_MODEL), lambda i, c: (c, 0)),
                  row, vec, vec, row],
        out_specs=[pl.BlockSpec((tt, cw), lambda i, c: (i, c)), row, row, _const((SUBLANES, LANES)),
                   pl.BlockSpec((2, tt, cw), lambda i, c: (0, i, c)), _const((SUBLANES, D_MODEL))],
        out_shape=[_sds((s_len, D_FF), BF16), _sds((s_len, D_MODEL), BF16), _sds((s_len, D_MODEL), F32),
                   _sds((SUBLANES, LANES), F32), _sds((2, s_len, D_FF), F32), _sds((SUBLANES, D_MODEL), F32)],
        scratch_shapes=[pltpu.VMEM((nc, 2, SUBLANES, cw), F32), pltpu.VMEM((tt, D_MODEL), F32)],
        compiler_params=_cparams(("arbitrary", "arbitrary")),
    )(up_pre, ffn_cw, ffn_cw, ffn_cb2, ffn_cb2, w_down, x1, gt_f, g_post, target)


def _ffn_bwd(d_y2, up_pre, up, ffn_cw, w_down, carry=None):
    s_len = d_y2.shape[0]
    tt = min(TT_BIG, s_len)
    nt = s_len // tt
    cw = FF_CW
    nc = D_FF // cw

    def body(dy2_ref, up_ref, upc_ref, cwg_ref, cwv_ref, wd_ref, dup_ref, cs_ref, nxt, cs_acc):
        i = pl.program_id(0)
        c = pl.program_id(1)

        @pl.when(i == 0)
        def _():
            nxt[c] = jnp.zeros((2, SUBLANES, cw), F32)
            cs_acc[c] = jnp.zeros((2, SUBLANES, cw), F32)

        d_act = _dot_nt(dy2_ref[...], wd_ref[...])
        uv = upc_ref[1]
        gl, dgl = _gelu_and_grad(upc_ref[0])
        d_ug = d_act * uv * dgl
        d_uv = d_act * gl
        for half, (d_u, cw_ref) in enumerate(((d_ug, cwg_ref), (d_uv, cwv_ref))):
            nx = nxt[c, half]
            x_in = up_ref[half]
            d_pre = cw_ref[FFN_CONV_K - 1:FFN_CONV_K, :] * d_u
            sums = [None] * (FFN_CONV_K + 1)
            sums[FFN_CONV_K - 1] = _colsum(d_u * x_in)
            for k in range(FFN_CONV_K - 1):
                ahead = _shift_up(d_u, nx, FFN_CONV_K - 1 - k)
                d_pre = d_pre + cw_ref[k:k + 1, :] * ahead
                sums[k] = _colsum(ahead * x_in)
            sums[FFN_CONV_K] = _colsum(d_u)
            pad = jnp.zeros((SUBLANES - FFN_CONV_K - 1, cw), F32)
            cs_acc[c, half] += jnp.concatenate(sums + [pad], axis=0)
            nxt[c, half] = d_u[0:SUBLANES]
            dup_ref[half] = d_pre.astype(BF16)

        for cc in range(nc):
            @pl.when((i == nt - 1) & (c == cc))
            def _():
                cs_ref[:, cc * cw:(cc + 1) * cw] = cs_acc[cc, 0]
                cs_ref[:, D_FF + cc * cw:D_FF + (cc + 1) * cw] = cs_acc[cc, 1]

    row = pl.BlockSpec((tt, D_MODEL), lambda i, c: (nt - 1 - i, 0))
    blk = pl.BlockSpec((2, tt, cw), lambda i, c: (0, nt - 1 - i, c))
    return _call(
        body, "ffn_bwd", (nt, nc),
        in_specs=[row, blk, blk,
                  pl.BlockSpec((FFN_CONV_K, cw), lambda i, c: (0, c)),
                  pl.BlockSpec((FFN_CONV_K, cw), lambda i, c: (0, c + nc)),
                  pl.BlockSpec((cw, D_MODEL), lambda i, c: (c, 0))],
        out_specs=[blk, _const((SUBLANES, 2 * D_FF))],
        out_shape=[_sds((2, s_len, D_FF), BF16), _sds((SUBLANES, 2 * D_FF), F32)],
        scratch=[pltpu.VMEM((nc, 2, SUBLANES, cw), F32), pltpu.VMEM((nc, 2, SUBLANES, cw), F32)],
        args=(d_y2, up_pre, up, ffn_cw, ffn_cw, w_down), carry=carry)


def _up_bwd(d_up, w_up, x1, dout, y, w_out, g_pre, sc_f, g_post, gt_m, carry=None):
    s_len = x1.shape[0]
    tt = min(TT_BIG, s_len)

    def body(du_ref, wu_ref, x1_ref, do_ref, y_ref, wo_ref, g2_ref, sc_ref, gp_ref, gt_ref,
             dx1_ref, dy_ref, dyc_ref, vs_ref):
        @pl.when(pl.program_id(0) == 0)
        def _():
            vs_ref[...] = jnp.zeros_like(vs_ref)

        d_h2 = _dot_nt(du_ref[0], wu_ref[:, 0:D_FF]) + _dot_nt(du_ref[1], wu_ref[:, D_FF:])
        n1, r1 = _rms(x1_ref[...])
        ng = n1 * g2_ref[...]
        vs_ref[0:1, :] += _colsum(d_h2)
        vs_ref[1:2, :] += _colsum(d_h2 * ng)
        d_ng = d_h2 * (1.0 + sc_ref[...])
        vs_ref[2:3, :] += _colsum(d_ng * n1)
        d_x1 = do_ref[...] + _rms_bwd(d_ng * g2_ref[...], n1, r1)
        dx1_ref[...] = d_x1
        n_y, r_y = _rms(y_ref[...])
        vs_ref[3:4, :] += _colsum(d_x1 * n_y * gp_ref[...])
        d_on = d_x1 * gt_ref[...]
        vs_ref[4:5, :] += _colsum(d_on * n_y)
        d_y = _rms_bwd(d_on * gp_ref[...], n_y, r_y).astype(BF16)
        dy_ref[...] = d_y
        dyc_ref[...] = _dot_nt(d_y, wo_ref[...])

    row = lambda c: pl.BlockSpec((tt, c), lambda i: (i, 0))
    vec = _const((1, D_MODEL))
    return _call(
        body, "up_bwd", (s_len // tt,),
        in_specs=[pl.BlockSpec((2, tt, D_FF), lambda i: (0, i, 0)), _whole(), row(D_MODEL), row(D_MODEL), row(D_MODEL),
                  _whole(), vec, vec, vec, vec],
        out_specs=[row(D_MODEL), row(D_MODEL), row(LRU_W + GMLP_W), _const((SUBLANES, D_MODEL))],
        out_shape=[_sds((s_len, D_MODEL), F32), _sds((s_len, D_MODEL), BF16), _sds((s_len, LRU_W + GMLP_W), F32),
                   _sds((SUBLANES, D_MODEL), F32)],
        scratch=[], args=(d_up, w_up, x1, dout, y, w_out, g_pre, sc_f, g_post, gt_m), carry=carry)


def _head_pair_block(hd):
    return (slice((hd // 2) * HEAD_DIM, (hd // 2 + 1) * HEAD_DIM), slice((hd % 2) * HEAD_DIM, (hd % 2 + 1) * HEAD_DIM))


def _mix_bwd(d_ycat, z, hl, conv_w, conv_b, wr_bd, wi_bd, b_r, b_i, lru_a, vn_g, vn_b, w_sp, w_sp_t, b_sp_t,
             g_lru, g_gmlp, carry=None):
    s_len = z.shape[0]
    tt = min(TT_MIX, s_len)
    nt = s_len // tt
    nblk = tt // POS_BLOCK
    hb = tt // SUBLANES

    def body(dyc_ref, z_ref, zh_ref, hl_ref, hh_ref, cw_ref, cb_ref, wr_ref, wi_ref, br_ref, bi_ref, la_ref,
             vg_ref, vb_ref, ws_ref, wst_ref, bst_ref, gl_ref, gg_ref,
             dz_ref, vs_ref, dcw_ref, dwrb_ref, dwib_ref, dws_ref, dbs_ref, nxt_dxc, nxt_a, nxt_lam, dwr_ref, dwi_ref):
        i = pl.program_id(0)
        first_tile = i == nt - 1

        @pl.when(i == 0)
        def _():
            for ref in (vs_ref, dcw_ref, dwr_ref, dwi_ref, dws_ref, dbs_ref, nxt_dxc, nxt_a, nxt_lam):
                ref[...] = jnp.zeros_like(ref)

        lx = z_ref[:, 0:LRU_W]
        gate = z_ref[:, LRU_W:2 * LRU_W]
        gu = z_ref[:, 2 * LRU_W:2 * LRU_W + GMLP_W]
        gv = z_ref[:, 2 * LRU_W + GMLP_W:]
        prev8 = jnp.where(first_tile, 0.0, zh_ref[...])
        hprev8 = jnp.where(first_tile, 0.0, hh_ref[...])

        xc, taps = _lru_conv(lx, prev8, cw_ref, cb_ref[...])
        a_par = la_ref[...]
        sp_a = _softplus(-a_par)
        r, ig, a, mult = _lru_gates(xc, wr_ref, wi_ref, br_ref[...], bi_ref[...], sp_a)
        hl = hl_ref[...]
        h_prev = _shift_down(hl, hprev8, 1)
        ggate, dggate = _gelu_and_grad(gate)
        y_lru = hl * ggate
        n_l, r_l = _rms(y_lru)
        d_nl = dyc_ref[:, 0:LRU_W]
        vs_ref[6:7, :] += _colsum(d_nl * n_l)
        d_yl = _rms_bwd(d_nl * gl_ref[...], n_l, r_l)
        d_hl = d_yl * ggate
        d_gate = d_yl * hl * dggate
        a_up = _shift_up(a, nxt_a[...], 1)
        a_cum, b_cum = _scan_rev(a_up, d_hl)
        lam = b_cum + a_cum * nxt_lam[0:1, :]
        nxt_a[...] = jnp.broadcast_to(a[0:1, :], nxt_a.shape)
        nxt_lam[...] = jnp.broadcast_to(lam[0:1, :], nxt_lam.shape)
        ixc = ig * xc
        d_la = lam * h_prev * a - lam * ixc * (a * a) / mult
        d_i = lam * mult * xc
        d_xc = lam * mult * ig
        vs_ref[3:4, :] += _colsum(d_la * r) * (LRU_C * _sigmoid(-a_par))
        d_pr = d_la * (-LRU_C * sp_a) * r * (1.0 - r)
        d_pi = d_i * ig * (1.0 - ig)
        vs_ref[1:2, :] += _colsum(d_pr)
        vs_ref[2:3, :] += _colsum(d_pi)
        dwr_ref[...] += _dot_tn(xc, d_pr)
        dwi_ref[...] += _dot_tn(xc, d_pi)
        d_xc = d_xc + _dot_nt(d_pr, wr_ref[...]) + _dot_nt(d_pi, wi_ref[...])
        vs_ref[0:1, :] += _colsum(d_xc)
        nx = nxt_dxc[...]
        d_lx = cw_ref[LRU_CONV_K - 1:LRU_CONV_K, :] * d_xc
        dcw_ref[LRU_CONV_K - 1:LRU_CONV_K, :] += _colsum(d_xc * lx)
        for k in range(LRU_CONV_K - 1):
            d_lx = d_lx + cw_ref[k:k + 1, :] * _shift_up(d_xc, nx, LRU_CONV_K - 1 - k)
            dcw_ref[k:k + 1, :] += _colsum(d_xc * taps[k])
        nxt_dxc[...] = d_xc[0:SUBLANES]
        dz_ref[:, 0:LRU_W] = d_lx.astype(BF16)
        dz_ref[:, LRU_W:2 * LRU_W] = d_gate.astype(BF16)

        u, du = _gelu_and_grad(gu)
        v, vhat, rs, dav = _gmlp_v(gv, vg_ref[...], vb_ref[...])
        mask = _ws_mask()
        sp_parts = []
        for nb in range(nblk):
            rowp = []
            for g in range(N_GROUPS):
                wsm = jnp.where(mask, ws_ref[g], 0.0)
                vblk = v[nb * POS_BLOCK:(nb + 1) * POS_BLOCK, g * LANES:(g + 1) * LANES]
                rowp.append(_dot(wsm, vblk) + bst_ref[:, g:g + 1])
            sp_parts.append(jnp.concatenate(rowp, axis=1))
        sp = jnp.concatenate(sp_parts, axis=0) if nblk > 1 else sp_parts[0]
        y_g = u * sp
        n_g, r_g = _rms(y_g)
        d_ng = dyc_ref[:, LRU_W:]
        vs_ref[7:8, :] += _colsum(d_ng * n_g)
        d_yg = _rms_bwd(d_ng * gg_ref[...], n_g, r_g)
        d_gu = d_yg * sp * du
        d_sp = d_yg * u
        mask_t = _ws_mask(transposed=True)
        ones8 = jnp.ones((SUBLANES, LANES), F32)
        dv_parts = []
        for nb in range(nblk):
            rowp = []
            for g in range(N_GROUPS):
                rs_, cs_ = slice(nb * POS_BLOCK, (nb + 1) * POS_BLOCK), slice(g * LANES, (g + 1) * LANES)
                dsp_blk = d_sp[rs_, cs_]
                dbs_ref[g:g + 1, :] += lax.dot_general(
                    ones8, dsp_blk, (((1,), (1,)), ((), ())), preferred_element_type=F32,
                    precision=lax.Precision.HIGHEST)[0:1, :]
                dws_ref[g] += _dot_nt(dsp_blk, v[rs_, cs_])
                wsm_t = jnp.where(mask_t, wst_ref[g], 0.0)
                rowp.append(_dot(wsm_t, dsp_blk))
            dv_parts.append(jnp.concatenate(rowp, axis=1))
        d_v = jnp.concatenate(dv_parts, axis=0) if nblk > 1 else dv_parts[0]
        vs_ref[4:5, :] += _colsum(d_v * vhat)
        vs_ref[5:6, :] += _colsum(d_v)
        d_vh = d_v * vg_ref[...]
        d_av = rs * (d_vh - jnp.mean(d_vh, axis=-1, keepdims=True)
                     - vhat * jnp.mean(d_vh * vhat, axis=-1, keepdims=True))
        dz_ref[:, 2 * LRU_W:2 * LRU_W + GMLP_W] = d_gu.astype(BF16)
        dz_ref[:, 2 * LRU_W + GMLP_W:] = (d_av * dav).astype(BF16)

        @pl.when(i == nt - 1)
        def _():
            for hd in range(N_HEADS):
                blk = slice(hd * HEAD_DIM, (hd + 1) * HEAD_DIM)
                dwrb_ref[_head_pair_block(hd)] = dwr_ref[blk, blk]
                dwib_ref[_head_pair_block(hd)] = dwi_ref[blk, blk]
            for g in range(N_GROUPS):
                dws_ref[g] = jnp.where(mask, dws_ref[g], 0.0)

    rev = lambda c: pl.BlockSpec((tt, c), lambda i: (nt - 1 - i, 0))
    halo = pl.BlockSpec((SUBLANES, LRU_W), lambda i: (jnp.maximum((nt - 1 - i) * hb - 1, 0), 0))
    v512 = _const((1, LRU_W))
    return _call(
        body, "mix_bwd", (nt,),
        in_specs=[rev(LRU_W + GMLP_W), rev(IN_COLS), halo, rev(LRU_W), halo,
                  _const((LRU_CONV_K, LRU_W)), v512, _whole(), _whole(), v512, v512, v512, v512, v512,
                  _whole(), _whole(), _whole(), v512, v512],
        out_specs=[rev(IN_COLS), _const((SUBLANES, LRU_W)), _const((SUBLANES, LRU_W)),
                   _const((LRU_W // 2, 2 * HEAD_DIM)), _const((LRU_W // 2, 2 * HEAD_DIM)),
                   _const((N_GROUPS, POS_BLOCK, POS_BLOCK)), _const((SUBLANES, POS_BLOCK))],
        out_shape=[_sds((s_len, IN_COLS), BF16), _sds((SUBLANES, LRU_W), F32), _sds((SUBLANES, LRU_W), F32),
                   _sds((LRU_W // 2, 2 * HEAD_DIM), F32), _sds((LRU_W // 2, 2 * HEAD_DIM), F32),
                   _sds((N_GROUPS, POS_BLOCK, POS_BLOCK), F32), _sds((SUBLANES, POS_BLOCK), F32)],
        scratch=[pltpu.VMEM((SUBLANES, LRU_W), F32), pltpu.VMEM((SUBLANES, LRU_W), F32),
                 pltpu.VMEM((SUBLANES, LRU_W), F32), pltpu.VMEM((LRU_W, LRU_W), F32), pltpu.VMEM((LRU_W, LRU_W), F32)],
        args=(d_ycat, z, z, hl, hl, conv_w, conv_b, wr_bd, wi_bd, b_r, b_i, lru_a, vn_g, vn_b, w_sp, w_sp_t, b_sp_t,
              g_lru, g_gmlp), carry=carry)


def _in_bwd(d_z, w_in, x, d_x1, g, sc, carry=None):
    s_len = x.shape[0]
    tt = min(TT_BIG, s_len)

    def body(dz_ref, w_ref, x_ref, dx1_ref, g_ref, sc_ref, gx_ref, vs_ref):
        @pl.when(pl.program_id(0) == 0)
        def _():
            vs_ref[...] = jnp.zeros_like(vs_ref)

        d_h = _dot_nt(dz_ref[...], w_ref[...])
        n, r = _rms(x_ref[...])
        vs_ref[0:1, :] += _colsum(d_h)
        vs_ref[1:2, :] += _colsum(d_h * n * g_ref[...])
        d_ng = d_h * (1.0 + sc_ref[...])
        vs_ref[2:3, :] += _colsum(d_ng * n)
        gx_ref[...] = dx1_ref[...] + _rms_bwd(d_ng * g_ref[...], n, r)

    row = lambda c: pl.BlockSpec((tt, c), lambda i: (i, 0))
    vec = _const((1, D_MODEL))
    return _call(
        body, "in_bwd", (s_len // tt,),
        in_specs=[row(IN_COLS), _whole(), row(D_MODEL), row(D_MODEL), vec, vec],
        out_specs=[row(D_MODEL), _const((SUBLANES, D_MODEL))],
        out_shape=[_sds((s_len, D_MODEL), F32), _sds((SUBLANES, D_MODEL), F32)],
        scratch=[], args=(d_z, w_in, x, d_x1, g, sc), carry=carry)


def _wgrad(a, b, tn, name, carry=None):
    s_len, k_dim = a.shape
    halves = b.ndim == 3
    n_dim = b.shape[-1] * (2 if halves else 1)
    ts = min(TT_WG, s_len)
    nj = n_dim // tn
    nt = s_len // ts

    def body(a_ref, b_ref, o_ref, ob_ref):
        t = pl.program_id(1)
        part = _dot_tn(a_ref[...], b_ref[0] if halves else b_ref[...])

        @pl.when(t == 0)
        def _():
            o_ref[...] = part

        @pl.when(t > 0)
        def _():
            o_ref[...] += part

        @pl.when(t == nt - 1)
        def _():
            ob_ref[...] = o_ref[...].astype(BF16)

    if halves:
        per_half = nj // 2
        b_spec = pl.BlockSpec((1, ts, tn), lambda j, t: (j // per_half, t, j % per_half))
    else:
        b_spec = pl.BlockSpec((ts, tn), lambda j, t: (t, j))
    o_spec = pl.BlockSpec((k_dim, tn), lambda j, t: (0, j))
    return _call(
        body, name, (nj, nt),
        in_specs=[pl.BlockSpec((ts, k_dim), lambda j, t: (t, 0)), b_spec],
        out_specs=[o_spec, o_spec],
        out_shape=[_sds((k_dim, n_dim), F32), _sds((k_dim, n_dim), BF16)],
        scratch=[], args=(a, b), carry=carry)


def _adam_math(w, g, m, v):
    m = ADAM_B1 * m + (1.0 - ADAM_B1) * g
    v = ADAM_B2 * v + (1.0 - ADAM_B2) * (g * g)
    m_hat = m / (1.0 - ADAM_B1 ** ADAM_STEP)
    v_hat = v / (1.0 - ADAM_B2 ** ADAM_STEP)
    delta = -ADAM_LR * (m_hat / (jnp.sqrt(v_hat) + ADAM_EPS) + ADAM_WD * w)
    return delta, m, v


def _row_tile(rows, cols, n_f32_arrays):
    budget = VMEM_LIMIT // 2
    tr = rows
    while tr % 2 == 0 and tr // 2 >= SUBLANES and (tr // 2) % SUBLANES == 0 and tr * cols * 4 * n_f32_arrays * 2 > budget:
        tr //= 2
    return tr


def _adamw_sum(w, g_full, recv, m, v, col_sharded, name):
    _, rows, cols = w.shape
    n_recv = len(recv)
    tr = _row_tile(rows, cols, 10)
    nb = rows // tr

    def body(me_ref, w_ref, g_ref, *rest):
        r_refs = rest[:n_recv]
        m_ref, v_ref, go_ref, d_ref, mo_ref, vo_ref = rest[n_recv:]
        g = g_ref[...]
        for r_ref in r_refs:
            for k in range(r_ref.shape[0]):
                g = g + r_ref[k].astype(F32)
        go_ref[0] = g
        d_ref[0], mo_ref[0], vo_ref[0] = _adam_math(w_ref[0], g, m_ref[0], v_ref[0])

    if col_sharded:
        own = pl.BlockSpec((tr, cols), lambda i, me: (i, me[0]))
    else:
        own = pl.BlockSpec((tr, cols), lambda i, me: (me[0] * nb + i, 0))
    blk = pl.BlockSpec((1, tr, cols), lambda i, me: (0, i, 0))
    return pl.pallas_call(
        body, name=name,
        grid_spec=pltpu.PrefetchScalarGridSpec(
            num_scalar_prefetch=1, grid=(nb,),
            in_specs=[blk, own] + [pl.BlockSpec((r.shape[0], tr, cols), lambda i, me: (0, i, 0)) for r in recv]
            + [blk, blk],
            out_specs=[blk] * 4),
        out_shape=[_sds((1, rows, cols), F32)] * 4,
        compiler_params=_cparams(("arbitrary",)),
    )(jnp.reshape(_dev_index(_my_pos()), (1,)).astype(jnp.int32), w, g_full, *recv, m, v)


def _row_of_each(ref, row):
    cols = ref.shape[1]
    rows = _rows((N_DEV, cols))
    out = jnp.zeros((N_DEV, cols), F32)
    for d in range(N_DEV):
        picked = ref[d * SUBLANES + row:d * SUBLANES + row + 1, :]
        out = jnp.where(rows == d, jnp.broadcast_to(picked, (N_DEV, cols)), out)
    return out


def _my_columns(full, width, me):
    out = jnp.zeros(full.shape[:-1] + (width,), F32)
    for d in range(N_DEV):
        out = out + jnp.where(me == d, full[:, d * width:(d + 1) * width], 0.0)
    return out


def _adamw_wada(c_all, vs_in_all, vs_up_all, vs_ffn_all, w, m, v):
    _, rows, cols = w.shape

    def body(c_ref, vi_ref, vu_ref, vf_ref, w_ref, m_ref, v_ref, go_ref, d_ref, mo_ref, vo_ref):
        me = _dev_index(_my_pos())
        cv = _row_of_each(c_ref, 0)
        ca = cv * _sigmoid(cv)
        dmod = jnp.concatenate([_row_of_each(vi_ref, 0), _row_of_each(vi_ref, 1), _row_of_each(vu_ref, 3),
                                _row_of_each(vu_ref, 0), _row_of_each(vu_ref, 1), _row_of_each(vf_ref, 0)], axis=1)
        dm = _my_columns(dmod, cols, me)
        g = lax.dot_general(ca, dm, (((0,), (0,)), ((), ())), preferred_element_type=F32,
                            precision=lax.Precision.HIGHEST)
        go_ref[0] = g
        d_ref[0], mo_ref[0], vo_ref[0] = _adam_math(w_ref[0], g, m_ref[0], v_ref[0])

    return pl.pallas_call(
        body, name="adamw_w_ada", out_shape=[_sds((1, rows, cols), F32)] * 4,
        in_specs=[_whole()] * 7, out_specs=[_whole()] * 4,
        compiler_params=_cparams(),
    )(c_all, vs_in_all, vs_up_all, vs_ffn_all, w, m, v)


def _adamw_small(gathered, reduced, params, conv_params):
    names = list(params) + list(conv_params)
    allp = {**params, **conv_params}
    n_g = len(gathered) + len(reduced)

    def body(*refs):
        g_refs = refs[:n_g]
        p_refs = refs[n_g:n_g + 3 * len(names)]
        o_refs = refs[n_g + 3 * len(names):]
        me = _dev_index(_my_pos())

        def total(ref):
            s = ref[0:SUBLANES, :]
            for d in range(1, N_DEV):
                s = s + ref[d * SUBLANES:(d + 1) * SUBLANES, :]
            return s

        vs_in, vs_up, vs_ffn, loss = [total(r) for r in g_refs[:4]]
        cs, vs_mix, dcw, dwr, dwi, dws, dbs = [r[...] for r in g_refs[4:]]
        o_refs[-1][...] = loss[0:1, 0:1]
        mine = lambda full, width: _my_columns(full, width, me)

        all_ = (slice(None), slice(None))
        heads = lambda row: [((0, slice(h, h + 1), slice(None)), row[:, h * HEAD_DIM:(h + 1) * HEAD_DIM])
                             for h in range(N_HEADS)]
        blocks = lambda pairs: [((0, h), pairs[_head_pair_block(h)]) for h in range(N_HEADS)]
        pieces = {
            "b_ada": [((slice(None), slice(k * D_MODEL, (k + 1) * D_MODEL)), row) for k, row in enumerate(
                (vs_in[0:1], vs_in[1:2], vs_up[3:4], vs_up[0:1], vs_up[1:2], vs_ffn[0:1]))],
            "g_mix_pre": [(all_, vs_in[2:3])], "g_mix_post": [(all_, vs_up[4:5])],
            "g_ffn_pre": [(all_, vs_up[2:3])], "g_ffn_post": [(all_, vs_ffn[1:2])],
            "conv_b": [(all_, vs_mix[0:1])], "b_rgate": heads(vs_mix[1:2]), "b_igate": heads(vs_mix[2:3]),
            "lru_a": [(all_, vs_mix[3:4])], "v_norm_g": [(all_, vs_mix[4:5])], "v_norm_b": [(all_, vs_mix[5:6])],
            "g_lru_out": [(all_, vs_mix[6:7])], "g_gmlp_out": [(all_, vs_mix[7:8])],
            "w_rgate": blocks(dwr), "w_igate": blocks(dwi),
            "w_spatial": [((0, g), dws[g * POS_BLOCK:(g + 1) * POS_BLOCK, :]) for g in range(N_GROUPS)],
            "b_spatial": [((0,), dbs[0:N_GROUPS])],
            "ffn_conv_b": [(all_, cs[FFN_CONV_K:FFN_CONV_K + 1])],
            "conv_w": [((0,), mine(dcw[0:LRU_CONV_K], LRU_W // N_DEV))],
            "ffn_conv_w": [((0,), mine(cs[0:FFN_CONV_K], 2 * D_FF // N_DEV))],
        }
        for n_i, name in enumerate(names):
            w_ref, m_ref, v_ref = p_refs[3 * n_i:3 * n_i + 3]
            go_ref, d_ref, mo_ref, vo_ref = o_refs[4 * n_i:4 * n_i + 4]
            for idx, g in pieces[name]:
                go_ref[idx] = g
                d_ref[idx], mo_ref[idx], vo_ref[idx] = _adam_math(w_ref[idx], g, m_ref[idx], v_ref[idx])

    flat_params = [a for n in names for a in allp[n]]
    out_shape = [_sds(allp[n][0].shape, F32) for n in names for _ in range(4)] + [_sds((1, 1), F32)]
    outs = pl.pallas_call(
        body, name="adamw_small", out_shape=out_shape,
        in_specs=[_whole()] * (n_g + len(flat_params)), out_specs=[_whole()] * len(out_shape),
        compiler_params=_cparams(),
    )(*gathered, *reduced, *flat_params)
    return {n: outs[4 * i:4 * i + 4] for i, n in enumerate(names)}, outs[-1]


def _my_pos():
    return lax.axis_index("x"), lax.axis_index("y"), lax.axis_index("c")


def _flip(pos, k):
    x, y, c = pos
    return (1 - x if k & 4 else x, 1 - y if k & 2 else y, 1 - c if k & 1 else c)


def _dev_index(pos):
    x, y, c = pos
    return 4 * x + 2 * y + c


def _all_gather_small(ins, outs, send_sems, recv_sems):
    n = len(ins)
    me = _my_pos()

    def slot(a, pos):
        rows = ins[a].shape[0]
        return outs[a].at[pl.ds(pl.multiple_of(_dev_index(pos) * rows, SUBLANES), rows), :]

    def copy(a, k, block):
        return pltpu.make_async_remote_copy(
            src_ref=ins[a], dst_ref=slot(a, block), send_sem=send_sems.at[a, k - 1], recv_sem=recv_sems.at[a, k - 1],
            device_id=_flip(me, k), device_id_type=MESH)

    sends = [copy(a, k, me) for a in range(n) for k in range(1, N_DEV)]
    for cp in sends:
        cp.start()
    for a in range(n):
        rows = ins[a].shape[0]
        outs[a][pl.ds(pl.multiple_of(_dev_index(me) * rows, SUBLANES), rows), :] = ins[a][...]
    for a in range(n):
        for k in range(1, N_DEV):
            copy(a, k, _flip(me, k)).wait_recv()
    for cp in sends:
        cp.wait_send()


def _prologue(c8, cw8, fcw8, w_ada, b_ada, carry):
    cols = w_ada.shape[1]

    def body(c_ref, cw_ref, fcw_ref, w_ref, b_ref, call_ref, cwall_ref, fcwall_ref, modall_ref, mod_scr,
             s1, r1, s2, r2):
        _all_gather_small([c_ref, cw_ref, fcw_ref], [call_ref, cwall_ref, fcwall_ref], s1, r1)
        cv = _row_of_each(call_ref, 0)
        ca = cv * _sigmoid(cv)
        b_cols = _my_columns(b_ref[...], cols, _dev_index(_my_pos()))
        mod_scr[...] = jnp.dot(ca, w_ref[...], preferred_element_type=F32, precision=lax.Precision.HIGHEST) + b_cols
        _all_gather_small([mod_scr], [modall_ref], s2, r2)

    sem = lambda n: pltpu.SemaphoreType.DMA((n, N_DEV - 1))
    return _call(
        body, "prologue", (1,), in_specs=[_whole()] * 5, out_specs=[_whole()] * 4,
        out_shape=[_sds((N_DEV * SUBLANES, a.shape[1]), F32) for a in (c8, cw8, fcw8)]
        + [_sds((N_DEV * N_DEV, cols), F32)],
        scratch=[pltpu.VMEM((N_DEV, cols), F32), sem(3), sem(3), sem(1), sem(1)],
        args=(c8, cw8, fcw8, w_ada, b_ada), carry=carry)


def _reduce_small(gath, red, carry=None):
    n_g, n_r = len(gath), len(red)
    chip_flips = (4, 2, 6)

    def body(*refs):
        g_in, r_in = refs[:n_g], refs[n_g:n_g + n_r]
        g_out, r_out = refs[n_g + n_r:2 * n_g + n_r], refs[2 * n_g + n_r:2 * (n_g + n_r)]
        scr = refs[2 * (n_g + n_r):]
        sib, land = scr[:n_r], scr[n_r:2 * n_r]
        g_send, g_recv, s_send, s_recv, i_send, i_recv, f_send, f_recv = scr[2 * n_r:]
        me = _my_pos()
        c = me[2]
        sibling = _flip(me, 1)

        def slot(a, pos):
            return g_out[a].at[pl.ds(pl.multiple_of(_dev_index(pos) * SUBLANES, SUBLANES), SUBLANES), :]

        def gcopy(a, k):
            return pltpu.make_async_remote_copy(
                src_ref=g_in[a], dst_ref=slot(a, me), send_sem=g_send.at[a, k - 1], recv_sem=g_recv.at[a, k - 1],
                device_id=_flip(me, k), device_id_type=MESH)

        def scopy(a):
            return pltpu.make_async_remote_copy(
                src_ref=r_in[a], dst_ref=sib[a], send_sem=s_send.at[a], recv_sem=s_recv.at[a],
                device_id=sibling, device_id_type=MESH)

        def icopy(a, j):
            return pltpu.make_async_remote_copy(
                src_ref=r_out[a], dst_ref=land[a].at[j], send_sem=i_send.at[a, j], recv_sem=i_recv.at[a, j],
                device_id=_flip(me, chip_flips[j]), device_id_type=MESH)

        def fcopy(a, j):
            return pltpu.make_async_remote_copy(
                src_ref=land[a].at[j], dst_ref=land[a].at[j], send_sem=f_send.at[a, j], recv_sem=f_recv.at[a, j],
                device_id=sibling, device_id_type=MESH)

        gathers = [gcopy(a, k) for a in range(n_g) for k in range(1, N_DEV)]
        swaps = [scopy(a) for a in range(n_r)]
        for cp in gathers + swaps:
            cp.start()
        for a in range(n_g):
            g_out[a][pl.ds(pl.multiple_of(_dev_index(me) * SUBLANES, SUBLANES), SUBLANES), :] = g_in[a][...]
        for a in range(n_r):
            swaps[a].wait_recv()
            r_out[a][...] = r_in[a][...] + sib[a][...]

        for core in range(2):
            mine = [a for a in range(n_r) if a % 2 == core]
            theirs = [a for a in range(n_r) if a % 2 != core]

            @pl.when(c == core)
            def _():
                out = [icopy(a, j) for a in mine for j in range(3)]
                for cp in out:
                    cp.start()
                fwd = []
                for a in mine:
                    for j in range(3):
                        icopy(a, j).wait_recv()
                        cp = fcopy(a, j)
                        cp.start()
                        fwd.append(cp)
                for a in theirs:
                    for j in range(3):
                        fcopy(a, j).wait_recv()
                for cp in out + fwd:
                    cp.wait_send()

        for a in range(n_r):
            r_out[a][...] = (r_out[a][...] + land[a][1]) + (land[a][0] + land[a][2])
        for a in range(n_g):
            for k in range(1, N_DEV):
                pltpu.make_async_remote_copy(
                    src_ref=g_in[a], dst_ref=slot(a, _flip(me, k)), send_sem=g_send.at[a, k - 1],
                    recv_sem=g_recv.at[a, k - 1], device_id=_flip(me, k), device_id_type=MESH).wait_recv()
        for cp in gathers + swaps:
            cp.wait_send()

    shapes = [tuple(a.shape) for a in red]
    outs, carried = _call(
        body, "reduce_small", (1,), in_specs=[_whole()] * (n_g + n_r), out_specs=[_whole()] * (n_g + n_r),
        out_shape=[_sds((N_DEV * SUBLANES, a.shape[1]), F32) for a in gath] + [_sds(s, F32) for s in shapes],
        scratch=[pltpu.VMEM(s, F32) for s in shapes] + [pltpu.VMEM((3,) + s, F32) for s in shapes]
        + [pltpu.SemaphoreType.DMA((n_g, N_DEV - 1)), pltpu.SemaphoreType.DMA((n_g, N_DEV - 1)),
           pltpu.SemaphoreType.DMA((n_r,)), pltpu.SemaphoreType.DMA((n_r,)),
           pltpu.SemaphoreType.DMA((n_r, 3)), pltpu.SemaphoreType.DMA((n_r, 3)),
           pltpu.SemaphoreType.DMA((n_r, 3)), pltpu.SemaphoreType.DMA((n_r, 3))],
        args=tuple(gath) + tuple(red), carry=carry)
    return (outs[:n_g], outs[n_g:]), carried


def _region(ref, shard_shape, col_sharded, pos):
    r, cdim = shard_shape
    d = _dev_index(pos)
    if col_sharded:
        return ref.at[:, pl.ds(pl.multiple_of(d * cdim, LANES), cdim)]
    return ref.at[pl.ds(pl.multiple_of(d * r, 2 * SUBLANES), r), :]


def _gather_carry(shards, col_sharded):
    n_w = len(shards)
    shapes = [tuple(s.shape) for s in shards]
    full_shapes = [(s[0], s[1] * N_DEV) if cs else (s[0] * N_DEV, s[1]) for s, cs in zip(shapes, col_sharded)]

    def tools(out_refs, scr):
        send_sems, recv_sems = scr[n_w], scr[n_w + 1]
        me = _my_pos()
        x, y, c = me
        sibling = (x, y, 1 - c)
        chips = [(1 - x, y), (x, 1 - y), (1 - x, 1 - y)]

        def region(w, pos):
            return _region(out_refs[w], shapes[w], col_sharded[w], pos)

        def copy(w, k, block, to, src=None):
            return pltpu.make_async_remote_copy(
                src_ref=region(w, block) if src is None else src, dst_ref=region(w, block),
                send_sem=send_sems.at[w, k], recv_sem=recv_sems.at[w, k], device_id=to, device_id_type=MESH)

        def first(w):
            return [copy(w, 0, me, sibling, src=scr[w])] + [
                copy(w, 1 + j, me, (*chip, c), src=scr[w]) for j, chip in enumerate(chips)]

        def mine(w):
            return pltpu.make_async_copy(scr[w], region(w, me), scr[n_w + 2].at[w])

        return me, c, sibling, chips, copy, first, mine

    def start(ins, outs, scr):
        _, _, _, _, _, first, mine = tools(outs, scr)
        for w in range(n_w):
            scr[w][...] = ins[w][...].astype(BF16)
            for cp in first(w) + [mine(w)]:
                cp.start()

    def finish(ins, outs, scr):
        me, c, sibling, chips, copy, first, mine = tools(outs, scr)
        passed = []
        for w in range(n_w):
            for j, chip in enumerate(chips):
                copy(w, 1 + j, (*chip, c), me).wait_recv()
                fwd = copy(w, 4 + j, (*chip, c), sibling)
                fwd.start()
                passed.append(fwd)
        for w in range(n_w):
            copy(w, 0, sibling, me).wait_recv()
            for j, chip in enumerate(chips):
                copy(w, 4 + j, (*chip, 1 - c), me).wait_recv()
        for w in range(n_w):
            for cp in first(w):
                cp.wait_send()
            mine(w).wait()
        for cp in passed:
            cp.wait_send()

    return _Carry(
        inputs=list(shards), in_specs=[_whole()] * n_w,
        out_shape=[_sds(s, BF16) for s in full_shapes], out_specs=[_any()] * n_w,
        scratch=[pltpu.VMEM(s, BF16) for s in shapes]
        + [pltpu.SemaphoreType.DMA((n_w, N_DEV - 1)), pltpu.SemaphoreType.DMA((n_w, N_DEV - 1)),
           pltpu.SemaphoreType.DMA((n_w,))],
        start=start, finish=finish)


def _scatter_carry(grads_bf, shard_shapes, col_sharded, relations):
    n_w = len(grads_bf)
    shapes = [tuple(s) for s in shard_shapes]

    def copies(ins, outs, scr):
        send_sems, recv_sems = scr
        me = _my_pos()
        out = []
        for w in range(n_w):
            for i, k in enumerate(relations[w]):
                peer = _flip(me, k)
                out.append(pltpu.make_async_remote_copy(
                    src_ref=_region(ins[w], shapes[w], col_sharded[w], peer), dst_ref=outs[w].at[i],
                    send_sem=send_sems.at[w, i], recv_sem=recv_sems.at[w, i],
                    device_id=peer, device_id_type=MESH))
        return out

    def start(ins, outs, scr):
        for cp in copies(ins, outs, scr):
            cp.start()

    def finish(ins, outs, scr):
        cps = copies(ins, outs, scr)
        for cp in cps:
            cp.wait_recv()
        for cp in cps:
            cp.wait_send()

    return _Carry(
        inputs=list(grads_bf), in_specs=[_any()] * n_w,
        out_shape=[_sds((len(r),) + s, BF16) for r, s in zip(relations, shapes)], out_specs=[_any()] * n_w,
        scratch=[pltpu.SemaphoreType.DMA((n_w, N_DEV - 1)), pltpu.SemaphoreType.DMA((n_w, N_DEV - 1))],
        start=start, finish=finish)


def _block_diag(w):
    eye = jnp.eye(N_HEADS, dtype=w.dtype)
    return (eye[:, None, :, None] * w[:, :, None, :]).reshape(N_HEADS * HEAD_DIM, N_HEADS * HEAD_DIM)


def _pad_rows(a):
    return jnp.pad(a, ((0, SUBLANES - a.shape[0]), (0, 0)))


def _columns_from_devices(gathered, rows):
    w = gathered.shape[1]
    return gathered.reshape(N_DEV, SUBLANES, w)[:, :rows].transpose(1, 0, 2).reshape(rows, N_DEV * w)


def _local_step(x2, target, mod, w_in_f, w_full, conv_w_full, ffn_cw_full,
                g_mix_pre, g_mix_post, conv_b, w_rgate, b_rgate, w_igate, b_igate, lru_a, v_norm_g, v_norm_b,
                w_spatial, b_spatial, g_lru_out, g_gmlp_out, g_ffn_pre, g_ffn_post, ffn_conv_b,
                gather=None, scatter=None):
    sh_m, sc_m, gt_m, sh_f, sc_f, gt_f = [mod[k] for k in range(N_MOD)]
    wr_bd = _block_diag(w_rgate[0]).astype(BF16)
    wi_bd = _block_diag(w_igate[0]).astype(BF16)
    b_r = b_rgate.reshape(1, LRU_W)
    b_i = b_igate.reshape(1, LRU_W)
    b_sp_t = b_spatial[0].T
    w_sp_t = jnp.swapaxes(w_spatial[0], 1, 2)

    def arriving(name):
        return gather(name) if gather else None

    near, far = (1, 2, 3, 4, 5), (6, 7)

    def leaving(*parts):
        return scatter(parts) if scatter else None

    def received(recv, parts, outs):
        for (name, _, _), out in zip(parts, outs):
            recv.setdefault(name, []).append(out)

    def landed(name, carried):
        return carried[0] if gather else w_full[name]

    mix_params = (conv_w_full, conv_b, wr_bd, wi_bd, b_r, b_i, lru_a, v_norm_g, v_norm_b)
    (z, h, ycat, hl), got = _mix_fwd(x2, sh_m, sc_m, g_mix_pre, w_in_f, *mix_params, w_spatial[0], b_sp_t,
                                     g_lru_out, g_gmlp_out, carry=arriving("w_up"))
    w_up_f = landed("w_up", got)
    w_out_f = w_full["w_out"]
    (y, x1, h2, up_pre), got = _out_up_fwd(ycat, x2, w_out_f, g_mix_post, gt_m, g_ffn_pre, sc_f, sh_f, w_up_f,
                                           carry=arriving("w_down"))
    w_down_f = landed("w_down", got)
    act, d_y2, dout, loss_acc, up, vs_ffn = _ffn_fwd(up_pre, ffn_cw_full, ffn_conv_b, w_down_f, x1, gt_f, g_ffn_post,
                                                      target)

    recv = {}
    gw_down, _ = _wgrad(act, d_y2, D_MODEL // 2, "wgrad_down")
    parts = [("w_down", gw_down[1], near + far)]
    (d_up, cs_ffn), got = _ffn_bwd(d_y2, up_pre, up, ffn_cw_full, w_down_f, carry=leaving(*parts))
    received(recv, parts, got)
    gw_up, _ = _wgrad(h2, d_up, D_FF // 2, "wgrad_up")
    parts = [("w_up", gw_up[1], near)]
    (d_x1, d_y, d_ycat, vs_up), got = _up_bwd(
        d_up, w_up_f, x1, dout, y, w_out_f, g_ffn_pre, sc_f, g_mix_post, gt_m, carry=leaving(*parts))
    received(recv, parts, got)
    gw_out, _ = _wgrad(ycat, d_y, D_MODEL, "wgrad_out")
    parts = [("w_up", gw_up[1], far), ("w_out", gw_out[1], near + far)]
    (d_z, vs_mix, dcw, d_wr, d_wi, d_ws, d_bs), got = _mix_bwd(
        d_ycat, z, hl, *mix_params, w_spatial[0], w_sp_t, b_sp_t, g_lru_out, g_gmlp_out, carry=leaving(*parts))
    received(recv, parts, got)
    gw_in, _ = _wgrad(h, d_z, IN_COLS // 2, "wgrad_in")
    parts = [("w_in", gw_in[1], near)]
    (grad_x, vs_in), got = _in_bwd(d_z, w_in_f, x2, d_x1, g_mix_pre, sc_m, carry=leaving(*parts))
    received(recv, parts, got)
    pending = [("w_in", gw_in[1], far)]

    gath = [vs_in, vs_up, vs_ffn, loss_acc]
    red = [cs_ffn, vs_mix, dcw, d_wr, d_wi, d_ws.reshape(N_GROUPS * POS_BLOCK, POS_BLOCK), d_bs]
    return dict(grad_x=grad_x, gath=gath, red=red, recv=recv, pending=pending,
                w_in=gw_in, w_out=gw_out, w_up=gw_up, w_down=gw_down)


def kernel(x, c, w_ada, b_ada, g_mix_pre, g_mix_post, w_in, conv_w, conv_b, w_rgate, b_rgate, w_igate, b_igate, lru_a, v_norm_g, v_norm_b, w_spatial, b_spatial, g_lru_out, g_gmlp_out, w_out, g_ffn_pre, g_ffn_post, w_up, ffn_conv_w, ffn_conv_b, w_down, loss_target, m_w_ada, m_b_ada, m_g_mix_pre, m_g_mix_post, m_w_in, m_conv_w, m_conv_b, m_w_rgate, m_b_rgate, m_w_igate, m_b_igate, m_lru_a, m_v_norm_g, m_v_norm_b, m_w_spatial, m_b_spatial, m_g_lru_out, m_g_gmlp_out, m_w_out, m_g_ffn_pre, m_g_ffn_post, m_w_up, m_ffn_conv_w, m_ffn_conv_b, m_w_down, v_w_ada, v_b_ada, v_g_mix_pre, v_g_mix_post, v_w_in, v_conv_w, v_conv_b, v_w_rgate, v_b_rgate, v_w_igate, v_b_igate, v_lru_a, v_v_norm_g, v_v_norm_b, v_w_spatial, v_b_spatial, v_g_lru_out, v_g_gmlp_out, v_w_out, v_g_ffn_pre, v_g_ffn_post, v_w_up, v_ffn_conv_w, v_ffn_conv_b, v_w_down):
    me = _dev_index(_my_pos())
    ada_cols = w_ada.shape[-1]

    big_w = dict(w_in=(w_in, m_w_in, v_w_in, True), w_out=(w_out, m_w_out, v_w_out, False),
                 w_up=(w_up, m_w_up, v_w_up, True), w_down=(w_down, m_w_down, v_w_down, False))

    def gather(*names):
        return _gather_carry([big_w[n][0][0] for n in names], [big_w[n][3] for n in names])

    def scatter(parts):
        return _scatter_carry([g for _, g, _ in parts], [big_w[n][0].shape[1:] for n, _, _ in parts],
                              [big_w[n][3] for n, _, _ in parts], [rel for _, _, rel in parts])

    (c_all, cw_all, fcw_all, mod_all), (w_in_f, w_out_f) = _prologue(
        jnp.broadcast_to(c, (SUBLANES, D_MODEL)), _pad_rows(conv_w[0]), _pad_rows(ffn_conv_w[0]), w_ada[0], b_ada,
        carry=gather("w_in", "w_out"))
    conv_w_full = _columns_from_devices(cw_all, LRU_CONV_K)
    ffn_cw_full = _columns_from_devices(fcw_all, FFN_CONV_K)
    mod = lax.dynamic_index_in_dim(mod_all.reshape(N_DEV, N_DEV, ada_cols), me, axis=1, keepdims=False)
    mod = mod.reshape(N_MOD, 1, D_MODEL)

    loc = _local_step(x[0], loss_target[0], mod, w_in_f, dict(w_out=w_out_f), conv_w_full, ffn_cw_full,
                      g_mix_pre, g_mix_post, conv_b, w_rgate, b_rgate, w_igate, b_igate, lru_a, v_norm_g, v_norm_b,
                      w_spatial, b_spatial, g_lru_out, g_gmlp_out, g_ffn_pre, g_ffn_post, ffn_conv_b,
                      gather=gather, scatter=scatter)
    grad_x = loc["grad_x"]

    (gathered, reduced), got = _reduce_small(loc["gath"], loc["red"], carry=scatter(loc["pending"]))
    for (name, _, _), out in zip(loc["pending"], got):
        loc["recv"][name].append(out)

    results = {}
    for name, (w_, m_, v_, cs) in big_w.items():
        results[name] = _adamw_sum(w_, loc[name][0], loc["recv"][name], m_, v_, cs, "adamw_" + name)

    params = dict(
        b_ada=(b_ada, m_b_ada, v_b_ada), g_mix_pre=(g_mix_pre, m_g_mix_pre, v_g_mix_pre),
        g_mix_post=(g_mix_post, m_g_mix_post, v_g_mix_post), conv_b=(conv_b, m_conv_b, v_conv_b),
        w_rgate=(w_rgate, m_w_rgate, v_w_rgate), b_rgate=(b_rgate, m_b_rgate, v_b_rgate),
        w_igate=(w_igate, m_w_igate, v_w_igate), b_igate=(b_igate, m_b_igate, v_b_igate),
        lru_a=(lru_a, m_lru_a, v_lru_a), v_norm_g=(v_norm_g, m_v_norm_g, v_v_norm_g),
        v_norm_b=(v_norm_b, m_v_norm_b, v_v_norm_b), w_spatial=(w_spatial, m_w_spatial, v_w_spatial),
        b_spatial=(b_spatial, m_b_spatial, v_b_spatial), g_lru_out=(g_lru_out, m_g_lru_out, v_g_lru_out),
        g_gmlp_out=(g_gmlp_out, m_g_gmlp_out, v_g_gmlp_out), g_ffn_pre=(g_ffn_pre, m_g_ffn_pre, v_g_ffn_pre),
        g_ffn_post=(g_ffn_post, m_g_ffn_post, v_g_ffn_post), ffn_conv_b=(ffn_conv_b, m_ffn_conv_b, v_ffn_conv_b))
    conv_params = dict(conv_w=(conv_w, m_conv_w, v_conv_w), ffn_conv_w=(ffn_conv_w, m_ffn_conv_w, v_ffn_conv_w))
    small_results, loss = _adamw_small(gathered, reduced, params, conv_params)
    results.update(small_results)
    loss = loss.reshape(())

    results["w_ada"] = _adamw_wada(c_all, gathered[0], gathered[1], gathered[2], w_ada, m_w_ada, v_w_ada)

    order = ["w_ada", "b_ada", "g_mix_pre", "g_mix_post", "w_in", "conv_w", "conv_b", "w_rgate", "b_rgate", "w_igate",
             "b_igate", "lru_a", "v_norm_g", "v_norm_b", "w_spatial", "b_spatial", "g_lru_out", "g_gmlp_out", "w_out",
             "g_ffn_pre", "g_ffn_post", "w_up", "ffn_conv_w", "ffn_conv_b", "w_down"]
    outs = [loss, grad_x[None]]
    for kind in range(4):
        outs += [results[n][kind] for n in order]
    return tuple(outs)
```

```python
import functools

import jax
import jax.numpy as jnp
from jax import lax
from jax.experimental import pallas as pl
from jax.experimental.pallas import tpu as pltpu

F32 = jnp.float32
BF16 = jnp.bfloat16

D_MODEL = 1024
LRU_W = 512
GMLP_W = 512
N_HEADS = 8
HEAD_DIM = 64
N_GROUPS = 4
POS_BLOCK = 128
CHUNK = 64
IN_COLS = 2048
D_FF = 3072
N_MOD = 6
N_DEV = 8
EPS = 1e-6
LRU_C = 8.0
LRU_CONV_K = 4
FFN_CONV_K = 3

ADAM_LR = 0.001
ADAM_B1 = 0.9
ADAM_B2 = 0.999
ADAM_EPS = 1e-08
ADAM_WD = 0.01
ADAM_STEP = 10

LANES = 128
SUBLANES = 8
TT_BIG = 512
TT_MIX = 256
TT_WG = 1024
FF_CW = 512
VMEM_LIMIT = 56 * 1024 * 1024

MESH = pl.DeviceIdType.MESH


def _sds(shape, dtype):
    return jax.ShapeDtypeStruct(shape, dtype)


def _cparams(sem=None):
    return pltpu.CompilerParams(dimension_semantics=sem, vmem_limit_bytes=VMEM_LIMIT)


def _whole():
    return pl.BlockSpec(memory_space=pltpu.VMEM)


def _const(shape):
    nd = len(shape)
    return pl.BlockSpec(shape, lambda *_: (0,) * nd)


def _any():
    return pl.BlockSpec(memory_space=pl.ANY)


class _Carry:
    def __init__(self, inputs, in_specs, out_shape, out_specs, scratch, start, finish):
        self.inputs, self.in_specs, self.out_shape, self.out_specs = inputs, in_specs, out_shape, out_specs
        self.scratch, self.start, self.finish = scratch, start, finish


def _call(body, name, grid, in_specs, out_specs, out_shape, scratch, args, carry=None):
    n_in, n_out, n_scr = len(in_specs), len(out_specs), len(scratch)
    c_in = len(carry.in_specs) if carry else 0
    c_out = len(carry.out_specs) if carry else 0

    def full_body(*refs):
        ins = refs[:n_in]
        c_ins = refs[n_in:n_in + c_in]
        outs = refs[n_in + c_in:n_in + c_in + n_out]
        c_outs = refs[n_in + c_in + n_out:n_in + c_in + n_out + c_out]
        scr = refs[n_in + c_in + n_out + c_out:n_in + c_in + n_out + c_out + n_scr]
        c_scr = refs[n_in + c_in + n_out + c_out + n_scr:]
        if carry:
            first = functools.reduce(lambda a, b: a & b, [pl.program_id(d) == 0 for d in range(len(grid))])
            last = functools.reduce(lambda a, b: a & b, [pl.program_id(d) == g - 1 for d, g in enumerate(grid)])

            @pl.when(first)
            def _():
                carry.start(c_ins, c_outs, c_scr)

        body(*ins, *outs, *scr)
        if carry:
            @pl.when(last)
            def _():
                carry.finish(c_ins, c_outs, c_scr)

    res = pl.pallas_call(
        full_body, name=name, grid=grid,
        in_specs=list(in_specs) + (list(carry.in_specs) if carry else []),
        out_specs=list(out_specs) + (list(carry.out_specs) if carry else []),
        out_shape=list(out_shape) + (list(carry.out_shape) if carry else []),
        scratch_shapes=list(scratch) + (list(carry.scratch) if carry else []),
        compiler_params=_cparams(("arbitrary",) * len(grid)),
    )(*args, *(carry.inputs if carry else []))
    return res[:n_out], res[n_out:]


def _gelu(x):
    u = 0.7978845608028654 * (x + 0.044715 * x * x * x)
    return 0.5 * x * (1.0 + jnp.tanh(u))


def _gelu_and_grad(x):
    x2 = x * x
    u = 0.7978845608028654 * (x + 0.044715 * x * x2)
    t = jnp.tanh(u)
    g = 0.5 * x * (1.0 + t)
    dg = 0.5 * (1.0 + t) + 0.5 * x * (1.0 - t * t) * 0.7978845608028654 * (1.0 + 3.0 * 0.044715 * x2)
    return g, dg


def _sigmoid(x):
    return 1.0 / (1.0 + jnp.exp(-x))


def _softplus(x):
    return jnp.maximum(x, 0.0) + jnp.log1p(jnp.exp(-jnp.abs(x)))


def _neg_expm1(x):
    series = -x * (1.0 + x * (0.5 + x * (1.0 / 6.0 + x * (1.0 / 24.0 + x * (1.0 / 120.0)))))
    return jnp.where(x > -0.1, series, 1.0 - jnp.exp(x))


def _dot(a, b):
    return jnp.dot(a.astype(BF16), b.astype(BF16), preferred_element_type=F32)


def _dot_nt(a, b):
    return lax.dot_general(a.astype(BF16), b.astype(BF16), (((1,), (1,)), ((), ())), preferred_element_type=F32)


def _dot_tn(a, b):
    return lax.dot_general(a.astype(BF16), b.astype(BF16), (((0,), (0,)), ((), ())), preferred_element_type=F32)


def _rows(shape):
    return lax.broadcasted_iota(jnp.int32, shape, 0)


def _shift_down(cur, prev8, s):
    if s == 0:
        return cur
    n = cur.shape[0]
    r = pltpu.roll(cur, s, 0)
    p = pltpu.roll(prev8, s, 0)
    top = jnp.where(_rows(p.shape) < s, p, r[0:SUBLANES])
    if n == SUBLANES:
        return top
    return jnp.concatenate([top, r[SUBLANES:]], axis=0)


def _shift_up(cur, next8, s):
    if s == 0:
        return cur
    n = cur.shape[0]
    r = pltpu.roll(cur, n - s, 0)
    q = pltpu.roll(next8, SUBLANES - s, 0)
    bot = jnp.where(_rows(q.shape) >= SUBLANES - s, q, r[n - SUBLANES:])
    if n == SUBLANES:
        return bot
    return jnp.concatenate([r[:n - SUBLANES], bot], axis=0)


def _scan_fwd(a, b):
    n = a.shape[0]
    rows = _rows(a.shape)
    s = 1
    while s < n:
        a_s = pltpu.roll(a, s, 0)
        b_s = pltpu.roll(b, s, 0)
        m = rows >= s
        b = jnp.where(m, a * b_s + b, b)
        a = jnp.where(m, a * a_s, a)
        s *= 2
    return a, b


def _scan_rev(a, b):
    n = a.shape[0]
    rows = _rows(a.shape)
    s = 1
    while s < n:
        a_s = pltpu.roll(a, n - s, 0)
        b_s = pltpu.roll(b, n - s, 0)
        m = rows < n - s
        b = jnp.where(m, b + a * b_s, b)
        a = jnp.where(m, a * a_s, a)
        s *= 2
    return a, b


def _rms(x):
    r = lax.rsqrt(jnp.mean(x * x, axis=-1, keepdims=True) + EPS)
    return x * r, r


def _rms_bwd(d_n, n, r):
    return r * (d_n - n * jnp.mean(d_n * n, axis=-1, keepdims=True))


def _colsum(x):
    return jnp.sum(x, axis=0, keepdims=True)


def _lru_gates(xc, wr_ref, wi_ref, br, bi, sp_a):
    r = _sigmoid(_dot(xc, wr_ref[...]) + br)
    i = _sigmoid(_dot(xc, wi_ref[...]) + bi)
    la = -LRU_C * r * sp_a
    a = jnp.exp(la)
    mult = jnp.sqrt(_neg_expm1(2.0 * la))
    return r, i, a, mult


def _lru_conv(lx, prev8, cw_ref, cb):
    xc = cb + cw_ref[LRU_CONV_K - 1:LRU_CONV_K, :] * lx
    taps = []
    for k in range(LRU_CONV_K - 1):
        tap = _shift_down(lx, prev8, LRU_CONV_K - 1 - k)
        taps.append(tap)
        xc = xc + cw_ref[k:k + 1, :] * tap
    return xc, taps


def _ws_mask(transposed=False):
    i = lax.broadcasted_iota(jnp.int32, (POS_BLOCK, POS_BLOCK), 0)
    j = lax.broadcasted_iota(jnp.int32, (POS_BLOCK, POS_BLOCK), 1)
    if transposed:
        i, j = j, i
    return (j // CHUNK) <= (i // CHUNK)


def _gmlp_v(gv, vg, vb):
    av, dav = _gelu_and_grad(gv)
    mu = jnp.mean(av, axis=-1, keepdims=True)
    cen = av - mu
    rs = lax.rsqrt(jnp.mean(cen * cen, axis=-1, keepdims=True) + EPS)
    vhat = cen * rs
    return vhat * vg + vb, vhat, rs, dav


def _mix_fwd(x, sh, sc, g_pre, w_in, conv_w, conv_b, wr_bd, wi_bd, b_r, b_i, lru_a, vn_g, vn_b, w_sp, b_sp_t,
             g_lru, g_gmlp, carry=None):
    s_len = x.shape[0]
    tt = min(TT_MIX, s_len)
    nblk = tt // POS_BLOCK

    def body(x_ref, sh_ref, sc_ref, g_ref, w_ref, cw_ref, cb_ref, wr_ref, wi_ref, br_ref, bi_ref, la_ref, vg_ref,
             vb_ref, ws_ref, bst_ref, gl_ref, gg_ref, z_ref, h_ref, y_ref, hl_ref, prev8, hcar):
        i = pl.program_id(0)

        @pl.when(i == 0)
        def _():
            prev8[...] = jnp.zeros_like(prev8)
            hcar[...] = jnp.zeros_like(hcar)

        n_x, _ = _rms(x_ref[...])
        h = (n_x * g_ref[...] * (1.0 + sc_ref[...]) + sh_ref[...]).astype(BF16)
        h_ref[...] = h
        z_ref[...] = jnp.dot(h, w_ref[...], preferred_element_type=F32)

        lx = z_ref[:, 0:LRU_W]
        gate = z_ref[:, LRU_W:2 * LRU_W]
        gu = z_ref[:, 2 * LRU_W:2 * LRU_W + GMLP_W]
        gv = z_ref[:, 2 * LRU_W + GMLP_W:]

        xc, _ = _lru_conv(lx, prev8[...], cw_ref, cb_ref[...])
        prev8[...] = lx[tt - SUBLANES:]
        sp_a = _softplus(-la_ref[...])
        _, ig, a, mult = _lru_gates(xc, wr_ref, wi_ref, br_ref[...], bi_ref[...], sp_a)
        bx = mult * (ig * xc)
        a_cum, b_cum = _scan_fwd(a, bx)
        hl = a_cum * hcar[0:1, :] + b_cum
        hcar[...] = jnp.broadcast_to(hl[tt - 1:tt, :], hcar.shape)
        hl_ref[...] = hl
        y_lru = hl * _gelu(gate)
        n_l, _ = _rms(y_lru)
        y_ref[:, 0:LRU_W] = (n_l * gl_ref[...]).astype(BF16)

        u = _gelu(gu)
        v, _, _, _ = _gmlp_v(gv, vg_ref[...], vb_ref[...])
        mask = _ws_mask()
        sp_parts = []
        for nb in range(nblk):
            row = []
            for g in range(N_GROUPS):
                wsm = jnp.where(mask, ws_ref[g], 0.0)
                vblk = v[nb * POS_BLOCK:(nb + 1) * POS_BLOCK, g * LANES:(g + 1) * LANES]
                row.append(_dot(wsm, vblk) + bst_ref[:, g:g + 1])
            sp_parts.append(jnp.concatenate(row, axis=1))
        sp = jnp.concatenate(sp_parts, axis=0) if nblk > 1 else sp_parts[0]
        n_g, _ = _rms(u * sp)
        y_ref[:, LRU_W:] = (n_g * gg_ref[...]).astype(BF16)

    row = lambda c: pl.BlockSpec((tt, c), lambda i: (i, 0))
    v512 = _const((1, LRU_W))
    vec = _const((1, D_MODEL))
    return _call(
        body, "mix_fwd", (s_len // tt,),
        in_specs=[row(D_MODEL), vec, vec, vec, _whole(),
                  _const((LRU_CONV_K, LRU_W)), v512, _whole(), _whole(), v512, v512, v512, v512, v512,
                  _whole(), _whole(), v512, v512],
        out_specs=[row(IN_COLS), row(D_MODEL), row(LRU_W + GMLP_W), row(LRU_W)],
        out_shape=[_sds((s_len, IN_COLS), F32), _sds((s_len, D_MODEL), BF16),
                   _sds((s_len, LRU_W + GMLP_W), BF16), _sds((s_len, LRU_W), F32)],
        scratch=[pltpu.VMEM((SUBLANES, LRU_W), F32), pltpu.VMEM((SUBLANES, LRU_W), F32)],
        args=(x, sh, sc, g_pre, w_in, conv_w, conv_b, wr_bd, wi_bd, b_r, b_i, lru_a, vn_g, vn_b, w_sp, b_sp_t,
              g_lru, g_gmlp), carry=carry)


FF_CHUNKS = N_DEV // 2
FF_CHUNK_W = D_FF // FF_CHUNKS


def _ffn_fwd(ycat, x, w_out, g_mix_post, gt_m, g_pre, sc_f, sh_f, w_up3, ffn_cw, ffn_cb, w_down, gt_f, g_post,
             target, carry=None):
    s_len = x.shape[0]
    tt = min(TT_MIX, s_len)
    nc, cw = FF_CHUNKS, FF_CHUNK_W

    def body(yc_ref, x_ref, wo_ref, gmp_ref, gtm_ref, g2_ref, sc_ref, sh_ref, wu_ref, cwg_ref, cwv_ref, cbg_ref,
             cbv_ref, wd_ref, gtf_ref, gp_ref, tg_ref,
             y_ref, x1_ref, h2_ref, up_ref, upc_ref, act_ref, dy2_ref, dout_ref, loss_ref, vs_ref,
             h2s, acc, prev):
        i = pl.program_id(0)
        c = pl.program_id(1)

        @pl.when(i == 0)
        def _():
            prev[c] = jnp.zeros((2, SUBLANES, cw), F32)

        @pl.when((i == 0) & (c == 0))
        def _():
            loss_ref[...] = jnp.zeros_like(loss_ref)
            vs_ref[...] = jnp.zeros_like(vs_ref)

        @pl.when(c == 0)
        def _():
            y = jnp.dot(yc_ref[...], wo_ref[...], preferred_element_type=F32)
            y_ref[...] = y
            n_y, _ = _rms(y)
            x1 = x_ref[...] + gtm_ref[...] * (n_y * gmp_ref[...])
            x1_ref[...] = x1
            n1, _ = _rms(x1)
            h2 = (n1 * g2_ref[...] * (1.0 + sc_ref[...]) + sh_ref[...]).astype(BF16)
            h2_ref[...] = h2
            h2s[...] = h2

        h2 = h2s[...]
        ug_pre = jnp.dot(h2, wu_ref[c], preferred_element_type=F32)
        uv_pre = jnp.dot(h2, wu_ref[nc + c], preferred_element_type=F32)
        up_ref[0] = ug_pre
        up_ref[1] = uv_pre
        ug, _ = _ffn_conv(ug_pre, prev[c, 0], cwg_ref, cbg_ref[...])
        uv, _ = _ffn_conv(uv_pre, prev[c, 1], cwv_ref, cbv_ref[...])
        prev[c, 0] = ug_pre[tt - SUBLANES:, :]
        prev[c, 1] = uv_pre[tt - SUBLANES:, :]
        upc_ref[0] = ug
        upc_ref[1] = uv
        act = (_gelu(ug) * uv).astype(BF16)
        act_ref[...] = act
        part = jnp.dot(act, wd_ref[pl.ds(pl.multiple_of(c * cw, cw), cw), :], preferred_element_type=F32)

        @pl.when(c == 0)
        def _():
            acc[...] = part

        @pl.when(c > 0)
        def _():
            acc[...] += part

        @pl.when(c == nc - 1)
        def _():
            n2, r2 = _rms(acc[...])
            out = x1_ref[...] + gtf_ref[...] * (n2 * gp_ref[...])
            err = out - tg_ref[...]
            do = err * (1.0 / D_MODEL)
            dout_ref[...] = do
            loss_ref[...] += jnp.broadcast_to(0.5 * jnp.sum(err * err, keepdims=True) * (1.0 / D_MODEL), loss_ref.shape)
            vs_ref[0:1, :] += _colsum(do * n2 * gp_ref[...])
            vs_ref[1:2, :] += _colsum(do * gtf_ref[...] * n2)
            dy2_ref[...] = _rms_bwd(do * gtf_ref[...] * gp_ref[...], n2, r2).astype(BF16)

    row = pl.BlockSpec((tt, D_MODEL), lambda i, c: (i, 0))
    vec = _const((1, D_MODEL))
    chunk2 = pl.BlockSpec((2, tt, cw), lambda i, c: (0, i, c))
    ffn_cb2 = ffn_cb.reshape(1, 2 * D_FF)
    return _call(
        body, "ffn_fwd", (s_len // tt, nc),
        in_specs=[row, row, _whole(), vec, vec, vec, vec, vec, _whole(),
                  pl.BlockSpec((FFN_CONV_K, cw), lambda i, c: (0, c)),
                  pl.BlockSpec((FFN_CONV_K, cw), lambda i, c: (0, c + nc)),
                  pl.BlockSpec((1, cw), lambda i, c: (0, c)),
                  pl.BlockSpec((1, cw), lambda i, c: (0, c + nc)),
                  _whole(), vec, vec, row],
        out_specs=[row, row, row, chunk2, chunk2, pl.BlockSpec((tt, cw), lambda i, c: (i, c)), row, row,
                   _const((SUBLANES, LANES)), _const((SUBLANES, D_MODEL))],
        out_shape=[_sds((s_len, D_MODEL), F32), _sds((s_len, D_MODEL), F32), _sds((s_len, D_MODEL), BF16),
                   _sds((2, s_len, D_FF), F32), _sds((2, s_len, D_FF), F32), _sds((s_len, D_FF), BF16),
                   _sds((s_len, D_MODEL), BF16), _sds((s_len, D_MODEL), F32),
                   _sds((SUBLANES, LANES), F32), _sds((SUBLANES, D_MODEL), F32)],
        scratch=[pltpu.VMEM((tt, D_MODEL), BF16), pltpu.VMEM((tt, D_MODEL), F32),
                 pltpu.VMEM((nc, 2, SUBLANES, cw), F32)],
        args=(ycat, x, w_out, g_mix_post, gt_m, g_pre, sc_f, sh_f, w_up3, ffn_cw, ffn_cw, ffn_cb2, ffn_cb2, w_down,
              gt_f, g_post, target), carry=carry)


def _ffn_conv(up_pre, prev8, cw_ref, cb):
    up = cb + cw_ref[FFN_CONV_K - 1:FFN_CONV_K, :] * up_pre
    taps = []
    for k in range(FFN_CONV_K - 1):
        tap = _shift_down(up_pre, prev8, FFN_CONV_K - 1 - k)
        taps.append(tap)
        up = up + cw_ref[k:k + 1, :] * tap
    return up, taps


def _ffn_bwd(d_y2, up_pre, up, ffn_cw, w_down, carry=None):
    s_len = d_y2.shape[0]
    tt = min(TT_BIG, s_len)
    nt = s_len // tt
    cw = FF_CW
    nc = D_FF // cw

    def body(dy2_ref, up_ref, upc_ref, cwg_ref, cwv_ref, wd_ref, dup_ref, cs_ref, nxt, cs_acc):
        i = pl.program_id(0)
        c = pl.program_id(1)

        @pl.when(i == 0)
        def _():
            nxt[c] = jnp.zeros((2, SUBLANES, cw), F32)
            cs_acc[c] = jnp.zeros((2, SUBLANES, cw), F32)

        d_act = _dot_nt(dy2_ref[...], wd_ref[...])
        uv = upc_ref[1]
        gl, dgl = _gelu_and_grad(upc_ref[0])
        d_ug = d_act * uv * dgl
        d_uv = d_act * gl
        for half, (d_u, cw_ref) in enumerate(((d_ug, cwg_ref), (d_uv, cwv_ref))):
            nx = nxt[c, half]
            x_in = up_ref[half]
            d_pre = cw_ref[FFN_CONV_K - 1:FFN_CONV_K, :] * d_u
            sums = [None] * (FFN_CONV_K + 1)
            sums[FFN_CONV_K - 1] = _colsum(d_u * x_in)
            for k in range(FFN_CONV_K - 1):
                ahead = _shift_up(d_u, nx, FFN_CONV_K - 1 - k)
                d_pre = d_pre + cw_ref[k:k + 1, :] * ahead
                sums[k] = _colsum(ahead * x_in)
            sums[FFN_CONV_K] = _colsum(d_u)
            pad = jnp.zeros((SUBLANES - FFN_CONV_K - 1, cw), F32)
            cs_acc[c, half] += jnp.concatenate(sums + [pad], axis=0)
            nxt[c, half] = d_u[0:SUBLANES]
            dup_ref[half] = d_pre.astype(BF16)

        for cc in range(nc):
            @pl.when((i == nt - 1) & (c == cc))
            def _():
                cs_ref[:, cc * cw:(cc + 1) * cw] = cs_acc[cc, 0]
                cs_ref[:, D_FF + cc * cw:D_FF + (cc + 1) * cw] = cs_acc[cc, 1]

    row = pl.BlockSpec((tt, D_MODEL), lambda i, c: (nt - 1 - i, 0))
    blk = pl.BlockSpec((2, tt, cw), lambda i, c: (0, nt - 1 - i, c))
    return _call(
        body, "ffn_bwd", (nt, nc),
        in_specs=[row, blk, blk,
                  pl.BlockSpec((FFN_CONV_K, cw), lambda i, c: (0, c)),
                  pl.BlockSpec((FFN_CONV_K, cw), lambda i, c: (0, c + nc)),
                  pl.BlockSpec((cw, D_MODEL), lambda i, c: (c, 0))],
        out_specs=[blk, _const((SUBLANES, 2 * D_FF))],
        out_shape=[_sds((2, s_len, D_FF), BF16), _sds((SUBLANES, 2 * D_FF), F32)],
        scratch=[pltpu.VMEM((nc, 2, SUBLANES, cw), F32), pltpu.VMEM((nc, 2, SUBLANES, cw), F32)],
        args=(d_y2, up_pre, up, ffn_cw, ffn_cw, w_down), carry=carry)


def _up_bwd(d_up, w_up, x1, dout, y, w_out, g_pre, sc_f, g_post, gt_m, carry=None):
    s_len = x1.shape[0]
    tt = min(TT_BIG, s_len)

    def body(du_ref, wu_ref, x1_ref, do_ref, y_ref, wo_ref, g2_ref, sc_ref, gp_ref, gt_ref,
             dx1_ref, dy_ref, dyc_ref, vs_ref):
        @pl.when(pl.program_id(0) == 0)
        def _():
            vs_ref[...] = jnp.zeros_like(vs_ref)

        d_h2 = jnp.zeros((tt, D_MODEL), F32)
        for half in range(2):
            for ch in range(FF_CHUNKS):
                d_h2 = d_h2 + _dot_nt(du_ref[half, :, ch * FF_CHUNK_W:(ch + 1) * FF_CHUNK_W],
                                      wu_ref[half * FF_CHUNKS + ch])
        n1, r1 = _rms(x1_ref[...])
        ng = n1 * g2_ref[...]
        vs_ref[0:1, :] += _colsum(d_h2)
        vs_ref[1:2, :] += _colsum(d_h2 * ng)
        d_ng = d_h2 * (1.0 + sc_ref[...])
        vs_ref[2:3, :] += _colsum(d_ng * n1)
        d_x1 = do_ref[...] + _rms_bwd(d_ng * g2_ref[...], n1, r1)
        dx1_ref[...] = d_x1
        n_y, r_y = _rms(y_ref[...])
        vs_ref[3:4, :] += _colsum(d_x1 * n_y * gp_ref[...])
        d_on = d_x1 * gt_ref[...]
        vs_ref[4:5, :] += _colsum(d_on * n_y)
        d_y = _rms_bwd(d_on * gp_ref[...], n_y, r_y).astype(BF16)
        dy_ref[...] = d_y
        dyc_ref[...] = _dot_nt(d_y, wo_ref[...])

    row = lambda c: pl.BlockSpec((tt, c), lambda i: (i, 0))
    vec = _const((1, D_MODEL))
    return _call(
        body, "up_bwd", (s_len // tt,),
        in_specs=[pl.BlockSpec((2, tt, D_FF), lambda i: (0, i, 0)), _whole(), row(D_MODEL), row(D_MODEL), row(D_MODEL),
                  _whole(), vec, vec, vec, vec],
        out_specs=[row(D_MODEL), row(D_MODEL), row(LRU_W + GMLP_W), _const((SUBLANES, D_MODEL))],
        out_shape=[_sds((s_len, D_MODEL), F32), _sds((s_len, D_MODEL), BF16), _sds((s_len, LRU_W + GMLP_W), F32),
                   _sds((SUBLANES, D_MODEL), F32)],
        scratch=[], args=(d_up, w_up, x1, dout, y, w_out, g_pre, sc_f, g_post, gt_m), carry=carry)


def _head_pair_block(hd):
    return (slice((hd // 2) * HEAD_DIM, (hd // 2 + 1) * HEAD_DIM), slice((hd % 2) * HEAD_DIM, (hd % 2 + 1) * HEAD_DIM))


def _mix_bwd(d_ycat, z, hl, conv_w, conv_b, wr_bd, wi_bd, b_r, b_i, lru_a, vn_g, vn_b, w_sp, w_sp_t, b_sp_t,
             g_lru, g_gmlp, carry=None):
    s_len = z.shape[0]
    tt = min(TT_MIX, s_len)
    nt = s_len // tt
    nblk = tt // POS_BLOCK
    hb = tt // SUBLANES

    def body(dyc_ref, z_ref, zh_ref, hl_ref, hh_ref, cw_ref, cb_ref, wr_ref, wi_ref, br_ref, bi_ref, la_ref,
             vg_ref, vb_ref, ws_ref, wst_ref, bst_ref, gl_ref, gg_ref,
             dz_ref, vs_ref, dcw_ref, dwrb_ref, dwib_ref, dws_ref, dbs_ref, nxt_dxc, nxt_a, nxt_lam, dwr_ref, dwi_ref):
        i = pl.program_id(0)
        first_tile = i == nt - 1

        @pl.when(i == 0)
        def _():
            for ref in (vs_ref, dcw_ref, dwr_ref, dwi_ref, dws_ref, dbs_ref, nxt_dxc, nxt_a, nxt_lam):
                ref[...] = jnp.zeros_like(ref)

        lx = z_ref[:, 0:LRU_W]
        gate = z_ref[:, LRU_W:2 * LRU_W]
        gu = z_ref[:, 2 * LRU_W:2 * LRU_W + GMLP_W]
        gv = z_ref[:, 2 * LRU_W + GMLP_W:]
        prev8 = jnp.where(first_tile, 0.0, zh_ref[...])
        hprev8 = jnp.where(first_tile, 0.0, hh_ref[...])

        xc, taps = _lru_conv(lx, prev8, cw_ref, cb_ref[...])
        a_par = la_ref[...]
        sp_a = _softplus(-a_par)
        r, ig, a, mult = _lru_gates(xc, wr_ref, wi_ref, br_ref[...], bi_ref[...], sp_a)
        hl = hl_ref[...]
        h_prev = _shift_down(hl, hprev8, 1)
        ggate, dggate = _gelu_and_grad(gate)
        y_lru = hl * ggate
        n_l, r_l = _rms(y_lru)
        d_nl = dyc_ref[:, 0:LRU_W]
        vs_ref[6:7, :] += _colsum(d_nl * n_l)
        d_yl = _rms_bwd(d_nl * gl_ref[...], n_l, r_l)
        d_hl = d_yl * ggate
        d_gate = d_yl * hl * dggate
        a_up = _shift_up(a, nxt_a[...], 1)
        a_cum, b_cum = _scan_rev(a_up, d_hl)
        lam = b_cum + a_cum * nxt_lam[0:1, :]
        nxt_a[...] = jnp.broadcast_to(a[0:1, :], nxt_a.shape)
        nxt_lam[...] = jnp.broadcast_to(lam[0:1, :], nxt_lam.shape)
        ixc = ig * xc
        d_la = lam * h_prev * a - lam * ixc * (a * a) / mult
        d_i = lam * mult * xc
        d_xc = lam * mult * ig
        vs_ref[3:4, :] += _colsum(d_la * r) * (LRU_C * _sigmoid(-a_par))
        d_pr = d_la * (-LRU_C * sp_a) * r * (1.0 - r)
        d_pi = d_i * ig * (1.0 - ig)
        vs_ref[1:2, :] += _colsum(d_pr)
        vs_ref[2:3, :] += _colsum(d_pi)
        dwr_ref[...] += _dot_tn(xc, d_pr)
        dwi_ref[...] += _dot_tn(xc, d_pi)
        d_xc = d_xc + _dot_nt(d_pr, wr_ref[...]) + _dot_nt(d_pi, wi_ref[...])
        vs_ref[0:1, :] += _colsum(d_xc)
        nx = nxt_dxc[...]
        d_lx = cw_ref[LRU_CONV_K - 1:LRU_CONV_K, :] * d_xc
        dcw_ref[LRU_CONV_K - 1:LRU_CONV_K, :] += _colsum(d_xc * lx)
        for k in range(LRU_CONV_K - 1):
            d_lx = d_lx + cw_ref[k:k + 1, :] * _shift_up(d_xc, nx, LRU_CONV_K - 1 - k)
            dcw_ref[k:k + 1, :] += _colsum(d_xc * taps[k])
        nxt_dxc[...] = d_xc[0:SUBLANES]
        dz_ref[:, 0:LRU_W] = d_lx.astype(BF16)
        dz_ref[:, LRU_W:2 * LRU_W] = d_gate.astype(BF16)

        u, du = _gelu_and_grad(gu)
        v, vhat, rs, dav = _gmlp_v(gv, vg_ref[...], vb_ref[...])
        mask = _ws_mask()
        sp_parts = []
        for nb in range(nblk):
            rowp = []
            for g in range(N_GROUPS):
                wsm = jnp.where(mask, ws_ref[g], 0.0)
                vblk = v[nb * POS_BLOCK:(nb + 1) * POS_BLOCK, g * LANES:(g + 1) * LANES]
                rowp.append(_dot(wsm, vblk) + bst_ref[:, g:g + 1])
            sp_parts.append(jnp.concatenate(rowp, axis=1))
        sp = jnp.concatenate(sp_parts, axis=0) if nblk > 1 else sp_parts[0]
        y_g = u * sp
        n_g, r_g = _rms(y_g)
        d_ng = dyc_ref[:, LRU_W:]
        vs_ref[7:8, :] += _colsum(d_ng * n_g)
        d_yg = _rms_bwd(d_ng * gg_ref[...], n_g, r_g)
        d_gu = d_yg * sp * du
        d_sp = d_yg * u
        mask_t = _ws_mask(transposed=True)
        ones8 = jnp.ones((SUBLANES, LANES), F32)
        dv_parts = []
        for nb in range(nblk):
            rowp = []
            for g in range(N_GROUPS):
                rs_, cs_ = slice(nb * POS_BLOCK, (nb + 1) * POS_BLOCK), slice(g * LANES, (g + 1) * LANES)
                dsp_blk = d_sp[rs_, cs_]
                dbs_ref[g:g + 1, :] += lax.dot_general(
                    ones8, dsp_blk, (((1,), (1,)), ((), ())), preferred_element_type=F32,
                    precision=lax.Precision.HIGHEST)[0:1, :]
                dws_ref[g] += _dot_nt(dsp_blk, v[rs_, cs_])
                wsm_t = jnp.where(mask_t, wst_ref[g], 0.0)
                rowp.append(_dot(wsm_t, dsp_blk))
            dv_parts.append(jnp.concatenate(rowp, axis=1))
        d_v = jnp.concatenate(dv_parts, axis=0) if nblk > 1 else dv_parts[0]
        vs_ref[4:5, :] += _colsum(d_v * vhat)
        vs_ref[5:6, :] += _colsum(d_v)
        d_vh = d_v * vg_ref[...]
        d_av = rs * (d_vh - jnp.mean(d_vh, axis=-1, keepdims=True)
                     - vhat * jnp.mean(d_vh * vhat, axis=-1, keepdims=True))
        dz_ref[:, 2 * LRU_W:2 * LRU_W + GMLP_W] = d_gu.astype(BF16)
        dz_ref[:, 2 * LRU_W + GMLP_W:] = (d_av * dav).astype(BF16)

        @pl.when(i == nt - 1)
        def _():
            for hd in range(N_HEADS):
                blk = slice(hd * HEAD_DIM, (hd + 1) * HEAD_DIM)
                dwrb_ref[_head_pair_block(hd)] = dwr_ref[blk, blk]
                dwib_ref[_head_pair_block(hd)] = dwi_ref[blk, blk]
            for g in range(N_GROUPS):
                dws_ref[g] = jnp.where(mask, dws_ref[g], 0.0)

    rev = lambda c: pl.BlockSpec((tt, c), lambda i: (nt - 1 - i, 0))
    halo = pl.BlockSpec((SUBLANES, LRU_W), lambda i: (jnp.maximum((nt - 1 - i) * hb - 1, 0), 0))
    v512 = _const((1, LRU_W))
    return _call(
        body, "mix_bwd", (nt,),
        in_specs=[rev(LRU_W + GMLP_W), rev(IN_COLS), halo, rev(LRU_W), halo,
                  _const((LRU_CONV_K, LRU_W)), v512, _whole(), _whole(), v512, v512, v512, v512, v512,
                  _whole(), _whole(), _whole(), v512, v512],
        out_specs=[rev(IN_COLS), _const((SUBLANES, LRU_W)), _const((SUBLANES, LRU_W)),
                   _const((LRU_W // 2, 2 * HEAD_DIM)), _const((LRU_W // 2, 2 * HEAD_DIM)),
                   _const((N_GROUPS, POS_BLOCK, POS_BLOCK)), _const((SUBLANES, POS_BLOCK))],
        out_shape=[_sds((s_len, IN_COLS), BF16), _sds((SUBLANES, LRU_W), F32), _sds((SUBLANES, LRU_W), F32),
                   _sds((LRU_W // 2, 2 * HEAD_DIM), F32), _sds((LRU_W // 2, 2 * HEAD_DIM), F32),
                   _sds((N_GROUPS, POS_BLOCK, POS_BLOCK), F32), _sds((SUBLANES, POS_BLOCK), F32)],
        scratch=[pltpu.VMEM((SUBLANES, LRU_W), F32), pltpu.VMEM((SUBLANES, LRU_W), F32),
                 pltpu.VMEM((SUBLANES, LRU_W), F32), pltpu.VMEM((LRU_W, LRU_W), F32), pltpu.VMEM((LRU_W, LRU_W), F32)],
        args=(d_ycat, z, z, hl, hl, conv_w, conv_b, wr_bd, wi_bd, b_r, b_i, lru_a, vn_g, vn_b, w_sp, w_sp_t, b_sp_t,
              g_lru, g_gmlp), carry=carry)


def _in_bwd(d_z, w_in, x, d_x1, g, sc, carry=None):
    s_len = x.shape[0]
    tt = min(TT_BIG, s_len)

    def body(dz_ref, w_ref, x_ref, dx1_ref, g_ref, sc_ref, gx_ref, vs_ref):
        @pl.when(pl.program_id(0) == 0)
        def _():
            vs_ref[...] = jnp.zeros_like(vs_ref)

        d_h = _dot_nt(dz_ref[...], w_ref[...])
        n, r = _rms(x_ref[...])
        vs_ref[0:1, :] += _colsum(d_h)
        vs_ref[1:2, :] += _colsum(d_h * n * g_ref[...])
        d_ng = d_h * (1.0 + sc_ref[...])
        vs_ref[2:3, :] += _colsum(d_ng * n)
        gx_ref[...] = dx1_ref[...] + _rms_bwd(d_ng * g_ref[...], n, r)

    row = lambda c: pl.BlockSpec((tt, c), lambda i: (i, 0))
    vec = _const((1, D_MODEL))
    return _call(
        body, "in_bwd", (s_len // tt,),
        in_specs=[row(IN_COLS), _whole(), row(D_MODEL), row(D_MODEL), vec, vec],
        out_specs=[row(D_MODEL), _const((SUBLANES, D_MODEL))],
        out_shape=[_sds((s_len, D_MODEL), F32), _sds((SUBLANES, D_MODEL), F32)],
        scratch=[], args=(d_z, w_in, x, d_x1, g, sc), carry=carry)


def _wgrad(a, b, tn, name, carry=None):
    s_len, k_dim = a.shape
    halves = b.ndim == 3
    n_dim = b.shape[-1] * (2 if halves else 1)
    ts = min(TT_WG, s_len)
    nj = n_dim // tn
    nt = s_len // ts

    def body(a_ref, b_ref, o_ref, ob_ref):
        t = pl.program_id(1)
        part = _dot_tn(a_ref[...], b_ref[0] if halves else b_ref[...])

        @pl.when(t == 0)
        def _():
            o_ref[...] = part

        @pl.when(t > 0)
        def _():
            o_ref[...] += part

        @pl.when(t == nt - 1)
        def _():
            ob_ref[...] = o_ref[...].astype(BF16)

    if halves:
        per_half = nj // 2
        b_spec = pl.BlockSpec((1, ts, tn), lambda j, t: (j // per_half, t, j % per_half))
    else:
        b_spec = pl.BlockSpec((ts, tn), lambda j, t: (t, j))
    o_spec = pl.BlockSpec((k_dim, tn), lambda j, t: (0, j))
    return _call(
        body, name, (nj, nt),
        in_specs=[pl.BlockSpec((ts, k_dim), lambda j, t: (t, 0)), b_spec],
        out_specs=[o_spec, o_spec],
        out_shape=[_sds((k_dim, n_dim), F32), _sds((k_dim, n_dim), BF16)],
        scratch=[], args=(a, b), carry=carry)


def _adam_math(w, g, m, v):
    m = ADAM_B1 * m + (1.0 - ADAM_B1) * g
    v = ADAM_B2 * v + (1.0 - ADAM_B2) * (g * g)
    m_hat = m / (1.0 - ADAM_B1 ** ADAM_STEP)
    v_hat = v / (1.0 - ADAM_B2 ** ADAM_STEP)
    delta = -ADAM_LR * (m_hat / (jnp.sqrt(v_hat) + ADAM_EPS) + ADAM_WD * w)
    return delta, m, v


def _row_tile(rows, cols, n_f32_arrays):
    budget = VMEM_LIMIT // 2
    tr = rows
    while tr % 2 == 0 and tr // 2 >= SUBLANES and (tr // 2) % SUBLANES == 0 and tr * cols * 4 * n_f32_arrays * 2 > budget:
        tr //= 2
    return tr


def _adamw_sum(w, g_full, recv, m, v, col_sharded, name):
    _, rows, cols = w.shape
    n_recv = len(recv)
    tr = _row_tile(rows, cols, 10)
    nb = rows // tr

    def body(me_ref, w_ref, g_ref, *rest):
        r_refs = rest[:n_recv]
        m_ref, v_ref, go_ref, d_ref, mo_ref, vo_ref = rest[n_recv:]
        g = g_ref[...]
        for r_ref in r_refs:
            for k in range(r_ref.shape[0]):
                g = g + r_ref[k].astype(F32)
        go_ref[0] = g
        d_ref[0], mo_ref[0], vo_ref[0] = _adam_math(w_ref[0], g, m_ref[0], v_ref[0])

    if col_sharded:
        own = pl.BlockSpec((tr, cols), lambda i, me: (i, me[0]))
    else:
        own = pl.BlockSpec((tr, cols), lambda i, me: (me[0] * nb + i, 0))
    blk = pl.BlockSpec((1, tr, cols), lambda i, me: (0, i, 0))
    return pl.pallas_call(
        body, name=name,
        grid_spec=pltpu.PrefetchScalarGridSpec(
            num_scalar_prefetch=1, grid=(nb,),
            in_specs=[blk, own] + [pl.BlockSpec((r.shape[0], tr, cols), lambda i, me: (0, i, 0)) for r in recv]
            + [blk, blk],
            out_specs=[blk] * 4),
        out_shape=[_sds((1, rows, cols), F32)] * 4,
        compiler_params=_cparams(("arbitrary",)),
    )(jnp.reshape(_dev_index(_my_pos()), (1,)).astype(jnp.int32), w, g_full, *recv, m, v)


def _row_of_each(ref, row):
    cols = ref.shape[1]
    rows = _rows((N_DEV, cols))
    out = jnp.zeros((N_DEV, cols), F32)
    for d in range(N_DEV):
        picked = ref[d * SUBLANES + row:d * SUBLANES + row + 1, :]
        out = jnp.where(rows == d, jnp.broadcast_to(picked, (N_DEV, cols)), out)
    return out


def _my_columns(full, width, me):
    out = jnp.zeros(full.shape[:-1] + (width,), F32)
    for d in range(N_DEV):
        out = out + jnp.where(me == d, full[:, d * width:(d + 1) * width], 0.0)
    return out


def _adamw_wada(c_all, vs_in_all, vs_up_all, vs_ffn_all, w, m, v):
    _, rows, cols = w.shape

    def body(c_ref, vi_ref, vu_ref, vf_ref, w_ref, m_ref, v_ref, go_ref, d_ref, mo_ref, vo_ref):
        me = _dev_index(_my_pos())
        cv = _row_of_each(c_ref, 0)
        ca = cv * _sigmoid(cv)
        dmod = jnp.concatenate([_row_of_each(vi_ref, 0), _row_of_each(vi_ref, 1), _row_of_each(vu_ref, 3),
                                _row_of_each(vu_ref, 0), _row_of_each(vu_ref, 1), _row_of_each(vf_ref, 0)], axis=1)
        dm = _my_columns(dmod, cols, me)
        g = lax.dot_general(ca, dm, (((0,), (0,)), ((), ())), preferred_element_type=F32,
                            precision=lax.Precision.HIGHEST)
        go_ref[0] = g
        d_ref[0], mo_ref[0], vo_ref[0] = _adam_math(w_ref[0], g, m_ref[0], v_ref[0])

    return pl.pallas_call(
        body, name="adamw_w_ada", out_shape=[_sds((1, rows, cols), F32)] * 4,
        in_specs=[_whole()] * 7, out_specs=[_whole()] * 4,
        compiler_params=_cparams(),
    )(c_all, vs_in_all, vs_up_all, vs_ffn_all, w, m, v)


def _adamw_small(gathered, reduced, params, conv_params):
    names = list(params) + list(conv_params)
    allp = {**params, **conv_params}
    n_g = len(gathered) + len(reduced)

    def body(*refs):
        g_refs = refs[:n_g]
        p_refs = refs[n_g:n_g + 3 * len(names)]
        o_refs = refs[n_g + 3 * len(names):]
        me = _dev_index(_my_pos())

        def total(ref):
            s = ref[0:SUBLANES, :]
            for d in range(1, N_DEV):
                s = s + ref[d * SUBLANES:(d + 1) * SUBLANES, :]
            return s

        vs_in, vs_up, vs_ffn, loss = [total(r) for r in g_refs[:4]]
        cs, vs_mix, dcw, dwr, dwi, dws, dbs = [r[...] for r in g_refs[4:]]
        o_refs[-1][...] = loss[0:1, 0:1]
        mine = lambda full, width: _my_columns(full, width, me)

        all_ = (slice(None), slice(None))
        heads = lambda row: [((0, slice(h, h + 1), slice(None)), row[:, h * HEAD_DIM:(h + 1) * HEAD_DIM])
                             for h in range(N_HEADS)]
        blocks = lambda pairs: [((0, h), pairs[_head_pair_block(h)]) for h in range(N_HEADS)]
        pieces = {
            "b_ada": [((slice(None), slice(k * D_MODEL, (k + 1) * D_MODEL)), row) for k, row in enumerate(
                (vs_in[0:1], vs_in[1:2], vs_up[3:4], vs_up[0:1], vs_up[1:2], vs_ffn[0:1]))],
            "g_mix_pre": [(all_, vs_in[2:3])], "g_mix_post": [(all_, vs_up[4:5])],
            "g_ffn_pre": [(all_, vs_up[2:3])], "g_ffn_post": [(all_, vs_ffn[1:2])],
            "conv_b": [(all_, vs_mix[0:1])], "b_rgate": heads(vs_mix[1:2]), "b_igate": heads(vs_mix[2:3]),
            "lru_a": [(all_, vs_mix[3:4])], "v_norm_g": [(all_, vs_mix[4:5])], "v_norm_b": [(all_, vs_mix[5:6])],
            "g_lru_out": [(all_, vs_mix[6:7])], "g_gmlp_out": [(all_, vs_mix[7:8])],
            "w_rgate": blocks(dwr), "w_igate": blocks(dwi),
            "w_spatial": [((0, g), dws[g * POS_BLOCK:(g + 1) * POS_BLOCK, :]) for g in range(N_GROUPS)],
            "b_spatial": [((0,), dbs[0:N_GROUPS])],
            "ffn_conv_b": [(all_, cs[FFN_CONV_K:FFN_CONV_K + 1])],
            "conv_w": [((0,), mine(dcw[0:LRU_CONV_K], LRU_W // N_DEV))],
            "ffn_conv_w": [((0,), mine(cs[0:FFN_CONV_K], 2 * D_FF // N_DEV))],
        }
        for n_i, name in enumerate(names):
            w_ref, m_ref, v_ref = p_refs[3 * n_i:3 * n_i + 3]
            go_ref, d_ref, mo_ref, vo_ref = o_refs[4 * n_i:4 * n_i + 4]
            for idx, g in pieces[name]:
                go_ref[idx] = g
                d_ref[idx], mo_ref[idx], vo_ref[idx] = _adam_math(w_ref[idx], g, m_ref[idx], v_ref[idx])

    flat_params = [a for n in names for a in allp[n]]
    out_shape = [_sds(allp[n][0].shape, F32) for n in names for _ in range(4)] + [_sds((1, 1), F32)]
    outs = pl.pallas_call(
        body, name="adamw_small", out_shape=out_shape,
        in_specs=[_whole()] * (n_g + len(flat_params)), out_specs=[_whole()] * len(out_shape),
        compiler_params=_cparams(),
    )(*gathered, *reduced, *flat_params)
    return {n: outs[4 * i:4 * i + 4] for i, n in enumerate(names)}, outs[-1]


def _my_pos():
    return lax.axis_index("x"), lax.axis_index("y"), lax.axis_index("c")


def _flip(pos, k):
    x, y, c = pos
    return (1 - x if k & 4 else x, 1 - y if k & 2 else y, 1 - c if k & 1 else c)


def _dev_index(pos):
    x, y, c = pos
    return 4 * x + 2 * y + c


def _all_gather_small(ins, outs, send_sems, recv_sems):
    n = len(ins)
    me = _my_pos()

    def slot(a, pos):
        rows = ins[a].shape[0]
        return outs[a].at[pl.ds(pl.multiple_of(_dev_index(pos) * rows, SUBLANES), rows), :]

    def copy(a, k, block):
        return pltpu.make_async_remote_copy(
            src_ref=ins[a], dst_ref=slot(a, block), send_sem=send_sems.at[a, k - 1], recv_sem=recv_sems.at[a, k - 1],
            device_id=_flip(me, k), device_id_type=MESH)

    sends = [copy(a, k, me) for a in range(n) for k in range(1, N_DEV)]
    for cp in sends:
        cp.start()
    for a in range(n):
        rows = ins[a].shape[0]
        outs[a][pl.ds(pl.multiple_of(_dev_index(me) * rows, SUBLANES), rows), :] = ins[a][...]
    for a in range(n):
        for k in range(1, N_DEV):
            copy(a, k, _flip(me, k)).wait_recv()
    for cp in sends:
        cp.wait_send()


def _prologue(c8, cw8, fcw8, w_ada, b_ada, carry):
    cols = w_ada.shape[1]

    def body(c_ref, cw_ref, fcw_ref, w_ref, b_ref, call_ref, cwall_ref, fcwall_ref, modall_ref, mod_scr,
             s1, r1, s2, r2):
        _all_gather_small([c_ref, cw_ref, fcw_ref], [call_ref, cwall_ref, fcwall_ref], s1, r1)
        cv = _row_of_each(call_ref, 0)
        ca = cv * _sigmoid(cv)
        b_cols = _my_columns(b_ref[...], cols, _dev_index(_my_pos()))
        mod_scr[...] = jnp.dot(ca, w_ref[...], preferred_element_type=F32, precision=lax.Precision.HIGHEST) + b_cols
        _all_gather_small([mod_scr], [modall_ref], s2, r2)

    sem = lambda n: pltpu.SemaphoreType.DMA((n, N_DEV - 1))
    return _call(
        body, "prologue", (1,), in_specs=[_whole()] * 5, out_specs=[_whole()] * 4,
        out_shape=[_sds((N_DEV * SUBLANES, a.shape[1]), F32) for a in (c8, cw8, fcw8)]
        + [_sds((N_DEV * N_DEV, cols), F32)],
        scratch=[pltpu.VMEM((N_DEV, cols), F32), sem(3), sem(3), sem(1), sem(1)],
        args=(c8, cw8, fcw8, w_ada, b_ada), carry=carry)


def _reduce_small(gath, red, carry=None):
    n_g, n_r = len(gath), len(red)
    chip_flips = (4, 2, 6)

    def body(*refs):
        g_in, r_in = refs[:n_g], refs[n_g:n_g + n_r]
        g_out, r_out = refs[n_g + n_r:2 * n_g + n_r], refs[2 * n_g + n_r:2 * (n_g + n_r)]
        scr = refs[2 * (n_g + n_r):]
        sib, land = scr[:n_r], scr[n_r:2 * n_r]
        g_send, g_recv, s_send, s_recv, i_send, i_recv, f_send, f_recv = scr[2 * n_r:]
        me = _my_pos()
        c = me[2]
        sibling = _flip(me, 1)

        def slot(a, pos):
            return g_out[a].at[pl.ds(pl.multiple_of(_dev_index(pos) * SUBLANES, SUBLANES), SUBLANES), :]

        def gcopy(a, k):
            return pltpu.make_async_remote_copy(
                src_ref=g_in[a], dst_ref=slot(a, me), send_sem=g_send.at[a, k - 1], recv_sem=g_recv.at[a, k - 1],
                device_id=_flip(me, k), device_id_type=MESH)

        def scopy(a):
            return pltpu.make_async_remote_copy(
                src_ref=r_in[a], dst_ref=sib[a], send_sem=s_send.at[a], recv_sem=s_recv.at[a],
                device_id=sibling, device_id_type=MESH)

        def icopy(a, j):
            return pltpu.make_async_remote_copy(
                src_ref=r_out[a], dst_ref=land[a].at[j], send_sem=i_send.at[a, j], recv_sem=i_recv.at[a, j],
                device_id=_flip(me, chip_flips[j]), device_id_type=MESH)

        def fcopy(a, j):
            return pltpu.make_async_remote_copy(
                src_ref=land[a].at[j], dst_ref=land[a].at[j], send_sem=f_send.at[a, j], recv_sem=f_recv.at[a, j],
                device_id=sibling, device_id_type=MESH)

        gathers = [gcopy(a, k) for a in range(n_g) for k in range(1, N_DEV)]
        swaps = [scopy(a) for a in range(n_r)]
        for cp in gathers + swaps:
            cp.start()
        for a in range(n_g):
            g_out[a][pl.ds(pl.multiple_of(_dev_index(me) * SUBLANES, SUBLANES), SUBLANES), :] = g_in[a][...]
        for a in range(n_r):
            swaps[a].wait_recv()
            r_out[a][...] = r_in[a][...] + sib[a][...]

        for core in range(2):
            mine = [a for a in range(n_r) if a % 2 == core]
            theirs = [a for a in range(n_r) if a % 2 != core]

            @pl.when(c == core)
            def _():
                out = [icopy(a, j) for a in mine for j in range(3)]
                for cp in out:
                    cp.start()
                fwd = []
                for a in mine:
                    for j in range(3):
                        icopy(a, j).wait_recv()
                        cp = fcopy(a, j)
                        cp.start()
                        fwd.append(cp)
                for a in theirs:
                    for j in range(3):
                        fcopy(a, j).wait_recv()
                for cp in out + fwd:
                    cp.wait_send()

        for a in range(n_r):
            r_out[a][...] = (r_out[a][...] + land[a][1]) + (land[a][0] + land[a][2])
        for a in range(n_g):
            for k in range(1, N_DEV):
                pltpu.make_async_remote_copy(
                    src_ref=g_in[a], dst_ref=slot(a, _flip(me, k)), send_sem=g_send.at[a, k - 1],
                    recv_sem=g_recv.at[a, k - 1], device_id=_flip(me, k), device_id_type=MESH).wait_recv()
        for cp in gathers + swaps:
            cp.wait_send()

    shapes = [tuple(a.shape) for a in red]
    outs, carried = _call(
        body, "reduce_small", (1,), in_specs=[_whole()] * (n_g + n_r), out_specs=[_whole()] * (n_g + n_r),
        out_shape=[_sds((N_DEV * SUBLANES, a.shape[1]), F32) for a in gath] + [_sds(s, F32) for s in shapes],
        scratch=[pltpu.VMEM(s, F32) for s in shapes] + [pltpu.VMEM((3,) + s, F32) for s in shapes]
        + [pltpu.SemaphoreType.DMA((n_g, N_DEV - 1)), pltpu.SemaphoreType.DMA((n_g, N_DEV - 1)),
           pltpu.SemaphoreType.DMA((n_r,)), pltpu.SemaphoreType.DMA((n_r,)),
           pltpu.SemaphoreType.DMA((n_r, 3)), pltpu.SemaphoreType.DMA((n_r, 3)),
           pltpu.SemaphoreType.DMA((n_r, 3)), pltpu.SemaphoreType.DMA((n_r, 3))],
        args=tuple(gath) + tuple(red), carry=carry)
    return (outs[:n_g], outs[n_g:]), carried


STACKED = "stacked"


def _region(ref, shard_shape, col_sharded, pos):
    r, cdim = shard_shape
    d = _dev_index(pos)
    if col_sharded == STACKED:
        return ref.at[d]
    if col_sharded:
        return ref.at[:, pl.ds(pl.multiple_of(d * cdim, LANES), cdim)]
    return ref.at[pl.ds(pl.multiple_of(d * r, 2 * SUBLANES), r), :]


def _gather_carry(shards, col_sharded):
    n_w = len(shards)
    shapes = [tuple(s.shape) for s in shards]
    full_shapes = [(N_DEV,) + s if cs == STACKED else (s[0], s[1] * N_DEV) if cs else (s[0] * N_DEV, s[1])
                   for s, cs in zip(shapes, col_sharded)]

    def tools(out_refs, scr):
        send_sems, recv_sems = scr[n_w], scr[n_w + 1]
        me = _my_pos()
        x, y, c = me
        sibling = (x, y, 1 - c)
        chips = [(1 - x, y), (x, 1 - y), (1 - x, 1 - y)]

        def region(w, pos):
            return _region(out_refs[w], shapes[w], col_sharded[w], pos)

        def copy(w, k, block, to, src=None):
            return pltpu.make_async_remote_copy(
                src_ref=region(w, block) if src is None else src, dst_ref=region(w, block),
                send_sem=send_sems.at[w, k], recv_sem=recv_sems.at[w, k], device_id=to, device_id_type=MESH)

        def first(w):
            return [copy(w, 0, me, sibling, src=scr[w])] + [
                copy(w, 1 + j, me, (*chip, c), src=scr[w]) for j, chip in enumerate(chips)]

        def mine(w):
            return pltpu.make_async_copy(scr[w], region(w, me), scr[n_w + 2].at[w])

        return me, c, sibling, chips, copy, first, mine

    def start(ins, outs, scr):
        _, _, _, _, _, first, mine = tools(outs, scr)
        for w in range(n_w):
            scr[w][...] = ins[w][...].astype(BF16)
            for cp in first(w) + [mine(w)]:
                cp.start()

    def finish(ins, outs, scr):
        me, c, sibling, chips, copy, first, mine = tools(outs, scr)
        passed = []
        for w in range(n_w):
            for j, chip in enumerate(chips):
                copy(w, 1 + j, (*chip, c), me).wait_recv()
                fwd = copy(w, 4 + j, (*chip, c), sibling)
                fwd.start()
                passed.append(fwd)
        for w in range(n_w):
            copy(w, 0, sibling, me).wait_recv()
            for j, chip in enumerate(chips):
                copy(w, 4 + j, (*chip, 1 - c), me).wait_recv()
        for w in range(n_w):
            for cp in first(w):
                cp.wait_send()
            mine(w).wait()
        for cp in passed:
            cp.wait_send()

    return _Carry(
        inputs=list(shards), in_specs=[_whole()] * n_w,
        out_shape=[_sds(s, BF16) for s in full_shapes], out_specs=[_any()] * n_w,
        scratch=[pltpu.VMEM(s, BF16) for s in shapes]
        + [pltpu.SemaphoreType.DMA((n_w, N_DEV - 1)), pltpu.SemaphoreType.DMA((n_w, N_DEV - 1)),
           pltpu.SemaphoreType.DMA((n_w,))],
        start=start, finish=finish)


def _scatter_carry(grads_bf, shard_shapes, col_sharded, relations):
    n_w = len(grads_bf)
    shapes = [tuple(s) for s in shard_shapes]

    def copies(ins, outs, scr):
        send_sems, recv_sems = scr
        me = _my_pos()
        out = []
        for w in range(n_w):
            for i, k in enumerate(relations[w]):
                peer = _flip(me, k)
                out.append(pltpu.make_async_remote_copy(
                    src_ref=_region(ins[w], shapes[w], col_sharded[w], peer), dst_ref=outs[w].at[i],
                    send_sem=send_sems.at[w, i], recv_sem=recv_sems.at[w, i],
                    device_id=peer, device_id_type=MESH))
        return out

    def start(ins, outs, scr):
        for cp in copies(ins, outs, scr):
            cp.start()

    def finish(ins, outs, scr):
        cps = copies(ins, outs, scr)
        for cp in cps:
            cp.wait_recv()
        for cp in cps:
            cp.wait_send()

    return _Carry(
        inputs=list(grads_bf), in_specs=[_any()] * n_w,
        out_shape=[_sds((len(r),) + s, BF16) for r, s in zip(relations, shapes)], out_specs=[_any()] * n_w,
        scratch=[pltpu.SemaphoreType.DMA((n_w, N_DEV - 1)), pltpu.SemaphoreType.DMA((n_w, N_DEV - 1))],
        start=start, finish=finish)


def _block_diag(w):
    eye = jnp.eye(N_HEADS, dtype=w.dtype)
    return (eye[:, None, :, None] * w[:, :, None, :]).reshape(N_HEADS * HEAD_DIM, N_HEADS * HEAD_DIM)


def _pad_rows(a):
    return jnp.pad(a, ((0, SUBLANES - a.shape[0]), (0, 0)))


def _columns_from_devices(gathered, rows):
    w = gathered.shape[1]
    return gathered.reshape(N_DEV, SUBLANES, w)[:, :rows].transpose(1, 0, 2).reshape(rows, N_DEV * w)


def _local_step(x2, target, mod, w_in_f, w_full, conv_w_full, ffn_cw_full,
                g_mix_pre, g_mix_post, conv_b, w_rgate, b_rgate, w_igate, b_igate, lru_a, v_norm_g, v_norm_b,
                w_spatial, b_spatial, g_lru_out, g_gmlp_out, g_ffn_pre, g_ffn_post, ffn_conv_b,
                gather=None, scatter=None):
    sh_m, sc_m, gt_m, sh_f, sc_f, gt_f = [mod[k] for k in range(N_MOD)]
    wr_bd = _block_diag(w_rgate[0]).astype(BF16)
    wi_bd = _block_diag(w_igate[0]).astype(BF16)
    b_r = b_rgate.reshape(1, LRU_W)
    b_i = b_igate.reshape(1, LRU_W)
    b_sp_t = b_spatial[0].T
    w_sp_t = jnp.swapaxes(w_spatial[0], 1, 2)

    def arriving(*names):
        return gather(*names) if gather else None

    near, far = (1, 2, 3, 4, 5), (6, 7)

    def leaving(*parts):
        return scatter(parts) if scatter else None

    def received(recv, parts, outs):
        for (name, _, _), out in zip(parts, outs):
            recv.setdefault(name, []).append(out)

    mix_params = (conv_w_full, conv_b, wr_bd, wi_bd, b_r, b_i, lru_a, v_norm_g, v_norm_b)
    (z, h, ycat, hl), got = _mix_fwd(x2, sh_m, sc_m, g_mix_pre, w_in_f, *mix_params, w_spatial[0], b_sp_t,
                                     g_lru_out, g_gmlp_out, carry=arriving("w_up", "w_down"))
    w_up_f, w_down_f = got if gather else (w_full["w_up"], w_full["w_down"])
    w_out_f = w_full["w_out"]
    (y, x1, h2, up_pre, up, act, d_y2, dout, loss_acc, vs_ffn), _ = _ffn_fwd(
        ycat, x2, w_out_f, g_mix_post, gt_m, g_ffn_pre, sc_f, sh_f, w_up_f, ffn_cw_full, ffn_conv_b, w_down_f,
        gt_f, g_ffn_post, target)

    recv = {}
    gw_down, _ = _wgrad(act, d_y2, D_MODEL // 2, "wgrad_down")
    parts = [("w_down", gw_down[1], near + far)]
    (d_up, cs_ffn), got = _ffn_bwd(d_y2, up_pre, up, ffn_cw_full, w_down_f, carry=leaving(*parts))
    received(recv, parts, got)
    gw_up, _ = _wgrad(h2, d_up, D_FF // 2, "wgrad_up")
    parts = [("w_up", gw_up[1], near)]
    (d_x1, d_y, d_ycat, vs_up), got = _up_bwd(
        d_up, w_up_f, x1, dout, y, w_out_f, g_ffn_pre, sc_f, g_mix_post, gt_m, carry=leaving(*parts))
    received(recv, parts, got)
    gw_out, _ = _wgrad(ycat, d_y, D_MODEL, "wgrad_out")
    parts = [("w_up", gw_up[1], far), ("w_out", gw_out[1], near + far)]
    (d_z, vs_mix, dcw, d_wr, d_wi, d_ws, d_bs), got = _mix_bwd(
        d_ycat, z, hl, *mix_params, w_spatial[0], w_sp_t, b_sp_t, g_lru_out, g_gmlp_out, carry=leaving(*parts))
    received(recv, parts, got)
    gw_in, _ = _wgrad(h, d_z, IN_COLS // 2, "wgrad_in")
    parts = [("w_in", gw_in[1], near)]
    (grad_x, vs_in), got = _in_bwd(d_z, w_in_f, x2, d_x1, g_mix_pre, sc_m, carry=leaving(*parts))
    received(recv, parts, got)
    pending = [("w_in", gw_in[1], far)]

    gath = [vs_in, vs_up, vs_ffn, loss_acc]
    red = [cs_ffn, vs_mix, dcw, d_wr, d_wi, d_ws.reshape(N_GROUPS * POS_BLOCK, POS_BLOCK), d_bs]
    return dict(grad_x=grad_x, gath=gath, red=red, recv=recv, pending=pending,
                w_in=gw_in, w_out=gw_out, w_up=gw_up, w_down=gw_down)


def kernel(x, c, w_ada, b_ada, g_mix_pre, g_mix_post, w_in, conv_w, conv_b, w_rgate, b_rgate, w_igate, b_igate, lru_a, v_norm_g, v_norm_b, w_spatial, b_spatial, g_lru_out, g_gmlp_out, w_out, g_ffn_pre, g_ffn_post, w_up, ffn_conv_w, ffn_conv_b, w_down, loss_target, m_w_ada, m_b_ada, m_g_mix_pre, m_g_mix_post, m_w_in, m_conv_w, m_conv_b, m_w_rgate, m_b_rgate, m_w_igate, m_b_igate, m_lru_a, m_v_norm_g, m_v_norm_b, m_w_spatial, m_b_spatial, m_g_lru_out, m_g_gmlp_out, m_w_out, m_g_ffn_pre, m_g_ffn_post, m_w_up, m_ffn_conv_w, m_ffn_conv_b, m_w_down, v_w_ada, v_b_ada, v_g_mix_pre, v_g_mix_post, v_w_in, v_conv_w, v_conv_b, v_w_rgate, v_b_rgate, v_w_igate, v_b_igate, v_lru_a, v_v_norm_g, v_v_norm_b, v_w_spatial, v_b_spatial, v_g_lru_out, v_g_gmlp_out, v_w_out, v_g_ffn_pre, v_g_ffn_post, v_w_up, v_ffn_conv_w, v_ffn_conv_b, v_w_down):
    me = _dev_index(_my_pos())
    ada_cols = w_ada.shape[-1]

    big_w = dict(w_in=(w_in, m_w_in, v_w_in, True), w_out=(w_out, m_w_out, v_w_out, False),
                 w_up=(w_up, m_w_up, v_w_up, True), w_down=(w_down, m_w_down, v_w_down, False))

    def gather(*names):
        return _gather_carry([big_w[n][0][0] for n in names], [STACKED if n == "w_up" else big_w[n][3] for n in names])

    def scatter(parts):
        return _scatter_carry([g for _, g, _ in parts], [big_w[n][0].shape[1:] for n, _, _ in parts],
                              [big_w[n][3] for n, _, _ in parts], [rel for _, _, rel in parts])

    (c_all, cw_all, fcw_all, mod_all), (w_in_f, w_out_f) = _prologue(
        jnp.broadcast_to(c, (SUBLANES, D_MODEL)), _pad_rows(conv_w[0]), _pad_rows(ffn_conv_w[0]), w_ada[0], b_ada,
        carry=gather("w_in", "w_out"))
    conv_w_full = _columns_from_devices(cw_all, LRU_CONV_K)
    ffn_cw_full = _columns_from_devices(fcw_all, FFN_CONV_K)
    mod = lax.dynamic_index_in_dim(mod_all.reshape(N_DEV, N_DEV, ada_cols), me, axis=1, keepdims=False)
    mod = mod.reshape(N_MOD, 1, D_MODEL)

    loc = _local_step(x[0], loss_target[0], mod, w_in_f, dict(w_out=w_out_f), conv_w_full, ffn_cw_full,
                      g_mix_pre, g_mix_post, conv_b, w_rgate, b_rgate, w_igate, b_igate, lru_a, v_norm_g, v_norm_b,
                      w_spatial, b_spatial, g_lru_out, g_gmlp_out, g_ffn_pre, g_ffn_post, ffn_conv_b,
                      gather=gather, scatter=scatter)
    grad_x = loc["grad_x"]

    (gathered, reduced), got = _reduce_small(loc["gath"], loc["red"], carry=scatter(loc["pending"]))
    for (name, _, _), out in zip(loc["pending"], got):
        loc["recv"][name].append(out)

    results = {}
    for name, (w_, m_, v_, cs) in big_w.items():
        results[name] = _adamw_sum(w_, loc[name][0], loc["recv"][name], m_, v_, cs, "adamw_" + name)

    params = dict(
        b_ada=(b_ada, m_b_ada, v_b_ada), g_mix_pre=(g_mix_pre, m_g_mix_pre, v_g_mix_pre),
        g_mix_post=(g_mix_post, m_g_mix_post, v_g_mix_post), conv_b=(conv_b, m_conv_b, v_conv_b),
        w_rgate=(w_rgate, m_w_rgate, v_w_rgate), b_rgate=(b_rgate, m_b_rgate, v_b_rgate),
        w_igate=(w_igate, m_w_igate, v_w_igate), b_igate=(b_igate, m_b_igate, v_b_igate),
        lru_a=(lru_a, m_lru_a, v_lru_a), v_norm_g=(v_norm_g, m_v_norm_g, v_v_norm_g),
        v_norm_b=(v_norm_b, m_v_norm_b, v_v_norm_b), w_spatial=(w_spatial, m_w_spatial, v_w_spatial),
        b_spatial=(b_spatial, m_b_spatial, v_b_spatial), g_lru_out=(g_lru_out, m_g_lru_out, v_g_lru_out),
        g_gmlp_out=(g_gmlp_out, m_g_gmlp_out, v_g_gmlp_out), g_ffn_pre=(g_ffn_pre, m_g_ffn_pre, v_g_ffn_pre),
        g_ffn_post=(g_ffn_post, m_g_ffn_post, v_g_ffn_post), ffn_conv_b=(ffn_conv_b, m_ffn_conv_b, v_ffn_conv_b))
    conv_params = dict(conv_w=(conv_w, m_conv_w, v_conv_w), ffn_conv_w=(ffn_conv_w, m_ffn_conv_w, v_ffn_conv_w))
    small_results, loss = _adamw_small(gathered, reduced, params, conv_params)
    results.update(small_results)
    loss = loss.reshape(())

    results["w_ada"] = _adamw_wada(c_all, gathered[0], gathered[1], gathered[2], w_ada, m_w_ada, v_w_ada)

    order = ["w_ada", "b_ada", "g_mix_pre", "g_mix_post", "w_in", "conv_w", "conv_b", "w_rgate", "b_rgate", "w_igate",
             "b_igate", "lru_a", "v_norm_g", "v_norm_b", "w_spatial", "b_spatial", "g_lru_out", "g_gmlp_out", "w_out",
             "g_ffn_pre", "g_ffn_post", "w_up", "ffn_conv_w", "ffn_conv_b", "w_down"]
    outs = [loss, grad_x[None]]
    for kind in range(4):
        outs += [results[n][kind] for n in order]
    return tuple(outs)
```

```python
import functools

import jax
import jax.numpy as jnp
from jax import lax
from jax.experimental import pallas as pl
from jax.experimental.pallas import tpu as pltpu

F32 = jnp.float32
BF16 = jnp.bfloat16

D_MODEL = 1024
LRU_W = 512
GMLP_W = 512
N_HEADS = 8
HEAD_DIM = 64
N_GROUPS = 4
POS_BLOCK = 128
CHUNK = 64
IN_COLS = 2048
D_FF = 3072
N_MOD = 6
N_DEV = 8
EPS = 1e-6
LRU_C = 8.0
LRU_CONV_K = 4
FFN_CONV_K = 3

ADAM_LR = 0.001
ADAM_B1 = 0.9
ADAM_B2 = 0.999
ADAM_EPS = 1e-08
ADAM_WD = 0.01
ADAM_STEP = 10

LANES = 128
SUBLANES = 8
TT_BIG = 512
TT_MIX = 256
TT_WG = 1024
VMEM_LIMIT = 56 * 1024 * 1024

MESH = pl.DeviceIdType.MESH


def _sds(shape, dtype):
    return jax.ShapeDtypeStruct(shape, dtype)


def _cparams(sem=None):
    return pltpu.CompilerParams(dimension_semantics=sem, vmem_limit_bytes=VMEM_LIMIT)


def _whole():
    return pl.BlockSpec(memory_space=pltpu.VMEM)


def _const(shape):
    nd = len(shape)
    return pl.BlockSpec(shape, lambda *_: (0,) * nd)


def _any():
    return pl.BlockSpec(memory_space=pl.ANY)


class _Carry:
    def __init__(self, inputs, in_specs, out_shape, out_specs, scratch, start, finish):
        self.inputs, self.in_specs, self.out_shape, self.out_specs = inputs, in_specs, out_shape, out_specs
        self.scratch, self.start, self.finish = scratch, start, finish


def _call(body, name, grid, in_specs, out_specs, out_shape, scratch, args, carry=None):
    n_in, n_out, n_scr = len(in_specs), len(out_specs), len(scratch)
    c_in = len(carry.in_specs) if carry else 0
    c_out = len(carry.out_specs) if carry else 0

    def full_body(*refs):
        ins = refs[:n_in]
        c_ins = refs[n_in:n_in + c_in]
        outs = refs[n_in + c_in:n_in + c_in + n_out]
        c_outs = refs[n_in + c_in + n_out:n_in + c_in + n_out + c_out]
        scr = refs[n_in + c_in + n_out + c_out:n_in + c_in + n_out + c_out + n_scr]
        c_scr = refs[n_in + c_in + n_out + c_out + n_scr:]
        if carry:
            first = functools.reduce(lambda a, b: a & b, [pl.program_id(d) == 0 for d in range(len(grid))])
            last = functools.reduce(lambda a, b: a & b, [pl.program_id(d) == g - 1 for d, g in enumerate(grid)])

            @pl.when(first)
            def _():
                carry.start(c_ins, c_outs, c_scr)

        body(*ins, *outs, *scr)
        if carry:
            @pl.when(last)
            def _():
                carry.finish(c_ins, c_outs, c_scr)

    res = pl.pallas_call(
        full_body, name=name, grid=grid,
        in_specs=list(in_specs) + (list(carry.in_specs) if carry else []),
        out_specs=list(out_specs) + (list(carry.out_specs) if carry else []),
        out_shape=list(out_shape) + (list(carry.out_shape) if carry else []),
        scratch_shapes=list(scratch) + (list(carry.scratch) if carry else []),
        compiler_params=_cparams(("arbitrary",) * len(grid)),
    )(*args, *(carry.inputs if carry else []))
    return res[:n_out], res[n_out:]


def _gelu(x):
    u = 0.7978845608028654 * (x + 0.044715 * x * x * x)
    return 0.5 * x * (1.0 + jnp.tanh(u))


def _gelu_and_grad(x):
    x2 = x * x
    u = 0.7978845608028654 * (x + 0.044715 * x * x2)
    t = jnp.tanh(u)
    g = 0.5 * x * (1.0 + t)
    dg = 0.5 * (1.0 + t) + 0.5 * x * (1.0 - t * t) * 0.7978845608028654 * (1.0 + 3.0 * 0.044715 * x2)
    return g, dg


def _sigmoid(x):
    return 1.0 / (1.0 + jnp.exp(-x))


def _softplus(x):
    return jnp.maximum(x, 0.0) + jnp.log1p(jnp.exp(-jnp.abs(x)))


def _neg_expm1(x):
    series = -x * (1.0 + x * (0.5 + x * (1.0 / 6.0 + x * (1.0 / 24.0 + x * (1.0 / 120.0)))))
    return jnp.where(x > -0.1, series, 1.0 - jnp.exp(x))


def _dot(a, b):
    return jnp.dot(a.astype(BF16), b.astype(BF16), preferred_element_type=F32)


def _dot_nt(a, b):
    return lax.dot_general(a.astype(BF16), b.astype(BF16), (((1,), (1,)), ((), ())), preferred_element_type=F32)


def _dot_tn(a, b):
    return lax.dot_general(a.astype(BF16), b.astype(BF16), (((0,), (0,)), ((), ())), preferred_element_type=F32)


def _rows(shape):
    return lax.broadcasted_iota(jnp.int32, shape, 0)


def _shift_down(cur, prev8, s):
    if s == 0:
        return cur
    n = cur.shape[0]
    r = pltpu.roll(cur, s, 0)
    p = pltpu.roll(prev8, s, 0)
    top = jnp.where(_rows(p.shape) < s, p, r[0:SUBLANES])
    if n == SUBLANES:
        return top
    return jnp.concatenate([top, r[SUBLANES:]], axis=0)


def _shift_up(cur, next8, s):
    if s == 0:
        return cur
    n = cur.shape[0]
    r = pltpu.roll(cur, n - s, 0)
    q = pltpu.roll(next8, SUBLANES - s, 0)
    bot = jnp.where(_rows(q.shape) >= SUBLANES - s, q, r[n - SUBLANES:])
    if n == SUBLANES:
        return bot
    return jnp.concatenate([r[:n - SUBLANES], bot], axis=0)


def _scan_fwd(a, b):
    n = a.shape[0]
    rows = _rows(a.shape)
    s = 1
    while s < n:
        a_s = pltpu.roll(a, s, 0)
        b_s = pltpu.roll(b, s, 0)
        m = rows >= s
        b = jnp.where(m, a * b_s + b, b)
        a = jnp.where(m, a * a_s, a)
        s *= 2
    return a, b


def _scan_rev(a, b):
    n = a.shape[0]
    rows = _rows(a.shape)
    s = 1
    while s < n:
        a_s = pltpu.roll(a, n - s, 0)
        b_s = pltpu.roll(b, n - s, 0)
        m = rows < n - s
        b = jnp.where(m, b + a * b_s, b)
        a = jnp.where(m, a * a_s, a)
        s *= 2
    return a, b


def _rms(x):
    r = lax.rsqrt(jnp.mean(x * x, axis=-1, keepdims=True) + EPS)
    return x * r, r


def _rms_bwd(d_n, n, r):
    return r * (d_n - n * jnp.mean(d_n * n, axis=-1, keepdims=True))


def _colsum(x):
    return jnp.sum(x, axis=0, keepdims=True)


def _lru_gates(xc, wr_ref, wi_ref, br, bi, sp_a):
    r = _sigmoid(_dot(xc, wr_ref[...]) + br)
    i = _sigmoid(_dot(xc, wi_ref[...]) + bi)
    la = -LRU_C * r * sp_a
    a = jnp.exp(la)
    mult = jnp.sqrt(_neg_expm1(2.0 * la))
    return r, i, a, mult


def _lru_conv(lx, prev8, cw_ref, cb):
    xc = cb + cw_ref[LRU_CONV_K - 1:LRU_CONV_K, :] * lx
    taps = []
    for k in range(LRU_CONV_K - 1):
        tap = _shift_down(lx, prev8, LRU_CONV_K - 1 - k)
        taps.append(tap)
        xc = xc + cw_ref[k:k + 1, :] * tap
    return xc, taps


def _ws_mask(transposed=False):
    i = lax.broadcasted_iota(jnp.int32, (POS_BLOCK, POS_BLOCK), 0)
    j = lax.broadcasted_iota(jnp.int32, (POS_BLOCK, POS_BLOCK), 1)
    if transposed:
        i, j = j, i
    return (j // CHUNK) <= (i // CHUNK)


def _gmlp_v(gv, vg, vb):
    av, dav = _gelu_and_grad(gv)
    mu = jnp.mean(av, axis=-1, keepdims=True)
    cen = av - mu
    rs = lax.rsqrt(jnp.mean(cen * cen, axis=-1, keepdims=True) + EPS)
    vhat = cen * rs
    return vhat * vg + vb, vhat, rs, dav


def _mix_fwd(x, sh, sc, g_pre, w_in, conv_w, conv_b, wr_bd, wi_bd, b_r, b_i, lru_a, vn_g, vn_b, w_sp, b_sp_t,
             g_lru, g_gmlp, carry=None):
    s_len = x.shape[0]
    tt = min(TT_MIX, s_len)
    nblk = tt // POS_BLOCK

    def body(x_ref, sh_ref, sc_ref, g_ref, w_ref, cw_ref, cb_ref, wr_ref, wi_ref, br_ref, bi_ref, la_ref, vg_ref,
             vb_ref, ws_ref, bst_ref, gl_ref, gg_ref, z_ref, h_ref, y_ref, hl_ref, prev8, hcar):
        i = pl.program_id(0)

        @pl.when(i == 0)
        def _():
            prev8[...] = jnp.zeros_like(prev8)
            hcar[...] = jnp.zeros_like(hcar)

        n_x, _ = _rms(x_ref[...])
        h = (n_x * g_ref[...] * (1.0 + sc_ref[...]) + sh_ref[...]).astype(BF16)
        h_ref[...] = h
        z_ref[...] = jnp.dot(h, w_ref[...], preferred_element_type=F32)

        lx = z_ref[:, 0:LRU_W]
        gate = z_ref[:, LRU_W:2 * LRU_W]
        gu = z_ref[:, 2 * LRU_W:2 * LRU_W + GMLP_W]
        gv = z_ref[:, 2 * LRU_W + GMLP_W:]

        xc, _ = _lru_conv(lx, prev8[...], cw_ref, cb_ref[...])
        prev8[...] = lx[tt - SUBLANES:]
        sp_a = _softplus(-la_ref[...])
        _, ig, a, mult = _lru_gates(xc, wr_ref, wi_ref, br_ref[...], bi_ref[...], sp_a)
        bx = mult * (ig * xc)
        a_cum, b_cum = _scan_fwd(a, bx)
        hl = a_cum * hcar[0:1, :] + b_cum
        hcar[...] = jnp.broadcast_to(hl[tt - 1:tt, :], hcar.shape)
        hl_ref[...] = hl
        y_lru = hl * _gelu(gate)
        n_l, _ = _rms(y_lru)
        y_ref[:, 0:LRU_W] = (n_l * gl_ref[...]).astype(BF16)

        u = _gelu(gu)
        v, _, _, _ = _gmlp_v(gv, vg_ref[...], vb_ref[...])
        mask = _ws_mask()
        sp_parts = []
        for nb in range(nblk):
            row = []
            for g in range(N_GROUPS):
                wsm = jnp.where(mask, ws_ref[g], 0.0)
                vblk = v[nb * POS_BLOCK:(nb + 1) * POS_BLOCK, g * LANES:(g + 1) * LANES]
                row.append(_dot(wsm, vblk) + bst_ref[:, g:g + 1])
            sp_parts.append(jnp.concatenate(row, axis=1))
        sp = jnp.concatenate(sp_parts, axis=0) if nblk > 1 else sp_parts[0]
        n_g, _ = _rms(u * sp)
        y_ref[:, LRU_W:] = (n_g * gg_ref[...]).astype(BF16)

    row = lambda c: pl.BlockSpec((tt, c), lambda i: (i, 0))
    v512 = _const((1, LRU_W))
    vec = _const((1, D_MODEL))
    return _call(
        body, "mix_fwd", (s_len // tt,),
        in_specs=[row(D_MODEL), vec, vec, vec, _whole(),
                  _const((LRU_CONV_K, LRU_W)), v512, _whole(), _whole(), v512, v512, v512, v512, v512,
                  _whole(), _whole(), v512, v512],
        out_specs=[row(IN_COLS), row(D_MODEL), row(LRU_W + GMLP_W), row(LRU_W)],
        out_shape=[_sds((s_len, IN_COLS), F32), _sds((s_len, D_MODEL), BF16),
                   _sds((s_len, LRU_W + GMLP_W), BF16), _sds((s_len, LRU_W), F32)],
        scratch=[pltpu.VMEM((SUBLANES, LRU_W), F32), pltpu.VMEM((SUBLANES, LRU_W), F32)],
        args=(x, sh, sc, g_pre, w_in, conv_w, conv_b, wr_bd, wi_bd, b_r, b_i, lru_a, vn_g, vn_b, w_sp, b_sp_t,
              g_lru, g_gmlp), carry=carry)


FF_CHUNKS = N_DEV // 2
FF_CHUNK_W = D_FF // FF_CHUNKS


def _ffn_fwd(ycat, x, w_out, g_mix_post, gt_m, g_pre, sc_f, sh_f, w_up3, ffn_cw, ffn_cb, w_down, gt_f, g_post,
             target, carry=None):
    s_len = x.shape[0]
    tt = min(TT_MIX, s_len)
    nc, cw = FF_CHUNKS, FF_CHUNK_W

    def body(yc_ref, x_ref, wo_ref, gmp_ref, gtm_ref, g2_ref, sc_ref, sh_ref, wu_ref, cwg_ref, cwv_ref, cbg_ref,
             cbv_ref, wd_ref, gtf_ref, gp_ref, tg_ref,
             y_ref, x1_ref, h2_ref, up_ref, upc_ref, act_ref, dy2_ref, dout_ref, loss_ref, vs_ref,
             h2s, acc, prev):
        i = pl.program_id(0)
        c = pl.program_id(1)

        @pl.when(i == 0)
        def _():
            prev[c] = jnp.zeros((2, SUBLANES, cw), F32)

        @pl.when((i == 0) & (c == 0))
        def _():
            loss_ref[...] = jnp.zeros_like(loss_ref)
            vs_ref[...] = jnp.zeros_like(vs_ref)

        @pl.when(c == 0)
        def _():
            y = jnp.dot(yc_ref[...], wo_ref[...], preferred_element_type=F32)
            y_ref[...] = y
            n_y, _ = _rms(y)
            x1 = x_ref[...] + gtm_ref[...] * (n_y * gmp_ref[...])
            x1_ref[...] = x1
            n1, _ = _rms(x1)
            h2 = (n1 * g2_ref[...] * (1.0 + sc_ref[...]) + sh_ref[...]).astype(BF16)
            h2_ref[...] = h2
            h2s[...] = h2

        h2 = h2s[...]
        ug_pre = jnp.dot(h2, wu_ref[c], preferred_element_type=F32)
        uv_pre = jnp.dot(h2, wu_ref[nc + c], preferred_element_type=F32)
        up_ref[0] = ug_pre
        up_ref[1] = uv_pre
        ug, _ = _ffn_conv(ug_pre, prev[c, 0], cwg_ref, cbg_ref[...])
        uv, _ = _ffn_conv(uv_pre, prev[c, 1], cwv_ref, cbv_ref[...])
        prev[c, 0] = ug_pre[tt - SUBLANES:, :]
        prev[c, 1] = uv_pre[tt - SUBLANES:, :]
        upc_ref[0] = ug
        upc_ref[1] = uv
        act = (_gelu(ug) * uv).astype(BF16)
        act_ref[...] = act
        part = jnp.dot(act, wd_ref[pl.ds(pl.multiple_of(c * cw, cw), cw), :], preferred_element_type=F32)

        @pl.when(c == 0)
        def _():
            acc[...] = part

        @pl.when(c > 0)
        def _():
            acc[...] += part

        @pl.when(c == nc - 1)
        def _():
            n2, r2 = _rms(acc[...])
            out = x1_ref[...] + gtf_ref[...] * (n2 * gp_ref[...])
            err = out - tg_ref[...]
            do = err * (1.0 / D_MODEL)
            dout_ref[...] = do
            loss_ref[...] += jnp.broadcast_to(0.5 * jnp.sum(err * err, keepdims=True) * (1.0 / D_MODEL), loss_ref.shape)
            vs_ref[0:1, :] += _colsum(do * n2 * gp_ref[...])
            vs_ref[1:2, :] += _colsum(do * gtf_ref[...] * n2)
            dy2_ref[...] = _rms_bwd(do * gtf_ref[...] * gp_ref[...], n2, r2).astype(BF16)

    row = pl.BlockSpec((tt, D_MODEL), lambda i, c: (i, 0))
    vec = _const((1, D_MODEL))
    chunk2 = pl.BlockSpec((2, tt, cw), lambda i, c: (0, i, c))
    ffn_cb2 = ffn_cb.reshape(1, 2 * D_FF)
    return _call(
        body, "ffn_fwd", (s_len // tt, nc),
        in_specs=[row, row, _whole(), vec, vec, vec, vec, vec, _whole(),
                  pl.BlockSpec((FFN_CONV_K, cw), lambda i, c: (0, c)),
                  pl.BlockSpec((FFN_CONV_K, cw), lambda i, c: (0, c + nc)),
                  pl.BlockSpec((1, cw), lambda i, c: (0, c)),
                  pl.BlockSpec((1, cw), lambda i, c: (0, c + nc)),
                  _whole(), vec, vec, row],
        out_specs=[row, row, row, chunk2, chunk2, pl.BlockSpec((tt, cw), lambda i, c: (i, c)), row, row,
                   _const((SUBLANES, LANES)), _const((SUBLANES, D_MODEL))],
        out_shape=[_sds((s_len, D_MODEL), F32), _sds((s_len, D_MODEL), F32), _sds((s_len, D_MODEL), BF16),
                   _sds((2, s_len, D_FF), F32), _sds((2, s_len, D_FF), F32), _sds((s_len, D_FF), BF16),
                   _sds((s_len, D_MODEL), BF16), _sds((s_len, D_MODEL), F32),
                   _sds((SUBLANES, LANES), F32), _sds((SUBLANES, D_MODEL), F32)],
        scratch=[pltpu.VMEM((tt, D_MODEL), BF16), pltpu.VMEM((tt, D_MODEL), F32),
                 pltpu.VMEM((nc, 2, SUBLANES, cw), F32)],
        args=(ycat, x, w_out, g_mix_post, gt_m, g_pre, sc_f, sh_f, w_up3, ffn_cw, ffn_cw, ffn_cb2, ffn_cb2, w_down,
              gt_f, g_post, target), carry=carry)


def _ffn_conv(up_pre, prev8, cw_ref, cb):
    up = cb + cw_ref[FFN_CONV_K - 1:FFN_CONV_K, :] * up_pre
    taps = []
    for k in range(FFN_CONV_K - 1):
        tap = _shift_down(up_pre, prev8, FFN_CONV_K - 1 - k)
        taps.append(tap)
        up = up + cw_ref[k:k + 1, :] * tap
    return up, taps


def _ffn_bwd(d_y2, up_pre, up, ffn_cw, w_down, w_up3, x1, dout, y, w_out, g_pre, sc_f, g_mix_post, gt_m, carry=None):
    s_len = d_y2.shape[0]
    tt = min(TT_MIX, s_len)
    nt = s_len // tt
    nc, cw = FF_CHUNKS, FF_CHUNK_W

    def body(dy2_ref, up_ref, upc_ref, cwg_ref, cwv_ref, wd_ref, wu_ref, x1_ref, do_ref, y_ref, wo_ref, g2_ref,
             sc_ref, gp_ref, gt_ref,
             dup_ref, cs_ref, dx1_ref, dy_ref, dyc_ref, vs_ref, nxt, cs_acc, acc):
        i = pl.program_id(0)
        c = pl.program_id(1)

        @pl.when(i == 0)
        def _():
            nxt[c] = jnp.zeros((2, SUBLANES, cw), F32)
            cs_acc[c] = jnp.zeros((2, SUBLANES, cw), F32)

        @pl.when((i == 0) & (c == 0))
        def _():
            vs_ref[...] = jnp.zeros_like(vs_ref)

        d_act = _dot_nt(dy2_ref[...], wd_ref[pl.ds(pl.multiple_of(c * cw, cw), cw), :])
        uv = upc_ref[1]
        gl, dgl = _gelu_and_grad(upc_ref[0])
        d_ug = d_act * uv * dgl
        d_uv = d_act * gl
        part = None
        for half, (d_u, cw_ref) in enumerate(((d_ug, cwg_ref), (d_uv, cwv_ref))):
            nx = nxt[c, half]
            x_in = up_ref[half]
            d_pre = cw_ref[FFN_CONV_K - 1:FFN_CONV_K, :] * d_u
            sums = [None] * (FFN_CONV_K + 1)
            sums[FFN_CONV_K - 1] = _colsum(d_u * x_in)
            for k in range(FFN_CONV_K - 1):
                ahead = _shift_up(d_u, nx, FFN_CONV_K - 1 - k)
                d_pre = d_pre + cw_ref[k:k + 1, :] * ahead
                sums[k] = _colsum(ahead * x_in)
            sums[FFN_CONV_K] = _colsum(d_u)
            pad = jnp.zeros((SUBLANES - FFN_CONV_K - 1, cw), F32)
            cs_acc[c, half] += jnp.concatenate(sums + [pad], axis=0)
            nxt[c, half] = d_u[0:SUBLANES]
            d_pre = d_pre.astype(BF16)
            dup_ref[half] = d_pre
            term = _dot_nt(d_pre, wu_ref[half * nc + c])
            part = term if part is None else part + term

        @pl.when(c == 0)
        def _():
            acc[...] = part

        @pl.when(c > 0)
        def _():
            acc[...] += part

        for cc in range(nc):
            @pl.when((i == nt - 1) & (c == cc))
            def _():
                cs_ref[:, cc * cw:(cc + 1) * cw] = cs_acc[cc, 0]
                cs_ref[:, D_FF + cc * cw:D_FF + (cc + 1) * cw] = cs_acc[cc, 1]

        @pl.when(c == nc - 1)
        def _():
            d_h2 = acc[...]
            n1, r1 = _rms(x1_ref[...])
            ng = n1 * g2_ref[...]
            vs_ref[0:1, :] += _colsum(d_h2)
            vs_ref[1:2, :] += _colsum(d_h2 * ng)
            d_ng = d_h2 * (1.0 + sc_ref[...])
            vs_ref[2:3, :] += _colsum(d_ng * n1)
            d_x1 = do_ref[...] + _rms_bwd(d_ng * g2_ref[...], n1, r1)
            dx1_ref[...] = d_x1
            n_y, r_y = _rms(y_ref[...])
            vs_ref[3:4, :] += _colsum(d_x1 * n_y * gp_ref[...])
            d_on = d_x1 * gt_ref[...]
            vs_ref[4:5, :] += _colsum(d_on * n_y)
            d_y = _rms_bwd(d_on * gp_ref[...], n_y, r_y).astype(BF16)
            dy_ref[...] = d_y
            dyc_ref[...] = _dot_nt(d_y, wo_ref[...])

    row = pl.BlockSpec((tt, D_MODEL), lambda i, c: (nt - 1 - i, 0))
    blk = pl.BlockSpec((2, tt, cw), lambda i, c: (0, nt - 1 - i, c))
    vec = _const((1, D_MODEL))
    return _call(
        body, "ffn_bwd", (nt, nc),
        in_specs=[row, blk, blk,
                  pl.BlockSpec((FFN_CONV_K, cw), lambda i, c: (0, c)),
                  pl.BlockSpec((FFN_CONV_K, cw), lambda i, c: (0, c + nc)),
                  _whole(), _whole(), row, row, row, _whole(), vec, vec, vec, vec],
        out_specs=[blk, _const((SUBLANES, 2 * D_FF)), row, row, row, _const((SUBLANES, D_MODEL))],
        out_shape=[_sds((2, s_len, D_FF), BF16), _sds((SUBLANES, 2 * D_FF), F32), _sds((s_len, D_MODEL), F32),
                   _sds((s_len, D_MODEL), BF16), _sds((s_len, LRU_W + GMLP_W), F32), _sds((SUBLANES, D_MODEL), F32)],
        scratch=[pltpu.VMEM((nc, 2, SUBLANES, cw), F32), pltpu.VMEM((nc, 2, SUBLANES, cw), F32),
                 pltpu.VMEM((tt, D_MODEL), F32)],
        args=(d_y2, up_pre, up, ffn_cw, ffn_cw, w_down, w_up3, x1, dout, y, w_out, g_pre, sc_f, g_mix_post, gt_m),
        carry=carry)


def _head_pair_block(hd):
    return (slice((hd // 2) * HEAD_DIM, (hd // 2 + 1) * HEAD_DIM), slice((hd % 2) * HEAD_DIM, (hd % 2 + 1) * HEAD_DIM))


def _mix_bwd(d_ycat, z, hl, conv_w, conv_b, wr_bd, wi_bd, b_r, b_i, lru_a, vn_g, vn_b, w_sp, w_sp_t, b_sp_t,
             g_lru, g_gmlp, carry=None):
    s_len = z.shape[0]
    tt = min(TT_MIX, s_len)
    nt = s_len // tt
    nblk = tt // POS_BLOCK
    hb = tt // SUBLANES

    def body(dyc_ref, z_ref, zh_ref, hl_ref, hh_ref, cw_ref, cb_ref, wr_ref, wi_ref, br_ref, bi_ref, la_ref,
             vg_ref, vb_ref, ws_ref, wst_ref, bst_ref, gl_ref, gg_ref,
             dz_ref, vs_ref, dcw_ref, dwrb_ref, dwib_ref, dws_ref, dbs_ref, nxt_dxc, nxt_a, nxt_lam, dwr_ref, dwi_ref):
        i = pl.program_id(0)
        first_tile = i == nt - 1

        @pl.when(i == 0)
        def _():
            for ref in (vs_ref, dcw_ref, dwr_ref, dwi_ref, dws_ref, dbs_ref, nxt_dxc, nxt_a, nxt_lam):
                ref[...] = jnp.zeros_like(ref)

        lx = z_ref[:, 0:LRU_W]
        gate = z_ref[:, LRU_W:2 * LRU_W]
        gu = z_ref[:, 2 * LRU_W:2 * LRU_W + GMLP_W]
        gv = z_ref[:, 2 * LRU_W + GMLP_W:]
        prev8 = jnp.where(first_tile, 0.0, zh_ref[...])
        hprev8 = jnp.where(first_tile, 0.0, hh_ref[...])

        xc, taps = _lru_conv(lx, prev8, cw_ref, cb_ref[...])
        a_par = la_ref[...]
        sp_a = _softplus(-a_par)
        r, ig, a, mult = _lru_gates(xc, wr_ref, wi_ref, br_ref[...], bi_ref[...], sp_a)
        hl = hl_ref[...]
        h_prev = _shift_down(hl, hprev8, 1)
        ggate, dggate = _gelu_and_grad(gate)
        y_lru = hl * ggate
        n_l, r_l = _rms(y_lru)
        d_nl = dyc_ref[:, 0:LRU_W]
        vs_ref[6:7, :] += _colsum(d_nl * n_l)
        d_yl = _rms_bwd(d_nl * gl_ref[...], n_l, r_l)
        d_hl = d_yl * ggate
        d_gate = d_yl * hl * dggate
        a_up = _shift_up(a, nxt_a[...], 1)
        a_cum, b_cum = _scan_rev(a_up, d_hl)
        lam = b_cum + a_cum * nxt_lam[0:1, :]
        nxt_a[...] = jnp.broadcast_to(a[0:1, :], nxt_a.shape)
        nxt_lam[...] = jnp.broadcast_to(lam[0:1, :], nxt_lam.shape)
        ixc = ig * xc
        d_la = lam * h_prev * a - lam * ixc * (a * a) / mult
        d_i = lam * mult * xc
        d_xc = lam * mult * ig
        vs_ref[3:4, :] += _colsum(d_la * r) * (LRU_C * _sigmoid(-a_par))
        d_pr = d_la * (-LRU_C * sp_a) * r * (1.0 - r)
        d_pi = d_i * ig * (1.0 - ig)
        vs_ref[1:2, :] += _colsum(d_pr)
        vs_ref[2:3, :] += _colsum(d_pi)
        dwr_ref[...] += _dot_tn(xc, d_pr)
        dwi_ref[...] += _dot_tn(xc, d_pi)
        d_xc = d_xc + _dot_nt(d_pr, wr_ref[...]) + _dot_nt(d_pi, wi_ref[...])
        vs_ref[0:1, :] += _colsum(d_xc)
        nx = nxt_dxc[...]
        d_lx = cw_ref[LRU_CONV_K - 1:LRU_CONV_K, :] * d_xc
        dcw_ref[LRU_CONV_K - 1:LRU_CONV_K, :] += _colsum(d_xc * lx)
        for k in range(LRU_CONV_K - 1):
            d_lx = d_lx + cw_ref[k:k + 1, :] * _shift_up(d_xc, nx, LRU_CONV_K - 1 - k)
            dcw_ref[k:k + 1, :] += _colsum(d_xc * taps[k])
        nxt_dxc[...] = d_xc[0:SUBLANES]
        dz_ref[:, 0:LRU_W] = d_lx.astype(BF16)
        dz_ref[:, LRU_W:2 * LRU_W] = d_gate.astype(BF16)

        u, du = _gelu_and_grad(gu)
        v, vhat, rs, dav = _gmlp_v(gv, vg_ref[...], vb_ref[...])
        mask = _ws_mask()
        sp_parts = []
        for nb in range(nblk):
            rowp = []
            for g in range(N_GROUPS):
                wsm = jnp.where(mask, ws_ref[g], 0.0)
                vblk = v[nb * POS_BLOCK:(nb + 1) * POS_BLOCK, g * LANES:(g + 1) * LANES]
                rowp.append(_dot(wsm, vblk) + bst_ref[:, g:g + 1])
            sp_parts.append(jnp.concatenate(rowp, axis=1))
        sp = jnp.concatenate(sp_parts, axis=0) if nblk > 1 else sp_parts[0]
        y_g = u * sp
        n_g, r_g = _rms(y_g)
        d_ng = dyc_ref[:, LRU_W:]
        vs_ref[7:8, :] += _colsum(d_ng * n_g)
        d_yg = _rms_bwd(d_ng * gg_ref[...], n_g, r_g)
        d_gu = d_yg * sp * du
        d_sp = d_yg * u
        mask_t = _ws_mask(transposed=True)
        ones8 = jnp.ones((SUBLANES, LANES), F32)
        dv_parts = []
        for nb in range(nblk):
            rowp = []
            for g in range(N_GROUPS):
                rs_, cs_ = slice(nb * POS_BLOCK, (nb + 1) * POS_BLOCK), slice(g * LANES, (g + 1) * LANES)
                dsp_blk = d_sp[rs_, cs_]
                dbs_ref[g:g + 1, :] += lax.dot_general(
                    ones8, dsp_blk, (((1,), (1,)), ((), ())), preferred_element_type=F32,
                    precision=lax.Precision.HIGHEST)[0:1, :]
                dws_ref[g] += _dot_nt(dsp_blk, v[rs_, cs_])
                wsm_t = jnp.where(mask_t, wst_ref[g], 0.0)
                rowp.append(_dot(wsm_t, dsp_blk))
            dv_parts.append(jnp.concatenate(rowp, axis=1))
        d_v = jnp.concatenate(dv_parts, axis=0) if nblk > 1 else dv_parts[0]
        vs_ref[4:5, :] += _colsum(d_v * vhat)
        vs_ref[5:6, :] += _colsum(d_v)
        d_vh = d_v * vg_ref[...]
        d_av = rs * (d_vh - jnp.mean(d_vh, axis=-1, keepdims=True)
                     - vhat * jnp.mean(d_vh * vhat, axis=-1, keepdims=True))
        dz_ref[:, 2 * LRU_W:2 * LRU_W + GMLP_W] = d_gu.astype(BF16)
        dz_ref[:, 2 * LRU_W + GMLP_W:] = (d_av * dav).astype(BF16)

        @pl.when(i == nt - 1)
        def _():
            for hd in range(N_HEADS):
                blk = slice(hd * HEAD_DIM, (hd + 1) * HEAD_DIM)
                dwrb_ref[_head_pair_block(hd)] = dwr_ref[blk, blk]
                dwib_ref[_head_pair_block(hd)] = dwi_ref[blk, blk]
            for g in range(N_GROUPS):
                dws_ref[g] = jnp.where(mask, dws_ref[g], 0.0)

    rev = lambda c: pl.BlockSpec((tt, c), lambda i: (nt - 1 - i, 0))
    halo = pl.BlockSpec((SUBLANES, LRU_W), lambda i: (jnp.maximum((nt - 1 - i) * hb - 1, 0), 0))
    v512 = _const((1, LRU_W))
    return _call(
        body, "mix_bwd", (nt,),
        in_specs=[rev(LRU_W + GMLP_W), rev(IN_COLS), halo, rev(LRU_W), halo,
                  _const((LRU_CONV_K, LRU_W)), v512, _whole(), _whole(), v512, v512, v512, v512, v512,
                  _whole(), _whole(), _whole(), v512, v512],
        out_specs=[rev(IN_COLS), _const((SUBLANES, LRU_W)), _const((SUBLANES, LRU_W)),
                   _const((LRU_W // 2, 2 * HEAD_DIM)), _const((LRU_W // 2, 2 * HEAD_DIM)),
                   _const((N_GROUPS, POS_BLOCK, POS_BLOCK)), _const((SUBLANES, POS_BLOCK))],
        out_shape=[_sds((s_len, IN_COLS), BF16), _sds((SUBLANES, LRU_W), F32), _sds((SUBLANES, LRU_W), F32),
                   _sds((LRU_W // 2, 2 * HEAD_DIM), F32), _sds((LRU_W // 2, 2 * HEAD_DIM), F32),
                   _sds((N_GROUPS, POS_BLOCK, POS_BLOCK), F32), _sds((SUBLANES, POS_BLOCK), F32)],
        scratch=[pltpu.VMEM((SUBLANES, LRU_W), F32), pltpu.VMEM((SUBLANES, LRU_W), F32),
                 pltpu.VMEM((SUBLANES, LRU_W), F32), pltpu.VMEM((LRU_W, LRU_W), F32), pltpu.VMEM((LRU_W, LRU_W), F32)],
        args=(d_ycat, z, z, hl, hl, conv_w, conv_b, wr_bd, wi_bd, b_r, b_i, lru_a, vn_g, vn_b, w_sp, w_sp_t, b_sp_t,
              g_lru, g_gmlp), carry=carry)


def _in_bwd(d_z, w_in, x, d_x1, g, sc, carry=None):
    s_len = x.shape[0]
    tt = min(TT_BIG, s_len)

    def body(dz_ref, w_ref, x_ref, dx1_ref, g_ref, sc_ref, gx_ref, vs_ref):
        @pl.when(pl.program_id(0) == 0)
        def _():
            vs_ref[...] = jnp.zeros_like(vs_ref)

        d_h = _dot_nt(dz_ref[...], w_ref[...])
        n, r = _rms(x_ref[...])
        vs_ref[0:1, :] += _colsum(d_h)
        vs_ref[1:2, :] += _colsum(d_h * n * g_ref[...])
        d_ng = d_h * (1.0 + sc_ref[...])
        vs_ref[2:3, :] += _colsum(d_ng * n)
        gx_ref[...] = dx1_ref[...] + _rms_bwd(d_ng * g_ref[...], n, r)

    row = lambda c: pl.BlockSpec((tt, c), lambda i: (i, 0))
    vec = _const((1, D_MODEL))
    return _call(
        body, "in_bwd", (s_len // tt,),
        in_specs=[row(IN_COLS), _whole(), row(D_MODEL), row(D_MODEL), vec, vec],
        out_specs=[row(D_MODEL), _const((SUBLANES, D_MODEL))],
        out_shape=[_sds((s_len, D_MODEL), F32), _sds((SUBLANES, D_MODEL), F32)],
        scratch=[], args=(d_z, w_in, x, d_x1, g, sc), carry=carry)


def _wgrad(a, b, tn, name, carry=None):
    s_len, k_dim = a.shape
    halves = b.ndim == 3
    n_dim = b.shape[-1] * (2 if halves else 1)
    ts = min(TT_WG, s_len)
    nj = n_dim // tn
    nt = s_len // ts

    def body(a_ref, b_ref, o_ref, ob_ref):
        t = pl.program_id(1)
        part = _dot_tn(a_ref[...], b_ref[0] if halves else b_ref[...])

        @pl.when(t == 0)
        def _():
            o_ref[...] = part

        @pl.when(t > 0)
        def _():
            o_ref[...] += part

        @pl.when(t == nt - 1)
        def _():
            ob_ref[...] = o_ref[...].astype(BF16)

    if halves:
        per_half = nj // 2
        b_spec = pl.BlockSpec((1, ts, tn), lambda j, t: (j // per_half, t, j % per_half))
    else:
        b_spec = pl.BlockSpec((ts, tn), lambda j, t: (t, j))
    o_spec = pl.BlockSpec((k_dim, tn), lambda j, t: (0, j))
    return _call(
        body, name, (nj, nt),
        in_specs=[pl.BlockSpec((ts, k_dim), lambda j, t: (t, 0)), b_spec],
        out_specs=[o_spec, o_spec],
        out_shape=[_sds((k_dim, n_dim), F32), _sds((k_dim, n_dim), BF16)],
        scratch=[], args=(a, b), carry=carry)


def _adam_math(w, g, m, v):
    m = ADAM_B1 * m + (1.0 - ADAM_B1) * g
    v = ADAM_B2 * v + (1.0 - ADAM_B2) * (g * g)
    m_hat = m / (1.0 - ADAM_B1 ** ADAM_STEP)
    v_hat = v / (1.0 - ADAM_B2 ** ADAM_STEP)
    delta = -ADAM_LR * (m_hat / (jnp.sqrt(v_hat) + ADAM_EPS) + ADAM_WD * w)
    return delta, m, v


def _row_tile(rows, cols, n_f32_arrays):
    budget = VMEM_LIMIT // 2
    tr = rows
    while tr % 2 == 0 and tr // 2 >= SUBLANES and (tr // 2) % SUBLANES == 0 and tr * cols * 4 * n_f32_arrays * 2 > budget:
        tr //= 2
    return tr


def _adamw_sum(w, g_full, recv, m, v, col_sharded, name):
    _, rows, cols = w.shape
    n_recv = len(recv)
    tr = _row_tile(rows, cols, 10)
    nb = rows // tr

    def body(me_ref, w_ref, g_ref, *rest):
        r_refs = rest[:n_recv]
        m_ref, v_ref, go_ref, d_ref, mo_ref, vo_ref = rest[n_recv:]
        g = g_ref[...]
        for r_ref in r_refs:
            for k in range(r_ref.shape[0]):
                g = g + r_ref[k].astype(F32)
        go_ref[0] = g
        d_ref[0], mo_ref[0], vo_ref[0] = _adam_math(w_ref[0], g, m_ref[0], v_ref[0])

    if col_sharded:
        own = pl.BlockSpec((tr, cols), lambda i, me: (i, me[0]))
    else:
        own = pl.BlockSpec((tr, cols), lambda i, me: (me[0] * nb + i, 0))
    blk = pl.BlockSpec((1, tr, cols), lambda i, me: (0, i, 0))
    return pl.pallas_call(
        body, name=name,
        grid_spec=pltpu.PrefetchScalarGridSpec(
            num_scalar_prefetch=1, grid=(nb,),
            in_specs=[blk, own] + [pl.BlockSpec((r.shape[0], tr, cols), lambda i, me: (0, i, 0)) for r in recv]
            + [blk, blk],
            out_specs=[blk] * 4),
        out_shape=[_sds((1, rows, cols), F32)] * 4,
        compiler_params=_cparams(("arbitrary",)),
    )(jnp.reshape(_dev_index(_my_pos()), (1,)).astype(jnp.int32), w, g_full, *recv, m, v)


def _row_of_each(ref, row):
    cols = ref.shape[1]
    rows = _rows((N_DEV, cols))
    out = jnp.zeros((N_DEV, cols), F32)
    for d in range(N_DEV):
        picked = ref[d * SUBLANES + row:d * SUBLANES + row + 1, :]
        out = jnp.where(rows == d, jnp.broadcast_to(picked, (N_DEV, cols)), out)
    return out


def _my_columns(full, width, me):
    out = jnp.zeros(full.shape[:-1] + (width,), F32)
    for d in range(N_DEV):
        out = out + jnp.where(me == d, full[:, d * width:(d + 1) * width], 0.0)
    return out


def _adamw_wada(c_all, vs_in_all, vs_up_all, vs_ffn_all, w, m, v):
    _, rows, cols = w.shape

    def body(c_ref, vi_ref, vu_ref, vf_ref, w_ref, m_ref, v_ref, go_ref, d_ref, mo_ref, vo_ref):
        me = _dev_index(_my_pos())
        cv = _row_of_each(c_ref, 0)
        ca = cv * _sigmoid(cv)
        dmod = jnp.concatenate([_row_of_each(vi_ref, 0), _row_of_each(vi_ref, 1), _row_of_each(vu_ref, 3),
                                _row_of_each(vu_ref, 0), _row_of_each(vu_ref, 1), _row_of_each(vf_ref, 0)], axis=1)
        dm = _my_columns(dmod, cols, me)
        g = lax.dot_general(ca, dm, (((0,), (0,)), ((), ())), preferred_element_type=F32,
                            precision=lax.Precision.HIGHEST)
        go_ref[0] = g
        d_ref[0], mo_ref[0], vo_ref[0] = _adam_math(w_ref[0], g, m_ref[0], v_ref[0])

    return pl.pallas_call(
        body, name="adamw_w_ada", out_shape=[_sds((1, rows, cols), F32)] * 4,
        in_specs=[_whole()] * 7, out_specs=[_whole()] * 4,
        compiler_params=_cparams(),
    )(c_all, vs_in_all, vs_up_all, vs_ffn_all, w, m, v)


def _adamw_small(gathered, reduced, params, conv_params):
    names = list(params) + list(conv_params)
    allp = {**params, **conv_params}
    n_g = len(gathered) + len(reduced)

    def body(*refs):
        g_refs = refs[:n_g]
        p_refs = refs[n_g:n_g + 3 * len(names)]
        o_refs = refs[n_g + 3 * len(names):]
        me = _dev_index(_my_pos())

        def total(ref):
            s = ref[0:SUBLANES, :]
            for d in range(1, N_DEV):
                s = s + ref[d * SUBLANES:(d + 1) * SUBLANES, :]
            return s

        vs_in, vs_up, vs_ffn, loss = [total(r) for r in g_refs[:4]]
        cs, vs_mix, dcw, dwr, dwi, dws, dbs = [r[...] for r in g_refs[4:]]
        o_refs[-1][...] = loss[0:1, 0:1]
        mine = lambda full, width: _my_columns(full, width, me)

        all_ = (slice(None), slice(None))
        heads = lambda row: [((0, slice(h, h + 1), slice(None)), row[:, h * HEAD_DIM:(h + 1) * HEAD_DIM])
                             for h in range(N_HEADS)]
        blocks = lambda pairs: [((0, h), pairs[_head_pair_block(h)]) for h in range(N_HEADS)]
        pieces = {
            "b_ada": [((slice(None), slice(k * D_MODEL, (k + 1) * D_MODEL)), row) for k, row in enumerate(
                (vs_in[0:1], vs_in[1:2], vs_up[3:4], vs_up[0:1], vs_up[1:2], vs_ffn[0:1]))],
            "g_mix_pre": [(all_, vs_in[2:3])], "g_mix_post": [(all_, vs_up[4:5])],
            "g_ffn_pre": [(all_, vs_up[2:3])], "g_ffn_post": [(all_, vs_ffn[1:2])],
            "conv_b": [(all_, vs_mix[0:1])], "b_rgate": heads(vs_mix[1:2]), "b_igate": heads(vs_mix[2:3]),
            "lru_a": [(all_, vs_mix[3:4])], "v_norm_g": [(all_, vs_mix[4:5])], "v_norm_b": [(all_, vs_mix[5:6])],
            "g_lru_out": [(all_, vs_mix[6:7])], "g_gmlp_out": [(all_, vs_mix[7:8])],
            "w_rgate": blocks(dwr), "w_igate": blocks(dwi),
            "w_spatial": [((0, g), dws[g * POS_BLOCK:(g + 1) * POS_BLOCK, :]) for g in range(N_GROUPS)],
            "b_spatial": [((0,), dbs[0:N_GROUPS])],
            "ffn_conv_b": [(all_, cs[FFN_CONV_K:FFN_CONV_K + 1])],
            "conv_w": [((0,), mine(dcw[0:LRU_CONV_K], LRU_W // N_DEV))],
            "ffn_conv_w": [((0,), mine(cs[0:FFN_CONV_K], 2 * D_FF // N_DEV))],
        }
        for n_i, name in enumerate(names):
            w_ref, m_ref, v_ref = p_refs[3 * n_i:3 * n_i + 3]
            go_ref, d_ref, mo_ref, vo_ref = o_refs[4 * n_i:4 * n_i + 4]
            for idx, g in pieces[name]:
                go_ref[idx] = g
                d_ref[idx], mo_ref[idx], vo_ref[idx] = _adam_math(w_ref[idx], g, m_ref[idx], v_ref[idx])

    flat_params = [a for n in names for a in allp[n]]
    out_shape = [_sds(allp[n][0].shape, F32) for n in names for _ in range(4)] + [_sds((1, 1), F32)]
    outs = pl.pallas_call(
        body, name="adamw_small", out_shape=out_shape,
        in_specs=[_whole()] * (n_g + len(flat_params)), out_specs=[_whole()] * len(out_shape),
        compiler_params=_cparams(),
    )(*gathered, *reduced, *flat_params)
    return {n: outs[4 * i:4 * i + 4] for i, n in enumerate(names)}, outs[-1]


def _my_pos():
    return lax.axis_index("x"), lax.axis_index("y"), lax.axis_index("c")


def _flip(pos, k):
    x, y, c = pos
    return (1 - x if k & 4 else x, 1 - y if k & 2 else y, 1 - c if k & 1 else c)


def _dev_index(pos):
    x, y, c = pos
    return 4 * x + 2 * y + c


def _all_gather_small(ins, outs, send_sems, recv_sems):
    n = len(ins)
    me = _my_pos()

    def slot(a, pos):
        rows = ins[a].shape[0]
        return outs[a].at[pl.ds(pl.multiple_of(_dev_index(pos) * rows, SUBLANES), rows), :]

    def copy(a, k, block):
        return pltpu.make_async_remote_copy(
            src_ref=ins[a], dst_ref=slot(a, block), send_sem=send_sems.at[a, k - 1], recv_sem=recv_sems.at[a, k - 1],
            device_id=_flip(me, k), device_id_type=MESH)

    sends = [copy(a, k, me) for a in range(n) for k in range(1, N_DEV)]
    for cp in sends:
        cp.start()
    for a in range(n):
        rows = ins[a].shape[0]
        outs[a][pl.ds(pl.multiple_of(_dev_index(me) * rows, SUBLANES), rows), :] = ins[a][...]
    for a in range(n):
        for k in range(1, N_DEV):
            copy(a, k, _flip(me, k)).wait_recv()
    for cp in sends:
        cp.wait_send()


def _prologue(c8, cw8, fcw8, w_ada, b_ada, carry):
    cols = w_ada.shape[1]

    def body(c_ref, cw_ref, fcw_ref, w_ref, b_ref, call_ref, cwall_ref, fcwall_ref, modall_ref, mod_scr,
             s1, r1, s2, r2):
        _all_gather_small([c_ref, cw_ref, fcw_ref], [call_ref, cwall_ref, fcwall_ref], s1, r1)
        cv = _row_of_each(call_ref, 0)
        ca = cv * _sigmoid(cv)
        b_cols = _my_columns(b_ref[...], cols, _dev_index(_my_pos()))
        mod_scr[...] = jnp.dot(ca, w_ref[...], preferred_element_type=F32, precision=lax.Precision.HIGHEST) + b_cols
        _all_gather_small([mod_scr], [modall_ref], s2, r2)

    sem = lambda n: pltpu.SemaphoreType.DMA((n, N_DEV - 1))
    return _call(
        body, "prologue", (1,), in_specs=[_whole()] * 5, out_specs=[_whole()] * 4,
        out_shape=[_sds((N_DEV * SUBLANES, a.shape[1]), F32) for a in (c8, cw8, fcw8)]
        + [_sds((N_DEV * N_DEV, cols), F32)],
        scratch=[pltpu.VMEM((N_DEV, cols), F32), sem(3), sem(3), sem(1), sem(1)],
        args=(c8, cw8, fcw8, w_ada, b_ada), carry=carry)


def _reduce_small(gath, red, carry=None):
    n_g, n_r = len(gath), len(red)
    chip_flips = (4, 2, 6)

    def body(*refs):
        g_in, r_in = refs[:n_g], refs[n_g:n_g + n_r]
        g_out, r_out = refs[n_g + n_r:2 * n_g + n_r], refs[2 * n_g + n_r:2 * (n_g + n_r)]
        scr = refs[2 * (n_g + n_r):]
        sib, land = scr[:n_r], scr[n_r:2 * n_r]
        g_send, g_recv, s_send, s_recv, i_send, i_recv, f_send, f_recv = scr[2 * n_r:]
        me = _my_pos()
        c = me[2]
        sibling = _flip(me, 1)

        def slot(a, pos):
            return g_out[a].at[pl.ds(pl.multiple_of(_dev_index(pos) * SUBLANES, SUBLANES), SUBLANES), :]

        def gcopy(a, k):
            return pltpu.make_async_remote_copy(
                src_ref=g_in[a], dst_ref=slot(a, me), send_sem=g_send.at[a, k - 1], recv_sem=g_recv.at[a, k - 1],
                device_id=_flip(me, k), device_id_type=MESH)

        def scopy(a):
            return pltpu.make_async_remote_copy(
                src_ref=r_in[a], dst_ref=sib[a], send_sem=s_send.at[a], recv_sem=s_recv.at[a],
                device_id=sibling, device_id_type=MESH)

        def icopy(a, j):
            return pltpu.make_async_remote_copy(
                src_ref=r_out[a], dst_ref=land[a].at[j], send_sem=i_send.at[a, j], recv_sem=i_recv.at[a, j],
                device_id=_flip(me, chip_flips[j]), device_id_type=MESH)

        def fcopy(a, j):
            return pltpu.make_async_remote_copy(
                src_ref=land[a].at[j], dst_ref=land[a].at[j], send_sem=f_send.at[a, j], recv_sem=f_recv.at[a, j],
                device_id=sibling, device_id_type=MESH)

        gathers = [gcopy(a, k) for a in range(n_g) for k in range(1, N_DEV)]
        swaps = [scopy(a) for a in range(n_r)]
        for cp in gathers + swaps:
            cp.start()
        for a in range(n_g):
            g_out[a][pl.ds(pl.multiple_of(_dev_index(me) * SUBLANES, SUBLANES), SUBLANES), :] = g_in[a][...]
        for a in range(n_r):
            swaps[a].wait_recv()
            r_out[a][...] = r_in[a][...] + sib[a][...]

        for core in range(2):
            mine = [a for a in range(n_r) if a % 2 == core]
            theirs = [a for a in range(n_r) if a % 2 != core]

            @pl.when(c == core)
            def _():
                out = [icopy(a, j) for a in mine for j in range(3)]
                for cp in out:
                    cp.start()
                fwd = []
                for a in mine:
                    for j in range(3):
                        icopy(a, j).wait_recv()
                        cp = fcopy(a, j)
                        cp.start()
                        fwd.append(cp)
                for a in theirs:
                    for j in range(3):
                        fcopy(a, j).wait_recv()
                for cp in out + fwd:
                    cp.wait_send()

        for a in range(n_r):
            r_out[a][...] = (r_out[a][...] + land[a][1]) + (land[a][0] + land[a][2])
        for a in range(n_g):
            for k in range(1, N_DEV):
                pltpu.make_async_remote_copy(
                    src_ref=g_in[a], dst_ref=slot(a, _flip(me, k)), send_sem=g_send.at[a, k - 1],
                    recv_sem=g_recv.at[a, k - 1], device_id=_flip(me, k), device_id_type=MESH).wait_recv()
        for cp in gathers + swaps:
            cp.wait_send()

    shapes = [tuple(a.shape) for a in red]
    outs, carried = _call(
        body, "reduce_small", (1,), in_specs=[_whole()] * (n_g + n_r), out_specs=[_whole()] * (n_g + n_r),
        out_shape=[_sds((N_DEV * SUBLANES, a.shape[1]), F32) for a in gath] + [_sds(s, F32) for s in shapes],
        scratch=[pltpu.VMEM(s, F32) for s in shapes] + [pltpu.VMEM((3,) + s, F32) for s in shapes]
        + [pltpu.SemaphoreType.DMA((n_g, N_DEV - 1)), pltpu.SemaphoreType.DMA((n_g, N_DEV - 1)),
           pltpu.SemaphoreType.DMA((n_r,)), pltpu.SemaphoreType.DMA((n_r,)),
           pltpu.SemaphoreType.DMA((n_r, 3)), pltpu.SemaphoreType.DMA((n_r, 3)),
           pltpu.SemaphoreType.DMA((n_r, 3)), pltpu.SemaphoreType.DMA((n_r, 3))],
        args=tuple(gath) + tuple(red), carry=carry)
    return (outs[:n_g], outs[n_g:]), carried


STACKED = "stacked"


def _region(ref, shard_shape, col_sharded, pos):
    r, cdim = shard_shape
    d = _dev_index(pos)
    if col_sharded == STACKED:
        return ref.at[d]
    if col_sharded:
        return ref.at[:, pl.ds(pl.multiple_of(d * cdim, LANES), cdim)]
    return ref.at[pl.ds(pl.multiple_of(d * r, 2 * SUBLANES), r), :]


def _gather_carry(shards, col_sharded):
    n_w = len(shards)
    shapes = [tuple(s.shape) for s in shards]
    full_shapes = [(N_DEV,) + s if cs == STACKED else (s[0], s[1] * N_DEV) if cs else (s[0] * N_DEV, s[1])
                   for s, cs in zip(shapes, col_sharded)]

    def tools(out_refs, scr):
        send_sems, recv_sems = scr[n_w], scr[n_w + 1]
        me = _my_pos()
        x, y, c = me
        sibling = (x, y, 1 - c)
        chips = [(1 - x, y), (x, 1 - y), (1 - x, 1 - y)]

        def region(w, pos):
            return _region(out_refs[w], shapes[w], col_sharded[w], pos)

        def copy(w, k, block, to, src=None):
            return pltpu.make_async_remote_copy(
                src_ref=region(w, block) if src is None else src, dst_ref=region(w, block),
                send_sem=send_sems.at[w, k], recv_sem=recv_sems.at[w, k], device_id=to, device_id_type=MESH)

        def first(w):
            return [copy(w, 0, me, sibling, src=scr[w])] + [
                copy(w, 1 + j, me, (*chip, c), src=scr[w]) for j, chip in enumerate(chips)]

        def mine(w):
            return pltpu.make_async_copy(scr[w], region(w, me), scr[n_w + 2].at[w])

        return me, c, sibling, chips, copy, first, mine

    def start(ins, outs, scr):
        _, _, _, _, _, first, mine = tools(outs, scr)
        for w in range(n_w):
            scr[w][...] = ins[w][...].astype(BF16)
            for cp in first(w) + [mine(w)]:
                cp.start()

    def finish(ins, outs, scr):
        me, c, sibling, chips, copy, first, mine = tools(outs, scr)
        passed = []
        for w in range(n_w):
            for j, chip in enumerate(chips):
                copy(w, 1 + j, (*chip, c), me).wait_recv()
                fwd = copy(w, 4 + j, (*chip, c), sibling)
                fwd.start()
                passed.append(fwd)
        for w in range(n_w):
            copy(w, 0, sibling, me).wait_recv()
            for j, chip in enumerate(chips):
                copy(w, 4 + j, (*chip, 1 - c), me).wait_recv()
        for w in range(n_w):
            for cp in first(w):
                cp.wait_send()
            mine(w).wait()
        for cp in passed:
            cp.wait_send()

    return _Carry(
        inputs=list(shards), in_specs=[_whole()] * n_w,
        out_shape=[_sds(s, BF16) for s in full_shapes], out_specs=[_any()] * n_w,
        scratch=[pltpu.VMEM(s, BF16) for s in shapes]
        + [pltpu.SemaphoreType.DMA((n_w, N_DEV - 1)), pltpu.SemaphoreType.DMA((n_w, N_DEV - 1)),
           pltpu.SemaphoreType.DMA((n_w,))],
        start=start, finish=finish)


def _scatter_carry(grads_bf, shard_shapes, col_sharded, relations):
    n_w = len(grads_bf)
    shapes = [tuple(s) for s in shard_shapes]

    def copies(ins, outs, scr):
        send_sems, recv_sems = scr
        me = _my_pos()
        out = []
        for w in range(n_w):
            for i, k in enumerate(relations[w]):
                peer = _flip(me, k)
                out.append(pltpu.make_async_remote_copy(
                    src_ref=_region(ins[w], shapes[w], col_sharded[w], peer), dst_ref=outs[w].at[i],
                    send_sem=send_sems.at[w, i], recv_sem=recv_sems.at[w, i],
                    device_id=peer, device_id_type=MESH))
        return out

    def start(ins, outs, scr):
        for cp in copies(ins, outs, scr):
            cp.start()

    def finish(ins, outs, scr):
        cps = copies(ins, outs, scr)
        for cp in cps:
            cp.wait_recv()
        for cp in cps:
            cp.wait_send()

    return _Carry(
        inputs=list(grads_bf), in_specs=[_any()] * n_w,
        out_shape=[_sds((len(r),) + s, BF16) for r, s in zip(relations, shapes)], out_specs=[_any()] * n_w,
        scratch=[pltpu.SemaphoreType.DMA((n_w, N_DEV - 1)), pltpu.SemaphoreType.DMA((n_w, N_DEV - 1))],
        start=start, finish=finish)


def _block_diag(w):
    eye = jnp.eye(N_HEADS, dtype=w.dtype)
    return (eye[:, None, :, None] * w[:, :, None, :]).reshape(N_HEADS * HEAD_DIM, N_HEADS * HEAD_DIM)


def _pad_rows(a):
    return jnp.pad(a, ((0, SUBLANES - a.shape[0]), (0, 0)))


def _columns_from_devices(gathered, rows):
    w = gathered.shape[1]
    return gathered.reshape(N_DEV, SUBLANES, w)[:, :rows].transpose(1, 0, 2).reshape(rows, N_DEV * w)


def _local_step(x2, target, mod, w_in_f, w_full, conv_w_full, ffn_cw_full,
                g_mix_pre, g_mix_post, conv_b, w_rgate, b_rgate, w_igate, b_igate, lru_a, v_norm_g, v_norm_b,
                w_spatial, b_spatial, g_lru_out, g_gmlp_out, g_ffn_pre, g_ffn_post, ffn_conv_b,
                gather=None, scatter=None):
    sh_m, sc_m, gt_m, sh_f, sc_f, gt_f = [mod[k] for k in range(N_MOD)]
    wr_bd = _block_diag(w_rgate[0]).astype(BF16)
    wi_bd = _block_diag(w_igate[0]).astype(BF16)
    b_r = b_rgate.reshape(1, LRU_W)
    b_i = b_igate.reshape(1, LRU_W)
    b_sp_t = b_spatial[0].T
    w_sp_t = jnp.swapaxes(w_spatial[0], 1, 2)

    def arriving(*names):
        return gather(*names) if gather else None

    near, far = (1, 2, 3, 4, 5), (6, 7)

    def leaving(*parts):
        return scatter(parts) if scatter else None

    def received(recv, parts, outs):
        for (name, _, _), out in zip(parts, outs):
            recv.setdefault(name, []).append(out)

    mix_params = (conv_w_full, conv_b, wr_bd, wi_bd, b_r, b_i, lru_a, v_norm_g, v_norm_b)
    (z, h, ycat, hl), got = _mix_fwd(x2, sh_m, sc_m, g_mix_pre, w_in_f, *mix_params, w_spatial[0], b_sp_t,
                                     g_lru_out, g_gmlp_out, carry=arriving("w_up", "w_down"))
    w_up_f, w_down_f = got if gather else (w_full["w_up"], w_full["w_down"])
    w_out_f = w_full["w_out"]
    (y, x1, h2, up_pre, up, act, d_y2, dout, loss_acc, vs_ffn), _ = _ffn_fwd(
        ycat, x2, w_out_f, g_mix_post, gt_m, g_ffn_pre, sc_f, sh_f, w_up_f, ffn_cw_full, ffn_conv_b, w_down_f,
        gt_f, g_ffn_post, target)

    recv = {}
    gw_down, _ = _wgrad(act, d_y2, D_MODEL // 2, "wgrad_down")
    parts = [("w_down", gw_down[1], near + far)]
    (d_up, cs_ffn, d_x1, d_y, d_ycat, vs_up), got = _ffn_bwd(
        d_y2, up_pre, up, ffn_cw_full, w_down_f, w_up_f, x1, dout, y, w_out_f, g_ffn_pre, sc_f, g_mix_post, gt_m,
        carry=leaving(*parts))
    received(recv, parts, got)
    gw_out, _ = _wgrad(ycat, d_y, D_MODEL, "wgrad_out")
    parts = [("w_out", gw_out[1], near + far)]
    gw_up, got = _wgrad(h2, d_up, D_FF // 2, "wgrad_up", carry=leaving(*parts))
    received(recv, parts, got)
    parts = [("w_up", gw_up[1], near)]
    (d_z, vs_mix, dcw, d_wr, d_wi, d_ws, d_bs), got = _mix_bwd(
        d_ycat, z, hl, *mix_params, w_spatial[0], w_sp_t, b_sp_t, g_lru_out, g_gmlp_out, carry=leaving(*parts))
    received(recv, parts, got)
    parts = [("w_up", gw_up[1], far)]
    gw_in, got = _wgrad(h, d_z, IN_COLS // 2, "wgrad_in", carry=leaving(*parts))
    received(recv, parts, got)
    parts = [("w_in", gw_in[1], near)]
    (grad_x, vs_in), got = _in_bwd(d_z, w_in_f, x2, d_x1, g_mix_pre, sc_m, carry=leaving(*parts))
    received(recv, parts, got)
    pending = [("w_in", gw_in[1], far)]

    gath = [vs_in, vs_up, vs_ffn, loss_acc]
    red = [cs_ffn, vs_mix, dcw, d_wr, d_wi, d_ws.reshape(N_GROUPS * POS_BLOCK, POS_BLOCK), d_bs]
    return dict(grad_x=grad_x, gath=gath, red=red, recv=recv, pending=pending,
                w_in=gw_in, w_out=gw_out, w_up=gw_up, w_down=gw_down)


def kernel(x, c, w_ada, b_ada, g_mix_pre, g_mix_post, w_in, conv_w, conv_b, w_rgate, b_rgate, w_igate, b_igate, lru_a, v_norm_g, v_norm_b, w_spatial, b_spatial, g_lru_out, g_gmlp_out, w_out, g_ffn_pre, g_ffn_post, w_up, ffn_conv_w, ffn_conv_b, w_down, loss_target, m_w_ada, m_b_ada, m_g_mix_pre, m_g_mix_post, m_w_in, m_conv_w, m_conv_b, m_w_rgate, m_b_rgate, m_w_igate, m_b_igate, m_lru_a, m_v_norm_g, m_v_norm_b, m_w_spatial, m_b_spatial, m_g_lru_out, m_g_gmlp_out, m_w_out, m_g_ffn_pre, m_g_ffn_post, m_w_up, m_ffn_conv_w, m_ffn_conv_b, m_w_down, v_w_ada, v_b_ada, v_g_mix_pre, v_g_mix_post, v_w_in, v_conv_w, v_conv_b, v_w_rgate, v_b_rgate, v_w_igate, v_b_igate, v_lru_a, v_v_norm_g, v_v_norm_b, v_w_spatial, v_b_spatial, v_g_lru_out, v_g_gmlp_out, v_w_out, v_g_ffn_pre, v_g_ffn_post, v_w_up, v_ffn_conv_w, v_ffn_conv_b, v_w_down):
    me = _dev_index(_my_pos())
    ada_cols = w_ada.shape[-1]

    big_w = dict(w_in=(w_in, m_w_in, v_w_in, True), w_out=(w_out, m_w_out, v_w_out, False),
                 w_up=(w_up, m_w_up, v_w_up, True), w_down=(w_down, m_w_down, v_w_down, False))

    def gather(*names):
        return _gather_carry([big_w[n][0][0] for n in names], [STACKED if n == "w_up" else big_w[n][3] for n in names])

    def scatter(parts):
        return _scatter_carry([g for _, g, _ in parts], [big_w[n][0].shape[1:] for n, _, _ in parts],
                              [big_w[n][3] for n, _, _ in parts], [rel for _, _, rel in parts])

    (c_all, cw_all, fcw_all, mod_all), (w_in_f, w_out_f) = _prologue(
        jnp.broadcast_to(c, (SUBLANES, D_MODEL)), _pad_rows(conv_w[0]), _pad_rows(ffn_conv_w[0]), w_ada[0], b_ada,
        carry=gather("w_in", "w_out"))
    conv_w_full = _columns_from_devices(cw_all, LRU_CONV_K)
    ffn_cw_full = _columns_from_devices(fcw_all, FFN_CONV_K)
    mod = lax.dynamic_index_in_dim(mod_all.reshape(N_DEV, N_DEV, ada_cols), me, axis=1, keepdims=False)
    mod = mod.reshape(N_MOD, 1, D_MODEL)

    loc = _local_step(x[0], loss_target[0], mod, w_in_f, dict(w_out=w_out_f), conv_w_full, ffn_cw_full,
                      g_mix_pre, g_mix_post, conv_b, w_rgate, b_rgate, w_igate, b_igate, lru_a, v_norm_g, v_norm_b,
                      w_spatial, b_spatial, g_lru_out, g_gmlp_out, g_ffn_pre, g_ffn_post, ffn_conv_b,
                      gather=gather, scatter=scatter)
    grad_x = loc["grad_x"]

    (gathered, reduced), got = _reduce_small(loc["gath"], loc["red"], carry=scatter(loc["pending"]))
    for (name, _, _), out in zip(loc["pending"], got):
        loc["recv"][name].append(out)

    results = {}
    for name, (w_, m_, v_, cs) in big_w.items():
        results[name] = _adamw_sum(w_, loc[name][0], loc["recv"][name], m_, v_, cs, "adamw_" + name)

    params = dict(
        b_ada=(b_ada, m_b_ada, v_b_ada), g_mix_pre=(g_mix_pre, m_g_mix_pre, v_g_mix_pre),
        g_mix_post=(g_mix_post, m_g_mix_post, v_g_mix_post), conv_b=(conv_b, m_conv_b, v_conv_b),
        w_rgate=(w_rgate, m_w_rgate, v_w_rgate), b_rgate=(b_rgate, m_b_rgate, v_b_rgate),
        w_igate=(w_igate, m_w_igate, v_w_igate), b_igate=(b_igate, m_b_igate, v_b_igate),
        lru_a=(lru_a, m_lru_a, v_lru_a), v_norm_g=(v_norm_g, m_v_norm_g, v_v_norm_g),
        v_norm_b=(v_norm_b, m_v_norm_b, v_v_norm_b), w_spatial=(w_spatial, m_w_spatial, v_w_spatial),
        b_spatial=(b_spatial, m_b_spatial, v_b_spatial), g_lru_out=(g_lru_out, m_g_lru_out, v_g_lru_out),
        g_gmlp_out=(g_gmlp_out, m_g_gmlp_out, v_g_gmlp_out), g_ffn_pre=(g_ffn_pre, m_g_ffn_pre, v_g_ffn_pre),
        g_ffn_post=(g_ffn_post, m_g_ffn_post, v_g_ffn_post), ffn_conv_b=(ffn_conv_b, m_ffn_conv_b, v_ffn_conv_b))
    conv_params = dict(conv_w=(conv_w, m_conv_w, v_conv_w), ffn_conv_w=(ffn_conv_w, m_ffn_conv_w, v_ffn_conv_w))
    small_results, loss = _adamw_small(gathered, reduced, params, conv_params)
    results.update(small_results)
    loss = loss.reshape(())

    results["w_ada"] = _adamw_wada(c_all, gathered[0], gathered[1], gathered[2], w_ada, m_w_ada, v_w_ada)

    order = ["w_ada", "b_ada", "g_mix_pre", "g_mix_post", "w_in", "conv_w", "conv_b", "w_rgate", "b_rgate", "w_igate",
             "b_igate", "lru_a", "v_norm_g", "v_norm_b", "w_spatial", "b_spatial", "g_lru_out", "g_gmlp_out", "w_out",
             "g_ffn_pre", "g_ffn_post", "w_up", "ffn_conv_w", "ffn_conv_b", "w_down"]
    outs = [loss, grad_x[None]]
    for kind in range(4):
        outs += [results[n][kind] for n in order]
    return tuple(outs)
```

```python
import functools

import jax
import jax.numpy as jnp
from jax import lax
from jax.experimental import pallas as pl
from jax.experimental.pallas import tpu as pltpu

F32 = jnp.float32
BF16 = jnp.bfloat16

D_MODEL = 1024
LRU_W = 512
GMLP_W = 512
N_HEADS = 8
HEAD_DIM = 64
N_GROUPS = 4
POS_BLOCK = 128
CHUNK = 64
IN_COLS = 2048
D_FF = 3072
N_MOD = 6
N_DEV = 8
EPS = 1e-6
LRU_C = 8.0
LRU_CONV_K = 4
FFN_CONV_K = 3

ADAM_LR = 0.001
ADAM_B1 = 0.9
ADAM_B2 = 0.999
ADAM_EPS = 1e-08
ADAM_WD = 0.01
ADAM_STEP = 10

LANES = 128
SUBLANES = 8
TT_BIG = 512
TT_MIX = 256
TT_WG = 1024
FF_CW = 512
VMEM_LIMIT = 56 * 1024 * 1024

MESH = pl.DeviceIdType.MESH


def _sds(shape, dtype):
    return jax.ShapeDtypeStruct(shape, dtype)


def _cparams(sem=None):
    return pltpu.CompilerParams(dimension_semantics=sem, vmem_limit_bytes=VMEM_LIMIT)


def _whole():
    return pl.BlockSpec(memory_space=pltpu.VMEM)


def _const(shape):
    nd = len(shape)
    return pl.BlockSpec(shape, lambda *_: (0,) * nd)


def _any():
    return pl.BlockSpec(memory_space=pl.ANY)


class _Carry:
    def __init__(self, inputs, in_specs, out_shape, out_specs, scratch, start, finish):
        self.inputs, self.in_specs, self.out_shape, self.out_specs = inputs, in_specs, out_shape, out_specs
        self.scratch, self.start, self.finish = scratch, start, finish


def _call(body, name, grid, in_specs, out_specs, out_shape, scratch, args, carry=None):
    n_in, n_out, n_scr = len(in_specs), len(out_specs), len(scratch)
    c_in = len(carry.in_specs) if carry else 0
    c_out = len(carry.out_specs) if carry else 0

    def full_body(*refs):
        ins = refs[:n_in]
        c_ins = refs[n_in:n_in + c_in]
        outs = refs[n_in + c_in:n_in + c_in + n_out]
        c_outs = refs[n_in + c_in + n_out:n_in + c_in + n_out + c_out]
        scr = refs[n_in + c_in + n_out + c_out:n_in + c_in + n_out + c_out + n_scr]
        c_scr = refs[n_in + c_in + n_out + c_out + n_scr:]
        if carry:
            first = functools.reduce(lambda a, b: a & b, [pl.program_id(d) == 0 for d in range(len(grid))])
            last = functools.reduce(lambda a, b: a & b, [pl.program_id(d) == g - 1 for d, g in enumerate(grid)])

            @pl.when(first)
            def _():
                carry.start(c_ins, c_outs, c_scr)

        body(*ins, *outs, *scr)
        if carry:
            @pl.when(last)
            def _():
                carry.finish(c_ins, c_outs, c_scr)

    res = pl.pallas_call(
        full_body, name=name, grid=grid,
        in_specs=list(in_specs) + (list(carry.in_specs) if carry else []),
        out_specs=list(out_specs) + (list(carry.out_specs) if carry else []),
        out_shape=list(out_shape) + (list(carry.out_shape) if carry else []),
        scratch_shapes=list(scratch) + (list(carry.scratch) if carry else []),
        compiler_params=_cparams(("arbitrary",) * len(grid)),
    )(*args, *(carry.inputs if carry else []))
    return res[:n_out], res[n_out:]


def _gelu(x):
    u = 0.7978845608028654 * (x + 0.044715 * x * x * x)
    return 0.5 * x * (1.0 + jnp.tanh(u))


def _gelu_and_grad(x):
    x2 = x * x
    u = 0.7978845608028654 * (x + 0.044715 * x * x2)
    t = jnp.tanh(u)
    g = 0.5 * x * (1.0 + t)
    dg = 0.5 * (1.0 + t) + 0.5 * x * (1.0 - t * t) * 0.7978845608028654 * (1.0 + 3.0 * 0.044715 * x2)
    return g, dg


def _sigmoid(x):
    return 1.0 / (1.0 + jnp.exp(-x))


def _softplus(x):
    return jnp.maximum(x, 0.0) + jnp.log1p(jnp.exp(-jnp.abs(x)))


def _neg_expm1(x):
    series = -x * (1.0 + x * (0.5 + x * (1.0 / 6.0 + x * (1.0 / 24.0 + x * (1.0 / 120.0)))))
    return jnp.where(x > -0.1, series, 1.0 - jnp.exp(x))


def _dot(a, b):
    return jnp.dot(a.astype(BF16), b.astype(BF16), preferred_element_type=F32)


def _dot_nt(a, b):
    return lax.dot_general(a.astype(BF16), b.astype(BF16), (((1,), (1,)), ((), ())), preferred_element_type=F32)


def _dot_tn(a, b):
    return lax.dot_general(a.astype(BF16), b.astype(BF16), (((0,), (0,)), ((), ())), preferred_element_type=F32)


def _rows(shape):
    return lax.broadcasted_iota(jnp.int32, shape, 0)


def _shift_down(cur, prev8, s):
    if s == 0:
        return cur
    n = cur.shape[0]
    r = pltpu.roll(cur, s, 0)
    p = pltpu.roll(prev8, s, 0)
    top = jnp.where(_rows(p.shape) < s, p, r[0:SUBLANES])
    if n == SUBLANES:
        return top
    return jnp.concatenate([top, r[SUBLANES:]], axis=0)


def _shift_up(cur, next8, s):
    if s == 0:
        return cur
    n = cur.shape[0]
    r = pltpu.roll(cur, n - s, 0)
    q = pltpu.roll(next8, SUBLANES - s, 0)
    bot = jnp.where(_rows(q.shape) >= SUBLANES - s, q, r[n - SUBLANES:])
    if n == SUBLANES:
        return bot
    return jnp.concatenate([r[:n - SUBLANES], bot], axis=0)


def _scan_fwd(a, b):
    n = a.shape[0]
    rows = _rows(a.shape)
    s = 1
    while s < n:
        a_s = pltpu.roll(a, s, 0)
        b_s = pltpu.roll(b, s, 0)
        m = rows >= s
        b = jnp.where(m, a * b_s + b, b)
        a = jnp.where(m, a * a_s, a)
        s *= 2
    return a, b


def _scan_rev(a, b):
    n = a.shape[0]
    rows = _rows(a.shape)
    s = 1
    while s < n:
        a_s = pltpu.roll(a, n - s, 0)
        b_s = pltpu.roll(b, n - s, 0)
        m = rows < n - s
        b = jnp.where(m, b + a * b_s, b)
        a = jnp.where(m, a * a_s, a)
        s *= 2
    return a, b


def _rms(x):
    r = lax.rsqrt(jnp.mean(x * x, axis=-1, keepdims=True) + EPS)
    return x * r, r


def _rms_bwd(d_n, n, r):
    return r * (d_n - n * jnp.mean(d_n * n, axis=-1, keepdims=True))


def _colsum(x):
    return jnp.sum(x, axis=0, keepdims=True)


def _lru_gates(xc, wr_ref, wi_ref, br, bi, sp_a):
    r = _sigmoid(_dot(xc, wr_ref[...]) + br)
    i = _sigmoid(_dot(xc, wi_ref[...]) + bi)
    la = -LRU_C * r * sp_a
    a = jnp.exp(la)
    mult = jnp.sqrt(_neg_expm1(2.0 * la))
    return r, i, a, mult


def _lru_conv(lx, prev8, cw_ref, cb):
    xc = cb + cw_ref[LRU_CONV_K - 1:LRU_CONV_K, :] * lx
    taps = []
    for k in range(LRU_CONV_K - 1):
        tap = _shift_down(lx, prev8, LRU_CONV_K - 1 - k)
        taps.append(tap)
        xc = xc + cw_ref[k:k + 1, :] * tap
    return xc, taps


def _ws_mask(transposed=False):
    i = lax.broadcasted_iota(jnp.int32, (POS_BLOCK, POS_BLOCK), 0)
    j = lax.broadcasted_iota(jnp.int32, (POS_BLOCK, POS_BLOCK), 1)
    if transposed:
        i, j = j, i
    return (j // CHUNK) <= (i // CHUNK)


def _gmlp_v(gv, vg, vb):
    av, dav = _gelu_and_grad(gv)
    mu = jnp.mean(av, axis=-1, keepdims=True)
    cen = av - mu
    rs = lax.rsqrt(jnp.mean(cen * cen, axis=-1, keepdims=True) + EPS)
    vhat = cen * rs
    return vhat * vg + vb, vhat, rs, dav


def _mix_fwd(x, sh, sc, g_pre, w_in, conv_w, conv_b, wr_bd, wi_bd, b_r, b_i, lru_a, vn_g, vn_b, w_sp, b_sp_t,
             g_lru, g_gmlp, w_out, g_post, gt_m, g_ffn_pre, sc_f, sh_f, carry=None):
    s_len = x.shape[0]
    tt = min(TT_MIX, s_len)
    nblk = tt // POS_BLOCK

    def body(x_ref, sh_ref, sc_ref, g_ref, w_ref, cw_ref, cb_ref, wr_ref, wi_ref, br_ref, bi_ref, la_ref, vg_ref,
             vb_ref, ws_ref, bst_ref, gl_ref, gg_ref, wo_ref, gp_ref, gtm_ref, g2_ref, scf_ref, shf_ref,
             z_ref, h_ref, y_ref, hl_ref, yo_ref, x1_ref, h2_ref, prev8, hcar):
        i = pl.program_id(0)

        @pl.when(i == 0)
        def _():
            prev8[...] = jnp.zeros_like(prev8)
            hcar[...] = jnp.zeros_like(hcar)

        n_x, _ = _rms(x_ref[...])
        h = (n_x * g_ref[...] * (1.0 + sc_ref[...]) + sh_ref[...]).astype(BF16)
        h_ref[...] = h
        z_ref[...] = jnp.dot(h, w_ref[...], preferred_element_type=F32)

        lx = z_ref[:, 0:LRU_W]
        gate = z_ref[:, LRU_W:2 * LRU_W]
        gu = z_ref[:, 2 * LRU_W:2 * LRU_W + GMLP_W]
        gv = z_ref[:, 2 * LRU_W + GMLP_W:]

        xc, _ = _lru_conv(lx, prev8[...], cw_ref, cb_ref[...])
        prev8[...] = lx[tt - SUBLANES:]
        sp_a = _softplus(-la_ref[...])
        _, ig, a, mult = _lru_gates(xc, wr_ref, wi_ref, br_ref[...], bi_ref[...], sp_a)
        bx = mult * (ig * xc)
        a_cum, b_cum = _scan_fwd(a, bx)
        hl = a_cum * hcar[0:1, :] + b_cum
        hcar[...] = jnp.broadcast_to(hl[tt - 1:tt, :], hcar.shape)
        hl_ref[...] = hl
        y_lru = hl * _gelu(gate)
        n_l, _ = _rms(y_lru)
        y_ref[:, 0:LRU_W] = (n_l * gl_ref[...]).astype(BF16)

        u = _gelu(gu)
        v, _, _, _ = _gmlp_v(gv, vg_ref[...], vb_ref[...])
        mask = _ws_mask()
        sp_parts = []
        for nb in range(nblk):
            row = []
            for g in range(N_GROUPS):
                wsm = jnp.where(mask, ws_ref[g], 0.0)
                vblk = v[nb * POS_BLOCK:(nb + 1) * POS_BLOCK, g * LANES:(g + 1) * LANES]
                row.append(_dot(wsm, vblk) + bst_ref[:, g:g + 1])
            sp_parts.append(jnp.concatenate(row, axis=1))
        sp = jnp.concatenate(sp_parts, axis=0) if nblk > 1 else sp_parts[0]
        n_g, _ = _rms(u * sp)
        y_ref[:, LRU_W:] = (n_g * gg_ref[...]).astype(BF16)

        y = jnp.dot(y_ref[...], wo_ref[...], preferred_element_type=F32)
        yo_ref[...] = y
        n_y, _ = _rms(y)
        x1 = x_ref[...] + gtm_ref[...] * (n_y * gp_ref[...])
        x1_ref[...] = x1
        n1, _ = _rms(x1)
        h2_ref[...] = (n1 * g2_ref[...] * (1.0 + scf_ref[...]) + shf_ref[...]).astype(BF16)

    row = lambda c: pl.BlockSpec((tt, c), lambda i: (i, 0))
    v512 = _const((1, LRU_W))
    vec = _const((1, D_MODEL))
    return _call(
        body, "mix_fwd", (s_len // tt,),
        in_specs=[row(D_MODEL), vec, vec, vec, _whole(),
                  _const((LRU_CONV_K, LRU_W)), v512, _whole(), _whole(), v512, v512, v512, v512, v512,
                  _whole(), _whole(), v512, v512, _whole(), vec, vec, vec, vec, vec],
        out_specs=[row(IN_COLS), row(D_MODEL), row(LRU_W + GMLP_W), row(LRU_W), row(D_MODEL), row(D_MODEL),
                   row(D_MODEL)],
        out_shape=[_sds((s_len, IN_COLS), F32), _sds((s_len, D_MODEL), BF16),
                   _sds((s_len, LRU_W + GMLP_W), BF16), _sds((s_len, LRU_W), F32),
                   _sds((s_len, D_MODEL), F32), _sds((s_len, D_MODEL), F32), _sds((s_len, D_MODEL), BF16)],
        scratch=[pltpu.VMEM((SUBLANES, LRU_W), F32), pltpu.VMEM((SUBLANES, LRU_W), F32)],
        args=(x, sh, sc, g_pre, w_in, conv_w, conv_b, wr_bd, wi_bd, b_r, b_i, lru_a, vn_g, vn_b, w_sp, b_sp_t,
              g_lru, g_gmlp, w_out, g_post, gt_m, g_ffn_pre, sc_f, sh_f), carry=carry)


FF_CHUNKS = N_DEV // 2
FF_CHUNK_W = D_FF // FF_CHUNKS


def _ffn_fwd(h2, x1, w_up3, ffn_cw, ffn_cb, w_down, gt_f, g_post, target, carry=None):
    s_len = x1.shape[0]
    tt = min(TT_MIX, s_len)
    nc, cw = FF_CHUNKS, FF_CHUNK_W

    def body(h2_ref, x1_ref, wu_ref, cwg_ref, cwv_ref, cbg_ref, cbv_ref, wd_ref, gtf_ref, gp_ref, tg_ref,
             up_ref, upc_ref, act_ref, dy2_ref, dout_ref, loss_ref, vs_ref, acc, prev):
        i = pl.program_id(0)
        c = pl.program_id(1)

        @pl.when(i == 0)
        def _():
            prev[c] = jnp.zeros((2, SUBLANES, cw), F32)

        @pl.when((i == 0) & (c == 0))
        def _():
            loss_ref[...] = jnp.zeros_like(loss_ref)
            vs_ref[...] = jnp.zeros_like(vs_ref)

        h2 = h2_ref[...]
        ug_pre = jnp.dot(h2, wu_ref[c], preferred_element_type=F32)
        uv_pre = jnp.dot(h2, wu_ref[nc + c], preferred_element_type=F32)
        up_ref[0] = ug_pre.astype(BF16)
        up_ref[1] = uv_pre.astype(BF16)
        ug, _ = _ffn_conv(ug_pre, prev[c, 0], cwg_ref, cbg_ref[...])
        uv, _ = _ffn_conv(uv_pre, prev[c, 1], cwv_ref, cbv_ref[...])
        prev[c, 0] = ug_pre[tt - SUBLANES:, :]
        prev[c, 1] = uv_pre[tt - SUBLANES:, :]
        upc_ref[0] = ug
        upc_ref[1] = uv
        act = (_gelu(ug) * uv).astype(BF16)
        act_ref[...] = act
        part = jnp.dot(act, wd_ref[pl.ds(pl.multiple_of(c * cw, cw), cw), :], preferred_element_type=F32)

        @pl.when(c == 0)
        def _():
            acc[...] = part

        @pl.when(c > 0)
        def _():
            acc[...] += part

        @pl.when(c == nc - 1)
        def _():
            n2, r2 = _rms(acc[...])
            out = x1_ref[...] + gtf_ref[...] * (n2 * gp_ref[...])
            err = out - tg_ref[...]
            do = err * (1.0 / D_MODEL)
            dout_ref[...] = do
            loss_ref[...] += jnp.broadcast_to(0.5 * jnp.sum(err * err, keepdims=True) * (1.0 / D_MODEL), loss_ref.shape)
            vs_ref[0:1, :] += _colsum(do * n2 * gp_ref[...])
            vs_ref[1:2, :] += _colsum(do * gtf_ref[...] * n2)
            dy2_ref[...] = _rms_bwd(do * gtf_ref[...] * gp_ref[...], n2, r2).astype(BF16)

    row = pl.BlockSpec((tt, D_MODEL), lambda i, c: (i, 0))
    vec = _const((1, D_MODEL))
    chunk2 = pl.BlockSpec((2, tt, cw), lambda i, c: (0, i, c))
    ffn_cb2 = ffn_cb.reshape(1, 2 * D_FF)
    return _call(
        body, "ffn_fwd", (s_len // tt, nc),
        in_specs=[row, row, _whole(),
                  pl.BlockSpec((FFN_CONV_K, cw), lambda i, c: (0, c)),
                  pl.BlockSpec((FFN_CONV_K, cw), lambda i, c: (0, c + nc)),
                  pl.BlockSpec((1, cw), lambda i, c: (0, c)),
                  pl.BlockSpec((1, cw), lambda i, c: (0, c + nc)),
                  _whole(), vec, vec, row],
        out_specs=[chunk2, chunk2, pl.BlockSpec((tt, cw), lambda i, c: (i, c)), row, row,
                   _const((SUBLANES, LANES)), _const((SUBLANES, D_MODEL))],
        out_shape=[_sds((2, s_len, D_FF), BF16), _sds((2, s_len, D_FF), F32), _sds((s_len, D_FF), BF16),
                   _sds((s_len, D_MODEL), BF16), _sds((s_len, D_MODEL), F32),
                   _sds((SUBLANES, LANES), F32), _sds((SUBLANES, D_MODEL), F32)],
        scratch=[pltpu.VMEM((tt, D_MODEL), F32), pltpu.VMEM((nc, 2, SUBLANES, cw), F32)],
        args=(h2, x1, w_up3, ffn_cw, ffn_cw, ffn_cb2, ffn_cb2, w_down, gt_f, g_post, target), carry=carry)


def _ffn_conv(up_pre, prev8, cw_ref, cb):
    up = cb + cw_ref[FFN_CONV_K - 1:FFN_CONV_K, :] * up_pre
    taps = []
    for k in range(FFN_CONV_K - 1):
        tap = _shift_down(up_pre, prev8, FFN_CONV_K - 1 - k)
        taps.append(tap)
        up = up + cw_ref[k:k + 1, :] * tap
    return up, taps


def _ffn_bwd(d_y2, up_pre, up, ffn_cw, w_down, carry=None):
    s_len = d_y2.shape[0]
    tt = min(TT_BIG, s_len)
    nt = s_len // tt
    cw = FF_CW
    nc = D_FF // cw

    def body(dy2_ref, up_ref, upc_ref, cwg_ref, cwv_ref, wd_ref, dup_ref, cs_ref, nxt, cs_acc):
        i = pl.program_id(0)
        c = pl.program_id(1)

        @pl.when(i == 0)
        def _():
            nxt[c] = jnp.zeros((2, SUBLANES, cw), F32)
            cs_acc[c] = jnp.zeros((2, SUBLANES, cw), F32)

        d_act = _dot_nt(dy2_ref[...], wd_ref[...])
        uv = upc_ref[1]
        gl, dgl = _gelu_and_grad(upc_ref[0])
        d_ug = d_act * uv * dgl
        d_uv = d_act * gl
        for half, (d_u, cw_ref) in enumerate(((d_ug, cwg_ref), (d_uv, cwv_ref))):
            nx = nxt[c, half]
            x_in = up_ref[half].astype(F32)
            d_pre = cw_ref[FFN_CONV_K - 1:FFN_CONV_K, :] * d_u
            sums = [None] * (FFN_CONV_K + 1)
            sums[FFN_CONV_K - 1] = _colsum(d_u * x_in)
            for k in range(FFN_CONV_K - 1):
                ahead = _shift_up(d_u, nx, FFN_CONV_K - 1 - k)
                d_pre = d_pre + cw_ref[k:k + 1, :] * ahead
                sums[k] = _colsum(ahead * x_in)
            sums[FFN_CONV_K] = _colsum(d_u)
            pad = jnp.zeros((SUBLANES - FFN_CONV_K - 1, cw), F32)
            cs_acc[c, half] += jnp.concatenate(sums + [pad], axis=0)
            nxt[c, half] = d_u[0:SUBLANES]
            dup_ref[half] = d_pre.astype(BF16)

        for cc in range(nc):
            @pl.when((i == nt - 1) & (c == cc))
            def _():
                cs_ref[:, cc * cw:(cc + 1) * cw] = cs_acc[cc, 0]
                cs_ref[:, D_FF + cc * cw:D_FF + (cc + 1) * cw] = cs_acc[cc, 1]

    row = pl.BlockSpec((tt, D_MODEL), lambda i, c: (nt - 1 - i, 0))
    blk = pl.BlockSpec((2, tt, cw), lambda i, c: (0, nt - 1 - i, c))
    return _call(
        body, "ffn_bwd", (nt, nc),
        in_specs=[row, blk, blk,
                  pl.BlockSpec((FFN_CONV_K, cw), lambda i, c: (0, c)),
                  pl.BlockSpec((FFN_CONV_K, cw), lambda i, c: (0, c + nc)),
                  pl.BlockSpec((cw, D_MODEL), lambda i, c: (c, 0))],
        out_specs=[blk, _const((SUBLANES, 2 * D_FF))],
        out_shape=[_sds((2, s_len, D_FF), BF16), _sds((SUBLANES, 2 * D_FF), F32)],
        scratch=[pltpu.VMEM((nc, 2, SUBLANES, cw), F32), pltpu.VMEM((nc, 2, SUBLANES, cw), F32)],
        args=(d_y2, up_pre, up, ffn_cw, ffn_cw, w_down), carry=carry)


def _up_bwd(d_up, w_up3, x1, dout, y, w_out, g_pre, sc_f, g_post, gt_m, carry=None):
    s_len = x1.shape[0]
    tt = min(TT_BIG, s_len)

    def body(du_ref, wu_ref, x1_ref, do_ref, y_ref, wo_ref, g2_ref, sc_ref, gp_ref, gt_ref,
             dx1_ref, dy_ref, dyc_ref, vs_ref):
        @pl.when(pl.program_id(0) == 0)
        def _():
            vs_ref[...] = jnp.zeros_like(vs_ref)

        d_h2 = jnp.zeros((tt, D_MODEL), F32)
        for half in range(2):
            for ch in range(FF_CHUNKS):
                d_h2 = d_h2 + _dot_nt(du_ref[half, :, ch * FF_CHUNK_W:(ch + 1) * FF_CHUNK_W],
                                      wu_ref[half * FF_CHUNKS + ch])
        n1, r1 = _rms(x1_ref[...])
        ng = n1 * g2_ref[...]
        vs_ref[0:1, :] += _colsum(d_h2)
        vs_ref[1:2, :] += _colsum(d_h2 * ng)
        d_ng = d_h2 * (1.0 + sc_ref[...])
        vs_ref[2:3, :] += _colsum(d_ng * n1)
        d_x1 = do_ref[...] + _rms_bwd(d_ng * g2_ref[...], n1, r1)
        dx1_ref[...] = d_x1
        n_y, r_y = _rms(y_ref[...])
        vs_ref[3:4, :] += _colsum(d_x1 * n_y * gp_ref[...])
        d_on = d_x1 * gt_ref[...]
        vs_ref[4:5, :] += _colsum(d_on * n_y)
        d_y = _rms_bwd(d_on * gp_ref[...], n_y, r_y).astype(BF16)
        dy_ref[...] = d_y
        dyc_ref[...] = _dot_nt(d_y, wo_ref[...])

    row = lambda c: pl.BlockSpec((tt, c), lambda i: (i, 0))
    vec = _const((1, D_MODEL))
    return _call(
        body, "up_bwd", (s_len // tt,),
        in_specs=[pl.BlockSpec((2, tt, D_FF), lambda i: (0, i, 0)), _whole(), row(D_MODEL), row(D_MODEL), row(D_MODEL),
                  _whole(), vec, vec, vec, vec],
        out_specs=[row(D_MODEL), row(D_MODEL), row(LRU_W + GMLP_W), _const((SUBLANES, D_MODEL))],
        out_shape=[_sds((s_len, D_MODEL), F32), _sds((s_len, D_MODEL), BF16), _sds((s_len, LRU_W + GMLP_W), F32),
                   _sds((SUBLANES, D_MODEL), F32)],
        scratch=[], args=(d_up, w_up3, x1, dout, y, w_out, g_pre, sc_f, g_post, gt_m), carry=carry)


def _head_pair_block(hd):
    return (slice((hd // 2) * HEAD_DIM, (hd // 2 + 1) * HEAD_DIM), slice((hd % 2) * HEAD_DIM, (hd % 2 + 1) * HEAD_DIM))


def _mix_bwd(d_ycat, z, hl, conv_w, conv_b, wr_bd, wi_bd, b_r, b_i, lru_a, vn_g, vn_b, w_sp, w_sp_t, b_sp_t,
             g_lru, g_gmlp, carry=None):
    s_len = z.shape[0]
    tt = min(TT_MIX, s_len)
    nt = s_len // tt
    nblk = tt // POS_BLOCK
    hb = tt // SUBLANES

    def body(dyc_ref, z_ref, zh_ref, hl_ref, hh_ref, cw_ref, cb_ref, wr_ref, wi_ref, br_ref, bi_ref, la_ref,
             vg_ref, vb_ref, ws_ref, wst_ref, bst_ref, gl_ref, gg_ref,
             dz_ref, vs_ref, dcw_ref, dwrb_ref, dwib_ref, dws_ref, dbs_ref, nxt_dxc, nxt_a, nxt_lam, dwr_ref, dwi_ref):
        i = pl.program_id(0)
        first_tile = i == nt - 1

        @pl.when(i == 0)
        def _():
            for ref in (vs_ref, dcw_ref, dwr_ref, dwi_ref, dws_ref, dbs_ref, nxt_dxc, nxt_a, nxt_lam):
                ref[...] = jnp.zeros_like(ref)

        lx = z_ref[:, 0:LRU_W]
        gate = z_ref[:, LRU_W:2 * LRU_W]
        gu = z_ref[:, 2 * LRU_W:2 * LRU_W + GMLP_W]
        gv = z_ref[:, 2 * LRU_W + GMLP_W:]
        prev8 = jnp.where(first_tile, 0.0, zh_ref[...])
        hprev8 = jnp.where(first_tile, 0.0, hh_ref[...])

        xc, taps = _lru_conv(lx, prev8, cw_ref, cb_ref[...])
        a_par = la_ref[...]
        sp_a = _softplus(-a_par)
        r, ig, a, mult = _lru_gates(xc, wr_ref, wi_ref, br_ref[...], bi_ref[...], sp_a)
        hl = hl_ref[...]
        h_prev = _shift_down(hl, hprev8, 1)
        ggate, dggate = _gelu_and_grad(gate)
        y_lru = hl * ggate
        n_l, r_l = _rms(y_lru)
        d_nl = dyc_ref[:, 0:LRU_W]
        vs_ref[6:7, :] += _colsum(d_nl * n_l)
        d_yl = _rms_bwd(d_nl * gl_ref[...], n_l, r_l)
        d_hl = d_yl * ggate
        d_gate = d_yl * hl * dggate
        a_up = _shift_up(a, nxt_a[...], 1)
        a_cum, b_cum = _scan_rev(a_up, d_hl)
        lam = b_cum + a_cum * nxt_lam[0:1, :]
        nxt_a[...] = jnp.broadcast_to(a[0:1, :], nxt_a.shape)
        nxt_lam[...] = jnp.broadcast_to(lam[0:1, :], nxt_lam.shape)
        ixc = ig * xc
        d_la = lam * h_prev * a - lam * ixc * (a * a) / mult
        d_i = lam * mult * xc
        d_xc = lam * mult * ig
        vs_ref[3:4, :] += _colsum(d_la * r) * (LRU_C * _sigmoid(-a_par))
        d_pr = d_la * (-LRU_C * sp_a) * r * (1.0 - r)
        d_pi = d_i * ig * (1.0 - ig)
        vs_ref[1:2, :] += _colsum(d_pr)
        vs_ref[2:3, :] += _colsum(d_pi)
        dwr_ref[...] += _dot_tn(xc, d_pr)
        dwi_ref[...] += _dot_tn(xc, d_pi)
        d_xc = d_xc + _dot_nt(d_pr, wr_ref[...]) + _dot_nt(d_pi, wi_ref[...])
        vs_ref[0:1, :] += _colsum(d_xc)
        nx = nxt_dxc[...]
        d_lx = cw_ref[LRU_CONV_K - 1:LRU_CONV_K, :] * d_xc
        dcw_ref[LRU_CONV_K - 1:LRU_CONV_K, :] += _colsum(d_xc * lx)
        for k in range(LRU_CONV_K - 1):
            d_lx = d_lx + cw_ref[k:k + 1, :] * _shift_up(d_xc, nx, LRU_CONV_K - 1 - k)
            dcw_ref[k:k + 1, :] += _colsum(d_xc * taps[k])
        nxt_dxc[...] = d_xc[0:SUBLANES]
        dz_ref[:, 0:LRU_W] = d_lx.astype(BF16)
        dz_ref[:, LRU_W:2 * LRU_W] = d_gate.astype(BF16)

        u, du = _gelu_and_grad(gu)
        v, vhat, rs, dav = _gmlp_v(gv, vg_ref[...], vb_ref[...])
        mask = _ws_mask()
        sp_parts = []
        for nb in range(nblk):
            rowp = []
            for g in range(N_GROUPS):
                wsm = jnp.where(mask, ws_ref[g], 0.0)
                vblk = v[nb * POS_BLOCK:(nb + 1) * POS_BLOCK, g * LANES:(g + 1) * LANES]
                rowp.append(_dot(wsm, vblk) + bst_ref[:, g:g + 1])
            sp_parts.append(jnp.concatenate(rowp, axis=1))
        sp = jnp.concatenate(sp_parts, axis=0) if nblk > 1 else sp_parts[0]
        y_g = u * sp
        n_g, r_g = _rms(y_g)
        d_ng = dyc_ref[:, LRU_W:]
        vs_ref[7:8, :] += _colsum(d_ng * n_g)
        d_yg = _rms_bwd(d_ng * gg_ref[...], n_g, r_g)
        d_gu = d_yg * sp * du
        d_sp = d_yg * u
        mask_t = _ws_mask(transposed=True)
        ones8 = jnp.ones((SUBLANES, LANES), F32)
        dv_parts = []
        for nb in range(nblk):
            rowp = []
            for g in range(N_GROUPS):
                rs_, cs_ = slice(nb * POS_BLOCK, (nb + 1) * POS_BLOCK), slice(g * LANES, (g + 1) * LANES)
                dsp_blk = d_sp[rs_, cs_]
                dbs_ref[g:g + 1, :] += lax.dot_general(
                    ones8, dsp_blk, (((1,), (1,)), ((), ())), preferred_element_type=F32,
                    precision=lax.Precision.HIGHEST)[0:1, :]
                dws_ref[g] += _dot_nt(dsp_blk, v[rs_, cs_])
                wsm_t = jnp.where(mask_t, wst_ref[g], 0.0)
                rowp.append(_dot(wsm_t, dsp_blk))
            dv_parts.append(jnp.concatenate(rowp, axis=1))
        d_v = jnp.concatenate(dv_parts, axis=0) if nblk > 1 else dv_parts[0]
        vs_ref[4:5, :] += _colsum(d_v * vhat)
        vs_ref[5:6, :] += _colsum(d_v)
        d_vh = d_v * vg_ref[...]
        d_av = rs * (d_vh - jnp.mean(d_vh, axis=-1, keepdims=True)
                     - vhat * jnp.mean(d_vh * vhat, axis=-1, keepdims=True))
        dz_ref[:, 2 * LRU_W:2 * LRU_W + GMLP_W] = d_gu.astype(BF16)
        dz_ref[:, 2 * LRU_W + GMLP_W:] = (d_av * dav).astype(BF16)

        @pl.when(i == nt - 1)
        def _():
            for hd in range(N_HEADS):
                blk = slice(hd * HEAD_DIM, (hd + 1) * HEAD_DIM)
                dwrb_ref[_head_pair_block(hd)] = dwr_ref[blk, blk]
                dwib_ref[_head_pair_block(hd)] = dwi_ref[blk, blk]
            for g in range(N_GROUPS):
                dws_ref[g] = jnp.where(mask, dws_ref[g], 0.0)

    rev = lambda c: pl.BlockSpec((tt, c), lambda i: (nt - 1 - i, 0))
    halo = pl.BlockSpec((SUBLANES, LRU_W), lambda i: (jnp.maximum((nt - 1 - i) * hb - 1, 0), 0))
    v512 = _const((1, LRU_W))
    return _call(
        body, "mix_bwd", (nt,),
        in_specs=[rev(LRU_W + GMLP_W), rev(IN_COLS), halo, rev(LRU_W), halo,
                  _const((LRU_CONV_K, LRU_W)), v512, _whole(), _whole(), v512, v512, v512, v512, v512,
                  _whole(), _whole(), _whole(), v512, v512],
        out_specs=[rev(IN_COLS), _const((SUBLANES, LRU_W)), _const((SUBLANES, LRU_W)),
                   _const((LRU_W // 2, 2 * HEAD_DIM)), _const((LRU_W // 2, 2 * HEAD_DIM)),
                   _const((N_GROUPS, POS_BLOCK, POS_BLOCK)), _const((SUBLANES, POS_BLOCK))],
        out_shape=[_sds((s_len, IN_COLS), BF16), _sds((SUBLANES, LRU_W), F32), _sds((SUBLANES, LRU_W), F32),
                   _sds((LRU_W // 2, 2 * HEAD_DIM), F32), _sds((LRU_W // 2, 2 * HEAD_DIM), F32),
                   _sds((N_GROUPS, POS_BLOCK, POS_BLOCK), F32), _sds((SUBLANES, POS_BLOCK), F32)],
        scratch=[pltpu.VMEM((SUBLANES, LRU_W), F32), pltpu.VMEM((SUBLANES, LRU_W), F32),
                 pltpu.VMEM((SUBLANES, LRU_W), F32), pltpu.VMEM((LRU_W, LRU_W), F32), pltpu.VMEM((LRU_W, LRU_W), F32)],
        args=(d_ycat, z, z, hl, hl, conv_w, conv_b, wr_bd, wi_bd, b_r, b_i, lru_a, vn_g, vn_b, w_sp, w_sp_t, b_sp_t,
              g_lru, g_gmlp), carry=carry)


def _in_bwd(d_z, w_in, x, d_x1, g, sc, carry=None):
    s_len = x.shape[0]
    tt = min(TT_BIG, s_len)

    def body(dz_ref, w_ref, x_ref, dx1_ref, g_ref, sc_ref, gx_ref, vs_ref):
        @pl.when(pl.program_id(0) == 0)
        def _():
            vs_ref[...] = jnp.zeros_like(vs_ref)

        d_h = _dot_nt(dz_ref[...], w_ref[...])
        n, r = _rms(x_ref[...])
        vs_ref[0:1, :] += _colsum(d_h)
        vs_ref[1:2, :] += _colsum(d_h * n * g_ref[...])
        d_ng = d_h * (1.0 + sc_ref[...])
        vs_ref[2:3, :] += _colsum(d_ng * n)
        gx_ref[...] = dx1_ref[...] + _rms_bwd(d_ng * g_ref[...], n, r)

    row = lambda c: pl.BlockSpec((tt, c), lambda i: (i, 0))
    vec = _const((1, D_MODEL))
    return _call(
        body, "in_bwd", (s_len // tt,),
        in_specs=[row(IN_COLS), _whole(), row(D_MODEL), row(D_MODEL), vec, vec],
        out_specs=[row(D_MODEL), _const((SUBLANES, D_MODEL))],
        out_shape=[_sds((s_len, D_MODEL), F32), _sds((SUBLANES, D_MODEL), F32)],
        scratch=[], args=(d_z, w_in, x, d_x1, g, sc), carry=carry)


def _wgrad(a, b, tn, name, carry=None):
    s_len, k_dim = a.shape
    halves = b.ndim == 3
    n_dim = b.shape[-1] * (2 if halves else 1)
    ts = min(TT_WG, s_len)
    nj = n_dim // tn
    nt = s_len // ts

    def body(a_ref, b_ref, o_ref, ob_ref):
        t = pl.program_id(1)
        part = _dot_tn(a_ref[...], b_ref[0] if halves else b_ref[...])

        @pl.when(t == 0)
        def _():
            o_ref[...] = part

        @pl.when(t > 0)
        def _():
            o_ref[...] += part

        @pl.when(t == nt - 1)
        def _():
            ob_ref[...] = o_ref[...].astype(BF16)

    if halves:
        per_half = nj // 2
        b_spec = pl.BlockSpec((1, ts, tn), lambda j, t: (j // per_half, t, j % per_half))
    else:
        b_spec = pl.BlockSpec((ts, tn), lambda j, t: (t, j))
    o_spec = pl.BlockSpec((k_dim, tn), lambda j, t: (0, j))
    return _call(
        body, name, (nj, nt),
        in_specs=[pl.BlockSpec((ts, k_dim), lambda j, t: (t, 0)), b_spec],
        out_specs=[o_spec, o_spec],
        out_shape=[_sds((k_dim, n_dim), F32), _sds((k_dim, n_dim), BF16)],
        scratch=[], args=(a, b), carry=carry)


def _adam_math(w, g, m, v):
    m = ADAM_B1 * m + (1.0 - ADAM_B1) * g
    v = ADAM_B2 * v + (1.0 - ADAM_B2) * (g * g)
    m_hat = m / (1.0 - ADAM_B1 ** ADAM_STEP)
    v_hat = v / (1.0 - ADAM_B2 ** ADAM_STEP)
    delta = -ADAM_LR * (m_hat / (jnp.sqrt(v_hat) + ADAM_EPS) + ADAM_WD * w)
    return delta, m, v


def _row_tile(rows, cols, n_f32_arrays):
    budget = VMEM_LIMIT // 2
    tr = rows
    while tr % 2 == 0 and tr // 2 >= SUBLANES and (tr // 2) % SUBLANES == 0 and tr * cols * 4 * n_f32_arrays * 2 > budget:
        tr //= 2
    return tr


def _adamw_sum(w, g_full, recv, m, v, col_sharded, name):
    _, rows, cols = w.shape
    n_recv = len(recv)
    tr = _row_tile(rows, cols, 10)
    nb = rows // tr

    def body(me_ref, w_ref, g_ref, *rest):
        r_refs = rest[:n_recv]
        m_ref, v_ref, go_ref, d_ref, mo_ref, vo_ref = rest[n_recv:]
        g = g_ref[...]
        for r_ref in r_refs:
            for k in range(r_ref.shape[0]):
                g = g + r_ref[k].astype(F32)
        go_ref[0] = g
        d_ref[0], mo_ref[0], vo_ref[0] = _adam_math(w_ref[0], g, m_ref[0], v_ref[0])

    if col_sharded:
        own = pl.BlockSpec((tr, cols), lambda i, me: (i, me[0]))
    else:
        own = pl.BlockSpec((tr, cols), lambda i, me: (me[0] * nb + i, 0))
    blk = pl.BlockSpec((1, tr, cols), lambda i, me: (0, i, 0))
    return pl.pallas_call(
        body, name=name,
        grid_spec=pltpu.PrefetchScalarGridSpec(
            num_scalar_prefetch=1, grid=(nb,),
            in_specs=[blk, own] + [pl.BlockSpec((r.shape[0], tr, cols), lambda i, me: (0, i, 0)) for r in recv]
            + [blk, blk],
            out_specs=[blk] * 4),
        out_shape=[_sds((1, rows, cols), F32)] * 4,
        compiler_params=_cparams(("arbitrary",)),
    )(jnp.reshape(_dev_index(_my_pos()), (1,)).astype(jnp.int32), w, g_full, *recv, m, v)


def _row_of_each(ref, row):
    cols = ref.shape[1]
    rows = _rows((N_DEV, cols))
    out = jnp.zeros((N_DEV, cols), F32)
    for d in range(N_DEV):
        picked = ref[d * SUBLANES + row:d * SUBLANES + row + 1, :]
        out = jnp.where(rows == d, jnp.broadcast_to(picked, (N_DEV, cols)), out)
    return out


def _my_columns(full, width, me):
    out = jnp.zeros(full.shape[:-1] + (width,), F32)
    for d in range(N_DEV):
        out = out + jnp.where(me == d, full[:, d * width:(d + 1) * width], 0.0)
    return out


def _adamw_wada(c_all, vs_in_all, vs_up_all, vs_ffn_all, w, m, v):
    _, rows, cols = w.shape

    def body(c_ref, vi_ref, vu_ref, vf_ref, w_ref, m_ref, v_ref, go_ref, d_ref, mo_ref, vo_ref):
        me = _dev_index(_my_pos())
        cv = _row_of_each(c_ref, 0)
        ca = cv * _sigmoid(cv)
        dmod = jnp.concatenate([_row_of_each(vi_ref, 0), _row_of_each(vi_ref, 1), _row_of_each(vu_ref, 3),
                                _row_of_each(vu_ref, 0), _row_of_each(vu_ref, 1), _row_of_each(vf_ref, 0)], axis=1)
        dm = _my_columns(dmod, cols, me)
        g = lax.dot_general(ca, dm, (((0,), (0,)), ((), ())), preferred_element_type=F32,
                            precision=lax.Precision.HIGHEST)
        go_ref[0] = g
        d_ref[0], mo_ref[0], vo_ref[0] = _adam_math(w_ref[0], g, m_ref[0], v_ref[0])

    return pl.pallas_call(
        body, name="adamw_w_ada", out_shape=[_sds((1, rows, cols), F32)] * 4,
        in_specs=[_whole()] * 7, out_specs=[_whole()] * 4,
        compiler_params=_cparams(),
    )(c_all, vs_in_all, vs_up_all, vs_ffn_all, w, m, v)


def _adamw_small(gathered, reduced, params, conv_params):
    names = list(params) + list(conv_params)
    allp = {**params, **conv_params}
    n_g = len(gathered) + len(reduced)

    def body(*refs):
        g_refs = refs[:n_g]
        p_refs = refs[n_g:n_g + 3 * len(names)]
        o_refs = refs[n_g + 3 * len(names):]
        me = _dev_index(_my_pos())

        def total(ref):
            s = ref[0:SUBLANES, :]
            for d in range(1, N_DEV):
                s = s + ref[d * SUBLANES:(d + 1) * SUBLANES, :]
            return s

        vs_in, vs_up, vs_ffn, loss = [total(r) for r in g_refs[:4]]
        cs, vs_mix, dcw, dwr, dwi, dws, dbs = [r[...] for r in g_refs[4:]]
        o_refs[-1][...] = loss[0:1, 0:1]
        mine = lambda full, width: _my_columns(full, width, me)

        all_ = (slice(None), slice(None))
        heads = lambda row: [((0, slice(h, h + 1), slice(None)), row[:, h * HEAD_DIM:(h + 1) * HEAD_DIM])
                             for h in range(N_HEADS)]
        blocks = lambda pairs: [((0, h), pairs[_head_pair_block(h)]) for h in range(N_HEADS)]
        pieces = {
            "b_ada": [((slice(None), slice(k * D_MODEL, (k + 1) * D_MODEL)), row) for k, row in enumerate(
                (vs_in[0:1], vs_in[1:2], vs_up[3:4], vs_up[0:1], vs_up[1:2], vs_ffn[0:1]))],
            "g_mix_pre": [(all_, vs_in[2:3])], "g_mix_post": [(all_, vs_up[4:5])],
            "g_ffn_pre": [(all_, vs_up[2:3])], "g_ffn_post": [(all_, vs_ffn[1:2])],
            "conv_b": [(all_, vs_mix[0:1])], "b_rgate": heads(vs_mix[1:2]), "b_igate": heads(vs_mix[2:3]),
            "lru_a": [(all_, vs_mix[3:4])], "v_norm_g": [(all_, vs_mix[4:5])], "v_norm_b": [(all_, vs_mix[5:6])],
            "g_lru_out": [(all_, vs_mix[6:7])], "g_gmlp_out": [(all_, vs_mix[7:8])],
            "w_rgate": blocks(dwr), "w_igate": blocks(dwi),
            "w_spatial": [((0, g), dws[g * POS_BLOCK:(g + 1) * POS_BLOCK, :]) for g in range(N_GROUPS)],
            "b_spatial": [((0,), dbs[0:N_GROUPS])],
            "ffn_conv_b": [(all_, cs[FFN_CONV_K:FFN_CONV_K + 1])],
            "conv_w": [((0,), mine(dcw[0:LRU_CONV_K], LRU_W // N_DEV))],
            "ffn_conv_w": [((0,), mine(cs[0:FFN_CONV_K], 2 * D_FF // N_DEV))],
        }
        for n_i, name in enumerate(names):
            w_ref, m_ref, v_ref = p_refs[3 * n_i:3 * n_i + 3]
            go_ref, d_ref, mo_ref, vo_ref = o_refs[4 * n_i:4 * n_i + 4]
            for idx, g in pieces[name]:
                go_ref[idx] = g
                d_ref[idx], mo_ref[idx], vo_ref[idx] = _adam_math(w_ref[idx], g, m_ref[idx], v_ref[idx])

    flat_params = [a for n in names for a in allp[n]]
    out_shape = [_sds(allp[n][0].shape, F32) for n in names for _ in range(4)] + [_sds((1, 1), F32)]
    outs = pl.pallas_call(
        body, name="adamw_small", out_shape=out_shape,
        in_specs=[_whole()] * (n_g + len(flat_params)), out_specs=[_whole()] * len(out_shape),
        compiler_params=_cparams(),
    )(*gathered, *reduced, *flat_params)
    return {n: outs[4 * i:4 * i + 4] for i, n in enumerate(names)}, outs[-1]


def _my_pos():
    return lax.axis_index("x"), lax.axis_index("y"), lax.axis_index("c")


def _flip(pos, k):
    x, y, c = pos
    return (1 - x if k & 4 else x, 1 - y if k & 2 else y, 1 - c if k & 1 else c)


def _dev_index(pos):
    x, y, c = pos
    return 4 * x + 2 * y + c


def _all_gather_small(ins, outs, send_sems, recv_sems):
    n = len(ins)
    me = _my_pos()

    def slot(a, pos):
        rows = ins[a].shape[0]
        return outs[a].at[pl.ds(pl.multiple_of(_dev_index(pos) * rows, SUBLANES), rows), :]

    def copy(a, k, block):
        return pltpu.make_async_remote_copy(
            src_ref=ins[a], dst_ref=slot(a, block), send_sem=send_sems.at[a, k - 1], recv_sem=recv_sems.at[a, k - 1],
            device_id=_flip(me, k), device_id_type=MESH)

    sends = [copy(a, k, me) for a in range(n) for k in range(1, N_DEV)]
    for cp in sends:
        cp.start()
    for a in range(n):
        rows = ins[a].shape[0]
        outs[a][pl.ds(pl.multiple_of(_dev_index(me) * rows, SUBLANES), rows), :] = ins[a][...]
    for a in range(n):
        for k in range(1, N_DEV):
            copy(a, k, _flip(me, k)).wait_recv()
    for cp in sends:
        cp.wait_send()


def _prologue(c8, cw8, fcw8, w_ada, b_ada, carry):
    cols = w_ada.shape[1]

    def body(c_ref, cw_ref, fcw_ref, w_ref, b_ref, call_ref, cwall_ref, fcwall_ref, modall_ref, mod_scr,
             s1, r1, s2, r2):
        _all_gather_small([c_ref, cw_ref, fcw_ref], [call_ref, cwall_ref, fcwall_ref], s1, r1)
        cv = _row_of_each(call_ref, 0)
        ca = cv * _sigmoid(cv)
        b_cols = _my_columns(b_ref[...], cols, _dev_index(_my_pos()))
        mod_scr[...] = jnp.dot(ca, w_ref[...], preferred_element_type=F32, precision=lax.Precision.HIGHEST) + b_cols
        _all_gather_small([mod_scr], [modall_ref], s2, r2)

    sem = lambda n: pltpu.SemaphoreType.DMA((n, N_DEV - 1))
    return _call(
        body, "prologue", (1,), in_specs=[_whole()] * 5, out_specs=[_whole()] * 4,
        out_shape=[_sds((N_DEV * SUBLANES, a.shape[1]), F32) for a in (c8, cw8, fcw8)]
        + [_sds((N_DEV * N_DEV, cols), F32)],
        scratch=[pltpu.VMEM((N_DEV, cols), F32), sem(3), sem(3), sem(1), sem(1)],
        args=(c8, cw8, fcw8, w_ada, b_ada), carry=carry)


def _reduce_small(gath, red, carry=None):
    n_g, n_r = len(gath), len(red)
    chip_flips = (4, 2, 6)

    def body(*refs):
        g_in, r_in = refs[:n_g], refs[n_g:n_g + n_r]
        g_out, r_out = refs[n_g + n_r:2 * n_g + n_r], refs[2 * n_g + n_r:2 * (n_g + n_r)]
        scr = refs[2 * (n_g + n_r):]
        sib, land = scr[:n_r], scr[n_r:2 * n_r]
        g_send, g_recv, s_send, s_recv, i_send, i_recv, f_send, f_recv = scr[2 * n_r:]
        me = _my_pos()
        c = me[2]
        sibling = _flip(me, 1)

        def slot(a, pos):
            return g_out[a].at[pl.ds(pl.multiple_of(_dev_index(pos) * SUBLANES, SUBLANES), SUBLANES), :]

        def gcopy(a, k):
            return pltpu.make_async_remote_copy(
                src_ref=g_in[a], dst_ref=slot(a, me), send_sem=g_send.at[a, k - 1], recv_sem=g_recv.at[a, k - 1],
                device_id=_flip(me, k), device_id_type=MESH)

        def scopy(a):
            return pltpu.make_async_remote_copy(
                src_ref=r_in[a], dst_ref=sib[a], send_sem=s_send.at[a], recv_sem=s_recv.at[a],
                device_id=sibling, device_id_type=MESH)

        def icopy(a, j):
            return pltpu.make_async_remote_copy(
                src_ref=r_out[a], dst_ref=land[a].at[j], send_sem=i_send.at[a, j], recv_sem=i_recv.at[a, j],
                device_id=_flip(me, chip_flips[j]), device_id_type=MESH)

        def fcopy(a, j):
            return pltpu.make_async_remote_copy(
                src_ref=land[a].at[j], dst_ref=land[a].at[j], send_sem=f_send.at[a, j], recv_sem=f_recv.at[a, j],
                device_id=sibling, device_id_type=MESH)

        gathers = [gcopy(a, k) for a in range(n_g) for k in range(1, N_DEV)]
        swaps = [scopy(a) for a in range(n_r)]
        for cp in gathers + swaps:
            cp.start()
        for a in range(n_g):
            g_out[a][pl.ds(pl.multiple_of(_dev_index(me) * SUBLANES, SUBLANES), SUBLANES), :] = g_in[a][...]
        for a in range(n_r):
            swaps[a].wait_recv()
            r_out[a][...] = r_in[a][...] + sib[a][...]

        for core in range(2):
            mine = [a for a in range(n_r) if a % 2 == core]
            theirs = [a for a in range(n_r) if a % 2 != core]

            @pl.when(c == core)
            def _():
                out = [icopy(a, j) for a in mine for j in range(3)]
                for cp in out:
                    cp.start()
                fwd = []
                for a in mine:
                    for j in range(3):
                        icopy(a, j).wait_recv()
                        cp = fcopy(a, j)
                        cp.start()
                        fwd.append(cp)
                for a in theirs:
                    for j in range(3):
                        fcopy(a, j).wait_recv()
                for cp in out + fwd:
                    cp.wait_send()

        for a in range(n_r):
            r_out[a][...] = (r_out[a][...] + land[a][1]) + (land[a][0] + land[a][2])
        for a in range(n_g):
            for k in range(1, N_DEV):
                pltpu.make_async_remote_copy(
                    src_ref=g_in[a], dst_ref=slot(a, _flip(me, k)), send_sem=g_send.at[a, k - 1],
                    recv_sem=g_recv.at[a, k - 1], device_id=_flip(me, k), device_id_type=MESH).wait_recv()
        for cp in gathers + swaps:
            cp.wait_send()

    shapes = [tuple(a.shape) for a in red]
    outs, carried = _call(
        body, "reduce_small", (1,), in_specs=[_whole()] * (n_g + n_r), out_specs=[_whole()] * (n_g + n_r),
        out_shape=[_sds((N_DEV * SUBLANES, a.shape[1]), F32) for a in gath] + [_sds(s, F32) for s in shapes],
        scratch=[pltpu.VMEM(s, F32) for s in shapes] + [pltpu.VMEM((3,) + s, F32) for s in shapes]
        + [pltpu.SemaphoreType.DMA((n_g, N_DEV - 1)), pltpu.SemaphoreType.DMA((n_g, N_DEV - 1)),
           pltpu.SemaphoreType.DMA((n_r,)), pltpu.SemaphoreType.DMA((n_r,)),
           pltpu.SemaphoreType.DMA((n_r, 3)), pltpu.SemaphoreType.DMA((n_r, 3)),
           pltpu.SemaphoreType.DMA((n_r, 3)), pltpu.SemaphoreType.DMA((n_r, 3))],
        args=tuple(gath) + tuple(red), carry=carry)
    return (outs[:n_g], outs[n_g:]), carried


STACKED = "stacked"


def _region(ref, shard_shape, col_sharded, pos):
    r, cdim = shard_shape
    d = _dev_index(pos)
    if col_sharded == STACKED:
        return ref.at[d]
    if col_sharded:
        return ref.at[:, pl.ds(pl.multiple_of(d * cdim, LANES), cdim)]
    return ref.at[pl.ds(pl.multiple_of(d * r, 2 * SUBLANES), r), :]


def _gather_carry(shards, col_sharded):
    n_w = len(shards)
    shapes = [tuple(s.shape) for s in shards]
    full_shapes = [(N_DEV,) + s if cs == STACKED else (s[0], s[1] * N_DEV) if cs else (s[0] * N_DEV, s[1])
                   for s, cs in zip(shapes, col_sharded)]

    def tools(out_refs, scr):
        send_sems, recv_sems = scr[n_w], scr[n_w + 1]
        me = _my_pos()
        x, y, c = me
        sibling = (x, y, 1 - c)
        chips = [(1 - x, y), (x, 1 - y), (1 - x, 1 - y)]

        def region(w, pos):
            return _region(out_refs[w], shapes[w], col_sharded[w], pos)

        def copy(w, k, block, to, src=None):
            return pltpu.make_async_remote_copy(
                src_ref=region(w, block) if src is None else src, dst_ref=region(w, block),
                send_sem=send_sems.at[w, k], recv_sem=recv_sems.at[w, k], device_id=to, device_id_type=MESH)

        def first(w):
            return [copy(w, 0, me, sibling, src=scr[w])] + [
                copy(w, 1 + j, me, (*chip, c), src=scr[w]) for j, chip in enumerate(chips)]

        def mine(w):
            return pltpu.make_async_copy(scr[w], region(w, me), scr[n_w + 2].at[w])

        return me, c, sibling, chips, copy, first, mine

    def start(ins, outs, scr):
        _, _, _, _, _, first, mine = tools(outs, scr)
        for w in range(n_w):
            scr[w][...] = ins[w][...].astype(BF16)
            for cp in first(w) + [mine(w)]:
                cp.start()

    def finish(ins, outs, scr):
        me, c, sibling, chips, copy, first, mine = tools(outs, scr)
        passed = []
        for w in range(n_w):
            for j, chip in enumerate(chips):
                copy(w, 1 + j, (*chip, c), me).wait_recv()
                fwd = copy(w, 4 + j, (*chip, c), sibling)
                fwd.start()
                passed.append(fwd)
        for w in range(n_w):
            copy(w, 0, sibling, me).wait_recv()
            for j, chip in enumerate(chips):
                copy(w, 4 + j, (*chip, 1 - c), me).wait_recv()
        for w in range(n_w):
            for cp in first(w):
                cp.wait_send()
            mine(w).wait()
        for cp in passed:
            cp.wait_send()

    return _Carry(
        inputs=list(shards), in_specs=[_whole()] * n_w,
        out_shape=[_sds(s, BF16) for s in full_shapes], out_specs=[_any()] * n_w,
        scratch=[pltpu.VMEM(s, BF16) for s in shapes]
        + [pltpu.SemaphoreType.DMA((n_w, N_DEV - 1)), pltpu.SemaphoreType.DMA((n_w, N_DEV - 1)),
           pltpu.SemaphoreType.DMA((n_w,))],
        start=start, finish=finish)


def _scatter_carry(grads_bf, shard_shapes, col_sharded, relations):
    n_w = len(grads_bf)
    shapes = [tuple(s) for s in shard_shapes]

    def copies(ins, outs, scr):
        send_sems, recv_sems = scr
        me = _my_pos()
        out = []
        for w in range(n_w):
            for i, k in enumerate(relations[w]):
                peer = _flip(me, k)
                out.append(pltpu.make_async_remote_copy(
                    src_ref=_region(ins[w], shapes[w], col_sharded[w], peer), dst_ref=outs[w].at[i],
                    send_sem=send_sems.at[w, i], recv_sem=recv_sems.at[w, i],
                    device_id=peer, device_id_type=MESH))
        return out

    def start(ins, outs, scr):
        for cp in copies(ins, outs, scr):
            cp.start()

    def finish(ins, outs, scr):
        cps = copies(ins, outs, scr)
        for cp in cps:
            cp.wait_recv()
        for cp in cps:
            cp.wait_send()

    return _Carry(
        inputs=list(grads_bf), in_specs=[_any()] * n_w,
        out_shape=[_sds((len(r),) + s, BF16) for r, s in zip(relations, shapes)], out_specs=[_any()] * n_w,
        scratch=[pltpu.SemaphoreType.DMA((n_w, N_DEV - 1)), pltpu.SemaphoreType.DMA((n_w, N_DEV - 1))],
        start=start, finish=finish)


def _block_diag(w):
    eye = jnp.eye(N_HEADS, dtype=w.dtype)
    return (eye[:, None, :, None] * w[:, :, None, :]).reshape(N_HEADS * HEAD_DIM, N_HEADS * HEAD_DIM)


def _pad_rows(a):
    return jnp.pad(a, ((0, SUBLANES - a.shape[0]), (0, 0)))


def _columns_from_devices(gathered, rows):
    w = gathered.shape[1]
    return gathered.reshape(N_DEV, SUBLANES, w)[:, :rows].transpose(1, 0, 2).reshape(rows, N_DEV * w)


def _local_step(x2, target, mod, w_in_f, w_full, conv_w_full, ffn_cw_full,
                g_mix_pre, g_mix_post, conv_b, w_rgate, b_rgate, w_igate, b_igate, lru_a, v_norm_g, v_norm_b,
                w_spatial, b_spatial, g_lru_out, g_gmlp_out, g_ffn_pre, g_ffn_post, ffn_conv_b,
                gather=None, scatter=None):
    sh_m, sc_m, gt_m, sh_f, sc_f, gt_f = [mod[k] for k in range(N_MOD)]
    wr_bd = _block_diag(w_rgate[0]).astype(BF16)
    wi_bd = _block_diag(w_igate[0]).astype(BF16)
    b_r = b_rgate.reshape(1, LRU_W)
    b_i = b_igate.reshape(1, LRU_W)
    b_sp_t = b_spatial[0].T
    w_sp_t = jnp.swapaxes(w_spatial[0], 1, 2)

    def arriving(*names):
        return gather(*names) if gather else None

    near, far = (1, 2, 3, 4, 5), (6, 7)

    def leaving(*parts):
        return scatter(parts) if scatter else None

    def received(recv, parts, outs):
        for (name, _, _), out in zip(parts, outs):
            recv.setdefault(name, []).append(out)

    mix_params = (conv_w_full, conv_b, wr_bd, wi_bd, b_r, b_i, lru_a, v_norm_g, v_norm_b)
    w_out_f = w_full["w_out"]
    (z, h, ycat, hl, y, x1, h2), got = _mix_fwd(
        x2, sh_m, sc_m, g_mix_pre, w_in_f, *mix_params, w_spatial[0], b_sp_t, g_lru_out, g_gmlp_out,
        w_out_f, g_mix_post, gt_m, g_ffn_pre, sc_f, sh_f, carry=arriving("w_up", "w_down"))
    w_up_f, w_down_f = got if gather else (w_full["w_up"], w_full["w_down"])
    (up_pre, up, act, d_y2, dout, loss_acc, vs_ffn), _ = _ffn_fwd(
        h2, x1, w_up_f, ffn_cw_full, ffn_conv_b, w_down_f, gt_f, g_ffn_post, target)

    recv = {}
    gw_down, _ = _wgrad(act, d_y2, D_MODEL // 2, "wgrad_down")
    parts = [("w_down", gw_down[1], near + far)]
    (d_up, cs_ffn), got = _ffn_bwd(d_y2, up_pre, up, ffn_cw_full, w_down_f, carry=leaving(*parts))
    received(recv, parts, got)
    gw_up, _ = _wgrad(h2, d_up, D_FF // 2, "wgrad_up")
    parts = [("w_up", gw_up[1], near)]
    (d_x1, d_y, d_ycat, vs_up), got = _up_bwd(
        d_up, w_up_f, x1, dout, y, w_out_f, g_ffn_pre, sc_f, g_mix_post, gt_m, carry=leaving(*parts))
    received(recv, parts, got)
    gw_out, _ = _wgrad(ycat, d_y, D_MODEL, "wgrad_out")
    parts = [("w_up", gw_up[1], far), ("w_out", gw_out[1], near + far)]
    (d_z, vs_mix, dcw, d_wr, d_wi, d_ws, d_bs), got = _mix_bwd(
        d_ycat, z, hl, *mix_params, w_spatial[0], w_sp_t, b_sp_t, g_lru_out, g_gmlp_out, carry=leaving(*parts))
    received(recv, parts, got)
    gw_in, _ = _wgrad(h, d_z, IN_COLS // 2, "wgrad_in")
    parts = [("w_in", gw_in[1], near)]
    (grad_x, vs_in), got = _in_bwd(d_z, w_in_f, x2, d_x1, g_mix_pre, sc_m, carry=leaving(*parts))
    received(recv, parts, got)
    pending = [("w_in", gw_in[1], far)]

    gath = [vs_in, vs_up, vs_ffn, loss_acc]
    red = [cs_ffn, vs_mix, dcw, d_wr, d_wi, d_ws.reshape(N_GROUPS * POS_BLOCK, POS_BLOCK), d_bs]
    return dict(grad_x=grad_x, gath=gath, red=red, recv=recv, pending=pending,
                w_in=gw_in, w_out=gw_out, w_up=gw_up, w_down=gw_down)


def kernel(x, c, w_ada, b_ada, g_mix_pre, g_mix_post, w_in, conv_w, conv_b, w_rgate, b_rgate, w_igate, b_igate, lru_a, v_norm_g, v_norm_b, w_spatial, b_spatial, g_lru_out, g_gmlp_out, w_out, g_ffn_pre, g_ffn_post, w_up, ffn_conv_w, ffn_conv_b, w_down, loss_target, m_w_ada, m_b_ada, m_g_mix_pre, m_g_mix_post, m_w_in, m_conv_w, m_conv_b, m_w_rgate, m_b_rgate, m_w_igate, m_b_igate, m_lru_a, m_v_norm_g, m_v_norm_b, m_w_spatial, m_b_spatial, m_g_lru_out, m_g_gmlp_out, m_w_out, m_g_ffn_pre, m_g_ffn_post, m_w_up, m_ffn_conv_w, m_ffn_conv_b, m_w_down, v_w_ada, v_b_ada, v_g_mix_pre, v_g_mix_post, v_w_in, v_conv_w, v_conv_b, v_w_rgate, v_b_rgate, v_w_igate, v_b_igate, v_lru_a, v_v_norm_g, v_v_norm_b, v_w_spatial, v_b_spatial, v_g_lru_out, v_g_gmlp_out, v_w_out, v_g_ffn_pre, v_g_ffn_post, v_w_up, v_ffn_conv_w, v_ffn_conv_b, v_w_down):
    me = _dev_index(_my_pos())
    ada_cols = w_ada.shape[-1]

    big_w = dict(w_in=(w_in, m_w_in, v_w_in, True), w_out=(w_out, m_w_out, v_w_out, False),
                 w_up=(w_up, m_w_up, v_w_up, True), w_down=(w_down, m_w_down, v_w_down, False))

    def gather(*names):
        return _gather_carry([big_w[n][0][0] for n in names], [STACKED if n == "w_up" else big_w[n][3] for n in names])

    def scatter(parts):
        return _scatter_carry([g for _, g, _ in parts], [big_w[n][0].shape[1:] for n, _, _ in parts],
                              [big_w[n][3] for n, _, _ in parts], [rel for _, _, rel in parts])

    (c_all, cw_all, fcw_all, mod_all), (w_in_f, w_out_f) = _prologue(
        jnp.broadcast_to(c, (SUBLANES, D_MODEL)), _pad_rows(conv_w[0]), _pad_rows(ffn_conv_w[0]), w_ada[0], b_ada,
        carry=gather("w_in", "w_out"))
    conv_w_full = _columns_from_devices(cw_all, LRU_CONV_K)
    ffn_cw_full = _columns_from_devices(fcw_all, FFN_CONV_K)
    mod = lax.dynamic_index_in_dim(mod_all.reshape(N_DEV, N_DEV, ada_cols), me, axis=1, keepdims=False)
    mod = mod.reshape(N_MOD, 1, D_MODEL)

    loc = _local_step(x[0], loss_target[0], mod, w_in_f, dict(w_out=w_out_f), conv_w_full, ffn_cw_full,
                      g_mix_pre, g_mix_post, conv_b, w_rgate, b_rgate, w_igate, b_igate, lru_a, v_norm_g, v_norm_b,
                      w_spatial, b_spatial, g_lru_out, g_gmlp_out, g_ffn_pre, g_ffn_post, ffn_conv_b,
                      gather=gather, scatter=scatter)
    grad_x = loc["grad_x"]

    (gathered, reduced), got = _reduce_small(loc["gath"], loc["red"], carry=scatter(loc["pending"]))
    for (name, _, _), out in zip(loc["pending"], got):
        loc["recv"][name].append(out)

    results = {}
    for name, (w_, m_, v_, cs) in big_w.items():
        results[name] = _adamw_sum(w_, loc[name][0], loc["recv"][name], m_, v_, cs, "adamw_" + name)

    params = dict(
        b_ada=(b_ada, m_b_ada, v_b_ada), g_mix_pre=(g_mix_pre, m_g_mix_pre, v_g_mix_pre),
        g_mix_post=(g_mix_post, m_g_mix_post, v_g_mix_post), conv_b=(conv_b, m_conv_b, v_conv_b),
        w_rgate=(w_rgate, m_w_rgate, v_w_rgate), b_rgate=(b_rgate, m_b_rgate, v_b_rgate),
        w_igate=(w_igate, m_w_igate, v_w_igate), b_igate=(b_igate, m_b_igate, v_b_igate),
        lru_a=(lru_a, m_lru_a, v_lru_a), v_norm_g=(v_norm_g, m_v_norm_g, v_v_norm_g),
        v_norm_b=(v_norm_b, m_v_norm_b, v_v_norm_b), w_spatial=(w_spatial, m_w_spatial, v_w_spatial),
        b_spatial=(b_spatial, m_b_spatial, v_b_spatial), g_lru_out=(g_lru_out, m_g_lru_out, v_g_lru_out),
        g_gmlp_out=(g_gmlp_out, m_g_gmlp_out, v_g_gmlp_out), g_ffn_pre=(g_ffn_pre, m_g_ffn_pre, v_g_ffn_pre),
        g_ffn_post=(g_ffn_post, m_g_ffn_post, v_g_ffn_post), ffn_conv_b=(ffn_conv_b, m_ffn_conv_b, v_ffn_conv_b))
    conv_params = dict(conv_w=(conv_w, m_conv_w, v_conv_w), ffn_conv_w=(ffn_conv_w, m_ffn_conv_w, v_ffn_conv_w))
    small_results, loss = _adamw_small(gathered, reduced, params, conv_params)
    results.update(small_results)
    loss = loss.reshape(())

    results["w_ada"] = _adamw_wada(c_all, gathered[0], gathered[1], gathered[2], w_ada, m_w_ada, v_w_ada)

    order = ["w_ada", "b_ada", "g_mix_pre", "g_mix_post", "w_in", "conv_w", "conv_b", "w_rgate", "b_rgate", "w_igate",
             "b_igate", "lru_a", "v_norm_g", "v_norm_b", "w_spatial", "b_spatial", "g_lru_out", "g_gmlp_out", "w_out",
             "g_ffn_pre", "g_ffn_post", "w_up", "ffn_conv_w", "ffn_conv_b", "w_down"]
    outs = [loss, grad_x[None]]
    for kind in range(4):
        outs += [results[n][kind] for n in order]
    return tuple(outs)
```

```python
import functools

import jax
import jax.numpy as jnp
from jax import lax
from jax.experimental import pallas as pl
from jax.experimental.pallas import tpu as pltpu

F32 = jnp.float32
BF16 = jnp.bfloat16

D_MODEL = 1024
LRU_W = 512
GMLP_W = 512
N_HEADS = 8
HEAD_DIM = 64
N_GROUPS = 4
POS_BLOCK = 128
CHUNK = 64
IN_COLS = 2048
D_FF = 3072
N_MOD = 6
N_DEV = 8
EPS = 1e-6
LRU_C = 8.0
LRU_CONV_K = 4
FFN_CONV_K = 3

ADAM_LR = 0.001
ADAM_B1 = 0.9
ADAM_B2 = 0.999
ADAM_EPS = 1e-08
ADAM_WD = 0.01
ADAM_STEP = 10

LANES = 128
SUBLANES = 8
TT_BIG = 512
TT_MIX = 256
TT_WG = 1024
FF_CW = 512
VMEM_LIMIT = 56 * 1024 * 1024

MESH = pl.DeviceIdType.MESH


def _sds(shape, dtype):
    return jax.ShapeDtypeStruct(shape, dtype)


def _cparams(sem=None):
    return pltpu.CompilerParams(dimension_semantics=sem, vmem_limit_bytes=VMEM_LIMIT)


def _whole():
    return pl.BlockSpec(memory_space=pltpu.VMEM)


def _const(shape):
    nd = len(shape)
    return pl.BlockSpec(shape, lambda *_: (0,) * nd)


def _any():
    return pl.BlockSpec(memory_space=pl.ANY)


class _Carry:
    def __init__(self, inputs, in_specs, out_shape, out_specs, scratch, start, finish):
        self.inputs, self.in_specs, self.out_shape, self.out_specs = inputs, in_specs, out_shape, out_specs
        self.scratch, self.start, self.finish = scratch, start, finish


def _call(body, name, grid, in_specs, out_specs, out_shape, scratch, args, carry=None):
    n_in, n_out, n_scr = len(in_specs), len(out_specs), len(scratch)
    c_in = len(carry.in_specs) if carry else 0
    c_out = len(carry.out_specs) if carry else 0

    def full_body(*refs):
        ins = refs[:n_in]
        c_ins = refs[n_in:n_in + c_in]
        outs = refs[n_in + c_in:n_in + c_in + n_out]
        c_outs = refs[n_in + c_in + n_out:n_in + c_in + n_out + c_out]
        scr = refs[n_in + c_in + n_out + c_out:n_in + c_in + n_out + c_out + n_scr]
        c_scr = refs[n_in + c_in + n_out + c_out + n_scr:]
        if carry:
            first = functools.reduce(lambda a, b: a & b, [pl.program_id(d) == 0 for d in range(len(grid))])
            last = functools.reduce(lambda a, b: a & b, [pl.program_id(d) == g - 1 for d, g in enumerate(grid)])

            @pl.when(first)
            def _():
                carry.start(c_ins, c_outs, c_scr)

        body(*ins, *outs, *scr)
        if carry:
            @pl.when(last)
            def _():
                carry.finish(c_ins, c_outs, c_scr)

    res = pl.pallas_call(
        full_body, name=name, grid=grid,
        in_specs=list(in_specs) + (list(carry.in_specs) if carry else []),
        out_specs=list(out_specs) + (list(carry.out_specs) if carry else []),
        out_shape=list(out_shape) + (list(carry.out_shape) if carry else []),
        scratch_shapes=list(scratch) + (list(carry.scratch) if carry else []),
        compiler_params=_cparams(("arbitrary",) * len(grid)),
    )(*args, *(carry.inputs if carry else []))
    return res[:n_out], res[n_out:]


def _gelu(x):
    u = 0.7978845608028654 * (x + 0.044715 * x * x * x)
    return 0.5 * x * (1.0 + jnp.tanh(u))


def _gelu_and_grad(x):
    x2 = x * x
    u = 0.7978845608028654 * (x + 0.044715 * x * x2)
    t = jnp.tanh(u)
    g = 0.5 * x * (1.0 + t)
    dg = 0.5 * (1.0 + t) + 0.5 * x * (1.0 - t * t) * 0.7978845608028654 * (1.0 + 3.0 * 0.044715 * x2)
    return g, dg


def _sigmoid(x):
    return 1.0 / (1.0 + jnp.exp(-x))


def _softplus(x):
    return jnp.maximum(x, 0.0) + jnp.log1p(jnp.exp(-jnp.abs(x)))


def _neg_expm1(x):
    series = -x * (1.0 + x * (0.5 + x * (1.0 / 6.0 + x * (1.0 / 24.0 + x * (1.0 / 120.0)))))
    return jnp.where(x > -0.1, series, 1.0 - jnp.exp(x))


def _dot(a, b):
    return jnp.dot(a.astype(BF16), b.astype(BF16), preferred_element_type=F32)


def _dot_nt(a, b):
    return lax.dot_general(a.astype(BF16), b.astype(BF16), (((1,), (1,)), ((), ())), preferred_element_type=F32)


def _dot_tn(a, b):
    return lax.dot_general(a.astype(BF16), b.astype(BF16), (((0,), (0,)), ((), ())), preferred_element_type=F32)


def _rows(shape):
    return lax.broadcasted_iota(jnp.int32, shape, 0)


def _shift_down(cur, prev8, s):
    if s == 0:
        return cur
    n = cur.shape[0]
    r = pltpu.roll(cur, s, 0)
    p = pltpu.roll(prev8, s, 0)
    top = jnp.where(_rows(p.shape) < s, p, r[0:SUBLANES])
    if n == SUBLANES:
        return top
    return jnp.concatenate([top, r[SUBLANES:]], axis=0)


def _shift_up(cur, next8, s):
    if s == 0:
        return cur
    n = cur.shape[0]
    r = pltpu.roll(cur, n - s, 0)
    q = pltpu.roll(next8, SUBLANES - s, 0)
    bot = jnp.where(_rows(q.shape) >= SUBLANES - s, q, r[n - SUBLANES:])
    if n == SUBLANES:
        return bot
    return jnp.concatenate([r[:n - SUBLANES], bot], axis=0)


def _scan_fwd(a, b):
    n = a.shape[0]
    rows = _rows(a.shape)
    s = 1
    while s < n:
        a_s = pltpu.roll(a, s, 0)
        b_s = pltpu.roll(b, s, 0)
        m = rows >= s
        b = jnp.where(m, a * b_s + b, b)
        a = jnp.where(m, a * a_s, a)
        s *= 2
    return a, b


def _scan_rev(a, b):
    n = a.shape[0]
    rows = _rows(a.shape)
    s = 1
    while s < n:
        a_s = pltpu.roll(a, n - s, 0)
        b_s = pltpu.roll(b, n - s, 0)
        m = rows < n - s
        b = jnp.where(m, b + a * b_s, b)
        a = jnp.where(m, a * a_s, a)
        s *= 2
    return a, b


def _rms(x):
    r = lax.rsqrt(jnp.mean(x * x, axis=-1, keepdims=True) + EPS)
    return x * r, r


def _rms_bwd(d_n, n, r):
    return r * (d_n - n * jnp.mean(d_n * n, axis=-1, keepdims=True))


def _colsum(x):
    return jnp.sum(x, axis=0, keepdims=True)


def _lru_gates(xc, wr_ref, wi_ref, br, bi, sp_a):
    r = _sigmoid(_dot(xc, wr_ref[...]) + br)
    i = _sigmoid(_dot(xc, wi_ref[...]) + bi)
    la = -LRU_C * r * sp_a
    a = jnp.exp(la)
    mult = jnp.sqrt(_neg_expm1(2.0 * la))
    return r, i, a, mult


def _lru_conv(lx, prev8, cw_ref, cb):
    xc = cb + cw_ref[LRU_CONV_K - 1:LRU_CONV_K, :] * lx
    taps = []
    for k in range(LRU_CONV_K - 1):
        tap = _shift_down(lx, prev8, LRU_CONV_K - 1 - k)
        taps.append(tap)
        xc = xc + cw_ref[k:k + 1, :] * tap
    return xc, taps


def _ws_mask(transposed=False):
    i = lax.broadcasted_iota(jnp.int32, (POS_BLOCK, POS_BLOCK), 0)
    j = lax.broadcasted_iota(jnp.int32, (POS_BLOCK, POS_BLOCK), 1)
    if transposed:
        i, j = j, i
    return (j // CHUNK) <= (i // CHUNK)


def _gmlp_v(gv, vg, vb):
    av, dav = _gelu_and_grad(gv)
    mu = jnp.mean(av, axis=-1, keepdims=True)
    cen = av - mu
    rs = lax.rsqrt(jnp.mean(cen * cen, axis=-1, keepdims=True) + EPS)
    vhat = cen * rs
    return vhat * vg + vb, vhat, rs, dav


def _mix_fwd(x, sh, sc, g_pre, w_in, conv_w, conv_b, wr_bd, wi_bd, b_r, b_i, lru_a, vn_g, vn_b, w_sp, b_sp_t,
             g_lru, g_gmlp, w_out, g_post, gt_m, g_ffn_pre, sc_f, sh_f, carry=None):
    s_len = x.shape[0]
    tt = min(TT_MIX, s_len)
    nblk = tt // POS_BLOCK

    def body(x_ref, sh_ref, sc_ref, g_ref, w_ref, cw_ref, cb_ref, wr_ref, wi_ref, br_ref, bi_ref, la_ref, vg_ref,
             vb_ref, ws_ref, bst_ref, gl_ref, gg_ref, wo_ref, gp_ref, gtm_ref, g2_ref, scf_ref, shf_ref,
             z_ref, h_ref, y_ref, hl_ref, yo_ref, x1_ref, h2_ref, prev8, hcar):
        i = pl.program_id(0)

        @pl.when(i == 0)
        def _():
            prev8[...] = jnp.zeros_like(prev8)
            hcar[...] = jnp.zeros_like(hcar)

        n_x, _ = _rms(x_ref[...])
        h = (n_x * g_ref[...] * (1.0 + sc_ref[...]) + sh_ref[...]).astype(BF16)
        h_ref[...] = h
        z_ref[...] = jnp.dot(h, w_ref[...], preferred_element_type=F32)

        lx = z_ref[:, 0:LRU_W]
        gate = z_ref[:, LRU_W:2 * LRU_W]
        gu = z_ref[:, 2 * LRU_W:2 * LRU_W + GMLP_W]
        gv = z_ref[:, 2 * LRU_W + GMLP_W:]

        xc, _ = _lru_conv(lx, prev8[...], cw_ref, cb_ref[...])
        prev8[...] = lx[tt - SUBLANES:]
        sp_a = _softplus(-la_ref[...])
        _, ig, a, mult = _lru_gates(xc, wr_ref, wi_ref, br_ref[...], bi_ref[...], sp_a)
        bx = mult * (ig * xc)
        a_cum, b_cum = _scan_fwd(a, bx)
        hl = a_cum * hcar[0:1, :] + b_cum
        hcar[...] = jnp.broadcast_to(hl[tt - 1:tt, :], hcar.shape)
        hl_ref[...] = hl
        y_lru = hl * _gelu(gate)
        n_l, _ = _rms(y_lru)
        y_ref[:, 0:LRU_W] = (n_l * gl_ref[...]).astype(BF16)

        u = _gelu(gu)
        v, _, _, _ = _gmlp_v(gv, vg_ref[...], vb_ref[...])
        mask = _ws_mask()
        sp_parts = []
        for nb in range(nblk):
            row = []
            for g in range(N_GROUPS):
                wsm = jnp.where(mask, ws_ref[g], 0.0)
                vblk = v[nb * POS_BLOCK:(nb + 1) * POS_BLOCK, g * LANES:(g + 1) * LANES]
                row.append(_dot(wsm, vblk) + bst_ref[:, g:g + 1])
            sp_parts.append(jnp.concatenate(row, axis=1))
        sp = jnp.concatenate(sp_parts, axis=0) if nblk > 1 else sp_parts[0]
        n_g, _ = _rms(u * sp)
        y_ref[:, LRU_W:] = (n_g * gg_ref[...]).astype(BF16)

        y = jnp.dot(y_ref[...], wo_ref[...], preferred_element_type=F32)
        yo_ref[...] = y
        n_y, _ = _rms(y)
        x1 = x_ref[...] + gtm_ref[...] * (n_y * gp_ref[...])
        x1_ref[...] = x1
        n1, _ = _rms(x1)
        h2_ref[...] = (n1 * g2_ref[...] * (1.0 + scf_ref[...]) + shf_ref[...]).astype(BF16)

    row = lambda c: pl.BlockSpec((tt, c), lambda i: (i, 0))
    v512 = _const((1, LRU_W))
    vec = _const((1, D_MODEL))
    return _call(
        body, "mix_fwd", (s_len // tt,),
        in_specs=[row(D_MODEL), vec, vec, vec, _whole(),
                  _const((LRU_CONV_K, LRU_W)), v512, _whole(), _whole(), v512, v512, v512, v512, v512,
                  _whole(), _whole(), v512, v512, _whole(), vec, vec, vec, vec, vec],
        out_specs=[row(IN_COLS), row(D_MODEL), row(LRU_W + GMLP_W), row(LRU_W), row(D_MODEL), row(D_MODEL),
                   row(D_MODEL)],
        out_shape=[_sds((s_len, IN_COLS), F32), _sds((s_len, D_MODEL), BF16),
                   _sds((s_len, LRU_W + GMLP_W), BF16), _sds((s_len, LRU_W), F32),
                   _sds((s_len, D_MODEL), F32), _sds((s_len, D_MODEL), F32), _sds((s_len, D_MODEL), BF16)],
        scratch=[pltpu.VMEM((SUBLANES, LRU_W), F32), pltpu.VMEM((SUBLANES, LRU_W), F32)],
        args=(x, sh, sc, g_pre, w_in, conv_w, conv_b, wr_bd, wi_bd, b_r, b_i, lru_a, vn_g, vn_b, w_sp, b_sp_t,
              g_lru, g_gmlp, w_out, g_post, gt_m, g_ffn_pre, sc_f, sh_f), carry=carry)


FF_CHUNKS = N_DEV // 2
FF_CHUNK_W = D_FF // FF_CHUNKS


def _ffn_fwd(h2, w_up3, ffn_cw, ffn_cb, carry=None):
    s_len = h2.shape[0]
    tt = min(TT_MIX, s_len)
    nc, cw = FF_CHUNKS, FF_CHUNK_W

    def body(h2_ref, wu_ref, cwg_ref, cwv_ref, cbg_ref, cbv_ref, up_ref, upc_ref, act_ref, prev):
        i = pl.program_id(0)
        c = pl.program_id(1)

        @pl.when(i == 0)
        def _():
            prev[c] = jnp.zeros((2, SUBLANES, cw), F32)

        h2 = h2_ref[...]
        ug_pre = jnp.dot(h2, wu_ref[c], preferred_element_type=F32)
        uv_pre = jnp.dot(h2, wu_ref[nc + c], preferred_element_type=F32)
        up_ref[0] = ug_pre.astype(BF16)
        up_ref[1] = uv_pre.astype(BF16)
        ug, _ = _ffn_conv(ug_pre, prev[c, 0], cwg_ref, cbg_ref[...])
        uv, _ = _ffn_conv(uv_pre, prev[c, 1], cwv_ref, cbv_ref[...])
        prev[c, 0] = ug_pre[tt - SUBLANES:, :]
        prev[c, 1] = uv_pre[tt - SUBLANES:, :]
        upc_ref[0] = ug
        upc_ref[1] = uv
        act_ref[...] = (_gelu(ug) * uv).astype(BF16)

    chunk2 = pl.BlockSpec((2, tt, cw), lambda i, c: (0, i, c))
    ffn_cb2 = ffn_cb.reshape(1, 2 * D_FF)
    return _call(
        body, "ffn_fwd", (s_len // tt, nc),
        in_specs=[pl.BlockSpec((tt, D_MODEL), lambda i, c: (i, 0)), _whole(),
                  pl.BlockSpec((FFN_CONV_K, cw), lambda i, c: (0, c)),
                  pl.BlockSpec((FFN_CONV_K, cw), lambda i, c: (0, c + nc)),
                  pl.BlockSpec((1, cw), lambda i, c: (0, c)),
                  pl.BlockSpec((1, cw), lambda i, c: (0, c + nc))],
        out_specs=[chunk2, chunk2, pl.BlockSpec((tt, cw), lambda i, c: (i, c))],
        out_shape=[_sds((2, s_len, D_FF), BF16), _sds((2, s_len, D_FF), F32), _sds((s_len, D_FF), BF16)],
        scratch=[pltpu.VMEM((nc, 2, SUBLANES, cw), F32)],
        args=(h2, w_up3, ffn_cw, ffn_cw, ffn_cb2, ffn_cb2), carry=carry)


def _ffn_tail(act, w_down, x1, gt_f, g_post, target):
    s_len = x1.shape[0]
    tt = min(TT_BIG, s_len)

    def body(act_ref, wd_ref, x1_ref, gtf_ref, gp_ref, tg_ref, dy2_ref, dout_ref, loss_ref, vs_ref):
        @pl.when(pl.program_id(0) == 0)
        def _():
            loss_ref[...] = jnp.zeros_like(loss_ref)
            vs_ref[...] = jnp.zeros_like(vs_ref)

        n2, r2 = _rms(jnp.dot(act_ref[...], wd_ref[...], preferred_element_type=F32))
        out = x1_ref[...] + gtf_ref[...] * (n2 * gp_ref[...])
        err = out - tg_ref[...]
        do = err * (1.0 / D_MODEL)
        dout_ref[...] = do
        loss_ref[...] += jnp.broadcast_to(0.5 * jnp.sum(err * err, keepdims=True) * (1.0 / D_MODEL), loss_ref.shape)
        vs_ref[0:1, :] += _colsum(do * n2 * gp_ref[...])
        vs_ref[1:2, :] += _colsum(do * gtf_ref[...] * n2)
        dy2_ref[...] = _rms_bwd(do * gtf_ref[...] * gp_ref[...], n2, r2).astype(BF16)

    row = lambda c: pl.BlockSpec((tt, c), lambda i: (i, 0))
    vec = _const((1, D_MODEL))
    outs, _ = _call(
        body, "ffn_tail", (s_len // tt,),
        in_specs=[row(D_FF), _whole(), row(D_MODEL), vec, vec, row(D_MODEL)],
        out_specs=[row(D_MODEL), row(D_MODEL), _const((SUBLANES, LANES)), _const((SUBLANES, D_MODEL))],
        out_shape=[_sds((s_len, D_MODEL), BF16), _sds((s_len, D_MODEL), F32), _sds((SUBLANES, LANES), F32),
                   _sds((SUBLANES, D_MODEL), F32)],
        scratch=[], args=(act, w_down, x1, gt_f, g_post, target))
    return outs


def _ffn_conv(up_pre, prev8, cw_ref, cb):
    up = cb + cw_ref[FFN_CONV_K - 1:FFN_CONV_K, :] * up_pre
    taps = []
    for k in range(FFN_CONV_K - 1):
        tap = _shift_down(up_pre, prev8, FFN_CONV_K - 1 - k)
        taps.append(tap)
        up = up + cw_ref[k:k + 1, :] * tap
    return up, taps


def _ffn_bwd(d_y2, up_pre, up, ffn_cw, w_down, carry=None):
    s_len = d_y2.shape[0]
    tt = min(TT_BIG, s_len)
    nt = s_len // tt
    cw = FF_CW
    nc = D_FF // cw

    def body(dy2_ref, up_ref, upc_ref, cwg_ref, cwv_ref, wd_ref, dup_ref, cs_ref, nxt, cs_acc):
        i = pl.program_id(0)
        c = pl.program_id(1)

        @pl.when(i == 0)
        def _():
            nxt[c] = jnp.zeros((2, SUBLANES, cw), F32)
            cs_acc[c] = jnp.zeros((2, SUBLANES, cw), F32)

        d_act = _dot_nt(dy2_ref[...], wd_ref[...])
        uv = upc_ref[1]
        gl, dgl = _gelu_and_grad(upc_ref[0])
        d_ug = d_act * uv * dgl
        d_uv = d_act * gl
        for half, (d_u, cw_ref) in enumerate(((d_ug, cwg_ref), (d_uv, cwv_ref))):
            nx = nxt[c, half]
            x_in = up_ref[half].astype(F32)
            d_pre = cw_ref[FFN_CONV_K - 1:FFN_CONV_K, :] * d_u
            sums = [None] * (FFN_CONV_K + 1)
            sums[FFN_CONV_K - 1] = _colsum(d_u * x_in)
            for k in range(FFN_CONV_K - 1):
                ahead = _shift_up(d_u, nx, FFN_CONV_K - 1 - k)
                d_pre = d_pre + cw_ref[k:k + 1, :] * ahead
                sums[k] = _colsum(ahead * x_in)
            sums[FFN_CONV_K] = _colsum(d_u)
            pad = jnp.zeros((SUBLANES - FFN_CONV_K - 1, cw), F32)
            cs_acc[c, half] += jnp.concatenate(sums + [pad], axis=0)
            nxt[c, half] = d_u[0:SUBLANES]
            dup_ref[half] = d_pre.astype(BF16)

        for cc in range(nc):
            @pl.when((i == nt - 1) & (c == cc))
            def _():
                cs_ref[:, cc * cw:(cc + 1) * cw] = cs_acc[cc, 0]
                cs_ref[:, D_FF + cc * cw:D_FF + (cc + 1) * cw] = cs_acc[cc, 1]

    row = pl.BlockSpec((tt, D_MODEL), lambda i, c: (nt - 1 - i, 0))
    blk = pl.BlockSpec((2, tt, cw), lambda i, c: (0, nt - 1 - i, c))
    return _call(
        body, "ffn_bwd", (nt, nc),
        in_specs=[row, blk, blk,
                  pl.BlockSpec((FFN_CONV_K, cw), lambda i, c: (0, c)),
                  pl.BlockSpec((FFN_CONV_K, cw), lambda i, c: (0, c + nc)),
                  pl.BlockSpec((cw, D_MODEL), lambda i, c: (c, 0))],
        out_specs=[blk, _const((SUBLANES, 2 * D_FF))],
        out_shape=[_sds((2, s_len, D_FF), BF16), _sds((SUBLANES, 2 * D_FF), F32)],
        scratch=[pltpu.VMEM((nc, 2, SUBLANES, cw), F32), pltpu.VMEM((nc, 2, SUBLANES, cw), F32)],
        args=(d_y2, up_pre, up, ffn_cw, ffn_cw, w_down), carry=carry)


def _up_bwd(d_up, w_up3, x1, dout, y, w_out, g_pre, sc_f, g_post, gt_m, carry=None):
    s_len = x1.shape[0]
    tt = min(TT_BIG, s_len)

    def body(du_ref, wu_ref, x1_ref, do_ref, y_ref, wo_ref, g2_ref, sc_ref, gp_ref, gt_ref,
             dx1_ref, dy_ref, dyc_ref, vs_ref):
        @pl.when(pl.program_id(0) == 0)
        def _():
            vs_ref[...] = jnp.zeros_like(vs_ref)

        d_h2 = jnp.zeros((tt, D_MODEL), F32)
        for half in range(2):
            for ch in range(FF_CHUNKS):
                d_h2 = d_h2 + _dot_nt(du_ref[half, :, ch * FF_CHUNK_W:(ch + 1) * FF_CHUNK_W],
                                      wu_ref[half * FF_CHUNKS + ch])
        n1, r1 = _rms(x1_ref[...])
        ng = n1 * g2_ref[...]
        vs_ref[0:1, :] += _colsum(d_h2)
        vs_ref[1:2, :] += _colsum(d_h2 * ng)
        d_ng = d_h2 * (1.0 + sc_ref[...])
        vs_ref[2:3, :] += _colsum(d_ng * n1)
        d_x1 = do_ref[...] + _rms_bwd(d_ng * g2_ref[...], n1, r1)
        dx1_ref[...] = d_x1
        n_y, r_y = _rms(y_ref[...])
        vs_ref[3:4, :] += _colsum(d_x1 * n_y * gp_ref[...])
        d_on = d_x1 * gt_ref[...]
        vs_ref[4:5, :] += _colsum(d_on * n_y)
        d_y = _rms_bwd(d_on * gp_ref[...], n_y, r_y).astype(BF16)
        dy_ref[...] = d_y
        dyc_ref[...] = _dot_nt(d_y, wo_ref[...])

    row = lambda c: pl.BlockSpec((tt, c), lambda i: (i, 0))
    vec = _const((1, D_MODEL))
    return _call(
        body, "up_bwd", (s_len // tt,),
        in_specs=[pl.BlockSpec((2, tt, D_FF), lambda i: (0, i, 0)), _whole(), row(D_MODEL), row(D_MODEL), row(D_MODEL),
                  _whole(), vec, vec, vec, vec],
        out_specs=[row(D_MODEL), row(D_MODEL), row(LRU_W + GMLP_W), _const((SUBLANES, D_MODEL))],
        out_shape=[_sds((s_len, D_MODEL), F32), _sds((s_len, D_MODEL), BF16), _sds((s_len, LRU_W + GMLP_W), F32),
                   _sds((SUBLANES, D_MODEL), F32)],
        scratch=[], args=(d_up, w_up3, x1, dout, y, w_out, g_pre, sc_f, g_post, gt_m), carry=carry)


def _head_pair_block(hd):
    return (slice((hd // 2) * HEAD_DIM, (hd // 2 + 1) * HEAD_DIM), slice((hd % 2) * HEAD_DIM, (hd % 2 + 1) * HEAD_DIM))


def _mix_bwd(d_ycat, z, hl, conv_w, conv_b, wr_bd, wi_bd, b_r, b_i, lru_a, vn_g, vn_b, w_sp, w_sp_t, b_sp_t,
             g_lru, g_gmlp, carry=None):
    s_len = z.shape[0]
    tt = min(TT_MIX, s_len)
    nt = s_len // tt
    nblk = tt // POS_BLOCK
    hb = tt // SUBLANES

    def body(dyc_ref, z_ref, zh_ref, hl_ref, hh_ref, cw_ref, cb_ref, wr_ref, wi_ref, br_ref, bi_ref, la_ref,
             vg_ref, vb_ref, ws_ref, wst_ref, bst_ref, gl_ref, gg_ref,
             dz_ref, vs_ref, dcw_ref, dwrb_ref, dwib_ref, dws_ref, dbs_ref, nxt_dxc, nxt_a, nxt_lam, dwr_ref, dwi_ref):
        i = pl.program_id(0)
        first_tile = i == nt - 1

        @pl.when(i == 0)
        def _():
            for ref in (vs_ref, dcw_ref, dwr_ref, dwi_ref, dws_ref, dbs_ref, nxt_dxc, nxt_a, nxt_lam):
                ref[...] = jnp.zeros_like(ref)

        lx = z_ref[:, 0:LRU_W]
        gate = z_ref[:, LRU_W:2 * LRU_W]
        gu = z_ref[:, 2 * LRU_W:2 * LRU_W + GMLP_W]
        gv = z_ref[:, 2 * LRU_W + GMLP_W:]
        prev8 = jnp.where(first_tile, 0.0, zh_ref[...])
        hprev8 = jnp.where(first_tile, 0.0, hh_ref[...])

        xc, taps = _lru_conv(lx, prev8, cw_ref, cb_ref[...])
        a_par = la_ref[...]
        sp_a = _softplus(-a_par)
        r, ig, a, mult = _lru_gates(xc, wr_ref, wi_ref, br_ref[...], bi_ref[...], sp_a)
        hl = hl_ref[...]
        h_prev = _shift_down(hl, hprev8, 1)
        ggate, dggate = _gelu_and_grad(gate)
        y_lru = hl * ggate
        n_l, r_l = _rms(y_lru)
        d_nl = dyc_ref[:, 0:LRU_W]
        vs_ref[6:7, :] += _colsum(d_nl * n_l)
        d_yl = _rms_bwd(d_nl * gl_ref[...], n_l, r_l)
        d_hl = d_yl * ggate
        d_gate = d_yl * hl * dggate
        a_up = _shift_up(a, nxt_a[...], 1)
        a_cum, b_cum = _scan_rev(a_up, d_hl)
        lam = b_cum + a_cum * nxt_lam[0:1, :]
        nxt_a[...] = jnp.broadcast_to(a[0:1, :], nxt_a.shape)
        nxt_lam[...] = jnp.broadcast_to(lam[0:1, :], nxt_lam.shape)
        ixc = ig * xc
        d_la = lam * h_prev * a - lam * ixc * (a * a) / mult
        d_i = lam * mult * xc
        d_xc = lam * mult * ig
        vs_ref[3:4, :] += _colsum(d_la * r) * (LRU_C * _sigmoid(-a_par))
        d_pr = d_la * (-LRU_C * sp_a) * r * (1.0 - r)
        d_pi = d_i * ig * (1.0 - ig)
        vs_ref[1:2, :] += _colsum(d_pr)
        vs_ref[2:3, :] += _colsum(d_pi)
        dwr_ref[...] += _dot_tn(xc, d_pr)
        dwi_ref[...] += _dot_tn(xc, d_pi)
        d_xc = d_xc + _dot_nt(d_pr, wr_ref[...]) + _dot_nt(d_pi, wi_ref[...])
        vs_ref[0:1, :] += _colsum(d_xc)
        nx = nxt_dxc[...]
        d_lx = cw_ref[LRU_CONV_K - 1:LRU_CONV_K, :] * d_xc
        dcw_ref[LRU_CONV_K - 1:LRU_CONV_K, :] += _colsum(d_xc * lx)
        for k in range(LRU_CONV_K - 1):
            d_lx = d_lx + cw_ref[k:k + 1, :] * _shift_up(d_xc, nx, LRU_CONV_K - 1 - k)
            dcw_ref[k:k + 1, :] += _colsum(d_xc * taps[k])
        nxt_dxc[...] = d_xc[0:SUBLANES]
        dz_ref[:, 0:LRU_W] = d_lx.astype(BF16)
        dz_ref[:, LRU_W:2 * LRU_W] = d_gate.astype(BF16)

        u, du = _gelu_and_grad(gu)
        v, vhat, rs, dav = _gmlp_v(gv, vg_ref[...], vb_ref[...])
        mask = _ws_mask()
        sp_parts = []
        for nb in range(nblk):
            rowp = []
            for g in range(N_GROUPS):
                wsm = jnp.where(mask, ws_ref[g], 0.0)
                vblk = v[nb * POS_BLOCK:(nb + 1) * POS_BLOCK, g * LANES:(g + 1) * LANES]
                rowp.append(_dot(wsm, vblk) + bst_ref[:, g:g + 1])
            sp_parts.append(jnp.concatenate(rowp, axis=1))
        sp = jnp.concatenate(sp_parts, axis=0) if nblk > 1 else sp_parts[0]
        y_g = u * sp
        n_g, r_g = _rms(y_g)
        d_ng = dyc_ref[:, LRU_W:]
        vs_ref[7:8, :] += _colsum(d_ng * n_g)
        d_yg = _rms_bwd(d_ng * gg_ref[...], n_g, r_g)
        d_gu = d_yg * sp * du
        d_sp = d_yg * u
        mask_t = _ws_mask(transposed=True)
        ones8 = jnp.ones((SUBLANES, LANES), F32)
        dv_parts = []
        for nb in range(nblk):
            rowp = []
            for g in range(N_GROUPS):
                rs_, cs_ = slice(nb * POS_BLOCK, (nb + 1) * POS_BLOCK), slice(g * LANES, (g + 1) * LANES)
                dsp_blk = d_sp[rs_, cs_]
                dbs_ref[g:g + 1, :] += lax.dot_general(
                    ones8, dsp_blk, (((1,), (1,)), ((), ())), preferred_element_type=F32,
                    precision=lax.Precision.HIGHEST)[0:1, :]
                dws_ref[g] += _dot_nt(dsp_blk, v[rs_, cs_])
                wsm_t = jnp.where(mask_t, wst_ref[g], 0.0)
                rowp.append(_dot(wsm_t, dsp_blk))
            dv_parts.append(jnp.concatenate(rowp, axis=1))
        d_v = jnp.concatenate(dv_parts, axis=0) if nblk > 1 else dv_parts[0]
        vs_ref[4:5, :] += _colsum(d_v * vhat)
        vs_ref[5:6, :] += _colsum(d_v)
        d_vh = d_v * vg_ref[...]
        d_av = rs * (d_vh - jnp.mean(d_vh, axis=-1, keepdims=True)
                     - vhat * jnp.mean(d_vh * vhat, axis=-1, keepdims=True))
        dz_ref[:, 2 * LRU_W:2 * LRU_W + GMLP_W] = d_gu.astype(BF16)
        dz_ref[:, 2 * LRU_W + GMLP_W:] = (d_av * dav).astype(BF16)

        @pl.when(i == nt - 1)
        def _():
            for hd in range(N_HEADS):
                blk = slice(hd * HEAD_DIM, (hd + 1) * HEAD_DIM)
                dwrb_ref[_head_pair_block(hd)] = dwr_ref[blk, blk]
                dwib_ref[_head_pair_block(hd)] = dwi_ref[blk, blk]
            for g in range(N_GROUPS):
                dws_ref[g] = jnp.where(mask, dws_ref[g], 0.0)

    rev = lambda c: pl.BlockSpec((tt, c), lambda i: (nt - 1 - i, 0))
    halo = pl.BlockSpec((SUBLANES, LRU_W), lambda i: (jnp.maximum((nt - 1 - i) * hb - 1, 0), 0))
    v512 = _const((1, LRU_W))
    return _call(
        body, "mix_bwd", (nt,),
        in_specs=[rev(LRU_W + GMLP_W), rev(IN_COLS), halo, rev(LRU_W), halo,
                  _const((LRU_CONV_K, LRU_W)), v512, _whole(), _whole(), v512, v512, v512, v512, v512,
                  _whole(), _whole(), _whole(), v512, v512],
        out_specs=[rev(IN_COLS), _const((SUBLANES, LRU_W)), _const((SUBLANES, LRU_W)),
                   _const((LRU_W // 2, 2 * HEAD_DIM)), _const((LRU_W // 2, 2 * HEAD_DIM)),
                   _const((N_GROUPS, POS_BLOCK, POS_BLOCK)), _const((SUBLANES, POS_BLOCK))],
        out_shape=[_sds((s_len, IN_COLS), BF16), _sds((SUBLANES, LRU_W), F32), _sds((SUBLANES, LRU_W), F32),
                   _sds((LRU_W // 2, 2 * HEAD_DIM), F32), _sds((LRU_W // 2, 2 * HEAD_DIM), F32),
                   _sds((N_GROUPS, POS_BLOCK, POS_BLOCK), F32), _sds((SUBLANES, POS_BLOCK), F32)],
        scratch=[pltpu.VMEM((SUBLANES, LRU_W), F32), pltpu.VMEM((SUBLANES, LRU_W), F32),
                 pltpu.VMEM((SUBLANES, LRU_W), F32), pltpu.VMEM((LRU_W, LRU_W), F32), pltpu.VMEM((LRU_W, LRU_W), F32)],
        args=(d_ycat, z, z, hl, hl, conv_w, conv_b, wr_bd, wi_bd, b_r, b_i, lru_a, vn_g, vn_b, w_sp, w_sp_t, b_sp_t,
              g_lru, g_gmlp), carry=carry)


def _in_bwd(d_z, w_in, x, d_x1, g, sc, carry=None):
    s_len = x.shape[0]
    tt = min(TT_BIG, s_len)

    def body(dz_ref, w_ref, x_ref, dx1_ref, g_ref, sc_ref, gx_ref, vs_ref):
        @pl.when(pl.program_id(0) == 0)
        def _():
            vs_ref[...] = jnp.zeros_like(vs_ref)

        d_h = _dot_nt(dz_ref[...], w_ref[...])
        n, r = _rms(x_ref[...])
        vs_ref[0:1, :] += _colsum(d_h)
        vs_ref[1:2, :] += _colsum(d_h * n * g_ref[...])
        d_ng = d_h * (1.0 + sc_ref[...])
        vs_ref[2:3, :] += _colsum(d_ng * n)
        gx_ref[...] = dx1_ref[...] + _rms_bwd(d_ng * g_ref[...], n, r)

    row = lambda c: pl.BlockSpec((tt, c), lambda i: (i, 0))
    vec = _const((1, D_MODEL))
    return _call(
        body, "in_bwd", (s_len // tt,),
        in_specs=[row(IN_COLS), _whole(), row(D_MODEL), row(D_MODEL), vec, vec],
        out_specs=[row(D_MODEL), _const((SUBLANES, D_MODEL))],
        out_shape=[_sds((s_len, D_MODEL), F32), _sds((SUBLANES, D_MODEL), F32)],
        scratch=[], args=(d_z, w_in, x, d_x1, g, sc), carry=carry)


def _wgrad(a, b, tn, name, carry=None):
    s_len, k_dim = a.shape
    halves = b.ndim == 3
    n_dim = b.shape[-1] * (2 if halves else 1)
    ts = min(TT_WG, s_len)
    nj = n_dim // tn
    nt = s_len // ts

    def body(a_ref, b_ref, o_ref, ob_ref):
        t = pl.program_id(1)
        part = _dot_tn(a_ref[...], b_ref[0] if halves else b_ref[...])

        @pl.when(t == 0)
        def _():
            o_ref[...] = part

        @pl.when(t > 0)
        def _():
            o_ref[...] += part

        @pl.when(t == nt - 1)
        def _():
            ob_ref[...] = o_ref[...].astype(BF16)

    if halves:
        per_half = nj // 2
        b_spec = pl.BlockSpec((1, ts, tn), lambda j, t: (j // per_half, t, j % per_half))
    else:
        b_spec = pl.BlockSpec((ts, tn), lambda j, t: (t, j))
    o_spec = pl.BlockSpec((k_dim, tn), lambda j, t: (0, j))
    return _call(
        body, name, (nj, nt),
        in_specs=[pl.BlockSpec((ts, k_dim), lambda j, t: (t, 0)), b_spec],
        out_specs=[o_spec, o_spec],
        out_shape=[_sds((k_dim, n_dim), F32), _sds((k_dim, n_dim), BF16)],
        scratch=[], args=(a, b), carry=carry)


def _adam_math(w, g, m, v):
    m = ADAM_B1 * m + (1.0 - ADAM_B1) * g
    v = ADAM_B2 * v + (1.0 - ADAM_B2) * (g * g)
    m_hat = m / (1.0 - ADAM_B1 ** ADAM_STEP)
    v_hat = v / (1.0 - ADAM_B2 ** ADAM_STEP)
    delta = -ADAM_LR * (m_hat / (jnp.sqrt(v_hat) + ADAM_EPS) + ADAM_WD * w)
    return delta, m, v


def _row_tile(rows, cols, n_f32_arrays):
    budget = VMEM_LIMIT // 2
    tr = rows
    while tr % 2 == 0 and tr // 2 >= SUBLANES and (tr // 2) % SUBLANES == 0 and tr * cols * 4 * n_f32_arrays * 2 > budget:
        tr //= 2
    return tr


def _adamw_sum(w, g_full, recv, m, v, col_sharded, name):
    _, rows, cols = w.shape
    n_recv = len(recv)
    tr = _row_tile(rows, cols, 10)
    nb = rows // tr

    def body(me_ref, w_ref, g_ref, *rest):
        r_refs = rest[:n_recv]
        m_ref, v_ref, go_ref, d_ref, mo_ref, vo_ref = rest[n_recv:]
        g = g_ref[...]
        for r_ref in r_refs:
            for k in range(r_ref.shape[0]):
                g = g + r_ref[k].astype(F32)
        go_ref[0] = g
        d_ref[0], mo_ref[0], vo_ref[0] = _adam_math(w_ref[0], g, m_ref[0], v_ref[0])

    if col_sharded:
        own = pl.BlockSpec((tr, cols), lambda i, me: (i, me[0]))
    else:
        own = pl.BlockSpec((tr, cols), lambda i, me: (me[0] * nb + i, 0))
    blk = pl.BlockSpec((1, tr, cols), lambda i, me: (0, i, 0))
    return pl.pallas_call(
        body, name=name,
        grid_spec=pltpu.PrefetchScalarGridSpec(
            num_scalar_prefetch=1, grid=(nb,),
            in_specs=[blk, own] + [pl.BlockSpec((r.shape[0], tr, cols), lambda i, me: (0, i, 0)) for r in recv]
            + [blk, blk],
            out_specs=[blk] * 4),
        out_shape=[_sds((1, rows, cols), F32)] * 4,
        compiler_params=_cparams(("arbitrary",)),
    )(jnp.reshape(_dev_index(_my_pos()), (1,)).astype(jnp.int32), w, g_full, *recv, m, v)


def _row_of_each(ref, row):
    cols = ref.shape[1]
    rows = _rows((N_DEV, cols))
    out = jnp.zeros((N_DEV, cols), F32)
    for d in range(N_DEV):
        picked = ref[d * SUBLANES + row:d * SUBLANES + row + 1, :]
        out = jnp.where(rows == d, jnp.broadcast_to(picked, (N_DEV, cols)), out)
    return out


def _my_columns(full, width, me):
    out = jnp.zeros(full.shape[:-1] + (width,), F32)
    for d in range(N_DEV):
        out = out + jnp.where(me == d, full[:, d * width:(d + 1) * width], 0.0)
    return out


def _adamw_wada(c_all, vs_in_all, vs_up_all, vs_ffn_all, w, m, v):
    _, rows, cols = w.shape

    def body(c_ref, vi_ref, vu_ref, vf_ref, w_ref, m_ref, v_ref, go_ref, d_ref, mo_ref, vo_ref):
        me = _dev_index(_my_pos())
        cv = _row_of_each(c_ref, 0)
        ca = cv * _sigmoid(cv)
        dmod = jnp.concatenate([_row_of_each(vi_ref, 0), _row_of_each(vi_ref, 1), _row_of_each(vu_ref, 3),
                                _row_of_each(vu_ref, 0), _row_of_each(vu_ref, 1), _row_of_each(vf_ref, 0)], axis=1)
        dm = _my_columns(dmod, cols, me)
        g = lax.dot_general(ca, dm, (((0,), (0,)), ((), ())), preferred_element_type=F32,
                            precision=lax.Precision.HIGHEST)
        go_ref[0] = g
        d_ref[0], mo_ref[0], vo_ref[0] = _adam_math(w_ref[0], g, m_ref[0], v_ref[0])

    return pl.pallas_call(
        body, name="adamw_w_ada", out_shape=[_sds((1, rows, cols), F32)] * 4,
        in_specs=[_whole()] * 7, out_specs=[_whole()] * 4,
        compiler_params=_cparams(),
    )(c_all, vs_in_all, vs_up_all, vs_ffn_all, w, m, v)


def _adamw_small(gathered, reduced, params, conv_params):
    names = list(params) + list(conv_params)
    allp = {**params, **conv_params}
    n_g = len(gathered) + len(reduced)

    def body(*refs):
        g_refs = refs[:n_g]
        p_refs = refs[n_g:n_g + 3 * len(names)]
        o_refs = refs[n_g + 3 * len(names):]
        me = _dev_index(_my_pos())

        def total(ref):
            s = ref[0:SUBLANES, :]
            for d in range(1, N_DEV):
                s = s + ref[d * SUBLANES:(d + 1) * SUBLANES, :]
            return s

        vs_in, vs_up, vs_ffn, loss = [total(r) for r in g_refs[:4]]
        cs, vs_mix, dcw, dwr, dwi, dws, dbs = [r[...] for r in g_refs[4:]]
        o_refs[-1][...] = loss[0:1, 0:1]
        mine = lambda full, width: _my_columns(full, width, me)

        all_ = (slice(None), slice(None))
        heads = lambda row: [((0, slice(h, h + 1), slice(None)), row[:, h * HEAD_DIM:(h + 1) * HEAD_DIM])
                             for h in range(N_HEADS)]
        blocks = lambda pairs: [((0, h), pairs[_head_pair_block(h)]) for h in range(N_HEADS)]
        pieces = {
            "b_ada": [((slice(None), slice(k * D_MODEL, (k + 1) * D_MODEL)), row) for k, row in enumerate(
                (vs_in[0:1], vs_in[1:2], vs_up[3:4], vs_up[0:1], vs_up[1:2], vs_ffn[0:1]))],
            "g_mix_pre": [(all_, vs_in[2:3])], "g_mix_post": [(all_, vs_up[4:5])],
            "g_ffn_pre": [(all_, vs_up[2:3])], "g_ffn_post": [(all_, vs_ffn[1:2])],
            "conv_b": [(all_, vs_mix[0:1])], "b_rgate": heads(vs_mix[1:2]), "b_igate": heads(vs_mix[2:3]),
            "lru_a": [(all_, vs_mix[3:4])], "v_norm_g": [(all_, vs_mix[4:5])], "v_norm_b": [(all_, vs_mix[5:6])],
            "g_lru_out": [(all_, vs_mix[6:7])], "g_gmlp_out": [(all_, vs_mix[7:8])],
            "w_rgate": blocks(dwr), "w_igate": blocks(dwi),
            "w_spatial": [((0, g), dws[g * POS_BLOCK:(g + 1) * POS_BLOCK, :]) for g in range(N_GROUPS)],
            "b_spatial": [((0,), dbs[0:N_GROUPS])],
            "ffn_conv_b": [(all_, cs[FFN_CONV_K:FFN_CONV_K + 1])],
            "conv_w": [((0,), mine(dcw[0:LRU_CONV_K], LRU_W // N_DEV))],
            "ffn_conv_w": [((0,), mine(cs[0:FFN_CONV_K], 2 * D_FF // N_DEV))],
        }
        for n_i, name in enumerate(names):
            w_ref, m_ref, v_ref = p_refs[3 * n_i:3 * n_i + 3]
            go_ref, d_ref, mo_ref, vo_ref = o_refs[4 * n_i:4 * n_i + 4]
            for idx, g in pieces[name]:
                go_ref[idx] = g
                d_ref[idx], mo_ref[idx], vo_ref[idx] = _adam_math(w_ref[idx], g, m_ref[idx], v_ref[idx])

    flat_params = [a for n in names for a in allp[n]]
    out_shape = [_sds(allp[n][0].shape, F32) for n in names for _ in range(4)] + [_sds((1, 1), F32)]
    outs = pl.pallas_call(
        body, name="adamw_small", out_shape=out_shape,
        in_specs=[_whole()] * (n_g + len(flat_params)), out_specs=[_whole()] * len(out_shape),
        compiler_params=_cparams(),
    )(*gathered, *reduced, *flat_params)
    return {n: outs[4 * i:4 * i + 4] for i, n in enumerate(names)}, outs[-1]


def _my_pos():
    return lax.axis_index("x"), lax.axis_index("y"), lax.axis_index("c")


def _flip(pos, k):
    x, y, c = pos
    return (1 - x if k & 4 else x, 1 - y if k & 2 else y, 1 - c if k & 1 else c)


def _dev_index(pos):
    x, y, c = pos
    return 4 * x + 2 * y + c


def _all_gather_small(ins, outs, send_sems, recv_sems):
    n = len(ins)
    me = _my_pos()

    def slot(a, pos):
        rows = ins[a].shape[0]
        return outs[a].at[pl.ds(pl.multiple_of(_dev_index(pos) * rows, SUBLANES), rows), :]

    def copy(a, k, block):
        return pltpu.make_async_remote_copy(
            src_ref=ins[a], dst_ref=slot(a, block), send_sem=send_sems.at[a, k - 1], recv_sem=recv_sems.at[a, k - 1],
            device_id=_flip(me, k), device_id_type=MESH)

    sends = [copy(a, k, me) for a in range(n) for k in range(1, N_DEV)]
    for cp in sends:
        cp.start()
    for a in range(n):
        rows = ins[a].shape[0]
        outs[a][pl.ds(pl.multiple_of(_dev_index(me) * rows, SUBLANES), rows), :] = ins[a][...]
    for a in range(n):
        for k in range(1, N_DEV):
            copy(a, k, _flip(me, k)).wait_recv()
    for cp in sends:
        cp.wait_send()


def _prologue(c8, cw8, fcw8, w_ada, b_ada, carry):
    cols = w_ada.shape[1]

    def body(c_ref, cw_ref, fcw_ref, w_ref, b_ref, call_ref, cwall_ref, fcwall_ref, modall_ref, mod_scr,
             s1, r1, s2, r2):
        _all_gather_small([c_ref, cw_ref, fcw_ref], [call_ref, cwall_ref, fcwall_ref], s1, r1)
        cv = _row_of_each(call_ref, 0)
        ca = cv * _sigmoid(cv)
        b_cols = _my_columns(b_ref[...], cols, _dev_index(_my_pos()))
        mod_scr[...] = jnp.dot(ca, w_ref[...], preferred_element_type=F32, precision=lax.Precision.HIGHEST) + b_cols
        _all_gather_small([mod_scr], [modall_ref], s2, r2)

    sem = lambda n: pltpu.SemaphoreType.DMA((n, N_DEV - 1))
    return _call(
        body, "prologue", (1,), in_specs=[_whole()] * 5, out_specs=[_whole()] * 4,
        out_shape=[_sds((N_DEV * SUBLANES, a.shape[1]), F32) for a in (c8, cw8, fcw8)]
        + [_sds((N_DEV * N_DEV, cols), F32)],
        scratch=[pltpu.VMEM((N_DEV, cols), F32), sem(3), sem(3), sem(1), sem(1)],
        args=(c8, cw8, fcw8, w_ada, b_ada), carry=carry)


def _reduce_small(gath, red, carry=None):
    n_g, n_r = len(gath), len(red)
    chip_flips = (4, 2, 6)

    def body(*refs):
        g_in, r_in = refs[:n_g], refs[n_g:n_g + n_r]
        g_out, r_out = refs[n_g + n_r:2 * n_g + n_r], refs[2 * n_g + n_r:2 * (n_g + n_r)]
        scr = refs[2 * (n_g + n_r):]
        sib, land = scr[:n_r], scr[n_r:2 * n_r]
        g_send, g_recv, s_send, s_recv, i_send, i_recv, f_send, f_recv = scr[2 * n_r:]
        me = _my_pos()
        c = me[2]
        sibling = _flip(me, 1)

        def slot(a, pos):
            return g_out[a].at[pl.ds(pl.multiple_of(_dev_index(pos) * SUBLANES, SUBLANES), SUBLANES), :]

        def gcopy(a, k):
            return pltpu.make_async_remote_copy(
                src_ref=g_in[a], dst_ref=slot(a, me), send_sem=g_send.at[a, k - 1], recv_sem=g_recv.at[a, k - 1],
                device_id=_flip(me, k), device_id_type=MESH)

        def scopy(a):
            return pltpu.make_async_remote_copy(
                src_ref=r_in[a], dst_ref=sib[a], send_sem=s_send.at[a], recv_sem=s_recv.at[a],
                device_id=sibling, device_id_type=MESH)

        def icopy(a, j):
            return pltpu.make_async_remote_copy(
                src_ref=r_out[a], dst_ref=land[a].at[j], send_sem=i_send.at[a, j], recv_sem=i_recv.at[a, j],
                device_id=_flip(me, chip_flips[j]), device_id_type=MESH)

        def fcopy(a, j):
            return pltpu.make_async_remote_copy(
                src_ref=land[a].at[j], dst_ref=land[a].at[j], send_sem=f_send.at[a, j], recv_sem=f_recv.at[a, j],
                device_id=sibling, device_id_type=MESH)

        gathers = [gcopy(a, k) for a in range(n_g) for k in range(1, N_DEV)]
        swaps = [scopy(a) for a in range(n_r)]
        for cp in gathers + swaps:
            cp.start()
        for a in range(n_g):
            g_out[a][pl.ds(pl.multiple_of(_dev_index(me) * SUBLANES, SUBLANES), SUBLANES), :] = g_in[a][...]
        for a in range(n_r):
            swaps[a].wait_recv()
            r_out[a][...] = r_in[a][...] + sib[a][...]

        for core in range(2):
            mine = [a for a in range(n_r) if a % 2 == core]
            theirs = [a for a in range(n_r) if a % 2 != core]

            @pl.when(c == core)
            def _():
                out = [icopy(a, j) for a in mine for j in range(3)]
                for cp in out:
                    cp.start()
                fwd = []
                for a in mine:
                    for j in range(3):
                        icopy(a, j).wait_recv()
                        cp = fcopy(a, j)
                        cp.start()
                        fwd.append(cp)
                for a in theirs:
                    for j in range(3):
                        fcopy(a, j).wait_recv()
                for cp in out + fwd:
                    cp.wait_send()

        for a in range(n_r):
            r_out[a][...] = (r_out[a][...] + land[a][1]) + (land[a][0] + land[a][2])
        for a in range(n_g):
            for k in range(1, N_DEV):
                pltpu.make_async_remote_copy(
                    src_ref=g_in[a], dst_ref=slot(a, _flip(me, k)), send_sem=g_send.at[a, k - 1],
                    recv_sem=g_recv.at[a, k - 1], device_id=_flip(me, k), device_id_type=MESH).wait_recv()
        for cp in gathers + swaps:
            cp.wait_send()

    shapes = [tuple(a.shape) for a in red]
    outs, carried = _call(
        body, "reduce_small", (1,), in_specs=[_whole()] * (n_g + n_r), out_specs=[_whole()] * (n_g + n_r),
        out_shape=[_sds((N_DEV * SUBLANES, a.shape[1]), F32) for a in gath] + [_sds(s, F32) for s in shapes],
        scratch=[pltpu.VMEM(s, F32) for s in shapes] + [pltpu.VMEM((3,) + s, F32) for s in shapes]
        + [pltpu.SemaphoreType.DMA((n_g, N_DEV - 1)), pltpu.SemaphoreType.DMA((n_g, N_DEV - 1)),
           pltpu.SemaphoreType.DMA((n_r,)), pltpu.SemaphoreType.DMA((n_r,)),
           pltpu.SemaphoreType.DMA((n_r, 3)), pltpu.SemaphoreType.DMA((n_r, 3)),
           pltpu.SemaphoreType.DMA((n_r, 3)), pltpu.SemaphoreType.DMA((n_r, 3))],
        args=tuple(gath) + tuple(red), carry=carry)
    return (outs[:n_g], outs[n_g:]), carried


STACKED = "stacked"


def _region(ref, shard_shape, col_sharded, pos):
    r, cdim = shard_shape
    d = _dev_index(pos)
    if col_sharded == STACKED:
        return ref.at[d]
    if col_sharded:
        return ref.at[:, pl.ds(pl.multiple_of(d * cdim, LANES), cdim)]
    return ref.at[pl.ds(pl.multiple_of(d * r, 2 * SUBLANES), r), :]


def _gather_carry(shards, col_sharded):
    n_w = len(shards)
    shapes = [tuple(s.shape) for s in shards]
    full_shapes = [(N_DEV,) + s if cs == STACKED else (s[0], s[1] * N_DEV) if cs else (s[0] * N_DEV, s[1])
                   for s, cs in zip(shapes, col_sharded)]

    def tools(out_refs, scr):
        send_sems, recv_sems = scr[n_w], scr[n_w + 1]
        me = _my_pos()
        x, y, c = me
        sibling = (x, y, 1 - c)
        chips = [(1 - x, y), (x, 1 - y), (1 - x, 1 - y)]

        def region(w, pos):
            return _region(out_refs[w], shapes[w], col_sharded[w], pos)

        def copy(w, k, block, to, src=None):
            return pltpu.make_async_remote_copy(
                src_ref=region(w, block) if src is None else src, dst_ref=region(w, block),
                send_sem=send_sems.at[w, k], recv_sem=recv_sems.at[w, k], device_id=to, device_id_type=MESH)

        def first(w):
            return [copy(w, 0, me, sibling, src=scr[w])] + [
                copy(w, 1 + j, me, (*chip, c), src=scr[w]) for j, chip in enumerate(chips)]

        def mine(w):
            return pltpu.make_async_copy(scr[w], region(w, me), scr[n_w + 2].at[w])

        return me, c, sibling, chips, copy, first, mine

    def start(ins, outs, scr):
        _, _, _, _, _, first, mine = tools(outs, scr)
        for w in range(n_w):
            scr[w][...] = ins[w][...].astype(BF16)
            for cp in first(w) + [mine(w)]:
                cp.start()

    def finish(ins, outs, scr):
        me, c, sibling, chips, copy, first, mine = tools(outs, scr)
        passed = []
        for w in range(n_w):
            for j, chip in enumerate(chips):
                copy(w, 1 + j, (*chip, c), me).wait_recv()
                fwd = copy(w, 4 + j, (*chip, c), sibling)
                fwd.start()
                passed.append(fwd)
        for w in range(n_w):
            copy(w, 0, sibling, me).wait_recv()
            for j, chip in enumerate(chips):
                copy(w, 4 + j, (*chip, 1 - c), me).wait_recv()
        for w in range(n_w):
            for cp in first(w):
                cp.wait_send()
            mine(w).wait()
        for cp in passed:
            cp.wait_send()

    return _Carry(
        inputs=list(shards), in_specs=[_whole()] * n_w,
        out_shape=[_sds(s, BF16) for s in full_shapes], out_specs=[_any()] * n_w,
        scratch=[pltpu.VMEM(s, BF16) for s in shapes]
        + [pltpu.SemaphoreType.DMA((n_w, N_DEV - 1)), pltpu.SemaphoreType.DMA((n_w, N_DEV - 1)),
           pltpu.SemaphoreType.DMA((n_w,))],
        start=start, finish=finish)


def _scatter_carry(grads_bf, shard_shapes, col_sharded, relations):
    n_w = len(grads_bf)
    shapes = [tuple(s) for s in shard_shapes]

    def copies(ins, outs, scr):
        send_sems, recv_sems = scr
        me = _my_pos()
        out = []
        for w in range(n_w):
            for i, k in enumerate(relations[w]):
                peer = _flip(me, k)
                out.append(pltpu.make_async_remote_copy(
                    src_ref=_region(ins[w], shapes[w], col_sharded[w], peer), dst_ref=outs[w].at[i],
                    send_sem=send_sems.at[w, i], recv_sem=recv_sems.at[w, i],
                    device_id=peer, device_id_type=MESH))
        return out

    def start(ins, outs, scr):
        for cp in copies(ins, outs, scr):
            cp.start()

    def finish(ins, outs, scr):
        cps = copies(ins, outs, scr)
        for cp in cps:
            cp.wait_recv()
        for cp in cps:
            cp.wait_send()

    return _Carry(
        inputs=list(grads_bf), in_specs=[_any()] * n_w,
        out_shape=[_sds((len(r),) + s, BF16) for r, s in zip(relations, shapes)], out_specs=[_any()] * n_w,
        scratch=[pltpu.SemaphoreType.DMA((n_w, N_DEV - 1)), pltpu.SemaphoreType.DMA((n_w, N_DEV - 1))],
        start=start, finish=finish)


def _block_diag(w):
    eye = jnp.eye(N_HEADS, dtype=w.dtype)
    return (eye[:, None, :, None] * w[:, :, None, :]).reshape(N_HEADS * HEAD_DIM, N_HEADS * HEAD_DIM)


def _pad_rows(a):
    return jnp.pad(a, ((0, SUBLANES - a.shape[0]), (0, 0)))


def _columns_from_devices(gathered, rows):
    w = gathered.shape[1]
    return gathered.reshape(N_DEV, SUBLANES, w)[:, :rows].transpose(1, 0, 2).reshape(rows, N_DEV * w)


def _local_step(x2, target, mod, w_in_f, w_full, conv_w_full, ffn_cw_full,
                g_mix_pre, g_mix_post, conv_b, w_rgate, b_rgate, w_igate, b_igate, lru_a, v_norm_g, v_norm_b,
                w_spatial, b_spatial, g_lru_out, g_gmlp_out, g_ffn_pre, g_ffn_post, ffn_conv_b,
                gather=None, scatter=None):
    sh_m, sc_m, gt_m, sh_f, sc_f, gt_f = [mod[k] for k in range(N_MOD)]
    wr_bd = _block_diag(w_rgate[0]).astype(BF16)
    wi_bd = _block_diag(w_igate[0]).astype(BF16)
    b_r = b_rgate.reshape(1, LRU_W)
    b_i = b_igate.reshape(1, LRU_W)
    b_sp_t = b_spatial[0].T
    w_sp_t = jnp.swapaxes(w_spatial[0], 1, 2)

    def arriving(*names):
        return gather(*names) if gather else None

    near, far = (1, 2, 3, 4, 5), (6, 7)

    def leaving(*parts):
        return scatter(parts) if scatter else None

    def received(recv, parts, outs):
        for (name, _, _), out in zip(parts, outs):
            recv.setdefault(name, []).append(out)

    mix_params = (conv_w_full, conv_b, wr_bd, wi_bd, b_r, b_i, lru_a, v_norm_g, v_norm_b)
    w_out_f = w_full["w_out"]
    (z, h, ycat, hl, y, x1, h2), got = _mix_fwd(
        x2, sh_m, sc_m, g_mix_pre, w_in_f, *mix_params, w_spatial[0], b_sp_t, g_lru_out, g_gmlp_out,
        w_out_f, g_mix_post, gt_m, g_ffn_pre, sc_f, sh_f, carry=arriving("w_up"))
    w_up_f = got[0] if gather else w_full["w_up"]
    (up_pre, up, act), got = _ffn_fwd(h2, w_up_f, ffn_cw_full, ffn_conv_b, carry=arriving("w_down"))
    w_down_f = got[0] if gather else w_full["w_down"]
    d_y2, dout, loss_acc, vs_ffn = _ffn_tail(act, w_down_f, x1, gt_f, g_ffn_post, target)

    recv = {}
    gw_down, _ = _wgrad(act, d_y2, D_MODEL // 2, "wgrad_down")
    parts = [("w_down", gw_down[1], near + far)]
    (d_up, cs_ffn), got = _ffn_bwd(d_y2, up_pre, up, ffn_cw_full, w_down_f, carry=leaving(*parts))
    received(recv, parts, got)
    gw_up, _ = _wgrad(h2, d_up, D_FF // 2, "wgrad_up")
    parts = [("w_up", gw_up[1], near)]
    (d_x1, d_y, d_ycat, vs_up), got = _up_bwd(
        d_up, w_up_f, x1, dout, y, w_out_f, g_ffn_pre, sc_f, g_mix_post, gt_m, carry=leaving(*parts))
    received(recv, parts, got)
    gw_out, _ = _wgrad(ycat, d_y, D_MODEL, "wgrad_out")
    parts = [("w_up", gw_up[1], far), ("w_out", gw_out[1], near + far)]
    (d_z, vs_mix, dcw, d_wr, d_wi, d_ws, d_bs), got = _mix_bwd(
        d_ycat, z, hl, *mix_params, w_spatial[0], w_sp_t, b_sp_t, g_lru_out, g_gmlp_out, carry=leaving(*parts))
    received(recv, parts, got)
    gw_in, _ = _wgrad(h, d_z, IN_COLS // 2, "wgrad_in")
    parts = [("w_in", gw_in[1], near)]
    (grad_x, vs_in), got = _in_bwd(d_z, w_in_f, x2, d_x1, g_mix_pre, sc_m, carry=leaving(*parts))
    received(recv, parts, got)
    pending = [("w_in", gw_in[1], far)]

    gath = [vs_in, vs_up, vs_ffn, loss_acc]
    red = [cs_ffn, vs_mix, dcw, d_wr, d_wi, d_ws.reshape(N_GROUPS * POS_BLOCK, POS_BLOCK), d_bs]
    return dict(grad_x=grad_x, gath=gath, red=red, recv=recv, pending=pending,
                w_in=gw_in, w_out=gw_out, w_up=gw_up, w_down=gw_down)


def kernel(x, c, w_ada, b_ada, g_mix_pre, g_mix_post, w_in, conv_w, conv_b, w_rgate, b_rgate, w_igate, b_igate, lru_a, v_norm_g, v_norm_b, w_spatial, b_spatial, g_lru_out, g_gmlp_out, w_out, g_ffn_pre, g_ffn_post, w_up, ffn_conv_w, ffn_conv_b, w_down, loss_target, m_w_ada, m_b_ada, m_g_mix_pre, m_g_mix_post, m_w_in, m_conv_w, m_conv_b, m_w_rgate, m_b_rgate, m_w_igate, m_b_igate, m_lru_a, m_v_norm_g, m_v_norm_b, m_w_spatial, m_b_spatial, m_g_lru_out, m_g_gmlp_out, m_w_out, m_g_ffn_pre, m_g_ffn_post, m_w_up, m_ffn_conv_w, m_ffn_conv_b, m_w_down, v_w_ada, v_b_ada, v_g_mix_pre, v_g_mix_post, v_w_in, v_conv_w, v_conv_b, v_w_rgate, v_b_rgate, v_w_igate, v_b_igate, v_lru_a, v_v_norm_g, v_v_norm_b, v_w_spatial, v_b_spatial, v_g_lru_out, v_g_gmlp_out, v_w_out, v_g_ffn_pre, v_g_ffn_post, v_w_up, v_ffn_conv_w, v_ffn_conv_b, v_w_down):
    me = _dev_index(_my_pos())
    ada_cols = w_ada.shape[-1]

    big_w = dict(w_in=(w_in, m_w_in, v_w_in, True), w_out=(w_out, m_w_out, v_w_out, False),
                 w_up=(w_up, m_w_up, v_w_up, True), w_down=(w_down, m_w_down, v_w_down, False))

    def gather(*names):
        return _gather_carry([big_w[n][0][0] for n in names], [STACKED if n == "w_up" else big_w[n][3] for n in names])

    def scatter(parts):
        return _scatter_carry([g for _, g, _ in parts], [big_w[n][0].shape[1:] for n, _, _ in parts],
                              [big_w[n][3] for n, _, _ in parts], [rel for _, _, rel in parts])

    (c_all, cw_all, fcw_all, mod_all), (w_in_f, w_out_f) = _prologue(
        jnp.broadcast_to(c, (SUBLANES, D_MODEL)), _pad_rows(conv_w[0]), _pad_rows(ffn_conv_w[0]), w_ada[0], b_ada,
        carry=gather("w_in", "w_out"))
    conv_w_full = _columns_from_devices(cw_all, LRU_CONV_K)
    ffn_cw_full = _columns_from_devices(fcw_all, FFN_CONV_K)
    mod = lax.dynamic_index_in_dim(mod_all.reshape(N_DEV, N_DEV, ada_cols), me, axis=1, keepdims=False)
    mod = mod.reshape(N_MOD, 1, D_MODEL)

    loc = _local_step(x[0], loss_target[0], mod, w_in_f, dict(w_out=w_out_f), conv_w_full, ffn_cw_full,
                      g_mix_pre, g_mix_post, conv_b, w_rgate, b_rgate, w_igate, b_igate, lru_a, v_norm_g, v_norm_b,
                      w_spatial, b_spatial, g_lru_out, g_gmlp_out, g_ffn_pre, g_ffn_post, ffn_conv_b,
                      gather=gather, scatter=scatter)
    grad_x = loc["grad_x"]

    (gathered, reduced), got = _reduce_small(loc["gath"], loc["red"], carry=scatter(loc["pending"]))
    for (name, _, _), out in zip(loc["pending"], got):
        loc["recv"][name].append(out)

    results = {}
    for name, (w_, m_, v_, cs) in big_w.items():
        results[name] = _adamw_sum(w_, loc[name][0], loc["recv"][name], m_, v_, cs, "adamw_" + name)

    params = dict(
        b_ada=(b_ada, m_b_ada, v_b_ada), g_mix_pre=(g_mix_pre, m_g_mix_pre, v_g_mix_pre),
        g_mix_post=(g_mix_post, m_g_mix_post, v_g_mix_post), conv_b=(conv_b, m_conv_b, v_conv_b),
        w_rgate=(w_rgate, m_w_rgate, v_w_rgate), b_rgate=(b_rgate, m_b_rgate, v_b_rgate),
        w_igate=(w_igate, m_w_igate, v_w_igate), b_igate=(b_igate, m_b_igate, v_b_igate),
        lru_a=(lru_a, m_lru_a, v_lru_a), v_norm_g=(v_norm_g, m_v_norm_g, v_v_norm_g),
        v_norm_b=(v_norm_b, m_v_norm_b, v_v_norm_b), w_spatial=(w_spatial, m_w_spatial, v_w_spatial),
        b_spatial=(b_spatial, m_b_spatial, v_b_spatial), g_lru_out=(g_lru_out, m_g_lru_out, v_g_lru_out),
        g_gmlp_out=(g_gmlp_out, m_g_gmlp_out, v_g_gmlp_out), g_ffn_pre=(g_ffn_pre, m_g_ffn_pre, v_g_ffn_pre),
        g_ffn_post=(g_ffn_post, m_g_ffn_post, v_g_ffn_post), ffn_conv_b=(ffn_conv_b, m_ffn_conv_b, v_ffn_conv_b))
    conv_params = dict(conv_w=(conv_w, m_conv_w, v_conv_w), ffn_conv_w=(ffn_conv_w, m_ffn_conv_w, v_ffn_conv_w))
    small_results, loss = _adamw_small(gathered, reduced, params, conv_params)
    results.update(small_results)
    loss = loss.reshape(())

    results["w_ada"] = _adamw_wada(c_all, gathered[0], gathered[1], gathered[2], w_ada, m_w_ada, v_w_ada)

    order = ["w_ada", "b_ada", "g_mix_pre", "g_mix_post", "w_in", "conv_w", "conv_b", "w_rgate", "b_rgate", "w_igate",
             "b_igate", "lru_a", "v_norm_g", "v_norm_b", "w_spatial", "b_spatial", "g_lru_out", "g_gmlp_out", "w_out",
             "g_ffn_pre", "g_ffn_post", "w_up", "ffn_conv_w", "ffn_conv_b", "w_down"]
    outs = [loss, grad_x[None]]
    for kind in range(4):
        outs += [results[n][kind] for n in order]
    return tuple(outs)
```

```python
import functools

import jax
import jax.numpy as jnp
from jax import lax
from jax.experimental import pallas as pl
from jax.experimental.pallas import tpu as pltpu

F32 = jnp.float32
BF16 = jnp.bfloat16

D_MODEL = 1024
LRU_W = 512
GMLP_W = 512
N_HEADS = 8
HEAD_DIM = 64
N_GROUPS = 4
POS_BLOCK = 128
CHUNK = 64
IN_COLS = 2048
D_FF = 3072
N_MOD = 6
N_DEV = 8
EPS = 1e-6
LRU_C = 8.0
LRU_CONV_K = 4
FFN_CONV_K = 3

ADAM_LR = 0.001
ADAM_B1 = 0.9
ADAM_B2 = 0.999
ADAM_EPS = 1e-08
ADAM_WD = 0.01
ADAM_STEP = 10

LANES = 128
SUBLANES = 8
TT_BIG = 512
TT_MIX = 256
FF_CW = 512
VMEM_LIMIT = 56 * 1024 * 1024

MESH = pl.DeviceIdType.MESH


def _sds(shape, dtype):
    return jax.ShapeDtypeStruct(shape, dtype)


def _cparams(sem=None):
    return pltpu.CompilerParams(dimension_semantics=sem, vmem_limit_bytes=VMEM_LIMIT)


def _whole():
    return pl.BlockSpec(memory_space=pltpu.VMEM)


def _const(shape):
    nd = len(shape)
    return pl.BlockSpec(shape, lambda *_: (0,) * nd)


def _any():
    return pl.BlockSpec(memory_space=pl.ANY)


class _Carry:
    def __init__(self, inputs, in_specs, out_shape, out_specs, scratch, start, finish):
        self.inputs, self.in_specs, self.out_shape, self.out_specs = inputs, in_specs, out_shape, out_specs
        self.scratch, self.start, self.finish = scratch, start, finish


def _call(body, name, grid, in_specs, out_specs, out_shape, scratch, args, carry=None):
    n_in, n_out, n_scr = len(in_specs), len(out_specs), len(scratch)
    c_in = len(carry.in_specs) if carry else 0
    c_out = len(carry.out_specs) if carry else 0

    def full_body(*refs):
        ins = refs[:n_in]
        c_ins = refs[n_in:n_in + c_in]
        outs = refs[n_in + c_in:n_in + c_in + n_out]
        c_outs = refs[n_in + c_in + n_out:n_in + c_in + n_out + c_out]
        scr = refs[n_in + c_in + n_out + c_out:n_in + c_in + n_out + c_out + n_scr]
        c_scr = refs[n_in + c_in + n_out + c_out + n_scr:]
        if carry:
            first = functools.reduce(lambda a, b: a & b, [pl.program_id(d) == 0 for d in range(len(grid))])
            last = functools.reduce(lambda a, b: a & b, [pl.program_id(d) == g - 1 for d, g in enumerate(grid)])

            @pl.when(first)
            def _():
                carry.start(c_ins, c_outs, c_scr)

        body(*ins, *outs, *scr)
        if carry:
            @pl.when(last)
            def _():
                carry.finish(c_ins, c_outs, c_scr)

    res = pl.pallas_call(
        full_body, name=name, grid=grid,
        in_specs=list(in_specs) + (list(carry.in_specs) if carry else []),
        out_specs=list(out_specs) + (list(carry.out_specs) if carry else []),
        out_shape=list(out_shape) + (list(carry.out_shape) if carry else []),
        scratch_shapes=list(scratch) + (list(carry.scratch) if carry else []),
        compiler_params=_cparams(("arbitrary",) * len(grid)),
    )(*args, *(carry.inputs if carry else []))
    return res[:n_out], res[n_out:]


def _gelu(x):
    u = 0.7978845608028654 * (x + 0.044715 * x * x * x)
    return 0.5 * x * (1.0 + jnp.tanh(u))


def _gelu_and_grad(x):
    x2 = x * x
    u = 0.7978845608028654 * (x + 0.044715 * x * x2)
    t = jnp.tanh(u)
    g = 0.5 * x * (1.0 + t)
    dg = 0.5 * (1.0 + t) + 0.5 * x * (1.0 - t * t) * 0.7978845608028654 * (1.0 + 3.0 * 0.044715 * x2)
    return g, dg


def _sigmoid(x):
    return 1.0 / (1.0 + jnp.exp(-x))


def _softplus(x):
    return jnp.maximum(x, 0.0) + jnp.log1p(jnp.exp(-jnp.abs(x)))


def _neg_expm1(x):
    series = -x * (1.0 + x * (0.5 + x * (1.0 / 6.0 + x * (1.0 / 24.0 + x * (1.0 / 120.0)))))
    return jnp.where(x > -0.1, series, 1.0 - jnp.exp(x))


def _dot(a, b):
    return jnp.dot(a.astype(BF16), b.astype(BF16), preferred_element_type=F32)


def _dot_nt(a, b):
    return lax.dot_general(a.astype(BF16), b.astype(BF16), (((1,), (1,)), ((), ())), preferred_element_type=F32)


def _dot_tn(a, b):
    return lax.dot_general(a.astype(BF16), b.astype(BF16), (((0,), (0,)), ((), ())), preferred_element_type=F32)


def _rows(shape):
    return lax.broadcasted_iota(jnp.int32, shape, 0)


def _shift_down(cur, prev8, s):
    if s == 0:
        return cur
    n = cur.shape[0]
    r = pltpu.roll(cur, s, 0)
    p = pltpu.roll(prev8, s, 0)
    top = jnp.where(_rows(p.shape) < s, p, r[0:SUBLANES])
    if n == SUBLANES:
        return top
    return jnp.concatenate([top, r[SUBLANES:]], axis=0)


def _shift_up(cur, next8, s):
    if s == 0:
        return cur
    n = cur.shape[0]
    r = pltpu.roll(cur, n - s, 0)
    q = pltpu.roll(next8, SUBLANES - s, 0)
    bot = jnp.where(_rows(q.shape) >= SUBLANES - s, q, r[n - SUBLANES:])
    if n == SUBLANES:
        return bot
    return jnp.concatenate([r[:n - SUBLANES], bot], axis=0)


def _scan_fwd(a, b):
    n = a.shape[0]
    rows = _rows(a.shape)
    s = 1
    while s < n:
        a_s = pltpu.roll(a, s, 0)
        b_s = pltpu.roll(b, s, 0)
        m = rows >= s
        b = jnp.where(m, a * b_s + b, b)
        a = jnp.where(m, a * a_s, a)
        s *= 2
    return a, b


def _scan_rev(a, b):
    n = a.shape[0]
    rows = _rows(a.shape)
    s = 1
    while s < n:
        a_s = pltpu.roll(a, n - s, 0)
        b_s = pltpu.roll(b, n - s, 0)
        m = rows < n - s
        b = jnp.where(m, b + a * b_s, b)
        a = jnp.where(m, a * a_s, a)
        s *= 2
    return a, b


def _rms(x):
    r = lax.rsqrt(jnp.mean(x * x, axis=-1, keepdims=True) + EPS)
    return x * r, r


def _rms_bwd(d_n, n, r):
    return r * (d_n - n * jnp.mean(d_n * n, axis=-1, keepdims=True))


def _colsum(x):
    return jnp.sum(x, axis=0, keepdims=True)


def _lru_gates(xc, wr_ref, wi_ref, br, bi, sp_a):
    r = _sigmoid(_dot(xc, wr_ref[...]) + br)
    i = _sigmoid(_dot(xc, wi_ref[...]) + bi)
    la = -LRU_C * r * sp_a
    a = jnp.exp(la)
    mult = jnp.sqrt(_neg_expm1(2.0 * la))
    return r, i, a, mult


def _lru_conv(lx, prev8, cw_ref, cb):
    xc = cb + cw_ref[LRU_CONV_K - 1:LRU_CONV_K, :] * lx
    taps = []
    for k in range(LRU_CONV_K - 1):
        tap = _shift_down(lx, prev8, LRU_CONV_K - 1 - k)
        taps.append(tap)
        xc = xc + cw_ref[k:k + 1, :] * tap
    return xc, taps


def _ws_mask(transposed=False):
    i = lax.broadcasted_iota(jnp.int32, (POS_BLOCK, POS_BLOCK), 0)
    j = lax.broadcasted_iota(jnp.int32, (POS_BLOCK, POS_BLOCK), 1)
    if transposed:
        i, j = j, i
    return (j // CHUNK) <= (i // CHUNK)


def _gmlp_v(gv, vg, vb):
    av, dav = _gelu_and_grad(gv)
    mu = jnp.mean(av, axis=-1, keepdims=True)
    cen = av - mu
    rs = lax.rsqrt(jnp.mean(cen * cen, axis=-1, keepdims=True) + EPS)
    vhat = cen * rs
    return vhat * vg + vb, vhat, rs, dav


def _mix_fwd(x, sh, sc, g_pre, w_in, conv_w, conv_b, wr_bd, wi_bd, b_r, b_i, lru_a, vn_g, vn_b, w_sp, b_sp_t,
             g_lru, g_gmlp, w_out, g_post, gt_m, g_ffn_pre, sc_f, sh_f, carry=None):
    s_len = x.shape[0]
    tt = min(TT_MIX, s_len)
    nblk = tt // POS_BLOCK

    def body(x_ref, sh_ref, sc_ref, g_ref, w_ref, cw_ref, cb_ref, wr_ref, wi_ref, br_ref, bi_ref, la_ref, vg_ref,
             vb_ref, ws_ref, bst_ref, gl_ref, gg_ref, wo_ref, gp_ref, gtm_ref, g2_ref, scf_ref, shf_ref,
             z_ref, h_ref, y_ref, hl_ref, yo_ref, x1_ref, h2_ref, prev8, hcar):
        i = pl.program_id(0)

        @pl.when(i == 0)
        def _():
            prev8[...] = jnp.zeros_like(prev8)
            hcar[...] = jnp.zeros_like(hcar)

        n_x, _ = _rms(x_ref[...])
        h = (n_x * g_ref[...] * (1.0 + sc_ref[...]) + sh_ref[...]).astype(BF16)
        h_ref[...] = h
        z_ref[...] = jnp.dot(h, w_ref[...], preferred_element_type=F32)

        lx = z_ref[:, 0:LRU_W]
        gate = z_ref[:, LRU_W:2 * LRU_W]
        gu = z_ref[:, 2 * LRU_W:2 * LRU_W + GMLP_W]
        gv = z_ref[:, 2 * LRU_W + GMLP_W:]

        xc, _ = _lru_conv(lx, prev8[...], cw_ref, cb_ref[...])
        prev8[...] = lx[tt - SUBLANES:]
        sp_a = _softplus(-la_ref[...])
        _, ig, a, mult = _lru_gates(xc, wr_ref, wi_ref, br_ref[...], bi_ref[...], sp_a)
        bx = mult * (ig * xc)
        a_cum, b_cum = _scan_fwd(a, bx)
        hl = a_cum * hcar[0:1, :] + b_cum
        hcar[...] = jnp.broadcast_to(hl[tt - 1:tt, :], hcar.shape)
        hl_ref[...] = hl
        y_lru = hl * _gelu(gate)
        n_l, _ = _rms(y_lru)
        y_ref[:, 0:LRU_W] = (n_l * gl_ref[...]).astype(BF16)

        u = _gelu(gu)
        v, _, _, _ = _gmlp_v(gv, vg_ref[...], vb_ref[...])
        mask = _ws_mask()
        sp_parts = []
        for nb in range(nblk):
            row = []
            for g in range(N_GROUPS):
                wsm = jnp.where(mask, ws_ref[g], 0.0)
                vblk = v[nb * POS_BLOCK:(nb + 1) * POS_BLOCK, g * LANES:(g + 1) * LANES]
                row.append(_dot(wsm, vblk) + bst_ref[:, g:g + 1])
            sp_parts.append(jnp.concatenate(row, axis=1))
        sp = jnp.concatenate(sp_parts, axis=0) if nblk > 1 else sp_parts[0]
        n_g, _ = _rms(u * sp)
        y_ref[:, LRU_W:] = (n_g * gg_ref[...]).astype(BF16)

        y = jnp.dot(y_ref[...], wo_ref[...], preferred_element_type=F32)
        yo_ref[...] = y
        n_y, _ = _rms(y)
        x1 = x_ref[...] + gtm_ref[...] * (n_y * gp_ref[...])
        x1_ref[...] = x1
        n1, _ = _rms(x1)
        h2_ref[...] = (n1 * g2_ref[...] * (1.0 + scf_ref[...]) + shf_ref[...]).astype(BF16)

    row = lambda c: pl.BlockSpec((tt, c), lambda i: (i, 0))
    v512 = _const((1, LRU_W))
    vec = _const((1, D_MODEL))
    return _call(
        body, "mix_fwd", (s_len // tt,),
        in_specs=[row(D_MODEL), vec, vec, vec, _whole(),
                  _const((LRU_CONV_K, LRU_W)), v512, _whole(), _whole(), v512, v512, v512, v512, v512,
                  _whole(), _whole(), v512, v512, _whole(), vec, vec, vec, vec, vec],
        out_specs=[row(IN_COLS), row(D_MODEL), row(LRU_W + GMLP_W), row(LRU_W), row(D_MODEL), row(D_MODEL),
                   row(D_MODEL)],
        out_shape=[_sds((s_len, IN_COLS), F32), _sds((s_len, D_MODEL), BF16),
                   _sds((s_len, LRU_W + GMLP_W), BF16), _sds((s_len, LRU_W), F32),
                   _sds((s_len, D_MODEL), F32), _sds((s_len, D_MODEL), F32), _sds((s_len, D_MODEL), BF16)],
        scratch=[pltpu.VMEM((SUBLANES, LRU_W), F32), pltpu.VMEM((SUBLANES, LRU_W), F32)],
        args=(x, sh, sc, g_pre, w_in, conv_w, conv_b, wr_bd, wi_bd, b_r, b_i, lru_a, vn_g, vn_b, w_sp, b_sp_t,
              g_lru, g_gmlp, w_out, g_post, gt_m, g_ffn_pre, sc_f, sh_f), carry=carry)


FF_CHUNKS = N_DEV // 2
FF_CHUNK_W = D_FF // FF_CHUNKS


def _ffn_fwd(h2, w_up3, ffn_cw, ffn_cb, carry=None):
    s_len = h2.shape[0]
    tt = min(TT_MIX, s_len)
    nc, cw = FF_CHUNKS, FF_CHUNK_W

    def body(h2_ref, wu_ref, cwg_ref, cwv_ref, cbg_ref, cbv_ref, up_ref, upc_ref, act_ref, prev):
        i = pl.program_id(0)
        c = pl.program_id(1)

        @pl.when(i == 0)
        def _():
            prev[c] = jnp.zeros((2, SUBLANES, cw), F32)

        h2 = h2_ref[...]
        ug_pre = jnp.dot(h2, wu_ref[c], preferred_element_type=F32)
        uv_pre = jnp.dot(h2, wu_ref[nc + c], preferred_element_type=F32)
        up_ref[0] = ug_pre.astype(BF16)
        up_ref[1] = uv_pre.astype(BF16)
        ug, _ = _ffn_conv(ug_pre, prev[c, 0], cwg_ref, cbg_ref[...])
        uv, _ = _ffn_conv(uv_pre, prev[c, 1], cwv_ref, cbv_ref[...])
        prev[c, 0] = ug_pre[tt - SUBLANES:, :]
        prev[c, 1] = uv_pre[tt - SUBLANES:, :]
        upc_ref[0] = ug
        upc_ref[1] = uv
        act_ref[...] = (_gelu(ug) * uv).astype(BF16)

    chunk2 = pl.BlockSpec((2, tt, cw), lambda i, c: (0, i, c))
    ffn_cb2 = ffn_cb.reshape(1, 2 * D_FF)
    return _call(
        body, "ffn_fwd", (s_len // tt, nc),
        in_specs=[pl.BlockSpec((tt, D_MODEL), lambda i, c: (i, 0)), _whole(),
                  pl.BlockSpec((FFN_CONV_K, cw), lambda i, c: (0, c)),
                  pl.BlockSpec((FFN_CONV_K, cw), lambda i, c: (0, c + nc)),
                  pl.BlockSpec((1, cw), lambda i, c: (0, c)),
                  pl.BlockSpec((1, cw), lambda i, c: (0, c + nc))],
        out_specs=[chunk2, chunk2, pl.BlockSpec((tt, cw), lambda i, c: (i, c))],
        out_shape=[_sds((2, s_len, D_FF), BF16), _sds((2, s_len, D_FF), F32), _sds((s_len, D_FF), BF16)],
        scratch=[pltpu.VMEM((nc, 2, SUBLANES, cw), F32)],
        args=(h2, w_up3, ffn_cw, ffn_cw, ffn_cb2, ffn_cb2), carry=carry)


def _ffn_tail(act, w_down, x1, gt_f, g_post, target):
    s_len = x1.shape[0]
    tt = min(TT_BIG, s_len)

    def body(act_ref, wd_ref, x1_ref, gtf_ref, gp_ref, tg_ref, dy2_ref, dout_ref, loss_ref, vs_ref):
        @pl.when(pl.program_id(0) == 0)
        def _():
            loss_ref[...] = jnp.zeros_like(loss_ref)
            vs_ref[...] = jnp.zeros_like(vs_ref)

        n2, r2 = _rms(jnp.dot(act_ref[...], wd_ref[...], preferred_element_type=F32))
        out = x1_ref[...] + gtf_ref[...] * (n2 * gp_ref[...])
        err = out - tg_ref[...]
        do = err * (1.0 / D_MODEL)
        dout_ref[...] = do
        loss_ref[...] += jnp.broadcast_to(0.5 * jnp.sum(err * err, keepdims=True) * (1.0 / D_MODEL), loss_ref.shape)
        vs_ref[0:1, :] += _colsum(do * n2 * gp_ref[...])
        vs_ref[1:2, :] += _colsum(do * gtf_ref[...] * n2)
        dy2_ref[...] = _rms_bwd(do * gtf_ref[...] * gp_ref[...], n2, r2).astype(BF16)

    row = lambda c: pl.BlockSpec((tt, c), lambda i: (i, 0))
    vec = _const((1, D_MODEL))
    outs, _ = _call(
        body, "ffn_tail", (s_len // tt,),
        in_specs=[row(D_FF), _whole(), row(D_MODEL), vec, vec, row(D_MODEL)],
        out_specs=[row(D_MODEL), row(D_MODEL), _const((SUBLANES, LANES)), _const((SUBLANES, D_MODEL))],
        out_shape=[_sds((s_len, D_MODEL), BF16), _sds((s_len, D_MODEL), F32), _sds((SUBLANES, LANES), F32),
                   _sds((SUBLANES, D_MODEL), F32)],
        scratch=[], args=(act, w_down, x1, gt_f, g_post, target))
    return outs


def _ffn_conv(up_pre, prev8, cw_ref, cb):
    up = cb + cw_ref[FFN_CONV_K - 1:FFN_CONV_K, :] * up_pre
    taps = []
    for k in range(FFN_CONV_K - 1):
        tap = _shift_down(up_pre, prev8, FFN_CONV_K - 1 - k)
        taps.append(tap)
        up = up + cw_ref[k:k + 1, :] * tap
    return up, taps


def _ffn_bwd(d_y2, up_pre, up, ffn_cw, w_down, carry=None):
    s_len = d_y2.shape[0]
    tt = min(TT_BIG, s_len)
    nt = s_len // tt
    cw = FF_CW
    nc = D_FF // cw

    def body(dy2_ref, up_ref, upc_ref, cwg_ref, cwv_ref, wd_ref, dup_ref, cs_ref, nxt, cs_acc):
        i = pl.program_id(0)
        c = pl.program_id(1)

        @pl.when(i == 0)
        def _():
            nxt[c] = jnp.zeros((2, SUBLANES, cw), F32)
            cs_acc[c] = jnp.zeros((2, SUBLANES, cw), F32)

        pw = cw // 2
        for piece in range(2):
            cols = slice(piece * pw, (piece + 1) * pw)
            d_act = _dot_nt(dy2_ref[...], wd_ref[cols, :])
            uv = upc_ref[1, :, cols]
            gl, dgl = _gelu_and_grad(upc_ref[0, :, cols])
            d_ug = d_act * uv * dgl
            d_uv = d_act * gl
            for half, (d_u, cw_ref) in enumerate(((d_ug, cwg_ref), (d_uv, cwv_ref))):
                nx = nxt[c, half, :, cols]
                x_in = up_ref[half, :, cols].astype(F32)
                d_pre = cw_ref[FFN_CONV_K - 1:FFN_CONV_K, cols] * d_u
                sums = [None] * (FFN_CONV_K + 1)
                sums[FFN_CONV_K - 1] = _colsum(d_u * x_in)
                for k in range(FFN_CONV_K - 1):
                    ahead = _shift_up(d_u, nx, FFN_CONV_K - 1 - k)
                    d_pre = d_pre + cw_ref[k:k + 1, cols] * ahead
                    sums[k] = _colsum(ahead * x_in)
                sums[FFN_CONV_K] = _colsum(d_u)
                pad = jnp.zeros((SUBLANES - FFN_CONV_K - 1, pw), F32)
                cs_acc[c, half, :, cols] += jnp.concatenate(sums + [pad], axis=0)
                nxt[c, half, :, cols] = d_u[0:SUBLANES]
                dup_ref[half, :, cols] = d_pre.astype(BF16)

        for cc in range(nc):
            @pl.when((i == nt - 1) & (c == cc))
            def _():
                cs_ref[:, cc * cw:(cc + 1) * cw] = cs_acc[cc, 0]
                cs_ref[:, D_FF + cc * cw:D_FF + (cc + 1) * cw] = cs_acc[cc, 1]

    row = pl.BlockSpec((tt, D_MODEL), lambda i, c: (nt - 1 - i, 0))
    blk = pl.BlockSpec((2, tt, cw), lambda i, c: (0, nt - 1 - i, c))
    return _call(
        body, "ffn_bwd", (nt, nc),
        in_specs=[row, blk, blk,
                  pl.BlockSpec((FFN_CONV_K, cw), lambda i, c: (0, c)),
                  pl.BlockSpec((FFN_CONV_K, cw), lambda i, c: (0, c + nc)),
                  pl.BlockSpec((cw, D_MODEL), lambda i, c: (c, 0))],
        out_specs=[blk, _const((SUBLANES, 2 * D_FF))],
        out_shape=[_sds((2, s_len, D_FF), BF16), _sds((SUBLANES, 2 * D_FF), F32)],
        scratch=[pltpu.VMEM((nc, 2, SUBLANES, cw), F32), pltpu.VMEM((nc, 2, SUBLANES, cw), F32)],
        args=(d_y2, up_pre, up, ffn_cw, ffn_cw, w_down), carry=carry)


def _up_bwd(d_up, w_up3, x1, dout, y, w_out, g_pre, sc_f, g_post, gt_m, carry=None):
    s_len = x1.shape[0]
    tt = min(TT_BIG, s_len)

    def body(du_ref, wu_ref, x1_ref, do_ref, y_ref, wo_ref, g2_ref, sc_ref, gp_ref, gt_ref,
             dx1_ref, dy_ref, dyc_ref, vs_ref):
        @pl.when(pl.program_id(0) == 0)
        def _():
            vs_ref[...] = jnp.zeros_like(vs_ref)

        d_h2 = jnp.zeros((tt, D_MODEL), F32)
        for half in range(2):
            for ch in range(FF_CHUNKS):
                d_h2 = d_h2 + _dot_nt(du_ref[half, :, ch * FF_CHUNK_W:(ch + 1) * FF_CHUNK_W],
                                      wu_ref[half * FF_CHUNKS + ch])
        n1, r1 = _rms(x1_ref[...])
        ng = n1 * g2_ref[...]
        vs_ref[0:1, :] += _colsum(d_h2)
        vs_ref[1:2, :] += _colsum(d_h2 * ng)
        d_ng = d_h2 * (1.0 + sc_ref[...])
        vs_ref[2:3, :] += _colsum(d_ng * n1)
        d_x1 = do_ref[...] + _rms_bwd(d_ng * g2_ref[...], n1, r1)
        dx1_ref[...] = d_x1
        n_y, r_y = _rms(y_ref[...])
        vs_ref[3:4, :] += _colsum(d_x1 * n_y * gp_ref[...])
        d_on = d_x1 * gt_ref[...]
        vs_ref[4:5, :] += _colsum(d_on * n_y)
        d_y = _rms_bwd(d_on * gp_ref[...], n_y, r_y).astype(BF16)
        dy_ref[...] = d_y
        dyc_ref[...] = _dot_nt(d_y, wo_ref[...])

    row = lambda c: pl.BlockSpec((tt, c), lambda i: (i, 0))
    vec = _const((1, D_MODEL))
    return _call(
        body, "up_bwd", (s_len // tt,),
        in_specs=[pl.BlockSpec((2, tt, D_FF), lambda i: (0, i, 0)), _whole(), row(D_MODEL), row(D_MODEL), row(D_MODEL),
                  _whole(), vec, vec, vec, vec],
        out_specs=[row(D_MODEL), row(D_MODEL), row(LRU_W + GMLP_W), _const((SUBLANES, D_MODEL))],
        out_shape=[_sds((s_len, D_MODEL), F32), _sds((s_len, D_MODEL), BF16), _sds((s_len, LRU_W + GMLP_W), F32),
                   _sds((SUBLANES, D_MODEL), F32)],
        scratch=[], args=(d_up, w_up3, x1, dout, y, w_out, g_pre, sc_f, g_post, gt_m), carry=carry)


def _head_pair_block(hd):
    return (slice((hd // 2) * HEAD_DIM, (hd // 2 + 1) * HEAD_DIM), slice((hd % 2) * HEAD_DIM, (hd % 2 + 1) * HEAD_DIM))


def _mix_bwd(d_ycat, z, hl, conv_w, conv_b, wr_bd, wi_bd, b_r, b_i, lru_a, vn_g, vn_b, w_sp, w_sp_t, b_sp_t,
             g_lru, g_gmlp, carry=None):
    s_len = z.shape[0]
    tt = min(TT_MIX, s_len)
    nt = s_len // tt
    nblk = tt // POS_BLOCK
    hb = tt // SUBLANES

    def body(dyc_ref, z_ref, zh_ref, hl_ref, hh_ref, cw_ref, cb_ref, wr_ref, wi_ref, br_ref, bi_ref, la_ref,
             vg_ref, vb_ref, ws_ref, wst_ref, bst_ref, gl_ref, gg_ref,
             dz_ref, vs_ref, dcw_ref, dwrb_ref, dwib_ref, dws_ref, dbs_ref, nxt_dxc, nxt_a, nxt_lam, dwr_ref, dwi_ref):
        i = pl.program_id(0)
        first_tile = i == nt - 1

        @pl.when(i == 0)
        def _():
            for ref in (vs_ref, dcw_ref, dwr_ref, dwi_ref, dws_ref, dbs_ref, nxt_dxc, nxt_a, nxt_lam):
                ref[...] = jnp.zeros_like(ref)

        lx = z_ref[:, 0:LRU_W]
        gate = z_ref[:, LRU_W:2 * LRU_W]
        gu = z_ref[:, 2 * LRU_W:2 * LRU_W + GMLP_W]
        gv = z_ref[:, 2 * LRU_W + GMLP_W:]
        prev8 = jnp.where(first_tile, 0.0, zh_ref[...])
        hprev8 = jnp.where(first_tile, 0.0, hh_ref[...])

        xc, taps = _lru_conv(lx, prev8, cw_ref, cb_ref[...])
        a_par = la_ref[...]
        sp_a = _softplus(-a_par)
        r, ig, a, mult = _lru_gates(xc, wr_ref, wi_ref, br_ref[...], bi_ref[...], sp_a)
        hl = hl_ref[...]
        h_prev = _shift_down(hl, hprev8, 1)
        ggate, dggate = _gelu_and_grad(gate)
        y_lru = hl * ggate
        n_l, r_l = _rms(y_lru)
        d_nl = dyc_ref[:, 0:LRU_W]
        vs_ref[6:7, :] += _colsum(d_nl * n_l)
        d_yl = _rms_bwd(d_nl * gl_ref[...], n_l, r_l)
        d_hl = d_yl * ggate
        d_gate = d_yl * hl * dggate
        a_up = _shift_up(a, nxt_a[...], 1)
        a_cum, b_cum = _scan_rev(a_up, d_hl)
        lam = b_cum + a_cum * nxt_lam[0:1, :]
        nxt_a[...] = jnp.broadcast_to(a[0:1, :], nxt_a.shape)
        nxt_lam[...] = jnp.broadcast_to(lam[0:1, :], nxt_lam.shape)
        ixc = ig * xc
        d_la = lam * h_prev * a - lam * ixc * (a * a) / mult
        d_i = lam * mult * xc
        d_xc = lam * mult * ig
        vs_ref[3:4, :] += _colsum(d_la * r) * (LRU_C * _sigmoid(-a_par))
        d_pr = d_la * (-LRU_C * sp_a) * r * (1.0 - r)
        d_pi = d_i * ig * (1.0 - ig)
        vs_ref[1:2, :] += _colsum(d_pr)
        vs_ref[2:3, :] += _colsum(d_pi)
        dwr_ref[...] += _dot_tn(xc, d_pr)
        dwi_ref[...] += _dot_tn(xc, d_pi)
        d_xc = d_xc + _dot_nt(d_pr, wr_ref[...]) + _dot_nt(d_pi, wi_ref[...])
        vs_ref[0:1, :] += _colsum(d_xc)
        nx = nxt_dxc[...]
        d_lx = cw_ref[LRU_CONV_K - 1:LRU_CONV_K, :] * d_xc
        dcw_ref[LRU_CONV_K - 1:LRU_CONV_K, :] += _colsum(d_xc * lx)
        for k in range(LRU_CONV_K - 1):
            d_lx = d_lx + cw_ref[k:k + 1, :] * _shift_up(d_xc, nx, LRU_CONV_K - 1 - k)
            dcw_ref[k:k + 1, :] += _colsum(d_xc * taps[k])
        nxt_dxc[...] = d_xc[0:SUBLANES]
        dz_ref[:, 0:LRU_W] = d_lx.astype(BF16)
        dz_ref[:, LRU_W:2 * LRU_W] = d_gate.astype(BF16)

        u, du = _gelu_and_grad(gu)
        v, vhat, rs, dav = _gmlp_v(gv, vg_ref[...], vb_ref[...])
        mask = _ws_mask()
        sp_parts = []
        for nb in range(nblk):
            rowp = []
            for g in range(N_GROUPS):
                wsm = jnp.where(mask, ws_ref[g], 0.0)
                vblk = v[nb * POS_BLOCK:(nb + 1) * POS_BLOCK, g * LANES:(g + 1) * LANES]
                rowp.append(_dot(wsm, vblk) + bst_ref[:, g:g + 1])
            sp_parts.append(jnp.concatenate(rowp, axis=1))
        sp = jnp.concatenate(sp_parts, axis=0) if nblk > 1 else sp_parts[0]
        y_g = u * sp
        n_g, r_g = _rms(y_g)
        d_ng = dyc_ref[:, LRU_W:]
        vs_ref[7:8, :] += _colsum(d_ng * n_g)
        d_yg = _rms_bwd(d_ng * gg_ref[...], n_g, r_g)
        d_gu = d_yg * sp * du
        d_sp = d_yg * u
        mask_t = _ws_mask(transposed=True)
        ones8 = jnp.ones((SUBLANES, LANES), F32)
        dv_parts = []
        for nb in range(nblk):
            rowp = []
            for g in range(N_GROUPS):
                rs_, cs_ = slice(nb * POS_BLOCK, (nb + 1) * POS_BLOCK), slice(g * LANES, (g + 1) * LANES)
                dsp_blk = d_sp[rs_, cs_]
                dbs_ref[g:g + 1, :] += lax.dot_general(
                    ones8, dsp_blk, (((1,), (1,)), ((), ())), preferred_element_type=F32,
                    precision=lax.Precision.HIGHEST)[0:1, :]
                dws_ref[g] += _dot_nt(dsp_blk, v[rs_, cs_])
                wsm_t = jnp.where(mask_t, wst_ref[g], 0.0)
                rowp.append(_dot(wsm_t, dsp_blk))
            dv_parts.append(jnp.concatenate(rowp, axis=1))
        d_v = jnp.concatenate(dv_parts, axis=0) if nblk > 1 else dv_parts[0]
        vs_ref[4:5, :] += _colsum(d_v * vhat)
        vs_ref[5:6, :] += _colsum(d_v)
        d_vh = d_v * vg_ref[...]
        d_av = rs * (d_vh - jnp.mean(d_vh, axis=-1, keepdims=True)
                     - vhat * jnp.mean(d_vh * vhat, axis=-1, keepdims=True))
        dz_ref[:, 2 * LRU_W:2 * LRU_W + GMLP_W] = d_gu.astype(BF16)
        dz_ref[:, 2 * LRU_W + GMLP_W:] = (d_av * dav).astype(BF16)

        @pl.when(i == nt - 1)
        def _():
            for hd in range(N_HEADS):
                blk = slice(hd * HEAD_DIM, (hd + 1) * HEAD_DIM)
                dwrb_ref[_head_pair_block(hd)] = dwr_ref[blk, blk]
                dwib_ref[_head_pair_block(hd)] = dwi_ref[blk, blk]
            for g in range(N_GROUPS):
                dws_ref[g] = jnp.where(mask, dws_ref[g], 0.0)

    rev = lambda c: pl.BlockSpec((tt, c), lambda i: (nt - 1 - i, 0))
    halo = pl.BlockSpec((SUBLANES, LRU_W), lambda i: (jnp.maximum((nt - 1 - i) * hb - 1, 0), 0))
    v512 = _const((1, LRU_W))
    return _call(
        body, "mix_bwd", (nt,),
        in_specs=[rev(LRU_W + GMLP_W), rev(IN_COLS), halo, rev(LRU_W), halo,
                  _const((LRU_CONV_K, LRU_W)), v512, _whole(), _whole(), v512, v512, v512, v512, v512,
                  _whole(), _whole(), _whole(), v512, v512],
        out_specs=[rev(IN_COLS), _const((SUBLANES, LRU_W)), _const((SUBLANES, LRU_W)),
                   _const((LRU_W // 2, 2 * HEAD_DIM)), _const((LRU_W // 2, 2 * HEAD_DIM)),
                   _const((N_GROUPS, POS_BLOCK, POS_BLOCK)), _const((SUBLANES, POS_BLOCK))],
        out_shape=[_sds((s_len, IN_COLS), BF16), _sds((SUBLANES, LRU_W), F32), _sds((SUBLANES, LRU_W), F32),
                   _sds((LRU_W // 2, 2 * HEAD_DIM), F32), _sds((LRU_W // 2, 2 * HEAD_DIM), F32),
                   _sds((N_GROUPS, POS_BLOCK, POS_BLOCK), F32), _sds((SUBLANES, POS_BLOCK), F32)],
        scratch=[pltpu.VMEM((SUBLANES, LRU_W), F32), pltpu.VMEM((SUBLANES, LRU_W), F32),
                 pltpu.VMEM((SUBLANES, LRU_W), F32), pltpu.VMEM((LRU_W, LRU_W), F32), pltpu.VMEM((LRU_W, LRU_W), F32)],
        args=(d_ycat, z, z, hl, hl, conv_w, conv_b, wr_bd, wi_bd, b_r, b_i, lru_a, vn_g, vn_b, w_sp, w_sp_t, b_sp_t,
              g_lru, g_gmlp), carry=carry)


def _in_bwd(d_z, w_in, x, d_x1, g, sc, carry=None):
    s_len = x.shape[0]
    tt = min(TT_BIG, s_len)

    def body(dz_ref, w_ref, x_ref, dx1_ref, g_ref, sc_ref, gx_ref, vs_ref):
        @pl.when(pl.program_id(0) == 0)
        def _():
            vs_ref[...] = jnp.zeros_like(vs_ref)

        d_h = _dot_nt(dz_ref[...], w_ref[...])
        n, r = _rms(x_ref[...])
        vs_ref[0:1, :] += _colsum(d_h)
        vs_ref[1:2, :] += _colsum(d_h * n * g_ref[...])
        d_ng = d_h * (1.0 + sc_ref[...])
        vs_ref[2:3, :] += _colsum(d_ng * n)
        gx_ref[...] = dx1_ref[...] + _rms_bwd(d_ng * g_ref[...], n, r)

    row = lambda c: pl.BlockSpec((tt, c), lambda i: (i, 0))
    vec = _const((1, D_MODEL))
    return _call(
        body, "in_bwd", (s_len // tt,),
        in_specs=[row(IN_COLS), _whole(), row(D_MODEL), row(D_MODEL), vec, vec],
        out_specs=[row(D_MODEL), _const((SUBLANES, D_MODEL))],
        out_shape=[_sds((s_len, D_MODEL), F32), _sds((SUBLANES, D_MODEL), F32)],
        scratch=[], args=(d_z, w_in, x, d_x1, g, sc), carry=carry)


def _wgrad(a, b, tile, name, by_rows=False, carry=None):
    s_len, k_dim = a.shape
    halves = b.ndim == 3
    n_dim = b.shape[-1] * (2 if halves else 1)

    def body(a_ref, b_ref, o_ref, ob_ref):
        out = _dot_tn(a_ref[...], b_ref[0] if halves else b_ref[...])
        o_ref[...] = out
        ob_ref[...] = out.astype(BF16)

    if by_rows:
        steps = k_dim // tile
        a_spec = pl.BlockSpec((s_len, tile), lambda j: (0, j))
        b_spec = pl.BlockSpec((s_len, n_dim), lambda j: (0, 0))
        o_spec = pl.BlockSpec((tile, n_dim), lambda j: (j, 0))
    else:
        steps = n_dim // tile
        a_spec = pl.BlockSpec((s_len, k_dim), lambda j: (0, 0))
        if halves:
            per_half = steps // 2
            b_spec = pl.BlockSpec((1, s_len, tile), lambda j: (j // per_half, 0, j % per_half))
        else:
            b_spec = pl.BlockSpec((s_len, tile), lambda j: (0, j))
        o_spec = pl.BlockSpec((k_dim, tile), lambda j: (0, j))
    return _call(
        body, name, (steps,), in_specs=[a_spec, b_spec], out_specs=[o_spec, o_spec],
        out_shape=[_sds((k_dim, n_dim), F32), _sds((k_dim, n_dim), BF16)],
        scratch=[], args=(a, b), carry=carry)


def _adam_math(w, g, m, v):
    m = ADAM_B1 * m + (1.0 - ADAM_B1) * g
    v = ADAM_B2 * v + (1.0 - ADAM_B2) * (g * g)
    m_hat = m / (1.0 - ADAM_B1 ** ADAM_STEP)
    v_hat = v / (1.0 - ADAM_B2 ** ADAM_STEP)
    delta = -ADAM_LR * (m_hat / (jnp.sqrt(v_hat) + ADAM_EPS) + ADAM_WD * w)
    return delta, m, v


def _row_tile(rows, cols, n_f32_arrays):
    budget = VMEM_LIMIT // 2
    tr = rows
    while tr % 2 == 0 and tr // 2 >= SUBLANES and (tr // 2) % SUBLANES == 0 and tr * cols * 4 * n_f32_arrays * 2 > budget:
        tr //= 2
    return tr


def _adamw_sum(w, g_full, recv, m, v, col_sharded, name):
    _, rows, cols = w.shape
    n_recv = len(recv)
    tr = _row_tile(rows, cols, 10)
    nb = rows // tr

    def body(me_ref, w_ref, g_ref, *rest):
        r_refs = rest[:n_recv]
        m_ref, v_ref, go_ref, d_ref, mo_ref, vo_ref = rest[n_recv:]
        g = g_ref[...]
        for r_ref in r_refs:
            for k in range(r_ref.shape[0]):
                g = g + r_ref[k].astype(F32)
        go_ref[0] = g
        d_ref[0], mo_ref[0], vo_ref[0] = _adam_math(w_ref[0], g, m_ref[0], v_ref[0])

    if col_sharded:
        own = pl.BlockSpec((tr, cols), lambda i, me: (i, me[0]))
    else:
        own = pl.BlockSpec((tr, cols), lambda i, me: (me[0] * nb + i, 0))
    blk = pl.BlockSpec((1, tr, cols), lambda i, me: (0, i, 0))
    return pl.pallas_call(
        body, name=name,
        grid_spec=pltpu.PrefetchScalarGridSpec(
            num_scalar_prefetch=1, grid=(nb,),
            in_specs=[blk, own] + [pl.BlockSpec((r.shape[0], tr, cols), lambda i, me: (0, i, 0)) for r in recv]
            + [blk, blk],
            out_specs=[blk] * 4),
        out_shape=[_sds((1, rows, cols), F32)] * 4,
        compiler_params=_cparams(("arbitrary",)),
    )(jnp.reshape(_dev_index(_my_pos()), (1,)).astype(jnp.int32), w, g_full, *recv, m, v)


def _row_of_each(ref, row):
    cols = ref.shape[1]
    rows = _rows((N_DEV, cols))
    out = jnp.zeros((N_DEV, cols), F32)
    for d in range(N_DEV):
        picked = ref[d * SUBLANES + row:d * SUBLANES + row + 1, :]
        out = jnp.where(rows == d, jnp.broadcast_to(picked, (N_DEV, cols)), out)
    return out


def _my_columns(full, width, me):
    out = jnp.zeros(full.shape[:-1] + (width,), F32)
    for d in range(N_DEV):
        out = out + jnp.where(me == d, full[:, d * width:(d + 1) * width], 0.0)
    return out


def _adamw_wada(c_all, vs_in_all, vs_up_all, vs_ffn_all, w, m, v):
    _, rows, cols = w.shape

    def body(c_ref, vi_ref, vu_ref, vf_ref, w_ref, m_ref, v_ref, go_ref, d_ref, mo_ref, vo_ref):
        me = _dev_index(_my_pos())
        cv = _row_of_each(c_ref, 0)
        ca = cv * _sigmoid(cv)
        dmod = jnp.concatenate([_row_of_each(vi_ref, 0), _row_of_each(vi_ref, 1), _row_of_each(vu_ref, 3),
                                _row_of_each(vu_ref, 0), _row_of_each(vu_ref, 1), _row_of_each(vf_ref, 0)], axis=1)
        dm = _my_columns(dmod, cols, me)
        g = lax.dot_general(ca, dm, (((0,), (0,)), ((), ())), preferred_element_type=F32,
                            precision=lax.Precision.HIGHEST)
        go_ref[0] = g
        d_ref[0], mo_ref[0], vo_ref[0] = _adam_math(w_ref[0], g, m_ref[0], v_ref[0])

    return pl.pallas_call(
        body, name="adamw_w_ada", out_shape=[_sds((1, rows, cols), F32)] * 4,
        in_specs=[_whole()] * 7, out_specs=[_whole()] * 4,
        compiler_params=_cparams(),
    )(c_all, vs_in_all, vs_up_all, vs_ffn_all, w, m, v)


def _adamw_small(gathered, reduced, params, conv_params):
    names = list(params) + list(conv_params)
    allp = {**params, **conv_params}
    n_g = len(gathered) + len(reduced)

    def body(*refs):
        g_refs = refs[:n_g]
        p_refs = refs[n_g:n_g + 3 * len(names)]
        o_refs = refs[n_g + 3 * len(names):]
        me = _dev_index(_my_pos())

        def total(ref):
            s = ref[0:SUBLANES, :]
            for d in range(1, N_DEV):
                s = s + ref[d * SUBLANES:(d + 1) * SUBLANES, :]
            return s

        vs_in, vs_up, vs_ffn, loss = [total(r) for r in g_refs[:4]]
        cs, vs_mix, dcw, dwr, dwi, dws, dbs = [r[...] for r in g_refs[4:]]
        o_refs[-1][...] = loss[0:1, 0:1]
        mine = lambda full, width: _my_columns(full, width, me)

        all_ = (slice(None), slice(None))
        heads = lambda row: [((0, slice(h, h + 1), slice(None)), row[:, h * HEAD_DIM:(h + 1) * HEAD_DIM])
                             for h in range(N_HEADS)]
        blocks = lambda pairs: [((0, h), pairs[_head_pair_block(h)]) for h in range(N_HEADS)]
        pieces = {
            "b_ada": [((slice(None), slice(k * D_MODEL, (k + 1) * D_MODEL)), row) for k, row in enumerate(
                (vs_in[0:1], vs_in[1:2], vs_up[3:4], vs_up[0:1], vs_up[1:2], vs_ffn[0:1]))],
            "g_mix_pre": [(all_, vs_in[2:3])], "g_mix_post": [(all_, vs_up[4:5])],
            "g_ffn_pre": [(all_, vs_up[2:3])], "g_ffn_post": [(all_, vs_ffn[1:2])],
            "conv_b": [(all_, vs_mix[0:1])], "b_rgate": heads(vs_mix[1:2]), "b_igate": heads(vs_mix[2:3]),
            "lru_a": [(all_, vs_mix[3:4])], "v_norm_g": [(all_, vs_mix[4:5])], "v_norm_b": [(all_, vs_mix[5:6])],
            "g_lru_out": [(all_, vs_mix[6:7])], "g_gmlp_out": [(all_, vs_mix[7:8])],
            "w_rgate": blocks(dwr), "w_igate": blocks(dwi),
            "w_spatial": [((0, g), dws[g * POS_BLOCK:(g + 1) * POS_BLOCK, :]) for g in range(N_GROUPS)],
            "b_spatial": [((0,), dbs[0:N_GROUPS])],
            "ffn_conv_b": [(all_, cs[FFN_CONV_K:FFN_CONV_K + 1])],
            "conv_w": [((0,), mine(dcw[0:LRU_CONV_K], LRU_W // N_DEV))],
            "ffn_conv_w": [((0,), mine(cs[0:FFN_CONV_K], 2 * D_FF // N_DEV))],
        }
        for n_i, name in enumerate(names):
            w_ref, m_ref, v_ref = p_refs[3 * n_i:3 * n_i + 3]
            go_ref, d_ref, mo_ref, vo_ref = o_refs[4 * n_i:4 * n_i + 4]
            for idx, g in pieces[name]:
                go_ref[idx] = g
                d_ref[idx], mo_ref[idx], vo_ref[idx] = _adam_math(w_ref[idx], g, m_ref[idx], v_ref[idx])

    flat_params = [a for n in names for a in allp[n]]
    out_shape = [_sds(allp[n][0].shape, F32) for n in names for _ in range(4)] + [_sds((1, 1), F32)]
    outs = pl.pallas_call(
        body, name="adamw_small", out_shape=out_shape,
        in_specs=[_whole()] * (n_g + len(flat_params)), out_specs=[_whole()] * len(out_shape),
        compiler_params=_cparams(),
    )(*gathered, *reduced, *flat_params)
    return {n: outs[4 * i:4 * i + 4] for i, n in enumerate(names)}, outs[-1]


def _my_pos():
    return lax.axis_index("x"), lax.axis_index("y"), lax.axis_index("c")


def _flip(pos, k):
    x, y, c = pos
    return (1 - x if k & 4 else x, 1 - y if k & 2 else y, 1 - c if k & 1 else c)


def _dev_index(pos):
    x, y, c = pos
    return 4 * x + 2 * y + c


def _all_gather_small(ins, outs, send_sems, recv_sems):
    n = len(ins)
    me = _my_pos()

    def slot(a, pos):
        rows = ins[a].shape[0]
        return outs[a].at[pl.ds(pl.multiple_of(_dev_index(pos) * rows, SUBLANES), rows), :]

    def copy(a, k, block):
        return pltpu.make_async_remote_copy(
            src_ref=ins[a], dst_ref=slot(a, block), send_sem=send_sems.at[a, k - 1], recv_sem=recv_sems.at[a, k - 1],
            device_id=_flip(me, k), device_id_type=MESH)

    sends = [copy(a, k, me) for a in range(n) for k in range(1, N_DEV)]
    for cp in sends:
        cp.start()
    for a in range(n):
        rows = ins[a].shape[0]
        outs[a][pl.ds(pl.multiple_of(_dev_index(me) * rows, SUBLANES), rows), :] = ins[a][...]
    for a in range(n):
        for k in range(1, N_DEV):
            copy(a, k, _flip(me, k)).wait_recv()
    for cp in sends:
        cp.wait_send()


def _prologue(c8, cw8, fcw8, w_ada, b_ada, carry):
    cols = w_ada.shape[1]

    def body(c_ref, cw_ref, fcw_ref, w_ref, b_ref, call_ref, cwall_ref, fcwall_ref, modall_ref, mod_scr,
             s1, r1, s2, r2):
        _all_gather_small([c_ref, cw_ref, fcw_ref], [call_ref, cwall_ref, fcwall_ref], s1, r1)
        cv = _row_of_each(call_ref, 0)
        ca = cv * _sigmoid(cv)
        b_cols = _my_columns(b_ref[...], cols, _dev_index(_my_pos()))
        mod_scr[...] = jnp.dot(ca, w_ref[...], preferred_element_type=F32, precision=lax.Precision.HIGHEST) + b_cols
        _all_gather_small([mod_scr], [modall_ref], s2, r2)

    sem = lambda n: pltpu.SemaphoreType.DMA((n, N_DEV - 1))
    return _call(
        body, "prologue", (1,), in_specs=[_whole()] * 5, out_specs=[_whole()] * 4,
        out_shape=[_sds((N_DEV * SUBLANES, a.shape[1]), F32) for a in (c8, cw8, fcw8)]
        + [_sds((N_DEV * N_DEV, cols), F32)],
        scratch=[pltpu.VMEM((N_DEV, cols), F32), sem(3), sem(3), sem(1), sem(1)],
        args=(c8, cw8, fcw8, w_ada, b_ada), carry=carry)


def _reduce_small(gath, red, carry=None):
    n_g, n_r = len(gath), len(red)
    chip_flips = (4, 2, 6)

    def body(*refs):
        g_in, r_in = refs[:n_g], refs[n_g:n_g + n_r]
        g_out, r_out = refs[n_g + n_r:2 * n_g + n_r], refs[2 * n_g + n_r:2 * (n_g + n_r)]
        scr = refs[2 * (n_g + n_r):]
        sib, land = scr[:n_r], scr[n_r:2 * n_r]
        g_send, g_recv, s_send, s_recv, i_send, i_recv, f_send, f_recv = scr[2 * n_r:]
        me = _my_pos()
        c = me[2]
        sibling = _flip(me, 1)

        def slot(a, pos):
            return g_out[a].at[pl.ds(pl.multiple_of(_dev_index(pos) * SUBLANES, SUBLANES), SUBLANES), :]

        def gcopy(a, k):
            return pltpu.make_async_remote_copy(
                src_ref=g_in[a], dst_ref=slot(a, me), send_sem=g_send.at[a, k - 1], recv_sem=g_recv.at[a, k - 1],
                device_id=_flip(me, k), device_id_type=MESH)

        def scopy(a):
            return pltpu.make_async_remote_copy(
                src_ref=r_in[a], dst_ref=sib[a], send_sem=s_send.at[a], recv_sem=s_recv.at[a],
                device_id=sibling, device_id_type=MESH)

        def icopy(a, j):
            return pltpu.make_async_remote_copy(
                src_ref=r_out[a], dst_ref=land[a].at[j], send_sem=i_send.at[a, j], recv_sem=i_recv.at[a, j],
                device_id=_flip(me, chip_flips[j]), device_id_type=MESH)

        def fcopy(a, j):
            return pltpu.make_async_remote_copy(
                src_ref=land[a].at[j], dst_ref=land[a].at[j], send_sem=f_send.at[a, j], recv_sem=f_recv.at[a, j],
                device_id=sibling, device_id_type=MESH)

        gathers = [gcopy(a, k) for a in range(n_g) for k in range(1, N_DEV)]
        swaps = [scopy(a) for a in range(n_r)]
        for cp in gathers + swaps:
            cp.start()
        for a in range(n_g):
            g_out[a][pl.ds(pl.multiple_of(_dev_index(me) * SUBLANES, SUBLANES), SUBLANES), :] = g_in[a][...]
        for a in range(n_r):
            swaps[a].wait_recv()
            r_out[a][...] = r_in[a][...] + sib[a][...]

        for core in range(2):
            mine = [a for a in range(n_r) if a % 2 == core]
            theirs = [a for a in range(n_r) if a % 2 != core]

            @pl.when(c == core)
            def _():
                out = [icopy(a, j) for a in mine for j in range(3)]
                for cp in out:
                    cp.start()
                fwd = []
                for a in mine:
                    for j in range(3):
                        icopy(a, j).wait_recv()
                        cp = fcopy(a, j)
                        cp.start()
                        fwd.append(cp)
                for a in theirs:
                    for j in range(3):
                        fcopy(a, j).wait_recv()
                for cp in out + fwd:
                    cp.wait_send()

        for a in range(n_r):
            r_out[a][...] = (r_out[a][...] + land[a][1]) + (land[a][0] + land[a][2])
        for a in range(n_g):
            for k in range(1, N_DEV):
                pltpu.make_async_remote_copy(
                    src_ref=g_in[a], dst_ref=slot(a, _flip(me, k)), send_sem=g_send.at[a, k - 1],
                    recv_sem=g_recv.at[a, k - 1], device_id=_flip(me, k), device_id_type=MESH).wait_recv()
        for cp in gathers + swaps:
            cp.wait_send()

    shapes = [tuple(a.shape) for a in red]
    outs, carried = _call(
        body, "reduce_small", (1,), in_specs=[_whole()] * (n_g + n_r), out_specs=[_whole()] * (n_g + n_r),
        out_shape=[_sds((N_DEV * SUBLANES, a.shape[1]), F32) for a in gath] + [_sds(s, F32) for s in shapes],
        scratch=[pltpu.VMEM(s, F32) for s in shapes] + [pltpu.VMEM((3,) + s, F32) for s in shapes]
        + [pltpu.SemaphoreType.DMA((n_g, N_DEV - 1)), pltpu.SemaphoreType.DMA((n_g, N_DEV - 1)),
           pltpu.SemaphoreType.DMA((n_r,)), pltpu.SemaphoreType.DMA((n_r,)),
           pltpu.SemaphoreType.DMA((n_r, 3)), pltpu.SemaphoreType.DMA((n_r, 3)),
           pltpu.SemaphoreType.DMA((n_r, 3)), pltpu.SemaphoreType.DMA((n_r, 3))],
        args=tuple(gath) + tuple(red), carry=carry)
    return (outs[:n_g], outs[n_g:]), carried


STACKED = "stacked"


def _region(ref, shard_shape, col_sharded, pos):
    r, cdim = shard_shape
    d = _dev_index(pos)
    if col_sharded == STACKED:
        return ref.at[d]
    if col_sharded:
        return ref.at[:, pl.ds(pl.multiple_of(d * cdim, LANES), cdim)]
    return ref.at[pl.ds(pl.multiple_of(d * r, 2 * SUBLANES), r), :]


def _gather_carry(shards, col_sharded):
    n_w = len(shards)
    shapes = [tuple(s.shape) for s in shards]
    full_shapes = [(N_DEV,) + s if cs == STACKED else (s[0], s[1] * N_DEV) if cs else (s[0] * N_DEV, s[1])
                   for s, cs in zip(shapes, col_sharded)]

    def tools(out_refs, scr):
        send_sems, recv_sems = scr[n_w], scr[n_w + 1]
        me = _my_pos()
        x, y, c = me
        sibling = (x, y, 1 - c)
        chips = [(1 - x, y), (x, 1 - y), (1 - x, 1 - y)]

        def region(w, pos):
            return _region(out_refs[w], shapes[w], col_sharded[w], pos)

        def copy(w, k, block, to, src=None):
            return pltpu.make_async_remote_copy(
                src_ref=region(w, block) if src is None else src, dst_ref=region(w, block),
                send_sem=send_sems.at[w, k], recv_sem=recv_sems.at[w, k], device_id=to, device_id_type=MESH)

        def first(w):
            return [copy(w, 0, me, sibling, src=scr[w])] + [
                copy(w, 1 + j, me, (*chip, c), src=scr[w]) for j, chip in enumerate(chips)]

        def mine(w):
            return pltpu.make_async_copy(scr[w], region(w, me), scr[n_w + 2].at[w])

        return me, c, sibling, chips, copy, first, mine

    def start(ins, outs, scr):
        _, _, _, _, _, first, mine = tools(outs, scr)
        for w in range(n_w):
            scr[w][...] = ins[w][...].astype(BF16)
            for cp in first(w) + [mine(w)]:
                cp.start()

    def finish(ins, outs, scr):
        me, c, sibling, chips, copy, first, mine = tools(outs, scr)
        passed = []
        for w in range(n_w):
            for j, chip in enumerate(chips):
                copy(w, 1 + j, (*chip, c), me).wait_recv()
                fwd = copy(w, 4 + j, (*chip, c), sibling)
                fwd.start()
                passed.append(fwd)
        for w in range(n_w):
            copy(w, 0, sibling, me).wait_recv()
            for j, chip in enumerate(chips):
                copy(w, 4 + j, (*chip, 1 - c), me).wait_recv()
        for w in range(n_w):
            for cp in first(w):
                cp.wait_send()
            mine(w).wait()
        for cp in passed:
            cp.wait_send()

    return _Carry(
        inputs=list(shards), in_specs=[_whole()] * n_w,
        out_shape=[_sds(s, BF16) for s in full_shapes], out_specs=[_any()] * n_w,
        scratch=[pltpu.VMEM(s, BF16) for s in shapes]
        + [pltpu.SemaphoreType.DMA((n_w, N_DEV - 1)), pltpu.SemaphoreType.DMA((n_w, N_DEV - 1)),
           pltpu.SemaphoreType.DMA((n_w,))],
        start=start, finish=finish)


def _scatter_carry(grads_bf, shard_shapes, col_sharded, relations):
    n_w = len(grads_bf)
    shapes = [tuple(s) for s in shard_shapes]

    def copies(ins, outs, scr):
        send_sems, recv_sems = scr
        me = _my_pos()
        out = []
        for w in range(n_w):
            for i, k in enumerate(relations[w]):
                peer = _flip(me, k)
                out.append(pltpu.make_async_remote_copy(
                    src_ref=_region(ins[w], shapes[w], col_sharded[w], peer), dst_ref=outs[w].at[i],
                    send_sem=send_sems.at[w, i], recv_sem=recv_sems.at[w, i],
                    device_id=peer, device_id_type=MESH))
        return out

    def start(ins, outs, scr):
        for cp in copies(ins, outs, scr):
            cp.start()

    def finish(ins, outs, scr):
        cps = copies(ins, outs, scr)
        for cp in cps:
            cp.wait_recv()
        for cp in cps:
            cp.wait_send()

    return _Carry(
        inputs=list(grads_bf), in_specs=[_any()] * n_w,
        out_shape=[_sds((len(r),) + s, BF16) for r, s in zip(relations, shapes)], out_specs=[_any()] * n_w,
        scratch=[pltpu.SemaphoreType.DMA((n_w, N_DEV - 1)), pltpu.SemaphoreType.DMA((n_w, N_DEV - 1))],
        start=start, finish=finish)


def _block_diag(w):
    eye = jnp.eye(N_HEADS, dtype=w.dtype)
    return (eye[:, None, :, None] * w[:, :, None, :]).reshape(N_HEADS * HEAD_DIM, N_HEADS * HEAD_DIM)


def _pad_rows(a):
    return jnp.pad(a, ((0, SUBLANES - a.shape[0]), (0, 0)))


def _columns_from_devices(gathered, rows):
    w = gathered.shape[1]
    return gathered.reshape(N_DEV, SUBLANES, w)[:, :rows].transpose(1, 0, 2).reshape(rows, N_DEV * w)


def _local_step(x2, target, mod, w_in_f, w_full, conv_w_full, ffn_cw_full,
                g_mix_pre, g_mix_post, conv_b, w_rgate, b_rgate, w_igate, b_igate, lru_a, v_norm_g, v_norm_b,
                w_spatial, b_spatial, g_lru_out, g_gmlp_out, g_ffn_pre, g_ffn_post, ffn_conv_b,
                gather=None, scatter=None):
    sh_m, sc_m, gt_m, sh_f, sc_f, gt_f = [mod[k] for k in range(N_MOD)]
    wr_bd = _block_diag(w_rgate[0]).astype(BF16)
    wi_bd = _block_diag(w_igate[0]).astype(BF16)
    b_r = b_rgate.reshape(1, LRU_W)
    b_i = b_igate.reshape(1, LRU_W)
    b_sp_t = b_spatial[0].T
    w_sp_t = jnp.swapaxes(w_spatial[0], 1, 2)

    def arriving(*names):
        return gather(*names) if gather else None

    near, far = (1, 2, 3, 4, 5), (6, 7)

    def leaving(*parts):
        return scatter(parts) if scatter else None

    def received(recv, parts, outs):
        for (name, _, _), out in zip(parts, outs):
            recv.setdefault(name, []).append(out)

    mix_params = (conv_w_full, conv_b, wr_bd, wi_bd, b_r, b_i, lru_a, v_norm_g, v_norm_b)
    w_out_f = w_full["w_out"]
    (z, h, ycat, hl, y, x1, h2), got = _mix_fwd(
        x2, sh_m, sc_m, g_mix_pre, w_in_f, *mix_params, w_spatial[0], b_sp_t, g_lru_out, g_gmlp_out,
        w_out_f, g_mix_post, gt_m, g_ffn_pre, sc_f, sh_f, carry=arriving("w_up"))
    w_up_f = got[0] if gather else w_full["w_up"]
    (up_pre, up, act), got = _ffn_fwd(h2, w_up_f, ffn_cw_full, ffn_conv_b, carry=arriving("w_down"))
    w_down_f = got[0] if gather else w_full["w_down"]
    d_y2, dout, loss_acc, vs_ffn = _ffn_tail(act, w_down_f, x1, gt_f, g_ffn_post, target)

    recv = {}
    gw_down, _ = _wgrad(act, d_y2, FF_CHUNK_W, "wgrad_down", by_rows=True)
    parts = [("w_down", gw_down[1], near + far)]
    (d_up, cs_ffn), got = _ffn_bwd(d_y2, up_pre, up, ffn_cw_full, w_down_f, carry=leaving(*parts))
    received(recv, parts, got)
    gw_up, _ = _wgrad(h2, d_up, FF_CHUNK_W, "wgrad_up")
    parts = [("w_up", gw_up[1], near)]
    (d_x1, d_y, d_ycat, vs_up), got = _up_bwd(
        d_up, w_up_f, x1, dout, y, w_out_f, g_ffn_pre, sc_f, g_mix_post, gt_m, carry=leaving(*parts))
    received(recv, parts, got)
    gw_out, _ = _wgrad(ycat, d_y, D_MODEL, "wgrad_out")
    parts = [("w_up", gw_up[1], far), ("w_out", gw_out[1], near + far)]
    (d_z, vs_mix, dcw, d_wr, d_wi, d_ws, d_bs), got = _mix_bwd(
        d_ycat, z, hl, *mix_params, w_spatial[0], w_sp_t, b_sp_t, g_lru_out, g_gmlp_out, carry=leaving(*parts))
    received(recv, parts, got)
    gw_in, _ = _wgrad(h, d_z, IN_COLS // 2, "wgrad_in")
    parts = [("w_in", gw_in[1], near)]
    (grad_x, vs_in), got = _in_bwd(d_z, w_in_f, x2, d_x1, g_mix_pre, sc_m, carry=leaving(*parts))
    received(recv, parts, got)
    pending = [("w_in", gw_in[1], far)]

    gath = [vs_in, vs_up, vs_ffn, loss_acc]
    red = [cs_ffn, vs_mix, dcw, d_wr, d_wi, d_ws.reshape(N_GROUPS * POS_BLOCK, POS_BLOCK), d_bs]
    return dict(grad_x=grad_x, gath=gath, red=red, recv=recv, pending=pending,
                w_in=gw_in, w_out=gw_out, w_up=gw_up, w_down=gw_down)


def kernel(x, c, w_ada, b_ada, g_mix_pre, g_mix_post, w_in, conv_w, conv_b, w_rgate, b_rgate, w_igate, b_igate, lru_a, v_norm_g, v_norm_b, w_spatial, b_spatial, g_lru_out, g_gmlp_out, w_out, g_ffn_pre, g_ffn_post, w_up, ffn_conv_w, ffn_conv_b, w_down, loss_target, m_w_ada, m_b_ada, m_g_mix_pre, m_g_mix_post, m_w_in, m_conv_w, m_conv_b, m_w_rgate, m_b_rgate, m_w_igate, m_b_igate, m_lru_a, m_v_norm_g, m_v_norm_b, m_w_spatial, m_b_spatial, m_g_lru_out, m_g_gmlp_out, m_w_out, m_g_ffn_pre, m_g_ffn_post, m_w_up, m_ffn_conv_w, m_ffn_conv_b, m_w_down, v_w_ada, v_b_ada, v_g_mix_pre, v_g_mix_post, v_w_in, v_conv_w, v_conv_b, v_w_rgate, v_b_rgate, v_w_igate, v_b_igate, v_lru_a, v_v_norm_g, v_v_norm_b, v_w_spatial, v_b_spatial, v_g_lru_out, v_g_gmlp_out, v_w_out, v_g_ffn_pre, v_g_ffn_post, v_w_up, v_ffn_conv_w, v_ffn_conv_b, v_w_down):
    me = _dev_index(_my_pos())
    ada_cols = w_ada.shape[-1]

    big_w = dict(w_in=(w_in, m_w_in, v_w_in, True), w_out=(w_out, m_w_out, v_w_out, False),
                 w_up=(w_up, m_w_up, v_w_up, True), w_down=(w_down, m_w_down, v_w_down, False))

    def gather(*names):
        return _gather_carry([big_w[n][0][0] for n in names], [STACKED if n == "w_up" else big_w[n][3] for n in names])

    def scatter(parts):
        return _scatter_carry([g for _, g, _ in parts], [big_w[n][0].shape[1:] for n, _, _ in parts],
                              [big_w[n][3] for n, _, _ in parts], [rel for _, _, rel in parts])

    (c_all, cw_all, fcw_all, mod_all), (w_in_f, w_out_f) = _prologue(
        jnp.broadcast_to(c, (SUBLANES, D_MODEL)), _pad_rows(conv_w[0]), _pad_rows(ffn_conv_w[0]), w_ada[0], b_ada,
        carry=gather("w_in", "w_out"))
    conv_w_full = _columns_from_devices(cw_all, LRU_CONV_K)
    ffn_cw_full = _columns_from_devices(fcw_all, FFN_CONV_K)
    mod = lax.dynamic_index_in_dim(mod_all.reshape(N_DEV, N_DEV, ada_cols), me, axis=1, keepdims=False)
    mod = mod.reshape(N_MOD, 1, D_MODEL)

    loc = _local_step(x[0], loss_target[0], mod, w_in_f, dict(w_out=w_out_f), conv_w_full, ffn_cw_full,
                      g_mix_pre, g_mix_post, conv_b, w_rgate, b_rgate, w_igate, b_igate, lru_a, v_norm_g, v_norm_b,
                      w_spatial, b_spatial, g_lru_out, g_gmlp_out, g_ffn_pre, g_ffn_post, ffn_conv_b,
                      gather=gather, scatter=scatter)
    grad_x = loc["grad_x"]

    (gathered, reduced), got = _reduce_small(loc["gath"], loc["red"], carry=scatter(loc["pending"]))
    for (name, _, _), out in zip(loc["pending"], got):
        loc["recv"][name].append(out)

    results = {}
    for name, (w_, m_, v_, cs) in big_w.items():
        results[name] = _adamw_sum(w_, loc[name][0], loc["recv"][name], m_, v_, cs, "adamw_" + name)

    params = dict(
        b_ada=(b_ada, m_b_ada, v_b_ada), g_mix_pre=(g_mix_pre, m_g_mix_pre, v_g_mix_pre),
        g_mix_post=(g_mix_post, m_g_mix_post, v_g_mix_post), conv_b=(conv_b, m_conv_b, v_conv_b),
        w_rgate=(w_rgate, m_w_rgate, v_w_rgate), b_rgate=(b_rgate, m_b_rgate, v_b_rgate),
        w_igate=(w_igate, m_w_igate, v_w_igate), b_igate=(b_igate, m_b_igate, v_b_igate),
        lru_a=(lru_a, m_lru_a, v_lru_a), v_norm_g=(v_norm_g, m_v_norm_g, v_v_norm_g),
        v_norm_b=(v_norm_b, m_v_norm_b, v_v_norm_b), w_spatial=(w_spatial, m_w_spatial, v_w_spatial),
        b_spatial=(b_spatial, m_b_spatial, v_b_spatial), g_lru_out=(g_lru_out, m_g_lru_out, v_g_lru_out),
        g_gmlp_out=(g_gmlp_out, m_g_gmlp_out, v_g_gmlp_out), g_ffn_pre=(g_ffn_pre, m_g_ffn_pre, v_g_ffn_pre),
        g_ffn_post=(g_ffn_post, m_g_ffn_post, v_g_ffn_post), ffn_conv_b=(ffn_conv_b, m_ffn_conv_b, v_ffn_conv_b))
    conv_params = dict(conv_w=(conv_w, m_conv_w, v_conv_w), ffn_conv_w=(ffn_conv_w, m_ffn_conv_w, v_ffn_conv_w))
    small_results, loss = _adamw_small(gathered, reduced, params, conv_params)
    results.update(small_results)
    loss = loss.reshape(())

    results["w_ada"] = _adamw_wada(c_all, gathered[0], gathered[1], gathered[2], w_ada, m_w_ada, v_w_ada)

    order = ["w_ada", "b_ada", "g_mix_pre", "g_mix_post", "w_in", "conv_w", "conv_b", "w_rgate", "b_rgate", "w_igate",
             "b_igate", "lru_a", "v_norm_g", "v_norm_b", "w_spatial", "b_spatial", "g_lru_out", "g_gmlp_out", "w_out",
             "g_ffn_pre", "g_ffn_post", "w_up", "ffn_conv_w", "ffn_conv_b", "w_down"]
    outs = [loss, grad_x[None]]
    for kind in range(4):
        outs += [results[n][kind] for n in order]
    return tuple(outs)
```

```python
import functools

import jax
import jax.numpy as jnp
from jax import lax
from jax.experimental import pallas as pl
from jax.experimental.pallas import tpu as pltpu

F32 = jnp.float32
BF16 = jnp.bfloat16

D_MODEL = 1024
LRU_W = 512
GMLP_W = 512
N_HEADS = 8
HEAD_DIM = 64
N_GROUPS = 4
POS_BLOCK = 128
CHUNK = 64
IN_COLS = 2048
D_FF = 3072
N_MOD = 6
N_DEV = 8
EPS = 1e-6
LRU_C = 8.0
LRU_CONV_K = 4
FFN_CONV_K = 3

ADAM_LR = 0.001
ADAM_B1 = 0.9
ADAM_B2 = 0.999
ADAM_EPS = 1e-08
ADAM_WD = 0.01
ADAM_STEP = 10

LANES = 128
SUBLANES = 8
TT_BIG = 512
TT_MIX = 256
FF_CW = 512
VMEM_LIMIT = 56 * 1024 * 1024

MESH = pl.DeviceIdType.MESH


def _sds(shape, dtype):
    return jax.ShapeDtypeStruct(shape, dtype)


def _cparams(sem=None):
    return pltpu.CompilerParams(dimension_semantics=sem, vmem_limit_bytes=VMEM_LIMIT)


def _whole():
    return pl.BlockSpec(memory_space=pltpu.VMEM)


def _const(shape):
    nd = len(shape)
    return pl.BlockSpec(shape, lambda *_: (0,) * nd)


def _any():
    return pl.BlockSpec(memory_space=pl.ANY)


class _Carry:
    def __init__(self, inputs, in_specs, out_shape, out_specs, scratch, start, finish):
        self.inputs, self.in_specs, self.out_shape, self.out_specs = inputs, in_specs, out_shape, out_specs
        self.scratch, self.start, self.finish = scratch, start, finish


def _call(body, name, grid, in_specs, out_specs, out_shape, scratch, args, carry=None):
    n_in, n_out, n_scr = len(in_specs), len(out_specs), len(scratch)
    c_in = len(carry.in_specs) if carry else 0
    c_out = len(carry.out_specs) if carry else 0

    def full_body(*refs):
        ins = refs[:n_in]
        c_ins = refs[n_in:n_in + c_in]
        outs = refs[n_in + c_in:n_in + c_in + n_out]
        c_outs = refs[n_in + c_in + n_out:n_in + c_in + n_out + c_out]
        scr = refs[n_in + c_in + n_out + c_out:n_in + c_in + n_out + c_out + n_scr]
        c_scr = refs[n_in + c_in + n_out + c_out + n_scr:]
        if carry:
            first = functools.reduce(lambda a, b: a & b, [pl.program_id(d) == 0 for d in range(len(grid))])
            last = functools.reduce(lambda a, b: a & b, [pl.program_id(d) == g - 1 for d, g in enumerate(grid)])

            @pl.when(first)
            def _():
                carry.start(c_ins, c_outs, c_scr)

        body(*ins, *outs, *scr)
        if carry:
            @pl.when(last)
            def _():
                carry.finish(c_ins, c_outs, c_scr)

    res = pl.pallas_call(
        full_body, name=name, grid=grid,
        in_specs=list(in_specs) + (list(carry.in_specs) if carry else []),
        out_specs=list(out_specs) + (list(carry.out_specs) if carry else []),
        out_shape=list(out_shape) + (list(carry.out_shape) if carry else []),
        scratch_shapes=list(scratch) + (list(carry.scratch) if carry else []),
        compiler_params=_cparams(("arbitrary",) * len(grid)),
    )(*args, *(carry.inputs if carry else []))
    return res[:n_out], res[n_out:]


def _gelu(x):
    u = 0.7978845608028654 * (x + 0.044715 * x * x * x)
    return 0.5 * x * (1.0 + jnp.tanh(u))


def _gelu_and_grad(x):
    x2 = x * x
    u = 0.7978845608028654 * (x + 0.044715 * x * x2)
    t = jnp.tanh(u)
    g = 0.5 * x * (1.0 + t)
    dg = 0.5 * (1.0 + t) + 0.5 * x * (1.0 - t * t) * 0.7978845608028654 * (1.0 + 3.0 * 0.044715 * x2)
    return g, dg


def _sigmoid(x):
    return 1.0 / (1.0 + jnp.exp(-x))


def _softplus(x):
    return jnp.maximum(x, 0.0) + jnp.log1p(jnp.exp(-jnp.abs(x)))


def _neg_expm1(x):
    series = -x * (1.0 + x * (0.5 + x * (1.0 / 6.0 + x * (1.0 / 24.0 + x * (1.0 / 120.0)))))
    return jnp.where(x > -0.1, series, 1.0 - jnp.exp(x))


def _dot(a, b):
    return jnp.dot(a.astype(BF16), b.astype(BF16), preferred_element_type=F32)


def _dot_nt(a, b):
    return lax.dot_general(a.astype(BF16), b.astype(BF16), (((1,), (1,)), ((), ())), preferred_element_type=F32)


def _dot_tn(a, b):
    return lax.dot_general(a.astype(BF16), b.astype(BF16), (((0,), (0,)), ((), ())), preferred_element_type=F32)


def _rows(shape):
    return lax.broadcasted_iota(jnp.int32, shape, 0)


def _shift_down(cur, prev8, s):
    if s == 0:
        return cur
    n = cur.shape[0]
    r = pltpu.roll(cur, s, 0)
    p = pltpu.roll(prev8, s, 0)
    top = jnp.where(_rows(p.shape) < s, p, r[0:SUBLANES])
    if n == SUBLANES:
        return top
    return jnp.concatenate([top, r[SUBLANES:]], axis=0)


def _shift_up(cur, next8, s):
    if s == 0:
        return cur
    n = cur.shape[0]
    r = pltpu.roll(cur, n - s, 0)
    q = pltpu.roll(next8, SUBLANES - s, 0)
    bot = jnp.where(_rows(q.shape) >= SUBLANES - s, q, r[n - SUBLANES:])
    if n == SUBLANES:
        return bot
    return jnp.concatenate([r[:n - SUBLANES], bot], axis=0)


def _scan_fwd(a, b):
    n = a.shape[0]
    rows = _rows(a.shape)
    s = 1
    while s < n:
        a_s = pltpu.roll(a, s, 0)
        b_s = pltpu.roll(b, s, 0)
        m = rows >= s
        b = jnp.where(m, a * b_s + b, b)
        a = jnp.where(m, a * a_s, a)
        s *= 2
    return a, b


def _scan_rev(a, b):
    n = a.shape[0]
    rows = _rows(a.shape)
    s = 1
    while s < n:
        a_s = pltpu.roll(a, n - s, 0)
        b_s = pltpu.roll(b, n - s, 0)
        m = rows < n - s
        b = jnp.where(m, b + a * b_s, b)
        a = jnp.where(m, a * a_s, a)
        s *= 2
    return a, b


def _rms(x):
    r = lax.rsqrt(jnp.mean(x * x, axis=-1, keepdims=True) + EPS)
    return x * r, r


def _rms_bwd(d_n, n, r):
    return r * (d_n - n * jnp.mean(d_n * n, axis=-1, keepdims=True))


def _colsum(x):
    return jnp.sum(x, axis=0, keepdims=True)


def _lru_gates(xc, wr_ref, wi_ref, br, bi, sp_a):
    r = _sigmoid(_dot(xc, wr_ref[...]) + br)
    i = _sigmoid(_dot(xc, wi_ref[...]) + bi)
    la = -LRU_C * r * sp_a
    a = jnp.exp(la)
    mult = jnp.sqrt(_neg_expm1(2.0 * la))
    return r, i, a, mult


def _lru_conv(lx, prev8, cw_ref, cb):
    xc = cb + cw_ref[LRU_CONV_K - 1:LRU_CONV_K, :] * lx
    taps = []
    for k in range(LRU_CONV_K - 1):
        tap = _shift_down(lx, prev8, LRU_CONV_K - 1 - k)
        taps.append(tap)
        xc = xc + cw_ref[k:k + 1, :] * tap
    return xc, taps


def _ws_mask(transposed=False):
    i = lax.broadcasted_iota(jnp.int32, (POS_BLOCK, POS_BLOCK), 0)
    j = lax.broadcasted_iota(jnp.int32, (POS_BLOCK, POS_BLOCK), 1)
    if transposed:
        i, j = j, i
    return (j // CHUNK) <= (i // CHUNK)


def _gmlp_v(gv, vg, vb):
    av, dav = _gelu_and_grad(gv)
    mu = jnp.mean(av, axis=-1, keepdims=True)
    cen = av - mu
    rs = lax.rsqrt(jnp.mean(cen * cen, axis=-1, keepdims=True) + EPS)
    vhat = cen * rs
    return vhat * vg + vb, vhat, rs, dav


def _mix_fwd(x, sh, sc, g_pre, w_in, conv_w, conv_b, wr_bd, wi_bd, b_r, b_i, lru_a, vn_g, vn_b, w_sp, b_sp_t,
             g_lru, g_gmlp, w_out, g_post, gt_m, g_ffn_pre, sc_f, sh_f, carry=None):
    s_len = x.shape[0]
    tt = min(TT_MIX, s_len)
    nblk = tt // POS_BLOCK

    def body(x_ref, sh_ref, sc_ref, g_ref, w_ref, cw_ref, cb_ref, wr_ref, wi_ref, br_ref, bi_ref, la_ref, vg_ref,
             vb_ref, ws_ref, bst_ref, gl_ref, gg_ref, wo_ref, gp_ref, gtm_ref, g2_ref, scf_ref, shf_ref,
             z_ref, h_ref, y_ref, hl_ref, yo_ref, x1_ref, h2_ref, prev8, hcar):
        i = pl.program_id(0)

        @pl.when(i == 0)
        def _():
            prev8[...] = jnp.zeros_like(prev8)
            hcar[...] = jnp.zeros_like(hcar)

        n_x, _ = _rms(x_ref[...])
        h = (n_x * g_ref[...] * (1.0 + sc_ref[...]) + sh_ref[...]).astype(BF16)
        h_ref[...] = h
        z_ref[...] = jnp.dot(h, w_ref[...], preferred_element_type=F32)

        lx = z_ref[:, 0:LRU_W]
        gate = z_ref[:, LRU_W:2 * LRU_W]
        gu = z_ref[:, 2 * LRU_W:2 * LRU_W + GMLP_W]
        gv = z_ref[:, 2 * LRU_W + GMLP_W:]

        xc, _ = _lru_conv(lx, prev8[...], cw_ref, cb_ref[...])
        prev8[...] = lx[tt - SUBLANES:]
        sp_a = _softplus(-la_ref[...])
        _, ig, a, mult = _lru_gates(xc, wr_ref, wi_ref, br_ref[...], bi_ref[...], sp_a)
        bx = mult * (ig * xc)
        a_cum, b_cum = _scan_fwd(a, bx)
        hl = a_cum * hcar[0:1, :] + b_cum
        hcar[...] = jnp.broadcast_to(hl[tt - 1:tt, :], hcar.shape)
        hl_ref[...] = hl
        y_lru = hl * _gelu(gate)
        n_l, _ = _rms(y_lru)
        y_ref[:, 0:LRU_W] = (n_l * gl_ref[...]).astype(BF16)

        u = _gelu(gu)
        v, _, _, _ = _gmlp_v(gv, vg_ref[...], vb_ref[...])
        mask = _ws_mask()
        sp_parts = []
        for nb in range(nblk):
            row = []
            for g in range(N_GROUPS):
                wsm = jnp.where(mask, ws_ref[g], 0.0)
                vblk = v[nb * POS_BLOCK:(nb + 1) * POS_BLOCK, g * LANES:(g + 1) * LANES]
                row.append(_dot(wsm, vblk) + bst_ref[:, g:g + 1])
            sp_parts.append(jnp.concatenate(row, axis=1))
        sp = jnp.concatenate(sp_parts, axis=0) if nblk > 1 else sp_parts[0]
        n_g, _ = _rms(u * sp)
        y_ref[:, LRU_W:] = (n_g * gg_ref[...]).astype(BF16)

        y = jnp.dot(y_ref[...], wo_ref[...], preferred_element_type=F32)
        yo_ref[...] = y
        n_y, _ = _rms(y)
        x1 = x_ref[...] + gtm_ref[...] * (n_y * gp_ref[...])
        x1_ref[...] = x1
        n1, _ = _rms(x1)
        h2_ref[...] = (n1 * g2_ref[...] * (1.0 + scf_ref[...]) + shf_ref[...]).astype(BF16)

    row = lambda c: pl.BlockSpec((tt, c), lambda i: (i, 0))
    v512 = _const((1, LRU_W))
    vec = _const((1, D_MODEL))
    return _call(
        body, "mix_fwd", (s_len // tt,),
        in_specs=[row(D_MODEL), vec, vec, vec, _whole(),
                  _const((LRU_CONV_K, LRU_W)), v512, _whole(), _whole(), v512, v512, v512, v512, v512,
                  _whole(), _whole(), v512, v512, _whole(), vec, vec, vec, vec, vec],
        out_specs=[row(IN_COLS), row(D_MODEL), row(LRU_W + GMLP_W), row(LRU_W), row(D_MODEL), row(D_MODEL),
                   row(D_MODEL)],
        out_shape=[_sds((s_len, IN_COLS), F32), _sds((s_len, D_MODEL), BF16),
                   _sds((s_len, LRU_W + GMLP_W), BF16), _sds((s_len, LRU_W), F32),
                   _sds((s_len, D_MODEL), F32), _sds((s_len, D_MODEL), F32), _sds((s_len, D_MODEL), BF16)],
        scratch=[pltpu.VMEM((SUBLANES, LRU_W), F32), pltpu.VMEM((SUBLANES, LRU_W), F32)],
        args=(x, sh, sc, g_pre, w_in, conv_w, conv_b, wr_bd, wi_bd, b_r, b_i, lru_a, vn_g, vn_b, w_sp, b_sp_t,
              g_lru, g_gmlp, w_out, g_post, gt_m, g_ffn_pre, sc_f, sh_f), carry=carry)


FF_CHUNKS = N_DEV // 2
FF_CHUNK_W = D_FF // FF_CHUNKS


def _ffn_fwd(h2, w_up3, ffn_cw, ffn_cb, carry=None):
    s_len = h2.shape[0]
    tt = min(TT_MIX, s_len)
    nc, cw = FF_CHUNKS, FF_CHUNK_W

    def body(h2_ref, wu_ref, cwg_ref, cwv_ref, cbg_ref, cbv_ref, up_ref, upc_ref, act_ref, prev):
        i = pl.program_id(0)
        c = pl.program_id(1)

        @pl.when(i == 0)
        def _():
            prev[c] = jnp.zeros((2, SUBLANES, cw), F32)

        h2 = h2_ref[...]
        ug_pre = jnp.dot(h2, wu_ref[c], preferred_element_type=F32)
        uv_pre = jnp.dot(h2, wu_ref[nc + c], preferred_element_type=F32)
        up_ref[0] = ug_pre.astype(BF16)
        up_ref[1] = uv_pre.astype(BF16)
        ug, _ = _ffn_conv(ug_pre, prev[c, 0], cwg_ref, cbg_ref[...])
        uv, _ = _ffn_conv(uv_pre, prev[c, 1], cwv_ref, cbv_ref[...])
        prev[c, 0] = ug_pre[tt - SUBLANES:, :]
        prev[c, 1] = uv_pre[tt - SUBLANES:, :]
        upc_ref[0] = ug
        upc_ref[1] = uv
        act_ref[...] = (_gelu(ug) * uv).astype(BF16)

    chunk2 = pl.BlockSpec((2, tt, cw), lambda i, c: (0, i, c))
    ffn_cb2 = ffn_cb.reshape(1, 2 * D_FF)
    return _call(
        body, "ffn_fwd", (s_len // tt, nc),
        in_specs=[pl.BlockSpec((tt, D_MODEL), lambda i, c: (i, 0)), _whole(),
                  pl.BlockSpec((FFN_CONV_K, cw), lambda i, c: (0, c)),
                  pl.BlockSpec((FFN_CONV_K, cw), lambda i, c: (0, c + nc)),
                  pl.BlockSpec((1, cw), lambda i, c: (0, c)),
                  pl.BlockSpec((1, cw), lambda i, c: (0, c + nc))],
        out_specs=[chunk2, chunk2, pl.BlockSpec((tt, cw), lambda i, c: (i, c))],
        out_shape=[_sds((2, s_len, D_FF), BF16), _sds((2, s_len, D_FF), F32), _sds((s_len, D_FF), BF16)],
        scratch=[pltpu.VMEM((nc, 2, SUBLANES, cw), F32)],
        args=(h2, w_up3, ffn_cw, ffn_cw, ffn_cb2, ffn_cb2), carry=carry)


def _ffn_tail(act, w_down, x1, gt_f, g_post, target):
    s_len = x1.shape[0]
    tt = min(TT_BIG, s_len)

    def body(act_ref, wd_ref, x1_ref, gtf_ref, gp_ref, tg_ref, dy2_ref, dout_ref, loss_ref, vs_ref):
        @pl.when(pl.program_id(0) == 0)
        def _():
            loss_ref[...] = jnp.zeros_like(loss_ref)
            vs_ref[...] = jnp.zeros_like(vs_ref)

        n2, r2 = _rms(jnp.dot(act_ref[...], wd_ref[...], preferred_element_type=F32))
        out = x1_ref[...] + gtf_ref[...] * (n2 * gp_ref[...])
        err = out - tg_ref[...]
        do = err * (1.0 / D_MODEL)
        dout_ref[...] = do
        loss_ref[...] += jnp.broadcast_to(0.5 * jnp.sum(err * err, keepdims=True) * (1.0 / D_MODEL), loss_ref.shape)
        vs_ref[0:1, :] += _colsum(do * n2 * gp_ref[...])
        vs_ref[1:2, :] += _colsum(do * gtf_ref[...] * n2)
        dy2_ref[...] = _rms_bwd(do * gtf_ref[...] * gp_ref[...], n2, r2).astype(BF16)

    row = lambda c: pl.BlockSpec((tt, c), lambda i: (i, 0))
    vec = _const((1, D_MODEL))
    outs, _ = _call(
        body, "ffn_tail", (s_len // tt,),
        in_specs=[row(D_FF), _whole(), row(D_MODEL), vec, vec, row(D_MODEL)],
        out_specs=[row(D_MODEL), row(D_MODEL), _const((SUBLANES, LANES)), _const((SUBLANES, D_MODEL))],
        out_shape=[_sds((s_len, D_MODEL), BF16), _sds((s_len, D_MODEL), F32), _sds((SUBLANES, LANES), F32),
                   _sds((SUBLANES, D_MODEL), F32)],
        scratch=[], args=(act, w_down, x1, gt_f, g_post, target))
    return outs


def _ffn_conv(up_pre, prev8, cw_ref, cb):
    up = cb + cw_ref[FFN_CONV_K - 1:FFN_CONV_K, :] * up_pre
    taps = []
    for k in range(FFN_CONV_K - 1):
        tap = _shift_down(up_pre, prev8, FFN_CONV_K - 1 - k)
        taps.append(tap)
        up = up + cw_ref[k:k + 1, :] * tap
    return up, taps


def _ffn_bwd(d_y2, up_pre, up, ffn_cw, w_down, carry=None):
    s_len = d_y2.shape[0]
    tt = min(TT_BIG, s_len)
    nt = s_len // tt
    cw = FF_CW
    nc = D_FF // cw

    def body(dy2_ref, up_ref, upc_ref, cwg_ref, cwv_ref, wd_ref, dup_ref, cs_ref, nxt, cs_acc):
        i = pl.program_id(0)
        c = pl.program_id(1)

        @pl.when(i == 0)
        def _():
            nxt[c] = jnp.zeros((2, SUBLANES, cw), F32)
            cs_acc[c] = jnp.zeros((2, SUBLANES, cw), F32)

        pw = cw // 2
        for piece in range(2):
            cols = slice(piece * pw, (piece + 1) * pw)
            d_act = _dot_nt(dy2_ref[...], wd_ref[cols, :])
            uv = upc_ref[1, :, cols]
            gl, dgl = _gelu_and_grad(upc_ref[0, :, cols])
            d_ug = d_act * uv * dgl
            d_uv = d_act * gl
            for half, (d_u, cw_ref) in enumerate(((d_ug, cwg_ref), (d_uv, cwv_ref))):
                nx = nxt[c, half, :, cols]
                x_in = up_ref[half, :, cols].astype(F32)
                d_pre = cw_ref[FFN_CONV_K - 1:FFN_CONV_K, cols] * d_u
                sums = [None] * (FFN_CONV_K + 1)
                sums[FFN_CONV_K - 1] = _colsum(d_u * x_in)
                for k in range(FFN_CONV_K - 1):
                    ahead = _shift_up(d_u, nx, FFN_CONV_K - 1 - k)
                    d_pre = d_pre + cw_ref[k:k + 1, cols] * ahead
                    sums[k] = _colsum(ahead * x_in)
                sums[FFN_CONV_K] = _colsum(d_u)
                pad = jnp.zeros((SUBLANES - FFN_CONV_K - 1, pw), F32)
                cs_acc[c, half, :, cols] += jnp.concatenate(sums + [pad], axis=0)
                nxt[c, half, :, cols] = d_u[0:SUBLANES]
                dup_ref[half, :, cols] = d_pre.astype(BF16)

        for cc in range(nc):
            @pl.when((i == nt - 1) & (c == cc))
            def _():
                cs_ref[:, cc * cw:(cc + 1) * cw] = cs_acc[cc, 0]
                cs_ref[:, D_FF + cc * cw:D_FF + (cc + 1) * cw] = cs_acc[cc, 1]

    row = pl.BlockSpec((tt, D_MODEL), lambda i, c: (nt - 1 - i, 0))
    blk = pl.BlockSpec((2, tt, cw), lambda i, c: (0, nt - 1 - i, c))
    return _call(
        body, "ffn_bwd", (nt, nc),
        in_specs=[row, blk, blk,
                  pl.BlockSpec((FFN_CONV_K, cw), lambda i, c: (0, c)),
                  pl.BlockSpec((FFN_CONV_K, cw), lambda i, c: (0, c + nc)),
                  pl.BlockSpec((cw, D_MODEL), lambda i, c: (c, 0))],
        out_specs=[blk, _const((SUBLANES, 2 * D_FF))],
        out_shape=[_sds((2, s_len, D_FF), BF16), _sds((SUBLANES, 2 * D_FF), F32)],
        scratch=[pltpu.VMEM((nc, 2, SUBLANES, cw), F32), pltpu.VMEM((nc, 2, SUBLANES, cw), F32)],
        args=(d_y2, up_pre, up, ffn_cw, ffn_cw, w_down), carry=carry)


def _up_bwd(d_up, w_up3, x1, dout, y, w_out, g_pre, sc_f, g_post, gt_m, carry=None):
    s_len = x1.shape[0]
    tt = min(TT_BIG, s_len)

    def body(du_ref, wu_ref, x1_ref, do_ref, y_ref, wo_ref, g2_ref, sc_ref, gp_ref, gt_ref,
             dx1_ref, dy_ref, dyc_ref, vs_ref):
        @pl.when(pl.program_id(0) == 0)
        def _():
            vs_ref[...] = jnp.zeros_like(vs_ref)

        d_h2 = jnp.zeros((tt, D_MODEL), F32)
        for half in range(2):
            for ch in range(FF_CHUNKS):
                d_h2 = d_h2 + _dot_nt(du_ref[half, :, ch * FF_CHUNK_W:(ch + 1) * FF_CHUNK_W],
                                      wu_ref[half * FF_CHUNKS + ch])
        n1, r1 = _rms(x1_ref[...])
        ng = n1 * g2_ref[...]
        vs_ref[0:1, :] += _colsum(d_h2)
        vs_ref[1:2, :] += _colsum(d_h2 * ng)
        d_ng = d_h2 * (1.0 + sc_ref[...])
        vs_ref[2:3, :] += _colsum(d_ng * n1)
        d_x1 = do_ref[...] + _rms_bwd(d_ng * g2_ref[...], n1, r1)
        dx1_ref[...] = d_x1
        n_y, r_y = _rms(y_ref[...])
        vs_ref[3:4, :] += _colsum(d_x1 * n_y * gp_ref[...])
        d_on = d_x1 * gt_ref[...]
        vs_ref[4:5, :] += _colsum(d_on * n_y)
        d_y = _rms_bwd(d_on * gp_ref[...], n_y, r_y).astype(BF16)
        dy_ref[...] = d_y
        dyc_ref[...] = _dot_nt(d_y, wo_ref[...])

    row = lambda c: pl.BlockSpec((tt, c), lambda i: (i, 0))
    vec = _const((1, D_MODEL))
    return _call(
        body, "up_bwd", (s_len // tt,),
        in_specs=[pl.BlockSpec((2, tt, D_FF), lambda i: (0, i, 0)), _whole(), row(D_MODEL), row(D_MODEL), row(D_MODEL),
                  _whole(), vec, vec, vec, vec],
        out_specs=[row(D_MODEL), row(D_MODEL), row(LRU_W + GMLP_W), _const((SUBLANES, D_MODEL))],
        out_shape=[_sds((s_len, D_MODEL), F32), _sds((s_len, D_MODEL), BF16), _sds((s_len, LRU_W + GMLP_W), F32),
                   _sds((SUBLANES, D_MODEL), F32)],
        scratch=[], args=(d_up, w_up3, x1, dout, y, w_out, g_pre, sc_f, g_post, gt_m), carry=carry)


def _head_pair_block(hd):
    return (slice((hd // 2) * HEAD_DIM, (hd // 2 + 1) * HEAD_DIM), slice((hd % 2) * HEAD_DIM, (hd % 2 + 1) * HEAD_DIM))


def _mix_bwd(d_ycat, z, hl, conv_w, conv_b, wr_bd, wi_bd, b_r, b_i, lru_a, vn_g, vn_b, w_sp, w_sp_t, b_sp_t,
             g_lru, g_gmlp, carry=None):
    s_len = z.shape[0]
    tt = min(TT_MIX, s_len)
    nt = s_len // tt
    nblk = tt // POS_BLOCK
    hb = tt // SUBLANES

    def body(dyc_ref, z_ref, zh_ref, hl_ref, hh_ref, cw_ref, cb_ref, wr_ref, wi_ref, br_ref, bi_ref, la_ref,
             vg_ref, vb_ref, ws_ref, wst_ref, bst_ref, gl_ref, gg_ref,
             dz_ref, vs_ref, dcw_ref, dwrb_ref, dwib_ref, dws_ref, dbs_ref, nxt_dxc, nxt_a, nxt_lam, dwr_ref, dwi_ref):
        i = pl.program_id(0)
        first_tile = i == nt - 1

        @pl.when(i == 0)
        def _():
            for ref in (vs_ref, dcw_ref, dwr_ref, dwi_ref, dws_ref, dbs_ref, nxt_dxc, nxt_a, nxt_lam):
                ref[...] = jnp.zeros_like(ref)

        lx = z_ref[:, 0:LRU_W]
        gate = z_ref[:, LRU_W:2 * LRU_W]
        gu = z_ref[:, 2 * LRU_W:2 * LRU_W + GMLP_W]
        gv = z_ref[:, 2 * LRU_W + GMLP_W:]
        prev8 = jnp.where(first_tile, 0.0, zh_ref[...])
        hprev8 = jnp.where(first_tile, 0.0, hh_ref[...])

        xc, taps = _lru_conv(lx, prev8, cw_ref, cb_ref[...])
        a_par = la_ref[...]
        sp_a = _softplus(-a_par)
        r, ig, a, mult = _lru_gates(xc, wr_ref, wi_ref, br_ref[...], bi_ref[...], sp_a)
        hl = hl_ref[...]
        h_prev = _shift_down(hl, hprev8, 1)
        ggate, dggate = _gelu_and_grad(gate)
        y_lru = hl * ggate
        n_l, r_l = _rms(y_lru)
        d_nl = dyc_ref[:, 0:LRU_W]
        vs_ref[6:7, :] += _colsum(d_nl * n_l)
        d_yl = _rms_bwd(d_nl * gl_ref[...], n_l, r_l)
        d_hl = d_yl * ggate
        d_gate = d_yl * hl * dggate
        a_up = _shift_up(a, nxt_a[...], 1)
        a_cum, b_cum = _scan_rev(a_up, d_hl)
        lam = b_cum + a_cum * nxt_lam[0:1, :]
        nxt_a[...] = jnp.broadcast_to(a[0:1, :], nxt_a.shape)
        nxt_lam[...] = jnp.broadcast_to(lam[0:1, :], nxt_lam.shape)
        ixc = ig * xc
        d_la = lam * h_prev * a - lam * ixc * (a * a) / mult
        d_i = lam * mult * xc
        d_xc = lam * mult * ig
        vs_ref[3:4, :] += _colsum(d_la * r) * (LRU_C * _sigmoid(-a_par))
        d_pr = d_la * (-LRU_C * sp_a) * r * (1.0 - r)
        d_pi = d_i * ig * (1.0 - ig)
        vs_ref[1:2, :] += _colsum(d_pr)
        vs_ref[2:3, :] += _colsum(d_pi)
        dwr_ref[...] += _dot_tn(xc, d_pr)
        dwi_ref[...] += _dot_tn(xc, d_pi)
        d_xc = d_xc + _dot_nt(d_pr, wr_ref[...]) + _dot_nt(d_pi, wi_ref[...])
        vs_ref[0:1, :] += _colsum(d_xc)
        nx = nxt_dxc[...]
        d_lx = cw_ref[LRU_CONV_K - 1:LRU_CONV_K, :] * d_xc
        dcw_ref[LRU_CONV_K - 1:LRU_CONV_K, :] += _colsum(d_xc * lx)
        for k in range(LRU_CONV_K - 1):
            d_lx = d_lx + cw_ref[k:k + 1, :] * _shift_up(d_xc, nx, LRU_CONV_K - 1 - k)
            dcw_ref[k:k + 1, :] += _colsum(d_xc * taps[k])
        nxt_dxc[...] = d_xc[0:SUBLANES]
        dz_ref[:, 0:LRU_W] = d_lx.astype(BF16)
        dz_ref[:, LRU_W:2 * LRU_W] = d_gate.astype(BF16)

        u, du = _gelu_and_grad(gu)
        v, vhat, rs, dav = _gmlp_v(gv, vg_ref[...], vb_ref[...])
        mask = _ws_mask()
        sp_parts = []
        for nb in range(nblk):
            rowp = []
            for g in range(N_GROUPS):
                wsm = jnp.where(mask, ws_ref[g], 0.0)
                vblk = v[nb * POS_BLOCK:(nb + 1) * POS_BLOCK, g * LANES:(g + 1) * LANES]
                rowp.append(_dot(wsm, vblk) + bst_ref[:, g:g + 1])
            sp_parts.append(jnp.concatenate(rowp, axis=1))
        sp = jnp.concatenate(sp_parts, axis=0) if nblk > 1 else sp_parts[0]
        y_g = u * sp
        n_g, r_g = _rms(y_g)
        d_ng = dyc_ref[:, LRU_W:]
        vs_ref[7:8, :] += _colsum(d_ng * n_g)
        d_yg = _rms_bwd(d_ng * gg_ref[...], n_g, r_g)
        d_gu = d_yg * sp * du
        d_sp = d_yg * u
        mask_t = _ws_mask(transposed=True)
        ones8 = jnp.ones((SUBLANES, LANES), F32)
        dv_parts = []
        for nb in range(nblk):
            rowp = []
            for g in range(N_GROUPS):
                rs_, cs_ = slice(nb * POS_BLOCK, (nb + 1) * POS_BLOCK), slice(g * LANES, (g + 1) * LANES)
                dsp_blk = d_sp[rs_, cs_]
                dbs_ref[g:g + 1, :] += lax.dot_general(
                    ones8, dsp_blk, (((1,), (1,)), ((), ())), preferred_element_type=F32,
                    precision=lax.Precision.HIGHEST)[0:1, :]
                dws_ref[g] += _dot_nt(dsp_blk, v[rs_, cs_])
                wsm_t = jnp.where(mask_t, wst_ref[g], 0.0)
                rowp.append(_dot(wsm_t, dsp_blk))
            dv_parts.append(jnp.concatenate(rowp, axis=1))
        d_v = jnp.concatenate(dv_parts, axis=0) if nblk > 1 else dv_parts[0]
        vs_ref[4:5, :] += _colsum(d_v * vhat)
        vs_ref[5:6, :] += _colsum(d_v)
        d_vh = d_v * vg_ref[...]
        d_av = rs * (d_vh - jnp.mean(d_vh, axis=-1, keepdims=True)
                     - vhat * jnp.mean(d_vh * vhat, axis=-1, keepdims=True))
        dz_ref[:, 2 * LRU_W:2 * LRU_W + GMLP_W] = d_gu.astype(BF16)
        dz_ref[:, 2 * LRU_W + GMLP_W:] = (d_av * dav).astype(BF16)

        @pl.when(i == nt - 1)
        def _():
            for hd in range(N_HEADS):
                blk = slice(hd * HEAD_DIM, (hd + 1) * HEAD_DIM)
                dwrb_ref[_head_pair_block(hd)] = dwr_ref[blk, blk]
                dwib_ref[_head_pair_block(hd)] = dwi_ref[blk, blk]
            for g in range(N_GROUPS):
                dws_ref[g] = jnp.where(mask, dws_ref[g], 0.0)

    rev = lambda c: pl.BlockSpec((tt, c), lambda i: (nt - 1 - i, 0))
    halo = pl.BlockSpec((SUBLANES, LRU_W), lambda i: (jnp.maximum((nt - 1 - i) * hb - 1, 0), 0))
    v512 = _const((1, LRU_W))
    return _call(
        body, "mix_bwd", (nt,),
        in_specs=[rev(LRU_W + GMLP_W), rev(IN_COLS), halo, rev(LRU_W), halo,
                  _const((LRU_CONV_K, LRU_W)), v512, _whole(), _whole(), v512, v512, v512, v512, v512,
                  _whole(), _whole(), _whole(), v512, v512],
        out_specs=[rev(IN_COLS), _const((SUBLANES, LRU_W)), _const((SUBLANES, LRU_W)),
                   _const((LRU_W // 2, 2 * HEAD_DIM)), _const((LRU_W // 2, 2 * HEAD_DIM)),
                   _const((N_GROUPS, POS_BLOCK, POS_BLOCK)), _const((SUBLANES, POS_BLOCK))],
        out_shape=[_sds((s_len, IN_COLS), BF16), _sds((SUBLANES, LRU_W), F32), _sds((SUBLANES, LRU_W), F32),
                   _sds((LRU_W // 2, 2 * HEAD_DIM), F32), _sds((LRU_W // 2, 2 * HEAD_DIM), F32),
                   _sds((N_GROUPS, POS_BLOCK, POS_BLOCK), F32), _sds((SUBLANES, POS_BLOCK), F32)],
        scratch=[pltpu.VMEM((SUBLANES, LRU_W), F32), pltpu.VMEM((SUBLANES, LRU_W), F32),
                 pltpu.VMEM((SUBLANES, LRU_W), F32), pltpu.VMEM((LRU_W, LRU_W), F32), pltpu.VMEM((LRU_W, LRU_W), F32)],
        args=(d_ycat, z, z, hl, hl, conv_w, conv_b, wr_bd, wi_bd, b_r, b_i, lru_a, vn_g, vn_b, w_sp, w_sp_t, b_sp_t,
              g_lru, g_gmlp), carry=carry)


def _in_bwd(d_z, w_in, x, d_x1, g, sc, carry=None):
    s_len = x.shape[0]
    tt = min(TT_BIG, s_len)

    def body(dz_ref, w_ref, x_ref, dx1_ref, g_ref, sc_ref, gx_ref, vs_ref):
        @pl.when(pl.program_id(0) == 0)
        def _():
            vs_ref[...] = jnp.zeros_like(vs_ref)

        d_h = _dot_nt(dz_ref[...], w_ref[...])
        n, r = _rms(x_ref[...])
        vs_ref[0:1, :] += _colsum(d_h)
        vs_ref[1:2, :] += _colsum(d_h * n * g_ref[...])
        d_ng = d_h * (1.0 + sc_ref[...])
        vs_ref[2:3, :] += _colsum(d_ng * n)
        gx_ref[...] = dx1_ref[...] + _rms_bwd(d_ng * g_ref[...], n, r)

    row = lambda c: pl.BlockSpec((tt, c), lambda i: (i, 0))
    vec = _const((1, D_MODEL))
    return _call(
        body, "in_bwd", (s_len // tt,),
        in_specs=[row(IN_COLS), _whole(), row(D_MODEL), row(D_MODEL), vec, vec],
        out_specs=[row(D_MODEL), _const((SUBLANES, D_MODEL))],
        out_shape=[_sds((s_len, D_MODEL), F32), _sds((SUBLANES, D_MODEL), F32)],
        scratch=[], args=(d_z, w_in, x, d_x1, g, sc), carry=carry)


def _wgrad(a, b, tile, name, by_rows=False, carry=None):
    s_len, k_dim = a.shape
    halves = b.ndim == 3
    n_dim = b.shape[-1] * (2 if halves else 1)

    def body(a_ref, b_ref, o_ref, ob_ref):
        out = _dot_tn(a_ref[...], b_ref[0] if halves else b_ref[...])
        o_ref[...] = out
        ob_ref[...] = out.astype(BF16)

    if by_rows:
        steps = k_dim // tile
        a_spec = pl.BlockSpec((s_len, tile), lambda j: (0, j))
        b_spec = pl.BlockSpec((s_len, n_dim), lambda j: (0, 0))
        o_spec = pl.BlockSpec((tile, n_dim), lambda j: (j, 0))
    else:
        steps = n_dim // tile
        a_spec = pl.BlockSpec((s_len, k_dim), lambda j: (0, 0))
        if halves:
            per_half = steps // 2
            b_spec = pl.BlockSpec((1, s_len, tile), lambda j: (j // per_half, 0, j % per_half))
        else:
            b_spec = pl.BlockSpec((s_len, tile), lambda j: (0, j))
        o_spec = pl.BlockSpec((k_dim, tile), lambda j: (0, j))
    return _call(
        body, name, (steps,), in_specs=[a_spec, b_spec], out_specs=[o_spec, o_spec],
        out_shape=[_sds((k_dim, n_dim), F32), _sds((k_dim, n_dim), BF16)],
        scratch=[], args=(a, b), carry=carry)


def _adam_math(w, g, m, v):
    m = ADAM_B1 * m + (1.0 - ADAM_B1) * g
    v = ADAM_B2 * v + (1.0 - ADAM_B2) * (g * g)
    m_hat = m / (1.0 - ADAM_B1 ** ADAM_STEP)
    v_hat = v / (1.0 - ADAM_B2 ** ADAM_STEP)
    delta = -ADAM_LR * (m_hat / (jnp.sqrt(v_hat) + ADAM_EPS) + ADAM_WD * w)
    return delta, m, v


def _row_tile(rows, cols, n_f32_arrays):
    budget = VMEM_LIMIT // 2
    tr = rows
    while tr % 2 == 0 and tr // 2 >= SUBLANES and (tr // 2) % SUBLANES == 0 and tr * cols * 4 * n_f32_arrays * 2 > budget:
        tr //= 2
    return tr


def _adamw_sum(w, g_full, recv, m, v, col_sharded, name):
    _, rows, cols = w.shape
    n_recv = len(recv)
    tr = _row_tile(rows, cols, 10)
    nb = rows // tr

    def body(me_ref, w_ref, g_ref, *rest):
        r_refs = rest[:n_recv]
        m_ref, v_ref, go_ref, d_ref, mo_ref, vo_ref = rest[n_recv:]
        g = g_ref[...]
        for r_ref in r_refs:
            for k in range(r_ref.shape[0]):
                g = g + r_ref[k].astype(F32)
        go_ref[0] = g
        d_ref[0], mo_ref[0], vo_ref[0] = _adam_math(w_ref[0], g, m_ref[0], v_ref[0])

    if col_sharded:
        own = pl.BlockSpec((tr, cols), lambda i, me: (i, me[0]))
    else:
        own = pl.BlockSpec((tr, cols), lambda i, me: (me[0] * nb + i, 0))
    blk = pl.BlockSpec((1, tr, cols), lambda i, me: (0, i, 0))
    return pl.pallas_call(
        body, name=name,
        grid_spec=pltpu.PrefetchScalarGridSpec(
            num_scalar_prefetch=1, grid=(nb,),
            in_specs=[blk, own] + [pl.BlockSpec((r.shape[0], tr, cols), lambda i, me: (0, i, 0)) for r in recv]
            + [blk, blk],
            out_specs=[blk] * 4),
        out_shape=[_sds((1, rows, cols), F32)] * 4,
        compiler_params=_cparams(("arbitrary",)),
    )(jnp.reshape(_dev_index(_my_pos()), (1,)).astype(jnp.int32), w, g_full, *recv, m, v)


def _row_of_each(ref, row):
    cols = ref.shape[1]
    rows = _rows((N_DEV, cols))
    out = jnp.zeros((N_DEV, cols), F32)
    for d in range(N_DEV):
        picked = ref[d * SUBLANES + row:d * SUBLANES + row + 1, :]
        out = jnp.where(rows == d, jnp.broadcast_to(picked, (N_DEV, cols)), out)
    return out


def _my_columns(full, width, me):
    out = jnp.zeros(full.shape[:-1] + (width,), F32)
    for d in range(N_DEV):
        out = out + jnp.where(me == d, full[:, d * width:(d + 1) * width], 0.0)
    return out


def _adamw_wada(c_all, vs_in_all, vs_up_all, vs_ffn_all, w, m, v):
    _, rows, cols = w.shape

    def body(c_ref, vi_ref, vu_ref, vf_ref, w_ref, m_ref, v_ref, go_ref, d_ref, mo_ref, vo_ref):
        me = _dev_index(_my_pos())
        cv = _row_of_each(c_ref, 0)
        ca = cv * _sigmoid(cv)
        dmod = jnp.concatenate([_row_of_each(vi_ref, 0), _row_of_each(vi_ref, 1), _row_of_each(vu_ref, 3),
                                _row_of_each(vu_ref, 0), _row_of_each(vu_ref, 1), _row_of_each(vf_ref, 0)], axis=1)
        dm = _my_columns(dmod, cols, me)
        g = lax.dot_general(ca, dm, (((0,), (0,)), ((), ())), preferred_element_type=F32,
                            precision=lax.Precision.HIGHEST)
        go_ref[0] = g
        d_ref[0], mo_ref[0], vo_ref[0] = _adam_math(w_ref[0], g, m_ref[0], v_ref[0])

    return pl.pallas_call(
        body, name="adamw_w_ada", out_shape=[_sds((1, rows, cols), F32)] * 4,
        in_specs=[_whole()] * 7, out_specs=[_whole()] * 4,
        compiler_params=_cparams(),
    )(c_all, vs_in_all, vs_up_all, vs_ffn_all, w, m, v)


def _adamw_small(gathered, reduced, params, conv_params):
    names = list(params) + list(conv_params)
    allp = {**params, **conv_params}
    n_g = len(gathered) + len(reduced)

    def body(*refs):
        g_refs = refs[:n_g]
        p_refs = refs[n_g:n_g + 3 * len(names)]
        o_refs = refs[n_g + 3 * len(names):]
        me = _dev_index(_my_pos())

        def total(ref):
            s = ref[0:SUBLANES, :]
            for d in range(1, N_DEV):
                s = s + ref[d * SUBLANES:(d + 1) * SUBLANES, :]
            return s

        vs_in, vs_up, vs_ffn, loss = [total(r) for r in g_refs[:4]]
        cs, vs_mix, dcw, dwr, dwi, dws, dbs = [r[...] for r in g_refs[4:]]
        o_refs[-1][...] = loss[0:1, 0:1]
        mine = lambda full, width: _my_columns(full, width, me)

        all_ = (slice(None), slice(None))
        heads = lambda row: [((0, slice(h, h + 1), slice(None)), row[:, h * HEAD_DIM:(h + 1) * HEAD_DIM])
                             for h in range(N_HEADS)]
        blocks = lambda pairs: [((0, h), pairs[_head_pair_block(h)]) for h in range(N_HEADS)]
        pieces = {
            "b_ada": [((slice(None), slice(k * D_MODEL, (k + 1) * D_MODEL)), row) for k, row in enumerate(
                (vs_in[0:1], vs_in[1:2], vs_up[3:4], vs_up[0:1], vs_up[1:2], vs_ffn[0:1]))],
            "g_mix_pre": [(all_, vs_in[2:3])], "g_mix_post": [(all_, vs_up[4:5])],
            "g_ffn_pre": [(all_, vs_up[2:3])], "g_ffn_post": [(all_, vs_ffn[1:2])],
            "conv_b": [(all_, vs_mix[0:1])], "b_rgate": heads(vs_mix[1:2]), "b_igate": heads(vs_mix[2:3]),
            "lru_a": [(all_, vs_mix[3:4])], "v_norm_g": [(all_, vs_mix[4:5])], "v_norm_b": [(all_, vs_mix[5:6])],
            "g_lru_out": [(all_, vs_mix[6:7])], "g_gmlp_out": [(all_, vs_mix[7:8])],
            "w_rgate": blocks(dwr), "w_igate": blocks(dwi),
            "w_spatial": [((0, g), dws[g * POS_BLOCK:(g + 1) * POS_BLOCK, :]) for g in range(N_GROUPS)],
            "b_spatial": [((0,), dbs[0:N_GROUPS])],
            "ffn_conv_b": [(all_, cs[FFN_CONV_K:FFN_CONV_K + 1])],
            "conv_w": [((0,), mine(dcw[0:LRU_CONV_K], LRU_W // N_DEV))],
            "ffn_conv_w": [((0,), mine(cs[0:FFN_CONV_K], 2 * D_FF // N_DEV))],
        }
        for n_i, name in enumerate(names):
            w_ref, m_ref, v_ref = p_refs[3 * n_i:3 * n_i + 3]
            go_ref, d_ref, mo_ref, vo_ref = o_refs[4 * n_i:4 * n_i + 4]
            for idx, g in pieces[name]:
                go_ref[idx] = g
                d_ref[idx], mo_ref[idx], vo_ref[idx] = _adam_math(w_ref[idx], g, m_ref[idx], v_ref[idx])

    flat_params = [a for n in names for a in allp[n]]
    out_shape = [_sds(allp[n][0].shape, F32) for n in names for _ in range(4)] + [_sds((1, 1), F32)]
    outs = pl.pallas_call(
        body, name="adamw_small", out_shape=out_shape,
        in_specs=[_whole()] * (n_g + len(flat_params)), out_specs=[_whole()] * len(out_shape),
        compiler_params=_cparams(),
    )(*gathered, *reduced, *flat_params)
    return {n: outs[4 * i:4 * i + 4] for i, n in enumerate(names)}, outs[-1]


def _my_pos():
    return lax.axis_index("x"), lax.axis_index("y"), lax.axis_index("c")


def _flip(pos, k):
    x, y, c = pos
    return (1 - x if k & 4 else x, 1 - y if k & 2 else y, 1 - c if k & 1 else c)


def _dev_index(pos):
    x, y, c = pos
    return 4 * x + 2 * y + c


def _all_gather_small(ins, outs, send_sems, recv_sems):
    n = len(ins)
    me = _my_pos()

    def slot(a, pos):
        rows = ins[a].shape[0]
        return outs[a].at[pl.ds(pl.multiple_of(_dev_index(pos) * rows, SUBLANES), rows), :]

    def copy(a, k, block):
        return pltpu.make_async_remote_copy(
            src_ref=ins[a], dst_ref=slot(a, block), send_sem=send_sems.at[a, k - 1], recv_sem=recv_sems.at[a, k - 1],
            device_id=_flip(me, k), device_id_type=MESH)

    sends = [copy(a, k, me) for a in range(n) for k in range(1, N_DEV)]
    for cp in sends:
        cp.start()
    for a in range(n):
        rows = ins[a].shape[0]
        outs[a][pl.ds(pl.multiple_of(_dev_index(me) * rows, SUBLANES), rows), :] = ins[a][...]
    for a in range(n):
        for k in range(1, N_DEV):
            copy(a, k, _flip(me, k)).wait_recv()
    for cp in sends:
        cp.wait_send()


def _prologue(c8, cw8, fcw8, w_ada, b_ada, carry):
    cols = w_ada.shape[1]

    def body(c_ref, cw_ref, fcw_ref, w_ref, b_ref, call_ref, cwall_ref, fcwall_ref, modall_ref, mod_scr,
             s1, r1, s2, r2):
        _all_gather_small([c_ref, cw_ref, fcw_ref], [call_ref, cwall_ref, fcwall_ref], s1, r1)
        cv = _row_of_each(call_ref, 0)
        ca = cv * _sigmoid(cv)
        b_cols = _my_columns(b_ref[...], cols, _dev_index(_my_pos()))
        mod_scr[...] = jnp.dot(ca, w_ref[...], preferred_element_type=F32, precision=lax.Precision.HIGHEST) + b_cols
        _all_gather_small([mod_scr], [modall_ref], s2, r2)

    sem = lambda n: pltpu.SemaphoreType.DMA((n, N_DEV - 1))
    return _call(
        body, "prologue", (1,), in_specs=[_whole()] * 5, out_specs=[_whole()] * 4,
        out_shape=[_sds((N_DEV * SUBLANES, a.shape[1]), F32) for a in (c8, cw8, fcw8)]
        + [_sds((N_DEV * N_DEV, cols), F32)],
        scratch=[pltpu.VMEM((N_DEV, cols), F32), sem(3), sem(3), sem(1), sem(1)],
        args=(c8, cw8, fcw8, w_ada, b_ada), carry=carry)


def _reduce_small(gath, red, carry=None):
    n_g, n_r = len(gath), len(red)
    chip_flips = (4, 2, 6)

    def body(*refs):
        g_in, r_in = refs[:n_g], refs[n_g:n_g + n_r]
        g_out, r_out = refs[n_g + n_r:2 * n_g + n_r], refs[2 * n_g + n_r:2 * (n_g + n_r)]
        scr = refs[2 * (n_g + n_r):]
        sib, land = scr[:n_r], scr[n_r:2 * n_r]
        g_send, g_recv, s_send, s_recv, i_send, i_recv, f_send, f_recv = scr[2 * n_r:]
        me = _my_pos()
        c = me[2]
        sibling = _flip(me, 1)

        def slot(a, pos):
            return g_out[a].at[pl.ds(pl.multiple_of(_dev_index(pos) * SUBLANES, SUBLANES), SUBLANES), :]

        def gcopy(a, k):
            return pltpu.make_async_remote_copy(
                src_ref=g_in[a], dst_ref=slot(a, me), send_sem=g_send.at[a, k - 1], recv_sem=g_recv.at[a, k - 1],
                device_id=_flip(me, k), device_id_type=MESH)

        def scopy(a):
            return pltpu.make_async_remote_copy(
                src_ref=r_in[a], dst_ref=sib[a], send_sem=s_send.at[a], recv_sem=s_recv.at[a],
                device_id=sibling, device_id_type=MESH)

        def icopy(a, j):
            return pltpu.make_async_remote_copy(
                src_ref=r_out[a], dst_ref=land[a].at[j], send_sem=i_send.at[a, j], recv_sem=i_recv.at[a, j],
                device_id=_flip(me, chip_flips[j]), device_id_type=MESH)

        def fcopy(a, j):
            return pltpu.make_async_remote_copy(
                src_ref=land[a].at[j], dst_ref=land[a].at[j], send_sem=f_send.at[a, j], recv_sem=f_recv.at[a, j],
                device_id=sibling, device_id_type=MESH)

        gathers = [gcopy(a, k) for a in range(n_g) for k in range(1, N_DEV)]
        swaps = [scopy(a) for a in range(n_r)]
        for cp in gathers + swaps:
            cp.start()
        for a in range(n_g):
            g_out[a][pl.ds(pl.multiple_of(_dev_index(me) * SUBLANES, SUBLANES), SUBLANES), :] = g_in[a][...]
        for a in range(n_r):
            swaps[a].wait_recv()
            r_out[a][...] = r_in[a][...] + sib[a][...]

        for core in range(2):
            mine = [a for a in range(n_r) if a % 2 == core]
            theirs = [a for a in range(n_r) if a % 2 != core]

            @pl.when(c == core)
            def _():
                out = [icopy(a, j) for a in mine for j in range(3)]
                for cp in out:
                    cp.start()
                fwd = []
                for a in mine:
                    for j in range(3):
                        icopy(a, j).wait_recv()
                        cp = fcopy(a, j)
                        cp.start()
                        fwd.append(cp)
                for a in theirs:
                    for j in range(3):
                        fcopy(a, j).wait_recv()
                for cp in out + fwd:
                    cp.wait_send()

        for a in range(n_r):
            r_out[a][...] = (r_out[a][...] + land[a][1]) + (land[a][0] + land[a][2])
        for a in range(n_g):
            for k in range(1, N_DEV):
                pltpu.make_async_remote_copy(
                    src_ref=g_in[a], dst_ref=slot(a, _flip(me, k)), send_sem=g_send.at[a, k - 1],
                    recv_sem=g_recv.at[a, k - 1], device_id=_flip(me, k), device_id_type=MESH).wait_recv()
        for cp in gathers + swaps:
            cp.wait_send()

    shapes = [tuple(a.shape) for a in red]
    outs, carried = _call(
        body, "reduce_small", (1,), in_specs=[_whole()] * (n_g + n_r), out_specs=[_whole()] * (n_g + n_r),
        out_shape=[_sds((N_DEV * SUBLANES, a.shape[1]), F32) for a in gath] + [_sds(s, F32) for s in shapes],
        scratch=[pltpu.VMEM(s, F32) for s in shapes] + [pltpu.VMEM((3,) + s, F32) for s in shapes]
        + [pltpu.SemaphoreType.DMA((n_g, N_DEV - 1)), pltpu.SemaphoreType.DMA((n_g, N_DEV - 1)),
           pltpu.SemaphoreType.DMA((n_r,)), pltpu.SemaphoreType.DMA((n_r,)),
           pltpu.SemaphoreType.DMA((n_r, 3)), pltpu.SemaphoreType.DMA((n_r, 3)),
           pltpu.SemaphoreType.DMA((n_r, 3)), pltpu.SemaphoreType.DMA((n_r, 3))],
        args=tuple(gath) + tuple(red), carry=carry)
    return (outs[:n_g], outs[n_g:]), carried


STACKED = "stacked"


def _region(ref, shard_shape, col_sharded, pos):
    r, cdim = shard_shape
    d = _dev_index(pos)
    if col_sharded == STACKED:
        return ref.at[d]
    if col_sharded:
        return ref.at[:, pl.ds(pl.multiple_of(d * cdim, LANES), cdim)]
    return ref.at[pl.ds(pl.multiple_of(d * r, 2 * SUBLANES), r), :]


def _gather_carry(shards, col_sharded):
    n_w = len(shards)
    shapes = [tuple(s.shape) for s in shards]
    full_shapes = [(N_DEV,) + s if cs == STACKED else (s[0], s[1] * N_DEV) if cs else (s[0] * N_DEV, s[1])
                   for s, cs in zip(shapes, col_sharded)]

    def tools(out_refs, scr):
        send_sems, recv_sems = scr[n_w], scr[n_w + 1]
        me = _my_pos()
        x, y, c = me
        sibling = (x, y, 1 - c)
        chips = [(1 - x, y), (x, 1 - y), (1 - x, 1 - y)]

        def region(w, pos):
            return _region(out_refs[w], shapes[w], col_sharded[w], pos)

        def copy(w, k, block, to, src=None):
            return pltpu.make_async_remote_copy(
                src_ref=region(w, block) if src is None else src, dst_ref=region(w, block),
                send_sem=send_sems.at[w, k], recv_sem=recv_sems.at[w, k], device_id=to, device_id_type=MESH)

        def first(w):
            return [copy(w, 0, me, sibling, src=scr[w])] + [
                copy(w, 1 + j, me, (*chip, c), src=scr[w]) for j, chip in enumerate(chips)]

        def mine(w):
            return pltpu.make_async_copy(scr[w], region(w, me), scr[n_w + 2].at[w])

        return me, c, sibling, chips, copy, first, mine

    def start(ins, outs, scr):
        _, _, _, _, _, first, mine = tools(outs, scr)
        for w in range(n_w):
            scr[w][...] = ins[w][...].astype(BF16)
            for cp in first(w) + [mine(w)]:
                cp.start()

    def finish(ins, outs, scr):
        me, c, sibling, chips, copy, first, mine = tools(outs, scr)
        passed = []
        for w in range(n_w):
            for j, chip in enumerate(chips):
                copy(w, 1 + j, (*chip, c), me).wait_recv()
                fwd = copy(w, 4 + j, (*chip, c), sibling)
                fwd.start()
                passed.append(fwd)
        for w in range(n_w):
            copy(w, 0, sibling, me).wait_recv()
            for j, chip in enumerate(chips):
                copy(w, 4 + j, (*chip, 1 - c), me).wait_recv()
        for w in range(n_w):
            for cp in first(w):
                cp.wait_send()
            mine(w).wait()
        for cp in passed:
            cp.wait_send()

    return _Carry(
        inputs=list(shards), in_specs=[_whole()] * n_w,
        out_shape=[_sds(s, BF16) for s in full_shapes], out_specs=[_any()] * n_w,
        scratch=[pltpu.VMEM(s, BF16) for s in shapes]
        + [pltpu.SemaphoreType.DMA((n_w, N_DEV - 1)), pltpu.SemaphoreType.DMA((n_w, N_DEV - 1)),
           pltpu.SemaphoreType.DMA((n_w,))],
        start=start, finish=finish)


def _scatter_carry(grads_bf, shard_shapes, col_sharded, relations):
    n_w = len(grads_bf)
    shapes = [tuple(s) for s in shard_shapes]

    def copies(ins, outs, scr):
        send_sems, recv_sems = scr
        me = _my_pos()
        out = []
        for w in range(n_w):
            for i, k in enumerate(relations[w]):
                peer = _flip(me, k)
                out.append(pltpu.make_async_remote_copy(
                    src_ref=_region(ins[w], shapes[w], col_sharded[w], peer), dst_ref=outs[w].at[i],
                    send_sem=send_sems.at[w, i], recv_sem=recv_sems.at[w, i],
                    device_id=peer, device_id_type=MESH))
        return out

    def start(ins, outs, scr):
        for cp in copies(ins, outs, scr):
            cp.start()

    def finish(ins, outs, scr):
        cps = copies(ins, outs, scr)
        for cp in cps:
            cp.wait_recv()
        for cp in cps:
            cp.wait_send()

    return _Carry(
        inputs=list(grads_bf), in_specs=[_any()] * n_w,
        out_shape=[_sds((len(r),) + s, BF16) for r, s in zip(relations, shapes)], out_specs=[_any()] * n_w,
        scratch=[pltpu.SemaphoreType.DMA((n_w, N_DEV - 1)), pltpu.SemaphoreType.DMA((n_w, N_DEV - 1))],
        start=start, finish=finish)


def _block_diag(w):
    eye = jnp.eye(N_HEADS, dtype=w.dtype)
    return (eye[:, None, :, None] * w[:, :, None, :]).reshape(N_HEADS * HEAD_DIM, N_HEADS * HEAD_DIM)


def _pad_rows(a):
    return jnp.pad(a, ((0, SUBLANES - a.shape[0]), (0, 0)))


def _columns_from_devices(gathered, rows):
    w = gathered.shape[1]
    return gathered.reshape(N_DEV, SUBLANES, w)[:, :rows].transpose(1, 0, 2).reshape(rows, N_DEV * w)


def _local_step(x2, target, mod, w_in_f, w_full, conv_w_full, ffn_cw_full,
                g_mix_pre, g_mix_post, conv_b, w_rgate, b_rgate, w_igate, b_igate, lru_a, v_norm_g, v_norm_b,
                w_spatial, b_spatial, g_lru_out, g_gmlp_out, g_ffn_pre, g_ffn_post, ffn_conv_b,
                gather=None, scatter=None):
    sh_m, sc_m, gt_m, sh_f, sc_f, gt_f = [mod[k] for k in range(N_MOD)]
    wr_bd = _block_diag(w_rgate[0]).astype(BF16)
    wi_bd = _block_diag(w_igate[0]).astype(BF16)
    b_r = b_rgate.reshape(1, LRU_W)
    b_i = b_igate.reshape(1, LRU_W)
    b_sp_t = b_spatial[0].T
    w_sp_t = jnp.swapaxes(w_spatial[0], 1, 2)

    def arriving(*names):
        return gather(*names) if gather else None

    near, far = (1, 2, 3, 4, 5), (6, 7)

    def leaving(*parts):
        return scatter(parts) if scatter else None

    def received(recv, parts, outs):
        for (name, _, _), out in zip(parts, outs):
            recv.setdefault(name, []).append(out)

    mix_params = (conv_w_full, conv_b, wr_bd, wi_bd, b_r, b_i, lru_a, v_norm_g, v_norm_b)
    w_out_f = w_full["w_out"]
    (z, h, ycat, hl, y, x1, h2), got = _mix_fwd(
        x2, sh_m, sc_m, g_mix_pre, w_in_f, *mix_params, w_spatial[0], b_sp_t, g_lru_out, g_gmlp_out,
        w_out_f, g_mix_post, gt_m, g_ffn_pre, sc_f, sh_f, carry=arriving("w_up"))
    w_up_f = got[0] if gather else w_full["w_up"]
    (up_pre, up, act), got = _ffn_fwd(h2, w_up_f, ffn_cw_full, ffn_conv_b, carry=arriving("w_down"))
    w_down_f = got[0] if gather else w_full["w_down"]
    d_y2, dout, loss_acc, vs_ffn = _ffn_tail(act, w_down_f, x1, gt_f, g_ffn_post, target)

    recv = {}
    gw_down, _ = _wgrad(act, d_y2, FF_CHUNK_W, "wgrad_down", by_rows=True)
    parts = [("w_down", gw_down[1], near + far)]
    (d_up, cs_ffn), got = _ffn_bwd(d_y2, up_pre, up, ffn_cw_full, w_down_f, carry=leaving(*parts))
    received(recv, parts, got)
    gw_up, _ = _wgrad(h2, d_up, FF_CHUNK_W, "wgrad_up")
    parts = [("w_up", gw_up[1], near)]
    (d_x1, d_y, d_ycat, vs_up), got = _up_bwd(
        d_up, w_up_f, x1, dout, y, w_out_f, g_ffn_pre, sc_f, g_mix_post, gt_m, carry=leaving(*parts))
    received(recv, parts, got)
    gw_out, _ = _wgrad(ycat, d_y, D_MODEL // 4, "wgrad_out")
    parts = [("w_up", gw_up[1], far), ("w_out", gw_out[1], near + far)]
    (d_z, vs_mix, dcw, d_wr, d_wi, d_ws, d_bs), got = _mix_bwd(
        d_ycat, z, hl, *mix_params, w_spatial[0], w_sp_t, b_sp_t, g_lru_out, g_gmlp_out, carry=leaving(*parts))
    received(recv, parts, got)
    gw_in, _ = _wgrad(h, d_z, IN_COLS // 4, "wgrad_in")
    (grad_x, vs_in), _ = _in_bwd(d_z, w_in_f, x2, d_x1, g_mix_pre, sc_m)
    pending = [("w_in", gw_in[1], near + far)]
    recv["w_in"] = []

    gath = [vs_in, vs_up, vs_ffn, loss_acc]
    red = [cs_ffn, vs_mix, dcw, d_wr, d_wi, d_ws.reshape(N_GROUPS * POS_BLOCK, POS_BLOCK), d_bs]
    return dict(grad_x=grad_x, gath=gath, red=red, recv=recv, pending=pending,
                w_in=gw_in, w_out=gw_out, w_up=gw_up, w_down=gw_down)


def kernel(x, c, w_ada, b_ada, g_mix_pre, g_mix_post, w_in, conv_w, conv_b, w_rgate, b_rgate, w_igate, b_igate, lru_a, v_norm_g, v_norm_b, w_spatial, b_spatial, g_lru_out, g_gmlp_out, w_out, g_ffn_pre, g_ffn_post, w_up, ffn_conv_w, ffn_conv_b, w_down, loss_target, m_w_ada, m_b_ada, m_g_mix_pre, m_g_mix_post, m_w_in, m_conv_w, m_conv_b, m_w_rgate, m_b_rgate, m_w_igate, m_b_igate, m_lru_a, m_v_norm_g, m_v_norm_b, m_w_spatial, m_b_spatial, m_g_lru_out, m_g_gmlp_out, m_w_out, m_g_ffn_pre, m_g_ffn_post, m_w_up, m_ffn_conv_w, m_ffn_conv_b, m_w_down, v_w_ada, v_b_ada, v_g_mix_pre, v_g_mix_post, v_w_in, v_conv_w, v_conv_b, v_w_rgate, v_b_rgate, v_w_igate, v_b_igate, v_lru_a, v_v_norm_g, v_v_norm_b, v_w_spatial, v_b_spatial, v_g_lru_out, v_g_gmlp_out, v_w_out, v_g_ffn_pre, v_g_ffn_post, v_w_up, v_ffn_conv_w, v_ffn_conv_b, v_w_down):
    me = _dev_index(_my_pos())
    ada_cols = w_ada.shape[-1]

    big_w = dict(w_in=(w_in, m_w_in, v_w_in, True), w_out=(w_out, m_w_out, v_w_out, False),
                 w_up=(w_up, m_w_up, v_w_up, True), w_down=(w_down, m_w_down, v_w_down, False))

    def gather(*names):
        return _gather_carry([big_w[n][0][0] for n in names], [STACKED if n == "w_up" else big_w[n][3] for n in names])

    def scatter(parts):
        return _scatter_carry([g for _, g, _ in parts], [big_w[n][0].shape[1:] for n, _, _ in parts],
                              [big_w[n][3] for n, _, _ in parts], [rel for _, _, rel in parts])

    (c_all, cw_all, fcw_all, mod_all), (w_in_f, w_out_f) = _prologue(
        jnp.broadcast_to(c, (SUBLANES, D_MODEL)), _pad_rows(conv_w[0]), _pad_rows(ffn_conv_w[0]), w_ada[0], b_ada,
        carry=gather("w_in", "w_out"))
    conv_w_full = _columns_from_devices(cw_all, LRU_CONV_K)
    ffn_cw_full = _columns_from_devices(fcw_all, FFN_CONV_K)
    mod = lax.dynamic_index_in_dim(mod_all.reshape(N_DEV, N_DEV, ada_cols), me, axis=1, keepdims=False)
    mod = mod.reshape(N_MOD, 1, D_MODEL)

    loc = _local_step(x[0], loss_target[0], mod, w_in_f, dict(w_out=w_out_f), conv_w_full, ffn_cw_full,
                      g_mix_pre, g_mix_post, conv_b, w_rgate, b_rgate, w_igate, b_igate, lru_a, v_norm_g, v_norm_b,
                      w_spatial, b_spatial, g_lru_out, g_gmlp_out, g_ffn_pre, g_ffn_post, ffn_conv_b,
                      gather=gather, scatter=scatter)
    grad_x = loc["grad_x"]

    (gathered, reduced), got = _reduce_small(loc["gath"], loc["red"], carry=scatter(loc["pending"]))
    for (name, _, _), out in zip(loc["pending"], got):
        loc["recv"][name].append(out)

    results = {}
    for name, (w_, m_, v_, cs) in big_w.items():
        results[name] = _adamw_sum(w_, loc[name][0], loc["recv"][name], m_, v_, cs, "adamw_" + name)

    params = dict(
        b_ada=(b_ada, m_b_ada, v_b_ada), g_mix_pre=(g_mix_pre, m_g_mix_pre, v_g_mix_pre),
        g_mix_post=(g_mix_post, m_g_mix_post, v_g_mix_post), conv_b=(conv_b, m_conv_b, v_conv_b),
        w_rgate=(w_rgate, m_w_rgate, v_w_rgate), b_rgate=(b_rgate, m_b_rgate, v_b_rgate),
        w_igate=(w_igate, m_w_igate, v_w_igate), b_igate=(b_igate, m_b_igate, v_b_igate),
        lru_a=(lru_a, m_lru_a, v_lru_a), v_norm_g=(v_norm_g, m_v_norm_g, v_v_norm_g),
        v_norm_b=(v_norm_b, m_v_norm_b, v_v_norm_b), w_spatial=(w_spatial, m_w_spatial, v_w_spatial),
        b_spatial=(b_spatial, m_b_spatial, v_b_spatial), g_lru_out=(g_lru_out, m_g_lru_out, v_g_lru_out),
        g_gmlp_out=(g_gmlp_out, m_g_gmlp_out, v_g_gmlp_out), g_ffn_pre=(g_ffn_pre, m_g_ffn_pre, v_g_ffn_pre),
        g_ffn_post=(g_ffn_post, m_g_ffn_post, v_g_ffn_post), ffn_conv_b=(ffn_conv_b, m_ffn_conv_b, v_ffn_conv_b))
    conv_params = dict(conv_w=(conv_w, m_conv_w, v_conv_w), ffn_conv_w=(ffn_conv_w, m_ffn_conv_w, v_ffn_conv_w))
    small_results, loss = _adamw_small(gathered, reduced, params, conv_params)
    results.update(small_results)
    loss = loss.reshape(())

    results["w_ada"] = _adamw_wada(c_all, gathered[0], gathered[1], gathered[2], w_ada, m_w_ada, v_w_ada)

    order = ["w_ada", "b_ada", "g_mix_pre", "g_mix_post", "w_in", "conv_w", "conv_b", "w_rgate", "b_rgate", "w_igate",
             "b_igate", "lru_a", "v_norm_g", "v_norm_b", "w_spatial", "b_spatial", "g_lru_out", "g_gmlp_out", "w_out",
             "g_ffn_pre", "g_ffn_post", "w_up", "ffn_conv_w", "ffn_conv_b", "w_down"]
    outs = [loss, grad_x[None]]
    for kind in range(4):
        outs += [results[n][kind] for n in order]
    return tuple(outs)
```

```python
import functools

import jax
import jax.numpy as jnp
from jax import lax
from jax.experimental import pallas as pl
from jax.experimental.pallas import tpu as pltpu

F32 = jnp.float32
BF16 = jnp.bfloat16

D_MODEL = 1024
LRU_W = 512
GMLP_W = 512
N_HEADS = 8
HEAD_DIM = 64
N_GROUPS = 4
POS_BLOCK = 128
CHUNK = 64
IN_COLS = 2048
D_FF = 3072
N_MOD = 6
N_DEV = 8
EPS = 1e-6
LRU_C = 8.0
LRU_CONV_K = 4
FFN_CONV_K = 3

ADAM_LR = 0.001
ADAM_B1 = 0.9
ADAM_B2 = 0.999
ADAM_EPS = 1e-08
ADAM_WD = 0.01
ADAM_STEP = 10

LANES = 128
SUBLANES = 8
TT_BIG = 512
TT_MIX = 256
FF_CW = 512
VMEM_LIMIT = 56 * 1024 * 1024

MESH = pl.DeviceIdType.MESH


def _sds(shape, dtype):
    return jax.ShapeDtypeStruct(shape, dtype)


def _cparams(sem=None):
    return pltpu.CompilerParams(dimension_semantics=sem, vmem_limit_bytes=VMEM_LIMIT)


def _whole():
    return pl.BlockSpec(memory_space=pltpu.VMEM)


def _const(shape):
    nd = len(shape)
    return pl.BlockSpec(shape, lambda *_: (0,) * nd)


def _any():
    return pl.BlockSpec(memory_space=pl.ANY)


class _Carry:
    def __init__(self, inputs, in_specs, out_shape, out_specs, scratch, start, finish):
        self.inputs, self.in_specs, self.out_shape, self.out_specs = inputs, in_specs, out_shape, out_specs
        self.scratch, self.start, self.finish = scratch, start, finish


def _call(body, name, grid, in_specs, out_specs, out_shape, scratch, args, carry=None, body_starts_carry=False):
    n_in, n_out, n_scr = len(in_specs), len(out_specs), len(scratch)
    c_in = len(carry.in_specs) if carry else 0
    c_out = len(carry.out_specs) if carry else 0

    def full_body(*refs):
        ins = refs[:n_in]
        c_ins = refs[n_in:n_in + c_in]
        outs = refs[n_in + c_in:n_in + c_in + n_out]
        c_outs = refs[n_in + c_in + n_out:n_in + c_in + n_out + c_out]
        scr = refs[n_in + c_in + n_out + c_out:n_in + c_in + n_out + c_out + n_scr]
        c_scr = refs[n_in + c_in + n_out + c_out + n_scr:]
        if carry:
            first = functools.reduce(lambda a, b: a & b, [pl.program_id(d) == 0 for d in range(len(grid))])
            last = functools.reduce(lambda a, b: a & b, [pl.program_id(d) == g - 1 for d, g in enumerate(grid)])

        if carry and not body_starts_carry:
            @pl.when(first)
            def _():
                carry.start(c_ins, c_outs, c_scr)

        if body_starts_carry:
            body(*ins, *outs, *scr, start_carry=(lambda: carry.start(c_ins, c_outs, c_scr)) if carry else (lambda: None))
        else:
            body(*ins, *outs, *scr)
        if carry:
            @pl.when(last)
            def _():
                carry.finish(c_ins, c_outs, c_scr)

    res = pl.pallas_call(
        full_body, name=name, grid=grid,
        in_specs=list(in_specs) + (list(carry.in_specs) if carry else []),
        out_specs=list(out_specs) + (list(carry.out_specs) if carry else []),
        out_shape=list(out_shape) + (list(carry.out_shape) if carry else []),
        scratch_shapes=list(scratch) + (list(carry.scratch) if carry else []),
        compiler_params=_cparams(("arbitrary",) * len(grid)),
    )(*args, *(carry.inputs if carry else []))
    return res[:n_out], res[n_out:]


GELU_C0 = 0.7978845608028654
GELU_C1 = GELU_C0 * 0.044715


def _gelu(x):
    t = jnp.tanh(x * (GELU_C0 + GELU_C1 * (x * x)))
    hx = 0.5 * x
    return hx + hx * t


def _gelu_and_grad(x):
    x2 = x * x
    t = jnp.tanh(x * (GELU_C0 + GELU_C1 * x2))
    hx = 0.5 * x
    g = hx + hx * t
    dg = (0.5 + 0.5 * t) + hx * (1.0 - t * t) * (GELU_C0 + 3.0 * GELU_C1 * x2)
    return g, dg


def _sigmoid(x):
    return 1.0 / (1.0 + jnp.exp(-x))


def _softplus(x):
    return jnp.maximum(x, 0.0) + jnp.log1p(jnp.exp(-jnp.abs(x)))


def _neg_expm1(x):
    series = -x * (1.0 + x * (0.5 + x * (1.0 / 6.0 + x * (1.0 / 24.0 + x * (1.0 / 120.0)))))
    return jnp.where(x > -0.1, series, 1.0 - jnp.exp(x))


def _dot(a, b):
    return jnp.dot(a.astype(BF16), b.astype(BF16), preferred_element_type=F32)


def _dot_nt(a, b):
    return lax.dot_general(a.astype(BF16), b.astype(BF16), (((1,), (1,)), ((), ())), preferred_element_type=F32)


def _dot_tn(a, b):
    return lax.dot_general(a.astype(BF16), b.astype(BF16), (((0,), (0,)), ((), ())), preferred_element_type=F32)


def _rows(shape):
    return lax.broadcasted_iota(jnp.int32, shape, 0)


def _shift_down(cur, prev8, s):
    if s == 0:
        return cur
    n = cur.shape[0]
    r = pltpu.roll(cur, s, 0)
    p = pltpu.roll(prev8, s, 0)
    top = jnp.where(_rows(p.shape) < s, p, r[0:SUBLANES])
    if n == SUBLANES:
        return top
    return jnp.concatenate([top, r[SUBLANES:]], axis=0)


def _shift_up(cur, next8, s):
    if s == 0:
        return cur
    n = cur.shape[0]
    r = pltpu.roll(cur, n - s, 0)
    q = pltpu.roll(next8, SUBLANES - s, 0)
    bot = jnp.where(_rows(q.shape) >= SUBLANES - s, q, r[n - SUBLANES:])
    if n == SUBLANES:
        return bot
    return jnp.concatenate([r[:n - SUBLANES], bot], axis=0)


def _scan_fwd(a, b):
    n = a.shape[0]
    rows = _rows(a.shape)
    s = 1
    while s < n:
        a_s = pltpu.roll(a, s, 0)
        b_s = pltpu.roll(b, s, 0)
        m = rows >= s
        b = jnp.where(m, a * b_s + b, b)
        a = jnp.where(m, a * a_s, a)
        s *= 2
    return a, b


def _scan_rev(a, b):
    n = a.shape[0]
    rows = _rows(a.shape)
    s = 1
    while s < n:
        a_s = pltpu.roll(a, n - s, 0)
        b_s = pltpu.roll(b, n - s, 0)
        m = rows < n - s
        b = jnp.where(m, b + a * b_s, b)
        a = jnp.where(m, a * a_s, a)
        s *= 2
    return a, b


def _rms(x):
    r = lax.rsqrt(jnp.mean(x * x, axis=-1, keepdims=True) + EPS)
    return x * r, r


def _rms_bwd(d_n, n, r):
    return r * (d_n - n * jnp.mean(d_n * n, axis=-1, keepdims=True))


def _colsum(x):
    return jnp.sum(x, axis=0, keepdims=True)


def _lru_gates(xc, wr_ref, wi_ref, br, bi, sp_a):
    r = _sigmoid(_dot(xc, wr_ref[...]) + br)
    i = _sigmoid(_dot(xc, wi_ref[...]) + bi)
    la = -LRU_C * r * sp_a
    a = jnp.exp(la)
    mult = jnp.sqrt(_neg_expm1(2.0 * la))
    return r, i, a, mult


def _lru_conv(lx, prev8, cw_ref, cb):
    xc = cb + cw_ref[LRU_CONV_K - 1:LRU_CONV_K, :] * lx
    taps = []
    for k in range(LRU_CONV_K - 1):
        tap = _shift_down(lx, prev8, LRU_CONV_K - 1 - k)
        taps.append(tap)
        xc = xc + cw_ref[k:k + 1, :] * tap
    return xc, taps


def _ws_mask(transposed=False):
    i = lax.broadcasted_iota(jnp.int32, (POS_BLOCK, POS_BLOCK), 0)
    j = lax.broadcasted_iota(jnp.int32, (POS_BLOCK, POS_BLOCK), 1)
    if transposed:
        i, j = j, i
    return (j // CHUNK) <= (i // CHUNK)


def _gmlp_v(gv, vg, vb):
    av, dav = _gelu_and_grad(gv)
    mu = jnp.mean(av, axis=-1, keepdims=True)
    cen = av - mu
    rs = lax.rsqrt(jnp.mean(cen * cen, axis=-1, keepdims=True) + EPS)
    vhat = cen * rs
    return vhat * vg + vb, vhat, rs, dav


def _mix_fwd(x, sh, sc, g_pre, w_in, conv_w, conv_b, wr_bd, wi_bd, b_r, b_i, lru_a, vn_g, vn_b, w_sp, b_sp_t,
             g_lru, g_gmlp, w_out, g_post, gt_m, g_ffn_pre, sc_f, sh_f, carry=None):
    s_len = x.shape[0]
    tt = min(TT_MIX, s_len)
    nblk = tt // POS_BLOCK

    def body(x_ref, sh_ref, sc_ref, g_ref, w_ref, cw_ref, cb_ref, wr_ref, wi_ref, br_ref, bi_ref, la_ref, vg_ref,
             vb_ref, ws_ref, bst_ref, gl_ref, gg_ref, wo_ref, gp_ref, gtm_ref, g2_ref, scf_ref, shf_ref,
             z_ref, h_ref, y_ref, hl_ref, yo_ref, x1_ref, h2_ref, prev8, hcar):
        i = pl.program_id(0)

        @pl.when(i == 0)
        def _():
            prev8[...] = jnp.zeros_like(prev8)
            hcar[...] = jnp.zeros_like(hcar)

        n_x, _ = _rms(x_ref[...])
        h = (n_x * g_ref[...] * (1.0 + sc_ref[...]) + sh_ref[...]).astype(BF16)
        h_ref[...] = h
        z_ref[...] = jnp.dot(h, w_ref[...], preferred_element_type=F32)

        lx = z_ref[:, 0:LRU_W]
        gate = z_ref[:, LRU_W:2 * LRU_W]
        gu = z_ref[:, 2 * LRU_W:2 * LRU_W + GMLP_W]
        gv = z_ref[:, 2 * LRU_W + GMLP_W:]

        xc, _ = _lru_conv(lx, prev8[...], cw_ref, cb_ref[...])
        prev8[...] = lx[tt - SUBLANES:]
        sp_a = _softplus(-la_ref[...])
        _, ig, a, mult = _lru_gates(xc, wr_ref, wi_ref, br_ref[...], bi_ref[...], sp_a)
        bx = mult * (ig * xc)
        a_cum, b_cum = _scan_fwd(a, bx)
        hl = a_cum * hcar[0:1, :] + b_cum
        hcar[...] = jnp.broadcast_to(hl[tt - 1:tt, :], hcar.shape)
        hl_ref[...] = hl
        y_lru = hl * _gelu(gate)
        n_l, _ = _rms(y_lru)
        y_ref[:, 0:LRU_W] = (n_l * gl_ref[...]).astype(BF16)

        u = _gelu(gu)
        v, _, _, _ = _gmlp_v(gv, vg_ref[...], vb_ref[...])
        mask = _ws_mask()
        sp_parts = []
        for nb in range(nblk):
            row = []
            for g in range(N_GROUPS):
                wsm = jnp.where(mask, ws_ref[g], 0.0)
                vblk = v[nb * POS_BLOCK:(nb + 1) * POS_BLOCK, g * LANES:(g + 1) * LANES]
                row.append(_dot(wsm, vblk) + bst_ref[:, g:g + 1])
            sp_parts.append(jnp.concatenate(row, axis=1))
        sp = jnp.concatenate(sp_parts, axis=0) if nblk > 1 else sp_parts[0]
        n_g, _ = _rms(u * sp)
        y_ref[:, LRU_W:] = (n_g * gg_ref[...]).astype(BF16)

        y = jnp.dot(y_ref[...], wo_ref[...], preferred_element_type=F32)
        yo_ref[...] = y
        n_y, _ = _rms(y)
        x1 = x_ref[...] + gtm_ref[...] * (n_y * gp_ref[...])
        x1_ref[...] = x1
        n1, _ = _rms(x1)
        h2_ref[...] = (n1 * g2_ref[...] * (1.0 + scf_ref[...]) + shf_ref[...]).astype(BF16)

    row = lambda c: pl.BlockSpec((tt, c), lambda i: (i, 0))
    v512 = _const((1, LRU_W))
    vec = _const((1, D_MODEL))
    return _call(
        body, "mix_fwd", (s_len // tt,),
        in_specs=[row(D_MODEL), vec, vec, vec, _whole(),
                  _const((LRU_CONV_K, LRU_W)), v512, _whole(), _whole(), v512, v512, v512, v512, v512,
                  _whole(), _whole(), v512, v512, _whole(), vec, vec, vec, vec, vec],
        out_specs=[row(IN_COLS), row(D_MODEL), row(LRU_W + GMLP_W), row(LRU_W), row(D_MODEL), row(D_MODEL),
                   row(D_MODEL)],
        out_shape=[_sds((s_len, IN_COLS), F32), _sds((s_len, D_MODEL), BF16),
                   _sds((s_len, LRU_W + GMLP_W), BF16), _sds((s_len, LRU_W), F32),
                   _sds((s_len, D_MODEL), F32), _sds((s_len, D_MODEL), F32), _sds((s_len, D_MODEL), BF16)],
        scratch=[pltpu.VMEM((SUBLANES, LRU_W), F32), pltpu.VMEM((SUBLANES, LRU_W), F32)],
        args=(x, sh, sc, g_pre, w_in, conv_w, conv_b, wr_bd, wi_bd, b_r, b_i, lru_a, vn_g, vn_b, w_sp, b_sp_t,
              g_lru, g_gmlp, w_out, g_post, gt_m, g_ffn_pre, sc_f, sh_f), carry=carry)


FF_CHUNKS = N_DEV // 2
FF_CHUNK_W = D_FF // FF_CHUNKS


def _ffn_fwd(h2, w_up3, ffn_cw, ffn_cb, carry=None):
    s_len = h2.shape[0]
    tt = min(TT_MIX, s_len)
    nc, cw = FF_CHUNKS, FF_CHUNK_W

    def body(h2_ref, wu_ref, cwg_ref, cwv_ref, cbg_ref, cbv_ref, up_ref, upc_ref, act_ref, prev):
        i = pl.program_id(0)
        c = pl.program_id(1)

        @pl.when(i == 0)
        def _():
            prev[c] = jnp.zeros((2, SUBLANES, cw), F32)

        h2 = h2_ref[...]
        ug_pre = jnp.dot(h2, wu_ref[c], preferred_element_type=F32)
        uv_pre = jnp.dot(h2, wu_ref[nc + c], preferred_element_type=F32)
        up_ref[0] = ug_pre.astype(BF16)
        up_ref[1] = uv_pre.astype(BF16)
        ug, _ = _ffn_conv(ug_pre, prev[c, 0], cwg_ref, cbg_ref[...])
        uv, _ = _ffn_conv(uv_pre, prev[c, 1], cwv_ref, cbv_ref[...])
        prev[c, 0] = ug_pre[tt - SUBLANES:, :]
        prev[c, 1] = uv_pre[tt - SUBLANES:, :]
        upc_ref[0] = ug
        upc_ref[1] = uv
        act_ref[...] = (_gelu(ug) * uv).astype(BF16)

    chunk2 = pl.BlockSpec((2, tt, cw), lambda i, c: (0, i, c))
    ffn_cb2 = ffn_cb.reshape(1, 2 * D_FF)
    return _call(
        body, "ffn_fwd", (s_len // tt, nc),
        in_specs=[pl.BlockSpec((tt, D_MODEL), lambda i, c: (i, 0)), _whole(),
                  pl.BlockSpec((FFN_CONV_K, cw), lambda i, c: (0, c)),
                  pl.BlockSpec((FFN_CONV_K, cw), lambda i, c: (0, c + nc)),
                  pl.BlockSpec((1, cw), lambda i, c: (0, c)),
                  pl.BlockSpec((1, cw), lambda i, c: (0, c + nc))],
        out_specs=[chunk2, chunk2, pl.BlockSpec((tt, cw), lambda i, c: (i, c))],
        out_shape=[_sds((2, s_len, D_FF), BF16), _sds((2, s_len, D_FF), F32), _sds((s_len, D_FF), BF16)],
        scratch=[pltpu.VMEM((nc, 2, SUBLANES, cw), F32)],
        args=(h2, w_up3, ffn_cw, ffn_cw, ffn_cb2, ffn_cb2), carry=carry)


def _ffn_tail(act, w_down, x1, gt_f, g_post, target):
    s_len = x1.shape[0]
    tt = min(TT_BIG, s_len)

    def body(act_ref, wd_ref, x1_ref, gtf_ref, gp_ref, tg_ref, dy2_ref, dout_ref, loss_ref, vs_ref):
        @pl.when(pl.program_id(0) == 0)
        def _():
            loss_ref[...] = jnp.zeros_like(loss_ref)
            vs_ref[...] = jnp.zeros_like(vs_ref)

        n2, r2 = _rms(jnp.dot(act_ref[...], wd_ref[...], preferred_element_type=F32))
        out = x1_ref[...] + gtf_ref[...] * (n2 * gp_ref[...])
        err = out - tg_ref[...]
        do = err * (1.0 / D_MODEL)
        dout_ref[...] = do
        loss_ref[...] += jnp.broadcast_to(0.5 * jnp.sum(err * err, keepdims=True) * (1.0 / D_MODEL), loss_ref.shape)
        vs_ref[0:1, :] += _colsum(do * n2 * gp_ref[...])
        vs_ref[1:2, :] += _colsum(do * gtf_ref[...] * n2)
        dy2_ref[...] = _rms_bwd(do * gtf_ref[...] * gp_ref[...], n2, r2).astype(BF16)

    row = lambda c: pl.BlockSpec((tt, c), lambda i: (i, 0))
    vec = _const((1, D_MODEL))
    outs, _ = _call(
        body, "ffn_tail", (s_len // tt,),
        in_specs=[row(D_FF), _whole(), row(D_MODEL), vec, vec, row(D_MODEL)],
        out_specs=[row(D_MODEL), row(D_MODEL), _const((SUBLANES, LANES)), _const((SUBLANES, D_MODEL))],
        out_shape=[_sds((s_len, D_MODEL), BF16), _sds((s_len, D_MODEL), F32), _sds((SUBLANES, LANES), F32),
                   _sds((SUBLANES, D_MODEL), F32)],
        scratch=[], args=(act, w_down, x1, gt_f, g_post, target))
    return outs


def _ffn_conv(up_pre, prev8, cw_ref, cb):
    up = cb + cw_ref[FFN_CONV_K - 1:FFN_CONV_K, :] * up_pre
    taps = []
    for k in range(FFN_CONV_K - 1):
        tap = _shift_down(up_pre, prev8, FFN_CONV_K - 1 - k)
        taps.append(tap)
        up = up + cw_ref[k:k + 1, :] * tap
    return up, taps


def _ffn_bwd(d_y2, up_pre, up, ffn_cw, w_down, carry=None):
    s_len = d_y2.shape[0]
    tt = min(TT_BIG, s_len)
    nt = s_len // tt
    cw = FF_CW
    nc = D_FF // cw

    def body(dy2_ref, up_ref, upc_ref, cwg_ref, cwv_ref, wd_ref, dup_ref, cs_ref, nxt, cs_acc):
        i = pl.program_id(0)
        c = pl.program_id(1)

        @pl.when(i == 0)
        def _():
            nxt[c] = jnp.zeros((2, SUBLANES, cw), F32)
            cs_acc[c] = jnp.zeros((2, SUBLANES, cw), F32)

        pw = cw // 2
        for piece in range(2):
            cols = slice(piece * pw, (piece + 1) * pw)
            d_act = _dot_nt(dy2_ref[...], wd_ref[cols, :])
            uv = upc_ref[1, :, cols]
            gl, dgl = _gelu_and_grad(upc_ref[0, :, cols])
            d_ug = d_act * uv * dgl
            d_uv = d_act * gl
            for half, (d_u, cw_ref) in enumerate(((d_ug, cwg_ref), (d_uv, cwv_ref))):
                nx = nxt[c, half, :, cols]
                x_in = up_ref[half, :, cols].astype(F32)
                d_pre = cw_ref[FFN_CONV_K - 1:FFN_CONV_K, cols] * d_u
                sums = [None] * (FFN_CONV_K + 1)
                sums[FFN_CONV_K - 1] = _colsum(d_u * x_in)
                for k in range(FFN_CONV_K - 1):
                    ahead = _shift_up(d_u, nx, FFN_CONV_K - 1 - k)
                    d_pre = d_pre + cw_ref[k:k + 1, cols] * ahead
                    sums[k] = _colsum(ahead * x_in)
                sums[FFN_CONV_K] = _colsum(d_u)
                pad = jnp.zeros((SUBLANES - FFN_CONV_K - 1, pw), F32)
                cs_acc[c, half, :, cols] += jnp.concatenate(sums + [pad], axis=0)
                nxt[c, half, :, cols] = d_u[0:SUBLANES]
                dup_ref[half, :, cols] = d_pre.astype(BF16)

        for cc in range(nc):
            @pl.when((i == nt - 1) & (c == cc))
            def _():
                cs_ref[:, cc * cw:(cc + 1) * cw] = cs_acc[cc, 0]
                cs_ref[:, D_FF + cc * cw:D_FF + (cc + 1) * cw] = cs_acc[cc, 1]

    row = pl.BlockSpec((tt, D_MODEL), lambda i, c: (nt - 1 - i, 0))
    blk = pl.BlockSpec((2, tt, cw), lambda i, c: (0, nt - 1 - i, c))
    return _call(
        body, "ffn_bwd", (nt, nc),
        in_specs=[row, blk, blk,
                  pl.BlockSpec((FFN_CONV_K, cw), lambda i, c: (0, c)),
                  pl.BlockSpec((FFN_CONV_K, cw), lambda i, c: (0, c + nc)),
                  pl.BlockSpec((cw, D_MODEL), lambda i, c: (c, 0))],
        out_specs=[blk, _const((SUBLANES, 2 * D_FF))],
        out_shape=[_sds((2, s_len, D_FF), BF16), _sds((SUBLANES, 2 * D_FF), F32)],
        scratch=[pltpu.VMEM((nc, 2, SUBLANES, cw), F32), pltpu.VMEM((nc, 2, SUBLANES, cw), F32)],
        args=(d_y2, up_pre, up, ffn_cw, ffn_cw, w_down), carry=carry)


def _up_bwd(d_up, w_up3, x1, dout, y, w_out, g_pre, sc_f, g_post, gt_m, carry=None):
    s_len = x1.shape[0]
    tt = min(TT_BIG, s_len)

    def body(du_ref, wu_ref, x1_ref, do_ref, y_ref, wo_ref, g2_ref, sc_ref, gp_ref, gt_ref,
             dx1_ref, dy_ref, dyc_ref, vs_ref):
        @pl.when(pl.program_id(0) == 0)
        def _():
            vs_ref[...] = jnp.zeros_like(vs_ref)

        d_h2 = jnp.zeros((tt, D_MODEL), F32)
        for half in range(2):
            for ch in range(FF_CHUNKS):
                d_h2 = d_h2 + _dot_nt(du_ref[half, :, ch * FF_CHUNK_W:(ch + 1) * FF_CHUNK_W],
                                      wu_ref[half * FF_CHUNKS + ch])
        n1, r1 = _rms(x1_ref[...])
        ng = n1 * g2_ref[...]
        vs_ref[0:1, :] += _colsum(d_h2)
        vs_ref[1:2, :] += _colsum(d_h2 * ng)
        d_ng = d_h2 * (1.0 + sc_ref[...])
        vs_ref[2:3, :] += _colsum(d_ng * n1)
        d_x1 = do_ref[...] + _rms_bwd(d_ng * g2_ref[...], n1, r1)
        dx1_ref[...] = d_x1
        n_y, r_y = _rms(y_ref[...])
        vs_ref[3:4, :] += _colsum(d_x1 * n_y * gp_ref[...])
        d_on = d_x1 * gt_ref[...]
        vs_ref[4:5, :] += _colsum(d_on * n_y)
        d_y = _rms_bwd(d_on * gp_ref[...], n_y, r_y).astype(BF16)
        dy_ref[...] = d_y
        dyc_ref[...] = _dot_nt(d_y, wo_ref[...])

    row = lambda c: pl.BlockSpec((tt, c), lambda i: (i, 0))
    vec = _const((1, D_MODEL))
    return _call(
        body, "up_bwd", (s_len // tt,),
        in_specs=[pl.BlockSpec((2, tt, D_FF), lambda i: (0, i, 0)), _whole(), row(D_MODEL), row(D_MODEL), row(D_MODEL),
                  _whole(), vec, vec, vec, vec],
        out_specs=[row(D_MODEL), row(D_MODEL), row(LRU_W + GMLP_W), _const((SUBLANES, D_MODEL))],
        out_shape=[_sds((s_len, D_MODEL), F32), _sds((s_len, D_MODEL), BF16), _sds((s_len, LRU_W + GMLP_W), F32),
                   _sds((SUBLANES, D_MODEL), F32)],
        scratch=[], args=(d_up, w_up3, x1, dout, y, w_out, g_pre, sc_f, g_post, gt_m), carry=carry)


def _head_pair_block(hd):
    return (slice((hd // 2) * HEAD_DIM, (hd // 2 + 1) * HEAD_DIM), slice((hd % 2) * HEAD_DIM, (hd % 2 + 1) * HEAD_DIM))


def _mix_bwd(d_ycat, z, hl, conv_w, conv_b, wr_bd, wi_bd, b_r, b_i, lru_a, vn_g, vn_b, w_sp, w_sp_t, b_sp_t,
             g_lru, g_gmlp, carry=None):
    s_len = z.shape[0]
    tt = min(TT_MIX, s_len)
    nt = s_len // tt
    nblk = tt // POS_BLOCK
    hb = tt // SUBLANES

    def body(dyc_ref, z_ref, zh_ref, hl_ref, hh_ref, cw_ref, cb_ref, wr_ref, wi_ref, br_ref, bi_ref, la_ref,
             vg_ref, vb_ref, ws_ref, wst_ref, bst_ref, gl_ref, gg_ref,
             dz_ref, vs_ref, dcw_ref, dwrb_ref, dwib_ref, dws_ref, dbs_ref, nxt_dxc, nxt_a, nxt_lam, dwr_ref, dwi_ref):
        i = pl.program_id(0)
        first_tile = i == nt - 1

        @pl.when(i == 0)
        def _():
            for ref in (vs_ref, dcw_ref, dwr_ref, dwi_ref, dws_ref, dbs_ref, nxt_dxc, nxt_a, nxt_lam):
                ref[...] = jnp.zeros_like(ref)

        lx = z_ref[:, 0:LRU_W]
        gate = z_ref[:, LRU_W:2 * LRU_W]
        gu = z_ref[:, 2 * LRU_W:2 * LRU_W + GMLP_W]
        gv = z_ref[:, 2 * LRU_W + GMLP_W:]
        prev8 = jnp.where(first_tile, 0.0, zh_ref[...])
        hprev8 = jnp.where(first_tile, 0.0, hh_ref[...])

        xc, taps = _lru_conv(lx, prev8, cw_ref, cb_ref[...])
        a_par = la_ref[...]
        sp_a = _softplus(-a_par)
        r, ig, a, mult = _lru_gates(xc, wr_ref, wi_ref, br_ref[...], bi_ref[...], sp_a)
        hl = hl_ref[...]
        h_prev = _shift_down(hl, hprev8, 1)
        ggate, dggate = _gelu_and_grad(gate)
        y_lru = hl * ggate
        n_l, r_l = _rms(y_lru)
        d_nl = dyc_ref[:, 0:LRU_W]
        vs_ref[6:7, :] += _colsum(d_nl * n_l)
        d_yl = _rms_bwd(d_nl * gl_ref[...], n_l, r_l)
        d_hl = d_yl * ggate
        d_gate = d_yl * hl * dggate
        a_up = _shift_up(a, nxt_a[...], 1)
        a_cum, b_cum = _scan_rev(a_up, d_hl)
        lam = b_cum + a_cum * nxt_lam[0:1, :]
        nxt_a[...] = jnp.broadcast_to(a[0:1, :], nxt_a.shape)
        nxt_lam[...] = jnp.broadcast_to(lam[0:1, :], nxt_lam.shape)
        ixc = ig * xc
        d_la = lam * h_prev * a - lam * ixc * (a * a) / mult
        d_i = lam * mult * xc
        d_xc = lam * mult * ig
        vs_ref[3:4, :] += _colsum(d_la * r) * (LRU_C * _sigmoid(-a_par))
        d_pr = d_la * (-LRU_C * sp_a) * r * (1.0 - r)
        d_pi = d_i * ig * (1.0 - ig)
        vs_ref[1:2, :] += _colsum(d_pr)
        vs_ref[2:3, :] += _colsum(d_pi)
        dwr_ref[...] += _dot_tn(xc, d_pr)
        dwi_ref[...] += _dot_tn(xc, d_pi)
        d_xc = d_xc + _dot_nt(d_pr, wr_ref[...]) + _dot_nt(d_pi, wi_ref[...])
        vs_ref[0:1, :] += _colsum(d_xc)
        nx = nxt_dxc[...]
        d_lx = cw_ref[LRU_CONV_K - 1:LRU_CONV_K, :] * d_xc
        dcw_ref[LRU_CONV_K - 1:LRU_CONV_K, :] += _colsum(d_xc * lx)
        for k in range(LRU_CONV_K - 1):
            d_lx = d_lx + cw_ref[k:k + 1, :] * _shift_up(d_xc, nx, LRU_CONV_K - 1 - k)
            dcw_ref[k:k + 1, :] += _colsum(d_xc * taps[k])
        nxt_dxc[...] = d_xc[0:SUBLANES]
        dz_ref[:, 0:LRU_W] = d_lx.astype(BF16)
        dz_ref[:, LRU_W:2 * LRU_W] = d_gate.astype(BF16)

        u, du = _gelu_and_grad(gu)
        v, vhat, rs, dav = _gmlp_v(gv, vg_ref[...], vb_ref[...])
        mask = _ws_mask()
        sp_parts = []
        for nb in range(nblk):
            rowp = []
            for g in range(N_GROUPS):
                wsm = jnp.where(mask, ws_ref[g], 0.0)
                vblk = v[nb * POS_BLOCK:(nb + 1) * POS_BLOCK, g * LANES:(g + 1) * LANES]
                rowp.append(_dot(wsm, vblk) + bst_ref[:, g:g + 1])
            sp_parts.append(jnp.concatenate(rowp, axis=1))
        sp = jnp.concatenate(sp_parts, axis=0) if nblk > 1 else sp_parts[0]
        y_g = u * sp
        n_g, r_g = _rms(y_g)
        d_ng = dyc_ref[:, LRU_W:]
        vs_ref[7:8, :] += _colsum(d_ng * n_g)
        d_yg = _rms_bwd(d_ng * gg_ref[...], n_g, r_g)
        d_gu = d_yg * sp * du
        d_sp = d_yg * u
        mask_t = _ws_mask(transposed=True)
        ones8 = jnp.ones((SUBLANES, LANES), F32)
        dv_parts = []
        for nb in range(nblk):
            rowp = []
            for g in range(N_GROUPS):
                rs_, cs_ = slice(nb * POS_BLOCK, (nb + 1) * POS_BLOCK), slice(g * LANES, (g + 1) * LANES)
                dsp_blk = d_sp[rs_, cs_]
                dbs_ref[g:g + 1, :] += lax.dot_general(
                    ones8, dsp_blk, (((1,), (1,)), ((), ())), preferred_element_type=F32,
                    precision=lax.Precision.HIGHEST)[0:1, :]
                dws_ref[g] += _dot_nt(dsp_blk, v[rs_, cs_])
                wsm_t = jnp.where(mask_t, wst_ref[g], 0.0)
                rowp.append(_dot(wsm_t, dsp_blk))
            dv_parts.append(jnp.concatenate(rowp, axis=1))
        d_v = jnp.concatenate(dv_parts, axis=0) if nblk > 1 else dv_parts[0]
        vs_ref[4:5, :] += _colsum(d_v * vhat)
        vs_ref[5:6, :] += _colsum(d_v)
        d_vh = d_v * vg_ref[...]
        d_av = rs * (d_vh - jnp.mean(d_vh, axis=-1, keepdims=True)
                     - vhat * jnp.mean(d_vh * vhat, axis=-1, keepdims=True))
        dz_ref[:, 2 * LRU_W:2 * LRU_W + GMLP_W] = d_gu.astype(BF16)
        dz_ref[:, 2 * LRU_W + GMLP_W:] = (d_av * dav).astype(BF16)

        @pl.when(i == nt - 1)
        def _():
            for hd in range(N_HEADS):
                blk = slice(hd * HEAD_DIM, (hd + 1) * HEAD_DIM)
                dwrb_ref[_head_pair_block(hd)] = dwr_ref[blk, blk]
                dwib_ref[_head_pair_block(hd)] = dwi_ref[blk, blk]
            for g in range(N_GROUPS):
                dws_ref[g] = jnp.where(mask, dws_ref[g], 0.0)

    rev = lambda c: pl.BlockSpec((tt, c), lambda i: (nt - 1 - i, 0))
    halo = pl.BlockSpec((SUBLANES, LRU_W), lambda i: (jnp.maximum((nt - 1 - i) * hb - 1, 0), 0))
    v512 = _const((1, LRU_W))
    return _call(
        body, "mix_bwd", (nt,),
        in_specs=[rev(LRU_W + GMLP_W), rev(IN_COLS), halo, rev(LRU_W), halo,
                  _const((LRU_CONV_K, LRU_W)), v512, _whole(), _whole(), v512, v512, v512, v512, v512,
                  _whole(), _whole(), _whole(), v512, v512],
        out_specs=[rev(IN_COLS), _const((SUBLANES, LRU_W)), _const((SUBLANES, LRU_W)),
                   _const((LRU_W // 2, 2 * HEAD_DIM)), _const((LRU_W // 2, 2 * HEAD_DIM)),
                   _const((N_GROUPS, POS_BLOCK, POS_BLOCK)), _const((SUBLANES, POS_BLOCK))],
        out_shape=[_sds((s_len, IN_COLS), BF16), _sds((SUBLANES, LRU_W), F32), _sds((SUBLANES, LRU_W), F32),
                   _sds((LRU_W // 2, 2 * HEAD_DIM), F32), _sds((LRU_W // 2, 2 * HEAD_DIM), F32),
                   _sds((N_GROUPS, POS_BLOCK, POS_BLOCK), F32), _sds((SUBLANES, POS_BLOCK), F32)],
        scratch=[pltpu.VMEM((SUBLANES, LRU_W), F32), pltpu.VMEM((SUBLANES, LRU_W), F32),
                 pltpu.VMEM((SUBLANES, LRU_W), F32), pltpu.VMEM((LRU_W, LRU_W), F32), pltpu.VMEM((LRU_W, LRU_W), F32)],
        args=(d_ycat, z, z, hl, hl, conv_w, conv_b, wr_bd, wi_bd, b_r, b_i, lru_a, vn_g, vn_b, w_sp, w_sp_t, b_sp_t,
              g_lru, g_gmlp), carry=carry)


def _in_bwd(d_z, w_in, x, d_x1, g, sc, carry=None):
    s_len = x.shape[0]
    tt = min(TT_BIG, s_len)

    def body(dz_ref, w_ref, x_ref, dx1_ref, g_ref, sc_ref, gx_ref, vs_ref):
        @pl.when(pl.program_id(0) == 0)
        def _():
            vs_ref[...] = jnp.zeros_like(vs_ref)

        d_h = _dot_nt(dz_ref[...], w_ref[...])
        n, r = _rms(x_ref[...])
        vs_ref[0:1, :] += _colsum(d_h)
        vs_ref[1:2, :] += _colsum(d_h * n * g_ref[...])
        d_ng = d_h * (1.0 + sc_ref[...])
        vs_ref[2:3, :] += _colsum(d_ng * n)
        gx_ref[...] = dx1_ref[...] + _rms_bwd(d_ng * g_ref[...], n, r)

    row = lambda c: pl.BlockSpec((tt, c), lambda i: (i, 0))
    vec = _const((1, D_MODEL))
    return _call(
        body, "in_bwd", (s_len // tt,),
        in_specs=[row(IN_COLS), _whole(), row(D_MODEL), row(D_MODEL), vec, vec],
        out_specs=[row(D_MODEL), _const((SUBLANES, D_MODEL))],
        out_shape=[_sds((s_len, D_MODEL), F32), _sds((SUBLANES, D_MODEL), F32)],
        scratch=[], args=(d_z, w_in, x, d_x1, g, sc), carry=carry)


def _wgrad(a, b, tile, name, by_rows=False, carry=None):
    s_len, k_dim = a.shape
    halves = b.ndim == 3
    n_dim = b.shape[-1] * (2 if halves else 1)

    def body(a_ref, b_ref, o_ref, ob_ref):
        out = _dot_tn(a_ref[...], b_ref[0] if halves else b_ref[...])
        o_ref[...] = out
        ob_ref[...] = out.astype(BF16)

    if by_rows:
        steps = k_dim // tile
        a_spec = pl.BlockSpec((s_len, tile), lambda j: (0, j))
        b_spec = pl.BlockSpec((s_len, n_dim), lambda j: (0, 0))
        o_spec = pl.BlockSpec((tile, n_dim), lambda j: (j, 0))
    else:
        steps = n_dim // tile
        a_spec = pl.BlockSpec((s_len, k_dim), lambda j: (0, 0))
        if halves:
            per_half = steps // 2
            b_spec = pl.BlockSpec((1, s_len, tile), lambda j: (j // per_half, 0, j % per_half))
        else:
            b_spec = pl.BlockSpec((s_len, tile), lambda j: (0, j))
        o_spec = pl.BlockSpec((k_dim, tile), lambda j: (0, j))
    return _call(
        body, name, (steps,), in_specs=[a_spec, b_spec], out_specs=[o_spec, o_spec],
        out_shape=[_sds((k_dim, n_dim), F32), _sds((k_dim, n_dim), BF16)],
        scratch=[], args=(a, b), carry=carry)


def _adam_math(w, g, m, v):
    m = ADAM_B1 * m + (1.0 - ADAM_B1) * g
    v = ADAM_B2 * v + (1.0 - ADAM_B2) * (g * g)
    m_hat = m / (1.0 - ADAM_B1 ** ADAM_STEP)
    v_hat = v / (1.0 - ADAM_B2 ** ADAM_STEP)
    delta = -ADAM_LR * (m_hat / (jnp.sqrt(v_hat) + ADAM_EPS) + ADAM_WD * w)
    return delta, m, v


def _row_tile(rows, cols, n_f32_arrays):
    budget = VMEM_LIMIT // 2
    tr = rows
    while tr % 2 == 0 and tr // 2 >= SUBLANES and (tr // 2) % SUBLANES == 0 and tr * cols * 4 * n_f32_arrays * 2 > budget:
        tr //= 2
    return tr


def _adamw_sum(w, g_full, recv, m, v, col_sharded, name):
    _, rows, cols = w.shape
    n_recv = len(recv)
    tr = _row_tile(rows, cols, 10)
    nb = rows // tr

    def body(me_ref, w_ref, g_ref, *rest):
        r_refs = rest[:n_recv]
        m_ref, v_ref, go_ref, d_ref, mo_ref, vo_ref = rest[n_recv:]
        g = g_ref[...]
        for r_ref in r_refs:
            for k in range(r_ref.shape[0]):
                g = g + r_ref[k].astype(F32)
        go_ref[0] = g
        d_ref[0], mo_ref[0], vo_ref[0] = _adam_math(w_ref[0], g, m_ref[0], v_ref[0])

    if col_sharded:
        own = pl.BlockSpec((tr, cols), lambda i, me: (i, me[0]))
    else:
        own = pl.BlockSpec((tr, cols), lambda i, me: (me[0] * nb + i, 0))
    blk = pl.BlockSpec((1, tr, cols), lambda i, me: (0, i, 0))
    return pl.pallas_call(
        body, name=name,
        grid_spec=pltpu.PrefetchScalarGridSpec(
            num_scalar_prefetch=1, grid=(nb,),
            in_specs=[blk, own] + [pl.BlockSpec((r.shape[0], tr, cols), lambda i, me: (0, i, 0)) for r in recv]
            + [blk, blk],
            out_specs=[blk] * 4),
        out_shape=[_sds((1, rows, cols), F32)] * 4,
        compiler_params=_cparams(("arbitrary",)),
    )(jnp.reshape(_dev_index(_my_pos()), (1,)).astype(jnp.int32), w, g_full, *recv, m, v)


def _row_of_each(ref, row):
    cols = ref.shape[1]
    rows = _rows((N_DEV, cols))
    out = jnp.zeros((N_DEV, cols), F32)
    for d in range(N_DEV):
        picked = ref[d * SUBLANES + row:d * SUBLANES + row + 1, :]
        out = jnp.where(rows == d, jnp.broadcast_to(picked, (N_DEV, cols)), out)
    return out


def _my_columns(full, width, me):
    out = jnp.zeros(full.shape[:-1] + (width,), F32)
    for d in range(N_DEV):
        out = out + jnp.where(me == d, full[:, d * width:(d + 1) * width], 0.0)
    return out


def _adamw_wada(c_all, vs_in_all, vs_up_all, vs_ffn_all, w, m, v):
    _, rows, cols = w.shape

    def body(c_ref, vi_ref, vu_ref, vf_ref, w_ref, m_ref, v_ref, go_ref, d_ref, mo_ref, vo_ref):
        me = _dev_index(_my_pos())
        cv = _row_of_each(c_ref, 0)
        ca = cv * _sigmoid(cv)
        dmod = jnp.concatenate([_row_of_each(vi_ref, 0), _row_of_each(vi_ref, 1), _row_of_each(vu_ref, 3),
                                _row_of_each(vu_ref, 0), _row_of_each(vu_ref, 1), _row_of_each(vf_ref, 0)], axis=1)
        dm = _my_columns(dmod, cols, me)
        g = lax.dot_general(ca, dm, (((0,), (0,)), ((), ())), preferred_element_type=F32,
                            precision=lax.Precision.HIGHEST)
        go_ref[0] = g
        d_ref[0], mo_ref[0], vo_ref[0] = _adam_math(w_ref[0], g, m_ref[0], v_ref[0])

    return pl.pallas_call(
        body, name="adamw_w_ada", out_shape=[_sds((1, rows, cols), F32)] * 4,
        in_specs=[_whole()] * 7, out_specs=[_whole()] * 4,
        compiler_params=_cparams(),
    )(c_all, vs_in_all, vs_up_all, vs_ffn_all, w, m, v)


def _adamw_small(gathered, reduced, params, conv_params):
    names = list(params) + list(conv_params)
    allp = {**params, **conv_params}
    n_g = len(gathered) + len(reduced)

    def body(*refs):
        g_refs = refs[:n_g]
        p_refs = refs[n_g:n_g + 3 * len(names)]
        o_refs = refs[n_g + 3 * len(names):]
        me = _dev_index(_my_pos())

        def total(ref):
            s = ref[0:SUBLANES, :]
            for d in range(1, N_DEV):
                s = s + ref[d * SUBLANES:(d + 1) * SUBLANES, :]
            return s

        vs_in, vs_up, vs_ffn, loss = [total(r) for r in g_refs[:4]]
        cs, vs_mix, dcw, dwr, dwi, dws, dbs = [r[...] for r in g_refs[4:]]
        o_refs[-1][...] = loss[0:1, 0:1]
        mine = lambda full, width: _my_columns(full, width, me)

        all_ = (slice(None), slice(None))
        heads = lambda row: [((0, slice(h, h + 1), slice(None)), row[:, h * HEAD_DIM:(h + 1) * HEAD_DIM])
                             for h in range(N_HEADS)]
        blocks = lambda pairs: [((0, h), pairs[_head_pair_block(h)]) for h in range(N_HEADS)]
        pieces = {
            "b_ada": [((slice(None), slice(k * D_MODEL, (k + 1) * D_MODEL)), row) for k, row in enumerate(
                (vs_in[0:1], vs_in[1:2], vs_up[3:4], vs_up[0:1], vs_up[1:2], vs_ffn[0:1]))],
            "g_mix_pre": [(all_, vs_in[2:3])], "g_mix_post": [(all_, vs_up[4:5])],
            "g_ffn_pre": [(all_, vs_up[2:3])], "g_ffn_post": [(all_, vs_ffn[1:2])],
            "conv_b": [(all_, vs_mix[0:1])], "b_rgate": heads(vs_mix[1:2]), "b_igate": heads(vs_mix[2:3]),
            "lru_a": [(all_, vs_mix[3:4])], "v_norm_g": [(all_, vs_mix[4:5])], "v_norm_b": [(all_, vs_mix[5:6])],
            "g_lru_out": [(all_, vs_mix[6:7])], "g_gmlp_out": [(all_, vs_mix[7:8])],
            "w_rgate": blocks(dwr), "w_igate": blocks(dwi),
            "w_spatial": [((0, g), dws[g * POS_BLOCK:(g + 1) * POS_BLOCK, :]) for g in range(N_GROUPS)],
            "b_spatial": [((0,), dbs[0:N_GROUPS])],
            "ffn_conv_b": [(all_, cs[FFN_CONV_K:FFN_CONV_K + 1])],
            "conv_w": [((0,), mine(dcw[0:LRU_CONV_K], LRU_W // N_DEV))],
            "ffn_conv_w": [((0,), mine(cs[0:FFN_CONV_K], 2 * D_FF // N_DEV))],
        }
        for n_i, name in enumerate(names):
            w_ref, m_ref, v_ref = p_refs[3 * n_i:3 * n_i + 3]
            go_ref, d_ref, mo_ref, vo_ref = o_refs[4 * n_i:4 * n_i + 4]
            for idx, g in pieces[name]:
                go_ref[idx] = g
                d_ref[idx], mo_ref[idx], vo_ref[idx] = _adam_math(w_ref[idx], g, m_ref[idx], v_ref[idx])

    flat_params = [a for n in names for a in allp[n]]
    out_shape = [_sds(allp[n][0].shape, F32) for n in names for _ in range(4)] + [_sds((1, 1), F32)]
    outs = pl.pallas_call(
        body, name="adamw_small", out_shape=out_shape,
        in_specs=[_whole()] * (n_g + len(flat_params)), out_specs=[_whole()] * len(out_shape),
        compiler_params=_cparams(),
    )(*gathered, *reduced, *flat_params)
    return {n: outs[4 * i:4 * i + 4] for i, n in enumerate(names)}, outs[-1]


def _my_pos():
    return lax.axis_index("x"), lax.axis_index("y"), lax.axis_index("c")


def _flip(pos, k):
    x, y, c = pos
    return (1 - x if k & 4 else x, 1 - y if k & 2 else y, 1 - c if k & 1 else c)


def _dev_index(pos):
    x, y, c = pos
    return 4 * x + 2 * y + c


def _all_gather_small(ins, outs, send_sems, recv_sems):
    n = len(ins)
    me = _my_pos()

    def slot(a, pos):
        rows = ins[a].shape[0]
        return outs[a].at[pl.ds(pl.multiple_of(_dev_index(pos) * rows, SUBLANES), rows), :]

    def copy(a, k, block):
        return pltpu.make_async_remote_copy(
            src_ref=ins[a], dst_ref=slot(a, block), send_sem=send_sems.at[a, k - 1], recv_sem=recv_sems.at[a, k - 1],
            device_id=_flip(me, k), device_id_type=MESH)

    sends = [copy(a, k, me) for a in range(n) for k in range(1, N_DEV)]
    for cp in sends:
        cp.start()
    for a in range(n):
        rows = ins[a].shape[0]
        outs[a][pl.ds(pl.multiple_of(_dev_index(me) * rows, SUBLANES), rows), :] = ins[a][...]
    for a in range(n):
        for k in range(1, N_DEV):
            copy(a, k, _flip(me, k)).wait_recv()
    for cp in sends:
        cp.wait_send()


def _prologue(c8, cw8, fcw8, w_ada, b_ada, carry):
    cols = w_ada.shape[1]

    def body(c_ref, cw_ref, fcw_ref, w_ref, b_ref, call_ref, cwall_ref, fcwall_ref, modall_ref, mod_scr,
             s1, r1, s2, r2, start_carry):
        _all_gather_small([c_ref, cw_ref, fcw_ref], [call_ref, cwall_ref, fcwall_ref], s1, r1)
        start_carry()
        cv = _row_of_each(call_ref, 0)
        ca = cv * _sigmoid(cv)
        b_cols = _my_columns(b_ref[...], cols, _dev_index(_my_pos()))
        mod_scr[...] = jnp.dot(ca, w_ref[...], preferred_element_type=F32, precision=lax.Precision.HIGHEST) + b_cols
        _all_gather_small([mod_scr], [modall_ref], s2, r2)

    sem = lambda n: pltpu.SemaphoreType.DMA((n, N_DEV - 1))
    return _call(
        body, "prologue", (1,), in_specs=[_whole()] * 5, out_specs=[_whole()] * 4,
        out_shape=[_sds((N_DEV * SUBLANES, a.shape[1]), F32) for a in (c8, cw8, fcw8)]
        + [_sds((N_DEV * N_DEV, cols), F32)],
        scratch=[pltpu.VMEM((N_DEV, cols), F32), sem(3), sem(3), sem(1), sem(1)],
        args=(c8, cw8, fcw8, w_ada, b_ada), carry=carry, body_starts_carry=True)


def _reduce_small(gath, red, carry=None):
    n_g, n_r = len(gath), len(red)
    chip_flips = (4, 2, 6)

    def body(*refs, start_carry):
        g_in, r_in = refs[:n_g], refs[n_g:n_g + n_r]
        g_out, r_out = refs[n_g + n_r:2 * n_g + n_r], refs[2 * n_g + n_r:2 * (n_g + n_r)]
        scr = refs[2 * (n_g + n_r):]
        sib, land = scr[:n_r], scr[n_r:2 * n_r]
        g_send, g_recv, s_send, s_recv, i_send, i_recv, f_send, f_recv = scr[2 * n_r:]
        me = _my_pos()
        c = me[2]
        sibling = _flip(me, 1)

        def slot(a, pos):
            return g_out[a].at[pl.ds(pl.multiple_of(_dev_index(pos) * SUBLANES, SUBLANES), SUBLANES), :]

        def gcopy(a, k):
            return pltpu.make_async_remote_copy(
                src_ref=g_in[a], dst_ref=slot(a, me), send_sem=g_send.at[a, k - 1], recv_sem=g_recv.at[a, k - 1],
                device_id=_flip(me, k), device_id_type=MESH)

        def scopy(a):
            return pltpu.make_async_remote_copy(
                src_ref=r_in[a], dst_ref=sib[a], send_sem=s_send.at[a], recv_sem=s_recv.at[a],
                device_id=sibling, device_id_type=MESH)

        def icopy(a, j):
            return pltpu.make_async_remote_copy(
                src_ref=r_out[a], dst_ref=land[a].at[j], send_sem=i_send.at[a, j], recv_sem=i_recv.at[a, j],
                device_id=_flip(me, chip_flips[j]), device_id_type=MESH)

        def fcopy(a, j):
            return pltpu.make_async_remote_copy(
                src_ref=land[a].at[j], dst_ref=land[a].at[j], send_sem=f_send.at[a, j], recv_sem=f_recv.at[a, j],
                device_id=sibling, device_id_type=MESH)

        gathers = [gcopy(a, k) for a in range(n_g) for k in range(1, N_DEV)]
        swaps = [scopy(a) for a in range(n_r)]
        for cp in gathers + swaps:
            cp.start()
        for a in range(n_g):
            g_out[a][pl.ds(pl.multiple_of(_dev_index(me) * SUBLANES, SUBLANES), SUBLANES), :] = g_in[a][...]
        for a in range(n_r):
            swaps[a].wait_recv()
            r_out[a][...] = r_in[a][...] + sib[a][...]

        for core in range(2):
            @pl.when(c == core)
            def _():
                for a in range(core, n_r, 2):
                    for j in range(3):
                        icopy(a, j).start()

        start_carry()

        for core in range(2):
            mine = [a for a in range(n_r) if a % 2 == core]
            theirs = [a for a in range(n_r) if a % 2 != core]

            @pl.when(c == core)
            def _():
                out = [icopy(a, j) for a in mine for j in range(3)]
                fwd = []
                for a in mine:
                    for j in range(3):
                        icopy(a, j).wait_recv()
                        cp = fcopy(a, j)
                        cp.start()
                        fwd.append(cp)
                for a in theirs:
                    for j in range(3):
                        fcopy(a, j).wait_recv()
                for cp in out + fwd:
                    cp.wait_send()

        for a in range(n_r):
            r_out[a][...] = (r_out[a][...] + land[a][1]) + (land[a][0] + land[a][2])
        for a in range(n_g):
            for k in range(1, N_DEV):
                pltpu.make_async_remote_copy(
                    src_ref=g_in[a], dst_ref=slot(a, _flip(me, k)), send_sem=g_send.at[a, k - 1],
                    recv_sem=g_recv.at[a, k - 1], device_id=_flip(me, k), device_id_type=MESH).wait_recv()
        for cp in gathers + swaps:
            cp.wait_send()

    shapes = [tuple(a.shape) for a in red]
    outs, carried = _call(
        body, "reduce_small", (1,), in_specs=[_whole()] * (n_g + n_r), out_specs=[_whole()] * (n_g + n_r),
        out_shape=[_sds((N_DEV * SUBLANES, a.shape[1]), F32) for a in gath] + [_sds(s, F32) for s in shapes],
        scratch=[pltpu.VMEM(s, F32) for s in shapes] + [pltpu.VMEM((3,) + s, F32) for s in shapes]
        + [pltpu.SemaphoreType.DMA((n_g, N_DEV - 1)), pltpu.SemaphoreType.DMA((n_g, N_DEV - 1)),
           pltpu.SemaphoreType.DMA((n_r,)), pltpu.SemaphoreType.DMA((n_r,)),
           pltpu.SemaphoreType.DMA((n_r, 3)), pltpu.SemaphoreType.DMA((n_r, 3)),
           pltpu.SemaphoreType.DMA((n_r, 3)), pltpu.SemaphoreType.DMA((n_r, 3))],
        args=tuple(gath) + tuple(red), carry=carry, body_starts_carry=True)
    return (outs[:n_g], outs[n_g:]), carried


STACKED = "stacked"


def _region(ref, shard_shape, col_sharded, pos):
    r, cdim = shard_shape
    d = _dev_index(pos)
    if col_sharded == STACKED:
        return ref.at[d]
    if col_sharded:
        return ref.at[:, pl.ds(pl.multiple_of(d * cdim, LANES), cdim)]
    return ref.at[pl.ds(pl.multiple_of(d * r, 2 * SUBLANES), r), :]


def _gather_carry(shards, col_sharded):
    n_w = len(shards)
    shapes = [tuple(s.shape) for s in shards]
    full_shapes = [(N_DEV,) + s if cs == STACKED else (s[0], s[1] * N_DEV) if cs else (s[0] * N_DEV, s[1])
                   for s, cs in zip(shapes, col_sharded)]

    def tools(out_refs, scr):
        send_sems, recv_sems = scr[n_w], scr[n_w + 1]
        me = _my_pos()
        x, y, c = me
        sibling = (x, y, 1 - c)
        chips = [(1 - x, y), (x, 1 - y), (1 - x, 1 - y)]

        def region(w, pos):
            return _region(out_refs[w], shapes[w], col_sharded[w], pos)

        def copy(w, k, block, to, src=None):
            return pltpu.make_async_remote_copy(
                src_ref=region(w, block) if src is None else src, dst_ref=region(w, block),
                send_sem=send_sems.at[w, k], recv_sem=recv_sems.at[w, k], device_id=to, device_id_type=MESH)

        def first(w):
            return [copy(w, 0, me, sibling, src=scr[w])] + [
                copy(w, 1 + j, me, (*chip, c), src=scr[w]) for j, chip in enumerate(chips)]

        def mine(w):
            return pltpu.make_async_copy(scr[w], region(w, me), scr[n_w + 2].at[w])

        return me, c, sibling, chips, copy, first, mine

    def start(ins, outs, scr):
        _, _, _, _, _, first, mine = tools(outs, scr)
        for w in range(n_w):
            scr[w][...] = ins[w][...].astype(BF16)
            for cp in first(w) + [mine(w)]:
                cp.start()

    def finish(ins, outs, scr):
        me, c, sibling, chips, copy, first, mine = tools(outs, scr)
        passed = []
        for w in range(n_w):
            for j, chip in enumerate(chips):
                copy(w, 1 + j, (*chip, c), me).wait_recv()
                fwd = copy(w, 4 + j, (*chip, c), sibling)
                fwd.start()
                passed.append(fwd)
        for w in range(n_w):
            copy(w, 0, sibling, me).wait_recv()
            for j, chip in enumerate(chips):
                copy(w, 4 + j, (*chip, 1 - c), me).wait_recv()
        for w in range(n_w):
            for cp in first(w):
                cp.wait_send()
            mine(w).wait()
        for cp in passed:
            cp.wait_send()

    return _Carry(
        inputs=list(shards), in_specs=[_whole()] * n_w,
        out_shape=[_sds(s, BF16) for s in full_shapes], out_specs=[_any()] * n_w,
        scratch=[pltpu.VMEM(s, BF16) for s in shapes]
        + [pltpu.SemaphoreType.DMA((n_w, N_DEV - 1)), pltpu.SemaphoreType.DMA((n_w, N_DEV - 1)),
           pltpu.SemaphoreType.DMA((n_w,))],
        start=start, finish=finish)


def _scatter_carry(grads_bf, shard_shapes, col_sharded, relations):
    n_w = len(grads_bf)
    shapes = [tuple(s) for s in shard_shapes]

    def copies(ins, outs, scr):
        send_sems, recv_sems = scr
        me = _my_pos()
        out = []
        for w in range(n_w):
            for i, k in enumerate(relations[w]):
                peer = _flip(me, k)
                out.append(pltpu.make_async_remote_copy(
                    src_ref=_region(ins[w], shapes[w], col_sharded[w], peer), dst_ref=outs[w].at[i],
                    send_sem=send_sems.at[w, i], recv_sem=recv_sems.at[w, i],
                    device_id=peer, device_id_type=MESH))
        return out

    def start(ins, outs, scr):
        for cp in copies(ins, outs, scr):
            cp.start()

    def finish(ins, outs, scr):
        cps = copies(ins, outs, scr)
        for cp in cps:
            cp.wait_recv()
        for cp in cps:
            cp.wait_send()

    return _Carry(
        inputs=list(grads_bf), in_specs=[_any()] * n_w,
        out_shape=[_sds((len(r),) + s, BF16) for r, s in zip(relations, shapes)], out_specs=[_any()] * n_w,
        scratch=[pltpu.SemaphoreType.DMA((n_w, N_DEV - 1)), pltpu.SemaphoreType.DMA((n_w, N_DEV - 1))],
        start=start, finish=finish)


def _block_diag(w):
    eye = jnp.eye(N_HEADS, dtype=w.dtype)
    return (eye[:, None, :, None] * w[:, :, None, :]).reshape(N_HEADS * HEAD_DIM, N_HEADS * HEAD_DIM)


def _pad_rows(a):
    return jnp.pad(a, ((0, SUBLANES - a.shape[0]), (0, 0)))


def _columns_from_devices(gathered, rows):
    w = gathered.shape[1]
    return gathered.reshape(N_DEV, SUBLANES, w)[:, :rows].transpose(1, 0, 2).reshape(rows, N_DEV * w)


def _local_step(x2, target, mod, w_in_f, w_full, conv_w_full, ffn_cw_full,
                g_mix_pre, g_mix_post, conv_b, w_rgate, b_rgate, w_igate, b_igate, lru_a, v_norm_g, v_norm_b,
                w_spatial, b_spatial, g_lru_out, g_gmlp_out, g_ffn_pre, g_ffn_post, ffn_conv_b,
                gather=None, scatter=None):
    sh_m, sc_m, gt_m, sh_f, sc_f, gt_f = [mod[k] for k in range(N_MOD)]
    wr_bd = _block_diag(w_rgate[0]).astype(BF16)
    wi_bd = _block_diag(w_igate[0]).astype(BF16)
    b_r = b_rgate.reshape(1, LRU_W)
    b_i = b_igate.reshape(1, LRU_W)
    b_sp_t = b_spatial[0].T
    w_sp_t = jnp.swapaxes(w_spatial[0], 1, 2)

    def arriving(*names):
        return gather(*names) if gather else None

    near, far = (1, 2, 3, 4, 5), (6, 7)

    def leaving(*parts):
        return scatter(parts) if scatter else None

    def received(recv, parts, outs):
        for (name, _, _), out in zip(parts, outs):
            recv.setdefault(name, []).append(out)

    mix_params = (conv_w_full, conv_b, wr_bd, wi_bd, b_r, b_i, lru_a, v_norm_g, v_norm_b)
    w_out_f = w_full["w_out"]
    (z, h, ycat, hl, y, x1, h2), got = _mix_fwd(
        x2, sh_m, sc_m, g_mix_pre, w_in_f, *mix_params, w_spatial[0], b_sp_t, g_lru_out, g_gmlp_out,
        w_out_f, g_mix_post, gt_m, g_ffn_pre, sc_f, sh_f, carry=arriving("w_up"))
    w_up_f = got[0] if gather else w_full["w_up"]
    (up_pre, up, act), got = _ffn_fwd(h2, w_up_f, ffn_cw_full, ffn_conv_b, carry=arriving("w_down"))
    w_down_f = got[0] if gather else w_full["w_down"]
    d_y2, dout, loss_acc, vs_ffn = _ffn_tail(act, w_down_f, x1, gt_f, g_ffn_post, target)

    recv = {}
    gw_down, _ = _wgrad(act, d_y2, FF_CHUNK_W, "wgrad_down", by_rows=True)
    parts = [("w_down", gw_down[1], near + far)]
    (d_up, cs_ffn), got = _ffn_bwd(d_y2, up_pre, up, ffn_cw_full, w_down_f, carry=leaving(*parts))
    received(recv, parts, got)
    gw_up, _ = _wgrad(h2, d_up, FF_CHUNK_W, "wgrad_up")
    parts = [("w_up", gw_up[1], near)]
    (d_x1, d_y, d_ycat, vs_up), got = _up_bwd(
        d_up, w_up_f, x1, dout, y, w_out_f, g_ffn_pre, sc_f, g_mix_post, gt_m, carry=leaving(*parts))
    received(recv, parts, got)
    gw_out, _ = _wgrad(ycat, d_y, D_MODEL // 4, "wgrad_out")
    parts = [("w_up", gw_up[1], far), ("w_out", gw_out[1], near + far)]
    (d_z, vs_mix, dcw, d_wr, d_wi, d_ws, d_bs), got = _mix_bwd(
        d_ycat, z, hl, *mix_params, w_spatial[0], w_sp_t, b_sp_t, g_lru_out, g_gmlp_out, carry=leaving(*parts))
    received(recv, parts, got)
    gw_in, _ = _wgrad(h, d_z, IN_COLS // 4, "wgrad_in")
    (grad_x, vs_in), _ = _in_bwd(d_z, w_in_f, x2, d_x1, g_mix_pre, sc_m)
    pending = [("w_in", gw_in[1], near + far)]
    recv["w_in"] = []

    gath = [vs_in, vs_up, vs_ffn, loss_acc]
    red = [cs_ffn, vs_mix, dcw, d_wr, d_wi, d_ws.reshape(N_GROUPS * POS_BLOCK, POS_BLOCK), d_bs]
    return dict(grad_x=grad_x, gath=gath, red=red, recv=recv, pending=pending,
                w_in=gw_in, w_out=gw_out, w_up=gw_up, w_down=gw_down)


def kernel(x, c, w_ada, b_ada, g_mix_pre, g_mix_post, w_in, conv_w, conv_b, w_rgate, b_rgate, w_igate, b_igate, lru_a, v_norm_g, v_norm_b, w_spatial, b_spatial, g_lru_out, g_gmlp_out, w_out, g_ffn_pre, g_ffn_post, w_up, ffn_conv_w, ffn_conv_b, w_down, loss_target, m_w_ada, m_b_ada, m_g_mix_pre, m_g_mix_post, m_w_in, m_conv_w, m_conv_b, m_w_rgate, m_b_rgate, m_w_igate, m_b_igate, m_lru_a, m_v_norm_g, m_v_norm_b, m_w_spatial, m_b_spatial, m_g_lru_out, m_g_gmlp_out, m_w_out, m_g_ffn_pre, m_g_ffn_post, m_w_up, m_ffn_conv_w, m_ffn_conv_b, m_w_down, v_w_ada, v_b_ada, v_g_mix_pre, v_g_mix_post, v_w_in, v_conv_w, v_conv_b, v_w_rgate, v_b_rgate, v_w_igate, v_b_igate, v_lru_a, v_v_norm_g, v_v_norm_b, v_w_spatial, v_b_spatial, v_g_lru_out, v_g_gmlp_out, v_w_out, v_g_ffn_pre, v_g_ffn_post, v_w_up, v_ffn_conv_w, v_ffn_conv_b, v_w_down):
    me = _dev_index(_my_pos())
    ada_cols = w_ada.shape[-1]

    big_w = dict(w_in=(w_in, m_w_in, v_w_in, True), w_out=(w_out, m_w_out, v_w_out, False),
                 w_up=(w_up, m_w_up, v_w_up, True), w_down=(w_down, m_w_down, v_w_down, False))

    def gather(*names):
        return _gather_carry([big_w[n][0][0] for n in names], [STACKED if n == "w_up" else big_w[n][3] for n in names])

    def scatter(parts):
        return _scatter_carry([g for _, g, _ in parts], [big_w[n][0].shape[1:] for n, _, _ in parts],
                              [big_w[n][3] for n, _, _ in parts], [rel for _, _, rel in parts])

    (c_all, cw_all, fcw_all, mod_all), (w_in_f, w_out_f) = _prologue(
        jnp.broadcast_to(c, (SUBLANES, D_MODEL)), _pad_rows(conv_w[0]), _pad_rows(ffn_conv_w[0]), w_ada[0], b_ada,
        carry=gather("w_in", "w_out"))
    conv_w_full = _columns_from_devices(cw_all, LRU_CONV_K)
    ffn_cw_full = _columns_from_devices(fcw_all, FFN_CONV_K)
    mod = lax.dynamic_index_in_dim(mod_all.reshape(N_DEV, N_DEV, ada_cols), me, axis=1, keepdims=False)
    mod = mod.reshape(N_MOD, 1, D_MODEL)

    loc = _local_step(x[0], loss_target[0], mod, w_in_f, dict(w_out=w_out_f), conv_w_full, ffn_cw_full,
                      g_mix_pre, g_mix_post, conv_b, w_rgate, b_rgate, w_igate, b_igate, lru_a, v_norm_g, v_norm_b,
                      w_spatial, b_spatial, g_lru_out, g_gmlp_out, g_ffn_pre, g_ffn_post, ffn_conv_b,
                      gather=gather, scatter=scatter)
    grad_x = loc["grad_x"]

    (gathered, reduced), got = _reduce_small(loc["gath"], loc["red"], carry=scatter(loc["pending"]))
    for (name, _, _), out in zip(loc["pending"], got):
        loc["recv"][name].append(out)

    results = {}
    for name, (w_, m_, v_, cs) in big_w.items():
        results[name] = _adamw_sum(w_, loc[name][0], loc["recv"][name], m_, v_, cs, "adamw_" + name)

    params = dict(
        b_ada=(b_ada, m_b_ada, v_b_ada), g_mix_pre=(g_mix_pre, m_g_mix_pre, v_g_mix_pre),
        g_mix_post=(g_mix_post, m_g_mix_post, v_g_mix_post), conv_b=(conv_b, m_conv_b, v_conv_b),
        w_rgate=(w_rgate, m_w_rgate, v_w_rgate), b_rgate=(b_rgate, m_b_rgate, v_b_rgate),
        w_igate=(w_igate, m_w_igate, v_w_igate), b_igate=(b_igate, m_b_igate, v_b_igate),
        lru_a=(lru_a, m_lru_a, v_lru_a), v_norm_g=(v_norm_g, m_v_norm_g, v_v_norm_g),
        v_norm_b=(v_norm_b, m_v_norm_b, v_v_norm_b), w_spatial=(w_spatial, m_w_spatial, v_w_spatial),
        b_spatial=(b_spatial, m_b_spatial, v_b_spatial), g_lru_out=(g_lru_out, m_g_lru_out, v_g_lru_out),
        g_gmlp_out=(g_gmlp_out, m_g_gmlp_out, v_g_gmlp_out), g_ffn_pre=(g_ffn_pre, m_g_ffn_pre, v_g_ffn_pre),
        g_ffn_post=(g_ffn_post, m_g_ffn_post, v_g_ffn_post), ffn_conv_b=(ffn_conv_b, m_ffn_conv_b, v_ffn_conv_b))
    conv_params = dict(conv_w=(conv_w, m_conv_w, v_conv_w), ffn_conv_w=(ffn_conv_w, m_ffn_conv_w, v_ffn_conv_w))
    small_results, loss = _adamw_small(gathered, reduced, params, conv_params)
    results.update(small_results)
    loss = loss.reshape(())

    results["w_ada"] = _adamw_wada(c_all, gathered[0], gathered[1], gathered[2], w_ada, m_w_ada, v_w_ada)

    order = ["w_ada", "b_ada", "g_mix_pre", "g_mix_post", "w_in", "conv_w", "conv_b", "w_rgate", "b_rgate", "w_igate",
             "b_igate", "lru_a", "v_norm_g", "v_norm_b", "w_spatial", "b_spatial", "g_lru_out", "g_gmlp_out", "w_out",
             "g_ffn_pre", "g_ffn_post", "w_up", "ffn_conv_w", "ffn_conv_b", "w_down"]
    outs = [loss, grad_x[None]]
    for kind in range(4):
        outs += [results[n][kind] for n in order]
    return tuple(outs)
```

```python
import functools

import jax
import jax.numpy as jnp
from jax import lax
from jax.experimental import pallas as pl
from jax.experimental.pallas import tpu as pltpu

F32 = jnp.float32
BF16 = jnp.bfloat16

D_MODEL = 1024
LRU_W = 512
GMLP_W = 512
N_HEADS = 8
HEAD_DIM = 64
N_GROUPS = 4
POS_BLOCK = 128
CHUNK = 64
IN_COLS = 2048
D_FF = 3072
N_MOD = 6
N_DEV = 8
EPS = 1e-6
LRU_C = 8.0
LRU_CONV_K = 4
FFN_CONV_K = 3

ADAM_LR = 0.001
ADAM_B1 = 0.9
ADAM_B2 = 0.999
ADAM_EPS = 1e-08
ADAM_WD = 0.01
ADAM_STEP = 10

LANES = 128
SUBLANES = 8
TT_BIG = 512
TT_MIX = 256
FF_CW = 512
VMEM_LIMIT = 56 * 1024 * 1024

MESH = pl.DeviceIdType.MESH


def _sds(shape, dtype):
    return jax.ShapeDtypeStruct(shape, dtype)


def _cparams(sem=None):
    return pltpu.CompilerParams(dimension_semantics=sem, vmem_limit_bytes=VMEM_LIMIT)


def _whole():
    return pl.BlockSpec(memory_space=pltpu.VMEM)


def _const(shape):
    nd = len(shape)
    return pl.BlockSpec(shape, lambda *_: (0,) * nd)


def _any():
    return pl.BlockSpec(memory_space=pl.ANY)


class _Carry:
    def __init__(self, inputs, in_specs, out_shape, out_specs, scratch, start, finish):
        self.inputs, self.in_specs, self.out_shape, self.out_specs = inputs, in_specs, out_shape, out_specs
        self.scratch, self.start, self.finish = scratch, start, finish


def _call(body, name, grid, in_specs, out_specs, out_shape, scratch, args, carry=None, body_starts_carry=False):
    n_in, n_out, n_scr = len(in_specs), len(out_specs), len(scratch)
    c_in = len(carry.in_specs) if carry else 0
    c_out = len(carry.out_specs) if carry else 0

    def full_body(*refs):
        ins = refs[:n_in]
        c_ins = refs[n_in:n_in + c_in]
        outs = refs[n_in + c_in:n_in + c_in + n_out]
        c_outs = refs[n_in + c_in + n_out:n_in + c_in + n_out + c_out]
        scr = refs[n_in + c_in + n_out + c_out:n_in + c_in + n_out + c_out + n_scr]
        c_scr = refs[n_in + c_in + n_out + c_out + n_scr:]
        if carry:
            first = functools.reduce(lambda a, b: a & b, [pl.program_id(d) == 0 for d in range(len(grid))])
            last = functools.reduce(lambda a, b: a & b, [pl.program_id(d) == g - 1 for d, g in enumerate(grid)])

        if carry and not body_starts_carry:
            @pl.when(first)
            def _():
                carry.start(c_ins, c_outs, c_scr)

        if body_starts_carry:
            body(*ins, *outs, *scr, start_carry=(lambda: carry.start(c_ins, c_outs, c_scr)) if carry else (lambda: None))
        else:
            body(*ins, *outs, *scr)
        if carry:
            @pl.when(last)
            def _():
                carry.finish(c_ins, c_outs, c_scr)

    res = pl.pallas_call(
        full_body, name=name, grid=grid,
        in_specs=list(in_specs) + (list(carry.in_specs) if carry else []),
        out_specs=list(out_specs) + (list(carry.out_specs) if carry else []),
        out_shape=list(out_shape) + (list(carry.out_shape) if carry else []),
        scratch_shapes=list(scratch) + (list(carry.scratch) if carry else []),
        compiler_params=_cparams(("arbitrary",) * len(grid)),
    )(*args, *(carry.inputs if carry else []))
    return res[:n_out], res[n_out:]


GELU_C0 = 0.7978845608028654
GELU_C1 = GELU_C0 * 0.044715


def _gelu(x):
    t = jnp.tanh(x * (GELU_C0 + GELU_C1 * (x * x)))
    hx = 0.5 * x
    return hx + hx * t


def _gelu_and_grad(x):
    x2 = x * x
    t = jnp.tanh(x * (GELU_C0 + GELU_C1 * x2))
    hx = 0.5 * x
    g = hx + hx * t
    dg = (0.5 + 0.5 * t) + hx * (1.0 - t * t) * (GELU_C0 + 3.0 * GELU_C1 * x2)
    return g, dg


def _sigmoid(x):
    return 1.0 / (1.0 + jnp.exp(-x))


def _softplus(x):
    return jnp.maximum(x, 0.0) + jnp.log1p(jnp.exp(-jnp.abs(x)))


def _neg_expm1(x):
    series = -x * (1.0 + x * (0.5 + x * (1.0 / 6.0 + x * (1.0 / 24.0 + x * (1.0 / 120.0)))))
    return jnp.where(x > -0.1, series, 1.0 - jnp.exp(x))


def _dot(a, b):
    return jnp.dot(a.astype(BF16), b.astype(BF16), preferred_element_type=F32)


def _dot_nt(a, b):
    return lax.dot_general(a.astype(BF16), b.astype(BF16), (((1,), (1,)), ((), ())), preferred_element_type=F32)


def _dot_tn(a, b):
    return lax.dot_general(a.astype(BF16), b.astype(BF16), (((0,), (0,)), ((), ())), preferred_element_type=F32)


def _rows(shape):
    return lax.broadcasted_iota(jnp.int32, shape, 0)


def _shift_down(cur, prev8, s):
    if s == 0:
        return cur
    n = cur.shape[0]
    r = pltpu.roll(cur, s, 0)
    p = pltpu.roll(prev8, s, 0)
    top = jnp.where(_rows(p.shape) < s, p, r[0:SUBLANES])
    if n == SUBLANES:
        return top
    return jnp.concatenate([top, r[SUBLANES:]], axis=0)


def _shift_up(cur, next8, s):
    if s == 0:
        return cur
    n = cur.shape[0]
    r = pltpu.roll(cur, n - s, 0)
    q = pltpu.roll(next8, SUBLANES - s, 0)
    bot = jnp.where(_rows(q.shape) >= SUBLANES - s, q, r[n - SUBLANES:])
    if n == SUBLANES:
        return bot
    return jnp.concatenate([r[:n - SUBLANES], bot], axis=0)


def _scan_fwd(a, b):
    n = a.shape[0]
    rows = _rows(a.shape)
    s = 1
    while s < n:
        a_s = pltpu.roll(a, s, 0)
        b_s = pltpu.roll(b, s, 0)
        m = rows >= s
        b = jnp.where(m, a * b_s + b, b)
        a = jnp.where(m, a * a_s, a)
        s *= 2
    return a, b


def _scan_rev(a, b):
    n = a.shape[0]
    rows = _rows(a.shape)
    s = 1
    while s < n:
        a_s = pltpu.roll(a, n - s, 0)
        b_s = pltpu.roll(b, n - s, 0)
        m = rows < n - s
        b = jnp.where(m, b + a * b_s, b)
        a = jnp.where(m, a * a_s, a)
        s *= 2
    return a, b


def _rms(x):
    r = lax.rsqrt(jnp.mean(x * x, axis=-1, keepdims=True) + EPS)
    return x * r, r


def _rms_bwd(d_n, n, r):
    return r * (d_n - n * jnp.mean(d_n * n, axis=-1, keepdims=True))


def _colsum(x):
    return jnp.sum(x, axis=0, keepdims=True)


ROW_PIECE = 256


def _row_pieces(tt):
    return [slice(r, r + min(ROW_PIECE, tt)) for r in range(0, tt, min(ROW_PIECE, tt))]


def _lru_gates(xc, wr_ref, wi_ref, br, bi, sp_a):
    r = _sigmoid(_dot(xc, wr_ref[...]) + br)
    i = _sigmoid(_dot(xc, wi_ref[...]) + bi)
    la = -LRU_C * r * sp_a
    a = jnp.exp(la)
    mult = jnp.sqrt(_neg_expm1(2.0 * la))
    return r, i, a, mult


def _lru_conv(lx, prev8, cw_ref, cb):
    xc = cb + cw_ref[LRU_CONV_K - 1:LRU_CONV_K, :] * lx
    taps = []
    for k in range(LRU_CONV_K - 1):
        tap = _shift_down(lx, prev8, LRU_CONV_K - 1 - k)
        taps.append(tap)
        xc = xc + cw_ref[k:k + 1, :] * tap
    return xc, taps


def _ws_mask(transposed=False):
    i = lax.broadcasted_iota(jnp.int32, (POS_BLOCK, POS_BLOCK), 0)
    j = lax.broadcasted_iota(jnp.int32, (POS_BLOCK, POS_BLOCK), 1)
    if transposed:
        i, j = j, i
    return (j // CHUNK) <= (i // CHUNK)


def _gmlp_v(gv, vg, vb):
    av, dav = _gelu_and_grad(gv)
    mu = jnp.mean(av, axis=-1, keepdims=True)
    cen = av - mu
    rs = lax.rsqrt(jnp.mean(cen * cen, axis=-1, keepdims=True) + EPS)
    vhat = cen * rs
    return vhat * vg + vb, vhat, rs, dav


def _mix_fwd(x, sh, sc, g_pre, w_in, conv_w, conv_b, wr_bd, wi_bd, b_r, b_i, lru_a, vn_g, vn_b, w_sp, b_sp_t,
             g_lru, g_gmlp, w_out, g_post, gt_m, g_ffn_pre, sc_f, sh_f, carry=None):
    s_len = x.shape[0]
    tt = min(TT_MIX, s_len)
    nblk = tt // POS_BLOCK

    def body(x_ref, sh_ref, sc_ref, g_ref, w_ref, cw_ref, cb_ref, wr_ref, wi_ref, br_ref, bi_ref, la_ref, vg_ref,
             vb_ref, ws_ref, bst_ref, gl_ref, gg_ref, wo_ref, gp_ref, gtm_ref, g2_ref, scf_ref, shf_ref,
             z_ref, h_ref, y_ref, hl_ref, yo_ref, x1_ref, h2_ref, prev8, hcar):
        i = pl.program_id(0)

        @pl.when(i == 0)
        def _():
            prev8[...] = jnp.zeros_like(prev8)
            hcar[...] = jnp.zeros_like(hcar)

        n_x, _ = _rms(x_ref[...])
        h = (n_x * g_ref[...] * (1.0 + sc_ref[...]) + sh_ref[...]).astype(BF16)
        h_ref[...] = h
        z_ref[...] = jnp.dot(h, w_ref[...], preferred_element_type=F32)

        lx = z_ref[:, 0:LRU_W]
        gate = z_ref[:, LRU_W:2 * LRU_W]
        gu = z_ref[:, 2 * LRU_W:2 * LRU_W + GMLP_W]
        gv = z_ref[:, 2 * LRU_W + GMLP_W:]

        xc, _ = _lru_conv(lx, prev8[...], cw_ref, cb_ref[...])
        prev8[...] = lx[tt - SUBLANES:]
        sp_a = _softplus(-la_ref[...])
        _, ig, a, mult = _lru_gates(xc, wr_ref, wi_ref, br_ref[...], bi_ref[...], sp_a)
        bx = mult * (ig * xc)
        a_cum, b_cum = _scan_fwd(a, bx)
        hl = a_cum * hcar[0:1, :] + b_cum
        hcar[...] = jnp.broadcast_to(hl[tt - 1:tt, :], hcar.shape)
        hl_ref[...] = hl
        y_lru = hl * _gelu(gate)
        n_l, _ = _rms(y_lru)
        y_ref[:, 0:LRU_W] = (n_l * gl_ref[...]).astype(BF16)

        u = _gelu(gu)
        v, _, _, _ = _gmlp_v(gv, vg_ref[...], vb_ref[...])
        mask = _ws_mask()
        sp_parts = []
        for nb in range(nblk):
            row = []
            for g in range(N_GROUPS):
                wsm = jnp.where(mask, ws_ref[g], 0.0)
                vblk = v[nb * POS_BLOCK:(nb + 1) * POS_BLOCK, g * LANES:(g + 1) * LANES]
                row.append(_dot(wsm, vblk) + bst_ref[:, g:g + 1])
            sp_parts.append(jnp.concatenate(row, axis=1))
        sp = jnp.concatenate(sp_parts, axis=0) if nblk > 1 else sp_parts[0]
        n_g, _ = _rms(u * sp)
        y_ref[:, LRU_W:] = (n_g * gg_ref[...]).astype(BF16)

        y = jnp.dot(y_ref[...], wo_ref[...], preferred_element_type=F32)
        yo_ref[...] = y
        n_y, _ = _rms(y)
        x1 = x_ref[...] + gtm_ref[...] * (n_y * gp_ref[...])
        x1_ref[...] = x1
        n1, _ = _rms(x1)
        h2_ref[...] = (n1 * g2_ref[...] * (1.0 + scf_ref[...]) + shf_ref[...]).astype(BF16)

    row = lambda c: pl.BlockSpec((tt, c), lambda i: (i, 0))
    v512 = _const((1, LRU_W))
    vec = _const((1, D_MODEL))
    return _call(
        body, "mix_fwd", (s_len // tt,),
        in_specs=[row(D_MODEL), vec, vec, vec, _whole(),
                  _const((LRU_CONV_K, LRU_W)), v512, _whole(), _whole(), v512, v512, v512, v512, v512,
                  _whole(), _whole(), v512, v512, _whole(), vec, vec, vec, vec, vec],
        out_specs=[row(IN_COLS), row(D_MODEL), row(LRU_W + GMLP_W), row(LRU_W), row(D_MODEL), row(D_MODEL),
                   row(D_MODEL)],
        out_shape=[_sds((s_len, IN_COLS), F32), _sds((s_len, D_MODEL), BF16),
                   _sds((s_len, LRU_W + GMLP_W), BF16), _sds((s_len, LRU_W), F32),
                   _sds((s_len, D_MODEL), F32), _sds((s_len, D_MODEL), F32), _sds((s_len, D_MODEL), BF16)],
        scratch=[pltpu.VMEM((SUBLANES, LRU_W), F32), pltpu.VMEM((SUBLANES, LRU_W), F32)],
        args=(x, sh, sc, g_pre, w_in, conv_w, conv_b, wr_bd, wi_bd, b_r, b_i, lru_a, vn_g, vn_b, w_sp, b_sp_t,
              g_lru, g_gmlp, w_out, g_post, gt_m, g_ffn_pre, sc_f, sh_f), carry=carry)


FF_CHUNKS = N_DEV // 2
FF_CHUNK_W = D_FF // FF_CHUNKS


def _ffn_fwd(h2, w_up3, ffn_cw, ffn_cb, carry=None):
    s_len = h2.shape[0]
    tt = min(TT_MIX, s_len)
    nc, cw = FF_CHUNKS, FF_CHUNK_W

    def body(h2_ref, wu_ref, cwg_ref, cwv_ref, cbg_ref, cbv_ref, up_ref, upc_ref, act_ref, prev):
        i = pl.program_id(0)
        c = pl.program_id(1)

        @pl.when(i == 0)
        def _():
            prev[c] = jnp.zeros((2, SUBLANES, cw), F32)

        h2 = h2_ref[...]
        ug_pre = jnp.dot(h2, wu_ref[c], preferred_element_type=F32)
        uv_pre = jnp.dot(h2, wu_ref[nc + c], preferred_element_type=F32)
        up_ref[0] = ug_pre.astype(BF16)
        up_ref[1] = uv_pre.astype(BF16)
        ug, _ = _ffn_conv(ug_pre, prev[c, 0], cwg_ref, cbg_ref[...])
        uv, _ = _ffn_conv(uv_pre, prev[c, 1], cwv_ref, cbv_ref[...])
        prev[c, 0] = ug_pre[tt - SUBLANES:, :]
        prev[c, 1] = uv_pre[tt - SUBLANES:, :]
        upc_ref[0] = ug
        upc_ref[1] = uv
        act_ref[...] = (_gelu(ug) * uv).astype(BF16)

    chunk2 = pl.BlockSpec((2, tt, cw), lambda i, c: (0, i, c))
    ffn_cb2 = ffn_cb.reshape(1, 2 * D_FF)
    return _call(
        body, "ffn_fwd", (s_len // tt, nc),
        in_specs=[pl.BlockSpec((tt, D_MODEL), lambda i, c: (i, 0)), _whole(),
                  pl.BlockSpec((FFN_CONV_K, cw), lambda i, c: (0, c)),
                  pl.BlockSpec((FFN_CONV_K, cw), lambda i, c: (0, c + nc)),
                  pl.BlockSpec((1, cw), lambda i, c: (0, c)),
                  pl.BlockSpec((1, cw), lambda i, c: (0, c + nc))],
        out_specs=[chunk2, chunk2, pl.BlockSpec((tt, cw), lambda i, c: (i, c))],
        out_shape=[_sds((2, s_len, D_FF), BF16), _sds((2, s_len, D_FF), F32), _sds((s_len, D_FF), BF16)],
        scratch=[pltpu.VMEM((nc, 2, SUBLANES, cw), F32)],
        args=(h2, w_up3, ffn_cw, ffn_cw, ffn_cb2, ffn_cb2), carry=carry)


def _ffn_tail(act, w_down, x1, gt_f, g_post, target):
    s_len = x1.shape[0]
    tt = min(TT_BIG, s_len)

    def body(act_ref, wd_ref, x1_ref, gtf_ref, gp_ref, tg_ref, dy2_ref, dout_ref, loss_ref, vs_ref):
        @pl.when(pl.program_id(0) == 0)
        def _():
            loss_ref[...] = jnp.zeros_like(loss_ref)
            vs_ref[...] = jnp.zeros_like(vs_ref)

        for rows in _row_pieces(tt):
            n2, r2 = _rms(jnp.dot(act_ref[rows, :], wd_ref[...], preferred_element_type=F32))
            out = x1_ref[rows, :] + gtf_ref[...] * (n2 * gp_ref[...])
            err = out - tg_ref[rows, :]
            do = err * (1.0 / D_MODEL)
            dout_ref[rows, :] = do
            loss_ref[...] += jnp.broadcast_to(0.5 * jnp.sum(err * err, keepdims=True) * (1.0 / D_MODEL),
                                              loss_ref.shape)
            vs_ref[0:1, :] += _colsum(do * n2 * gp_ref[...])
            vs_ref[1:2, :] += _colsum(do * gtf_ref[...] * n2)
            dy2_ref[rows, :] = _rms_bwd(do * gtf_ref[...] * gp_ref[...], n2, r2).astype(BF16)

    row = lambda c: pl.BlockSpec((tt, c), lambda i: (i, 0))
    vec = _const((1, D_MODEL))
    outs, _ = _call(
        body, "ffn_tail", (s_len // tt,),
        in_specs=[row(D_FF), _whole(), row(D_MODEL), vec, vec, row(D_MODEL)],
        out_specs=[row(D_MODEL), row(D_MODEL), _const((SUBLANES, LANES)), _const((SUBLANES, D_MODEL))],
        out_shape=[_sds((s_len, D_MODEL), BF16), _sds((s_len, D_MODEL), F32), _sds((SUBLANES, LANES), F32),
                   _sds((SUBLANES, D_MODEL), F32)],
        scratch=[], args=(act, w_down, x1, gt_f, g_post, target))
    return outs


def _ffn_conv(up_pre, prev8, cw_ref, cb):
    up = cb + cw_ref[FFN_CONV_K - 1:FFN_CONV_K, :] * up_pre
    taps = []
    for k in range(FFN_CONV_K - 1):
        tap = _shift_down(up_pre, prev8, FFN_CONV_K - 1 - k)
        taps.append(tap)
        up = up + cw_ref[k:k + 1, :] * tap
    return up, taps


def _ffn_bwd(d_y2, up_pre, up, ffn_cw, w_down, carry=None):
    s_len = d_y2.shape[0]
    tt = min(TT_BIG, s_len)
    nt = s_len // tt
    cw = FF_CW
    nc = D_FF // cw

    def body(dy2_ref, up_ref, upc_ref, cwg_ref, cwv_ref, wd_ref, dup_ref, cs_ref, nxt, cs_acc):
        i = pl.program_id(0)
        c = pl.program_id(1)

        @pl.when(i == 0)
        def _():
            nxt[c] = jnp.zeros((2, SUBLANES, cw), F32)
            cs_acc[c] = jnp.zeros((2, SUBLANES, cw), F32)

        pw = cw // 2
        for piece in range(2):
            cols = slice(piece * pw, (piece + 1) * pw)
            d_act = _dot_nt(dy2_ref[...], wd_ref[cols, :])
            uv = upc_ref[1, :, cols]
            gl, dgl = _gelu_and_grad(upc_ref[0, :, cols])
            d_ug = d_act * uv * dgl
            d_uv = d_act * gl
            for half, (d_u, cw_ref) in enumerate(((d_ug, cwg_ref), (d_uv, cwv_ref))):
                nx = nxt[c, half, :, cols]
                x_in = up_ref[half, :, cols].astype(F32)
                d_pre = cw_ref[FFN_CONV_K - 1:FFN_CONV_K, cols] * d_u
                sums = [None] * (FFN_CONV_K + 1)
                sums[FFN_CONV_K - 1] = _colsum(d_u * x_in)
                for k in range(FFN_CONV_K - 1):
                    ahead = _shift_up(d_u, nx, FFN_CONV_K - 1 - k)
                    d_pre = d_pre + cw_ref[k:k + 1, cols] * ahead
                    sums[k] = _colsum(ahead * x_in)
                sums[FFN_CONV_K] = _colsum(d_u)
                pad = jnp.zeros((SUBLANES - FFN_CONV_K - 1, pw), F32)
                cs_acc[c, half, :, cols] += jnp.concatenate(sums + [pad], axis=0)
                nxt[c, half, :, cols] = d_u[0:SUBLANES]
                dup_ref[half, :, cols] = d_pre.astype(BF16)

        for cc in range(nc):
            @pl.when((i == nt - 1) & (c == cc))
            def _():
                cs_ref[:, cc * cw:(cc + 1) * cw] = cs_acc[cc, 0]
                cs_ref[:, D_FF + cc * cw:D_FF + (cc + 1) * cw] = cs_acc[cc, 1]

    row = pl.BlockSpec((tt, D_MODEL), lambda i, c: (nt - 1 - i, 0))
    blk = pl.BlockSpec((2, tt, cw), lambda i, c: (0, nt - 1 - i, c))
    return _call(
        body, "ffn_bwd", (nt, nc),
        in_specs=[row, blk, blk,
                  pl.BlockSpec((FFN_CONV_K, cw), lambda i, c: (0, c)),
                  pl.BlockSpec((FFN_CONV_K, cw), lambda i, c: (0, c + nc)),
                  pl.BlockSpec((cw, D_MODEL), lambda i, c: (c, 0))],
        out_specs=[blk, _const((SUBLANES, 2 * D_FF))],
        out_shape=[_sds((2, s_len, D_FF), BF16), _sds((SUBLANES, 2 * D_FF), F32)],
        scratch=[pltpu.VMEM((nc, 2, SUBLANES, cw), F32), pltpu.VMEM((nc, 2, SUBLANES, cw), F32)],
        args=(d_y2, up_pre, up, ffn_cw, ffn_cw, w_down), carry=carry)


def _up_bwd(d_up, w_up3, x1, dout, y, w_out, g_pre, sc_f, g_post, gt_m, carry=None):
    s_len = x1.shape[0]
    tt = min(TT_BIG, s_len)

    def body(du_ref, wu_ref, x1_ref, do_ref, y_ref, wo_ref, g2_ref, sc_ref, gp_ref, gt_ref,
             dx1_ref, dy_ref, dyc_ref, vs_ref):
        @pl.when(pl.program_id(0) == 0)
        def _():
            vs_ref[...] = jnp.zeros_like(vs_ref)

        for rows in _row_pieces(tt):
            d_h2 = jnp.zeros((rows.stop - rows.start, D_MODEL), F32)
            for half in range(2):
                for ch in range(FF_CHUNKS):
                    d_h2 = d_h2 + _dot_nt(du_ref[half, rows, ch * FF_CHUNK_W:(ch + 1) * FF_CHUNK_W],
                                          wu_ref[half * FF_CHUNKS + ch])
            n1, r1 = _rms(x1_ref[rows, :])
            ng = n1 * g2_ref[...]
            vs_ref[0:1, :] += _colsum(d_h2)
            vs_ref[1:2, :] += _colsum(d_h2 * ng)
            d_ng = d_h2 * (1.0 + sc_ref[...])
            vs_ref[2:3, :] += _colsum(d_ng * n1)
            d_x1 = do_ref[rows, :] + _rms_bwd(d_ng * g2_ref[...], n1, r1)
            dx1_ref[rows, :] = d_x1
            n_y, r_y = _rms(y_ref[rows, :])
            vs_ref[3:4, :] += _colsum(d_x1 * n_y * gp_ref[...])
            d_on = d_x1 * gt_ref[...]
            vs_ref[4:5, :] += _colsum(d_on * n_y)
            d_y = _rms_bwd(d_on * gp_ref[...], n_y, r_y).astype(BF16)
            dy_ref[rows, :] = d_y
            dyc_ref[rows, :] = _dot_nt(d_y, wo_ref[...])

    row = lambda c: pl.BlockSpec((tt, c), lambda i: (i, 0))
    vec = _const((1, D_MODEL))
    return _call(
        body, "up_bwd", (s_len // tt,),
        in_specs=[pl.BlockSpec((2, tt, D_FF), lambda i: (0, i, 0)), _whole(), row(D_MODEL), row(D_MODEL), row(D_MODEL),
                  _whole(), vec, vec, vec, vec],
        out_specs=[row(D_MODEL), row(D_MODEL), row(LRU_W + GMLP_W), _const((SUBLANES, D_MODEL))],
        out_shape=[_sds((s_len, D_MODEL), F32), _sds((s_len, D_MODEL), BF16), _sds((s_len, LRU_W + GMLP_W), F32),
                   _sds((SUBLANES, D_MODEL), F32)],
        scratch=[], args=(d_up, w_up3, x1, dout, y, w_out, g_pre, sc_f, g_post, gt_m), carry=carry)


def _head_pair_block(hd):
    return (slice((hd // 2) * HEAD_DIM, (hd // 2 + 1) * HEAD_DIM), slice((hd % 2) * HEAD_DIM, (hd % 2 + 1) * HEAD_DIM))


def _mix_bwd(d_ycat, z, hl, conv_w, conv_b, wr_bd, wi_bd, b_r, b_i, lru_a, vn_g, vn_b, w_sp, w_sp_t, b_sp_t,
             g_lru, g_gmlp, carry=None):
    s_len = z.shape[0]
    tt = min(TT_MIX, s_len)
    nt = s_len // tt
    nblk = tt // POS_BLOCK
    hb = tt // SUBLANES

    def body(dyc_ref, z_ref, zh_ref, hl_ref, hh_ref, cw_ref, cb_ref, wr_ref, wi_ref, br_ref, bi_ref, la_ref,
             vg_ref, vb_ref, ws_ref, wst_ref, bst_ref, gl_ref, gg_ref,
             dz_ref, vs_ref, dcw_ref, dwrb_ref, dwib_ref, dws_ref, dbs_ref, nxt_dxc, nxt_a, nxt_lam, dwr_ref, dwi_ref):
        i = pl.program_id(0)
        first_tile = i == nt - 1

        @pl.when(i == 0)
        def _():
            for ref in (vs_ref, dcw_ref, dwr_ref, dwi_ref, dws_ref, dbs_ref, nxt_dxc, nxt_a, nxt_lam):
                ref[...] = jnp.zeros_like(ref)

        lx = z_ref[:, 0:LRU_W]
        gate = z_ref[:, LRU_W:2 * LRU_W]
        gu = z_ref[:, 2 * LRU_W:2 * LRU_W + GMLP_W]
        gv = z_ref[:, 2 * LRU_W + GMLP_W:]
        prev8 = jnp.where(first_tile, 0.0, zh_ref[...])
        hprev8 = jnp.where(first_tile, 0.0, hh_ref[...])

        xc, taps = _lru_conv(lx, prev8, cw_ref, cb_ref[...])
        a_par = la_ref[...]
        sp_a = _softplus(-a_par)
        r, ig, a, mult = _lru_gates(xc, wr_ref, wi_ref, br_ref[...], bi_ref[...], sp_a)
        hl = hl_ref[...]
        h_prev = _shift_down(hl, hprev8, 1)
        ggate, dggate = _gelu_and_grad(gate)
        y_lru = hl * ggate
        n_l, r_l = _rms(y_lru)
        d_nl = dyc_ref[:, 0:LRU_W]
        vs_ref[6:7, :] += _colsum(d_nl * n_l)
        d_yl = _rms_bwd(d_nl * gl_ref[...], n_l, r_l)
        d_hl = d_yl * ggate
        d_gate = d_yl * hl * dggate
        a_up = _shift_up(a, nxt_a[...], 1)
        a_cum, b_cum = _scan_rev(a_up, d_hl)
        lam = b_cum + a_cum * nxt_lam[0:1, :]
        nxt_a[...] = jnp.broadcast_to(a[0:1, :], nxt_a.shape)
        nxt_lam[...] = jnp.broadcast_to(lam[0:1, :], nxt_lam.shape)
        ixc = ig * xc
        d_la = lam * h_prev * a - lam * ixc * (a * a) / mult
        d_i = lam * mult * xc
        d_xc = lam * mult * ig
        vs_ref[3:4, :] += _colsum(d_la * r) * (LRU_C * _sigmoid(-a_par))
        d_pr = d_la * (-LRU_C * sp_a) * r * (1.0 - r)
        d_pi = d_i * ig * (1.0 - ig)
        vs_ref[1:2, :] += _colsum(d_pr)
        vs_ref[2:3, :] += _colsum(d_pi)
        dwr_ref[...] += _dot_tn(xc, d_pr)
        dwi_ref[...] += _dot_tn(xc, d_pi)
        d_xc = d_xc + _dot_nt(d_pr, wr_ref[...]) + _dot_nt(d_pi, wi_ref[...])
        vs_ref[0:1, :] += _colsum(d_xc)
        nx = nxt_dxc[...]
        d_lx = cw_ref[LRU_CONV_K - 1:LRU_CONV_K, :] * d_xc
        dcw_ref[LRU_CONV_K - 1:LRU_CONV_K, :] += _colsum(d_xc * lx)
        for k in range(LRU_CONV_K - 1):
            d_lx = d_lx + cw_ref[k:k + 1, :] * _shift_up(d_xc, nx, LRU_CONV_K - 1 - k)
            dcw_ref[k:k + 1, :] += _colsum(d_xc * taps[k])
        nxt_dxc[...] = d_xc[0:SUBLANES]
        dz_ref[:, 0:LRU_W] = d_lx.astype(BF16)
        dz_ref[:, LRU_W:2 * LRU_W] = d_gate.astype(BF16)

        u, du = _gelu_and_grad(gu)
        v, vhat, rs, dav = _gmlp_v(gv, vg_ref[...], vb_ref[...])
        mask = _ws_mask()
        sp_parts = []
        for nb in range(nblk):
            rowp = []
            for g in range(N_GROUPS):
                wsm = jnp.where(mask, ws_ref[g], 0.0)
                vblk = v[nb * POS_BLOCK:(nb + 1) * POS_BLOCK, g * LANES:(g + 1) * LANES]
                rowp.append(_dot(wsm, vblk) + bst_ref[:, g:g + 1])
            sp_parts.append(jnp.concatenate(rowp, axis=1))
        sp = jnp.concatenate(sp_parts, axis=0) if nblk > 1 else sp_parts[0]
        y_g = u * sp
        n_g, r_g = _rms(y_g)
        d_ng = dyc_ref[:, LRU_W:]
        vs_ref[7:8, :] += _colsum(d_ng * n_g)
        d_yg = _rms_bwd(d_ng * gg_ref[...], n_g, r_g)
        d_gu = d_yg * sp * du
        d_sp = d_yg * u
        mask_t = _ws_mask(transposed=True)
        ones8 = jnp.ones((SUBLANES, LANES), F32)
        dv_parts = []
        for nb in range(nblk):
            rowp = []
            for g in range(N_GROUPS):
                rs_, cs_ = slice(nb * POS_BLOCK, (nb + 1) * POS_BLOCK), slice(g * LANES, (g + 1) * LANES)
                dsp_blk = d_sp[rs_, cs_]
                dbs_ref[g:g + 1, :] += lax.dot_general(
                    ones8, dsp_blk, (((1,), (1,)), ((), ())), preferred_element_type=F32,
                    precision=lax.Precision.HIGHEST)[0:1, :]
                dws_ref[g] += _dot_nt(dsp_blk, v[rs_, cs_])
                wsm_t = jnp.where(mask_t, wst_ref[g], 0.0)
                rowp.append(_dot(wsm_t, dsp_blk))
            dv_parts.append(jnp.concatenate(rowp, axis=1))
        d_v = jnp.concatenate(dv_parts, axis=0) if nblk > 1 else dv_parts[0]
        vs_ref[4:5, :] += _colsum(d_v * vhat)
        vs_ref[5:6, :] += _colsum(d_v)
        d_vh = d_v * vg_ref[...]
        d_av = rs * (d_vh - jnp.mean(d_vh, axis=-1, keepdims=True)
                     - vhat * jnp.mean(d_vh * vhat, axis=-1, keepdims=True))
        dz_ref[:, 2 * LRU_W:2 * LRU_W + GMLP_W] = d_gu.astype(BF16)
        dz_ref[:, 2 * LRU_W + GMLP_W:] = (d_av * dav).astype(BF16)

        @pl.when(i == nt - 1)
        def _():
            for hd in range(N_HEADS):
                blk = slice(hd * HEAD_DIM, (hd + 1) * HEAD_DIM)
                dwrb_ref[_head_pair_block(hd)] = dwr_ref[blk, blk]
                dwib_ref[_head_pair_block(hd)] = dwi_ref[blk, blk]
            for g in range(N_GROUPS):
                dws_ref[g] = jnp.where(mask, dws_ref[g], 0.0)

    rev = lambda c: pl.BlockSpec((tt, c), lambda i: (nt - 1 - i, 0))
    halo = pl.BlockSpec((SUBLANES, LRU_W), lambda i: (jnp.maximum((nt - 1 - i) * hb - 1, 0), 0))
    v512 = _const((1, LRU_W))
    return _call(
        body, "mix_bwd", (nt,),
        in_specs=[rev(LRU_W + GMLP_W), rev(IN_COLS), halo, rev(LRU_W), halo,
                  _const((LRU_CONV_K, LRU_W)), v512, _whole(), _whole(), v512, v512, v512, v512, v512,
                  _whole(), _whole(), _whole(), v512, v512],
        out_specs=[rev(IN_COLS), _const((SUBLANES, LRU_W)), _const((SUBLANES, LRU_W)),
                   _const((LRU_W // 2, 2 * HEAD_DIM)), _const((LRU_W // 2, 2 * HEAD_DIM)),
                   _const((N_GROUPS, POS_BLOCK, POS_BLOCK)), _const((SUBLANES, POS_BLOCK))],
        out_shape=[_sds((s_len, IN_COLS), BF16), _sds((SUBLANES, LRU_W), F32), _sds((SUBLANES, LRU_W), F32),
                   _sds((LRU_W // 2, 2 * HEAD_DIM), F32), _sds((LRU_W // 2, 2 * HEAD_DIM), F32),
                   _sds((N_GROUPS, POS_BLOCK, POS_BLOCK), F32), _sds((SUBLANES, POS_BLOCK), F32)],
        scratch=[pltpu.VMEM((SUBLANES, LRU_W), F32), pltpu.VMEM((SUBLANES, LRU_W), F32),
                 pltpu.VMEM((SUBLANES, LRU_W), F32), pltpu.VMEM((LRU_W, LRU_W), F32), pltpu.VMEM((LRU_W, LRU_W), F32)],
        args=(d_ycat, z, z, hl, hl, conv_w, conv_b, wr_bd, wi_bd, b_r, b_i, lru_a, vn_g, vn_b, w_sp, w_sp_t, b_sp_t,
              g_lru, g_gmlp), carry=carry)


def _in_bwd(d_z, w_in, x, d_x1, g, sc, carry=None):
    s_len = x.shape[0]
    tt = min(TT_BIG, s_len)

    def body(dz_ref, w_ref, x_ref, dx1_ref, g_ref, sc_ref, gx_ref, vs_ref):
        @pl.when(pl.program_id(0) == 0)
        def _():
            vs_ref[...] = jnp.zeros_like(vs_ref)

        for rows in _row_pieces(tt):
            d_h = _dot_nt(dz_ref[rows, :], w_ref[...])
            n, r = _rms(x_ref[rows, :])
            vs_ref[0:1, :] += _colsum(d_h)
            vs_ref[1:2, :] += _colsum(d_h * n * g_ref[...])
            d_ng = d_h * (1.0 + sc_ref[...])
            vs_ref[2:3, :] += _colsum(d_ng * n)
            gx_ref[rows, :] = dx1_ref[rows, :] + _rms_bwd(d_ng * g_ref[...], n, r)

    row = lambda c: pl.BlockSpec((tt, c), lambda i: (i, 0))
    vec = _const((1, D_MODEL))
    return _call(
        body, "in_bwd", (s_len // tt,),
        in_specs=[row(IN_COLS), _whole(), row(D_MODEL), row(D_MODEL), vec, vec],
        out_specs=[row(D_MODEL), _const((SUBLANES, D_MODEL))],
        out_shape=[_sds((s_len, D_MODEL), F32), _sds((SUBLANES, D_MODEL), F32)],
        scratch=[], args=(d_z, w_in, x, d_x1, g, sc), carry=carry)


def _wgrad(a, b, tile, name, by_rows=False, carry=None):
    s_len, k_dim = a.shape
    halves = b.ndim == 3
    n_dim = b.shape[-1] * (2 if halves else 1)

    def body(a_ref, b_ref, o_ref, ob_ref):
        out = _dot_tn(a_ref[...], b_ref[0] if halves else b_ref[...])
        o_ref[...] = out
        ob_ref[...] = out.astype(BF16)

    if by_rows:
        steps = k_dim // tile
        a_spec = pl.BlockSpec((s_len, tile), lambda j: (0, j))
        b_spec = pl.BlockSpec((s_len, n_dim), lambda j: (0, 0))
        o_spec = pl.BlockSpec((tile, n_dim), lambda j: (j, 0))
    else:
        steps = n_dim // tile
        a_spec = pl.BlockSpec((s_len, k_dim), lambda j: (0, 0))
        if halves:
            per_half = steps // 2
            b_spec = pl.BlockSpec((1, s_len, tile), lambda j: (j // per_half, 0, j % per_half))
        else:
            b_spec = pl.BlockSpec((s_len, tile), lambda j: (0, j))
        o_spec = pl.BlockSpec((k_dim, tile), lambda j: (0, j))
    return _call(
        body, name, (steps,), in_specs=[a_spec, b_spec], out_specs=[o_spec, o_spec],
        out_shape=[_sds((k_dim, n_dim), F32), _sds((k_dim, n_dim), BF16)],
        scratch=[], args=(a, b), carry=carry)


def _adam_math(w, g, m, v):
    m = ADAM_B1 * m + (1.0 - ADAM_B1) * g
    v = ADAM_B2 * v + (1.0 - ADAM_B2) * (g * g)
    m_hat = m / (1.0 - ADAM_B1 ** ADAM_STEP)
    v_hat = v / (1.0 - ADAM_B2 ** ADAM_STEP)
    delta = -ADAM_LR * (m_hat / (jnp.sqrt(v_hat) + ADAM_EPS) + ADAM_WD * w)
    return delta, m, v


def _row_tile(rows, cols, n_f32_arrays):
    budget = VMEM_LIMIT // 2
    tr = rows
    while tr % 2 == 0 and tr // 2 >= SUBLANES and (tr // 2) % SUBLANES == 0 and tr * cols * 4 * n_f32_arrays * 2 > budget:
        tr //= 2
    return tr


def _adamw_sum(w, g_full, recv, m, v, col_sharded, name):
    _, rows, cols = w.shape
    n_recv = len(recv)
    tr = _row_tile(rows, cols, 10)
    nb = rows // tr

    def body(me_ref, w_ref, g_ref, *rest):
        r_refs = rest[:n_recv]
        m_ref, v_ref, go_ref, d_ref, mo_ref, vo_ref = rest[n_recv:]
        g = g_ref[...]
        for r_ref in r_refs:
            for k in range(r_ref.shape[0]):
                g = g + r_ref[k].astype(F32)
        go_ref[0] = g
        d_ref[0], mo_ref[0], vo_ref[0] = _adam_math(w_ref[0], g, m_ref[0], v_ref[0])

    if col_sharded:
        own = pl.BlockSpec((tr, cols), lambda i, me: (i, me[0]))
    else:
        own = pl.BlockSpec((tr, cols), lambda i, me: (me[0] * nb + i, 0))
    blk = pl.BlockSpec((1, tr, cols), lambda i, me: (0, i, 0))
    return pl.pallas_call(
        body, name=name,
        grid_spec=pltpu.PrefetchScalarGridSpec(
            num_scalar_prefetch=1, grid=(nb,),
            in_specs=[blk, own] + [pl.BlockSpec((r.shape[0], tr, cols), lambda i, me: (0, i, 0)) for r in recv]
            + [blk, blk],
            out_specs=[blk] * 4),
        out_shape=[_sds((1, rows, cols), F32)] * 4,
        compiler_params=_cparams(("arbitrary",)),
    )(jnp.reshape(_dev_index(_my_pos()), (1,)).astype(jnp.int32), w, g_full, *recv, m, v)


def _row_of_each(ref, row):
    cols = ref.shape[1]
    rows = _rows((N_DEV, cols))
    out = jnp.zeros((N_DEV, cols), F32)
    for d in range(N_DEV):
        picked = ref[d * SUBLANES + row:d * SUBLANES + row + 1, :]
        out = jnp.where(rows == d, jnp.broadcast_to(picked, (N_DEV, cols)), out)
    return out


def _my_columns(full, width, me):
    out = jnp.zeros(full.shape[:-1] + (width,), F32)
    for d in range(N_DEV):
        out = out + jnp.where(me == d, full[:, d * width:(d + 1) * width], 0.0)
    return out


def _adamw_wada(c_all, vs_in_all, vs_up_all, vs_ffn_all, w, m, v):
    _, rows, cols = w.shape

    def body(c_ref, vi_ref, vu_ref, vf_ref, w_ref, m_ref, v_ref, go_ref, d_ref, mo_ref, vo_ref):
        me = _dev_index(_my_pos())
        cv = _row_of_each(c_ref, 0)
        ca = cv * _sigmoid(cv)
        dmod = jnp.concatenate([_row_of_each(vi_ref, 0), _row_of_each(vi_ref, 1), _row_of_each(vu_ref, 3),
                                _row_of_each(vu_ref, 0), _row_of_each(vu_ref, 1), _row_of_each(vf_ref, 0)], axis=1)
        dm = _my_columns(dmod, cols, me)
        g = lax.dot_general(ca, dm, (((0,), (0,)), ((), ())), preferred_element_type=F32,
                            precision=lax.Precision.HIGHEST)
        go_ref[0] = g
        d_ref[0], mo_ref[0], vo_ref[0] = _adam_math(w_ref[0], g, m_ref[0], v_ref[0])

    return pl.pallas_call(
        body, name="adamw_w_ada", out_shape=[_sds((1, rows, cols), F32)] * 4,
        in_specs=[_whole()] * 7, out_specs=[_whole()] * 4,
        compiler_params=_cparams(),
    )(c_all, vs_in_all, vs_up_all, vs_ffn_all, w, m, v)


def _adamw_small(gathered, reduced, params, conv_params):
    names = list(params) + list(conv_params)
    allp = {**params, **conv_params}
    n_g = len(gathered) + len(reduced)

    def body(*refs):
        g_refs = refs[:n_g]
        p_refs = refs[n_g:n_g + 3 * len(names)]
        o_refs = refs[n_g + 3 * len(names):]
        me = _dev_index(_my_pos())

        def total(ref):
            s = ref[0:SUBLANES, :]
            for d in range(1, N_DEV):
                s = s + ref[d * SUBLANES:(d + 1) * SUBLANES, :]
            return s

        vs_in, vs_up, vs_ffn, loss = [total(r) for r in g_refs[:4]]
        cs, vs_mix, dcw, dwr, dwi, dws, dbs = [r[...] for r in g_refs[4:]]
        o_refs[-1][...] = loss[0:1, 0:1]
        mine = lambda full, width: _my_columns(full, width, me)

        all_ = (slice(None), slice(None))
        heads = lambda row: [((0, slice(h, h + 1), slice(None)), row[:, h * HEAD_DIM:(h + 1) * HEAD_DIM])
                             for h in range(N_HEADS)]
        blocks = lambda pairs: [((0, h), pairs[_head_pair_block(h)]) for h in range(N_HEADS)]
        pieces = {
            "b_ada": [((slice(None), slice(k * D_MODEL, (k + 1) * D_MODEL)), row) for k, row in enumerate(
                (vs_in[0:1], vs_in[1:2], vs_up[3:4], vs_up[0:1], vs_up[1:2], vs_ffn[0:1]))],
            "g_mix_pre": [(all_, vs_in[2:3])], "g_mix_post": [(all_, vs_up[4:5])],
            "g_ffn_pre": [(all_, vs_up[2:3])], "g_ffn_post": [(all_, vs_ffn[1:2])],
            "conv_b": [(all_, vs_mix[0:1])], "b_rgate": heads(vs_mix[1:2]), "b_igate": heads(vs_mix[2:3]),
            "lru_a": [(all_, vs_mix[3:4])], "v_norm_g": [(all_, vs_mix[4:5])], "v_norm_b": [(all_, vs_mix[5:6])],
            "g_lru_out": [(all_, vs_mix[6:7])], "g_gmlp_out": [(all_, vs_mix[7:8])],
            "w_rgate": blocks(dwr), "w_igate": blocks(dwi),
            "w_spatial": [((0, g), dws[g * POS_BLOCK:(g + 1) * POS_BLOCK, :]) for g in range(N_GROUPS)],
            "b_spatial": [((0,), dbs[0:N_GROUPS])],
            "ffn_conv_b": [(all_, cs[FFN_CONV_K:FFN_CONV_K + 1])],
            "conv_w": [((0,), mine(dcw[0:LRU_CONV_K], LRU_W // N_DEV))],
            "ffn_conv_w": [((0,), mine(cs[0:FFN_CONV_K], 2 * D_FF // N_DEV))],
        }
        for n_i, name in enumerate(names):
            w_ref, m_ref, v_ref = p_refs[3 * n_i:3 * n_i + 3]
            go_ref, d_ref, mo_ref, vo_ref = o_refs[4 * n_i:4 * n_i + 4]
            for idx, g in pieces[name]:
                go_ref[idx] = g
                d_ref[idx], mo_ref[idx], vo_ref[idx] = _adam_math(w_ref[idx], g, m_ref[idx], v_ref[idx])

    flat_params = [a for n in names for a in allp[n]]
    out_shape = [_sds(allp[n][0].shape, F32) for n in names for _ in range(4)] + [_sds((1, 1), F32)]
    outs = pl.pallas_call(
        body, name="adamw_small", out_shape=out_shape,
        in_specs=[_whole()] * (n_g + len(flat_params)), out_specs=[_whole()] * len(out_shape),
        compiler_params=_cparams(),
    )(*gathered, *reduced, *flat_params)
    return {n: outs[4 * i:4 * i + 4] for i, n in enumerate(names)}, outs[-1]


def _my_pos():
    return lax.axis_index("x"), lax.axis_index("y"), lax.axis_index("c")


def _flip(pos, k):
    x, y, c = pos
    return (1 - x if k & 4 else x, 1 - y if k & 2 else y, 1 - c if k & 1 else c)


def _dev_index(pos):
    x, y, c = pos
    return 4 * x + 2 * y + c


def _all_gather_small(ins, outs, send_sems, recv_sems):
    n = len(ins)
    me = _my_pos()

    def slot(a, pos):
        rows = ins[a].shape[0]
        return outs[a].at[pl.ds(pl.multiple_of(_dev_index(pos) * rows, SUBLANES), rows), :]

    def copy(a, k, block):
        return pltpu.make_async_remote_copy(
            src_ref=ins[a], dst_ref=slot(a, block), send_sem=send_sems.at[a, k - 1], recv_sem=recv_sems.at[a, k - 1],
            device_id=_flip(me, k), device_id_type=MESH)

    sends = [copy(a, k, me) for a in range(n) for k in range(1, N_DEV)]
    for cp in sends:
        cp.start()
    for a in range(n):
        rows = ins[a].shape[0]
        outs[a][pl.ds(pl.multiple_of(_dev_index(me) * rows, SUBLANES), rows), :] = ins[a][...]
    for a in range(n):
        for k in range(1, N_DEV):
            copy(a, k, _flip(me, k)).wait_recv()
    for cp in sends:
        cp.wait_send()


def _prologue(c8, cw8, fcw8, w_ada, b_ada, carry):
    cols = w_ada.shape[1]

    def body(c_ref, cw_ref, fcw_ref, w_ref, b_ref, call_ref, cwall_ref, fcwall_ref, modall_ref, mod_scr,
             s1, r1, s2, r2, start_carry):
        _all_gather_small([c_ref, cw_ref, fcw_ref], [call_ref, cwall_ref, fcwall_ref], s1, r1)
        start_carry()
        cv = _row_of_each(call_ref, 0)
        ca = cv * _sigmoid(cv)
        b_cols = _my_columns(b_ref[...], cols, _dev_index(_my_pos()))
        mod_scr[...] = jnp.dot(ca, w_ref[...], preferred_element_type=F32, precision=lax.Precision.HIGHEST) + b_cols
        _all_gather_small([mod_scr], [modall_ref], s2, r2)

    sem = lambda n: pltpu.SemaphoreType.DMA((n, N_DEV - 1))
    return _call(
        body, "prologue", (1,), in_specs=[_whole()] * 5, out_specs=[_whole()] * 4,
        out_shape=[_sds((N_DEV * SUBLANES, a.shape[1]), F32) for a in (c8, cw8, fcw8)]
        + [_sds((N_DEV * N_DEV, cols), F32)],
        scratch=[pltpu.VMEM((N_DEV, cols), F32), sem(3), sem(3), sem(1), sem(1)],
        args=(c8, cw8, fcw8, w_ada, b_ada), carry=carry, body_starts_carry=True)


def _reduce_small(gath, red, carry=None):
    n_g, n_r = len(gath), len(red)
    chip_flips = (4, 2, 6)

    def body(*refs, start_carry):
        g_in, r_in = refs[:n_g], refs[n_g:n_g + n_r]
        g_out, r_out = refs[n_g + n_r:2 * n_g + n_r], refs[2 * n_g + n_r:2 * (n_g + n_r)]
        scr = refs[2 * (n_g + n_r):]
        sib, land = scr[:n_r], scr[n_r:2 * n_r]
        g_send, g_recv, s_send, s_recv, i_send, i_recv, f_send, f_recv = scr[2 * n_r:]
        me = _my_pos()
        c = me[2]
        sibling = _flip(me, 1)

        def slot(a, pos):
            return g_out[a].at[pl.ds(pl.multiple_of(_dev_index(pos) * SUBLANES, SUBLANES), SUBLANES), :]

        def gcopy(a, k):
            return pltpu.make_async_remote_copy(
                src_ref=g_in[a], dst_ref=slot(a, me), send_sem=g_send.at[a, k - 1], recv_sem=g_recv.at[a, k - 1],
                device_id=_flip(me, k), device_id_type=MESH)

        def scopy(a):
            return pltpu.make_async_remote_copy(
                src_ref=r_in[a], dst_ref=sib[a], send_sem=s_send.at[a], recv_sem=s_recv.at[a],
                device_id=sibling, device_id_type=MESH)

        def icopy(a, j):
            return pltpu.make_async_remote_copy(
                src_ref=r_out[a], dst_ref=land[a].at[j], send_sem=i_send.at[a, j], recv_sem=i_recv.at[a, j],
                device_id=_flip(me, chip_flips[j]), device_id_type=MESH)

        def fcopy(a, j):
            return pltpu.make_async_remote_copy(
                src_ref=land[a].at[j], dst_ref=land[a].at[j], send_sem=f_send.at[a, j], recv_sem=f_recv.at[a, j],
                device_id=sibling, device_id_type=MESH)

        gathers = [gcopy(a, k) for a in range(n_g) for k in range(1, N_DEV)]
        swaps = [scopy(a) for a in range(n_r)]
        for cp in gathers + swaps:
            cp.start()
        for a in range(n_g):
            g_out[a][pl.ds(pl.multiple_of(_dev_index(me) * SUBLANES, SUBLANES), SUBLANES), :] = g_in[a][...]
        for a in range(n_r):
            swaps[a].wait_recv()
            r_out[a][...] = r_in[a][...] + sib[a][...]

        for core in range(2):
            @pl.when(c == core)
            def _():
                for a in range(core, n_r, 2):
                    for j in range(3):
                        icopy(a, j).start()

        start_carry()

        for core in range(2):
            mine = [a for a in range(n_r) if a % 2 == core]
            theirs = [a for a in range(n_r) if a % 2 != core]

            @pl.when(c == core)
            def _():
                out = [icopy(a, j) for a in mine for j in range(3)]
                fwd = []
                for a in mine:
                    for j in range(3):
                        icopy(a, j).wait_recv()
                        cp = fcopy(a, j)
                        cp.start()
                        fwd.append(cp)
                for a in theirs:
                    for j in range(3):
                        fcopy(a, j).wait_recv()
                for cp in out + fwd:
                    cp.wait_send()

        for a in range(n_r):
            r_out[a][...] = (r_out[a][...] + land[a][1]) + (land[a][0] + land[a][2])
        for a in range(n_g):
            for k in range(1, N_DEV):
                pltpu.make_async_remote_copy(
                    src_ref=g_in[a], dst_ref=slot(a, _flip(me, k)), send_sem=g_send.at[a, k - 1],
                    recv_sem=g_recv.at[a, k - 1], device_id=_flip(me, k), device_id_type=MESH).wait_recv()
        for cp in gathers + swaps:
            cp.wait_send()

    shapes = [tuple(a.shape) for a in red]
    outs, carried = _call(
        body, "reduce_small", (1,), in_specs=[_whole()] * (n_g + n_r), out_specs=[_whole()] * (n_g + n_r),
        out_shape=[_sds((N_DEV * SUBLANES, a.shape[1]), F32) for a in gath] + [_sds(s, F32) for s in shapes],
        scratch=[pltpu.VMEM(s, F32) for s in shapes] + [pltpu.VMEM((3,) + s, F32) for s in shapes]
        + [pltpu.SemaphoreType.DMA((n_g, N_DEV - 1)), pltpu.SemaphoreType.DMA((n_g, N_DEV - 1)),
           pltpu.SemaphoreType.DMA((n_r,)), pltpu.SemaphoreType.DMA((n_r,)),
           pltpu.SemaphoreType.DMA((n_r, 3)), pltpu.SemaphoreType.DMA((n_r, 3)),
           pltpu.SemaphoreType.DMA((n_r, 3)), pltpu.SemaphoreType.DMA((n_r, 3))],
        args=tuple(gath) + tuple(red), carry=carry, body_starts_carry=True)
    return (outs[:n_g], outs[n_g:]), carried


STACKED = "stacked"


def _region(ref, shard_shape, col_sharded, pos):
    r, cdim = shard_shape
    d = _dev_index(pos)
    if col_sharded == STACKED:
        return ref.at[d]
    if col_sharded:
        return ref.at[:, pl.ds(pl.multiple_of(d * cdim, LANES), cdim)]
    return ref.at[pl.ds(pl.multiple_of(d * r, 2 * SUBLANES), r), :]


def _gather_carry(shards, col_sharded):
    n_w = len(shards)
    shapes = [tuple(s.shape) for s in shards]
    full_shapes = [(N_DEV,) + s if cs == STACKED else (s[0], s[1] * N_DEV) if cs else (s[0] * N_DEV, s[1])
                   for s, cs in zip(shapes, col_sharded)]

    def tools(out_refs, scr):
        send_sems, recv_sems = scr[n_w], scr[n_w + 1]
        me = _my_pos()
        x, y, c = me
        sibling = (x, y, 1 - c)
        chips = [(1 - x, y), (x, 1 - y), (1 - x, 1 - y)]

        def region(w, pos):
            return _region(out_refs[w], shapes[w], col_sharded[w], pos)

        def copy(w, k, block, to, src=None):
            return pltpu.make_async_remote_copy(
                src_ref=region(w, block) if src is None else src, dst_ref=region(w, block),
                send_sem=send_sems.at[w, k], recv_sem=recv_sems.at[w, k], device_id=to, device_id_type=MESH)

        def first(w):
            return [copy(w, 0, me, sibling, src=scr[w])] + [
                copy(w, 1 + j, me, (*chip, c), src=scr[w]) for j, chip in enumerate(chips)]

        def mine(w):
            return pltpu.make_async_copy(scr[w], region(w, me), scr[n_w + 2].at[w])

        return me, c, sibling, chips, copy, first, mine

    def start(ins, outs, scr):
        _, _, _, _, _, first, mine = tools(outs, scr)
        for w in range(n_w):
            scr[w][...] = ins[w][...].astype(BF16)
            for cp in first(w) + [mine(w)]:
                cp.start()

    def finish(ins, outs, scr):
        me, c, sibling, chips, copy, first, mine = tools(outs, scr)
        passed = []
        for w in range(n_w):
            for j, chip in enumerate(chips):
                copy(w, 1 + j, (*chip, c), me).wait_recv()
                fwd = copy(w, 4 + j, (*chip, c), sibling)
                fwd.start()
                passed.append(fwd)
        for w in range(n_w):
            copy(w, 0, sibling, me).wait_recv()
            for j, chip in enumerate(chips):
                copy(w, 4 + j, (*chip, 1 - c), me).wait_recv()
        for w in range(n_w):
            for cp in first(w):
                cp.wait_send()
            mine(w).wait()
        for cp in passed:
            cp.wait_send()

    return _Carry(
        inputs=list(shards), in_specs=[_whole()] * n_w,
        out_shape=[_sds(s, BF16) for s in full_shapes], out_specs=[_any()] * n_w,
        scratch=[pltpu.VMEM(s, BF16) for s in shapes]
        + [pltpu.SemaphoreType.DMA((n_w, N_DEV - 1)), pltpu.SemaphoreType.DMA((n_w, N_DEV - 1)),
           pltpu.SemaphoreType.DMA((n_w,))],
        start=start, finish=finish)


def _scatter_carry(grads_bf, shard_shapes, col_sharded, relations):
    n_w = len(grads_bf)
    shapes = [tuple(s) for s in shard_shapes]

    def copies(ins, outs, scr):
        send_sems, recv_sems = scr
        me = _my_pos()
        out = []
        for w in range(n_w):
            for i, k in enumerate(relations[w]):
                peer = _flip(me, k)
                out.append(pltpu.make_async_remote_copy(
                    src_ref=_region(ins[w], shapes[w], col_sharded[w], peer), dst_ref=outs[w].at[i],
                    send_sem=send_sems.at[w, i], recv_sem=recv_sems.at[w, i],
                    device_id=peer, device_id_type=MESH))
        return out

    def start(ins, outs, scr):
        for cp in copies(ins, outs, scr):
            cp.start()

    def finish(ins, outs, scr):
        cps = copies(ins, outs, scr)
        for cp in cps:
            cp.wait_recv()
        for cp in cps:
            cp.wait_send()

    return _Carry(
        inputs=list(grads_bf), in_specs=[_any()] * n_w,
        out_shape=[_sds((len(r),) + s, BF16) for r, s in zip(relations, shapes)], out_specs=[_any()] * n_w,
        scratch=[pltpu.SemaphoreType.DMA((n_w, N_DEV - 1)), pltpu.SemaphoreType.DMA((n_w, N_DEV - 1))],
        start=start, finish=finish)


def _block_diag(w):
    eye = jnp.eye(N_HEADS, dtype=w.dtype)
    return (eye[:, None, :, None] * w[:, :, None, :]).reshape(N_HEADS * HEAD_DIM, N_HEADS * HEAD_DIM)


def _pad_rows(a):
    return jnp.pad(a, ((0, SUBLANES - a.shape[0]), (0, 0)))


def _columns_from_devices(gathered, rows):
    w = gathered.shape[1]
    return gathered.reshape(N_DEV, SUBLANES, w)[:, :rows].transpose(1, 0, 2).reshape(rows, N_DEV * w)


def _local_step(x2, target, mod, w_in_f, w_full, conv_w_full, ffn_cw_full,
                g_mix_pre, g_mix_post, conv_b, w_rgate, b_rgate, w_igate, b_igate, lru_a, v_norm_g, v_norm_b,
                w_spatial, b_spatial, g_lru_out, g_gmlp_out, g_ffn_pre, g_ffn_post, ffn_conv_b,
                gather=None, scatter=None):
    sh_m, sc_m, gt_m, sh_f, sc_f, gt_f = [mod[k] for k in range(N_MOD)]
    wr_bd = _block_diag(w_rgate[0]).astype(BF16)
    wi_bd = _block_diag(w_igate[0]).astype(BF16)
    b_r = b_rgate.reshape(1, LRU_W)
    b_i = b_igate.reshape(1, LRU_W)
    b_sp_t = b_spatial[0].T
    w_sp_t = jnp.swapaxes(w_spatial[0], 1, 2)

    def arriving(*names):
        return gather(*names) if gather else None

    near, far = (1, 2, 3, 4, 5), (6, 7)

    def leaving(*parts):
        return scatter(parts) if scatter else None

    def received(recv, parts, outs):
        for (name, _, _), out in zip(parts, outs):
            recv.setdefault(name, []).append(out)

    mix_params = (conv_w_full, conv_b, wr_bd, wi_bd, b_r, b_i, lru_a, v_norm_g, v_norm_b)
    w_out_f = w_full["w_out"]
    (z, h, ycat, hl, y, x1, h2), got = _mix_fwd(
        x2, sh_m, sc_m, g_mix_pre, w_in_f, *mix_params, w_spatial[0], b_sp_t, g_lru_out, g_gmlp_out,
        w_out_f, g_mix_post, gt_m, g_ffn_pre, sc_f, sh_f, carry=arriving("w_up"))
    w_up_f = got[0] if gather else w_full["w_up"]
    (up_pre, up, act), got = _ffn_fwd(h2, w_up_f, ffn_cw_full, ffn_conv_b, carry=arriving("w_down"))
    w_down_f = got[0] if gather else w_full["w_down"]
    d_y2, dout, loss_acc, vs_ffn = _ffn_tail(act, w_down_f, x1, gt_f, g_ffn_post, target)

    recv = {}
    gw_down, _ = _wgrad(act, d_y2, FF_CHUNK_W, "wgrad_down", by_rows=True)
    parts = [("w_down", gw_down[1], near + far)]
    (d_up, cs_ffn), got = _ffn_bwd(d_y2, up_pre, up, ffn_cw_full, w_down_f, carry=leaving(*parts))
    received(recv, parts, got)
    gw_up, _ = _wgrad(h2, d_up, FF_CHUNK_W, "wgrad_up")
    parts = [("w_up", gw_up[1], near)]
    (d_x1, d_y, d_ycat, vs_up), got = _up_bwd(
        d_up, w_up_f, x1, dout, y, w_out_f, g_ffn_pre, sc_f, g_mix_post, gt_m, carry=leaving(*parts))
    received(recv, parts, got)
    gw_out, _ = _wgrad(ycat, d_y, D_MODEL // 4, "wgrad_out")
    parts = [("w_up", gw_up[1], far), ("w_out", gw_out[1], near + far)]
    (d_z, vs_mix, dcw, d_wr, d_wi, d_ws, d_bs), got = _mix_bwd(
        d_ycat, z, hl, *mix_params, w_spatial[0], w_sp_t, b_sp_t, g_lru_out, g_gmlp_out, carry=leaving(*parts))
    received(recv, parts, got)
    gw_in, _ = _wgrad(h, d_z, IN_COLS // 4, "wgrad_in")
    (grad_x, vs_in), _ = _in_bwd(d_z, w_in_f, x2, d_x1, g_mix_pre, sc_m)
    pending = [("w_in", gw_in[1], near + far)]
    recv["w_in"] = []

    gath = [vs_in, vs_up, vs_ffn, loss_acc]
    red = [cs_ffn, vs_mix, dcw, d_wr, d_wi, d_ws.reshape(N_GROUPS * POS_BLOCK, POS_BLOCK), d_bs]
    return dict(grad_x=grad_x, gath=gath, red=red, recv=recv, pending=pending,
                w_in=gw_in, w_out=gw_out, w_up=gw_up, w_down=gw_down)


def kernel(x, c, w_ada, b_ada, g_mix_pre, g_mix_post, w_in, conv_w, conv_b, w_rgate, b_rgate, w_igate, b_igate, lru_a, v_norm_g, v_norm_b, w_spatial, b_spatial, g_lru_out, g_gmlp_out, w_out, g_ffn_pre, g_ffn_post, w_up, ffn_conv_w, ffn_conv_b, w_down, loss_target, m_w_ada, m_b_ada, m_g_mix_pre, m_g_mix_post, m_w_in, m_conv_w, m_conv_b, m_w_rgate, m_b_rgate, m_w_igate, m_b_igate, m_lru_a, m_v_norm_g, m_v_norm_b, m_w_spatial, m_b_spatial, m_g_lru_out, m_g_gmlp_out, m_w_out, m_g_ffn_pre, m_g_ffn_post, m_w_up, m_ffn_conv_w, m_ffn_conv_b, m_w_down, v_w_ada, v_b_ada, v_g_mix_pre, v_g_mix_post, v_w_in, v_conv_w, v_conv_b, v_w_rgate, v_b_rgate, v_w_igate, v_b_igate, v_lru_a, v_v_norm_g, v_v_norm_b, v_w_spatial, v_b_spatial, v_g_lru_out, v_g_gmlp_out, v_w_out, v_g_ffn_pre, v_g_ffn_post, v_w_up, v_ffn_conv_w, v_ffn_conv_b, v_w_down):
    me = _dev_index(_my_pos())
    ada_cols = w_ada.shape[-1]

    big_w = dict(w_in=(w_in, m_w_in, v_w_in, True), w_out=(w_out, m_w_out, v_w_out, False),
                 w_up=(w_up, m_w_up, v_w_up, True), w_down=(w_down, m_w_down, v_w_down, False))

    def gather(*names):
        return _gather_carry([big_w[n][0][0] for n in names], [STACKED if n == "w_up" else big_w[n][3] for n in names])

    def scatter(parts):
        return _scatter_carry([g for _, g, _ in parts], [big_w[n][0].shape[1:] for n, _, _ in parts],
                              [big_w[n][3] for n, _, _ in parts], [rel for _, _, rel in parts])

    (c_all, cw_all, fcw_all, mod_all), (w_in_f, w_out_f) = _prologue(
        jnp.broadcast_to(c, (SUBLANES, D_MODEL)), _pad_rows(conv_w[0]), _pad_rows(ffn_conv_w[0]), w_ada[0], b_ada,
        carry=gather("w_in", "w_out"))
    conv_w_full = _columns_from_devices(cw_all, LRU_CONV_K)
    ffn_cw_full = _columns_from_devices(fcw_all, FFN_CONV_K)
    mod = lax.dynamic_index_in_dim(mod_all.reshape(N_DEV, N_DEV, ada_cols), me, axis=1, keepdims=False)
    mod = mod.reshape(N_MOD, 1, D_MODEL)

    loc = _local_step(x[0], loss_target[0], mod, w_in_f, dict(w_out=w_out_f), conv_w_full, ffn_cw_full,
                      g_mix_pre, g_mix_post, conv_b, w_rgate, b_rgate, w_igate, b_igate, lru_a, v_norm_g, v_norm_b,
                      w_spatial, b_spatial, g_lru_out, g_gmlp_out, g_ffn_pre, g_ffn_post, ffn_conv_b,
                      gather=gather, scatter=scatter)
    grad_x = loc["grad_x"]

    (gathered, reduced), got = _reduce_small(loc["gath"], loc["red"], carry=scatter(loc["pending"]))
    for (name, _, _), out in zip(loc["pending"], got):
        loc["recv"][name].append(out)

    results = {}
    for name, (w_, m_, v_, cs) in big_w.items():
        results[name] = _adamw_sum(w_, loc[name][0], loc["recv"][name], m_, v_, cs, "adamw_" + name)

    params = dict(
        b_ada=(b_ada, m_b_ada, v_b_ada), g_mix_pre=(g_mix_pre, m_g_mix_pre, v_g_mix_pre),
        g_mix_post=(g_mix_post, m_g_mix_post, v_g_mix_post), conv_b=(conv_b, m_conv_b, v_conv_b),
        w_rgate=(w_rgate, m_w_rgate, v_w_rgate), b_rgate=(b_rgate, m_b_rgate, v_b_rgate),
        w_igate=(w_igate, m_w_igate, v_w_igate), b_igate=(b_igate, m_b_igate, v_b_igate),
        lru_a=(lru_a, m_lru_a, v_lru_a), v_norm_g=(v_norm_g, m_v_norm_g, v_v_norm_g),
        v_norm_b=(v_norm_b, m_v_norm_b, v_v_norm_b), w_spatial=(w_spatial, m_w_spatial, v_w_spatial),
        b_spatial=(b_spatial, m_b_spatial, v_b_spatial), g_lru_out=(g_lru_out, m_g_lru_out, v_g_lru_out),
        g_gmlp_out=(g_gmlp_out, m_g_gmlp_out, v_g_gmlp_out), g_ffn_pre=(g_ffn_pre, m_g_ffn_pre, v_g_ffn_pre),
        g_ffn_post=(g_ffn_post, m_g_ffn_post, v_g_ffn_post), ffn_conv_b=(ffn_conv_b, m_ffn_conv_b, v_ffn_conv_b))
    conv_params = dict(conv_w=(conv_w, m_conv_w, v_conv_w), ffn_conv_w=(ffn_conv_w, m_ffn_conv_w, v_ffn_conv_w))
    small_results, loss = _adamw_small(gathered, reduced, params, conv_params)
    results.update(small_results)
    loss = loss.reshape(())

    results["w_ada"] = _adamw_wada(c_all, gathered[0], gathered[1], gathered[2], w_ada, m_w_ada, v_w_ada)

    order = ["w_ada", "b_ada", "g_mix_pre", "g_mix_post", "w_in", "conv_w", "conv_b", "w_rgate", "b_rgate", "w_igate",
             "b_igate", "lru_a", "v_norm_g", "v_norm_b", "w_spatial", "b_spatial", "g_lru_out", "g_gmlp_out", "w_out",
             "g_ffn_pre", "g_ffn_post", "w_up", "ffn_conv_w", "ffn_conv_b", "w_down"]
    outs = [loss, grad_x[None]]
    for kind in range(4):
        outs += [results[n][kind] for n in order]
    return tuple(outs)
```

```python
import functools

import jax
import jax.numpy as jnp
from jax import lax
from jax.experimental import pallas as pl
from jax.experimental.pallas import tpu as pltpu

F32 = jnp.float32
BF16 = jnp.bfloat16

D_MODEL = 1024
LRU_W = 512
GMLP_W = 512
N_HEADS = 8
HEAD_DIM = 64
N_GROUPS = 4
POS_BLOCK = 128
CHUNK = 64
IN_COLS = 2048
D_FF = 3072
N_MOD = 6
N_DEV = 8
EPS = 1e-6
LRU_C = 8.0
LRU_CONV_K = 4
FFN_CONV_K = 3

ADAM_LR = 0.001
ADAM_B1 = 0.9
ADAM_B2 = 0.999
ADAM_EPS = 1e-08
ADAM_WD = 0.01
ADAM_STEP = 10

LANES = 128
SUBLANES = 8
TT_BIG = 512
TT_MIX = 256
FF_CW = 512
VMEM_LIMIT = 56 * 1024 * 1024

MESH = pl.DeviceIdType.MESH


def _sds(shape, dtype):
    return jax.ShapeDtypeStruct(shape, dtype)


def _cparams(sem=None):
    return pltpu.CompilerParams(dimension_semantics=sem, vmem_limit_bytes=VMEM_LIMIT)


def _whole():
    return pl.BlockSpec(memory_space=pltpu.VMEM)


def _const(shape):
    nd = len(shape)
    return pl.BlockSpec(shape, lambda *_: (0,) * nd)


def _any():
    return pl.BlockSpec(memory_space=pl.ANY)


class _Carry:
    def __init__(self, inputs, in_specs, out_shape, out_specs, scratch, start, finish):
        self.inputs, self.in_specs, self.out_shape, self.out_specs = inputs, in_specs, out_shape, out_specs
        self.scratch, self.start, self.finish = scratch, start, finish


def _call(body, name, grid, in_specs, out_specs, out_shape, scratch, args, carry=None, body_starts_carry=False):
    n_in, n_out, n_scr = len(in_specs), len(out_specs), len(scratch)
    c_in = len(carry.in_specs) if carry else 0
    c_out = len(carry.out_specs) if carry else 0

    def full_body(*refs):
        ins = refs[:n_in]
        c_ins = refs[n_in:n_in + c_in]
        outs = refs[n_in + c_in:n_in + c_in + n_out]
        c_outs = refs[n_in + c_in + n_out:n_in + c_in + n_out + c_out]
        scr = refs[n_in + c_in + n_out + c_out:n_in + c_in + n_out + c_out + n_scr]
        c_scr = refs[n_in + c_in + n_out + c_out + n_scr:]
        if carry:
            first = functools.reduce(lambda a, b: a & b, [pl.program_id(d) == 0 for d in range(len(grid))])
            last = functools.reduce(lambda a, b: a & b, [pl.program_id(d) == g - 1 for d, g in enumerate(grid)])

        if carry and not body_starts_carry:
            @pl.when(first)
            def _():
                carry.start(c_ins, c_outs, c_scr)

        if body_starts_carry:
            body(*ins, *outs, *scr, start_carry=(lambda: carry.start(c_ins, c_outs, c_scr)) if carry else (lambda: None))
        else:
            body(*ins, *outs, *scr)
        if carry:
            @pl.when(last)
            def _():
                carry.finish(c_ins, c_outs, c_scr)

    res = pl.pallas_call(
        full_body, name=name, grid=grid,
        in_specs=list(in_specs) + (list(carry.in_specs) if carry else []),
        out_specs=list(out_specs) + (list(carry.out_specs) if carry else []),
        out_shape=list(out_shape) + (list(carry.out_shape) if carry else []),
        scratch_shapes=list(scratch) + (list(carry.scratch) if carry else []),
        compiler_params=_cparams(("arbitrary",) * len(grid)),
    )(*args, *(carry.inputs if carry else []))
    return res[:n_out], res[n_out:]


GELU_C0 = 0.7978845608028654
GELU_C1 = GELU_C0 * 0.044715


def _gelu(x):
    t = jnp.tanh(x * (GELU_C0 + GELU_C1 * (x * x)))
    hx = 0.5 * x
    return hx + hx * t


def _gelu_and_grad(x):
    x2 = x * x
    t = jnp.tanh(x * (GELU_C0 + GELU_C1 * x2))
    hx = 0.5 * x
    g = hx + hx * t
    dg = (0.5 + 0.5 * t) + hx * (1.0 - t * t) * (GELU_C0 + 3.0 * GELU_C1 * x2)
    return g, dg


def _sigmoid(x):
    return 1.0 / (1.0 + jnp.exp(-x))


def _softplus(x):
    return jnp.maximum(x, 0.0) + jnp.log1p(jnp.exp(-jnp.abs(x)))


def _neg_expm1(x):
    series = -x * (1.0 + x * (0.5 + x * (1.0 / 6.0 + x * (1.0 / 24.0 + x * (1.0 / 120.0)))))
    return jnp.where(x > -0.1, series, 1.0 - jnp.exp(x))


def _dot(a, b):
    return jnp.dot(a.astype(BF16), b.astype(BF16), preferred_element_type=F32)


def _dot_nt(a, b):
    return lax.dot_general(a.astype(BF16), b.astype(BF16), (((1,), (1,)), ((), ())), preferred_element_type=F32)


def _dot_tn(a, b):
    return lax.dot_general(a.astype(BF16), b.astype(BF16), (((0,), (0,)), ((), ())), preferred_element_type=F32)


def _rows(shape):
    return lax.broadcasted_iota(jnp.int32, shape, 0)


def _shift_down(cur, prev8, s):
    if s == 0:
        return cur
    n = cur.shape[0]
    r = pltpu.roll(cur, s, 0)
    p = pltpu.roll(prev8, s, 0)
    top = jnp.where(_rows(p.shape) < s, p, r[0:SUBLANES])
    if n == SUBLANES:
        return top
    return jnp.concatenate([top, r[SUBLANES:]], axis=0)


def _shift_up(cur, next8, s):
    if s == 0:
        return cur
    n = cur.shape[0]
    r = pltpu.roll(cur, n - s, 0)
    q = pltpu.roll(next8, SUBLANES - s, 0)
    bot = jnp.where(_rows(q.shape) >= SUBLANES - s, q, r[n - SUBLANES:])
    if n == SUBLANES:
        return bot
    return jnp.concatenate([r[:n - SUBLANES], bot], axis=0)


def _scan_fwd(a, b):
    n = a.shape[0]
    rows = _rows(a.shape)
    s = 1
    while s < n:
        a_s = pltpu.roll(a, s, 0)
        b_s = pltpu.roll(b, s, 0)
        m = rows >= s
        b = jnp.where(m, a * b_s + b, b)
        a = jnp.where(m, a * a_s, a)
        s *= 2
    return a, b


def _scan_rev(a, b):
    n = a.shape[0]
    rows = _rows(a.shape)
    s = 1
    while s < n:
        a_s = pltpu.roll(a, n - s, 0)
        b_s = pltpu.roll(b, n - s, 0)
        m = rows < n - s
        b = jnp.where(m, b + a * b_s, b)
        a = jnp.where(m, a * a_s, a)
        s *= 2
    return a, b


def _rms(x):
    r = lax.rsqrt(jnp.mean(x * x, axis=-1, keepdims=True) + EPS)
    return x * r, r


def _rms_bwd(d_n, n, r):
    return r * (d_n - n * jnp.mean(d_n * n, axis=-1, keepdims=True))


def _colsum(x):
    return jnp.sum(x, axis=0, keepdims=True)


ROW_PIECE = 256


def _row_pieces(tt):
    return [slice(r, r + min(ROW_PIECE, tt)) for r in range(0, tt, min(ROW_PIECE, tt))]


def _lru_gates(xc, wr_ref, wi_ref, br, bi, sp_a):
    r = _sigmoid(_dot(xc, wr_ref[...]) + br)
    i = _sigmoid(_dot(xc, wi_ref[...]) + bi)
    la = -LRU_C * r * sp_a
    a = jnp.exp(la)
    mult = jnp.sqrt(_neg_expm1(2.0 * la))
    return r, i, a, mult


def _lru_conv(lx, prev8, cw_ref, cb):
    xc = cb + cw_ref[LRU_CONV_K - 1:LRU_CONV_K, :] * lx
    taps = []
    for k in range(LRU_CONV_K - 1):
        tap = _shift_down(lx, prev8, LRU_CONV_K - 1 - k)
        taps.append(tap)
        xc = xc + cw_ref[k:k + 1, :] * tap
    return xc, taps


def _ws_mask(transposed=False):
    i = lax.broadcasted_iota(jnp.int32, (POS_BLOCK, POS_BLOCK), 0)
    j = lax.broadcasted_iota(jnp.int32, (POS_BLOCK, POS_BLOCK), 1)
    if transposed:
        i, j = j, i
    return (j // CHUNK) <= (i // CHUNK)


def _gmlp_v(gv, vg, vb):
    av, dav = _gelu_and_grad(gv)
    mu = jnp.mean(av, axis=-1, keepdims=True)
    cen = av - mu
    rs = lax.rsqrt(jnp.mean(cen * cen, axis=-1, keepdims=True) + EPS)
    vhat = cen * rs
    return vhat * vg + vb, vhat, rs, dav


def _mix_fwd(x, sh, sc, g_pre, w_in, conv_w, conv_b, wr_bd, wi_bd, b_r, b_i, lru_a, vn_g, vn_b, w_sp, b_sp_t,
             g_lru, g_gmlp, w_out, g_post, gt_m, g_ffn_pre, sc_f, sh_f, carry=None):
    s_len = x.shape[0]
    tt = min(TT_MIX, s_len)
    nblk = tt // POS_BLOCK

    def body(x_ref, sh_ref, sc_ref, g_ref, w_ref, cw_ref, cb_ref, wr_ref, wi_ref, br_ref, bi_ref, la_ref, vg_ref,
             vb_ref, ws_ref, bst_ref, gl_ref, gg_ref, wo_ref, gp_ref, gtm_ref, g2_ref, scf_ref, shf_ref,
             z_ref, h_ref, y_ref, hl_ref, yo_ref, x1_ref, h2_ref, prev8, hcar):
        i = pl.program_id(0)

        @pl.when(i == 0)
        def _():
            prev8[...] = jnp.zeros_like(prev8)
            hcar[...] = jnp.zeros_like(hcar)

        n_x, _ = _rms(x_ref[...])
        h = (n_x * g_ref[...] * (1.0 + sc_ref[...]) + sh_ref[...]).astype(BF16)
        h_ref[...] = h
        z_ref[...] = jnp.dot(h, w_ref[...], preferred_element_type=F32)

        lx = z_ref[:, 0:LRU_W]
        gate = z_ref[:, LRU_W:2 * LRU_W]
        gu = z_ref[:, 2 * LRU_W:2 * LRU_W + GMLP_W]
        gv = z_ref[:, 2 * LRU_W + GMLP_W:]

        xc, _ = _lru_conv(lx, prev8[...], cw_ref, cb_ref[...])
        prev8[...] = lx[tt - SUBLANES:]
        sp_a = _softplus(-la_ref[...])
        _, ig, a, mult = _lru_gates(xc, wr_ref, wi_ref, br_ref[...], bi_ref[...], sp_a)
        bx = mult * (ig * xc)
        a_cum, b_cum = _scan_fwd(a, bx)
        hl = a_cum * hcar[0:1, :] + b_cum
        hcar[...] = jnp.broadcast_to(hl[tt - 1:tt, :], hcar.shape)
        hl_ref[...] = hl
        y_lru = hl * _gelu(gate)
        n_l, _ = _rms(y_lru)
        y_ref[:, 0:LRU_W] = (n_l * gl_ref[...]).astype(BF16)

        u = _gelu(gu)
        v, _, _, _ = _gmlp_v(gv, vg_ref[...], vb_ref[...])
        mask = _ws_mask()
        sp_parts = []
        for nb in range(nblk):
            row = []
            for g in range(N_GROUPS):
                wsm = jnp.where(mask, ws_ref[g], 0.0)
                vblk = v[nb * POS_BLOCK:(nb + 1) * POS_BLOCK, g * LANES:(g + 1) * LANES]
                row.append(_dot(wsm, vblk) + bst_ref[:, g:g + 1])
            sp_parts.append(jnp.concatenate(row, axis=1))
        sp = jnp.concatenate(sp_parts, axis=0) if nblk > 1 else sp_parts[0]
        n_g, _ = _rms(u * sp)
        y_ref[:, LRU_W:] = (n_g * gg_ref[...]).astype(BF16)

        y = jnp.dot(y_ref[...], wo_ref[...], preferred_element_type=F32)
        yo_ref[...] = y
        n_y, _ = _rms(y)
        x1 = x_ref[...] + gtm_ref[...] * (n_y * gp_ref[...])
        x1_ref[...] = x1
        n1, _ = _rms(x1)
        h2_ref[...] = (n1 * g2_ref[...] * (1.0 + scf_ref[...]) + shf_ref[...]).astype(BF16)

    row = lambda c: pl.BlockSpec((tt, c), lambda i: (i, 0))
    v512 = _const((1, LRU_W))
    vec = _const((1, D_MODEL))
    return _call(
        body, "mix_fwd", (s_len // tt,),
        in_specs=[row(D_MODEL), vec, vec, vec, _whole(),
                  _const((LRU_CONV_K, LRU_W)), v512, _whole(), _whole(), v512, v512, v512, v512, v512,
                  _whole(), _whole(), v512, v512, _whole(), vec, vec, vec, vec, vec],
        out_specs=[row(IN_COLS), row(D_MODEL), row(LRU_W + GMLP_W), row(LRU_W), row(D_MODEL), row(D_MODEL),
                   row(D_MODEL)],
        out_shape=[_sds((s_len, IN_COLS), F32), _sds((s_len, D_MODEL), BF16),
                   _sds((s_len, LRU_W + GMLP_W), BF16), _sds((s_len, LRU_W), F32),
                   _sds((s_len, D_MODEL), F32), _sds((s_len, D_MODEL), F32), _sds((s_len, D_MODEL), BF16)],
        scratch=[pltpu.VMEM((SUBLANES, LRU_W), F32), pltpu.VMEM((SUBLANES, LRU_W), F32)],
        args=(x, sh, sc, g_pre, w_in, conv_w, conv_b, wr_bd, wi_bd, b_r, b_i, lru_a, vn_g, vn_b, w_sp, b_sp_t,
              g_lru, g_gmlp, w_out, g_post, gt_m, g_ffn_pre, sc_f, sh_f), carry=carry)


FF_CHUNKS = N_DEV // 2
FF_CHUNK_W = D_FF // FF_CHUNKS


def _ffn_fwd(h2, w_up3, ffn_cw, ffn_cb, carry=None):
    s_len = h2.shape[0]
    tt = min(TT_MIX, s_len)
    nc, cw = FF_CHUNKS, FF_CHUNK_W

    def body(h2_ref, wu_ref, cwg_ref, cwv_ref, cbg_ref, cbv_ref, up_ref, upc_ref, act_ref, prev):
        i = pl.program_id(0)
        c = pl.program_id(1)

        @pl.when(i == 0)
        def _():
            prev[c] = jnp.zeros((2, SUBLANES, cw), F32)

        h2 = h2_ref[...]
        ug_pre = jnp.dot(h2, wu_ref[c], preferred_element_type=F32)
        uv_pre = jnp.dot(h2, wu_ref[nc + c], preferred_element_type=F32)
        up_ref[0] = ug_pre.astype(BF16)
        up_ref[1] = uv_pre.astype(BF16)
        ug, _ = _ffn_conv(ug_pre, prev[c, 0], cwg_ref, cbg_ref[...])
        uv, _ = _ffn_conv(uv_pre, prev[c, 1], cwv_ref, cbv_ref[...])
        prev[c, 0] = ug_pre[tt - SUBLANES:, :]
        prev[c, 1] = uv_pre[tt - SUBLANES:, :]
        upc_ref[0] = ug.astype(BF16)
        upc_ref[1] = uv.astype(BF16)
        act_ref[...] = (_gelu(ug) * uv).astype(BF16)

    chunk2 = pl.BlockSpec((2, tt, cw), lambda i, c: (0, i, c))
    ffn_cb2 = ffn_cb.reshape(1, 2 * D_FF)
    return _call(
        body, "ffn_fwd", (s_len // tt, nc),
        in_specs=[pl.BlockSpec((tt, D_MODEL), lambda i, c: (i, 0)), _whole(),
                  pl.BlockSpec((FFN_CONV_K, cw), lambda i, c: (0, c)),
                  pl.BlockSpec((FFN_CONV_K, cw), lambda i, c: (0, c + nc)),
                  pl.BlockSpec((1, cw), lambda i, c: (0, c)),
                  pl.BlockSpec((1, cw), lambda i, c: (0, c + nc))],
        out_specs=[chunk2, chunk2, pl.BlockSpec((tt, cw), lambda i, c: (i, c))],
        out_shape=[_sds((2, s_len, D_FF), BF16), _sds((2, s_len, D_FF), BF16), _sds((s_len, D_FF), BF16)],
        scratch=[pltpu.VMEM((nc, 2, SUBLANES, cw), F32)],
        args=(h2, w_up3, ffn_cw, ffn_cw, ffn_cb2, ffn_cb2), carry=carry)


def _ffn_tail(act, w_down, x1, gt_f, g_post, target):
    s_len = x1.shape[0]
    tt = min(TT_BIG, s_len)

    def body(act_ref, wd_ref, x1_ref, gtf_ref, gp_ref, tg_ref, dy2_ref, dout_ref, loss_ref, vs_ref):
        @pl.when(pl.program_id(0) == 0)
        def _():
            loss_ref[...] = jnp.zeros_like(loss_ref)
            vs_ref[...] = jnp.zeros_like(vs_ref)

        for rows in _row_pieces(tt):
            n2, r2 = _rms(jnp.dot(act_ref[rows, :], wd_ref[...], preferred_element_type=F32))
            out = x1_ref[rows, :] + gtf_ref[...] * (n2 * gp_ref[...])
            err = out - tg_ref[rows, :]
            do = err * (1.0 / D_MODEL)
            dout_ref[rows, :] = do
            loss_ref[...] += jnp.broadcast_to(0.5 * jnp.sum(err * err, keepdims=True) * (1.0 / D_MODEL),
                                              loss_ref.shape)
            vs_ref[0:1, :] += _colsum(do * n2 * gp_ref[...])
            vs_ref[1:2, :] += _colsum(do * gtf_ref[...] * n2)
            dy2_ref[rows, :] = _rms_bwd(do * gtf_ref[...] * gp_ref[...], n2, r2).astype(BF16)

    row = lambda c: pl.BlockSpec((tt, c), lambda i: (i, 0))
    vec = _const((1, D_MODEL))
    outs, _ = _call(
        body, "ffn_tail", (s_len // tt,),
        in_specs=[row(D_FF), _whole(), row(D_MODEL), vec, vec, row(D_MODEL)],
        out_specs=[row(D_MODEL), row(D_MODEL), _const((SUBLANES, LANES)), _const((SUBLANES, D_MODEL))],
        out_shape=[_sds((s_len, D_MODEL), BF16), _sds((s_len, D_MODEL), F32), _sds((SUBLANES, LANES), F32),
                   _sds((SUBLANES, D_MODEL), F32)],
        scratch=[], args=(act, w_down, x1, gt_f, g_post, target))
    return outs


def _ffn_conv(up_pre, prev8, cw_ref, cb):
    up = cb + cw_ref[FFN_CONV_K - 1:FFN_CONV_K, :] * up_pre
    taps = []
    for k in range(FFN_CONV_K - 1):
        tap = _shift_down(up_pre, prev8, FFN_CONV_K - 1 - k)
        taps.append(tap)
        up = up + cw_ref[k:k + 1, :] * tap
    return up, taps


def _ffn_bwd(d_y2, up_pre, up, ffn_cw, w_down, carry=None):
    s_len = d_y2.shape[0]
    tt = min(TT_BIG, s_len)
    nt = s_len // tt
    cw = FF_CW
    nc = D_FF // cw

    def body(dy2_ref, up_ref, upc_ref, cwg_ref, cwv_ref, wd_ref, dup_ref, cs_ref, nxt, cs_acc):
        i = pl.program_id(0)
        c = pl.program_id(1)

        @pl.when(i == 0)
        def _():
            nxt[c] = jnp.zeros((2, SUBLANES, cw), F32)
            cs_acc[c] = jnp.zeros((2, SUBLANES, cw), F32)

        pw = cw // 2
        for piece in range(2):
            cols = slice(piece * pw, (piece + 1) * pw)
            d_act = _dot_nt(dy2_ref[...], wd_ref[pl.ds(pl.multiple_of(c * cw + piece * pw, pw), pw), :])
            uv = upc_ref[1, :, cols].astype(F32)
            gl, dgl = _gelu_and_grad(upc_ref[0, :, cols].astype(F32))
            d_ug = d_act * uv * dgl
            d_uv = d_act * gl
            for half, (d_u, cw_ref) in enumerate(((d_ug, cwg_ref), (d_uv, cwv_ref))):
                nx = nxt[c, half, :, cols]
                x_in = up_ref[half, :, cols].astype(F32)
                d_pre = cw_ref[FFN_CONV_K - 1:FFN_CONV_K, cols] * d_u
                sums = [None] * (FFN_CONV_K + 1)
                sums[FFN_CONV_K - 1] = _colsum(d_u * x_in)
                for k in range(FFN_CONV_K - 1):
                    ahead = _shift_up(d_u, nx, FFN_CONV_K - 1 - k)
                    d_pre = d_pre + cw_ref[k:k + 1, cols] * ahead
                    sums[k] = _colsum(ahead * x_in)
                sums[FFN_CONV_K] = _colsum(d_u)
                pad = jnp.zeros((SUBLANES - FFN_CONV_K - 1, pw), F32)
                cs_acc[c, half, :, cols] += jnp.concatenate(sums + [pad], axis=0)
                nxt[c, half, :, cols] = d_u[0:SUBLANES]
                dup_ref[half, :, cols] = d_pre.astype(BF16)

        for cc in range(nc):
            @pl.when((i == nt - 1) & (c == cc))
            def _():
                cs_ref[:, cc * cw:(cc + 1) * cw] = cs_acc[cc, 0]
                cs_ref[:, D_FF + cc * cw:D_FF + (cc + 1) * cw] = cs_acc[cc, 1]

    row = pl.BlockSpec((tt, D_MODEL), lambda i, c: (nt - 1 - i, 0))
    blk = pl.BlockSpec((2, tt, cw), lambda i, c: (0, nt - 1 - i, c))
    return _call(
        body, "ffn_bwd", (nt, nc),
        in_specs=[row, blk, blk,
                  pl.BlockSpec((FFN_CONV_K, cw), lambda i, c: (0, c)),
                  pl.BlockSpec((FFN_CONV_K, cw), lambda i, c: (0, c + nc)),
                  _whole()],
        out_specs=[blk, _const((SUBLANES, 2 * D_FF))],
        out_shape=[_sds((2, s_len, D_FF), BF16), _sds((SUBLANES, 2 * D_FF), F32)],
        scratch=[pltpu.VMEM((nc, 2, SUBLANES, cw), F32), pltpu.VMEM((nc, 2, SUBLANES, cw), F32)],
        args=(d_y2, up_pre, up, ffn_cw, ffn_cw, w_down), carry=carry)


def _up_bwd(d_up, w_up3, x1, dout, y, w_out, g_pre, sc_f, g_post, gt_m, carry=None):
    s_len = x1.shape[0]
    tt = min(TT_BIG, s_len)

    def body(du_ref, wu_ref, x1_ref, do_ref, y_ref, wo_ref, g2_ref, sc_ref, gp_ref, gt_ref,
             dx1_ref, dy_ref, dyc_ref, vs_ref):
        @pl.when(pl.program_id(0) == 0)
        def _():
            vs_ref[...] = jnp.zeros_like(vs_ref)

        for rows in _row_pieces(tt):
            d_h2 = jnp.zeros((rows.stop - rows.start, D_MODEL), F32)
            for half in range(2):
                for ch in range(FF_CHUNKS):
                    d_h2 = d_h2 + _dot_nt(du_ref[half, rows, ch * FF_CHUNK_W:(ch + 1) * FF_CHUNK_W],
                                          wu_ref[half * FF_CHUNKS + ch])
            n1, r1 = _rms(x1_ref[rows, :])
            ng = n1 * g2_ref[...]
            vs_ref[0:1, :] += _colsum(d_h2)
            vs_ref[1:2, :] += _colsum(d_h2 * ng)
            d_ng = d_h2 * (1.0 + sc_ref[...])
            vs_ref[2:3, :] += _colsum(d_ng * n1)
            d_x1 = do_ref[rows, :] + _rms_bwd(d_ng * g2_ref[...], n1, r1)
            dx1_ref[rows, :] = d_x1
            n_y, r_y = _rms(y_ref[rows, :])
            vs_ref[3:4, :] += _colsum(d_x1 * n_y * gp_ref[...])
            d_on = d_x1 * gt_ref[...]
            vs_ref[4:5, :] += _colsum(d_on * n_y)
            d_y = _rms_bwd(d_on * gp_ref[...], n_y, r_y).astype(BF16)
            dy_ref[rows, :] = d_y
            dyc_ref[rows, :] = _dot_nt(d_y, wo_ref[...])

    row = lambda c: pl.BlockSpec((tt, c), lambda i: (i, 0))
    vec = _const((1, D_MODEL))
    return _call(
        body, "up_bwd", (s_len // tt,),
        in_specs=[pl.BlockSpec((2, tt, D_FF), lambda i: (0, i, 0)), _whole(), row(D_MODEL), row(D_MODEL), row(D_MODEL),
                  _whole(), vec, vec, vec, vec],
        out_specs=[row(D_MODEL), row(D_MODEL), row(LRU_W + GMLP_W), _const((SUBLANES, D_MODEL))],
        out_shape=[_sds((s_len, D_MODEL), F32), _sds((s_len, D_MODEL), BF16), _sds((s_len, LRU_W + GMLP_W), F32),
                   _sds((SUBLANES, D_MODEL), F32)],
        scratch=[], args=(d_up, w_up3, x1, dout, y, w_out, g_pre, sc_f, g_post, gt_m), carry=carry)


def _head_pair_block(hd):
    return (slice((hd // 2) * HEAD_DIM, (hd // 2 + 1) * HEAD_DIM), slice((hd % 2) * HEAD_DIM, (hd % 2 + 1) * HEAD_DIM))


def _mix_bwd(d_ycat, z, hl, conv_w, conv_b, wr_bd, wi_bd, b_r, b_i, lru_a, vn_g, vn_b, w_sp, w_sp_t, b_sp_t,
             g_lru, g_gmlp, carry=None):
    s_len = z.shape[0]
    tt = min(TT_MIX, s_len)
    nt = s_len // tt
    nblk = tt // POS_BLOCK
    hb = tt // SUBLANES

    def body(dyc_ref, z_ref, zh_ref, hl_ref, hh_ref, cw_ref, cb_ref, wr_ref, wi_ref, br_ref, bi_ref, la_ref,
             vg_ref, vb_ref, ws_ref, wst_ref, bst_ref, gl_ref, gg_ref,
             dz_ref, vs_ref, dcw_ref, dwrb_ref, dwib_ref, dws_ref, dbs_ref, nxt_dxc, nxt_a, nxt_lam, dwr_ref, dwi_ref):
        i = pl.program_id(0)
        first_tile = i == nt - 1

        @pl.when(i == 0)
        def _():
            for ref in (vs_ref, dcw_ref, dwr_ref, dwi_ref, dws_ref, dbs_ref, nxt_dxc, nxt_a, nxt_lam):
                ref[...] = jnp.zeros_like(ref)

        lx = z_ref[:, 0:LRU_W]
        gate = z_ref[:, LRU_W:2 * LRU_W]
        gu = z_ref[:, 2 * LRU_W:2 * LRU_W + GMLP_W]
        gv = z_ref[:, 2 * LRU_W + GMLP_W:]
        prev8 = jnp.where(first_tile, 0.0, zh_ref[...])
        hprev8 = jnp.where(first_tile, 0.0, hh_ref[...])

        xc, taps = _lru_conv(lx, prev8, cw_ref, cb_ref[...])
        a_par = la_ref[...]
        sp_a = _softplus(-a_par)
        r, ig, a, mult = _lru_gates(xc, wr_ref, wi_ref, br_ref[...], bi_ref[...], sp_a)
        hl = hl_ref[...]
        h_prev = _shift_down(hl, hprev8, 1)
        ggate, dggate = _gelu_and_grad(gate)
        y_lru = hl * ggate
        n_l, r_l = _rms(y_lru)
        d_nl = dyc_ref[:, 0:LRU_W]
        vs_ref[6:7, :] += _colsum(d_nl * n_l)
        d_yl = _rms_bwd(d_nl * gl_ref[...], n_l, r_l)
        d_hl = d_yl * ggate
        d_gate = d_yl * hl * dggate
        a_up = _shift_up(a, nxt_a[...], 1)
        a_cum, b_cum = _scan_rev(a_up, d_hl)
        lam = b_cum + a_cum * nxt_lam[0:1, :]
        nxt_a[...] = jnp.broadcast_to(a[0:1, :], nxt_a.shape)
        nxt_lam[...] = jnp.broadcast_to(lam[0:1, :], nxt_lam.shape)
        ixc = ig * xc
        d_la = lam * h_prev * a - lam * ixc * (a * a) / mult
        d_i = lam * mult * xc
        d_xc = lam * mult * ig
        vs_ref[3:4, :] += _colsum(d_la * r) * (LRU_C * _sigmoid(-a_par))
        d_pr = d_la * (-LRU_C * sp_a) * r * (1.0 - r)
        d_pi = d_i * ig * (1.0 - ig)
        vs_ref[1:2, :] += _colsum(d_pr)
        vs_ref[2:3, :] += _colsum(d_pi)
        dwr_ref[...] += _dot_tn(xc, d_pr)
        dwi_ref[...] += _dot_tn(xc, d_pi)
        d_xc = d_xc + _dot_nt(d_pr, wr_ref[...]) + _dot_nt(d_pi, wi_ref[...])
        vs_ref[0:1, :] += _colsum(d_xc)
        nx = nxt_dxc[...]
        d_lx = cw_ref[LRU_CONV_K - 1:LRU_CONV_K, :] * d_xc
        dcw_ref[LRU_CONV_K - 1:LRU_CONV_K, :] += _colsum(d_xc * lx)
        for k in range(LRU_CONV_K - 1):
            d_lx = d_lx + cw_ref[k:k + 1, :] * _shift_up(d_xc, nx, LRU_CONV_K - 1 - k)
            dcw_ref[k:k + 1, :] += _colsum(d_xc * taps[k])
        nxt_dxc[...] = d_xc[0:SUBLANES]
        dz_ref[:, 0:LRU_W] = d_lx.astype(BF16)
        dz_ref[:, LRU_W:2 * LRU_W] = d_gate.astype(BF16)

        u, du = _gelu_and_grad(gu)
        v, vhat, rs, dav = _gmlp_v(gv, vg_ref[...], vb_ref[...])
        mask = _ws_mask()
        sp_parts = []
        for nb in range(nblk):
            rowp = []
            for g in range(N_GROUPS):
                wsm = jnp.where(mask, ws_ref[g], 0.0)
                vblk = v[nb * POS_BLOCK:(nb + 1) * POS_BLOCK, g * LANES:(g + 1) * LANES]
                rowp.append(_dot(wsm, vblk) + bst_ref[:, g:g + 1])
            sp_parts.append(jnp.concatenate(rowp, axis=1))
        sp = jnp.concatenate(sp_parts, axis=0) if nblk > 1 else sp_parts[0]
        y_g = u * sp
        n_g, r_g = _rms(y_g)
        d_ng = dyc_ref[:, LRU_W:]
        vs_ref[7:8, :] += _colsum(d_ng * n_g)
        d_yg = _rms_bwd(d_ng * gg_ref[...], n_g, r_g)
        d_gu = d_yg * sp * du
        d_sp = d_yg * u
        mask_t = _ws_mask(transposed=True)
        ones8 = jnp.ones((SUBLANES, LANES), F32)
        dv_parts = []
        for nb in range(nblk):
            rowp = []
            for g in range(N_GROUPS):
                rs_, cs_ = slice(nb * POS_BLOCK, (nb + 1) * POS_BLOCK), slice(g * LANES, (g + 1) * LANES)
                dsp_blk = d_sp[rs_, cs_]
                dbs_ref[g:g + 1, :] += lax.dot_general(
                    ones8, dsp_blk, (((1,), (1,)), ((), ())), preferred_element_type=F32,
                    precision=lax.Precision.HIGHEST)[0:1, :]
                dws_ref[g] += _dot_nt(dsp_blk, v[rs_, cs_])
                wsm_t = jnp.where(mask_t, wst_ref[g], 0.0)
                rowp.append(_dot(wsm_t, dsp_blk))
            dv_parts.append(jnp.concatenate(rowp, axis=1))
        d_v = jnp.concatenate(dv_parts, axis=0) if nblk > 1 else dv_parts[0]
        vs_ref[4:5, :] += _colsum(d_v * vhat)
        vs_ref[5:6, :] += _colsum(d_v)
        d_vh = d_v * vg_ref[...]
        d_av = rs * (d_vh - jnp.mean(d_vh, axis=-1, keepdims=True)
                     - vhat * jnp.mean(d_vh * vhat, axis=-1, keepdims=True))
        dz_ref[:, 2 * LRU_W:2 * LRU_W + GMLP_W] = d_gu.astype(BF16)
        dz_ref[:, 2 * LRU_W + GMLP_W:] = (d_av * dav).astype(BF16)

        @pl.when(i == nt - 1)
        def _():
            for hd in range(N_HEADS):
                blk = slice(hd * HEAD_DIM, (hd + 1) * HEAD_DIM)
                dwrb_ref[_head_pair_block(hd)] = dwr_ref[blk, blk]
                dwib_ref[_head_pair_block(hd)] = dwi_ref[blk, blk]
            for g in range(N_GROUPS):
                dws_ref[g] = jnp.where(mask, dws_ref[g], 0.0)

    rev = lambda c: pl.BlockSpec((tt, c), lambda i: (nt - 1 - i, 0))
    halo = pl.BlockSpec((SUBLANES, LRU_W), lambda i: (jnp.maximum((nt - 1 - i) * hb - 1, 0), 0))
    v512 = _const((1, LRU_W))
    return _call(
        body, "mix_bwd", (nt,),
        in_specs=[rev(LRU_W + GMLP_W), rev(IN_COLS), halo, rev(LRU_W), halo,
                  _const((LRU_CONV_K, LRU_W)), v512, _whole(), _whole(), v512, v512, v512, v512, v512,
                  _whole(), _whole(), _whole(), v512, v512],
        out_specs=[rev(IN_COLS), _const((SUBLANES, LRU_W)), _const((SUBLANES, LRU_W)),
                   _const((LRU_W // 2, 2 * HEAD_DIM)), _const((LRU_W // 2, 2 * HEAD_DIM)),
                   _const((N_GROUPS, POS_BLOCK, POS_BLOCK)), _const((SUBLANES, POS_BLOCK))],
        out_shape=[_sds((s_len, IN_COLS), BF16), _sds((SUBLANES, LRU_W), F32), _sds((SUBLANES, LRU_W), F32),
                   _sds((LRU_W // 2, 2 * HEAD_DIM), F32), _sds((LRU_W // 2, 2 * HEAD_DIM), F32),
                   _sds((N_GROUPS, POS_BLOCK, POS_BLOCK), F32), _sds((SUBLANES, POS_BLOCK), F32)],
        scratch=[pltpu.VMEM((SUBLANES, LRU_W), F32), pltpu.VMEM((SUBLANES, LRU_W), F32),
                 pltpu.VMEM((SUBLANES, LRU_W), F32), pltpu.VMEM((LRU_W, LRU_W), F32), pltpu.VMEM((LRU_W, LRU_W), F32)],
        args=(d_ycat, z, z, hl, hl, conv_w, conv_b, wr_bd, wi_bd, b_r, b_i, lru_a, vn_g, vn_b, w_sp, w_sp_t, b_sp_t,
              g_lru, g_gmlp), carry=carry)


def _in_bwd(d_z, w_in, x, d_x1, g, sc, carry=None):
    s_len = x.shape[0]
    tt = min(TT_BIG, s_len)

    def body(dz_ref, w_ref, x_ref, dx1_ref, g_ref, sc_ref, gx_ref, vs_ref):
        @pl.when(pl.program_id(0) == 0)
        def _():
            vs_ref[...] = jnp.zeros_like(vs_ref)

        for rows in _row_pieces(tt):
            d_h = _dot_nt(dz_ref[rows, :], w_ref[...])
            n, r = _rms(x_ref[rows, :])
            vs_ref[0:1, :] += _colsum(d_h)
            vs_ref[1:2, :] += _colsum(d_h * n * g_ref[...])
            d_ng = d_h * (1.0 + sc_ref[...])
            vs_ref[2:3, :] += _colsum(d_ng * n)
            gx_ref[rows, :] = dx1_ref[rows, :] + _rms_bwd(d_ng * g_ref[...], n, r)

    row = lambda c: pl.BlockSpec((tt, c), lambda i: (i, 0))
    vec = _const((1, D_MODEL))
    return _call(
        body, "in_bwd", (s_len // tt,),
        in_specs=[row(IN_COLS), _whole(), row(D_MODEL), row(D_MODEL), vec, vec],
        out_specs=[row(D_MODEL), _const((SUBLANES, D_MODEL))],
        out_shape=[_sds((s_len, D_MODEL), F32), _sds((SUBLANES, D_MODEL), F32)],
        scratch=[], args=(d_z, w_in, x, d_x1, g, sc), carry=carry)


def _wgrad(a, b, name, by_rows=False, carry=None):
    s_len, k_dim = a.shape
    halves = b.ndim == 3
    n_dim = b.shape[-1] * (2 if halves else 1)

    def body(a_ref, b_ref, ob_ref, own_ref):
        out = _dot_tn(a_ref[...], b_ref[0] if halves else b_ref[...])
        ob_ref[...] = out.astype(BF16)

        @pl.when(pl.program_id(0) == _dev_index(_my_pos()))
        def _():
            own_ref[...] = out

    if by_rows:
        tile = k_dim // N_DEV
        a_spec = pl.BlockSpec((s_len, tile), lambda j: (0, j))
        b_spec = pl.BlockSpec((s_len, n_dim), lambda j: (0, 0))
        o_spec = pl.BlockSpec((tile, n_dim), lambda j: (j, 0))
        own_shape = (tile, n_dim)
    else:
        tile = n_dim // N_DEV
        a_spec = pl.BlockSpec((s_len, k_dim), lambda j: (0, 0))
        if halves:
            per_half = N_DEV // 2
            b_spec = pl.BlockSpec((1, s_len, tile), lambda j: (j // per_half, 0, j % per_half))
        else:
            b_spec = pl.BlockSpec((s_len, tile), lambda j: (0, j))
        o_spec = pl.BlockSpec((k_dim, tile), lambda j: (0, j))
        own_shape = (k_dim, tile)
    return _call(
        body, name, (N_DEV,), in_specs=[a_spec, b_spec], out_specs=[o_spec, _const(own_shape)],
        out_shape=[_sds((k_dim, n_dim), BF16), _sds(own_shape, F32)],
        scratch=[], args=(a, b), carry=carry)


def _adam_math(w, g, m, v):
    m = ADAM_B1 * m + (1.0 - ADAM_B1) * g
    v = ADAM_B2 * v + (1.0 - ADAM_B2) * (g * g)
    m_hat = m / (1.0 - ADAM_B1 ** ADAM_STEP)
    v_hat = v / (1.0 - ADAM_B2 ** ADAM_STEP)
    delta = -ADAM_LR * (m_hat / (jnp.sqrt(v_hat) + ADAM_EPS) + ADAM_WD * w)
    return delta, m, v


def _row_tile(rows, cols, n_f32_arrays):
    budget = VMEM_LIMIT // 2
    tr = rows
    while tr % 2 == 0 and tr // 2 >= SUBLANES and (tr // 2) % SUBLANES == 0 and tr * cols * 4 * n_f32_arrays * 2 > budget:
        tr //= 2
    return tr


def _adamw_sum(w, g_own, recv, m, v, name):
    _, rows, cols = w.shape
    n_recv = len(recv)
    tr = _row_tile(rows, cols, 10)
    nb = rows // tr

    def body(w_ref, g_ref, *rest):
        r_refs = rest[:n_recv]
        m_ref, v_ref, go_ref, d_ref, mo_ref, vo_ref = rest[n_recv:]
        g = g_ref[...]
        for r_ref in r_refs:
            for k in range(r_ref.shape[0]):
                g = g + r_ref[k].astype(F32)
        go_ref[0] = g
        d_ref[0], mo_ref[0], vo_ref[0] = _adam_math(w_ref[0], g, m_ref[0], v_ref[0])

    blk = pl.BlockSpec((1, tr, cols), lambda i: (0, i, 0))
    return pl.pallas_call(
        body, name=name, grid=(nb,),
        in_specs=[blk, pl.BlockSpec((tr, cols), lambda i: (i, 0))]
        + [pl.BlockSpec((r.shape[0], tr, cols), lambda i: (0, i, 0)) for r in recv] + [blk, blk],
        out_specs=[blk] * 4, out_shape=[_sds((1, rows, cols), F32)] * 4,
        compiler_params=_cparams(("arbitrary",)),
    )(w, g_own, *recv, m, v)


def _row_of_each(ref, row):
    cols = ref.shape[1]
    rows = _rows((N_DEV, cols))
    out = jnp.zeros((N_DEV, cols), F32)
    for d in range(N_DEV):
        picked = ref[d * SUBLANES + row:d * SUBLANES + row + 1, :]
        out = jnp.where(rows == d, jnp.broadcast_to(picked, (N_DEV, cols)), out)
    return out


def _my_columns(full, width, me):
    out = jnp.zeros(full.shape[:-1] + (width,), F32)
    for d in range(N_DEV):
        out = out + jnp.where(me == d, full[:, d * width:(d + 1) * width], 0.0)
    return out


def _adamw_wada(c_all, vs_in_all, vs_up_all, vs_ffn_all, w, m, v):
    _, rows, cols = w.shape

    def body(c_ref, vi_ref, vu_ref, vf_ref, w_ref, m_ref, v_ref, go_ref, d_ref, mo_ref, vo_ref):
        me = _dev_index(_my_pos())
        cv = _row_of_each(c_ref, 0)
        ca = cv * _sigmoid(cv)
        dmod = jnp.concatenate([_row_of_each(vi_ref, 0), _row_of_each(vi_ref, 1), _row_of_each(vu_ref, 3),
                                _row_of_each(vu_ref, 0), _row_of_each(vu_ref, 1), _row_of_each(vf_ref, 0)], axis=1)
        dm = _my_columns(dmod, cols, me)
        g = lax.dot_general(ca, dm, (((0,), (0,)), ((), ())), preferred_element_type=F32,
                            precision=lax.Precision.HIGHEST)
        go_ref[0] = g
        d_ref[0], mo_ref[0], vo_ref[0] = _adam_math(w_ref[0], g, m_ref[0], v_ref[0])

    return pl.pallas_call(
        body, name="adamw_w_ada", out_shape=[_sds((1, rows, cols), F32)] * 4,
        in_specs=[_whole()] * 7, out_specs=[_whole()] * 4,
        compiler_params=_cparams(),
    )(c_all, vs_in_all, vs_up_all, vs_ffn_all, w, m, v)


def _adamw_small(gathered, reduced, params, conv_params):
    names = list(params) + list(conv_params)
    allp = {**params, **conv_params}
    n_g = len(gathered) + len(reduced)

    def body(*refs):
        g_refs = refs[:n_g]
        p_refs = refs[n_g:n_g + 3 * len(names)]
        o_refs = refs[n_g + 3 * len(names):]
        me = _dev_index(_my_pos())

        def total(ref):
            s = ref[0:SUBLANES, :]
            for d in range(1, N_DEV):
                s = s + ref[d * SUBLANES:(d + 1) * SUBLANES, :]
            return s

        vs_in, vs_up, vs_ffn, loss = [total(r) for r in g_refs[:4]]
        cs, vs_mix, dcw, dwr, dwi, dws, dbs = [r[...] for r in g_refs[4:]]
        o_refs[-1][...] = loss[0:1, 0:1]
        mine = lambda full, width: _my_columns(full, width, me)

        all_ = (slice(None), slice(None))
        heads = lambda row: [((0, slice(h, h + 1), slice(None)), row[:, h * HEAD_DIM:(h + 1) * HEAD_DIM])
                             for h in range(N_HEADS)]
        blocks = lambda pairs: [((0, h), pairs[_head_pair_block(h)]) for h in range(N_HEADS)]
        pieces = {
            "b_ada": [((slice(None), slice(k * D_MODEL, (k + 1) * D_MODEL)), row) for k, row in enumerate(
                (vs_in[0:1], vs_in[1:2], vs_up[3:4], vs_up[0:1], vs_up[1:2], vs_ffn[0:1]))],
            "g_mix_pre": [(all_, vs_in[2:3])], "g_mix_post": [(all_, vs_up[4:5])],
            "g_ffn_pre": [(all_, vs_up[2:3])], "g_ffn_post": [(all_, vs_ffn[1:2])],
            "conv_b": [(all_, vs_mix[0:1])], "b_rgate": heads(vs_mix[1:2]), "b_igate": heads(vs_mix[2:3]),
            "lru_a": [(all_, vs_mix[3:4])], "v_norm_g": [(all_, vs_mix[4:5])], "v_norm_b": [(all_, vs_mix[5:6])],
            "g_lru_out": [(all_, vs_mix[6:7])], "g_gmlp_out": [(all_, vs_mix[7:8])],
            "w_rgate": blocks(dwr), "w_igate": blocks(dwi),
            "w_spatial": [((0, g), dws[g * POS_BLOCK:(g + 1) * POS_BLOCK, :]) for g in range(N_GROUPS)],
            "b_spatial": [((0,), dbs[0:N_GROUPS])],
            "ffn_conv_b": [(all_, cs[FFN_CONV_K:FFN_CONV_K + 1])],
            "conv_w": [((0,), mine(dcw[0:LRU_CONV_K], LRU_W // N_DEV))],
            "ffn_conv_w": [((0,), mine(cs[0:FFN_CONV_K], 2 * D_FF // N_DEV))],
        }
        for n_i, name in enumerate(names):
            w_ref, m_ref, v_ref = p_refs[3 * n_i:3 * n_i + 3]
            go_ref, d_ref, mo_ref, vo_ref = o_refs[4 * n_i:4 * n_i + 4]
            for idx, g in pieces[name]:
                go_ref[idx] = g
                d_ref[idx], mo_ref[idx], vo_ref[idx] = _adam_math(w_ref[idx], g, m_ref[idx], v_ref[idx])

    flat_params = [a for n in names for a in allp[n]]
    out_shape = [_sds(allp[n][0].shape, F32) for n in names for _ in range(4)] + [_sds((1, 1), F32)]
    outs = pl.pallas_call(
        body, name="adamw_small", out_shape=out_shape,
        in_specs=[_whole()] * (n_g + len(flat_params)), out_specs=[_whole()] * len(out_shape),
        compiler_params=_cparams(),
    )(*gathered, *reduced, *flat_params)
    return {n: outs[4 * i:4 * i + 4] for i, n in enumerate(names)}, outs[-1]


def _my_pos():
    return lax.axis_index("x"), lax.axis_index("y"), lax.axis_index("c")


def _flip(pos, k):
    x, y, c = pos
    return (1 - x if k & 4 else x, 1 - y if k & 2 else y, 1 - c if k & 1 else c)


def _dev_index(pos):
    x, y, c = pos
    return 4 * x + 2 * y + c


def _all_gather_small(ins, outs, send_sems, recv_sems):
    n = len(ins)
    me = _my_pos()

    def slot(a, pos):
        rows = ins[a].shape[0]
        return outs[a].at[pl.ds(pl.multiple_of(_dev_index(pos) * rows, SUBLANES), rows), :]

    def copy(a, k, block):
        return pltpu.make_async_remote_copy(
            src_ref=ins[a], dst_ref=slot(a, block), send_sem=send_sems.at[a, k - 1], recv_sem=recv_sems.at[a, k - 1],
            device_id=_flip(me, k), device_id_type=MESH)

    sends = [copy(a, k, me) for a in range(n) for k in range(1, N_DEV)]
    for cp in sends:
        cp.start()
    for a in range(n):
        rows = ins[a].shape[0]
        outs[a][pl.ds(pl.multiple_of(_dev_index(me) * rows, SUBLANES), rows), :] = ins[a][...]
    for a in range(n):
        for k in range(1, N_DEV):
            copy(a, k, _flip(me, k)).wait_recv()
    for cp in sends:
        cp.wait_send()


def _prologue(c8, cw8, fcw8, w_ada, b_ada, carry):
    cols = w_ada.shape[1]

    def body(c_ref, cw_ref, fcw_ref, w_ref, b_ref, call_ref, cwall_ref, fcwall_ref, modall_ref, mod_scr,
             s1, r1, s2, r2, start_carry):
        _all_gather_small([c_ref, cw_ref, fcw_ref], [call_ref, cwall_ref, fcwall_ref], s1, r1)
        start_carry()
        cv = _row_of_each(call_ref, 0)
        ca = cv * _sigmoid(cv)
        b_cols = _my_columns(b_ref[...], cols, _dev_index(_my_pos()))
        mod_scr[...] = jnp.dot(ca, w_ref[...], preferred_element_type=F32, precision=lax.Precision.HIGHEST) + b_cols
        _all_gather_small([mod_scr], [modall_ref], s2, r2)

    sem = lambda n: pltpu.SemaphoreType.DMA((n, N_DEV - 1))
    return _call(
        body, "prologue", (1,), in_specs=[_whole()] * 5, out_specs=[_whole()] * 4,
        out_shape=[_sds((N_DEV * SUBLANES, a.shape[1]), F32) for a in (c8, cw8, fcw8)]
        + [_sds((N_DEV * N_DEV, cols), F32)],
        scratch=[pltpu.VMEM((N_DEV, cols), F32), sem(3), sem(3), sem(1), sem(1)],
        args=(c8, cw8, fcw8, w_ada, b_ada), carry=carry, body_starts_carry=True)


def _reduce_small(gath, red, carry=None):
    n_g, n_r = len(gath), len(red)
    chip_flips = (4, 2, 6)

    def body(*refs, start_carry):
        g_in, r_in = refs[:n_g], refs[n_g:n_g + n_r]
        g_out, r_out = refs[n_g + n_r:2 * n_g + n_r], refs[2 * n_g + n_r:2 * (n_g + n_r)]
        scr = refs[2 * (n_g + n_r):]
        sib, land = scr[:n_r], scr[n_r:2 * n_r]
        g_send, g_recv, s_send, s_recv, i_send, i_recv, f_send, f_recv = scr[2 * n_r:]
        me = _my_pos()
        c = me[2]
        sibling = _flip(me, 1)

        def slot(a, pos):
            return g_out[a].at[pl.ds(pl.multiple_of(_dev_index(pos) * SUBLANES, SUBLANES), SUBLANES), :]

        def gcopy(a, k):
            return pltpu.make_async_remote_copy(
                src_ref=g_in[a], dst_ref=slot(a, me), send_sem=g_send.at[a, k - 1], recv_sem=g_recv.at[a, k - 1],
                device_id=_flip(me, k), device_id_type=MESH)

        def scopy(a):
            return pltpu.make_async_remote_copy(
                src_ref=r_in[a], dst_ref=sib[a], send_sem=s_send.at[a], recv_sem=s_recv.at[a],
                device_id=sibling, device_id_type=MESH)

        def icopy(a, j):
            return pltpu.make_async_remote_copy(
                src_ref=r_out[a], dst_ref=land[a].at[j], send_sem=i_send.at[a, j], recv_sem=i_recv.at[a, j],
                device_id=_flip(me, chip_flips[j]), device_id_type=MESH)

        def fcopy(a, j):
            return pltpu.make_async_remote_copy(
                src_ref=land[a].at[j], dst_ref=land[a].at[j], send_sem=f_send.at[a, j], recv_sem=f_recv.at[a, j],
                device_id=sibling, device_id_type=MESH)

        gathers = [gcopy(a, k) for a in range(n_g) for k in range(1, N_DEV)]
        swaps = [scopy(a) for a in range(n_r)]
        for cp in gathers + swaps:
            cp.start()
        for a in range(n_g):
            g_out[a][pl.ds(pl.multiple_of(_dev_index(me) * SUBLANES, SUBLANES), SUBLANES), :] = g_in[a][...]
        for a in range(n_r):
            swaps[a].wait_recv()
            r_out[a][...] = r_in[a][...] + sib[a][...]

        for core in range(2):
            @pl.when(c == core)
            def _():
                for a in range(core, n_r, 2):
                    for j in range(3):
                        icopy(a, j).start()

        start_carry()

        for core in range(2):
            mine = [a for a in range(n_r) if a % 2 == core]
            theirs = [a for a in range(n_r) if a % 2 != core]

            @pl.when(c == core)
            def _():
                out = [icopy(a, j) for a in mine for j in range(3)]
                fwd = []
                for a in mine:
                    for j in range(3):
                        icopy(a, j).wait_recv()
                        cp = fcopy(a, j)
                        cp.start()
                        fwd.append(cp)
                for a in theirs:
                    for j in range(3):
                        fcopy(a, j).wait_recv()
                for cp in out + fwd:
                    cp.wait_send()

        for a in range(n_r):
            r_out[a][...] = (r_out[a][...] + land[a][1]) + (land[a][0] + land[a][2])
        for a in range(n_g):
            for k in range(1, N_DEV):
                pltpu.make_async_remote_copy(
                    src_ref=g_in[a], dst_ref=slot(a, _flip(me, k)), send_sem=g_send.at[a, k - 1],
                    recv_sem=g_recv.at[a, k - 1], device_id=_flip(me, k), device_id_type=MESH).wait_recv()
        for cp in gathers + swaps:
            cp.wait_send()

    shapes = [tuple(a.shape) for a in red]
    outs, carried = _call(
        body, "reduce_small", (1,), in_specs=[_whole()] * (n_g + n_r), out_specs=[_whole()] * (n_g + n_r),
        out_shape=[_sds((N_DEV * SUBLANES, a.shape[1]), F32) for a in gath] + [_sds(s, F32) for s in shapes],
        scratch=[pltpu.VMEM(s, F32) for s in shapes] + [pltpu.VMEM((3,) + s, F32) for s in shapes]
        + [pltpu.SemaphoreType.DMA((n_g, N_DEV - 1)), pltpu.SemaphoreType.DMA((n_g, N_DEV - 1)),
           pltpu.SemaphoreType.DMA((n_r,)), pltpu.SemaphoreType.DMA((n_r,)),
           pltpu.SemaphoreType.DMA((n_r, 3)), pltpu.SemaphoreType.DMA((n_r, 3)),
           pltpu.SemaphoreType.DMA((n_r, 3)), pltpu.SemaphoreType.DMA((n_r, 3))],
        args=tuple(gath) + tuple(red), carry=carry, body_starts_carry=True)
    return (outs[:n_g], outs[n_g:]), carried


STACKED = "stacked"


def _region(ref, shard_shape, col_sharded, pos):
    r, cdim = shard_shape
    d = _dev_index(pos)
    if col_sharded == STACKED:
        return ref.at[d]
    if col_sharded:
        return ref.at[:, pl.ds(pl.multiple_of(d * cdim, LANES), cdim)]
    return ref.at[pl.ds(pl.multiple_of(d * r, 2 * SUBLANES), r), :]


def _gather_carry(shards, col_sharded):
    n_w = len(shards)
    shapes = [tuple(s.shape) for s in shards]
    full_shapes = [(N_DEV,) + s if cs == STACKED else (s[0], s[1] * N_DEV) if cs else (s[0] * N_DEV, s[1])
                   for s, cs in zip(shapes, col_sharded)]

    def tools(out_refs, scr):
        send_sems, recv_sems = scr[n_w], scr[n_w + 1]
        me = _my_pos()
        x, y, c = me
        sibling = (x, y, 1 - c)
        chips = [(1 - x, y), (x, 1 - y), (1 - x, 1 - y)]

        def region(w, pos):
            return _region(out_refs[w], shapes[w], col_sharded[w], pos)

        def copy(w, k, block, to, src=None):
            return pltpu.make_async_remote_copy(
                src_ref=region(w, block) if src is None else src, dst_ref=region(w, block),
                send_sem=send_sems.at[w, k], recv_sem=recv_sems.at[w, k], device_id=to, device_id_type=MESH)

        def first(w):
            return [copy(w, 0, me, sibling, src=scr[w])] + [
                copy(w, 1 + j, me, (*chip, c), src=scr[w]) for j, chip in enumerate(chips)]

        def mine(w):
            return pltpu.make_async_copy(scr[w], region(w, me), scr[n_w + 2].at[w])

        return me, c, sibling, chips, copy, first, mine

    def start(ins, outs, scr):
        _, _, _, _, _, first, mine = tools(outs, scr)
        for w in range(n_w):
            scr[w][...] = ins[w][...].astype(BF16)
            for cp in first(w) + [mine(w)]:
                cp.start()

    def finish(ins, outs, scr):
        me, c, sibling, chips, copy, first, mine = tools(outs, scr)
        passed = []
        for w in range(n_w):
            for j, chip in enumerate(chips):
                copy(w, 1 + j, (*chip, c), me).wait_recv()
                fwd = copy(w, 4 + j, (*chip, c), sibling)
                fwd.start()
                passed.append(fwd)
        for w in range(n_w):
            copy(w, 0, sibling, me).wait_recv()
            for j, chip in enumerate(chips):
                copy(w, 4 + j, (*chip, 1 - c), me).wait_recv()
        for w in range(n_w):
            for cp in first(w):
                cp.wait_send()
            mine(w).wait()
        for cp in passed:
            cp.wait_send()

    return _Carry(
        inputs=list(shards), in_specs=[_whole()] * n_w,
        out_shape=[_sds(s, BF16) for s in full_shapes], out_specs=[_any()] * n_w,
        scratch=[pltpu.VMEM(s, BF16) for s in shapes]
        + [pltpu.SemaphoreType.DMA((n_w, N_DEV - 1)), pltpu.SemaphoreType.DMA((n_w, N_DEV - 1)),
           pltpu.SemaphoreType.DMA((n_w,))],
        start=start, finish=finish)


def _scatter_carry(grads_bf, shard_shapes, col_sharded, relations):
    n_w = len(grads_bf)
    shapes = [tuple(s) for s in shard_shapes]

    def copies(ins, outs, scr):
        send_sems, recv_sems = scr
        me = _my_pos()
        out = []
        for w in range(n_w):
            for i, k in enumerate(relations[w]):
                peer = _flip(me, k)
                out.append(pltpu.make_async_remote_copy(
                    src_ref=_region(ins[w], shapes[w], col_sharded[w], peer), dst_ref=outs[w].at[i],
                    send_sem=send_sems.at[w, i], recv_sem=recv_sems.at[w, i],
                    device_id=peer, device_id_type=MESH))
        return out

    def start(ins, outs, scr):
        for cp in copies(ins, outs, scr):
            cp.start()

    def finish(ins, outs, scr):
        cps = copies(ins, outs, scr)
        for cp in cps:
            cp.wait_recv()
        for cp in cps:
            cp.wait_send()

    return _Carry(
        inputs=list(grads_bf), in_specs=[_any()] * n_w,
        out_shape=[_sds((len(r),) + s, BF16) for r, s in zip(relations, shapes)], out_specs=[_any()] * n_w,
        scratch=[pltpu.SemaphoreType.DMA((n_w, N_DEV - 1)), pltpu.SemaphoreType.DMA((n_w, N_DEV - 1))],
        start=start, finish=finish)


def _block_diag(w):
    eye = jnp.eye(N_HEADS, dtype=w.dtype)
    return (eye[:, None, :, None] * w[:, :, None, :]).reshape(N_HEADS * HEAD_DIM, N_HEADS * HEAD_DIM)


def _pad_rows(a):
    return jnp.pad(a, ((0, SUBLANES - a.shape[0]), (0, 0)))


def _columns_from_devices(gathered, rows):
    w = gathered.shape[1]
    return gathered.reshape(N_DEV, SUBLANES, w)[:, :rows].transpose(1, 0, 2).reshape(rows, N_DEV * w)


def _local_step(x2, target, mod, w_in_f, w_full, conv_w_full, ffn_cw_full,
                g_mix_pre, g_mix_post, conv_b, w_rgate, b_rgate, w_igate, b_igate, lru_a, v_norm_g, v_norm_b,
                w_spatial, b_spatial, g_lru_out, g_gmlp_out, g_ffn_pre, g_ffn_post, ffn_conv_b,
                gather=None, scatter=None):
    sh_m, sc_m, gt_m, sh_f, sc_f, gt_f = [mod[k] for k in range(N_MOD)]
    wr_bd = _block_diag(w_rgate[0]).astype(BF16)
    wi_bd = _block_diag(w_igate[0]).astype(BF16)
    b_r = b_rgate.reshape(1, LRU_W)
    b_i = b_igate.reshape(1, LRU_W)
    b_sp_t = b_spatial[0].T
    w_sp_t = jnp.swapaxes(w_spatial[0], 1, 2)

    def arriving(*names):
        return gather(*names) if gather else None

    near, far = (1, 2, 3, 4, 5), (6, 7)

    def leaving(*parts):
        return scatter(parts) if scatter else None

    def received(recv, parts, outs):
        for (name, _, _), out in zip(parts, outs):
            recv.setdefault(name, []).append(out)

    mix_params = (conv_w_full, conv_b, wr_bd, wi_bd, b_r, b_i, lru_a, v_norm_g, v_norm_b)
    w_out_f = w_full["w_out"]
    (z, h, ycat, hl, y, x1, h2), got = _mix_fwd(
        x2, sh_m, sc_m, g_mix_pre, w_in_f, *mix_params, w_spatial[0], b_sp_t, g_lru_out, g_gmlp_out,
        w_out_f, g_mix_post, gt_m, g_ffn_pre, sc_f, sh_f, carry=arriving("w_up"))
    w_up_f = got[0] if gather else w_full["w_up"]
    (up_pre, up, act), got = _ffn_fwd(h2, w_up_f, ffn_cw_full, ffn_conv_b, carry=arriving("w_down"))
    w_down_f = got[0] if gather else w_full["w_down"]
    d_y2, dout, loss_acc, vs_ffn = _ffn_tail(act, w_down_f, x1, gt_f, g_ffn_post, target)

    recv = {}
    gw_down, _ = _wgrad(act, d_y2, "wgrad_down", by_rows=True)
    parts = [("w_down", gw_down[0], near + far)]
    (d_up, cs_ffn), got = _ffn_bwd(d_y2, up_pre, up, ffn_cw_full, w_down_f, carry=leaving(*parts))
    received(recv, parts, got)
    gw_up, _ = _wgrad(h2, d_up, "wgrad_up")
    parts = [("w_up", gw_up[0], near)]
    (d_x1, d_y, d_ycat, vs_up), got = _up_bwd(
        d_up, w_up_f, x1, dout, y, w_out_f, g_ffn_pre, sc_f, g_mix_post, gt_m, carry=leaving(*parts))
    received(recv, parts, got)
    gw_out, _ = _wgrad(ycat, d_y, "wgrad_out", by_rows=True)
    parts = [("w_up", gw_up[0], far), ("w_out", gw_out[0], near + far)]
    (d_z, vs_mix, dcw, d_wr, d_wi, d_ws, d_bs), got = _mix_bwd(
        d_ycat, z, hl, *mix_params, w_spatial[0], w_sp_t, b_sp_t, g_lru_out, g_gmlp_out, carry=leaving(*parts))
    received(recv, parts, got)
    gw_in, _ = _wgrad(h, d_z, "wgrad_in")
    (grad_x, vs_in), _ = _in_bwd(d_z, w_in_f, x2, d_x1, g_mix_pre, sc_m)
    pending = [("w_in", gw_in[0], near + far)]
    recv["w_in"] = []

    gath = [vs_in, vs_up, vs_ffn, loss_acc]
    red = [cs_ffn, vs_mix, dcw, d_wr, d_wi, d_ws.reshape(N_GROUPS * POS_BLOCK, POS_BLOCK), d_bs]
    return dict(grad_x=grad_x, gath=gath, red=red, recv=recv, pending=pending,
                w_in=gw_in, w_out=gw_out, w_up=gw_up, w_down=gw_down)


def kernel(x, c, w_ada, b_ada, g_mix_pre, g_mix_post, w_in, conv_w, conv_b, w_rgate, b_rgate, w_igate, b_igate, lru_a, v_norm_g, v_norm_b, w_spatial, b_spatial, g_lru_out, g_gmlp_out, w_out, g_ffn_pre, g_ffn_post, w_up, ffn_conv_w, ffn_conv_b, w_down, loss_target, m_w_ada, m_b_ada, m_g_mix_pre, m_g_mix_post, m_w_in, m_conv_w, m_conv_b, m_w_rgate, m_b_rgate, m_w_igate, m_b_igate, m_lru_a, m_v_norm_g, m_v_norm_b, m_w_spatial, m_b_spatial, m_g_lru_out, m_g_gmlp_out, m_w_out, m_g_ffn_pre, m_g_ffn_post, m_w_up, m_ffn_conv_w, m_ffn_conv_b, m_w_down, v_w_ada, v_b_ada, v_g_mix_pre, v_g_mix_post, v_w_in, v_conv_w, v_conv_b, v_w_rgate, v_b_rgate, v_w_igate, v_b_igate, v_lru_a, v_v_norm_g, v_v_norm_b, v_w_spatial, v_b_spatial, v_g_lru_out, v_g_gmlp_out, v_w_out, v_g_ffn_pre, v_g_ffn_post, v_w_up, v_ffn_conv_w, v_ffn_conv_b, v_w_down):
    me = _dev_index(_my_pos())
    ada_cols = w_ada.shape[-1]

    big_w = dict(w_in=(w_in, m_w_in, v_w_in, True), w_out=(w_out, m_w_out, v_w_out, False),
                 w_up=(w_up, m_w_up, v_w_up, True), w_down=(w_down, m_w_down, v_w_down, False))

    def gather(*names):
        return _gather_carry([big_w[n][0][0] for n in names], [STACKED if n == "w_up" else big_w[n][3] for n in names])

    def scatter(parts):
        return _scatter_carry([g for _, g, _ in parts], [big_w[n][0].shape[1:] for n, _, _ in parts],
                              [big_w[n][3] for n, _, _ in parts], [rel for _, _, rel in parts])

    (c_all, cw_all, fcw_all, mod_all), (w_in_f, w_out_f) = _prologue(
        jnp.broadcast_to(c, (SUBLANES, D_MODEL)), _pad_rows(conv_w[0]), _pad_rows(ffn_conv_w[0]), w_ada[0], b_ada,
        carry=gather("w_in", "w_out"))
    conv_w_full = _columns_from_devices(cw_all, LRU_CONV_K)
    ffn_cw_full = _columns_from_devices(fcw_all, FFN_CONV_K)
    mod = lax.dynamic_index_in_dim(mod_all.reshape(N_DEV, N_DEV, ada_cols), me, axis=1, keepdims=False)
    mod = mod.reshape(N_MOD, 1, D_MODEL)

    loc = _local_step(x[0], loss_target[0], mod, w_in_f, dict(w_out=w_out_f), conv_w_full, ffn_cw_full,
                      g_mix_pre, g_mix_post, conv_b, w_rgate, b_rgate, w_igate, b_igate, lru_a, v_norm_g, v_norm_b,
                      w_spatial, b_spatial, g_lru_out, g_gmlp_out, g_ffn_pre, g_ffn_post, ffn_conv_b,
                      gather=gather, scatter=scatter)
    grad_x = loc["grad_x"]

    (gathered, reduced), got = _reduce_small(loc["gath"], loc["red"], carry=scatter(loc["pending"]))
    for (name, _, _), out in zip(loc["pending"], got):
        loc["recv"][name].append(out)

    results = {}
    for name, (w_, m_, v_, cs) in big_w.items():
        results[name] = _adamw_sum(w_, loc[name][1], loc["recv"][name], m_, v_, "adamw_" + name)

    params = dict(
        b_ada=(b_ada, m_b_ada, v_b_ada), g_mix_pre=(g_mix_pre, m_g_mix_pre, v_g_mix_pre),
        g_mix_post=(g_mix_post, m_g_mix_post, v_g_mix_post), conv_b=(conv_b, m_conv_b, v_conv_b),
        w_rgate=(w_rgate, m_w_rgate, v_w_rgate), b_rgate=(b_rgate, m_b_rgate, v_b_rgate),
        w_igate=(w_igate, m_w_igate, v_w_igate), b_igate=(b_igate, m_b_igate, v_b_igate),
        lru_a=(lru_a, m_lru_a, v_lru_a), v_norm_g=(v_norm_g, m_v_norm_g, v_v_norm_g),
        v_norm_b=(v_norm_b, m_v_norm_b, v_v_norm_b), w_spatial=(w_spatial, m_w_spatial, v_w_spatial),
        b_spatial=(b_spatial, m_b_spatial, v_b_spatial), g_lru_out=(g_lru_out, m_g_lru_out, v_g_lru_out),
        g_gmlp_out=(g_gmlp_out, m_g_gmlp_out, v_g_gmlp_out), g_ffn_pre=(g_ffn_pre, m_g_ffn_pre, v_g_ffn_pre),
        g_ffn_post=(g_ffn_post, m_g_ffn_post, v_g_ffn_post), ffn_conv_b=(ffn_conv_b, m_ffn_conv_b, v_ffn_conv_b))
    conv_params = dict(conv_w=(conv_w, m_conv_w, v_conv_w), ffn_conv_w=(ffn_conv_w, m_ffn_conv_w, v_ffn_conv_w))
    small_results, loss = _adamw_small(gathered, reduced, params, conv_params)
    results.update(small_results)
    loss = loss.reshape(())

    results["w_ada"] = _adamw_wada(c_all, gathered[0], gathered[1], gathered[2], w_ada, m_w_ada, v_w_ada)

    order = ["w_ada", "b_ada", "g_mix_pre", "g_mix_post", "w_in", "conv_w", "conv_b", "w_rgate", "b_rgate", "w_igate",
             "b_igate", "lru_a", "v_norm_g", "v_norm_b", "w_spatial", "b_spatial", "g_lru_out", "g_gmlp_out", "w_out",
             "g_ffn_pre", "g_ffn_post", "w_up", "ffn_conv_w", "ffn_conv_b", "w_down"]
    outs = [loss, grad_x[None]]
    for kind in range(4):
        outs += [results[n][kind] for n in order]
    return tuple(outs)
```

```python
import functools

import jax
import jax.numpy as jnp
from jax import lax
from jax.experimental import pallas as pl
from jax.experimental.pallas import tpu as pltpu

F32 = jnp.float32
BF16 = jnp.bfloat16

D_MODEL = 1024
LRU_W = 512
GMLP_W = 512
N_HEADS = 8
HEAD_DIM = 64
N_GROUPS = 4
POS_BLOCK = 128
CHUNK = 64
IN_COLS = 2048
D_FF = 3072
N_MOD = 6
N_DEV = 8
EPS = 1e-6
LRU_C = 8.0
LRU_CONV_K = 4
FFN_CONV_K = 3

ADAM_LR = 0.001
ADAM_B1 = 0.9
ADAM_B2 = 0.999
ADAM_EPS = 1e-08
ADAM_WD = 0.01
ADAM_STEP = 10

LANES = 128
SUBLANES = 8
TT_BIG = 512
TT_MIX = 256
FF_CW = 512
VMEM_LIMIT = 56 * 1024 * 1024

MESH = pl.DeviceIdType.MESH


def _sds(shape, dtype):
    return jax.ShapeDtypeStruct(shape, dtype)


def _cparams(sem=None):
    return pltpu.CompilerParams(dimension_semantics=sem, vmem_limit_bytes=VMEM_LIMIT)


def _whole():
    return pl.BlockSpec(memory_space=pltpu.VMEM)


def _const(shape):
    nd = len(shape)
    return pl.BlockSpec(shape, lambda *_: (0,) * nd)


def _any():
    return pl.BlockSpec(memory_space=pl.ANY)


class _Carry:
    def __init__(self, inputs, in_specs, out_shape, out_specs, scratch, start, finish):
        self.inputs, self.in_specs, self.out_shape, self.out_specs = inputs, in_specs, out_shape, out_specs
        self.scratch, self.start, self.finish = scratch, start, finish


def _call(body, name, grid, in_specs, out_specs, out_shape, scratch, args, carry=None, body_starts_carry=False):
    n_in, n_out, n_scr = len(in_specs), len(out_specs), len(scratch)
    c_in = len(carry.in_specs) if carry else 0
    c_out = len(carry.out_specs) if carry else 0

    def full_body(*refs):
        ins = refs[:n_in]
        c_ins = refs[n_in:n_in + c_in]
        outs = refs[n_in + c_in:n_in + c_in + n_out]
        c_outs = refs[n_in + c_in + n_out:n_in + c_in + n_out + c_out]
        scr = refs[n_in + c_in + n_out + c_out:n_in + c_in + n_out + c_out + n_scr]
        c_scr = refs[n_in + c_in + n_out + c_out + n_scr:]
        if carry:
            first = functools.reduce(lambda a, b: a & b, [pl.program_id(d) == 0 for d in range(len(grid))])
            last = functools.reduce(lambda a, b: a & b, [pl.program_id(d) == g - 1 for d, g in enumerate(grid)])

        if carry and not body_starts_carry:
            @pl.when(first)
            def _():
                carry.start(c_ins, c_outs, c_scr)

        if body_starts_carry:
            body(*ins, *outs, *scr, start_carry=(lambda: carry.start(c_ins, c_outs, c_scr)) if carry else (lambda: None))
        else:
            body(*ins, *outs, *scr)
        if carry:
            @pl.when(last)
            def _():
                carry.finish(c_ins, c_outs, c_scr)

    res = pl.pallas_call(
        full_body, name=name, grid=grid,
        in_specs=list(in_specs) + (list(carry.in_specs) if carry else []),
        out_specs=list(out_specs) + (list(carry.out_specs) if carry else []),
        out_shape=list(out_shape) + (list(carry.out_shape) if carry else []),
        scratch_shapes=list(scratch) + (list(carry.scratch) if carry else []),
        compiler_params=_cparams(("arbitrary",) * len(grid)),
    )(*args, *(carry.inputs if carry else []))
    return res[:n_out], res[n_out:]


GELU_C0 = 0.7978845608028654
GELU_C1 = GELU_C0 * 0.044715


def _gelu(x):
    t = jnp.tanh(x * (GELU_C0 + GELU_C1 * (x * x)))
    hx = 0.5 * x
    return hx + hx * t


def _gelu_and_grad(x):
    x2 = x * x
    t = jnp.tanh(x * (GELU_C0 + GELU_C1 * x2))
    hx = 0.5 * x
    g = hx + hx * t
    dg = (0.5 + 0.5 * t) + hx * (1.0 - t * t) * (GELU_C0 + 3.0 * GELU_C1 * x2)
    return g, dg


def _sigmoid(x):
    return 1.0 / (1.0 + jnp.exp(-x))


def _softplus(x):
    return jnp.maximum(x, 0.0) + jnp.log1p(jnp.exp(-jnp.abs(x)))


def _neg_expm1(x):
    series = -x * (1.0 + x * (0.5 + x * (1.0 / 6.0 + x * (1.0 / 24.0 + x * (1.0 / 120.0)))))
    return jnp.where(x > -0.1, series, 1.0 - jnp.exp(x))


def _dot(a, b):
    return jnp.dot(a.astype(BF16), b.astype(BF16), preferred_element_type=F32)


def _dot_nt(a, b):
    return lax.dot_general(a.astype(BF16), b.astype(BF16), (((1,), (1,)), ((), ())), preferred_element_type=F32)


def _dot_tn(a, b):
    return lax.dot_general(a.astype(BF16), b.astype(BF16), (((0,), (0,)), ((), ())), preferred_element_type=F32)


def _rows(shape):
    return lax.broadcasted_iota(jnp.int32, shape, 0)


def _shift_down(cur, prev8, s):
    if s == 0:
        return cur
    n = cur.shape[0]
    r = pltpu.roll(cur, s, 0)
    p = pltpu.roll(prev8, s, 0)
    top = jnp.where(_rows(p.shape) < s, p, r[0:SUBLANES])
    if n == SUBLANES:
        return top
    return jnp.concatenate([top, r[SUBLANES:]], axis=0)


def _shift_up(cur, next8, s):
    if s == 0:
        return cur
    n = cur.shape[0]
    r = pltpu.roll(cur, n - s, 0)
    q = pltpu.roll(next8, SUBLANES - s, 0)
    bot = jnp.where(_rows(q.shape) >= SUBLANES - s, q, r[n - SUBLANES:])
    if n == SUBLANES:
        return bot
    return jnp.concatenate([r[:n - SUBLANES], bot], axis=0)


def _scan_fwd(a, b, h_in):
    n = a.shape[0]
    in_group = _rows(a.shape) & (SUBLANES - 1)
    s = 1
    while s < SUBLANES:
        a_s = pltpu.roll(a, s, 0)
        b_s = pltpu.roll(b, s, 0)
        m = in_group >= s
        b = jnp.where(m, a * b_s + b, b)
        a = jnp.where(m, a * a_s, a)
        s *= 2
    out, carry = [], h_in
    for g in range(n // SUBLANES):
        rows = slice(g * SUBLANES, (g + 1) * SUBLANES)
        h_g = a[rows] * carry + b[rows]
        out.append(h_g)
        carry = h_g[SUBLANES - 1:SUBLANES, :]
    return jnp.concatenate(out, axis=0)


def _scan_rev(a, b, l_in):
    n = a.shape[0]
    in_group = _rows(a.shape) & (SUBLANES - 1)
    s = 1
    while s < SUBLANES:
        a_s = pltpu.roll(a, n - s, 0)
        b_s = pltpu.roll(b, n - s, 0)
        m = in_group < SUBLANES - s
        b = jnp.where(m, b + a * b_s, b)
        a = jnp.where(m, a * a_s, a)
        s *= 2
    out, carry = [], l_in
    for g in reversed(range(n // SUBLANES)):
        rows = slice(g * SUBLANES, (g + 1) * SUBLANES)
        l_g = b[rows] + a[rows] * carry
        out.append(l_g)
        carry = l_g[0:1, :]
    return jnp.concatenate(out[::-1], axis=0)


def _rms(x):
    r = lax.rsqrt(jnp.mean(x * x, axis=-1, keepdims=True) + EPS)
    return x * r, r


def _rms_bwd(d_n, n, r):
    return r * (d_n - n * jnp.mean(d_n * n, axis=-1, keepdims=True))


def _colsum(x):
    return jnp.sum(x, axis=0, keepdims=True)


ROW_PIECE = 256


def _row_pieces(tt):
    return [slice(r, r + min(ROW_PIECE, tt)) for r in range(0, tt, min(ROW_PIECE, tt))]


def _lru_gates(xc, wr_ref, wi_ref, br, bi, sp_a):
    r = _sigmoid(_dot(xc, wr_ref[...]) + br)
    i = _sigmoid(_dot(xc, wi_ref[...]) + bi)
    la = -LRU_C * r * sp_a
    a = jnp.exp(la)
    mult = jnp.sqrt(_neg_expm1(2.0 * la))
    return r, i, a, mult


def _lru_conv(lx, prev8, cw_ref, cb):
    xc = cb + cw_ref[LRU_CONV_K - 1:LRU_CONV_K, :] * lx
    taps = []
    for k in range(LRU_CONV_K - 1):
        tap = _shift_down(lx, prev8, LRU_CONV_K - 1 - k)
        taps.append(tap)
        xc = xc + cw_ref[k:k + 1, :] * tap
    return xc, taps


def _ws_mask(transposed=False):
    i = lax.broadcasted_iota(jnp.int32, (POS_BLOCK, POS_BLOCK), 0)
    j = lax.broadcasted_iota(jnp.int32, (POS_BLOCK, POS_BLOCK), 1)
    if transposed:
        i, j = j, i
    return (j // CHUNK) <= (i // CHUNK)


def _gmlp_v(gv, vg, vb):
    av, dav = _gelu_and_grad(gv)
    mu = jnp.mean(av, axis=-1, keepdims=True)
    cen = av - mu
    rs = lax.rsqrt(jnp.mean(cen * cen, axis=-1, keepdims=True) + EPS)
    vhat = cen * rs
    return vhat * vg + vb, vhat, rs, dav


def _mix_fwd(x, sh, sc, g_pre, w_in, conv_w, conv_b, wr_bd, wi_bd, b_r, b_i, lru_a, vn_g, vn_b, w_sp, b_sp_t,
             g_lru, g_gmlp, w_out, g_post, gt_m, g_ffn_pre, sc_f, sh_f, carry=None):
    s_len = x.shape[0]
    tt = min(TT_MIX, s_len)
    nblk = tt // POS_BLOCK

    def body(x_ref, sh_ref, sc_ref, g_ref, w_ref, cw_ref, cb_ref, wr_ref, wi_ref, br_ref, bi_ref, la_ref, vg_ref,
             vb_ref, ws_ref, bst_ref, gl_ref, gg_ref, wo_ref, gp_ref, gtm_ref, g2_ref, scf_ref, shf_ref,
             z_ref, h_ref, y_ref, hl_ref, yo_ref, x1_ref, h2_ref, prev8, hcar):
        i = pl.program_id(0)

        @pl.when(i == 0)
        def _():
            prev8[...] = jnp.zeros_like(prev8)
            hcar[...] = jnp.zeros_like(hcar)

        n_x, _ = _rms(x_ref[...])
        h = (n_x * g_ref[...] * (1.0 + sc_ref[...]) + sh_ref[...]).astype(BF16)
        h_ref[...] = h
        z_ref[...] = jnp.dot(h, w_ref[...], preferred_element_type=F32)

        lx = z_ref[:, 0:LRU_W]
        gate = z_ref[:, LRU_W:2 * LRU_W]
        gu = z_ref[:, 2 * LRU_W:2 * LRU_W + GMLP_W]
        gv = z_ref[:, 2 * LRU_W + GMLP_W:]

        xc, _ = _lru_conv(lx, prev8[...], cw_ref, cb_ref[...])
        prev8[...] = lx[tt - SUBLANES:]
        sp_a = _softplus(-la_ref[...])
        _, ig, a, mult = _lru_gates(xc, wr_ref, wi_ref, br_ref[...], bi_ref[...], sp_a)
        bx = mult * (ig * xc)
        hl = _scan_fwd(a, bx, hcar[0:1, :])
        hcar[...] = jnp.broadcast_to(hl[tt - 1:tt, :], hcar.shape)
        hl_ref[...] = hl
        y_lru = hl * _gelu(gate)
        n_l, _ = _rms(y_lru)
        y_ref[:, 0:LRU_W] = (n_l * gl_ref[...]).astype(BF16)

        u = _gelu(gu)
        v, _, _, _ = _gmlp_v(gv, vg_ref[...], vb_ref[...])
        mask = _ws_mask()
        sp_parts = []
        for nb in range(nblk):
            row = []
            for g in range(N_GROUPS):
                wsm = jnp.where(mask, ws_ref[g], 0.0)
                vblk = v[nb * POS_BLOCK:(nb + 1) * POS_BLOCK, g * LANES:(g + 1) * LANES]
                row.append(_dot(wsm, vblk) + bst_ref[:, g:g + 1])
            sp_parts.append(jnp.concatenate(row, axis=1))
        sp = jnp.concatenate(sp_parts, axis=0) if nblk > 1 else sp_parts[0]
        n_g, _ = _rms(u * sp)
        y_ref[:, LRU_W:] = (n_g * gg_ref[...]).astype(BF16)

        y = jnp.dot(y_ref[...], wo_ref[...], preferred_element_type=F32)
        yo_ref[...] = y
        n_y, _ = _rms(y)
        x1 = x_ref[...] + gtm_ref[...] * (n_y * gp_ref[...])
        x1_ref[...] = x1
        n1, _ = _rms(x1)
        h2_ref[...] = (n1 * g2_ref[...] * (1.0 + scf_ref[...]) + shf_ref[...]).astype(BF16)

    row = lambda c: pl.BlockSpec((tt, c), lambda i: (i, 0))
    v512 = _const((1, LRU_W))
    vec = _const((1, D_MODEL))
    return _call(
        body, "mix_fwd", (s_len // tt,),
        in_specs=[row(D_MODEL), vec, vec, vec, _whole(),
                  _const((LRU_CONV_K, LRU_W)), v512, _whole(), _whole(), v512, v512, v512, v512, v512,
                  _whole(), _whole(), v512, v512, _whole(), vec, vec, vec, vec, vec],
        out_specs=[row(IN_COLS), row(D_MODEL), row(LRU_W + GMLP_W), row(LRU_W), row(D_MODEL), row(D_MODEL),
                   row(D_MODEL)],
        out_shape=[_sds((s_len, IN_COLS), F32), _sds((s_len, D_MODEL), BF16),
                   _sds((s_len, LRU_W + GMLP_W), BF16), _sds((s_len, LRU_W), F32),
                   _sds((s_len, D_MODEL), F32), _sds((s_len, D_MODEL), F32), _sds((s_len, D_MODEL), BF16)],
        scratch=[pltpu.VMEM((SUBLANES, LRU_W), F32), pltpu.VMEM((SUBLANES, LRU_W), F32)],
        args=(x, sh, sc, g_pre, w_in, conv_w, conv_b, wr_bd, wi_bd, b_r, b_i, lru_a, vn_g, vn_b, w_sp, b_sp_t,
              g_lru, g_gmlp, w_out, g_post, gt_m, g_ffn_pre, sc_f, sh_f), carry=carry)


FF_CHUNKS = N_DEV // 2
FF_CHUNK_W = D_FF // FF_CHUNKS


def _ffn_fwd(h2, w_up3, ffn_cw, ffn_cb, carry=None):
    s_len = h2.shape[0]
    tt = min(TT_MIX, s_len)
    nc, cw = FF_CHUNKS, FF_CHUNK_W

    def body(h2_ref, wu_ref, cwg_ref, cwv_ref, cbg_ref, cbv_ref, up_ref, upc_ref, act_ref, prev):
        i = pl.program_id(0)
        c = pl.program_id(1)

        @pl.when(i == 0)
        def _():
            prev[c] = jnp.zeros((2, SUBLANES, cw), F32)

        h2 = h2_ref[...]
        ug_pre = jnp.dot(h2, wu_ref[c], preferred_element_type=F32)
        uv_pre = jnp.dot(h2, wu_ref[nc + c], preferred_element_type=F32)
        up_ref[0] = ug_pre.astype(BF16)
        up_ref[1] = uv_pre.astype(BF16)
        ug, _ = _ffn_conv(ug_pre, prev[c, 0], cwg_ref, cbg_ref[...])
        uv, _ = _ffn_conv(uv_pre, prev[c, 1], cwv_ref, cbv_ref[...])
        prev[c, 0] = ug_pre[tt - SUBLANES:, :]
        prev[c, 1] = uv_pre[tt - SUBLANES:, :]
        upc_ref[0] = ug
        upc_ref[1] = uv
        act_ref[...] = (_gelu(ug) * uv).astype(BF16)

    chunk2 = pl.BlockSpec((2, tt, cw), lambda i, c: (0, i, c))
    ffn_cb2 = ffn_cb.reshape(1, 2 * D_FF)
    return _call(
        body, "ffn_fwd", (s_len // tt, nc),
        in_specs=[pl.BlockSpec((tt, D_MODEL), lambda i, c: (i, 0)), _whole(),
                  pl.BlockSpec((FFN_CONV_K, cw), lambda i, c: (0, c)),
                  pl.BlockSpec((FFN_CONV_K, cw), lambda i, c: (0, c + nc)),
                  pl.BlockSpec((1, cw), lambda i, c: (0, c)),
                  pl.BlockSpec((1, cw), lambda i, c: (0, c + nc))],
        out_specs=[chunk2, chunk2, pl.BlockSpec((tt, cw), lambda i, c: (i, c))],
        out_shape=[_sds((2, s_len, D_FF), BF16), _sds((2, s_len, D_FF), F32), _sds((s_len, D_FF), BF16)],
        scratch=[pltpu.VMEM((nc, 2, SUBLANES, cw), F32)],
        args=(h2, w_up3, ffn_cw, ffn_cw, ffn_cb2, ffn_cb2), carry=carry)


def _ffn_tail(act, w_down, x1, gt_f, g_post, target):
    s_len = x1.shape[0]
    tt = min(TT_BIG, s_len)

    def body(act_ref, wd_ref, x1_ref, gtf_ref, gp_ref, tg_ref, dy2_ref, dout_ref, loss_ref, vs_ref):
        @pl.when(pl.program_id(0) == 0)
        def _():
            loss_ref[...] = jnp.zeros_like(loss_ref)
            vs_ref[...] = jnp.zeros_like(vs_ref)

        for rows in _row_pieces(tt):
            n2, r2 = _rms(jnp.dot(act_ref[rows, :], wd_ref[...], preferred_element_type=F32))
            out = x1_ref[rows, :] + gtf_ref[...] * (n2 * gp_ref[...])
            err = out - tg_ref[rows, :]
            do = err * (1.0 / D_MODEL)
            dout_ref[rows, :] = do
            loss_ref[...] += jnp.broadcast_to(0.5 * jnp.sum(err * err, keepdims=True) * (1.0 / D_MODEL),
                                              loss_ref.shape)
            vs_ref[0:1, :] += _colsum(do * n2 * gp_ref[...])
            vs_ref[1:2, :] += _colsum(do * gtf_ref[...] * n2)
            dy2_ref[rows, :] = _rms_bwd(do * gtf_ref[...] * gp_ref[...], n2, r2).astype(BF16)

    row = lambda c: pl.BlockSpec((tt, c), lambda i: (i, 0))
    vec = _const((1, D_MODEL))
    outs, _ = _call(
        body, "ffn_tail", (s_len // tt,),
        in_specs=[row(D_FF), _whole(), row(D_MODEL), vec, vec, row(D_MODEL)],
        out_specs=[row(D_MODEL), row(D_MODEL), _const((SUBLANES, LANES)), _const((SUBLANES, D_MODEL))],
        out_shape=[_sds((s_len, D_MODEL), BF16), _sds((s_len, D_MODEL), F32), _sds((SUBLANES, LANES), F32),
                   _sds((SUBLANES, D_MODEL), F32)],
        scratch=[], args=(act, w_down, x1, gt_f, g_post, target))
    return outs


def _ffn_conv(up_pre, prev8, cw_ref, cb):
    up = cb + cw_ref[FFN_CONV_K - 1:FFN_CONV_K, :] * up_pre
    taps = []
    for k in range(FFN_CONV_K - 1):
        tap = _shift_down(up_pre, prev8, FFN_CONV_K - 1 - k)
        taps.append(tap)
        up = up + cw_ref[k:k + 1, :] * tap
    return up, taps


def _ffn_bwd(d_y2, up_pre, up, ffn_cw, w_down, carry=None):
    s_len = d_y2.shape[0]
    tt = min(TT_BIG, s_len)
    nt = s_len // tt
    cw = FF_CW
    nc = D_FF // cw

    def body(dy2_ref, up_ref, upc_ref, cwg_ref, cwv_ref, wd_ref, dup_ref, cs_ref, nxt, cs_acc):
        i = pl.program_id(0)
        c = pl.program_id(1)

        @pl.when(i == 0)
        def _():
            nxt[c] = jnp.zeros((2, SUBLANES, cw), F32)
            cs_acc[c] = jnp.zeros((2, SUBLANES, cw), F32)

        pw = cw // 2
        for piece in range(2):
            cols = slice(piece * pw, (piece + 1) * pw)
            d_act = _dot_nt(dy2_ref[...], wd_ref[pl.ds(pl.multiple_of(c * cw + piece * pw, pw), pw), :])
            uv = upc_ref[1, :, cols]
            gl, dgl = _gelu_and_grad(upc_ref[0, :, cols])
            d_ug = d_act * uv * dgl
            d_uv = d_act * gl
            for half, (d_u, cw_ref) in enumerate(((d_ug, cwg_ref), (d_uv, cwv_ref))):
                nx = nxt[c, half, :, cols]
                x_in = up_ref[half, :, cols].astype(F32)
                d_pre = cw_ref[FFN_CONV_K - 1:FFN_CONV_K, cols] * d_u
                sums = [None] * (FFN_CONV_K + 1)
                sums[FFN_CONV_K - 1] = _colsum(d_u * x_in)
                for k in range(FFN_CONV_K - 1):
                    ahead = _shift_up(d_u, nx, FFN_CONV_K - 1 - k)
                    d_pre = d_pre + cw_ref[k:k + 1, cols] * ahead
                    sums[k] = _colsum(ahead * x_in)
                sums[FFN_CONV_K] = _colsum(d_u)
                pad = jnp.zeros((SUBLANES - FFN_CONV_K - 1, pw), F32)
                cs_acc[c, half, :, cols] += jnp.concatenate(sums + [pad], axis=0)
                nxt[c, half, :, cols] = d_u[0:SUBLANES]
                dup_ref[half, :, cols] = d_pre.astype(BF16)

        for cc in range(nc):
            @pl.when((i == nt - 1) & (c == cc))
            def _():
                cs_ref[:, cc * cw:(cc + 1) * cw] = cs_acc[cc, 0]
                cs_ref[:, D_FF + cc * cw:D_FF + (cc + 1) * cw] = cs_acc[cc, 1]

    row = pl.BlockSpec((tt, D_MODEL), lambda i, c: (nt - 1 - i, 0))
    blk = pl.BlockSpec((2, tt, cw), lambda i, c: (0, nt - 1 - i, c))
    return _call(
        body, "ffn_bwd", (nt, nc),
        in_specs=[row, blk, blk,
                  pl.BlockSpec((FFN_CONV_K, cw), lambda i, c: (0, c)),
                  pl.BlockSpec((FFN_CONV_K, cw), lambda i, c: (0, c + nc)),
                  _whole()],
        out_specs=[blk, _const((SUBLANES, 2 * D_FF))],
        out_shape=[_sds((2, s_len, D_FF), BF16), _sds((SUBLANES, 2 * D_FF), F32)],
        scratch=[pltpu.VMEM((nc, 2, SUBLANES, cw), F32), pltpu.VMEM((nc, 2, SUBLANES, cw), F32)],
        args=(d_y2, up_pre, up, ffn_cw, ffn_cw, w_down), carry=carry)


def _up_bwd(d_up, w_up3, x1, dout, y, w_out, g_pre, sc_f, g_post, gt_m, carry=None):
    s_len = x1.shape[0]
    tt = min(TT_BIG, s_len)

    def body(du_ref, wu_ref, x1_ref, do_ref, y_ref, wo_ref, g2_ref, sc_ref, gp_ref, gt_ref,
             dx1_ref, dy_ref, dyc_ref, vs_ref):
        @pl.when(pl.program_id(0) == 0)
        def _():
            vs_ref[...] = jnp.zeros_like(vs_ref)

        for rows in _row_pieces(tt):
            d_h2 = jnp.zeros((rows.stop - rows.start, D_MODEL), F32)
            for half in range(2):
                for ch in range(FF_CHUNKS):
                    d_h2 = d_h2 + _dot_nt(du_ref[half, rows, ch * FF_CHUNK_W:(ch + 1) * FF_CHUNK_W],
                                          wu_ref[half * FF_CHUNKS + ch])
            n1, r1 = _rms(x1_ref[rows, :])
            ng = n1 * g2_ref[...]
            vs_ref[0:1, :] += _colsum(d_h2)
            vs_ref[1:2, :] += _colsum(d_h2 * ng)
            d_ng = d_h2 * (1.0 + sc_ref[...])
            vs_ref[2:3, :] += _colsum(d_ng * n1)
            d_x1 = do_ref[rows, :] + _rms_bwd(d_ng * g2_ref[...], n1, r1)
            dx1_ref[rows, :] = d_x1
            n_y, r_y = _rms(y_ref[rows, :])
            vs_ref[3:4, :] += _colsum(d_x1 * n_y * gp_ref[...])
            d_on = d_x1 * gt_ref[...]
            vs_ref[4:5, :] += _colsum(d_on * n_y)
            d_y = _rms_bwd(d_on * gp_ref[...], n_y, r_y).astype(BF16)
            dy_ref[rows, :] = d_y
            dyc_ref[rows, :] = _dot_nt(d_y, wo_ref[...])

    row = lambda c: pl.BlockSpec((tt, c), lambda i: (i, 0))
    vec = _const((1, D_MODEL))
    return _call(
        body, "up_bwd", (s_len // tt,),
        in_specs=[pl.BlockSpec((2, tt, D_FF), lambda i: (0, i, 0)), _whole(), row(D_MODEL), row(D_MODEL), row(D_MODEL),
                  _whole(), vec, vec, vec, vec],
        out_specs=[row(D_MODEL), row(D_MODEL), row(LRU_W + GMLP_W), _const((SUBLANES, D_MODEL))],
        out_shape=[_sds((s_len, D_MODEL), F32), _sds((s_len, D_MODEL), BF16), _sds((s_len, LRU_W + GMLP_W), F32),
                   _sds((SUBLANES, D_MODEL), F32)],
        scratch=[], args=(d_up, w_up3, x1, dout, y, w_out, g_pre, sc_f, g_post, gt_m), carry=carry)


def _head_pair_block(hd):
    return (slice((hd // 2) * HEAD_DIM, (hd // 2 + 1) * HEAD_DIM), slice((hd % 2) * HEAD_DIM, (hd % 2 + 1) * HEAD_DIM))


def _mix_bwd(d_ycat, z, hl, conv_w, conv_b, wr_bd, wi_bd, b_r, b_i, lru_a, vn_g, vn_b, w_sp, w_sp_t, b_sp_t,
             g_lru, g_gmlp, carry=None):
    s_len = z.shape[0]
    tt = min(TT_MIX, s_len)
    nt = s_len // tt
    nblk = tt // POS_BLOCK
    hb = tt // SUBLANES

    def body(dyc_ref, z_ref, zh_ref, hl_ref, hh_ref, cw_ref, cb_ref, wr_ref, wi_ref, br_ref, bi_ref, la_ref,
             vg_ref, vb_ref, ws_ref, wst_ref, bst_ref, gl_ref, gg_ref,
             dz_ref, vs_ref, dcw_ref, dwrb_ref, dwib_ref, dws_ref, dbs_ref, nxt_dxc, nxt_a, nxt_lam, dwr_ref, dwi_ref):
        i = pl.program_id(0)
        first_tile = i == nt - 1

        @pl.when(i == 0)
        def _():
            for ref in (vs_ref, dcw_ref, dwr_ref, dwi_ref, dws_ref, dbs_ref, nxt_dxc, nxt_a, nxt_lam):
                ref[...] = jnp.zeros_like(ref)

        lx = z_ref[:, 0:LRU_W]
        gate = z_ref[:, LRU_W:2 * LRU_W]
        gu = z_ref[:, 2 * LRU_W:2 * LRU_W + GMLP_W]
        gv = z_ref[:, 2 * LRU_W + GMLP_W:]
        prev8 = jnp.where(first_tile, 0.0, zh_ref[...])
        hprev8 = jnp.where(first_tile, 0.0, hh_ref[...])

        xc, taps = _lru_conv(lx, prev8, cw_ref, cb_ref[...])
        a_par = la_ref[...]
        sp_a = _softplus(-a_par)
        r, ig, a, mult = _lru_gates(xc, wr_ref, wi_ref, br_ref[...], bi_ref[...], sp_a)
        hl = hl_ref[...]
        h_prev = _shift_down(hl, hprev8, 1)
        ggate, dggate = _gelu_and_grad(gate)
        y_lru = hl * ggate
        n_l, r_l = _rms(y_lru)
        d_nl = dyc_ref[:, 0:LRU_W]
        vs_ref[6:7, :] += _colsum(d_nl * n_l)
        d_yl = _rms_bwd(d_nl * gl_ref[...], n_l, r_l)
        d_hl = d_yl * ggate
        d_gate = d_yl * hl * dggate
        a_up = _shift_up(a, nxt_a[...], 1)
        lam = _scan_rev(a_up, d_hl, nxt_lam[0:1, :])
        nxt_a[...] = jnp.broadcast_to(a[0:1, :], nxt_a.shape)
        nxt_lam[...] = jnp.broadcast_to(lam[0:1, :], nxt_lam.shape)
        ixc = ig * xc
        d_la = lam * h_prev * a - lam * ixc * (a * a) / mult
        d_i = lam * mult * xc
        d_xc = lam * mult * ig
        vs_ref[3:4, :] += _colsum(d_la * r) * (LRU_C * _sigmoid(-a_par))
        d_pr = d_la * (-LRU_C * sp_a) * r * (1.0 - r)
        d_pi = d_i * ig * (1.0 - ig)
        vs_ref[1:2, :] += _colsum(d_pr)
        vs_ref[2:3, :] += _colsum(d_pi)
        dwr_ref[...] += _dot_tn(xc, d_pr)
        dwi_ref[...] += _dot_tn(xc, d_pi)
        d_xc = d_xc + _dot_nt(d_pr, wr_ref[...]) + _dot_nt(d_pi, wi_ref[...])
        vs_ref[0:1, :] += _colsum(d_xc)
        nx = nxt_dxc[...]
        d_lx = cw_ref[LRU_CONV_K - 1:LRU_CONV_K, :] * d_xc
        dcw_ref[LRU_CONV_K - 1:LRU_CONV_K, :] += _colsum(d_xc * lx)
        for k in range(LRU_CONV_K - 1):
            d_lx = d_lx + cw_ref[k:k + 1, :] * _shift_up(d_xc, nx, LRU_CONV_K - 1 - k)
            dcw_ref[k:k + 1, :] += _colsum(d_xc * taps[k])
        nxt_dxc[...] = d_xc[0:SUBLANES]
        dz_ref[:, 0:LRU_W] = d_lx.astype(BF16)
        dz_ref[:, LRU_W:2 * LRU_W] = d_gate.astype(BF16)

        u, du = _gelu_and_grad(gu)
        v, vhat, rs, dav = _gmlp_v(gv, vg_ref[...], vb_ref[...])
        mask = _ws_mask()
        sp_parts = []
        for nb in range(nblk):
            rowp = []
            for g in range(N_GROUPS):
                wsm = jnp.where(mask, ws_ref[g], 0.0)
                vblk = v[nb * POS_BLOCK:(nb + 1) * POS_BLOCK, g * LANES:(g + 1) * LANES]
                rowp.append(_dot(wsm, vblk) + bst_ref[:, g:g + 1])
            sp_parts.append(jnp.concatenate(rowp, axis=1))
        sp = jnp.concatenate(sp_parts, axis=0) if nblk > 1 else sp_parts[0]
        y_g = u * sp
        n_g, r_g = _rms(y_g)
        d_ng = dyc_ref[:, LRU_W:]
        vs_ref[7:8, :] += _colsum(d_ng * n_g)
        d_yg = _rms_bwd(d_ng * gg_ref[...], n_g, r_g)
        d_gu = d_yg * sp * du
        d_sp = d_yg * u
        mask_t = _ws_mask(transposed=True)
        ones8 = jnp.ones((SUBLANES, LANES), F32)
        dv_parts = []
        for nb in range(nblk):
            rowp = []
            for g in range(N_GROUPS):
                rs_, cs_ = slice(nb * POS_BLOCK, (nb + 1) * POS_BLOCK), slice(g * LANES, (g + 1) * LANES)
                dsp_blk = d_sp[rs_, cs_]
                dbs_ref[g:g + 1, :] += lax.dot_general(
                    ones8, dsp_blk, (((1,), (1,)), ((), ())), preferred_element_type=F32,
                    precision=lax.Precision.HIGHEST)[0:1, :]
                dws_ref[g] += _dot_nt(dsp_blk, v[rs_, cs_])
                wsm_t = jnp.where(mask_t, wst_ref[g], 0.0)
                rowp.append(_dot(wsm_t, dsp_blk))
            dv_parts.append(jnp.concatenate(rowp, axis=1))
        d_v = jnp.concatenate(dv_parts, axis=0) if nblk > 1 else dv_parts[0]
        vs_ref[4:5, :] += _colsum(d_v * vhat)
        vs_ref[5:6, :] += _colsum(d_v)
        d_vh = d_v * vg_ref[...]
        d_av = rs * (d_vh - jnp.mean(d_vh, axis=-1, keepdims=True)
                     - vhat * jnp.mean(d_vh * vhat, axis=-1, keepdims=True))
        dz_ref[:, 2 * LRU_W:2 * LRU_W + GMLP_W] = d_gu.astype(BF16)
        dz_ref[:, 2 * LRU_W + GMLP_W:] = (d_av * dav).astype(BF16)

        @pl.when(i == nt - 1)
        def _():
            for hd in range(N_HEADS):
                blk = slice(hd * HEAD_DIM, (hd + 1) * HEAD_DIM)
                dwrb_ref[_head_pair_block(hd)] = dwr_ref[blk, blk]
                dwib_ref[_head_pair_block(hd)] = dwi_ref[blk, blk]
            for g in range(N_GROUPS):
                dws_ref[g] = jnp.where(mask, dws_ref[g], 0.0)

    rev = lambda c: pl.BlockSpec((tt, c), lambda i: (nt - 1 - i, 0))
    halo = pl.BlockSpec((SUBLANES, LRU_W), lambda i: (jnp.maximum((nt - 1 - i) * hb - 1, 0), 0))
    v512 = _const((1, LRU_W))
    return _call(
        body, "mix_bwd", (nt,),
        in_specs=[rev(LRU_W + GMLP_W), rev(IN_COLS), halo, rev(LRU_W), halo,
                  _const((LRU_CONV_K, LRU_W)), v512, _whole(), _whole(), v512, v512, v512, v512, v512,
                  _whole(), _whole(), _whole(), v512, v512],
        out_specs=[rev(IN_COLS), _const((SUBLANES, LRU_W)), _const((SUBLANES, LRU_W)),
                   _const((LRU_W // 2, 2 * HEAD_DIM)), _const((LRU_W // 2, 2 * HEAD_DIM)),
                   _const((N_GROUPS, POS_BLOCK, POS_BLOCK)), _const((SUBLANES, POS_BLOCK))],
        out_shape=[_sds((s_len, IN_COLS), BF16), _sds((SUBLANES, LRU_W), F32), _sds((SUBLANES, LRU_W), F32),
                   _sds((LRU_W // 2, 2 * HEAD_DIM), F32), _sds((LRU_W // 2, 2 * HEAD_DIM), F32),
                   _sds((N_GROUPS, POS_BLOCK, POS_BLOCK), F32), _sds((SUBLANES, POS_BLOCK), F32)],
        scratch=[pltpu.VMEM((SUBLANES, LRU_W), F32), pltpu.VMEM((SUBLANES, LRU_W), F32),
                 pltpu.VMEM((SUBLANES, LRU_W), F32), pltpu.VMEM((LRU_W, LRU_W), F32), pltpu.VMEM((LRU_W, LRU_W), F32)],
        args=(d_ycat, z, z, hl, hl, conv_w, conv_b, wr_bd, wi_bd, b_r, b_i, lru_a, vn_g, vn_b, w_sp, w_sp_t, b_sp_t,
              g_lru, g_gmlp), carry=carry)


def _in_bwd(d_z, w_in, x, d_x1, g, sc, carry=None):
    s_len = x.shape[0]
    tt = min(TT_BIG, s_len)

    def body(dz_ref, w_ref, x_ref, dx1_ref, g_ref, sc_ref, gx_ref, vs_ref):
        @pl.when(pl.program_id(0) == 0)
        def _():
            vs_ref[...] = jnp.zeros_like(vs_ref)

        for rows in _row_pieces(tt):
            d_h = _dot_nt(dz_ref[rows, :], w_ref[...])
            n, r = _rms(x_ref[rows, :])
            vs_ref[0:1, :] += _colsum(d_h)
            vs_ref[1:2, :] += _colsum(d_h * n * g_ref[...])
            d_ng = d_h * (1.0 + sc_ref[...])
            vs_ref[2:3, :] += _colsum(d_ng * n)
            gx_ref[rows, :] = dx1_ref[rows, :] + _rms_bwd(d_ng * g_ref[...], n, r)

    row = lambda c: pl.BlockSpec((tt, c), lambda i: (i, 0))
    vec = _const((1, D_MODEL))
    return _call(
        body, "in_bwd", (s_len // tt,),
        in_specs=[row(IN_COLS), _whole(), row(D_MODEL), row(D_MODEL), vec, vec],
        out_specs=[row(D_MODEL), _const((SUBLANES, D_MODEL))],
        out_shape=[_sds((s_len, D_MODEL), F32), _sds((SUBLANES, D_MODEL), F32)],
        scratch=[], args=(d_z, w_in, x, d_x1, g, sc), carry=carry)


def _wgrad(a, b, name, by_rows=False, carry=None):
    s_len, k_dim = a.shape
    halves = b.ndim == 3
    n_dim = b.shape[-1] * (2 if halves else 1)

    def body(a_ref, b_ref, ob_ref, own_ref):
        out = _dot_tn(a_ref[...], b_ref[0] if halves else b_ref[...])
        ob_ref[...] = out.astype(BF16)

        @pl.when(pl.program_id(0) == _dev_index(_my_pos()))
        def _():
            own_ref[...] = out

    if by_rows:
        tile = k_dim // N_DEV
        a_spec = pl.BlockSpec((s_len, tile), lambda j: (0, j))
        b_spec = pl.BlockSpec((s_len, n_dim), lambda j: (0, 0))
        o_spec = pl.BlockSpec((tile, n_dim), lambda j: (j, 0))
        own_shape = (tile, n_dim)
    else:
        tile = n_dim // N_DEV
        a_spec = pl.BlockSpec((s_len, k_dim), lambda j: (0, 0))
        if halves:
            per_half = N_DEV // 2
            b_spec = pl.BlockSpec((1, s_len, tile), lambda j: (j // per_half, 0, j % per_half))
        else:
            b_spec = pl.BlockSpec((s_len, tile), lambda j: (0, j))
        o_spec = pl.BlockSpec((k_dim, tile), lambda j: (0, j))
        own_shape = (k_dim, tile)
    return _call(
        body, name, (N_DEV,), in_specs=[a_spec, b_spec], out_specs=[o_spec, _const(own_shape)],
        out_shape=[_sds((k_dim, n_dim), BF16), _sds(own_shape, F32)],
        scratch=[], args=(a, b), carry=carry)


def _adam_math(w, g, m, v):
    m = ADAM_B1 * m + (1.0 - ADAM_B1) * g
    v = ADAM_B2 * v + (1.0 - ADAM_B2) * (g * g)
    m_hat = m / (1.0 - ADAM_B1 ** ADAM_STEP)
    v_hat = v / (1.0 - ADAM_B2 ** ADAM_STEP)
    delta = -ADAM_LR * (m_hat / (jnp.sqrt(v_hat) + ADAM_EPS) + ADAM_WD * w)
    return delta, m, v


def _row_tile(rows, cols, n_f32_arrays):
    budget = VMEM_LIMIT // 2
    tr = rows
    while tr % 2 == 0 and tr // 2 >= SUBLANES and (tr // 2) % SUBLANES == 0 and tr * cols * 4 * n_f32_arrays * 2 > budget:
        tr //= 2
    return tr


def _adamw_sum(w, g_own, recv, m, v, name):
    _, rows, cols = w.shape
    n_recv = len(recv)
    tr = _row_tile(rows, cols, 10)
    nb = rows // tr

    def body(w_ref, g_ref, *rest):
        r_refs = rest[:n_recv]
        m_ref, v_ref, go_ref, d_ref, mo_ref, vo_ref = rest[n_recv:]
        g = g_ref[...]
        for r_ref in r_refs:
            for k in range(r_ref.shape[0]):
                g = g + r_ref[k].astype(F32)
        go_ref[0] = g
        d_ref[0], mo_ref[0], vo_ref[0] = _adam_math(w_ref[0], g, m_ref[0], v_ref[0])

    blk = pl.BlockSpec((1, tr, cols), lambda i: (0, i, 0))
    return pl.pallas_call(
        body, name=name, grid=(nb,),
        in_specs=[blk, pl.BlockSpec((tr, cols), lambda i: (i, 0))]
        + [pl.BlockSpec((r.shape[0], tr, cols), lambda i: (0, i, 0)) for r in recv] + [blk, blk],
        out_specs=[blk] * 4, out_shape=[_sds((1, rows, cols), F32)] * 4,
        compiler_params=_cparams(("arbitrary",)),
    )(w, g_own, *recv, m, v)


def _row_of_each(ref, row):
    cols = ref.shape[1]
    rows = _rows((N_DEV, cols))
    out = jnp.zeros((N_DEV, cols), F32)
    for d in range(N_DEV):
        picked = ref[d * SUBLANES + row:d * SUBLANES + row + 1, :]
        out = jnp.where(rows == d, jnp.broadcast_to(picked, (N_DEV, cols)), out)
    return out


def _my_columns(full, width, me):
    out = jnp.zeros(full.shape[:-1] + (width,), F32)
    for d in range(N_DEV):
        out = out + jnp.where(me == d, full[:, d * width:(d + 1) * width], 0.0)
    return out


def _adamw_wada(c_all, vs_in_all, vs_up_all, vs_ffn_all, w, m, v):
    _, rows, cols = w.shape

    def body(c_ref, vi_ref, vu_ref, vf_ref, w_ref, m_ref, v_ref, go_ref, d_ref, mo_ref, vo_ref):
        me = _dev_index(_my_pos())
        cv = _row_of_each(c_ref, 0)
        ca = cv * _sigmoid(cv)
        dmod = jnp.concatenate([_row_of_each(vi_ref, 0), _row_of_each(vi_ref, 1), _row_of_each(vu_ref, 3),
                                _row_of_each(vu_ref, 0), _row_of_each(vu_ref, 1), _row_of_each(vf_ref, 0)], axis=1)
        dm = _my_columns(dmod, cols, me)
        g = lax.dot_general(ca, dm, (((0,), (0,)), ((), ())), preferred_element_type=F32,
                            precision=lax.Precision.HIGHEST)
        go_ref[0] = g
        d_ref[0], mo_ref[0], vo_ref[0] = _adam_math(w_ref[0], g, m_ref[0], v_ref[0])

    return pl.pallas_call(
        body, name="adamw_w_ada", out_shape=[_sds((1, rows, cols), F32)] * 4,
        in_specs=[_whole()] * 7, out_specs=[_whole()] * 4,
        compiler_params=_cparams(),
    )(c_all, vs_in_all, vs_up_all, vs_ffn_all, w, m, v)


def _adamw_small(gathered, reduced, params, conv_params):
    names = list(params) + list(conv_params)
    allp = {**params, **conv_params}
    n_g = len(gathered) + len(reduced)

    def body(*refs):
        g_refs = refs[:n_g]
        p_refs = refs[n_g:n_g + 3 * len(names)]
        o_refs = refs[n_g + 3 * len(names):]
        me = _dev_index(_my_pos())

        def total(ref):
            s = ref[0:SUBLANES, :]
            for d in range(1, N_DEV):
                s = s + ref[d * SUBLANES:(d + 1) * SUBLANES, :]
            return s

        vs_in, vs_up, vs_ffn, loss = [total(r) for r in g_refs[:4]]
        cs, vs_mix, dcw, dwr, dwi, dws, dbs = [r[...] for r in g_refs[4:]]
        o_refs[-1][...] = loss[0:1, 0:1]
        mine = lambda full, width: _my_columns(full, width, me)

        all_ = (slice(None), slice(None))
        heads = lambda row: [((0, slice(h, h + 1), slice(None)), row[:, h * HEAD_DIM:(h + 1) * HEAD_DIM])
                             for h in range(N_HEADS)]
        blocks = lambda pairs: [((0, h), pairs[_head_pair_block(h)]) for h in range(N_HEADS)]
        pieces = {
            "b_ada": [((slice(None), slice(k * D_MODEL, (k + 1) * D_MODEL)), row) for k, row in enumerate(
                (vs_in[0:1], vs_in[1:2], vs_up[3:4], vs_up[0:1], vs_up[1:2], vs_ffn[0:1]))],
            "g_mix_pre": [(all_, vs_in[2:3])], "g_mix_post": [(all_, vs_up[4:5])],
            "g_ffn_pre": [(all_, vs_up[2:3])], "g_ffn_post": [(all_, vs_ffn[1:2])],
            "conv_b": [(all_, vs_mix[0:1])], "b_rgate": heads(vs_mix[1:2]), "b_igate": heads(vs_mix[2:3]),
            "lru_a": [(all_, vs_mix[3:4])], "v_norm_g": [(all_, vs_mix[4:5])], "v_norm_b": [(all_, vs_mix[5:6])],
            "g_lru_out": [(all_, vs_mix[6:7])], "g_gmlp_out": [(all_, vs_mix[7:8])],
            "w_rgate": blocks(dwr), "w_igate": blocks(dwi),
            "w_spatial": [((0, g), dws[g * POS_BLOCK:(g + 1) * POS_BLOCK, :]) for g in range(N_GROUPS)],
            "b_spatial": [((0,), dbs[0:N_GROUPS])],
            "ffn_conv_b": [(all_, cs[FFN_CONV_K:FFN_CONV_K + 1])],
            "conv_w": [((0,), mine(dcw[0:LRU_CONV_K], LRU_W // N_DEV))],
            "ffn_conv_w": [((0,), mine(cs[0:FFN_CONV_K], 2 * D_FF // N_DEV))],
        }
        for n_i, name in enumerate(names):
            w_ref, m_ref, v_ref = p_refs[3 * n_i:3 * n_i + 3]
            go_ref, d_ref, mo_ref, vo_ref = o_refs[4 * n_i:4 * n_i + 4]
            for idx, g in pieces[name]:
                go_ref[idx] = g
                d_ref[idx], mo_ref[idx], vo_ref[idx] = _adam_math(w_ref[idx], g, m_ref[idx], v_ref[idx])

    flat_params = [a for n in names for a in allp[n]]
    out_shape = [_sds(allp[n][0].shape, F32) for n in names for _ in range(4)] + [_sds((1, 1), F32)]
    outs = pl.pallas_call(
        body, name="adamw_small", out_shape=out_shape,
        in_specs=[_whole()] * (n_g + len(flat_params)), out_specs=[_whole()] * len(out_shape),
        compiler_params=_cparams(),
    )(*gathered, *reduced, *flat_params)
    return {n: outs[4 * i:4 * i + 4] for i, n in enumerate(names)}, outs[-1]


def _my_pos():
    return lax.axis_index("x"), lax.axis_index("y"), lax.axis_index("c")


def _flip(pos, k):
    x, y, c = pos
    return (1 - x if k & 4 else x, 1 - y if k & 2 else y, 1 - c if k & 1 else c)


def _dev_index(pos):
    x, y, c = pos
    return 4 * x + 2 * y + c


def _all_gather_small(ins, outs, send_sems, recv_sems):
    n = len(ins)
    me = _my_pos()

    def slot(a, pos):
        rows = ins[a].shape[0]
        return outs[a].at[pl.ds(pl.multiple_of(_dev_index(pos) * rows, SUBLANES), rows), :]

    def copy(a, k, block):
        return pltpu.make_async_remote_copy(
            src_ref=ins[a], dst_ref=slot(a, block), send_sem=send_sems.at[a, k - 1], recv_sem=recv_sems.at[a, k - 1],
            device_id=_flip(me, k), device_id_type=MESH)

    sends = [copy(a, k, me) for a in range(n) for k in range(1, N_DEV)]
    for cp in sends:
        cp.start()
    for a in range(n):
        rows = ins[a].shape[0]
        outs[a][pl.ds(pl.multiple_of(_dev_index(me) * rows, SUBLANES), rows), :] = ins[a][...]
    for a in range(n):
        for k in range(1, N_DEV):
            copy(a, k, _flip(me, k)).wait_recv()
    for cp in sends:
        cp.wait_send()


def _prologue(c8, cw8, fcw8, w_ada, b_ada, carry):
    cols = w_ada.shape[1]

    def body(c_ref, cw_ref, fcw_ref, w_ref, b_ref, call_ref, cwall_ref, fcwall_ref, modall_ref, mod_scr,
             s1, r1, s2, r2, start_carry):
        _all_gather_small([c_ref, cw_ref, fcw_ref], [call_ref, cwall_ref, fcwall_ref], s1, r1)
        start_carry()
        cv = _row_of_each(call_ref, 0)
        ca = cv * _sigmoid(cv)
        b_cols = _my_columns(b_ref[...], cols, _dev_index(_my_pos()))
        mod_scr[...] = jnp.dot(ca, w_ref[...], preferred_element_type=F32, precision=lax.Precision.HIGHEST) + b_cols
        _all_gather_small([mod_scr], [modall_ref], s2, r2)

    sem = lambda n: pltpu.SemaphoreType.DMA((n, N_DEV - 1))
    return _call(
        body, "prologue", (1,), in_specs=[_whole()] * 5, out_specs=[_whole()] * 4,
        out_shape=[_sds((N_DEV * SUBLANES, a.shape[1]), F32) for a in (c8, cw8, fcw8)]
        + [_sds((N_DEV * N_DEV, cols), F32)],
        scratch=[pltpu.VMEM((N_DEV, cols), F32), sem(3), sem(3), sem(1), sem(1)],
        args=(c8, cw8, fcw8, w_ada, b_ada), carry=carry, body_starts_carry=True)


def _reduce_small(gath, red, carry=None):
    n_g, n_r = len(gath), len(red)
    chip_flips = (4, 2, 6)

    def body(*refs, start_carry):
        g_in, r_in = refs[:n_g], refs[n_g:n_g + n_r]
        g_out, r_out = refs[n_g + n_r:2 * n_g + n_r], refs[2 * n_g + n_r:2 * (n_g + n_r)]
        scr = refs[2 * (n_g + n_r):]
        sib, land = scr[:n_r], scr[n_r:2 * n_r]
        g_send, g_recv, s_send, s_recv, i_send, i_recv, f_send, f_recv = scr[2 * n_r:]
        me = _my_pos()
        c = me[2]
        sibling = _flip(me, 1)

        def slot(a, pos):
            return g_out[a].at[pl.ds(pl.multiple_of(_dev_index(pos) * SUBLANES, SUBLANES), SUBLANES), :]

        def gcopy(a, k):
            return pltpu.make_async_remote_copy(
                src_ref=g_in[a], dst_ref=slot(a, me), send_sem=g_send.at[a, k - 1], recv_sem=g_recv.at[a, k - 1],
                device_id=_flip(me, k), device_id_type=MESH)

        def scopy(a):
            return pltpu.make_async_remote_copy(
                src_ref=r_in[a], dst_ref=sib[a], send_sem=s_send.at[a], recv_sem=s_recv.at[a],
                device_id=sibling, device_id_type=MESH)

        def icopy(a, j):
            return pltpu.make_async_remote_copy(
                src_ref=r_out[a], dst_ref=land[a].at[j], send_sem=i_send.at[a, j], recv_sem=i_recv.at[a, j],
                device_id=_flip(me, chip_flips[j]), device_id_type=MESH)

        def fcopy(a, j):
            return pltpu.make_async_remote_copy(
                src_ref=land[a].at[j], dst_ref=land[a].at[j], send_sem=f_send.at[a, j], recv_sem=f_recv.at[a, j],
                device_id=sibling, device_id_type=MESH)

        gathers = [gcopy(a, k) for a in range(n_g) for k in range(1, N_DEV)]
        swaps = [scopy(a) for a in range(n_r)]
        for cp in gathers + swaps:
            cp.start()
        for a in range(n_g):
            g_out[a][pl.ds(pl.multiple_of(_dev_index(me) * SUBLANES, SUBLANES), SUBLANES), :] = g_in[a][...]
        for a in range(n_r):
            swaps[a].wait_recv()
            r_out[a][...] = r_in[a][...] + sib[a][...]

        for core in range(2):
            @pl.when(c == core)
            def _():
                for a in range(core, n_r, 2):
                    for j in range(3):
                        icopy(a, j).start()

        start_carry()

        for core in range(2):
            mine = [a for a in range(n_r) if a % 2 == core]
            theirs = [a for a in range(n_r) if a % 2 != core]

            @pl.when(c == core)
            def _():
                out = [icopy(a, j) for a in mine for j in range(3)]
                fwd = []
                for a in mine:
                    for j in range(3):
                        icopy(a, j).wait_recv()
                        cp = fcopy(a, j)
                        cp.start()
                        fwd.append(cp)
                for a in theirs:
                    for j in range(3):
                        fcopy(a, j).wait_recv()
                for cp in out + fwd:
                    cp.wait_send()

        for a in range(n_r):
            r_out[a][...] = (r_out[a][...] + land[a][1]) + (land[a][0] + land[a][2])
        for a in range(n_g):
            for k in range(1, N_DEV):
                pltpu.make_async_remote_copy(
                    src_ref=g_in[a], dst_ref=slot(a, _flip(me, k)), send_sem=g_send.at[a, k - 1],
                    recv_sem=g_recv.at[a, k - 1], device_id=_flip(me, k), device_id_type=MESH).wait_recv()
        for cp in gathers + swaps:
            cp.wait_send()

    shapes = [tuple(a.shape) for a in red]
    outs, carried = _call(
        body, "reduce_small", (1,), in_specs=[_whole()] * (n_g + n_r), out_specs=[_whole()] * (n_g + n_r),
        out_shape=[_sds((N_DEV * SUBLANES, a.shape[1]), F32) for a in gath] + [_sds(s, F32) for s in shapes],
        scratch=[pltpu.VMEM(s, F32) for s in shapes] + [pltpu.VMEM((3,) + s, F32) for s in shapes]
        + [pltpu.SemaphoreType.DMA((n_g, N_DEV - 1)), pltpu.SemaphoreType.DMA((n_g, N_DEV - 1)),
           pltpu.SemaphoreType.DMA((n_r,)), pltpu.SemaphoreType.DMA((n_r,)),
           pltpu.SemaphoreType.DMA((n_r, 3)), pltpu.SemaphoreType.DMA((n_r, 3)),
           pltpu.SemaphoreType.DMA((n_r, 3)), pltpu.SemaphoreType.DMA((n_r, 3))],
        args=tuple(gath) + tuple(red), carry=carry, body_starts_carry=True)
    return (outs[:n_g], outs[n_g:]), carried


STACKED = "stacked"


def _region(ref, shard_shape, col_sharded, pos):
    r, cdim = shard_shape
    d = _dev_index(pos)
    if col_sharded == STACKED:
        return ref.at[d]
    if col_sharded:
        return ref.at[:, pl.ds(pl.multiple_of(d * cdim, LANES), cdim)]
    return ref.at[pl.ds(pl.multiple_of(d * r, 2 * SUBLANES), r), :]


def _gather_carry(shards, col_sharded):
    n_w = len(shards)
    shapes = [tuple(s.shape) for s in shards]
    full_shapes = [(N_DEV,) + s if cs == STACKED else (s[0], s[1] * N_DEV) if cs else (s[0] * N_DEV, s[1])
                   for s, cs in zip(shapes, col_sharded)]

    def tools(out_refs, scr):
        send_sems, recv_sems = scr[n_w], scr[n_w + 1]
        me = _my_pos()
        x, y, c = me
        sibling = (x, y, 1 - c)
        chips = [(1 - x, y), (x, 1 - y), (1 - x, 1 - y)]

        def region(w, pos):
            return _region(out_refs[w], shapes[w], col_sharded[w], pos)

        def copy(w, k, block, to, src=None):
            return pltpu.make_async_remote_copy(
                src_ref=region(w, block) if src is None else src, dst_ref=region(w, block),
                send_sem=send_sems.at[w, k], recv_sem=recv_sems.at[w, k], device_id=to, device_id_type=MESH)

        def first(w):
            return [copy(w, 0, me, sibling, src=scr[w])] + [
                copy(w, 1 + j, me, (*chip, c), src=scr[w]) for j, chip in enumerate(chips)]

        def mine(w):
            return pltpu.make_async_copy(scr[w], region(w, me), scr[n_w + 2].at[w])

        return me, c, sibling, chips, copy, first, mine

    def start(ins, outs, scr):
        _, _, _, _, _, first, mine = tools(outs, scr)
        for w in range(n_w):
            scr[w][...] = ins[w][...].astype(BF16)
            for cp in first(w) + [mine(w)]:
                cp.start()

    def finish(ins, outs, scr):
        me, c, sibling, chips, copy, first, mine = tools(outs, scr)
        passed = []
        for w in range(n_w):
            for j, chip in enumerate(chips):
                copy(w, 1 + j, (*chip, c), me).wait_recv()
                fwd = copy(w, 4 + j, (*chip, c), sibling)
                fwd.start()
                passed.append(fwd)
        for w in range(n_w):
            copy(w, 0, sibling, me).wait_recv()
            for j, chip in enumerate(chips):
                copy(w, 4 + j, (*chip, 1 - c), me).wait_recv()
        for w in range(n_w):
            for cp in first(w):
                cp.wait_send()
            mine(w).wait()
        for cp in passed:
            cp.wait_send()

    return _Carry(
        inputs=list(shards), in_specs=[_whole()] * n_w,
        out_shape=[_sds(s, BF16) for s in full_shapes], out_specs=[_any()] * n_w,
        scratch=[pltpu.VMEM(s, BF16) for s in shapes]
        + [pltpu.SemaphoreType.DMA((n_w, N_DEV - 1)), pltpu.SemaphoreType.DMA((n_w, N_DEV - 1)),
           pltpu.SemaphoreType.DMA((n_w,))],
        start=start, finish=finish)


def _scatter_carry(grads_bf, shard_shapes, col_sharded, relations):
    n_w = len(grads_bf)
    shapes = [tuple(s) for s in shard_shapes]

    def copies(ins, outs, scr):
        send_sems, recv_sems = scr
        me = _my_pos()
        out = []
        for w in range(n_w):
            for i, k in enumerate(relations[w]):
                peer = _flip(me, k)
                out.append(pltpu.make_async_remote_copy(
                    src_ref=_region(ins[w], shapes[w], col_sharded[w], peer), dst_ref=outs[w].at[i],
                    send_sem=send_sems.at[w, i], recv_sem=recv_sems.at[w, i],
                    device_id=peer, device_id_type=MESH))
        return out

    def start(ins, outs, scr):
        for cp in copies(ins, outs, scr):
            cp.start()

    def finish(ins, outs, scr):
        cps = copies(ins, outs, scr)
        for cp in cps:
            cp.wait_recv()
        for cp in cps:
            cp.wait_send()

    return _Carry(
        inputs=list(grads_bf), in_specs=[_any()] * n_w,
        out_shape=[_sds((len(r),) + s, BF16) for r, s in zip(relations, shapes)], out_specs=[_any()] * n_w,
        scratch=[pltpu.SemaphoreType.DMA((n_w, N_DEV - 1)), pltpu.SemaphoreType.DMA((n_w, N_DEV - 1))],
        start=start, finish=finish)


def _block_diag(w):
    eye = jnp.eye(N_HEADS, dtype=w.dtype)
    return (eye[:, None, :, None] * w[:, :, None, :]).reshape(N_HEADS * HEAD_DIM, N_HEADS * HEAD_DIM)


def _pad_rows(a):
    return jnp.pad(a, ((0, SUBLANES - a.shape[0]), (0, 0)))


def _columns_from_devices(gathered, rows):
    w = gathered.shape[1]
    return gathered.reshape(N_DEV, SUBLANES, w)[:, :rows].transpose(1, 0, 2).reshape(rows, N_DEV * w)


def _local_step(x2, target, mod, w_in_f, w_full, conv_w_full, ffn_cw_full,
                g_mix_pre, g_mix_post, conv_b, w_rgate, b_rgate, w_igate, b_igate, lru_a, v_norm_g, v_norm_b,
                w_spatial, b_spatial, g_lru_out, g_gmlp_out, g_ffn_pre, g_ffn_post, ffn_conv_b,
                gather=None, scatter=None):
    sh_m, sc_m, gt_m, sh_f, sc_f, gt_f = [mod[k] for k in range(N_MOD)]
    wr_bd = _block_diag(w_rgate[0]).astype(BF16)
    wi_bd = _block_diag(w_igate[0]).astype(BF16)
    b_r = b_rgate.reshape(1, LRU_W)
    b_i = b_igate.reshape(1, LRU_W)
    b_sp_t = b_spatial[0].T
    w_sp_t = jnp.swapaxes(w_spatial[0], 1, 2)

    def arriving(*names):
        return gather(*names) if gather else None

    near, far = (1, 2, 3, 4, 5), (6, 7)

    def leaving(*parts):
        return scatter(parts) if scatter else None

    def received(recv, parts, outs):
        for (name, _, _), out in zip(parts, outs):
            recv.setdefault(name, []).append(out)

    mix_params = (conv_w_full, conv_b, wr_bd, wi_bd, b_r, b_i, lru_a, v_norm_g, v_norm_b)
    w_out_f = w_full["w_out"]
    (z, h, ycat, hl, y, x1, h2), got = _mix_fwd(
        x2, sh_m, sc_m, g_mix_pre, w_in_f, *mix_params, w_spatial[0], b_sp_t, g_lru_out, g_gmlp_out,
        w_out_f, g_mix_post, gt_m, g_ffn_pre, sc_f, sh_f, carry=arriving("w_up"))
    w_up_f = got[0] if gather else w_full["w_up"]
    (up_pre, up, act), got = _ffn_fwd(h2, w_up_f, ffn_cw_full, ffn_conv_b, carry=arriving("w_down"))
    w_down_f = got[0] if gather else w_full["w_down"]
    d_y2, dout, loss_acc, vs_ffn = _ffn_tail(act, w_down_f, x1, gt_f, g_ffn_post, target)

    recv = {}
    gw_down, _ = _wgrad(act, d_y2, "wgrad_down", by_rows=True)
    parts = [("w_down", gw_down[0], near + far)]
    (d_up, cs_ffn), got = _ffn_bwd(d_y2, up_pre, up, ffn_cw_full, w_down_f, carry=leaving(*parts))
    received(recv, parts, got)
    gw_up, _ = _wgrad(h2, d_up, "wgrad_up")
    parts = [("w_up", gw_up[0], near)]
    (d_x1, d_y, d_ycat, vs_up), got = _up_bwd(
        d_up, w_up_f, x1, dout, y, w_out_f, g_ffn_pre, sc_f, g_mix_post, gt_m, carry=leaving(*parts))
    received(recv, parts, got)
    gw_out, _ = _wgrad(ycat, d_y, "wgrad_out", by_rows=True)
    parts = [("w_up", gw_up[0], far), ("w_out", gw_out[0], near + far)]
    (d_z, vs_mix, dcw, d_wr, d_wi, d_ws, d_bs), got = _mix_bwd(
        d_ycat, z, hl, *mix_params, w_spatial[0], w_sp_t, b_sp_t, g_lru_out, g_gmlp_out, carry=leaving(*parts))
    received(recv, parts, got)
    gw_in, _ = _wgrad(h, d_z, "wgrad_in")
    (grad_x, vs_in), _ = _in_bwd(d_z, w_in_f, x2, d_x1, g_mix_pre, sc_m)
    pending = [("w_in", gw_in[0], near + far)]
    recv["w_in"] = []

    gath = [vs_in, vs_up, vs_ffn, loss_acc]
    red = [cs_ffn, vs_mix, dcw, d_wr, d_wi, d_ws.reshape(N_GROUPS * POS_BLOCK, POS_BLOCK), d_bs]
    return dict(grad_x=grad_x, gath=gath, red=red, recv=recv, pending=pending,
                w_in=gw_in, w_out=gw_out, w_up=gw_up, w_down=gw_down)


def kernel(x, c, w_ada, b_ada, g_mix_pre, g_mix_post, w_in, conv_w, conv_b, w_rgate, b_rgate, w_igate, b_igate, lru_a, v_norm_g, v_norm_b, w_spatial, b_spatial, g_lru_out, g_gmlp_out, w_out, g_ffn_pre, g_ffn_post, w_up, ffn_conv_w, ffn_conv_b, w_down, loss_target, m_w_ada, m_b_ada, m_g_mix_pre, m_g_mix_post, m_w_in, m_conv_w, m_conv_b, m_w_rgate, m_b_rgate, m_w_igate, m_b_igate, m_lru_a, m_v_norm_g, m_v_norm_b, m_w_spatial, m_b_spatial, m_g_lru_out, m_g_gmlp_out, m_w_out, m_g_ffn_pre, m_g_ffn_post, m_w_up, m_ffn_conv_w, m_ffn_conv_b, m_w_down, v_w_ada, v_b_ada, v_g_mix_pre, v_g_mix_post, v_w_in, v_conv_w, v_conv_b, v_w_rgate, v_b_rgate, v_w_igate, v_b_igate, v_lru_a, v_v_norm_g, v_v_norm_b, v_w_spatial, v_b_spatial, v_g_lru_out, v_g_gmlp_out, v_w_out, v_g_ffn_pre, v_g_ffn_post, v_w_up, v_ffn_conv_w, v_ffn_conv_b, v_w_down):
    me = _dev_index(_my_pos())
    ada_cols = w_ada.shape[-1]

    big_w = dict(w_in=(w_in, m_w_in, v_w_in, True), w_out=(w_out, m_w_out, v_w_out, False),
                 w_up=(w_up, m_w_up, v_w_up, True), w_down=(w_down, m_w_down, v_w_down, False))

    def gather(*names):
        return _gather_carry([big_w[n][0][0] for n in names], [STACKED if n == "w_up" else big_w[n][3] for n in names])

    def scatter(parts):
        return _scatter_carry([g for _, g, _ in parts], [big_w[n][0].shape[1:] for n, _, _ in parts],
                              [big_w[n][3] for n, _, _ in parts], [rel for _, _, rel in parts])

    (c_all, cw_all, fcw_all, mod_all), (w_in_f, w_out_f) = _prologue(
        jnp.broadcast_to(c, (SUBLANES, D_MODEL)), _pad_rows(conv_w[0]), _pad_rows(ffn_conv_w[0]), w_ada[0], b_ada,
        carry=gather("w_in", "w_out"))
    conv_w_full = _columns_from_devices(cw_all, LRU_CONV_K)
    ffn_cw_full = _columns_from_devices(fcw_all, FFN_CONV_K)
    mod = lax.dynamic_index_in_dim(mod_all.reshape(N_DEV, N_DEV, ada_cols), me, axis=1, keepdims=False)
    mod = mod.reshape(N_MOD, 1, D_MODEL)

    loc = _local_step(x[0], loss_target[0], mod, w_in_f, dict(w_out=w_out_f), conv_w_full, ffn_cw_full,
                      g_mix_pre, g_mix_post, conv_b, w_rgate, b_rgate, w_igate, b_igate, lru_a, v_norm_g, v_norm_b,
                      w_spatial, b_spatial, g_lru_out, g_gmlp_out, g_ffn_pre, g_ffn_post, ffn_conv_b,
                      gather=gather, scatter=scatter)
    grad_x = loc["grad_x"]

    (gathered, reduced), got = _reduce_small(loc["gath"], loc["red"], carry=scatter(loc["pending"]))
    for (name, _, _), out in zip(loc["pending"], got):
        loc["recv"][name].append(out)

    results = {}
    for name, (w_, m_, v_, cs) in big_w.items():
        results[name] = _adamw_sum(w_, loc[name][1], loc["recv"][name], m_, v_, "adamw_" + name)

    params = dict(
        b_ada=(b_ada, m_b_ada, v_b_ada), g_mix_pre=(g_mix_pre, m_g_mix_pre, v_g_mix_pre),
        g_mix_post=(g_mix_post, m_g_mix_post, v_g_mix_post), conv_b=(conv_b, m_conv_b, v_conv_b),
        w_rgate=(w_rgate, m_w_rgate, v_w_rgate), b_rgate=(b_rgate, m_b_rgate, v_b_rgate),
        w_igate=(w_igate, m_w_igate, v_w_igate), b_igate=(b_igate, m_b_igate, v_b_igate),
        lru_a=(lru_a, m_lru_a, v_lru_a), v_norm_g=(v_norm_g, m_v_norm_g, v_v_norm_g),
        v_norm_b=(v_norm_b, m_v_norm_b, v_v_norm_b), w_spatial=(w_spatial, m_w_spatial, v_w_spatial),
        b_spatial=(b_spatial, m_b_spatial, v_b_spatial), g_lru_out=(g_lru_out, m_g_lru_out, v_g_lru_out),
        g_gmlp_out=(g_gmlp_out, m_g_gmlp_out, v_g_gmlp_out), g_ffn_pre=(g_ffn_pre, m_g_ffn_pre, v_g_ffn_pre),
        g_ffn_post=(g_ffn_post, m_g_ffn_post, v_g_ffn_post), ffn_conv_b=(ffn_conv_b, m_ffn_conv_b, v_ffn_conv_b))
    conv_params = dict(conv_w=(conv_w, m_conv_w, v_conv_w), ffn_conv_w=(ffn_conv_w, m_ffn_conv_w, v_ffn_conv_w))
    small_results, loss = _adamw_small(gathered, reduced, params, conv_params)
    results.update(small_results)
    loss = loss.reshape(())

    results["w_ada"] = _adamw_wada(c_all, gathered[0], gathered[1], gathered[2], w_ada, m_w_ada, v_w_ada)

    order = ["w_ada", "b_ada", "g_mix_pre", "g_mix_post", "w_in", "conv_w", "conv_b", "w_rgate", "b_rgate", "w_igate",
             "b_igate", "lru_a", "v_norm_g", "v_norm_b", "w_spatial", "b_spatial", "g_lru_out", "g_gmlp_out", "w_out",
             "g_ffn_pre", "g_ffn_post", "w_up", "ffn_conv_w", "ffn_conv_b", "w_down"]
    outs = [loss, grad_x[None]]
    for kind in range(4):
        outs += [results[n][kind] for n in order]
    return tuple(outs)
```

```python
import functools

import jax
import jax.numpy as jnp
from jax import lax
from jax.experimental import pallas as pl
from jax.experimental.pallas import tpu as pltpu

F32 = jnp.float32
BF16 = jnp.bfloat16

D_MODEL = 1024
LRU_W = 512
GMLP_W = 512
N_HEADS = 8
HEAD_DIM = 64
N_GROUPS = 4
POS_BLOCK = 128
CHUNK = 64
IN_COLS = 2048
D_FF = 3072
N_MOD = 6
N_DEV = 8
EPS = 1e-6
LRU_C = 8.0
LRU_CONV_K = 4
FFN_CONV_K = 3

ADAM_LR = 0.001
ADAM_B1 = 0.9
ADAM_B2 = 0.999
ADAM_EPS = 1e-08
ADAM_WD = 0.01
ADAM_STEP = 10

LANES = 128
SUBLANES = 8
TT_BIG = 512
TT_MIX = 256
FF_CW = 512
VMEM_LIMIT = 56 * 1024 * 1024

MESH = pl.DeviceIdType.MESH


def _sds(shape, dtype):
    return jax.ShapeDtypeStruct(shape, dtype)


def _cparams(sem=None):
    return pltpu.CompilerParams(dimension_semantics=sem, vmem_limit_bytes=VMEM_LIMIT)


def _whole():
    return pl.BlockSpec(memory_space=pltpu.VMEM)


def _const(shape):
    nd = len(shape)
    return pl.BlockSpec(shape, lambda *_: (0,) * nd)


def _any():
    return pl.BlockSpec(memory_space=pl.ANY)


class _Carry:
    def __init__(self, inputs, in_specs, out_shape, out_specs, scratch, start, finish):
        self.inputs, self.in_specs, self.out_shape, self.out_specs = inputs, in_specs, out_shape, out_specs
        self.scratch, self.start, self.finish = scratch, start, finish


def _call(body, name, grid, in_specs, out_specs, out_shape, scratch, args, carry=None, body_starts_carry=False):
    n_in, n_out, n_scr = len(in_specs), len(out_specs), len(scratch)
    c_in = len(carry.in_specs) if carry else 0
    c_out = len(carry.out_specs) if carry else 0

    def full_body(*refs):
        ins = refs[:n_in]
        c_ins = refs[n_in:n_in + c_in]
        outs = refs[n_in + c_in:n_in + c_in + n_out]
        c_outs = refs[n_in + c_in + n_out:n_in + c_in + n_out + c_out]
        scr = refs[n_in + c_in + n_out + c_out:n_in + c_in + n_out + c_out + n_scr]
        c_scr = refs[n_in + c_in + n_out + c_out + n_scr:]
        if carry:
            first = functools.reduce(lambda a, b: a & b, [pl.program_id(d) == 0 for d in range(len(grid))])
            last = functools.reduce(lambda a, b: a & b, [pl.program_id(d) == g - 1 for d, g in enumerate(grid)])

        if carry and not body_starts_carry:
            @pl.when(first)
            def _():
                carry.start(c_ins, c_outs, c_scr)

        if body_starts_carry:
            body(*ins, *outs, *scr, start_carry=(lambda: carry.start(c_ins, c_outs, c_scr)) if carry else (lambda: None))
        else:
            body(*ins, *outs, *scr)
        if carry:
            @pl.when(last)
            def _():
                carry.finish(c_ins, c_outs, c_scr)

    res = pl.pallas_call(
        full_body, name=name, grid=grid,
        in_specs=list(in_specs) + (list(carry.in_specs) if carry else []),
        out_specs=list(out_specs) + (list(carry.out_specs) if carry else []),
        out_shape=list(out_shape) + (list(carry.out_shape) if carry else []),
        scratch_shapes=list(scratch) + (list(carry.scratch) if carry else []),
        compiler_params=_cparams(("arbitrary",) * len(grid)),
    )(*args, *(carry.inputs if carry else []))
    return res[:n_out], res[n_out:]


GELU_C0 = 0.7978845608028654
GELU_C1 = GELU_C0 * 0.044715


def _gelu(x):
    t = jnp.tanh(x * (GELU_C0 + GELU_C1 * (x * x)))
    hx = 0.5 * x
    return hx + hx * t


def _gelu_and_grad(x):
    x2 = x * x
    t = jnp.tanh(x * (GELU_C0 + GELU_C1 * x2))
    hx = 0.5 * x
    g = hx + hx * t
    dg = (0.5 + 0.5 * t) + hx * (1.0 - t * t) * (GELU_C0 + 3.0 * GELU_C1 * x2)
    return g, dg


def _sigmoid(x):
    return 1.0 / (1.0 + jnp.exp(-x))


def _softplus(x):
    return jnp.maximum(x, 0.0) + jnp.log1p(jnp.exp(-jnp.abs(x)))


def _neg_expm1(x):
    series = -x * (1.0 + x * (0.5 + x * (1.0 / 6.0 + x * (1.0 / 24.0 + x * (1.0 / 120.0)))))
    return jnp.where(x > -0.1, series, 1.0 - jnp.exp(x))


def _dot(a, b):
    return jnp.dot(a.astype(BF16), b.astype(BF16), preferred_element_type=F32)


def _dot_nt(a, b):
    return lax.dot_general(a.astype(BF16), b.astype(BF16), (((1,), (1,)), ((), ())), preferred_element_type=F32)


def _dot_tn(a, b):
    return lax.dot_general(a.astype(BF16), b.astype(BF16), (((0,), (0,)), ((), ())), preferred_element_type=F32)


def _rows(shape):
    return lax.broadcasted_iota(jnp.int32, shape, 0)


def _shift_down(cur, prev8, s):
    if s == 0:
        return cur
    n = cur.shape[0]
    r = pltpu.roll(cur, s, 0)
    p = pltpu.roll(prev8, s, 0)
    top = jnp.where(_rows(p.shape) < s, p, r[0:SUBLANES])
    if n == SUBLANES:
        return top
    return jnp.concatenate([top, r[SUBLANES:]], axis=0)


def _shift_up(cur, next8, s):
    if s == 0:
        return cur
    n = cur.shape[0]
    r = pltpu.roll(cur, n - s, 0)
    q = pltpu.roll(next8, SUBLANES - s, 0)
    bot = jnp.where(_rows(q.shape) >= SUBLANES - s, q, r[n - SUBLANES:])
    if n == SUBLANES:
        return bot
    return jnp.concatenate([r[:n - SUBLANES], bot], axis=0)


def _scan_fwd(a, b, h_in):
    n = a.shape[0]
    in_group = _rows(a.shape) & (SUBLANES - 1)
    s = 1
    while s < SUBLANES:
        a_s = pltpu.roll(a, s, 0)
        b_s = pltpu.roll(b, s, 0)
        m = in_group >= s
        b = jnp.where(m, a * b_s + b, b)
        a = jnp.where(m, a * a_s, a)
        s *= 2
    out, carry = [], h_in
    for g in range(n // SUBLANES):
        rows = slice(g * SUBLANES, (g + 1) * SUBLANES)
        h_g = a[rows] * carry + b[rows]
        out.append(h_g)
        carry = h_g[SUBLANES - 1:SUBLANES, :]
    return jnp.concatenate(out, axis=0)


def _scan_rev(a, b, l_in):
    n = a.shape[0]
    in_group = _rows(a.shape) & (SUBLANES - 1)
    s = 1
    while s < SUBLANES:
        a_s = pltpu.roll(a, n - s, 0)
        b_s = pltpu.roll(b, n - s, 0)
        m = in_group < SUBLANES - s
        b = jnp.where(m, b + a * b_s, b)
        a = jnp.where(m, a * a_s, a)
        s *= 2
    out, carry = [], l_in
    for g in reversed(range(n // SUBLANES)):
        rows = slice(g * SUBLANES, (g + 1) * SUBLANES)
        l_g = b[rows] + a[rows] * carry
        out.append(l_g)
        carry = l_g[0:1, :]
    return jnp.concatenate(out[::-1], axis=0)


def _rms(x):
    r = lax.rsqrt(jnp.mean(x * x, axis=-1, keepdims=True) + EPS)
    return x * r, r


def _rms_bwd(d_n, n, r):
    return r * (d_n - n * jnp.mean(d_n * n, axis=-1, keepdims=True))


def _colsum(x):
    return jnp.sum(x, axis=0, keepdims=True)


ROW_PIECE = 256


def _row_pieces(tt):
    return [slice(r, r + min(ROW_PIECE, tt)) for r in range(0, tt, min(ROW_PIECE, tt))]


def _lru_gates(xc, wr_ref, wi_ref, br, bi, sp_a):
    r = _sigmoid(_dot(xc, wr_ref[...]) + br)
    i = _sigmoid(_dot(xc, wi_ref[...]) + bi)
    la = -LRU_C * r * sp_a
    a = jnp.exp(la)
    mult = jnp.sqrt(_neg_expm1(2.0 * la))
    return r, i, a, mult


def _lru_conv(lx, prev8, cw_ref, cb):
    xc = cb + cw_ref[LRU_CONV_K - 1:LRU_CONV_K, :] * lx
    taps = []
    for k in range(LRU_CONV_K - 1):
        tap = _shift_down(lx, prev8, LRU_CONV_K - 1 - k)
        taps.append(tap)
        xc = xc + cw_ref[k:k + 1, :] * tap
    return xc, taps


def _ws_mask(transposed=False):
    i = lax.broadcasted_iota(jnp.int32, (POS_BLOCK, POS_BLOCK), 0)
    j = lax.broadcasted_iota(jnp.int32, (POS_BLOCK, POS_BLOCK), 1)
    if transposed:
        i, j = j, i
    return (j // CHUNK) <= (i // CHUNK)


def _gmlp_v(gv, vg, vb):
    av, dav = _gelu_and_grad(gv)
    mu = jnp.mean(av, axis=-1, keepdims=True)
    cen = av - mu
    rs = lax.rsqrt(jnp.mean(cen * cen, axis=-1, keepdims=True) + EPS)
    vhat = cen * rs
    return vhat * vg + vb, vhat, rs, dav


def _mix_fwd(x, sh, sc, g_pre, w_in, conv_w, conv_b, wr_bd, wi_bd, b_r, b_i, lru_a, vn_g, vn_b, w_sp, b_sp_t,
             g_lru, g_gmlp, w_out, g_post, gt_m, g_ffn_pre, sc_f, sh_f, carry=None):
    s_len = x.shape[0]
    tt = min(TT_MIX, s_len)
    nblk = tt // POS_BLOCK

    def body(x_ref, sh_ref, sc_ref, g_ref, w_ref, cw_ref, cb_ref, wr_ref, wi_ref, br_ref, bi_ref, la_ref, vg_ref,
             vb_ref, ws_ref, bst_ref, gl_ref, gg_ref, wo_ref, gp_ref, gtm_ref, g2_ref, scf_ref, shf_ref,
             z_ref, h_ref, y_ref, hl_ref, yo_ref, x1_ref, h2_ref, prev8, hcar):
        i = pl.program_id(0)

        @pl.when(i == 0)
        def _():
            prev8[...] = jnp.zeros_like(prev8)
            hcar[...] = jnp.zeros_like(hcar)

        n_x, _ = _rms(x_ref[...])
        h = (n_x * g_ref[...] * (1.0 + sc_ref[...]) + sh_ref[...]).astype(BF16)
        h_ref[...] = h
        z_ref[...] = jnp.dot(h, w_ref[...], preferred_element_type=F32)

        lx = z_ref[:, 0:LRU_W]
        gate = z_ref[:, LRU_W:2 * LRU_W]
        gu = z_ref[:, 2 * LRU_W:2 * LRU_W + GMLP_W]
        gv = z_ref[:, 2 * LRU_W + GMLP_W:]

        xc, _ = _lru_conv(lx, prev8[...], cw_ref, cb_ref[...])
        prev8[...] = lx[tt - SUBLANES:]
        sp_a = _softplus(-la_ref[...])
        _, ig, a, mult = _lru_gates(xc, wr_ref, wi_ref, br_ref[...], bi_ref[...], sp_a)
        bx = mult * (ig * xc)
        hl = _scan_fwd(a, bx, hcar[0:1, :])
        hcar[...] = jnp.broadcast_to(hl[tt - 1:tt, :], hcar.shape)
        hl_ref[...] = hl
        y_lru = hl * _gelu(gate)
        n_l, _ = _rms(y_lru)
        y_ref[:, 0:LRU_W] = (n_l * gl_ref[...]).astype(BF16)

        u = _gelu(gu)
        v, _, _, _ = _gmlp_v(gv, vg_ref[...], vb_ref[...])
        mask = _ws_mask()
        sp_parts = []
        for nb in range(nblk):
            row = []
            for g in range(N_GROUPS):
                wsm = jnp.where(mask, ws_ref[g], 0.0)
                vblk = v[nb * POS_BLOCK:(nb + 1) * POS_BLOCK, g * LANES:(g + 1) * LANES]
                row.append(_dot(wsm, vblk) + bst_ref[:, g:g + 1])
            sp_parts.append(jnp.concatenate(row, axis=1))
        sp = jnp.concatenate(sp_parts, axis=0) if nblk > 1 else sp_parts[0]
        n_g, _ = _rms(u * sp)
        y_ref[:, LRU_W:] = (n_g * gg_ref[...]).astype(BF16)

        y = jnp.dot(y_ref[...], wo_ref[...], preferred_element_type=F32)
        yo_ref[...] = y
        n_y, _ = _rms(y)
        x1 = x_ref[...] + gtm_ref[...] * (n_y * gp_ref[...])
        x1_ref[...] = x1
        n1, _ = _rms(x1)
        h2_ref[...] = (n1 * g2_ref[...] * (1.0 + scf_ref[...]) + shf_ref[...]).astype(BF16)

    row = lambda c: pl.BlockSpec((tt, c), lambda i: (i, 0))
    v512 = _const((1, LRU_W))
    vec = _const((1, D_MODEL))
    return _call(
        body, "mix_fwd", (s_len // tt,),
        in_specs=[row(D_MODEL), vec, vec, vec, _whole(),
                  _const((LRU_CONV_K, LRU_W)), v512, _whole(), _whole(), v512, v512, v512, v512, v512,
                  _whole(), _whole(), v512, v512, _whole(), vec, vec, vec, vec, vec],
        out_specs=[row(IN_COLS), row(D_MODEL), row(LRU_W + GMLP_W), row(LRU_W), row(D_MODEL), row(D_MODEL),
                   row(D_MODEL)],
        out_shape=[_sds((s_len, IN_COLS), F32), _sds((s_len, D_MODEL), BF16),
                   _sds((s_len, LRU_W + GMLP_W), BF16), _sds((s_len, LRU_W), F32),
                   _sds((s_len, D_MODEL), F32), _sds((s_len, D_MODEL), F32), _sds((s_len, D_MODEL), BF16)],
        scratch=[pltpu.VMEM((SUBLANES, LRU_W), F32), pltpu.VMEM((SUBLANES, LRU_W), F32)],
        args=(x, sh, sc, g_pre, w_in, conv_w, conv_b, wr_bd, wi_bd, b_r, b_i, lru_a, vn_g, vn_b, w_sp, b_sp_t,
              g_lru, g_gmlp, w_out, g_post, gt_m, g_ffn_pre, sc_f, sh_f), carry=carry)


FF_CHUNKS = N_DEV // 2
FF_CHUNK_W = D_FF // FF_CHUNKS


def _ffn_fwd(h2, w_up3, ffn_cw, ffn_cb, carry=None):
    s_len = h2.shape[0]
    tt = min(TT_MIX, s_len)
    nc, cw = FF_CHUNKS, FF_CHUNK_W

    def body(h2_ref, wu_ref, cwg_ref, cwv_ref, cbg_ref, cbv_ref, up_ref, upc_ref, act_ref, prev):
        i = pl.program_id(0)
        c = pl.program_id(1)

        @pl.when(i == 0)
        def _():
            prev[c] = jnp.zeros((2, SUBLANES, cw), F32)

        h2 = h2_ref[...]
        ug_pre = jnp.dot(h2, wu_ref[c], preferred_element_type=F32)
        uv_pre = jnp.dot(h2, wu_ref[nc + c], preferred_element_type=F32)
        up_ref[0] = ug_pre.astype(BF16)
        up_ref[1] = uv_pre.astype(BF16)
        ug, _ = _ffn_conv(ug_pre, prev[c, 0], cwg_ref, cbg_ref[...])
        uv, _ = _ffn_conv(uv_pre, prev[c, 1], cwv_ref, cbv_ref[...])
        prev[c, 0] = ug_pre[tt - SUBLANES:, :]
        prev[c, 1] = uv_pre[tt - SUBLANES:, :]
        upc_ref[0] = ug
        upc_ref[1] = uv
        act_ref[...] = (_gelu(ug) * uv).astype(BF16)

    chunk2 = pl.BlockSpec((2, tt, cw), lambda i, c: (0, i, c))
    ffn_cb2 = ffn_cb.reshape(1, 2 * D_FF)
    return _call(
        body, "ffn_fwd", (s_len // tt, nc),
        in_specs=[pl.BlockSpec((tt, D_MODEL), lambda i, c: (i, 0)), _whole(),
                  pl.BlockSpec((FFN_CONV_K, cw), lambda i, c: (0, c)),
                  pl.BlockSpec((FFN_CONV_K, cw), lambda i, c: (0, c + nc)),
                  pl.BlockSpec((1, cw), lambda i, c: (0, c)),
                  pl.BlockSpec((1, cw), lambda i, c: (0, c + nc))],
        out_specs=[chunk2, chunk2, pl.BlockSpec((tt, cw), lambda i, c: (i, c))],
        out_shape=[_sds((2, s_len, D_FF), BF16), _sds((2, s_len, D_FF), F32), _sds((s_len, D_FF), BF16)],
        scratch=[pltpu.VMEM((nc, 2, SUBLANES, cw), F32)],
        args=(h2, w_up3, ffn_cw, ffn_cw, ffn_cb2, ffn_cb2), carry=carry)


def _ffn_tail(act, w_down, x1, gt_f, g_post, target):
    s_len = x1.shape[0]
    tt = min(TT_BIG, s_len)

    def body(act_ref, wd_ref, x1_ref, gtf_ref, gp_ref, tg_ref, dy2_ref, dout_ref, loss_ref, vs_ref):
        @pl.when(pl.program_id(0) == 0)
        def _():
            loss_ref[...] = jnp.zeros_like(loss_ref)
            vs_ref[...] = jnp.zeros_like(vs_ref)

        for rows in _row_pieces(tt):
            n2, r2 = _rms(jnp.dot(act_ref[rows, :], wd_ref[...], preferred_element_type=F32))
            out = x1_ref[rows, :] + gtf_ref[...] * (n2 * gp_ref[...])
            err = out - tg_ref[rows, :]
            do = err * (1.0 / D_MODEL)
            dout_ref[rows, :] = do
            loss_ref[...] += jnp.broadcast_to(0.5 * jnp.sum(err * err, keepdims=True) * (1.0 / D_MODEL),
                                              loss_ref.shape)
            vs_ref[0:1, :] += _colsum(do * n2 * gp_ref[...])
            vs_ref[1:2, :] += _colsum(do * gtf_ref[...] * n2)
            dy2_ref[rows, :] = _rms_bwd(do * gtf_ref[...] * gp_ref[...], n2, r2).astype(BF16)

    row = lambda c: pl.BlockSpec((tt, c), lambda i: (i, 0))
    vec = _const((1, D_MODEL))
    outs, _ = _call(
        body, "ffn_tail", (s_len // tt,),
        in_specs=[row(D_FF), _whole(), row(D_MODEL), vec, vec, row(D_MODEL)],
        out_specs=[row(D_MODEL), row(D_MODEL), _const((SUBLANES, LANES)), _const((SUBLANES, D_MODEL))],
        out_shape=[_sds((s_len, D_MODEL), BF16), _sds((s_len, D_MODEL), F32), _sds((SUBLANES, LANES), F32),
                   _sds((SUBLANES, D_MODEL), F32)],
        scratch=[], args=(act, w_down, x1, gt_f, g_post, target))
    return outs


def _ffn_conv(up_pre, prev8, cw_ref, cb):
    up = cb + cw_ref[FFN_CONV_K - 1:FFN_CONV_K, :] * up_pre
    taps = []
    for k in range(FFN_CONV_K - 1):
        tap = _shift_down(up_pre, prev8, FFN_CONV_K - 1 - k)
        taps.append(tap)
        up = up + cw_ref[k:k + 1, :] * tap
    return up, taps


def _ffn_bwd(d_y2, up_pre, up, ffn_cw, w_down, carry=None):
    s_len = d_y2.shape[0]
    tt = min(TT_BIG, s_len)
    nt = s_len // tt
    cw = FF_CW
    nc = D_FF // cw

    def body(dy2_ref, up_ref, upc_ref, cwg_ref, cwv_ref, wd_ref, dup_ref, cs_ref, nxt, cs_acc):
        i = pl.program_id(0)
        c = pl.program_id(1)

        @pl.when(i == 0)
        def _():
            nxt[c] = jnp.zeros((2, SUBLANES, cw), F32)
            cs_acc[c] = jnp.zeros((2, SUBLANES, cw), F32)

        pw = cw // 2
        for piece in range(2):
            cols = slice(piece * pw, (piece + 1) * pw)
            d_act = _dot_nt(dy2_ref[...], wd_ref[pl.ds(pl.multiple_of(c * cw + piece * pw, pw), pw), :])
            uv = upc_ref[1, :, cols]
            gl, dgl = _gelu_and_grad(upc_ref[0, :, cols])
            d_ug = d_act * uv * dgl
            d_uv = d_act * gl
            for half, (d_u, cw_ref) in enumerate(((d_ug, cwg_ref), (d_uv, cwv_ref))):
                nx = nxt[c, half, :, cols]
                x_in = up_ref[half, :, cols].astype(F32)
                d_pre = cw_ref[FFN_CONV_K - 1:FFN_CONV_K, cols] * d_u
                sums = [None] * (FFN_CONV_K + 1)
                sums[FFN_CONV_K - 1] = _colsum(d_u * x_in)
                for k in range(FFN_CONV_K - 1):
                    ahead = _shift_up(d_u, nx, FFN_CONV_K - 1 - k)
                    d_pre = d_pre + cw_ref[k:k + 1, cols] * ahead
                    sums[k] = _colsum(ahead * x_in)
                sums[FFN_CONV_K] = _colsum(d_u)
                pad = jnp.zeros((SUBLANES - FFN_CONV_K - 1, pw), F32)
                cs_acc[c, half, :, cols] += jnp.concatenate(sums + [pad], axis=0)
                nxt[c, half, :, cols] = d_u[0:SUBLANES]
                dup_ref[half, :, cols] = d_pre.astype(BF16)

        for cc in range(nc):
            @pl.when((i == nt - 1) & (c == cc))
            def _():
                cs_ref[:, cc * cw:(cc + 1) * cw] = cs_acc[cc, 0]
                cs_ref[:, D_FF + cc * cw:D_FF + (cc + 1) * cw] = cs_acc[cc, 1]

    row = pl.BlockSpec((tt, D_MODEL), lambda i, c: (nt - 1 - i, 0))
    blk = pl.BlockSpec((2, tt, cw), lambda i, c: (0, nt - 1 - i, c))
    return _call(
        body, "ffn_bwd", (nt, nc),
        in_specs=[row, blk, blk,
                  pl.BlockSpec((FFN_CONV_K, cw), lambda i, c: (0, c)),
                  pl.BlockSpec((FFN_CONV_K, cw), lambda i, c: (0, c + nc)),
                  _whole()],
        out_specs=[blk, _const((SUBLANES, 2 * D_FF))],
        out_shape=[_sds((2, s_len, D_FF), BF16), _sds((SUBLANES, 2 * D_FF), F32)],
        scratch=[pltpu.VMEM((nc, 2, SUBLANES, cw), F32), pltpu.VMEM((nc, 2, SUBLANES, cw), F32)],
        args=(d_y2, up_pre, up, ffn_cw, ffn_cw, w_down), carry=carry)


def _up_bwd(d_up, w_up3, x1, dout, y, w_out, g_pre, sc_f, g_post, gt_m, carry=None):
    s_len = x1.shape[0]
    tt = min(TT_BIG, s_len)

    def body(du_ref, wu_ref, x1_ref, do_ref, y_ref, wo_ref, g2_ref, sc_ref, gp_ref, gt_ref,
             dx1_ref, dy_ref, dyc_ref, vs_ref):
        @pl.when(pl.program_id(0) == 0)
        def _():
            vs_ref[...] = jnp.zeros_like(vs_ref)

        for rows in _row_pieces(tt):
            d_h2 = jnp.zeros((rows.stop - rows.start, D_MODEL), F32)
            for half in range(2):
                for ch in range(FF_CHUNKS):
                    d_h2 = d_h2 + _dot_nt(du_ref[half, rows, ch * FF_CHUNK_W:(ch + 1) * FF_CHUNK_W],
                                          wu_ref[half * FF_CHUNKS + ch])
            n1, r1 = _rms(x1_ref[rows, :])
            ng = n1 * g2_ref[...]
            vs_ref[0:1, :] += _colsum(d_h2)
            vs_ref[1:2, :] += _colsum(d_h2 * ng)
            d_ng = d_h2 * (1.0 + sc_ref[...])
            vs_ref[2:3, :] += _colsum(d_ng * n1)
            d_x1 = do_ref[rows, :] + _rms_bwd(d_ng * g2_ref[...], n1, r1)
            dx1_ref[rows, :] = d_x1
            n_y, r_y = _rms(y_ref[rows, :])
            vs_ref[3:4, :] += _colsum(d_x1 * n_y * gp_ref[...])
            d_on = d_x1 * gt_ref[...]
            vs_ref[4:5, :] += _colsum(d_on * n_y)
            d_y = _rms_bwd(d_on * gp_ref[...], n_y, r_y).astype(BF16)
            dy_ref[rows, :] = d_y
            dyc_ref[rows, :] = _dot_nt(d_y, wo_ref[...])

    row = lambda c: pl.BlockSpec((tt, c), lambda i: (i, 0))
    vec = _const((1, D_MODEL))
    return _call(
        body, "up_bwd", (s_len // tt,),
        in_specs=[pl.BlockSpec((2, tt, D_FF), lambda i: (0, i, 0)), _whole(), row(D_MODEL), row(D_MODEL), row(D_MODEL),
                  _whole(), vec, vec, vec, vec],
        out_specs=[row(D_MODEL), row(D_MODEL), row(LRU_W + GMLP_W), _const((SUBLANES, D_MODEL))],
        out_shape=[_sds((s_len, D_MODEL), F32), _sds((s_len, D_MODEL), BF16), _sds((s_len, LRU_W + GMLP_W), F32),
                   _sds((SUBLANES, D_MODEL), F32)],
        scratch=[], args=(d_up, w_up3, x1, dout, y, w_out, g_pre, sc_f, g_post, gt_m), carry=carry)


def _head_pair_block(hd):
    return (slice((hd // 2) * HEAD_DIM, (hd // 2 + 1) * HEAD_DIM), slice((hd % 2) * HEAD_DIM, (hd % 2 + 1) * HEAD_DIM))


def _mix_bwd(d_ycat, z, hl, conv_w, conv_b, wr_bd, wi_bd, b_r, b_i, lru_a, vn_g, vn_b, w_sp, w_sp_t, b_sp_t,
             g_lru, g_gmlp, carry=None):
    s_len = z.shape[0]
    tt = min(TT_MIX, s_len)
    nt = s_len // tt
    nblk = tt // POS_BLOCK
    hb = tt // SUBLANES

    def body(dyc_ref, z_ref, zh_ref, hl_ref, hh_ref, cw_ref, cb_ref, wr_ref, wi_ref, br_ref, bi_ref, la_ref,
             vg_ref, vb_ref, ws_ref, wst_ref, bst_ref, gl_ref, gg_ref,
             dz_ref, vs_ref, dcw_ref, dwrb_ref, dwib_ref, dws_ref, dbs_ref, nxt_dxc, nxt_a, nxt_lam, dwr_ref, dwi_ref):
        i = pl.program_id(0)
        first_tile = i == nt - 1

        @pl.when(i == 0)
        def _():
            for ref in (vs_ref, dcw_ref, dwr_ref, dwi_ref, dws_ref, dbs_ref, nxt_dxc, nxt_a, nxt_lam):
                ref[...] = jnp.zeros_like(ref)

        lx = z_ref[:, 0:LRU_W]
        gate = z_ref[:, LRU_W:2 * LRU_W]
        gu = z_ref[:, 2 * LRU_W:2 * LRU_W + GMLP_W]
        gv = z_ref[:, 2 * LRU_W + GMLP_W:]
        prev8 = jnp.where(first_tile, 0.0, zh_ref[...])
        hprev8 = jnp.where(first_tile, 0.0, hh_ref[...])

        xc, taps = _lru_conv(lx, prev8, cw_ref, cb_ref[...])
        a_par = la_ref[...]
        sp_a = _softplus(-a_par)
        r, ig, a, mult = _lru_gates(xc, wr_ref, wi_ref, br_ref[...], bi_ref[...], sp_a)
        hl = hl_ref[...]
        h_prev = _shift_down(hl, hprev8, 1)
        ggate, dggate = _gelu_and_grad(gate)
        y_lru = hl * ggate
        n_l, r_l = _rms(y_lru)
        d_nl = dyc_ref[:, 0:LRU_W]
        vs_ref[6:7, :] += _colsum(d_nl * n_l)
        d_yl = _rms_bwd(d_nl * gl_ref[...], n_l, r_l)
        d_hl = d_yl * ggate
        d_gate = d_yl * hl * dggate
        a_up = _shift_up(a, nxt_a[...], 1)
        lam = _scan_rev(a_up, d_hl, nxt_lam[0:1, :])
        nxt_a[...] = jnp.broadcast_to(a[0:1, :], nxt_a.shape)
        nxt_lam[...] = jnp.broadcast_to(lam[0:1, :], nxt_lam.shape)
        ixc = ig * xc
        d_la = lam * h_prev * a - lam * ixc * (a * a) / mult
        d_i = lam * mult * xc
        d_xc = lam * mult * ig
        vs_ref[3:4, :] += _colsum(d_la * r) * (LRU_C * _sigmoid(-a_par))
        d_pr = d_la * (-LRU_C * sp_a) * r * (1.0 - r)
        d_pi = d_i * ig * (1.0 - ig)
        vs_ref[1:2, :] += _colsum(d_pr)
        vs_ref[2:3, :] += _colsum(d_pi)
        dwr_ref[...] += _dot_tn(xc, d_pr)
        dwi_ref[...] += _dot_tn(xc, d_pi)
        d_xc = d_xc + _dot_nt(d_pr, wr_ref[...]) + _dot_nt(d_pi, wi_ref[...])
        vs_ref[0:1, :] += _colsum(d_xc)
        nx = nxt_dxc[...]
        d_lx = cw_ref[LRU_CONV_K - 1:LRU_CONV_K, :] * d_xc
        dcw_ref[LRU_CONV_K - 1:LRU_CONV_K, :] += _colsum(d_xc * lx)
        for k in range(LRU_CONV_K - 1):
            d_lx = d_lx + cw_ref[k:k + 1, :] * _shift_up(d_xc, nx, LRU_CONV_K - 1 - k)
            dcw_ref[k:k + 1, :] += _colsum(d_xc * taps[k])
        nxt_dxc[...] = d_xc[0:SUBLANES]
        dz_ref[:, 0:LRU_W] = d_lx.astype(BF16)
        dz_ref[:, LRU_W:2 * LRU_W] = d_gate.astype(BF16)

        u, du = _gelu_and_grad(gu)
        v, vhat, rs, dav = _gmlp_v(gv, vg_ref[...], vb_ref[...])
        mask = _ws_mask()
        sp_parts = []
        for nb in range(nblk):
            rowp = []
            for g in range(N_GROUPS):
                wsm = jnp.where(mask, ws_ref[g], 0.0)
                vblk = v[nb * POS_BLOCK:(nb + 1) * POS_BLOCK, g * LANES:(g + 1) * LANES]
                rowp.append(_dot(wsm, vblk) + bst_ref[:, g:g + 1])
            sp_parts.append(jnp.concatenate(rowp, axis=1))
        sp = jnp.concatenate(sp_parts, axis=0) if nblk > 1 else sp_parts[0]
        y_g = u * sp
        n_g, r_g = _rms(y_g)
        d_ng = dyc_ref[:, LRU_W:]
        vs_ref[7:8, :] += _colsum(d_ng * n_g)
        d_yg = _rms_bwd(d_ng * gg_ref[...], n_g, r_g)
        d_gu = d_yg * sp * du
        d_sp = d_yg * u
        mask_t = _ws_mask(transposed=True)
        ones8 = jnp.ones((SUBLANES, LANES), F32)
        dv_parts = []
        for nb in range(nblk):
            rowp = []
            for g in range(N_GROUPS):
                rs_, cs_ = slice(nb * POS_BLOCK, (nb + 1) * POS_BLOCK), slice(g * LANES, (g + 1) * LANES)
                dsp_blk = d_sp[rs_, cs_]
                dbs_ref[g:g + 1, :] += lax.dot_general(
                    ones8, dsp_blk, (((1,), (1,)), ((), ())), preferred_element_type=F32,
                    precision=lax.Precision.HIGHEST)[0:1, :]
                dws_ref[g] += _dot_nt(dsp_blk, v[rs_, cs_])
                wsm_t = jnp.where(mask_t, wst_ref[g], 0.0)
                rowp.append(_dot(wsm_t, dsp_blk))
            dv_parts.append(jnp.concatenate(rowp, axis=1))
        d_v = jnp.concatenate(dv_parts, axis=0) if nblk > 1 else dv_parts[0]
        vs_ref[4:5, :] += _colsum(d_v * vhat)
        vs_ref[5:6, :] += _colsum(d_v)
        d_vh = d_v * vg_ref[...]
        d_av = rs * (d_vh - jnp.mean(d_vh, axis=-1, keepdims=True)
                     - vhat * jnp.mean(d_vh * vhat, axis=-1, keepdims=True))
        dz_ref[:, 2 * LRU_W:2 * LRU_W + GMLP_W] = d_gu.astype(BF16)
        dz_ref[:, 2 * LRU_W + GMLP_W:] = (d_av * dav).astype(BF16)

        @pl.when(i == nt - 1)
        def _():
            for hd in range(N_HEADS):
                blk = slice(hd * HEAD_DIM, (hd + 1) * HEAD_DIM)
                dwrb_ref[_head_pair_block(hd)] = dwr_ref[blk, blk]
                dwib_ref[_head_pair_block(hd)] = dwi_ref[blk, blk]
            for g in range(N_GROUPS):
                dws_ref[g] = jnp.where(mask, dws_ref[g], 0.0)

    rev = lambda c: pl.BlockSpec((tt, c), lambda i: (nt - 1 - i, 0))
    halo = pl.BlockSpec((SUBLANES, LRU_W), lambda i: (jnp.maximum((nt - 1 - i) * hb - 1, 0), 0))
    v512 = _const((1, LRU_W))
    return _call(
        body, "mix_bwd", (nt,),
        in_specs=[rev(LRU_W + GMLP_W), rev(IN_COLS), halo, rev(LRU_W), halo,
                  _const((LRU_CONV_K, LRU_W)), v512, _whole(), _whole(), v512, v512, v512, v512, v512,
                  _whole(), _whole(), _whole(), v512, v512],
        out_specs=[rev(IN_COLS), _const((SUBLANES, LRU_W)), _const((SUBLANES, LRU_W)),
                   _const((LRU_W // 2, 2 * HEAD_DIM)), _const((LRU_W // 2, 2 * HEAD_DIM)),
                   _const((N_GROUPS, POS_BLOCK, POS_BLOCK)), _const((SUBLANES, POS_BLOCK))],
        out_shape=[_sds((s_len, IN_COLS), BF16), _sds((SUBLANES, LRU_W), F32), _sds((SUBLANES, LRU_W), F32),
                   _sds((LRU_W // 2, 2 * HEAD_DIM), F32), _sds((LRU_W // 2, 2 * HEAD_DIM), F32),
                   _sds((N_GROUPS, POS_BLOCK, POS_BLOCK), F32), _sds((SUBLANES, POS_BLOCK), F32)],
        scratch=[pltpu.VMEM((SUBLANES, LRU_W), F32), pltpu.VMEM((SUBLANES, LRU_W), F32),
                 pltpu.VMEM((SUBLANES, LRU_W), F32), pltpu.VMEM((LRU_W, LRU_W), F32), pltpu.VMEM((LRU_W, LRU_W), F32)],
        args=(d_ycat, z, z, hl, hl, conv_w, conv_b, wr_bd, wi_bd, b_r, b_i, lru_a, vn_g, vn_b, w_sp, w_sp_t, b_sp_t,
              g_lru, g_gmlp), carry=carry)


def _in_bwd(d_z, w_in, x, d_x1, g, sc, carry=None):
    s_len = x.shape[0]
    tt = min(TT_BIG, s_len)

    def body(dz_ref, w_ref, x_ref, dx1_ref, g_ref, sc_ref, gx_ref, vs_ref):
        @pl.when(pl.program_id(0) == 0)
        def _():
            vs_ref[...] = jnp.zeros_like(vs_ref)

        for rows in _row_pieces(tt):
            d_h = _dot_nt(dz_ref[rows, :], w_ref[...])
            n, r = _rms(x_ref[rows, :])
            vs_ref[0:1, :] += _colsum(d_h)
            vs_ref[1:2, :] += _colsum(d_h * n * g_ref[...])
            d_ng = d_h * (1.0 + sc_ref[...])
            vs_ref[2:3, :] += _colsum(d_ng * n)
            gx_ref[rows, :] = dx1_ref[rows, :] + _rms_bwd(d_ng * g_ref[...], n, r)

    row = lambda c: pl.BlockSpec((tt, c), lambda i: (i, 0))
    vec = _const((1, D_MODEL))
    return _call(
        body, "in_bwd", (s_len // tt,),
        in_specs=[row(IN_COLS), _whole(), row(D_MODEL), row(D_MODEL), vec, vec],
        out_specs=[row(D_MODEL), _const((SUBLANES, D_MODEL))],
        out_shape=[_sds((s_len, D_MODEL), F32), _sds((SUBLANES, D_MODEL), F32)],
        scratch=[], args=(d_z, w_in, x, d_x1, g, sc), carry=carry)


def _wgrad(a, b, name, by_rows=False, carry=None):
    s_len, k_dim = a.shape
    halves = b.ndim == 3
    n_dim = b.shape[-1] * (2 if halves else 1)

    def body(a_ref, b_ref, ob_ref, own_ref):
        out = _dot_tn(a_ref[...], b_ref[0] if halves else b_ref[...])
        ob_ref[...] = out.astype(BF16)

        @pl.when(pl.program_id(0) == _dev_index(_my_pos()))
        def _():
            own_ref[...] = out

    if by_rows:
        tile = k_dim // N_DEV
        a_spec = pl.BlockSpec((s_len, tile), lambda j: (0, j))
        b_spec = pl.BlockSpec((s_len, n_dim), lambda j: (0, 0))
        o_spec = pl.BlockSpec((tile, n_dim), lambda j: (j, 0))
        own_shape = (tile, n_dim)
    else:
        tile = n_dim // N_DEV
        a_spec = pl.BlockSpec((s_len, k_dim), lambda j: (0, 0))
        if halves:
            per_half = N_DEV // 2
            b_spec = pl.BlockSpec((1, s_len, tile), lambda j: (j // per_half, 0, j % per_half))
        else:
            b_spec = pl.BlockSpec((s_len, tile), lambda j: (0, j))
        o_spec = pl.BlockSpec((k_dim, tile), lambda j: (0, j))
        own_shape = (k_dim, tile)
    return _call(
        body, name, (N_DEV,), in_specs=[a_spec, b_spec], out_specs=[o_spec, _const(own_shape)],
        out_shape=[_sds((k_dim, n_dim), BF16), _sds(own_shape, F32)],
        scratch=[], args=(a, b), carry=carry)


def _adam_math(w, g, m, v):
    m = ADAM_B1 * m + (1.0 - ADAM_B1) * g
    v = ADAM_B2 * v + (1.0 - ADAM_B2) * (g * g)
    m_hat = m / (1.0 - ADAM_B1 ** ADAM_STEP)
    v_hat = v / (1.0 - ADAM_B2 ** ADAM_STEP)
    delta = -ADAM_LR * (m_hat / (jnp.sqrt(v_hat) + ADAM_EPS) + ADAM_WD * w)
    return delta, m, v


def _row_tile(rows, cols, n_f32_arrays):
    budget = VMEM_LIMIT // 2
    tr = rows
    while tr % 2 == 0 and tr // 2 >= SUBLANES and (tr // 2) % SUBLANES == 0 and tr * cols * 4 * n_f32_arrays * 2 > budget:
        tr //= 2
    return tr


def _adamw_sum(w, g_own, recv, m, v, name):
    _, rows, cols = w.shape
    n_recv = len(recv)
    tr = _row_tile(rows, cols, 10)
    nb = rows // tr

    def body(w_ref, g_ref, *rest):
        r_refs = rest[:n_recv]
        m_ref, v_ref, go_ref, d_ref, mo_ref, vo_ref = rest[n_recv:]
        g = g_ref[...]
        for r_ref in r_refs:
            for k in range(r_ref.shape[0]):
                g = g + r_ref[k].astype(F32)
        go_ref[0] = g
        d_ref[0], mo_ref[0], vo_ref[0] = _adam_math(w_ref[0], g, m_ref[0], v_ref[0])

    blk = pl.BlockSpec((1, tr, cols), lambda i: (0, i, 0))
    return pl.pallas_call(
        body, name=name, grid=(nb,),
        in_specs=[blk, pl.BlockSpec((tr, cols), lambda i: (i, 0))]
        + [pl.BlockSpec((r.shape[0], tr, cols), lambda i: (0, i, 0)) for r in recv] + [blk, blk],
        out_specs=[blk] * 4, out_shape=[_sds((1, rows, cols), F32)] * 4,
        compiler_params=_cparams(("arbitrary",)),
    )(w, g_own, *recv, m, v)


def _row_of_each(ref, row):
    cols = ref.shape[1]
    rows = _rows((N_DEV, cols))
    out = jnp.zeros((N_DEV, cols), F32)
    for d in range(N_DEV):
        picked = ref[d * SUBLANES + row:d * SUBLANES + row + 1, :]
        out = jnp.where(rows == d, jnp.broadcast_to(picked, (N_DEV, cols)), out)
    return out


def _my_columns(full, width, me):
    out = jnp.zeros(full.shape[:-1] + (width,), F32)
    for d in range(N_DEV):
        out = out + jnp.where(me == d, full[:, d * width:(d + 1) * width], 0.0)
    return out


def _adamw_wada(c_all, vs_in_all, vs_up_all, vs_ffn_all, w, m, v):
    _, rows, cols = w.shape

    def body(c_ref, vi_ref, vu_ref, vf_ref, w_ref, m_ref, v_ref, go_ref, d_ref, mo_ref, vo_ref):
        me = _dev_index(_my_pos())
        cv = _row_of_each(c_ref, 0)
        ca = cv * _sigmoid(cv)
        dmod = jnp.concatenate([_row_of_each(vi_ref, 0), _row_of_each(vi_ref, 1), _row_of_each(vu_ref, 3),
                                _row_of_each(vu_ref, 0), _row_of_each(vu_ref, 1), _row_of_each(vf_ref, 0)], axis=1)
        dm = _my_columns(dmod, cols, me)
        g = lax.dot_general(ca, dm, (((0,), (0,)), ((), ())), preferred_element_type=F32,
                            precision=lax.Precision.HIGHEST)
        go_ref[0] = g
        d_ref[0], mo_ref[0], vo_ref[0] = _adam_math(w_ref[0], g, m_ref[0], v_ref[0])

    return pl.pallas_call(
        body, name="adamw_w_ada", out_shape=[_sds((1, rows, cols), F32)] * 4,
        in_specs=[_whole()] * 7, out_specs=[_whole()] * 4,
        compiler_params=_cparams(),
    )(c_all, vs_in_all, vs_up_all, vs_ffn_all, w, m, v)


def _adamw_small(gathered, reduced, params, conv_params):
    names = list(params) + list(conv_params)
    allp = {**params, **conv_params}
    n_g = len(gathered) + len(reduced)

    def body(*refs):
        g_refs = refs[:n_g]
        p_refs = refs[n_g:n_g + 3 * len(names)]
        o_refs = refs[n_g + 3 * len(names):]
        me = _dev_index(_my_pos())

        def total(ref):
            s = ref[0:SUBLANES, :]
            for d in range(1, N_DEV):
                s = s + ref[d * SUBLANES:(d + 1) * SUBLANES, :]
            return s

        vs_in, vs_up, vs_ffn, loss = [total(r) for r in g_refs[:4]]
        cs, vs_mix, dcw, dwr, dwi, dws, dbs = [r[...] for r in g_refs[4:]]
        o_refs[-1][...] = loss[0:1, 0:1]
        mine = lambda full, width: _my_columns(full, width, me)

        all_ = (slice(None), slice(None))
        heads = lambda row: [((0, slice(h, h + 1), slice(None)), row[:, h * HEAD_DIM:(h + 1) * HEAD_DIM])
                             for h in range(N_HEADS)]
        blocks = lambda pairs: [((0, h), pairs[_head_pair_block(h)]) for h in range(N_HEADS)]
        pieces = {
            "b_ada": [((slice(None), slice(k * D_MODEL, (k + 1) * D_MODEL)), row) for k, row in enumerate(
                (vs_in[0:1], vs_in[1:2], vs_up[3:4], vs_up[0:1], vs_up[1:2], vs_ffn[0:1]))],
            "g_mix_pre": [(all_, vs_in[2:3])], "g_mix_post": [(all_, vs_up[4:5])],
            "g_ffn_pre": [(all_, vs_up[2:3])], "g_ffn_post": [(all_, vs_ffn[1:2])],
            "conv_b": [(all_, vs_mix[0:1])], "b_rgate": heads(vs_mix[1:2]), "b_igate": heads(vs_mix[2:3]),
            "lru_a": [(all_, vs_mix[3:4])], "v_norm_g": [(all_, vs_mix[4:5])], "v_norm_b": [(all_, vs_mix[5:6])],
            "g_lru_out": [(all_, vs_mix[6:7])], "g_gmlp_out": [(all_, vs_mix[7:8])],
            "w_rgate": blocks(dwr), "w_igate": blocks(dwi),
            "w_spatial": [((0, g), dws[g * POS_BLOCK:(g + 1) * POS_BLOCK, :]) for g in range(N_GROUPS)],
            "b_spatial": [((0,), dbs[0:N_GROUPS])],
            "ffn_conv_b": [(all_, cs[FFN_CONV_K:FFN_CONV_K + 1])],
            "conv_w": [((0,), mine(dcw[0:LRU_CONV_K], LRU_W // N_DEV))],
            "ffn_conv_w": [((0,), mine(cs[0:FFN_CONV_K], 2 * D_FF // N_DEV))],
        }
        for n_i, name in enumerate(names):
            w_ref, m_ref, v_ref = p_refs[3 * n_i:3 * n_i + 3]
            go_ref, d_ref, mo_ref, vo_ref = o_refs[4 * n_i:4 * n_i + 4]
            for idx, g in pieces[name]:
                go_ref[idx] = g
                d_ref[idx], mo_ref[idx], vo_ref[idx] = _adam_math(w_ref[idx], g, m_ref[idx], v_ref[idx])

    flat_params = [a for n in names for a in allp[n]]
    out_shape = [_sds(allp[n][0].shape, F32) for n in names for _ in range(4)] + [_sds((1, 1), F32)]
    outs = pl.pallas_call(
        body, name="adamw_small", out_shape=out_shape,
        in_specs=[_whole()] * (n_g + len(flat_params)), out_specs=[_whole()] * len(out_shape),
        compiler_params=_cparams(),
    )(*gathered, *reduced, *flat_params)
    return {n: outs[4 * i:4 * i + 4] for i, n in enumerate(names)}, outs[-1]


def _my_pos():
    return lax.axis_index("x"), lax.axis_index("y"), lax.axis_index("c")


def _flip(pos, k):
    x, y, c = pos
    return (1 - x if k & 4 else x, 1 - y if k & 2 else y, 1 - c if k & 1 else c)


def _dev_index(pos):
    x, y, c = pos
    return 4 * x + 2 * y + c


def _all_gather_small(ins, outs, send_sems, recv_sems):
    n = len(ins)
    me = _my_pos()

    def slot(a, pos):
        rows = ins[a].shape[0]
        return outs[a].at[pl.ds(pl.multiple_of(_dev_index(pos) * rows, SUBLANES), rows), :]

    def copy(a, k, block):
        return pltpu.make_async_remote_copy(
            src_ref=ins[a], dst_ref=slot(a, block), send_sem=send_sems.at[a, k - 1], recv_sem=recv_sems.at[a, k - 1],
            device_id=_flip(me, k), device_id_type=MESH)

    sends = [copy(a, k, me) for a in range(n) for k in range(1, N_DEV)]
    for cp in sends:
        cp.start()
    for a in range(n):
        rows = ins[a].shape[0]
        outs[a][pl.ds(pl.multiple_of(_dev_index(me) * rows, SUBLANES), rows), :] = ins[a][...]
    for a in range(n):
        for k in range(1, N_DEV):
            copy(a, k, _flip(me, k)).wait_recv()
    for cp in sends:
        cp.wait_send()


def _prologue(c8, cw8, fcw8, w_ada, b_ada, carry):
    cols = w_ada.shape[1]

    def body(c_ref, cw_ref, fcw_ref, w_ref, b_ref, call_ref, cwall_ref, fcwall_ref, modall_ref, mod_scr,
             s1, r1, s2, r2, start_carry):
        _all_gather_small([c_ref, cw_ref, fcw_ref], [call_ref, cwall_ref, fcwall_ref], s1, r1)
        start_carry()
        cv = _row_of_each(call_ref, 0)
        ca = cv * _sigmoid(cv)
        b_cols = _my_columns(b_ref[...], cols, _dev_index(_my_pos()))
        mod_scr[...] = jnp.dot(ca, w_ref[...], preferred_element_type=F32, precision=lax.Precision.HIGHEST) + b_cols
        _all_gather_small([mod_scr], [modall_ref], s2, r2)

    sem = lambda n: pltpu.SemaphoreType.DMA((n, N_DEV - 1))
    return _call(
        body, "prologue", (1,), in_specs=[_whole()] * 5, out_specs=[_whole()] * 4,
        out_shape=[_sds((N_DEV * SUBLANES, a.shape[1]), F32) for a in (c8, cw8, fcw8)]
        + [_sds((N_DEV * N_DEV, cols), F32)],
        scratch=[pltpu.VMEM((N_DEV, cols), F32), sem(3), sem(3), sem(1), sem(1)],
        args=(c8, cw8, fcw8, w_ada, b_ada), carry=carry, body_starts_carry=True)


def _reduce_small(gath, red, carry=None):
    n_g, n_r = len(gath), len(red)
    chip_flips = (4, 2, 6)

    def body(*refs, start_carry):
        g_in, r_in = refs[:n_g], refs[n_g:n_g + n_r]
        g_out, r_out = refs[n_g + n_r:2 * n_g + n_r], refs[2 * n_g + n_r:2 * (n_g + n_r)]
        scr = refs[2 * (n_g + n_r):]
        sib, land = scr[:n_r], scr[n_r:2 * n_r]
        g_send, g_recv, s_send, s_recv, i_send, i_recv, f_send, f_recv = scr[2 * n_r:]
        me = _my_pos()
        c = me[2]
        sibling = _flip(me, 1)

        def slot(a, pos):
            return g_out[a].at[pl.ds(pl.multiple_of(_dev_index(pos) * SUBLANES, SUBLANES), SUBLANES), :]

        def gcopy(a, k):
            return pltpu.make_async_remote_copy(
                src_ref=g_in[a], dst_ref=slot(a, me), send_sem=g_send.at[a, k - 1], recv_sem=g_recv.at[a, k - 1],
                device_id=_flip(me, k), device_id_type=MESH)

        def scopy(a):
            return pltpu.make_async_remote_copy(
                src_ref=r_in[a], dst_ref=sib[a], send_sem=s_send.at[a], recv_sem=s_recv.at[a],
                device_id=sibling, device_id_type=MESH)

        def icopy(a, j):
            return pltpu.make_async_remote_copy(
                src_ref=r_out[a], dst_ref=land[a].at[j], send_sem=i_send.at[a, j], recv_sem=i_recv.at[a, j],
                device_id=_flip(me, chip_flips[j]), device_id_type=MESH)

        def fcopy(a, j):
            return pltpu.make_async_remote_copy(
                src_ref=land[a].at[j], dst_ref=land[a].at[j], send_sem=f_send.at[a, j], recv_sem=f_recv.at[a, j],
                device_id=sibling, device_id_type=MESH)

        gathers = [gcopy(a, k) for a in range(n_g) for k in range(1, N_DEV)]
        swaps = [scopy(a) for a in range(n_r)]
        for cp in gathers + swaps:
            cp.start()
        for a in range(n_g):
            g_out[a][pl.ds(pl.multiple_of(_dev_index(me) * SUBLANES, SUBLANES), SUBLANES), :] = g_in[a][...]
        for a in range(n_r):
            swaps[a].wait_recv()
            r_out[a][...] = r_in[a][...] + sib[a][...]

        for core in range(2):
            @pl.when(c == core)
            def _():
                for a in range(core, n_r, 2):
                    for j in range(3):
                        icopy(a, j).start()

        start_carry()

        for core in range(2):
            mine = [a for a in range(n_r) if a % 2 == core]
            theirs = [a for a in range(n_r) if a % 2 != core]

            @pl.when(c == core)
            def _():
                out = [icopy(a, j) for a in mine for j in range(3)]
                fwd = []
                for a in mine:
                    for j in range(3):
                        icopy(a, j).wait_recv()
                        cp = fcopy(a, j)
                        cp.start()
                        fwd.append(cp)
                for a in theirs:
                    for j in range(3):
                        fcopy(a, j).wait_recv()
                for cp in out + fwd:
                    cp.wait_send()

        for a in range(n_r):
            r_out[a][...] = (r_out[a][...] + land[a][1]) + (land[a][0] + land[a][2])
        for a in range(n_g):
            for k in range(1, N_DEV):
                pltpu.make_async_remote_copy(
                    src_ref=g_in[a], dst_ref=slot(a, _flip(me, k)), send_sem=g_send.at[a, k - 1],
                    recv_sem=g_recv.at[a, k - 1], device_id=_flip(me, k), device_id_type=MESH).wait_recv()
        for cp in gathers + swaps:
            cp.wait_send()

    shapes = [tuple(a.shape) for a in red]
    outs, carried = _call(
        body, "reduce_small", (1,), in_specs=[_whole()] * (n_g + n_r), out_specs=[_whole()] * (n_g + n_r),
        out_shape=[_sds((N_DEV * SUBLANES, a.shape[1]), F32) for a in gath] + [_sds(s, F32) for s in shapes],
        scratch=[pltpu.VMEM(s, F32) for s in shapes] + [pltpu.VMEM((3,) + s, F32) for s in shapes]
        + [pltpu.SemaphoreType.DMA((n_g, N_DEV - 1)), pltpu.SemaphoreType.DMA((n_g, N_DEV - 1)),
           pltpu.SemaphoreType.DMA((n_r,)), pltpu.SemaphoreType.DMA((n_r,)),
           pltpu.SemaphoreType.DMA((n_r, 3)), pltpu.SemaphoreType.DMA((n_r, 3)),
           pltpu.SemaphoreType.DMA((n_r, 3)), pltpu.SemaphoreType.DMA((n_r, 3))],
        args=tuple(gath) + tuple(red), carry=carry, body_starts_carry=True)
    return (outs[:n_g], outs[n_g:]), carried


STACKED = "stacked"


def _region(ref, shard_shape, col_sharded, pos):
    r, cdim = shard_shape
    d = _dev_index(pos)
    if col_sharded == STACKED:
        return ref.at[d]
    if col_sharded:
        return ref.at[:, pl.ds(pl.multiple_of(d * cdim, LANES), cdim)]
    return ref.at[pl.ds(pl.multiple_of(d * r, 2 * SUBLANES), r), :]


def _gather_carry(shards, col_sharded):
    n_w = len(shards)
    shapes = [tuple(s.shape) for s in shards]
    full_shapes = [(N_DEV,) + s if cs == STACKED else (s[0], s[1] * N_DEV) if cs else (s[0] * N_DEV, s[1])
                   for s, cs in zip(shapes, col_sharded)]

    def tools(out_refs, scr):
        send_sems, recv_sems = scr[n_w], scr[n_w + 1]
        me = _my_pos()
        x, y, c = me
        sibling = (x, y, 1 - c)
        chips = [(1 - x, y), (x, 1 - y), (1 - x, 1 - y)]

        def region(w, pos):
            return _region(out_refs[w], shapes[w], col_sharded[w], pos)

        def copy(w, k, block, to, src=None):
            return pltpu.make_async_remote_copy(
                src_ref=region(w, block) if src is None else src, dst_ref=region(w, block),
                send_sem=send_sems.at[w, k], recv_sem=recv_sems.at[w, k], device_id=to, device_id_type=MESH)

        def first(w):
            return [copy(w, 0, me, sibling, src=scr[w])] + [
                copy(w, 1 + j, me, (*chip, c), src=scr[w]) for j, chip in enumerate(chips)]

        def mine(w):
            return pltpu.make_async_copy(scr[w], region(w, me), scr[n_w + 2].at[w])

        return me, c, sibling, chips, copy, first, mine

    def start(ins, outs, scr):
        _, _, _, _, _, first, mine = tools(outs, scr)
        for w in range(n_w):
            scr[w][...] = ins[w][...].astype(BF16)
            for cp in first(w) + [mine(w)]:
                cp.start()

    def finish(ins, outs, scr):
        me, c, sibling, chips, copy, first, mine = tools(outs, scr)
        passed = []
        for w in range(n_w):
            for j, chip in enumerate(chips):
                copy(w, 1 + j, (*chip, c), me).wait_recv()
                fwd = copy(w, 4 + j, (*chip, c), sibling)
                fwd.start()
                passed.append(fwd)
        for w in range(n_w):
            copy(w, 0, sibling, me).wait_recv()
            for j, chip in enumerate(chips):
                copy(w, 4 + j, (*chip, 1 - c), me).wait_recv()
        for w in range(n_w):
            for cp in first(w):
                cp.wait_send()
            mine(w).wait()
        for cp in passed:
            cp.wait_send()

    return _Carry(
        inputs=list(shards), in_specs=[_whole()] * n_w,
        out_shape=[_sds(s, BF16) for s in full_shapes], out_specs=[_any()] * n_w,
        scratch=[pltpu.VMEM(s, BF16) for s in shapes]
        + [pltpu.SemaphoreType.DMA((n_w, N_DEV - 1)), pltpu.SemaphoreType.DMA((n_w, N_DEV - 1)),
           pltpu.SemaphoreType.DMA((n_w,))],
        start=start, finish=finish)


CHIP_FLIPS = (4, 2, 6)


def _pair_reduce(g_bf, g_own, col_sharded):
    shape = tuple(g_own.shape)
    n = len(CHIP_FLIPS)

    def body(g_ref, own_ref, hown_ref, hout_ref, mine, sib, send_sems, recv_sems, local_sems):
        me = _my_pos()
        sibling = _flip(me, 1)
        flips = (0,) + CHIP_FLIPS

        def region(pos):
            return _region(g_ref, shape, col_sharded, pos)

        local = [pltpu.make_async_copy(region(_flip(me, f)), mine.at[s], local_sems.at[s])
                 for s, f in enumerate(CHIP_FLIPS)]
        sends = [pltpu.make_async_remote_copy(
            src_ref=region(_flip(sibling, f)), dst_ref=sib.at[s], send_sem=send_sems.at[s], recv_sem=recv_sems.at[s],
            device_id=sibling, device_id_type=MESH) for s, f in enumerate(flips)]
        for cp in local + sends:
            cp.start()
        for cp in local:
            cp.wait()
        for cp in sends:
            cp.wait_recv()
        hown_ref[...] = own_ref[...] + sib[0].astype(F32)
        for s in range(n):
            hout_ref[s] = (mine[s].astype(F32) + sib[s + 1].astype(F32)).astype(BF16)
        for cp in sends:
            cp.wait_send()

    return pl.pallas_call(
        body, name="pair_reduce", out_shape=[_sds(shape, F32), _sds((n,) + shape, BF16)],
        in_specs=[_any(), _whole()], out_specs=[_whole(), _whole()],
        scratch_shapes=[pltpu.VMEM((n,) + shape, BF16), pltpu.VMEM((n + 1,) + shape, BF16),
                        pltpu.SemaphoreType.DMA((n + 1,)), pltpu.SemaphoreType.DMA((n + 1,)),
                        pltpu.SemaphoreType.DMA((n,))],
        compiler_params=pltpu.CompilerParams(vmem_limit_bytes=VMEM_LIMIT),
    )(g_bf, g_own)


def _chip_scatter_carry(h_out):
    n = len(CHIP_FLIPS)

    def copies(ins, outs, scr):
        send_sems, recv_sems = scr
        me = _my_pos()
        return [pltpu.make_async_remote_copy(
            src_ref=ins[0].at[j], dst_ref=outs[0].at[j], send_sem=send_sems.at[j], recv_sem=recv_sems.at[j],
            device_id=_flip(me, CHIP_FLIPS[j]), device_id_type=MESH) for j in range(n)]

    def start(ins, outs, scr):
        for cp in copies(ins, outs, scr):
            cp.start()

    def finish(ins, outs, scr):
        cps = copies(ins, outs, scr)
        for cp in cps:
            cp.wait_recv()
        for cp in cps:
            cp.wait_send()

    return _Carry(inputs=[h_out], in_specs=[_any()], out_shape=[_sds(tuple(h_out.shape), BF16)], out_specs=[_any()],
                  scratch=[pltpu.SemaphoreType.DMA((n,)), pltpu.SemaphoreType.DMA((n,))], start=start, finish=finish)


def _scatter_carry(grads_bf, shard_shapes, col_sharded, relations):
    n_w = len(grads_bf)
    shapes = [tuple(s) for s in shard_shapes]

    def copies(ins, outs, scr):
        send_sems, recv_sems = scr
        me = _my_pos()
        out = []
        for w in range(n_w):
            for i, k in enumerate(relations[w]):
                peer = _flip(me, k)
                out.append(pltpu.make_async_remote_copy(
                    src_ref=_region(ins[w], shapes[w], col_sharded[w], peer), dst_ref=outs[w].at[i],
                    send_sem=send_sems.at[w, i], recv_sem=recv_sems.at[w, i],
                    device_id=peer, device_id_type=MESH))
        return out

    def start(ins, outs, scr):
        for cp in copies(ins, outs, scr):
            cp.start()

    def finish(ins, outs, scr):
        cps = copies(ins, outs, scr)
        for cp in cps:
            cp.wait_recv()
        for cp in cps:
            cp.wait_send()

    return _Carry(
        inputs=list(grads_bf), in_specs=[_any()] * n_w,
        out_shape=[_sds((len(r),) + s, BF16) for r, s in zip(relations, shapes)], out_specs=[_any()] * n_w,
        scratch=[pltpu.SemaphoreType.DMA((n_w, N_DEV - 1)), pltpu.SemaphoreType.DMA((n_w, N_DEV - 1))],
        start=start, finish=finish)


def _block_diag(w):
    eye = jnp.eye(N_HEADS, dtype=w.dtype)
    return (eye[:, None, :, None] * w[:, :, None, :]).reshape(N_HEADS * HEAD_DIM, N_HEADS * HEAD_DIM)


def _pad_rows(a):
    return jnp.pad(a, ((0, SUBLANES - a.shape[0]), (0, 0)))


def _columns_from_devices(gathered, rows):
    w = gathered.shape[1]
    return gathered.reshape(N_DEV, SUBLANES, w)[:, :rows].transpose(1, 0, 2).reshape(rows, N_DEV * w)


def _local_step(x2, target, mod, w_in_f, w_full, conv_w_full, ffn_cw_full,
                g_mix_pre, g_mix_post, conv_b, w_rgate, b_rgate, w_igate, b_igate, lru_a, v_norm_g, v_norm_b,
                w_spatial, b_spatial, g_lru_out, g_gmlp_out, g_ffn_pre, g_ffn_post, ffn_conv_b,
                gather=None, scatter=None):
    sh_m, sc_m, gt_m, sh_f, sc_f, gt_f = [mod[k] for k in range(N_MOD)]
    wr_bd = _block_diag(w_rgate[0]).astype(BF16)
    wi_bd = _block_diag(w_igate[0]).astype(BF16)
    b_r = b_rgate.reshape(1, LRU_W)
    b_i = b_igate.reshape(1, LRU_W)
    b_sp_t = b_spatial[0].T
    w_sp_t = jnp.swapaxes(w_spatial[0], 1, 2)

    def arriving(*names):
        return gather(*names) if gather else None

    near, far = (1, 2, 3, 4, 5), (6, 7)

    def leaving(*parts):
        return scatter(parts) if scatter else None

    def received(recv, parts, outs):
        for (name, _, _), out in zip(parts, outs):
            recv.setdefault(name, []).append(out)

    mix_params = (conv_w_full, conv_b, wr_bd, wi_bd, b_r, b_i, lru_a, v_norm_g, v_norm_b)
    w_out_f = w_full["w_out"]
    (z, h, ycat, hl, y, x1, h2), got = _mix_fwd(
        x2, sh_m, sc_m, g_mix_pre, w_in_f, *mix_params, w_spatial[0], b_sp_t, g_lru_out, g_gmlp_out,
        w_out_f, g_mix_post, gt_m, g_ffn_pre, sc_f, sh_f, carry=arriving("w_up"))
    w_up_f = got[0] if gather else w_full["w_up"]
    (up_pre, up, act), got = _ffn_fwd(h2, w_up_f, ffn_cw_full, ffn_conv_b, carry=arriving("w_down"))
    w_down_f = got[0] if gather else w_full["w_down"]
    d_y2, dout, loss_acc, vs_ffn = _ffn_tail(act, w_down_f, x1, gt_f, g_ffn_post, target)

    recv = {}
    gw_down, _ = _wgrad(act, d_y2, "wgrad_down", by_rows=True)
    parts = [("w_down", gw_down[0], near + far)]
    (d_up, cs_ffn), got = _ffn_bwd(d_y2, up_pre, up, ffn_cw_full, w_down_f, carry=leaving(*parts))
    received(recv, parts, got)
    gw_up, _ = _wgrad(h2, d_up, "wgrad_up")
    parts = [("w_up", gw_up[0], near)]
    (d_x1, d_y, d_ycat, vs_up), got = _up_bwd(
        d_up, w_up_f, x1, dout, y, w_out_f, g_ffn_pre, sc_f, g_mix_post, gt_m, carry=leaving(*parts))
    received(recv, parts, got)
    gw_out, _ = _wgrad(ycat, d_y, "wgrad_out", by_rows=True)
    parts = [("w_up", gw_up[0], far), ("w_out", gw_out[0], near + far)]
    (d_z, vs_mix, dcw, d_wr, d_wi, d_ws, d_bs), got = _mix_bwd(
        d_ycat, z, hl, *mix_params, w_spatial[0], w_sp_t, b_sp_t, g_lru_out, g_gmlp_out, carry=leaving(*parts))
    received(recv, parts, got)
    gw_in, _ = _wgrad(h, d_z, "wgrad_in")
    pending = None
    if scatter:
        h_own, pending = _pair_reduce(gw_in[0], gw_in[1], True)
        gw_in = (gw_in[0], h_own)
    (grad_x, vs_in), _ = _in_bwd(d_z, w_in_f, x2, d_x1, g_mix_pre, sc_m)
    recv["w_in"] = []

    gath = [vs_in, vs_up, vs_ffn, loss_acc]
    red = [cs_ffn, vs_mix, dcw, d_wr, d_wi, d_ws.reshape(N_GROUPS * POS_BLOCK, POS_BLOCK), d_bs]
    return dict(grad_x=grad_x, gath=gath, red=red, recv=recv, pending=pending,
                w_in=gw_in, w_out=gw_out, w_up=gw_up, w_down=gw_down)


def kernel(x, c, w_ada, b_ada, g_mix_pre, g_mix_post, w_in, conv_w, conv_b, w_rgate, b_rgate, w_igate, b_igate, lru_a, v_norm_g, v_norm_b, w_spatial, b_spatial, g_lru_out, g_gmlp_out, w_out, g_ffn_pre, g_ffn_post, w_up, ffn_conv_w, ffn_conv_b, w_down, loss_target, m_w_ada, m_b_ada, m_g_mix_pre, m_g_mix_post, m_w_in, m_conv_w, m_conv_b, m_w_rgate, m_b_rgate, m_w_igate, m_b_igate, m_lru_a, m_v_norm_g, m_v_norm_b, m_w_spatial, m_b_spatial, m_g_lru_out, m_g_gmlp_out, m_w_out, m_g_ffn_pre, m_g_ffn_post, m_w_up, m_ffn_conv_w, m_ffn_conv_b, m_w_down, v_w_ada, v_b_ada, v_g_mix_pre, v_g_mix_post, v_w_in, v_conv_w, v_conv_b, v_w_rgate, v_b_rgate, v_w_igate, v_b_igate, v_lru_a, v_v_norm_g, v_v_norm_b, v_w_spatial, v_b_spatial, v_g_lru_out, v_g_gmlp_out, v_w_out, v_g_ffn_pre, v_g_ffn_post, v_w_up, v_ffn_conv_w, v_ffn_conv_b, v_w_down):
    me = _dev_index(_my_pos())
    ada_cols = w_ada.shape[-1]

    big_w = dict(w_in=(w_in, m_w_in, v_w_in, True), w_out=(w_out, m_w_out, v_w_out, False),
                 w_up=(w_up, m_w_up, v_w_up, True), w_down=(w_down, m_w_down, v_w_down, False))

    def gather(*names):
        return _gather_carry([big_w[n][0][0] for n in names], [STACKED if n == "w_up" else big_w[n][3] for n in names])

    def scatter(parts):
        return _scatter_carry([g for _, g, _ in parts], [big_w[n][0].shape[1:] for n, _, _ in parts],
                              [big_w[n][3] for n, _, _ in parts], [rel for _, _, rel in parts])

    (c_all, cw_all, fcw_all, mod_all), (w_in_f, w_out_f) = _prologue(
        jnp.broadcast_to(c, (SUBLANES, D_MODEL)), _pad_rows(conv_w[0]), _pad_rows(ffn_conv_w[0]), w_ada[0], b_ada,
        carry=gather("w_in", "w_out"))
    conv_w_full = _columns_from_devices(cw_all, LRU_CONV_K)
    ffn_cw_full = _columns_from_devices(fcw_all, FFN_CONV_K)
    mod = lax.dynamic_index_in_dim(mod_all.reshape(N_DEV, N_DEV, ada_cols), me, axis=1, keepdims=False)
    mod = mod.reshape(N_MOD, 1, D_MODEL)

    loc = _local_step(x[0], loss_target[0], mod, w_in_f, dict(w_out=w_out_f), conv_w_full, ffn_cw_full,
                      g_mix_pre, g_mix_post, conv_b, w_rgate, b_rgate, w_igate, b_igate, lru_a, v_norm_g, v_norm_b,
                      w_spatial, b_spatial, g_lru_out, g_gmlp_out, g_ffn_pre, g_ffn_post, ffn_conv_b,
                      gather=gather, scatter=scatter)
    grad_x = loc["grad_x"]

    (gathered, reduced), got = _reduce_small(loc["gath"], loc["red"], carry=_chip_scatter_carry(loc["pending"]))
    loc["recv"]["w_in"].append(got[0])

    results = {}
    for name, (w_, m_, v_, cs) in big_w.items():
        results[name] = _adamw_sum(w_, loc[name][1], loc["recv"][name], m_, v_, "adamw_" + name)

    params = dict(
        b_ada=(b_ada, m_b_ada, v_b_ada), g_mix_pre=(g_mix_pre, m_g_mix_pre, v_g_mix_pre),
        g_mix_post=(g_mix_post, m_g_mix_post, v_g_mix_post), conv_b=(conv_b, m_conv_b, v_conv_b),
        w_rgate=(w_rgate, m_w_rgate, v_w_rgate), b_rgate=(b_rgate, m_b_rgate, v_b_rgate),
        w_igate=(w_igate, m_w_igate, v_w_igate), b_igate=(b_igate, m_b_igate, v_b_igate),
        lru_a=(lru_a, m_lru_a, v_lru_a), v_norm_g=(v_norm_g, m_v_norm_g, v_v_norm_g),
        v_norm_b=(v_norm_b, m_v_norm_b, v_v_norm_b), w_spatial=(w_spatial, m_w_spatial, v_w_spatial),
        b_spatial=(b_spatial, m_b_spatial, v_b_spatial), g_lru_out=(g_lru_out, m_g_lru_out, v_g_lru_out),
        g_gmlp_out=(g_gmlp_out, m_g_gmlp_out, v_g_gmlp_out), g_ffn_pre=(g_ffn_pre, m_g_ffn_pre, v_g_ffn_pre),
        g_ffn_post=(g_ffn_post, m_g_ffn_post, v_g_ffn_post), ffn_conv_b=(ffn_conv_b, m_ffn_conv_b, v_ffn_conv_b))
    conv_params = dict(conv_w=(conv_w, m_conv_w, v_conv_w), ffn_conv_w=(ffn_conv_w, m_ffn_conv_w, v_ffn_conv_w))
    small_results, loss = _adamw_small(gathered, reduced, params, conv_params)
    results.update(small_results)
    loss = loss.reshape(())

    results["w_ada"] = _adamw_wada(c_all, gathered[0], gathered[1], gathered[2], w_ada, m_w_ada, v_w_ada)

    order = ["w_ada", "b_ada", "g_mix_pre", "g_mix_post", "w_in", "conv_w", "conv_b", "w_rgate", "b_rgate", "w_igate",
             "b_igate", "lru_a", "v_norm_g", "v_norm_b", "w_spatial", "b_spatial", "g_lru_out", "g_gmlp_out", "w_out",
             "g_ffn_pre", "g_ffn_post", "w_up", "ffn_conv_w", "ffn_conv_b", "w_down"]
    outs = [loss, grad_x[None]]
    for kind in range(4):
        outs += [results[n][kind] for n in order]
    return tuple(outs)
```

```python
import functools

import jax
import jax.numpy as jnp
from jax import lax
from jax.experimental import pallas as pl
from jax.experimental.pallas import tpu as pltpu

F32 = jnp.float32
BF16 = jnp.bfloat16

D_MODEL = 1024
LRU_W = 512
GMLP_W = 512
N_HEADS = 8
HEAD_DIM = 64
N_GROUPS = 4
POS_BLOCK = 128
CHUNK = 64
IN_COLS = 2048
D_FF = 3072
N_MOD = 6
N_DEV = 8
EPS = 1e-6
LRU_C = 8.0
LRU_CONV_K = 4
FFN_CONV_K = 3

ADAM_LR = 0.001
ADAM_B1 = 0.9
ADAM_B2 = 0.999
ADAM_EPS = 1e-08
ADAM_WD = 0.01
ADAM_STEP = 10

LANES = 128
SUBLANES = 8
TT_BIG = 512
TT_MIX = 256
FF_CW = 1024
VMEM_LIMIT = 56 * 1024 * 1024

MESH = pl.DeviceIdType.MESH


def _sds(shape, dtype):
    return jax.ShapeDtypeStruct(shape, dtype)


def _cparams(sem=None):
    return pltpu.CompilerParams(dimension_semantics=sem, vmem_limit_bytes=VMEM_LIMIT)


def _whole():
    return pl.BlockSpec(memory_space=pltpu.VMEM)


def _const(shape):
    nd = len(shape)
    return pl.BlockSpec(shape, lambda *_: (0,) * nd)


def _any():
    return pl.BlockSpec(memory_space=pl.ANY)


class _Carry:
    def __init__(self, inputs, in_specs, out_shape, out_specs, scratch, start, finish):
        self.inputs, self.in_specs, self.out_shape, self.out_specs = inputs, in_specs, out_shape, out_specs
        self.scratch, self.start, self.finish = scratch, start, finish


def _call(body, name, grid, in_specs, out_specs, out_shape, scratch, args, carry=None, body_starts_carry=False):
    n_in, n_out, n_scr = len(in_specs), len(out_specs), len(scratch)
    c_in = len(carry.in_specs) if carry else 0
    c_out = len(carry.out_specs) if carry else 0

    def full_body(*refs):
        ins = refs[:n_in]
        c_ins = refs[n_in:n_in + c_in]
        outs = refs[n_in + c_in:n_in + c_in + n_out]
        c_outs = refs[n_in + c_in + n_out:n_in + c_in + n_out + c_out]
        scr = refs[n_in + c_in + n_out + c_out:n_in + c_in + n_out + c_out + n_scr]
        c_scr = refs[n_in + c_in + n_out + c_out + n_scr:]
        if carry:
            first = functools.reduce(lambda a, b: a & b, [pl.program_id(d) == 0 for d in range(len(grid))])
            last = functools.reduce(lambda a, b: a & b, [pl.program_id(d) == g - 1 for d, g in enumerate(grid)])

        if carry and not body_starts_carry:
            @pl.when(first)
            def _():
                carry.start(c_ins, c_outs, c_scr)

        if body_starts_carry:
            body(*ins, *outs, *scr, start_carry=(lambda: carry.start(c_ins, c_outs, c_scr)) if carry else (lambda: None))
        else:
            body(*ins, *outs, *scr)
        if carry:
            @pl.when(last)
            def _():
                carry.finish(c_ins, c_outs, c_scr)

    res = pl.pallas_call(
        full_body, name=name, grid=grid,
        in_specs=list(in_specs) + (list(carry.in_specs) if carry else []),
        out_specs=list(out_specs) + (list(carry.out_specs) if carry else []),
        out_shape=list(out_shape) + (list(carry.out_shape) if carry else []),
        scratch_shapes=list(scratch) + (list(carry.scratch) if carry else []),
        compiler_params=_cparams(("arbitrary",) * len(grid)),
    )(*args, *(carry.inputs if carry else []))
    return res[:n_out], res[n_out:]


GELU_C0 = 0.7978845608028654
GELU_C1 = GELU_C0 * 0.044715


def _gelu(x):
    t = jnp.tanh(x * (GELU_C0 + GELU_C1 * (x * x)))
    hx = 0.5 * x
    return hx + hx * t


def _gelu_and_grad(x):
    x2 = x * x
    t = jnp.tanh(x * (GELU_C0 + GELU_C1 * x2))
    hx = 0.5 * x
    g = hx + hx * t
    dg = (0.5 + 0.5 * t) + hx * (1.0 - t * t) * (GELU_C0 + 3.0 * GELU_C1 * x2)
    return g, dg


def _sigmoid(x):
    return 1.0 / (1.0 + jnp.exp(-x))


def _softplus(x):
    return jnp.maximum(x, 0.0) + jnp.log1p(jnp.exp(-jnp.abs(x)))


def _neg_expm1(x):
    series = -x * (1.0 + x * (0.5 + x * (1.0 / 6.0 + x * (1.0 / 24.0 + x * (1.0 / 120.0)))))
    return jnp.where(x > -0.1, series, 1.0 - jnp.exp(x))


def _dot(a, b):
    return jnp.dot(a.astype(BF16), b.astype(BF16), preferred_element_type=F32)


def _dot_nt(a, b):
    return lax.dot_general(a.astype(BF16), b.astype(BF16), (((1,), (1,)), ((), ())), preferred_element_type=F32)


def _dot_tn(a, b):
    return lax.dot_general(a.astype(BF16), b.astype(BF16), (((0,), (0,)), ((), ())), preferred_element_type=F32)


def _rows(shape):
    return lax.broadcasted_iota(jnp.int32, shape, 0)


def _shift_down(cur, prev8, s):
    if s == 0:
        return cur
    n = cur.shape[0]
    r = pltpu.roll(cur, s, 0)
    p = pltpu.roll(prev8, s, 0)
    top = jnp.where(_rows(p.shape) < s, p, r[0:SUBLANES])
    if n == SUBLANES:
        return top
    return jnp.concatenate([top, r[SUBLANES:]], axis=0)


def _shift_up(cur, next8, s):
    if s == 0:
        return cur
    n = cur.shape[0]
    r = pltpu.roll(cur, n - s, 0)
    q = pltpu.roll(next8, SUBLANES - s, 0)
    bot = jnp.where(_rows(q.shape) >= SUBLANES - s, q, r[n - SUBLANES:])
    if n == SUBLANES:
        return bot
    return jnp.concatenate([r[:n - SUBLANES], bot], axis=0)


def _scan_fwd(a, b, h_in):
    n = a.shape[0]
    in_group = _rows(a.shape) & (SUBLANES - 1)
    s = 1
    while s < SUBLANES:
        a_s = pltpu.roll(a, s, 0)
        b_s = pltpu.roll(b, s, 0)
        m = in_group >= s
        b = jnp.where(m, a * b_s + b, b)
        a = jnp.where(m, a * a_s, a)
        s *= 2
    out, carry = [], h_in
    for g in range(n // SUBLANES):
        rows = slice(g * SUBLANES, (g + 1) * SUBLANES)
        h_g = a[rows] * carry + b[rows]
        out.append(h_g)
        carry = h_g[SUBLANES - 1:SUBLANES, :]
    return jnp.concatenate(out, axis=0)


def _scan_rev(a, b, l_in):
    n = a.shape[0]
    in_group = _rows(a.shape) & (SUBLANES - 1)
    s = 1
    while s < SUBLANES:
        a_s = pltpu.roll(a, n - s, 0)
        b_s = pltpu.roll(b, n - s, 0)
        m = in_group < SUBLANES - s
        b = jnp.where(m, b + a * b_s, b)
        a = jnp.where(m, a * a_s, a)
        s *= 2
    out, carry = [], l_in
    for g in reversed(range(n // SUBLANES)):
        rows = slice(g * SUBLANES, (g + 1) * SUBLANES)
        l_g = b[rows] + a[rows] * carry
        out.append(l_g)
        carry = l_g[0:1, :]
    return jnp.concatenate(out[::-1], axis=0)


def _rms(x):
    r = lax.rsqrt(jnp.mean(x * x, axis=-1, keepdims=True) + EPS)
    return x * r, r


def _rms_bwd(d_n, n, r):
    return r * (d_n - n * jnp.mean(d_n * n, axis=-1, keepdims=True))


def _colsum(x):
    return jnp.sum(x, axis=0, keepdims=True)


ROW_PIECE = 256


def _row_pieces(tt):
    return [slice(r, r + min(ROW_PIECE, tt)) for r in range(0, tt, min(ROW_PIECE, tt))]


def _lru_gates(xc, wr_ref, wi_ref, br, bi, sp_a):
    r = _sigmoid(_dot(xc, wr_ref[...]) + br)
    i = _sigmoid(_dot(xc, wi_ref[...]) + bi)
    la = -LRU_C * r * sp_a
    a = jnp.exp(la)
    mult = jnp.sqrt(_neg_expm1(2.0 * la))
    return r, i, a, mult


def _lru_conv(lx, prev8, cw_ref, cb):
    xc = cb + cw_ref[LRU_CONV_K - 1:LRU_CONV_K, :] * lx
    taps = []
    for k in range(LRU_CONV_K - 1):
        tap = _shift_down(lx, prev8, LRU_CONV_K - 1 - k)
        taps.append(tap)
        xc = xc + cw_ref[k:k + 1, :] * tap
    return xc, taps


def _ws_mask(transposed=False):
    i = lax.broadcasted_iota(jnp.int32, (POS_BLOCK, POS_BLOCK), 0)
    j = lax.broadcasted_iota(jnp.int32, (POS_BLOCK, POS_BLOCK), 1)
    if transposed:
        i, j = j, i
    return (j // CHUNK) <= (i // CHUNK)


def _gmlp_v(gv, vg, vb):
    av, dav = _gelu_and_grad(gv)
    mu = jnp.mean(av, axis=-1, keepdims=True)
    cen = av - mu
    rs = lax.rsqrt(jnp.mean(cen * cen, axis=-1, keepdims=True) + EPS)
    vhat = cen * rs
    return vhat * vg + vb, vhat, rs, dav


def _mix_fwd(x, sh, sc, g_pre, w_in, conv_w, conv_b, wr_bd, wi_bd, b_r, b_i, lru_a, vn_g, vn_b, w_sp, b_sp_t,
             g_lru, g_gmlp, w_out, g_post, gt_m, g_ffn_pre, sc_f, sh_f, carry=None):
    s_len = x.shape[0]
    tt = min(TT_MIX, s_len)
    nblk = tt // POS_BLOCK

    def body(x_ref, sh_ref, sc_ref, g_ref, w_ref, cw_ref, cb_ref, wr_ref, wi_ref, br_ref, bi_ref, la_ref, vg_ref,
             vb_ref, ws_ref, bst_ref, gl_ref, gg_ref, wo_ref, gp_ref, gtm_ref, g2_ref, scf_ref, shf_ref,
             z_ref, h_ref, y_ref, hl_ref, yo_ref, x1_ref, h2_ref, prev8, hcar):
        i = pl.program_id(0)

        @pl.when(i == 0)
        def _():
            prev8[...] = jnp.zeros_like(prev8)
            hcar[...] = jnp.zeros_like(hcar)

        n_x, _ = _rms(x_ref[...])
        h = (n_x * g_ref[...] * (1.0 + sc_ref[...]) + sh_ref[...]).astype(BF16)
        h_ref[...] = h
        z_ref[...] = jnp.dot(h, w_ref[...], preferred_element_type=F32)

        lx = z_ref[:, 0:LRU_W]
        gate = z_ref[:, LRU_W:2 * LRU_W]
        gu = z_ref[:, 2 * LRU_W:2 * LRU_W + GMLP_W]
        gv = z_ref[:, 2 * LRU_W + GMLP_W:]

        xc, _ = _lru_conv(lx, prev8[...], cw_ref, cb_ref[...])
        prev8[...] = lx[tt - SUBLANES:]
        sp_a = _softplus(-la_ref[...])
        _, ig, a, mult = _lru_gates(xc, wr_ref, wi_ref, br_ref[...], bi_ref[...], sp_a)
        bx = mult * (ig * xc)
        hl = _scan_fwd(a, bx, hcar[0:1, :])
        hcar[...] = jnp.broadcast_to(hl[tt - 1:tt, :], hcar.shape)
        hl_ref[...] = hl
        y_lru = hl * _gelu(gate)
        n_l, _ = _rms(y_lru)
        y_ref[:, 0:LRU_W] = (n_l * gl_ref[...]).astype(BF16)

        u = _gelu(gu)
        v, _, _, _ = _gmlp_v(gv, vg_ref[...], vb_ref[...])
        mask = _ws_mask()
        sp_parts = []
        for nb in range(nblk):
            row = []
            for g in range(N_GROUPS):
                wsm = jnp.where(mask, ws_ref[g], 0.0)
                vblk = v[nb * POS_BLOCK:(nb + 1) * POS_BLOCK, g * LANES:(g + 1) * LANES]
                row.append(_dot(wsm, vblk) + bst_ref[:, g:g + 1])
            sp_parts.append(jnp.concatenate(row, axis=1))
        sp = jnp.concatenate(sp_parts, axis=0) if nblk > 1 else sp_parts[0]
        n_g, _ = _rms(u * sp)
        y_ref[:, LRU_W:] = (n_g * gg_ref[...]).astype(BF16)

        y = jnp.dot(y_ref[...], wo_ref[...], preferred_element_type=F32)
        yo_ref[...] = y
        n_y, _ = _rms(y)
        x1 = x_ref[...] + gtm_ref[...] * (n_y * gp_ref[...])
        x1_ref[...] = x1
        n1, _ = _rms(x1)
        h2_ref[...] = (n1 * g2_ref[...] * (1.0 + scf_ref[...]) + shf_ref[...]).astype(BF16)

    row = lambda c: pl.BlockSpec((tt, c), lambda i: (i, 0))
    v512 = _const((1, LRU_W))
    vec = _const((1, D_MODEL))
    return _call(
        body, "mix_fwd", (s_len // tt,),
        in_specs=[row(D_MODEL), vec, vec, vec, _whole(),
                  _const((LRU_CONV_K, LRU_W)), v512, _whole(), _whole(), v512, v512, v512, v512, v512,
                  _whole(), _whole(), v512, v512, _whole(), vec, vec, vec, vec, vec],
        out_specs=[row(IN_COLS), row(D_MODEL), row(LRU_W + GMLP_W), row(LRU_W), row(D_MODEL), row(D_MODEL),
                   row(D_MODEL)],
        out_shape=[_sds((s_len, IN_COLS), F32), _sds((s_len, D_MODEL), BF16),
                   _sds((s_len, LRU_W + GMLP_W), BF16), _sds((s_len, LRU_W), F32),
                   _sds((s_len, D_MODEL), F32), _sds((s_len, D_MODEL), F32), _sds((s_len, D_MODEL), BF16)],
        scratch=[pltpu.VMEM((SUBLANES, LRU_W), F32), pltpu.VMEM((SUBLANES, LRU_W), F32)],
        args=(x, sh, sc, g_pre, w_in, conv_w, conv_b, wr_bd, wi_bd, b_r, b_i, lru_a, vn_g, vn_b, w_sp, b_sp_t,
              g_lru, g_gmlp, w_out, g_post, gt_m, g_ffn_pre, sc_f, sh_f), carry=carry)


FF_CHUNKS = N_DEV // 2
FF_CHUNK_W = D_FF // FF_CHUNKS


def _ffn_fwd(h2, w_up3, ffn_cw, ffn_cb, carry=None):
    s_len = h2.shape[0]
    tt = min(TT_BIG, s_len)
    nc, cw = FF_CHUNKS, FF_CHUNK_W

    def body(h2_ref, wu_ref, cwg_ref, cwv_ref, cbg_ref, cbv_ref, up_ref, upc_ref, act_ref, prev):
        i = pl.program_id(0)
        c = pl.program_id(1)

        @pl.when(i == 0)
        def _():
            prev[c] = jnp.zeros((2, SUBLANES, cw), F32)

        h2 = h2_ref[...]
        ug_pre = jnp.dot(h2, wu_ref[c], preferred_element_type=F32)
        uv_pre = jnp.dot(h2, wu_ref[nc + c], preferred_element_type=F32)
        up_ref[0] = ug_pre.astype(BF16)
        up_ref[1] = uv_pre.astype(BF16)
        ug, _ = _ffn_conv(ug_pre, prev[c, 0], cwg_ref, cbg_ref[...])
        uv, _ = _ffn_conv(uv_pre, prev[c, 1], cwv_ref, cbv_ref[...])
        prev[c, 0] = ug_pre[tt - SUBLANES:, :]
        prev[c, 1] = uv_pre[tt - SUBLANES:, :]
        upc_ref[0] = ug
        upc_ref[1] = uv
        act_ref[...] = (_gelu(ug) * uv).astype(BF16)

    chunk2 = pl.BlockSpec((2, tt, cw), lambda i, c: (0, i, c))
    ffn_cb2 = ffn_cb.reshape(1, 2 * D_FF)
    return _call(
        body, "ffn_fwd", (s_len // tt, nc),
        in_specs=[pl.BlockSpec((tt, D_MODEL), lambda i, c: (i, 0)), _whole(),
                  pl.BlockSpec((FFN_CONV_K, cw), lambda i, c: (0, c)),
                  pl.BlockSpec((FFN_CONV_K, cw), lambda i, c: (0, c + nc)),
                  pl.BlockSpec((1, cw), lambda i, c: (0, c)),
                  pl.BlockSpec((1, cw), lambda i, c: (0, c + nc))],
        out_specs=[chunk2, chunk2, pl.BlockSpec((tt, cw), lambda i, c: (i, c))],
        out_shape=[_sds((2, s_len, D_FF), BF16), _sds((2, s_len, D_FF), F32), _sds((s_len, D_FF), BF16)],
        scratch=[pltpu.VMEM((nc, 2, SUBLANES, cw), F32)],
        args=(h2, w_up3, ffn_cw, ffn_cw, ffn_cb2, ffn_cb2), carry=carry)


def _ffn_tail(act, w_down, x1, gt_f, g_post, target):
    s_len = x1.shape[0]
    tt = min(TT_BIG, s_len)

    def body(act_ref, wd_ref, x1_ref, gtf_ref, gp_ref, tg_ref, dy2_ref, dout_ref, loss_ref, vs_ref):
        @pl.when(pl.program_id(0) == 0)
        def _():
            loss_ref[...] = jnp.zeros_like(loss_ref)
            vs_ref[...] = jnp.zeros_like(vs_ref)

        for rows in _row_pieces(tt):
            n2, r2 = _rms(jnp.dot(act_ref[rows, :], wd_ref[...], preferred_element_type=F32))
            out = x1_ref[rows, :] + gtf_ref[...] * (n2 * gp_ref[...])
            err = out - tg_ref[rows, :]
            do = err * (1.0 / D_MODEL)
            dout_ref[rows, :] = do
            loss_ref[...] += jnp.broadcast_to(0.5 * jnp.sum(err * err, keepdims=True) * (1.0 / D_MODEL),
                                              loss_ref.shape)
            vs_ref[0:1, :] += _colsum(do * n2 * gp_ref[...])
            vs_ref[1:2, :] += _colsum(do * gtf_ref[...] * n2)
            dy2_ref[rows, :] = _rms_bwd(do * gtf_ref[...] * gp_ref[...], n2, r2).astype(BF16)

    row = lambda c: pl.BlockSpec((tt, c), lambda i: (i, 0))
    vec = _const((1, D_MODEL))
    outs, _ = _call(
        body, "ffn_tail", (s_len // tt,),
        in_specs=[row(D_FF), _whole(), row(D_MODEL), vec, vec, row(D_MODEL)],
        out_specs=[row(D_MODEL), row(D_MODEL), _const((SUBLANES, LANES)), _const((SUBLANES, D_MODEL))],
        out_shape=[_sds((s_len, D_MODEL), BF16), _sds((s_len, D_MODEL), F32), _sds((SUBLANES, LANES), F32),
                   _sds((SUBLANES, D_MODEL), F32)],
        scratch=[], args=(act, w_down, x1, gt_f, g_post, target))
    return outs


def _ffn_conv(up_pre, prev8, cw_ref, cb):
    up = cb + cw_ref[FFN_CONV_K - 1:FFN_CONV_K, :] * up_pre
    taps = []
    for k in range(FFN_CONV_K - 1):
        tap = _shift_down(up_pre, prev8, FFN_CONV_K - 1 - k)
        taps.append(tap)
        up = up + cw_ref[k:k + 1, :] * tap
    return up, taps


def _ffn_bwd(d_y2, up_pre, up, ffn_cw, w_down, carry=None):
    s_len = d_y2.shape[0]
    tt = min(TT_BIG, s_len)
    nt = s_len // tt
    cw = FF_CW
    nc = D_FF // cw

    def body(dy2_ref, up_ref, upc_ref, cwg_ref, cwv_ref, wd_ref, dup_ref, cs_ref, nxt, cs_acc):
        i = pl.program_id(0)
        c = pl.program_id(1)

        @pl.when(i == 0)
        def _():
            nxt[c] = jnp.zeros((2, SUBLANES, cw), F32)
            cs_acc[c] = jnp.zeros((2, SUBLANES, cw), F32)

        pw = 2 * LANES
        for piece in range(cw // pw):
            cols = slice(piece * pw, (piece + 1) * pw)
            d_act = _dot_nt(dy2_ref[...], wd_ref[pl.ds(pl.multiple_of(c * cw + piece * pw, pw), pw), :])
            uv = upc_ref[1, :, cols]
            gl, dgl = _gelu_and_grad(upc_ref[0, :, cols])
            d_ug = d_act * uv * dgl
            d_uv = d_act * gl
            for half, (d_u, cw_ref) in enumerate(((d_ug, cwg_ref), (d_uv, cwv_ref))):
                nx = nxt[c, half, :, cols]
                x_in = up_ref[half, :, cols].astype(F32)
                d_pre = cw_ref[FFN_CONV_K - 1:FFN_CONV_K, cols] * d_u
                sums = [None] * (FFN_CONV_K + 1)
                sums[FFN_CONV_K - 1] = _colsum(d_u * x_in)
                for k in range(FFN_CONV_K - 1):
                    ahead = _shift_up(d_u, nx, FFN_CONV_K - 1 - k)
                    d_pre = d_pre + cw_ref[k:k + 1, cols] * ahead
                    sums[k] = _colsum(ahead * x_in)
                sums[FFN_CONV_K] = _colsum(d_u)
                pad = jnp.zeros((SUBLANES - FFN_CONV_K - 1, pw), F32)
                cs_acc[c, half, :, cols] += jnp.concatenate(sums + [pad], axis=0)
                nxt[c, half, :, cols] = d_u[0:SUBLANES]
                dup_ref[half, :, cols] = d_pre.astype(BF16)

        for cc in range(nc):
            @pl.when((i == nt - 1) & (c == cc))
            def _():
                cs_ref[:, cc * cw:(cc + 1) * cw] = cs_acc[cc, 0]
                cs_ref[:, D_FF + cc * cw:D_FF + (cc + 1) * cw] = cs_acc[cc, 1]

    row = pl.BlockSpec((tt, D_MODEL), lambda i, c: (nt - 1 - i, 0))
    blk = pl.BlockSpec((2, tt, cw), lambda i, c: (0, nt - 1 - i, c))
    return _call(
        body, "ffn_bwd", (nt, nc),
        in_specs=[row, blk, blk,
                  pl.BlockSpec((FFN_CONV_K, cw), lambda i, c: (0, c)),
                  pl.BlockSpec((FFN_CONV_K, cw), lambda i, c: (0, c + nc)),
                  _whole()],
        out_specs=[blk, _const((SUBLANES, 2 * D_FF))],
        out_shape=[_sds((2, s_len, D_FF), BF16), _sds((SUBLANES, 2 * D_FF), F32)],
        scratch=[pltpu.VMEM((nc, 2, SUBLANES, cw), F32), pltpu.VMEM((nc, 2, SUBLANES, cw), F32)],
        args=(d_y2, up_pre, up, ffn_cw, ffn_cw, w_down), carry=carry)


def _up_bwd(d_up, w_up3, x1, dout, y, w_out, g_pre, sc_f, g_post, gt_m, carry=None):
    s_len = x1.shape[0]
    tt = min(TT_BIG, s_len)

    def body(du_ref, wu_ref, x1_ref, do_ref, y_ref, wo_ref, g2_ref, sc_ref, gp_ref, gt_ref,
             dx1_ref, dy_ref, dyc_ref, vs_ref):
        @pl.when(pl.program_id(0) == 0)
        def _():
            vs_ref[...] = jnp.zeros_like(vs_ref)

        for rows in _row_pieces(tt):
            d_h2 = jnp.zeros((rows.stop - rows.start, D_MODEL), F32)
            for half in range(2):
                for ch in range(FF_CHUNKS):
                    d_h2 = d_h2 + _dot_nt(du_ref[half, rows, ch * FF_CHUNK_W:(ch + 1) * FF_CHUNK_W],
                                          wu_ref[half * FF_CHUNKS + ch])
            n1, r1 = _rms(x1_ref[rows, :])
            ng = n1 * g2_ref[...]
            vs_ref[0:1, :] += _colsum(d_h2)
            vs_ref[1:2, :] += _colsum(d_h2 * ng)
            d_ng = d_h2 * (1.0 + sc_ref[...])
            vs_ref[2:3, :] += _colsum(d_ng * n1)
            d_x1 = do_ref[rows, :] + _rms_bwd(d_ng * g2_ref[...], n1, r1)
            dx1_ref[rows, :] = d_x1
            n_y, r_y = _rms(y_ref[rows, :])
            vs_ref[3:4, :] += _colsum(d_x1 * n_y * gp_ref[...])
            d_on = d_x1 * gt_ref[...]
            vs_ref[4:5, :] += _colsum(d_on * n_y)
            d_y = _rms_bwd(d_on * gp_ref[...], n_y, r_y).astype(BF16)
            dy_ref[rows, :] = d_y
            dyc_ref[rows, :] = _dot_nt(d_y, wo_ref[...])

    row = lambda c: pl.BlockSpec((tt, c), lambda i: (i, 0))
    vec = _const((1, D_MODEL))
    return _call(
        body, "up_bwd", (s_len // tt,),
        in_specs=[pl.BlockSpec((2, tt, D_FF), lambda i: (0, i, 0)), _whole(), row(D_MODEL), row(D_MODEL), row(D_MODEL),
                  _whole(), vec, vec, vec, vec],
        out_specs=[row(D_MODEL), row(D_MODEL), row(LRU_W + GMLP_W), _const((SUBLANES, D_MODEL))],
        out_shape=[_sds((s_len, D_MODEL), F32), _sds((s_len, D_MODEL), BF16), _sds((s_len, LRU_W + GMLP_W), F32),
                   _sds((SUBLANES, D_MODEL), F32)],
        scratch=[], args=(d_up, w_up3, x1, dout, y, w_out, g_pre, sc_f, g_post, gt_m), carry=carry)


def _head_pair_block(hd):
    return (slice((hd // 2) * HEAD_DIM, (hd // 2 + 1) * HEAD_DIM), slice((hd % 2) * HEAD_DIM, (hd % 2 + 1) * HEAD_DIM))


def _mix_bwd(d_ycat, z, hl, conv_w, conv_b, wr_bd, wi_bd, b_r, b_i, lru_a, vn_g, vn_b, w_sp, w_sp_t, b_sp_t,
             g_lru, g_gmlp, carry=None):
    s_len = z.shape[0]
    tt = min(TT_MIX, s_len)
    nt = s_len // tt
    nblk = tt // POS_BLOCK
    hb = tt // SUBLANES

    def body(dyc_ref, z_ref, zh_ref, hl_ref, hh_ref, cw_ref, cb_ref, wr_ref, wi_ref, br_ref, bi_ref, la_ref,
             vg_ref, vb_ref, ws_ref, wst_ref, bst_ref, gl_ref, gg_ref,
             dz_ref, vs_ref, dcw_ref, dwrb_ref, dwib_ref, dws_ref, dbs_ref, nxt_dxc, nxt_a, nxt_lam, dwr_ref, dwi_ref):
        i = pl.program_id(0)
        first_tile = i == nt - 1

        @pl.when(i == 0)
        def _():
            for ref in (vs_ref, dcw_ref, dwr_ref, dwi_ref, dws_ref, dbs_ref, nxt_dxc, nxt_a, nxt_lam):
                ref[...] = jnp.zeros_like(ref)

        lx = z_ref[:, 0:LRU_W]
        gate = z_ref[:, LRU_W:2 * LRU_W]
        gu = z_ref[:, 2 * LRU_W:2 * LRU_W + GMLP_W]
        gv = z_ref[:, 2 * LRU_W + GMLP_W:]
        prev8 = jnp.where(first_tile, 0.0, zh_ref[...])
        hprev8 = jnp.where(first_tile, 0.0, hh_ref[...])

        xc, taps = _lru_conv(lx, prev8, cw_ref, cb_ref[...])
        a_par = la_ref[...]
        sp_a = _softplus(-a_par)
        r, ig, a, mult = _lru_gates(xc, wr_ref, wi_ref, br_ref[...], bi_ref[...], sp_a)
        hl = hl_ref[...]
        h_prev = _shift_down(hl, hprev8, 1)
        ggate, dggate = _gelu_and_grad(gate)
        y_lru = hl * ggate
        n_l, r_l = _rms(y_lru)
        d_nl = dyc_ref[:, 0:LRU_W]
        vs_ref[6:7, :] += _colsum(d_nl * n_l)
        d_yl = _rms_bwd(d_nl * gl_ref[...], n_l, r_l)
        d_hl = d_yl * ggate
        d_gate = d_yl * hl * dggate
        a_up = _shift_up(a, nxt_a[...], 1)
        lam = _scan_rev(a_up, d_hl, nxt_lam[0:1, :])
        nxt_a[...] = jnp.broadcast_to(a[0:1, :], nxt_a.shape)
        nxt_lam[...] = jnp.broadcast_to(lam[0:1, :], nxt_lam.shape)
        ixc = ig * xc
        d_la = lam * h_prev * a - lam * ixc * (a * a) / mult
        d_i = lam * mult * xc
        d_xc = lam * mult * ig
        vs_ref[3:4, :] += _colsum(d_la * r) * (LRU_C * _sigmoid(-a_par))
        d_pr = d_la * (-LRU_C * sp_a) * r * (1.0 - r)
        d_pi = d_i * ig * (1.0 - ig)
        vs_ref[1:2, :] += _colsum(d_pr)
        vs_ref[2:3, :] += _colsum(d_pi)
        dwr_ref[...] += _dot_tn(xc, d_pr)
        dwi_ref[...] += _dot_tn(xc, d_pi)
        d_xc = d_xc + _dot_nt(d_pr, wr_ref[...]) + _dot_nt(d_pi, wi_ref[...])
        vs_ref[0:1, :] += _colsum(d_xc)
        nx = nxt_dxc[...]
        d_lx = cw_ref[LRU_CONV_K - 1:LRU_CONV_K, :] * d_xc
        dcw_ref[LRU_CONV_K - 1:LRU_CONV_K, :] += _colsum(d_xc * lx)
        for k in range(LRU_CONV_K - 1):
            d_lx = d_lx + cw_ref[k:k + 1, :] * _shift_up(d_xc, nx, LRU_CONV_K - 1 - k)
            dcw_ref[k:k + 1, :] += _colsum(d_xc * taps[k])
        nxt_dxc[...] = d_xc[0:SUBLANES]
        dz_ref[:, 0:LRU_W] = d_lx.astype(BF16)
        dz_ref[:, LRU_W:2 * LRU_W] = d_gate.astype(BF16)

        u, du = _gelu_and_grad(gu)
        v, vhat, rs, dav = _gmlp_v(gv, vg_ref[...], vb_ref[...])
        mask = _ws_mask()
        sp_parts = []
        for nb in range(nblk):
            rowp = []
            for g in range(N_GROUPS):
                wsm = jnp.where(mask, ws_ref[g], 0.0)
                vblk = v[nb * POS_BLOCK:(nb + 1) * POS_BLOCK, g * LANES:(g + 1) * LANES]
                rowp.append(_dot(wsm, vblk) + bst_ref[:, g:g + 1])
            sp_parts.append(jnp.concatenate(rowp, axis=1))
        sp = jnp.concatenate(sp_parts, axis=0) if nblk > 1 else sp_parts[0]
        y_g = u * sp
        n_g, r_g = _rms(y_g)
        d_ng = dyc_ref[:, LRU_W:]
        vs_ref[7:8, :] += _colsum(d_ng * n_g)
        d_yg = _rms_bwd(d_ng * gg_ref[...], n_g, r_g)
        d_gu = d_yg * sp * du
        d_sp = d_yg * u
        mask_t = _ws_mask(transposed=True)
        ones8 = jnp.ones((SUBLANES, LANES), F32)
        dv_parts = []
        for nb in range(nblk):
            rowp = []
            for g in range(N_GROUPS):
                rs_, cs_ = slice(nb * POS_BLOCK, (nb + 1) * POS_BLOCK), slice(g * LANES, (g + 1) * LANES)
                dsp_blk = d_sp[rs_, cs_]
                dbs_ref[g:g + 1, :] += lax.dot_general(
                    ones8, dsp_blk, (((1,), (1,)), ((), ())), preferred_element_type=F32,
                    precision=lax.Precision.HIGHEST)[0:1, :]
                dws_ref[g] += _dot_nt(dsp_blk, v[rs_, cs_])
                wsm_t = jnp.where(mask_t, wst_ref[g], 0.0)
                rowp.append(_dot(wsm_t, dsp_blk))
            dv_parts.append(jnp.concatenate(rowp, axis=1))
        d_v = jnp.concatenate(dv_parts, axis=0) if nblk > 1 else dv_parts[0]
        vs_ref[4:5, :] += _colsum(d_v * vhat)
        vs_ref[5:6, :] += _colsum(d_v)
        d_vh = d_v * vg_ref[...]
        d_av = rs * (d_vh - jnp.mean(d_vh, axis=-1, keepdims=True)
                     - vhat * jnp.mean(d_vh * vhat, axis=-1, keepdims=True))
        dz_ref[:, 2 * LRU_W:2 * LRU_W + GMLP_W] = d_gu.astype(BF16)
        dz_ref[:, 2 * LRU_W + GMLP_W:] = (d_av * dav).astype(BF16)

        @pl.when(i == nt - 1)
        def _():
            for hd in range(N_HEADS):
                blk = slice(hd * HEAD_DIM, (hd + 1) * HEAD_DIM)
                dwrb_ref[_head_pair_block(hd)] = dwr_ref[blk, blk]
                dwib_ref[_head_pair_block(hd)] = dwi_ref[blk, blk]
            for g in range(N_GROUPS):
                dws_ref[g] = jnp.where(mask, dws_ref[g], 0.0)

    rev = lambda c: pl.BlockSpec((tt, c), lambda i: (nt - 1 - i, 0))
    halo = pl.BlockSpec((SUBLANES, LRU_W), lambda i: (jnp.maximum((nt - 1 - i) * hb - 1, 0), 0))
    v512 = _const((1, LRU_W))
    return _call(
        body, "mix_bwd", (nt,),
        in_specs=[rev(LRU_W + GMLP_W), rev(IN_COLS), halo, rev(LRU_W), halo,
                  _const((LRU_CONV_K, LRU_W)), v512, _whole(), _whole(), v512, v512, v512, v512, v512,
                  _whole(), _whole(), _whole(), v512, v512],
        out_specs=[rev(IN_COLS), _const((SUBLANES, LRU_W)), _const((SUBLANES, LRU_W)),
                   _const((LRU_W // 2, 2 * HEAD_DIM)), _const((LRU_W // 2, 2 * HEAD_DIM)),
                   _const((N_GROUPS, POS_BLOCK, POS_BLOCK)), _const((SUBLANES, POS_BLOCK))],
        out_shape=[_sds((s_len, IN_COLS), BF16), _sds((SUBLANES, LRU_W), F32), _sds((SUBLANES, LRU_W), F32),
                   _sds((LRU_W // 2, 2 * HEAD_DIM), F32), _sds((LRU_W // 2, 2 * HEAD_DIM), F32),
                   _sds((N_GROUPS, POS_BLOCK, POS_BLOCK), F32), _sds((SUBLANES, POS_BLOCK), F32)],
        scratch=[pltpu.VMEM((SUBLANES, LRU_W), F32), pltpu.VMEM((SUBLANES, LRU_W), F32),
                 pltpu.VMEM((SUBLANES, LRU_W), F32), pltpu.VMEM((LRU_W, LRU_W), F32), pltpu.VMEM((LRU_W, LRU_W), F32)],
        args=(d_ycat, z, z, hl, hl, conv_w, conv_b, wr_bd, wi_bd, b_r, b_i, lru_a, vn_g, vn_b, w_sp, w_sp_t, b_sp_t,
              g_lru, g_gmlp), carry=carry)


def _in_bwd(d_z, w_in, x, d_x1, g, sc, carry=None):
    s_len = x.shape[0]
    tt = min(TT_BIG, s_len)

    def body(dz_ref, w_ref, x_ref, dx1_ref, g_ref, sc_ref, gx_ref, vs_ref):
        @pl.when(pl.program_id(0) == 0)
        def _():
            vs_ref[...] = jnp.zeros_like(vs_ref)

        for rows in _row_pieces(tt):
            d_h = _dot_nt(dz_ref[rows, :], w_ref[...])
            n, r = _rms(x_ref[rows, :])
            vs_ref[0:1, :] += _colsum(d_h)
            vs_ref[1:2, :] += _colsum(d_h * n * g_ref[...])
            d_ng = d_h * (1.0 + sc_ref[...])
            vs_ref[2:3, :] += _colsum(d_ng * n)
            gx_ref[rows, :] = dx1_ref[rows, :] + _rms_bwd(d_ng * g_ref[...], n, r)

    row = lambda c: pl.BlockSpec((tt, c), lambda i: (i, 0))
    vec = _const((1, D_MODEL))
    return _call(
        body, "in_bwd", (s_len // tt,),
        in_specs=[row(IN_COLS), _whole(), row(D_MODEL), row(D_MODEL), vec, vec],
        out_specs=[row(D_MODEL), _const((SUBLANES, D_MODEL))],
        out_shape=[_sds((s_len, D_MODEL), F32), _sds((SUBLANES, D_MODEL), F32)],
        scratch=[], args=(d_z, w_in, x, d_x1, g, sc), carry=carry)


def _wgrad(a, b, name, by_rows=False, carry=None):
    s_len, k_dim = a.shape
    halves = b.ndim == 3
    n_dim = b.shape[-1] * (2 if halves else 1)

    def body(a_ref, b_ref, ob_ref, own_ref):
        out = _dot_tn(a_ref[...], b_ref[0] if halves else b_ref[...])
        ob_ref[...] = out.astype(BF16)

        @pl.when(pl.program_id(0) == _dev_index(_my_pos()))
        def _():
            own_ref[...] = out

    if by_rows:
        tile = k_dim // N_DEV
        a_spec = pl.BlockSpec((s_len, tile), lambda j: (0, j))
        b_spec = pl.BlockSpec((s_len, n_dim), lambda j: (0, 0))
        o_spec = pl.BlockSpec((tile, n_dim), lambda j: (j, 0))
        own_shape = (tile, n_dim)
    else:
        tile = n_dim // N_DEV
        a_spec = pl.BlockSpec((s_len, k_dim), lambda j: (0, 0))
        if halves:
            per_half = N_DEV // 2
            b_spec = pl.BlockSpec((1, s_len, tile), lambda j: (j // per_half, 0, j % per_half))
        else:
            b_spec = pl.BlockSpec((s_len, tile), lambda j: (0, j))
        o_spec = pl.BlockSpec((k_dim, tile), lambda j: (0, j))
        own_shape = (k_dim, tile)
    return _call(
        body, name, (N_DEV,), in_specs=[a_spec, b_spec], out_specs=[o_spec, _const(own_shape)],
        out_shape=[_sds((k_dim, n_dim), BF16), _sds(own_shape, F32)],
        scratch=[], args=(a, b), carry=carry)


def _adam_math(w, g, m, v):
    m = ADAM_B1 * m + (1.0 - ADAM_B1) * g
    v = ADAM_B2 * v + (1.0 - ADAM_B2) * (g * g)
    m_hat = m / (1.0 - ADAM_B1 ** ADAM_STEP)
    v_hat = v / (1.0 - ADAM_B2 ** ADAM_STEP)
    delta = -ADAM_LR * (m_hat / (jnp.sqrt(v_hat) + ADAM_EPS) + ADAM_WD * w)
    return delta, m, v


def _row_tile(rows, cols, n_f32_arrays):
    budget = VMEM_LIMIT // 2
    tr = rows
    while tr % 2 == 0 and tr // 2 >= SUBLANES and (tr // 2) % SUBLANES == 0 and tr * cols * 4 * n_f32_arrays * 2 > budget:
        tr //= 2
    return tr


def _adamw_sum(w, g_own, recv, m, v, name):
    _, rows, cols = w.shape
    n_recv = len(recv)
    tr = _row_tile(rows, cols, 10)
    nb = rows // tr

    def body(w_ref, g_ref, *rest):
        r_refs = rest[:n_recv]
        m_ref, v_ref, go_ref, d_ref, mo_ref, vo_ref = rest[n_recv:]
        g = g_ref[...]
        for r_ref in r_refs:
            for k in range(r_ref.shape[0]):
                g = g + r_ref[k].astype(F32)
        go_ref[0] = g
        d_ref[0], mo_ref[0], vo_ref[0] = _adam_math(w_ref[0], g, m_ref[0], v_ref[0])

    blk = pl.BlockSpec((1, tr, cols), lambda i: (0, i, 0))
    return pl.pallas_call(
        body, name=name, grid=(nb,),
        in_specs=[blk, pl.BlockSpec((tr, cols), lambda i: (i, 0))]
        + [pl.BlockSpec((r.shape[0], tr, cols), lambda i: (0, i, 0)) for r in recv] + [blk, blk],
        out_specs=[blk] * 4, out_shape=[_sds((1, rows, cols), F32)] * 4,
        compiler_params=_cparams(("arbitrary",)),
    )(w, g_own, *recv, m, v)


def _row_of_each(ref, row):
    cols = ref.shape[1]
    rows = _rows((N_DEV, cols))
    out = jnp.zeros((N_DEV, cols), F32)
    for d in range(N_DEV):
        picked = ref[d * SUBLANES + row:d * SUBLANES + row + 1, :]
        out = jnp.where(rows == d, jnp.broadcast_to(picked, (N_DEV, cols)), out)
    return out


def _my_columns(full, width, me):
    out = jnp.zeros(full.shape[:-1] + (width,), F32)
    for d in range(N_DEV):
        out = out + jnp.where(me == d, full[:, d * width:(d + 1) * width], 0.0)
    return out


def _adamw_wada(c_all, vs_in_all, vs_up_all, vs_ffn_all, w, m, v):
    _, rows, cols = w.shape

    def body(c_ref, vi_ref, vu_ref, vf_ref, w_ref, m_ref, v_ref, go_ref, d_ref, mo_ref, vo_ref):
        me = _dev_index(_my_pos())
        cv = _row_of_each(c_ref, 0)
        ca = cv * _sigmoid(cv)
        dmod = jnp.concatenate([_row_of_each(vi_ref, 0), _row_of_each(vi_ref, 1), _row_of_each(vu_ref, 3),
                                _row_of_each(vu_ref, 0), _row_of_each(vu_ref, 1), _row_of_each(vf_ref, 0)], axis=1)
        dm = _my_columns(dmod, cols, me)
        g = lax.dot_general(ca, dm, (((0,), (0,)), ((), ())), preferred_element_type=F32,
                            precision=lax.Precision.HIGHEST)
        go_ref[0] = g
        d_ref[0], mo_ref[0], vo_ref[0] = _adam_math(w_ref[0], g, m_ref[0], v_ref[0])

    return pl.pallas_call(
        body, name="adamw_w_ada", out_shape=[_sds((1, rows, cols), F32)] * 4,
        in_specs=[_whole()] * 7, out_specs=[_whole()] * 4,
        compiler_params=_cparams(),
    )(c_all, vs_in_all, vs_up_all, vs_ffn_all, w, m, v)


def _adamw_small(gathered, reduced, params, conv_params):
    names = list(params) + list(conv_params)
    allp = {**params, **conv_params}
    n_g = len(gathered) + len(reduced)

    def body(*refs):
        g_refs = refs[:n_g]
        p_refs = refs[n_g:n_g + 3 * len(names)]
        o_refs = refs[n_g + 3 * len(names):]
        me = _dev_index(_my_pos())

        def total(ref):
            s = ref[0:SUBLANES, :]
            for d in range(1, N_DEV):
                s = s + ref[d * SUBLANES:(d + 1) * SUBLANES, :]
            return s

        vs_in, vs_up, vs_ffn, loss = [total(r) for r in g_refs[:4]]
        cs, vs_mix, dcw, dwr, dwi, dws, dbs = [r[...] for r in g_refs[4:]]
        o_refs[-1][...] = loss[0:1, 0:1]
        mine = lambda full, width: _my_columns(full, width, me)

        all_ = (slice(None), slice(None))
        heads = lambda row: [((0, slice(h, h + 1), slice(None)), row[:, h * HEAD_DIM:(h + 1) * HEAD_DIM])
                             for h in range(N_HEADS)]
        blocks = lambda pairs: [((0, h), pairs[_head_pair_block(h)]) for h in range(N_HEADS)]
        pieces = {
            "b_ada": [((slice(None), slice(k * D_MODEL, (k + 1) * D_MODEL)), row) for k, row in enumerate(
                (vs_in[0:1], vs_in[1:2], vs_up[3:4], vs_up[0:1], vs_up[1:2], vs_ffn[0:1]))],
            "g_mix_pre": [(all_, vs_in[2:3])], "g_mix_post": [(all_, vs_up[4:5])],
            "g_ffn_pre": [(all_, vs_up[2:3])], "g_ffn_post": [(all_, vs_ffn[1:2])],
            "conv_b": [(all_, vs_mix[0:1])], "b_rgate": heads(vs_mix[1:2]), "b_igate": heads(vs_mix[2:3]),
            "lru_a": [(all_, vs_mix[3:4])], "v_norm_g": [(all_, vs_mix[4:5])], "v_norm_b": [(all_, vs_mix[5:6])],
            "g_lru_out": [(all_, vs_mix[6:7])], "g_gmlp_out": [(all_, vs_mix[7:8])],
            "w_rgate": blocks(dwr), "w_igate": blocks(dwi),
            "w_spatial": [((0, g), dws[g * POS_BLOCK:(g + 1) * POS_BLOCK, :]) for g in range(N_GROUPS)],
            "b_spatial": [((0,), dbs[0:N_GROUPS])],
            "ffn_conv_b": [(all_, cs[FFN_CONV_K:FFN_CONV_K + 1])],
            "conv_w": [((0,), mine(dcw[0:LRU_CONV_K], LRU_W // N_DEV))],
            "ffn_conv_w": [((0,), mine(cs[0:FFN_CONV_K], 2 * D_FF // N_DEV))],
        }
        for n_i, name in enumerate(names):
            w_ref, m_ref, v_ref = p_refs[3 * n_i:3 * n_i + 3]
            go_ref, d_ref, mo_ref, vo_ref = o_refs[4 * n_i:4 * n_i + 4]
            for idx, g in pieces[name]:
                go_ref[idx] = g
                d_ref[idx], mo_ref[idx], vo_ref[idx] = _adam_math(w_ref[idx], g, m_ref[idx], v_ref[idx])

    flat_params = [a for n in names for a in allp[n]]
    out_shape = [_sds(allp[n][0].shape, F32) for n in names for _ in range(4)] + [_sds((1, 1), F32)]
    outs = pl.pallas_call(
        body, name="adamw_small", out_shape=out_shape,
        in_specs=[_whole()] * (n_g + len(flat_params)), out_specs=[_whole()] * len(out_shape),
        compiler_params=_cparams(),
    )(*gathered, *reduced, *flat_params)
    return {n: outs[4 * i:4 * i + 4] for i, n in enumerate(names)}, outs[-1]


def _my_pos():
    return lax.axis_index("x"), lax.axis_index("y"), lax.axis_index("c")


def _flip(pos, k):
    x, y, c = pos
    return (1 - x if k & 4 else x, 1 - y if k & 2 else y, 1 - c if k & 1 else c)


def _dev_index(pos):
    x, y, c = pos
    return 4 * x + 2 * y + c


def _all_gather_small(ins, outs, send_sems, recv_sems):
    n = len(ins)
    me = _my_pos()

    def slot(a, pos):
        rows = ins[a].shape[0]
        return outs[a].at[pl.ds(pl.multiple_of(_dev_index(pos) * rows, SUBLANES), rows), :]

    def copy(a, k, block):
        return pltpu.make_async_remote_copy(
            src_ref=ins[a], dst_ref=slot(a, block), send_sem=send_sems.at[a, k - 1], recv_sem=recv_sems.at[a, k - 1],
            device_id=_flip(me, k), device_id_type=MESH)

    sends = [copy(a, k, me) for a in range(n) for k in range(1, N_DEV)]
    for cp in sends:
        cp.start()
    for a in range(n):
        rows = ins[a].shape[0]
        outs[a][pl.ds(pl.multiple_of(_dev_index(me) * rows, SUBLANES), rows), :] = ins[a][...]
    for a in range(n):
        for k in range(1, N_DEV):
            copy(a, k, _flip(me, k)).wait_recv()
    for cp in sends:
        cp.wait_send()


def _prologue(c8, cw8, fcw8, w_ada, b_ada, carry):
    cols = w_ada.shape[1]

    def body(c_ref, cw_ref, fcw_ref, w_ref, b_ref, call_ref, cwall_ref, fcwall_ref, modall_ref, mod_scr,
             s1, r1, s2, r2, start_carry):
        _all_gather_small([c_ref, cw_ref, fcw_ref], [call_ref, cwall_ref, fcwall_ref], s1, r1)
        start_carry()
        cv = _row_of_each(call_ref, 0)
        ca = cv * _sigmoid(cv)
        b_cols = _my_columns(b_ref[...], cols, _dev_index(_my_pos()))
        mod_scr[...] = jnp.dot(ca, w_ref[...], preferred_element_type=F32, precision=lax.Precision.HIGHEST) + b_cols
        _all_gather_small([mod_scr], [modall_ref], s2, r2)

    sem = lambda n: pltpu.SemaphoreType.DMA((n, N_DEV - 1))
    return _call(
        body, "prologue", (1,), in_specs=[_whole()] * 5, out_specs=[_whole()] * 4,
        out_shape=[_sds((N_DEV * SUBLANES, a.shape[1]), F32) for a in (c8, cw8, fcw8)]
        + [_sds((N_DEV * N_DEV, cols), F32)],
        scratch=[pltpu.VMEM((N_DEV, cols), F32), sem(3), sem(3), sem(1), sem(1)],
        args=(c8, cw8, fcw8, w_ada, b_ada), carry=carry, body_starts_carry=True)


def _reduce_small(gath, red, carry=None):
    n_g, n_r = len(gath), len(red)
    chip_flips = CHIP_FLIPS

    def body(*refs, start_carry):
        g_in, r_in = refs[:n_g], refs[n_g:n_g + n_r]
        g_out, r_out = refs[n_g + n_r:2 * n_g + n_r], refs[2 * n_g + n_r:2 * (n_g + n_r)]
        scr = refs[2 * (n_g + n_r):]
        sib, land = scr[:n_r], scr[n_r:2 * n_r]
        g_send, g_recv, s_send, s_recv, i_send, i_recv, f_send, f_recv = scr[2 * n_r:]
        me = _my_pos()
        c = me[2]
        sibling = _flip(me, 1)

        def slot(a, pos):
            return g_out[a].at[pl.ds(pl.multiple_of(_dev_index(pos) * SUBLANES, SUBLANES), SUBLANES), :]

        def gcopy(a, k):
            return pltpu.make_async_remote_copy(
                src_ref=g_in[a], dst_ref=slot(a, me), send_sem=g_send.at[a, k - 1], recv_sem=g_recv.at[a, k - 1],
                device_id=_flip(me, k), device_id_type=MESH)

        def scopy(a):
            return pltpu.make_async_remote_copy(
                src_ref=r_in[a], dst_ref=sib[a], send_sem=s_send.at[a], recv_sem=s_recv.at[a],
                device_id=sibling, device_id_type=MESH)

        def icopy(a, j):
            return pltpu.make_async_remote_copy(
                src_ref=r_out[a], dst_ref=land[a].at[j], send_sem=i_send.at[a, j], recv_sem=i_recv.at[a, j],
                device_id=_flip(me, chip_flips[j]), device_id_type=MESH)

        def fcopy(a, j):
            return pltpu.make_async_remote_copy(
                src_ref=land[a].at[j], dst_ref=land[a].at[j], send_sem=f_send.at[a, j], recv_sem=f_recv.at[a, j],
                device_id=sibling, device_id_type=MESH)

        gathers = [gcopy(a, k) for a in range(n_g) for k in range(1, N_DEV)]
        swaps = [scopy(a) for a in range(n_r)]
        for cp in gathers + swaps:
            cp.start()
        for a in range(n_g):
            g_out[a][pl.ds(pl.multiple_of(_dev_index(me) * SUBLANES, SUBLANES), SUBLANES), :] = g_in[a][...]
        for a in range(n_r):
            swaps[a].wait_recv()
            r_out[a][...] = r_in[a][...] + sib[a][...]

        for core in range(2):
            @pl.when(c == core)
            def _():
                for a in range(core, n_r, 2):
                    for j in range(3):
                        icopy(a, j).start()

        start_carry()

        for core in range(2):
            mine = [a for a in range(n_r) if a % 2 == core]
            theirs = [a for a in range(n_r) if a % 2 != core]

            @pl.when(c == core)
            def _():
                out = [icopy(a, j) for a in mine for j in range(3)]
                fwd = []
                for a in mine:
                    for j in range(3):
                        icopy(a, j).wait_recv()
                        cp = fcopy(a, j)
                        cp.start()
                        fwd.append(cp)
                for a in theirs:
                    for j in range(3):
                        fcopy(a, j).wait_recv()
                for cp in out + fwd:
                    cp.wait_send()

        for a in range(n_r):
            r_out[a][...] = (r_out[a][...] + land[a][1]) + (land[a][0] + land[a][2])
        for a in range(n_g):
            for k in range(1, N_DEV):
                pltpu.make_async_remote_copy(
                    src_ref=g_in[a], dst_ref=slot(a, _flip(me, k)), send_sem=g_send.at[a, k - 1],
                    recv_sem=g_recv.at[a, k - 1], device_id=_flip(me, k), device_id_type=MESH).wait_recv()
        for cp in gathers + swaps:
            cp.wait_send()

    shapes = [tuple(a.shape) for a in red]
    outs, carried = _call(
        body, "reduce_small", (1,), in_specs=[_whole()] * (n_g + n_r), out_specs=[_whole()] * (n_g + n_r),
        out_shape=[_sds((N_DEV * SUBLANES, a.shape[1]), F32) for a in gath] + [_sds(s, F32) for s in shapes],
        scratch=[pltpu.VMEM(s, F32) for s in shapes] + [pltpu.VMEM((3,) + s, F32) for s in shapes]
        + [pltpu.SemaphoreType.DMA((n_g, N_DEV - 1)), pltpu.SemaphoreType.DMA((n_g, N_DEV - 1)),
           pltpu.SemaphoreType.DMA((n_r,)), pltpu.SemaphoreType.DMA((n_r,)),
           pltpu.SemaphoreType.DMA((n_r, 3)), pltpu.SemaphoreType.DMA((n_r, 3)),
           pltpu.SemaphoreType.DMA((n_r, 3)), pltpu.SemaphoreType.DMA((n_r, 3))],
        args=tuple(gath) + tuple(red), carry=carry, body_starts_carry=True)
    return (outs[:n_g], outs[n_g:]), carried


STACKED = "stacked"


def _region(ref, shard_shape, col_sharded, pos):
    r, cdim = shard_shape
    d = _dev_index(pos)
    if col_sharded == STACKED:
        return ref.at[d]
    if col_sharded:
        return ref.at[:, pl.ds(pl.multiple_of(d * cdim, LANES), cdim)]
    return ref.at[pl.ds(pl.multiple_of(d * r, 2 * SUBLANES), r), :]


def _gather_carry(shards, col_sharded):
    n_w = len(shards)
    shapes = [tuple(s.shape) for s in shards]
    full_shapes = [(N_DEV,) + s if cs == STACKED else (s[0], s[1] * N_DEV) if cs else (s[0] * N_DEV, s[1])
                   for s, cs in zip(shapes, col_sharded)]

    def tools(out_refs, scr):
        send_sems, recv_sems = scr[n_w], scr[n_w + 1]
        me = _my_pos()
        x, y, c = me
        sibling = (x, y, 1 - c)
        chips = [(1 - x, y), (x, 1 - y), (1 - x, 1 - y)]

        def region(w, pos):
            return _region(out_refs[w], shapes[w], col_sharded[w], pos)

        def copy(w, k, block, to, src=None):
            return pltpu.make_async_remote_copy(
                src_ref=region(w, block) if src is None else src, dst_ref=region(w, block),
                send_sem=send_sems.at[w, k], recv_sem=recv_sems.at[w, k], device_id=to, device_id_type=MESH)

        def first(w):
            return [copy(w, 0, me, sibling, src=scr[w])] + [
                copy(w, 1 + j, me, (*chip, c), src=scr[w]) for j, chip in enumerate(chips)]

        def mine(w):
            return pltpu.make_async_copy(scr[w], region(w, me), scr[n_w + 2].at[w])

        return me, c, sibling, chips, copy, first, mine

    def start(ins, outs, scr):
        _, _, _, _, _, first, mine = tools(outs, scr)
        for w in range(n_w):
            scr[w][...] = ins[w][...].astype(BF16)
            for cp in first(w) + [mine(w)]:
                cp.start()

    def finish(ins, outs, scr):
        me, c, sibling, chips, copy, first, mine = tools(outs, scr)
        passed = []
        for w in range(n_w):
            for j, chip in enumerate(chips):
                copy(w, 1 + j, (*chip, c), me).wait_recv()
                fwd = copy(w, 4 + j, (*chip, c), sibling)
                fwd.start()
                passed.append(fwd)
        for w in range(n_w):
            copy(w, 0, sibling, me).wait_recv()
            for j, chip in enumerate(chips):
                copy(w, 4 + j, (*chip, 1 - c), me).wait_recv()
        for w in range(n_w):
            for cp in first(w):
                cp.wait_send()
            mine(w).wait()
        for cp in passed:
            cp.wait_send()

    return _Carry(
        inputs=list(shards), in_specs=[_whole()] * n_w,
        out_shape=[_sds(s, BF16) for s in full_shapes], out_specs=[_any()] * n_w,
        scratch=[pltpu.VMEM(s, BF16) for s in shapes]
        + [pltpu.SemaphoreType.DMA((n_w, N_DEV - 1)), pltpu.SemaphoreType.DMA((n_w, N_DEV - 1)),
           pltpu.SemaphoreType.DMA((n_w,))],
        start=start, finish=finish)


CHIP_FLIPS = (4, 2, 6)


def _pair_reduce(g_bf, g_own, col_sharded):
    shape = tuple(g_own.shape)
    n = len(CHIP_FLIPS)

    def body(g_ref, own_ref, hown_ref, hout_ref, mine, sib, send_sems, recv_sems, local_sems):
        me = _my_pos()
        sibling = _flip(me, 1)
        flips = (0,) + CHIP_FLIPS

        def region(pos):
            return _region(g_ref, shape, col_sharded, pos)

        local = [pltpu.make_async_copy(region(_flip(me, f)), mine.at[s], local_sems.at[s])
                 for s, f in enumerate(CHIP_FLIPS)]
        sends = [pltpu.make_async_remote_copy(
            src_ref=region(_flip(sibling, f)), dst_ref=sib.at[s], send_sem=send_sems.at[s], recv_sem=recv_sems.at[s],
            device_id=sibling, device_id_type=MESH) for s, f in enumerate(flips)]
        for cp in local + sends:
            cp.start()
        for cp in local:
            cp.wait()
        for cp in sends:
            cp.wait_recv()
        hown_ref[...] = own_ref[...] + sib[0].astype(F32)
        for s in range(n):
            hout_ref[s] = (mine[s].astype(F32) + sib[s + 1].astype(F32)).astype(BF16)
        for cp in sends:
            cp.wait_send()

    return pl.pallas_call(
        body, name="pair_reduce", out_shape=[_sds(shape, F32), _sds((n,) + shape, BF16)],
        in_specs=[_any(), _whole()], out_specs=[_whole(), _whole()],
        scratch_shapes=[pltpu.VMEM((n,) + shape, BF16), pltpu.VMEM((n + 1,) + shape, BF16),
                        pltpu.SemaphoreType.DMA((n + 1,)), pltpu.SemaphoreType.DMA((n + 1,)),
                        pltpu.SemaphoreType.DMA((n,))],
        compiler_params=pltpu.CompilerParams(vmem_limit_bytes=VMEM_LIMIT),
    )(g_bf, g_own)


def _chip_scatter_carry(h_out):
    n = len(CHIP_FLIPS)

    def copies(ins, outs, scr):
        send_sems, recv_sems = scr
        me = _my_pos()
        return [pltpu.make_async_remote_copy(
            src_ref=ins[0].at[j], dst_ref=outs[0].at[j], send_sem=send_sems.at[j], recv_sem=recv_sems.at[j],
            device_id=_flip(me, CHIP_FLIPS[j]), device_id_type=MESH) for j in range(n)]

    def start(ins, outs, scr):
        for cp in copies(ins, outs, scr):
            cp.start()

    def finish(ins, outs, scr):
        cps = copies(ins, outs, scr)
        for cp in cps:
            cp.wait_recv()
        for cp in cps:
            cp.wait_send()

    return _Carry(inputs=[h_out], in_specs=[_any()], out_shape=[_sds(tuple(h_out.shape), BF16)], out_specs=[_any()],
                  scratch=[pltpu.SemaphoreType.DMA((n,)), pltpu.SemaphoreType.DMA((n,))], start=start, finish=finish)


def _scatter_carry(grads_bf, shard_shapes, col_sharded, relations):
    n_w = len(grads_bf)
    shapes = [tuple(s) for s in shard_shapes]

    def copies(ins, outs, scr):
        send_sems, recv_sems = scr
        me = _my_pos()
        out = []
        for w in range(n_w):
            for i, k in enumerate(relations[w]):
                peer = _flip(me, k)
                out.append(pltpu.make_async_remote_copy(
                    src_ref=_region(ins[w], shapes[w], col_sharded[w], peer), dst_ref=outs[w].at[i],
                    send_sem=send_sems.at[w, i], recv_sem=recv_sems.at[w, i],
                    device_id=peer, device_id_type=MESH))
        return out

    def start(ins, outs, scr):
        for cp in copies(ins, outs, scr):
            cp.start()

    def finish(ins, outs, scr):
        cps = copies(ins, outs, scr)
        for cp in cps:
            cp.wait_recv()
        for cp in cps:
            cp.wait_send()

    return _Carry(
        inputs=list(grads_bf), in_specs=[_any()] * n_w,
        out_shape=[_sds((len(r),) + s, BF16) for r, s in zip(relations, shapes)], out_specs=[_any()] * n_w,
        scratch=[pltpu.SemaphoreType.DMA((n_w, N_DEV - 1)), pltpu.SemaphoreType.DMA((n_w, N_DEV - 1))],
        start=start, finish=finish)


def _block_diag(w):
    eye = jnp.eye(N_HEADS, dtype=w.dtype)
    return (eye[:, None, :, None] * w[:, :, None, :]).reshape(N_HEADS * HEAD_DIM, N_HEADS * HEAD_DIM)


def _pad_rows(a):
    return jnp.pad(a, ((0, SUBLANES - a.shape[0]), (0, 0)))


def _columns_from_devices(gathered, rows):
    w = gathered.shape[1]
    return gathered.reshape(N_DEV, SUBLANES, w)[:, :rows].transpose(1, 0, 2).reshape(rows, N_DEV * w)


def _local_step(x2, target, mod, w_in_f, w_full, conv_w_full, ffn_cw_full,
                g_mix_pre, g_mix_post, conv_b, w_rgate, b_rgate, w_igate, b_igate, lru_a, v_norm_g, v_norm_b,
                w_spatial, b_spatial, g_lru_out, g_gmlp_out, g_ffn_pre, g_ffn_post, ffn_conv_b,
                gather=None, scatter=None):
    sh_m, sc_m, gt_m, sh_f, sc_f, gt_f = [mod[k] for k in range(N_MOD)]
    wr_bd = _block_diag(w_rgate[0]).astype(BF16)
    wi_bd = _block_diag(w_igate[0]).astype(BF16)
    b_r = b_rgate.reshape(1, LRU_W)
    b_i = b_igate.reshape(1, LRU_W)
    b_sp_t = b_spatial[0].T
    w_sp_t = jnp.swapaxes(w_spatial[0], 1, 2)

    def arriving(*names):
        return gather(*names) if gather else None

    near, far = (1, 2, 3, 4, 5), (6, 7)

    def leaving(*parts):
        return scatter(parts) if scatter else None

    def received(recv, parts, outs):
        for (name, _, _), out in zip(parts, outs):
            recv.setdefault(name, []).append(out)

    mix_params = (conv_w_full, conv_b, wr_bd, wi_bd, b_r, b_i, lru_a, v_norm_g, v_norm_b)
    w_out_f = w_full["w_out"]
    (z, h, ycat, hl, y, x1, h2), got = _mix_fwd(
        x2, sh_m, sc_m, g_mix_pre, w_in_f, *mix_params, w_spatial[0], b_sp_t, g_lru_out, g_gmlp_out,
        w_out_f, g_mix_post, gt_m, g_ffn_pre, sc_f, sh_f, carry=arriving("w_up"))
    w_up_f = got[0] if gather else w_full["w_up"]
    (up_pre, up, act), got = _ffn_fwd(h2, w_up_f, ffn_cw_full, ffn_conv_b, carry=arriving("w_down"))
    w_down_f = got[0] if gather else w_full["w_down"]
    d_y2, dout, loss_acc, vs_ffn = _ffn_tail(act, w_down_f, x1, gt_f, g_ffn_post, target)

    recv = {}
    gw_down, _ = _wgrad(act, d_y2, "wgrad_down", by_rows=True)
    parts = [("w_down", gw_down[0], near + far)]
    (d_up, cs_ffn), got = _ffn_bwd(d_y2, up_pre, up, ffn_cw_full, w_down_f, carry=leaving(*parts))
    received(recv, parts, got)
    gw_up, _ = _wgrad(h2, d_up, "wgrad_up")
    parts = [("w_up", gw_up[0], near)]
    (d_x1, d_y, d_ycat, vs_up), got = _up_bwd(
        d_up, w_up_f, x1, dout, y, w_out_f, g_ffn_pre, sc_f, g_mix_post, gt_m, carry=leaving(*parts))
    received(recv, parts, got)
    gw_out, _ = _wgrad(ycat, d_y, "wgrad_out", by_rows=True)
    parts = [("w_up", gw_up[0], far), ("w_out", gw_out[0], near + far)]
    (d_z, vs_mix, dcw, d_wr, d_wi, d_ws, d_bs), got = _mix_bwd(
        d_ycat, z, hl, *mix_params, w_spatial[0], w_sp_t, b_sp_t, g_lru_out, g_gmlp_out, carry=leaving(*parts))
    received(recv, parts, got)
    gw_in, _ = _wgrad(h, d_z, "wgrad_in")
    pending = None
    if scatter:
        h_own, pending = _pair_reduce(gw_in[0], gw_in[1], True)
        gw_in = (gw_in[0], h_own)
    (grad_x, vs_in), _ = _in_bwd(d_z, w_in_f, x2, d_x1, g_mix_pre, sc_m)
    recv["w_in"] = []

    gath = [vs_in, vs_up, vs_ffn, loss_acc]
    red = [cs_ffn, vs_mix, dcw, d_wr, d_wi, d_ws.reshape(N_GROUPS * POS_BLOCK, POS_BLOCK), d_bs]
    return dict(grad_x=grad_x, gath=gath, red=red, recv=recv, pending=pending,
                w_in=gw_in, w_out=gw_out, w_up=gw_up, w_down=gw_down)


def kernel(x, c, w_ada, b_ada, g_mix_pre, g_mix_post, w_in, conv_w, conv_b, w_rgate, b_rgate, w_igate, b_igate, lru_a, v_norm_g, v_norm_b, w_spatial, b_spatial, g_lru_out, g_gmlp_out, w_out, g_ffn_pre, g_ffn_post, w_up, ffn_conv_w, ffn_conv_b, w_down, loss_target, m_w_ada, m_b_ada, m_g_mix_pre, m_g_mix_post, m_w_in, m_conv_w, m_conv_b, m_w_rgate, m_b_rgate, m_w_igate, m_b_igate, m_lru_a, m_v_norm_g, m_v_norm_b, m_w_spatial, m_b_spatial, m_g_lru_out, m_g_gmlp_out, m_w_out, m_g_ffn_pre, m_g_ffn_post, m_w_up, m_ffn_conv_w, m_ffn_conv_b, m_w_down, v_w_ada, v_b_ada, v_g_mix_pre, v_g_mix_post, v_w_in, v_conv_w, v_conv_b, v_w_rgate, v_b_rgate, v_w_igate, v_b_igate, v_lru_a, v_v_norm_g, v_v_norm_b, v_w_spatial, v_b_spatial, v_g_lru_out, v_g_gmlp_out, v_w_out, v_g_ffn_pre, v_g_ffn_post, v_w_up, v_ffn_conv_w, v_ffn_conv_b, v_w_down):
    me = _dev_index(_my_pos())
    ada_cols = w_ada.shape[-1]

    big_w = dict(w_in=(w_in, m_w_in, v_w_in, True), w_out=(w_out, m_w_out, v_w_out, False),
                 w_up=(w_up, m_w_up, v_w_up, True), w_down=(w_down, m_w_down, v_w_down, False))

    def gather(*names):
        return _gather_carry([big_w[n][0][0] for n in names], [STACKED if n == "w_up" else big_w[n][3] for n in names])

    def scatter(parts):
        return _scatter_carry([g for _, g, _ in parts], [big_w[n][0].shape[1:] for n, _, _ in parts],
                              [big_w[n][3] for n, _, _ in parts], [rel for _, _, rel in parts])

    (c_all, cw_all, fcw_all, mod_all), (w_in_f, w_out_f) = _prologue(
        jnp.broadcast_to(c, (SUBLANES, D_MODEL)), _pad_rows(conv_w[0]), _pad_rows(ffn_conv_w[0]), w_ada[0], b_ada,
        carry=gather("w_in", "w_out"))
    conv_w_full = _columns_from_devices(cw_all, LRU_CONV_K)
    ffn_cw_full = _columns_from_devices(fcw_all, FFN_CONV_K)
    mod = lax.dynamic_index_in_dim(mod_all.reshape(N_DEV, N_DEV, ada_cols), me, axis=1, keepdims=False)
    mod = mod.reshape(N_MOD, 1, D_MODEL)

    loc = _local_step(x[0], loss_target[0], mod, w_in_f, dict(w_out=w_out_f), conv_w_full, ffn_cw_full,
                      g_mix_pre, g_mix_post, conv_b, w_rgate, b_rgate, w_igate, b_igate, lru_a, v_norm_g, v_norm_b,
                      w_spatial, b_spatial, g_lru_out, g_gmlp_out, g_ffn_pre, g_ffn_post, ffn_conv_b,
                      gather=gather, scatter=scatter)
    grad_x = loc["grad_x"]

    (gathered, reduced), got = _reduce_small(loc["gath"], loc["red"], carry=_chip_scatter_carry(loc["pending"]))
    loc["recv"]["w_in"].append(got[0])

    results = {}
    for name, (w_, m_, v_, cs) in big_w.items():
        results[name] = _adamw_sum(w_, loc[name][1], loc["recv"][name], m_, v_, "adamw_" + name)

    params = dict(
        b_ada=(b_ada, m_b_ada, v_b_ada), g_mix_pre=(g_mix_pre, m_g_mix_pre, v_g_mix_pre),
        g_mix_post=(g_mix_post, m_g_mix_post, v_g_mix_post), conv_b=(conv_b, m_conv_b, v_conv_b),
        w_rgate=(w_rgate, m_w_rgate, v_w_rgate), b_rgate=(b_rgate, m_b_rgate, v_b_rgate),
        w_igate=(w_igate, m_w_igate, v_w_igate), b_igate=(b_igate, m_b_igate, v_b_igate),
        lru_a=(lru_a, m_lru_a, v_lru_a), v_norm_g=(v_norm_g, m_v_norm_g, v_v_norm_g),
        v_norm_b=(v_norm_b, m_v_norm_b, v_v_norm_b), w_spatial=(w_spatial, m_w_spatial, v_w_spatial),
        b_spatial=(b_spatial, m_b_spatial, v_b_spatial), g_lru_out=(g_lru_out, m_g_lru_out, v_g_lru_out),
        g_gmlp_out=(g_gmlp_out, m_g_gmlp_out, v_g_gmlp_out), g_ffn_pre=(g_ffn_pre, m_g_ffn_pre, v_g_ffn_pre),
        g_ffn_post=(g_ffn_post, m_g_ffn_post, v_g_ffn_post), ffn_conv_b=(ffn_conv_b, m_ffn_conv_b, v_ffn_conv_b))
    conv_params = dict(conv_w=(conv_w, m_conv_w, v_conv_w), ffn_conv_w=(ffn_conv_w, m_ffn_conv_w, v_ffn_conv_w))
    small_results, loss = _adamw_small(gathered, reduced, params, conv_params)
    results.update(small_results)
    loss = loss.reshape(())

    results["w_ada"] = _adamw_wada(c_all, gathered[0], gathered[1], gathered[2], w_ada, m_w_ada, v_w_ada)

    order = ["w_ada", "b_ada", "g_mix_pre", "g_mix_post", "w_in", "conv_w", "conv_b", "w_rgate", "b_rgate", "w_igate",
             "b_igate", "lru_a", "v_norm_g", "v_norm_b", "w_spatial", "b_spatial", "g_lru_out", "g_gmlp_out", "w_out",
             "g_ffn_pre", "g_ffn_post", "w_up", "ffn_conv_w", "ffn_conv_b", "w_down"]
    outs = [loss, grad_x[None]]
    for kind in range(4):
        outs += [results[n][kind] for n in order]
    return tuple(outs)
```

```python
import functools

import jax
import jax.numpy as jnp
from jax import lax
from jax.experimental import pallas as pl
from jax.experimental.pallas import tpu as pltpu

F32 = jnp.float32
BF16 = jnp.bfloat16

D_MODEL = 1024
LRU_W = 512
GMLP_W = 512
N_HEADS = 8
HEAD_DIM = 64
N_GROUPS = 4
POS_BLOCK = 128
CHUNK = 64
IN_COLS = 2048
D_FF = 3072
N_MOD = 6
N_DEV = 8
EPS = 1e-6
LRU_C = 8.0
LRU_CONV_K = 4
FFN_CONV_K = 3

ADAM_LR = 0.001
ADAM_B1 = 0.9
ADAM_B2 = 0.999
ADAM_EPS = 1e-08
ADAM_WD = 0.01
ADAM_STEP = 10

LANES = 128
SUBLANES = 8
TT_BIG = 512
TT_ROWS = 1024
TT_MIX = 256
FF_CW = 1536
VMEM_LIMIT = 56 * 1024 * 1024

MESH = pl.DeviceIdType.MESH


def _sds(shape, dtype):
    return jax.ShapeDtypeStruct(shape, dtype)


def _cparams(sem=None):
    return pltpu.CompilerParams(dimension_semantics=sem, vmem_limit_bytes=VMEM_LIMIT)


def _whole():
    return pl.BlockSpec(memory_space=pltpu.VMEM)


def _const(shape):
    nd = len(shape)
    return pl.BlockSpec(shape, lambda *_: (0,) * nd)


def _any():
    return pl.BlockSpec(memory_space=pl.ANY)


class _Carry:
    def __init__(self, inputs, in_specs, out_shape, out_specs, scratch, start, finish):
        self.inputs, self.in_specs, self.out_shape, self.out_specs = inputs, in_specs, out_shape, out_specs
        self.scratch, self.start, self.finish = scratch, start, finish


def _call(body, name, grid, in_specs, out_specs, out_shape, scratch, args, carry=None, body_starts_carry=False):
    n_in, n_out, n_scr = len(in_specs), len(out_specs), len(scratch)
    c_in = len(carry.in_specs) if carry else 0
    c_out = len(carry.out_specs) if carry else 0

    def full_body(*refs):
        ins = refs[:n_in]
        c_ins = refs[n_in:n_in + c_in]
        outs = refs[n_in + c_in:n_in + c_in + n_out]
        c_outs = refs[n_in + c_in + n_out:n_in + c_in + n_out + c_out]
        scr = refs[n_in + c_in + n_out + c_out:n_in + c_in + n_out + c_out + n_scr]
        c_scr = refs[n_in + c_in + n_out + c_out + n_scr:]
        if carry:
            first = functools.reduce(lambda a, b: a & b, [pl.program_id(d) == 0 for d in range(len(grid))])
            last = functools.reduce(lambda a, b: a & b, [pl.program_id(d) == g - 1 for d, g in enumerate(grid)])

        if carry and not body_starts_carry:
            @pl.when(first)
            def _():
                carry.start(c_ins, c_outs, c_scr)

        if body_starts_carry:
            body(*ins, *outs, *scr, start_carry=(lambda: carry.start(c_ins, c_outs, c_scr)) if carry else (lambda: None))
        else:
            body(*ins, *outs, *scr)
        if carry:
            @pl.when(last)
            def _():
                carry.finish(c_ins, c_outs, c_scr)

    res = pl.pallas_call(
        full_body, name=name, grid=grid,
        in_specs=list(in_specs) + (list(carry.in_specs) if carry else []),
        out_specs=list(out_specs) + (list(carry.out_specs) if carry else []),
        out_shape=list(out_shape) + (list(carry.out_shape) if carry else []),
        scratch_shapes=list(scratch) + (list(carry.scratch) if carry else []),
        compiler_params=_cparams(("arbitrary",) * len(grid)),
    )(*args, *(carry.inputs if carry else []))
    return res[:n_out], res[n_out:]


GELU_C0 = 0.7978845608028654
GELU_C1 = GELU_C0 * 0.044715


def _gelu(x):
    t = jnp.tanh(x * (GELU_C0 + GELU_C1 * (x * x)))
    hx = 0.5 * x
    return hx + hx * t


def _gelu_and_grad(x):
    x2 = x * x
    t = jnp.tanh(x * (GELU_C0 + GELU_C1 * x2))
    hx = 0.5 * x
    g = hx + hx * t
    dg = (0.5 + 0.5 * t) + hx * (1.0 - t * t) * (GELU_C0 + 3.0 * GELU_C1 * x2)
    return g, dg


def _sigmoid(x):
    return 1.0 / (1.0 + jnp.exp(-x))


def _softplus(x):
    return jnp.maximum(x, 0.0) + jnp.log1p(jnp.exp(-jnp.abs(x)))


def _neg_expm1(x):
    series = -x * (1.0 + x * (0.5 + x * (1.0 / 6.0 + x * (1.0 / 24.0 + x * (1.0 / 120.0)))))
    return jnp.where(x > -0.1, series, 1.0 - jnp.exp(x))


def _dot(a, b):
    return jnp.dot(a.astype(BF16), b.astype(BF16), preferred_element_type=F32)


def _dot_nt(a, b):
    return lax.dot_general(a.astype(BF16), b.astype(BF16), (((1,), (1,)), ((), ())), preferred_element_type=F32)


def _dot_tn(a, b):
    return lax.dot_general(a.astype(BF16), b.astype(BF16), (((0,), (0,)), ((), ())), preferred_element_type=F32)


def _rows(shape):
    return lax.broadcasted_iota(jnp.int32, shape, 0)


def _shift_down(cur, prev8, s):
    if s == 0:
        return cur
    n = cur.shape[0]
    r = pltpu.roll(cur, s, 0)
    p = pltpu.roll(prev8, s, 0)
    top = jnp.where(_rows(p.shape) < s, p, r[0:SUBLANES])
    if n == SUBLANES:
        return top
    return jnp.concatenate([top, r[SUBLANES:]], axis=0)


def _shift_up(cur, next8, s):
    if s == 0:
        return cur
    n = cur.shape[0]
    r = pltpu.roll(cur, n - s, 0)
    q = pltpu.roll(next8, SUBLANES - s, 0)
    bot = jnp.where(_rows(q.shape) >= SUBLANES - s, q, r[n - SUBLANES:])
    if n == SUBLANES:
        return bot
    return jnp.concatenate([r[:n - SUBLANES], bot], axis=0)


def _scan_fwd(a, b, h_in):
    n = a.shape[0]
    in_group = _rows(a.shape) & (SUBLANES - 1)
    s = 1
    while s < SUBLANES:
        a_s = pltpu.roll(a, s, 0)
        b_s = pltpu.roll(b, s, 0)
        m = in_group >= s
        b = jnp.where(m, a * b_s + b, b)
        a = jnp.where(m, a * a_s, a)
        s *= 2
    out, carry = [], h_in
    for g in range(n // SUBLANES):
        rows = slice(g * SUBLANES, (g + 1) * SUBLANES)
        h_g = a[rows] * carry + b[rows]
        out.append(h_g)
        carry = h_g[SUBLANES - 1:SUBLANES, :]
    return jnp.concatenate(out, axis=0)


def _scan_rev(a, b, l_in):
    n = a.shape[0]
    in_group = _rows(a.shape) & (SUBLANES - 1)
    s = 1
    while s < SUBLANES:
        a_s = pltpu.roll(a, n - s, 0)
        b_s = pltpu.roll(b, n - s, 0)
        m = in_group < SUBLANES - s
        b = jnp.where(m, b + a * b_s, b)
        a = jnp.where(m, a * a_s, a)
        s *= 2
    out, carry = [], l_in
    for g in reversed(range(n // SUBLANES)):
        rows = slice(g * SUBLANES, (g + 1) * SUBLANES)
        l_g = b[rows] + a[rows] * carry
        out.append(l_g)
        carry = l_g[0:1, :]
    return jnp.concatenate(out[::-1], axis=0)


def _rms(x):
    r = lax.rsqrt(jnp.mean(x * x, axis=-1, keepdims=True) + EPS)
    return x * r, r


def _rms_bwd(d_n, n, r):
    return r * (d_n - n * jnp.mean(d_n * n, axis=-1, keepdims=True))


def _colsum(x):
    return jnp.sum(x, axis=0, keepdims=True)


ROW_PIECE = 256


def _row_pieces(tt):
    return [slice(r, r + min(ROW_PIECE, tt)) for r in range(0, tt, min(ROW_PIECE, tt))]


def _lru_gates(xc, wr_ref, wi_ref, br, bi, sp_a):
    r = _sigmoid(_dot(xc, wr_ref[...]) + br)
    i = _sigmoid(_dot(xc, wi_ref[...]) + bi)
    la = -LRU_C * r * sp_a
    a = jnp.exp(la)
    mult = jnp.sqrt(_neg_expm1(2.0 * la))
    return r, i, a, mult


def _lru_conv(lx, prev8, cw_ref, cb):
    xc = cb + cw_ref[LRU_CONV_K - 1:LRU_CONV_K, :] * lx
    taps = []
    for k in range(LRU_CONV_K - 1):
        tap = _shift_down(lx, prev8, LRU_CONV_K - 1 - k)
        taps.append(tap)
        xc = xc + cw_ref[k:k + 1, :] * tap
    return xc, taps


def _ws_mask(transposed=False):
    i = lax.broadcasted_iota(jnp.int32, (POS_BLOCK, POS_BLOCK), 0)
    j = lax.broadcasted_iota(jnp.int32, (POS_BLOCK, POS_BLOCK), 1)
    if transposed:
        i, j = j, i
    return (j // CHUNK) <= (i // CHUNK)


def _gmlp_v(gv, vg, vb):
    av, dav = _gelu_and_grad(gv)
    mu = jnp.mean(av, axis=-1, keepdims=True)
    cen = av - mu
    rs = lax.rsqrt(jnp.mean(cen * cen, axis=-1, keepdims=True) + EPS)
    vhat = cen * rs
    return vhat * vg + vb, vhat, rs, dav


def _mix_fwd(x, sh, sc, g_pre, w_in, conv_w, conv_b, wr_bd, wi_bd, b_r, b_i, lru_a, vn_g, vn_b, w_sp, b_sp_t,
             g_lru, g_gmlp, w_out, g_post, gt_m, g_ffn_pre, sc_f, sh_f, carry=None):
    s_len = x.shape[0]
    tt = min(TT_MIX, s_len)
    nblk = tt // POS_BLOCK

    def body(x_ref, sh_ref, sc_ref, g_ref, w_ref, cw_ref, cb_ref, wr_ref, wi_ref, br_ref, bi_ref, la_ref, vg_ref,
             vb_ref, ws_ref, bst_ref, gl_ref, gg_ref, wo_ref, gp_ref, gtm_ref, g2_ref, scf_ref, shf_ref,
             z_ref, h_ref, y_ref, hl_ref, yo_ref, x1_ref, h2_ref, prev8, hcar):
        i = pl.program_id(0)

        @pl.when(i == 0)
        def _():
            prev8[...] = jnp.zeros_like(prev8)
            hcar[...] = jnp.zeros_like(hcar)

        n_x, _ = _rms(x_ref[...])
        h = (n_x * g_ref[...] * (1.0 + sc_ref[...]) + sh_ref[...]).astype(BF16)
        h_ref[...] = h
        z_ref[...] = jnp.dot(h, w_ref[...], preferred_element_type=F32)

        lx = z_ref[:, 0:LRU_W]
        gate = z_ref[:, LRU_W:2 * LRU_W]
        gu = z_ref[:, 2 * LRU_W:2 * LRU_W + GMLP_W]
        gv = z_ref[:, 2 * LRU_W + GMLP_W:]

        xc, _ = _lru_conv(lx, prev8[...], cw_ref, cb_ref[...])
        prev8[...] = lx[tt - SUBLANES:]
        sp_a = _softplus(-la_ref[...])
        _, ig, a, mult = _lru_gates(xc, wr_ref, wi_ref, br_ref[...], bi_ref[...], sp_a)
        bx = mult * (ig * xc)
        hl = _scan_fwd(a, bx, hcar[0:1, :])
        hcar[...] = jnp.broadcast_to(hl[tt - 1:tt, :], hcar.shape)
        hl_ref[...] = hl
        y_lru = hl * _gelu(gate)
        n_l, _ = _rms(y_lru)
        y_ref[:, 0:LRU_W] = (n_l * gl_ref[...]).astype(BF16)

        u = _gelu(gu)
        v, _, _, _ = _gmlp_v(gv, vg_ref[...], vb_ref[...])
        mask = _ws_mask()
        sp_parts = []
        for nb in range(nblk):
            row = []
            for g in range(N_GROUPS):
                wsm = jnp.where(mask, ws_ref[g], 0.0)
                vblk = v[nb * POS_BLOCK:(nb + 1) * POS_BLOCK, g * LANES:(g + 1) * LANES]
                row.append(_dot(wsm, vblk) + bst_ref[:, g:g + 1])
            sp_parts.append(jnp.concatenate(row, axis=1))
        sp = jnp.concatenate(sp_parts, axis=0) if nblk > 1 else sp_parts[0]
        n_g, _ = _rms(u * sp)
        y_ref[:, LRU_W:] = (n_g * gg_ref[...]).astype(BF16)

        y = jnp.dot(y_ref[...], wo_ref[...], preferred_element_type=F32)
        yo_ref[...] = y
        n_y, _ = _rms(y)
        x1 = x_ref[...] + gtm_ref[...] * (n_y * gp_ref[...])
        x1_ref[...] = x1
        n1, _ = _rms(x1)
        h2_ref[...] = (n1 * g2_ref[...] * (1.0 + scf_ref[...]) + shf_ref[...]).astype(BF16)

    row = lambda c: pl.BlockSpec((tt, c), lambda i: (i, 0))
    v512 = _const((1, LRU_W))
    vec = _const((1, D_MODEL))
    return _call(
        body, "mix_fwd", (s_len // tt,),
        in_specs=[row(D_MODEL), vec, vec, vec, _whole(),
                  _const((LRU_CONV_K, LRU_W)), v512, _whole(), _whole(), v512, v512, v512, v512, v512,
                  _whole(), _whole(), v512, v512, _whole(), vec, vec, vec, vec, vec],
        out_specs=[row(IN_COLS), row(D_MODEL), row(LRU_W + GMLP_W), row(LRU_W), row(D_MODEL), row(D_MODEL),
                   row(D_MODEL)],
        out_shape=[_sds((s_len, IN_COLS), F32), _sds((s_len, D_MODEL), BF16),
                   _sds((s_len, LRU_W + GMLP_W), BF16), _sds((s_len, LRU_W), F32),
                   _sds((s_len, D_MODEL), F32), _sds((s_len, D_MODEL), F32), _sds((s_len, D_MODEL), BF16)],
        scratch=[pltpu.VMEM((SUBLANES, LRU_W), F32), pltpu.VMEM((SUBLANES, LRU_W), F32)],
        args=(x, sh, sc, g_pre, w_in, conv_w, conv_b, wr_bd, wi_bd, b_r, b_i, lru_a, vn_g, vn_b, w_sp, b_sp_t,
              g_lru, g_gmlp, w_out, g_post, gt_m, g_ffn_pre, sc_f, sh_f), carry=carry)


FF_CHUNKS = N_DEV // 2
FF_CHUNK_W = D_FF // FF_CHUNKS


def _ffn_fwd(h2, w_up3, ffn_cw, ffn_cb, carry=None):
    s_len = h2.shape[0]
    tt = min(TT_BIG, s_len)
    nc, cw = FF_CHUNKS, FF_CHUNK_W
    per_step = 2
    ng, gw = nc // per_step, per_step * cw

    def body(h2_ref, wu_ref, cwg_ref, cwv_ref, cbg_ref, cbv_ref, up_ref, upc_ref, act_ref, prev):
        i = pl.program_id(0)
        g = pl.program_id(1)
        h2 = h2_ref[...]
        for sub in range(per_step):
            c = g * per_step + sub
            cols = slice(sub * cw, (sub + 1) * cw)

            @pl.when(i == 0)
            def _():
                prev[c] = jnp.zeros((2, SUBLANES, cw), F32)

            ug_pre = jnp.dot(h2, wu_ref[c], preferred_element_type=F32)
            uv_pre = jnp.dot(h2, wu_ref[nc + c], preferred_element_type=F32)
            up_ref[0, :, cols] = ug_pre.astype(BF16)
            up_ref[1, :, cols] = uv_pre.astype(BF16)
            ug, _ = _ffn_conv(ug_pre, prev[c, 0], cwg_ref.at[:, cols], cbg_ref[:, cols])
            uv, _ = _ffn_conv(uv_pre, prev[c, 1], cwv_ref.at[:, cols], cbv_ref[:, cols])
            prev[c, 0] = ug_pre[tt - SUBLANES:, :]
            prev[c, 1] = uv_pre[tt - SUBLANES:, :]
            upc_ref[0, :, cols] = ug
            upc_ref[1, :, cols] = uv
            act_ref[:, cols] = (_gelu(ug) * uv).astype(BF16)

    chunk2 = pl.BlockSpec((2, tt, gw), lambda i, g: (0, i, g))
    ffn_cb2 = ffn_cb.reshape(1, 2 * D_FF)
    return _call(
        body, "ffn_fwd", (s_len // tt, ng),
        in_specs=[pl.BlockSpec((tt, D_MODEL), lambda i, g: (i, 0)), _whole(),
                  pl.BlockSpec((FFN_CONV_K, gw), lambda i, g: (0, g)),
                  pl.BlockSpec((FFN_CONV_K, gw), lambda i, g: (0, g + ng)),
                  pl.BlockSpec((1, gw), lambda i, g: (0, g)),
                  pl.BlockSpec((1, gw), lambda i, g: (0, g + ng))],
        out_specs=[chunk2, chunk2, pl.BlockSpec((tt, gw), lambda i, g: (i, g))],
        out_shape=[_sds((2, s_len, D_FF), BF16), _sds((2, s_len, D_FF), F32), _sds((s_len, D_FF), BF16)],
        scratch=[pltpu.VMEM((nc, 2, SUBLANES, cw), F32)],
        args=(h2, w_up3, ffn_cw, ffn_cw, ffn_cb2, ffn_cb2), carry=carry)


def _ffn_tail(act, w_down, x1, gt_f, g_post, target):
    s_len = x1.shape[0]
    tt = min(TT_ROWS, s_len)

    def body(act_ref, wd_ref, x1_ref, gtf_ref, gp_ref, tg_ref, dy2_ref, dout_ref, loss_ref, vs_ref):
        @pl.when(pl.program_id(0) == 0)
        def _():
            loss_ref[...] = jnp.zeros_like(loss_ref)
            vs_ref[...] = jnp.zeros_like(vs_ref)

        for rows in _row_pieces(tt):
            n2, r2 = _rms(jnp.dot(act_ref[rows, :], wd_ref[...], preferred_element_type=F32))
            out = x1_ref[rows, :] + gtf_ref[...] * (n2 * gp_ref[...])
            err = out - tg_ref[rows, :]
            do = err * (1.0 / D_MODEL)
            dout_ref[rows, :] = do
            loss_ref[...] += jnp.broadcast_to(0.5 * jnp.sum(err * err, keepdims=True) * (1.0 / D_MODEL),
                                              loss_ref.shape)
            vs_ref[0:1, :] += _colsum(do * n2 * gp_ref[...])
            vs_ref[1:2, :] += _colsum(do * gtf_ref[...] * n2)
            dy2_ref[rows, :] = _rms_bwd(do * gtf_ref[...] * gp_ref[...], n2, r2).astype(BF16)

    row = lambda c: pl.BlockSpec((tt, c), lambda i: (i, 0))
    vec = _const((1, D_MODEL))
    outs, _ = _call(
        body, "ffn_tail", (s_len // tt,),
        in_specs=[row(D_FF), _whole(), row(D_MODEL), vec, vec, row(D_MODEL)],
        out_specs=[row(D_MODEL), row(D_MODEL), _const((SUBLANES, LANES)), _const((SUBLANES, D_MODEL))],
        out_shape=[_sds((s_len, D_MODEL), BF16), _sds((s_len, D_MODEL), F32), _sds((SUBLANES, LANES), F32),
                   _sds((SUBLANES, D_MODEL), F32)],
        scratch=[], args=(act, w_down, x1, gt_f, g_post, target))
    return outs


def _ffn_conv(up_pre, prev8, cw_ref, cb):
    up = cb + cw_ref[FFN_CONV_K - 1:FFN_CONV_K, :] * up_pre
    taps = []
    for k in range(FFN_CONV_K - 1):
        tap = _shift_down(up_pre, prev8, FFN_CONV_K - 1 - k)
        taps.append(tap)
        up = up + cw_ref[k:k + 1, :] * tap
    return up, taps


def _ffn_bwd(d_y2, up_pre, up, ffn_cw, w_down, carry=None):
    s_len = d_y2.shape[0]
    tt = min(TT_BIG, s_len)
    nt = s_len // tt
    cw = FF_CW
    nc = D_FF // cw

    def body(dy2_ref, up_ref, upc_ref, cwg_ref, cwv_ref, wd_ref, dup_ref, cs_ref, nxt, cs_acc):
        i = pl.program_id(0)
        c = pl.program_id(1)

        @pl.when(i == 0)
        def _():
            nxt[c] = jnp.zeros((2, SUBLANES, cw), F32)
            cs_acc[c] = jnp.zeros((2, SUBLANES, cw), F32)

        pw = 2 * LANES
        for piece in range(cw // pw):
            cols = slice(piece * pw, (piece + 1) * pw)
            d_act = _dot_nt(dy2_ref[...], wd_ref[pl.ds(pl.multiple_of(c * cw + piece * pw, pw), pw), :])
            uv = upc_ref[1, :, cols]
            gl, dgl = _gelu_and_grad(upc_ref[0, :, cols])
            d_ug = d_act * uv * dgl
            d_uv = d_act * gl
            for half, (d_u, cw_ref) in enumerate(((d_ug, cwg_ref), (d_uv, cwv_ref))):
                nx = nxt[c, half, :, cols]
                x_in = up_ref[half, :, cols].astype(F32)
                d_pre = cw_ref[FFN_CONV_K - 1:FFN_CONV_K, cols] * d_u
                sums = [None] * (FFN_CONV_K + 1)
                sums[FFN_CONV_K - 1] = _colsum(d_u * x_in)
                for k in range(FFN_CONV_K - 1):
                    ahead = _shift_up(d_u, nx, FFN_CONV_K - 1 - k)
                    d_pre = d_pre + cw_ref[k:k + 1, cols] * ahead
                    sums[k] = _colsum(ahead * x_in)
                sums[FFN_CONV_K] = _colsum(d_u)
                pad = jnp.zeros((SUBLANES - FFN_CONV_K - 1, pw), F32)
                cs_acc[c, half, :, cols] += jnp.concatenate(sums + [pad], axis=0)
                nxt[c, half, :, cols] = d_u[0:SUBLANES]
                dup_ref[half, :, cols] = d_pre.astype(BF16)

        for cc in range(nc):
            @pl.when((i == nt - 1) & (c == cc))
            def _():
                cs_ref[:, cc * cw:(cc + 1) * cw] = cs_acc[cc, 0]
                cs_ref[:, D_FF + cc * cw:D_FF + (cc + 1) * cw] = cs_acc[cc, 1]

    row = pl.BlockSpec((tt, D_MODEL), lambda i, c: (nt - 1 - i, 0))
    blk = pl.BlockSpec((2, tt, cw), lambda i, c: (0, nt - 1 - i, c))
    return _call(
        body, "ffn_bwd", (nt, nc),
        in_specs=[row, blk, blk,
                  pl.BlockSpec((FFN_CONV_K, cw), lambda i, c: (0, c)),
                  pl.BlockSpec((FFN_CONV_K, cw), lambda i, c: (0, c + nc)),
                  _whole()],
        out_specs=[blk, _const((SUBLANES, 2 * D_FF))],
        out_shape=[_sds((2, s_len, D_FF), BF16), _sds((SUBLANES, 2 * D_FF), F32)],
        scratch=[pltpu.VMEM((nc, 2, SUBLANES, cw), F32), pltpu.VMEM((nc, 2, SUBLANES, cw), F32)],
        args=(d_y2, up_pre, up, ffn_cw, ffn_cw, w_down), carry=carry)


def _up_bwd(d_up, w_up3, x1, dout, y, w_out, g_pre, sc_f, g_post, gt_m, carry=None):
    s_len = x1.shape[0]
    tt = min(TT_BIG, s_len)

    def body(du_ref, wu_ref, x1_ref, do_ref, y_ref, wo_ref, g2_ref, sc_ref, gp_ref, gt_ref,
             dx1_ref, dy_ref, dyc_ref, vs_ref):
        @pl.when(pl.program_id(0) == 0)
        def _():
            vs_ref[...] = jnp.zeros_like(vs_ref)

        for rows in _row_pieces(tt):
            d_h2 = jnp.zeros((rows.stop - rows.start, D_MODEL), F32)
            for half in range(2):
                for ch in range(FF_CHUNKS):
                    d_h2 = d_h2 + _dot_nt(du_ref[half, rows, ch * FF_CHUNK_W:(ch + 1) * FF_CHUNK_W],
                                          wu_ref[half * FF_CHUNKS + ch])
            n1, r1 = _rms(x1_ref[rows, :])
            ng = n1 * g2_ref[...]
            vs_ref[0:1, :] += _colsum(d_h2)
            vs_ref[1:2, :] += _colsum(d_h2 * ng)
            d_ng = d_h2 * (1.0 + sc_ref[...])
            vs_ref[2:3, :] += _colsum(d_ng * n1)
            d_x1 = do_ref[rows, :] + _rms_bwd(d_ng * g2_ref[...], n1, r1)
            dx1_ref[rows, :] = d_x1
            n_y, r_y = _rms(y_ref[rows, :])
            vs_ref[3:4, :] += _colsum(d_x1 * n_y * gp_ref[...])
            d_on = d_x1 * gt_ref[...]
            vs_ref[4:5, :] += _colsum(d_on * n_y)
            d_y = _rms_bwd(d_on * gp_ref[...], n_y, r_y).astype(BF16)
            dy_ref[rows, :] = d_y
            dyc_ref[rows, :] = _dot_nt(d_y, wo_ref[...])

    row = lambda c: pl.BlockSpec((tt, c), lambda i: (i, 0))
    vec = _const((1, D_MODEL))
    return _call(
        body, "up_bwd", (s_len // tt,),
        in_specs=[pl.BlockSpec((2, tt, D_FF), lambda i: (0, i, 0)), _whole(), row(D_MODEL), row(D_MODEL), row(D_MODEL),
                  _whole(), vec, vec, vec, vec],
        out_specs=[row(D_MODEL), row(D_MODEL), row(LRU_W + GMLP_W), _const((SUBLANES, D_MODEL))],
        out_shape=[_sds((s_len, D_MODEL), F32), _sds((s_len, D_MODEL), BF16), _sds((s_len, LRU_W + GMLP_W), F32),
                   _sds((SUBLANES, D_MODEL), F32)],
        scratch=[], args=(d_up, w_up3, x1, dout, y, w_out, g_pre, sc_f, g_post, gt_m), carry=carry)


def _head_pair_block(hd):
    return (slice((hd // 2) * HEAD_DIM, (hd // 2 + 1) * HEAD_DIM), slice((hd % 2) * HEAD_DIM, (hd % 2 + 1) * HEAD_DIM))


def _mix_bwd(d_ycat, z, hl, conv_w, conv_b, wr_bd, wi_bd, b_r, b_i, lru_a, vn_g, vn_b, w_sp, w_sp_t, b_sp_t,
             g_lru, g_gmlp, carry=None):
    s_len = z.shape[0]
    tt = min(TT_MIX, s_len)
    nt = s_len // tt
    nblk = tt // POS_BLOCK
    hb = tt // SUBLANES

    def body(dyc_ref, z_ref, zh_ref, hl_ref, hh_ref, cw_ref, cb_ref, wr_ref, wi_ref, br_ref, bi_ref, la_ref,
             vg_ref, vb_ref, ws_ref, wst_ref, bst_ref, gl_ref, gg_ref,
             dz_ref, vs_ref, dcw_ref, dwrb_ref, dwib_ref, dws_ref, dbs_ref, nxt_dxc, nxt_a, nxt_lam, dwr_ref, dwi_ref):
        i = pl.program_id(0)
        first_tile = i == nt - 1

        @pl.when(i == 0)
        def _():
            for ref in (vs_ref, dcw_ref, dwr_ref, dwi_ref, dws_ref, dbs_ref, nxt_dxc, nxt_a, nxt_lam):
                ref[...] = jnp.zeros_like(ref)

        lx = z_ref[:, 0:LRU_W]
        gate = z_ref[:, LRU_W:2 * LRU_W]
        gu = z_ref[:, 2 * LRU_W:2 * LRU_W + GMLP_W]
        gv = z_ref[:, 2 * LRU_W + GMLP_W:]
        prev8 = jnp.where(first_tile, 0.0, zh_ref[...])
        hprev8 = jnp.where(first_tile, 0.0, hh_ref[...])

        xc, taps = _lru_conv(lx, prev8, cw_ref, cb_ref[...])
        a_par = la_ref[...]
        sp_a = _softplus(-a_par)
        r, ig, a, mult = _lru_gates(xc, wr_ref, wi_ref, br_ref[...], bi_ref[...], sp_a)
        hl = hl_ref[...]
        h_prev = _shift_down(hl, hprev8, 1)
        ggate, dggate = _gelu_and_grad(gate)
        y_lru = hl * ggate
        n_l, r_l = _rms(y_lru)
        d_nl = dyc_ref[:, 0:LRU_W]
        vs_ref[6:7, :] += _colsum(d_nl * n_l)
        d_yl = _rms_bwd(d_nl * gl_ref[...], n_l, r_l)
        d_hl = d_yl * ggate
        d_gate = d_yl * hl * dggate
        a_up = _shift_up(a, nxt_a[...], 1)
        lam = _scan_rev(a_up, d_hl, nxt_lam[0:1, :])
        nxt_a[...] = jnp.broadcast_to(a[0:1, :], nxt_a.shape)
        nxt_lam[...] = jnp.broadcast_to(lam[0:1, :], nxt_lam.shape)
        ixc = ig * xc
        d_la = lam * h_prev * a - lam * ixc * (a * a) / mult
        d_i = lam * mult * xc
        d_xc = lam * mult * ig
        vs_ref[3:4, :] += _colsum(d_la * r) * (LRU_C * _sigmoid(-a_par))
        d_pr = d_la * (-LRU_C * sp_a) * r * (1.0 - r)
        d_pi = d_i * ig * (1.0 - ig)
        vs_ref[1:2, :] += _colsum(d_pr)
        vs_ref[2:3, :] += _colsum(d_pi)
        dwr_ref[...] += _dot_tn(xc, d_pr)
        dwi_ref[...] += _dot_tn(xc, d_pi)
        d_xc = d_xc + _dot_nt(d_pr, wr_ref[...]) + _dot_nt(d_pi, wi_ref[...])
        vs_ref[0:1, :] += _colsum(d_xc)
        nx = nxt_dxc[...]
        d_lx = cw_ref[LRU_CONV_K - 1:LRU_CONV_K, :] * d_xc
        dcw_ref[LRU_CONV_K - 1:LRU_CONV_K, :] += _colsum(d_xc * lx)
        for k in range(LRU_CONV_K - 1):
            d_lx = d_lx + cw_ref[k:k + 1, :] * _shift_up(d_xc, nx, LRU_CONV_K - 1 - k)
            dcw_ref[k:k + 1, :] += _colsum(d_xc * taps[k])
        nxt_dxc[...] = d_xc[0:SUBLANES]
        dz_ref[:, 0:LRU_W] = d_lx.astype(BF16)
        dz_ref[:, LRU_W:2 * LRU_W] = d_gate.astype(BF16)

        u, du = _gelu_and_grad(gu)
        v, vhat, rs, dav = _gmlp_v(gv, vg_ref[...], vb_ref[...])
        mask = _ws_mask()
        sp_parts = []
        for nb in range(nblk):
            rowp = []
            for g in range(N_GROUPS):
                wsm = jnp.where(mask, ws_ref[g], 0.0)
                vblk = v[nb * POS_BLOCK:(nb + 1) * POS_BLOCK, g * LANES:(g + 1) * LANES]
                rowp.append(_dot(wsm, vblk) + bst_ref[:, g:g + 1])
            sp_parts.append(jnp.concatenate(rowp, axis=1))
        sp = jnp.concatenate(sp_parts, axis=0) if nblk > 1 else sp_parts[0]
        y_g = u * sp
        n_g, r_g = _rms(y_g)
        d_ng = dyc_ref[:, LRU_W:]
        vs_ref[7:8, :] += _colsum(d_ng * n_g)
        d_yg = _rms_bwd(d_ng * gg_ref[...], n_g, r_g)
        d_gu = d_yg * sp * du
        d_sp = d_yg * u
        mask_t = _ws_mask(transposed=True)
        ones8 = jnp.ones((SUBLANES, LANES), F32)
        dv_parts = []
        for nb in range(nblk):
            rowp = []
            for g in range(N_GROUPS):
                rs_, cs_ = slice(nb * POS_BLOCK, (nb + 1) * POS_BLOCK), slice(g * LANES, (g + 1) * LANES)
                dsp_blk = d_sp[rs_, cs_]
                dbs_ref[g:g + 1, :] += lax.dot_general(
                    ones8, dsp_blk, (((1,), (1,)), ((), ())), preferred_element_type=F32,
                    precision=lax.Precision.HIGHEST)[0:1, :]
                dws_ref[g] += _dot_nt(dsp_blk, v[rs_, cs_])
                wsm_t = jnp.where(mask_t, wst_ref[g], 0.0)
                rowp.append(_dot(wsm_t, dsp_blk))
            dv_parts.append(jnp.concatenate(rowp, axis=1))
        d_v = jnp.concatenate(dv_parts, axis=0) if nblk > 1 else dv_parts[0]
        vs_ref[4:5, :] += _colsum(d_v * vhat)
        vs_ref[5:6, :] += _colsum(d_v)
        d_vh = d_v * vg_ref[...]
        d_av = rs * (d_vh - jnp.mean(d_vh, axis=-1, keepdims=True)
                     - vhat * jnp.mean(d_vh * vhat, axis=-1, keepdims=True))
        dz_ref[:, 2 * LRU_W:2 * LRU_W + GMLP_W] = d_gu.astype(BF16)
        dz_ref[:, 2 * LRU_W + GMLP_W:] = (d_av * dav).astype(BF16)

        @pl.when(i == nt - 1)
        def _():
            for hd in range(N_HEADS):
                blk = slice(hd * HEAD_DIM, (hd + 1) * HEAD_DIM)
                dwrb_ref[_head_pair_block(hd)] = dwr_ref[blk, blk]
                dwib_ref[_head_pair_block(hd)] = dwi_ref[blk, blk]
            for g in range(N_GROUPS):
                dws_ref[g] = jnp.where(mask, dws_ref[g], 0.0)

    rev = lambda c: pl.BlockSpec((tt, c), lambda i: (nt - 1 - i, 0))
    halo = pl.BlockSpec((SUBLANES, LRU_W), lambda i: (jnp.maximum((nt - 1 - i) * hb - 1, 0), 0))
    v512 = _const((1, LRU_W))
    return _call(
        body, "mix_bwd", (nt,),
        in_specs=[rev(LRU_W + GMLP_W), rev(IN_COLS), halo, rev(LRU_W), halo,
                  _const((LRU_CONV_K, LRU_W)), v512, _whole(), _whole(), v512, v512, v512, v512, v512,
                  _whole(), _whole(), _whole(), v512, v512],
        out_specs=[rev(IN_COLS), _const((SUBLANES, LRU_W)), _const((SUBLANES, LRU_W)),
                   _const((LRU_W // 2, 2 * HEAD_DIM)), _const((LRU_W // 2, 2 * HEAD_DIM)),
                   _const((N_GROUPS, POS_BLOCK, POS_BLOCK)), _const((SUBLANES, POS_BLOCK))],
        out_shape=[_sds((s_len, IN_COLS), BF16), _sds((SUBLANES, LRU_W), F32), _sds((SUBLANES, LRU_W), F32),
                   _sds((LRU_W // 2, 2 * HEAD_DIM), F32), _sds((LRU_W // 2, 2 * HEAD_DIM), F32),
                   _sds((N_GROUPS, POS_BLOCK, POS_BLOCK), F32), _sds((SUBLANES, POS_BLOCK), F32)],
        scratch=[pltpu.VMEM((SUBLANES, LRU_W), F32), pltpu.VMEM((SUBLANES, LRU_W), F32),
                 pltpu.VMEM((SUBLANES, LRU_W), F32), pltpu.VMEM((LRU_W, LRU_W), F32), pltpu.VMEM((LRU_W, LRU_W), F32)],
        args=(d_ycat, z, z, hl, hl, conv_w, conv_b, wr_bd, wi_bd, b_r, b_i, lru_a, vn_g, vn_b, w_sp, w_sp_t, b_sp_t,
              g_lru, g_gmlp), carry=carry)


def _in_bwd(d_z, w_in, x, d_x1, g, sc, carry=None):
    s_len = x.shape[0]
    tt = min(TT_ROWS, s_len)

    def body(dz_ref, w_ref, x_ref, dx1_ref, g_ref, sc_ref, gx_ref, vs_ref):
        @pl.when(pl.program_id(0) == 0)
        def _():
            vs_ref[...] = jnp.zeros_like(vs_ref)

        for rows in _row_pieces(tt):
            d_h = _dot_nt(dz_ref[rows, :], w_ref[...])
            n, r = _rms(x_ref[rows, :])
            vs_ref[0:1, :] += _colsum(d_h)
            vs_ref[1:2, :] += _colsum(d_h * n * g_ref[...])
            d_ng = d_h * (1.0 + sc_ref[...])
            vs_ref[2:3, :] += _colsum(d_ng * n)
            gx_ref[rows, :] = dx1_ref[rows, :] + _rms_bwd(d_ng * g_ref[...], n, r)

    row = lambda c: pl.BlockSpec((tt, c), lambda i: (i, 0))
    vec = _const((1, D_MODEL))
    return _call(
        body, "in_bwd", (s_len // tt,),
        in_specs=[row(IN_COLS), _whole(), row(D_MODEL), row(D_MODEL), vec, vec],
        out_specs=[row(D_MODEL), _const((SUBLANES, D_MODEL))],
        out_shape=[_sds((s_len, D_MODEL), F32), _sds((SUBLANES, D_MODEL), F32)],
        scratch=[], args=(d_z, w_in, x, d_x1, g, sc), carry=carry)


def _wgrad(a, b, name, by_rows=False, carry=None):
    s_len, k_dim = a.shape
    halves = b.ndim == 3
    n_dim = b.shape[-1] * (2 if halves else 1)

    per_tile = 2
    steps = N_DEV // per_tile
    reg = (k_dim if by_rows else n_dim) // N_DEV

    def body(a_ref, b_ref, ob_ref, own_ref):
        out = _dot_tn(a_ref[...], b_ref[0] if halves else b_ref[...])
        ob_ref[...] = out.astype(BF16)
        me = _dev_index(_my_pos())
        for part in range(per_tile):
            @pl.when((pl.program_id(0) == me // per_tile) & (me % per_tile == part))
            def _():
                piece = slice(part * reg, (part + 1) * reg)
                own_ref[...] = out[piece, :] if by_rows else out[:, piece]

    tile = per_tile * reg
    if by_rows:
        a_spec = pl.BlockSpec((s_len, tile), lambda j: (0, j))
        b_spec = pl.BlockSpec((s_len, n_dim), lambda j: (0, 0))
        o_spec = pl.BlockSpec((tile, n_dim), lambda j: (j, 0))
        own_shape = (reg, n_dim)
    else:
        a_spec = pl.BlockSpec((s_len, k_dim), lambda j: (0, 0))
        if halves:
            per_half = steps // 2
            b_spec = pl.BlockSpec((1, s_len, tile), lambda j: (j // per_half, 0, j % per_half))
        else:
            b_spec = pl.BlockSpec((s_len, tile), lambda j: (0, j))
        o_spec = pl.BlockSpec((k_dim, tile), lambda j: (0, j))
        own_shape = (k_dim, reg)
    return _call(
        body, name, (steps,), in_specs=[a_spec, b_spec], out_specs=[o_spec, _const(own_shape)],
        out_shape=[_sds((k_dim, n_dim), BF16), _sds(own_shape, F32)],
        scratch=[], args=(a, b), carry=carry)


def _adam_math(w, g, m, v):
    m = ADAM_B1 * m + (1.0 - ADAM_B1) * g
    v = ADAM_B2 * v + (1.0 - ADAM_B2) * (g * g)
    m_hat = m / (1.0 - ADAM_B1 ** ADAM_STEP)
    v_hat = v / (1.0 - ADAM_B2 ** ADAM_STEP)
    delta = -ADAM_LR * (m_hat / (jnp.sqrt(v_hat) + ADAM_EPS) + ADAM_WD * w)
    return delta, m, v


def _row_tile(rows, cols, n_f32_arrays):
    budget = VMEM_LIMIT // 2
    tr = rows
    while tr % 2 == 0 and tr // 2 >= SUBLANES and (tr // 2) % SUBLANES == 0 and tr * cols * 4 * n_f32_arrays * 2 > budget:
        tr //= 2
    return tr


def _adamw_sum(w, g_own, recv, m, v, name):
    _, rows, cols = w.shape
    n_recv = len(recv)
    tr = _row_tile(rows, cols, 10)
    nb = rows // tr

    def body(w_ref, g_ref, *rest):
        r_refs = rest[:n_recv]
        m_ref, v_ref, go_ref, d_ref, mo_ref, vo_ref = rest[n_recv:]
        g = g_ref[...]
        for r_ref in r_refs:
            for k in range(r_ref.shape[0]):
                g = g + r_ref[k].astype(F32)
        go_ref[0] = g
        d_ref[0], mo_ref[0], vo_ref[0] = _adam_math(w_ref[0], g, m_ref[0], v_ref[0])

    blk = pl.BlockSpec((1, tr, cols), lambda i: (0, i, 0))
    return pl.pallas_call(
        body, name=name, grid=(nb,),
        in_specs=[blk, pl.BlockSpec((tr, cols), lambda i: (i, 0))]
        + [pl.BlockSpec((r.shape[0], tr, cols), lambda i: (0, i, 0)) for r in recv] + [blk, blk],
        out_specs=[blk] * 4, out_shape=[_sds((1, rows, cols), F32)] * 4,
        compiler_params=_cparams(("arbitrary",)),
    )(w, g_own, *recv, m, v)


def _row_of_each(ref, row):
    cols = ref.shape[1]
    rows = _rows((N_DEV, cols))
    out = jnp.zeros((N_DEV, cols), F32)
    for d in range(N_DEV):
        picked = ref[d * SUBLANES + row:d * SUBLANES + row + 1, :]
        out = jnp.where(rows == d, jnp.broadcast_to(picked, (N_DEV, cols)), out)
    return out


def _my_columns(full, width, me):
    out = jnp.zeros(full.shape[:-1] + (width,), F32)
    for d in range(N_DEV):
        out = out + jnp.where(me == d, full[:, d * width:(d + 1) * width], 0.0)
    return out


def _adamw_wada(c_all, vs_in_all, vs_up_all, vs_ffn_all, w, m, v):
    _, rows, cols = w.shape

    def body(c_ref, vi_ref, vu_ref, vf_ref, w_ref, m_ref, v_ref, go_ref, d_ref, mo_ref, vo_ref):
        me = _dev_index(_my_pos())
        cv = _row_of_each(c_ref, 0)
        ca = cv * _sigmoid(cv)
        dmod = jnp.concatenate([_row_of_each(vi_ref, 0), _row_of_each(vi_ref, 1), _row_of_each(vu_ref, 3),
                                _row_of_each(vu_ref, 0), _row_of_each(vu_ref, 1), _row_of_each(vf_ref, 0)], axis=1)
        dm = _my_columns(dmod, cols, me)
        g = lax.dot_general(ca, dm, (((0,), (0,)), ((), ())), preferred_element_type=F32,
                            precision=lax.Precision.HIGHEST)
        go_ref[0] = g
        d_ref[0], mo_ref[0], vo_ref[0] = _adam_math(w_ref[0], g, m_ref[0], v_ref[0])

    return pl.pallas_call(
        body, name="adamw_w_ada", out_shape=[_sds((1, rows, cols), F32)] * 4,
        in_specs=[_whole()] * 7, out_specs=[_whole()] * 4,
        compiler_params=_cparams(),
    )(c_all, vs_in_all, vs_up_all, vs_ffn_all, w, m, v)


def _adamw_small(gathered, reduced, params, conv_params):
    names = list(params) + list(conv_params)
    allp = {**params, **conv_params}
    n_g = len(gathered) + len(reduced)

    def body(*refs):
        g_refs = refs[:n_g]
        p_refs = refs[n_g:n_g + 3 * len(names)]
        o_refs = refs[n_g + 3 * len(names):]
        me = _dev_index(_my_pos())

        def total(ref):
            s = ref[0:SUBLANES, :]
            for d in range(1, N_DEV):
                s = s + ref[d * SUBLANES:(d + 1) * SUBLANES, :]
            return s

        vs_in, vs_up, vs_ffn, loss = [total(r) for r in g_refs[:4]]
        cs, vs_mix, dcw, dwr, dwi, dws, dbs = [r[...] for r in g_refs[4:]]
        o_refs[-1][...] = loss[0:1, 0:1]
        mine = lambda full, width: _my_columns(full, width, me)

        all_ = (slice(None), slice(None))
        heads = lambda row: [((0, slice(h, h + 1), slice(None)), row[:, h * HEAD_DIM:(h + 1) * HEAD_DIM])
                             for h in range(N_HEADS)]
        blocks = lambda pairs: [((0, h), pairs[_head_pair_block(h)]) for h in range(N_HEADS)]
        pieces = {
            "b_ada": [((slice(None), slice(k * D_MODEL, (k + 1) * D_MODEL)), row) for k, row in enumerate(
                (vs_in[0:1], vs_in[1:2], vs_up[3:4], vs_up[0:1], vs_up[1:2], vs_ffn[0:1]))],
            "g_mix_pre": [(all_, vs_in[2:3])], "g_mix_post": [(all_, vs_up[4:5])],
            "g_ffn_pre": [(all_, vs_up[2:3])], "g_ffn_post": [(all_, vs_ffn[1:2])],
            "conv_b": [(all_, vs_mix[0:1])], "b_rgate": heads(vs_mix[1:2]), "b_igate": heads(vs_mix[2:3]),
            "lru_a": [(all_, vs_mix[3:4])], "v_norm_g": [(all_, vs_mix[4:5])], "v_norm_b": [(all_, vs_mix[5:6])],
            "g_lru_out": [(all_, vs_mix[6:7])], "g_gmlp_out": [(all_, vs_mix[7:8])],
            "w_rgate": blocks(dwr), "w_igate": blocks(dwi),
            "w_spatial": [((0, g), dws[g * POS_BLOCK:(g + 1) * POS_BLOCK, :]) for g in range(N_GROUPS)],
            "b_spatial": [((0,), dbs[0:N_GROUPS])],
            "ffn_conv_b": [(all_, cs[FFN_CONV_K:FFN_CONV_K + 1])],
            "conv_w": [((0,), mine(dcw[0:LRU_CONV_K], LRU_W // N_DEV))],
            "ffn_conv_w": [((0,), mine(cs[0:FFN_CONV_K], 2 * D_FF // N_DEV))],
        }
        for n_i, name in enumerate(names):
            w_ref, m_ref, v_ref = p_refs[3 * n_i:3 * n_i + 3]
            go_ref, d_ref, mo_ref, vo_ref = o_refs[4 * n_i:4 * n_i + 4]
            for idx, g in pieces[name]:
                go_ref[idx] = g
                d_ref[idx], mo_ref[idx], vo_ref[idx] = _adam_math(w_ref[idx], g, m_ref[idx], v_ref[idx])

    flat_params = [a for n in names for a in allp[n]]
    out_shape = [_sds(allp[n][0].shape, F32) for n in names for _ in range(4)] + [_sds((1, 1), F32)]
    outs = pl.pallas_call(
        body, name="adamw_small", out_shape=out_shape,
        in_specs=[_whole()] * (n_g + len(flat_params)), out_specs=[_whole()] * len(out_shape),
        compiler_params=_cparams(),
    )(*gathered, *reduced, *flat_params)
    return {n: outs[4 * i:4 * i + 4] for i, n in enumerate(names)}, outs[-1]


def _my_pos():
    return lax.axis_index("x"), lax.axis_index("y"), lax.axis_index("c")


def _flip(pos, k):
    x, y, c = pos
    return (1 - x if k & 4 else x, 1 - y if k & 2 else y, 1 - c if k & 1 else c)


def _dev_index(pos):
    x, y, c = pos
    return 4 * x + 2 * y + c


def _all_gather_small(ins, outs, send_sems, recv_sems):
    n = len(ins)
    me = _my_pos()

    def slot(a, pos):
        rows = ins[a].shape[0]
        return outs[a].at[pl.ds(pl.multiple_of(_dev_index(pos) * rows, SUBLANES), rows), :]

    def copy(a, k, block):
        return pltpu.make_async_remote_copy(
            src_ref=ins[a], dst_ref=slot(a, block), send_sem=send_sems.at[a, k - 1], recv_sem=recv_sems.at[a, k - 1],
            device_id=_flip(me, k), device_id_type=MESH)

    sends = [copy(a, k, me) for a in range(n) for k in range(1, N_DEV)]
    for cp in sends:
        cp.start()
    for a in range(n):
        rows = ins[a].shape[0]
        outs[a][pl.ds(pl.multiple_of(_dev_index(me) * rows, SUBLANES), rows), :] = ins[a][...]
    for a in range(n):
        for k in range(1, N_DEV):
            copy(a, k, _flip(me, k)).wait_recv()
    for cp in sends:
        cp.wait_send()


def _prologue(c8, cw8, fcw8, w_ada, b_ada, carry):
    cols = w_ada.shape[1]

    def body(c_ref, cw_ref, fcw_ref, w_ref, b_ref, call_ref, cwall_ref, fcwall_ref, modall_ref, mod_scr,
             s1, r1, s2, r2, start_carry):
        _all_gather_small([c_ref, cw_ref, fcw_ref], [call_ref, cwall_ref, fcwall_ref], s1, r1)
        start_carry()
        cv = _row_of_each(call_ref, 0)
        ca = cv * _sigmoid(cv)
        b_cols = _my_columns(b_ref[...], cols, _dev_index(_my_pos()))
        mod_scr[...] = jnp.dot(ca, w_ref[...], preferred_element_type=F32, precision=lax.Precision.HIGHEST) + b_cols
        _all_gather_small([mod_scr], [modall_ref], s2, r2)

    sem = lambda n: pltpu.SemaphoreType.DMA((n, N_DEV - 1))
    return _call(
        body, "prologue", (1,), in_specs=[_whole()] * 5, out_specs=[_whole()] * 4,
        out_shape=[_sds((N_DEV * SUBLANES, a.shape[1]), F32) for a in (c8, cw8, fcw8)]
        + [_sds((N_DEV * N_DEV, cols), F32)],
        scratch=[pltpu.VMEM((N_DEV, cols), F32), sem(3), sem(3), sem(1), sem(1)],
        args=(c8, cw8, fcw8, w_ada, b_ada), carry=carry, body_starts_carry=True)


def _reduce_small(gath, red, carry=None):
    n_g, n_r = len(gath), len(red)
    chip_flips = CHIP_FLIPS

    def body(*refs, start_carry):
        g_in, r_in = refs[:n_g], refs[n_g:n_g + n_r]
        g_out, r_out = refs[n_g + n_r:2 * n_g + n_r], refs[2 * n_g + n_r:2 * (n_g + n_r)]
        scr = refs[2 * (n_g + n_r):]
        sib, land = scr[:n_r], scr[n_r:2 * n_r]
        g_send, g_recv, s_send, s_recv, i_send, i_recv, f_send, f_recv = scr[2 * n_r:]
        me = _my_pos()
        c = me[2]
        sibling = _flip(me, 1)

        def slot(a, pos):
            return g_out[a].at[pl.ds(pl.multiple_of(_dev_index(pos) * SUBLANES, SUBLANES), SUBLANES), :]

        def gcopy(a, k):
            return pltpu.make_async_remote_copy(
                src_ref=g_in[a], dst_ref=slot(a, me), send_sem=g_send.at[a, k - 1], recv_sem=g_recv.at[a, k - 1],
                device_id=_flip(me, k), device_id_type=MESH)

        def scopy(a):
            return pltpu.make_async_remote_copy(
                src_ref=r_in[a], dst_ref=sib[a], send_sem=s_send.at[a], recv_sem=s_recv.at[a],
                device_id=sibling, device_id_type=MESH)

        def icopy(a, j):
            return pltpu.make_async_remote_copy(
                src_ref=r_out[a], dst_ref=land[a].at[j], send_sem=i_send.at[a, j], recv_sem=i_recv.at[a, j],
                device_id=_flip(me, chip_flips[j]), device_id_type=MESH)

        def fcopy(a, j):
            return pltpu.make_async_remote_copy(
                src_ref=land[a].at[j], dst_ref=land[a].at[j], send_sem=f_send.at[a, j], recv_sem=f_recv.at[a, j],
                device_id=sibling, device_id_type=MESH)

        gathers = [gcopy(a, k) for a in range(n_g) for k in range(1, N_DEV)]
        swaps = [scopy(a) for a in range(n_r)]
        for cp in gathers + swaps:
            cp.start()
        for a in range(n_g):
            g_out[a][pl.ds(pl.multiple_of(_dev_index(me) * SUBLANES, SUBLANES), SUBLANES), :] = g_in[a][...]
        for a in range(n_r):
            swaps[a].wait_recv()
            r_out[a][...] = r_in[a][...] + sib[a][...]

        for core in range(2):
            @pl.when(c == core)
            def _():
                for a in range(core, n_r, 2):
                    for j in range(3):
                        icopy(a, j).start()

        start_carry()

        for core in range(2):
            mine = [a for a in range(n_r) if a % 2 == core]
            theirs = [a for a in range(n_r) if a % 2 != core]

            @pl.when(c == core)
            def _():
                out = [icopy(a, j) for a in mine for j in range(3)]
                fwd = []
                for a in mine:
                    for j in range(3):
                        icopy(a, j).wait_recv()
                        cp = fcopy(a, j)
                        cp.start()
                        fwd.append(cp)
                for a in theirs:
                    for j in range(3):
                        fcopy(a, j).wait_recv()
                for cp in out + fwd:
                    cp.wait_send()

        for a in range(n_r):
            r_out[a][...] = (r_out[a][...] + land[a][1]) + (land[a][0] + land[a][2])
        for a in range(n_g):
            for k in range(1, N_DEV):
                pltpu.make_async_remote_copy(
                    src_ref=g_in[a], dst_ref=slot(a, _flip(me, k)), send_sem=g_send.at[a, k - 1],
                    recv_sem=g_recv.at[a, k - 1], device_id=_flip(me, k), device_id_type=MESH).wait_recv()
        for cp in gathers + swaps:
            cp.wait_send()

    shapes = [tuple(a.shape) for a in red]
    outs, carried = _call(
        body, "reduce_small", (1,), in_specs=[_whole()] * (n_g + n_r), out_specs=[_whole()] * (n_g + n_r),
        out_shape=[_sds((N_DEV * SUBLANES, a.shape[1]), F32) for a in gath] + [_sds(s, F32) for s in shapes],
        scratch=[pltpu.VMEM(s, F32) for s in shapes] + [pltpu.VMEM((3,) + s, F32) for s in shapes]
        + [pltpu.SemaphoreType.DMA((n_g, N_DEV - 1)), pltpu.SemaphoreType.DMA((n_g, N_DEV - 1)),
           pltpu.SemaphoreType.DMA((n_r,)), pltpu.SemaphoreType.DMA((n_r,)),
           pltpu.SemaphoreType.DMA((n_r, 3)), pltpu.SemaphoreType.DMA((n_r, 3)),
           pltpu.SemaphoreType.DMA((n_r, 3)), pltpu.SemaphoreType.DMA((n_r, 3))],
        args=tuple(gath) + tuple(red), carry=carry, body_starts_carry=True)
    return (outs[:n_g], outs[n_g:]), carried


STACKED = "stacked"


def _region(ref, shard_shape, col_sharded, pos):
    r, cdim = shard_shape
    d = _dev_index(pos)
    if col_sharded == STACKED:
        return ref.at[d]
    if col_sharded:
        return ref.at[:, pl.ds(pl.multiple_of(d * cdim, LANES), cdim)]
    return ref.at[pl.ds(pl.multiple_of(d * r, 2 * SUBLANES), r), :]


def _gather_carry(shards, col_sharded):
    n_w = len(shards)
    shapes = [tuple(s.shape) for s in shards]
    full_shapes = [(N_DEV,) + s if cs == STACKED else (s[0], s[1] * N_DEV) if cs else (s[0] * N_DEV, s[1])
                   for s, cs in zip(shapes, col_sharded)]

    def tools(out_refs, scr):
        send_sems, recv_sems = scr[n_w], scr[n_w + 1]
        me = _my_pos()
        x, y, c = me
        sibling = (x, y, 1 - c)
        chips = [(1 - x, y), (x, 1 - y), (1 - x, 1 - y)]

        def region(w, pos):
            return _region(out_refs[w], shapes[w], col_sharded[w], pos)

        def copy(w, k, block, to, src=None):
            return pltpu.make_async_remote_copy(
                src_ref=region(w, block) if src is None else src, dst_ref=region(w, block),
                send_sem=send_sems.at[w, k], recv_sem=recv_sems.at[w, k], device_id=to, device_id_type=MESH)

        def first(w):
            return [copy(w, 0, me, sibling, src=scr[w])] + [
                copy(w, 1 + j, me, (*chip, c), src=scr[w]) for j, chip in enumerate(chips)]

        def mine(w):
            return pltpu.make_async_copy(scr[w], region(w, me), scr[n_w + 2].at[w])

        return me, c, sibling, chips, copy, first, mine

    def start(ins, outs, scr):
        _, _, _, _, _, first, mine = tools(outs, scr)
        for w in range(n_w):
            scr[w][...] = ins[w][...].astype(BF16)
            for cp in first(w) + [mine(w)]:
                cp.start()

    def finish(ins, outs, scr):
        me, c, sibling, chips, copy, first, mine = tools(outs, scr)
        passed = []
        for w in range(n_w):
            for j, chip in enumerate(chips):
                copy(w, 1 + j, (*chip, c), me).wait_recv()
                fwd = copy(w, 4 + j, (*chip, c), sibling)
                fwd.start()
                passed.append(fwd)
        for w in range(n_w):
            copy(w, 0, sibling, me).wait_recv()
            for j, chip in enumerate(chips):
                copy(w, 4 + j, (*chip, 1 - c), me).wait_recv()
        for w in range(n_w):
            for cp in first(w):
                cp.wait_send()
            mine(w).wait()
        for cp in passed:
            cp.wait_send()

    return _Carry(
        inputs=list(shards), in_specs=[_whole()] * n_w,
        out_shape=[_sds(s, BF16) for s in full_shapes], out_specs=[_any()] * n_w,
        scratch=[pltpu.VMEM(s, BF16) for s in shapes]
        + [pltpu.SemaphoreType.DMA((n_w, N_DEV - 1)), pltpu.SemaphoreType.DMA((n_w, N_DEV - 1)),
           pltpu.SemaphoreType.DMA((n_w,))],
        start=start, finish=finish)


CHIP_FLIPS = (4, 2, 6)


def _pair_reduce(g_bf, g_own, col_sharded):
    shape = tuple(g_own.shape)
    n = len(CHIP_FLIPS)

    def body(g_ref, own_ref, hown_ref, hout_ref, mine, sib, send_sems, recv_sems, local_sems):
        me = _my_pos()
        sibling = _flip(me, 1)
        flips = (0,) + CHIP_FLIPS

        def region(pos):
            return _region(g_ref, shape, col_sharded, pos)

        local = [pltpu.make_async_copy(region(_flip(me, f)), mine.at[s], local_sems.at[s])
                 for s, f in enumerate(CHIP_FLIPS)]
        sends = [pltpu.make_async_remote_copy(
            src_ref=region(_flip(sibling, f)), dst_ref=sib.at[s], send_sem=send_sems.at[s], recv_sem=recv_sems.at[s],
            device_id=sibling, device_id_type=MESH) for s, f in enumerate(flips)]
        for cp in local + sends:
            cp.start()
        for cp in local:
            cp.wait()
        for cp in sends:
            cp.wait_recv()
        hown_ref[...] = own_ref[...] + sib[0].astype(F32)
        for s in range(n):
            hout_ref[s] = (mine[s].astype(F32) + sib[s + 1].astype(F32)).astype(BF16)
        for cp in sends:
            cp.wait_send()

    return pl.pallas_call(
        body, name="pair_reduce", out_shape=[_sds(shape, F32), _sds((n,) + shape, BF16)],
        in_specs=[_any(), _whole()], out_specs=[_whole(), _whole()],
        scratch_shapes=[pltpu.VMEM((n,) + shape, BF16), pltpu.VMEM((n + 1,) + shape, BF16),
                        pltpu.SemaphoreType.DMA((n + 1,)), pltpu.SemaphoreType.DMA((n + 1,)),
                        pltpu.SemaphoreType.DMA((n,))],
        compiler_params=pltpu.CompilerParams(vmem_limit_bytes=VMEM_LIMIT),
    )(g_bf, g_own)


def _chip_scatter_carry(h_out):
    n = len(CHIP_FLIPS)

    def copies(ins, outs, scr):
        send_sems, recv_sems = scr
        me = _my_pos()
        return [pltpu.make_async_remote_copy(
            src_ref=ins[0].at[j], dst_ref=outs[0].at[j], send_sem=send_sems.at[j], recv_sem=recv_sems.at[j],
            device_id=_flip(me, CHIP_FLIPS[j]), device_id_type=MESH) for j in range(n)]

    def start(ins, outs, scr):
        for cp in copies(ins, outs, scr):
            cp.start()

    def finish(ins, outs, scr):
        cps = copies(ins, outs, scr)
        for cp in cps:
            cp.wait_recv()
        for cp in cps:
            cp.wait_send()

    return _Carry(inputs=[h_out], in_specs=[_any()], out_shape=[_sds(tuple(h_out.shape), BF16)], out_specs=[_any()],
                  scratch=[pltpu.SemaphoreType.DMA((n,)), pltpu.SemaphoreType.DMA((n,))], start=start, finish=finish)


def _scatter_carry(grads_bf, shard_shapes, col_sharded, relations):
    n_w = len(grads_bf)
    shapes = [tuple(s) for s in shard_shapes]

    def copies(ins, outs, scr):
        send_sems, recv_sems = scr
        me = _my_pos()
        out = []
        for w in range(n_w):
            for i, k in enumerate(relations[w]):
                peer = _flip(me, k)
                out.append(pltpu.make_async_remote_copy(
                    src_ref=_region(ins[w], shapes[w], col_sharded[w], peer), dst_ref=outs[w].at[i],
                    send_sem=send_sems.at[w, i], recv_sem=recv_sems.at[w, i],
                    device_id=peer, device_id_type=MESH))
        return out

    def start(ins, outs, scr):
        for cp in copies(ins, outs, scr):
            cp.start()

    def finish(ins, outs, scr):
        cps = copies(ins, outs, scr)
        for cp in cps:
            cp.wait_recv()
        for cp in cps:
            cp.wait_send()

    return _Carry(
        inputs=list(grads_bf), in_specs=[_any()] * n_w,
        out_shape=[_sds((len(r),) + s, BF16) for r, s in zip(relations, shapes)], out_specs=[_any()] * n_w,
        scratch=[pltpu.SemaphoreType.DMA((n_w, N_DEV - 1)), pltpu.SemaphoreType.DMA((n_w, N_DEV - 1))],
        start=start, finish=finish)


def _block_diag(w):
    eye = jnp.eye(N_HEADS, dtype=w.dtype)
    return (eye[:, None, :, None] * w[:, :, None, :]).reshape(N_HEADS * HEAD_DIM, N_HEADS * HEAD_DIM)


def _pad_rows(a):
    return jnp.pad(a, ((0, SUBLANES - a.shape[0]), (0, 0)))


def _columns_from_devices(gathered, rows):
    w = gathered.shape[1]
    return gathered.reshape(N_DEV, SUBLANES, w)[:, :rows].transpose(1, 0, 2).reshape(rows, N_DEV * w)


def _local_step(x2, target, mod, w_in_f, w_full, conv_w_full, ffn_cw_full,
                g_mix_pre, g_mix_post, conv_b, w_rgate, b_rgate, w_igate, b_igate, lru_a, v_norm_g, v_norm_b,
                w_spatial, b_spatial, g_lru_out, g_gmlp_out, g_ffn_pre, g_ffn_post, ffn_conv_b,
                gather=None, scatter=None):
    sh_m, sc_m, gt_m, sh_f, sc_f, gt_f = [mod[k] for k in range(N_MOD)]
    wr_bd = _block_diag(w_rgate[0]).astype(BF16)
    wi_bd = _block_diag(w_igate[0]).astype(BF16)
    b_r = b_rgate.reshape(1, LRU_W)
    b_i = b_igate.reshape(1, LRU_W)
    b_sp_t = b_spatial[0].T
    w_sp_t = jnp.swapaxes(w_spatial[0], 1, 2)

    def arriving(*names):
        return gather(*names) if gather else None

    near, far = (1, 2, 3, 4, 5), (6, 7)

    def leaving(*parts):
        return scatter(parts) if scatter else None

    def received(recv, parts, outs):
        for (name, _, _), out in zip(parts, outs):
            recv.setdefault(name, []).append(out)

    mix_params = (conv_w_full, conv_b, wr_bd, wi_bd, b_r, b_i, lru_a, v_norm_g, v_norm_b)
    w_out_f = w_full["w_out"]
    (z, h, ycat, hl, y, x1, h2), got = _mix_fwd(
        x2, sh_m, sc_m, g_mix_pre, w_in_f, *mix_params, w_spatial[0], b_sp_t, g_lru_out, g_gmlp_out,
        w_out_f, g_mix_post, gt_m, g_ffn_pre, sc_f, sh_f, carry=arriving("w_up"))
    w_up_f = got[0] if gather else w_full["w_up"]
    (up_pre, up, act), got = _ffn_fwd(h2, w_up_f, ffn_cw_full, ffn_conv_b, carry=arriving("w_down"))
    w_down_f = got[0] if gather else w_full["w_down"]
    d_y2, dout, loss_acc, vs_ffn = _ffn_tail(act, w_down_f, x1, gt_f, g_ffn_post, target)

    recv = {}
    gw_down, _ = _wgrad(act, d_y2, "wgrad_down", by_rows=True)
    parts = [("w_down", gw_down[0], near + far)]
    (d_up, cs_ffn), got = _ffn_bwd(d_y2, up_pre, up, ffn_cw_full, w_down_f, carry=leaving(*parts))
    received(recv, parts, got)
    gw_up, _ = _wgrad(h2, d_up, "wgrad_up")
    parts = [("w_up", gw_up[0], near)]
    (d_x1, d_y, d_ycat, vs_up), got = _up_bwd(
        d_up, w_up_f, x1, dout, y, w_out_f, g_ffn_pre, sc_f, g_mix_post, gt_m, carry=leaving(*parts))
    received(recv, parts, got)
    gw_out, _ = _wgrad(ycat, d_y, "wgrad_out", by_rows=True)
    parts = [("w_up", gw_up[0], far), ("w_out", gw_out[0], near + far)]
    (d_z, vs_mix, dcw, d_wr, d_wi, d_ws, d_bs), got = _mix_bwd(
        d_ycat, z, hl, *mix_params, w_spatial[0], w_sp_t, b_sp_t, g_lru_out, g_gmlp_out, carry=leaving(*parts))
    received(recv, parts, got)
    gw_in, _ = _wgrad(h, d_z, "wgrad_in")
    pending = None
    if scatter:
        h_own, pending = _pair_reduce(gw_in[0], gw_in[1], True)
        gw_in = (gw_in[0], h_own)
    (grad_x, vs_in), _ = _in_bwd(d_z, w_in_f, x2, d_x1, g_mix_pre, sc_m)
    recv["w_in"] = []

    gath = [vs_in, vs_up, vs_ffn, loss_acc]
    red = [cs_ffn, vs_mix, dcw, d_wr, d_wi, d_ws.reshape(N_GROUPS * POS_BLOCK, POS_BLOCK), d_bs]
    return dict(grad_x=grad_x, gath=gath, red=red, recv=recv, pending=pending,
                w_in=gw_in, w_out=gw_out, w_up=gw_up, w_down=gw_down)


def kernel(x, c, w_ada, b_ada, g_mix_pre, g_mix_post, w_in, conv_w, conv_b, w_rgate, b_rgate, w_igate, b_igate, lru_a, v_norm_g, v_norm_b, w_spatial, b_spatial, g_lru_out, g_gmlp_out, w_out, g_ffn_pre, g_ffn_post, w_up, ffn_conv_w, ffn_conv_b, w_down, loss_target, m_w_ada, m_b_ada, m_g_mix_pre, m_g_mix_post, m_w_in, m_conv_w, m_conv_b, m_w_rgate, m_b_rgate, m_w_igate, m_b_igate, m_lru_a, m_v_norm_g, m_v_norm_b, m_w_spatial, m_b_spatial, m_g_lru_out, m_g_gmlp_out, m_w_out, m_g_ffn_pre, m_g_ffn_post, m_w_up, m_ffn_conv_w, m_ffn_conv_b, m_w_down, v_w_ada, v_b_ada, v_g_mix_pre, v_g_mix_post, v_w_in, v_conv_w, v_conv_b, v_w_rgate, v_b_rgate, v_w_igate, v_b_igate, v_lru_a, v_v_norm_g, v_v_norm_b, v_w_spatial, v_b_spatial, v_g_lru_out, v_g_gmlp_out, v_w_out, v_g_ffn_pre, v_g_ffn_post, v_w_up, v_ffn_conv_w, v_ffn_conv_b, v_w_down):
    me = _dev_index(_my_pos())
    ada_cols = w_ada.shape[-1]

    big_w = dict(w_in=(w_in, m_w_in, v_w_in, True), w_out=(w_out, m_w_out, v_w_out, False),
                 w_up=(w_up, m_w_up, v_w_up, True), w_down=(w_down, m_w_down, v_w_down, False))

    def gather(*names):
        return _gather_carry([big_w[n][0][0] for n in names], [STACKED if n == "w_up" else big_w[n][3] for n in names])

    def scatter(parts):
        return _scatter_carry([g for _, g, _ in parts], [big_w[n][0].shape[1:] for n, _, _ in parts],
                              [big_w[n][3] for n, _, _ in parts], [rel for _, _, rel in parts])

    (c_all, cw_all, fcw_all, mod_all), (w_in_f, w_out_f) = _prologue(
        jnp.broadcast_to(c, (SUBLANES, D_MODEL)), _pad_rows(conv_w[0]), _pad_rows(ffn_conv_w[0]), w_ada[0], b_ada,
        carry=gather("w_in", "w_out"))
    conv_w_full = _columns_from_devices(cw_all, LRU_CONV_K)
    ffn_cw_full = _columns_from_devices(fcw_all, FFN_CONV_K)
    mod = lax.dynamic_index_in_dim(mod_all.reshape(N_DEV, N_DEV, ada_cols), me, axis=1, keepdims=False)
    mod = mod.reshape(N_MOD, 1, D_MODEL)

    loc = _local_step(x[0], loss_target[0], mod, w_in_f, dict(w_out=w_out_f), conv_w_full, ffn_cw_full,
                      g_mix_pre, g_mix_post, conv_b, w_rgate, b_rgate, w_igate, b_igate, lru_a, v_norm_g, v_norm_b,
                      w_spatial, b_spatial, g_lru_out, g_gmlp_out, g_ffn_pre, g_ffn_post, ffn_conv_b,
                      gather=gather, scatter=scatter)
    grad_x = loc["grad_x"]

    (gathered, reduced), got = _reduce_small(loc["gath"], loc["red"], carry=_chip_scatter_carry(loc["pending"]))
    loc["recv"]["w_in"].append(got[0])

    results = {}
    for name, (w_, m_, v_, cs) in big_w.items():
        results[name] = _adamw_sum(w_, loc[name][1], loc["recv"][name], m_, v_, "adamw_" + name)

    params = dict(
        b_ada=(b_ada, m_b_ada, v_b_ada), g_mix_pre=(g_mix_pre, m_g_mix_pre, v_g_mix_pre),
        g_mix_post=(g_mix_post, m_g_mix_post, v_g_mix_post), conv_b=(conv_b, m_conv_b, v_conv_b),
        w_rgate=(w_rgate, m_w_rgate, v_w_rgate), b_rgate=(b_rgate, m_b_rgate, v_b_rgate),
        w_igate=(w_igate, m_w_igate, v_w_igate), b_igate=(b_igate, m_b_igate, v_b_igate),
        lru_a=(lru_a, m_lru_a, v_lru_a), v_norm_g=(v_norm_g, m_v_norm_g, v_v_norm_g),
        v_norm_b=(v_norm_b, m_v_norm_b, v_v_norm_b), w_spatial=(w_spatial, m_w_spatial, v_w_spatial),
        b_spatial=(b_spatial, m_b_spatial, v_b_spatial), g_lru_out=(g_lru_out, m_g_lru_out, v_g_lru_out),
        g_gmlp_out=(g_gmlp_out, m_g_gmlp_out, v_g_gmlp_out), g_ffn_pre=(g_ffn_pre, m_g_ffn_pre, v_g_ffn_pre),
        g_ffn_post=(g_ffn_post, m_g_ffn_post, v_g_ffn_post), ffn_conv_b=(ffn_conv_b, m_ffn_conv_b, v_ffn_conv_b))
    conv_params = dict(conv_w=(conv_w, m_conv_w, v_conv_w), ffn_conv_w=(ffn_conv_w, m_ffn_conv_w, v_ffn_conv_w))
    small_results, loss = _adamw_small(gathered, reduced, params, conv_params)
    results.update(small_results)
    loss = loss.reshape(())

    results["w_ada"] = _adamw_wada(c_all, gathered[0], gathered[1], gathered[2], w_ada, m_w_ada, v_w_ada)

    order = ["w_ada", "b_ada", "g_mix_pre", "g_mix_post", "w_in", "conv_w", "conv_b", "w_rgate", "b_rgate", "w_igate",
             "b_igate", "lru_a", "v_norm_g", "v_norm_b", "w_spatial", "b_spatial", "g_lru_out", "g_gmlp_out", "w_out",
             "g_ffn_pre", "g_ffn_post", "w_up", "ffn_conv_w", "ffn_conv_b", "w_down"]
    outs = [loss, grad_x[None]]
    for kind in range(4):
        outs += [results[n][kind] for n in order]
    return tuple(outs)
```

```python
import functools

import jax
import jax.numpy as jnp
from jax import lax
from jax.experimental import pallas as pl
from jax.experimental.pallas import tpu as pltpu

F32 = jnp.float32
BF16 = jnp.bfloat16

D_MODEL = 1024
LRU_W = 512
GMLP_W = 512
N_HEADS = 8
HEAD_DIM = 64
N_GROUPS = 4
POS_BLOCK = 128
CHUNK = 64
IN_COLS = 2048
D_FF = 3072
N_MOD = 6
N_DEV = 8
EPS = 1e-6
LRU_C = 8.0
LRU_CONV_K = 4
FFN_CONV_K = 3

ADAM_LR = 0.001
ADAM_B1 = 0.9
ADAM_B2 = 0.999
ADAM_EPS = 1e-08
ADAM_WD = 0.01
ADAM_STEP = 10

LANES = 128
SUBLANES = 8
TT_BIG = 512
TT_MIX = 256
FF_CW = 1024
VMEM_LIMIT = 56 * 1024 * 1024

MESH = pl.DeviceIdType.MESH


def _sds(shape, dtype):
    return jax.ShapeDtypeStruct(shape, dtype)


def _cparams(sem=None):
    return pltpu.CompilerParams(dimension_semantics=sem, vmem_limit_bytes=VMEM_LIMIT)


def _whole():
    return pl.BlockSpec(memory_space=pltpu.VMEM)


def _const(shape):
    nd = len(shape)
    return pl.BlockSpec(shape, lambda *_: (0,) * nd)


def _any():
    return pl.BlockSpec(memory_space=pl.ANY)


class _Carry:
    def __init__(self, inputs, in_specs, out_shape, out_specs, scratch, start=None, finish=None, each=None):
        self.inputs, self.in_specs, self.out_shape, self.out_specs = inputs, in_specs, out_shape, out_specs
        self.scratch, self.start, self.finish, self.each = scratch, start, finish, each


def _call(body, name, grid, in_specs, out_specs, out_shape, scratch, args, carry=None, body_starts_carry=False):
    n_in, n_out, n_scr = len(in_specs), len(out_specs), len(scratch)
    c_in = len(carry.in_specs) if carry else 0
    c_out = len(carry.out_specs) if carry else 0

    def full_body(*refs):
        ins = refs[:n_in]
        c_ins = refs[n_in:n_in + c_in]
        outs = refs[n_in + c_in:n_in + c_in + n_out]
        c_outs = refs[n_in + c_in + n_out:n_in + c_in + n_out + c_out]
        scr = refs[n_in + c_in + n_out + c_out:n_in + c_in + n_out + c_out + n_scr]
        c_scr = refs[n_in + c_in + n_out + c_out + n_scr:]
        if carry:
            first = functools.reduce(lambda a, b: a & b, [pl.program_id(d) == 0 for d in range(len(grid))])
            last = functools.reduce(lambda a, b: a & b, [pl.program_id(d) == g - 1 for d, g in enumerate(grid)])

        if carry and carry.start and not body_starts_carry:
            @pl.when(first)
            def _():
                carry.start(c_ins, c_outs, c_scr)

        if body_starts_carry:
            body(*ins, *outs, *scr, start_carry=(lambda: carry.start(c_ins, c_outs, c_scr)) if carry else (lambda: None))
        else:
            body(*ins, *outs, *scr)
        if carry and carry.each:
            carry.each(c_ins, c_outs, c_scr)
        if carry and carry.finish:
            @pl.when(last)
            def _():
                carry.finish(c_ins, c_outs, c_scr)

    res = pl.pallas_call(
        full_body, name=name, grid=grid,
        in_specs=list(in_specs) + (list(carry.in_specs) if carry else []),
        out_specs=list(out_specs) + (list(carry.out_specs) if carry else []),
        out_shape=list(out_shape) + (list(carry.out_shape) if carry else []),
        scratch_shapes=list(scratch) + (list(carry.scratch) if carry else []),
        compiler_params=_cparams(("arbitrary",) * len(grid)),
    )(*args, *(carry.inputs if carry else []))
    return res[:n_out], res[n_out:]


GELU_C0 = 0.7978845608028654
GELU_C1 = GELU_C0 * 0.044715


def _gelu(x):
    t = jnp.tanh(x * (GELU_C0 + GELU_C1 * (x * x)))
    hx = 0.5 * x
    return hx + hx * t


def _gelu_and_grad(x):
    x2 = x * x
    t = jnp.tanh(x * (GELU_C0 + GELU_C1 * x2))
    hx = 0.5 * x
    g = hx + hx * t
    dg = (0.5 + 0.5 * t) + hx * (1.0 - t * t) * (GELU_C0 + 3.0 * GELU_C1 * x2)
    return g, dg


def _sigmoid(x):
    return 1.0 / (1.0 + jnp.exp(-x))


def _softplus(x):
    return jnp.maximum(x, 0.0) + jnp.log1p(jnp.exp(-jnp.abs(x)))


def _neg_expm1(x):
    series = -x * (1.0 + x * (0.5 + x * (1.0 / 6.0 + x * (1.0 / 24.0 + x * (1.0 / 120.0)))))
    return jnp.where(x > -0.1, series, 1.0 - jnp.exp(x))


def _dot(a, b):
    return jnp.dot(a.astype(BF16), b.astype(BF16), preferred_element_type=F32)


def _dot_nt(a, b):
    return lax.dot_general(a.astype(BF16), b.astype(BF16), (((1,), (1,)), ((), ())), preferred_element_type=F32)


def _dot_tn(a, b):
    return lax.dot_general(a.astype(BF16), b.astype(BF16), (((0,), (0,)), ((), ())), preferred_element_type=F32)


def _rows(shape):
    return lax.broadcasted_iota(jnp.int32, shape, 0)


def _shift_down(cur, prev8, s):
    if s == 0:
        return cur
    n = cur.shape[0]
    r = pltpu.roll(cur, s, 0)
    p = pltpu.roll(prev8, s, 0)
    top = jnp.where(_rows(p.shape) < s, p, r[0:SUBLANES])
    if n == SUBLANES:
        return top
    return jnp.concatenate([top, r[SUBLANES:]], axis=0)


def _shift_up(cur, next8, s):
    if s == 0:
        return cur
    n = cur.shape[0]
    r = pltpu.roll(cur, n - s, 0)
    q = pltpu.roll(next8, SUBLANES - s, 0)
    bot = jnp.where(_rows(q.shape) >= SUBLANES - s, q, r[n - SUBLANES:])
    if n == SUBLANES:
        return bot
    return jnp.concatenate([r[:n - SUBLANES], bot], axis=0)


def _scan_fwd(a, b, h_in):
    n = a.shape[0]
    in_group = _rows(a.shape) & (SUBLANES - 1)
    s = 1
    while s < SUBLANES:
        a_s = pltpu.roll(a, s, 0)
        b_s = pltpu.roll(b, s, 0)
        m = in_group >= s
        b = jnp.where(m, a * b_s + b, b)
        a = jnp.where(m, a * a_s, a)
        s *= 2
    out, carry = [], h_in
    for g in range(n // SUBLANES):
        rows = slice(g * SUBLANES, (g + 1) * SUBLANES)
        h_g = a[rows] * carry + b[rows]
        out.append(h_g)
        carry = h_g[SUBLANES - 1:SUBLANES, :]
    return jnp.concatenate(out, axis=0)


def _scan_rev(a, b, l_in):
    n = a.shape[0]
    in_group = _rows(a.shape) & (SUBLANES - 1)
    s = 1
    while s < SUBLANES:
        a_s = pltpu.roll(a, n - s, 0)
        b_s = pltpu.roll(b, n - s, 0)
        m = in_group < SUBLANES - s
        b = jnp.where(m, b + a * b_s, b)
        a = jnp.where(m, a * a_s, a)
        s *= 2
    out, carry = [], l_in
    for g in reversed(range(n // SUBLANES)):
        rows = slice(g * SUBLANES, (g + 1) * SUBLANES)
        l_g = b[rows] + a[rows] * carry
        out.append(l_g)
        carry = l_g[0:1, :]
    return jnp.concatenate(out[::-1], axis=0)


def _rms(x):
    r = lax.rsqrt(jnp.mean(x * x, axis=-1, keepdims=True) + EPS)
    return x * r, r


def _rms_bwd(d_n, n, r):
    return r * (d_n - n * jnp.mean(d_n * n, axis=-1, keepdims=True))


def _colsum(x):
    return jnp.sum(x, axis=0, keepdims=True)


ROW_PIECE = 256


def _row_pieces(tt):
    return [slice(r, r + min(ROW_PIECE, tt)) for r in range(0, tt, min(ROW_PIECE, tt))]


def _lru_gates(xc, wr_ref, wi_ref, br, bi, sp_a):
    r = _sigmoid(_dot(xc, wr_ref[...]) + br)
    i = _sigmoid(_dot(xc, wi_ref[...]) + bi)
    la = -LRU_C * r * sp_a
    a = jnp.exp(la)
    mult = jnp.sqrt(_neg_expm1(2.0 * la))
    return r, i, a, mult


def _lru_conv(lx, prev8, cw_ref, cb):
    xc = cb + cw_ref[LRU_CONV_K - 1:LRU_CONV_K, :] * lx
    taps = []
    for k in range(LRU_CONV_K - 1):
        tap = _shift_down(lx, prev8, LRU_CONV_K - 1 - k)
        taps.append(tap)
        xc = xc + cw_ref[k:k + 1, :] * tap
    return xc, taps


def _ws_mask(transposed=False):
    i = lax.broadcasted_iota(jnp.int32, (POS_BLOCK, POS_BLOCK), 0)
    j = lax.broadcasted_iota(jnp.int32, (POS_BLOCK, POS_BLOCK), 1)
    if transposed:
        i, j = j, i
    return (j // CHUNK) <= (i // CHUNK)


def _gmlp_v(gv, vg, vb):
    av, dav = _gelu_and_grad(gv)
    mu = jnp.mean(av, axis=-1, keepdims=True)
    cen = av - mu
    rs = lax.rsqrt(jnp.mean(cen * cen, axis=-1, keepdims=True) + EPS)
    vhat = cen * rs
    return vhat * vg + vb, vhat, rs, dav


def _mix_fwd(x, sh, sc, g_pre, w_in, conv_w, conv_b, wr_bd, wi_bd, b_r, b_i, lru_a, vn_g, vn_b, w_sp, b_sp_t,
             g_lru, g_gmlp, w_out, g_post, gt_m, g_ffn_pre, sc_f, sh_f, carry=None):
    s_len = x.shape[0]
    tt = min(TT_MIX, s_len)
    nblk = tt // POS_BLOCK

    def body(x_ref, sh_ref, sc_ref, g_ref, w_ref, cw_ref, cb_ref, wr_ref, wi_ref, br_ref, bi_ref, la_ref, vg_ref,
             vb_ref, ws_ref, bst_ref, gl_ref, gg_ref, wo_ref, gp_ref, gtm_ref, g2_ref, scf_ref, shf_ref,
             z_ref, h_ref, y_ref, hl_ref, yo_ref, x1_ref, h2_ref, prev8, hcar):
        i = pl.program_id(0)

        @pl.when(i == 0)
        def _():
            prev8[...] = jnp.zeros_like(prev8)
            hcar[...] = jnp.zeros_like(hcar)

        n_x, _ = _rms(x_ref[...])
        h = (n_x * g_ref[...] * (1.0 + sc_ref[...]) + sh_ref[...]).astype(BF16)
        h_ref[...] = h
        z_ref[...] = jnp.dot(h, w_ref[...], preferred_element_type=F32)

        lx = z_ref[:, 0:LRU_W]
        gate = z_ref[:, LRU_W:2 * LRU_W]
        gu = z_ref[:, 2 * LRU_W:2 * LRU_W + GMLP_W]
        gv = z_ref[:, 2 * LRU_W + GMLP_W:]

        xc, _ = _lru_conv(lx, prev8[...], cw_ref, cb_ref[...])
        prev8[...] = lx[tt - SUBLANES:]
        sp_a = _softplus(-la_ref[...])
        _, ig, a, mult = _lru_gates(xc, wr_ref, wi_ref, br_ref[...], bi_ref[...], sp_a)
        bx = mult * (ig * xc)
        hl = _scan_fwd(a, bx, hcar[0:1, :])
        hcar[...] = jnp.broadcast_to(hl[tt - 1:tt, :], hcar.shape)
        hl_ref[...] = hl
        y_lru = hl * _gelu(gate)
        n_l, _ = _rms(y_lru)
        y_ref[:, 0:LRU_W] = (n_l * gl_ref[...]).astype(BF16)

        u = _gelu(gu)
        v, _, _, _ = _gmlp_v(gv, vg_ref[...], vb_ref[...])
        mask = _ws_mask()
        sp_parts = []
        for nb in range(nblk):
            row = []
            for g in range(N_GROUPS):
                wsm = jnp.where(mask, ws_ref[g], 0.0)
                vblk = v[nb * POS_BLOCK:(nb + 1) * POS_BLOCK, g * LANES:(g + 1) * LANES]
                row.append(_dot(wsm, vblk) + bst_ref[:, g:g + 1])
            sp_parts.append(jnp.concatenate(row, axis=1))
        sp = jnp.concatenate(sp_parts, axis=0) if nblk > 1 else sp_parts[0]
        n_g, _ = _rms(u * sp)
        y_ref[:, LRU_W:] = (n_g * gg_ref[...]).astype(BF16)

        y = jnp.dot(y_ref[...], wo_ref[...], preferred_element_type=F32)
        yo_ref[...] = y
        n_y, _ = _rms(y)
        x1 = x_ref[...] + gtm_ref[...] * (n_y * gp_ref[...])
        x1_ref[...] = x1
        n1, _ = _rms(x1)
        h2_ref[...] = (n1 * g2_ref[...] * (1.0 + scf_ref[...]) + shf_ref[...]).astype(BF16)

    row = lambda c: pl.BlockSpec((tt, c), lambda i: (i, 0))
    v512 = _const((1, LRU_W))
    vec = _const((1, D_MODEL))
    return _call(
        body, "mix_fwd", (s_len // tt,),
        in_specs=[row(D_MODEL), vec, vec, vec, _whole(),
                  _const((LRU_CONV_K, LRU_W)), v512, _whole(), _whole(), v512, v512, v512, v512, v512,
                  _whole(), _whole(), v512, v512, _whole(), vec, vec, vec, vec, vec],
        out_specs=[row(IN_COLS), row(D_MODEL), row(LRU_W + GMLP_W), row(LRU_W), row(D_MODEL), row(D_MODEL),
                   row(D_MODEL)],
        out_shape=[_sds((s_len, IN_COLS), F32), _sds((s_len, D_MODEL), BF16),
                   _sds((s_len, LRU_W + GMLP_W), BF16), _sds((s_len, LRU_W), F32),
                   _sds((s_len, D_MODEL), F32), _sds((s_len, D_MODEL), F32), _sds((s_len, D_MODEL), BF16)],
        scratch=[pltpu.VMEM((SUBLANES, LRU_W), F32), pltpu.VMEM((SUBLANES, LRU_W), F32)],
        args=(x, sh, sc, g_pre, w_in, conv_w, conv_b, wr_bd, wi_bd, b_r, b_i, lru_a, vn_g, vn_b, w_sp, b_sp_t,
              g_lru, g_gmlp, w_out, g_post, gt_m, g_ffn_pre, sc_f, sh_f), carry=carry)


FF_CHUNKS = N_DEV // 2
FF_CHUNK_W = D_FF // FF_CHUNKS


def _ffn_fwd(h2, w_up3, ffn_cw, ffn_cb, carry=None):
    s_len = h2.shape[0]
    tt = min(TT_BIG, s_len)
    nc, cw = FF_CHUNKS, FF_CHUNK_W

    def body(h2_ref, wu_ref, cwg_ref, cwv_ref, cbg_ref, cbv_ref, up_ref, upc_ref, act_ref, prev):
        i = pl.program_id(0)
        c = pl.program_id(1)

        @pl.when(i == 0)
        def _():
            prev[c] = jnp.zeros((2, SUBLANES, cw), F32)

        h2 = h2_ref[...]
        ug_pre = jnp.dot(h2, wu_ref[c], preferred_element_type=F32)
        uv_pre = jnp.dot(h2, wu_ref[nc + c], preferred_element_type=F32)
        up_ref[0] = ug_pre.astype(BF16)
        up_ref[1] = uv_pre.astype(BF16)
        ug, _ = _ffn_conv(ug_pre, prev[c, 0], cwg_ref, cbg_ref[...])
        uv, _ = _ffn_conv(uv_pre, prev[c, 1], cwv_ref, cbv_ref[...])
        prev[c, 0] = ug_pre[tt - SUBLANES:, :]
        prev[c, 1] = uv_pre[tt - SUBLANES:, :]
        upc_ref[0] = ug
        upc_ref[1] = uv
        act_ref[...] = (_gelu(ug) * uv).astype(BF16)

    chunk2 = pl.BlockSpec((2, tt, cw), lambda i, c: (0, i, c))
    ffn_cb2 = ffn_cb.reshape(1, 2 * D_FF)
    return _call(
        body, "ffn_fwd", (s_len // tt, nc),
        in_specs=[pl.BlockSpec((tt, D_MODEL), lambda i, c: (i, 0)), _whole(),
                  pl.BlockSpec((FFN_CONV_K, cw), lambda i, c: (0, c)),
                  pl.BlockSpec((FFN_CONV_K, cw), lambda i, c: (0, c + nc)),
                  pl.BlockSpec((1, cw), lambda i, c: (0, c)),
                  pl.BlockSpec((1, cw), lambda i, c: (0, c + nc))],
        out_specs=[chunk2, chunk2, pl.BlockSpec((tt, cw), lambda i, c: (i, c))],
        out_shape=[_sds((2, s_len, D_FF), BF16), _sds((2, s_len, D_FF), F32), _sds((s_len, D_FF), BF16)],
        scratch=[pltpu.VMEM((nc, 2, SUBLANES, cw), F32)],
        args=(h2, w_up3, ffn_cw, ffn_cw, ffn_cb2, ffn_cb2), carry=carry)


def _ffn_tail(act, w_down, x1, gt_f, g_post, target):
    s_len = x1.shape[0]
    tt = min(TT_BIG, s_len)

    def body(act_ref, wd_ref, x1_ref, gtf_ref, gp_ref, tg_ref, dy2_ref, dout_ref, loss_ref, vs_ref):
        @pl.when(pl.program_id(0) == 0)
        def _():
            loss_ref[...] = jnp.zeros_like(loss_ref)
            vs_ref[...] = jnp.zeros_like(vs_ref)

        for rows in _row_pieces(tt):
            n2, r2 = _rms(jnp.dot(act_ref[rows, :], wd_ref[...], preferred_element_type=F32))
            out = x1_ref[rows, :] + gtf_ref[...] * (n2 * gp_ref[...])
            err = out - tg_ref[rows, :]
            do = err * (1.0 / D_MODEL)
            dout_ref[rows, :] = do
            loss_ref[...] += jnp.broadcast_to(0.5 * jnp.sum(err * err, keepdims=True) * (1.0 / D_MODEL),
                                              loss_ref.shape)
            vs_ref[0:1, :] += _colsum(do * n2 * gp_ref[...])
            vs_ref[1:2, :] += _colsum(do * gtf_ref[...] * n2)
            dy2_ref[rows, :] = _rms_bwd(do * gtf_ref[...] * gp_ref[...], n2, r2).astype(BF16)

    row = lambda c: pl.BlockSpec((tt, c), lambda i: (i, 0))
    vec = _const((1, D_MODEL))
    outs, _ = _call(
        body, "ffn_tail", (s_len // tt,),
        in_specs=[row(D_FF), _whole(), row(D_MODEL), vec, vec, row(D_MODEL)],
        out_specs=[row(D_MODEL), row(D_MODEL), _const((SUBLANES, LANES)), _const((SUBLANES, D_MODEL))],
        out_shape=[_sds((s_len, D_MODEL), BF16), _sds((s_len, D_MODEL), F32), _sds((SUBLANES, LANES), F32),
                   _sds((SUBLANES, D_MODEL), F32)],
        scratch=[], args=(act, w_down, x1, gt_f, g_post, target))
    return outs


def _ffn_conv(up_pre, prev8, cw_ref, cb):
    up = cb + cw_ref[FFN_CONV_K - 1:FFN_CONV_K, :] * up_pre
    taps = []
    for k in range(FFN_CONV_K - 1):
        tap = _shift_down(up_pre, prev8, FFN_CONV_K - 1 - k)
        taps.append(tap)
        up = up + cw_ref[k:k + 1, :] * tap
    return up, taps


def _ffn_bwd(d_y2, up_pre, up, ffn_cw, w_down, carry=None):
    s_len = d_y2.shape[0]
    tt = min(TT_BIG, s_len)
    nt = s_len // tt
    cw = FF_CW
    nc = D_FF // cw

    def body(dy2_ref, up_ref, upc_ref, cwg_ref, cwv_ref, wd_ref, dup_ref, cs_ref, nxt, cs_acc):
        i = pl.program_id(0)
        c = pl.program_id(1)

        @pl.when(i == 0)
        def _():
            nxt[c] = jnp.zeros((2, SUBLANES, cw), F32)
            cs_acc[c] = jnp.zeros((2, SUBLANES, cw), F32)

        pw = 2 * LANES
        for piece in range(cw // pw):
            cols = slice(piece * pw, (piece + 1) * pw)
            d_act = _dot_nt(dy2_ref[...], wd_ref[pl.ds(pl.multiple_of(c * cw + piece * pw, pw), pw), :])
            uv = upc_ref[1, :, cols]
            gl, dgl = _gelu_and_grad(upc_ref[0, :, cols])
            d_ug = d_act * uv * dgl
            d_uv = d_act * gl
            for half, (d_u, cw_ref) in enumerate(((d_ug, cwg_ref), (d_uv, cwv_ref))):
                nx = nxt[c, half, :, cols]
                x_in = up_ref[half, :, cols].astype(F32)
                d_pre = cw_ref[FFN_CONV_K - 1:FFN_CONV_K, cols] * d_u
                sums = [None] * (FFN_CONV_K + 1)
                sums[FFN_CONV_K - 1] = _colsum(d_u * x_in)
                for k in range(FFN_CONV_K - 1):
                    ahead = _shift_up(d_u, nx, FFN_CONV_K - 1 - k)
                    d_pre = d_pre + cw_ref[k:k + 1, cols] * ahead
                    sums[k] = _colsum(ahead * x_in)
                sums[FFN_CONV_K] = _colsum(d_u)
                pad = jnp.zeros((SUBLANES - FFN_CONV_K - 1, pw), F32)
                cs_acc[c, half, :, cols] += jnp.concatenate(sums + [pad], axis=0)
                nxt[c, half, :, cols] = d_u[0:SUBLANES]
                dup_ref[half, :, cols] = d_pre.astype(BF16)

        for cc in range(nc):
            @pl.when((i == nt - 1) & (c == cc))
            def _():
                cs_ref[:, cc * cw:(cc + 1) * cw] = cs_acc[cc, 0]
                cs_ref[:, D_FF + cc * cw:D_FF + (cc + 1) * cw] = cs_acc[cc, 1]

    row = pl.BlockSpec((tt, D_MODEL), lambda i, c: (nt - 1 - i, 0))
    blk = pl.BlockSpec((2, tt, cw), lambda i, c: (0, nt - 1 - i, c))
    return _call(
        body, "ffn_bwd", (nt, nc),
        in_specs=[row, blk, blk,
                  pl.BlockSpec((FFN_CONV_K, cw), lambda i, c: (0, c)),
                  pl.BlockSpec((FFN_CONV_K, cw), lambda i, c: (0, c + nc)),
                  _whole()],
        out_specs=[blk, _const((SUBLANES, 2 * D_FF))],
        out_shape=[_sds((2, s_len, D_FF), BF16), _sds((SUBLANES, 2 * D_FF), F32)],
        scratch=[pltpu.VMEM((nc, 2, SUBLANES, cw), F32), pltpu.VMEM((nc, 2, SUBLANES, cw), F32)],
        args=(d_y2, up_pre, up, ffn_cw, ffn_cw, w_down), carry=carry)


def _up_bwd(d_up, w_up3, x1, dout, y, w_out, g_pre, sc_f, g_post, gt_m, carry=None):
    s_len = x1.shape[0]
    tt = min(TT_BIG, s_len)

    def body(du_ref, wu_ref, x1_ref, do_ref, y_ref, wo_ref, g2_ref, sc_ref, gp_ref, gt_ref,
             dx1_ref, dy_ref, dyc_ref, vs_ref):
        @pl.when(pl.program_id(0) == 0)
        def _():
            vs_ref[...] = jnp.zeros_like(vs_ref)

        for rows in _row_pieces(tt):
            d_h2 = jnp.zeros((rows.stop - rows.start, D_MODEL), F32)
            for half in range(2):
                for ch in range(FF_CHUNKS):
                    d_h2 = d_h2 + _dot_nt(du_ref[half, rows, ch * FF_CHUNK_W:(ch + 1) * FF_CHUNK_W],
                                          wu_ref[half * FF_CHUNKS + ch])
            n1, r1 = _rms(x1_ref[rows, :])
            ng = n1 * g2_ref[...]
            vs_ref[0:1, :] += _colsum(d_h2)
            vs_ref[1:2, :] += _colsum(d_h2 * ng)
            d_ng = d_h2 * (1.0 + sc_ref[...])
            vs_ref[2:3, :] += _colsum(d_ng * n1)
            d_x1 = do_ref[rows, :] + _rms_bwd(d_ng * g2_ref[...], n1, r1)
            dx1_ref[rows, :] = d_x1
            n_y, r_y = _rms(y_ref[rows, :])
            vs_ref[3:4, :] += _colsum(d_x1 * n_y * gp_ref[...])
            d_on = d_x1 * gt_ref[...]
            vs_ref[4:5, :] += _colsum(d_on * n_y)
            d_y = _rms_bwd(d_on * gp_ref[...], n_y, r_y).astype(BF16)
            dy_ref[rows, :] = d_y
            dyc_ref[rows, :] = _dot_nt(d_y, wo_ref[...])

    row = lambda c: pl.BlockSpec((tt, c), lambda i: (i, 0))
    vec = _const((1, D_MODEL))
    return _call(
        body, "up_bwd", (s_len // tt,),
        in_specs=[pl.BlockSpec((2, tt, D_FF), lambda i: (0, i, 0)), _whole(), row(D_MODEL), row(D_MODEL), row(D_MODEL),
                  _whole(), vec, vec, vec, vec],
        out_specs=[row(D_MODEL), row(D_MODEL), row(LRU_W + GMLP_W), _const((SUBLANES, D_MODEL))],
        out_shape=[_sds((s_len, D_MODEL), F32), _sds((s_len, D_MODEL), BF16), _sds((s_len, LRU_W + GMLP_W), F32),
                   _sds((SUBLANES, D_MODEL), F32)],
        scratch=[], args=(d_up, w_up3, x1, dout, y, w_out, g_pre, sc_f, g_post, gt_m), carry=carry)


def _head_pair_block(hd):
    return (slice((hd // 2) * HEAD_DIM, (hd // 2 + 1) * HEAD_DIM), slice((hd % 2) * HEAD_DIM, (hd % 2 + 1) * HEAD_DIM))


def _mix_bwd(d_ycat, z, hl, conv_w, conv_b, wr_bd, wi_bd, b_r, b_i, lru_a, vn_g, vn_b, w_sp, w_sp_t, b_sp_t,
             g_lru, g_gmlp, carry=None):
    s_len = z.shape[0]
    tt = min(TT_MIX, s_len)
    nt = s_len // tt
    nblk = tt // POS_BLOCK
    hb = tt // SUBLANES

    def body(dyc_ref, z_ref, zh_ref, hl_ref, hh_ref, cw_ref, cb_ref, wr_ref, wi_ref, br_ref, bi_ref, la_ref,
             vg_ref, vb_ref, ws_ref, wst_ref, bst_ref, gl_ref, gg_ref,
             dz_ref, vs_ref, dcw_ref, dwrb_ref, dwib_ref, dws_ref, dbs_ref, nxt_dxc, nxt_a, nxt_lam, dwr_ref, dwi_ref):
        i = pl.program_id(0)
        first_tile = i == nt - 1

        @pl.when(i == 0)
        def _():
            for ref in (vs_ref, dcw_ref, dwr_ref, dwi_ref, dws_ref, dbs_ref, nxt_dxc, nxt_a, nxt_lam):
                ref[...] = jnp.zeros_like(ref)

        lx = z_ref[:, 0:LRU_W]
        gate = z_ref[:, LRU_W:2 * LRU_W]
        gu = z_ref[:, 2 * LRU_W:2 * LRU_W + GMLP_W]
        gv = z_ref[:, 2 * LRU_W + GMLP_W:]
        prev8 = jnp.where(first_tile, 0.0, zh_ref[...])
        hprev8 = jnp.where(first_tile, 0.0, hh_ref[...])

        xc, taps = _lru_conv(lx, prev8, cw_ref, cb_ref[...])
        a_par = la_ref[...]
        sp_a = _softplus(-a_par)
        r, ig, a, mult = _lru_gates(xc, wr_ref, wi_ref, br_ref[...], bi_ref[...], sp_a)
        hl = hl_ref[...]
        h_prev = _shift_down(hl, hprev8, 1)
        ggate, dggate = _gelu_and_grad(gate)
        y_lru = hl * ggate
        n_l, r_l = _rms(y_lru)
        d_nl = dyc_ref[:, 0:LRU_W]
        vs_ref[6:7, :] += _colsum(d_nl * n_l)
        d_yl = _rms_bwd(d_nl * gl_ref[...], n_l, r_l)
        d_hl = d_yl * ggate
        d_gate = d_yl * hl * dggate
        a_up = _shift_up(a, nxt_a[...], 1)
        lam = _scan_rev(a_up, d_hl, nxt_lam[0:1, :])
        nxt_a[...] = jnp.broadcast_to(a[0:1, :], nxt_a.shape)
        nxt_lam[...] = jnp.broadcast_to(lam[0:1, :], nxt_lam.shape)
        ixc = ig * xc
        d_la = lam * h_prev * a - lam * ixc * (a * a) / mult
        d_i = lam * mult * xc
        d_xc = lam * mult * ig
        vs_ref[3:4, :] += _colsum(d_la * r) * (LRU_C * _sigmoid(-a_par))
        d_pr = d_la * (-LRU_C * sp_a) * r * (1.0 - r)
        d_pi = d_i * ig * (1.0 - ig)
        vs_ref[1:2, :] += _colsum(d_pr)
        vs_ref[2:3, :] += _colsum(d_pi)
        dwr_ref[...] += _dot_tn(xc, d_pr)
        dwi_ref[...] += _dot_tn(xc, d_pi)
        d_xc = d_xc + _dot_nt(d_pr, wr_ref[...]) + _dot_nt(d_pi, wi_ref[...])
        vs_ref[0:1, :] += _colsum(d_xc)
        nx = nxt_dxc[...]
        d_lx = cw_ref[LRU_CONV_K - 1:LRU_CONV_K, :] * d_xc
        dcw_ref[LRU_CONV_K - 1:LRU_CONV_K, :] += _colsum(d_xc * lx)
        for k in range(LRU_CONV_K - 1):
            d_lx = d_lx + cw_ref[k:k + 1, :] * _shift_up(d_xc, nx, LRU_CONV_K - 1 - k)
            dcw_ref[k:k + 1, :] += _colsum(d_xc * taps[k])
        nxt_dxc[...] = d_xc[0:SUBLANES]
        dz_ref[:, 0:LRU_W] = d_lx.astype(BF16)
        dz_ref[:, LRU_W:2 * LRU_W] = d_gate.astype(BF16)

        u, du = _gelu_and_grad(gu)
        v, vhat, rs, dav = _gmlp_v(gv, vg_ref[...], vb_ref[...])
        mask = _ws_mask()
        sp_parts = []
        for nb in range(nblk):
            rowp = []
            for g in range(N_GROUPS):
                wsm = jnp.where(mask, ws_ref[g], 0.0)
                vblk = v[nb * POS_BLOCK:(nb + 1) * POS_BLOCK, g * LANES:(g + 1) * LANES]
                rowp.append(_dot(wsm, vblk) + bst_ref[:, g:g + 1])
            sp_parts.append(jnp.concatenate(rowp, axis=1))
        sp = jnp.concatenate(sp_parts, axis=0) if nblk > 1 else sp_parts[0]
        y_g = u * sp
        n_g, r_g = _rms(y_g)
        d_ng = dyc_ref[:, LRU_W:]
        vs_ref[7:8, :] += _colsum(d_ng * n_g)
        d_yg = _rms_bwd(d_ng * gg_ref[...], n_g, r_g)
        d_gu = d_yg * sp * du
        d_sp = d_yg * u
        mask_t = _ws_mask(transposed=True)
        ones8 = jnp.ones((SUBLANES, LANES), F32)
        dv_parts = []
        for nb in range(nblk):
            rowp = []
            for g in range(N_GROUPS):
                rs_, cs_ = slice(nb * POS_BLOCK, (nb + 1) * POS_BLOCK), slice(g * LANES, (g + 1) * LANES)
                dsp_blk = d_sp[rs_, cs_]
                dbs_ref[g:g + 1, :] += lax.dot_general(
                    ones8, dsp_blk, (((1,), (1,)), ((), ())), preferred_element_type=F32,
                    precision=lax.Precision.HIGHEST)[0:1, :]
                dws_ref[g] += _dot_nt(dsp_blk, v[rs_, cs_])
                wsm_t = jnp.where(mask_t, wst_ref[g], 0.0)
                rowp.append(_dot(wsm_t, dsp_blk))
            dv_parts.append(jnp.concatenate(rowp, axis=1))
        d_v = jnp.concatenate(dv_parts, axis=0) if nblk > 1 else dv_parts[0]
        vs_ref[4:5, :] += _colsum(d_v * vhat)
        vs_ref[5:6, :] += _colsum(d_v)
        d_vh = d_v * vg_ref[...]
        d_av = rs * (d_vh - jnp.mean(d_vh, axis=-1, keepdims=True)
                     - vhat * jnp.mean(d_vh * vhat, axis=-1, keepdims=True))
        dz_ref[:, 2 * LRU_W:2 * LRU_W + GMLP_W] = d_gu.astype(BF16)
        dz_ref[:, 2 * LRU_W + GMLP_W:] = (d_av * dav).astype(BF16)

        @pl.when(i == nt - 1)
        def _():
            for hd in range(N_HEADS):
                blk = slice(hd * HEAD_DIM, (hd + 1) * HEAD_DIM)
                dwrb_ref[_head_pair_block(hd)] = dwr_ref[blk, blk]
                dwib_ref[_head_pair_block(hd)] = dwi_ref[blk, blk]
            for g in range(N_GROUPS):
                dws_ref[g] = jnp.where(mask, dws_ref[g], 0.0)

    rev = lambda c: pl.BlockSpec((tt, c), lambda i: (nt - 1 - i, 0))
    halo = pl.BlockSpec((SUBLANES, LRU_W), lambda i: (jnp.maximum((nt - 1 - i) * hb - 1, 0), 0))
    v512 = _const((1, LRU_W))
    return _call(
        body, "mix_bwd", (nt,),
        in_specs=[rev(LRU_W + GMLP_W), rev(IN_COLS), halo, rev(LRU_W), halo,
                  _const((LRU_CONV_K, LRU_W)), v512, _whole(), _whole(), v512, v512, v512, v512, v512,
                  _whole(), _whole(), _whole(), v512, v512],
        out_specs=[rev(IN_COLS), _const((SUBLANES, LRU_W)), _const((SUBLANES, LRU_W)),
                   _const((LRU_W // 2, 2 * HEAD_DIM)), _const((LRU_W // 2, 2 * HEAD_DIM)),
                   _const((N_GROUPS, POS_BLOCK, POS_BLOCK)), _const((SUBLANES, POS_BLOCK))],
        out_shape=[_sds((s_len, IN_COLS), BF16), _sds((SUBLANES, LRU_W), F32), _sds((SUBLANES, LRU_W), F32),
                   _sds((LRU_W // 2, 2 * HEAD_DIM), F32), _sds((LRU_W // 2, 2 * HEAD_DIM), F32),
                   _sds((N_GROUPS, POS_BLOCK, POS_BLOCK), F32), _sds((SUBLANES, POS_BLOCK), F32)],
        scratch=[pltpu.VMEM((SUBLANES, LRU_W), F32), pltpu.VMEM((SUBLANES, LRU_W), F32),
                 pltpu.VMEM((SUBLANES, LRU_W), F32), pltpu.VMEM((LRU_W, LRU_W), F32), pltpu.VMEM((LRU_W, LRU_W), F32)],
        args=(d_ycat, z, z, hl, hl, conv_w, conv_b, wr_bd, wi_bd, b_r, b_i, lru_a, vn_g, vn_b, w_sp, w_sp_t, b_sp_t,
              g_lru, g_gmlp), carry=carry)


def _in_bwd(d_z, w_in, x, d_x1, g, sc, carry=None):
    s_len = x.shape[0]
    tt = min(TT_BIG, s_len)

    def body(dz_ref, w_ref, x_ref, dx1_ref, g_ref, sc_ref, gx_ref, vs_ref):
        @pl.when(pl.program_id(0) == 0)
        def _():
            vs_ref[...] = jnp.zeros_like(vs_ref)

        for rows in _row_pieces(tt):
            d_h = _dot_nt(dz_ref[rows, :], w_ref[...])
            n, r = _rms(x_ref[rows, :])
            vs_ref[0:1, :] += _colsum(d_h)
            vs_ref[1:2, :] += _colsum(d_h * n * g_ref[...])
            d_ng = d_h * (1.0 + sc_ref[...])
            vs_ref[2:3, :] += _colsum(d_ng * n)
            gx_ref[rows, :] = dx1_ref[rows, :] + _rms_bwd(d_ng * g_ref[...], n, r)

    row = lambda c: pl.BlockSpec((tt, c), lambda i: (i, 0))
    vec = _const((1, D_MODEL))
    return _call(
        body, "in_bwd", (s_len // tt,),
        in_specs=[row(IN_COLS), _whole(), row(D_MODEL), row(D_MODEL), vec, vec],
        out_specs=[row(D_MODEL), _const((SUBLANES, D_MODEL))],
        out_shape=[_sds((s_len, D_MODEL), F32), _sds((SUBLANES, D_MODEL), F32)],
        scratch=[], args=(d_z, w_in, x, d_x1, g, sc), carry=carry)


def _wgrad(a, b, name, by_rows=False, carry=None):
    s_len, k_dim = a.shape
    halves = b.ndim == 3
    n_dim = b.shape[-1] * (2 if halves else 1)

    def body(a_ref, b_ref, ob_ref, own_ref):
        out = _dot_tn(a_ref[...], b_ref[0] if halves else b_ref[...])
        ob_ref[...] = out.astype(BF16)

        @pl.when(pl.program_id(0) == _dev_index(_my_pos()))
        def _():
            own_ref[...] = out

    if by_rows:
        tile = k_dim // N_DEV
        a_spec = pl.BlockSpec((s_len, tile), lambda j: (0, j))
        b_spec = pl.BlockSpec((s_len, n_dim), lambda j: (0, 0))
        o_spec = pl.BlockSpec((tile, n_dim), lambda j: (j, 0))
        own_shape = (tile, n_dim)
    else:
        tile = n_dim // N_DEV
        a_spec = pl.BlockSpec((s_len, k_dim), lambda j: (0, 0))
        if halves:
            per_half = N_DEV // 2
            b_spec = pl.BlockSpec((1, s_len, tile), lambda j: (j // per_half, 0, j % per_half))
        else:
            b_spec = pl.BlockSpec((s_len, tile), lambda j: (0, j))
        o_spec = pl.BlockSpec((k_dim, tile), lambda j: (0, j))
        own_shape = (k_dim, tile)
    return _call(
        body, name, (N_DEV,), in_specs=[a_spec, b_spec], out_specs=[o_spec, _const(own_shape)],
        out_shape=[_sds((k_dim, n_dim), BF16), _sds(own_shape, F32)],
        scratch=[], args=(a, b), carry=carry)


def _adam_math(w, g, m, v):
    m = ADAM_B1 * m + (1.0 - ADAM_B1) * g
    v = ADAM_B2 * v + (1.0 - ADAM_B2) * (g * g)
    m_hat = m / (1.0 - ADAM_B1 ** ADAM_STEP)
    v_hat = v / (1.0 - ADAM_B2 ** ADAM_STEP)
    delta = -ADAM_LR * (m_hat / (jnp.sqrt(v_hat) + ADAM_EPS) + ADAM_WD * w)
    return delta, m, v


def _row_tile(rows, cols, n_f32_arrays):
    budget = VMEM_LIMIT // 2
    tr = rows
    while tr % 2 == 0 and tr // 2 >= SUBLANES and (tr // 2) % SUBLANES == 0 and tr * cols * 4 * n_f32_arrays * 2 > budget:
        tr //= 2
    return tr


def _adamw_sum_block(w_ref, g_ref, r_refs, m_ref, v_ref, go_ref, d_ref, mo_ref, vo_ref):
    g = g_ref[...]
    for r_ref in r_refs:
        for k in range(r_ref.shape[0]):
            g = g + r_ref[k].astype(F32)
    go_ref[0] = g
    d_ref[0], mo_ref[0], vo_ref[0] = _adam_math(w_ref[0], g, m_ref[0], v_ref[0])


def _adamw_rider(parts, steps):
    inputs, in_specs, out_shape, out_specs, n_recvs = [], [], [], [], []
    for w, g_own, recv, m, v in parts:
        _, rows, cols = w.shape
        tr = rows // steps
        blk = pl.BlockSpec((1, tr, cols), lambda i: (0, i, 0))
        inputs += [w, g_own, *recv, m, v]
        in_specs += ([blk, pl.BlockSpec((tr, cols), lambda i: (i, 0))]
                     + [pl.BlockSpec((r.shape[0], tr, cols), lambda i: (0, i, 0)) for r in recv] + [blk, blk])
        out_shape += [_sds((1, rows, cols), F32)] * 4
        out_specs += [blk] * 4
        n_recvs.append(len(recv))

    def each(ins, outs, scr):
        for n_recv in n_recvs:
            _adamw_sum_block(ins[0], ins[1], ins[2:2 + n_recv], ins[2 + n_recv], ins[3 + n_recv], *outs[:4])
            ins, outs = ins[4 + n_recv:], outs[4:]

    return _Carry(inputs=inputs, in_specs=in_specs, out_shape=out_shape, out_specs=out_specs, scratch=[], each=each)


def _adamw_sum(w, g_own, recv, m, v, name):
    _, rows, cols = w.shape
    n_recv = len(recv)
    tr = _row_tile(rows, cols, 10)
    nb = rows // tr

    def body(w_ref, g_ref, *rest):
        _adamw_sum_block(w_ref, g_ref, rest[:n_recv], *rest[n_recv:])

    blk = pl.BlockSpec((1, tr, cols), lambda i: (0, i, 0))
    return pl.pallas_call(
        body, name=name, grid=(nb,),
        in_specs=[blk, pl.BlockSpec((tr, cols), lambda i: (i, 0))]
        + [pl.BlockSpec((r.shape[0], tr, cols), lambda i: (0, i, 0)) for r in recv] + [blk, blk],
        out_specs=[blk] * 4, out_shape=[_sds((1, rows, cols), F32)] * 4,
        compiler_params=_cparams(("arbitrary",)),
    )(w, g_own, *recv, m, v)


def _row_of_each(ref, row):
    cols = ref.shape[1]
    rows = _rows((N_DEV, cols))
    out = jnp.zeros((N_DEV, cols), F32)
    for d in range(N_DEV):
        picked = ref[d * SUBLANES + row:d * SUBLANES + row + 1, :]
        out = jnp.where(rows == d, jnp.broadcast_to(picked, (N_DEV, cols)), out)
    return out


def _my_columns(full, width, me):
    out = jnp.zeros(full.shape[:-1] + (width,), F32)
    for d in range(N_DEV):
        out = out + jnp.where(me == d, full[:, d * width:(d + 1) * width], 0.0)
    return out


def _adamw_wada(c_all, vs_in_all, vs_up_all, vs_ffn_all, w, m, v):
    _, rows, cols = w.shape

    def body(c_ref, vi_ref, vu_ref, vf_ref, w_ref, m_ref, v_ref, go_ref, d_ref, mo_ref, vo_ref):
        me = _dev_index(_my_pos())
        cv = _row_of_each(c_ref, 0)
        ca = cv * _sigmoid(cv)
        dmod = jnp.concatenate([_row_of_each(vi_ref, 0), _row_of_each(vi_ref, 1), _row_of_each(vu_ref, 3),
                                _row_of_each(vu_ref, 0), _row_of_each(vu_ref, 1), _row_of_each(vf_ref, 0)], axis=1)
        dm = _my_columns(dmod, cols, me)
        g = lax.dot_general(ca, dm, (((0,), (0,)), ((), ())), preferred_element_type=F32,
                            precision=lax.Precision.HIGHEST)
        go_ref[0] = g
        d_ref[0], mo_ref[0], vo_ref[0] = _adam_math(w_ref[0], g, m_ref[0], v_ref[0])

    return pl.pallas_call(
        body, name="adamw_w_ada", out_shape=[_sds((1, rows, cols), F32)] * 4,
        in_specs=[_whole()] * 7, out_specs=[_whole()] * 4,
        compiler_params=_cparams(),
    )(c_all, vs_in_all, vs_up_all, vs_ffn_all, w, m, v)


def _adamw_small(gathered, reduced, params, conv_params):
    names = list(params) + list(conv_params)
    allp = {**params, **conv_params}
    n_g = len(gathered) + len(reduced)

    def body(*refs):
        g_refs = refs[:n_g]
        p_refs = refs[n_g:n_g + 3 * len(names)]
        o_refs = refs[n_g + 3 * len(names):]
        me = _dev_index(_my_pos())

        def total(ref):
            s = ref[0:SUBLANES, :]
            for d in range(1, N_DEV):
                s = s + ref[d * SUBLANES:(d + 1) * SUBLANES, :]
            return s

        vs_in, vs_up, vs_ffn, loss = [total(r) for r in g_refs[:4]]
        cs, vs_mix, dcw, dwr, dwi, dws, dbs = [r[...] for r in g_refs[4:]]
        o_refs[-1][...] = loss[0:1, 0:1]
        mine = lambda full, width: _my_columns(full, width, me)

        all_ = (slice(None), slice(None))
        heads = lambda row: [((0, slice(h, h + 1), slice(None)), row[:, h * HEAD_DIM:(h + 1) * HEAD_DIM])
                             for h in range(N_HEADS)]
        blocks = lambda pairs: [((0, h), pairs[_head_pair_block(h)]) for h in range(N_HEADS)]
        pieces = {
            "b_ada": [((slice(None), slice(k * D_MODEL, (k + 1) * D_MODEL)), row) for k, row in enumerate(
                (vs_in[0:1], vs_in[1:2], vs_up[3:4], vs_up[0:1], vs_up[1:2], vs_ffn[0:1]))],
            "g_mix_pre": [(all_, vs_in[2:3])], "g_mix_post": [(all_, vs_up[4:5])],
            "g_ffn_pre": [(all_, vs_up[2:3])], "g_ffn_post": [(all_, vs_ffn[1:2])],
            "conv_b": [(all_, vs_mix[0:1])], "b_rgate": heads(vs_mix[1:2]), "b_igate": heads(vs_mix[2:3]),
            "lru_a": [(all_, vs_mix[3:4])], "v_norm_g": [(all_, vs_mix[4:5])], "v_norm_b": [(all_, vs_mix[5:6])],
            "g_lru_out": [(all_, vs_mix[6:7])], "g_gmlp_out": [(all_, vs_mix[7:8])],
            "w_rgate": blocks(dwr), "w_igate": blocks(dwi),
            "w_spatial": [((0, g), dws[g * POS_BLOCK:(g + 1) * POS_BLOCK, :]) for g in range(N_GROUPS)],
            "b_spatial": [((0,), dbs[0:N_GROUPS])],
            "ffn_conv_b": [(all_, cs[FFN_CONV_K:FFN_CONV_K + 1])],
            "conv_w": [((0,), mine(dcw[0:LRU_CONV_K], LRU_W // N_DEV))],
            "ffn_conv_w": [((0,), mine(cs[0:FFN_CONV_K], 2 * D_FF // N_DEV))],
        }
        for n_i, name in enumerate(names):
            w_ref, m_ref, v_ref = p_refs[3 * n_i:3 * n_i + 3]
            go_ref, d_ref, mo_ref, vo_ref = o_refs[4 * n_i:4 * n_i + 4]
            for idx, g in pieces[name]:
                go_ref[idx] = g
                d_ref[idx], mo_ref[idx], vo_ref[idx] = _adam_math(w_ref[idx], g, m_ref[idx], v_ref[idx])

    flat_params = [a for n in names for a in allp[n]]
    out_shape = [_sds(allp[n][0].shape, F32) for n in names for _ in range(4)] + [_sds((1, 1), F32)]
    outs = pl.pallas_call(
        body, name="adamw_small", out_shape=out_shape,
        in_specs=[_whole()] * (n_g + len(flat_params)), out_specs=[_whole()] * len(out_shape),
        compiler_params=_cparams(),
    )(*gathered, *reduced, *flat_params)
    return {n: outs[4 * i:4 * i + 4] for i, n in enumerate(names)}, outs[-1]


def _my_pos():
    return lax.axis_index("x"), lax.axis_index("y"), lax.axis_index("c")


def _flip(pos, k):
    x, y, c = pos
    return (1 - x if k & 4 else x, 1 - y if k & 2 else y, 1 - c if k & 1 else c)


def _dev_index(pos):
    x, y, c = pos
    return 4 * x + 2 * y + c


def _all_gather_small(ins, outs, send_sems, recv_sems):
    n = len(ins)
    me = _my_pos()

    def slot(a, pos):
        rows = ins[a].shape[0]
        return outs[a].at[pl.ds(pl.multiple_of(_dev_index(pos) * rows, SUBLANES), rows), :]

    def copy(a, k, block):
        return pltpu.make_async_remote_copy(
            src_ref=ins[a], dst_ref=slot(a, block), send_sem=send_sems.at[a, k - 1], recv_sem=recv_sems.at[a, k - 1],
            device_id=_flip(me, k), device_id_type=MESH)

    sends = [copy(a, k, me) for a in range(n) for k in range(1, N_DEV)]
    for cp in sends:
        cp.start()
    for a in range(n):
        rows = ins[a].shape[0]
        outs[a][pl.ds(pl.multiple_of(_dev_index(me) * rows, SUBLANES), rows), :] = ins[a][...]
    for a in range(n):
        for k in range(1, N_DEV):
            copy(a, k, _flip(me, k)).wait_recv()
    for cp in sends:
        cp.wait_send()


def _prologue(c8, cw8, fcw8, w_ada, b_ada, carry):
    cols = w_ada.shape[1]

    def body(c_ref, cw_ref, fcw_ref, w_ref, b_ref, call_ref, cwall_ref, fcwall_ref, modall_ref, mod_scr,
             s1, r1, s2, r2, start_carry):
        _all_gather_small([c_ref, cw_ref, fcw_ref], [call_ref, cwall_ref, fcwall_ref], s1, r1)
        start_carry()
        cv = _row_of_each(call_ref, 0)
        ca = cv * _sigmoid(cv)
        b_cols = _my_columns(b_ref[...], cols, _dev_index(_my_pos()))
        mod_scr[...] = jnp.dot(ca, w_ref[...], preferred_element_type=F32, precision=lax.Precision.HIGHEST) + b_cols
        _all_gather_small([mod_scr], [modall_ref], s2, r2)

    sem = lambda n: pltpu.SemaphoreType.DMA((n, N_DEV - 1))
    return _call(
        body, "prologue", (1,), in_specs=[_whole()] * 5, out_specs=[_whole()] * 4,
        out_shape=[_sds((N_DEV * SUBLANES, a.shape[1]), F32) for a in (c8, cw8, fcw8)]
        + [_sds((N_DEV * N_DEV, cols), F32)],
        scratch=[pltpu.VMEM((N_DEV, cols), F32), sem(3), sem(3), sem(1), sem(1)],
        args=(c8, cw8, fcw8, w_ada, b_ada), carry=carry, body_starts_carry=True)


def _reduce_small(gath, red, carry=None):
    n_g, n_r = len(gath), len(red)
    chip_flips = CHIP_FLIPS

    def body(*refs, start_carry):
        g_in, r_in = refs[:n_g], refs[n_g:n_g + n_r]
        g_out, r_out = refs[n_g + n_r:2 * n_g + n_r], refs[2 * n_g + n_r:2 * (n_g + n_r)]
        scr = refs[2 * (n_g + n_r):]
        sib, land = scr[:n_r], scr[n_r:2 * n_r]
        g_send, g_recv, s_send, s_recv, i_send, i_recv, f_send, f_recv = scr[2 * n_r:]
        me = _my_pos()
        c = me[2]
        sibling = _flip(me, 1)

        def slot(a, pos):
            return g_out[a].at[pl.ds(pl.multiple_of(_dev_index(pos) * SUBLANES, SUBLANES), SUBLANES), :]

        def gcopy(a, k):
            return pltpu.make_async_remote_copy(
                src_ref=g_in[a], dst_ref=slot(a, me), send_sem=g_send.at[a, k - 1], recv_sem=g_recv.at[a, k - 1],
                device_id=_flip(me, k), device_id_type=MESH)

        def scopy(a):
            return pltpu.make_async_remote_copy(
                src_ref=r_in[a], dst_ref=sib[a], send_sem=s_send.at[a], recv_sem=s_recv.at[a],
                device_id=sibling, device_id_type=MESH)

        def icopy(a, j):
            return pltpu.make_async_remote_copy(
                src_ref=r_out[a], dst_ref=land[a].at[j], send_sem=i_send.at[a, j], recv_sem=i_recv.at[a, j],
                device_id=_flip(me, chip_flips[j]), device_id_type=MESH)

        def fcopy(a, j):
            return pltpu.make_async_remote_copy(
                src_ref=land[a].at[j], dst_ref=land[a].at[j], send_sem=f_send.at[a, j], recv_sem=f_recv.at[a, j],
                device_id=sibling, device_id_type=MESH)

        gathers = [gcopy(a, k) for a in range(n_g) for k in range(1, N_DEV)]
        swaps = [scopy(a) for a in range(n_r)]
        for cp in gathers + swaps:
            cp.start()
        for a in range(n_g):
            g_out[a][pl.ds(pl.multiple_of(_dev_index(me) * SUBLANES, SUBLANES), SUBLANES), :] = g_in[a][...]
        for a in range(n_r):
            swaps[a].wait_recv()
            r_out[a][...] = r_in[a][...] + sib[a][...]

        for core in range(2):
            @pl.when(c == core)
            def _():
                for a in range(core, n_r, 2):
                    for j in range(3):
                        icopy(a, j).start()

        start_carry()

        for core in range(2):
            mine = [a for a in range(n_r) if a % 2 == core]
            theirs = [a for a in range(n_r) if a % 2 != core]

            @pl.when(c == core)
            def _():
                out = [icopy(a, j) for a in mine for j in range(3)]
                fwd = []
                for a in mine:
                    for j in range(3):
                        icopy(a, j).wait_recv()
                        cp = fcopy(a, j)
                        cp.start()
                        fwd.append(cp)
                for a in theirs:
                    for j in range(3):
                        fcopy(a, j).wait_recv()
                for cp in out + fwd:
                    cp.wait_send()

        for a in range(n_r):
            r_out[a][...] = (r_out[a][...] + land[a][1]) + (land[a][0] + land[a][2])
        for a in range(n_g):
            for k in range(1, N_DEV):
                pltpu.make_async_remote_copy(
                    src_ref=g_in[a], dst_ref=slot(a, _flip(me, k)), send_sem=g_send.at[a, k - 1],
                    recv_sem=g_recv.at[a, k - 1], device_id=_flip(me, k), device_id_type=MESH).wait_recv()
        for cp in gathers + swaps:
            cp.wait_send()

    shapes = [tuple(a.shape) for a in red]
    outs, carried = _call(
        body, "reduce_small", (1,), in_specs=[_whole()] * (n_g + n_r), out_specs=[_whole()] * (n_g + n_r),
        out_shape=[_sds((N_DEV * SUBLANES, a.shape[1]), F32) for a in gath] + [_sds(s, F32) for s in shapes],
        scratch=[pltpu.VMEM(s, F32) for s in shapes] + [pltpu.VMEM((3,) + s, F32) for s in shapes]
        + [pltpu.SemaphoreType.DMA((n_g, N_DEV - 1)), pltpu.SemaphoreType.DMA((n_g, N_DEV - 1)),
           pltpu.SemaphoreType.DMA((n_r,)), pltpu.SemaphoreType.DMA((n_r,)),
           pltpu.SemaphoreType.DMA((n_r, 3)), pltpu.SemaphoreType.DMA((n_r, 3)),
           pltpu.SemaphoreType.DMA((n_r, 3)), pltpu.SemaphoreType.DMA((n_r, 3))],
        args=tuple(gath) + tuple(red), carry=carry, body_starts_carry=True)
    return (outs[:n_g], outs[n_g:]), carried


STACKED = "stacked"


def _region(ref, shard_shape, col_sharded, pos):
    r, cdim = shard_shape
    d = _dev_index(pos)
    if col_sharded == STACKED:
        return ref.at[d]
    if col_sharded:
        return ref.at[:, pl.ds(pl.multiple_of(d * cdim, LANES), cdim)]
    return ref.at[pl.ds(pl.multiple_of(d * r, 2 * SUBLANES), r), :]


def _gather_carry(shards, col_sharded):
    n_w = len(shards)
    shapes = [tuple(s.shape) for s in shards]
    full_shapes = [(N_DEV,) + s if cs == STACKED else (s[0], s[1] * N_DEV) if cs else (s[0] * N_DEV, s[1])
                   for s, cs in zip(shapes, col_sharded)]

    def tools(out_refs, scr):
        send_sems, recv_sems = scr[n_w], scr[n_w + 1]
        me = _my_pos()
        x, y, c = me
        sibling = (x, y, 1 - c)
        chips = [(1 - x, y), (x, 1 - y), (1 - x, 1 - y)]

        def region(w, pos):
            return _region(out_refs[w], shapes[w], col_sharded[w], pos)

        def copy(w, k, block, to, src=None):
            return pltpu.make_async_remote_copy(
                src_ref=region(w, block) if src is None else src, dst_ref=region(w, block),
                send_sem=send_sems.at[w, k], recv_sem=recv_sems.at[w, k], device_id=to, device_id_type=MESH)

        def first(w):
            return [copy(w, 0, me, sibling, src=scr[w])] + [
                copy(w, 1 + j, me, (*chip, c), src=scr[w]) for j, chip in enumerate(chips)]

        def mine(w):
            return pltpu.make_async_copy(scr[w], region(w, me), scr[n_w + 2].at[w])

        return me, c, sibling, chips, copy, first, mine

    def start(ins, outs, scr):
        _, _, _, _, _, first, mine = tools(outs, scr)
        for w in range(n_w):
            scr[w][...] = ins[w][...].astype(BF16)
            for cp in first(w) + [mine(w)]:
                cp.start()

    def finish(ins, outs, scr):
        me, c, sibling, chips, copy, first, mine = tools(outs, scr)
        passed = []
        for w in range(n_w):
            for j, chip in enumerate(chips):
                copy(w, 1 + j, (*chip, c), me).wait_recv()
                fwd = copy(w, 4 + j, (*chip, c), sibling)
                fwd.start()
                passed.append(fwd)
        for w in range(n_w):
            copy(w, 0, sibling, me).wait_recv()
            for j, chip in enumerate(chips):
                copy(w, 4 + j, (*chip, 1 - c), me).wait_recv()
        for w in range(n_w):
            for cp in first(w):
                cp.wait_send()
            mine(w).wait()
        for cp in passed:
            cp.wait_send()

    return _Carry(
        inputs=list(shards), in_specs=[_whole()] * n_w,
        out_shape=[_sds(s, BF16) for s in full_shapes], out_specs=[_any()] * n_w,
        scratch=[pltpu.VMEM(s, BF16) for s in shapes]
        + [pltpu.SemaphoreType.DMA((n_w, N_DEV - 1)), pltpu.SemaphoreType.DMA((n_w, N_DEV - 1)),
           pltpu.SemaphoreType.DMA((n_w,))],
        start=start, finish=finish)


CHIP_FLIPS = (4, 2, 6)


def _pair_reduce(g_bf, g_own, col_sharded):
    shape = tuple(g_own.shape)
    n = len(CHIP_FLIPS)

    def body(g_ref, own_ref, hown_ref, hout_ref, mine, sib, send_sems, recv_sems, local_sems):
        me = _my_pos()
        sibling = _flip(me, 1)
        flips = (0,) + CHIP_FLIPS

        def region(pos):
            return _region(g_ref, shape, col_sharded, pos)

        local = [pltpu.make_async_copy(region(_flip(me, f)), mine.at[s], local_sems.at[s])
                 for s, f in enumerate(CHIP_FLIPS)]
        sends = [pltpu.make_async_remote_copy(
            src_ref=region(_flip(sibling, f)), dst_ref=sib.at[s], send_sem=send_sems.at[s], recv_sem=recv_sems.at[s],
            device_id=sibling, device_id_type=MESH) for s, f in enumerate(flips)]
        for cp in local + sends:
            cp.start()
        for cp in local:
            cp.wait()
        for cp in sends:
            cp.wait_recv()
        hown_ref[...] = own_ref[...] + sib[0].astype(F32)
        for s in range(n):
            hout_ref[s] = (mine[s].astype(F32) + sib[s + 1].astype(F32)).astype(BF16)
        for cp in sends:
            cp.wait_send()

    return pl.pallas_call(
        body, name="pair_reduce", out_shape=[_sds(shape, F32), _sds((n,) + shape, BF16)],
        in_specs=[_any(), _whole()], out_specs=[_whole(), _whole()],
        scratch_shapes=[pltpu.VMEM((n,) + shape, BF16), pltpu.VMEM((n + 1,) + shape, BF16),
                        pltpu.SemaphoreType.DMA((n + 1,)), pltpu.SemaphoreType.DMA((n + 1,)),
                        pltpu.SemaphoreType.DMA((n,))],
        compiler_params=pltpu.CompilerParams(vmem_limit_bytes=VMEM_LIMIT),
    )(g_bf, g_own)


def _chip_scatter_carry(h_out):
    n = len(CHIP_FLIPS)

    def copies(ins, outs, scr):
        send_sems, recv_sems = scr
        me = _my_pos()
        return [pltpu.make_async_remote_copy(
            src_ref=ins[0].at[j], dst_ref=outs[0].at[j], send_sem=send_sems.at[j], recv_sem=recv_sems.at[j],
            device_id=_flip(me, CHIP_FLIPS[j]), device_id_type=MESH) for j in range(n)]

    def start(ins, outs, scr):
        for cp in copies(ins, outs, scr):
            cp.start()

    def finish(ins, outs, scr):
        cps = copies(ins, outs, scr)
        for cp in cps:
            cp.wait_recv()
        for cp in cps:
            cp.wait_send()

    return _Carry(inputs=[h_out], in_specs=[_any()], out_shape=[_sds(tuple(h_out.shape), BF16)], out_specs=[_any()],
                  scratch=[pltpu.SemaphoreType.DMA((n,)), pltpu.SemaphoreType.DMA((n,))], start=start, finish=finish)


def _scatter_carry(grads_bf, shard_shapes, col_sharded, relations):
    n_w = len(grads_bf)
    shapes = [tuple(s) for s in shard_shapes]

    def copies(ins, outs, scr):
        send_sems, recv_sems = scr
        me = _my_pos()
        out = []
        for w in range(n_w):
            for i, k in enumerate(relations[w]):
                peer = _flip(me, k)
                out.append(pltpu.make_async_remote_copy(
                    src_ref=_region(ins[w], shapes[w], col_sharded[w], peer), dst_ref=outs[w].at[i],
                    send_sem=send_sems.at[w, i], recv_sem=recv_sems.at[w, i],
                    device_id=peer, device_id_type=MESH))
        return out

    def start(ins, outs, scr):
        for cp in copies(ins, outs, scr):
            cp.start()

    def finish(ins, outs, scr):
        cps = copies(ins, outs, scr)
        for cp in cps:
            cp.wait_recv()
        for cp in cps:
            cp.wait_send()

    return _Carry(
        inputs=list(grads_bf), in_specs=[_any()] * n_w,
        out_shape=[_sds((len(r),) + s, BF16) for r, s in zip(relations, shapes)], out_specs=[_any()] * n_w,
        scratch=[pltpu.SemaphoreType.DMA((n_w, N_DEV - 1)), pltpu.SemaphoreType.DMA((n_w, N_DEV - 1))],
        start=start, finish=finish)


def _block_diag(w):
    eye = jnp.eye(N_HEADS, dtype=w.dtype)
    return (eye[:, None, :, None] * w[:, :, None, :]).reshape(N_HEADS * HEAD_DIM, N_HEADS * HEAD_DIM)


def _pad_rows(a):
    return jnp.pad(a, ((0, SUBLANES - a.shape[0]), (0, 0)))


def _columns_from_devices(gathered, rows):
    w = gathered.shape[1]
    return gathered.reshape(N_DEV, SUBLANES, w)[:, :rows].transpose(1, 0, 2).reshape(rows, N_DEV * w)


def _local_step(x2, target, mod, w_in_f, w_full, conv_w_full, ffn_cw_full,
                g_mix_pre, g_mix_post, conv_b, w_rgate, b_rgate, w_igate, b_igate, lru_a, v_norm_g, v_norm_b,
                w_spatial, b_spatial, g_lru_out, g_gmlp_out, g_ffn_pre, g_ffn_post, ffn_conv_b,
                gather=None, scatter=None, adam=None):
    sh_m, sc_m, gt_m, sh_f, sc_f, gt_f = [mod[k] for k in range(N_MOD)]
    wr_bd = _block_diag(w_rgate[0]).astype(BF16)
    wi_bd = _block_diag(w_igate[0]).astype(BF16)
    b_r = b_rgate.reshape(1, LRU_W)
    b_i = b_igate.reshape(1, LRU_W)
    b_sp_t = b_spatial[0].T
    w_sp_t = jnp.swapaxes(w_spatial[0], 1, 2)

    def arriving(*names):
        return gather(*names) if gather else None

    near, far = (1, 2, 3, 4, 5), (6, 7)

    def leaving(*parts):
        return scatter(parts) if scatter else None

    def received(recv, parts, outs):
        for (name, _, _), out in zip(parts, outs):
            recv.setdefault(name, []).append(out)

    mix_params = (conv_w_full, conv_b, wr_bd, wi_bd, b_r, b_i, lru_a, v_norm_g, v_norm_b)
    w_out_f = w_full["w_out"]
    (z, h, ycat, hl, y, x1, h2), got = _mix_fwd(
        x2, sh_m, sc_m, g_mix_pre, w_in_f, *mix_params, w_spatial[0], b_sp_t, g_lru_out, g_gmlp_out,
        w_out_f, g_mix_post, gt_m, g_ffn_pre, sc_f, sh_f, carry=arriving("w_up"))
    w_up_f = got[0] if gather else w_full["w_up"]
    (up_pre, up, act), got = _ffn_fwd(h2, w_up_f, ffn_cw_full, ffn_conv_b, carry=arriving("w_down"))
    w_down_f = got[0] if gather else w_full["w_down"]
    d_y2, dout, loss_acc, vs_ffn = _ffn_tail(act, w_down_f, x1, gt_f, g_ffn_post, target)

    recv, updated = {}, {}

    def updating(grads):
        if not adam:
            return None
        return _adamw_rider([(adam[n][0], g[1], recv[n], adam[n][1], adam[n][2]) for n, g in grads.items()], N_DEV)

    def updates(grads, outs):
        for j, n in enumerate(grads):
            updated[n] = tuple(outs[4 * j:4 * j + 4])

    gw_down, _ = _wgrad(act, d_y2, "wgrad_down", by_rows=True)
    parts = [("w_down", gw_down[0], near + far)]
    (d_up, cs_ffn), got = _ffn_bwd(d_y2, up_pre, up, ffn_cw_full, w_down_f, carry=leaving(*parts))
    received(recv, parts, got)
    gw_up, got = _wgrad(h2, d_up, "wgrad_up", carry=updating(dict(w_down=gw_down)))
    updates(dict(w_down=gw_down), got)
    parts = [("w_up", gw_up[0], near)]
    (d_x1, d_y, d_ycat, vs_up), got = _up_bwd(
        d_up, w_up_f, x1, dout, y, w_out_f, g_ffn_pre, sc_f, g_mix_post, gt_m, carry=leaving(*parts))
    received(recv, parts, got)
    gw_out, _ = _wgrad(ycat, d_y, "wgrad_out", by_rows=True)
    parts = [("w_up", gw_up[0], far), ("w_out", gw_out[0], near + far)]
    (d_z, vs_mix, dcw, d_wr, d_wi, d_ws, d_bs), got = _mix_bwd(
        d_ycat, z, hl, *mix_params, w_spatial[0], w_sp_t, b_sp_t, g_lru_out, g_gmlp_out, carry=leaving(*parts))
    received(recv, parts, got)
    gw_in, got = _wgrad(h, d_z, "wgrad_in", carry=updating(dict(w_up=gw_up, w_out=gw_out)))
    updates(dict(w_up=gw_up, w_out=gw_out), got)
    pending = None
    if scatter:
        h_own, pending = _pair_reduce(gw_in[0], gw_in[1], True)
        gw_in = (gw_in[0], h_own)
    (grad_x, vs_in), _ = _in_bwd(d_z, w_in_f, x2, d_x1, g_mix_pre, sc_m)
    recv["w_in"] = []

    gath = [vs_in, vs_up, vs_ffn, loss_acc]
    red = [cs_ffn, vs_mix, dcw, d_wr, d_wi, d_ws.reshape(N_GROUPS * POS_BLOCK, POS_BLOCK), d_bs]
    return dict(grad_x=grad_x, gath=gath, red=red, recv=recv, pending=pending, updated=updated,
                w_in=gw_in, w_out=gw_out, w_up=gw_up, w_down=gw_down)


def kernel(x, c, w_ada, b_ada, g_mix_pre, g_mix_post, w_in, conv_w, conv_b, w_rgate, b_rgate, w_igate, b_igate, lru_a, v_norm_g, v_norm_b, w_spatial, b_spatial, g_lru_out, g_gmlp_out, w_out, g_ffn_pre, g_ffn_post, w_up, ffn_conv_w, ffn_conv_b, w_down, loss_target, m_w_ada, m_b_ada, m_g_mix_pre, m_g_mix_post, m_w_in, m_conv_w, m_conv_b, m_w_rgate, m_b_rgate, m_w_igate, m_b_igate, m_lru_a, m_v_norm_g, m_v_norm_b, m_w_spatial, m_b_spatial, m_g_lru_out, m_g_gmlp_out, m_w_out, m_g_ffn_pre, m_g_ffn_post, m_w_up, m_ffn_conv_w, m_ffn_conv_b, m_w_down, v_w_ada, v_b_ada, v_g_mix_pre, v_g_mix_post, v_w_in, v_conv_w, v_conv_b, v_w_rgate, v_b_rgate, v_w_igate, v_b_igate, v_lru_a, v_v_norm_g, v_v_norm_b, v_w_spatial, v_b_spatial, v_g_lru_out, v_g_gmlp_out, v_w_out, v_g_ffn_pre, v_g_ffn_post, v_w_up, v_ffn_conv_w, v_ffn_conv_b, v_w_down):
    me = _dev_index(_my_pos())
    ada_cols = w_ada.shape[-1]

    big_w = dict(w_in=(w_in, m_w_in, v_w_in, True), w_out=(w_out, m_w_out, v_w_out, False),
                 w_up=(w_up, m_w_up, v_w_up, True), w_down=(w_down, m_w_down, v_w_down, False))

    def gather(*names):
        return _gather_carry([big_w[n][0][0] for n in names], [STACKED if n == "w_up" else big_w[n][3] for n in names])

    def scatter(parts):
        return _scatter_carry([g for _, g, _ in parts], [big_w[n][0].shape[1:] for n, _, _ in parts],
                              [big_w[n][3] for n, _, _ in parts], [rel for _, _, rel in parts])

    (c_all, cw_all, fcw_all, mod_all), (w_in_f, w_out_f) = _prologue(
        jnp.broadcast_to(c, (SUBLANES, D_MODEL)), _pad_rows(conv_w[0]), _pad_rows(ffn_conv_w[0]), w_ada[0], b_ada,
        carry=gather("w_in", "w_out"))
    conv_w_full = _columns_from_devices(cw_all, LRU_CONV_K)
    ffn_cw_full = _columns_from_devices(fcw_all, FFN_CONV_K)
    mod = lax.dynamic_index_in_dim(mod_all.reshape(N_DEV, N_DEV, ada_cols), me, axis=1, keepdims=False)
    mod = mod.reshape(N_MOD, 1, D_MODEL)

    loc = _local_step(x[0], loss_target[0], mod, w_in_f, dict(w_out=w_out_f), conv_w_full, ffn_cw_full,
                      g_mix_pre, g_mix_post, conv_b, w_rgate, b_rgate, w_igate, b_igate, lru_a, v_norm_g, v_norm_b,
                      w_spatial, b_spatial, g_lru_out, g_gmlp_out, g_ffn_pre, g_ffn_post, ffn_conv_b,
                      gather=gather, scatter=scatter,
                      adam={n: big_w[n][:3] for n in ("w_out", "w_up", "w_down")})
    grad_x = loc["grad_x"]

    (gathered, reduced), got = _reduce_small(loc["gath"], loc["red"], carry=_chip_scatter_carry(loc["pending"]))
    loc["recv"]["w_in"].append(got[0])

    results = dict(loc["updated"])
    w_, m_, v_, _ = big_w["w_in"]
    results["w_in"] = _adamw_sum(w_, loc["w_in"][1], loc["recv"]["w_in"], m_, v_, "adamw_w_in")

    params = dict(
        b_ada=(b_ada, m_b_ada, v_b_ada), g_mix_pre=(g_mix_pre, m_g_mix_pre, v_g_mix_pre),
        g_mix_post=(g_mix_post, m_g_mix_post, v_g_mix_post), conv_b=(conv_b, m_conv_b, v_conv_b),
        w_rgate=(w_rgate, m_w_rgate, v_w_rgate), b_rgate=(b_rgate, m_b_rgate, v_b_rgate),
        w_igate=(w_igate, m_w_igate, v_w_igate), b_igate=(b_igate, m_b_igate, v_b_igate),
        lru_a=(lru_a, m_lru_a, v_lru_a), v_norm_g=(v_norm_g, m_v_norm_g, v_v_norm_g),
        v_norm_b=(v_norm_b, m_v_norm_b, v_v_norm_b), w_spatial=(w_spatial, m_w_spatial, v_w_spatial),
        b_spatial=(b_spatial, m_b_spatial, v_b_spatial), g_lru_out=(g_lru_out, m_g_lru_out, v_g_lru_out),
        g_gmlp_out=(g_gmlp_out, m_g_gmlp_out, v_g_gmlp_out), g_ffn_pre=(g_ffn_pre, m_g_ffn_pre, v_g_ffn_pre),
        g_ffn_post=(g_ffn_post, m_g_ffn_post, v_g_ffn_post), ffn_conv_b=(ffn_conv_b, m_ffn_conv_b, v_ffn_conv_b))
    conv_params = dict(conv_w=(conv_w, m_conv_w, v_conv_w), ffn_conv_w=(ffn_conv_w, m_ffn_conv_w, v_ffn_conv_w))
    small_results, loss = _adamw_small(gathered, reduced, params, conv_params)
    results.update(small_results)
    loss = loss.reshape(())

    results["w_ada"] = _adamw_wada(c_all, gathered[0], gathered[1], gathered[2], w_ada, m_w_ada, v_w_ada)

    order = ["w_ada", "b_ada", "g_mix_pre", "g_mix_post", "w_in", "conv_w", "conv_b", "w_rgate", "b_rgate", "w_igate",
             "b_igate", "lru_a", "v_norm_g", "v_norm_b", "w_spatial", "b_spatial", "g_lru_out", "g_gmlp_out", "w_out",
             "g_ffn_pre", "g_ffn_post", "w_up", "ffn_conv_w", "ffn_conv_b", "w_down"]
    outs = [loss, grad_x[None]]
    for kind in range(4):
        outs += [results[n][kind] for n in order]
    return tuple(outs)
```

```python
import functools

import jax
import jax.numpy as jnp
from jax import lax
from jax.experimental import pallas as pl
from jax.experimental.pallas import tpu as pltpu

F32 = jnp.float32
BF16 = jnp.bfloat16

D_MODEL = 1024
LRU_W = 512
GMLP_W = 512
N_HEADS = 8
HEAD_DIM = 64
N_GROUPS = 4
POS_BLOCK = 128
CHUNK = 64
IN_COLS = 2048
D_FF = 3072
N_MOD = 6
N_DEV = 8
EPS = 1e-6
LRU_C = 8.0
LRU_CONV_K = 4
FFN_CONV_K = 3

ADAM_LR = 0.001
ADAM_B1 = 0.9
ADAM_B2 = 0.999
ADAM_EPS = 1e-08
ADAM_WD = 0.01
ADAM_STEP = 10

LANES = 128
SUBLANES = 8
TT_BIG = 512
TT_MIX = 256
FF_CW = 1024
VMEM_LIMIT = 56 * 1024 * 1024

MESH = pl.DeviceIdType.MESH


def _sds(shape, dtype):
    return jax.ShapeDtypeStruct(shape, dtype)


def _cparams(sem=None):
    return pltpu.CompilerParams(dimension_semantics=sem, vmem_limit_bytes=VMEM_LIMIT)


def _whole():
    return pl.BlockSpec(memory_space=pltpu.VMEM)


def _const(shape):
    nd = len(shape)
    return pl.BlockSpec(shape, lambda *_: (0,) * nd)


def _any():
    return pl.BlockSpec(memory_space=pl.ANY)


class _Carry:
    def __init__(self, inputs, in_specs, out_shape, out_specs, scratch, start=None, finish=None, each=None):
        self.inputs, self.in_specs, self.out_shape, self.out_specs = inputs, in_specs, out_shape, out_specs
        self.scratch, self.start, self.finish, self.each = scratch, start, finish, each


def _call(body, name, grid, in_specs, out_specs, out_shape, scratch, args, carry=None, body_starts_carry=False):
    n_in, n_out, n_scr = len(in_specs), len(out_specs), len(scratch)
    c_in = len(carry.in_specs) if carry else 0
    c_out = len(carry.out_specs) if carry else 0

    def full_body(*refs):
        ins = refs[:n_in]
        c_ins = refs[n_in:n_in + c_in]
        outs = refs[n_in + c_in:n_in + c_in + n_out]
        c_outs = refs[n_in + c_in + n_out:n_in + c_in + n_out + c_out]
        scr = refs[n_in + c_in + n_out + c_out:n_in + c_in + n_out + c_out + n_scr]
        c_scr = refs[n_in + c_in + n_out + c_out + n_scr:]
        if carry:
            first = functools.reduce(lambda a, b: a & b, [pl.program_id(d) == 0 for d in range(len(grid))])
            last = functools.reduce(lambda a, b: a & b, [pl.program_id(d) == g - 1 for d, g in enumerate(grid)])

        if carry and carry.start and not body_starts_carry:
            @pl.when(first)
            def _():
                carry.start(c_ins, c_outs, c_scr)

        if body_starts_carry:
            body(*ins, *outs, *scr, start_carry=(lambda: carry.start(c_ins, c_outs, c_scr)) if carry else (lambda: None))
        else:
            body(*ins, *outs, *scr)
        if carry and carry.each:
            carry.each(c_ins, c_outs, c_scr)
        if carry and carry.finish:
            @pl.when(last)
            def _():
                carry.finish(c_ins, c_outs, c_scr)

    res = pl.pallas_call(
        full_body, name=name, grid=grid,
        in_specs=list(in_specs) + (list(carry.in_specs) if carry else []),
        out_specs=list(out_specs) + (list(carry.out_specs) if carry else []),
        out_shape=list(out_shape) + (list(carry.out_shape) if carry else []),
        scratch_shapes=list(scratch) + (list(carry.scratch) if carry else []),
        compiler_params=_cparams(("arbitrary",) * len(grid)),
    )(*args, *(carry.inputs if carry else []))
    return res[:n_out], res[n_out:]


GELU_C0 = 0.7978845608028654
GELU_C1 = GELU_C0 * 0.044715


def _gelu(x):
    t = jnp.tanh(x * (GELU_C0 + GELU_C1 * (x * x)))
    hx = 0.5 * x
    return hx + hx * t


def _gelu_and_grad(x):
    x2 = x * x
    t = jnp.tanh(x * (GELU_C0 + GELU_C1 * x2))
    hx = 0.5 * x
    g = hx + hx * t
    dg = (0.5 + 0.5 * t) + hx * (1.0 - t * t) * (GELU_C0 + 3.0 * GELU_C1 * x2)
    return g, dg


def _sigmoid(x):
    return 1.0 / (1.0 + jnp.exp(-x))


def _softplus(x):
    return jnp.maximum(x, 0.0) + jnp.log1p(jnp.exp(-jnp.abs(x)))


def _neg_expm1(x):
    series = -x * (1.0 + x * (0.5 + x * (1.0 / 6.0 + x * (1.0 / 24.0 + x * (1.0 / 120.0)))))
    return jnp.where(x > -0.1, series, 1.0 - jnp.exp(x))


def _dot(a, b):
    return jnp.dot(a.astype(BF16), b.astype(BF16), preferred_element_type=F32)


def _dot_nt(a, b):
    return lax.dot_general(a.astype(BF16), b.astype(BF16), (((1,), (1,)), ((), ())), preferred_element_type=F32)


def _dot_tn(a, b):
    return lax.dot_general(a.astype(BF16), b.astype(BF16), (((0,), (0,)), ((), ())), preferred_element_type=F32)


def _rows(shape):
    return lax.broadcasted_iota(jnp.int32, shape, 0)


def _shift_down(cur, prev8, s):
    if s == 0:
        return cur
    n = cur.shape[0]
    r = pltpu.roll(cur, s, 0)
    p = pltpu.roll(prev8, s, 0)
    top = jnp.where(_rows(p.shape) < s, p, r[0:SUBLANES])
    if n == SUBLANES:
        return top
    return jnp.concatenate([top, r[SUBLANES:]], axis=0)


def _shift_up(cur, next8, s):
    if s == 0:
        return cur
    n = cur.shape[0]
    r = pltpu.roll(cur, n - s, 0)
    q = pltpu.roll(next8, SUBLANES - s, 0)
    bot = jnp.where(_rows(q.shape) >= SUBLANES - s, q, r[n - SUBLANES:])
    if n == SUBLANES:
        return bot
    return jnp.concatenate([r[:n - SUBLANES], bot], axis=0)


def _scan_fwd(a, b, h_in):
    n = a.shape[0]
    in_group = _rows(a.shape) & (SUBLANES - 1)
    s = 1
    while s < SUBLANES:
        a_s = pltpu.roll(a, s, 0)
        b_s = pltpu.roll(b, s, 0)
        m = in_group >= s
        b = jnp.where(m, a * b_s + b, b)
        a = jnp.where(m, a * a_s, a)
        s *= 2
    out, carry = [], h_in
    for g in range(n // SUBLANES):
        rows = slice(g * SUBLANES, (g + 1) * SUBLANES)
        h_g = a[rows] * carry + b[rows]
        out.append(h_g)
        carry = h_g[SUBLANES - 1:SUBLANES, :]
    return jnp.concatenate(out, axis=0)


def _scan_rev(a, b, l_in):
    n = a.shape[0]
    in_group = _rows(a.shape) & (SUBLANES - 1)
    s = 1
    while s < SUBLANES:
        a_s = pltpu.roll(a, n - s, 0)
        b_s = pltpu.roll(b, n - s, 0)
        m = in_group < SUBLANES - s
        b = jnp.where(m, b + a * b_s, b)
        a = jnp.where(m, a * a_s, a)
        s *= 2
    out, carry = [], l_in
    for g in reversed(range(n // SUBLANES)):
        rows = slice(g * SUBLANES, (g + 1) * SUBLANES)
        l_g = b[rows] + a[rows] * carry
        out.append(l_g)
        carry = l_g[0:1, :]
    return jnp.concatenate(out[::-1], axis=0)


def _rms(x):
    r = lax.rsqrt(jnp.mean(x * x, axis=-1, keepdims=True) + EPS)
    return x * r, r


def _rms_bwd(d_n, n, r):
    return r * (d_n - n * jnp.mean(d_n * n, axis=-1, keepdims=True))


def _colsum(x):
    return jnp.sum(x, axis=0, keepdims=True)


ROW_PIECE = 256


def _row_pieces(tt):
    return [slice(r, r + min(ROW_PIECE, tt)) for r in range(0, tt, min(ROW_PIECE, tt))]


def _lru_gates(xc, wr_ref, wi_ref, br, bi, sp_a):
    r = _sigmoid(_dot(xc, wr_ref[...]) + br)
    i = _sigmoid(_dot(xc, wi_ref[...]) + bi)
    la = -LRU_C * r * sp_a
    a = jnp.exp(la)
    mult = jnp.sqrt(_neg_expm1(2.0 * la))
    return r, i, a, mult


def _lru_conv(lx, prev8, cw_ref, cb):
    xc = cb + cw_ref[LRU_CONV_K - 1:LRU_CONV_K, :] * lx
    taps = []
    for k in range(LRU_CONV_K - 1):
        tap = _shift_down(lx, prev8, LRU_CONV_K - 1 - k)
        taps.append(tap)
        xc = xc + cw_ref[k:k + 1, :] * tap
    return xc, taps


def _ws_mask(transposed=False):
    i = lax.broadcasted_iota(jnp.int32, (POS_BLOCK, POS_BLOCK), 0)
    j = lax.broadcasted_iota(jnp.int32, (POS_BLOCK, POS_BLOCK), 1)
    if transposed:
        i, j = j, i
    return (j // CHUNK) <= (i // CHUNK)


def _gmlp_v(gv, vg, vb):
    av, dav = _gelu_and_grad(gv)
    mu = jnp.mean(av, axis=-1, keepdims=True)
    cen = av - mu
    rs = lax.rsqrt(jnp.mean(cen * cen, axis=-1, keepdims=True) + EPS)
    vhat = cen * rs
    return vhat * vg + vb, vhat, rs, dav


def _mix_fwd(x, sh, sc, g_pre, w_in, conv_w, conv_b, wr_bd, wi_bd, b_r, b_i, lru_a, vn_g, vn_b, w_sp, b_sp_t,
             g_lru, g_gmlp, w_out, g_post, gt_m, g_ffn_pre, sc_f, sh_f, carry=None):
    s_len = x.shape[0]
    tt = min(TT_MIX, s_len)
    nblk = tt // POS_BLOCK

    def body(x_ref, sh_ref, sc_ref, g_ref, w_ref, cw_ref, cb_ref, wr_ref, wi_ref, br_ref, bi_ref, la_ref, vg_ref,
             vb_ref, ws_ref, bst_ref, gl_ref, gg_ref, wo_ref, gp_ref, gtm_ref, g2_ref, scf_ref, shf_ref,
             z_ref, h_ref, y_ref, hl_ref, yo_ref, x1_ref, h2_ref, prev8, hcar):
        i = pl.program_id(0)

        @pl.when(i == 0)
        def _():
            prev8[...] = jnp.zeros_like(prev8)
            hcar[...] = jnp.zeros_like(hcar)

        n_x, _ = _rms(x_ref[...])
        h = (n_x * g_ref[...] * (1.0 + sc_ref[...]) + sh_ref[...]).astype(BF16)
        h_ref[...] = h
        z_ref[...] = jnp.dot(h, w_ref[...], preferred_element_type=F32)

        lx = z_ref[:, 0:LRU_W]
        gate = z_ref[:, LRU_W:2 * LRU_W]
        gu = z_ref[:, 2 * LRU_W:2 * LRU_W + GMLP_W]
        gv = z_ref[:, 2 * LRU_W + GMLP_W:]

        xc, _ = _lru_conv(lx, prev8[...], cw_ref, cb_ref[...])
        prev8[...] = lx[tt - SUBLANES:]
        sp_a = _softplus(-la_ref[...])
        _, ig, a, mult = _lru_gates(xc, wr_ref, wi_ref, br_ref[...], bi_ref[...], sp_a)
        bx = mult * (ig * xc)
        hl = _scan_fwd(a, bx, hcar[0:1, :])
        hcar[...] = jnp.broadcast_to(hl[tt - 1:tt, :], hcar.shape)
        hl_ref[...] = hl
        y_lru = hl * _gelu(gate)
        n_l, _ = _rms(y_lru)
        y_ref[:, 0:LRU_W] = (n_l * gl_ref[...]).astype(BF16)

        u = _gelu(gu)
        v, _, _, _ = _gmlp_v(gv, vg_ref[...], vb_ref[...])
        mask = _ws_mask()
        sp_parts = []
        for nb in range(nblk):
            row = []
            for g in range(N_GROUPS):
                wsm = jnp.where(mask, ws_ref[g], 0.0)
                vblk = v[nb * POS_BLOCK:(nb + 1) * POS_BLOCK, g * LANES:(g + 1) * LANES]
                row.append(_dot(wsm, vblk) + bst_ref[:, g:g + 1])
            sp_parts.append(jnp.concatenate(row, axis=1))
        sp = jnp.concatenate(sp_parts, axis=0) if nblk > 1 else sp_parts[0]
        n_g, _ = _rms(u * sp)
        y_ref[:, LRU_W:] = (n_g * gg_ref[...]).astype(BF16)

        y = jnp.dot(y_ref[...], wo_ref[...], preferred_element_type=F32)
        yo_ref[...] = y
        n_y, _ = _rms(y)
        x1 = x_ref[...] + gtm_ref[...] * (n_y * gp_ref[...])
        x1_ref[...] = x1
        n1, _ = _rms(x1)
        h2_ref[...] = (n1 * g2_ref[...] * (1.0 + scf_ref[...]) + shf_ref[...]).astype(BF16)

    row = lambda c: pl.BlockSpec((tt, c), lambda i: (i, 0))
    v512 = _const((1, LRU_W))
    vec = _const((1, D_MODEL))
    return _call(
        body, "mix_fwd", (s_len // tt,),
        in_specs=[row(D_MODEL), vec, vec, vec, _whole(),
                  _const((LRU_CONV_K, LRU_W)), v512, _whole(), _whole(), v512, v512, v512, v512, v512,
                  _whole(), _whole(), v512, v512, _whole(), vec, vec, vec, vec, vec],
        out_specs=[row(IN_COLS), row(D_MODEL), row(LRU_W + GMLP_W), row(LRU_W), row(D_MODEL), row(D_MODEL),
                   row(D_MODEL)],
        out_shape=[_sds((s_len, IN_COLS), F32), _sds((s_len, D_MODEL), BF16),
                   _sds((s_len, LRU_W + GMLP_W), BF16), _sds((s_len, LRU_W), F32),
                   _sds((s_len, D_MODEL), F32), _sds((s_len, D_MODEL), F32), _sds((s_len, D_MODEL), BF16)],
        scratch=[pltpu.VMEM((SUBLANES, LRU_W), F32), pltpu.VMEM((SUBLANES, LRU_W), F32)],
        args=(x, sh, sc, g_pre, w_in, conv_w, conv_b, wr_bd, wi_bd, b_r, b_i, lru_a, vn_g, vn_b, w_sp, b_sp_t,
              g_lru, g_gmlp, w_out, g_post, gt_m, g_ffn_pre, sc_f, sh_f), carry=carry)


FF_CHUNKS = N_DEV // 2
FF_CHUNK_W = D_FF // FF_CHUNKS


def _ffn_fwd(h2, w_up3, ffn_cw, ffn_cb, carry=None):
    s_len = h2.shape[0]
    tt = min(TT_BIG, s_len)
    nc, cw = FF_CHUNKS, FF_CHUNK_W

    def body(h2_ref, wu_ref, cwg_ref, cwv_ref, cbg_ref, cbv_ref, up_ref, upc_ref, act_ref, prev):
        i = pl.program_id(0)
        c = pl.program_id(1)

        @pl.when(i == 0)
        def _():
            prev[c] = jnp.zeros((2, SUBLANES, cw), F32)

        h2 = h2_ref[...]
        ug_pre = jnp.dot(h2, wu_ref[c], preferred_element_type=F32)
        uv_pre = jnp.dot(h2, wu_ref[nc + c], preferred_element_type=F32)
        up_ref[0] = ug_pre.astype(BF16)
        up_ref[1] = uv_pre.astype(BF16)
        ug, _ = _ffn_conv(ug_pre, prev[c, 0], cwg_ref, cbg_ref[...])
        uv, _ = _ffn_conv(uv_pre, prev[c, 1], cwv_ref, cbv_ref[...])
        prev[c, 0] = ug_pre[tt - SUBLANES:, :]
        prev[c, 1] = uv_pre[tt - SUBLANES:, :]
        upc_ref[0] = ug
        upc_ref[1] = uv
        act_ref[...] = (_gelu(ug) * uv).astype(BF16)

    chunk2 = pl.BlockSpec((2, tt, cw), lambda i, c: (0, i, c))
    ffn_cb2 = ffn_cb.reshape(1, 2 * D_FF)
    return _call(
        body, "ffn_fwd", (s_len // tt, nc),
        in_specs=[pl.BlockSpec((tt, D_MODEL), lambda i, c: (i, 0)), _whole(),
                  pl.BlockSpec((FFN_CONV_K, cw), lambda i, c: (0, c)),
                  pl.BlockSpec((FFN_CONV_K, cw), lambda i, c: (0, c + nc)),
                  pl.BlockSpec((1, cw), lambda i, c: (0, c)),
                  pl.BlockSpec((1, cw), lambda i, c: (0, c + nc))],
        out_specs=[chunk2, chunk2, pl.BlockSpec((tt, cw), lambda i, c: (i, c))],
        out_shape=[_sds((2, s_len, D_FF), BF16), _sds((2, s_len, D_FF), F32), _sds((s_len, D_FF), BF16)],
        scratch=[pltpu.VMEM((nc, 2, SUBLANES, cw), F32)],
        args=(h2, w_up3, ffn_cw, ffn_cw, ffn_cb2, ffn_cb2), carry=carry)


def _ffn_tail(act, w_down, x1, gt_f, g_post, target):
    s_len = x1.shape[0]
    tt = min(TT_BIG, s_len)

    def body(act_ref, wd_ref, x1_ref, gtf_ref, gp_ref, tg_ref, dy2_ref, dout_ref, loss_ref, vs_ref):
        @pl.when(pl.program_id(0) == 0)
        def _():
            loss_ref[...] = jnp.zeros_like(loss_ref)
            vs_ref[...] = jnp.zeros_like(vs_ref)

        for rows in _row_pieces(tt):
            n2, r2 = _rms(jnp.dot(act_ref[rows, :], wd_ref[...], preferred_element_type=F32))
            out = x1_ref[rows, :] + gtf_ref[...] * (n2 * gp_ref[...])
            err = out - tg_ref[rows, :]
            do = err * (1.0 / D_MODEL)
            dout_ref[rows, :] = do
            loss_ref[...] += jnp.broadcast_to(0.5 * jnp.sum(err * err, keepdims=True) * (1.0 / D_MODEL),
                                              loss_ref.shape)
            vs_ref[0:1, :] += _colsum(do * n2 * gp_ref[...])
            vs_ref[1:2, :] += _colsum(do * gtf_ref[...] * n2)
            dy2_ref[rows, :] = _rms_bwd(do * gtf_ref[...] * gp_ref[...], n2, r2).astype(BF16)

    row = lambda c: pl.BlockSpec((tt, c), lambda i: (i, 0))
    vec = _const((1, D_MODEL))
    outs, _ = _call(
        body, "ffn_tail", (s_len // tt,),
        in_specs=[row(D_FF), _whole(), row(D_MODEL), vec, vec, row(D_MODEL)],
        out_specs=[row(D_MODEL), row(D_MODEL), _const((SUBLANES, LANES)), _const((SUBLANES, D_MODEL))],
        out_shape=[_sds((s_len, D_MODEL), BF16), _sds((s_len, D_MODEL), F32), _sds((SUBLANES, LANES), F32),
                   _sds((SUBLANES, D_MODEL), F32)],
        scratch=[], args=(act, w_down, x1, gt_f, g_post, target))
    return outs


def _ffn_conv(up_pre, prev8, cw_ref, cb):
    up = cb + cw_ref[FFN_CONV_K - 1:FFN_CONV_K, :] * up_pre
    taps = []
    for k in range(FFN_CONV_K - 1):
        tap = _shift_down(up_pre, prev8, FFN_CONV_K - 1 - k)
        taps.append(tap)
        up = up + cw_ref[k:k + 1, :] * tap
    return up, taps


def _ffn_bwd(d_y2, up_pre, up, ffn_cw, w_down, carry=None):
    s_len = d_y2.shape[0]
    tt = min(TT_BIG, s_len)
    nt = s_len // tt
    cw = FF_CW
    nc = D_FF // cw

    def body(dy2_ref, up_ref, upc_ref, cwg_ref, cwv_ref, wd_ref, dup_ref, cs_ref, nxt, cs_acc):
        i = pl.program_id(0)
        c = pl.program_id(1)

        @pl.when(i == 0)
        def _():
            nxt[c] = jnp.zeros((2, SUBLANES, cw), F32)
            cs_acc[c] = jnp.zeros((2, SUBLANES, cw), F32)

        pw = 2 * LANES
        for piece in range(cw // pw):
            cols = slice(piece * pw, (piece + 1) * pw)
            d_act = _dot_nt(dy2_ref[...], wd_ref[pl.ds(pl.multiple_of(c * cw + piece * pw, pw), pw), :])
            uv = upc_ref[1, :, cols]
            gl, dgl = _gelu_and_grad(upc_ref[0, :, cols])
            d_ug = d_act * uv * dgl
            d_uv = d_act * gl
            for half, (d_u, cw_ref) in enumerate(((d_ug, cwg_ref), (d_uv, cwv_ref))):
                nx = nxt[c, half, :, cols]
                x_in = up_ref[half, :, cols].astype(F32)
                d_pre = cw_ref[FFN_CONV_K - 1:FFN_CONV_K, cols] * d_u
                sums = [None] * (FFN_CONV_K + 1)
                sums[FFN_CONV_K - 1] = _colsum(d_u * x_in)
                for k in range(FFN_CONV_K - 1):
                    ahead = _shift_up(d_u, nx, FFN_CONV_K - 1 - k)
                    d_pre = d_pre + cw_ref[k:k + 1, cols] * ahead
                    sums[k] = _colsum(ahead * x_in)
                sums[FFN_CONV_K] = _colsum(d_u)
                pad = jnp.zeros((SUBLANES - FFN_CONV_K - 1, pw), F32)
                cs_acc[c, half, :, cols] += jnp.concatenate(sums + [pad], axis=0)
                nxt[c, half, :, cols] = d_u[0:SUBLANES]
                dup_ref[half, :, cols] = d_pre.astype(BF16)

        for cc in range(nc):
            @pl.when((i == nt - 1) & (c == cc))
            def _():
                cs_ref[:, cc * cw:(cc + 1) * cw] = cs_acc[cc, 0]
                cs_ref[:, D_FF + cc * cw:D_FF + (cc + 1) * cw] = cs_acc[cc, 1]

    row = pl.BlockSpec((tt, D_MODEL), lambda i, c: (nt - 1 - i, 0))
    blk = pl.BlockSpec((2, tt, cw), lambda i, c: (0, nt - 1 - i, c))
    return _call(
        body, "ffn_bwd", (nt, nc),
        in_specs=[row, blk, blk,
                  pl.BlockSpec((FFN_CONV_K, cw), lambda i, c: (0, c)),
                  pl.BlockSpec((FFN_CONV_K, cw), lambda i, c: (0, c + nc)),
                  _whole()],
        out_specs=[blk, _const((SUBLANES, 2 * D_FF))],
        out_shape=[_sds((2, s_len, D_FF), BF16), _sds((SUBLANES, 2 * D_FF), F32)],
        scratch=[pltpu.VMEM((nc, 2, SUBLANES, cw), F32), pltpu.VMEM((nc, 2, SUBLANES, cw), F32)],
        args=(d_y2, up_pre, up, ffn_cw, ffn_cw, w_down), carry=carry)


def _up_bwd(d_up, w_up3, x1, dout, y, w_out, g_pre, sc_f, g_post, gt_m, carry=None):
    s_len = x1.shape[0]
    tt = min(TT_BIG, s_len)

    def body(du_ref, wu_ref, x1_ref, do_ref, y_ref, wo_ref, g2_ref, sc_ref, gp_ref, gt_ref,
             dx1_ref, dy_ref, dyc_ref, vs_ref):
        @pl.when(pl.program_id(0) == 0)
        def _():
            vs_ref[...] = jnp.zeros_like(vs_ref)

        for rows in _row_pieces(tt):
            d_h2 = jnp.zeros((rows.stop - rows.start, D_MODEL), F32)
            for half in range(2):
                for ch in range(FF_CHUNKS):
                    d_h2 = d_h2 + _dot_nt(du_ref[half, rows, ch * FF_CHUNK_W:(ch + 1) * FF_CHUNK_W],
                                          wu_ref[half * FF_CHUNKS + ch])
            n1, r1 = _rms(x1_ref[rows, :])
            ng = n1 * g2_ref[...]
            vs_ref[0:1, :] += _colsum(d_h2)
            vs_ref[1:2, :] += _colsum(d_h2 * ng)
            d_ng = d_h2 * (1.0 + sc_ref[...])
            vs_ref[2:3, :] += _colsum(d_ng * n1)
            d_x1 = do_ref[rows, :] + _rms_bwd(d_ng * g2_ref[...], n1, r1)
            dx1_ref[rows, :] = d_x1
            n_y, r_y = _rms(y_ref[rows, :])
            vs_ref[3:4, :] += _colsum(d_x1 * n_y * gp_ref[...])
            d_on = d_x1 * gt_ref[...]
            vs_ref[4:5, :] += _colsum(d_on * n_y)
            d_y = _rms_bwd(d_on * gp_ref[...], n_y, r_y).astype(BF16)
            dy_ref[rows, :] = d_y
            dyc_ref[rows, :] = _dot_nt(d_y, wo_ref[...])

    row = lambda c: pl.BlockSpec((tt, c), lambda i: (i, 0))
    vec = _const((1, D_MODEL))
    return _call(
        body, "up_bwd", (s_len // tt,),
        in_specs=[pl.BlockSpec((2, tt, D_FF), lambda i: (0, i, 0)), _whole(), row(D_MODEL), row(D_MODEL), row(D_MODEL),
                  _whole(), vec, vec, vec, vec],
        out_specs=[row(D_MODEL), row(D_MODEL), row(LRU_W + GMLP_W), _const((SUBLANES, D_MODEL))],
        out_shape=[_sds((s_len, D_MODEL), F32), _sds((s_len, D_MODEL), BF16), _sds((s_len, LRU_W + GMLP_W), F32),
                   _sds((SUBLANES, D_MODEL), F32)],
        scratch=[], args=(d_up, w_up3, x1, dout, y, w_out, g_pre, sc_f, g_post, gt_m), carry=carry)


def _head_pair_block(hd):
    return (slice((hd // 2) * HEAD_DIM, (hd // 2 + 1) * HEAD_DIM), slice((hd % 2) * HEAD_DIM, (hd % 2 + 1) * HEAD_DIM))


def _mix_bwd(d_ycat, z, hl, conv_w, conv_b, wr_bd, wi_bd, b_r, b_i, lru_a, vn_g, vn_b, w_sp, w_sp_t, b_sp_t,
             g_lru, g_gmlp, carry=None):
    s_len = z.shape[0]
    tt = min(TT_MIX, s_len)
    nt = s_len // tt
    nblk = tt // POS_BLOCK
    hb = tt // SUBLANES

    def body(dyc_ref, z_ref, zh_ref, hl_ref, hh_ref, cw_ref, cb_ref, wr_ref, wi_ref, br_ref, bi_ref, la_ref,
             vg_ref, vb_ref, ws_ref, wst_ref, bst_ref, gl_ref, gg_ref,
             dz_ref, vs_ref, dcw_ref, dwrb_ref, dwib_ref, dws_ref, dbs_ref, nxt_dxc, nxt_a, nxt_lam, dwr_ref, dwi_ref):
        i = pl.program_id(0)
        first_tile = i == nt - 1

        @pl.when(i == 0)
        def _():
            for ref in (vs_ref, dcw_ref, dwr_ref, dwi_ref, dws_ref, dbs_ref, nxt_dxc, nxt_a, nxt_lam):
                ref[...] = jnp.zeros_like(ref)

        lx = z_ref[:, 0:LRU_W]
        gate = z_ref[:, LRU_W:2 * LRU_W]
        gu = z_ref[:, 2 * LRU_W:2 * LRU_W + GMLP_W]
        gv = z_ref[:, 2 * LRU_W + GMLP_W:]
        prev8 = jnp.where(first_tile, 0.0, zh_ref[...])
        hprev8 = jnp.where(first_tile, 0.0, hh_ref[...])

        xc, taps = _lru_conv(lx, prev8, cw_ref, cb_ref[...])
        a_par = la_ref[...]
        sp_a = _softplus(-a_par)
        r, ig, a, mult = _lru_gates(xc, wr_ref, wi_ref, br_ref[...], bi_ref[...], sp_a)
        hl = hl_ref[...]
        h_prev = _shift_down(hl, hprev8, 1)
        ggate, dggate = _gelu_and_grad(gate)
        y_lru = hl * ggate
        n_l, r_l = _rms(y_lru)
        d_nl = dyc_ref[:, 0:LRU_W]
        vs_ref[6:7, :] += _colsum(d_nl * n_l)
        d_yl = _rms_bwd(d_nl * gl_ref[...], n_l, r_l)
        d_hl = d_yl * ggate
        d_gate = d_yl * hl * dggate
        a_up = _shift_up(a, nxt_a[...], 1)
        lam = _scan_rev(a_up, d_hl, nxt_lam[0:1, :])
        nxt_a[...] = jnp.broadcast_to(a[0:1, :], nxt_a.shape)
        nxt_lam[...] = jnp.broadcast_to(lam[0:1, :], nxt_lam.shape)
        ixc = ig * xc
        d_la = lam * h_prev * a - lam * ixc * (a * a) / mult
        d_i = lam * mult * xc
        d_xc = lam * mult * ig
        vs_ref[3:4, :] += _colsum(d_la * r) * (LRU_C * _sigmoid(-a_par))
        d_pr = d_la * (-LRU_C * sp_a) * r * (1.0 - r)
        d_pi = d_i * ig * (1.0 - ig)
        vs_ref[1:2, :] += _colsum(d_pr)
        vs_ref[2:3, :] += _colsum(d_pi)
        dwr_ref[...] += _dot_tn(xc, d_pr)
        dwi_ref[...] += _dot_tn(xc, d_pi)
        d_xc = d_xc + _dot_nt(d_pr, wr_ref[...]) + _dot_nt(d_pi, wi_ref[...])
        vs_ref[0:1, :] += _colsum(d_xc)
        nx = nxt_dxc[...]
        d_lx = cw_ref[LRU_CONV_K - 1:LRU_CONV_K, :] * d_xc
        dcw_ref[LRU_CONV_K - 1:LRU_CONV_K, :] += _colsum(d_xc * lx)
        for k in range(LRU_CONV_K - 1):
            d_lx = d_lx + cw_ref[k:k + 1, :] * _shift_up(d_xc, nx, LRU_CONV_K - 1 - k)
            dcw_ref[k:k + 1, :] += _colsum(d_xc * taps[k])
        nxt_dxc[...] = d_xc[0:SUBLANES]
        dz_ref[:, 0:LRU_W] = d_lx.astype(BF16)
        dz_ref[:, LRU_W:2 * LRU_W] = d_gate.astype(BF16)

        u, du = _gelu_and_grad(gu)
        v, vhat, rs, dav = _gmlp_v(gv, vg_ref[...], vb_ref[...])
        mask = _ws_mask()
        sp_parts = []
        for nb in range(nblk):
            rowp = []
            for g in range(N_GROUPS):
                wsm = jnp.where(mask, ws_ref[g], 0.0)
                vblk = v[nb * POS_BLOCK:(nb + 1) * POS_BLOCK, g * LANES:(g + 1) * LANES]
                rowp.append(_dot(wsm, vblk) + bst_ref[:, g:g + 1])
            sp_parts.append(jnp.concatenate(rowp, axis=1))
        sp = jnp.concatenate(sp_parts, axis=0) if nblk > 1 else sp_parts[0]
        y_g = u * sp
        n_g, r_g = _rms(y_g)
        d_ng = dyc_ref[:, LRU_W:]
        vs_ref[7:8, :] += _colsum(d_ng * n_g)
        d_yg = _rms_bwd(d_ng * gg_ref[...], n_g, r_g)
        d_gu = d_yg * sp * du
        d_sp = d_yg * u
        mask_t = _ws_mask(transposed=True)
        ones8 = jnp.ones((SUBLANES, LANES), F32)
        dv_parts = []
        for nb in range(nblk):
            rowp = []
            for g in range(N_GROUPS):
                rs_, cs_ = slice(nb * POS_BLOCK, (nb + 1) * POS_BLOCK), slice(g * LANES, (g + 1) * LANES)
                dsp_blk = d_sp[rs_, cs_]
                dbs_ref[g:g + 1, :] += lax.dot_general(
                    ones8, dsp_blk, (((1,), (1,)), ((), ())), preferred_element_type=F32,
                    precision=lax.Precision.HIGHEST)[0:1, :]
                dws_ref[g] += _dot_nt(dsp_blk, v[rs_, cs_])
                wsm_t = jnp.where(mask_t, wst_ref[g], 0.0)
                rowp.append(_dot(wsm_t, dsp_blk))
            dv_parts.append(jnp.concatenate(rowp, axis=1))
        d_v = jnp.concatenate(dv_parts, axis=0) if nblk > 1 else dv_parts[0]
        vs_ref[4:5, :] += _colsum(d_v * vhat)
        vs_ref[5:6, :] += _colsum(d_v)
        d_vh = d_v * vg_ref[...]
        d_av = rs * (d_vh - jnp.mean(d_vh, axis=-1, keepdims=True)
                     - vhat * jnp.mean(d_vh * vhat, axis=-1, keepdims=True))
        dz_ref[:, 2 * LRU_W:2 * LRU_W + GMLP_W] = d_gu.astype(BF16)
        dz_ref[:, 2 * LRU_W + GMLP_W:] = (d_av * dav).astype(BF16)

        @pl.when(i == nt - 1)
        def _():
            for hd in range(N_HEADS):
                blk = slice(hd * HEAD_DIM, (hd + 1) * HEAD_DIM)
                dwrb_ref[_head_pair_block(hd)] = dwr_ref[blk, blk]
                dwib_ref[_head_pair_block(hd)] = dwi_ref[blk, blk]
            for g in range(N_GROUPS):
                dws_ref[g] = jnp.where(mask, dws_ref[g], 0.0)

    rev = lambda c: pl.BlockSpec((tt, c), lambda i: (nt - 1 - i, 0))
    halo = pl.BlockSpec((SUBLANES, LRU_W), lambda i: (jnp.maximum((nt - 1 - i) * hb - 1, 0), 0))
    v512 = _const((1, LRU_W))
    return _call(
        body, "mix_bwd", (nt,),
        in_specs=[rev(LRU_W + GMLP_W), rev(IN_COLS), halo, rev(LRU_W), halo,
                  _const((LRU_CONV_K, LRU_W)), v512, _whole(), _whole(), v512, v512, v512, v512, v512,
                  _whole(), _whole(), _whole(), v512, v512],
        out_specs=[rev(IN_COLS), _const((SUBLANES, LRU_W)), _const((SUBLANES, LRU_W)),
                   _const((LRU_W // 2, 2 * HEAD_DIM)), _const((LRU_W // 2, 2 * HEAD_DIM)),
                   _const((N_GROUPS, POS_BLOCK, POS_BLOCK)), _const((SUBLANES, POS_BLOCK))],
        out_shape=[_sds((s_len, IN_COLS), BF16), _sds((SUBLANES, LRU_W), F32), _sds((SUBLANES, LRU_W), F32),
                   _sds((LRU_W // 2, 2 * HEAD_DIM), F32), _sds((LRU_W // 2, 2 * HEAD_DIM), F32),
                   _sds((N_GROUPS, POS_BLOCK, POS_BLOCK), F32), _sds((SUBLANES, POS_BLOCK), F32)],
        scratch=[pltpu.VMEM((SUBLANES, LRU_W), F32), pltpu.VMEM((SUBLANES, LRU_W), F32),
                 pltpu.VMEM((SUBLANES, LRU_W), F32), pltpu.VMEM((LRU_W, LRU_W), F32), pltpu.VMEM((LRU_W, LRU_W), F32)],
        args=(d_ycat, z, z, hl, hl, conv_w, conv_b, wr_bd, wi_bd, b_r, b_i, lru_a, vn_g, vn_b, w_sp, w_sp_t, b_sp_t,
              g_lru, g_gmlp), carry=carry)


def _in_bwd(d_z, w_in, x, d_x1, g, sc, carry=None):
    s_len = x.shape[0]
    tt = min(TT_BIG, s_len)

    def body(dz_ref, w_ref, x_ref, dx1_ref, g_ref, sc_ref, gx_ref, vs_ref):
        @pl.when(pl.program_id(0) == 0)
        def _():
            vs_ref[...] = jnp.zeros_like(vs_ref)

        for rows in _row_pieces(tt):
            d_h = _dot_nt(dz_ref[rows, :], w_ref[...])
            n, r = _rms(x_ref[rows, :])
            vs_ref[0:1, :] += _colsum(d_h)
            vs_ref[1:2, :] += _colsum(d_h * n * g_ref[...])
            d_ng = d_h * (1.0 + sc_ref[...])
            vs_ref[2:3, :] += _colsum(d_ng * n)
            gx_ref[rows, :] = dx1_ref[rows, :] + _rms_bwd(d_ng * g_ref[...], n, r)

    row = lambda c: pl.BlockSpec((tt, c), lambda i: (i, 0))
    vec = _const((1, D_MODEL))
    return _call(
        body, "in_bwd", (s_len // tt,),
        in_specs=[row(IN_COLS), _whole(), row(D_MODEL), row(D_MODEL), vec, vec],
        out_specs=[row(D_MODEL), _const((SUBLANES, D_MODEL))],
        out_shape=[_sds((s_len, D_MODEL), F32), _sds((SUBLANES, D_MODEL), F32)],
        scratch=[], args=(d_z, w_in, x, d_x1, g, sc), carry=carry)


def _wgrad(a, b, name, by_rows=False, carry=None):
    s_len, k_dim = a.shape
    halves = b.ndim == 3
    n_dim = b.shape[-1] * (2 if halves else 1)

    def body(a_ref, b_ref, ob_ref, own_ref):
        out = _dot_tn(a_ref[...], b_ref[0] if halves else b_ref[...])
        ob_ref[...] = out.astype(BF16)

        @pl.when(pl.program_id(0) == _dev_index(_my_pos()))
        def _():
            own_ref[...] = out

    if by_rows:
        tile = k_dim // N_DEV
        a_spec = pl.BlockSpec((s_len, tile), lambda j: (0, j))
        b_spec = pl.BlockSpec((s_len, n_dim), lambda j: (0, 0))
        o_spec = pl.BlockSpec((tile, n_dim), lambda j: (j, 0))
        own_shape = (tile, n_dim)
    else:
        tile = n_dim // N_DEV
        a_spec = pl.BlockSpec((s_len, k_dim), lambda j: (0, 0))
        if halves:
            per_half = N_DEV // 2
            b_spec = pl.BlockSpec((1, s_len, tile), lambda j: (j // per_half, 0, j % per_half))
        else:
            b_spec = pl.BlockSpec((s_len, tile), lambda j: (0, j))
        o_spec = pl.BlockSpec((k_dim, tile), lambda j: (0, j))
        own_shape = (k_dim, tile)
    return _call(
        body, name, (N_DEV,), in_specs=[a_spec, b_spec], out_specs=[o_spec, _const(own_shape)],
        out_shape=[_sds((k_dim, n_dim), BF16), _sds(own_shape, F32)],
        scratch=[], args=(a, b), carry=carry)


def _adam_math(w, g, m, v):
    m = ADAM_B1 * m + (1.0 - ADAM_B1) * g
    v = ADAM_B2 * v + (1.0 - ADAM_B2) * (g * g)
    m_hat = m / (1.0 - ADAM_B1 ** ADAM_STEP)
    v_hat = v / (1.0 - ADAM_B2 ** ADAM_STEP)
    delta = -ADAM_LR * (m_hat / (jnp.sqrt(v_hat) + ADAM_EPS) + ADAM_WD * w)
    return delta, m, v


def _row_tile(rows, cols, n_f32_arrays):
    budget = VMEM_LIMIT // 2
    tr = rows
    while tr % 2 == 0 and tr // 2 >= SUBLANES and (tr // 2) % SUBLANES == 0 and tr * cols * 4 * n_f32_arrays * 2 > budget:
        tr //= 2
    return tr


def _adamw_sum_block(w_ref, g_ref, r_refs, m_ref, v_ref, go_ref, d_ref, mo_ref, vo_ref):
    g = g_ref[...]
    for r_ref in r_refs:
        for k in range(r_ref.shape[0]):
            g = g + r_ref[k].astype(F32)
    go_ref[0] = g
    d_ref[0], mo_ref[0], vo_ref[0] = _adam_math(w_ref[0], g, m_ref[0], v_ref[0])


def _adamw_rider(parts, steps):
    inputs, in_specs, out_shape, out_specs, n_recvs = [], [], [], [], []
    for w, g_own, recv, m, v in parts:
        _, rows, cols = w.shape
        tr = rows // steps
        blk = pl.BlockSpec((1, tr, cols), lambda i: (0, i, 0))
        inputs += [w, g_own, *recv, m, v]
        in_specs += ([blk, pl.BlockSpec((tr, cols), lambda i: (i, 0))]
                     + [pl.BlockSpec((r.shape[0], tr, cols), lambda i: (0, i, 0)) for r in recv] + [blk, blk])
        out_shape += [_sds((1, rows, cols), F32)] * 4
        out_specs += [blk] * 4
        n_recvs.append(len(recv))

    def each(ins, outs, scr):
        for n_recv in n_recvs:
            _adamw_sum_block(ins[0], ins[1], ins[2:2 + n_recv], ins[2 + n_recv], ins[3 + n_recv], *outs[:4])
            ins, outs = ins[4 + n_recv:], outs[4:]

    return _Carry(inputs=inputs, in_specs=in_specs, out_shape=out_shape, out_specs=out_specs, scratch=[], each=each)


def _adamw_sum(w, g_own, recv, m, v, name):
    _, rows, cols = w.shape
    n_recv = len(recv)
    tr = _row_tile(rows, cols, 10)
    nb = rows // tr

    def body(w_ref, g_ref, *rest):
        _adamw_sum_block(w_ref, g_ref, rest[:n_recv], *rest[n_recv:])

    blk = pl.BlockSpec((1, tr, cols), lambda i: (0, i, 0))
    return pl.pallas_call(
        body, name=name, grid=(nb,),
        in_specs=[blk, pl.BlockSpec((tr, cols), lambda i: (i, 0))]
        + [pl.BlockSpec((r.shape[0], tr, cols), lambda i: (0, i, 0)) for r in recv] + [blk, blk],
        out_specs=[blk] * 4, out_shape=[_sds((1, rows, cols), F32)] * 4,
        compiler_params=_cparams(("arbitrary",)),
    )(w, g_own, *recv, m, v)


def _row_of_each(ref, row):
    cols = ref.shape[1]
    rows = _rows((N_DEV, cols))
    out = jnp.zeros((N_DEV, cols), F32)
    for d in range(N_DEV):
        picked = ref[d * SUBLANES + row:d * SUBLANES + row + 1, :]
        out = jnp.where(rows == d, jnp.broadcast_to(picked, (N_DEV, cols)), out)
    return out


def _my_columns(full, width, me):
    out = jnp.zeros(full.shape[:-1] + (width,), F32)
    for d in range(N_DEV):
        out = out + jnp.where(me == d, full[:, d * width:(d + 1) * width], 0.0)
    return out


def _adamw_wada(c_all, vs_in_all, vs_up_all, vs_ffn_all, w, m, v):
    _, rows, cols = w.shape

    def body(c_ref, vi_ref, vu_ref, vf_ref, w_ref, m_ref, v_ref, go_ref, d_ref, mo_ref, vo_ref):
        me = _dev_index(_my_pos())
        cv = _row_of_each(c_ref, 0)
        ca = cv * _sigmoid(cv)
        dmod = jnp.concatenate([_row_of_each(vi_ref, 0), _row_of_each(vi_ref, 1), _row_of_each(vu_ref, 3),
                                _row_of_each(vu_ref, 0), _row_of_each(vu_ref, 1), _row_of_each(vf_ref, 0)], axis=1)
        dm = _my_columns(dmod, cols, me)
        g = lax.dot_general(ca, dm, (((0,), (0,)), ((), ())), preferred_element_type=F32,
                            precision=lax.Precision.HIGHEST)
        go_ref[0] = g
        d_ref[0], mo_ref[0], vo_ref[0] = _adam_math(w_ref[0], g, m_ref[0], v_ref[0])

    return pl.pallas_call(
        body, name="adamw_w_ada", out_shape=[_sds((1, rows, cols), F32)] * 4,
        in_specs=[_whole()] * 7, out_specs=[_whole()] * 4,
        compiler_params=_cparams(),
    )(c_all, vs_in_all, vs_up_all, vs_ffn_all, w, m, v)


def _adamw_small(gathered, reduced, params, conv_params):
    names = list(params) + list(conv_params)
    allp = {**params, **conv_params}
    n_g = len(gathered) + len(reduced)

    def body(*refs):
        g_refs = refs[:n_g]
        p_refs = refs[n_g:n_g + 3 * len(names)]
        o_refs = refs[n_g + 3 * len(names):]
        me = _dev_index(_my_pos())

        def total(ref):
            s = ref[0:SUBLANES, :]
            for d in range(1, N_DEV):
                s = s + ref[d * SUBLANES:(d + 1) * SUBLANES, :]
            return s

        vs_in, vs_up, vs_ffn, loss = [total(r) for r in g_refs[:4]]
        cs, vs_mix, dcw, dwr, dwi, dws, dbs = [r[...] for r in g_refs[4:]]
        o_refs[-1][...] = loss[0:1, 0:1]
        mine = lambda full, width: _my_columns(full, width, me)

        all_ = (slice(None), slice(None))
        heads = lambda row: [((0, slice(h, h + 1), slice(None)), row[:, h * HEAD_DIM:(h + 1) * HEAD_DIM])
                             for h in range(N_HEADS)]
        blocks = lambda pairs: [((0, h), pairs[_head_pair_block(h)]) for h in range(N_HEADS)]
        pieces = {
            "b_ada": [((slice(None), slice(k * D_MODEL, (k + 1) * D_MODEL)), row) for k, row in enumerate(
                (vs_in[0:1], vs_in[1:2], vs_up[3:4], vs_up[0:1], vs_up[1:2], vs_ffn[0:1]))],
            "g_mix_pre": [(all_, vs_in[2:3])], "g_mix_post": [(all_, vs_up[4:5])],
            "g_ffn_pre": [(all_, vs_up[2:3])], "g_ffn_post": [(all_, vs_ffn[1:2])],
            "conv_b": [(all_, vs_mix[0:1])], "b_rgate": heads(vs_mix[1:2]), "b_igate": heads(vs_mix[2:3]),
            "lru_a": [(all_, vs_mix[3:4])], "v_norm_g": [(all_, vs_mix[4:5])], "v_norm_b": [(all_, vs_mix[5:6])],
            "g_lru_out": [(all_, vs_mix[6:7])], "g_gmlp_out": [(all_, vs_mix[7:8])],
            "w_rgate": blocks(dwr), "w_igate": blocks(dwi),
            "w_spatial": [((0, g), dws[g * POS_BLOCK:(g + 1) * POS_BLOCK, :]) for g in range(N_GROUPS)],
            "b_spatial": [((0,), dbs[0:N_GROUPS])],
            "ffn_conv_b": [(all_, cs[FFN_CONV_K:FFN_CONV_K + 1])],
            "conv_w": [((0,), mine(dcw[0:LRU_CONV_K], LRU_W // N_DEV))],
            "ffn_conv_w": [((0,), mine(cs[0:FFN_CONV_K], 2 * D_FF // N_DEV))],
        }
        for n_i, name in enumerate(names):
            w_ref, m_ref, v_ref = p_refs[3 * n_i:3 * n_i + 3]
            go_ref, d_ref, mo_ref, vo_ref = o_refs[4 * n_i:4 * n_i + 4]
            for idx, g in pieces[name]:
                go_ref[idx] = g
                d_ref[idx], mo_ref[idx], vo_ref[idx] = _adam_math(w_ref[idx], g, m_ref[idx], v_ref[idx])

    flat_params = [a for n in names for a in allp[n]]
    out_shape = [_sds(allp[n][0].shape, F32) for n in names for _ in range(4)] + [_sds((1, 1), F32)]
    outs = pl.pallas_call(
        body, name="adamw_small", out_shape=out_shape,
        in_specs=[_whole()] * (n_g + len(flat_params)), out_specs=[_whole()] * len(out_shape),
        compiler_params=_cparams(),
    )(*gathered, *reduced, *flat_params)
    return {n: outs[4 * i:4 * i + 4] for i, n in enumerate(names)}, outs[-1]


def _my_pos():
    return lax.axis_index("x"), lax.axis_index("y"), lax.axis_index("c")


def _flip(pos, k):
    x, y, c = pos
    return (1 - x if k & 4 else x, 1 - y if k & 2 else y, 1 - c if k & 1 else c)


def _dev_index(pos):
    x, y, c = pos
    return 4 * x + 2 * y + c


def _all_gather_small(ins, outs, send_sems, recv_sems):
    n = len(ins)
    me = _my_pos()

    def slot(a, pos):
        rows = ins[a].shape[0]
        return outs[a].at[pl.ds(pl.multiple_of(_dev_index(pos) * rows, SUBLANES), rows), :]

    def copy(a, k, block):
        return pltpu.make_async_remote_copy(
            src_ref=ins[a], dst_ref=slot(a, block), send_sem=send_sems.at[a, k - 1], recv_sem=recv_sems.at[a, k - 1],
            device_id=_flip(me, k), device_id_type=MESH)

    sends = [copy(a, k, me) for a in range(n) for k in range(1, N_DEV)]
    for cp in sends:
        cp.start()
    for a in range(n):
        rows = ins[a].shape[0]
        outs[a][pl.ds(pl.multiple_of(_dev_index(me) * rows, SUBLANES), rows), :] = ins[a][...]
    for a in range(n):
        for k in range(1, N_DEV):
            copy(a, k, _flip(me, k)).wait_recv()
    for cp in sends:
        cp.wait_send()


def _prologue(c8, cw8, fcw8, w_ada, b_ada, carry):
    cols = w_ada.shape[1]

    def body(c_ref, cw_ref, fcw_ref, w_ref, b_ref, call_ref, cwall_ref, fcwall_ref, modall_ref, mod_scr,
             s1, r1, s2, r2, start_carry):
        _all_gather_small([c_ref, cw_ref, fcw_ref], [call_ref, cwall_ref, fcwall_ref], s1, r1)
        start_carry()
        cv = _row_of_each(call_ref, 0)
        ca = cv * _sigmoid(cv)
        b_cols = _my_columns(b_ref[...], cols, _dev_index(_my_pos()))
        mod_scr[...] = jnp.dot(ca, w_ref[...], preferred_element_type=F32, precision=lax.Precision.HIGHEST) + b_cols
        _all_gather_small([mod_scr], [modall_ref], s2, r2)

    sem = lambda n: pltpu.SemaphoreType.DMA((n, N_DEV - 1))
    return _call(
        body, "prologue", (1,), in_specs=[_whole()] * 5, out_specs=[_whole()] * 4,
        out_shape=[_sds((N_DEV * SUBLANES, a.shape[1]), F32) for a in (c8, cw8, fcw8)]
        + [_sds((N_DEV * N_DEV, cols), F32)],
        scratch=[pltpu.VMEM((N_DEV, cols), F32), sem(3), sem(3), sem(1), sem(1)],
        args=(c8, cw8, fcw8, w_ada, b_ada), carry=carry, body_starts_carry=True)


def _reduce_small(gath, red, carry=None):
    n_g, n_r = len(gath), len(red)
    chip_flips = CHIP_FLIPS

    def body(*refs, start_carry):
        g_in, r_in = refs[:n_g], refs[n_g:n_g + n_r]
        g_out, r_out = refs[n_g + n_r:2 * n_g + n_r], refs[2 * n_g + n_r:2 * (n_g + n_r)]
        scr = refs[2 * (n_g + n_r):]
        sib, land = scr[:n_r], scr[n_r:2 * n_r]
        g_send, g_recv, s_send, s_recv, i_send, i_recv, f_send, f_recv = scr[2 * n_r:]
        me = _my_pos()
        c = me[2]
        sibling = _flip(me, 1)

        def slot(a, pos):
            return g_out[a].at[pl.ds(pl.multiple_of(_dev_index(pos) * SUBLANES, SUBLANES), SUBLANES), :]

        def gcopy(a, k):
            return pltpu.make_async_remote_copy(
                src_ref=g_in[a], dst_ref=slot(a, me), send_sem=g_send.at[a, k - 1], recv_sem=g_recv.at[a, k - 1],
                device_id=_flip(me, k), device_id_type=MESH)

        def scopy(a):
            return pltpu.make_async_remote_copy(
                src_ref=r_in[a], dst_ref=sib[a], send_sem=s_send.at[a], recv_sem=s_recv.at[a],
                device_id=sibling, device_id_type=MESH)

        def icopy(a, j):
            return pltpu.make_async_remote_copy(
                src_ref=r_out[a], dst_ref=land[a].at[j], send_sem=i_send.at[a, j], recv_sem=i_recv.at[a, j],
                device_id=_flip(me, chip_flips[j]), device_id_type=MESH)

        def fcopy(a, j):
            return pltpu.make_async_remote_copy(
                src_ref=land[a].at[j], dst_ref=land[a].at[j], send_sem=f_send.at[a, j], recv_sem=f_recv.at[a, j],
                device_id=sibling, device_id_type=MESH)

        gathers = [gcopy(a, k) for a in range(n_g) for k in range(1, N_DEV)]
        swaps = [scopy(a) for a in range(n_r)]
        for cp in gathers + swaps:
            cp.start()
        for a in range(n_g):
            g_out[a][pl.ds(pl.multiple_of(_dev_index(me) * SUBLANES, SUBLANES), SUBLANES), :] = g_in[a][...]
        for a in range(n_r):
            swaps[a].wait_recv()
            r_out[a][...] = r_in[a][...] + sib[a][...]

        for core in range(2):
            @pl.when(c == core)
            def _():
                for a in range(core, n_r, 2):
                    for j in range(3):
                        icopy(a, j).start()

        start_carry()

        for core in range(2):
            mine = [a for a in range(n_r) if a % 2 == core]
            theirs = [a for a in range(n_r) if a % 2 != core]

            @pl.when(c == core)
            def _():
                out = [icopy(a, j) for a in mine for j in range(3)]
                fwd = []
                for a in mine:
                    for j in range(3):
                        icopy(a, j).wait_recv()
                        cp = fcopy(a, j)
                        cp.start()
                        fwd.append(cp)
                for a in theirs:
                    for j in range(3):
                        fcopy(a, j).wait_recv()
                for cp in out + fwd:
                    cp.wait_send()

        for a in range(n_r):
            r_out[a][...] = (r_out[a][...] + land[a][1]) + (land[a][0] + land[a][2])
        for a in range(n_g):
            for k in range(1, N_DEV):
                pltpu.make_async_remote_copy(
                    src_ref=g_in[a], dst_ref=slot(a, _flip(me, k)), send_sem=g_send.at[a, k - 1],
                    recv_sem=g_recv.at[a, k - 1], device_id=_flip(me, k), device_id_type=MESH).wait_recv()
        for cp in gathers + swaps:
            cp.wait_send()

    shapes = [tuple(a.shape) for a in red]
    outs, carried = _call(
        body, "reduce_small", (1,), in_specs=[_whole()] * (n_g + n_r), out_specs=[_whole()] * (n_g + n_r),
        out_shape=[_sds((N_DEV * SUBLANES, a.shape[1]), F32) for a in gath] + [_sds(s, F32) for s in shapes],
        scratch=[pltpu.VMEM(s, F32) for s in shapes] + [pltpu.VMEM((3,) + s, F32) for s in shapes]
        + [pltpu.SemaphoreType.DMA((n_g, N_DEV - 1)), pltpu.SemaphoreType.DMA((n_g, N_DEV - 1)),
           pltpu.SemaphoreType.DMA((n_r,)), pltpu.SemaphoreType.DMA((n_r,)),
           pltpu.SemaphoreType.DMA((n_r, 3)), pltpu.SemaphoreType.DMA((n_r, 3)),
           pltpu.SemaphoreType.DMA((n_r, 3)), pltpu.SemaphoreType.DMA((n_r, 3))],
        args=tuple(gath) + tuple(red), carry=carry, body_starts_carry=True)
    return (outs[:n_g], outs[n_g:]), carried


STACKED = "stacked"


def _region(ref, shard_shape, col_sharded, pos):
    r, cdim = shard_shape
    d = _dev_index(pos)
    if col_sharded == STACKED:
        return ref.at[d]
    if col_sharded:
        return ref.at[:, pl.ds(pl.multiple_of(d * cdim, LANES), cdim)]
    return ref.at[pl.ds(pl.multiple_of(d * r, 2 * SUBLANES), r), :]


def _gather_carry(shards, col_sharded):
    n_w = len(shards)
    shapes = [tuple(s.shape) for s in shards]
    full_shapes = [(N_DEV,) + s if cs == STACKED else (s[0], s[1] * N_DEV) if cs else (s[0] * N_DEV, s[1])
                   for s, cs in zip(shapes, col_sharded)]

    def tools(out_refs, scr):
        send_sems, recv_sems = scr[n_w], scr[n_w + 1]
        me = _my_pos()
        x, y, c = me
        sibling = (x, y, 1 - c)
        chips = [(1 - x, y), (x, 1 - y), (1 - x, 1 - y)]

        def region(w, pos):
            return _region(out_refs[w], shapes[w], col_sharded[w], pos)

        def copy(w, k, block, to, src=None):
            return pltpu.make_async_remote_copy(
                src_ref=region(w, block) if src is None else src, dst_ref=region(w, block),
                send_sem=send_sems.at[w, k], recv_sem=recv_sems.at[w, k], device_id=to, device_id_type=MESH)

        def first(w):
            return [copy(w, 0, me, sibling, src=scr[w])] + [
                copy(w, 1 + j, me, (*chip, c), src=scr[w]) for j, chip in enumerate(chips)]

        def mine(w):
            return pltpu.make_async_copy(scr[w], region(w, me), scr[n_w + 2].at[w])

        return me, c, sibling, chips, copy, first, mine

    def start(ins, outs, scr):
        _, _, _, _, _, first, mine = tools(outs, scr)
        for w in range(n_w):
            scr[w][...] = ins[w][...].astype(BF16)
            for cp in first(w) + [mine(w)]:
                cp.start()

    def finish(ins, outs, scr):
        me, c, sibling, chips, copy, first, mine = tools(outs, scr)
        passed = []
        for w in range(n_w):
            for j, chip in enumerate(chips):
                copy(w, 1 + j, (*chip, c), me).wait_recv()
                fwd = copy(w, 4 + j, (*chip, c), sibling)
                fwd.start()
                passed.append(fwd)
        for w in range(n_w):
            copy(w, 0, sibling, me).wait_recv()
            for j, chip in enumerate(chips):
                copy(w, 4 + j, (*chip, 1 - c), me).wait_recv()
        for w in range(n_w):
            for cp in first(w):
                cp.wait_send()
            mine(w).wait()
        for cp in passed:
            cp.wait_send()

    return _Carry(
        inputs=list(shards), in_specs=[_whole()] * n_w,
        out_shape=[_sds(s, BF16) for s in full_shapes], out_specs=[_any()] * n_w,
        scratch=[pltpu.VMEM(s, BF16) for s in shapes]
        + [pltpu.SemaphoreType.DMA((n_w, N_DEV - 1)), pltpu.SemaphoreType.DMA((n_w, N_DEV - 1)),
           pltpu.SemaphoreType.DMA((n_w,))],
        start=start, finish=finish)


CHIP_FLIPS = (4, 2, 6)


def _pair_reduce(g_bf, g_own, col_sharded):
    shape = tuple(g_own.shape)
    n = len(CHIP_FLIPS)

    def body(g_ref, own_ref, hown_ref, hout_ref, mine, sib, send_sems, recv_sems, local_sems):
        me = _my_pos()
        sibling = _flip(me, 1)
        flips = (0,) + CHIP_FLIPS

        def region(pos):
            return _region(g_ref, shape, col_sharded, pos)

        local = [pltpu.make_async_copy(region(_flip(me, f)), mine.at[s], local_sems.at[s])
                 for s, f in enumerate(CHIP_FLIPS)]
        sends = [pltpu.make_async_remote_copy(
            src_ref=region(_flip(sibling, f)), dst_ref=sib.at[s], send_sem=send_sems.at[s], recv_sem=recv_sems.at[s],
            device_id=sibling, device_id_type=MESH) for s, f in enumerate(flips)]
        for cp in local + sends:
            cp.start()
        for cp in local:
            cp.wait()
        for cp in sends:
            cp.wait_recv()
        hown_ref[...] = own_ref[...] + sib[0].astype(F32)
        for s in range(n):
            hout_ref[s] = (mine[s].astype(F32) + sib[s + 1].astype(F32)).astype(BF16)
        for cp in sends:
            cp.wait_send()

    return pl.pallas_call(
        body, name="pair_reduce", out_shape=[_sds(shape, F32), _sds((n,) + shape, BF16)],
        in_specs=[_any(), _whole()], out_specs=[_whole(), _whole()],
        scratch_shapes=[pltpu.VMEM((n,) + shape, BF16), pltpu.VMEM((n + 1,) + shape, BF16),
                        pltpu.SemaphoreType.DMA((n + 1,)), pltpu.SemaphoreType.DMA((n + 1,)),
                        pltpu.SemaphoreType.DMA((n,))],
        compiler_params=pltpu.CompilerParams(vmem_limit_bytes=VMEM_LIMIT),
    )(g_bf, g_own)


def _chip_scatter_carry(h_out):
    n = len(CHIP_FLIPS)

    def copies(ins, outs, scr):
        send_sems, recv_sems = scr
        me = _my_pos()
        return [pltpu.make_async_remote_copy(
            src_ref=ins[0].at[j], dst_ref=outs[0].at[j], send_sem=send_sems.at[j], recv_sem=recv_sems.at[j],
            device_id=_flip(me, CHIP_FLIPS[j]), device_id_type=MESH) for j in range(n)]

    def start(ins, outs, scr):
        for cp in copies(ins, outs, scr):
            cp.start()

    def finish(ins, outs, scr):
        cps = copies(ins, outs, scr)
        for cp in cps:
            cp.wait_recv()
        for cp in cps:
            cp.wait_send()

    return _Carry(inputs=[h_out], in_specs=[_any()], out_shape=[_sds(tuple(h_out.shape), BF16)], out_specs=[_any()],
                  scratch=[pltpu.SemaphoreType.DMA((n,)), pltpu.SemaphoreType.DMA((n,))], start=start, finish=finish)


def _scatter_carry(grads_bf, shard_shapes, col_sharded, relations):
    n_w = len(grads_bf)
    shapes = [tuple(s) for s in shard_shapes]

    def copies(ins, outs, scr):
        send_sems, recv_sems = scr
        me = _my_pos()
        out = []
        for w in range(n_w):
            for i, k in enumerate(relations[w]):
                peer = _flip(me, k)
                out.append(pltpu.make_async_remote_copy(
                    src_ref=_region(ins[w], shapes[w], col_sharded[w], peer), dst_ref=outs[w].at[i],
                    send_sem=send_sems.at[w, i], recv_sem=recv_sems.at[w, i],
                    device_id=peer, device_id_type=MESH))
        return out

    def start(ins, outs, scr):
        for cp in copies(ins, outs, scr):
            cp.start()

    def finish(ins, outs, scr):
        cps = copies(ins, outs, scr)
        for cp in cps:
            cp.wait_recv()
        for cp in cps:
            cp.wait_send()

    return _Carry(
        inputs=list(grads_bf), in_specs=[_any()] * n_w,
        out_shape=[_sds((len(r),) + s, BF16) for r, s in zip(relations, shapes)], out_specs=[_any()] * n_w,
        scratch=[pltpu.SemaphoreType.DMA((n_w, N_DEV - 1)), pltpu.SemaphoreType.DMA((n_w, N_DEV - 1))],
        start=start, finish=finish)


def _block_diag(w):
    eye = jnp.eye(N_HEADS, dtype=w.dtype)
    return (eye[:, None, :, None] * w[:, :, None, :]).reshape(N_HEADS * HEAD_DIM, N_HEADS * HEAD_DIM)


def _pad_rows(a):
    return jnp.pad(a, ((0, SUBLANES - a.shape[0]), (0, 0)))


def _columns_from_devices(gathered, rows):
    w = gathered.shape[1]
    return gathered.reshape(N_DEV, SUBLANES, w)[:, :rows].transpose(1, 0, 2).reshape(rows, N_DEV * w)


def _local_step(x2, target, mod, w_in_f, w_full, conv_w_full, ffn_cw_full,
                g_mix_pre, g_mix_post, conv_b, w_rgate, b_rgate, w_igate, b_igate, lru_a, v_norm_g, v_norm_b,
                w_spatial, b_spatial, g_lru_out, g_gmlp_out, g_ffn_pre, g_ffn_post, ffn_conv_b,
                gather=None, scatter=None, adam=None):
    sh_m, sc_m, gt_m, sh_f, sc_f, gt_f = [mod[k] for k in range(N_MOD)]
    wr_bd = _block_diag(w_rgate[0]).astype(BF16)
    wi_bd = _block_diag(w_igate[0]).astype(BF16)
    b_r = b_rgate.reshape(1, LRU_W)
    b_i = b_igate.reshape(1, LRU_W)
    b_sp_t = b_spatial[0].T
    w_sp_t = jnp.swapaxes(w_spatial[0], 1, 2)

    def arriving(*names):
        return gather(*names) if gather else None

    near, far = (1, 2, 3, 4, 5), (6, 7)

    def leaving(*parts):
        return scatter(parts) if scatter else None

    def received(recv, parts, outs):
        for (name, _, _), out in zip(parts, outs):
            recv.setdefault(name, []).append(out)

    mix_params = (conv_w_full, conv_b, wr_bd, wi_bd, b_r, b_i, lru_a, v_norm_g, v_norm_b)
    w_out_f = w_full["w_out"]
    (z, h, ycat, hl, y, x1, h2), got = _mix_fwd(
        x2, sh_m, sc_m, g_mix_pre, w_in_f, *mix_params, w_spatial[0], b_sp_t, g_lru_out, g_gmlp_out,
        w_out_f, g_mix_post, gt_m, g_ffn_pre, sc_f, sh_f, carry=arriving("w_up"))
    w_up_f = got[0] if gather else w_full["w_up"]
    (up_pre, up, act), got = _ffn_fwd(h2, w_up_f, ffn_cw_full, ffn_conv_b, carry=arriving("w_down"))
    w_down_f = got[0] if gather else w_full["w_down"]
    d_y2, dout, loss_acc, vs_ffn = _ffn_tail(act, w_down_f, x1, gt_f, g_ffn_post, target)

    recv, updated = {}, {}

    def updating(grads):
        if not adam:
            return None
        return _adamw_rider([(adam[n][0], g[1], recv[n], adam[n][1], adam[n][2]) for n, g in grads.items()], N_DEV)

    def updates(grads, outs):
        for j, n in enumerate(grads):
            updated[n] = tuple(outs[4 * j:4 * j + 4])

    gw_down, _ = _wgrad(act, d_y2, "wgrad_down", by_rows=True)
    parts = [("w_down", gw_down[0], near + far)]
    (d_up, cs_ffn), got = _ffn_bwd(d_y2, up_pre, up, ffn_cw_full, w_down_f, carry=leaving(*parts))
    received(recv, parts, got)
    gw_up, got = _wgrad(h2, d_up, "wgrad_up", carry=updating(dict(w_down=gw_down)))
    updates(dict(w_down=gw_down), got)
    parts = [("w_up", gw_up[0], near)]
    (d_x1, d_y, d_ycat, vs_up), got = _up_bwd(
        d_up, w_up_f, x1, dout, y, w_out_f, g_ffn_pre, sc_f, g_mix_post, gt_m, carry=leaving(*parts))
    received(recv, parts, got)
    gw_out, _ = _wgrad(ycat, d_y, "wgrad_out", by_rows=True)
    parts = [("w_up", gw_up[0], far), ("w_out", gw_out[0], near + far)]
    (d_z, vs_mix, dcw, d_wr, d_wi, d_ws, d_bs), got = _mix_bwd(
        d_ycat, z, hl, *mix_params, w_spatial[0], w_sp_t, b_sp_t, g_lru_out, g_gmlp_out, carry=leaving(*parts))
    received(recv, parts, got)
    gw_in, got = _wgrad(h, d_z, "wgrad_in", carry=updating(dict(w_up=gw_up, w_out=gw_out)))
    updates(dict(w_up=gw_up, w_out=gw_out), got)
    chip_sums = None
    if scatter:
        h_own, h_out = _pair_reduce(gw_in[0], gw_in[1], True)
        gw_in = (gw_in[0], h_own)
        chip_sums = _chip_scatter_carry(h_out)
    (grad_x, vs_in), got = _in_bwd(d_z, w_in_f, x2, d_x1, g_mix_pre, sc_m, carry=chip_sums)
    recv["w_in"] = list(got)

    gath = [vs_in, vs_up, vs_ffn, loss_acc]
    red = [cs_ffn, vs_mix, dcw, d_wr, d_wi, d_ws.reshape(N_GROUPS * POS_BLOCK, POS_BLOCK), d_bs]
    return dict(grad_x=grad_x, gath=gath, red=red, recv=recv, updated=updated,
                w_in=gw_in, w_out=gw_out, w_up=gw_up, w_down=gw_down)


def kernel(x, c, w_ada, b_ada, g_mix_pre, g_mix_post, w_in, conv_w, conv_b, w_rgate, b_rgate, w_igate, b_igate, lru_a, v_norm_g, v_norm_b, w_spatial, b_spatial, g_lru_out, g_gmlp_out, w_out, g_ffn_pre, g_ffn_post, w_up, ffn_conv_w, ffn_conv_b, w_down, loss_target, m_w_ada, m_b_ada, m_g_mix_pre, m_g_mix_post, m_w_in, m_conv_w, m_conv_b, m_w_rgate, m_b_rgate, m_w_igate, m_b_igate, m_lru_a, m_v_norm_g, m_v_norm_b, m_w_spatial, m_b_spatial, m_g_lru_out, m_g_gmlp_out, m_w_out, m_g_ffn_pre, m_g_ffn_post, m_w_up, m_ffn_conv_w, m_ffn_conv_b, m_w_down, v_w_ada, v_b_ada, v_g_mix_pre, v_g_mix_post, v_w_in, v_conv_w, v_conv_b, v_w_rgate, v_b_rgate, v_w_igate, v_b_igate, v_lru_a, v_v_norm_g, v_v_norm_b, v_w_spatial, v_b_spatial, v_g_lru_out, v_g_gmlp_out, v_w_out, v_g_ffn_pre, v_g_ffn_post, v_w_up, v_ffn_conv_w, v_ffn_conv_b, v_w_down):
    me = _dev_index(_my_pos())
    ada_cols = w_ada.shape[-1]

    big_w = dict(w_in=(w_in, m_w_in, v_w_in, True), w_out=(w_out, m_w_out, v_w_out, False),
                 w_up=(w_up, m_w_up, v_w_up, True), w_down=(w_down, m_w_down, v_w_down, False))

    def gather(*names):
        return _gather_carry([big_w[n][0][0] for n in names], [STACKED if n == "w_up" else big_w[n][3] for n in names])

    def scatter(parts):
        return _scatter_carry([g for _, g, _ in parts], [big_w[n][0].shape[1:] for n, _, _ in parts],
                              [big_w[n][3] for n, _, _ in parts], [rel for _, _, rel in parts])

    (c_all, cw_all, fcw_all, mod_all), (w_in_f, w_out_f) = _prologue(
        jnp.broadcast_to(c, (SUBLANES, D_MODEL)), _pad_rows(conv_w[0]), _pad_rows(ffn_conv_w[0]), w_ada[0], b_ada,
        carry=gather("w_in", "w_out"))
    conv_w_full = _columns_from_devices(cw_all, LRU_CONV_K)
    ffn_cw_full = _columns_from_devices(fcw_all, FFN_CONV_K)
    mod = lax.dynamic_index_in_dim(mod_all.reshape(N_DEV, N_DEV, ada_cols), me, axis=1, keepdims=False)
    mod = mod.reshape(N_MOD, 1, D_MODEL)

    loc = _local_step(x[0], loss_target[0], mod, w_in_f, dict(w_out=w_out_f), conv_w_full, ffn_cw_full,
                      g_mix_pre, g_mix_post, conv_b, w_rgate, b_rgate, w_igate, b_igate, lru_a, v_norm_g, v_norm_b,
                      w_spatial, b_spatial, g_lru_out, g_gmlp_out, g_ffn_pre, g_ffn_post, ffn_conv_b,
                      gather=gather, scatter=scatter,
                      adam={n: big_w[n][:3] for n in ("w_out", "w_up", "w_down")})
    grad_x = loc["grad_x"]

    (gathered, reduced), _ = _reduce_small(loc["gath"], loc["red"])

    results = dict(loc["updated"])
    w_, m_, v_, _ = big_w["w_in"]
    results["w_in"] = _adamw_sum(w_, loc["w_in"][1], loc["recv"]["w_in"], m_, v_, "adamw_w_in")

    params = dict(
        b_ada=(b_ada, m_b_ada, v_b_ada), g_mix_pre=(g_mix_pre, m_g_mix_pre, v_g_mix_pre),
        g_mix_post=(g_mix_post, m_g_mix_post, v_g_mix_post), conv_b=(conv_b, m_conv_b, v_conv_b),
        w_rgate=(w_rgate, m_w_rgate, v_w_rgate), b_rgate=(b_rgate, m_b_rgate, v_b_rgate),
        w_igate=(w_igate, m_w_igate, v_w_igate), b_igate=(b_igate, m_b_igate, v_b_igate),
        lru_a=(lru_a, m_lru_a, v_lru_a), v_norm_g=(v_norm_g, m_v_norm_g, v_v_norm_g),
        v_norm_b=(v_norm_b, m_v_norm_b, v_v_norm_b), w_spatial=(w_spatial, m_w_spatial, v_w_spatial),
        b_spatial=(b_spatial, m_b_spatial, v_b_spatial), g_lru_out=(g_lru_out, m_g_lru_out, v_g_lru_out),
        g_gmlp_out=(g_gmlp_out, m_g_gmlp_out, v_g_gmlp_out), g_ffn_pre=(g_ffn_pre, m_g_ffn_pre, v_g_ffn_pre),
        g_ffn_post=(g_ffn_post, m_g_ffn_post, v_g_ffn_post), ffn_conv_b=(ffn_conv_b, m_ffn_conv_b, v_ffn_conv_b))
    conv_params = dict(conv_w=(conv_w, m_conv_w, v_conv_w), ffn_conv_w=(ffn_conv_w, m_ffn_conv_w, v_ffn_conv_w))
    small_results, loss = _adamw_small(gathered, reduced, params, conv_params)
    results.update(small_results)
    loss = loss.reshape(())

    results["w_ada"] = _adamw_wada(c_all, gathered[0], gathered[1], gathered[2], w_ada, m_w_ada, v_w_ada)

    order = ["w_ada", "b_ada", "g_mix_pre", "g_mix_post", "w_in", "conv_w", "conv_b", "w_rgate", "b_rgate", "w_igate",
             "b_igate", "lru_a", "v_norm_g", "v_norm_b", "w_spatial", "b_spatial", "g_lru_out", "g_gmlp_out", "w_out",
             "g_ffn_pre", "g_ffn_post", "w_up", "ffn_conv_w", "ffn_conv_b", "w_down"]
    outs = [loss, grad_x[None]]
    for kind in range(4):
        outs += [results[n][kind] for n in order]
    return tuple(outs)
```

```python
import functools

import jax
import jax.numpy as jnp
from jax import lax
from jax.experimental import pallas as pl
from jax.experimental.pallas import tpu as pltpu

F32 = jnp.float32
BF16 = jnp.bfloat16

D_MODEL = 1024
LRU_W = 512
GMLP_W = 512
N_HEADS = 8
HEAD_DIM = 64
N_GROUPS = 4
POS_BLOCK = 128
CHUNK = 64
IN_COLS = 2048
D_FF = 3072
N_MOD = 6
N_DEV = 8
EPS = 1e-6
LRU_C = 8.0
LRU_CONV_K = 4
FFN_CONV_K = 3

ADAM_LR = 0.001
ADAM_B1 = 0.9
ADAM_B2 = 0.999
ADAM_EPS = 1e-08
ADAM_WD = 0.01
ADAM_STEP = 10

LANES = 128
SUBLANES = 8
TT_BIG = 512
TT_MIX = 256
FF_CW = 1024
VMEM_LIMIT = 56 * 1024 * 1024

MESH = pl.DeviceIdType.MESH


def _sds(shape, dtype):
    return jax.ShapeDtypeStruct(shape, dtype)


def _cparams(sem=None):
    return pltpu.CompilerParams(dimension_semantics=sem, vmem_limit_bytes=VMEM_LIMIT)


def _whole():
    return pl.BlockSpec(memory_space=pltpu.VMEM)


def _const(shape):
    nd = len(shape)
    return pl.BlockSpec(shape, lambda *_: (0,) * nd)


def _any():
    return pl.BlockSpec(memory_space=pl.ANY)


class _Carry:
    def __init__(self, inputs, in_specs, out_shape, out_specs, scratch, start=None, finish=None, each=None):
        self.inputs, self.in_specs, self.out_shape, self.out_specs = inputs, in_specs, out_shape, out_specs
        self.scratch, self.start, self.finish, self.each = scratch, start, finish, each


def _call(body, name, grid, in_specs, out_specs, out_shape, scratch, args, carry=None, body_starts_carry=False):
    n_in, n_out, n_scr = len(in_specs), len(out_specs), len(scratch)
    c_in = len(carry.in_specs) if carry else 0
    c_out = len(carry.out_specs) if carry else 0

    def full_body(*refs):
        ins = refs[:n_in]
        c_ins = refs[n_in:n_in + c_in]
        outs = refs[n_in + c_in:n_in + c_in + n_out]
        c_outs = refs[n_in + c_in + n_out:n_in + c_in + n_out + c_out]
        scr = refs[n_in + c_in + n_out + c_out:n_in + c_in + n_out + c_out + n_scr]
        c_scr = refs[n_in + c_in + n_out + c_out + n_scr:]
        if carry:
            first = functools.reduce(lambda a, b: a & b, [pl.program_id(d) == 0 for d in range(len(grid))])
            last = functools.reduce(lambda a, b: a & b, [pl.program_id(d) == g - 1 for d, g in enumerate(grid)])

        if carry and carry.start and not body_starts_carry:
            @pl.when(first)
            def _():
                carry.start(c_ins, c_outs, c_scr)

        if body_starts_carry:
            body(*ins, *outs, *scr, start_carry=(lambda: carry.start(c_ins, c_outs, c_scr)) if carry else (lambda: None))
        else:
            body(*ins, *outs, *scr)
        if carry and carry.each:
            carry.each(c_ins, c_outs, c_scr)
        if carry and carry.finish:
            @pl.when(last)
            def _():
                carry.finish(c_ins, c_outs, c_scr)

    res = pl.pallas_call(
        full_body, name=name, grid=grid,
        in_specs=list(in_specs) + (list(carry.in_specs) if carry else []),
        out_specs=list(out_specs) + (list(carry.out_specs) if carry else []),
        out_shape=list(out_shape) + (list(carry.out_shape) if carry else []),
        scratch_shapes=list(scratch) + (list(carry.scratch) if carry else []),
        compiler_params=_cparams(("arbitrary",) * len(grid)),
    )(*args, *(carry.inputs if carry else []))
    return res[:n_out], res[n_out:]


GELU_C0 = 0.7978845608028654
GELU_C1 = GELU_C0 * 0.044715


def _gelu(x):
    t = jnp.tanh(x * (GELU_C0 + GELU_C1 * (x * x)))
    hx = 0.5 * x
    return hx + hx * t


def _gelu_and_grad(x):
    x2 = x * x
    t = jnp.tanh(x * (GELU_C0 + GELU_C1 * x2))
    hx = 0.5 * x
    g = hx + hx * t
    dg = (0.5 + 0.5 * t) + hx * (1.0 - t * t) * (GELU_C0 + 3.0 * GELU_C1 * x2)
    return g, dg


def _sigmoid(x):
    return 1.0 / (1.0 + jnp.exp(-x))


def _softplus(x):
    return jnp.maximum(x, 0.0) + jnp.log1p(jnp.exp(-jnp.abs(x)))


def _neg_expm1(x):
    series = -x * (1.0 + x * (0.5 + x * (1.0 / 6.0 + x * (1.0 / 24.0 + x * (1.0 / 120.0)))))
    return jnp.where(x > -0.1, series, 1.0 - jnp.exp(x))


def _dot(a, b):
    return jnp.dot(a.astype(BF16), b.astype(BF16), preferred_element_type=F32)


def _dot_nt(a, b):
    return lax.dot_general(a.astype(BF16), b.astype(BF16), (((1,), (1,)), ((), ())), preferred_element_type=F32)


def _dot_tn(a, b):
    return lax.dot_general(a.astype(BF16), b.astype(BF16), (((0,), (0,)), ((), ())), preferred_element_type=F32)


def _rows(shape):
    return lax.broadcasted_iota(jnp.int32, shape, 0)


def _shift_down(cur, prev8, s):
    if s == 0:
        return cur
    n = cur.shape[0]
    r = pltpu.roll(cur, s, 0)
    p = pltpu.roll(prev8, s, 0)
    top = jnp.where(_rows(p.shape) < s, p, r[0:SUBLANES])
    if n == SUBLANES:
        return top
    return jnp.concatenate([top, r[SUBLANES:]], axis=0)


def _shift_up(cur, next8, s):
    if s == 0:
        return cur
    n = cur.shape[0]
    r = pltpu.roll(cur, n - s, 0)
    q = pltpu.roll(next8, SUBLANES - s, 0)
    bot = jnp.where(_rows(q.shape) >= SUBLANES - s, q, r[n - SUBLANES:])
    if n == SUBLANES:
        return bot
    return jnp.concatenate([r[:n - SUBLANES], bot], axis=0)


def _scan_fwd(a, b, h_in):
    n = a.shape[0]
    in_group = _rows(a.shape) & (SUBLANES - 1)
    s = 1
    while s < SUBLANES:
        a_s = pltpu.roll(a, s, 0)
        b_s = pltpu.roll(b, s, 0)
        m = in_group >= s
        b = jnp.where(m, a * b_s + b, b)
        a = jnp.where(m, a * a_s, a)
        s *= 2
    out, carry = [], h_in
    for g in range(n // SUBLANES):
        rows = slice(g * SUBLANES, (g + 1) * SUBLANES)
        h_g = a[rows] * carry + b[rows]
        out.append(h_g)
        carry = h_g[SUBLANES - 1:SUBLANES, :]
    return jnp.concatenate(out, axis=0)


def _scan_rev(a, b, l_in):
    n = a.shape[0]
    in_group = _rows(a.shape) & (SUBLANES - 1)
    s = 1
    while s < SUBLANES:
        a_s = pltpu.roll(a, n - s, 0)
        b_s = pltpu.roll(b, n - s, 0)
        m = in_group < SUBLANES - s
        b = jnp.where(m, b + a * b_s, b)
        a = jnp.where(m, a * a_s, a)
        s *= 2
    out, carry = [], l_in
    for g in reversed(range(n // SUBLANES)):
        rows = slice(g * SUBLANES, (g + 1) * SUBLANES)
        l_g = b[rows] + a[rows] * carry
        out.append(l_g)
        carry = l_g[0:1, :]
    return jnp.concatenate(out[::-1], axis=0)


def _rms(x):
    r = lax.rsqrt(jnp.mean(x * x, axis=-1, keepdims=True) + EPS)
    return x * r, r


def _rms_bwd(d_n, n, r):
    return r * (d_n - n * jnp.mean(d_n * n, axis=-1, keepdims=True))


def _colsum(x):
    return jnp.sum(x, axis=0, keepdims=True)


ROW_PIECE = 256


def _row_pieces(tt):
    return [slice(r, r + min(ROW_PIECE, tt)) for r in range(0, tt, min(ROW_PIECE, tt))]


def _lru_gates(xc, wr_ref, wi_ref, br, bi, sp_a):
    r = _sigmoid(_dot(xc, wr_ref[...]) + br)
    i = _sigmoid(_dot(xc, wi_ref[...]) + bi)
    la = -LRU_C * r * sp_a
    a = jnp.exp(la)
    mult = jnp.sqrt(_neg_expm1(2.0 * la))
    return r, i, a, mult


def _lru_conv(lx, prev8, cw_ref, cb):
    xc = cb + cw_ref[LRU_CONV_K - 1:LRU_CONV_K, :] * lx
    taps = []
    for k in range(LRU_CONV_K - 1):
        tap = _shift_down(lx, prev8, LRU_CONV_K - 1 - k)
        taps.append(tap)
        xc = xc + cw_ref[k:k + 1, :] * tap
    return xc, taps


def _ws_mask(transposed=False):
    i = lax.broadcasted_iota(jnp.int32, (POS_BLOCK, POS_BLOCK), 0)
    j = lax.broadcasted_iota(jnp.int32, (POS_BLOCK, POS_BLOCK), 1)
    if transposed:
        i, j = j, i
    return (j // CHUNK) <= (i // CHUNK)


def _gmlp_v(gv, vg, vb):
    av, dav = _gelu_and_grad(gv)
    mu = jnp.mean(av, axis=-1, keepdims=True)
    cen = av - mu
    rs = lax.rsqrt(jnp.mean(cen * cen, axis=-1, keepdims=True) + EPS)
    vhat = cen * rs
    return vhat * vg + vb, vhat, rs, dav


def _mix_fwd(x, sh, sc, g_pre, w_in, conv_w, conv_b, wr_bd, wi_bd, b_r, b_i, lru_a, vn_g, vn_b, w_sp, b_sp_t,
             g_lru, g_gmlp, w_out, g_post, gt_m, g_ffn_pre, sc_f, sh_f, carry=None):
    s_len = x.shape[0]
    tt = min(TT_MIX, s_len)
    nblk = tt // POS_BLOCK

    def body(x_ref, sh_ref, sc_ref, g_ref, w_ref, cw_ref, cb_ref, wr_ref, wi_ref, br_ref, bi_ref, la_ref, vg_ref,
             vb_ref, ws_ref, bst_ref, gl_ref, gg_ref, wo_ref, gp_ref, gtm_ref, g2_ref, scf_ref, shf_ref,
             z_ref, h_ref, y_ref, hl_ref, yo_ref, x1_ref, h2_ref, prev8, hcar):
        i = pl.program_id(0)

        @pl.when(i == 0)
        def _():
            prev8[...] = jnp.zeros_like(prev8)
            hcar[...] = jnp.zeros_like(hcar)

        n_x, _ = _rms(x_ref[...])
        h = (n_x * g_ref[...] * (1.0 + sc_ref[...]) + sh_ref[...]).astype(BF16)
        h_ref[...] = h
        z_ref[...] = jnp.dot(h, w_ref[...], preferred_element_type=F32)

        lx = z_ref[:, 0:LRU_W]
        gate = z_ref[:, LRU_W:2 * LRU_W]
        gu = z_ref[:, 2 * LRU_W:2 * LRU_W + GMLP_W]
        gv = z_ref[:, 2 * LRU_W + GMLP_W:]

        xc, _ = _lru_conv(lx, prev8[...], cw_ref, cb_ref[...])
        prev8[...] = lx[tt - SUBLANES:]
        sp_a = _softplus(-la_ref[...])
        _, ig, a, mult = _lru_gates(xc, wr_ref, wi_ref, br_ref[...], bi_ref[...], sp_a)
        bx = mult * (ig * xc)
        hl = _scan_fwd(a, bx, hcar[0:1, :])
        hcar[...] = jnp.broadcast_to(hl[tt - 1:tt, :], hcar.shape)
        hl_ref[...] = hl
        y_lru = hl * _gelu(gate)
        n_l, _ = _rms(y_lru)
        y_ref[:, 0:LRU_W] = (n_l * gl_ref[...]).astype(BF16)

        u = _gelu(gu)
        v, _, _, _ = _gmlp_v(gv, vg_ref[...], vb_ref[...])
        mask = _ws_mask()
        sp_parts = []
        for nb in range(nblk):
            row = []
            for g in range(N_GROUPS):
                wsm = jnp.where(mask, ws_ref[g], 0.0)
                vblk = v[nb * POS_BLOCK:(nb + 1) * POS_BLOCK, g * LANES:(g + 1) * LANES]
                row.append(_dot(wsm, vblk) + bst_ref[:, g:g + 1])
            sp_parts.append(jnp.concatenate(row, axis=1))
        sp = jnp.concatenate(sp_parts, axis=0) if nblk > 1 else sp_parts[0]
        n_g, _ = _rms(u * sp)
        y_ref[:, LRU_W:] = (n_g * gg_ref[...]).astype(BF16)

        y = jnp.dot(y_ref[...], wo_ref[...], preferred_element_type=F32)
        yo_ref[...] = y
        n_y, _ = _rms(y)
        x1 = x_ref[...] + gtm_ref[...] * (n_y * gp_ref[...])
        x1_ref[...] = x1
        n1, _ = _rms(x1)
        h2_ref[...] = (n1 * g2_ref[...] * (1.0 + scf_ref[...]) + shf_ref[...]).astype(BF16)

    row = lambda c: pl.BlockSpec((tt, c), lambda i: (i, 0))
    v512 = _const((1, LRU_W))
    vec = _const((1, D_MODEL))
    return _call(
        body, "mix_fwd", (s_len // tt,),
        in_specs=[row(D_MODEL), vec, vec, vec, _whole(),
                  _const((LRU_CONV_K, LRU_W)), v512, _whole(), _whole(), v512, v512, v512, v512, v512,
                  _whole(), _whole(), v512, v512, _whole(), vec, vec, vec, vec, vec],
        out_specs=[row(IN_COLS), row(D_MODEL), row(LRU_W + GMLP_W), row(LRU_W), row(D_MODEL), row(D_MODEL),
                   row(D_MODEL)],
        out_shape=[_sds((s_len, IN_COLS), F32), _sds((s_len, D_MODEL), BF16),
                   _sds((s_len, LRU_W + GMLP_W), BF16), _sds((s_len, LRU_W), F32),
                   _sds((s_len, D_MODEL), F32), _sds((s_len, D_MODEL), F32), _sds((s_len, D_MODEL), BF16)],
        scratch=[pltpu.VMEM((SUBLANES, LRU_W), F32), pltpu.VMEM((SUBLANES, LRU_W), F32)],
        args=(x, sh, sc, g_pre, w_in, conv_w, conv_b, wr_bd, wi_bd, b_r, b_i, lru_a, vn_g, vn_b, w_sp, b_sp_t,
              g_lru, g_gmlp, w_out, g_post, gt_m, g_ffn_pre, sc_f, sh_f), carry=carry)


FF_CHUNKS = N_DEV // 2
FF_CHUNK_W = D_FF // FF_CHUNKS


def _ffn_fwd(h2, w_up3, ffn_cw, ffn_cb, carry=None):
    s_len = h2.shape[0]
    tt = min(TT_BIG, s_len)
    nc, cw = FF_CHUNKS, FF_CHUNK_W

    def body(h2_ref, wu_ref, cwg_ref, cwv_ref, cbg_ref, cbv_ref, up_ref, upc_ref, act_ref, prev):
        i = pl.program_id(0)
        c = pl.program_id(1)

        @pl.when(i == 0)
        def _():
            prev[c] = jnp.zeros((2, SUBLANES, cw), F32)

        h2 = h2_ref[...]
        ug_pre = jnp.dot(h2, wu_ref[c], preferred_element_type=F32)
        uv_pre = jnp.dot(h2, wu_ref[nc + c], preferred_element_type=F32)
        up_ref[0] = ug_pre.astype(BF16)
        up_ref[1] = uv_pre.astype(BF16)
        ug, _ = _ffn_conv(ug_pre, prev[c, 0], cwg_ref, cbg_ref[...])
        uv, _ = _ffn_conv(uv_pre, prev[c, 1], cwv_ref, cbv_ref[...])
        prev[c, 0] = ug_pre[tt - SUBLANES:, :]
        prev[c, 1] = uv_pre[tt - SUBLANES:, :]
        upc_ref[0] = ug
        upc_ref[1] = uv
        act_ref[...] = (_gelu(ug) * uv).astype(BF16)

    chunk2 = pl.BlockSpec((2, tt, cw), lambda i, c: (0, i, c))
    ffn_cb2 = ffn_cb.reshape(1, 2 * D_FF)
    return _call(
        body, "ffn_fwd", (s_len // tt, nc),
        in_specs=[pl.BlockSpec((tt, D_MODEL), lambda i, c: (i, 0)), _whole(),
                  pl.BlockSpec((FFN_CONV_K, cw), lambda i, c: (0, c)),
                  pl.BlockSpec((FFN_CONV_K, cw), lambda i, c: (0, c + nc)),
                  pl.BlockSpec((1, cw), lambda i, c: (0, c)),
                  pl.BlockSpec((1, cw), lambda i, c: (0, c + nc))],
        out_specs=[chunk2, chunk2, pl.BlockSpec((tt, cw), lambda i, c: (i, c))],
        out_shape=[_sds((2, s_len, D_FF), BF16), _sds((2, s_len, D_FF), F32), _sds((s_len, D_FF), BF16)],
        scratch=[pltpu.VMEM((nc, 2, SUBLANES, cw), F32)],
        args=(h2, w_up3, ffn_cw, ffn_cw, ffn_cb2, ffn_cb2), carry=carry)


def _ffn_tail(act, w_down, x1, gt_f, g_post, target):
    s_len = x1.shape[0]
    tt = min(TT_BIG, s_len)

    def body(act_ref, wd_ref, x1_ref, gtf_ref, gp_ref, tg_ref, dy2_ref, dout_ref, loss_ref, vs_ref):
        @pl.when(pl.program_id(0) == 0)
        def _():
            loss_ref[...] = jnp.zeros_like(loss_ref)
            vs_ref[...] = jnp.zeros_like(vs_ref)

        for rows in _row_pieces(tt):
            n2, r2 = _rms(jnp.dot(act_ref[rows, :], wd_ref[...], preferred_element_type=F32))
            out = x1_ref[rows, :] + gtf_ref[...] * (n2 * gp_ref[...])
            err = out - tg_ref[rows, :]
            do = err * (1.0 / D_MODEL)
            dout_ref[rows, :] = do
            loss_ref[...] += jnp.broadcast_to(0.5 * jnp.sum(err * err, keepdims=True) * (1.0 / D_MODEL),
                                              loss_ref.shape)
            vs_ref[0:1, :] += _colsum(do * n2 * gp_ref[...])
            vs_ref[1:2, :] += _colsum(do * gtf_ref[...] * n2)
            dy2_ref[rows, :] = _rms_bwd(do * gtf_ref[...] * gp_ref[...], n2, r2).astype(BF16)

    row = lambda c: pl.BlockSpec((tt, c), lambda i: (i, 0))
    vec = _const((1, D_MODEL))
    outs, _ = _call(
        body, "ffn_tail", (s_len // tt,),
        in_specs=[row(D_FF), _whole(), row(D_MODEL), vec, vec, row(D_MODEL)],
        out_specs=[row(D_MODEL), row(D_MODEL), _const((SUBLANES, LANES)), _const((SUBLANES, D_MODEL))],
        out_shape=[_sds((s_len, D_MODEL), BF16), _sds((s_len, D_MODEL), F32), _sds((SUBLANES, LANES), F32),
                   _sds((SUBLANES, D_MODEL), F32)],
        scratch=[], args=(act, w_down, x1, gt_f, g_post, target))
    return outs


def _ffn_conv(up_pre, prev8, cw_ref, cb):
    up = cb + cw_ref[FFN_CONV_K - 1:FFN_CONV_K, :] * up_pre
    taps = []
    for k in range(FFN_CONV_K - 1):
        tap = _shift_down(up_pre, prev8, FFN_CONV_K - 1 - k)
        taps.append(tap)
        up = up + cw_ref[k:k + 1, :] * tap
    return up, taps


def _ffn_bwd(d_y2, up_pre, up, ffn_cw, w_down, carry=None):
    s_len = d_y2.shape[0]
    tt = min(TT_BIG, s_len)
    nt = s_len // tt
    cw = FF_CW
    nc = D_FF // cw

    def body(dy2_ref, up_ref, upc_ref, cwg_ref, cwv_ref, wd_ref, dup_ref, cs_ref, nxt, cs_acc):
        i = pl.program_id(0)
        c = pl.program_id(1)

        @pl.when(i == 0)
        def _():
            nxt[c] = jnp.zeros((2, SUBLANES, cw), F32)
            cs_acc[c] = jnp.zeros((2, SUBLANES, cw), F32)

        pw = 2 * LANES
        for piece in range(cw // pw):
            cols = slice(piece * pw, (piece + 1) * pw)
            d_act = _dot_nt(dy2_ref[...], wd_ref[pl.ds(pl.multiple_of(c * cw + piece * pw, pw), pw), :])
            uv = upc_ref[1, :, cols]
            gl, dgl = _gelu_and_grad(upc_ref[0, :, cols])
            d_ug = d_act * uv * dgl
            d_uv = d_act * gl
            for half, (d_u, cw_ref) in enumerate(((d_ug, cwg_ref), (d_uv, cwv_ref))):
                nx = nxt[c, half, :, cols]
                x_in = up_ref[half, :, cols].astype(F32)
                d_pre = cw_ref[FFN_CONV_K - 1:FFN_CONV_K, cols] * d_u
                sums = [None] * (FFN_CONV_K + 1)
                sums[FFN_CONV_K - 1] = _colsum(d_u * x_in)
                for k in range(FFN_CONV_K - 1):
                    ahead = _shift_up(d_u, nx, FFN_CONV_K - 1 - k)
                    d_pre = d_pre + cw_ref[k:k + 1, cols] * ahead
                    sums[k] = _colsum(ahead * x_in)
                sums[FFN_CONV_K] = _colsum(d_u)
                pad = jnp.zeros((SUBLANES - FFN_CONV_K - 1, pw), F32)
                cs_acc[c, half, :, cols] += jnp.concatenate(sums + [pad], axis=0)
                nxt[c, half, :, cols] = d_u[0:SUBLANES]
                dup_ref[half, :, cols] = d_pre.astype(BF16)

        for cc in range(nc):
            @pl.when((i == nt - 1) & (c == cc))
            def _():
                cs_ref[:, cc * cw:(cc + 1) * cw] = cs_acc[cc, 0]
                cs_ref[:, D_FF + cc * cw:D_FF + (cc + 1) * cw] = cs_acc[cc, 1]

    row = pl.BlockSpec((tt, D_MODEL), lambda i, c: (nt - 1 - i, 0))
    blk = pl.BlockSpec((2, tt, cw), lambda i, c: (0, nt - 1 - i, c))
    return _call(
        body, "ffn_bwd", (nt, nc),
        in_specs=[row, blk, blk,
                  pl.BlockSpec((FFN_CONV_K, cw), lambda i, c: (0, c)),
                  pl.BlockSpec((FFN_CONV_K, cw), lambda i, c: (0, c + nc)),
                  _whole()],
        out_specs=[blk, _const((SUBLANES, 2 * D_FF))],
        out_shape=[_sds((2, s_len, D_FF), BF16), _sds((SUBLANES, 2 * D_FF), F32)],
        scratch=[pltpu.VMEM((nc, 2, SUBLANES, cw), F32), pltpu.VMEM((nc, 2, SUBLANES, cw), F32)],
        args=(d_y2, up_pre, up, ffn_cw, ffn_cw, w_down), carry=carry)


def _up_bwd(d_up, w_up3, x1, dout, y, w_out, g_pre, sc_f, g_post, gt_m, carry=None):
    s_len = x1.shape[0]
    tt = min(TT_BIG, s_len)

    def body(du_ref, wu_ref, x1_ref, do_ref, y_ref, wo_ref, g2_ref, sc_ref, gp_ref, gt_ref,
             dx1_ref, dy_ref, dyc_ref, vs_ref):
        @pl.when(pl.program_id(0) == 0)
        def _():
            vs_ref[...] = jnp.zeros_like(vs_ref)

        for rows in _row_pieces(tt):
            d_h2 = jnp.zeros((rows.stop - rows.start, D_MODEL), F32)
            for half in range(2):
                for ch in range(FF_CHUNKS):
                    d_h2 = d_h2 + _dot_nt(du_ref[half, rows, ch * FF_CHUNK_W:(ch + 1) * FF_CHUNK_W],
                                          wu_ref[half * FF_CHUNKS + ch])
            n1, r1 = _rms(x1_ref[rows, :])
            ng = n1 * g2_ref[...]
            vs_ref[0:1, :] += _colsum(d_h2)
            vs_ref[1:2, :] += _colsum(d_h2 * ng)
            d_ng = d_h2 * (1.0 + sc_ref[...])
            vs_ref[2:3, :] += _colsum(d_ng * n1)
            d_x1 = do_ref[rows, :] + _rms_bwd(d_ng * g2_ref[...], n1, r1)
            dx1_ref[rows, :] = d_x1
            n_y, r_y = _rms(y_ref[rows, :])
            vs_ref[3:4, :] += _colsum(d_x1 * n_y * gp_ref[...])
            d_on = d_x1 * gt_ref[...]
            vs_ref[4:5, :] += _colsum(d_on * n_y)
            d_y = _rms_bwd(d_on * gp_ref[...], n_y, r_y).astype(BF16)
            dy_ref[rows, :] = d_y
            dyc_ref[rows, :] = _dot_nt(d_y, wo_ref[...])

    row = lambda c: pl.BlockSpec((tt, c), lambda i: (i, 0))
    vec = _const((1, D_MODEL))
    return _call(
        body, "up_bwd", (s_len // tt,),
        in_specs=[pl.BlockSpec((2, tt, D_FF), lambda i: (0, i, 0)), _whole(), row(D_MODEL), row(D_MODEL), row(D_MODEL),
                  _whole(), vec, vec, vec, vec],
        out_specs=[row(D_MODEL), row(D_MODEL), row(LRU_W + GMLP_W), _const((SUBLANES, D_MODEL))],
        out_shape=[_sds((s_len, D_MODEL), F32), _sds((s_len, D_MODEL), BF16), _sds((s_len, LRU_W + GMLP_W), F32),
                   _sds((SUBLANES, D_MODEL), F32)],
        scratch=[], args=(d_up, w_up3, x1, dout, y, w_out, g_pre, sc_f, g_post, gt_m), carry=carry)


def _head_pair_block(hd):
    return (slice((hd // 2) * HEAD_DIM, (hd // 2 + 1) * HEAD_DIM), slice((hd % 2) * HEAD_DIM, (hd % 2 + 1) * HEAD_DIM))


def _mix_bwd(d_ycat, z, hl, conv_w, conv_b, wr_bd, wi_bd, b_r, b_i, lru_a, vn_g, vn_b, w_sp, w_sp_t, b_sp_t,
             g_lru, g_gmlp, carry=None):
    s_len = z.shape[0]
    tt = min(TT_MIX, s_len)
    nt = s_len // tt
    nblk = tt // POS_BLOCK
    hb = tt // SUBLANES

    def body(dyc_ref, z_ref, zh_ref, hl_ref, hh_ref, cw_ref, cb_ref, wr_ref, wi_ref, br_ref, bi_ref, la_ref,
             vg_ref, vb_ref, ws_ref, wst_ref, bst_ref, gl_ref, gg_ref,
             dz_ref, vs_ref, dcw_ref, dwrb_ref, dwib_ref, dws_ref, dbs_ref, nxt_dxc, nxt_a, nxt_lam, dwr_ref, dwi_ref):
        i = pl.program_id(0)
        first_tile = i == nt - 1

        @pl.when(i == 0)
        def _():
            for ref in (vs_ref, dcw_ref, dwr_ref, dwi_ref, dws_ref, dbs_ref, nxt_dxc, nxt_a, nxt_lam):
                ref[...] = jnp.zeros_like(ref)

        lx = z_ref[:, 0:LRU_W]
        gate = z_ref[:, LRU_W:2 * LRU_W]
        gu = z_ref[:, 2 * LRU_W:2 * LRU_W + GMLP_W]
        gv = z_ref[:, 2 * LRU_W + GMLP_W:]
        prev8 = jnp.where(first_tile, 0.0, zh_ref[...])
        hprev8 = jnp.where(first_tile, 0.0, hh_ref[...])

        xc, taps = _lru_conv(lx, prev8, cw_ref, cb_ref[...])
        a_par = la_ref[...]
        sp_a = _softplus(-a_par)
        r, ig, a, mult = _lru_gates(xc, wr_ref, wi_ref, br_ref[...], bi_ref[...], sp_a)
        hl = hl_ref[...]
        h_prev = _shift_down(hl, hprev8, 1)
        ggate, dggate = _gelu_and_grad(gate)
        y_lru = hl * ggate
        n_l, r_l = _rms(y_lru)
        d_nl = dyc_ref[:, 0:LRU_W]
        vs_ref[6:7, :] += _colsum(d_nl * n_l)
        d_yl = _rms_bwd(d_nl * gl_ref[...], n_l, r_l)
        d_hl = d_yl * ggate
        d_gate = d_yl * hl * dggate
        a_up = _shift_up(a, nxt_a[...], 1)
        lam = _scan_rev(a_up, d_hl, nxt_lam[0:1, :])
        nxt_a[...] = jnp.broadcast_to(a[0:1, :], nxt_a.shape)
        nxt_lam[...] = jnp.broadcast_to(lam[0:1, :], nxt_lam.shape)
        ixc = ig * xc
        d_la = lam * h_prev * a - lam * ixc * (a * a) / mult
        d_i = lam * mult * xc
        d_xc = lam * mult * ig
        vs_ref[3:4, :] += _colsum(d_la * r) * (LRU_C * _sigmoid(-a_par))
        d_pr = d_la * (-LRU_C * sp_a) * r * (1.0 - r)
        d_pi = d_i * ig * (1.0 - ig)
        vs_ref[1:2, :] += _colsum(d_pr)
        vs_ref[2:3, :] += _colsum(d_pi)
        dwr_ref[...] += _dot_tn(xc, d_pr)
        dwi_ref[...] += _dot_tn(xc, d_pi)
        d_xc = d_xc + _dot_nt(d_pr, wr_ref[...]) + _dot_nt(d_pi, wi_ref[...])
        vs_ref[0:1, :] += _colsum(d_xc)
        nx = nxt_dxc[...]
        d_lx = cw_ref[LRU_CONV_K - 1:LRU_CONV_K, :] * d_xc
        dcw_ref[LRU_CONV_K - 1:LRU_CONV_K, :] += _colsum(d_xc * lx)
        for k in range(LRU_CONV_K - 1):
            d_lx = d_lx + cw_ref[k:k + 1, :] * _shift_up(d_xc, nx, LRU_CONV_K - 1 - k)
            dcw_ref[k:k + 1, :] += _colsum(d_xc * taps[k])
        nxt_dxc[...] = d_xc[0:SUBLANES]
        dz_ref[:, 0:LRU_W] = d_lx.astype(BF16)
        dz_ref[:, LRU_W:2 * LRU_W] = d_gate.astype(BF16)

        u, du = _gelu_and_grad(gu)
        v, vhat, rs, dav = _gmlp_v(gv, vg_ref[...], vb_ref[...])
        mask = _ws_mask()
        sp_parts = []
        for nb in range(nblk):
            rowp = []
            for g in range(N_GROUPS):
                wsm = jnp.where(mask, ws_ref[g], 0.0)
                vblk = v[nb * POS_BLOCK:(nb + 1) * POS_BLOCK, g * LANES:(g + 1) * LANES]
                rowp.append(_dot(wsm, vblk) + bst_ref[:, g:g + 1])
            sp_parts.append(jnp.concatenate(rowp, axis=1))
        sp = jnp.concatenate(sp_parts, axis=0) if nblk > 1 else sp_parts[0]
        y_g = u * sp
        n_g, r_g = _rms(y_g)
        d_ng = dyc_ref[:, LRU_W:]
        vs_ref[7:8, :] += _colsum(d_ng * n_g)
        d_yg = _rms_bwd(d_ng * gg_ref[...], n_g, r_g)
        d_gu = d_yg * sp * du
        d_sp = d_yg * u
        mask_t = _ws_mask(transposed=True)
        ones8 = jnp.ones((SUBLANES, LANES), F32)
        dv_parts = []
        for nb in range(nblk):
            rowp = []
            for g in range(N_GROUPS):
                rs_, cs_ = slice(nb * POS_BLOCK, (nb + 1) * POS_BLOCK), slice(g * LANES, (g + 1) * LANES)
                dsp_blk = d_sp[rs_, cs_]
                dbs_ref[g:g + 1, :] += lax.dot_general(
                    ones8, dsp_blk, (((1,), (1,)), ((), ())), preferred_element_type=F32,
                    precision=lax.Precision.HIGHEST)[0:1, :]
                dws_ref[g] += _dot_nt(dsp_blk, v[rs_, cs_])
                wsm_t = jnp.where(mask_t, wst_ref[g], 0.0)
                rowp.append(_dot(wsm_t, dsp_blk))
            dv_parts.append(jnp.concatenate(rowp, axis=1))
        d_v = jnp.concatenate(dv_parts, axis=0) if nblk > 1 else dv_parts[0]
        vs_ref[4:5, :] += _colsum(d_v * vhat)
        vs_ref[5:6, :] += _colsum(d_v)
        d_vh = d_v * vg_ref[...]
        d_av = rs * (d_vh - jnp.mean(d_vh, axis=-1, keepdims=True)
                     - vhat * jnp.mean(d_vh * vhat, axis=-1, keepdims=True))
        dz_ref[:, 2 * LRU_W:2 * LRU_W + GMLP_W] = d_gu.astype(BF16)
        dz_ref[:, 2 * LRU_W + GMLP_W:] = (d_av * dav).astype(BF16)

        @pl.when(i == nt - 1)
        def _():
            for hd in range(N_HEADS):
                blk = slice(hd * HEAD_DIM, (hd + 1) * HEAD_DIM)
                dwrb_ref[_head_pair_block(hd)] = dwr_ref[blk, blk]
                dwib_ref[_head_pair_block(hd)] = dwi_ref[blk, blk]
            for g in range(N_GROUPS):
                dws_ref[g] = jnp.where(mask, dws_ref[g], 0.0)

    rev = lambda c: pl.BlockSpec((tt, c), lambda i: (nt - 1 - i, 0))
    halo = pl.BlockSpec((SUBLANES, LRU_W), lambda i: (jnp.maximum((nt - 1 - i) * hb - 1, 0), 0))
    v512 = _const((1, LRU_W))
    return _call(
        body, "mix_bwd", (nt,),
        in_specs=[rev(LRU_W + GMLP_W), rev(IN_COLS), halo, rev(LRU_W), halo,
                  _const((LRU_CONV_K, LRU_W)), v512, _whole(), _whole(), v512, v512, v512, v512, v512,
                  _whole(), _whole(), _whole(), v512, v512],
        out_specs=[rev(IN_COLS), _const((SUBLANES, LRU_W)), _const((SUBLANES, LRU_W)),
                   _const((LRU_W // 2, 2 * HEAD_DIM)), _const((LRU_W // 2, 2 * HEAD_DIM)),
                   _const((N_GROUPS, POS_BLOCK, POS_BLOCK)), _const((SUBLANES, POS_BLOCK))],
        out_shape=[_sds((s_len, IN_COLS), BF16), _sds((SUBLANES, LRU_W), F32), _sds((SUBLANES, LRU_W), F32),
                   _sds((LRU_W // 2, 2 * HEAD_DIM), F32), _sds((LRU_W // 2, 2 * HEAD_DIM), F32),
                   _sds((N_GROUPS, POS_BLOCK, POS_BLOCK), F32), _sds((SUBLANES, POS_BLOCK), F32)],
        scratch=[pltpu.VMEM((SUBLANES, LRU_W), F32), pltpu.VMEM((SUBLANES, LRU_W), F32),
                 pltpu.VMEM((SUBLANES, LRU_W), F32), pltpu.VMEM((LRU_W, LRU_W), F32), pltpu.VMEM((LRU_W, LRU_W), F32)],
        args=(d_ycat, z, z, hl, hl, conv_w, conv_b, wr_bd, wi_bd, b_r, b_i, lru_a, vn_g, vn_b, w_sp, w_sp_t, b_sp_t,
              g_lru, g_gmlp), carry=carry)


def _in_bwd(d_z, w_in, x, d_x1, g, sc, carry=None):
    s_len = x.shape[0]
    tt = min(TT_BIG, s_len)

    def body(dz_ref, w_ref, x_ref, dx1_ref, g_ref, sc_ref, gx_ref, vs_ref):
        @pl.when(pl.program_id(0) == 0)
        def _():
            vs_ref[...] = jnp.zeros_like(vs_ref)

        for rows in _row_pieces(tt):
            d_h = _dot_nt(dz_ref[rows, :], w_ref[...])
            n, r = _rms(x_ref[rows, :])
            vs_ref[0:1, :] += _colsum(d_h)
            vs_ref[1:2, :] += _colsum(d_h * n * g_ref[...])
            d_ng = d_h * (1.0 + sc_ref[...])
            vs_ref[2:3, :] += _colsum(d_ng * n)
            gx_ref[rows, :] = dx1_ref[rows, :] + _rms_bwd(d_ng * g_ref[...], n, r)

    row = lambda c: pl.BlockSpec((tt, c), lambda i: (i, 0))
    vec = _const((1, D_MODEL))
    return _call(
        body, "in_bwd", (s_len // tt,),
        in_specs=[row(IN_COLS), _whole(), row(D_MODEL), row(D_MODEL), vec, vec],
        out_specs=[row(D_MODEL), _const((SUBLANES, D_MODEL))],
        out_shape=[_sds((s_len, D_MODEL), F32), _sds((SUBLANES, D_MODEL), F32)],
        scratch=[], args=(d_z, w_in, x, d_x1, g, sc), carry=carry)


def _wgrad(a, b, name, by_rows=False, carry=None):
    s_len, k_dim = a.shape
    halves = b.ndim == 3
    n_dim = b.shape[-1] * (2 if halves else 1)

    def body(a_ref, b_ref, ob_ref, own_ref):
        out = _dot_tn(a_ref[...], b_ref[0] if halves else b_ref[...])
        ob_ref[...] = out.astype(BF16)

        @pl.when(pl.program_id(0) == _dev_index(_my_pos()))
        def _():
            own_ref[...] = out

    if by_rows:
        tile = k_dim // N_DEV
        a_spec = pl.BlockSpec((s_len, tile), lambda j: (0, j))
        b_spec = pl.BlockSpec((s_len, n_dim), lambda j: (0, 0))
        o_spec = pl.BlockSpec((tile, n_dim), lambda j: (j, 0))
        own_shape = (tile, n_dim)
    else:
        tile = n_dim // N_DEV
        a_spec = pl.BlockSpec((s_len, k_dim), lambda j: (0, 0))
        if halves:
            per_half = N_DEV // 2
            b_spec = pl.BlockSpec((1, s_len, tile), lambda j: (j // per_half, 0, j % per_half))
        else:
            b_spec = pl.BlockSpec((s_len, tile), lambda j: (0, j))
        o_spec = pl.BlockSpec((k_dim, tile), lambda j: (0, j))
        own_shape = (k_dim, tile)
    return _call(
        body, name, (N_DEV,), in_specs=[a_spec, b_spec], out_specs=[o_spec, _const(own_shape)],
        out_shape=[_sds((k_dim, n_dim), BF16), _sds(own_shape, F32)],
        scratch=[], args=(a, b), carry=carry)


def _adam_math(w, g, m, v):
    m = ADAM_B1 * m + (1.0 - ADAM_B1) * g
    v = ADAM_B2 * v + (1.0 - ADAM_B2) * (g * g)
    m_hat = m / (1.0 - ADAM_B1 ** ADAM_STEP)
    v_hat = v / (1.0 - ADAM_B2 ** ADAM_STEP)
    delta = -ADAM_LR * (m_hat / (jnp.sqrt(v_hat) + ADAM_EPS) + ADAM_WD * w)
    return delta, m, v


def _row_tile(rows, cols, n_f32_arrays):
    budget = VMEM_LIMIT // 2
    tr = rows
    while tr % 2 == 0 and tr // 2 >= SUBLANES and (tr // 2) % SUBLANES == 0 and tr * cols * 4 * n_f32_arrays * 2 > budget:
        tr //= 2
    return tr


def _adamw_sum_block(w_ref, g_ref, r_refs, m_ref, v_ref, go_ref, d_ref, mo_ref, vo_ref):
    g = g_ref[...]
    for r_ref in r_refs:
        for k in range(r_ref.shape[0]):
            g = g + r_ref[k].astype(F32)
    go_ref[0] = g
    d_ref[0], mo_ref[0], vo_ref[0] = _adam_math(w_ref[0], g, m_ref[0], v_ref[0])


def _adamw_rider(parts, steps):
    inputs, in_specs, out_shape, out_specs, n_recvs = [], [], [], [], []
    for w, g_own, recv, m, v in parts:
        _, rows, cols = w.shape
        tr = rows // steps
        blk = pl.BlockSpec((1, tr, cols), lambda i: (0, i, 0))
        inputs += [w, g_own, *recv, m, v]
        in_specs += ([blk, pl.BlockSpec((tr, cols), lambda i: (i, 0))]
                     + [pl.BlockSpec((r.shape[0], tr, cols), lambda i: (0, i, 0)) for r in recv] + [blk, blk])
        out_shape += [_sds((1, rows, cols), F32)] * 4
        out_specs += [blk] * 4
        n_recvs.append(len(recv))

    def each(ins, outs, scr):
        for n_recv in n_recvs:
            _adamw_sum_block(ins[0], ins[1], ins[2:2 + n_recv], ins[2 + n_recv], ins[3 + n_recv], *outs[:4])
            ins, outs = ins[4 + n_recv:], outs[4:]

    return _Carry(inputs=inputs, in_specs=in_specs, out_shape=out_shape, out_specs=out_specs, scratch=[], each=each)


def _adamw_sum(w, g_own, recv, m, v, name):
    _, rows, cols = w.shape
    n_recv = len(recv)
    tr = _row_tile(rows, cols, 10)
    nb = rows // tr

    def body(w_ref, g_ref, *rest):
        _adamw_sum_block(w_ref, g_ref, rest[:n_recv], *rest[n_recv:])

    blk = pl.BlockSpec((1, tr, cols), lambda i: (0, i, 0))
    return pl.pallas_call(
        body, name=name, grid=(nb,),
        in_specs=[blk, pl.BlockSpec((tr, cols), lambda i: (i, 0))]
        + [pl.BlockSpec((r.shape[0], tr, cols), lambda i: (0, i, 0)) for r in recv] + [blk, blk],
        out_specs=[blk] * 4, out_shape=[_sds((1, rows, cols), F32)] * 4,
        compiler_params=_cparams(("arbitrary",)),
    )(w, g_own, *recv, m, v)


def _row_of_each(ref, row):
    cols = ref.shape[1]
    rows = _rows((N_DEV, cols))
    out = jnp.zeros((N_DEV, cols), F32)
    for d in range(N_DEV):
        picked = ref[d * SUBLANES + row:d * SUBLANES + row + 1, :]
        out = jnp.where(rows == d, jnp.broadcast_to(picked, (N_DEV, cols)), out)
    return out


def _my_columns(full, width, me):
    out = jnp.zeros(full.shape[:-1] + (width,), F32)
    for d in range(N_DEV):
        out = out + jnp.where(me == d, full[:, d * width:(d + 1) * width], 0.0)
    return out


def _adamw_wada(c_all, vs_in_all, vs_up_all, vs_ffn_all, w, m, v):
    _, rows, cols = w.shape

    def body(c_ref, vi_ref, vu_ref, vf_ref, w_ref, m_ref, v_ref, go_ref, d_ref, mo_ref, vo_ref):
        me = _dev_index(_my_pos())
        cv = _row_of_each(c_ref, 0)
        ca = cv * _sigmoid(cv)
        dmod = jnp.concatenate([_row_of_each(vi_ref, 0), _row_of_each(vi_ref, 1), _row_of_each(vu_ref, 3),
                                _row_of_each(vu_ref, 0), _row_of_each(vu_ref, 1), _row_of_each(vf_ref, 0)], axis=1)
        dm = _my_columns(dmod, cols, me)
        g = lax.dot_general(ca, dm, (((0,), (0,)), ((), ())), preferred_element_type=F32,
                            precision=lax.Precision.HIGHEST)
        go_ref[0] = g
        d_ref[0], mo_ref[0], vo_ref[0] = _adam_math(w_ref[0], g, m_ref[0], v_ref[0])

    return pl.pallas_call(
        body, name="adamw_w_ada", out_shape=[_sds((1, rows, cols), F32)] * 4,
        in_specs=[_whole()] * 7, out_specs=[_whole()] * 4,
        compiler_params=_cparams(),
    )(c_all, vs_in_all, vs_up_all, vs_ffn_all, w, m, v)


def _adamw_small(gathered, reduced, params, conv_params):
    names = list(params) + list(conv_params)
    allp = {**params, **conv_params}
    n_g = len(gathered) + len(reduced)

    def body(*refs):
        g_refs = refs[:n_g]
        p_refs = refs[n_g:n_g + 3 * len(names)]
        o_refs = refs[n_g + 3 * len(names):]
        me = _dev_index(_my_pos())

        def total(ref):
            s = ref[0:SUBLANES, :]
            for d in range(1, N_DEV):
                s = s + ref[d * SUBLANES:(d + 1) * SUBLANES, :]
            return s

        vs_in, vs_up, vs_ffn, loss = [total(r) for r in g_refs[:4]]
        cs, vs_mix, dcw, dwr, dwi, dws, dbs = [r[...] for r in g_refs[4:]]
        o_refs[-1][...] = loss[0:1, 0:1]
        mine = lambda full, width: _my_columns(full, width, me)

        all_ = (slice(None), slice(None))
        heads = lambda row: [((0, slice(h, h + 1), slice(None)), row[:, h * HEAD_DIM:(h + 1) * HEAD_DIM])
                             for h in range(N_HEADS)]
        blocks = lambda pairs: [((0, h), pairs[_head_pair_block(h)]) for h in range(N_HEADS)]
        pieces = {
            "b_ada": [((slice(None), slice(k * D_MODEL, (k + 1) * D_MODEL)), row) for k, row in enumerate(
                (vs_in[0:1], vs_in[1:2], vs_up[3:4], vs_up[0:1], vs_up[1:2], vs_ffn[0:1]))],
            "g_mix_pre": [(all_, vs_in[2:3])], "g_mix_post": [(all_, vs_up[4:5])],
            "g_ffn_pre": [(all_, vs_up[2:3])], "g_ffn_post": [(all_, vs_ffn[1:2])],
            "conv_b": [(all_, vs_mix[0:1])], "b_rgate": heads(vs_mix[1:2]), "b_igate": heads(vs_mix[2:3]),
            "lru_a": [(all_, vs_mix[3:4])], "v_norm_g": [(all_, vs_mix[4:5])], "v_norm_b": [(all_, vs_mix[5:6])],
            "g_lru_out": [(all_, vs_mix[6:7])], "g_gmlp_out": [(all_, vs_mix[7:8])],
            "w_rgate": blocks(dwr), "w_igate": blocks(dwi),
            "w_spatial": [((0, g), dws[g * POS_BLOCK:(g + 1) * POS_BLOCK, :]) for g in range(N_GROUPS)],
            "b_spatial": [((0,), dbs[0:N_GROUPS])],
            "ffn_conv_b": [(all_, cs[FFN_CONV_K:FFN_CONV_K + 1])],
            "conv_w": [((0,), mine(dcw[0:LRU_CONV_K], LRU_W // N_DEV))],
        }
        ffn_cw_rows = mine(cs[0:FFN_CONV_K], 2 * D_FF // N_DEV)
        pieces["ffn_conv_w"] = [((k,), ffn_cw_rows[k:k + 1]) for k in range(FFN_CONV_K)]
        for n_i, name in enumerate(names):
            w_ref, m_ref, v_ref = p_refs[3 * n_i:3 * n_i + 3]
            go_ref, d_ref, mo_ref, vo_ref = o_refs[4 * n_i:4 * n_i + 4]
            for idx, g in pieces[name]:
                go_ref[idx] = g
                d_ref[idx], mo_ref[idx], vo_ref[idx] = _adam_math(w_ref[idx], g, m_ref[idx], v_ref[idx])

    flat_params = [a for n in names for a in allp[n]]
    out_shape = [_sds(allp[n][0].shape, F32) for n in names for _ in range(4)] + [_sds((1, 1), F32)]
    outs = pl.pallas_call(
        body, name="adamw_small", out_shape=out_shape,
        in_specs=[_whole()] * (n_g + len(flat_params)), out_specs=[_whole()] * len(out_shape),
        compiler_params=_cparams(),
    )(*gathered, *reduced, *flat_params)
    return {n: outs[4 * i:4 * i + 4] for i, n in enumerate(names)}, outs[-1]


def _my_pos():
    return lax.axis_index("x"), lax.axis_index("y"), lax.axis_index("c")


def _flip(pos, k):
    x, y, c = pos
    return (1 - x if k & 4 else x, 1 - y if k & 2 else y, 1 - c if k & 1 else c)


def _dev_index(pos):
    x, y, c = pos
    return 4 * x + 2 * y + c


def _all_gather_small(ins, outs, send_sems, recv_sems):
    n = len(ins)
    me = _my_pos()

    def slot(a, pos):
        rows = ins[a].shape[0]
        return outs[a].at[pl.ds(pl.multiple_of(_dev_index(pos) * rows, SUBLANES), rows), :]

    def copy(a, k, block):
        return pltpu.make_async_remote_copy(
            src_ref=ins[a], dst_ref=slot(a, block), send_sem=send_sems.at[a, k - 1], recv_sem=recv_sems.at[a, k - 1],
            device_id=_flip(me, k), device_id_type=MESH)

    sends = [copy(a, k, me) for a in range(n) for k in range(1, N_DEV)]
    for cp in sends:
        cp.start()
    for a in range(n):
        rows = ins[a].shape[0]
        outs[a][pl.ds(pl.multiple_of(_dev_index(me) * rows, SUBLANES), rows), :] = ins[a][...]
    for a in range(n):
        for k in range(1, N_DEV):
            copy(a, k, _flip(me, k)).wait_recv()
    for cp in sends:
        cp.wait_send()


def _prologue(c, cw, fcw, w_ada, b_ada, carry):
    cols = w_ada.shape[1]

    def body(c_ref, cw_ref, fcw_ref, w_ref, b_ref, call_ref, cwall_ref, fcwall_ref, modall_ref, mod_scr,
             c8, cw8, fcw8, s1, r1, s2, r2, start_carry):
        c8[...] = jnp.broadcast_to(c_ref[...], c8.shape)
        cw8[...] = jnp.zeros(cw8.shape, F32)
        cw8[0:LRU_CONV_K, :] = cw_ref[...]
        fcw8[...] = jnp.zeros(fcw8.shape, F32)
        for k in range(FFN_CONV_K):
            fcw8[k:k + 1, :] = fcw_ref[k]
        _all_gather_small([c8, cw8, fcw8], [call_ref, cwall_ref, fcwall_ref], s1, r1)
        start_carry()
        cv = _row_of_each(call_ref, 0)
        ca = cv * _sigmoid(cv)
        b_cols = _my_columns(b_ref[...], cols, _dev_index(_my_pos()))
        mod_scr[...] = jnp.dot(ca, w_ref[...], preferred_element_type=F32, precision=lax.Precision.HIGHEST) + b_cols
        _all_gather_small([mod_scr], [modall_ref], s2, r2)

    sem = lambda n: pltpu.SemaphoreType.DMA((n, N_DEV - 1))
    return _call(
        body, "prologue", (1,), in_specs=[_whole()] * 5, out_specs=[_whole()] * 4,
        out_shape=[_sds((N_DEV * SUBLANES, a.shape[-1]), F32) for a in (c, cw, fcw)]
        + [_sds((N_DEV * N_DEV, cols), F32)],
        scratch=[pltpu.VMEM((N_DEV, cols), F32)] + [pltpu.VMEM((SUBLANES, a.shape[-1]), F32) for a in (c, cw, fcw)]
        + [sem(3), sem(3), sem(1), sem(1)],
        args=(c, cw, fcw, w_ada, b_ada), carry=carry, body_starts_carry=True)


def _reduce_small(gath, red, carry=None):
    n_g, n_r = len(gath), len(red)
    chip_flips = CHIP_FLIPS

    def body(*refs, start_carry):
        g_in, r_in = refs[:n_g], refs[n_g:n_g + n_r]
        g_out, r_out = refs[n_g + n_r:2 * n_g + n_r], refs[2 * n_g + n_r:2 * (n_g + n_r)]
        scr = refs[2 * (n_g + n_r):]
        sib, land = scr[:n_r], scr[n_r:2 * n_r]
        g_send, g_recv, s_send, s_recv, i_send, i_recv, f_send, f_recv = scr[2 * n_r:]
        me = _my_pos()
        c = me[2]
        sibling = _flip(me, 1)

        def slot(a, pos):
            return g_out[a].at[pl.ds(pl.multiple_of(_dev_index(pos) * SUBLANES, SUBLANES), SUBLANES), :]

        def gcopy(a, k):
            return pltpu.make_async_remote_copy(
                src_ref=g_in[a], dst_ref=slot(a, me), send_sem=g_send.at[a, k - 1], recv_sem=g_recv.at[a, k - 1],
                device_id=_flip(me, k), device_id_type=MESH)

        def scopy(a):
            return pltpu.make_async_remote_copy(
                src_ref=r_in[a], dst_ref=sib[a], send_sem=s_send.at[a], recv_sem=s_recv.at[a],
                device_id=sibling, device_id_type=MESH)

        def icopy(a, j):
            return pltpu.make_async_remote_copy(
                src_ref=r_out[a], dst_ref=land[a].at[j], send_sem=i_send.at[a, j], recv_sem=i_recv.at[a, j],
                device_id=_flip(me, chip_flips[j]), device_id_type=MESH)

        def fcopy(a, j):
            return pltpu.make_async_remote_copy(
                src_ref=land[a].at[j], dst_ref=land[a].at[j], send_sem=f_send.at[a, j], recv_sem=f_recv.at[a, j],
                device_id=sibling, device_id_type=MESH)

        gathers = [gcopy(a, k) for a in range(n_g) for k in range(1, N_DEV)]
        swaps = [scopy(a) for a in range(n_r)]
        for cp in gathers + swaps:
            cp.start()
        for a in range(n_g):
            g_out[a][pl.ds(pl.multiple_of(_dev_index(me) * SUBLANES, SUBLANES), SUBLANES), :] = g_in[a][...]
        for a in range(n_r):
            swaps[a].wait_recv()
            r_out[a][...] = r_in[a][...] + sib[a][...]

        for core in range(2):
            @pl.when(c == core)
            def _():
                for a in range(core, n_r, 2):
                    for j in range(3):
                        icopy(a, j).start()

        start_carry()

        for core in range(2):
            mine = [a for a in range(n_r) if a % 2 == core]
            theirs = [a for a in range(n_r) if a % 2 != core]

            @pl.when(c == core)
            def _():
                out = [icopy(a, j) for a in mine for j in range(3)]
                fwd = []
                for a in mine:
                    for j in range(3):
                        icopy(a, j).wait_recv()
                        cp = fcopy(a, j)
                        cp.start()
                        fwd.append(cp)
                for a in theirs:
                    for j in range(3):
                        fcopy(a, j).wait_recv()
                for cp in out + fwd:
                    cp.wait_send()

        for a in range(n_r):
            r_out[a][...] = (r_out[a][...] + land[a][1]) + (land[a][0] + land[a][2])
        for a in range(n_g):
            for k in range(1, N_DEV):
                pltpu.make_async_remote_copy(
                    src_ref=g_in[a], dst_ref=slot(a, _flip(me, k)), send_sem=g_send.at[a, k - 1],
                    recv_sem=g_recv.at[a, k - 1], device_id=_flip(me, k), device_id_type=MESH).wait_recv()
        for cp in gathers + swaps:
            cp.wait_send()

    shapes = [tuple(a.shape) for a in red]
    outs, carried = _call(
        body, "reduce_small", (1,), in_specs=[_whole()] * (n_g + n_r), out_specs=[_whole()] * (n_g + n_r),
        out_shape=[_sds((N_DEV * SUBLANES, a.shape[1]), F32) for a in gath] + [_sds(s, F32) for s in shapes],
        scratch=[pltpu.VMEM(s, F32) for s in shapes] + [pltpu.VMEM((3,) + s, F32) for s in shapes]
        + [pltpu.SemaphoreType.DMA((n_g, N_DEV - 1)), pltpu.SemaphoreType.DMA((n_g, N_DEV - 1)),
           pltpu.SemaphoreType.DMA((n_r,)), pltpu.SemaphoreType.DMA((n_r,)),
           pltpu.SemaphoreType.DMA((n_r, 3)), pltpu.SemaphoreType.DMA((n_r, 3)),
           pltpu.SemaphoreType.DMA((n_r, 3)), pltpu.SemaphoreType.DMA((n_r, 3))],
        args=tuple(gath) + tuple(red), carry=carry, body_starts_carry=True)
    return (outs[:n_g], outs[n_g:]), carried


STACKED = "stacked"


def _region(ref, shard_shape, col_sharded, pos):
    r, cdim = shard_shape
    d = _dev_index(pos)
    if col_sharded == STACKED:
        return ref.at[d]
    if col_sharded:
        return ref.at[:, pl.ds(pl.multiple_of(d * cdim, LANES), cdim)]
    return ref.at[pl.ds(pl.multiple_of(d * r, 2 * SUBLANES), r), :]


def _gather_carry(shards, col_sharded):
    n_w = len(shards)
    shapes = [tuple(s.shape) for s in shards]
    full_shapes = [(N_DEV,) + s if cs == STACKED else (s[0], s[1] * N_DEV) if cs else (s[0] * N_DEV, s[1])
                   for s, cs in zip(shapes, col_sharded)]

    def tools(out_refs, scr):
        send_sems, recv_sems = scr[n_w], scr[n_w + 1]
        me = _my_pos()
        x, y, c = me
        sibling = (x, y, 1 - c)
        chips = [(1 - x, y), (x, 1 - y), (1 - x, 1 - y)]

        def region(w, pos):
            return _region(out_refs[w], shapes[w], col_sharded[w], pos)

        def copy(w, k, block, to, src=None):
            return pltpu.make_async_remote_copy(
                src_ref=region(w, block) if src is None else src, dst_ref=region(w, block),
                send_sem=send_sems.at[w, k], recv_sem=recv_sems.at[w, k], device_id=to, device_id_type=MESH)

        def first(w):
            return [copy(w, 0, me, sibling, src=scr[w])] + [
                copy(w, 1 + j, me, (*chip, c), src=scr[w]) for j, chip in enumerate(chips)]

        def mine(w):
            return pltpu.make_async_copy(scr[w], region(w, me), scr[n_w + 2].at[w])

        return me, c, sibling, chips, copy, first, mine

    def start(ins, outs, scr):
        _, _, _, _, _, first, mine = tools(outs, scr)
        for w in range(n_w):
            scr[w][...] = ins[w][...].astype(BF16)
            for cp in first(w) + [mine(w)]:
                cp.start()

    def finish(ins, outs, scr):
        me, c, sibling, chips, copy, first, mine = tools(outs, scr)
        passed = []
        for w in range(n_w):
            for j, chip in enumerate(chips):
                copy(w, 1 + j, (*chip, c), me).wait_recv()
                fwd = copy(w, 4 + j, (*chip, c), sibling)
                fwd.start()
                passed.append(fwd)
        for w in range(n_w):
            copy(w, 0, sibling, me).wait_recv()
            for j, chip in enumerate(chips):
                copy(w, 4 + j, (*chip, 1 - c), me).wait_recv()
        for w in range(n_w):
            for cp in first(w):
                cp.wait_send()
            mine(w).wait()
        for cp in passed:
            cp.wait_send()

    return _Carry(
        inputs=list(shards), in_specs=[_whole()] * n_w,
        out_shape=[_sds(s, BF16) for s in full_shapes], out_specs=[_any()] * n_w,
        scratch=[pltpu.VMEM(s, BF16) for s in shapes]
        + [pltpu.SemaphoreType.DMA((n_w, N_DEV - 1)), pltpu.SemaphoreType.DMA((n_w, N_DEV - 1)),
           pltpu.SemaphoreType.DMA((n_w,))],
        start=start, finish=finish)


CHIP_FLIPS = (4, 2, 6)


def _pair_reduce(g_bf, g_own, col_sharded):
    shape = tuple(g_own.shape)
    n = len(CHIP_FLIPS)

    def body(g_ref, own_ref, hown_ref, hout_ref, mine, sib, send_sems, recv_sems, local_sems):
        me = _my_pos()
        sibling = _flip(me, 1)
        flips = (0,) + CHIP_FLIPS

        def region(pos):
            return _region(g_ref, shape, col_sharded, pos)

        local = [pltpu.make_async_copy(region(_flip(me, f)), mine.at[s], local_sems.at[s])
                 for s, f in enumerate(CHIP_FLIPS)]
        sends = [pltpu.make_async_remote_copy(
            src_ref=region(_flip(sibling, f)), dst_ref=sib.at[s], send_sem=send_sems.at[s], recv_sem=recv_sems.at[s],
            device_id=sibling, device_id_type=MESH) for s, f in enumerate(flips)]
        for cp in local + sends:
            cp.start()
        for cp in local:
            cp.wait()
        for cp in sends:
            cp.wait_recv()
        hown_ref[...] = own_ref[...] + sib[0].astype(F32)
        for s in range(n):
            hout_ref[s] = (mine[s].astype(F32) + sib[s + 1].astype(F32)).astype(BF16)
        for cp in sends:
            cp.wait_send()

    return pl.pallas_call(
        body, name="pair_reduce", out_shape=[_sds(shape, F32), _sds((n,) + shape, BF16)],
        in_specs=[_any(), _whole()], out_specs=[_whole(), _whole()],
        scratch_shapes=[pltpu.VMEM((n,) + shape, BF16), pltpu.VMEM((n + 1,) + shape, BF16),
                        pltpu.SemaphoreType.DMA((n + 1,)), pltpu.SemaphoreType.DMA((n + 1,)),
                        pltpu.SemaphoreType.DMA((n,))],
        compiler_params=pltpu.CompilerParams(vmem_limit_bytes=VMEM_LIMIT),
    )(g_bf, g_own)


def _chip_scatter_carry(h_out):
    n = len(CHIP_FLIPS)

    def copies(ins, outs, scr):
        send_sems, recv_sems = scr
        me = _my_pos()
        return [pltpu.make_async_remote_copy(
            src_ref=ins[0].at[j], dst_ref=outs[0].at[j], send_sem=send_sems.at[j], recv_sem=recv_sems.at[j],
            device_id=_flip(me, CHIP_FLIPS[j]), device_id_type=MESH) for j in range(n)]

    def start(ins, outs, scr):
        for cp in copies(ins, outs, scr):
            cp.start()

    def finish(ins, outs, scr):
        cps = copies(ins, outs, scr)
        for cp in cps:
            cp.wait_recv()
        for cp in cps:
            cp.wait_send()

    return _Carry(inputs=[h_out], in_specs=[_any()], out_shape=[_sds(tuple(h_out.shape), BF16)], out_specs=[_any()],
                  scratch=[pltpu.SemaphoreType.DMA((n,)), pltpu.SemaphoreType.DMA((n,))], start=start, finish=finish)


def _scatter_carry(grads_bf, shard_shapes, col_sharded, relations):
    n_w = len(grads_bf)
    shapes = [tuple(s) for s in shard_shapes]

    def copies(ins, outs, scr):
        send_sems, recv_sems = scr
        me = _my_pos()
        out = []
        for w in range(n_w):
            for i, k in enumerate(relations[w]):
                peer = _flip(me, k)
                out.append(pltpu.make_async_remote_copy(
                    src_ref=_region(ins[w], shapes[w], col_sharded[w], peer), dst_ref=outs[w].at[i],
                    send_sem=send_sems.at[w, i], recv_sem=recv_sems.at[w, i],
                    device_id=peer, device_id_type=MESH))
        return out

    def start(ins, outs, scr):
        for cp in copies(ins, outs, scr):
            cp.start()

    def finish(ins, outs, scr):
        cps = copies(ins, outs, scr)
        for cp in cps:
            cp.wait_recv()
        for cp in cps:
            cp.wait_send()

    return _Carry(
        inputs=list(grads_bf), in_specs=[_any()] * n_w,
        out_shape=[_sds((len(r),) + s, BF16) for r, s in zip(relations, shapes)], out_specs=[_any()] * n_w,
        scratch=[pltpu.SemaphoreType.DMA((n_w, N_DEV - 1)), pltpu.SemaphoreType.DMA((n_w, N_DEV - 1))],
        start=start, finish=finish)


def _block_diag(w):
    eye = jnp.eye(N_HEADS, dtype=w.dtype)
    return (eye[:, None, :, None] * w[:, :, None, :]).reshape(N_HEADS * HEAD_DIM, N_HEADS * HEAD_DIM)


def _columns_from_devices(gathered, rows):
    w = gathered.shape[1]
    return gathered.reshape(N_DEV, SUBLANES, w)[:, :rows].transpose(1, 0, 2).reshape(rows, N_DEV * w)


def _local_step(x2, target, mod, w_in_f, w_full, conv_w_full, ffn_cw_full,
                g_mix_pre, g_mix_post, conv_b, w_rgate, b_rgate, w_igate, b_igate, lru_a, v_norm_g, v_norm_b,
                w_spatial, b_spatial, g_lru_out, g_gmlp_out, g_ffn_pre, g_ffn_post, ffn_conv_b,
                gather=None, scatter=None, adam=None):
    sh_m, sc_m, gt_m, sh_f, sc_f, gt_f = [mod[k] for k in range(N_MOD)]
    wr_bd = _block_diag(w_rgate[0]).astype(BF16)
    wi_bd = _block_diag(w_igate[0]).astype(BF16)
    b_r = b_rgate.reshape(1, LRU_W)
    b_i = b_igate.reshape(1, LRU_W)
    b_sp_t = b_spatial[0].T
    w_sp_t = jnp.swapaxes(w_spatial[0], 1, 2)

    def arriving(*names):
        return gather(*names) if gather else None

    near, far = (1, 2, 3, 4, 5), (6, 7)

    def leaving(*parts):
        return scatter(parts) if scatter else None

    def received(recv, parts, outs):
        for (name, _, _), out in zip(parts, outs):
            recv.setdefault(name, []).append(out)

    mix_params = (conv_w_full, conv_b, wr_bd, wi_bd, b_r, b_i, lru_a, v_norm_g, v_norm_b)
    w_out_f = w_full["w_out"]
    (z, h, ycat, hl, y, x1, h2), got = _mix_fwd(
        x2, sh_m, sc_m, g_mix_pre, w_in_f, *mix_params, w_spatial[0], b_sp_t, g_lru_out, g_gmlp_out,
        w_out_f, g_mix_post, gt_m, g_ffn_pre, sc_f, sh_f, carry=arriving("w_up"))
    w_up_f = got[0] if gather else w_full["w_up"]
    (up_pre, up, act), got = _ffn_fwd(h2, w_up_f, ffn_cw_full, ffn_conv_b, carry=arriving("w_down"))
    w_down_f = got[0] if gather else w_full["w_down"]
    d_y2, dout, loss_acc, vs_ffn = _ffn_tail(act, w_down_f, x1, gt_f, g_ffn_post, target)

    recv, updated = {}, {}

    def updating(grads):
        if not adam:
            return None
        return _adamw_rider([(adam[n][0], g[1], recv[n], adam[n][1], adam[n][2]) for n, g in grads.items()], N_DEV)

    def updates(grads, outs):
        for j, n in enumerate(grads):
            updated[n] = tuple(outs[4 * j:4 * j + 4])

    gw_down, _ = _wgrad(act, d_y2, "wgrad_down", by_rows=True)
    parts = [("w_down", gw_down[0], near + far)]
    (d_up, cs_ffn), got = _ffn_bwd(d_y2, up_pre, up, ffn_cw_full, w_down_f, carry=leaving(*parts))
    received(recv, parts, got)
    gw_up, got = _wgrad(h2, d_up, "wgrad_up", carry=updating(dict(w_down=gw_down)))
    updates(dict(w_down=gw_down), got)
    parts = [("w_up", gw_up[0], near)]
    (d_x1, d_y, d_ycat, vs_up), got = _up_bwd(
        d_up, w_up_f, x1, dout, y, w_out_f, g_ffn_pre, sc_f, g_mix_post, gt_m, carry=leaving(*parts))
    received(recv, parts, got)
    gw_out, _ = _wgrad(ycat, d_y, "wgrad_out", by_rows=True)
    parts = [("w_up", gw_up[0], far), ("w_out", gw_out[0], near + far)]
    (d_z, vs_mix, dcw, d_wr, d_wi, d_ws, d_bs), got = _mix_bwd(
        d_ycat, z, hl, *mix_params, w_spatial[0], w_sp_t, b_sp_t, g_lru_out, g_gmlp_out, carry=leaving(*parts))
    received(recv, parts, got)
    gw_in, got = _wgrad(h, d_z, "wgrad_in", carry=updating(dict(w_up=gw_up, w_out=gw_out)))
    updates(dict(w_up=gw_up, w_out=gw_out), got)
    chip_sums = None
    if scatter:
        h_own, h_out = _pair_reduce(gw_in[0], gw_in[1], True)
        gw_in = (gw_in[0], h_own)
        chip_sums = _chip_scatter_carry(h_out)
    (grad_x, vs_in), got = _in_bwd(d_z, w_in_f, x2, d_x1, g_mix_pre, sc_m, carry=chip_sums)
    recv["w_in"] = list(got)

    gath = [vs_in, vs_up, vs_ffn, loss_acc]
    red = [cs_ffn, vs_mix, dcw, d_wr, d_wi, d_ws.reshape(N_GROUPS * POS_BLOCK, POS_BLOCK), d_bs]
    return dict(grad_x=grad_x, gath=gath, red=red, recv=recv, updated=updated,
                w_in=gw_in, w_out=gw_out, w_up=gw_up, w_down=gw_down)


def kernel(x, c, w_ada, b_ada, g_mix_pre, g_mix_post, w_in, conv_w, conv_b, w_rgate, b_rgate, w_igate, b_igate, lru_a, v_norm_g, v_norm_b, w_spatial, b_spatial, g_lru_out, g_gmlp_out, w_out, g_ffn_pre, g_ffn_post, w_up, ffn_conv_w, ffn_conv_b, w_down, loss_target, m_w_ada, m_b_ada, m_g_mix_pre, m_g_mix_post, m_w_in, m_conv_w, m_conv_b, m_w_rgate, m_b_rgate, m_w_igate, m_b_igate, m_lru_a, m_v_norm_g, m_v_norm_b, m_w_spatial, m_b_spatial, m_g_lru_out, m_g_gmlp_out, m_w_out, m_g_ffn_pre, m_g_ffn_post, m_w_up, m_ffn_conv_w, m_ffn_conv_b, m_w_down, v_w_ada, v_b_ada, v_g_mix_pre, v_g_mix_post, v_w_in, v_conv_w, v_conv_b, v_w_rgate, v_b_rgate, v_w_igate, v_b_igate, v_lru_a, v_v_norm_g, v_v_norm_b, v_w_spatial, v_b_spatial, v_g_lru_out, v_g_gmlp_out, v_w_out, v_g_ffn_pre, v_g_ffn_post, v_w_up, v_ffn_conv_w, v_ffn_conv_b, v_w_down):
    me = _dev_index(_my_pos())
    ada_cols = w_ada.shape[-1]

    big_w = dict(w_in=(w_in, m_w_in, v_w_in, True), w_out=(w_out, m_w_out, v_w_out, False),
                 w_up=(w_up, m_w_up, v_w_up, True), w_down=(w_down, m_w_down, v_w_down, False))

    def gather(*names):
        return _gather_carry([big_w[n][0][0] for n in names], [STACKED if n == "w_up" else big_w[n][3] for n in names])

    def scatter(parts):
        return _scatter_carry([g for _, g, _ in parts], [big_w[n][0].shape[1:] for n, _, _ in parts],
                              [big_w[n][3] for n, _, _ in parts], [rel for _, _, rel in parts])

    ffn_cw_taps = tuple(a.reshape(FFN_CONV_K, 1, -1) for a in (ffn_conv_w, m_ffn_conv_w, v_ffn_conv_w))
    (c_all, cw_all, fcw_all, mod_all), (w_in_f, w_out_f) = _prologue(
        c, conv_w[0], ffn_cw_taps[0], w_ada[0], b_ada, carry=gather("w_in", "w_out"))
    conv_w_full = _columns_from_devices(cw_all, LRU_CONV_K)
    ffn_cw_full = _columns_from_devices(fcw_all, FFN_CONV_K)
    mod = lax.dynamic_index_in_dim(mod_all.reshape(N_DEV, N_DEV, ada_cols), me, axis=1, keepdims=False)
    mod = mod.reshape(N_MOD, 1, D_MODEL)

    loc = _local_step(x[0], loss_target[0], mod, w_in_f, dict(w_out=w_out_f), conv_w_full, ffn_cw_full,
                      g_mix_pre, g_mix_post, conv_b, w_rgate, b_rgate, w_igate, b_igate, lru_a, v_norm_g, v_norm_b,
                      w_spatial, b_spatial, g_lru_out, g_gmlp_out, g_ffn_pre, g_ffn_post, ffn_conv_b,
                      gather=gather, scatter=scatter,
                      adam={n: big_w[n][:3] for n in ("w_out", "w_up", "w_down")})
    grad_x = loc["grad_x"]

    (gathered, reduced), _ = _reduce_small(loc["gath"], loc["red"])

    results = dict(loc["updated"])
    w_, m_, v_, _ = big_w["w_in"]
    results["w_in"] = _adamw_sum(w_, loc["w_in"][1], loc["recv"]["w_in"], m_, v_, "adamw_w_in")

    params = dict(
        b_ada=(b_ada, m_b_ada, v_b_ada), g_mix_pre=(g_mix_pre, m_g_mix_pre, v_g_mix_pre),
        g_mix_post=(g_mix_post, m_g_mix_post, v_g_mix_post), conv_b=(conv_b, m_conv_b, v_conv_b),
        w_rgate=(w_rgate, m_w_rgate, v_w_rgate), b_rgate=(b_rgate, m_b_rgate, v_b_rgate),
        w_igate=(w_igate, m_w_igate, v_w_igate), b_igate=(b_igate, m_b_igate, v_b_igate),
        lru_a=(lru_a, m_lru_a, v_lru_a), v_norm_g=(v_norm_g, m_v_norm_g, v_v_norm_g),
        v_norm_b=(v_norm_b, m_v_norm_b, v_v_norm_b), w_spatial=(w_spatial, m_w_spatial, v_w_spatial),
        b_spatial=(b_spatial, m_b_spatial, v_b_spatial), g_lru_out=(g_lru_out, m_g_lru_out, v_g_lru_out),
        g_gmlp_out=(g_gmlp_out, m_g_gmlp_out, v_g_gmlp_out), g_ffn_pre=(g_ffn_pre, m_g_ffn_pre, v_g_ffn_pre),
        g_ffn_post=(g_ffn_post, m_g_ffn_post, v_g_ffn_post), ffn_conv_b=(ffn_conv_b, m_ffn_conv_b, v_ffn_conv_b))
    conv_params = dict(conv_w=(conv_w, m_conv_w, v_conv_w), ffn_conv_w=ffn_cw_taps)
    small_results, loss = _adamw_small(gathered, reduced, params, conv_params)
    results.update(small_results)
    results["ffn_conv_w"] = tuple(a.reshape(ffn_conv_w.shape) for a in results["ffn_conv_w"])
    loss = loss.reshape(())

    results["w_ada"] = _adamw_wada(c_all, gathered[0], gathered[1], gathered[2], w_ada, m_w_ada, v_w_ada)

    order = ["w_ada", "b_ada", "g_mix_pre", "g_mix_post", "w_in", "conv_w", "conv_b", "w_rgate", "b_rgate", "w_igate",
             "b_igate", "lru_a", "v_norm_g", "v_norm_b", "w_spatial", "b_spatial", "g_lru_out", "g_gmlp_out", "w_out",
             "g_ffn_pre", "g_ffn_post", "w_up", "ffn_conv_w", "ffn_conv_b", "w_down"]
    outs = [loss, grad_x[None]]
    for kind in range(4):
        outs += [results[n][kind] for n in order]
    return tuple(outs)
```

```python
import functools

import jax
import jax.numpy as jnp
from jax import lax
from jax.experimental import pallas as pl
from jax.experimental.pallas import tpu as pltpu

F32 = jnp.float32
BF16 = jnp.bfloat16

D_MODEL = 1024
LRU_W = 512
GMLP_W = 512
N_HEADS = 8
HEAD_DIM = 64
N_GROUPS = 4
POS_BLOCK = 128
CHUNK = 64
IN_COLS = 2048
D_FF = 3072
N_MOD = 6
N_DEV = 8
EPS = 1e-6
LRU_C = 8.0
LRU_CONV_K = 4
FFN_CONV_K = 3

ADAM_LR = 0.001
ADAM_B1 = 0.9
ADAM_B2 = 0.999
ADAM_EPS = 1e-08
ADAM_WD = 0.01
ADAM_STEP = 10

LANES = 128
SUBLANES = 8
TT_BIG = 512
TT_MIX = 256
FF_CW = 1024
VMEM_LIMIT = 56 * 1024 * 1024

MESH = pl.DeviceIdType.MESH


def _sds(shape, dtype):
    return jax.ShapeDtypeStruct(shape, dtype)


def _cparams(sem=None):
    return pltpu.CompilerParams(dimension_semantics=sem, vmem_limit_bytes=VMEM_LIMIT)


def _whole():
    return pl.BlockSpec(memory_space=pltpu.VMEM)


def _const(shape):
    nd = len(shape)
    return pl.BlockSpec(shape, lambda *_: (0,) * nd)


def _any():
    return pl.BlockSpec(memory_space=pl.ANY)


class _Carry:
    def __init__(self, inputs, in_specs, out_shape, out_specs, scratch, start=None, finish=None, each=None):
        self.inputs, self.in_specs, self.out_shape, self.out_specs = inputs, in_specs, out_shape, out_specs
        self.scratch, self.start, self.finish, self.each = scratch, start, finish, each


def _call(body, name, grid, in_specs, out_specs, out_shape, scratch, args, carry=None, body_starts_carry=False):
    n_in, n_out, n_scr = len(in_specs), len(out_specs), len(scratch)
    c_in = len(carry.in_specs) if carry else 0
    c_out = len(carry.out_specs) if carry else 0

    def full_body(*refs):
        ins = refs[:n_in]
        c_ins = refs[n_in:n_in + c_in]
        outs = refs[n_in + c_in:n_in + c_in + n_out]
        c_outs = refs[n_in + c_in + n_out:n_in + c_in + n_out + c_out]
        scr = refs[n_in + c_in + n_out + c_out:n_in + c_in + n_out + c_out + n_scr]
        c_scr = refs[n_in + c_in + n_out + c_out + n_scr:]
        if carry:
            first = functools.reduce(lambda a, b: a & b, [pl.program_id(d) == 0 for d in range(len(grid))])
            last = functools.reduce(lambda a, b: a & b, [pl.program_id(d) == g - 1 for d, g in enumerate(grid)])

        if carry and carry.start and not body_starts_carry:
            @pl.when(first)
            def _():
                carry.start(c_ins, c_outs, c_scr)

        if body_starts_carry:
            body(*ins, *outs, *scr, start_carry=(lambda: carry.start(c_ins, c_outs, c_scr)) if carry else (lambda: None))
        else:
            body(*ins, *outs, *scr)
        if carry and carry.each:
            carry.each(c_ins, c_outs, c_scr)
        if carry and carry.finish:
            @pl.when(last)
            def _():
                carry.finish(c_ins, c_outs, c_scr)

    res = pl.pallas_call(
        full_body, name=name, grid=grid,
        in_specs=list(in_specs) + (list(carry.in_specs) if carry else []),
        out_specs=list(out_specs) + (list(carry.out_specs) if carry else []),
        out_shape=list(out_shape) + (list(carry.out_shape) if carry else []),
        scratch_shapes=list(scratch) + (list(carry.scratch) if carry else []),
        compiler_params=_cparams(("arbitrary",) * len(grid)),
    )(*args, *(carry.inputs if carry else []))
    return res[:n_out], res[n_out:]


GELU_C0 = 0.7978845608028654
GELU_C1 = GELU_C0 * 0.044715


def _gelu(x):
    t = jnp.tanh(x * (GELU_C0 + GELU_C1 * (x * x)))
    hx = 0.5 * x
    return hx + hx * t


def _gelu_and_grad(x):
    x2 = x * x
    t = jnp.tanh(x * (GELU_C0 + GELU_C1 * x2))
    hx = 0.5 * x
    g = hx + hx * t
    dg = (0.5 + 0.5 * t) + hx * (1.0 - t * t) * (GELU_C0 + 3.0 * GELU_C1 * x2)
    return g, dg


def _sigmoid(x):
    return 1.0 / (1.0 + jnp.exp(-x))


def _softplus(x):
    return jnp.maximum(x, 0.0) + jnp.log1p(jnp.exp(-jnp.abs(x)))


def _neg_expm1(x):
    series = -x * (1.0 + x * (0.5 + x * (1.0 / 6.0 + x * (1.0 / 24.0 + x * (1.0 / 120.0)))))
    return jnp.where(x > -0.1, series, 1.0 - jnp.exp(x))


def _dot(a, b):
    return jnp.dot(a.astype(BF16), b.astype(BF16), preferred_element_type=F32)


def _dot_nt(a, b):
    return lax.dot_general(a.astype(BF16), b.astype(BF16), (((1,), (1,)), ((), ())), preferred_element_type=F32)


def _dot_tn(a, b):
    return lax.dot_general(a.astype(BF16), b.astype(BF16), (((0,), (0,)), ((), ())), preferred_element_type=F32)


def _rows(shape):
    return lax.broadcasted_iota(jnp.int32, shape, 0)


def _shift_down(cur, prev8, s):
    if s == 0:
        return cur
    n = cur.shape[0]
    r = pltpu.roll(cur, s, 0)
    p = pltpu.roll(prev8, s, 0)
    top = jnp.where(_rows(p.shape) < s, p, r[0:SUBLANES])
    if n == SUBLANES:
        return top
    return jnp.concatenate([top, r[SUBLANES:]], axis=0)


def _shift_up(cur, next8, s):
    if s == 0:
        return cur
    n = cur.shape[0]
    r = pltpu.roll(cur, n - s, 0)
    q = pltpu.roll(next8, SUBLANES - s, 0)
    bot = jnp.where(_rows(q.shape) >= SUBLANES - s, q, r[n - SUBLANES:])
    if n == SUBLANES:
        return bot
    return jnp.concatenate([r[:n - SUBLANES], bot], axis=0)


def _scan_fwd(a, b, h_in):
    n = a.shape[0]
    in_group = _rows(a.shape) & (SUBLANES - 1)
    s = 1
    while s < SUBLANES:
        a_s = pltpu.roll(a, s, 0)
        b_s = pltpu.roll(b, s, 0)
        m = in_group >= s
        b = jnp.where(m, a * b_s + b, b)
        a = jnp.where(m, a * a_s, a)
        s *= 2
    out, carry = [], h_in
    for g in range(n // SUBLANES):
        rows = slice(g * SUBLANES, (g + 1) * SUBLANES)
        h_g = a[rows] * carry + b[rows]
        out.append(h_g)
        carry = h_g[SUBLANES - 1:SUBLANES, :]
    return jnp.concatenate(out, axis=0)


def _scan_rev(a, b, l_in):
    n = a.shape[0]
    in_group = _rows(a.shape) & (SUBLANES - 1)
    s = 1
    while s < SUBLANES:
        a_s = pltpu.roll(a, n - s, 0)
        b_s = pltpu.roll(b, n - s, 0)
        m = in_group < SUBLANES - s
        b = jnp.where(m, b + a * b_s, b)
        a = jnp.where(m, a * a_s, a)
        s *= 2
    out, carry = [], l_in
    for g in reversed(range(n // SUBLANES)):
        rows = slice(g * SUBLANES, (g + 1) * SUBLANES)
        l_g = b[rows] + a[rows] * carry
        out.append(l_g)
        carry = l_g[0:1, :]
    return jnp.concatenate(out[::-1], axis=0)


def _rms(x):
    r = lax.rsqrt(jnp.mean(x * x, axis=-1, keepdims=True) + EPS)
    return x * r, r


def _rms_bwd(d_n, n, r):
    return r * (d_n - n * jnp.mean(d_n * n, axis=-1, keepdims=True))


def _colsum(x):
    return jnp.sum(x, axis=0, keepdims=True)


ROW_PIECE = 256


def _row_pieces(tt):
    return [slice(r, r + min(ROW_PIECE, tt)) for r in range(0, tt, min(ROW_PIECE, tt))]


def _lru_gates(xc, wr_ref, wi_ref, br, bi, sp_a):
    r = _sigmoid(_dot(xc, wr_ref[...]) + br)
    i = _sigmoid(_dot(xc, wi_ref[...]) + bi)
    la = -LRU_C * r * sp_a
    a = jnp.exp(la)
    mult = jnp.sqrt(_neg_expm1(2.0 * la))
    return r, i, a, mult


def _lru_conv(lx, prev8, cw_ref, cb):
    xc = cb + cw_ref[LRU_CONV_K - 1:LRU_CONV_K, :] * lx
    taps = []
    for k in range(LRU_CONV_K - 1):
        tap = _shift_down(lx, prev8, LRU_CONV_K - 1 - k)
        taps.append(tap)
        xc = xc + cw_ref[k:k + 1, :] * tap
    return xc, taps


def _ws_mask(transposed=False):
    i = lax.broadcasted_iota(jnp.int32, (POS_BLOCK, POS_BLOCK), 0)
    j = lax.broadcasted_iota(jnp.int32, (POS_BLOCK, POS_BLOCK), 1)
    if transposed:
        i, j = j, i
    return (j // CHUNK) <= (i // CHUNK)


def _gmlp_v(gv, vg, vb):
    av, dav = _gelu_and_grad(gv)
    mu = jnp.mean(av, axis=-1, keepdims=True)
    cen = av - mu
    rs = lax.rsqrt(jnp.mean(cen * cen, axis=-1, keepdims=True) + EPS)
    vhat = cen * rs
    return vhat * vg + vb, vhat, rs, dav


def _mix_fwd(x, sh, sc, g_pre, w_in, conv_w, conv_b, wr_bd, wi_bd, b_r, b_i, lru_a, vn_g, vn_b, w_sp, b_sp_t,
             g_lru, g_gmlp, w_out, g_post, gt_m, g_ffn_pre, sc_f, sh_f, carry=None):
    s_len = x.shape[0]
    tt = min(TT_MIX, s_len)
    nblk = tt // POS_BLOCK

    def body(x_ref, sh_ref, sc_ref, g_ref, w_ref, cw_ref, cb_ref, wr_ref, wi_ref, br_ref, bi_ref, la_ref, vg_ref,
             vb_ref, ws_ref, bst_ref, gl_ref, gg_ref, wo_ref, gp_ref, gtm_ref, g2_ref, scf_ref, shf_ref,
             z_ref, h_ref, y_ref, hl_ref, yo_ref, x1_ref, h2_ref, prev8, hcar):
        i = pl.program_id(0)

        @pl.when(i == 0)
        def _():
            prev8[...] = jnp.zeros_like(prev8)
            hcar[...] = jnp.zeros_like(hcar)

        n_x, _ = _rms(x_ref[...])
        h = (n_x * g_ref[...] * (1.0 + sc_ref[...]) + sh_ref[...]).astype(BF16)
        h_ref[...] = h
        z_ref[...] = jnp.dot(h, w_ref[...], preferred_element_type=F32)

        lx = z_ref[:, 0:LRU_W]
        gate = z_ref[:, LRU_W:2 * LRU_W]
        gu = z_ref[:, 2 * LRU_W:2 * LRU_W + GMLP_W]
        gv = z_ref[:, 2 * LRU_W + GMLP_W:]

        xc, _ = _lru_conv(lx, prev8[...], cw_ref, cb_ref[...])
        prev8[...] = lx[tt - SUBLANES:]
        sp_a = _softplus(-la_ref[...])
        _, ig, a, mult = _lru_gates(xc, wr_ref, wi_ref, br_ref[...], bi_ref[...], sp_a)
        bx = mult * (ig * xc)
        hl = _scan_fwd(a, bx, hcar[0:1, :])
        hcar[...] = jnp.broadcast_to(hl[tt - 1:tt, :], hcar.shape)
        hl_ref[...] = hl
        y_lru = hl * _gelu(gate)
        n_l, _ = _rms(y_lru)
        y_ref[:, 0:LRU_W] = (n_l * gl_ref[...]).astype(BF16)

        u = _gelu(gu)
        v, _, _, _ = _gmlp_v(gv, vg_ref[...], vb_ref[...])
        mask = _ws_mask()
        sp_parts = []
        for nb in range(nblk):
            row = []
            for g in range(N_GROUPS):
                wsm = jnp.where(mask, ws_ref[g], 0.0)
                vblk = v[nb * POS_BLOCK:(nb + 1) * POS_BLOCK, g * LANES:(g + 1) * LANES]
                row.append(_dot(wsm, vblk) + bst_ref[:, g:g + 1])
            sp_parts.append(jnp.concatenate(row, axis=1))
        sp = jnp.concatenate(sp_parts, axis=0) if nblk > 1 else sp_parts[0]
        n_g, _ = _rms(u * sp)
        y_ref[:, LRU_W:] = (n_g * gg_ref[...]).astype(BF16)

        y = jnp.dot(y_ref[...], wo_ref[...], preferred_element_type=F32)
        yo_ref[...] = y
        n_y, _ = _rms(y)
        x1 = x_ref[...] + gtm_ref[...] * (n_y * gp_ref[...])
        x1_ref[...] = x1
        n1, _ = _rms(x1)
        h2_ref[...] = (n1 * g2_ref[...] * (1.0 + scf_ref[...]) + shf_ref[...]).astype(BF16)

    row = lambda c: pl.BlockSpec((tt, c), lambda i: (i, 0))
    v512 = _const((1, LRU_W))
    vec = _const((1, D_MODEL))
    return _call(
        body, "mix_fwd", (s_len // tt,),
        in_specs=[row(D_MODEL), vec, vec, vec, _whole(),
                  _const((LRU_CONV_K, LRU_W)), v512, _whole(), _whole(), v512, v512, v512, v512, v512,
                  _whole(), _whole(), v512, v512, _whole(), vec, vec, vec, vec, vec],
        out_specs=[row(IN_COLS), row(D_MODEL), row(LRU_W + GMLP_W), row(LRU_W), row(D_MODEL), row(D_MODEL),
                   row(D_MODEL)],
        out_shape=[_sds((s_len, IN_COLS), F32), _sds((s_len, D_MODEL), BF16),
                   _sds((s_len, LRU_W + GMLP_W), BF16), _sds((s_len, LRU_W), F32),
                   _sds((s_len, D_MODEL), F32), _sds((s_len, D_MODEL), F32), _sds((s_len, D_MODEL), BF16)],
        scratch=[pltpu.VMEM((SUBLANES, LRU_W), F32), pltpu.VMEM((SUBLANES, LRU_W), F32)],
        args=(x, sh, sc, g_pre, w_in, conv_w, conv_b, wr_bd, wi_bd, b_r, b_i, lru_a, vn_g, vn_b, w_sp, b_sp_t,
              g_lru, g_gmlp, w_out, g_post, gt_m, g_ffn_pre, sc_f, sh_f), carry=carry)


FF_CHUNKS = N_DEV // 2
FF_CHUNK_W = D_FF // FF_CHUNKS


def _ffn_fwd(h2, w_up3, ffn_cw, ffn_cb, carry=None):
    s_len = h2.shape[0]
    tt = min(TT_BIG, s_len)
    nc, cw = FF_CHUNKS, FF_CHUNK_W

    def body(h2_ref, wu_ref, cwg_ref, cwv_ref, cbg_ref, cbv_ref, up_ref, upc_ref, act_ref, prev):
        i = pl.program_id(0)
        c = pl.program_id(1)

        @pl.when(i == 0)
        def _():
            prev[c] = jnp.zeros((2, SUBLANES, cw), F32)

        h2 = h2_ref[...]
        ug_pre = jnp.dot(h2, wu_ref[c], preferred_element_type=F32)
        uv_pre = jnp.dot(h2, wu_ref[nc + c], preferred_element_type=F32)
        up_ref[0] = ug_pre.astype(BF16)
        up_ref[1] = uv_pre.astype(BF16)
        ug, _ = _ffn_conv(ug_pre, prev[c, 0], cwg_ref, cbg_ref[...])
        uv, _ = _ffn_conv(uv_pre, prev[c, 1], cwv_ref, cbv_ref[...])
        prev[c, 0] = ug_pre[tt - SUBLANES:, :]
        prev[c, 1] = uv_pre[tt - SUBLANES:, :]
        upc_ref[0] = ug
        upc_ref[1] = uv
        act_ref[...] = (_gelu(ug) * uv).astype(BF16)

    chunk2 = pl.BlockSpec((2, tt, cw), lambda i, c: (0, i, c))
    ffn_cb2 = ffn_cb.reshape(1, 2 * D_FF)
    return _call(
        body, "ffn_fwd", (s_len // tt, nc),
        in_specs=[pl.BlockSpec((tt, D_MODEL), lambda i, c: (i, 0)), _whole(),
                  pl.BlockSpec((FFN_CONV_K, cw), lambda i, c: (0, c)),
                  pl.BlockSpec((FFN_CONV_K, cw), lambda i, c: (0, c + nc)),
                  pl.BlockSpec((1, cw), lambda i, c: (0, c)),
                  pl.BlockSpec((1, cw), lambda i, c: (0, c + nc))],
        out_specs=[chunk2, chunk2, pl.BlockSpec((tt, cw), lambda i, c: (i, c))],
        out_shape=[_sds((2, s_len, D_FF), BF16), _sds((2, s_len, D_FF), F32), _sds((s_len, D_FF), BF16)],
        scratch=[pltpu.VMEM((nc, 2, SUBLANES, cw), F32)],
        args=(h2, w_up3, ffn_cw, ffn_cw, ffn_cb2, ffn_cb2), carry=carry)


def _ffn_tail(act, w_down, x1, gt_f, g_post, target):
    s_len = x1.shape[0]
    tt = min(TT_BIG, s_len)

    def body(act_ref, wd_ref, x1_ref, gtf_ref, gp_ref, tg_ref, dy2_ref, dout_ref, loss_ref, vs_ref):
        @pl.when(pl.program_id(0) == 0)
        def _():
            loss_ref[...] = jnp.zeros_like(loss_ref)
            vs_ref[...] = jnp.zeros_like(vs_ref)

        for rows in _row_pieces(tt):
            n2, r2 = _rms(jnp.dot(act_ref[rows, :], wd_ref[...], preferred_element_type=F32))
            out = x1_ref[rows, :] + gtf_ref[...] * (n2 * gp_ref[...])
            err = out - tg_ref[rows, :]
            do = err * (1.0 / D_MODEL)
            dout_ref[rows, :] = do
            loss_ref[...] += jnp.broadcast_to(0.5 * jnp.sum(err * err, keepdims=True) * (1.0 / D_MODEL),
                                              loss_ref.shape)
            vs_ref[0:1, :] += _colsum(do * n2 * gp_ref[...])
            vs_ref[1:2, :] += _colsum(do * gtf_ref[...] * n2)
            dy2_ref[rows, :] = _rms_bwd(do * gtf_ref[...] * gp_ref[...], n2, r2).astype(BF16)

    row = lambda c: pl.BlockSpec((tt, c), lambda i: (i, 0))
    vec = _const((1, D_MODEL))
    outs, _ = _call(
        body, "ffn_tail", (s_len // tt,),
        in_specs=[row(D_FF), _whole(), row(D_MODEL), vec, vec, row(D_MODEL)],
        out_specs=[row(D_MODEL), row(D_MODEL), _const((SUBLANES, LANES)), _const((SUBLANES, D_MODEL))],
        out_shape=[_sds((s_len, D_MODEL), BF16), _sds((s_len, D_MODEL), F32), _sds((SUBLANES, LANES), F32),
                   _sds((SUBLANES, D_MODEL), F32)],
        scratch=[], args=(act, w_down, x1, gt_f, g_post, target))
    return outs


def _ffn_conv(up_pre, prev8, cw_ref, cb):
    up = cb + cw_ref[FFN_CONV_K - 1:FFN_CONV_K, :] * up_pre
    taps = []
    for k in range(FFN_CONV_K - 1):
        tap = _shift_down(up_pre, prev8, FFN_CONV_K - 1 - k)
        taps.append(tap)
        up = up + cw_ref[k:k + 1, :] * tap
    return up, taps


def _ffn_bwd(d_y2, up_pre, up, ffn_cw, w_down, carry=None):
    s_len = d_y2.shape[0]
    tt = min(TT_BIG, s_len)
    nt = s_len // tt
    cw = FF_CW
    nc = D_FF // cw

    def body(dy2_ref, up_ref, upc_ref, cwg_ref, cwv_ref, wd_ref, dup_ref, cs_ref, nxt, cs_acc):
        i = pl.program_id(0)
        c = pl.program_id(1)

        @pl.when(i == 0)
        def _():
            nxt[c] = jnp.zeros((2, SUBLANES, cw), F32)
            cs_acc[c] = jnp.zeros((2, SUBLANES, cw), F32)

        pw = 2 * LANES
        for piece in range(cw // pw):
            cols = slice(piece * pw, (piece + 1) * pw)
            d_act = _dot_nt(dy2_ref[...], wd_ref[pl.ds(pl.multiple_of(c * cw + piece * pw, pw), pw), :])
            uv = upc_ref[1, :, cols]
            gl, dgl = _gelu_and_grad(upc_ref[0, :, cols])
            d_ug = d_act * uv * dgl
            d_uv = d_act * gl
            for half, (d_u, cw_ref) in enumerate(((d_ug, cwg_ref), (d_uv, cwv_ref))):
                nx = nxt[c, half, :, cols]
                x_in = up_ref[half, :, cols].astype(F32)
                d_pre = cw_ref[FFN_CONV_K - 1:FFN_CONV_K, cols] * d_u
                sums = [None] * (FFN_CONV_K + 1)
                sums[FFN_CONV_K - 1] = _colsum(d_u * x_in)
                for k in range(FFN_CONV_K - 1):
                    ahead = _shift_up(d_u, nx, FFN_CONV_K - 1 - k)
                    d_pre = d_pre + cw_ref[k:k + 1, cols] * ahead
                    sums[k] = _colsum(ahead * x_in)
                sums[FFN_CONV_K] = _colsum(d_u)
                pad = jnp.zeros((SUBLANES - FFN_CONV_K - 1, pw), F32)
                cs_acc[c, half, :, cols] += jnp.concatenate(sums + [pad], axis=0)
                nxt[c, half, :, cols] = d_u[0:SUBLANES]
                dup_ref[half, :, cols] = d_pre.astype(BF16)

        for cc in range(nc):
            @pl.when((i == nt - 1) & (c == cc))
            def _():
                cs_ref[:, cc * cw:(cc + 1) * cw] = cs_acc[cc, 0]
                cs_ref[:, D_FF + cc * cw:D_FF + (cc + 1) * cw] = cs_acc[cc, 1]

    row = pl.BlockSpec((tt, D_MODEL), lambda i, c: (nt - 1 - i, 0))
    blk = pl.BlockSpec((2, tt, cw), lambda i, c: (0, nt - 1 - i, c))
    return _call(
        body, "ffn_bwd", (nt, nc),
        in_specs=[row, blk, blk,
                  pl.BlockSpec((FFN_CONV_K, cw), lambda i, c: (0, c)),
                  pl.BlockSpec((FFN_CONV_K, cw), lambda i, c: (0, c + nc)),
                  _whole()],
        out_specs=[blk, _const((SUBLANES, 2 * D_FF))],
        out_shape=[_sds((2, s_len, D_FF), BF16), _sds((SUBLANES, 2 * D_FF), F32)],
        scratch=[pltpu.VMEM((nc, 2, SUBLANES, cw), F32), pltpu.VMEM((nc, 2, SUBLANES, cw), F32)],
        args=(d_y2, up_pre, up, ffn_cw, ffn_cw, w_down), carry=carry)


def _up_bwd(d_up, w_up3, x1, dout, y, w_out, g_pre, sc_f, g_post, gt_m, carry=None):
    s_len = x1.shape[0]
    tt = min(TT_BIG, s_len)

    def body(du_ref, wu_ref, x1_ref, do_ref, y_ref, wo_ref, g2_ref, sc_ref, gp_ref, gt_ref,
             dx1_ref, dy_ref, dyc_ref, vs_ref):
        @pl.when(pl.program_id(0) == 0)
        def _():
            vs_ref[...] = jnp.zeros_like(vs_ref)

        for rows in _row_pieces(tt):
            d_h2 = jnp.zeros((rows.stop - rows.start, D_MODEL), F32)
            for half in range(2):
                for ch in range(FF_CHUNKS):
                    d_h2 = d_h2 + _dot_nt(du_ref[half, rows, ch * FF_CHUNK_W:(ch + 1) * FF_CHUNK_W],
                                          wu_ref[half * FF_CHUNKS + ch])
            n1, r1 = _rms(x1_ref[rows, :])
            ng = n1 * g2_ref[...]
            vs_ref[0:1, :] += _colsum(d_h2)
            vs_ref[1:2, :] += _colsum(d_h2 * ng)
            d_ng = d_h2 * (1.0 + sc_ref[...])
            vs_ref[2:3, :] += _colsum(d_ng * n1)
            d_x1 = do_ref[rows, :] + _rms_bwd(d_ng * g2_ref[...], n1, r1)
            dx1_ref[rows, :] = d_x1
            n_y, r_y = _rms(y_ref[rows, :])
            vs_ref[3:4, :] += _colsum(d_x1 * n_y * gp_ref[...])
            d_on = d_x1 * gt_ref[...]
            vs_ref[4:5, :] += _colsum(d_on * n_y)
            d_y = _rms_bwd(d_on * gp_ref[...], n_y, r_y).astype(BF16)
            dy_ref[rows, :] = d_y
            dyc_ref[rows, :] = _dot_nt(d_y, wo_ref[...])

    row = lambda c: pl.BlockSpec((tt, c), lambda i: (i, 0))
    vec = _const((1, D_MODEL))
    return _call(
        body, "up_bwd", (s_len // tt,),
        in_specs=[pl.BlockSpec((2, tt, D_FF), lambda i: (0, i, 0)), _whole(), row(D_MODEL), row(D_MODEL), row(D_MODEL),
                  _whole(), vec, vec, vec, vec],
        out_specs=[row(D_MODEL), row(D_MODEL), row(LRU_W + GMLP_W), _const((SUBLANES, D_MODEL))],
        out_shape=[_sds((s_len, D_MODEL), F32), _sds((s_len, D_MODEL), BF16), _sds((s_len, LRU_W + GMLP_W), F32),
                   _sds((SUBLANES, D_MODEL), F32)],
        scratch=[], args=(d_up, w_up3, x1, dout, y, w_out, g_pre, sc_f, g_post, gt_m), carry=carry)


def _head_pair_block(hd):
    return (slice((hd // 2) * HEAD_DIM, (hd // 2 + 1) * HEAD_DIM), slice((hd % 2) * HEAD_DIM, (hd % 2 + 1) * HEAD_DIM))


def _mix_bwd(d_ycat, z, hl, conv_w, conv_b, wr_bd, wi_bd, b_r, b_i, lru_a, vn_g, vn_b, w_sp, w_sp_t, b_sp_t,
             g_lru, g_gmlp, carry=None):
    s_len = z.shape[0]
    tt = min(TT_MIX, s_len)
    nt = s_len // tt
    nblk = tt // POS_BLOCK
    hb = tt // SUBLANES

    def body(dyc_ref, z_ref, zh_ref, hl_ref, hh_ref, cw_ref, cb_ref, wr_ref, wi_ref, br_ref, bi_ref, la_ref,
             vg_ref, vb_ref, ws_ref, wst_ref, bst_ref, gl_ref, gg_ref,
             dz_ref, vs_ref, dcw_ref, dwrb_ref, dwib_ref, dws_ref, dbs_ref, nxt_dxc, nxt_a, nxt_lam, dwr_ref, dwi_ref):
        i = pl.program_id(0)
        first_tile = i == nt - 1

        @pl.when(i == 0)
        def _():
            for ref in (vs_ref, dcw_ref, dwr_ref, dwi_ref, dws_ref, dbs_ref, nxt_dxc, nxt_a, nxt_lam):
                ref[...] = jnp.zeros_like(ref)

        lx = z_ref[:, 0:LRU_W]
        gate = z_ref[:, LRU_W:2 * LRU_W]
        gu = z_ref[:, 2 * LRU_W:2 * LRU_W + GMLP_W]
        gv = z_ref[:, 2 * LRU_W + GMLP_W:]
        prev8 = jnp.where(first_tile, 0.0, zh_ref[...])
        hprev8 = jnp.where(first_tile, 0.0, hh_ref[...])

        xc, taps = _lru_conv(lx, prev8, cw_ref, cb_ref[...])
        a_par = la_ref[...]
        sp_a = _softplus(-a_par)
        r, ig, a, mult = _lru_gates(xc, wr_ref, wi_ref, br_ref[...], bi_ref[...], sp_a)
        hl = hl_ref[...]
        h_prev = _shift_down(hl, hprev8, 1)
        ggate, dggate = _gelu_and_grad(gate)
        y_lru = hl * ggate
        n_l, r_l = _rms(y_lru)
        d_nl = dyc_ref[:, 0:LRU_W]
        vs_ref[6:7, :] += _colsum(d_nl * n_l)
        d_yl = _rms_bwd(d_nl * gl_ref[...], n_l, r_l)
        d_hl = d_yl * ggate
        d_gate = d_yl * hl * dggate
        a_up = _shift_up(a, nxt_a[...], 1)
        lam = _scan_rev(a_up, d_hl, nxt_lam[0:1, :])
        nxt_a[...] = jnp.broadcast_to(a[0:1, :], nxt_a.shape)
        nxt_lam[...] = jnp.broadcast_to(lam[0:1, :], nxt_lam.shape)
        ixc = ig * xc
        d_la = lam * h_prev * a - lam * ixc * (a * a) / mult
        d_i = lam * mult * xc
        d_xc = lam * mult * ig
        vs_ref[3:4, :] += _colsum(d_la * r) * (LRU_C * _sigmoid(-a_par))
        d_pr = d_la * (-LRU_C * sp_a) * r * (1.0 - r)
        d_pi = d_i * ig * (1.0 - ig)
        vs_ref[1:2, :] += _colsum(d_pr)
        vs_ref[2:3, :] += _colsum(d_pi)
        dwr_ref[...] += _dot_tn(xc, d_pr)
        dwi_ref[...] += _dot_tn(xc, d_pi)
        d_xc = d_xc + _dot_nt(d_pr, wr_ref[...]) + _dot_nt(d_pi, wi_ref[...])
        vs_ref[0:1, :] += _colsum(d_xc)
        nx = nxt_dxc[...]
        d_lx = cw_ref[LRU_CONV_K - 1:LRU_CONV_K, :] * d_xc
        dcw_ref[LRU_CONV_K - 1:LRU_CONV_K, :] += _colsum(d_xc * lx)
        for k in range(LRU_CONV_K - 1):
            d_lx = d_lx + cw_ref[k:k + 1, :] * _shift_up(d_xc, nx, LRU_CONV_K - 1 - k)
            dcw_ref[k:k + 1, :] += _colsum(d_xc * taps[k])
        nxt_dxc[...] = d_xc[0:SUBLANES]
        dz_ref[:, 0:LRU_W] = d_lx.astype(BF16)
        dz_ref[:, LRU_W:2 * LRU_W] = d_gate.astype(BF16)

        u, du = _gelu_and_grad(gu)
        v, vhat, rs, dav = _gmlp_v(gv, vg_ref[...], vb_ref[...])
        mask = _ws_mask()
        sp_parts = []
        for nb in range(nblk):
            rowp = []
            for g in range(N_GROUPS):
                wsm = jnp.where(mask, ws_ref[g], 0.0)
                vblk = v[nb * POS_BLOCK:(nb + 1) * POS_BLOCK, g * LANES:(g + 1) * LANES]
                rowp.append(_dot(wsm, vblk) + bst_ref[:, g:g + 1])
            sp_parts.append(jnp.concatenate(rowp, axis=1))
        sp = jnp.concatenate(sp_parts, axis=0) if nblk > 1 else sp_parts[0]
        y_g = u * sp
        n_g, r_g = _rms(y_g)
        d_ng = dyc_ref[:, LRU_W:]
        vs_ref[7:8, :] += _colsum(d_ng * n_g)
        d_yg = _rms_bwd(d_ng * gg_ref[...], n_g, r_g)
        d_gu = d_yg * sp * du
        d_sp = d_yg * u
        mask_t = _ws_mask(transposed=True)
        ones8 = jnp.ones((SUBLANES, LANES), F32)
        dv_parts = []
        for nb in range(nblk):
            rowp = []
            for g in range(N_GROUPS):
                rs_, cs_ = slice(nb * POS_BLOCK, (nb + 1) * POS_BLOCK), slice(g * LANES, (g + 1) * LANES)
                dsp_blk = d_sp[rs_, cs_]
                dbs_ref[g:g + 1, :] += lax.dot_general(
                    ones8, dsp_blk, (((1,), (1,)), ((), ())), preferred_element_type=F32,
                    precision=lax.Precision.HIGHEST)[0:1, :]
                dws_ref[g] += _dot_nt(dsp_blk, v[rs_, cs_])
                wsm_t = jnp.where(mask_t, wst_ref[g], 0.0)
                rowp.append(_dot(wsm_t, dsp_blk))
            dv_parts.append(jnp.concatenate(rowp, axis=1))
        d_v = jnp.concatenate(dv_parts, axis=0) if nblk > 1 else dv_parts[0]
        vs_ref[4:5, :] += _colsum(d_v * vhat)
        vs_ref[5:6, :] += _colsum(d_v)
        d_vh = d_v * vg_ref[...]
        d_av = rs * (d_vh - jnp.mean(d_vh, axis=-1, keepdims=True)
                     - vhat * jnp.mean(d_vh * vhat, axis=-1, keepdims=True))
        dz_ref[:, 2 * LRU_W:2 * LRU_W + GMLP_W] = d_gu.astype(BF16)
        dz_ref[:, 2 * LRU_W + GMLP_W:] = (d_av * dav).astype(BF16)

        @pl.when(i == nt - 1)
        def _():
            for hd in range(N_HEADS):
                blk = slice(hd * HEAD_DIM, (hd + 1) * HEAD_DIM)
                dwrb_ref[_head_pair_block(hd)] = dwr_ref[blk, blk]
                dwib_ref[_head_pair_block(hd)] = dwi_ref[blk, blk]
            for g in range(N_GROUPS):
                dws_ref[g] = jnp.where(mask, dws_ref[g], 0.0)

    rev = lambda c: pl.BlockSpec((tt, c), lambda i: (nt - 1 - i, 0))
    halo = pl.BlockSpec((SUBLANES, LRU_W), lambda i: (jnp.maximum((nt - 1 - i) * hb - 1, 0), 0))
    v512 = _const((1, LRU_W))
    return _call(
        body, "mix_bwd", (nt,),
        in_specs=[rev(LRU_W + GMLP_W), rev(IN_COLS), halo, rev(LRU_W), halo,
                  _const((LRU_CONV_K, LRU_W)), v512, _whole(), _whole(), v512, v512, v512, v512, v512,
                  _whole(), _whole(), _whole(), v512, v512],
        out_specs=[rev(IN_COLS), _const((SUBLANES, LRU_W)), _const((SUBLANES, LRU_W)),
                   _const((LRU_W // 2, 2 * HEAD_DIM)), _const((LRU_W // 2, 2 * HEAD_DIM)),
                   _const((N_GROUPS, POS_BLOCK, POS_BLOCK)), _const((SUBLANES, POS_BLOCK))],
        out_shape=[_sds((s_len, IN_COLS), BF16), _sds((SUBLANES, LRU_W), F32), _sds((SUBLANES, LRU_W), F32),
                   _sds((LRU_W // 2, 2 * HEAD_DIM), F32), _sds((LRU_W // 2, 2 * HEAD_DIM), F32),
                   _sds((N_GROUPS, POS_BLOCK, POS_BLOCK), F32), _sds((SUBLANES, POS_BLOCK), F32)],
        scratch=[pltpu.VMEM((SUBLANES, LRU_W), F32), pltpu.VMEM((SUBLANES, LRU_W), F32),
                 pltpu.VMEM((SUBLANES, LRU_W), F32), pltpu.VMEM((LRU_W, LRU_W), F32), pltpu.VMEM((LRU_W, LRU_W), F32)],
        args=(d_ycat, z, z, hl, hl, conv_w, conv_b, wr_bd, wi_bd, b_r, b_i, lru_a, vn_g, vn_b, w_sp, w_sp_t, b_sp_t,
              g_lru, g_gmlp), carry=carry)


def _in_bwd(d_z, w_in, x, d_x1, g, sc, carry=None):
    s_len = x.shape[0]
    tt = min(TT_BIG, s_len)

    def body(dz_ref, w_ref, x_ref, dx1_ref, g_ref, sc_ref, gx_ref, vs_ref):
        @pl.when(pl.program_id(0) == 0)
        def _():
            vs_ref[...] = jnp.zeros_like(vs_ref)

        for rows in _row_pieces(tt):
            d_h = _dot_nt(dz_ref[rows, :], w_ref[...])
            n, r = _rms(x_ref[rows, :])
            vs_ref[0:1, :] += _colsum(d_h)
            vs_ref[1:2, :] += _colsum(d_h * n * g_ref[...])
            d_ng = d_h * (1.0 + sc_ref[...])
            vs_ref[2:3, :] += _colsum(d_ng * n)
            gx_ref[rows, :] = dx1_ref[rows, :] + _rms_bwd(d_ng * g_ref[...], n, r)

    row = lambda c: pl.BlockSpec((tt, c), lambda i: (i, 0))
    vec = _const((1, D_MODEL))
    return _call(
        body, "in_bwd", (s_len // tt,),
        in_specs=[row(IN_COLS), _whole(), row(D_MODEL), row(D_MODEL), vec, vec],
        out_specs=[row(D_MODEL), _const((SUBLANES, D_MODEL))],
        out_shape=[_sds((s_len, D_MODEL), F32), _sds((SUBLANES, D_MODEL), F32)],
        scratch=[], args=(d_z, w_in, x, d_x1, g, sc), carry=carry)


def _wgrad(a, b, name, by_rows=False, carry=None):
    s_len, k_dim = a.shape
    halves = b.ndim == 3
    n_dim = b.shape[-1] * (2 if halves else 1)

    def body(a_ref, b_ref, ob_ref, own_ref):
        out = _dot_tn(a_ref[...], b_ref[0] if halves else b_ref[...])
        ob_ref[...] = out.astype(BF16)

        @pl.when(pl.program_id(0) == _dev_index(_my_pos()))
        def _():
            own_ref[...] = out

    if by_rows:
        tile = k_dim // N_DEV
        a_spec = pl.BlockSpec((s_len, tile), lambda j: (0, j))
        b_spec = pl.BlockSpec((s_len, n_dim), lambda j: (0, 0))
        o_spec = pl.BlockSpec((tile, n_dim), lambda j: (j, 0))
        own_shape = (tile, n_dim)
    else:
        tile = n_dim // N_DEV
        a_spec = pl.BlockSpec((s_len, k_dim), lambda j: (0, 0))
        if halves:
            per_half = N_DEV // 2
            b_spec = pl.BlockSpec((1, s_len, tile), lambda j: (j // per_half, 0, j % per_half))
        else:
            b_spec = pl.BlockSpec((s_len, tile), lambda j: (0, j))
        o_spec = pl.BlockSpec((k_dim, tile), lambda j: (0, j))
        own_shape = (k_dim, tile)
    return _call(
        body, name, (N_DEV,), in_specs=[a_spec, b_spec], out_specs=[o_spec, _const(own_shape)],
        out_shape=[_sds((k_dim, n_dim), BF16), _sds(own_shape, F32)],
        scratch=[], args=(a, b), carry=carry)


def _adam_math(w, g, m, v):
    m = ADAM_B1 * m + (1.0 - ADAM_B1) * g
    v = ADAM_B2 * v + (1.0 - ADAM_B2) * (g * g)
    m_hat = m / (1.0 - ADAM_B1 ** ADAM_STEP)
    v_hat = v / (1.0 - ADAM_B2 ** ADAM_STEP)
    delta = -ADAM_LR * (m_hat / (jnp.sqrt(v_hat) + ADAM_EPS) + ADAM_WD * w)
    return delta, m, v


def _row_tile(rows, cols, n_f32_arrays):
    budget = VMEM_LIMIT // 2
    tr = rows
    while tr % 2 == 0 and tr // 2 >= SUBLANES and (tr // 2) % SUBLANES == 0 and tr * cols * 4 * n_f32_arrays * 2 > budget:
        tr //= 2
    return tr


def _adamw_sum_block(w_ref, g_ref, r_refs, m_ref, v_ref, go_ref, d_ref, mo_ref, vo_ref):
    g = g_ref[...]
    for r_ref in r_refs:
        for k in range(r_ref.shape[0]):
            g = g + r_ref[k].astype(F32)
    go_ref[0] = g
    d_ref[0], mo_ref[0], vo_ref[0] = _adam_math(w_ref[0], g, m_ref[0], v_ref[0])


def _adamw_rider(parts, steps):
    inputs, in_specs, out_shape, out_specs, n_recvs = [], [], [], [], []
    for w, g_own, recv, m, v in parts:
        _, rows, cols = w.shape
        tr = rows // steps
        blk = pl.BlockSpec((1, tr, cols), lambda i: (0, i, 0))
        inputs += [w, g_own, *recv, m, v]
        in_specs += ([blk, pl.BlockSpec((tr, cols), lambda i: (i, 0))]
                     + [pl.BlockSpec((r.shape[0], tr, cols), lambda i: (0, i, 0)) for r in recv] + [blk, blk])
        out_shape += [_sds((1, rows, cols), F32)] * 4
        out_specs += [blk] * 4
        n_recvs.append(len(recv))

    def each(ins, outs, scr):
        for n_recv in n_recvs:
            _adamw_sum_block(ins[0], ins[1], ins[2:2 + n_recv], ins[2 + n_recv], ins[3 + n_recv], *outs[:4])
            ins, outs = ins[4 + n_recv:], outs[4:]

    return _Carry(inputs=inputs, in_specs=in_specs, out_shape=out_shape, out_specs=out_specs, scratch=[], each=each)


def _adamw_sum(w, g_own, recv, m, v, name):
    _, rows, cols = w.shape
    n_recv = len(recv)
    tr = _row_tile(rows, cols, 10)
    nb = rows // tr

    def body(w_ref, g_ref, *rest):
        _adamw_sum_block(w_ref, g_ref, rest[:n_recv], *rest[n_recv:])

    blk = pl.BlockSpec((1, tr, cols), lambda i: (0, i, 0))
    return pl.pallas_call(
        body, name=name, grid=(nb,),
        in_specs=[blk, pl.BlockSpec((tr, cols), lambda i: (i, 0))]
        + [pl.BlockSpec((r.shape[0], tr, cols), lambda i: (0, i, 0)) for r in recv] + [blk, blk],
        out_specs=[blk] * 4, out_shape=[_sds((1, rows, cols), F32)] * 4,
        compiler_params=_cparams(("arbitrary",)),
    )(w, g_own, *recv, m, v)


def _row_of_each(ref, row):
    cols = ref.shape[1]
    rows = _rows((N_DEV, cols))
    out = jnp.zeros((N_DEV, cols), F32)
    for d in range(N_DEV):
        picked = ref[d * SUBLANES + row:d * SUBLANES + row + 1, :]
        out = jnp.where(rows == d, jnp.broadcast_to(picked, (N_DEV, cols)), out)
    return out


def _my_columns(full, width, me):
    out = jnp.zeros(full.shape[:-1] + (width,), F32)
    for d in range(N_DEV):
        out = out + jnp.where(me == d, full[:, d * width:(d + 1) * width], 0.0)
    return out


def _adamw_wada(c_all, vs_in_all, vs_up_all, vs_ffn_all, w, m, v):
    _, rows, cols = w.shape

    def body(c_ref, vi_ref, vu_ref, vf_ref, w_ref, m_ref, v_ref, go_ref, d_ref, mo_ref, vo_ref):
        me = _dev_index(_my_pos())
        cv = _row_of_each(c_ref, 0)
        ca = cv * _sigmoid(cv)
        dmod = jnp.concatenate([_row_of_each(vi_ref, 0), _row_of_each(vi_ref, 1), _row_of_each(vu_ref, 3),
                                _row_of_each(vu_ref, 0), _row_of_each(vu_ref, 1), _row_of_each(vf_ref, 0)], axis=1)
        dm = _my_columns(dmod, cols, me)
        g = lax.dot_general(ca, dm, (((0,), (0,)), ((), ())), preferred_element_type=F32,
                            precision=lax.Precision.HIGHEST)
        go_ref[0] = g
        d_ref[0], mo_ref[0], vo_ref[0] = _adam_math(w_ref[0], g, m_ref[0], v_ref[0])

    return pl.pallas_call(
        body, name="adamw_w_ada", out_shape=[_sds((1, rows, cols), F32)] * 4,
        in_specs=[_whole()] * 7, out_specs=[_whole()] * 4,
        compiler_params=_cparams(),
    )(c_all, vs_in_all, vs_up_all, vs_ffn_all, w, m, v)


def _adamw_small(gathered, reduced, params, conv_params):
    names = list(params) + list(conv_params)
    allp = {**params, **conv_params}
    n_g = len(gathered) + len(reduced)

    def body(*refs):
        g_refs = refs[:n_g]
        p_refs = refs[n_g:n_g + 3 * len(names)]
        o_refs = refs[n_g + 3 * len(names):]
        me = _dev_index(_my_pos())

        def total(ref):
            s = ref[0:SUBLANES, :]
            for d in range(1, N_DEV):
                s = s + ref[d * SUBLANES:(d + 1) * SUBLANES, :]
            return s

        vs_in, vs_up, vs_ffn, loss = [total(r) for r in g_refs[:4]]
        cs, vs_mix, dcw, dwr, dwi, dws, dbs = [r[...] for r in g_refs[4:]]
        o_refs[-1][...] = loss[0:1, 0:1]
        mine = lambda full, width: _my_columns(full, width, me)

        all_ = (slice(None), slice(None))
        heads = lambda row: [((0, slice(h, h + 1), slice(None)), row[:, h * HEAD_DIM:(h + 1) * HEAD_DIM])
                             for h in range(N_HEADS)]
        blocks = lambda pairs: [((0, h), pairs[_head_pair_block(h)]) for h in range(N_HEADS)]
        pieces = {
            "b_ada": [((slice(None), slice(k * D_MODEL, (k + 1) * D_MODEL)), row) for k, row in enumerate(
                (vs_in[0:1], vs_in[1:2], vs_up[3:4], vs_up[0:1], vs_up[1:2], vs_ffn[0:1]))],
            "g_mix_pre": [(all_, vs_in[2:3])], "g_mix_post": [(all_, vs_up[4:5])],
            "g_ffn_pre": [(all_, vs_up[2:3])], "g_ffn_post": [(all_, vs_ffn[1:2])],
            "conv_b": [(all_, vs_mix[0:1])], "b_rgate": heads(vs_mix[1:2]), "b_igate": heads(vs_mix[2:3]),
            "lru_a": [(all_, vs_mix[3:4])], "v_norm_g": [(all_, vs_mix[4:5])], "v_norm_b": [(all_, vs_mix[5:6])],
            "g_lru_out": [(all_, vs_mix[6:7])], "g_gmlp_out": [(all_, vs_mix[7:8])],
            "w_rgate": blocks(dwr), "w_igate": blocks(dwi),
            "w_spatial": [((0, g), dws[g * POS_BLOCK:(g + 1) * POS_BLOCK, :]) for g in range(N_GROUPS)],
            "b_spatial": [((0,), dbs[0:N_GROUPS])],
            "ffn_conv_b": [(all_, cs[FFN_CONV_K:FFN_CONV_K + 1])],
            "conv_w": [((0,), mine(dcw[0:LRU_CONV_K], LRU_W // N_DEV))],
        }
        ffn_cw_rows = mine(cs[0:FFN_CONV_K], 2 * D_FF // N_DEV)
        pieces["ffn_conv_w"] = [((k,), ffn_cw_rows[k:k + 1]) for k in range(FFN_CONV_K)]
        for n_i, name in enumerate(names):
            w_ref, m_ref, v_ref = p_refs[3 * n_i:3 * n_i + 3]
            go_ref, d_ref, mo_ref, vo_ref = o_refs[4 * n_i:4 * n_i + 4]
            for idx, g in pieces[name]:
                go_ref[idx] = g
                d_ref[idx], mo_ref[idx], vo_ref[idx] = _adam_math(w_ref[idx], g, m_ref[idx], v_ref[idx])

    flat_params = [a for n in names for a in allp[n]]
    out_shape = [_sds(allp[n][0].shape, F32) for n in names for _ in range(4)] + [_sds((1, 1), F32)]
    outs = pl.pallas_call(
        body, name="adamw_small", out_shape=out_shape,
        in_specs=[_whole()] * (n_g + len(flat_params)), out_specs=[_whole()] * len(out_shape),
        compiler_params=_cparams(),
    )(*gathered, *reduced, *flat_params)
    return {n: outs[4 * i:4 * i + 4] for i, n in enumerate(names)}, outs[-1]


def _my_pos():
    return lax.axis_index("x"), lax.axis_index("y"), lax.axis_index("c")


def _flip(pos, k):
    x, y, c = pos
    return (1 - x if k & 4 else x, 1 - y if k & 2 else y, 1 - c if k & 1 else c)


def _dev_index(pos):
    x, y, c = pos
    return 4 * x + 2 * y + c


def _all_gather_small(ins, outs, send_sems, recv_sems, meanwhile=None):
    n = len(ins)
    me = _my_pos()

    def slot(a, pos):
        rows = ins[a].shape[0]
        return outs[a].at[pl.ds(pl.multiple_of(_dev_index(pos) * rows, SUBLANES), rows), :]

    def copy(a, k, block):
        return pltpu.make_async_remote_copy(
            src_ref=ins[a], dst_ref=slot(a, block), send_sem=send_sems.at[a, k - 1], recv_sem=recv_sems.at[a, k - 1],
            device_id=_flip(me, k), device_id_type=MESH)

    sends = [copy(a, k, me) for a in range(n) for k in range(1, N_DEV)]
    for cp in sends:
        cp.start()
    for a in range(n):
        rows = ins[a].shape[0]
        outs[a][pl.ds(pl.multiple_of(_dev_index(me) * rows, SUBLANES), rows), :] = ins[a][...]
    if meanwhile:
        meanwhile()
    for a in range(n):
        for k in range(1, N_DEV):
            copy(a, k, _flip(me, k)).wait_recv()
    for cp in sends:
        cp.wait_send()


def _prologue(c, cw, fcw, w_ada, b_ada, w_rgate, w_igate, carry):
    cols = w_ada.shape[1]
    cw_w, fcw_w = cw.shape[-1], fcw.shape[-1]

    def body(c_ref, cw_ref, fcw_ref, w_ref, b_ref, wr_ref, wi_ref,
             call_ref, cwf_ref, fcwf_ref, modrow_ref, wrbd_ref, wibd_ref,
             mod_scr, c8, cw8, fcw8, cwall, fcwall, modall, s1, r1, s2, r2, start_carry):
        me = _dev_index(_my_pos())
        c8[...] = jnp.broadcast_to(c_ref[...], c8.shape)
        cw8[...] = jnp.zeros(cw8.shape, F32)
        cw8[0:LRU_CONV_K, :] = cw_ref[...]
        fcw8[...] = jnp.zeros(fcw8.shape, F32)
        for k in range(FFN_CONV_K):
            fcw8[k:k + 1, :] = fcw_ref[k]

        def block_diagonals():
            for bd_ref, hb_ref in ((wrbd_ref, wr_ref), (wibd_ref, wi_ref)):
                bd_ref[...] = jnp.zeros(bd_ref.shape, BF16)
                for h in range(N_HEADS):
                    span = slice(h * HEAD_DIM, (h + 1) * HEAD_DIM)
                    bd_ref[span, span] = hb_ref[h].astype(BF16)

        def conv_weights():
            for d in range(N_DEV):
                cwf_ref[:, d * cw_w:(d + 1) * cw_w] = cwall[d * SUBLANES:d * SUBLANES + LRU_CONV_K, :]
                fcwf_ref[:, d * fcw_w:(d + 1) * fcw_w] = fcwall[d * SUBLANES:d * SUBLANES + FFN_CONV_K, :]

        _all_gather_small([c8, cw8, fcw8], [call_ref, cwall, fcwall], s1, r1, meanwhile=block_diagonals)
        start_carry()
        cv = _row_of_each(call_ref, 0)
        ca = cv * _sigmoid(cv)
        b_cols = _my_columns(b_ref[...], cols, me)
        mod_scr[...] = jnp.dot(ca, w_ref[...], preferred_element_type=F32, precision=lax.Precision.HIGHEST) + b_cols
        _all_gather_small([mod_scr], [modall], s2, r2, meanwhile=conv_weights)
        mine = _rows((N_DEV, cols)) == me
        for d in range(N_DEV):
            piece = jnp.where(mine, modall[d * N_DEV:(d + 1) * N_DEV, :], 0.0)
            modrow_ref[:, d * cols:(d + 1) * cols] = jnp.sum(piece, axis=0, keepdims=True)

    sem = lambda n: pltpu.SemaphoreType.DMA((n, N_DEV - 1))
    gate = N_HEADS * HEAD_DIM
    return _call(
        body, "prologue", (1,), in_specs=[_whole()] * 7, out_specs=[_whole()] * 6,
        out_shape=[_sds((N_DEV * SUBLANES, c.shape[-1]), F32), _sds((LRU_CONV_K, N_DEV * cw_w), F32),
                   _sds((FFN_CONV_K, N_DEV * fcw_w), F32), _sds((1, N_DEV * cols), F32),
                   _sds((gate, gate), BF16), _sds((gate, gate), BF16)],
        scratch=[pltpu.VMEM((N_DEV, cols), F32)] + [pltpu.VMEM((SUBLANES, a.shape[-1]), F32) for a in (c, cw, fcw)]
        + [pltpu.VMEM((N_DEV * SUBLANES, cw_w), F32), pltpu.VMEM((N_DEV * SUBLANES, fcw_w), F32),
           pltpu.VMEM((N_DEV * N_DEV, cols), F32), sem(3), sem(3), sem(1), sem(1)],
        args=(c, cw, fcw, w_ada, b_ada, w_rgate, w_igate), carry=carry, body_starts_carry=True)


def _reduce_small(gath, red, carry=None):
    n_g, n_r = len(gath), len(red)
    chip_flips = CHIP_FLIPS

    def body(*refs, start_carry):
        g_in, r_in = refs[:n_g], refs[n_g:n_g + n_r]
        g_out, r_out = refs[n_g + n_r:2 * n_g + n_r], refs[2 * n_g + n_r:2 * (n_g + n_r)]
        scr = refs[2 * (n_g + n_r):]
        sib, land = scr[:n_r], scr[n_r:2 * n_r]
        g_send, g_recv, s_send, s_recv, i_send, i_recv, f_send, f_recv = scr[2 * n_r:]
        me = _my_pos()
        c = me[2]
        sibling = _flip(me, 1)

        def slot(a, pos):
            return g_out[a].at[pl.ds(pl.multiple_of(_dev_index(pos) * SUBLANES, SUBLANES), SUBLANES), :]

        def gcopy(a, k):
            return pltpu.make_async_remote_copy(
                src_ref=g_in[a], dst_ref=slot(a, me), send_sem=g_send.at[a, k - 1], recv_sem=g_recv.at[a, k - 1],
                device_id=_flip(me, k), device_id_type=MESH)

        def scopy(a):
            return pltpu.make_async_remote_copy(
                src_ref=r_in[a], dst_ref=sib[a], send_sem=s_send.at[a], recv_sem=s_recv.at[a],
                device_id=sibling, device_id_type=MESH)

        def icopy(a, j):
            return pltpu.make_async_remote_copy(
                src_ref=r_out[a], dst_ref=land[a].at[j], send_sem=i_send.at[a, j], recv_sem=i_recv.at[a, j],
                device_id=_flip(me, chip_flips[j]), device_id_type=MESH)

        def fcopy(a, j):
            return pltpu.make_async_remote_copy(
                src_ref=land[a].at[j], dst_ref=land[a].at[j], send_sem=f_send.at[a, j], recv_sem=f_recv.at[a, j],
                device_id=sibling, device_id_type=MESH)

        gathers = [gcopy(a, k) for a in range(n_g) for k in range(1, N_DEV)]
        swaps = [scopy(a) for a in range(n_r)]
        for cp in gathers + swaps:
            cp.start()
        for a in range(n_g):
            g_out[a][pl.ds(pl.multiple_of(_dev_index(me) * SUBLANES, SUBLANES), SUBLANES), :] = g_in[a][...]
        for a in range(n_r):
            swaps[a].wait_recv()
            r_out[a][...] = r_in[a][...] + sib[a][...]

        for core in range(2):
            @pl.when(c == core)
            def _():
                for a in range(core, n_r, 2):
                    for j in range(3):
                        icopy(a, j).start()

        start_carry()

        for core in range(2):
            mine = [a for a in range(n_r) if a % 2 == core]
            theirs = [a for a in range(n_r) if a % 2 != core]

            @pl.when(c == core)
            def _():
                out = [icopy(a, j) for a in mine for j in range(3)]
                fwd = []
                for a in mine:
                    for j in range(3):
                        icopy(a, j).wait_recv()
                        cp = fcopy(a, j)
                        cp.start()
                        fwd.append(cp)
                for a in theirs:
                    for j in range(3):
                        fcopy(a, j).wait_recv()
                for cp in out + fwd:
                    cp.wait_send()

        for a in range(n_r):
            r_out[a][...] = (r_out[a][...] + land[a][1]) + (land[a][0] + land[a][2])
        for a in range(n_g):
            for k in range(1, N_DEV):
                pltpu.make_async_remote_copy(
                    src_ref=g_in[a], dst_ref=slot(a, _flip(me, k)), send_sem=g_send.at[a, k - 1],
                    recv_sem=g_recv.at[a, k - 1], device_id=_flip(me, k), device_id_type=MESH).wait_recv()
        for cp in gathers + swaps:
            cp.wait_send()

    shapes = [tuple(a.shape) for a in red]
    outs, carried = _call(
        body, "reduce_small", (1,), in_specs=[_whole()] * (n_g + n_r), out_specs=[_whole()] * (n_g + n_r),
        out_shape=[_sds((N_DEV * SUBLANES, a.shape[1]), F32) for a in gath] + [_sds(s, F32) for s in shapes],
        scratch=[pltpu.VMEM(s, F32) for s in shapes] + [pltpu.VMEM((3,) + s, F32) for s in shapes]
        + [pltpu.SemaphoreType.DMA((n_g, N_DEV - 1)), pltpu.SemaphoreType.DMA((n_g, N_DEV - 1)),
           pltpu.SemaphoreType.DMA((n_r,)), pltpu.SemaphoreType.DMA((n_r,)),
           pltpu.SemaphoreType.DMA((n_r, 3)), pltpu.SemaphoreType.DMA((n_r, 3)),
           pltpu.SemaphoreType.DMA((n_r, 3)), pltpu.SemaphoreType.DMA((n_r, 3))],
        args=tuple(gath) + tuple(red), carry=carry, body_starts_carry=True)
    return (outs[:n_g], outs[n_g:]), carried


STACKED = "stacked"


def _region(ref, shard_shape, col_sharded, pos):
    r, cdim = shard_shape
    d = _dev_index(pos)
    if col_sharded == STACKED:
        return ref.at[d]
    if col_sharded:
        return ref.at[:, pl.ds(pl.multiple_of(d * cdim, LANES), cdim)]
    return ref.at[pl.ds(pl.multiple_of(d * r, 2 * SUBLANES), r), :]


def _gather_carry(shards, col_sharded):
    n_w = len(shards)
    shapes = [tuple(s.shape) for s in shards]
    full_shapes = [(N_DEV,) + s if cs == STACKED else (s[0], s[1] * N_DEV) if cs else (s[0] * N_DEV, s[1])
                   for s, cs in zip(shapes, col_sharded)]

    def tools(out_refs, scr):
        send_sems, recv_sems = scr[n_w], scr[n_w + 1]
        me = _my_pos()
        x, y, c = me
        sibling = (x, y, 1 - c)
        chips = [(1 - x, y), (x, 1 - y), (1 - x, 1 - y)]

        def region(w, pos):
            return _region(out_refs[w], shapes[w], col_sharded[w], pos)

        def copy(w, k, block, to, src=None):
            return pltpu.make_async_remote_copy(
                src_ref=region(w, block) if src is None else src, dst_ref=region(w, block),
                send_sem=send_sems.at[w, k], recv_sem=recv_sems.at[w, k], device_id=to, device_id_type=MESH)

        def first(w):
            return [copy(w, 0, me, sibling, src=scr[w])] + [
                copy(w, 1 + j, me, (*chip, c), src=scr[w]) for j, chip in enumerate(chips)]

        def mine(w):
            return pltpu.make_async_copy(scr[w], region(w, me), scr[n_w + 2].at[w])

        return me, c, sibling, chips, copy, first, mine

    def start(ins, outs, scr):
        _, _, _, _, _, first, mine = tools(outs, scr)
        for w in range(n_w):
            scr[w][...] = ins[w][...].astype(BF16)
            for cp in first(w) + [mine(w)]:
                cp.start()

    def finish(ins, outs, scr):
        me, c, sibling, chips, copy, first, mine = tools(outs, scr)
        passed = []
        for w in range(n_w):
            for j, chip in enumerate(chips):
                copy(w, 1 + j, (*chip, c), me).wait_recv()
                fwd = copy(w, 4 + j, (*chip, c), sibling)
                fwd.start()
                passed.append(fwd)
        for w in range(n_w):
            copy(w, 0, sibling, me).wait_recv()
            for j, chip in enumerate(chips):
                copy(w, 4 + j, (*chip, 1 - c), me).wait_recv()
        for w in range(n_w):
            for cp in first(w):
                cp.wait_send()
            mine(w).wait()
        for cp in passed:
            cp.wait_send()

    return _Carry(
        inputs=list(shards), in_specs=[_whole()] * n_w,
        out_shape=[_sds(s, BF16) for s in full_shapes], out_specs=[_any()] * n_w,
        scratch=[pltpu.VMEM(s, BF16) for s in shapes]
        + [pltpu.SemaphoreType.DMA((n_w, N_DEV - 1)), pltpu.SemaphoreType.DMA((n_w, N_DEV - 1)),
           pltpu.SemaphoreType.DMA((n_w,))],
        start=start, finish=finish)


CHIP_FLIPS = (4, 2, 6)


def _pair_reduce(g_bf, g_own, col_sharded):
    shape = tuple(g_own.shape)
    n = len(CHIP_FLIPS)

    def body(g_ref, own_ref, hown_ref, hout_ref, mine, sib, send_sems, recv_sems, local_sems):
        me = _my_pos()
        sibling = _flip(me, 1)
        flips = (0,) + CHIP_FLIPS

        def region(pos):
            return _region(g_ref, shape, col_sharded, pos)

        local = [pltpu.make_async_copy(region(_flip(me, f)), mine.at[s], local_sems.at[s])
                 for s, f in enumerate(CHIP_FLIPS)]
        sends = [pltpu.make_async_remote_copy(
            src_ref=region(_flip(sibling, f)), dst_ref=sib.at[s], send_sem=send_sems.at[s], recv_sem=recv_sems.at[s],
            device_id=sibling, device_id_type=MESH) for s, f in enumerate(flips)]
        for cp in local + sends:
            cp.start()
        for cp in local:
            cp.wait()
        for cp in sends:
            cp.wait_recv()
        hown_ref[...] = own_ref[...] + sib[0].astype(F32)
        for s in range(n):
            hout_ref[s] = (mine[s].astype(F32) + sib[s + 1].astype(F32)).astype(BF16)
        for cp in sends:
            cp.wait_send()

    return pl.pallas_call(
        body, name="pair_reduce", out_shape=[_sds(shape, F32), _sds((n,) + shape, BF16)],
        in_specs=[_any(), _whole()], out_specs=[_whole(), _whole()],
        scratch_shapes=[pltpu.VMEM((n,) + shape, BF16), pltpu.VMEM((n + 1,) + shape, BF16),
                        pltpu.SemaphoreType.DMA((n + 1,)), pltpu.SemaphoreType.DMA((n + 1,)),
                        pltpu.SemaphoreType.DMA((n,))],
        compiler_params=pltpu.CompilerParams(vmem_limit_bytes=VMEM_LIMIT),
    )(g_bf, g_own)


def _chip_scatter_carry(h_out):
    n = len(CHIP_FLIPS)

    def copies(ins, outs, scr):
        send_sems, recv_sems = scr
        me = _my_pos()
        return [pltpu.make_async_remote_copy(
            src_ref=ins[0].at[j], dst_ref=outs[0].at[j], send_sem=send_sems.at[j], recv_sem=recv_sems.at[j],
            device_id=_flip(me, CHIP_FLIPS[j]), device_id_type=MESH) for j in range(n)]

    def start(ins, outs, scr):
        for cp in copies(ins, outs, scr):
            cp.start()

    def finish(ins, outs, scr):
        cps = copies(ins, outs, scr)
        for cp in cps:
            cp.wait_recv()
        for cp in cps:
            cp.wait_send()

    return _Carry(inputs=[h_out], in_specs=[_any()], out_shape=[_sds(tuple(h_out.shape), BF16)], out_specs=[_any()],
                  scratch=[pltpu.SemaphoreType.DMA((n,)), pltpu.SemaphoreType.DMA((n,))], start=start, finish=finish)


def _scatter_carry(grads_bf, shard_shapes, col_sharded, relations):
    n_w = len(grads_bf)
    shapes = [tuple(s) for s in shard_shapes]

    def copies(ins, outs, scr):
        send_sems, recv_sems = scr
        me = _my_pos()
        out = []
        for w in range(n_w):
            for i, k in enumerate(relations[w]):
                peer = _flip(me, k)
                out.append(pltpu.make_async_remote_copy(
                    src_ref=_region(ins[w], shapes[w], col_sharded[w], peer), dst_ref=outs[w].at[i],
                    send_sem=send_sems.at[w, i], recv_sem=recv_sems.at[w, i],
                    device_id=peer, device_id_type=MESH))
        return out

    def start(ins, outs, scr):
        for cp in copies(ins, outs, scr):
            cp.start()

    def finish(ins, outs, scr):
        cps = copies(ins, outs, scr)
        for cp in cps:
            cp.wait_recv()
        for cp in cps:
            cp.wait_send()

    return _Carry(
        inputs=list(grads_bf), in_specs=[_any()] * n_w,
        out_shape=[_sds((len(r),) + s, BF16) for r, s in zip(relations, shapes)], out_specs=[_any()] * n_w,
        scratch=[pltpu.SemaphoreType.DMA((n_w, N_DEV - 1)), pltpu.SemaphoreType.DMA((n_w, N_DEV - 1))],
        start=start, finish=finish)


def _block_diag(w):
    eye = jnp.eye(N_HEADS, dtype=w.dtype)
    return (eye[:, None, :, None] * w[:, :, None, :]).reshape(N_HEADS * HEAD_DIM, N_HEADS * HEAD_DIM)


def _local_step(x2, target, mod, w_in_f, w_full, conv_w_full, ffn_cw_full,
                g_mix_pre, g_mix_post, conv_b, w_rgate, b_rgate, w_igate, b_igate, lru_a, v_norm_g, v_norm_b,
                w_spatial, b_spatial, g_lru_out, g_gmlp_out, g_ffn_pre, g_ffn_post, ffn_conv_b,
                gather=None, scatter=None, adam=None, gate_bd=None):
    sh_m, sc_m, gt_m, sh_f, sc_f, gt_f = [mod[k] for k in range(N_MOD)]
    wr_bd, wi_bd = gate_bd if gate_bd else [_block_diag(w[0]).astype(BF16) for w in (w_rgate, w_igate)]
    b_r = b_rgate.reshape(1, LRU_W)
    b_i = b_igate.reshape(1, LRU_W)
    b_sp_t = b_spatial[0].T
    w_sp_t = jnp.swapaxes(w_spatial[0], 1, 2)

    def arriving(*names):
        return gather(*names) if gather else None

    near, far = (1, 2, 3, 4, 5), (6, 7)

    def leaving(*parts):
        return scatter(parts) if scatter else None

    def received(recv, parts, outs):
        for (name, _, _), out in zip(parts, outs):
            recv.setdefault(name, []).append(out)

    mix_params = (conv_w_full, conv_b, wr_bd, wi_bd, b_r, b_i, lru_a, v_norm_g, v_norm_b)
    w_out_f = w_full["w_out"]
    (z, h, ycat, hl, y, x1, h2), got = _mix_fwd(
        x2, sh_m, sc_m, g_mix_pre, w_in_f, *mix_params, w_spatial[0], b_sp_t, g_lru_out, g_gmlp_out,
        w_out_f, g_mix_post, gt_m, g_ffn_pre, sc_f, sh_f, carry=arriving("w_up"))
    w_up_f = got[0] if gather else w_full["w_up"]
    (up_pre, up, act), got = _ffn_fwd(h2, w_up_f, ffn_cw_full, ffn_conv_b, carry=arriving("w_down"))
    w_down_f = got[0] if gather else w_full["w_down"]
    d_y2, dout, loss_acc, vs_ffn = _ffn_tail(act, w_down_f, x1, gt_f, g_ffn_post, target)

    recv, updated = {}, {}

    def updating(grads):
        if not adam:
            return None
        return _adamw_rider([(adam[n][0], g[1], recv[n], adam[n][1], adam[n][2]) for n, g in grads.items()], N_DEV)

    def updates(grads, outs):
        for j, n in enumerate(grads):
            updated[n] = tuple(outs[4 * j:4 * j + 4])

    gw_down, _ = _wgrad(act, d_y2, "wgrad_down", by_rows=True)
    parts = [("w_down", gw_down[0], near + far)]
    (d_up, cs_ffn), got = _ffn_bwd(d_y2, up_pre, up, ffn_cw_full, w_down_f, carry=leaving(*parts))
    received(recv, parts, got)
    gw_up, got = _wgrad(h2, d_up, "wgrad_up", carry=updating(dict(w_down=gw_down)))
    updates(dict(w_down=gw_down), got)
    parts = [("w_up", gw_up[0], near)]
    (d_x1, d_y, d_ycat, vs_up), got = _up_bwd(
        d_up, w_up_f, x1, dout, y, w_out_f, g_ffn_pre, sc_f, g_mix_post, gt_m, carry=leaving(*parts))
    received(recv, parts, got)
    gw_out, _ = _wgrad(ycat, d_y, "wgrad_out", by_rows=True)
    parts = [("w_up", gw_up[0], far), ("w_out", gw_out[0], near + far)]
    (d_z, vs_mix, dcw, d_wr, d_wi, d_ws, d_bs), got = _mix_bwd(
        d_ycat, z, hl, *mix_params, w_spatial[0], w_sp_t, b_sp_t, g_lru_out, g_gmlp_out, carry=leaving(*parts))
    received(recv, parts, got)
    gw_in, got = _wgrad(h, d_z, "wgrad_in", carry=updating(dict(w_up=gw_up, w_out=gw_out)))
    updates(dict(w_up=gw_up, w_out=gw_out), got)
    chip_sums = None
    if scatter:
        h_own, h_out = _pair_reduce(gw_in[0], gw_in[1], True)
        gw_in = (gw_in[0], h_own)
        chip_sums = _chip_scatter_carry(h_out)
    (grad_x, vs_in), got = _in_bwd(d_z, w_in_f, x2, d_x1, g_mix_pre, sc_m, carry=chip_sums)
    recv["w_in"] = list(got)

    gath = [vs_in, vs_up, vs_ffn, loss_acc]
    red = [cs_ffn, vs_mix, dcw, d_wr, d_wi, d_ws.reshape(N_GROUPS * POS_BLOCK, POS_BLOCK), d_bs]
    return dict(grad_x=grad_x, gath=gath, red=red, recv=recv, updated=updated,
                w_in=gw_in, w_out=gw_out, w_up=gw_up, w_down=gw_down)


def kernel(x, c, w_ada, b_ada, g_mix_pre, g_mix_post, w_in, conv_w, conv_b, w_rgate, b_rgate, w_igate, b_igate, lru_a, v_norm_g, v_norm_b, w_spatial, b_spatial, g_lru_out, g_gmlp_out, w_out, g_ffn_pre, g_ffn_post, w_up, ffn_conv_w, ffn_conv_b, w_down, loss_target, m_w_ada, m_b_ada, m_g_mix_pre, m_g_mix_post, m_w_in, m_conv_w, m_conv_b, m_w_rgate, m_b_rgate, m_w_igate, m_b_igate, m_lru_a, m_v_norm_g, m_v_norm_b, m_w_spatial, m_b_spatial, m_g_lru_out, m_g_gmlp_out, m_w_out, m_g_ffn_pre, m_g_ffn_post, m_w_up, m_ffn_conv_w, m_ffn_conv_b, m_w_down, v_w_ada, v_b_ada, v_g_mix_pre, v_g_mix_post, v_w_in, v_conv_w, v_conv_b, v_w_rgate, v_b_rgate, v_w_igate, v_b_igate, v_lru_a, v_v_norm_g, v_v_norm_b, v_w_spatial, v_b_spatial, v_g_lru_out, v_g_gmlp_out, v_w_out, v_g_ffn_pre, v_g_ffn_post, v_w_up, v_ffn_conv_w, v_ffn_conv_b, v_w_down):
    big_w = dict(w_in=(w_in, m_w_in, v_w_in, True), w_out=(w_out, m_w_out, v_w_out, False),
                 w_up=(w_up, m_w_up, v_w_up, True), w_down=(w_down, m_w_down, v_w_down, False))

    def gather(*names):
        return _gather_carry([big_w[n][0][0] for n in names], [STACKED if n == "w_up" else big_w[n][3] for n in names])

    def scatter(parts):
        return _scatter_carry([g for _, g, _ in parts], [big_w[n][0].shape[1:] for n, _, _ in parts],
                              [big_w[n][3] for n, _, _ in parts], [rel for _, _, rel in parts])

    ffn_cw_taps = tuple(a.reshape(FFN_CONV_K, 1, -1) for a in (ffn_conv_w, m_ffn_conv_w, v_ffn_conv_w))
    (c_all, conv_w_full, ffn_cw_full, mod_row, wr_bd, wi_bd), (w_in_f, w_out_f) = _prologue(
        c, conv_w[0], ffn_cw_taps[0], w_ada[0], b_ada, w_rgate[0], w_igate[0], carry=gather("w_in", "w_out"))
    mod = mod_row.reshape(N_MOD, 1, D_MODEL)

    loc = _local_step(x[0], loss_target[0], mod, w_in_f, dict(w_out=w_out_f), conv_w_full, ffn_cw_full,
                      g_mix_pre, g_mix_post, conv_b, w_rgate, b_rgate, w_igate, b_igate, lru_a, v_norm_g, v_norm_b,
                      w_spatial, b_spatial, g_lru_out, g_gmlp_out, g_ffn_pre, g_ffn_post, ffn_conv_b,
                      gather=gather, scatter=scatter,
                      adam={n: big_w[n][:3] for n in ("w_out", "w_up", "w_down")}, gate_bd=(wr_bd, wi_bd))
    grad_x = loc["grad_x"]

    (gathered, reduced), _ = _reduce_small(loc["gath"], loc["red"])

    results = dict(loc["updated"])
    w_, m_, v_, _ = big_w["w_in"]
    results["w_in"] = _adamw_sum(w_, loc["w_in"][1], loc["recv"]["w_in"], m_, v_, "adamw_w_in")

    params = dict(
        b_ada=(b_ada, m_b_ada, v_b_ada), g_mix_pre=(g_mix_pre, m_g_mix_pre, v_g_mix_pre),
        g_mix_post=(g_mix_post, m_g_mix_post, v_g_mix_post), conv_b=(conv_b, m_conv_b, v_conv_b),
        w_rgate=(w_rgate, m_w_rgate, v_w_rgate), b_rgate=(b_rgate, m_b_rgate, v_b_rgate),
        w_igate=(w_igate, m_w_igate, v_w_igate), b_igate=(b_igate, m_b_igate, v_b_igate),
        lru_a=(lru_a, m_lru_a, v_lru_a), v_norm_g=(v_norm_g, m_v_norm_g, v_v_norm_g),
        v_norm_b=(v_norm_b, m_v_norm_b, v_v_norm_b), w_spatial=(w_spatial, m_w_spatial, v_w_spatial),
        b_spatial=(b_spatial, m_b_spatial, v_b_spatial), g_lru_out=(g_lru_out, m_g_lru_out, v_g_lru_out),
        g_gmlp_out=(g_gmlp_out, m_g_gmlp_out, v_g_gmlp_out), g_ffn_pre=(g_ffn_pre, m_g_ffn_pre, v_g_ffn_pre),
        g_ffn_post=(g_ffn_post, m_g_ffn_post, v_g_ffn_post), ffn_conv_b=(ffn_conv_b, m_ffn_conv_b, v_ffn_conv_b))
    conv_params = dict(conv_w=(conv_w, m_conv_w, v_conv_w), ffn_conv_w=ffn_cw_taps)
    small_results, loss = _adamw_small(gathered, reduced, params, conv_params)
    results.update(small_results)
    results["ffn_conv_w"] = tuple(a.reshape(ffn_conv_w.shape) for a in results["ffn_conv_w"])
    loss = loss.reshape(())

    results["w_ada"] = _adamw_wada(c_all, gathered[0], gathered[1], gathered[2], w_ada, m_w_ada, v_w_ada)

    order = ["w_ada", "b_ada", "g_mix_pre", "g_mix_post", "w_in", "conv_w", "conv_b", "w_rgate", "b_rgate", "w_igate",
             "b_igate", "lru_a", "v_norm_g", "v_norm_b", "w_spatial", "b_spatial", "g_lru_out", "g_gmlp_out", "w_out",
             "g_ffn_pre", "g_ffn_post", "w_up", "ffn_conv_w", "ffn_conv_b", "w_down"]
    outs = [loss, grad_x[None]]
    for kind in range(4):
        outs += [results[n][kind] for n in order]
    return tuple(outs)
```

```python
import functools
import math

import jax
import jax.numpy as jnp
from jax import lax
from jax.experimental import pallas as pl
from jax.experimental.pallas import tpu as pltpu

F32 = jnp.float32
BF16 = jnp.bfloat16

D_MODEL = 1024
LRU_W = 512
GMLP_W = 512
N_HEADS = 8
HEAD_DIM = 64
N_GROUPS = 4
POS_BLOCK = 128
CHUNK = 64
IN_COLS = 2048
D_FF = 3072
N_MOD = 6
N_DEV = 8
EPS = 1e-6
LRU_C = 8.0
LRU_CONV_K = 4
FFN_CONV_K = 3

ADAM_LR = 0.001
ADAM_B1 = 0.9
ADAM_B2 = 0.999
ADAM_EPS = 1e-08
ADAM_WD = 0.01
ADAM_STEP = 10

LANES = 128
SUBLANES = 8
TT_BIG = 512
TT_MIX = 256
FF_CW = 1024
VMEM_LIMIT = 56 * 1024 * 1024

MESH = pl.DeviceIdType.MESH


def _sds(shape, dtype):
    return jax.ShapeDtypeStruct(shape, dtype)


def _cparams(sem=None):
    return pltpu.CompilerParams(dimension_semantics=sem, vmem_limit_bytes=VMEM_LIMIT)


def _whole():
    return pl.BlockSpec(memory_space=pltpu.VMEM)


def _const(shape):
    nd = len(shape)
    return pl.BlockSpec(shape, lambda *_: (0,) * nd)


def _any():
    return pl.BlockSpec(memory_space=pl.ANY)


class _Carry:
    def __init__(self, inputs, in_specs, out_shape, out_specs, scratch, start=None, finish=None, each=None):
        self.inputs, self.in_specs, self.out_shape, self.out_specs = inputs, in_specs, out_shape, out_specs
        self.scratch, self.start, self.finish, self.each = scratch, start, finish, each


def _call(body, name, grid, in_specs, out_specs, out_shape, scratch, args, carry=None, body_starts_carry=False):
    n_in, n_out, n_scr = len(in_specs), len(out_specs), len(scratch)
    c_in = len(carry.in_specs) if carry else 0
    c_out = len(carry.out_specs) if carry else 0

    def full_body(*refs):
        ins = refs[:n_in]
        c_ins = refs[n_in:n_in + c_in]
        outs = refs[n_in + c_in:n_in + c_in + n_out]
        c_outs = refs[n_in + c_in + n_out:n_in + c_in + n_out + c_out]
        scr = refs[n_in + c_in + n_out + c_out:n_in + c_in + n_out + c_out + n_scr]
        c_scr = refs[n_in + c_in + n_out + c_out + n_scr:]
        if carry:
            first = functools.reduce(lambda a, b: a & b, [pl.program_id(d) == 0 for d in range(len(grid))])
            last = functools.reduce(lambda a, b: a & b, [pl.program_id(d) == g - 1 for d, g in enumerate(grid)])

        if carry and carry.start and not body_starts_carry:
            @pl.when(first)
            def _():
                carry.start(c_ins, c_outs, c_scr)

        if body_starts_carry:
            body(*ins, *outs, *scr, start_carry=(lambda: carry.start(c_ins, c_outs, c_scr)) if carry else (lambda: None))
        else:
            body(*ins, *outs, *scr)
        if carry and carry.each:
            carry.each(c_ins, c_outs, c_scr)
        if carry and carry.finish:
            @pl.when(last)
            def _():
                carry.finish(c_ins, c_outs, c_scr)

    res = pl.pallas_call(
        full_body, name=name, grid=grid,
        in_specs=list(in_specs) + (list(carry.in_specs) if carry else []),
        out_specs=list(out_specs) + (list(carry.out_specs) if carry else []),
        out_shape=list(out_shape) + (list(carry.out_shape) if carry else []),
        scratch_shapes=list(scratch) + (list(carry.scratch) if carry else []),
        compiler_params=_cparams(("arbitrary",) * len(grid)),
    )(*args, *(carry.inputs if carry else []))
    return res[:n_out], res[n_out:]


GELU_C0 = 0.7978845608028654
GELU_C1 = GELU_C0 * 0.044715


def _gelu(x):
    t = jnp.tanh(x * (GELU_C0 + GELU_C1 * (x * x)))
    hx = 0.5 * x
    return hx + hx * t


def _gelu_and_grad(x):
    x2 = x * x
    t = jnp.tanh(x * (GELU_C0 + GELU_C1 * x2))
    hx = 0.5 * x
    g = hx + hx * t
    dg = (0.5 + 0.5 * t) + hx * (1.0 - t * t) * (GELU_C0 + 3.0 * GELU_C1 * x2)
    return g, dg


def _sigmoid(x):
    return 1.0 / (1.0 + jnp.exp(-x))


def _softplus(x):
    return jnp.maximum(x, 0.0) + jnp.log1p(jnp.exp(-jnp.abs(x)))


def _neg_expm1(x):
    series = -x * (1.0 + x * (0.5 + x * (1.0 / 6.0 + x * (1.0 / 24.0 + x * (1.0 / 120.0)))))
    return jnp.where(x > -0.1, series, 1.0 - jnp.exp(x))


def _dot(a, b):
    return jnp.dot(a.astype(BF16), b.astype(BF16), preferred_element_type=F32)


def _dot_nt(a, b):
    return lax.dot_general(a.astype(BF16), b.astype(BF16), (((1,), (1,)), ((), ())), preferred_element_type=F32)


def _dot_tn(a, b):
    return lax.dot_general(a.astype(BF16), b.astype(BF16), (((0,), (0,)), ((), ())), preferred_element_type=F32)


def _rows(shape):
    return lax.broadcasted_iota(jnp.int32, shape, 0)


def _shift_down(cur, prev8, s):
    if s == 0:
        return cur
    n = cur.shape[0]
    r = pltpu.roll(cur, s, 0)
    p = pltpu.roll(prev8, s, 0)
    top = jnp.where(_rows(p.shape) < s, p, r[0:SUBLANES])
    if n == SUBLANES:
        return top
    return jnp.concatenate([top, r[SUBLANES:]], axis=0)


def _shift_up(cur, next8, s):
    if s == 0:
        return cur
    n = cur.shape[0]
    r = pltpu.roll(cur, n - s, 0)
    q = pltpu.roll(next8, SUBLANES - s, 0)
    bot = jnp.where(_rows(q.shape) >= SUBLANES - s, q, r[n - SUBLANES:])
    if n == SUBLANES:
        return bot
    return jnp.concatenate([r[:n - SUBLANES], bot], axis=0)


def _scan_fwd(a, b, h_in):
    n = a.shape[0]
    in_group = _rows(a.shape) & (SUBLANES - 1)
    s = 1
    while s < SUBLANES:
        a_s = pltpu.roll(a, s, 0)
        b_s = pltpu.roll(b, s, 0)
        m = in_group >= s
        b = jnp.where(m, a * b_s + b, b)
        a = jnp.where(m, a * a_s, a)
        s *= 2
    out, carry = [], h_in
    for g in range(n // SUBLANES):
        rows = slice(g * SUBLANES, (g + 1) * SUBLANES)
        h_g = a[rows] * carry + b[rows]
        out.append(h_g)
        carry = h_g[SUBLANES - 1:SUBLANES, :]
    return jnp.concatenate(out, axis=0)


def _scan_rev(a, b, l_in):
    n = a.shape[0]
    in_group = _rows(a.shape) & (SUBLANES - 1)
    s = 1
    while s < SUBLANES:
        a_s = pltpu.roll(a, n - s, 0)
        b_s = pltpu.roll(b, n - s, 0)
        m = in_group < SUBLANES - s
        b = jnp.where(m, b + a * b_s, b)
        a = jnp.where(m, a * a_s, a)
        s *= 2
    out, carry = [], l_in
    for g in reversed(range(n // SUBLANES)):
        rows = slice(g * SUBLANES, (g + 1) * SUBLANES)
        l_g = b[rows] + a[rows] * carry
        out.append(l_g)
        carry = l_g[0:1, :]
    return jnp.concatenate(out[::-1], axis=0)


def _rms(x):
    r = lax.rsqrt(jnp.mean(x * x, axis=-1, keepdims=True) + EPS)
    return x * r, r


def _rms_bwd(d_n, n, r):
    return r * (d_n - n * jnp.mean(d_n * n, axis=-1, keepdims=True))


def _colsum(x):
    return jnp.sum(x, axis=0, keepdims=True)


ROW_PIECE = 256


def _row_pieces(tt):
    return [slice(r, r + min(ROW_PIECE, tt)) for r in range(0, tt, min(ROW_PIECE, tt))]


def _lru_gates(xc, wr_ref, wi_ref, br, bi, sp_a):
    r = _sigmoid(_dot(xc, wr_ref[...]) + br)
    i = _sigmoid(_dot(xc, wi_ref[...]) + bi)
    la = -LRU_C * r * sp_a
    a = jnp.exp(la)
    mult = jnp.sqrt(_neg_expm1(2.0 * la))
    return r, i, a, mult


def _lru_conv(lx, prev8, cw_ref, cb):
    xc = cb + cw_ref[LRU_CONV_K - 1:LRU_CONV_K, :] * lx
    taps = []
    for k in range(LRU_CONV_K - 1):
        tap = _shift_down(lx, prev8, LRU_CONV_K - 1 - k)
        taps.append(tap)
        xc = xc + cw_ref[k:k + 1, :] * tap
    return xc, taps


def _ws_mask(transposed=False):
    i = lax.broadcasted_iota(jnp.int32, (POS_BLOCK, POS_BLOCK), 0)
    j = lax.broadcasted_iota(jnp.int32, (POS_BLOCK, POS_BLOCK), 1)
    if transposed:
        i, j = j, i
    return (j // CHUNK) <= (i // CHUNK)


def _gmlp_v(gv, vg, vb):
    av, dav = _gelu_and_grad(gv)
    mu = jnp.mean(av, axis=-1, keepdims=True)
    cen = av - mu
    rs = lax.rsqrt(jnp.mean(cen * cen, axis=-1, keepdims=True) + EPS)
    vhat = cen * rs
    return vhat * vg + vb, vhat, rs, dav


def _mix_fwd(x, sh, sc, g_pre, w_in, conv_w, conv_b, wr_bd, wi_bd, b_r, b_i, lru_a, vn_g, vn_b, w_sp, b_sp_t,
             g_lru, g_gmlp, w_out, g_post, gt_m, g_ffn_pre, sc_f, sh_f, carry=None):
    s_len = x.shape[0]
    tt = min(TT_MIX, s_len)
    nblk = tt // POS_BLOCK

    def body(x_ref, sh_ref, sc_ref, g_ref, w_ref, cw_ref, cb_ref, wr_ref, wi_ref, br_ref, bi_ref, la_ref, vg_ref,
             vb_ref, ws_ref, bst_ref, gl_ref, gg_ref, wo_ref, gp_ref, gtm_ref, g2_ref, scf_ref, shf_ref,
             z_ref, h_ref, y_ref, hl_ref, yo_ref, x1_ref, h2_ref, prev8, hcar):
        i = pl.program_id(0)

        @pl.when(i == 0)
        def _():
            prev8[...] = jnp.zeros_like(prev8)
            hcar[...] = jnp.zeros_like(hcar)

        n_x, _ = _rms(x_ref[...])
        h = (n_x * g_ref[...] * (1.0 + sc_ref[...]) + sh_ref[...]).astype(BF16)
        h_ref[...] = h
        z_ref[...] = jnp.dot(h, w_ref[...], preferred_element_type=F32)

        lx = z_ref[:, 0:LRU_W]
        gate = z_ref[:, LRU_W:2 * LRU_W]
        gu = z_ref[:, 2 * LRU_W:2 * LRU_W + GMLP_W]
        gv = z_ref[:, 2 * LRU_W + GMLP_W:]

        xc, _ = _lru_conv(lx, prev8[...], cw_ref, cb_ref[...])
        prev8[...] = lx[tt - SUBLANES:]
        sp_a = _softplus(-la_ref[...])
        _, ig, a, mult = _lru_gates(xc, wr_ref, wi_ref, br_ref[...], bi_ref[...], sp_a)
        bx = mult * (ig * xc)
        hl = _scan_fwd(a, bx, hcar[0:1, :])
        hcar[...] = jnp.broadcast_to(hl[tt - 1:tt, :], hcar.shape)
        hl_ref[...] = hl
        y_lru = hl * _gelu(gate)
        n_l, _ = _rms(y_lru)
        y_ref[:, 0:LRU_W] = (n_l * gl_ref[...]).astype(BF16)

        u = _gelu(gu)
        v, _, _, _ = _gmlp_v(gv, vg_ref[...], vb_ref[...])
        mask = _ws_mask()
        sp_parts = []
        for nb in range(nblk):
            row = []
            for g in range(N_GROUPS):
                wsm = jnp.where(mask, ws_ref[g], 0.0)
                vblk = v[nb * POS_BLOCK:(nb + 1) * POS_BLOCK, g * LANES:(g + 1) * LANES]
                row.append(_dot(wsm, vblk) + bst_ref[:, g:g + 1])
            sp_parts.append(jnp.concatenate(row, axis=1))
        sp = jnp.concatenate(sp_parts, axis=0) if nblk > 1 else sp_parts[0]
        n_g, _ = _rms(u * sp)
        y_ref[:, LRU_W:] = (n_g * gg_ref[...]).astype(BF16)

        y = jnp.dot(y_ref[...], wo_ref[...], preferred_element_type=F32)
        yo_ref[...] = y
        n_y, _ = _rms(y)
        x1 = x_ref[...] + gtm_ref[...] * (n_y * gp_ref[...])
        x1_ref[...] = x1
        n1, _ = _rms(x1)
        h2_ref[...] = (n1 * g2_ref[...] * (1.0 + scf_ref[...]) + shf_ref[...]).astype(BF16)

    row = lambda c: pl.BlockSpec((tt, c), lambda i: (i, 0))
    v512 = _const((1, LRU_W))
    vec = _const((1, D_MODEL))
    return _call(
        body, "mix_fwd", (s_len // tt,),
        in_specs=[row(D_MODEL), vec, vec, vec, _whole(),
                  _const((LRU_CONV_K, LRU_W)), v512, _whole(), _whole(), v512, v512, v512, v512, v512,
                  _whole(), _whole(), v512, v512, _whole(), vec, vec, vec, vec, vec],
        out_specs=[row(IN_COLS), row(D_MODEL), row(LRU_W + GMLP_W), row(LRU_W), row(D_MODEL), row(D_MODEL),
                   row(D_MODEL)],
        out_shape=[_sds((s_len, IN_COLS), F32), _sds((s_len, D_MODEL), BF16),
                   _sds((s_len, LRU_W + GMLP_W), BF16), _sds((s_len, LRU_W), F32),
                   _sds((s_len, D_MODEL), F32), _sds((s_len, D_MODEL), F32), _sds((s_len, D_MODEL), BF16)],
        scratch=[pltpu.VMEM((SUBLANES, LRU_W), F32), pltpu.VMEM((SUBLANES, LRU_W), F32)],
        args=(x, sh, sc, g_pre, w_in, conv_w, conv_b, wr_bd, wi_bd, b_r, b_i, lru_a, vn_g, vn_b, w_sp, b_sp_t,
              g_lru, g_gmlp, w_out, g_post, gt_m, g_ffn_pre, sc_f, sh_f), carry=carry)


FF_CHUNKS = N_DEV // 2
FF_CHUNK_W = D_FF // FF_CHUNKS


def _ffn_fwd(h2, w_up3, ffn_cw, ffn_cb, carry=None):
    s_len = h2.shape[0]
    tt = min(TT_BIG, s_len)
    nc, cw = FF_CHUNKS, FF_CHUNK_W

    def body(h2_ref, wu_ref, cwg_ref, cwv_ref, cbg_ref, cbv_ref, up_ref, upc_ref, act_ref, prev):
        i = pl.program_id(0)
        c = pl.program_id(1)

        @pl.when(i == 0)
        def _():
            prev[c] = jnp.zeros((2, SUBLANES, cw), F32)

        h2 = h2_ref[...]
        ug_pre = jnp.dot(h2, wu_ref[c], preferred_element_type=F32)
        uv_pre = jnp.dot(h2, wu_ref[nc + c], preferred_element_type=F32)
        up_ref[0] = ug_pre.astype(BF16)
        up_ref[1] = uv_pre.astype(BF16)
        ug, _ = _ffn_conv(ug_pre, prev[c, 0], cwg_ref, cbg_ref[...])
        uv, _ = _ffn_conv(uv_pre, prev[c, 1], cwv_ref, cbv_ref[...])
        prev[c, 0] = ug_pre[tt - SUBLANES:, :]
        prev[c, 1] = uv_pre[tt - SUBLANES:, :]
        upc_ref[0] = ug
        upc_ref[1] = uv
        act_ref[...] = (_gelu(ug) * uv).astype(BF16)

    chunk2 = pl.BlockSpec((2, tt, cw), lambda i, c: (0, i, c))
    ffn_cb2 = ffn_cb.reshape(1, 2 * D_FF)
    return _call(
        body, "ffn_fwd", (s_len // tt, nc),
        in_specs=[pl.BlockSpec((tt, D_MODEL), lambda i, c: (i, 0)), _whole(),
                  pl.BlockSpec((FFN_CONV_K, cw), lambda i, c: (0, c)),
                  pl.BlockSpec((FFN_CONV_K, cw), lambda i, c: (0, c + nc)),
                  pl.BlockSpec((1, cw), lambda i, c: (0, c)),
                  pl.BlockSpec((1, cw), lambda i, c: (0, c + nc))],
        out_specs=[chunk2, chunk2, pl.BlockSpec((tt, cw), lambda i, c: (i, c))],
        out_shape=[_sds((2, s_len, D_FF), BF16), _sds((2, s_len, D_FF), F32), _sds((s_len, D_FF), BF16)],
        scratch=[pltpu.VMEM((nc, 2, SUBLANES, cw), F32)],
        args=(h2, w_up3, ffn_cw, ffn_cw, ffn_cb2, ffn_cb2), carry=carry)


def _ffn_tail(act, w_down, x1, gt_f, g_post, target):
    s_len = x1.shape[0]
    tt = min(TT_BIG, s_len)

    def body(act_ref, wd_ref, x1_ref, gtf_ref, gp_ref, tg_ref, dy2_ref, dout_ref, loss_ref, vs_ref):
        @pl.when(pl.program_id(0) == 0)
        def _():
            loss_ref[...] = jnp.zeros_like(loss_ref)
            vs_ref[...] = jnp.zeros_like(vs_ref)

        for rows in _row_pieces(tt):
            n2, r2 = _rms(jnp.dot(act_ref[rows, :], wd_ref[...], preferred_element_type=F32))
            out = x1_ref[rows, :] + gtf_ref[...] * (n2 * gp_ref[...])
            err = out - tg_ref[rows, :]
            do = err * (1.0 / D_MODEL)
            dout_ref[rows, :] = do
            loss_ref[...] += jnp.broadcast_to(0.5 * jnp.sum(err * err, keepdims=True) * (1.0 / D_MODEL),
                                              loss_ref.shape)
            vs_ref[0:1, :] += _colsum(do * n2 * gp_ref[...])
            vs_ref[1:2, :] += _colsum(do * gtf_ref[...] * n2)
            dy2_ref[rows, :] = _rms_bwd(do * gtf_ref[...] * gp_ref[...], n2, r2).astype(BF16)

    row = lambda c: pl.BlockSpec((tt, c), lambda i: (i, 0))
    vec = _const((1, D_MODEL))
    outs, _ = _call(
        body, "ffn_tail", (s_len // tt,),
        in_specs=[row(D_FF), _whole(), row(D_MODEL), vec, vec, row(D_MODEL)],
        out_specs=[row(D_MODEL), row(D_MODEL), _const((SUBLANES, LANES)), _const((SUBLANES, D_MODEL))],
        out_shape=[_sds((s_len, D_MODEL), BF16), _sds((s_len, D_MODEL), F32), _sds((SUBLANES, LANES), F32),
                   _sds((SUBLANES, D_MODEL), F32)],
        scratch=[], args=(act, w_down, x1, gt_f, g_post, target))
    return outs


def _ffn_conv(up_pre, prev8, cw_ref, cb):
    up = cb + cw_ref[FFN_CONV_K - 1:FFN_CONV_K, :] * up_pre
    taps = []
    for k in range(FFN_CONV_K - 1):
        tap = _shift_down(up_pre, prev8, FFN_CONV_K - 1 - k)
        taps.append(tap)
        up = up + cw_ref[k:k + 1, :] * tap
    return up, taps


def _ffn_bwd(d_y2, up_pre, up, ffn_cw, w_down, carry=None):
    s_len = d_y2.shape[0]
    tt = min(TT_BIG, s_len)
    nt = s_len // tt
    cw = FF_CW
    nc = D_FF // cw

    def body(dy2_ref, up_ref, upc_ref, cwg_ref, cwv_ref, wd_ref, dup_ref, cs_ref, nxt, cs_acc):
        i = pl.program_id(0)
        c = pl.program_id(1)

        @pl.when(i == 0)
        def _():
            nxt[c] = jnp.zeros((2, SUBLANES, cw), F32)
            cs_acc[c] = jnp.zeros((2, SUBLANES, cw), F32)

        pw = 2 * LANES
        for piece in range(cw // pw):
            cols = slice(piece * pw, (piece + 1) * pw)
            d_act = _dot_nt(dy2_ref[...], wd_ref[pl.ds(pl.multiple_of(c * cw + piece * pw, pw), pw), :])
            uv = upc_ref[1, :, cols]
            gl, dgl = _gelu_and_grad(upc_ref[0, :, cols])
            d_ug = d_act * uv * dgl
            d_uv = d_act * gl
            for half, (d_u, cw_ref) in enumerate(((d_ug, cwg_ref), (d_uv, cwv_ref))):
                nx = nxt[c, half, :, cols]
                x_in = up_ref[half, :, cols].astype(F32)
                d_pre = cw_ref[FFN_CONV_K - 1:FFN_CONV_K, cols] * d_u
                sums = [None] * (FFN_CONV_K + 1)
                sums[FFN_CONV_K - 1] = _colsum(d_u * x_in)
                for k in range(FFN_CONV_K - 1):
                    ahead = _shift_up(d_u, nx, FFN_CONV_K - 1 - k)
                    d_pre = d_pre + cw_ref[k:k + 1, cols] * ahead
                    sums[k] = _colsum(ahead * x_in)
                sums[FFN_CONV_K] = _colsum(d_u)
                pad = jnp.zeros((SUBLANES - FFN_CONV_K - 1, pw), F32)
                cs_acc[c, half, :, cols] += jnp.concatenate(sums + [pad], axis=0)
                nxt[c, half, :, cols] = d_u[0:SUBLANES]
                dup_ref[half, :, cols] = d_pre.astype(BF16)

        for cc in range(nc):
            @pl.when((i == nt - 1) & (c == cc))
            def _():
                cs_ref[:, cc * cw:(cc + 1) * cw] = cs_acc[cc, 0]
                cs_ref[:, D_FF + cc * cw:D_FF + (cc + 1) * cw] = cs_acc[cc, 1]

    row = pl.BlockSpec((tt, D_MODEL), lambda i, c: (nt - 1 - i, 0))
    blk = pl.BlockSpec((2, tt, cw), lambda i, c: (0, nt - 1 - i, c))
    return _call(
        body, "ffn_bwd", (nt, nc),
        in_specs=[row, blk, blk,
                  pl.BlockSpec((FFN_CONV_K, cw), lambda i, c: (0, c)),
                  pl.BlockSpec((FFN_CONV_K, cw), lambda i, c: (0, c + nc)),
                  _whole()],
        out_specs=[blk, _const((SUBLANES, 2 * D_FF))],
        out_shape=[_sds((2, s_len, D_FF), BF16), _sds((SUBLANES, 2 * D_FF), F32)],
        scratch=[pltpu.VMEM((nc, 2, SUBLANES, cw), F32), pltpu.VMEM((nc, 2, SUBLANES, cw), F32)],
        args=(d_y2, up_pre, up, ffn_cw, ffn_cw, w_down), carry=carry)


def _up_bwd(d_up, w_up3, x1, dout, y, w_out, g_pre, sc_f, g_post, gt_m, carry=None):
    s_len = x1.shape[0]
    tt = min(TT_BIG, s_len)

    def body(du_ref, wu_ref, x1_ref, do_ref, y_ref, wo_ref, g2_ref, sc_ref, gp_ref, gt_ref,
             dx1_ref, dy_ref, dyc_ref, vs_ref):
        @pl.when(pl.program_id(0) == 0)
        def _():
            vs_ref[...] = jnp.zeros_like(vs_ref)

        for rows in _row_pieces(tt):
            d_h2 = jnp.zeros((rows.stop - rows.start, D_MODEL), F32)
            for half in range(2):
                for ch in range(FF_CHUNKS):
                    d_h2 = d_h2 + _dot_nt(du_ref[half, rows, ch * FF_CHUNK_W:(ch + 1) * FF_CHUNK_W],
                                          wu_ref[half * FF_CHUNKS + ch])
            n1, r1 = _rms(x1_ref[rows, :])
            ng = n1 * g2_ref[...]
            vs_ref[0:1, :] += _colsum(d_h2)
            vs_ref[1:2, :] += _colsum(d_h2 * ng)
            d_ng = d_h2 * (1.0 + sc_ref[...])
            vs_ref[2:3, :] += _colsum(d_ng * n1)
            d_x1 = do_ref[rows, :] + _rms_bwd(d_ng * g2_ref[...], n1, r1)
            dx1_ref[rows, :] = d_x1
            n_y, r_y = _rms(y_ref[rows, :])
            vs_ref[3:4, :] += _colsum(d_x1 * n_y * gp_ref[...])
            d_on = d_x1 * gt_ref[...]
            vs_ref[4:5, :] += _colsum(d_on * n_y)
            d_y = _rms_bwd(d_on * gp_ref[...], n_y, r_y).astype(BF16)
            dy_ref[rows, :] = d_y
            dyc_ref[rows, :] = _dot_nt(d_y, wo_ref[...])

    row = lambda c: pl.BlockSpec((tt, c), lambda i: (i, 0))
    vec = _const((1, D_MODEL))
    return _call(
        body, "up_bwd", (s_len // tt,),
        in_specs=[pl.BlockSpec((2, tt, D_FF), lambda i: (0, i, 0)), _whole(), row(D_MODEL), row(D_MODEL), row(D_MODEL),
                  _whole(), vec, vec, vec, vec],
        out_specs=[row(D_MODEL), row(D_MODEL), row(LRU_W + GMLP_W), _const((SUBLANES, D_MODEL))],
        out_shape=[_sds((s_len, D_MODEL), F32), _sds((s_len, D_MODEL), BF16), _sds((s_len, LRU_W + GMLP_W), F32),
                   _sds((SUBLANES, D_MODEL), F32)],
        scratch=[], args=(d_up, w_up3, x1, dout, y, w_out, g_pre, sc_f, g_post, gt_m), carry=carry)


def _head_pair_block(hd):
    return (slice((hd // 2) * HEAD_DIM, (hd // 2 + 1) * HEAD_DIM), slice((hd % 2) * HEAD_DIM, (hd % 2 + 1) * HEAD_DIM))


def _mix_bwd(d_ycat, z, hl, conv_w, conv_b, wr_bd, wi_bd, b_r, b_i, lru_a, vn_g, vn_b, w_sp, w_sp_t, b_sp_t,
             g_lru, g_gmlp, carry=None):
    s_len = z.shape[0]
    tt = min(TT_MIX, s_len)
    nt = s_len // tt
    nblk = tt // POS_BLOCK
    hb = tt // SUBLANES

    def body(dyc_ref, z_ref, zh_ref, hl_ref, hh_ref, cw_ref, cb_ref, wr_ref, wi_ref, br_ref, bi_ref, la_ref,
             vg_ref, vb_ref, ws_ref, wst_ref, bst_ref, gl_ref, gg_ref,
             dz_ref, vs_ref, dcw_ref, dwrb_ref, dwib_ref, dws_ref, dbs_ref, nxt_dxc, nxt_a, nxt_lam, dwr_ref, dwi_ref):
        i = pl.program_id(0)
        first_tile = i == nt - 1

        @pl.when(i == 0)
        def _():
            for ref in (vs_ref, dcw_ref, dwr_ref, dwi_ref, dws_ref, dbs_ref, nxt_dxc, nxt_a, nxt_lam):
                ref[...] = jnp.zeros_like(ref)

        lx = z_ref[:, 0:LRU_W]
        gate = z_ref[:, LRU_W:2 * LRU_W]
        gu = z_ref[:, 2 * LRU_W:2 * LRU_W + GMLP_W]
        gv = z_ref[:, 2 * LRU_W + GMLP_W:]
        prev8 = jnp.where(first_tile, 0.0, zh_ref[...])
        hprev8 = jnp.where(first_tile, 0.0, hh_ref[...])

        xc, taps = _lru_conv(lx, prev8, cw_ref, cb_ref[...])
        a_par = la_ref[...]
        sp_a = _softplus(-a_par)
        r, ig, a, mult = _lru_gates(xc, wr_ref, wi_ref, br_ref[...], bi_ref[...], sp_a)
        hl = hl_ref[...]
        h_prev = _shift_down(hl, hprev8, 1)
        ggate, dggate = _gelu_and_grad(gate)
        y_lru = hl * ggate
        n_l, r_l = _rms(y_lru)
        d_nl = dyc_ref[:, 0:LRU_W]
        vs_ref[6:7, :] += _colsum(d_nl * n_l)
        d_yl = _rms_bwd(d_nl * gl_ref[...], n_l, r_l)
        d_hl = d_yl * ggate
        d_gate = d_yl * hl * dggate
        a_up = _shift_up(a, nxt_a[...], 1)
        lam = _scan_rev(a_up, d_hl, nxt_lam[0:1, :])
        nxt_a[...] = jnp.broadcast_to(a[0:1, :], nxt_a.shape)
        nxt_lam[...] = jnp.broadcast_to(lam[0:1, :], nxt_lam.shape)
        ixc = ig * xc
        d_la = lam * h_prev * a - lam * ixc * (a * a) / mult
        d_i = lam * mult * xc
        d_xc = lam * mult * ig
        vs_ref[3:4, :] += _colsum(d_la * r) * (LRU_C * _sigmoid(-a_par))
        d_pr = d_la * (-LRU_C * sp_a) * r * (1.0 - r)
        d_pi = d_i * ig * (1.0 - ig)
        vs_ref[1:2, :] += _colsum(d_pr)
        vs_ref[2:3, :] += _colsum(d_pi)
        dwr_ref[...] += _dot_tn(xc, d_pr)
        dwi_ref[...] += _dot_tn(xc, d_pi)
        d_xc = d_xc + _dot_nt(d_pr, wr_ref[...]) + _dot_nt(d_pi, wi_ref[...])
        vs_ref[0:1, :] += _colsum(d_xc)
        nx = nxt_dxc[...]
        d_lx = cw_ref[LRU_CONV_K - 1:LRU_CONV_K, :] * d_xc
        dcw_ref[LRU_CONV_K - 1:LRU_CONV_K, :] += _colsum(d_xc * lx)
        for k in range(LRU_CONV_K - 1):
            d_lx = d_lx + cw_ref[k:k + 1, :] * _shift_up(d_xc, nx, LRU_CONV_K - 1 - k)
            dcw_ref[k:k + 1, :] += _colsum(d_xc * taps[k])
        nxt_dxc[...] = d_xc[0:SUBLANES]
        dz_ref[:, 0:LRU_W] = d_lx.astype(BF16)
        dz_ref[:, LRU_W:2 * LRU_W] = d_gate.astype(BF16)

        u, du = _gelu_and_grad(gu)
        v, vhat, rs, dav = _gmlp_v(gv, vg_ref[...], vb_ref[...])
        mask = _ws_mask()
        sp_parts = []
        for nb in range(nblk):
            rowp = []
            for g in range(N_GROUPS):
                wsm = jnp.where(mask, ws_ref[g], 0.0)
                vblk = v[nb * POS_BLOCK:(nb + 1) * POS_BLOCK, g * LANES:(g + 1) * LANES]
                rowp.append(_dot(wsm, vblk) + bst_ref[:, g:g + 1])
            sp_parts.append(jnp.concatenate(rowp, axis=1))
        sp = jnp.concatenate(sp_parts, axis=0) if nblk > 1 else sp_parts[0]
        y_g = u * sp
        n_g, r_g = _rms(y_g)
        d_ng = dyc_ref[:, LRU_W:]
        vs_ref[7:8, :] += _colsum(d_ng * n_g)
        d_yg = _rms_bwd(d_ng * gg_ref[...], n_g, r_g)
        d_gu = d_yg * sp * du
        d_sp = d_yg * u
        mask_t = _ws_mask(transposed=True)
        ones8 = jnp.ones((SUBLANES, LANES), F32)
        dv_parts = []
        for nb in range(nblk):
            rowp = []
            for g in range(N_GROUPS):
                rs_, cs_ = slice(nb * POS_BLOCK, (nb + 1) * POS_BLOCK), slice(g * LANES, (g + 1) * LANES)
                dsp_blk = d_sp[rs_, cs_]
                dbs_ref[g:g + 1, :] += lax.dot_general(
                    ones8, dsp_blk, (((1,), (1,)), ((), ())), preferred_element_type=F32,
                    precision=lax.Precision.HIGHEST)[0:1, :]
                dws_ref[g] += _dot_nt(dsp_blk, v[rs_, cs_])
                wsm_t = jnp.where(mask_t, wst_ref[g], 0.0)
                rowp.append(_dot(wsm_t, dsp_blk))
            dv_parts.append(jnp.concatenate(rowp, axis=1))
        d_v = jnp.concatenate(dv_parts, axis=0) if nblk > 1 else dv_parts[0]
        vs_ref[4:5, :] += _colsum(d_v * vhat)
        vs_ref[5:6, :] += _colsum(d_v)
        d_vh = d_v * vg_ref[...]
        d_av = rs * (d_vh - jnp.mean(d_vh, axis=-1, keepdims=True)
                     - vhat * jnp.mean(d_vh * vhat, axis=-1, keepdims=True))
        dz_ref[:, 2 * LRU_W:2 * LRU_W + GMLP_W] = d_gu.astype(BF16)
        dz_ref[:, 2 * LRU_W + GMLP_W:] = (d_av * dav).astype(BF16)

        @pl.when(i == nt - 1)
        def _():
            for hd in range(N_HEADS):
                blk = slice(hd * HEAD_DIM, (hd + 1) * HEAD_DIM)
                dwrb_ref[_head_pair_block(hd)] = dwr_ref[blk, blk]
                dwib_ref[_head_pair_block(hd)] = dwi_ref[blk, blk]
            for g in range(N_GROUPS):
                dws_ref[g] = jnp.where(mask, dws_ref[g], 0.0)

    rev = lambda c: pl.BlockSpec((tt, c), lambda i: (nt - 1 - i, 0))
    halo = pl.BlockSpec((SUBLANES, LRU_W), lambda i: (jnp.maximum((nt - 1 - i) * hb - 1, 0), 0))
    v512 = _const((1, LRU_W))
    return _call(
        body, "mix_bwd", (nt,),
        in_specs=[rev(LRU_W + GMLP_W), rev(IN_COLS), halo, rev(LRU_W), halo,
                  _const((LRU_CONV_K, LRU_W)), v512, _whole(), _whole(), v512, v512, v512, v512, v512,
                  _whole(), _whole(), _whole(), v512, v512],
        out_specs=[rev(IN_COLS), _const((SUBLANES, LRU_W)), _const((SUBLANES, LRU_W)),
                   _const((LRU_W // 2, 2 * HEAD_DIM)), _const((LRU_W // 2, 2 * HEAD_DIM)),
                   _const((N_GROUPS, POS_BLOCK, POS_BLOCK)), _const((SUBLANES, POS_BLOCK))],
        out_shape=[_sds((s_len, IN_COLS), BF16), _sds((SUBLANES, LRU_W), F32), _sds((SUBLANES, LRU_W), F32),
                   _sds((LRU_W // 2, 2 * HEAD_DIM), F32), _sds((LRU_W // 2, 2 * HEAD_DIM), F32),
                   _sds((N_GROUPS, POS_BLOCK, POS_BLOCK), F32), _sds((SUBLANES, POS_BLOCK), F32)],
        scratch=[pltpu.VMEM((SUBLANES, LRU_W), F32), pltpu.VMEM((SUBLANES, LRU_W), F32),
                 pltpu.VMEM((SUBLANES, LRU_W), F32), pltpu.VMEM((LRU_W, LRU_W), F32), pltpu.VMEM((LRU_W, LRU_W), F32)],
        args=(d_ycat, z, z, hl, hl, conv_w, conv_b, wr_bd, wi_bd, b_r, b_i, lru_a, vn_g, vn_b, w_sp, w_sp_t, b_sp_t,
              g_lru, g_gmlp), carry=carry)


def _in_bwd(d_z, w_in, x, d_x1, g, sc, carry=None):
    s_len = x.shape[0]
    tt = min(TT_BIG, s_len)

    def body(dz_ref, w_ref, x_ref, dx1_ref, g_ref, sc_ref, gx_ref, vs_ref):
        @pl.when(pl.program_id(0) == 0)
        def _():
            vs_ref[...] = jnp.zeros_like(vs_ref)

        for rows in _row_pieces(tt):
            d_h = _dot_nt(dz_ref[rows, :], w_ref[...])
            n, r = _rms(x_ref[rows, :])
            vs_ref[0:1, :] += _colsum(d_h)
            vs_ref[1:2, :] += _colsum(d_h * n * g_ref[...])
            d_ng = d_h * (1.0 + sc_ref[...])
            vs_ref[2:3, :] += _colsum(d_ng * n)
            gx_ref[rows, :] = dx1_ref[rows, :] + _rms_bwd(d_ng * g_ref[...], n, r)

    row = lambda c: pl.BlockSpec((tt, c), lambda i: (i, 0))
    vec = _const((1, D_MODEL))
    return _call(
        body, "in_bwd", (s_len // tt,),
        in_specs=[row(IN_COLS), _whole(), row(D_MODEL), row(D_MODEL), vec, vec],
        out_specs=[row(D_MODEL), _const((SUBLANES, D_MODEL))],
        out_shape=[_sds((s_len, D_MODEL), F32), _sds((SUBLANES, D_MODEL), F32)],
        scratch=[], args=(d_z, w_in, x, d_x1, g, sc), carry=carry)


def _wgrad(a, b, name, by_rows=False, carry=None):
    s_len, k_dim = a.shape
    halves = b.ndim == 3
    n_dim = b.shape[-1] * (2 if halves else 1)

    def body(a_ref, b_ref, ob_ref, own_ref):
        out = _dot_tn(a_ref[...], b_ref[0] if halves else b_ref[...])
        ob_ref[...] = out.astype(BF16)

        @pl.when(pl.program_id(0) == _dev_index(_my_pos()))
        def _():
            own_ref[...] = out

    if by_rows:
        tile = k_dim // N_DEV
        a_spec = pl.BlockSpec((s_len, tile), lambda j: (0, j))
        b_spec = pl.BlockSpec((s_len, n_dim), lambda j: (0, 0))
        o_spec = pl.BlockSpec((tile, n_dim), lambda j: (j, 0))
        own_shape = (tile, n_dim)
    else:
        tile = n_dim // N_DEV
        a_spec = pl.BlockSpec((s_len, k_dim), lambda j: (0, 0))
        if halves:
            per_half = N_DEV // 2
            b_spec = pl.BlockSpec((1, s_len, tile), lambda j: (j // per_half, 0, j % per_half))
        else:
            b_spec = pl.BlockSpec((s_len, tile), lambda j: (0, j))
        o_spec = pl.BlockSpec((k_dim, tile), lambda j: (0, j))
        own_shape = (k_dim, tile)
    return _call(
        body, name, (N_DEV,), in_specs=[a_spec, b_spec], out_specs=[o_spec, _const(own_shape)],
        out_shape=[_sds((k_dim, n_dim), BF16), _sds(own_shape, F32)],
        scratch=[], args=(a, b), carry=carry)


def _adam_math(w, g, m, v):
    m = ADAM_B1 * m + (1.0 - ADAM_B1) * g
    v = ADAM_B2 * v + (1.0 - ADAM_B2) * (g * g)
    m_hat = m / (1.0 - ADAM_B1 ** ADAM_STEP)
    v_hat = v / (1.0 - ADAM_B2 ** ADAM_STEP)
    delta = -ADAM_LR * (m_hat / (jnp.sqrt(v_hat) + ADAM_EPS) + ADAM_WD * w)
    return delta, m, v


def _row_tile(rows, cols, n_f32_arrays):
    budget = VMEM_LIMIT // 2
    tr = rows
    while tr % 2 == 0 and tr // 2 >= SUBLANES and (tr // 2) % SUBLANES == 0 and tr * cols * 4 * n_f32_arrays * 2 > budget:
        tr //= 2
    return tr


def _adamw_sum_block(w_ref, g_ref, r_refs, m_ref, v_ref, go_ref, d_ref, mo_ref, vo_ref):
    g = g_ref[...]
    for r_ref in r_refs:
        for k in range(r_ref.shape[0]):
            g = g + r_ref[k].astype(F32)
    go_ref[0] = g
    d_ref[0], mo_ref[0], vo_ref[0] = _adam_math(w_ref[0], g, m_ref[0], v_ref[0])


def _adamw_rider(parts, steps):
    inputs, in_specs, out_shape, out_specs, n_recvs = [], [], [], [], []
    for w, g_own, recv, m, v in parts:
        _, rows, cols = w.shape
        tr = rows // steps
        blk = pl.BlockSpec((1, tr, cols), lambda i: (0, i, 0))
        inputs += [w, g_own, *recv, m, v]
        in_specs += ([blk, pl.BlockSpec((tr, cols), lambda i: (i, 0))]
                     + [pl.BlockSpec((r.shape[0], tr, cols), lambda i: (0, i, 0)) for r in recv] + [blk, blk])
        out_shape += [_sds((1, rows, cols), F32)] * 4
        out_specs += [blk] * 4
        n_recvs.append(len(recv))

    def each(ins, outs, scr):
        for n_recv in n_recvs:
            _adamw_sum_block(ins[0], ins[1], ins[2:2 + n_recv], ins[2 + n_recv], ins[3 + n_recv], *outs[:4])
            ins, outs = ins[4 + n_recv:], outs[4:]

    return _Carry(inputs=inputs, in_specs=in_specs, out_shape=out_shape, out_specs=out_specs, scratch=[], each=each)


def _adamw_sum(w, g_own, recv, m, v, name):
    _, rows, cols = w.shape
    n_recv = len(recv)
    tr = _row_tile(rows, cols, 10)
    nb = rows // tr

    def body(w_ref, g_ref, *rest):
        _adamw_sum_block(w_ref, g_ref, rest[:n_recv], *rest[n_recv:])

    blk = pl.BlockSpec((1, tr, cols), lambda i: (0, i, 0))
    return pl.pallas_call(
        body, name=name, grid=(nb,),
        in_specs=[blk, pl.BlockSpec((tr, cols), lambda i: (i, 0))]
        + [pl.BlockSpec((r.shape[0], tr, cols), lambda i: (0, i, 0)) for r in recv] + [blk, blk],
        out_specs=[blk] * 4, out_shape=[_sds((1, rows, cols), F32)] * 4,
        compiler_params=_cparams(("arbitrary",)),
    )(w, g_own, *recv, m, v)


def _row_of_each(ref, row):
    cols = ref.shape[1]
    rows = _rows((N_DEV, cols))
    out = jnp.zeros((N_DEV, cols), F32)
    for d in range(N_DEV):
        picked = ref[d * SUBLANES + row:d * SUBLANES + row + 1, :]
        out = jnp.where(rows == d, jnp.broadcast_to(picked, (N_DEV, cols)), out)
    return out


def _my_columns(full, width, me):
    out = jnp.zeros(full.shape[:-1] + (width,), F32)
    for d in range(N_DEV):
        out = out + jnp.where(me == d, full[:, d * width:(d + 1) * width], 0.0)
    return out


def _adamw_wada(c_all, vs_in_all, vs_up_all, vs_ffn_all, w, m, v):
    _, rows, cols = w.shape

    def body(c_ref, vi_ref, vu_ref, vf_ref, w_ref, m_ref, v_ref, go_ref, d_ref, mo_ref, vo_ref):
        me = _dev_index(_my_pos())
        cv = _row_of_each(c_ref, 0)
        ca = cv * _sigmoid(cv)
        dmod = jnp.concatenate([_row_of_each(vi_ref, 0), _row_of_each(vi_ref, 1), _row_of_each(vu_ref, 3),
                                _row_of_each(vu_ref, 0), _row_of_each(vu_ref, 1), _row_of_each(vf_ref, 0)], axis=1)
        dm = _my_columns(dmod, cols, me)
        g = lax.dot_general(ca, dm, (((0,), (0,)), ((), ())), preferred_element_type=F32,
                            precision=lax.Precision.HIGHEST)
        go_ref[0] = g
        d_ref[0], mo_ref[0], vo_ref[0] = _adam_math(w_ref[0], g, m_ref[0], v_ref[0])

    return pl.pallas_call(
        body, name="adamw_w_ada", out_shape=[_sds((1, rows, cols), F32)] * 4,
        in_specs=[_whole()] * 7, out_specs=[_whole()] * 4,
        compiler_params=_cparams(),
    )(c_all, vs_in_all, vs_up_all, vs_ffn_all, w, m, v)


def _adamw_small(gathered, reduced, params, conv_params):
    names = list(params) + list(conv_params)
    allp = {**params, **conv_params}
    n_g = len(gathered) + len(reduced)

    def body(*refs):
        g_refs = refs[:n_g]
        p_refs = refs[n_g:n_g + 3 * len(names)]
        o_refs = refs[n_g + 3 * len(names):]
        me = _dev_index(_my_pos())

        def total(ref):
            s = ref[0:SUBLANES, :]
            for d in range(1, N_DEV):
                s = s + ref[d * SUBLANES:(d + 1) * SUBLANES, :]
            return s

        vs_in, vs_up, vs_ffn, loss = [total(r) for r in g_refs[:4]]
        cs, vs_mix, dcw, dwr, dwi, dws, dbs = [r[...] for r in g_refs[4:]]
        o_refs[-1][...] = loss[0:1, 0:1]
        mine = lambda full, width: _my_columns(full, width, me)

        all_ = (slice(None), slice(None))
        heads = lambda row: [((0, slice(h, h + 1), slice(None)), row[:, h * HEAD_DIM:(h + 1) * HEAD_DIM])
                             for h in range(N_HEADS)]
        blocks = lambda pairs: [((0, h), pairs[_head_pair_block(h)]) for h in range(N_HEADS)]
        pieces = {
            "b_ada": [((slice(None), slice(k * D_MODEL, (k + 1) * D_MODEL)), row) for k, row in enumerate(
                (vs_in[0:1], vs_in[1:2], vs_up[3:4], vs_up[0:1], vs_up[1:2], vs_ffn[0:1]))],
            "g_mix_pre": [(all_, vs_in[2:3])], "g_mix_post": [(all_, vs_up[4:5])],
            "g_ffn_pre": [(all_, vs_up[2:3])], "g_ffn_post": [(all_, vs_ffn[1:2])],
            "conv_b": [(all_, vs_mix[0:1])], "b_rgate": heads(vs_mix[1:2]), "b_igate": heads(vs_mix[2:3]),
            "lru_a": [(all_, vs_mix[3:4])], "v_norm_g": [(all_, vs_mix[4:5])], "v_norm_b": [(all_, vs_mix[5:6])],
            "g_lru_out": [(all_, vs_mix[6:7])], "g_gmlp_out": [(all_, vs_mix[7:8])],
            "w_rgate": blocks(dwr), "w_igate": blocks(dwi),
            "w_spatial": [((0, g), dws[g * POS_BLOCK:(g + 1) * POS_BLOCK, :]) for g in range(N_GROUPS)],
            "b_spatial": [((0,), dbs[0:N_GROUPS])],
            "ffn_conv_b": [(all_, cs[FFN_CONV_K:FFN_CONV_K + 1])],
            "conv_w": [((0,), mine(dcw[0:LRU_CONV_K], LRU_W // N_DEV))],
        }
        ffn_cw_rows = mine(cs[0:FFN_CONV_K], 2 * D_FF // N_DEV)
        pieces["ffn_conv_w"] = [((k,), ffn_cw_rows[k:k + 1]) for k in range(FFN_CONV_K)]
        for n_i, name in enumerate(names):
            w_ref, m_ref, v_ref = p_refs[3 * n_i:3 * n_i + 3]
            go_ref, d_ref, mo_ref, vo_ref = o_refs[4 * n_i:4 * n_i + 4]
            for idx, g in pieces[name]:
                go_ref[idx] = g
                d_ref[idx], mo_ref[idx], vo_ref[idx] = _adam_math(w_ref[idx], g, m_ref[idx], v_ref[idx])

    flat_params = [a for n in names for a in allp[n]]
    out_shape = [_sds(allp[n][0].shape, F32) for n in names for _ in range(4)] + [_sds((1, 1), F32)]
    outs = pl.pallas_call(
        body, name="adamw_small", out_shape=out_shape,
        in_specs=[_whole()] * (n_g + len(flat_params)), out_specs=[_whole()] * len(out_shape),
        compiler_params=_cparams(),
    )(*gathered, *reduced, *flat_params)
    return {n: outs[4 * i:4 * i + 4] for i, n in enumerate(names)}, outs[-1]


def _my_pos():
    return lax.axis_index("x"), lax.axis_index("y"), lax.axis_index("c")


def _flip(pos, k):
    x, y, c = pos
    return (1 - x if k & 4 else x, 1 - y if k & 2 else y, 1 - c if k & 1 else c)


def _dev_index(pos):
    x, y, c = pos
    return 4 * x + 2 * y + c


def _all_gather_small(ins, outs, send_sems, recv_sems, meanwhile=None):
    n = len(ins)
    me = _my_pos()

    def slot(a, pos):
        rows = ins[a].shape[0]
        return outs[a].at[pl.ds(pl.multiple_of(_dev_index(pos) * rows, SUBLANES), rows), :]

    def copy(a, k, block):
        return pltpu.make_async_remote_copy(
            src_ref=ins[a], dst_ref=slot(a, block), send_sem=send_sems.at[a, k - 1], recv_sem=recv_sems.at[a, k - 1],
            device_id=_flip(me, k), device_id_type=MESH)

    sends = [copy(a, k, me) for a in range(n) for k in range(1, N_DEV)]
    for cp in sends:
        cp.start()
    for a in range(n):
        rows = ins[a].shape[0]
        outs[a][pl.ds(pl.multiple_of(_dev_index(me) * rows, SUBLANES), rows), :] = ins[a][...]
    if meanwhile:
        meanwhile()
    for a in range(n):
        for k in range(1, N_DEV):
            copy(a, k, _flip(me, k)).wait_recv()
    for cp in sends:
        cp.wait_send()


def _prologue(c, cw, fcw, w_ada, b_ada, w_rgate, w_igate, b_rgate, b_igate, carry):
    cols = w_ada.shape[1]
    cw_w, fcw_w = cw.shape[-1], fcw.shape[-1]
    step = math.gcd(cols, D_MODEL)

    def body(c_ref, cw_ref, fcw_ref, w_ref, b_ref, wr_ref, wi_ref, br_ref, bi_ref,
             call_ref, cwf_ref, fcwf_ref, wrbd_ref, wibd_ref, brow_ref, birow_ref, *rest, start_carry):
        mod_refs, (mod_scr, c8, cw8, fcw8, cwall, fcwall, modall, s1, r1, s2, r2) = rest[:N_MOD], rest[N_MOD:]
        me = _dev_index(_my_pos())
        c8[...] = jnp.broadcast_to(c_ref[...], c8.shape)
        cw8[...] = jnp.zeros(cw8.shape, F32)
        cw8[0:LRU_CONV_K, :] = cw_ref[...]
        fcw8[...] = jnp.zeros(fcw8.shape, F32)
        for k in range(FFN_CONV_K):
            fcw8[k:k + 1, :] = fcw_ref[k]

        def block_diagonals():
            for bd_ref, hb_ref in ((wrbd_ref, wr_ref), (wibd_ref, wi_ref)):
                bd_ref[...] = jnp.zeros(bd_ref.shape, BF16)
                for h in range(N_HEADS):
                    span = slice(h * HEAD_DIM, (h + 1) * HEAD_DIM)
                    bd_ref[span, span] = hb_ref[h].astype(BF16)
            for row_ref, hb_ref in ((brow_ref, br_ref), (birow_ref, bi_ref)):
                for h in range(N_HEADS):
                    row_ref[:, h * HEAD_DIM:(h + 1) * HEAD_DIM] = hb_ref[h:h + 1, :]

        def conv_weights():
            for d in range(N_DEV):
                cwf_ref[:, d * cw_w:(d + 1) * cw_w] = cwall[d * SUBLANES:d * SUBLANES + LRU_CONV_K, :]
                fcwf_ref[:, d * fcw_w:(d + 1) * fcw_w] = fcwall[d * SUBLANES:d * SUBLANES + FFN_CONV_K, :]

        _all_gather_small([c8, cw8, fcw8], [call_ref, cwall, fcwall], s1, r1, meanwhile=block_diagonals)
        start_carry()
        cv = _row_of_each(call_ref, 0)
        ca = cv * _sigmoid(cv)
        b_cols = _my_columns(b_ref[...], cols, me)
        mod_scr[...] = jnp.dot(ca, w_ref[...], preferred_element_type=F32, precision=lax.Precision.HIGHEST) + b_cols
        _all_gather_small([mod_scr], [modall], s2, r2, meanwhile=conv_weights)
        mine = _rows((N_DEV, cols)) == me
        for d in range(N_DEV):
            piece = jnp.sum(jnp.where(mine, modall[d * N_DEV:(d + 1) * N_DEV, :], 0.0), axis=0, keepdims=True)
            for t in range(cols // step):
                at = d * cols + t * step
                mod_refs[at // D_MODEL][:, at % D_MODEL:at % D_MODEL + step] = piece[:, t * step:(t + 1) * step]

    sem = lambda n: pltpu.SemaphoreType.DMA((n, N_DEV - 1))
    gate = N_HEADS * HEAD_DIM
    return _call(
        body, "prologue", (1,), in_specs=[_whole()] * 9, out_specs=[_whole()] * (7 + N_MOD),
        out_shape=[_sds((N_DEV * SUBLANES, c.shape[-1]), F32), _sds((LRU_CONV_K, N_DEV * cw_w), F32),
                   _sds((FFN_CONV_K, N_DEV * fcw_w), F32), _sds((gate, gate), BF16), _sds((gate, gate), BF16),
                   _sds((1, gate), F32), _sds((1, gate), F32)] + [_sds((1, D_MODEL), F32)] * N_MOD,
        scratch=[pltpu.VMEM((N_DEV, cols), F32)] + [pltpu.VMEM((SUBLANES, a.shape[-1]), F32) for a in (c, cw, fcw)]
        + [pltpu.VMEM((N_DEV * SUBLANES, cw_w), F32), pltpu.VMEM((N_DEV * SUBLANES, fcw_w), F32),
           pltpu.VMEM((N_DEV * N_DEV, cols), F32), sem(3), sem(3), sem(1), sem(1)],
        args=(c, cw, fcw, w_ada, b_ada, w_rgate, w_igate, b_rgate, b_igate), carry=carry, body_starts_carry=True)


def _reduce_small(gath, red, carry=None):
    n_g, n_r = len(gath), len(red)
    chip_flips = CHIP_FLIPS

    def body(*refs, start_carry):
        g_in, r_in = refs[:n_g], refs[n_g:n_g + n_r]
        g_out, r_out = refs[n_g + n_r:2 * n_g + n_r], refs[2 * n_g + n_r:2 * (n_g + n_r)]
        scr = refs[2 * (n_g + n_r):]
        sib, land = scr[:n_r], scr[n_r:2 * n_r]
        g_send, g_recv, s_send, s_recv, i_send, i_recv, f_send, f_recv = scr[2 * n_r:]
        me = _my_pos()
        c = me[2]
        sibling = _flip(me, 1)

        def slot(a, pos):
            return g_out[a].at[pl.ds(pl.multiple_of(_dev_index(pos) * SUBLANES, SUBLANES), SUBLANES), :]

        def gcopy(a, k):
            return pltpu.make_async_remote_copy(
                src_ref=g_in[a], dst_ref=slot(a, me), send_sem=g_send.at[a, k - 1], recv_sem=g_recv.at[a, k - 1],
                device_id=_flip(me, k), device_id_type=MESH)

        def scopy(a):
            return pltpu.make_async_remote_copy(
                src_ref=r_in[a], dst_ref=sib[a], send_sem=s_send.at[a], recv_sem=s_recv.at[a],
                device_id=sibling, device_id_type=MESH)

        def icopy(a, j):
            return pltpu.make_async_remote_copy(
                src_ref=r_out[a], dst_ref=land[a].at[j], send_sem=i_send.at[a, j], recv_sem=i_recv.at[a, j],
                device_id=_flip(me, chip_flips[j]), device_id_type=MESH)

        def fcopy(a, j):
            return pltpu.make_async_remote_copy(
                src_ref=land[a].at[j], dst_ref=land[a].at[j], send_sem=f_send.at[a, j], recv_sem=f_recv.at[a, j],
                device_id=sibling, device_id_type=MESH)

        gathers = [gcopy(a, k) for a in range(n_g) for k in range(1, N_DEV)]
        swaps = [scopy(a) for a in range(n_r)]
        for cp in gathers + swaps:
            cp.start()
        for a in range(n_g):
            g_out[a][pl.ds(pl.multiple_of(_dev_index(me) * SUBLANES, SUBLANES), SUBLANES), :] = g_in[a][...]
        for a in range(n_r):
            swaps[a].wait_recv()
            r_out[a][...] = r_in[a][...] + sib[a][...]

        for core in range(2):
            @pl.when(c == core)
            def _():
                for a in range(core, n_r, 2):
                    for j in range(3):
                        icopy(a, j).start()

        start_carry()

        for core in range(2):
            mine = [a for a in range(n_r) if a % 2 == core]
            theirs = [a for a in range(n_r) if a % 2 != core]

            @pl.when(c == core)
            def _():
                out = [icopy(a, j) for a in mine for j in range(3)]
                fwd = []
                for a in mine:
                    for j in range(3):
                        icopy(a, j).wait_recv()
                        cp = fcopy(a, j)
                        cp.start()
                        fwd.append(cp)
                for a in theirs:
                    for j in range(3):
                        fcopy(a, j).wait_recv()
                for cp in out + fwd:
                    cp.wait_send()

        for a in range(n_r):
            r_out[a][...] = (r_out[a][...] + land[a][1]) + (land[a][0] + land[a][2])
        for a in range(n_g):
            for k in range(1, N_DEV):
                pltpu.make_async_remote_copy(
                    src_ref=g_in[a], dst_ref=slot(a, _flip(me, k)), send_sem=g_send.at[a, k - 1],
                    recv_sem=g_recv.at[a, k - 1], device_id=_flip(me, k), device_id_type=MESH).wait_recv()
        for cp in gathers + swaps:
            cp.wait_send()

    shapes = [tuple(a.shape) for a in red]
    outs, carried = _call(
        body, "reduce_small", (1,), in_specs=[_whole()] * (n_g + n_r), out_specs=[_whole()] * (n_g + n_r),
        out_shape=[_sds((N_DEV * SUBLANES, a.shape[1]), F32) for a in gath] + [_sds(s, F32) for s in shapes],
        scratch=[pltpu.VMEM(s, F32) for s in shapes] + [pltpu.VMEM((3,) + s, F32) for s in shapes]
        + [pltpu.SemaphoreType.DMA((n_g, N_DEV - 1)), pltpu.SemaphoreType.DMA((n_g, N_DEV - 1)),
           pltpu.SemaphoreType.DMA((n_r,)), pltpu.SemaphoreType.DMA((n_r,)),
           pltpu.SemaphoreType.DMA((n_r, 3)), pltpu.SemaphoreType.DMA((n_r, 3)),
           pltpu.SemaphoreType.DMA((n_r, 3)), pltpu.SemaphoreType.DMA((n_r, 3))],
        args=tuple(gath) + tuple(red), carry=carry, body_starts_carry=True)
    return (outs[:n_g], outs[n_g:]), carried


STACKED = "stacked"


def _region(ref, shard_shape, col_sharded, pos):
    r, cdim = shard_shape
    d = _dev_index(pos)
    if col_sharded == STACKED:
        return ref.at[d]
    if col_sharded:
        return ref.at[:, pl.ds(pl.multiple_of(d * cdim, LANES), cdim)]
    return ref.at[pl.ds(pl.multiple_of(d * r, 2 * SUBLANES), r), :]


def _gather_carry(shards, col_sharded):
    n_w = len(shards)
    shapes = [tuple(s.shape) for s in shards]
    full_shapes = [(N_DEV,) + s if cs == STACKED else (s[0], s[1] * N_DEV) if cs else (s[0] * N_DEV, s[1])
                   for s, cs in zip(shapes, col_sharded)]

    def tools(out_refs, scr):
        send_sems, recv_sems = scr[n_w], scr[n_w + 1]
        me = _my_pos()
        x, y, c = me
        sibling = (x, y, 1 - c)
        chips = [(1 - x, y), (x, 1 - y), (1 - x, 1 - y)]

        def region(w, pos):
            return _region(out_refs[w], shapes[w], col_sharded[w], pos)

        def copy(w, k, block, to, src=None):
            return pltpu.make_async_remote_copy(
                src_ref=region(w, block) if src is None else src, dst_ref=region(w, block),
                send_sem=send_sems.at[w, k], recv_sem=recv_sems.at[w, k], device_id=to, device_id_type=MESH)

        def first(w):
            return [copy(w, 0, me, sibling, src=scr[w])] + [
                copy(w, 1 + j, me, (*chip, c), src=scr[w]) for j, chip in enumerate(chips)]

        def mine(w):
            return pltpu.make_async_copy(scr[w], region(w, me), scr[n_w + 2].at[w])

        return me, c, sibling, chips, copy, first, mine

    def start(ins, outs, scr):
        _, _, _, _, _, first, mine = tools(outs, scr)
        for w in range(n_w):
            scr[w][...] = ins[w][...].astype(BF16)
            for cp in first(w) + [mine(w)]:
                cp.start()

    def finish(ins, outs, scr):
        me, c, sibling, chips, copy, first, mine = tools(outs, scr)
        passed = []
        for w in range(n_w):
            for j, chip in enumerate(chips):
                copy(w, 1 + j, (*chip, c), me).wait_recv()
                fwd = copy(w, 4 + j, (*chip, c), sibling)
                fwd.start()
                passed.append(fwd)
        for w in range(n_w):
            copy(w, 0, sibling, me).wait_recv()
            for j, chip in enumerate(chips):
                copy(w, 4 + j, (*chip, 1 - c), me).wait_recv()
        for w in range(n_w):
            for cp in first(w):
                cp.wait_send()
            mine(w).wait()
        for cp in passed:
            cp.wait_send()

    return _Carry(
        inputs=list(shards), in_specs=[_whole()] * n_w,
        out_shape=[_sds(s, BF16) for s in full_shapes], out_specs=[_any()] * n_w,
        scratch=[pltpu.VMEM(s, BF16) for s in shapes]
        + [pltpu.SemaphoreType.DMA((n_w, N_DEV - 1)), pltpu.SemaphoreType.DMA((n_w, N_DEV - 1)),
           pltpu.SemaphoreType.DMA((n_w,))],
        start=start, finish=finish)


CHIP_FLIPS = (4, 2, 6)


def _pair_reduce(g_bf, g_own, col_sharded):
    shape = tuple(g_own.shape)
    n = len(CHIP_FLIPS)

    def body(g_ref, own_ref, hown_ref, hout_ref, mine, sib, send_sems, recv_sems, local_sems):
        me = _my_pos()
        sibling = _flip(me, 1)
        flips = (0,) + CHIP_FLIPS

        def region(pos):
            return _region(g_ref, shape, col_sharded, pos)

        local = [pltpu.make_async_copy(region(_flip(me, f)), mine.at[s], local_sems.at[s])
                 for s, f in enumerate(CHIP_FLIPS)]
        sends = [pltpu.make_async_remote_copy(
            src_ref=region(_flip(sibling, f)), dst_ref=sib.at[s], send_sem=send_sems.at[s], recv_sem=recv_sems.at[s],
            device_id=sibling, device_id_type=MESH) for s, f in enumerate(flips)]
        for cp in local + sends:
            cp.start()
        for cp in local:
            cp.wait()
        for cp in sends:
            cp.wait_recv()
        hown_ref[...] = own_ref[...] + sib[0].astype(F32)
        for s in range(n):
            hout_ref[s] = (mine[s].astype(F32) + sib[s + 1].astype(F32)).astype(BF16)
        for cp in sends:
            cp.wait_send()

    return pl.pallas_call(
        body, name="pair_reduce", out_shape=[_sds(shape, F32), _sds((n,) + shape, BF16)],
        in_specs=[_any(), _whole()], out_specs=[_whole(), _whole()],
        scratch_shapes=[pltpu.VMEM((n,) + shape, BF16), pltpu.VMEM((n + 1,) + shape, BF16),
                        pltpu.SemaphoreType.DMA((n + 1,)), pltpu.SemaphoreType.DMA((n + 1,)),
                        pltpu.SemaphoreType.DMA((n,))],
        compiler_params=pltpu.CompilerParams(vmem_limit_bytes=VMEM_LIMIT),
    )(g_bf, g_own)


def _chip_scatter_carry(h_out):
    n = len(CHIP_FLIPS)

    def copies(ins, outs, scr):
        send_sems, recv_sems = scr
        me = _my_pos()
        return [pltpu.make_async_remote_copy(
            src_ref=ins[0].at[j], dst_ref=outs[0].at[j], send_sem=send_sems.at[j], recv_sem=recv_sems.at[j],
            device_id=_flip(me, CHIP_FLIPS[j]), device_id_type=MESH) for j in range(n)]

    def start(ins, outs, scr):
        for cp in copies(ins, outs, scr):
            cp.start()

    def finish(ins, outs, scr):
        cps = copies(ins, outs, scr)
        for cp in cps:
            cp.wait_recv()
        for cp in cps:
            cp.wait_send()

    return _Carry(inputs=[h_out], in_specs=[_any()], out_shape=[_sds(tuple(h_out.shape), BF16)], out_specs=[_any()],
                  scratch=[pltpu.SemaphoreType.DMA((n,)), pltpu.SemaphoreType.DMA((n,))], start=start, finish=finish)


def _scatter_carry(grads_bf, shard_shapes, col_sharded, relations):
    n_w = len(grads_bf)
    shapes = [tuple(s) for s in shard_shapes]

    def copies(ins, outs, scr):
        send_sems, recv_sems = scr
        me = _my_pos()
        out = []
        for w in range(n_w):
            for i, k in enumerate(relations[w]):
                peer = _flip(me, k)
                out.append(pltpu.make_async_remote_copy(
                    src_ref=_region(ins[w], shapes[w], col_sharded[w], peer), dst_ref=outs[w].at[i],
                    send_sem=send_sems.at[w, i], recv_sem=recv_sems.at[w, i],
                    device_id=peer, device_id_type=MESH))
        return out

    def start(ins, outs, scr):
        for cp in copies(ins, outs, scr):
            cp.start()

    def finish(ins, outs, scr):
        cps = copies(ins, outs, scr)
        for cp in cps:
            cp.wait_recv()
        for cp in cps:
            cp.wait_send()

    return _Carry(
        inputs=list(grads_bf), in_specs=[_any()] * n_w,
        out_shape=[_sds((len(r),) + s, BF16) for r, s in zip(relations, shapes)], out_specs=[_any()] * n_w,
        scratch=[pltpu.SemaphoreType.DMA((n_w, N_DEV - 1)), pltpu.SemaphoreType.DMA((n_w, N_DEV - 1))],
        start=start, finish=finish)


def _block_diag(w):
    eye = jnp.eye(N_HEADS, dtype=w.dtype)
    return (eye[:, None, :, None] * w[:, :, None, :]).reshape(N_HEADS * HEAD_DIM, N_HEADS * HEAD_DIM)


def _local_step(x2, target, mod, w_in_f, w_full, conv_w_full, ffn_cw_full,
                g_mix_pre, g_mix_post, conv_b, w_rgate, b_rgate, w_igate, b_igate, lru_a, v_norm_g, v_norm_b,
                w_spatial, b_spatial, g_lru_out, g_gmlp_out, g_ffn_pre, g_ffn_post, ffn_conv_b,
                gather=None, scatter=None, adam=None, gate_bd=None):
    sh_m, sc_m, gt_m, sh_f, sc_f, gt_f = [mod[k] for k in range(N_MOD)]
    if gate_bd:
        wr_bd, wi_bd, b_r, b_i = gate_bd
    else:
        wr_bd, wi_bd = [_block_diag(w[0]).astype(BF16) for w in (w_rgate, w_igate)]
        b_r, b_i = b_rgate.reshape(1, LRU_W), b_igate.reshape(1, LRU_W)
    b_sp_t = b_spatial[0].T
    w_sp_t = jnp.swapaxes(w_spatial[0], 1, 2)

    def arriving(*names):
        return gather(*names) if gather else None

    near, far = (1, 2, 3, 4, 5), (6, 7)

    def leaving(*parts):
        return scatter(parts) if scatter else None

    def received(recv, parts, outs):
        for (name, _, _), out in zip(parts, outs):
            recv.setdefault(name, []).append(out)

    mix_params = (conv_w_full, conv_b, wr_bd, wi_bd, b_r, b_i, lru_a, v_norm_g, v_norm_b)
    w_out_f = w_full["w_out"]
    (z, h, ycat, hl, y, x1, h2), got = _mix_fwd(
        x2, sh_m, sc_m, g_mix_pre, w_in_f, *mix_params, w_spatial[0], b_sp_t, g_lru_out, g_gmlp_out,
        w_out_f, g_mix_post, gt_m, g_ffn_pre, sc_f, sh_f, carry=arriving("w_up"))
    w_up_f = got[0] if gather else w_full["w_up"]
    (up_pre, up, act), got = _ffn_fwd(h2, w_up_f, ffn_cw_full, ffn_conv_b, carry=arriving("w_down"))
    w_down_f = got[0] if gather else w_full["w_down"]
    d_y2, dout, loss_acc, vs_ffn = _ffn_tail(act, w_down_f, x1, gt_f, g_ffn_post, target)

    recv, updated = {}, {}

    def updating(grads):
        if not adam:
            return None
        return _adamw_rider([(adam[n][0], g[1], recv[n], adam[n][1], adam[n][2]) for n, g in grads.items()], N_DEV)

    def updates(grads, outs):
        for j, n in enumerate(grads):
            updated[n] = tuple(outs[4 * j:4 * j + 4])

    gw_down, _ = _wgrad(act, d_y2, "wgrad_down", by_rows=True)
    parts = [("w_down", gw_down[0], near + far)]
    (d_up, cs_ffn), got = _ffn_bwd(d_y2, up_pre, up, ffn_cw_full, w_down_f, carry=leaving(*parts))
    received(recv, parts, got)
    gw_up, got = _wgrad(h2, d_up, "wgrad_up", carry=updating(dict(w_down=gw_down)))
    updates(dict(w_down=gw_down), got)
    parts = [("w_up", gw_up[0], near)]
    (d_x1, d_y, d_ycat, vs_up), got = _up_bwd(
        d_up, w_up_f, x1, dout, y, w_out_f, g_ffn_pre, sc_f, g_mix_post, gt_m, carry=leaving(*parts))
    received(recv, parts, got)
    gw_out, _ = _wgrad(ycat, d_y, "wgrad_out", by_rows=True)
    parts = [("w_up", gw_up[0], far), ("w_out", gw_out[0], near + far)]
    (d_z, vs_mix, dcw, d_wr, d_wi, d_ws, d_bs), got = _mix_bwd(
        d_ycat, z, hl, *mix_params, w_spatial[0], w_sp_t, b_sp_t, g_lru_out, g_gmlp_out, carry=leaving(*parts))
    received(recv, parts, got)
    gw_in, got = _wgrad(h, d_z, "wgrad_in", carry=updating(dict(w_up=gw_up, w_out=gw_out)))
    updates(dict(w_up=gw_up, w_out=gw_out), got)
    chip_sums = None
    if scatter:
        h_own, h_out = _pair_reduce(gw_in[0], gw_in[1], True)
        gw_in = (gw_in[0], h_own)
        chip_sums = _chip_scatter_carry(h_out)
    (grad_x, vs_in), got = _in_bwd(d_z, w_in_f, x2, d_x1, g_mix_pre, sc_m, carry=chip_sums)
    recv["w_in"] = list(got)

    gath = [vs_in, vs_up, vs_ffn, loss_acc]
    red = [cs_ffn, vs_mix, dcw, d_wr, d_wi, d_ws.reshape(N_GROUPS * POS_BLOCK, POS_BLOCK), d_bs]
    return dict(grad_x=grad_x, gath=gath, red=red, recv=recv, updated=updated,
                w_in=gw_in, w_out=gw_out, w_up=gw_up, w_down=gw_down)


def kernel(x, c, w_ada, b_ada, g_mix_pre, g_mix_post, w_in, conv_w, conv_b, w_rgate, b_rgate, w_igate, b_igate, lru_a, v_norm_g, v_norm_b, w_spatial, b_spatial, g_lru_out, g_gmlp_out, w_out, g_ffn_pre, g_ffn_post, w_up, ffn_conv_w, ffn_conv_b, w_down, loss_target, m_w_ada, m_b_ada, m_g_mix_pre, m_g_mix_post, m_w_in, m_conv_w, m_conv_b, m_w_rgate, m_b_rgate, m_w_igate, m_b_igate, m_lru_a, m_v_norm_g, m_v_norm_b, m_w_spatial, m_b_spatial, m_g_lru_out, m_g_gmlp_out, m_w_out, m_g_ffn_pre, m_g_ffn_post, m_w_up, m_ffn_conv_w, m_ffn_conv_b, m_w_down, v_w_ada, v_b_ada, v_g_mix_pre, v_g_mix_post, v_w_in, v_conv_w, v_conv_b, v_w_rgate, v_b_rgate, v_w_igate, v_b_igate, v_lru_a, v_v_norm_g, v_v_norm_b, v_w_spatial, v_b_spatial, v_g_lru_out, v_g_gmlp_out, v_w_out, v_g_ffn_pre, v_g_ffn_post, v_w_up, v_ffn_conv_w, v_ffn_conv_b, v_w_down):
    big_w = dict(w_in=(w_in, m_w_in, v_w_in, True), w_out=(w_out, m_w_out, v_w_out, False),
                 w_up=(w_up, m_w_up, v_w_up, True), w_down=(w_down, m_w_down, v_w_down, False))

    def gather(*names):
        return _gather_carry([big_w[n][0][0] for n in names], [STACKED if n == "w_up" else big_w[n][3] for n in names])

    def scatter(parts):
        return _scatter_carry([g for _, g, _ in parts], [big_w[n][0].shape[1:] for n, _, _ in parts],
                              [big_w[n][3] for n, _, _ in parts], [rel for _, _, rel in parts])

    ffn_cw_taps = tuple(a.reshape(FFN_CONV_K, 1, -1) for a in (ffn_conv_w, m_ffn_conv_w, v_ffn_conv_w))
    (c_all, conv_w_full, ffn_cw_full, *gate_bd, sh_m, sc_m, gt_m, sh_f, sc_f, gt_f), (w_in_f, w_out_f) = _prologue(
        c, conv_w[0], ffn_cw_taps[0], w_ada[0], b_ada, w_rgate[0], w_igate[0], b_rgate[0], b_igate[0],
        carry=gather("w_in", "w_out"))
    mod = (sh_m, sc_m, gt_m, sh_f, sc_f, gt_f)

    loc = _local_step(x[0], loss_target[0], mod, w_in_f, dict(w_out=w_out_f), conv_w_full, ffn_cw_full,
                      g_mix_pre, g_mix_post, conv_b, w_rgate, b_rgate, w_igate, b_igate, lru_a, v_norm_g, v_norm_b,
                      w_spatial, b_spatial, g_lru_out, g_gmlp_out, g_ffn_pre, g_ffn_post, ffn_conv_b,
                      gather=gather, scatter=scatter,
                      adam={n: big_w[n][:3] for n in ("w_out", "w_up", "w_down")}, gate_bd=gate_bd)
    grad_x = loc["grad_x"]

    (gathered, reduced), _ = _reduce_small(loc["gath"], loc["red"])

    results = dict(loc["updated"])
    w_, m_, v_, _ = big_w["w_in"]
    results["w_in"] = _adamw_sum(w_, loc["w_in"][1], loc["recv"]["w_in"], m_, v_, "adamw_w_in")

    params = dict(
        b_ada=(b_ada, m_b_ada, v_b_ada), g_mix_pre=(g_mix_pre, m_g_mix_pre, v_g_mix_pre),
        g_mix_post=(g_mix_post, m_g_mix_post, v_g_mix_post), conv_b=(conv_b, m_conv_b, v_conv_b),
        w_rgate=(w_rgate, m_w_rgate, v_w_rgate), b_rgate=(b_rgate, m_b_rgate, v_b_rgate),
        w_igate=(w_igate, m_w_igate, v_w_igate), b_igate=(b_igate, m_b_igate, v_b_igate),
        lru_a=(lru_a, m_lru_a, v_lru_a), v_norm_g=(v_norm_g, m_v_norm_g, v_v_norm_g),
        v_norm_b=(v_norm_b, m_v_norm_b, v_v_norm_b), w_spatial=(w_spatial, m_w_spatial, v_w_spatial),
        b_spatial=(b_spatial, m_b_spatial, v_b_spatial), g_lru_out=(g_lru_out, m_g_lru_out, v_g_lru_out),
        g_gmlp_out=(g_gmlp_out, m_g_gmlp_out, v_g_gmlp_out), g_ffn_pre=(g_ffn_pre, m_g_ffn_pre, v_g_ffn_pre),
        g_ffn_post=(g_ffn_post, m_g_ffn_post, v_g_ffn_post), ffn_conv_b=(ffn_conv_b, m_ffn_conv_b, v_ffn_conv_b))
    conv_params = dict(conv_w=(conv_w, m_conv_w, v_conv_w), ffn_conv_w=ffn_cw_taps)
    small_results, loss = _adamw_small(gathered, reduced, params, conv_params)
    results.update(small_results)
    results["ffn_conv_w"] = tuple(a.reshape(ffn_conv_w.shape) for a in results["ffn_conv_w"])
    loss = loss.reshape(())

    results["w_ada"] = _adamw_wada(c_all, gathered[0], gathered[1], gathered[2], w_ada, m_w_ada, v_w_ada)

    order = ["w_ada", "b_ada", "g_mix_pre", "g_mix_post", "w_in", "conv_w", "conv_b", "w_rgate", "b_rgate", "w_igate",
             "b_igate", "lru_a", "v_norm_g", "v_norm_b", "w_spatial", "b_spatial", "g_lru_out", "g_gmlp_out", "w_out",
             "g_ffn_pre", "g_ffn_post", "w_up", "ffn_conv_w", "ffn_conv_b", "w_down"]
    outs = [loss, grad_x[None]]
    for kind in range(4):
        outs += [results[n][kind] for n in order]
    return tuple(outs)
```

```python
import functools
import math

import jax
import jax.numpy as jnp
from jax import lax
from jax.experimental import pallas as pl
from jax.experimental.pallas import tpu as pltpu

F32 = jnp.float32
BF16 = jnp.bfloat16

D_MODEL = 1024
LRU_W = 512
GMLP_W = 512
N_HEADS = 8
HEAD_DIM = 64
N_GROUPS = 4
POS_BLOCK = 128
CHUNK = 64
IN_COLS = 2048
D_FF = 3072
N_MOD = 6
N_DEV = 8
EPS = 1e-6
LRU_C = 8.0
LRU_CONV_K = 4
FFN_CONV_K = 3

ADAM_LR = 0.001
ADAM_B1 = 0.9
ADAM_B2 = 0.999
ADAM_EPS = 1e-08
ADAM_WD = 0.01
ADAM_STEP = 10

LANES = 128
SUBLANES = 8
TT_BIG = 512
TT_MIX = 256
FF_CW = 1024
VMEM_LIMIT = 56 * 1024 * 1024

MESH = pl.DeviceIdType.MESH


def _sds(shape, dtype):
    return jax.ShapeDtypeStruct(shape, dtype)


def _cparams(sem=None):
    return pltpu.CompilerParams(dimension_semantics=sem, vmem_limit_bytes=VMEM_LIMIT)


def _whole():
    return pl.BlockSpec(memory_space=pltpu.VMEM)


def _const(shape):
    nd = len(shape)
    return pl.BlockSpec(shape, lambda *_: (0,) * nd)


def _any():
    return pl.BlockSpec(memory_space=pl.ANY)


class _Carry:
    def __init__(self, inputs, in_specs, out_shape, out_specs, scratch, start=None, finish=None, each=None, mid=None):
        self.inputs, self.in_specs, self.out_shape, self.out_specs = inputs, in_specs, out_shape, out_specs
        self.scratch, self.start, self.finish, self.each, self.mid = scratch, start, finish, each, mid


def _call(body, name, grid, in_specs, out_specs, out_shape, scratch, args, carry=None, body_starts_carry=False):
    n_in, n_out, n_scr = len(in_specs), len(out_specs), len(scratch)
    c_in = len(carry.in_specs) if carry else 0
    c_out = len(carry.out_specs) if carry else 0

    def full_body(*refs):
        ins = refs[:n_in]
        c_ins = refs[n_in:n_in + c_in]
        outs = refs[n_in + c_in:n_in + c_in + n_out]
        c_outs = refs[n_in + c_in + n_out:n_in + c_in + n_out + c_out]
        scr = refs[n_in + c_in + n_out + c_out:n_in + c_in + n_out + c_out + n_scr]
        c_scr = refs[n_in + c_in + n_out + c_out + n_scr:]
        if carry:
            first = functools.reduce(lambda a, b: a & b, [pl.program_id(d) == 0 for d in range(len(grid))])
            last = functools.reduce(lambda a, b: a & b, [pl.program_id(d) == g - 1 for d, g in enumerate(grid)])

        if carry and carry.start and not body_starts_carry:
            @pl.when(first)
            def _():
                carry.start(c_ins, c_outs, c_scr)

        if carry and carry.mid:
            @pl.when(pl.program_id(0) == carry.mid[0])
            def _():
                carry.mid[1](c_ins, c_outs, c_scr)

        if body_starts_carry:
            body(*ins, *outs, *scr, start_carry=(lambda: carry.start(c_ins, c_outs, c_scr)) if carry else (lambda: None))
        else:
            body(*ins, *outs, *scr)
        if carry and carry.each:
            carry.each(c_ins, c_outs, c_scr)
        if carry and carry.finish:
            @pl.when(last)
            def _():
                carry.finish(c_ins, c_outs, c_scr)

    res = pl.pallas_call(
        full_body, name=name, grid=grid,
        in_specs=list(in_specs) + (list(carry.in_specs) if carry else []),
        out_specs=list(out_specs) + (list(carry.out_specs) if carry else []),
        out_shape=list(out_shape) + (list(carry.out_shape) if carry else []),
        scratch_shapes=list(scratch) + (list(carry.scratch) if carry else []),
        compiler_params=_cparams(("arbitrary",) * len(grid)),
    )(*args, *(carry.inputs if carry else []))
    return res[:n_out], res[n_out:]


GELU_C0 = 0.7978845608028654
GELU_C1 = GELU_C0 * 0.044715


def _gelu(x):
    t = jnp.tanh(x * (GELU_C0 + GELU_C1 * (x * x)))
    hx = 0.5 * x
    return hx + hx * t


def _gelu_and_grad(x):
    x2 = x * x
    t = jnp.tanh(x * (GELU_C0 + GELU_C1 * x2))
    hx = 0.5 * x
    g = hx + hx * t
    dg = (0.5 + 0.5 * t) + hx * (1.0 - t * t) * (GELU_C0 + 3.0 * GELU_C1 * x2)
    return g, dg


def _sigmoid(x):
    return 1.0 / (1.0 + jnp.exp(-x))


def _softplus(x):
    return jnp.maximum(x, 0.0) + jnp.log1p(jnp.exp(-jnp.abs(x)))


def _neg_expm1(x):
    series = -x * (1.0 + x * (0.5 + x * (1.0 / 6.0 + x * (1.0 / 24.0 + x * (1.0 / 120.0)))))
    return jnp.where(x > -0.1, series, 1.0 - jnp.exp(x))


def _dot(a, b):
    return jnp.dot(a.astype(BF16), b.astype(BF16), preferred_element_type=F32)


def _dot_nt(a, b):
    return lax.dot_general(a.astype(BF16), b.astype(BF16), (((1,), (1,)), ((), ())), preferred_element_type=F32)


def _dot_tn(a, b):
    return lax.dot_general(a.astype(BF16), b.astype(BF16), (((0,), (0,)), ((), ())), preferred_element_type=F32)


def _rows(shape):
    return lax.broadcasted_iota(jnp.int32, shape, 0)


def _shift_down(cur, prev8, s):
    if s == 0:
        return cur
    n = cur.shape[0]
    r = pltpu.roll(cur, s, 0)
    p = pltpu.roll(prev8, s, 0)
    top = jnp.where(_rows(p.shape) < s, p, r[0:SUBLANES])
    if n == SUBLANES:
        return top
    return jnp.concatenate([top, r[SUBLANES:]], axis=0)


def _shift_up(cur, next8, s):
    if s == 0:
        return cur
    n = cur.shape[0]
    r = pltpu.roll(cur, n - s, 0)
    q = pltpu.roll(next8, SUBLANES - s, 0)
    bot = jnp.where(_rows(q.shape) >= SUBLANES - s, q, r[n - SUBLANES:])
    if n == SUBLANES:
        return bot
    return jnp.concatenate([r[:n - SUBLANES], bot], axis=0)


def _scan_fwd(a, b, h_in):
    n = a.shape[0]
    in_group = _rows(a.shape) & (SUBLANES - 1)
    s = 1
    while s < SUBLANES:
        a_s = pltpu.roll(a, s, 0)
        b_s = pltpu.roll(b, s, 0)
        m = in_group >= s
        b = jnp.where(m, a * b_s + b, b)
        a = jnp.where(m, a * a_s, a)
        s *= 2
    out, carry = [], h_in
    for g in range(n // SUBLANES):
        rows = slice(g * SUBLANES, (g + 1) * SUBLANES)
        h_g = a[rows] * carry + b[rows]
        out.append(h_g)
        carry = h_g[SUBLANES - 1:SUBLANES, :]
    return jnp.concatenate(out, axis=0)


def _scan_rev(a, b, l_in):
    n = a.shape[0]
    in_group = _rows(a.shape) & (SUBLANES - 1)
    s = 1
    while s < SUBLANES:
        a_s = pltpu.roll(a, n - s, 0)
        b_s = pltpu.roll(b, n - s, 0)
        m = in_group < SUBLANES - s
        b = jnp.where(m, b + a * b_s, b)
        a = jnp.where(m, a * a_s, a)
        s *= 2
    out, carry = [], l_in
    for g in reversed(range(n // SUBLANES)):
        rows = slice(g * SUBLANES, (g + 1) * SUBLANES)
        l_g = b[rows] + a[rows] * carry
        out.append(l_g)
        carry = l_g[0:1, :]
    return jnp.concatenate(out[::-1], axis=0)


def _rms(x):
    r = lax.rsqrt(jnp.mean(x * x, axis=-1, keepdims=True) + EPS)
    return x * r, r


def _rms_bwd(d_n, n, r):
    return r * (d_n - n * jnp.mean(d_n * n, axis=-1, keepdims=True))


def _colsum(x):
    return jnp.sum(x, axis=0, keepdims=True)


ROW_PIECE = 256


def _row_pieces(tt):
    return [slice(r, r + min(ROW_PIECE, tt)) for r in range(0, tt, min(ROW_PIECE, tt))]


def _lru_gates(xc, wr_ref, wi_ref, br, bi, sp_a):
    r = _sigmoid(_dot(xc, wr_ref[...]) + br)
    i = _sigmoid(_dot(xc, wi_ref[...]) + bi)
    la = -LRU_C * r * sp_a
    a = jnp.exp(la)
    mult = jnp.sqrt(_neg_expm1(2.0 * la))
    return r, i, a, mult


def _lru_conv(lx, prev8, cw_ref, cb):
    xc = cb + cw_ref[LRU_CONV_K - 1:LRU_CONV_K, :] * lx
    taps = []
    for k in range(LRU_CONV_K - 1):
        tap = _shift_down(lx, prev8, LRU_CONV_K - 1 - k)
        taps.append(tap)
        xc = xc + cw_ref[k:k + 1, :] * tap
    return xc, taps


def _ws_mask(transposed=False):
    i = lax.broadcasted_iota(jnp.int32, (POS_BLOCK, POS_BLOCK), 0)
    j = lax.broadcasted_iota(jnp.int32, (POS_BLOCK, POS_BLOCK), 1)
    if transposed:
        i, j = j, i
    return (j // CHUNK) <= (i // CHUNK)


def _gmlp_v(gv, vg, vb):
    av, dav = _gelu_and_grad(gv)
    mu = jnp.mean(av, axis=-1, keepdims=True)
    cen = av - mu
    rs = lax.rsqrt(jnp.mean(cen * cen, axis=-1, keepdims=True) + EPS)
    vhat = cen * rs
    return vhat * vg + vb, vhat, rs, dav


def _mix_fwd(x, sh, sc, g_pre, w_in, conv_w, conv_b, wr_bd, wi_bd, b_r, b_i, lru_a, vn_g, vn_b, w_sp, b_sp_t,
             g_lru, g_gmlp, w_out, g_post, gt_m, g_ffn_pre, sc_f, sh_f, carry=None):
    s_len = x.shape[0]
    tt = min(TT_MIX, s_len)
    nblk = tt // POS_BLOCK

    def body(x_ref, sh_ref, sc_ref, g_ref, w_ref, cw_ref, cb_ref, wr_ref, wi_ref, br_ref, bi_ref, la_ref, vg_ref,
             vb_ref, ws_ref, bst_ref, gl_ref, gg_ref, wo_ref, gp_ref, gtm_ref, g2_ref, scf_ref, shf_ref,
             z_ref, h_ref, y_ref, hl_ref, yo_ref, x1_ref, h2_ref, prev8, hcar):
        i = pl.program_id(0)

        @pl.when(i == 0)
        def _():
            prev8[...] = jnp.zeros_like(prev8)
            hcar[...] = jnp.zeros_like(hcar)

        n_x, _ = _rms(x_ref[...])
        h = (n_x * g_ref[...] * (1.0 + sc_ref[...]) + sh_ref[...]).astype(BF16)
        h_ref[...] = h
        z_ref[...] = jnp.dot(h, w_ref[...], preferred_element_type=F32)

        lx = z_ref[:, 0:LRU_W]
        gate = z_ref[:, LRU_W:2 * LRU_W]
        gu = z_ref[:, 2 * LRU_W:2 * LRU_W + GMLP_W]
        gv = z_ref[:, 2 * LRU_W + GMLP_W:]

        xc, _ = _lru_conv(lx, prev8[...], cw_ref, cb_ref[...])
        prev8[...] = lx[tt - SUBLANES:]
        sp_a = _softplus(-la_ref[...])
        _, ig, a, mult = _lru_gates(xc, wr_ref, wi_ref, br_ref[...], bi_ref[...], sp_a)
        bx = mult * (ig * xc)
        hl = _scan_fwd(a, bx, hcar[0:1, :])
        hcar[...] = jnp.broadcast_to(hl[tt - 1:tt, :], hcar.shape)
        hl_ref[...] = hl
        y_lru = hl * _gelu(gate)
        n_l, _ = _rms(y_lru)
        y_ref[:, 0:LRU_W] = (n_l * gl_ref[...]).astype(BF16)

        u = _gelu(gu)
        v, _, _, _ = _gmlp_v(gv, vg_ref[...], vb_ref[...])
        mask = _ws_mask()
        sp_parts = []
        for nb in range(nblk):
            row = []
            for g in range(N_GROUPS):
                wsm = jnp.where(mask, ws_ref[g], 0.0)
                vblk = v[nb * POS_BLOCK:(nb + 1) * POS_BLOCK, g * LANES:(g + 1) * LANES]
                row.append(_dot(wsm, vblk) + bst_ref[:, g:g + 1])
            sp_parts.append(jnp.concatenate(row, axis=1))
        sp = jnp.concatenate(sp_parts, axis=0) if nblk > 1 else sp_parts[0]
        n_g, _ = _rms(u * sp)
        y_ref[:, LRU_W:] = (n_g * gg_ref[...]).astype(BF16)

        y = jnp.dot(y_ref[...], wo_ref[...], preferred_element_type=F32)
        yo_ref[...] = y
        n_y, _ = _rms(y)
        x1 = x_ref[...] + gtm_ref[...] * (n_y * gp_ref[...])
        x1_ref[...] = x1
        n1, _ = _rms(x1)
        h2_ref[...] = (n1 * g2_ref[...] * (1.0 + scf_ref[...]) + shf_ref[...]).astype(BF16)

    row = lambda c: pl.BlockSpec((tt, c), lambda i: (i, 0))
    v512 = _const((1, LRU_W))
    vec = _const((1, D_MODEL))
    return _call(
        body, "mix_fwd", (s_len // tt,),
        in_specs=[row(D_MODEL), vec, vec, vec, _whole(),
                  _const((LRU_CONV_K, LRU_W)), v512, _whole(), _whole(), v512, v512, v512, v512, v512,
                  _whole(), _whole(), v512, v512, _whole(), vec, vec, vec, vec, vec],
        out_specs=[row(IN_COLS), row(D_MODEL), row(LRU_W + GMLP_W), row(LRU_W), row(D_MODEL), row(D_MODEL),
                   row(D_MODEL)],
        out_shape=[_sds((s_len, IN_COLS), F32), _sds((s_len, D_MODEL), BF16),
                   _sds((s_len, LRU_W + GMLP_W), BF16), _sds((s_len, LRU_W), F32),
                   _sds((s_len, D_MODEL), F32), _sds((s_len, D_MODEL), F32), _sds((s_len, D_MODEL), BF16)],
        scratch=[pltpu.VMEM((SUBLANES, LRU_W), F32), pltpu.VMEM((SUBLANES, LRU_W), F32)],
        args=(x, sh, sc, g_pre, w_in, conv_w, conv_b, wr_bd, wi_bd, b_r, b_i, lru_a, vn_g, vn_b, w_sp, b_sp_t,
              g_lru, g_gmlp, w_out, g_post, gt_m, g_ffn_pre, sc_f, sh_f), carry=carry)


FF_CHUNKS = N_DEV // 2
FF_CHUNK_W = D_FF // FF_CHUNKS


def _ffn_fwd(h2, w_up3, ffn_cw, ffn_cb, carry=None):
    s_len = h2.shape[0]
    tt = min(TT_BIG, s_len)
    nc, cw = FF_CHUNKS, FF_CHUNK_W

    def body(h2_ref, wu_ref, cwg_ref, cwv_ref, cbg_ref, cbv_ref, up_ref, upc_ref, act_ref, prev):
        i = pl.program_id(0)
        c = pl.program_id(1)

        @pl.when(i == 0)
        def _():
            prev[c] = jnp.zeros((2, SUBLANES, cw), F32)

        h2 = h2_ref[...]
        ug_pre = jnp.dot(h2, wu_ref[c], preferred_element_type=F32)
        uv_pre = jnp.dot(h2, wu_ref[nc + c], preferred_element_type=F32)
        up_ref[0] = ug_pre.astype(BF16)
        up_ref[1] = uv_pre.astype(BF16)
        ug, _ = _ffn_conv(ug_pre, prev[c, 0], cwg_ref, cbg_ref[...])
        uv, _ = _ffn_conv(uv_pre, prev[c, 1], cwv_ref, cbv_ref[...])
        prev[c, 0] = ug_pre[tt - SUBLANES:, :]
        prev[c, 1] = uv_pre[tt - SUBLANES:, :]
        upc_ref[0] = ug
        upc_ref[1] = uv
        act_ref[...] = (_gelu(ug) * uv).astype(BF16)

    chunk2 = pl.BlockSpec((2, tt, cw), lambda i, c: (0, i, c))
    ffn_cb2 = ffn_cb.reshape(1, 2 * D_FF)
    return _call(
        body, "ffn_fwd", (s_len // tt, nc),
        in_specs=[pl.BlockSpec((tt, D_MODEL), lambda i, c: (i, 0)), _whole(),
                  pl.BlockSpec((FFN_CONV_K, cw), lambda i, c: (0, c)),
                  pl.BlockSpec((FFN_CONV_K, cw), lambda i, c: (0, c + nc)),
                  pl.BlockSpec((1, cw), lambda i, c: (0, c)),
                  pl.BlockSpec((1, cw), lambda i, c: (0, c + nc))],
        out_specs=[chunk2, chunk2, pl.BlockSpec((tt, cw), lambda i, c: (i, c))],
        out_shape=[_sds((2, s_len, D_FF), BF16), _sds((2, s_len, D_FF), F32), _sds((s_len, D_FF), BF16)],
        scratch=[pltpu.VMEM((nc, 2, SUBLANES, cw), F32)],
        args=(h2, w_up3, ffn_cw, ffn_cw, ffn_cb2, ffn_cb2), carry=carry)


def _ffn_tail(act, w_down, x1, gt_f, g_post, target):
    s_len = x1.shape[0]
    tt = min(TT_BIG, s_len)

    def body(act_ref, wd_ref, x1_ref, gtf_ref, gp_ref, tg_ref, dy2_ref, dout_ref, loss_ref, vs_ref):
        @pl.when(pl.program_id(0) == 0)
        def _():
            loss_ref[...] = jnp.zeros_like(loss_ref)
            vs_ref[...] = jnp.zeros_like(vs_ref)

        for rows in _row_pieces(tt):
            n2, r2 = _rms(jnp.dot(act_ref[rows, :], wd_ref[...], preferred_element_type=F32))
            out = x1_ref[rows, :] + gtf_ref[...] * (n2 * gp_ref[...])
            err = out - tg_ref[rows, :]
            do = err * (1.0 / D_MODEL)
            dout_ref[rows, :] = do
            loss_ref[...] += jnp.broadcast_to(0.5 * jnp.sum(err * err, keepdims=True) * (1.0 / D_MODEL),
                                              loss_ref.shape)
            vs_ref[0:1, :] += _colsum(do * n2 * gp_ref[...])
            vs_ref[1:2, :] += _colsum(do * gtf_ref[...] * n2)
            dy2_ref[rows, :] = _rms_bwd(do * gtf_ref[...] * gp_ref[...], n2, r2).astype(BF16)

    row = lambda c: pl.BlockSpec((tt, c), lambda i: (i, 0))
    vec = _const((1, D_MODEL))
    outs, _ = _call(
        body, "ffn_tail", (s_len // tt,),
        in_specs=[row(D_FF), _whole(), row(D_MODEL), vec, vec, row(D_MODEL)],
        out_specs=[row(D_MODEL), row(D_MODEL), _const((SUBLANES, LANES)), _const((SUBLANES, D_MODEL))],
        out_shape=[_sds((s_len, D_MODEL), BF16), _sds((s_len, D_MODEL), F32), _sds((SUBLANES, LANES), F32),
                   _sds((SUBLANES, D_MODEL), F32)],
        scratch=[], args=(act, w_down, x1, gt_f, g_post, target))
    return outs


def _ffn_conv(up_pre, prev8, cw_ref, cb):
    up = cb + cw_ref[FFN_CONV_K - 1:FFN_CONV_K, :] * up_pre
    taps = []
    for k in range(FFN_CONV_K - 1):
        tap = _shift_down(up_pre, prev8, FFN_CONV_K - 1 - k)
        taps.append(tap)
        up = up + cw_ref[k:k + 1, :] * tap
    return up, taps


def _ffn_bwd(d_y2, up_pre, up, ffn_cw, w_down, carry=None):
    s_len = d_y2.shape[0]
    tt = min(TT_BIG, s_len)
    nt = s_len // tt
    cw = FF_CW
    nc = D_FF // cw

    def body(dy2_ref, up_ref, upc_ref, cwg_ref, cwv_ref, wd_ref, dup_ref, cs_ref, nxt, cs_acc):
        i = pl.program_id(0)
        c = pl.program_id(1)

        @pl.when(i == 0)
        def _():
            nxt[c] = jnp.zeros((2, SUBLANES, cw), F32)
            cs_acc[c] = jnp.zeros((2, SUBLANES, cw), F32)

        pw = 2 * LANES
        for piece in range(cw // pw):
            cols = slice(piece * pw, (piece + 1) * pw)
            d_act = _dot_nt(dy2_ref[...], wd_ref[pl.ds(pl.multiple_of(c * cw + piece * pw, pw), pw), :])
            uv = upc_ref[1, :, cols]
            gl, dgl = _gelu_and_grad(upc_ref[0, :, cols])
            d_ug = d_act * uv * dgl
            d_uv = d_act * gl
            for half, (d_u, cw_ref) in enumerate(((d_ug, cwg_ref), (d_uv, cwv_ref))):
                nx = nxt[c, half, :, cols]
                x_in = up_ref[half, :, cols].astype(F32)
                d_pre = cw_ref[FFN_CONV_K - 1:FFN_CONV_K, cols] * d_u
                sums = [None] * (FFN_CONV_K + 1)
                sums[FFN_CONV_K - 1] = _colsum(d_u * x_in)
                for k in range(FFN_CONV_K - 1):
                    ahead = _shift_up(d_u, nx, FFN_CONV_K - 1 - k)
                    d_pre = d_pre + cw_ref[k:k + 1, cols] * ahead
                    sums[k] = _colsum(ahead * x_in)
                sums[FFN_CONV_K] = _colsum(d_u)
                pad = jnp.zeros((SUBLANES - FFN_CONV_K - 1, pw), F32)
                cs_acc[c, half, :, cols] += jnp.concatenate(sums + [pad], axis=0)
                nxt[c, half, :, cols] = d_u[0:SUBLANES]
                dup_ref[half, :, cols] = d_pre.astype(BF16)

        for cc in range(nc):
            @pl.when((i == nt - 1) & (c == cc))
            def _():
                cs_ref[:, cc * cw:(cc + 1) * cw] = cs_acc[cc, 0]
                cs_ref[:, D_FF + cc * cw:D_FF + (cc + 1) * cw] = cs_acc[cc, 1]

    row = pl.BlockSpec((tt, D_MODEL), lambda i, c: (nt - 1 - i, 0))
    blk = pl.BlockSpec((2, tt, cw), lambda i, c: (0, nt - 1 - i, c))
    return _call(
        body, "ffn_bwd", (nt, nc),
        in_specs=[row, blk, blk,
                  pl.BlockSpec((FFN_CONV_K, cw), lambda i, c: (0, c)),
                  pl.BlockSpec((FFN_CONV_K, cw), lambda i, c: (0, c + nc)),
                  _whole()],
        out_specs=[blk, _const((SUBLANES, 2 * D_FF))],
        out_shape=[_sds((2, s_len, D_FF), BF16), _sds((SUBLANES, 2 * D_FF), F32)],
        scratch=[pltpu.VMEM((nc, 2, SUBLANES, cw), F32), pltpu.VMEM((nc, 2, SUBLANES, cw), F32)],
        args=(d_y2, up_pre, up, ffn_cw, ffn_cw, w_down), carry=carry)


def _up_bwd(d_up, w_up3, x1, dout, y, w_out, g_pre, sc_f, g_post, gt_m, carry=None):
    s_len = x1.shape[0]
    tt = min(TT_BIG, s_len)

    def body(du_ref, wu_ref, x1_ref, do_ref, y_ref, wo_ref, g2_ref, sc_ref, gp_ref, gt_ref,
             dx1_ref, dy_ref, dyc_ref, vs_ref):
        @pl.when(pl.program_id(0) == 0)
        def _():
            vs_ref[...] = jnp.zeros_like(vs_ref)

        for rows in _row_pieces(tt):
            d_h2 = jnp.zeros((rows.stop - rows.start, D_MODEL), F32)
            for half in range(2):
                for ch in range(FF_CHUNKS):
                    d_h2 = d_h2 + _dot_nt(du_ref[half, rows, ch * FF_CHUNK_W:(ch + 1) * FF_CHUNK_W],
                                          wu_ref[half * FF_CHUNKS + ch])
            n1, r1 = _rms(x1_ref[rows, :])
            ng = n1 * g2_ref[...]
            vs_ref[0:1, :] += _colsum(d_h2)
            vs_ref[1:2, :] += _colsum(d_h2 * ng)
            d_ng = d_h2 * (1.0 + sc_ref[...])
            vs_ref[2:3, :] += _colsum(d_ng * n1)
            d_x1 = do_ref[rows, :] + _rms_bwd(d_ng * g2_ref[...], n1, r1)
            dx1_ref[rows, :] = d_x1
            n_y, r_y = _rms(y_ref[rows, :])
            vs_ref[3:4, :] += _colsum(d_x1 * n_y * gp_ref[...])
            d_on = d_x1 * gt_ref[...]
            vs_ref[4:5, :] += _colsum(d_on * n_y)
            d_y = _rms_bwd(d_on * gp_ref[...], n_y, r_y).astype(BF16)
            dy_ref[rows, :] = d_y
            dyc_ref[rows, :] = _dot_nt(d_y, wo_ref[...])

    row = lambda c: pl.BlockSpec((tt, c), lambda i: (i, 0))
    vec = _const((1, D_MODEL))
    return _call(
        body, "up_bwd", (s_len // tt,),
        in_specs=[pl.BlockSpec((2, tt, D_FF), lambda i: (0, i, 0)), _whole(), row(D_MODEL), row(D_MODEL), row(D_MODEL),
                  _whole(), vec, vec, vec, vec],
        out_specs=[row(D_MODEL), row(D_MODEL), row(LRU_W + GMLP_W), _const((SUBLANES, D_MODEL))],
        out_shape=[_sds((s_len, D_MODEL), F32), _sds((s_len, D_MODEL), BF16), _sds((s_len, LRU_W + GMLP_W), F32),
                   _sds((SUBLANES, D_MODEL), F32)],
        scratch=[], args=(d_up, w_up3, x1, dout, y, w_out, g_pre, sc_f, g_post, gt_m), carry=carry)


def _head_pair_block(hd):
    return (slice((hd // 2) * HEAD_DIM, (hd // 2 + 1) * HEAD_DIM), slice((hd % 2) * HEAD_DIM, (hd % 2 + 1) * HEAD_DIM))


def _mix_bwd(d_ycat, z, hl, conv_w, conv_b, wr_bd, wi_bd, b_r, b_i, lru_a, vn_g, vn_b, w_sp, w_sp_t, b_sp_t,
             g_lru, g_gmlp, carry=None):
    s_len = z.shape[0]
    tt = min(TT_MIX, s_len)
    nt = s_len // tt
    nblk = tt // POS_BLOCK
    hb = tt // SUBLANES

    def body(dyc_ref, z_ref, zh_ref, hl_ref, hh_ref, cw_ref, cb_ref, wr_ref, wi_ref, br_ref, bi_ref, la_ref,
             vg_ref, vb_ref, ws_ref, wst_ref, bst_ref, gl_ref, gg_ref,
             dz_ref, vs_ref, dcw_ref, dwrb_ref, dwib_ref, dws_ref, dbs_ref, nxt_dxc, nxt_a, nxt_lam, dwr_ref, dwi_ref):
        i = pl.program_id(0)
        first_tile = i == nt - 1

        @pl.when(i == 0)
        def _():
            for ref in (vs_ref, dcw_ref, dwr_ref, dwi_ref, dws_ref, dbs_ref, nxt_dxc, nxt_a, nxt_lam):
                ref[...] = jnp.zeros_like(ref)

        lx = z_ref[:, 0:LRU_W]
        gate = z_ref[:, LRU_W:2 * LRU_W]
        gu = z_ref[:, 2 * LRU_W:2 * LRU_W + GMLP_W]
        gv = z_ref[:, 2 * LRU_W + GMLP_W:]
        prev8 = jnp.where(first_tile, 0.0, zh_ref[...])
        hprev8 = jnp.where(first_tile, 0.0, hh_ref[...])

        xc, taps = _lru_conv(lx, prev8, cw_ref, cb_ref[...])
        a_par = la_ref[...]
        sp_a = _softplus(-a_par)
        r, ig, a, mult = _lru_gates(xc, wr_ref, wi_ref, br_ref[...], bi_ref[...], sp_a)
        hl = hl_ref[...]
        h_prev = _shift_down(hl, hprev8, 1)
        ggate, dggate = _gelu_and_grad(gate)
        y_lru = hl * ggate
        n_l, r_l = _rms(y_lru)
        d_nl = dyc_ref[:, 0:LRU_W]
        vs_ref[6:7, :] += _colsum(d_nl * n_l)
        d_yl = _rms_bwd(d_nl * gl_ref[...], n_l, r_l)
        d_hl = d_yl * ggate
        d_gate = d_yl * hl * dggate
        a_up = _shift_up(a, nxt_a[...], 1)
        lam = _scan_rev(a_up, d_hl, nxt_lam[0:1, :])
        nxt_a[...] = jnp.broadcast_to(a[0:1, :], nxt_a.shape)
        nxt_lam[...] = jnp.broadcast_to(lam[0:1, :], nxt_lam.shape)
        ixc = ig * xc
        d_la = lam * h_prev * a - lam * ixc * (a * a) / mult
        d_i = lam * mult * xc
        d_xc = lam * mult * ig
        vs_ref[3:4, :] += _colsum(d_la * r) * (LRU_C * _sigmoid(-a_par))
        d_pr = d_la * (-LRU_C * sp_a) * r * (1.0 - r)
        d_pi = d_i * ig * (1.0 - ig)
        vs_ref[1:2, :] += _colsum(d_pr)
        vs_ref[2:3, :] += _colsum(d_pi)
        dwr_ref[...] += _dot_tn(xc, d_pr)
        dwi_ref[...] += _dot_tn(xc, d_pi)
        d_xc = d_xc + _dot_nt(d_pr, wr_ref[...]) + _dot_nt(d_pi, wi_ref[...])
        vs_ref[0:1, :] += _colsum(d_xc)
        nx = nxt_dxc[...]
        d_lx = cw_ref[LRU_CONV_K - 1:LRU_CONV_K, :] * d_xc
        dcw_ref[LRU_CONV_K - 1:LRU_CONV_K, :] += _colsum(d_xc * lx)
        for k in range(LRU_CONV_K - 1):
            d_lx = d_lx + cw_ref[k:k + 1, :] * _shift_up(d_xc, nx, LRU_CONV_K - 1 - k)
            dcw_ref[k:k + 1, :] += _colsum(d_xc * taps[k])
        nxt_dxc[...] = d_xc[0:SUBLANES]
        dz_ref[:, 0:LRU_W] = d_lx.astype(BF16)
        dz_ref[:, LRU_W:2 * LRU_W] = d_gate.astype(BF16)

        u, du = _gelu_and_grad(gu)
        v, vhat, rs, dav = _gmlp_v(gv, vg_ref[...], vb_ref[...])
        mask = _ws_mask()
        sp_parts = []
        for nb in range(nblk):
            rowp = []
            for g in range(N_GROUPS):
                wsm = jnp.where(mask, ws_ref[g], 0.0)
                vblk = v[nb * POS_BLOCK:(nb + 1) * POS_BLOCK, g * LANES:(g + 1) * LANES]
                rowp.append(_dot(wsm, vblk) + bst_ref[:, g:g + 1])
            sp_parts.append(jnp.concatenate(rowp, axis=1))
        sp = jnp.concatenate(sp_parts, axis=0) if nblk > 1 else sp_parts[0]
        y_g = u * sp
        n_g, r_g = _rms(y_g)
        d_ng = dyc_ref[:, LRU_W:]
        vs_ref[7:8, :] += _colsum(d_ng * n_g)
        d_yg = _rms_bwd(d_ng * gg_ref[...], n_g, r_g)
        d_gu = d_yg * sp * du
        d_sp = d_yg * u
        mask_t = _ws_mask(transposed=True)
        ones8 = jnp.ones((SUBLANES, LANES), F32)
        dv_parts = []
        for nb in range(nblk):
            rowp = []
            for g in range(N_GROUPS):
                rs_, cs_ = slice(nb * POS_BLOCK, (nb + 1) * POS_BLOCK), slice(g * LANES, (g + 1) * LANES)
                dsp_blk = d_sp[rs_, cs_]
                dbs_ref[g:g + 1, :] += lax.dot_general(
                    ones8, dsp_blk, (((1,), (1,)), ((), ())), preferred_element_type=F32,
                    precision=lax.Precision.HIGHEST)[0:1, :]
                dws_ref[g] += _dot_nt(dsp_blk, v[rs_, cs_])
                wsm_t = jnp.where(mask_t, wst_ref[g], 0.0)
                rowp.append(_dot(wsm_t, dsp_blk))
            dv_parts.append(jnp.concatenate(rowp, axis=1))
        d_v = jnp.concatenate(dv_parts, axis=0) if nblk > 1 else dv_parts[0]
        vs_ref[4:5, :] += _colsum(d_v * vhat)
        vs_ref[5:6, :] += _colsum(d_v)
        d_vh = d_v * vg_ref[...]
        d_av = rs * (d_vh - jnp.mean(d_vh, axis=-1, keepdims=True)
                     - vhat * jnp.mean(d_vh * vhat, axis=-1, keepdims=True))
        dz_ref[:, 2 * LRU_W:2 * LRU_W + GMLP_W] = d_gu.astype(BF16)
        dz_ref[:, 2 * LRU_W + GMLP_W:] = (d_av * dav).astype(BF16)

        @pl.when(i == nt - 1)
        def _():
            for hd in range(N_HEADS):
                blk = slice(hd * HEAD_DIM, (hd + 1) * HEAD_DIM)
                dwrb_ref[_head_pair_block(hd)] = dwr_ref[blk, blk]
                dwib_ref[_head_pair_block(hd)] = dwi_ref[blk, blk]
            for g in range(N_GROUPS):
                dws_ref[g] = jnp.where(mask, dws_ref[g], 0.0)

    rev = lambda c: pl.BlockSpec((tt, c), lambda i: (nt - 1 - i, 0))
    halo = pl.BlockSpec((SUBLANES, LRU_W), lambda i: (jnp.maximum((nt - 1 - i) * hb - 1, 0), 0))
    v512 = _const((1, LRU_W))
    return _call(
        body, "mix_bwd", (nt,),
        in_specs=[rev(LRU_W + GMLP_W), rev(IN_COLS), halo, rev(LRU_W), halo,
                  _const((LRU_CONV_K, LRU_W)), v512, _whole(), _whole(), v512, v512, v512, v512, v512,
                  _whole(), _whole(), _whole(), v512, v512],
        out_specs=[rev(IN_COLS), _const((SUBLANES, LRU_W)), _const((SUBLANES, LRU_W)),
                   _const((LRU_W // 2, 2 * HEAD_DIM)), _const((LRU_W // 2, 2 * HEAD_DIM)),
                   _const((N_GROUPS, POS_BLOCK, POS_BLOCK)), _const((SUBLANES, POS_BLOCK))],
        out_shape=[_sds((s_len, IN_COLS), BF16), _sds((SUBLANES, LRU_W), F32), _sds((SUBLANES, LRU_W), F32),
                   _sds((LRU_W // 2, 2 * HEAD_DIM), F32), _sds((LRU_W // 2, 2 * HEAD_DIM), F32),
                   _sds((N_GROUPS, POS_BLOCK, POS_BLOCK), F32), _sds((SUBLANES, POS_BLOCK), F32)],
        scratch=[pltpu.VMEM((SUBLANES, LRU_W), F32), pltpu.VMEM((SUBLANES, LRU_W), F32),
                 pltpu.VMEM((SUBLANES, LRU_W), F32), pltpu.VMEM((LRU_W, LRU_W), F32), pltpu.VMEM((LRU_W, LRU_W), F32)],
        args=(d_ycat, z, z, hl, hl, conv_w, conv_b, wr_bd, wi_bd, b_r, b_i, lru_a, vn_g, vn_b, w_sp, w_sp_t, b_sp_t,
              g_lru, g_gmlp), carry=carry)


def _in_bwd(d_z, w_in, x, d_x1, g, sc, carry=None):
    s_len = x.shape[0]
    tt = min(TT_BIG, s_len)

    def body(dz_ref, w_ref, x_ref, dx1_ref, g_ref, sc_ref, gx_ref, vs_ref):
        @pl.when(pl.program_id(0) == 0)
        def _():
            vs_ref[...] = jnp.zeros_like(vs_ref)

        for rows in _row_pieces(tt):
            d_h = _dot_nt(dz_ref[rows, :], w_ref[...])
            n, r = _rms(x_ref[rows, :])
            vs_ref[0:1, :] += _colsum(d_h)
            vs_ref[1:2, :] += _colsum(d_h * n * g_ref[...])
            d_ng = d_h * (1.0 + sc_ref[...])
            vs_ref[2:3, :] += _colsum(d_ng * n)
            gx_ref[rows, :] = dx1_ref[rows, :] + _rms_bwd(d_ng * g_ref[...], n, r)

    row = lambda c: pl.BlockSpec((tt, c), lambda i: (i, 0))
    vec = _const((1, D_MODEL))
    return _call(
        body, "in_bwd", (s_len // tt,),
        in_specs=[row(IN_COLS), _whole(), row(D_MODEL), row(D_MODEL), vec, vec],
        out_specs=[row(D_MODEL), _const((SUBLANES, D_MODEL))],
        out_shape=[_sds((s_len, D_MODEL), F32), _sds((SUBLANES, D_MODEL), F32)],
        scratch=[], args=(d_z, w_in, x, d_x1, g, sc), carry=carry)


def _wgrad(a, b, name, by_rows=False, carry=None):
    s_len, k_dim = a.shape
    halves = b.ndim == 3
    n_dim = b.shape[-1] * (2 if halves else 1)

    def body(a_ref, b_ref, ob_ref, own_ref):
        out = _dot_tn(a_ref[...], b_ref[0] if halves else b_ref[...])
        ob_ref[...] = out.astype(BF16)

        @pl.when(pl.program_id(0) == _dev_index(_my_pos()))
        def _():
            own_ref[...] = out

    if by_rows:
        tile = k_dim // N_DEV
        a_spec = pl.BlockSpec((s_len, tile), lambda j: (0, j))
        b_spec = pl.BlockSpec((s_len, n_dim), lambda j: (0, 0))
        o_spec = pl.BlockSpec((tile, n_dim), lambda j: (j, 0))
        own_shape = (tile, n_dim)
    else:
        tile = n_dim // N_DEV
        a_spec = pl.BlockSpec((s_len, k_dim), lambda j: (0, 0))
        if halves:
            per_half = N_DEV // 2
            b_spec = pl.BlockSpec((1, s_len, tile), lambda j: (j // per_half, 0, j % per_half))
        else:
            b_spec = pl.BlockSpec((s_len, tile), lambda j: (0, j))
        o_spec = pl.BlockSpec((k_dim, tile), lambda j: (0, j))
        own_shape = (k_dim, tile)
    return _call(
        body, name, (N_DEV,), in_specs=[a_spec, b_spec], out_specs=[o_spec, _const(own_shape)],
        out_shape=[_sds((k_dim, n_dim), BF16), _sds(own_shape, F32)],
        scratch=[], args=(a, b), carry=carry)


def _adam_math(w, g, m, v):
    m = ADAM_B1 * m + (1.0 - ADAM_B1) * g
    v = ADAM_B2 * v + (1.0 - ADAM_B2) * (g * g)
    m_hat = m / (1.0 - ADAM_B1 ** ADAM_STEP)
    v_hat = v / (1.0 - ADAM_B2 ** ADAM_STEP)
    delta = -ADAM_LR * (m_hat / (jnp.sqrt(v_hat) + ADAM_EPS) + ADAM_WD * w)
    return delta, m, v


def _row_tile(rows, cols, n_f32_arrays):
    budget = VMEM_LIMIT // 2
    tr = rows
    while tr % 2 == 0 and tr // 2 >= SUBLANES and (tr // 2) % SUBLANES == 0 and tr * cols * 4 * n_f32_arrays * 2 > budget:
        tr //= 2
    return tr


def _adamw_sum_block(w_ref, g_ref, r_refs, m_ref, v_ref, go_ref, d_ref, mo_ref, vo_ref):
    g = g_ref[...]
    for r_ref in r_refs:
        for k in range(r_ref.shape[0]):
            g = g + r_ref[k].astype(F32)
    go_ref[0] = g
    d_ref[0], mo_ref[0], vo_ref[0] = _adam_math(w_ref[0], g, m_ref[0], v_ref[0])


def _adamw_rider(parts, steps):
    inputs, in_specs, out_shape, out_specs, n_recvs = [], [], [], [], []
    for w, g_own, recv, m, v in parts:
        _, rows, cols = w.shape
        tr = rows // steps
        blk = pl.BlockSpec((1, tr, cols), lambda i: (0, i, 0))
        inputs += [w, g_own, *recv, m, v]
        in_specs += ([blk, pl.BlockSpec((tr, cols), lambda i: (i, 0))]
                     + [pl.BlockSpec((r.shape[0], tr, cols), lambda i: (0, i, 0)) for r in recv] + [blk, blk])
        out_shape += [_sds((1, rows, cols), F32)] * 4
        out_specs += [blk] * 4
        n_recvs.append(len(recv))

    def each(ins, outs, scr):
        for n_recv in n_recvs:
            _adamw_sum_block(ins[0], ins[1], ins[2:2 + n_recv], ins[2 + n_recv], ins[3 + n_recv], *outs[:4])
            ins, outs = ins[4 + n_recv:], outs[4:]

    return _Carry(inputs=inputs, in_specs=in_specs, out_shape=out_shape, out_specs=out_specs, scratch=[], each=each)


def _adamw_sum(w, g_own, recv, m, v, name):
    _, rows, cols = w.shape
    n_recv = len(recv)
    tr = _row_tile(rows, cols, 10)
    nb = rows // tr

    def body(w_ref, g_ref, *rest):
        _adamw_sum_block(w_ref, g_ref, rest[:n_recv], *rest[n_recv:])

    blk = pl.BlockSpec((1, tr, cols), lambda i: (0, i, 0))
    return pl.pallas_call(
        body, name=name, grid=(nb,),
        in_specs=[blk, pl.BlockSpec((tr, cols), lambda i: (i, 0))]
        + [pl.BlockSpec((r.shape[0], tr, cols), lambda i: (0, i, 0)) for r in recv] + [blk, blk],
        out_specs=[blk] * 4, out_shape=[_sds((1, rows, cols), F32)] * 4,
        compiler_params=_cparams(("arbitrary",)),
    )(w, g_own, *recv, m, v)


def _row_of_each(ref, row):
    cols = ref.shape[1]
    rows = _rows((N_DEV, cols))
    out = jnp.zeros((N_DEV, cols), F32)
    for d in range(N_DEV):
        picked = ref[d * SUBLANES + row:d * SUBLANES + row + 1, :]
        out = jnp.where(rows == d, jnp.broadcast_to(picked, (N_DEV, cols)), out)
    return out


def _my_columns(full, width, me):
    out = jnp.zeros(full.shape[:-1] + (width,), F32)
    for d in range(N_DEV):
        out = out + jnp.where(me == d, full[:, d * width:(d + 1) * width], 0.0)
    return out


def _adamw_wada(c_all, vs_in_all, vs_up_all, vs_ffn_all, w, m, v):
    _, rows, cols = w.shape

    def body(c_ref, vi_ref, vu_ref, vf_ref, w_ref, m_ref, v_ref, go_ref, d_ref, mo_ref, vo_ref):
        me = _dev_index(_my_pos())
        cv = _row_of_each(c_ref, 0)
        ca = cv * _sigmoid(cv)
        dmod = jnp.concatenate([_row_of_each(vi_ref, 0), _row_of_each(vi_ref, 1), _row_of_each(vu_ref, 3),
                                _row_of_each(vu_ref, 0), _row_of_each(vu_ref, 1), _row_of_each(vf_ref, 0)], axis=1)
        dm = _my_columns(dmod, cols, me)
        g = lax.dot_general(ca, dm, (((0,), (0,)), ((), ())), preferred_element_type=F32,
                            precision=lax.Precision.HIGHEST)
        go_ref[0] = g
        d_ref[0], mo_ref[0], vo_ref[0] = _adam_math(w_ref[0], g, m_ref[0], v_ref[0])

    return pl.pallas_call(
        body, name="adamw_w_ada", out_shape=[_sds((1, rows, cols), F32)] * 4,
        in_specs=[_whole()] * 7, out_specs=[_whole()] * 4,
        compiler_params=_cparams(),
    )(c_all, vs_in_all, vs_up_all, vs_ffn_all, w, m, v)


def _adamw_small(gathered, reduced, params, conv_params):
    names = list(params) + list(conv_params)
    allp = {**params, **conv_params}
    n_g = len(gathered) + len(reduced)

    def body(*refs):
        g_refs = refs[:n_g]
        p_refs = refs[n_g:n_g + 3 * len(names)]
        o_refs = refs[n_g + 3 * len(names):]
        me = _dev_index(_my_pos())

        def total(ref):
            s = ref[0:SUBLANES, :]
            for d in range(1, N_DEV):
                s = s + ref[d * SUBLANES:(d + 1) * SUBLANES, :]
            return s

        vs_in, vs_up, vs_ffn, loss = [total(r) for r in g_refs[:4]]
        cs, vs_mix, dcw, dwr, dwi, dws, dbs = [r[...] for r in g_refs[4:]]
        o_refs[-1][...] = loss[0:1, 0:1]
        mine = lambda full, width: _my_columns(full, width, me)

        all_ = (slice(None), slice(None))
        heads = lambda row: [((0, slice(h, h + 1), slice(None)), row[:, h * HEAD_DIM:(h + 1) * HEAD_DIM])
                             for h in range(N_HEADS)]
        blocks = lambda pairs: [((0, h), pairs[_head_pair_block(h)]) for h in range(N_HEADS)]
        pieces = {
            "b_ada": [((slice(None), slice(k * D_MODEL, (k + 1) * D_MODEL)), row) for k, row in enumerate(
                (vs_in[0:1], vs_in[1:2], vs_up[3:4], vs_up[0:1], vs_up[1:2], vs_ffn[0:1]))],
            "g_mix_pre": [(all_, vs_in[2:3])], "g_mix_post": [(all_, vs_up[4:5])],
            "g_ffn_pre": [(all_, vs_up[2:3])], "g_ffn_post": [(all_, vs_ffn[1:2])],
            "conv_b": [(all_, vs_mix[0:1])], "b_rgate": heads(vs_mix[1:2]), "b_igate": heads(vs_mix[2:3]),
            "lru_a": [(all_, vs_mix[3:4])], "v_norm_g": [(all_, vs_mix[4:5])], "v_norm_b": [(all_, vs_mix[5:6])],
            "g_lru_out": [(all_, vs_mix[6:7])], "g_gmlp_out": [(all_, vs_mix[7:8])],
            "w_rgate": blocks(dwr), "w_igate": blocks(dwi),
            "w_spatial": [((0, g), dws[g * POS_BLOCK:(g + 1) * POS_BLOCK, :]) for g in range(N_GROUPS)],
            "b_spatial": [((0,), dbs[0:N_GROUPS])],
            "ffn_conv_b": [(all_, cs[FFN_CONV_K:FFN_CONV_K + 1])],
            "conv_w": [((0,), mine(dcw[0:LRU_CONV_K], LRU_W // N_DEV))],
        }
        ffn_cw_rows = mine(cs[0:FFN_CONV_K], 2 * D_FF // N_DEV)
        pieces["ffn_conv_w"] = [((k,), ffn_cw_rows[k:k + 1]) for k in range(FFN_CONV_K)]
        for n_i, name in enumerate(names):
            w_ref, m_ref, v_ref = p_refs[3 * n_i:3 * n_i + 3]
            go_ref, d_ref, mo_ref, vo_ref = o_refs[4 * n_i:4 * n_i + 4]
            for idx, g in pieces[name]:
                go_ref[idx] = g
                d_ref[idx], mo_ref[idx], vo_ref[idx] = _adam_math(w_ref[idx], g, m_ref[idx], v_ref[idx])

    flat_params = [a for n in names for a in allp[n]]
    out_shape = [_sds(allp[n][0].shape, F32) for n in names for _ in range(4)] + [_sds((1, 1), F32)]
    outs = pl.pallas_call(
        body, name="adamw_small", out_shape=out_shape,
        in_specs=[_whole()] * (n_g + len(flat_params)), out_specs=[_whole()] * len(out_shape),
        compiler_params=_cparams(),
    )(*gathered, *reduced, *flat_params)
    return {n: outs[4 * i:4 * i + 4] for i, n in enumerate(names)}, outs[-1]


def _my_pos():
    return lax.axis_index("x"), lax.axis_index("y"), lax.axis_index("c")


def _flip(pos, k):
    x, y, c = pos
    return (1 - x if k & 4 else x, 1 - y if k & 2 else y, 1 - c if k & 1 else c)


def _dev_index(pos):
    x, y, c = pos
    return 4 * x + 2 * y + c


def _all_gather_small(ins, outs, send_sems, recv_sems, meanwhile=None):
    n = len(ins)
    me = _my_pos()

    def slot(a, pos):
        rows = ins[a].shape[0]
        return outs[a].at[pl.ds(pl.multiple_of(_dev_index(pos) * rows, SUBLANES), rows), :]

    def copy(a, k, block):
        return pltpu.make_async_remote_copy(
            src_ref=ins[a], dst_ref=slot(a, block), send_sem=send_sems.at[a, k - 1], recv_sem=recv_sems.at[a, k - 1],
            device_id=_flip(me, k), device_id_type=MESH)

    sends = [copy(a, k, me) for a in range(n) for k in range(1, N_DEV)]
    for cp in sends:
        cp.start()
    for a in range(n):
        rows = ins[a].shape[0]
        outs[a][pl.ds(pl.multiple_of(_dev_index(me) * rows, SUBLANES), rows), :] = ins[a][...]
    if meanwhile:
        meanwhile()
    for a in range(n):
        for k in range(1, N_DEV):
            copy(a, k, _flip(me, k)).wait_recv()
    for cp in sends:
        cp.wait_send()


def _prologue(c, cw, fcw, w_ada, b_ada, w_rgate, w_igate, b_rgate, b_igate, carry):
    cols = w_ada.shape[1]
    cw_w, fcw_w = cw.shape[-1], fcw.shape[-1]
    step = math.gcd(cols, D_MODEL)

    def body(c_ref, cw_ref, fcw_ref, w_ref, b_ref, wr_ref, wi_ref, br_ref, bi_ref,
             call_ref, cwf_ref, fcwf_ref, wrbd_ref, wibd_ref, brow_ref, birow_ref, *rest, start_carry):
        mod_refs, (mod_scr, c8, cw8, fcw8, cwall, fcwall, modall, s1, r1, s2, r2) = rest[:N_MOD], rest[N_MOD:]
        me = _dev_index(_my_pos())
        c8[...] = jnp.broadcast_to(c_ref[...], c8.shape)
        cw8[...] = jnp.zeros(cw8.shape, F32)
        cw8[0:LRU_CONV_K, :] = cw_ref[...]
        fcw8[...] = jnp.zeros(fcw8.shape, F32)
        for k in range(FFN_CONV_K):
            fcw8[k:k + 1, :] = fcw_ref[k]

        def block_diagonals():
            for bd_ref, hb_ref in ((wrbd_ref, wr_ref), (wibd_ref, wi_ref)):
                bd_ref[...] = jnp.zeros(bd_ref.shape, BF16)
                for h in range(N_HEADS):
                    span = slice(h * HEAD_DIM, (h + 1) * HEAD_DIM)
                    bd_ref[span, span] = hb_ref[h].astype(BF16)
            for row_ref, hb_ref in ((brow_ref, br_ref), (birow_ref, bi_ref)):
                for h in range(N_HEADS):
                    row_ref[:, h * HEAD_DIM:(h + 1) * HEAD_DIM] = hb_ref[h:h + 1, :]

        def conv_weights():
            for d in range(N_DEV):
                cwf_ref[:, d * cw_w:(d + 1) * cw_w] = cwall[d * SUBLANES:d * SUBLANES + LRU_CONV_K, :]
                fcwf_ref[:, d * fcw_w:(d + 1) * fcw_w] = fcwall[d * SUBLANES:d * SUBLANES + FFN_CONV_K, :]

        _all_gather_small([c8, cw8, fcw8], [call_ref, cwall, fcwall], s1, r1, meanwhile=block_diagonals)
        start_carry()
        cv = _row_of_each(call_ref, 0)
        ca = cv * _sigmoid(cv)
        b_cols = _my_columns(b_ref[...], cols, me)
        mod_scr[...] = jnp.dot(ca, w_ref[...], preferred_element_type=F32, precision=lax.Precision.HIGHEST) + b_cols
        _all_gather_small([mod_scr], [modall], s2, r2, meanwhile=conv_weights)
        mine = _rows((N_DEV, cols)) == me
        for d in range(N_DEV):
            piece = jnp.sum(jnp.where(mine, modall[d * N_DEV:(d + 1) * N_DEV, :], 0.0), axis=0, keepdims=True)
            for t in range(cols // step):
                at = d * cols + t * step
                mod_refs[at // D_MODEL][:, at % D_MODEL:at % D_MODEL + step] = piece[:, t * step:(t + 1) * step]

    sem = lambda n: pltpu.SemaphoreType.DMA((n, N_DEV - 1))
    gate = N_HEADS * HEAD_DIM
    return _call(
        body, "prologue", (1,), in_specs=[_whole()] * 9, out_specs=[_whole()] * (7 + N_MOD),
        out_shape=[_sds((N_DEV * SUBLANES, c.shape[-1]), F32), _sds((LRU_CONV_K, N_DEV * cw_w), F32),
                   _sds((FFN_CONV_K, N_DEV * fcw_w), F32), _sds((gate, gate), BF16), _sds((gate, gate), BF16),
                   _sds((1, gate), F32), _sds((1, gate), F32)] + [_sds((1, D_MODEL), F32)] * N_MOD,
        scratch=[pltpu.VMEM((N_DEV, cols), F32)] + [pltpu.VMEM((SUBLANES, a.shape[-1]), F32) for a in (c, cw, fcw)]
        + [pltpu.VMEM((N_DEV * SUBLANES, cw_w), F32), pltpu.VMEM((N_DEV * SUBLANES, fcw_w), F32),
           pltpu.VMEM((N_DEV * N_DEV, cols), F32), sem(3), sem(3), sem(1), sem(1)],
        args=(c, cw, fcw, w_ada, b_ada, w_rgate, w_igate, b_rgate, b_igate), carry=carry, body_starts_carry=True)


def _reduce_small(gath, red, carry=None):
    n_g, n_r = len(gath), len(red)
    chip_flips = CHIP_FLIPS

    def body(*refs, start_carry):
        g_in, r_in = refs[:n_g], refs[n_g:n_g + n_r]
        g_out, r_out = refs[n_g + n_r:2 * n_g + n_r], refs[2 * n_g + n_r:2 * (n_g + n_r)]
        scr = refs[2 * (n_g + n_r):]
        sib, land = scr[:n_r], scr[n_r:2 * n_r]
        g_send, g_recv, s_send, s_recv, i_send, i_recv, f_send, f_recv = scr[2 * n_r:]
        me = _my_pos()
        c = me[2]
        sibling = _flip(me, 1)

        def slot(a, pos):
            return g_out[a].at[pl.ds(pl.multiple_of(_dev_index(pos) * SUBLANES, SUBLANES), SUBLANES), :]

        def gcopy(a, k):
            return pltpu.make_async_remote_copy(
                src_ref=g_in[a], dst_ref=slot(a, me), send_sem=g_send.at[a, k - 1], recv_sem=g_recv.at[a, k - 1],
                device_id=_flip(me, k), device_id_type=MESH)

        def scopy(a):
            return pltpu.make_async_remote_copy(
                src_ref=r_in[a], dst_ref=sib[a], send_sem=s_send.at[a], recv_sem=s_recv.at[a],
                device_id=sibling, device_id_type=MESH)

        def icopy(a, j):
            return pltpu.make_async_remote_copy(
                src_ref=r_out[a], dst_ref=land[a].at[j], send_sem=i_send.at[a, j], recv_sem=i_recv.at[a, j],
                device_id=_flip(me, chip_flips[j]), device_id_type=MESH)

        def fcopy(a, j):
            return pltpu.make_async_remote_copy(
                src_ref=land[a].at[j], dst_ref=land[a].at[j], send_sem=f_send.at[a, j], recv_sem=f_recv.at[a, j],
                device_id=sibling, device_id_type=MESH)

        gathers = [gcopy(a, k) for a in range(n_g) for k in range(1, N_DEV)]
        swaps = [scopy(a) for a in range(n_r)]
        for cp in gathers + swaps:
            cp.start()
        for a in range(n_g):
            g_out[a][pl.ds(pl.multiple_of(_dev_index(me) * SUBLANES, SUBLANES), SUBLANES), :] = g_in[a][...]
        for a in range(n_r):
            swaps[a].wait_recv()
            r_out[a][...] = r_in[a][...] + sib[a][...]

        for core in range(2):
            @pl.when(c == core)
            def _():
                for a in range(core, n_r, 2):
                    for j in range(3):
                        icopy(a, j).start()

        start_carry()

        for core in range(2):
            mine = [a for a in range(n_r) if a % 2 == core]
            theirs = [a for a in range(n_r) if a % 2 != core]

            @pl.when(c == core)
            def _():
                out = [icopy(a, j) for a in mine for j in range(3)]
                fwd = []
                for a in mine:
                    for j in range(3):
                        icopy(a, j).wait_recv()
                        cp = fcopy(a, j)
                        cp.start()
                        fwd.append(cp)
                for a in theirs:
                    for j in range(3):
                        fcopy(a, j).wait_recv()
                for cp in out + fwd:
                    cp.wait_send()

        for a in range(n_r):
            r_out[a][...] = (r_out[a][...] + land[a][1]) + (land[a][0] + land[a][2])
        for a in range(n_g):
            for k in range(1, N_DEV):
                pltpu.make_async_remote_copy(
                    src_ref=g_in[a], dst_ref=slot(a, _flip(me, k)), send_sem=g_send.at[a, k - 1],
                    recv_sem=g_recv.at[a, k - 1], device_id=_flip(me, k), device_id_type=MESH).wait_recv()
        for cp in gathers + swaps:
            cp.wait_send()

    shapes = [tuple(a.shape) for a in red]
    outs, carried = _call(
        body, "reduce_small", (1,), in_specs=[_whole()] * (n_g + n_r), out_specs=[_whole()] * (n_g + n_r),
        out_shape=[_sds((N_DEV * SUBLANES, a.shape[1]), F32) for a in gath] + [_sds(s, F32) for s in shapes],
        scratch=[pltpu.VMEM(s, F32) for s in shapes] + [pltpu.VMEM((3,) + s, F32) for s in shapes]
        + [pltpu.SemaphoreType.DMA((n_g, N_DEV - 1)), pltpu.SemaphoreType.DMA((n_g, N_DEV - 1)),
           pltpu.SemaphoreType.DMA((n_r,)), pltpu.SemaphoreType.DMA((n_r,)),
           pltpu.SemaphoreType.DMA((n_r, 3)), pltpu.SemaphoreType.DMA((n_r, 3)),
           pltpu.SemaphoreType.DMA((n_r, 3)), pltpu.SemaphoreType.DMA((n_r, 3))],
        args=tuple(gath) + tuple(red), carry=carry, body_starts_carry=True)
    return (outs[:n_g], outs[n_g:]), carried


STACKED = "stacked"


def _region(ref, shard_shape, col_sharded, pos):
    r, cdim = shard_shape
    d = _dev_index(pos)
    if col_sharded == STACKED:
        return ref.at[d]
    if col_sharded:
        return ref.at[:, pl.ds(pl.multiple_of(d * cdim, LANES), cdim)]
    return ref.at[pl.ds(pl.multiple_of(d * r, 2 * SUBLANES), r), :]


def _gather_carry(shards, col_sharded):
    n_w = len(shards)
    shapes = [tuple(s.shape) for s in shards]
    full_shapes = [(N_DEV,) + s if cs == STACKED else (s[0], s[1] * N_DEV) if cs else (s[0] * N_DEV, s[1])
                   for s, cs in zip(shapes, col_sharded)]

    def tools(out_refs, scr):
        send_sems, recv_sems = scr[n_w], scr[n_w + 1]
        me = _my_pos()
        x, y, c = me
        sibling = (x, y, 1 - c)
        chips = [(1 - x, y), (x, 1 - y), (1 - x, 1 - y)]

        def region(w, pos):
            return _region(out_refs[w], shapes[w], col_sharded[w], pos)

        def copy(w, k, block, to, src=None):
            return pltpu.make_async_remote_copy(
                src_ref=region(w, block) if src is None else src, dst_ref=region(w, block),
                send_sem=send_sems.at[w, k], recv_sem=recv_sems.at[w, k], device_id=to, device_id_type=MESH)

        def first(w):
            return [copy(w, 0, me, sibling, src=scr[w])] + [
                copy(w, 1 + j, me, (*chip, c), src=scr[w]) for j, chip in enumerate(chips)]

        def mine(w):
            return pltpu.make_async_copy(scr[w], region(w, me), scr[n_w + 2].at[w])

        return me, c, sibling, chips, copy, first, mine

    def start(ins, outs, scr):
        _, _, _, _, _, first, mine = tools(outs, scr)
        for w in range(n_w):
            scr[w][...] = ins[w][...].astype(BF16)
            for cp in first(w) + [mine(w)]:
                cp.start()

    def finish(ins, outs, scr):
        me, c, sibling, chips, copy, first, mine = tools(outs, scr)
        passed = []
        for w in range(n_w):
            for j, chip in enumerate(chips):
                copy(w, 1 + j, (*chip, c), me).wait_recv()
                fwd = copy(w, 4 + j, (*chip, c), sibling)
                fwd.start()
                passed.append(fwd)
        for w in range(n_w):
            copy(w, 0, sibling, me).wait_recv()
            for j, chip in enumerate(chips):
                copy(w, 4 + j, (*chip, 1 - c), me).wait_recv()
        for w in range(n_w):
            for cp in first(w):
                cp.wait_send()
            mine(w).wait()
        for cp in passed:
            cp.wait_send()

    return _Carry(
        inputs=list(shards), in_specs=[_whole()] * n_w,
        out_shape=[_sds(s, BF16) for s in full_shapes], out_specs=[_any()] * n_w,
        scratch=[pltpu.VMEM(s, BF16) for s in shapes]
        + [pltpu.SemaphoreType.DMA((n_w, N_DEV - 1)), pltpu.SemaphoreType.DMA((n_w, N_DEV - 1)),
           pltpu.SemaphoreType.DMA((n_w,))],
        start=start, finish=finish)


CHIP_FLIPS = (4, 2, 6)


def _two_level_scatter_carry(g_bf, g_own, col_sharded, mid_step):
    shape = tuple(g_own.shape)
    n = len(CHIP_FLIPS)

    def pair_copies(ins, scr):
        mine, sib, _, send_sems, recv_sems, local_sems = scr[:6]
        me = _my_pos()
        sibling = _flip(me, 1)

        def region(pos):
            return _region(ins[0], shape, col_sharded, pos)

        local = [pltpu.make_async_copy(region(_flip(me, f)), mine.at[s], local_sems.at[s])
                 for s, f in enumerate(CHIP_FLIPS)]
        sends = [pltpu.make_async_remote_copy(
            src_ref=region(_flip(sibling, f)), dst_ref=sib.at[s], send_sem=send_sems.at[s], recv_sem=recv_sems.at[s],
            device_id=sibling, device_id_type=MESH) for s, f in enumerate((0,) + CHIP_FLIPS)]
        return local, sends

    def chip_copies(outs, scr):
        h_out, chip_send, chip_recv = scr[2], scr[6], scr[7]
        me = _my_pos()
        return [pltpu.make_async_remote_copy(
            src_ref=h_out.at[j], dst_ref=outs[1].at[j], send_sem=chip_send.at[j], recv_sem=chip_recv.at[j],
            device_id=_flip(me, CHIP_FLIPS[j]), device_id_type=MESH) for j in range(n)]

    def start(ins, outs, scr):
        local, sends = pair_copies(ins, scr)
        for cp in local + sends:
            cp.start()

    def mid(ins, outs, scr):
        mine, sib, h_out = scr[:3]
        local, sends = pair_copies(ins, scr)
        for cp in local:
            cp.wait()
        for cp in sends:
            cp.wait_recv()
        outs[0][...] = ins[1][...] + sib[0].astype(F32)
        for s in range(n):
            h_out[s] = (mine[s].astype(F32) + sib[s + 1].astype(F32)).astype(BF16)
        for cp in chip_copies(outs, scr):
            cp.start()

    def finish(ins, outs, scr):
        cps = chip_copies(outs, scr)
        for cp in cps:
            cp.wait_recv()
        for cp in cps + pair_copies(ins, scr)[1]:
            cp.wait_send()

    return _Carry(
        inputs=[g_bf, g_own], in_specs=[_any(), _whole()],
        out_shape=[_sds(shape, F32), _sds((n,) + shape, BF16)], out_specs=[_whole(), _any()],
        scratch=[pltpu.VMEM((n,) + shape, BF16), pltpu.VMEM((n + 1,) + shape, BF16), pltpu.VMEM((n,) + shape, BF16),
                 pltpu.SemaphoreType.DMA((n + 1,)), pltpu.SemaphoreType.DMA((n + 1,)), pltpu.SemaphoreType.DMA((n,)),
                 pltpu.SemaphoreType.DMA((n,)), pltpu.SemaphoreType.DMA((n,))],
        start=start, finish=finish, mid=(mid_step, mid))


def _scatter_carry(grads_bf, shard_shapes, col_sharded, relations):
    n_w = len(grads_bf)
    shapes = [tuple(s) for s in shard_shapes]

    def copies(ins, outs, scr):
        send_sems, recv_sems = scr
        me = _my_pos()
        out = []
        for w in range(n_w):
            for i, k in enumerate(relations[w]):
                peer = _flip(me, k)
                out.append(pltpu.make_async_remote_copy(
                    src_ref=_region(ins[w], shapes[w], col_sharded[w], peer), dst_ref=outs[w].at[i],
                    send_sem=send_sems.at[w, i], recv_sem=recv_sems.at[w, i],
                    device_id=peer, device_id_type=MESH))
        return out

    def start(ins, outs, scr):
        for cp in copies(ins, outs, scr):
            cp.start()

    def finish(ins, outs, scr):
        cps = copies(ins, outs, scr)
        for cp in cps:
            cp.wait_recv()
        for cp in cps:
            cp.wait_send()

    return _Carry(
        inputs=list(grads_bf), in_specs=[_any()] * n_w,
        out_shape=[_sds((len(r),) + s, BF16) for r, s in zip(relations, shapes)], out_specs=[_any()] * n_w,
        scratch=[pltpu.SemaphoreType.DMA((n_w, N_DEV - 1)), pltpu.SemaphoreType.DMA((n_w, N_DEV - 1))],
        start=start, finish=finish)


def _block_diag(w):
    eye = jnp.eye(N_HEADS, dtype=w.dtype)
    return (eye[:, None, :, None] * w[:, :, None, :]).reshape(N_HEADS * HEAD_DIM, N_HEADS * HEAD_DIM)


def _local_step(x2, target, mod, w_in_f, w_full, conv_w_full, ffn_cw_full,
                g_mix_pre, g_mix_post, conv_b, w_rgate, b_rgate, w_igate, b_igate, lru_a, v_norm_g, v_norm_b,
                w_spatial, b_spatial, g_lru_out, g_gmlp_out, g_ffn_pre, g_ffn_post, ffn_conv_b,
                gather=None, scatter=None, adam=None, gate_bd=None):
    sh_m, sc_m, gt_m, sh_f, sc_f, gt_f = [mod[k] for k in range(N_MOD)]
    if gate_bd:
        wr_bd, wi_bd, b_r, b_i = gate_bd
    else:
        wr_bd, wi_bd = [_block_diag(w[0]).astype(BF16) for w in (w_rgate, w_igate)]
        b_r, b_i = b_rgate.reshape(1, LRU_W), b_igate.reshape(1, LRU_W)
    b_sp_t = b_spatial[0].T
    w_sp_t = jnp.swapaxes(w_spatial[0], 1, 2)

    def arriving(*names):
        return gather(*names) if gather else None

    near, far = (1, 2, 3, 4, 5), (6, 7)

    def leaving(*parts):
        return scatter(parts) if scatter else None

    def received(recv, parts, outs):
        for (name, _, _), out in zip(parts, outs):
            recv.setdefault(name, []).append(out)

    mix_params = (conv_w_full, conv_b, wr_bd, wi_bd, b_r, b_i, lru_a, v_norm_g, v_norm_b)
    w_out_f = w_full["w_out"]
    (z, h, ycat, hl, y, x1, h2), got = _mix_fwd(
        x2, sh_m, sc_m, g_mix_pre, w_in_f, *mix_params, w_spatial[0], b_sp_t, g_lru_out, g_gmlp_out,
        w_out_f, g_mix_post, gt_m, g_ffn_pre, sc_f, sh_f, carry=arriving("w_up"))
    w_up_f = got[0] if gather else w_full["w_up"]
    (up_pre, up, act), got = _ffn_fwd(h2, w_up_f, ffn_cw_full, ffn_conv_b, carry=arriving("w_down"))
    w_down_f = got[0] if gather else w_full["w_down"]
    d_y2, dout, loss_acc, vs_ffn = _ffn_tail(act, w_down_f, x1, gt_f, g_ffn_post, target)

    recv, updated = {}, {}

    def updating(grads):
        if not adam:
            return None
        return _adamw_rider([(adam[n][0], g[1], recv[n], adam[n][1], adam[n][2]) for n, g in grads.items()], N_DEV)

    def updates(grads, outs):
        for j, n in enumerate(grads):
            updated[n] = tuple(outs[4 * j:4 * j + 4])

    gw_down, _ = _wgrad(act, d_y2, "wgrad_down", by_rows=True)
    parts = [("w_down", gw_down[0], near + far)]
    (d_up, cs_ffn), got = _ffn_bwd(d_y2, up_pre, up, ffn_cw_full, w_down_f, carry=leaving(*parts))
    received(recv, parts, got)
    gw_up, got = _wgrad(h2, d_up, "wgrad_up", carry=updating(dict(w_down=gw_down)))
    updates(dict(w_down=gw_down), got)
    parts = [("w_up", gw_up[0], near)]
    (d_x1, d_y, d_ycat, vs_up), got = _up_bwd(
        d_up, w_up_f, x1, dout, y, w_out_f, g_ffn_pre, sc_f, g_mix_post, gt_m, carry=leaving(*parts))
    received(recv, parts, got)
    gw_out, _ = _wgrad(ycat, d_y, "wgrad_out", by_rows=True)
    parts = [("w_up", gw_up[0], far), ("w_out", gw_out[0], near + far)]
    (d_z, vs_mix, dcw, d_wr, d_wi, d_ws, d_bs), got = _mix_bwd(
        d_ycat, z, hl, *mix_params, w_spatial[0], w_sp_t, b_sp_t, g_lru_out, g_gmlp_out, carry=leaving(*parts))
    received(recv, parts, got)
    gw_in, got = _wgrad(h, d_z, "wgrad_in", carry=updating(dict(w_up=gw_up, w_out=gw_out)))
    updates(dict(w_up=gw_up, w_out=gw_out), got)
    in_bwd_steps = x2.shape[0] // min(TT_BIG, x2.shape[0])
    two_level = _two_level_scatter_carry(gw_in[0], gw_in[1], True, min(1, in_bwd_steps - 1)) if scatter else None
    (grad_x, vs_in), got = _in_bwd(d_z, w_in_f, x2, d_x1, g_mix_pre, sc_m, carry=two_level)
    if scatter:
        gw_in = (gw_in[0], got[0])
    recv["w_in"] = list(got[1:])

    gath = [vs_in, vs_up, vs_ffn, loss_acc]
    red = [cs_ffn, vs_mix, dcw, d_wr, d_wi, d_ws.reshape(N_GROUPS * POS_BLOCK, POS_BLOCK), d_bs]
    return dict(grad_x=grad_x, gath=gath, red=red, recv=recv, updated=updated,
                w_in=gw_in, w_out=gw_out, w_up=gw_up, w_down=gw_down)


def kernel(x, c, w_ada, b_ada, g_mix_pre, g_mix_post, w_in, conv_w, conv_b, w_rgate, b_rgate, w_igate, b_igate, lru_a, v_norm_g, v_norm_b, w_spatial, b_spatial, g_lru_out, g_gmlp_out, w_out, g_ffn_pre, g_ffn_post, w_up, ffn_conv_w, ffn_conv_b, w_down, loss_target, m_w_ada, m_b_ada, m_g_mix_pre, m_g_mix_post, m_w_in, m_conv_w, m_conv_b, m_w_rgate, m_b_rgate, m_w_igate, m_b_igate, m_lru_a, m_v_norm_g, m_v_norm_b, m_w_spatial, m_b_spatial, m_g_lru_out, m_g_gmlp_out, m_w_out, m_g_ffn_pre, m_g_ffn_post, m_w_up, m_ffn_conv_w, m_ffn_conv_b, m_w_down, v_w_ada, v_b_ada, v_g_mix_pre, v_g_mix_post, v_w_in, v_conv_w, v_conv_b, v_w_rgate, v_b_rgate, v_w_igate, v_b_igate, v_lru_a, v_v_norm_g, v_v_norm_b, v_w_spatial, v_b_spatial, v_g_lru_out, v_g_gmlp_out, v_w_out, v_g_ffn_pre, v_g_ffn_post, v_w_up, v_ffn_conv_w, v_ffn_conv_b, v_w_down):
    big_w = dict(w_in=(w_in, m_w_in, v_w_in, True), w_out=(w_out, m_w_out, v_w_out, False),
                 w_up=(w_up, m_w_up, v_w_up, True), w_down=(w_down, m_w_down, v_w_down, False))

    def gather(*names):
        return _gather_carry([big_w[n][0][0] for n in names], [STACKED if n == "w_up" else big_w[n][3] for n in names])

    def scatter(parts):
        return _scatter_carry([g for _, g, _ in parts], [big_w[n][0].shape[1:] for n, _, _ in parts],
                              [big_w[n][3] for n, _, _ in parts], [rel for _, _, rel in parts])

    ffn_cw_taps = tuple(a.reshape(FFN_CONV_K, 1, -1) for a in (ffn_conv_w, m_ffn_conv_w, v_ffn_conv_w))
    (c_all, conv_w_full, ffn_cw_full, *gate_bd, sh_m, sc_m, gt_m, sh_f, sc_f, gt_f), (w_in_f, w_out_f) = _prologue(
        c, conv_w[0], ffn_cw_taps[0], w_ada[0], b_ada, w_rgate[0], w_igate[0], b_rgate[0], b_igate[0],
        carry=gather("w_in", "w_out"))
    mod = (sh_m, sc_m, gt_m, sh_f, sc_f, gt_f)

    loc = _local_step(x[0], loss_target[0], mod, w_in_f, dict(w_out=w_out_f), conv_w_full, ffn_cw_full,
                      g_mix_pre, g_mix_post, conv_b, w_rgate, b_rgate, w_igate, b_igate, lru_a, v_norm_g, v_norm_b,
                      w_spatial, b_spatial, g_lru_out, g_gmlp_out, g_ffn_pre, g_ffn_post, ffn_conv_b,
                      gather=gather, scatter=scatter,
                      adam={n: big_w[n][:3] for n in ("w_out", "w_up", "w_down")}, gate_bd=gate_bd)
    grad_x = loc["grad_x"]

    (gathered, reduced), _ = _reduce_small(loc["gath"], loc["red"])

    results = dict(loc["updated"])
    w_, m_, v_, _ = big_w["w_in"]
    results["w_in"] = _adamw_sum(w_, loc["w_in"][1], loc["recv"]["w_in"], m_, v_, "adamw_w_in")

    params = dict(
        b_ada=(b_ada, m_b_ada, v_b_ada), g_mix_pre=(g_mix_pre, m_g_mix_pre, v_g_mix_pre),
        g_mix_post=(g_mix_post, m_g_mix_post, v_g_mix_post), conv_b=(conv_b, m_conv_b, v_conv_b),
        w_rgate=(w_rgate, m_w_rgate, v_w_rgate), b_rgate=(b_rgate, m_b_rgate, v_b_rgate),
        w_igate=(w_igate, m_w_igate, v_w_igate), b_igate=(b_igate, m_b_igate, v_b_igate),
        lru_a=(lru_a, m_lru_a, v_lru_a), v_norm_g=(v_norm_g, m_v_norm_g, v_v_norm_g),
        v_norm_b=(v_norm_b, m_v_norm_b, v_v_norm_b), w_spatial=(w_spatial, m_w_spatial, v_w_spatial),
        b_spatial=(b_spatial, m_b_spatial, v_b_spatial), g_lru_out=(g_lru_out, m_g_lru_out, v_g_lru_out),
        g_gmlp_out=(g_gmlp_out, m_g_gmlp_out, v_g_gmlp_out), g_ffn_pre=(g_ffn_pre, m_g_ffn_pre, v_g_ffn_pre),
        g_ffn_post=(g_ffn_post, m_g_ffn_post, v_g_ffn_post), ffn_conv_b=(ffn_conv_b, m_ffn_conv_b, v_ffn_conv_b))
    conv_params = dict(conv_w=(conv_w, m_conv_w, v_conv_w), ffn_conv_w=ffn_cw_taps)
    small_results, loss = _adamw_small(gathered, reduced, params, conv_params)
    results.update(small_results)
    results["ffn_conv_w"] = tuple(a.reshape(ffn_conv_w.shape) for a in results["ffn_conv_w"])
    loss = loss.reshape(())

    results["w_ada"] = _adamw_wada(c_all, gathered[0], gathered[1], gathered[2], w_ada, m_w_ada, v_w_ada)

    order = ["w_ada", "b_ada", "g_mix_pre", "g_mix_post", "w_in", "conv_w", "conv_b", "w_rgate", "b_rgate", "w_igate",
             "b_igate", "lru_a", "v_norm_g", "v_norm_b", "w_spatial", "b_spatial", "g_lru_out", "g_gmlp_out", "w_out",
             "g_ffn_pre", "g_ffn_post", "w_up", "ffn_conv_w", "ffn_conv_b", "w_down"]
    outs = [loss, grad_x[None]]
    for kind in range(4):
        outs += [results[n][kind] for n in order]
    return tuple(outs)
```

```python
import functools
import math

import jax
import jax.numpy as jnp
from jax import lax
from jax.experimental import pallas as pl
from jax.experimental.pallas import tpu as pltpu

F32 = jnp.float32
BF16 = jnp.bfloat16

D_MODEL = 1024
LRU_W = 512
GMLP_W = 512
N_HEADS = 8
HEAD_DIM = 64
N_GROUPS = 4
POS_BLOCK = 128
CHUNK = 64
IN_COLS = 2048
D_FF = 3072
N_MOD = 6
N_DEV = 8
EPS = 1e-6
LRU_C = 8.0
LRU_CONV_K = 4
FFN_CONV_K = 3

ADAM_LR = 0.001
ADAM_B1 = 0.9
ADAM_B2 = 0.999
ADAM_EPS = 1e-08
ADAM_WD = 0.01
ADAM_STEP = 10

LANES = 128
SUBLANES = 8
TT_BIG = 512
TT_MIX = 256
FF_CW = 1024
VMEM_LIMIT = 56 * 1024 * 1024

MESH = pl.DeviceIdType.MESH


def _sds(shape, dtype):
    return jax.ShapeDtypeStruct(shape, dtype)


def _cparams(sem=None):
    return pltpu.CompilerParams(dimension_semantics=sem, vmem_limit_bytes=VMEM_LIMIT)


def _whole():
    return pl.BlockSpec(memory_space=pltpu.VMEM)


def _const(shape):
    nd = len(shape)
    return pl.BlockSpec(shape, lambda *_: (0,) * nd)


def _any():
    return pl.BlockSpec(memory_space=pl.ANY)


class _Carry:
    def __init__(self, inputs, in_specs, out_shape, out_specs, scratch, start=None, finish=None, each=None, mid=None):
        self.inputs, self.in_specs, self.out_shape, self.out_specs = inputs, in_specs, out_shape, out_specs
        self.scratch, self.start, self.finish, self.each, self.mid = scratch, start, finish, each, mid


def _call(body, name, grid, in_specs, out_specs, out_shape, scratch, args, carry=None, body_starts_carry=False):
    n_in, n_out, n_scr = len(in_specs), len(out_specs), len(scratch)
    c_in = len(carry.in_specs) if carry else 0
    c_out = len(carry.out_specs) if carry else 0

    def full_body(*refs):
        ins = refs[:n_in]
        c_ins = refs[n_in:n_in + c_in]
        outs = refs[n_in + c_in:n_in + c_in + n_out]
        c_outs = refs[n_in + c_in + n_out:n_in + c_in + n_out + c_out]
        scr = refs[n_in + c_in + n_out + c_out:n_in + c_in + n_out + c_out + n_scr]
        c_scr = refs[n_in + c_in + n_out + c_out + n_scr:]
        if carry:
            first = functools.reduce(lambda a, b: a & b, [pl.program_id(d) == 0 for d in range(len(grid))])
            last = functools.reduce(lambda a, b: a & b, [pl.program_id(d) == g - 1 for d, g in enumerate(grid)])

        if carry and carry.start and not body_starts_carry:
            @pl.when(first)
            def _():
                carry.start(c_ins, c_outs, c_scr)

        if carry and carry.mid:
            @pl.when(pl.program_id(0) == carry.mid[0])
            def _():
                carry.mid[1](c_ins, c_outs, c_scr)

        if body_starts_carry:
            body(*ins, *outs, *scr, start_carry=(lambda: carry.start(c_ins, c_outs, c_scr)) if carry else (lambda: None))
        else:
            body(*ins, *outs, *scr)
        if carry and carry.each:
            carry.each(c_ins, c_outs, c_scr)
        if carry and carry.finish:
            @pl.when(last)
            def _():
                carry.finish(c_ins, c_outs, c_scr)

    res = pl.pallas_call(
        full_body, name=name, grid=grid,
        in_specs=list(in_specs) + (list(carry.in_specs) if carry else []),
        out_specs=list(out_specs) + (list(carry.out_specs) if carry else []),
        out_shape=list(out_shape) + (list(carry.out_shape) if carry else []),
        scratch_shapes=list(scratch) + (list(carry.scratch) if carry else []),
        compiler_params=_cparams(("arbitrary",) * len(grid)),
    )(*args, *(carry.inputs if carry else []))
    return res[:n_out], res[n_out:]


GELU_C0 = 0.7978845608028654
GELU_C1 = GELU_C0 * 0.044715


def _gelu(x):
    t = jnp.tanh(x * (GELU_C0 + GELU_C1 * (x * x)))
    hx = 0.5 * x
    return hx + hx * t


def _gelu_and_grad(x):
    x2 = x * x
    t = jnp.tanh(x * (GELU_C0 + GELU_C1 * x2))
    hx = 0.5 * x
    g = hx + hx * t
    dg = (0.5 + 0.5 * t) + hx * (1.0 - t * t) * (GELU_C0 + 3.0 * GELU_C1 * x2)
    return g, dg


def _sigmoid(x):
    return 1.0 / (1.0 + jnp.exp(-x))


def _softplus(x):
    return jnp.maximum(x, 0.0) + jnp.log1p(jnp.exp(-jnp.abs(x)))


def _neg_expm1(x):
    series = -x * (1.0 + x * (0.5 + x * (1.0 / 6.0 + x * (1.0 / 24.0 + x * (1.0 / 120.0)))))
    return jnp.where(x > -0.1, series, 1.0 - jnp.exp(x))


def _dot(a, b):
    return jnp.dot(a.astype(BF16), b.astype(BF16), preferred_element_type=F32)


def _dot_nt(a, b):
    return lax.dot_general(a.astype(BF16), b.astype(BF16), (((1,), (1,)), ((), ())), preferred_element_type=F32)


def _dot_tn(a, b):
    return lax.dot_general(a.astype(BF16), b.astype(BF16), (((0,), (0,)), ((), ())), preferred_element_type=F32)


def _rows(shape):
    return lax.broadcasted_iota(jnp.int32, shape, 0)


def _shift_down(cur, prev8, s):
    if s == 0:
        return cur
    n = cur.shape[0]
    r = pltpu.roll(cur, s, 0)
    p = pltpu.roll(prev8, s, 0)
    top = jnp.where(_rows(p.shape) < s, p, r[0:SUBLANES])
    if n == SUBLANES:
        return top
    return jnp.concatenate([top, r[SUBLANES:]], axis=0)


def _shift_up(cur, next8, s):
    if s == 0:
        return cur
    n = cur.shape[0]
    r = pltpu.roll(cur, n - s, 0)
    q = pltpu.roll(next8, SUBLANES - s, 0)
    bot = jnp.where(_rows(q.shape) >= SUBLANES - s, q, r[n - SUBLANES:])
    if n == SUBLANES:
        return bot
    return jnp.concatenate([r[:n - SUBLANES], bot], axis=0)


def _scan_fwd(a, b, h_in):
    n = a.shape[0]
    in_group = _rows(a.shape) & (SUBLANES - 1)
    s = 1
    while s < SUBLANES:
        a_s = pltpu.roll(a, s, 0)
        b_s = pltpu.roll(b, s, 0)
        m = in_group >= s
        b = jnp.where(m, a * b_s + b, b)
        a = jnp.where(m, a * a_s, a)
        s *= 2
    out, carry = [], h_in
    for g in range(n // SUBLANES):
        rows = slice(g * SUBLANES, (g + 1) * SUBLANES)
        h_g = a[rows] * carry + b[rows]
        out.append(h_g)
        carry = h_g[SUBLANES - 1:SUBLANES, :]
    return jnp.concatenate(out, axis=0)


def _scan_rev(a, b, l_in):
    n = a.shape[0]
    in_group = _rows(a.shape) & (SUBLANES - 1)
    s = 1
    while s < SUBLANES:
        a_s = pltpu.roll(a, n - s, 0)
        b_s = pltpu.roll(b, n - s, 0)
        m = in_group < SUBLANES - s
        b = jnp.where(m, b + a * b_s, b)
        a = jnp.where(m, a * a_s, a)
        s *= 2
    out, carry = [], l_in
    for g in reversed(range(n // SUBLANES)):
        rows = slice(g * SUBLANES, (g + 1) * SUBLANES)
        l_g = b[rows] + a[rows] * carry
        out.append(l_g)
        carry = l_g[0:1, :]
    return jnp.concatenate(out[::-1], axis=0)


def _rms(x):
    r = lax.rsqrt(jnp.mean(x * x, axis=-1, keepdims=True) + EPS)
    return x * r, r


def _rms_bwd(d_n, n, r):
    return r * (d_n - n * jnp.mean(d_n * n, axis=-1, keepdims=True))


def _colsum(x):
    return jnp.sum(x, axis=0, keepdims=True)


ROW_PIECE = 256


def _row_pieces(tt):
    return [slice(r, r + min(ROW_PIECE, tt)) for r in range(0, tt, min(ROW_PIECE, tt))]


def _lru_gates(xc, wr_ref, wi_ref, br, bi, sp_a):
    r = _sigmoid(_dot(xc, wr_ref[...]) + br)
    i = _sigmoid(_dot(xc, wi_ref[...]) + bi)
    la = -LRU_C * r * sp_a
    a = jnp.exp(la)
    mult = jnp.sqrt(_neg_expm1(2.0 * la))
    return r, i, a, mult


def _lru_conv(lx, prev8, cw_ref, cb):
    xc = cb + cw_ref[LRU_CONV_K - 1:LRU_CONV_K, :] * lx
    taps = []
    for k in range(LRU_CONV_K - 1):
        tap = _shift_down(lx, prev8, LRU_CONV_K - 1 - k)
        taps.append(tap)
        xc = xc + cw_ref[k:k + 1, :] * tap
    return xc, taps


def _ws_mask(transposed=False):
    i = lax.broadcasted_iota(jnp.int32, (POS_BLOCK, POS_BLOCK), 0)
    j = lax.broadcasted_iota(jnp.int32, (POS_BLOCK, POS_BLOCK), 1)
    if transposed:
        i, j = j, i
    return (j // CHUNK) <= (i // CHUNK)


def _gmlp_v(gv, vg, vb):
    av, dav = _gelu_and_grad(gv)
    mu = jnp.mean(av, axis=-1, keepdims=True)
    cen = av - mu
    rs = lax.rsqrt(jnp.mean(cen * cen, axis=-1, keepdims=True) + EPS)
    vhat = cen * rs
    return vhat * vg + vb, vhat, rs, dav


def _mix_fwd(x, sh, sc, g_pre, w_in, conv_w, conv_b, wr_bd, wi_bd, b_r, b_i, lru_a, vn_g, vn_b, w_sp, b_sp_t,
             g_lru, g_gmlp, w_out, g_post, gt_m, g_ffn_pre, sc_f, sh_f, carry=None):
    s_len = x.shape[0]
    tt = min(TT_MIX, s_len)
    nblk = tt // POS_BLOCK

    def body(x_ref, sh_ref, sc_ref, g_ref, w_ref, cw_ref, cb_ref, wr_ref, wi_ref, br_ref, bi_ref, la_ref, vg_ref,
             vb_ref, ws_ref, bst_ref, gl_ref, gg_ref, wo_ref, gp_ref, gtm_ref, g2_ref, scf_ref, shf_ref,
             z_ref, h_ref, y_ref, hl_ref, yo_ref, x1_ref, h2_ref, prev8, hcar):
        i = pl.program_id(0)

        @pl.when(i == 0)
        def _():
            prev8[...] = jnp.zeros_like(prev8)
            hcar[...] = jnp.zeros_like(hcar)

        n_x, _ = _rms(x_ref[...])
        h = (n_x * g_ref[...] * (1.0 + sc_ref[...]) + sh_ref[...]).astype(BF16)
        h_ref[...] = h
        z_ref[...] = jnp.dot(h, w_ref[...], preferred_element_type=F32)

        lx = z_ref[:, 0:LRU_W]
        gate = z_ref[:, LRU_W:2 * LRU_W]
        gu = z_ref[:, 2 * LRU_W:2 * LRU_W + GMLP_W]
        gv = z_ref[:, 2 * LRU_W + GMLP_W:]

        xc, _ = _lru_conv(lx, prev8[...], cw_ref, cb_ref[...])
        prev8[...] = lx[tt - SUBLANES:]
        sp_a = _softplus(-la_ref[...])
        _, ig, a, mult = _lru_gates(xc, wr_ref, wi_ref, br_ref[...], bi_ref[...], sp_a)
        bx = mult * (ig * xc)
        hl = _scan_fwd(a, bx, hcar[0:1, :])
        hcar[...] = jnp.broadcast_to(hl[tt - 1:tt, :], hcar.shape)
        hl_ref[...] = hl
        y_lru = hl * _gelu(gate)
        n_l, _ = _rms(y_lru)
        y_ref[:, 0:LRU_W] = (n_l * gl_ref[...]).astype(BF16)

        u = _gelu(gu)
        v, _, _, _ = _gmlp_v(gv, vg_ref[...], vb_ref[...])
        mask = _ws_mask()
        sp_parts = []
        for nb in range(nblk):
            row = []
            for g in range(N_GROUPS):
                wsm = jnp.where(mask, ws_ref[g], 0.0)
                vblk = v[nb * POS_BLOCK:(nb + 1) * POS_BLOCK, g * LANES:(g + 1) * LANES]
                row.append(_dot(wsm, vblk) + bst_ref[:, g:g + 1])
            sp_parts.append(jnp.concatenate(row, axis=1))
        sp = jnp.concatenate(sp_parts, axis=0) if nblk > 1 else sp_parts[0]
        n_g, _ = _rms(u * sp)
        y_ref[:, LRU_W:] = (n_g * gg_ref[...]).astype(BF16)

        y = jnp.dot(y_ref[...], wo_ref[...], preferred_element_type=F32)
        yo_ref[...] = y
        n_y, _ = _rms(y)
        x1 = x_ref[...] + gtm_ref[...] * (n_y * gp_ref[...])
        x1_ref[...] = x1
        n1, _ = _rms(x1)
        h2_ref[...] = (n1 * g2_ref[...] * (1.0 + scf_ref[...]) + shf_ref[...]).astype(BF16)

    row = lambda c: pl.BlockSpec((tt, c), lambda i: (i, 0))
    v512 = _const((1, LRU_W))
    vec = _const((1, D_MODEL))
    return _call(
        body, "mix_fwd", (s_len // tt,),
        in_specs=[row(D_MODEL), vec, vec, vec, _whole(),
                  _const((LRU_CONV_K, LRU_W)), v512, _whole(), _whole(), v512, v512, v512, v512, v512,
                  _whole(), _whole(), v512, v512, _whole(), vec, vec, vec, vec, vec],
        out_specs=[row(IN_COLS), row(D_MODEL), row(LRU_W + GMLP_W), row(LRU_W), row(D_MODEL), row(D_MODEL),
                   row(D_MODEL)],
        out_shape=[_sds((s_len, IN_COLS), F32), _sds((s_len, D_MODEL), BF16),
                   _sds((s_len, LRU_W + GMLP_W), BF16), _sds((s_len, LRU_W), F32),
                   _sds((s_len, D_MODEL), F32), _sds((s_len, D_MODEL), F32), _sds((s_len, D_MODEL), BF16)],
        scratch=[pltpu.VMEM((SUBLANES, LRU_W), F32), pltpu.VMEM((SUBLANES, LRU_W), F32)],
        args=(x, sh, sc, g_pre, w_in, conv_w, conv_b, wr_bd, wi_bd, b_r, b_i, lru_a, vn_g, vn_b, w_sp, b_sp_t,
              g_lru, g_gmlp, w_out, g_post, gt_m, g_ffn_pre, sc_f, sh_f), carry=carry)


FF_CHUNKS = N_DEV // 2
FF_CHUNK_W = D_FF // FF_CHUNKS


def _ffn_fwd(h2, w_up3, ffn_cw, ffn_cb, carry=None):
    s_len = h2.shape[0]
    tt = min(TT_BIG, s_len)
    nc, cw = FF_CHUNKS, FF_CHUNK_W

    def body(h2_ref, wu_ref, cwg_ref, cwv_ref, cbg_ref, cbv_ref, up_ref, upc_ref, act_ref, prev):
        i = pl.program_id(0)
        c = pl.program_id(1)

        @pl.when(i == 0)
        def _():
            prev[c] = jnp.zeros((2, SUBLANES, cw), F32)

        h2 = h2_ref[...]
        ug_pre = jnp.dot(h2, wu_ref[c], preferred_element_type=F32)
        uv_pre = jnp.dot(h2, wu_ref[nc + c], preferred_element_type=F32)
        up_ref[0] = ug_pre.astype(BF16)
        up_ref[1] = uv_pre.astype(BF16)
        ug, _ = _ffn_conv(ug_pre, prev[c, 0], cwg_ref, cbg_ref[...])
        uv, _ = _ffn_conv(uv_pre, prev[c, 1], cwv_ref, cbv_ref[...])
        prev[c, 0] = ug_pre[tt - SUBLANES:, :]
        prev[c, 1] = uv_pre[tt - SUBLANES:, :]
        upc_ref[0] = ug
        upc_ref[1] = uv
        act_ref[...] = (_gelu(ug) * uv).astype(BF16)

    chunk2 = pl.BlockSpec((2, tt, cw), lambda i, c: (0, i, c))
    ffn_cb2 = ffn_cb.reshape(1, 2 * D_FF)
    return _call(
        body, "ffn_fwd", (s_len // tt, nc),
        in_specs=[pl.BlockSpec((tt, D_MODEL), lambda i, c: (i, 0)), _whole(),
                  pl.BlockSpec((FFN_CONV_K, cw), lambda i, c: (0, c)),
                  pl.BlockSpec((FFN_CONV_K, cw), lambda i, c: (0, c + nc)),
                  pl.BlockSpec((1, cw), lambda i, c: (0, c)),
                  pl.BlockSpec((1, cw), lambda i, c: (0, c + nc))],
        out_specs=[chunk2, chunk2, pl.BlockSpec((tt, cw), lambda i, c: (i, c))],
        out_shape=[_sds((2, s_len, D_FF), BF16), _sds((2, s_len, D_FF), F32), _sds((s_len, D_FF), BF16)],
        scratch=[pltpu.VMEM((nc, 2, SUBLANES, cw), F32)],
        args=(h2, w_up3, ffn_cw, ffn_cw, ffn_cb2, ffn_cb2), carry=carry)


def _ffn_tail(act, w_down, x1, gt_f, g_post, target):
    s_len = x1.shape[0]
    tt = min(TT_BIG, s_len)

    def body(act_ref, wd_ref, x1_ref, gtf_ref, gp_ref, tg_ref, dy2_ref, dout_ref, loss_ref, vs_ref):
        @pl.when(pl.program_id(0) == 0)
        def _():
            loss_ref[...] = jnp.zeros_like(loss_ref)
            vs_ref[...] = jnp.zeros_like(vs_ref)

        for rows in _row_pieces(tt):
            n2, r2 = _rms(jnp.dot(act_ref[rows, :], wd_ref[...], preferred_element_type=F32))
            out = x1_ref[rows, :] + gtf_ref[...] * (n2 * gp_ref[...])
            err = out - tg_ref[rows, :]
            do = err * (1.0 / D_MODEL)
            dout_ref[rows, :] = do
            loss_ref[...] += jnp.broadcast_to(0.5 * jnp.sum(err * err, keepdims=True) * (1.0 / D_MODEL),
                                              loss_ref.shape)
            vs_ref[0:1, :] += _colsum(do * n2 * gp_ref[...])
            vs_ref[1:2, :] += _colsum(do * gtf_ref[...] * n2)
            dy2_ref[rows, :] = _rms_bwd(do * gtf_ref[...] * gp_ref[...], n2, r2).astype(BF16)

    row = lambda c: pl.BlockSpec((tt, c), lambda i: (i, 0))
    vec = _const((1, D_MODEL))
    outs, _ = _call(
        body, "ffn_tail", (s_len // tt,),
        in_specs=[row(D_FF), _whole(), row(D_MODEL), vec, vec, row(D_MODEL)],
        out_specs=[row(D_MODEL), row(D_MODEL), _const((SUBLANES, LANES)), _const((SUBLANES, D_MODEL))],
        out_shape=[_sds((s_len, D_MODEL), BF16), _sds((s_len, D_MODEL), F32), _sds((SUBLANES, LANES), F32),
                   _sds((SUBLANES, D_MODEL), F32)],
        scratch=[], args=(act, w_down, x1, gt_f, g_post, target))
    return outs


def _ffn_conv(up_pre, prev8, cw_ref, cb):
    up = cb + cw_ref[FFN_CONV_K - 1:FFN_CONV_K, :] * up_pre
    taps = []
    for k in range(FFN_CONV_K - 1):
        tap = _shift_down(up_pre, prev8, FFN_CONV_K - 1 - k)
        taps.append(tap)
        up = up + cw_ref[k:k + 1, :] * tap
    return up, taps


def _ffn_bwd(d_y2, up_pre, up, ffn_cw, w_down, carry=None):
    s_len = d_y2.shape[0]
    tt = min(TT_BIG, s_len)
    nt = s_len // tt
    cw = FF_CW
    nc = D_FF // cw

    def body(dy2_ref, up_ref, upc_ref, cwg_ref, cwv_ref, wd_ref, dup_ref, cs_ref, nxt, cs_acc):
        i = pl.program_id(0)
        c = pl.program_id(1)

        @pl.when(i == 0)
        def _():
            nxt[c] = jnp.zeros((2, SUBLANES, cw), F32)
            cs_acc[c] = jnp.zeros((2, SUBLANES, cw), F32)

        pw = 2 * LANES
        for piece in range(cw // pw):
            cols = slice(piece * pw, (piece + 1) * pw)
            d_act = _dot_nt(dy2_ref[...], wd_ref[pl.ds(pl.multiple_of(c * cw + piece * pw, pw), pw), :])
            uv = upc_ref[1, :, cols]
            gl, dgl = _gelu_and_grad(upc_ref[0, :, cols])
            d_ug = d_act * uv * dgl
            d_uv = d_act * gl
            for half, (d_u, cw_ref) in enumerate(((d_ug, cwg_ref), (d_uv, cwv_ref))):
                nx = nxt[c, half, :, cols]
                x_in = up_ref[half, :, cols].astype(F32)
                d_pre = cw_ref[FFN_CONV_K - 1:FFN_CONV_K, cols] * d_u
                sums = [None] * (FFN_CONV_K + 1)
                sums[FFN_CONV_K - 1] = _colsum(d_u * x_in)
                for k in range(FFN_CONV_K - 1):
                    ahead = _shift_up(d_u, nx, FFN_CONV_K - 1 - k)
                    d_pre = d_pre + cw_ref[k:k + 1, cols] * ahead
                    sums[k] = _colsum(ahead * x_in)
                sums[FFN_CONV_K] = _colsum(d_u)
                pad = jnp.zeros((SUBLANES - FFN_CONV_K - 1, pw), F32)
                cs_acc[c, half, :, cols] += jnp.concatenate(sums + [pad], axis=0)
                nxt[c, half, :, cols] = d_u[0:SUBLANES]
                dup_ref[half, :, cols] = d_pre.astype(BF16)

        for cc in range(nc):
            @pl.when((i == nt - 1) & (c == cc))
            def _():
                cs_ref[:, cc * cw:(cc + 1) * cw] = cs_acc[cc, 0]
                cs_ref[:, D_FF + cc * cw:D_FF + (cc + 1) * cw] = cs_acc[cc, 1]

    row = pl.BlockSpec((tt, D_MODEL), lambda i, c: (nt - 1 - i, 0))
    blk = pl.BlockSpec((2, tt, cw), lambda i, c: (0, nt - 1 - i, c))
    return _call(
        body, "ffn_bwd", (nt, nc),
        in_specs=[row, blk, blk,
                  pl.BlockSpec((FFN_CONV_K, cw), lambda i, c: (0, c)),
                  pl.BlockSpec((FFN_CONV_K, cw), lambda i, c: (0, c + nc)),
                  _whole()],
        out_specs=[blk, _const((SUBLANES, 2 * D_FF))],
        out_shape=[_sds((2, s_len, D_FF), BF16), _sds((SUBLANES, 2 * D_FF), F32)],
        scratch=[pltpu.VMEM((nc, 2, SUBLANES, cw), F32), pltpu.VMEM((nc, 2, SUBLANES, cw), F32)],
        args=(d_y2, up_pre, up, ffn_cw, ffn_cw, w_down), carry=carry)


def _up_bwd(d_up, w_up3, x1, dout, y, w_out, g_pre, sc_f, g_post, gt_m, carry=None):
    s_len = x1.shape[0]
    tt = min(TT_BIG, s_len)

    def body(du_ref, wu_ref, x1_ref, do_ref, y_ref, wo_ref, g2_ref, sc_ref, gp_ref, gt_ref,
             dx1_ref, dy_ref, dyc_ref, vs_ref):
        @pl.when(pl.program_id(0) == 0)
        def _():
            vs_ref[...] = jnp.zeros_like(vs_ref)

        for rows in _row_pieces(tt):
            d_h2 = jnp.zeros((rows.stop - rows.start, D_MODEL), F32)
            for half in range(2):
                for ch in range(FF_CHUNKS):
                    d_h2 = d_h2 + _dot_nt(du_ref[half, rows, ch * FF_CHUNK_W:(ch + 1) * FF_CHUNK_W],
                                          wu_ref[half * FF_CHUNKS + ch])
            n1, r1 = _rms(x1_ref[rows, :])
            ng = n1 * g2_ref[...]
            vs_ref[0:1, :] += _colsum(d_h2)
            vs_ref[1:2, :] += _colsum(d_h2 * ng)
            d_ng = d_h2 * (1.0 + sc_ref[...])
            vs_ref[2:3, :] += _colsum(d_ng * n1)
            d_x1 = do_ref[rows, :] + _rms_bwd(d_ng * g2_ref[...], n1, r1)
            dx1_ref[rows, :] = d_x1
            n_y, r_y = _rms(y_ref[rows, :])
            vs_ref[3:4, :] += _colsum(d_x1 * n_y * gp_ref[...])
            d_on = d_x1 * gt_ref[...]
            vs_ref[4:5, :] += _colsum(d_on * n_y)
            d_y = _rms_bwd(d_on * gp_ref[...], n_y, r_y).astype(BF16)
            dy_ref[rows, :] = d_y
            dyc_ref[rows, :] = _dot_nt(d_y, wo_ref[...])

    row = lambda c: pl.BlockSpec((tt, c), lambda i: (i, 0))
    vec = _const((1, D_MODEL))
    return _call(
        body, "up_bwd", (s_len // tt,),
        in_specs=[pl.BlockSpec((2, tt, D_FF), lambda i: (0, i, 0)), _whole(), row(D_MODEL), row(D_MODEL), row(D_MODEL),
                  _whole(), vec, vec, vec, vec],
        out_specs=[row(D_MODEL), row(D_MODEL), row(LRU_W + GMLP_W), _const((SUBLANES, D_MODEL))],
        out_shape=[_sds((s_len, D_MODEL), F32), _sds((s_len, D_MODEL), BF16), _sds((s_len, LRU_W + GMLP_W), F32),
                   _sds((SUBLANES, D_MODEL), F32)],
        scratch=[], args=(d_up, w_up3, x1, dout, y, w_out, g_pre, sc_f, g_post, gt_m), carry=carry)


def _head_pair_block(hd):
    return (slice((hd // 2) * HEAD_DIM, (hd // 2 + 1) * HEAD_DIM), slice((hd % 2) * HEAD_DIM, (hd % 2 + 1) * HEAD_DIM))


def _mix_bwd(d_ycat, z, hl, conv_w, conv_b, wr_bd, wi_bd, b_r, b_i, lru_a, vn_g, vn_b, w_sp, w_sp_t, b_sp_t,
             g_lru, g_gmlp, carry=None):
    s_len = z.shape[0]
    tt = min(TT_MIX, s_len)
    nt = s_len // tt
    nblk = tt // POS_BLOCK
    hb = tt // SUBLANES

    def body(dyc_ref, z_ref, zh_ref, hl_ref, hh_ref, cw_ref, cb_ref, wr_ref, wi_ref, br_ref, bi_ref, la_ref,
             vg_ref, vb_ref, ws_ref, wst_ref, bst_ref, gl_ref, gg_ref,
             dz_ref, vs_ref, dcw_ref, dwrb_ref, dwib_ref, dws_ref, dbs_ref, nxt_dxc, nxt_a, nxt_lam, dwr_ref, dwi_ref):
        i = pl.program_id(0)
        first_tile = i == nt - 1

        @pl.when(i == 0)
        def _():
            for ref in (vs_ref, dcw_ref, dwr_ref, dwi_ref, dws_ref, dbs_ref, nxt_dxc, nxt_a, nxt_lam):
                ref[...] = jnp.zeros_like(ref)

        lx = z_ref[:, 0:LRU_W]
        gate = z_ref[:, LRU_W:2 * LRU_W]
        gu = z_ref[:, 2 * LRU_W:2 * LRU_W + GMLP_W]
        gv = z_ref[:, 2 * LRU_W + GMLP_W:]
        prev8 = jnp.where(first_tile, 0.0, zh_ref[...])
        hprev8 = jnp.where(first_tile, 0.0, hh_ref[...])

        xc, taps = _lru_conv(lx, prev8, cw_ref, cb_ref[...])
        a_par = la_ref[...]
        sp_a = _softplus(-a_par)
        r, ig, a, mult = _lru_gates(xc, wr_ref, wi_ref, br_ref[...], bi_ref[...], sp_a)
        hl = hl_ref[...]
        h_prev = _shift_down(hl, hprev8, 1)
        ggate, dggate = _gelu_and_grad(gate)
        y_lru = hl * ggate
        n_l, r_l = _rms(y_lru)
        d_nl = dyc_ref[:, 0:LRU_W]
        vs_ref[6:7, :] += _colsum(d_nl * n_l)
        d_yl = _rms_bwd(d_nl * gl_ref[...], n_l, r_l)
        d_hl = d_yl * ggate
        d_gate = d_yl * hl * dggate
        a_up = _shift_up(a, nxt_a[...], 1)
        lam = _scan_rev(a_up, d_hl, nxt_lam[0:1, :])
        nxt_a[...] = jnp.broadcast_to(a[0:1, :], nxt_a.shape)
        nxt_lam[...] = jnp.broadcast_to(lam[0:1, :], nxt_lam.shape)
        ixc = ig * xc
        d_la = lam * h_prev * a - lam * ixc * (a * a) / mult
        d_i = lam * mult * xc
        d_xc = lam * mult * ig
        vs_ref[3:4, :] += _colsum(d_la * r) * (LRU_C * _sigmoid(-a_par))
        d_pr = d_la * (-LRU_C * sp_a) * r * (1.0 - r)
        d_pi = d_i * ig * (1.0 - ig)
        vs_ref[1:2, :] += _colsum(d_pr)
        vs_ref[2:3, :] += _colsum(d_pi)
        dwr_ref[...] += _dot_tn(xc, d_pr)
        dwi_ref[...] += _dot_tn(xc, d_pi)
        d_xc = d_xc + _dot_nt(d_pr, wr_ref[...]) + _dot_nt(d_pi, wi_ref[...])
        vs_ref[0:1, :] += _colsum(d_xc)
        nx = nxt_dxc[...]
        d_lx = cw_ref[LRU_CONV_K - 1:LRU_CONV_K, :] * d_xc
        dcw_ref[LRU_CONV_K - 1:LRU_CONV_K, :] += _colsum(d_xc * lx)
        for k in range(LRU_CONV_K - 1):
            d_lx = d_lx + cw_ref[k:k + 1, :] * _shift_up(d_xc, nx, LRU_CONV_K - 1 - k)
            dcw_ref[k:k + 1, :] += _colsum(d_xc * taps[k])
        nxt_dxc[...] = d_xc[0:SUBLANES]
        dz_ref[:, 0:LRU_W] = d_lx.astype(BF16)
        dz_ref[:, LRU_W:2 * LRU_W] = d_gate.astype(BF16)

        u, du = _gelu_and_grad(gu)
        v, vhat, rs, dav = _gmlp_v(gv, vg_ref[...], vb_ref[...])
        mask = _ws_mask()
        sp_parts = []
        for nb in range(nblk):
            rowp = []
            for g in range(N_GROUPS):
                wsm = jnp.where(mask, ws_ref[g], 0.0)
                vblk = v[nb * POS_BLOCK:(nb + 1) * POS_BLOCK, g * LANES:(g + 1) * LANES]
                rowp.append(_dot(wsm, vblk) + bst_ref[:, g:g + 1])
            sp_parts.append(jnp.concatenate(rowp, axis=1))
        sp = jnp.concatenate(sp_parts, axis=0) if nblk > 1 else sp_parts[0]
        y_g = u * sp
        n_g, r_g = _rms(y_g)
        d_ng = dyc_ref[:, LRU_W:]
        vs_ref[7:8, :] += _colsum(d_ng * n_g)
        d_yg = _rms_bwd(d_ng * gg_ref[...], n_g, r_g)
        d_gu = d_yg * sp * du
        d_sp = d_yg * u
        mask_t = _ws_mask(transposed=True)
        ones8 = jnp.ones((SUBLANES, LANES), F32)
        dv_parts = []
        for nb in range(nblk):
            rowp = []
            for g in range(N_GROUPS):
                rs_, cs_ = slice(nb * POS_BLOCK, (nb + 1) * POS_BLOCK), slice(g * LANES, (g + 1) * LANES)
                dsp_blk = d_sp[rs_, cs_]
                dbs_ref[g:g + 1, :] += lax.dot_general(
                    ones8, dsp_blk, (((1,), (1,)), ((), ())), preferred_element_type=F32,
                    precision=lax.Precision.HIGHEST)[0:1, :]
                dws_ref[g] += _dot_nt(dsp_blk, v[rs_, cs_])
                wsm_t = jnp.where(mask_t, wst_ref[g], 0.0)
                rowp.append(_dot(wsm_t, dsp_blk))
            dv_parts.append(jnp.concatenate(rowp, axis=1))
        d_v = jnp.concatenate(dv_parts, axis=0) if nblk > 1 else dv_parts[0]
        vs_ref[4:5, :] += _colsum(d_v * vhat)
        vs_ref[5:6, :] += _colsum(d_v)
        d_vh = d_v * vg_ref[...]
        d_av = rs * (d_vh - jnp.mean(d_vh, axis=-1, keepdims=True)
                     - vhat * jnp.mean(d_vh * vhat, axis=-1, keepdims=True))
        dz_ref[:, 2 * LRU_W:2 * LRU_W + GMLP_W] = d_gu.astype(BF16)
        dz_ref[:, 2 * LRU_W + GMLP_W:] = (d_av * dav).astype(BF16)

        @pl.when(i == nt - 1)
        def _():
            for hd in range(N_HEADS):
                blk = slice(hd * HEAD_DIM, (hd + 1) * HEAD_DIM)
                dwrb_ref[_head_pair_block(hd)] = dwr_ref[blk, blk]
                dwib_ref[_head_pair_block(hd)] = dwi_ref[blk, blk]
            for g in range(N_GROUPS):
                dws_ref[g] = jnp.where(mask, dws_ref[g], 0.0)

    rev = lambda c: pl.BlockSpec((tt, c), lambda i: (nt - 1 - i, 0))
    halo = pl.BlockSpec((SUBLANES, LRU_W), lambda i: (jnp.maximum((nt - 1 - i) * hb - 1, 0), 0))
    v512 = _const((1, LRU_W))
    return _call(
        body, "mix_bwd", (nt,),
        in_specs=[rev(LRU_W + GMLP_W), rev(IN_COLS), halo, rev(LRU_W), halo,
                  _const((LRU_CONV_K, LRU_W)), v512, _whole(), _whole(), v512, v512, v512, v512, v512,
                  _whole(), _whole(), _whole(), v512, v512],
        out_specs=[rev(IN_COLS), _const((SUBLANES, LRU_W)), _const((SUBLANES, LRU_W)),
                   _const((LRU_W // 2, 2 * HEAD_DIM)), _const((LRU_W // 2, 2 * HEAD_DIM)),
                   _const((N_GROUPS, POS_BLOCK, POS_BLOCK)), _const((SUBLANES, POS_BLOCK))],
        out_shape=[_sds((s_len, IN_COLS), BF16), _sds((SUBLANES, LRU_W), F32), _sds((SUBLANES, LRU_W), F32),
                   _sds((LRU_W // 2, 2 * HEAD_DIM), F32), _sds((LRU_W // 2, 2 * HEAD_DIM), F32),
                   _sds((N_GROUPS, POS_BLOCK, POS_BLOCK), F32), _sds((SUBLANES, POS_BLOCK), F32)],
        scratch=[pltpu.VMEM((SUBLANES, LRU_W), F32), pltpu.VMEM((SUBLANES, LRU_W), F32),
                 pltpu.VMEM((SUBLANES, LRU_W), F32), pltpu.VMEM((LRU_W, LRU_W), F32), pltpu.VMEM((LRU_W, LRU_W), F32)],
        args=(d_ycat, z, z, hl, hl, conv_w, conv_b, wr_bd, wi_bd, b_r, b_i, lru_a, vn_g, vn_b, w_sp, w_sp_t, b_sp_t,
              g_lru, g_gmlp), carry=carry)


def _in_bwd(d_z, w_in, x, d_x1, g, sc, carry=None):
    s_len = x.shape[0]
    tt = min(TT_BIG, s_len)

    def body(dz_ref, w_ref, x_ref, dx1_ref, g_ref, sc_ref, gx_ref, vs_ref):
        @pl.when(pl.program_id(0) == 0)
        def _():
            vs_ref[...] = jnp.zeros_like(vs_ref)

        for rows in _row_pieces(tt):
            d_h = _dot_nt(dz_ref[rows, :], w_ref[...])
            n, r = _rms(x_ref[rows, :])
            vs_ref[0:1, :] += _colsum(d_h)
            vs_ref[1:2, :] += _colsum(d_h * n * g_ref[...])
            d_ng = d_h * (1.0 + sc_ref[...])
            vs_ref[2:3, :] += _colsum(d_ng * n)
            gx_ref[rows, :] = dx1_ref[rows, :] + _rms_bwd(d_ng * g_ref[...], n, r)

    row = lambda c: pl.BlockSpec((tt, c), lambda i: (i, 0))
    vec = _const((1, D_MODEL))
    return _call(
        body, "in_bwd", (s_len // tt,),
        in_specs=[row(IN_COLS), _whole(), row(D_MODEL), row(D_MODEL), vec, vec],
        out_specs=[row(D_MODEL), _const((SUBLANES, D_MODEL))],
        out_shape=[_sds((s_len, D_MODEL), F32), _sds((SUBLANES, D_MODEL), F32)],
        scratch=[], args=(d_z, w_in, x, d_x1, g, sc), carry=carry)


def _wgrad(a, b, name, by_rows=False, carry=None):
    s_len, k_dim = a.shape
    halves = b.ndim == 3
    n_dim = b.shape[-1] * (2 if halves else 1)

    def body(a_ref, b_ref, ob_ref, own_ref):
        out = _dot_tn(a_ref[...], b_ref[0] if halves else b_ref[...])
        ob_ref[...] = out.astype(BF16)

        @pl.when(pl.program_id(0) == _dev_index(_my_pos()))
        def _():
            own_ref[...] = out

    if by_rows:
        tile = k_dim // N_DEV
        a_spec = pl.BlockSpec((s_len, tile), lambda j: (0, j))
        b_spec = pl.BlockSpec((s_len, n_dim), lambda j: (0, 0))
        o_spec = pl.BlockSpec((tile, n_dim), lambda j: (j, 0))
        own_shape = (tile, n_dim)
    else:
        tile = n_dim // N_DEV
        a_spec = pl.BlockSpec((s_len, k_dim), lambda j: (0, 0))
        if halves:
            per_half = N_DEV // 2
            b_spec = pl.BlockSpec((1, s_len, tile), lambda j: (j // per_half, 0, j % per_half))
        else:
            b_spec = pl.BlockSpec((s_len, tile), lambda j: (0, j))
        o_spec = pl.BlockSpec((k_dim, tile), lambda j: (0, j))
        own_shape = (k_dim, tile)
    return _call(
        body, name, (N_DEV,), in_specs=[a_spec, b_spec], out_specs=[o_spec, _const(own_shape)],
        out_shape=[_sds((k_dim, n_dim), BF16), _sds(own_shape, F32)],
        scratch=[], args=(a, b), carry=carry)


def _adam_math(w, g, m, v):
    m = ADAM_B1 * m + (1.0 - ADAM_B1) * g
    v = ADAM_B2 * v + (1.0 - ADAM_B2) * (g * g)
    m_hat = m / (1.0 - ADAM_B1 ** ADAM_STEP)
    v_hat = v / (1.0 - ADAM_B2 ** ADAM_STEP)
    delta = -ADAM_LR * (m_hat / (jnp.sqrt(v_hat) + ADAM_EPS) + ADAM_WD * w)
    return delta, m, v


def _row_tile(rows, cols, n_f32_arrays):
    budget = VMEM_LIMIT // 2
    tr = rows
    while tr % 2 == 0 and tr // 2 >= SUBLANES and (tr // 2) % SUBLANES == 0 and tr * cols * 4 * n_f32_arrays * 2 > budget:
        tr //= 2
    return tr


def _adamw_sum_block(w_ref, g_ref, r_refs, m_ref, v_ref, go_ref, d_ref, mo_ref, vo_ref):
    g = g_ref[...]
    for r_ref in r_refs:
        for k in range(r_ref.shape[0]):
            g = g + r_ref[k].astype(F32)
    go_ref[0] = g
    d_ref[0], mo_ref[0], vo_ref[0] = _adam_math(w_ref[0], g, m_ref[0], v_ref[0])


def _adamw_rider(parts, steps):
    inputs, in_specs, out_shape, out_specs, n_recvs = [], [], [], [], []
    for w, g_own, recv, m, v in parts:
        _, rows, cols = w.shape
        tr = rows // steps
        blk = pl.BlockSpec((1, tr, cols), lambda i: (0, i, 0))
        inputs += [w, g_own, *recv, m, v]
        in_specs += ([blk, pl.BlockSpec((tr, cols), lambda i: (i, 0))]
                     + [pl.BlockSpec((r.shape[0], tr, cols), lambda i: (0, i, 0)) for r in recv] + [blk, blk])
        out_shape += [_sds((1, rows, cols), F32)] * 4
        out_specs += [blk] * 4
        n_recvs.append(len(recv))

    def each(ins, outs, scr):
        for n_recv in n_recvs:
            _adamw_sum_block(ins[0], ins[1], ins[2:2 + n_recv], ins[2 + n_recv], ins[3 + n_recv], *outs[:4])
            ins, outs = ins[4 + n_recv:], outs[4:]

    return _Carry(inputs=inputs, in_specs=in_specs, out_shape=out_shape, out_specs=out_specs, scratch=[], each=each)


def _adamw_sum(w, g_own, recv, m, v, name):
    _, rows, cols = w.shape
    n_recv = len(recv)
    tr = _row_tile(rows, cols, 10)
    nb = rows // tr

    def body(w_ref, g_ref, *rest):
        _adamw_sum_block(w_ref, g_ref, rest[:n_recv], *rest[n_recv:])

    blk = pl.BlockSpec((1, tr, cols), lambda i: (0, i, 0))
    return pl.pallas_call(
        body, name=name, grid=(nb,),
        in_specs=[blk, pl.BlockSpec((tr, cols), lambda i: (i, 0))]
        + [pl.BlockSpec((r.shape[0], tr, cols), lambda i: (0, i, 0)) for r in recv] + [blk, blk],
        out_specs=[blk] * 4, out_shape=[_sds((1, rows, cols), F32)] * 4,
        compiler_params=_cparams(("arbitrary",)),
    )(w, g_own, *recv, m, v)


def _row_of_each(ref, row):
    cols = ref.shape[1]
    rows = _rows((N_DEV, cols))
    out = jnp.zeros((N_DEV, cols), F32)
    for d in range(N_DEV):
        picked = ref[d * SUBLANES + row:d * SUBLANES + row + 1, :]
        out = jnp.where(rows == d, jnp.broadcast_to(picked, (N_DEV, cols)), out)
    return out


def _my_columns(full, width, me):
    out = jnp.zeros(full.shape[:-1] + (width,), F32)
    for d in range(N_DEV):
        out = out + jnp.where(me == d, full[:, d * width:(d + 1) * width], 0.0)
    return out


def _adamw_wada(c_all, vs_in_all, vs_up_all, vs_ffn_all, w, m, v):
    _, rows, cols = w.shape

    def body(c_ref, vi_ref, vu_ref, vf_ref, w_ref, m_ref, v_ref, go_ref, d_ref, mo_ref, vo_ref):
        me = _dev_index(_my_pos())
        cv = _row_of_each(c_ref, 0)
        ca = cv * _sigmoid(cv)
        dmod = jnp.concatenate([_row_of_each(vi_ref, 0), _row_of_each(vi_ref, 1), _row_of_each(vu_ref, 3),
                                _row_of_each(vu_ref, 0), _row_of_each(vu_ref, 1), _row_of_each(vf_ref, 0)], axis=1)
        dm = _my_columns(dmod, cols, me)
        g = lax.dot_general(ca, dm, (((0,), (0,)), ((), ())), preferred_element_type=F32,
                            precision=lax.Precision.HIGHEST)
        go_ref[0] = g
        d_ref[0], mo_ref[0], vo_ref[0] = _adam_math(w_ref[0], g, m_ref[0], v_ref[0])

    return pl.pallas_call(
        body, name="adamw_w_ada", out_shape=[_sds((1, rows, cols), F32)] * 4,
        in_specs=[_whole()] * 7, out_specs=[_whole()] * 4,
        compiler_params=_cparams(),
    )(c_all, vs_in_all, vs_up_all, vs_ffn_all, w, m, v)


def _adamw_small(gathered, reduced, params, conv_params):
    names = list(params) + list(conv_params)
    allp = {**params, **conv_params}
    n_g = len(gathered) + len(reduced)

    def body(*refs):
        g_refs = refs[:n_g]
        p_refs = refs[n_g:n_g + 3 * len(names)]
        o_refs = refs[n_g + 3 * len(names):]
        me = _dev_index(_my_pos())

        def total(ref):
            s = ref[0:SUBLANES, :]
            for d in range(1, N_DEV):
                s = s + ref[d * SUBLANES:(d + 1) * SUBLANES, :]
            return s

        vs_in, vs_up, vs_ffn, loss = [total(r) for r in g_refs[:4]]
        cs, vs_mix, dcw, dwr, dwi, dws, dbs = [r[...] for r in g_refs[4:]]
        o_refs[-1][...] = loss[0:1, 0:1]
        mine = lambda full, width: _my_columns(full, width, me)

        all_ = (slice(None), slice(None))
        heads = lambda row: [((0, slice(h, h + 1), slice(None)), row[:, h * HEAD_DIM:(h + 1) * HEAD_DIM])
                             for h in range(N_HEADS)]
        blocks = lambda pairs: [((0, h), pairs[_head_pair_block(h)]) for h in range(N_HEADS)]
        pieces = {
            "b_ada": [((slice(None), slice(k * D_MODEL, (k + 1) * D_MODEL)), row) for k, row in enumerate(
                (vs_in[0:1], vs_in[1:2], vs_up[3:4], vs_up[0:1], vs_up[1:2], vs_ffn[0:1]))],
            "g_mix_pre": [(all_, vs_in[2:3])], "g_mix_post": [(all_, vs_up[4:5])],
            "g_ffn_pre": [(all_, vs_up[2:3])], "g_ffn_post": [(all_, vs_ffn[1:2])],
            "conv_b": [(all_, vs_mix[0:1])], "b_rgate": heads(vs_mix[1:2]), "b_igate": heads(vs_mix[2:3]),
            "lru_a": [(all_, vs_mix[3:4])], "v_norm_g": [(all_, vs_mix[4:5])], "v_norm_b": [(all_, vs_mix[5:6])],
            "g_lru_out": [(all_, vs_mix[6:7])], "g_gmlp_out": [(all_, vs_mix[7:8])],
            "w_rgate": blocks(dwr), "w_igate": blocks(dwi),
            "w_spatial": [((0, g), dws[g * POS_BLOCK:(g + 1) * POS_BLOCK, :]) for g in range(N_GROUPS)],
            "b_spatial": [((0,), dbs[0:N_GROUPS])],
            "ffn_conv_b": [(all_, cs[FFN_CONV_K:FFN_CONV_K + 1])],
            "conv_w": [((0,), mine(dcw[0:LRU_CONV_K], LRU_W // N_DEV))],
        }
        ffn_cw_rows = mine(cs[0:FFN_CONV_K], 2 * D_FF // N_DEV)
        pieces["ffn_conv_w"] = [((k,), ffn_cw_rows[k:k + 1]) for k in range(FFN_CONV_K)]
        for n_i, name in enumerate(names):
            w_ref, m_ref, v_ref = p_refs[3 * n_i:3 * n_i + 3]
            go_ref, d_ref, mo_ref, vo_ref = o_refs[4 * n_i:4 * n_i + 4]
            for idx, g in pieces[name]:
                go_ref[idx] = g
                d_ref[idx], mo_ref[idx], vo_ref[idx] = _adam_math(w_ref[idx], g, m_ref[idx], v_ref[idx])

    flat_params = [a for n in names for a in allp[n]]
    out_shape = [_sds(allp[n][0].shape, F32) for n in names for _ in range(4)] + [_sds((1, 1), F32)]
    outs = pl.pallas_call(
        body, name="adamw_small", out_shape=out_shape,
        in_specs=[_whole()] * (n_g + len(flat_params)), out_specs=[_whole()] * len(out_shape),
        compiler_params=_cparams(),
    )(*gathered, *reduced, *flat_params)
    return {n: outs[4 * i:4 * i + 4] for i, n in enumerate(names)}, outs[-1]


def _my_pos():
    return lax.axis_index("x"), lax.axis_index("y"), lax.axis_index("c")


def _flip(pos, k):
    x, y, c = pos
    return (1 - x if k & 4 else x, 1 - y if k & 2 else y, 1 - c if k & 1 else c)


def _dev_index(pos):
    x, y, c = pos
    return 4 * x + 2 * y + c


def _all_gather_small(ins, outs, send_sems, recv_sems, meanwhile=None):
    n = len(ins)
    me = _my_pos()

    def slot(a, pos):
        rows = ins[a].shape[0]
        return outs[a].at[pl.ds(pl.multiple_of(_dev_index(pos) * rows, SUBLANES), rows), :]

    def copy(a, k, block):
        return pltpu.make_async_remote_copy(
            src_ref=ins[a], dst_ref=slot(a, block), send_sem=send_sems.at[a, k - 1], recv_sem=recv_sems.at[a, k - 1],
            device_id=_flip(me, k), device_id_type=MESH)

    sends = [copy(a, k, me) for a in range(n) for k in range(1, N_DEV)]
    for cp in sends:
        cp.start()
    for a in range(n):
        rows = ins[a].shape[0]
        outs[a][pl.ds(pl.multiple_of(_dev_index(me) * rows, SUBLANES), rows), :] = ins[a][...]
    if meanwhile:
        meanwhile()
    for a in range(n):
        for k in range(1, N_DEV):
            copy(a, k, _flip(me, k)).wait_recv()
    for cp in sends:
        cp.wait_send()


def _prologue(c, cw, fcw, w_ada, b_ada, w_rgate, w_igate, b_rgate, b_igate, carry):
    cols = w_ada.shape[1]
    cw_w, fcw_w = cw.shape[-1], fcw.shape[-1]
    step = math.gcd(cols, D_MODEL)
    c_at = slice(0, D_MODEL)
    fcw_at = slice(D_MODEL, D_MODEL + fcw_w)
    cw_at = slice(D_MODEL + fcw_w, D_MODEL + fcw_w + cw_w)
    pack_w = -(-cw_at.stop // LANES) * LANES

    def body(c_ref, cw_ref, fcw_ref, w_ref, b_ref, wr_ref, wi_ref, br_ref, bi_ref,
             call_ref, cwf_ref, fcwf_ref, wrbd_ref, wibd_ref, brow_ref, birow_ref, *rest, start_carry):
        mod_refs, (mod_scr, pack8, packall, modall, s1, r1, s2, r2) = rest[:N_MOD], rest[N_MOD:]
        me = _dev_index(_my_pos())
        pack8[...] = jnp.zeros(pack8.shape, F32)
        pack8[:, c_at] = jnp.broadcast_to(c_ref[...], (SUBLANES, D_MODEL))
        pack8[0:LRU_CONV_K, cw_at] = cw_ref[...]
        for k in range(FFN_CONV_K):
            pack8[k:k + 1, fcw_at] = fcw_ref[k]

        def block_diagonals():
            for bd_ref, hb_ref in ((wrbd_ref, wr_ref), (wibd_ref, wi_ref)):
                bd_ref[...] = jnp.zeros(bd_ref.shape, BF16)
                for h in range(N_HEADS):
                    span = slice(h * HEAD_DIM, (h + 1) * HEAD_DIM)
                    bd_ref[span, span] = hb_ref[h].astype(BF16)
            for row_ref, hb_ref in ((brow_ref, br_ref), (birow_ref, bi_ref)):
                for h in range(N_HEADS):
                    row_ref[:, h * HEAD_DIM:(h + 1) * HEAD_DIM] = hb_ref[h:h + 1, :]

        def conv_weights():
            for d in range(N_DEV):
                cwf_ref[:, d * cw_w:(d + 1) * cw_w] = packall[d * SUBLANES:d * SUBLANES + LRU_CONV_K, cw_at]
                fcwf_ref[:, d * fcw_w:(d + 1) * fcw_w] = packall[d * SUBLANES:d * SUBLANES + FFN_CONV_K, fcw_at]

        _all_gather_small([pack8], [packall], s1, r1, meanwhile=block_diagonals)
        start_carry()
        call_ref[...] = packall[:, c_at]
        cv = _row_of_each(call_ref, 0)
        ca = cv * _sigmoid(cv)
        b_cols = _my_columns(b_ref[...], cols, me)
        mod_scr[...] = jnp.dot(ca, w_ref[...], preferred_element_type=F32, precision=lax.Precision.HIGHEST) + b_cols
        _all_gather_small([mod_scr], [modall], s2, r2, meanwhile=conv_weights)
        mine = _rows((N_DEV, cols)) == me
        for d in range(N_DEV):
            piece = jnp.sum(jnp.where(mine, modall[d * N_DEV:(d + 1) * N_DEV, :], 0.0), axis=0, keepdims=True)
            for t in range(cols // step):
                at = d * cols + t * step
                mod_refs[at // D_MODEL][:, at % D_MODEL:at % D_MODEL + step] = piece[:, t * step:(t + 1) * step]

    sem = lambda n: pltpu.SemaphoreType.DMA((n, N_DEV - 1))
    gate = N_HEADS * HEAD_DIM
    return _call(
        body, "prologue", (1,), in_specs=[_whole()] * 9, out_specs=[_whole()] * (7 + N_MOD),
        out_shape=[_sds((N_DEV * SUBLANES, c.shape[-1]), F32), _sds((LRU_CONV_K, N_DEV * cw_w), F32),
                   _sds((FFN_CONV_K, N_DEV * fcw_w), F32), _sds((gate, gate), BF16), _sds((gate, gate), BF16),
                   _sds((1, gate), F32), _sds((1, gate), F32)] + [_sds((1, D_MODEL), F32)] * N_MOD,
        scratch=[pltpu.VMEM((N_DEV, cols), F32), pltpu.VMEM((SUBLANES, pack_w), F32),
                 pltpu.VMEM((N_DEV * SUBLANES, pack_w), F32), pltpu.VMEM((N_DEV * N_DEV, cols), F32),
                 sem(1), sem(1), sem(1), sem(1)],
        args=(c, cw, fcw, w_ada, b_ada, w_rgate, w_igate, b_rgate, b_igate), carry=carry, body_starts_carry=True)


def _reduce_small(gath, red, carry=None):
    n_g, n_r = len(gath), len(red)
    chip_flips = CHIP_FLIPS

    def body(*refs, start_carry):
        g_in, r_in = refs[:n_g], refs[n_g:n_g + n_r]
        g_out, r_out = refs[n_g + n_r:2 * n_g + n_r], refs[2 * n_g + n_r:2 * (n_g + n_r)]
        scr = refs[2 * (n_g + n_r):]
        sib, land = scr[:n_r], scr[n_r:2 * n_r]
        gpack, gall = scr[2 * n_r:2 * n_r + 2]
        g_send, g_recv, s_send, s_recv, i_send, i_recv, f_send, f_recv = scr[2 * n_r + 2:]
        me = _my_pos()
        c = me[2]
        sibling = _flip(me, 1)

        for a in range(n_g):
            gpack[:, g_at[a]] = g_in[a][...]

        def slot(pos):
            return gall.at[pl.ds(pl.multiple_of(_dev_index(pos) * SUBLANES, SUBLANES), SUBLANES), :]

        def gcopy(k, block):
            return pltpu.make_async_remote_copy(
                src_ref=gpack, dst_ref=slot(block), send_sem=g_send.at[0, k - 1], recv_sem=g_recv.at[0, k - 1],
                device_id=_flip(me, k), device_id_type=MESH)

        def scopy(a):
            return pltpu.make_async_remote_copy(
                src_ref=r_in[a], dst_ref=sib[a], send_sem=s_send.at[a], recv_sem=s_recv.at[a],
                device_id=sibling, device_id_type=MESH)

        def icopy(a, j):
            return pltpu.make_async_remote_copy(
                src_ref=r_out[a], dst_ref=land[a].at[j], send_sem=i_send.at[a, j], recv_sem=i_recv.at[a, j],
                device_id=_flip(me, chip_flips[j]), device_id_type=MESH)

        def fcopy(a, j):
            return pltpu.make_async_remote_copy(
                src_ref=land[a].at[j], dst_ref=land[a].at[j], send_sem=f_send.at[a, j], recv_sem=f_recv.at[a, j],
                device_id=sibling, device_id_type=MESH)

        gathers = [gcopy(k, me) for k in range(1, N_DEV)]
        swaps = [scopy(a) for a in range(n_r)]
        for cp in gathers + swaps:
            cp.start()
        gall[pl.ds(pl.multiple_of(_dev_index(me) * SUBLANES, SUBLANES), SUBLANES), :] = gpack[...]
        for a in range(n_r):
            swaps[a].wait_recv()
            r_out[a][...] = r_in[a][...] + sib[a][...]

        for core in range(2):
            @pl.when(c == core)
            def _():
                for a in range(core, n_r, 2):
                    for j in range(3):
                        icopy(a, j).start()

        start_carry()

        for core in range(2):
            mine = [a for a in range(n_r) if a % 2 == core]
            theirs = [a for a in range(n_r) if a % 2 != core]

            @pl.when(c == core)
            def _():
                out = [icopy(a, j) for a in mine for j in range(3)]
                fwd = []
                for a in mine:
                    for j in range(3):
                        icopy(a, j).wait_recv()
                        cp = fcopy(a, j)
                        cp.start()
                        fwd.append(cp)
                for a in theirs:
                    for j in range(3):
                        fcopy(a, j).wait_recv()
                for cp in out + fwd:
                    cp.wait_send()

        for a in range(n_r):
            r_out[a][...] = (r_out[a][...] + land[a][1]) + (land[a][0] + land[a][2])
        for k in range(1, N_DEV):
            gcopy(k, _flip(me, k)).wait_recv()
        for a in range(n_g):
            g_out[a][...] = gall[:, g_at[a]]
        for cp in gathers + swaps:
            cp.wait_send()

    shapes = [tuple(a.shape) for a in red]
    g_ends = [sum(a.shape[1] for a in gath[:i + 1]) for i in range(n_g)]
    g_at = [slice(end - a.shape[1], end) for a, end in zip(gath, g_ends)]
    outs, carried = _call(
        body, "reduce_small", (1,), in_specs=[_whole()] * (n_g + n_r), out_specs=[_whole()] * (n_g + n_r),
        out_shape=[_sds((N_DEV * SUBLANES, a.shape[1]), F32) for a in gath] + [_sds(s, F32) for s in shapes],
        scratch=[pltpu.VMEM(s, F32) for s in shapes] + [pltpu.VMEM((3,) + s, F32) for s in shapes]
        + [pltpu.VMEM((SUBLANES, g_ends[-1]), F32), pltpu.VMEM((N_DEV * SUBLANES, g_ends[-1]), F32)]
        + [pltpu.SemaphoreType.DMA((1, N_DEV - 1)), pltpu.SemaphoreType.DMA((1, N_DEV - 1)),
           pltpu.SemaphoreType.DMA((n_r,)), pltpu.SemaphoreType.DMA((n_r,)),
           pltpu.SemaphoreType.DMA((n_r, 3)), pltpu.SemaphoreType.DMA((n_r, 3)),
           pltpu.SemaphoreType.DMA((n_r, 3)), pltpu.SemaphoreType.DMA((n_r, 3))],
        args=tuple(gath) + tuple(red), carry=carry, body_starts_carry=True)
    return (outs[:n_g], outs[n_g:]), carried


STACKED = "stacked"


def _region(ref, shard_shape, col_sharded, pos):
    r, cdim = shard_shape
    d = _dev_index(pos)
    if col_sharded == STACKED:
        return ref.at[d]
    if col_sharded:
        return ref.at[:, pl.ds(pl.multiple_of(d * cdim, LANES), cdim)]
    return ref.at[pl.ds(pl.multiple_of(d * r, 2 * SUBLANES), r), :]


def _gather_carry(shards, col_sharded):
    n_w = len(shards)
    shapes = [tuple(s.shape) for s in shards]
    full_shapes = [(N_DEV,) + s if cs == STACKED else (s[0], s[1] * N_DEV) if cs else (s[0] * N_DEV, s[1])
                   for s, cs in zip(shapes, col_sharded)]

    def tools(out_refs, scr):
        send_sems, recv_sems = scr[n_w], scr[n_w + 1]
        me = _my_pos()
        x, y, c = me
        sibling = (x, y, 1 - c)
        chips = [(1 - x, y), (x, 1 - y), (1 - x, 1 - y)]

        def region(w, pos):
            return _region(out_refs[w], shapes[w], col_sharded[w], pos)

        def copy(w, k, block, to, src=None):
            return pltpu.make_async_remote_copy(
                src_ref=region(w, block) if src is None else src, dst_ref=region(w, block),
                send_sem=send_sems.at[w, k], recv_sem=recv_sems.at[w, k], device_id=to, device_id_type=MESH)

        def first(w):
            return [copy(w, 0, me, sibling, src=scr[w])] + [
                copy(w, 1 + j, me, (*chip, c), src=scr[w]) for j, chip in enumerate(chips)]

        def mine(w):
            return pltpu.make_async_copy(scr[w], region(w, me), scr[n_w + 2].at[w])

        return me, c, sibling, chips, copy, first, mine

    def start(ins, outs, scr):
        _, _, _, _, _, first, mine = tools(outs, scr)
        for w in range(n_w):
            scr[w][...] = ins[w][...].astype(BF16)
            for cp in first(w) + [mine(w)]:
                cp.start()

    def finish(ins, outs, scr):
        me, c, sibling, chips, copy, first, mine = tools(outs, scr)
        passed = []
        for w in range(n_w):
            for j, chip in enumerate(chips):
                copy(w, 1 + j, (*chip, c), me).wait_recv()
                fwd = copy(w, 4 + j, (*chip, c), sibling)
                fwd.start()
                passed.append(fwd)
        for w in range(n_w):
            copy(w, 0, sibling, me).wait_recv()
            for j, chip in enumerate(chips):
                copy(w, 4 + j, (*chip, 1 - c), me).wait_recv()
        for w in range(n_w):
            for cp in first(w):
                cp.wait_send()
            mine(w).wait()
        for cp in passed:
            cp.wait_send()

    return _Carry(
        inputs=list(shards), in_specs=[_whole()] * n_w,
        out_shape=[_sds(s, BF16) for s in full_shapes], out_specs=[_any()] * n_w,
        scratch=[pltpu.VMEM(s, BF16) for s in shapes]
        + [pltpu.SemaphoreType.DMA((n_w, N_DEV - 1)), pltpu.SemaphoreType.DMA((n_w, N_DEV - 1)),
           pltpu.SemaphoreType.DMA((n_w,))],
        start=start, finish=finish)


CHIP_FLIPS = (4, 2, 6)


def _two_level_scatter_carry(g_bf, g_own, col_sharded, mid_step):
    shape = tuple(g_own.shape)
    n = len(CHIP_FLIPS)

    def pair_copies(ins, scr):
        mine, sib, _, send_sems, recv_sems, local_sems = scr[:6]
        me = _my_pos()
        sibling = _flip(me, 1)

        def region(pos):
            return _region(ins[0], shape, col_sharded, pos)

        local = [pltpu.make_async_copy(region(_flip(me, f)), mine.at[s], local_sems.at[s])
                 for s, f in enumerate(CHIP_FLIPS)]
        sends = [pltpu.make_async_remote_copy(
            src_ref=region(_flip(sibling, f)), dst_ref=sib.at[s], send_sem=send_sems.at[s], recv_sem=recv_sems.at[s],
            device_id=sibling, device_id_type=MESH) for s, f in enumerate((0,) + CHIP_FLIPS)]
        return local, sends

    def chip_copies(outs, scr):
        h_out, chip_send, chip_recv = scr[2], scr[6], scr[7]
        me = _my_pos()
        return [pltpu.make_async_remote_copy(
            src_ref=h_out.at[j], dst_ref=outs[1].at[j], send_sem=chip_send.at[j], recv_sem=chip_recv.at[j],
            device_id=_flip(me, CHIP_FLIPS[j]), device_id_type=MESH) for j in range(n)]

    def start(ins, outs, scr):
        local, sends = pair_copies(ins, scr)
        for cp in local + sends:
            cp.start()

    def mid(ins, outs, scr):
        mine, sib, h_out = scr[:3]
        local, sends = pair_copies(ins, scr)
        for cp in local:
            cp.wait()
        for cp in sends:
            cp.wait_recv()
        outs[0][...] = ins[1][...] + sib[0].astype(F32)
        for s in range(n):
            h_out[s] = (mine[s].astype(F32) + sib[s + 1].astype(F32)).astype(BF16)
        for cp in chip_copies(outs, scr):
            cp.start()

    def finish(ins, outs, scr):
        cps = chip_copies(outs, scr)
        for cp in cps:
            cp.wait_recv()
        for cp in cps + pair_copies(ins, scr)[1]:
            cp.wait_send()

    return _Carry(
        inputs=[g_bf, g_own], in_specs=[_any(), _whole()],
        out_shape=[_sds(shape, F32), _sds((n,) + shape, BF16)], out_specs=[_whole(), _any()],
        scratch=[pltpu.VMEM((n,) + shape, BF16), pltpu.VMEM((n + 1,) + shape, BF16), pltpu.VMEM((n,) + shape, BF16),
                 pltpu.SemaphoreType.DMA((n + 1,)), pltpu.SemaphoreType.DMA((n + 1,)), pltpu.SemaphoreType.DMA((n,)),
                 pltpu.SemaphoreType.DMA((n,)), pltpu.SemaphoreType.DMA((n,))],
        start=start, finish=finish, mid=(mid_step, mid))


def _scatter_carry(grads_bf, shard_shapes, col_sharded, relations):
    n_w = len(grads_bf)
    shapes = [tuple(s) for s in shard_shapes]

    def copies(ins, outs, scr):
        send_sems, recv_sems = scr
        me = _my_pos()
        out = []
        for w in range(n_w):
            for i, k in enumerate(relations[w]):
                peer = _flip(me, k)
                out.append(pltpu.make_async_remote_copy(
                    src_ref=_region(ins[w], shapes[w], col_sharded[w], peer), dst_ref=outs[w].at[i],
                    send_sem=send_sems.at[w, i], recv_sem=recv_sems.at[w, i],
                    device_id=peer, device_id_type=MESH))
        return out

    def start(ins, outs, scr):
        for cp in copies(ins, outs, scr):
            cp.start()

    def finish(ins, outs, scr):
        cps = copies(ins, outs, scr)
        for cp in cps:
            cp.wait_recv()
        for cp in cps:
            cp.wait_send()

    return _Carry(
        inputs=list(grads_bf), in_specs=[_any()] * n_w,
        out_shape=[_sds((len(r),) + s, BF16) for r, s in zip(relations, shapes)], out_specs=[_any()] * n_w,
        scratch=[pltpu.SemaphoreType.DMA((n_w, N_DEV - 1)), pltpu.SemaphoreType.DMA((n_w, N_DEV - 1))],
        start=start, finish=finish)


def _block_diag(w):
    eye = jnp.eye(N_HEADS, dtype=w.dtype)
    return (eye[:, None, :, None] * w[:, :, None, :]).reshape(N_HEADS * HEAD_DIM, N_HEADS * HEAD_DIM)


def _local_step(x2, target, mod, w_in_f, w_full, conv_w_full, ffn_cw_full,
                g_mix_pre, g_mix_post, conv_b, w_rgate, b_rgate, w_igate, b_igate, lru_a, v_norm_g, v_norm_b,
                w_spatial, b_spatial, g_lru_out, g_gmlp_out, g_ffn_pre, g_ffn_post, ffn_conv_b,
                gather=None, scatter=None, adam=None, gate_bd=None):
    sh_m, sc_m, gt_m, sh_f, sc_f, gt_f = [mod[k] for k in range(N_MOD)]
    if gate_bd:
        wr_bd, wi_bd, b_r, b_i = gate_bd
    else:
        wr_bd, wi_bd = [_block_diag(w[0]).astype(BF16) for w in (w_rgate, w_igate)]
        b_r, b_i = b_rgate.reshape(1, LRU_W), b_igate.reshape(1, LRU_W)
    b_sp_t = b_spatial[0].T
    w_sp_t = jnp.swapaxes(w_spatial[0], 1, 2)

    def arriving(*names):
        return gather(*names) if gather else None

    near, far = (1, 2, 3, 4, 5), (6, 7)

    def leaving(*parts):
        return scatter(parts) if scatter else None

    def received(recv, parts, outs):
        for (name, _, _), out in zip(parts, outs):
            recv.setdefault(name, []).append(out)

    mix_params = (conv_w_full, conv_b, wr_bd, wi_bd, b_r, b_i, lru_a, v_norm_g, v_norm_b)
    w_out_f = w_full["w_out"]
    (z, h, ycat, hl, y, x1, h2), got = _mix_fwd(
        x2, sh_m, sc_m, g_mix_pre, w_in_f, *mix_params, w_spatial[0], b_sp_t, g_lru_out, g_gmlp_out,
        w_out_f, g_mix_post, gt_m, g_ffn_pre, sc_f, sh_f, carry=arriving("w_up"))
    w_up_f = got[0] if gather else w_full["w_up"]
    (up_pre, up, act), got = _ffn_fwd(h2, w_up_f, ffn_cw_full, ffn_conv_b, carry=arriving("w_down"))
    w_down_f = got[0] if gather else w_full["w_down"]
    d_y2, dout, loss_acc, vs_ffn = _ffn_tail(act, w_down_f, x1, gt_f, g_ffn_post, target)

    recv, updated = {}, {}

    def updating(grads):
        if not adam:
            return None
        return _adamw_rider([(adam[n][0], g[1], recv[n], adam[n][1], adam[n][2]) for n, g in grads.items()], N_DEV)

    def updates(grads, outs):
        for j, n in enumerate(grads):
            updated[n] = tuple(outs[4 * j:4 * j + 4])

    gw_down, _ = _wgrad(act, d_y2, "wgrad_down", by_rows=True)
    parts = [("w_down", gw_down[0], near + far)]
    (d_up, cs_ffn), got = _ffn_bwd(d_y2, up_pre, up, ffn_cw_full, w_down_f, carry=leaving(*parts))
    received(recv, parts, got)
    gw_up, got = _wgrad(h2, d_up, "wgrad_up", carry=updating(dict(w_down=gw_down)))
    updates(dict(w_down=gw_down), got)
    parts = [("w_up", gw_up[0], near)]
    (d_x1, d_y, d_ycat, vs_up), got = _up_bwd(
        d_up, w_up_f, x1, dout, y, w_out_f, g_ffn_pre, sc_f, g_mix_post, gt_m, carry=leaving(*parts))
    received(recv, parts, got)
    gw_out, _ = _wgrad(ycat, d_y, "wgrad_out", by_rows=True)
    parts = [("w_up", gw_up[0], far), ("w_out", gw_out[0], near + far)]
    (d_z, vs_mix, dcw, d_wr, d_wi, d_ws, d_bs), got = _mix_bwd(
        d_ycat, z, hl, *mix_params, w_spatial[0], w_sp_t, b_sp_t, g_lru_out, g_gmlp_out, carry=leaving(*parts))
    received(recv, parts, got)
    gw_in, got = _wgrad(h, d_z, "wgrad_in", carry=updating(dict(w_up=gw_up, w_out=gw_out)))
    updates(dict(w_up=gw_up, w_out=gw_out), got)
    in_bwd_steps = x2.shape[0] // min(TT_BIG, x2.shape[0])
    two_level = _two_level_scatter_carry(gw_in[0], gw_in[1], True, min(1, in_bwd_steps - 1)) if scatter else None
    (grad_x, vs_in), got = _in_bwd(d_z, w_in_f, x2, d_x1, g_mix_pre, sc_m, carry=two_level)
    if scatter:
        gw_in = (gw_in[0], got[0])
    recv["w_in"] = list(got[1:])

    gath = [vs_in, vs_up, vs_ffn, loss_acc]
    red = [cs_ffn, vs_mix, dcw, d_wr, d_wi, d_ws.reshape(N_GROUPS * POS_BLOCK, POS_BLOCK), d_bs]
    return dict(grad_x=grad_x, gath=gath, red=red, recv=recv, updated=updated,
                w_in=gw_in, w_out=gw_out, w_up=gw_up, w_down=gw_down)


def kernel(x, c, w_ada, b_ada, g_mix_pre, g_mix_post, w_in, conv_w, conv_b, w_rgate, b_rgate, w_igate, b_igate, lru_a, v_norm_g, v_norm_b, w_spatial, b_spatial, g_lru_out, g_gmlp_out, w_out, g_ffn_pre, g_ffn_post, w_up, ffn_conv_w, ffn_conv_b, w_down, loss_target, m_w_ada, m_b_ada, m_g_mix_pre, m_g_mix_post, m_w_in, m_conv_w, m_conv_b, m_w_rgate, m_b_rgate, m_w_igate, m_b_igate, m_lru_a, m_v_norm_g, m_v_norm_b, m_w_spatial, m_b_spatial, m_g_lru_out, m_g_gmlp_out, m_w_out, m_g_ffn_pre, m_g_ffn_post, m_w_up, m_ffn_conv_w, m_ffn_conv_b, m_w_down, v_w_ada, v_b_ada, v_g_mix_pre, v_g_mix_post, v_w_in, v_conv_w, v_conv_b, v_w_rgate, v_b_rgate, v_w_igate, v_b_igate, v_lru_a, v_v_norm_g, v_v_norm_b, v_w_spatial, v_b_spatial, v_g_lru_out, v_g_gmlp_out, v_w_out, v_g_ffn_pre, v_g_ffn_post, v_w_up, v_ffn_conv_w, v_ffn_conv_b, v_w_down):
    big_w = dict(w_in=(w_in, m_w_in, v_w_in, True), w_out=(w_out, m_w_out, v_w_out, False),
                 w_up=(w_up, m_w_up, v_w_up, True), w_down=(w_down, m_w_down, v_w_down, False))

    def gather(*names):
        return _gather_carry([big_w[n][0][0] for n in names], [STACKED if n == "w_up" else big_w[n][3] for n in names])

    def scatter(parts):
        return _scatter_carry([g for _, g, _ in parts], [big_w[n][0].shape[1:] for n, _, _ in parts],
                              [big_w[n][3] for n, _, _ in parts], [rel for _, _, rel in parts])

    ffn_cw_taps = tuple(a.reshape(FFN_CONV_K, 1, -1) for a in (ffn_conv_w, m_ffn_conv_w, v_ffn_conv_w))
    (c_all, conv_w_full, ffn_cw_full, *gate_bd, sh_m, sc_m, gt_m, sh_f, sc_f, gt_f), (w_in_f, w_out_f) = _prologue(
        c, conv_w[0], ffn_cw_taps[0], w_ada[0], b_ada, w_rgate[0], w_igate[0], b_rgate[0], b_igate[0],
        carry=gather("w_in", "w_out"))
    mod = (sh_m, sc_m, gt_m, sh_f, sc_f, gt_f)

    loc = _local_step(x[0], loss_target[0], mod, w_in_f, dict(w_out=w_out_f), conv_w_full, ffn_cw_full,
                      g_mix_pre, g_mix_post, conv_b, w_rgate, b_rgate, w_igate, b_igate, lru_a, v_norm_g, v_norm_b,
                      w_spatial, b_spatial, g_lru_out, g_gmlp_out, g_ffn_pre, g_ffn_post, ffn_conv_b,
                      gather=gather, scatter=scatter,
                      adam={n: big_w[n][:3] for n in ("w_out", "w_up", "w_down")}, gate_bd=gate_bd)
    grad_x = loc["grad_x"]

    (gathered, reduced), _ = _reduce_small(loc["gath"], loc["red"])

    results = dict(loc["updated"])
    w_, m_, v_, _ = big_w["w_in"]
    results["w_in"] = _adamw_sum(w_, loc["w_in"][1], loc["recv"]["w_in"], m_, v_, "adamw_w_in")

    params = dict(
        b_ada=(b_ada, m_b_ada, v_b_ada), g_mix_pre=(g_mix_pre, m_g_mix_pre, v_g_mix_pre),
        g_mix_post=(g_mix_post, m_g_mix_post, v_g_mix_post), conv_b=(conv_b, m_conv_b, v_conv_b),
        w_rgate=(w_rgate, m_w_rgate, v_w_rgate), b_rgate=(b_rgate, m_b_rgate, v_b_rgate),
        w_igate=(w_igate, m_w_igate, v_w_igate), b_igate=(b_igate, m_b_igate, v_b_igate),
        lru_a=(lru_a, m_lru_a, v_lru_a), v_norm_g=(v_norm_g, m_v_norm_g, v_v_norm_g),
        v_norm_b=(v_norm_b, m_v_norm_b, v_v_norm_b), w_spatial=(w_spatial, m_w_spatial, v_w_spatial),
        b_spatial=(b_spatial, m_b_spatial, v_b_spatial), g_lru_out=(g_lru_out, m_g_lru_out, v_g_lru_out),
        g_gmlp_out=(g_gmlp_out, m_g_gmlp_out, v_g_gmlp_out), g_ffn_pre=(g_ffn_pre, m_g_ffn_pre, v_g_ffn_pre),
        g_ffn_post=(g_ffn_post, m_g_ffn_post, v_g_ffn_post), ffn_conv_b=(ffn_conv_b, m_ffn_conv_b, v_ffn_conv_b))
    conv_params = dict(conv_w=(conv_w, m_conv_w, v_conv_w), ffn_conv_w=ffn_cw_taps)
    small_results, loss = _adamw_small(gathered, reduced, params, conv_params)
    results.update(small_results)
    results["ffn_conv_w"] = tuple(a.reshape(ffn_conv_w.shape) for a in results["ffn_conv_w"])
    loss = loss.reshape(())

    results["w_ada"] = _adamw_wada(c_all, gathered[0], gathered[1], gathered[2], w_ada, m_w_ada, v_w_ada)

    order = ["w_ada", "b_ada", "g_mix_pre", "g_mix_post", "w_in", "conv_w", "conv_b", "w_rgate", "b_rgate", "w_igate",
             "b_igate", "lru_a", "v_norm_g", "v_norm_b", "w_spatial", "b_spatial", "g_lru_out", "g_gmlp_out", "w_out",
             "g_ffn_pre", "g_ffn_post", "w_up", "ffn_conv_w", "ffn_conv_b", "w_down"]
    outs = [loss, grad_x[None]]
    for kind in range(4):
        outs += [results[n][kind] for n in order]
    return tuple(outs)
```

```python
import functools
import math

import jax
import jax.numpy as jnp
from jax import lax
from jax.experimental import pallas as pl
from jax.experimental.pallas import tpu as pltpu

F32 = jnp.float32
BF16 = jnp.bfloat16

D_MODEL = 1024
LRU_W = 512
GMLP_W = 512
N_HEADS = 8
HEAD_DIM = 64
N_GROUPS = 4
POS_BLOCK = 128
CHUNK = 64
IN_COLS = 2048
D_FF = 3072
N_MOD = 6
N_DEV = 8
EPS = 1e-6
LRU_C = 8.0
LRU_CONV_K = 4
FFN_CONV_K = 3

ADAM_LR = 0.001
ADAM_B1 = 0.9
ADAM_B2 = 0.999
ADAM_EPS = 1e-08
ADAM_WD = 0.01
ADAM_STEP = 10

LANES = 128
SUBLANES = 8
TT_BIG = 512
TT_MIX = 256
FF_CW = 1024
VMEM_LIMIT = 56 * 1024 * 1024

MESH = pl.DeviceIdType.MESH


def _sds(shape, dtype):
    return jax.ShapeDtypeStruct(shape, dtype)


def _cparams(sem=None):
    return pltpu.CompilerParams(dimension_semantics=sem, vmem_limit_bytes=VMEM_LIMIT)


def _whole():
    return pl.BlockSpec(memory_space=pltpu.VMEM)


def _const(shape):
    nd = len(shape)
    return pl.BlockSpec(shape, lambda *_: (0,) * nd)


def _any():
    return pl.BlockSpec(memory_space=pl.ANY)


class _Carry:
    def __init__(self, inputs, in_specs, out_shape, out_specs, scratch, start=None, finish=None, each=None, mid=None):
        self.inputs, self.in_specs, self.out_shape, self.out_specs = inputs, in_specs, out_shape, out_specs
        self.scratch, self.start, self.finish, self.each, self.mid = scratch, start, finish, each, mid


def _call(body, name, grid, in_specs, out_specs, out_shape, scratch, args, carry=None, body_starts_carry=False):
    n_in, n_out, n_scr = len(in_specs), len(out_specs), len(scratch)
    c_in = len(carry.in_specs) if carry else 0
    c_out = len(carry.out_specs) if carry else 0

    def full_body(*refs):
        ins = refs[:n_in]
        c_ins = refs[n_in:n_in + c_in]
        outs = refs[n_in + c_in:n_in + c_in + n_out]
        c_outs = refs[n_in + c_in + n_out:n_in + c_in + n_out + c_out]
        scr = refs[n_in + c_in + n_out + c_out:n_in + c_in + n_out + c_out + n_scr]
        c_scr = refs[n_in + c_in + n_out + c_out + n_scr:]
        if carry:
            first = functools.reduce(lambda a, b: a & b, [pl.program_id(d) == 0 for d in range(len(grid))])
            last = functools.reduce(lambda a, b: a & b, [pl.program_id(d) == g - 1 for d, g in enumerate(grid)])

        if carry and carry.start and not body_starts_carry:
            @pl.when(first)
            def _():
                carry.start(c_ins, c_outs, c_scr)

        if carry and carry.mid:
            @pl.when(pl.program_id(0) == carry.mid[0])
            def _():
                carry.mid[1](c_ins, c_outs, c_scr)

        if body_starts_carry:
            body(*ins, *outs, *scr, start_carry=(lambda: carry.start(c_ins, c_outs, c_scr)) if carry else (lambda: None))
        else:
            body(*ins, *outs, *scr)
        if carry and carry.each:
            carry.each(c_ins, c_outs, c_scr)
        if carry and carry.finish:
            @pl.when(last)
            def _():
                carry.finish(c_ins, c_outs, c_scr)

    res = pl.pallas_call(
        full_body, name=name, grid=grid,
        in_specs=list(in_specs) + (list(carry.in_specs) if carry else []),
        out_specs=list(out_specs) + (list(carry.out_specs) if carry else []),
        out_shape=list(out_shape) + (list(carry.out_shape) if carry else []),
        scratch_shapes=list(scratch) + (list(carry.scratch) if carry else []),
        compiler_params=_cparams(("arbitrary",) * len(grid)),
    )(*args, *(carry.inputs if carry else []))
    return res[:n_out], res[n_out:]


GELU_C0 = 0.7978845608028654
GELU_C1 = GELU_C0 * 0.044715


def _gelu(x):
    t = jnp.tanh(x * (GELU_C0 + GELU_C1 * (x * x)))
    hx = 0.5 * x
    return hx + hx * t


def _gelu_and_grad(x):
    x2 = x * x
    t = jnp.tanh(x * (GELU_C0 + GELU_C1 * x2))
    hx = 0.5 * x
    g = hx + hx * t
    dg = (0.5 + 0.5 * t) + hx * (1.0 - t * t) * (GELU_C0 + 3.0 * GELU_C1 * x2)
    return g, dg


def _sigmoid(x):
    return 1.0 / (1.0 + jnp.exp(-x))


def _softplus(x):
    return jnp.maximum(x, 0.0) + jnp.log1p(jnp.exp(-jnp.abs(x)))


def _neg_expm1(x):
    series = -x * (1.0 + x * (0.5 + x * (1.0 / 6.0 + x * (1.0 / 24.0 + x * (1.0 / 120.0)))))
    return jnp.where(x > -0.1, series, 1.0 - jnp.exp(x))


def _dot(a, b):
    return jnp.dot(a.astype(BF16), b.astype(BF16), preferred_element_type=F32)


def _dot_nt(a, b):
    return lax.dot_general(a.astype(BF16), b.astype(BF16), (((1,), (1,)), ((), ())), preferred_element_type=F32)


def _dot_tn(a, b):
    return lax.dot_general(a.astype(BF16), b.astype(BF16), (((0,), (0,)), ((), ())), preferred_element_type=F32)


def _rows(shape):
    return lax.broadcasted_iota(jnp.int32, shape, 0)


def _shift_down(cur, prev8, s):
    if s == 0:
        return cur
    n = cur.shape[0]
    r = pltpu.roll(cur, s, 0)
    p = pltpu.roll(prev8, s, 0)
    top = jnp.where(_rows(p.shape) < s, p, r[0:SUBLANES])
    if n == SUBLANES:
        return top
    return jnp.concatenate([top, r[SUBLANES:]], axis=0)


def _shift_up(cur, next8, s):
    if s == 0:
        return cur
    n = cur.shape[0]
    r = pltpu.roll(cur, n - s, 0)
    q = pltpu.roll(next8, SUBLANES - s, 0)
    bot = jnp.where(_rows(q.shape) >= SUBLANES - s, q, r[n - SUBLANES:])
    if n == SUBLANES:
        return bot
    return jnp.concatenate([r[:n - SUBLANES], bot], axis=0)


def _scan_fwd(a, b, h_in):
    n = a.shape[0]
    in_group = _rows(a.shape) & (SUBLANES - 1)
    s = 1
    while s < SUBLANES:
        a_s = pltpu.roll(a, s, 0)
        b_s = pltpu.roll(b, s, 0)
        m = in_group >= s
        b = jnp.where(m, a * b_s + b, b)
        a = jnp.where(m, a * a_s, a)
        s *= 2
    out, carry = [], h_in
    for g in range(n // SUBLANES):
        rows = slice(g * SUBLANES, (g + 1) * SUBLANES)
        h_g = a[rows] * carry + b[rows]
        out.append(h_g)
        carry = h_g[SUBLANES - 1:SUBLANES, :]
    return jnp.concatenate(out, axis=0)


def _scan_rev(a, b, l_in):
    n = a.shape[0]
    in_group = _rows(a.shape) & (SUBLANES - 1)
    s = 1
    while s < SUBLANES:
        a_s = pltpu.roll(a, n - s, 0)
        b_s = pltpu.roll(b, n - s, 0)
        m = in_group < SUBLANES - s
        b = jnp.where(m, b + a * b_s, b)
        a = jnp.where(m, a * a_s, a)
        s *= 2
    out, carry = [], l_in
    for g in reversed(range(n // SUBLANES)):
        rows = slice(g * SUBLANES, (g + 1) * SUBLANES)
        l_g = b[rows] + a[rows] * carry
        out.append(l_g)
        carry = l_g[0:1, :]
    return jnp.concatenate(out[::-1], axis=0)


def _rms(x):
    r = lax.rsqrt(jnp.mean(x * x, axis=-1, keepdims=True) + EPS)
    return x * r, r


def _rms_bwd(d_n, n, r):
    return r * (d_n - n * jnp.mean(d_n * n, axis=-1, keepdims=True))


def _colsum(x):
    return jnp.sum(x, axis=0, keepdims=True)


ROW_PIECE = 256


def _row_pieces(tt):
    return [slice(r, r + min(ROW_PIECE, tt)) for r in range(0, tt, min(ROW_PIECE, tt))]


def _lru_gates(xc, wr_ref, wi_ref, br, bi, sp_a):
    r = _sigmoid(_dot(xc, wr_ref[...]) + br)
    i = _sigmoid(_dot(xc, wi_ref[...]) + bi)
    la = -LRU_C * r * sp_a
    a = jnp.exp(la)
    mult = jnp.sqrt(_neg_expm1(2.0 * la))
    return r, i, a, mult


def _lru_conv(lx, prev8, cw_ref, cb):
    xc = cb + cw_ref[LRU_CONV_K - 1:LRU_CONV_K, :] * lx
    taps = []
    for k in range(LRU_CONV_K - 1):
        tap = _shift_down(lx, prev8, LRU_CONV_K - 1 - k)
        taps.append(tap)
        xc = xc + cw_ref[k:k + 1, :] * tap
    return xc, taps


def _ws_mask(transposed=False):
    i = lax.broadcasted_iota(jnp.int32, (POS_BLOCK, POS_BLOCK), 0)
    j = lax.broadcasted_iota(jnp.int32, (POS_BLOCK, POS_BLOCK), 1)
    if transposed:
        i, j = j, i
    return (j // CHUNK) <= (i // CHUNK)


def _gmlp_v(gv, vg, vb):
    av, dav = _gelu_and_grad(gv)
    mu = jnp.mean(av, axis=-1, keepdims=True)
    cen = av - mu
    rs = lax.rsqrt(jnp.mean(cen * cen, axis=-1, keepdims=True) + EPS)
    vhat = cen * rs
    return vhat * vg + vb, vhat, rs, dav


def _mix_fwd(x, sh, sc, g_pre, w_in, conv_w, conv_b, wr_bd, wi_bd, b_r, b_i, lru_a, vn_g, vn_b, w_sp, b_sp_t,
             g_lru, g_gmlp, w_out, g_post, gt_m, g_ffn_pre, sc_f, sh_f, carry=None):
    s_len = x.shape[0]
    tt = min(TT_MIX, s_len)
    nblk = tt // POS_BLOCK

    def body(x_ref, sh_ref, sc_ref, g_ref, w_ref, cw_ref, cb_ref, wr_ref, wi_ref, br_ref, bi_ref, la_ref, vg_ref,
             vb_ref, ws_ref, bst_ref, gl_ref, gg_ref, wo_ref, gp_ref, gtm_ref, g2_ref, scf_ref, shf_ref,
             z_ref, h_ref, y_ref, hl_ref, yo_ref, x1_ref, h2_ref, prev8, hcar):
        i = pl.program_id(0)

        @pl.when(i == 0)
        def _():
            prev8[...] = jnp.zeros_like(prev8)
            hcar[...] = jnp.zeros_like(hcar)

        n_x, _ = _rms(x_ref[...])
        h = (n_x * g_ref[...] * (1.0 + sc_ref[...]) + sh_ref[...]).astype(BF16)
        h_ref[...] = h
        z_ref[...] = jnp.dot(h, w_ref[...], preferred_element_type=F32)

        lx = z_ref[:, 0:LRU_W]
        gate = z_ref[:, LRU_W:2 * LRU_W]
        gu = z_ref[:, 2 * LRU_W:2 * LRU_W + GMLP_W]
        gv = z_ref[:, 2 * LRU_W + GMLP_W:]

        xc, _ = _lru_conv(lx, prev8[...], cw_ref, cb_ref[...])
        prev8[...] = lx[tt - SUBLANES:]
        sp_a = _softplus(-la_ref[...])
        _, ig, a, mult = _lru_gates(xc, wr_ref, wi_ref, br_ref[...], bi_ref[...], sp_a)
        bx = mult * (ig * xc)
        hl = _scan_fwd(a, bx, hcar[0:1, :])
        hcar[...] = jnp.broadcast_to(hl[tt - 1:tt, :], hcar.shape)
        hl_ref[...] = hl
        y_lru = hl * _gelu(gate)
        n_l, _ = _rms(y_lru)
        y_ref[:, 0:LRU_W] = (n_l * gl_ref[...]).astype(BF16)

        u = _gelu(gu)
        v, _, _, _ = _gmlp_v(gv, vg_ref[...], vb_ref[...])
        mask = _ws_mask()
        sp_parts = []
        for nb in range(nblk):
            row = []
            for g in range(N_GROUPS):
                wsm = jnp.where(mask, ws_ref[g], 0.0)
                vblk = v[nb * POS_BLOCK:(nb + 1) * POS_BLOCK, g * LANES:(g + 1) * LANES]
                row.append(_dot(wsm, vblk) + bst_ref[:, g:g + 1])
            sp_parts.append(jnp.concatenate(row, axis=1))
        sp = jnp.concatenate(sp_parts, axis=0) if nblk > 1 else sp_parts[0]
        n_g, _ = _rms(u * sp)
        y_ref[:, LRU_W:] = (n_g * gg_ref[...]).astype(BF16)

        y = jnp.dot(y_ref[...], wo_ref[...], preferred_element_type=F32)
        yo_ref[...] = y
        n_y, _ = _rms(y)
        x1 = x_ref[...] + gtm_ref[...] * (n_y * gp_ref[...])
        x1_ref[...] = x1
        n1, _ = _rms(x1)
        h2_ref[...] = (n1 * g2_ref[...] * (1.0 + scf_ref[...]) + shf_ref[...]).astype(BF16)

    row = lambda c: pl.BlockSpec((tt, c), lambda i: (i, 0))
    v512 = _const((1, LRU_W))
    vec = _const((1, D_MODEL))
    return _call(
        body, "mix_fwd", (s_len // tt,),
        in_specs=[row(D_MODEL), vec, vec, vec, _whole(),
                  _const((LRU_CONV_K, LRU_W)), v512, _whole(), _whole(), v512, v512, v512, v512, v512,
                  _whole(), _whole(), v512, v512, _whole(), vec, vec, vec, vec, vec],
        out_specs=[row(IN_COLS), row(D_MODEL), row(LRU_W + GMLP_W), row(LRU_W), row(D_MODEL), row(D_MODEL),
                   row(D_MODEL)],
        out_shape=[_sds((s_len, IN_COLS), F32), _sds((s_len, D_MODEL), BF16),
                   _sds((s_len, LRU_W + GMLP_W), BF16), _sds((s_len, LRU_W), F32),
                   _sds((s_len, D_MODEL), F32), _sds((s_len, D_MODEL), F32), _sds((s_len, D_MODEL), BF16)],
        scratch=[pltpu.VMEM((SUBLANES, LRU_W), F32), pltpu.VMEM((SUBLANES, LRU_W), F32)],
        args=(x, sh, sc, g_pre, w_in, conv_w, conv_b, wr_bd, wi_bd, b_r, b_i, lru_a, vn_g, vn_b, w_sp, b_sp_t,
              g_lru, g_gmlp, w_out, g_post, gt_m, g_ffn_pre, sc_f, sh_f), carry=carry)


FF_CHUNKS = N_DEV // 2
FF_CHUNK_W = D_FF // FF_CHUNKS


def _ffn_fwd(h2, w_up3, ffn_cw, ffn_cb, carry=None):
    s_len = h2.shape[0]
    tt = min(TT_BIG, s_len)
    nc, cw = FF_CHUNKS, FF_CHUNK_W

    def body(h2_ref, wu_ref, cwg_ref, cwv_ref, cbg_ref, cbv_ref, up_ref, upc_ref, act_ref, prev):
        i = pl.program_id(0)
        c = pl.program_id(1)

        @pl.when(i == 0)
        def _():
            prev[c] = jnp.zeros((2, SUBLANES, cw), F32)

        h2 = h2_ref[...]
        ug_pre = jnp.dot(h2, wu_ref[c], preferred_element_type=F32)
        uv_pre = jnp.dot(h2, wu_ref[nc + c], preferred_element_type=F32)
        up_ref[0] = ug_pre.astype(BF16)
        up_ref[1] = uv_pre.astype(BF16)
        ug, _ = _ffn_conv(ug_pre, prev[c, 0], cwg_ref, cbg_ref[...])
        uv, _ = _ffn_conv(uv_pre, prev[c, 1], cwv_ref, cbv_ref[...])
        prev[c, 0] = ug_pre[tt - SUBLANES:, :]
        prev[c, 1] = uv_pre[tt - SUBLANES:, :]
        upc_ref[0] = ug
        upc_ref[1] = uv
        act_ref[...] = (_gelu(ug) * uv).astype(BF16)

    chunk2 = pl.BlockSpec((2, tt, cw), lambda i, c: (0, i, c))
    ffn_cb2 = ffn_cb.reshape(1, 2 * D_FF)
    return _call(
        body, "ffn_fwd", (s_len // tt, nc),
        in_specs=[pl.BlockSpec((tt, D_MODEL), lambda i, c: (i, 0)), _whole(),
                  pl.BlockSpec((FFN_CONV_K, cw), lambda i, c: (0, c)),
                  pl.BlockSpec((FFN_CONV_K, cw), lambda i, c: (0, c + nc)),
                  pl.BlockSpec((1, cw), lambda i, c: (0, c)),
                  pl.BlockSpec((1, cw), lambda i, c: (0, c + nc))],
        out_specs=[chunk2, chunk2, pl.BlockSpec((tt, cw), lambda i, c: (i, c))],
        out_shape=[_sds((2, s_len, D_FF), BF16), _sds((2, s_len, D_FF), F32), _sds((s_len, D_FF), BF16)],
        scratch=[pltpu.VMEM((nc, 2, SUBLANES, cw), F32)],
        args=(h2, w_up3, ffn_cw, ffn_cw, ffn_cb2, ffn_cb2), carry=carry)


def _ffn_tail(act, w_down, x1, gt_f, g_post, target):
    s_len = x1.shape[0]
    tt = min(TT_BIG, s_len)

    def body(act_ref, wd_ref, x1_ref, gtf_ref, gp_ref, tg_ref, dy2_ref, dout_ref, loss_ref, vs_ref):
        @pl.when(pl.program_id(0) == 0)
        def _():
            loss_ref[...] = jnp.zeros_like(loss_ref)
            vs_ref[...] = jnp.zeros_like(vs_ref)

        for rows in _row_pieces(tt):
            n2, r2 = _rms(jnp.dot(act_ref[rows, :], wd_ref[...], preferred_element_type=F32))
            out = x1_ref[rows, :] + gtf_ref[...] * (n2 * gp_ref[...])
            err = out - tg_ref[rows, :]
            do = err * (1.0 / D_MODEL)
            dout_ref[rows, :] = do
            loss_ref[...] += jnp.broadcast_to(0.5 * jnp.sum(err * err, keepdims=True) * (1.0 / D_MODEL),
                                              loss_ref.shape)
            vs_ref[0:1, :] += _colsum(do * n2 * gp_ref[...])
            vs_ref[1:2, :] += _colsum(do * gtf_ref[...] * n2)
            dy2_ref[rows, :] = _rms_bwd(do * gtf_ref[...] * gp_ref[...], n2, r2).astype(BF16)

    row = lambda c: pl.BlockSpec((tt, c), lambda i: (i, 0))
    vec = _const((1, D_MODEL))
    outs, _ = _call(
        body, "ffn_tail", (s_len // tt,),
        in_specs=[row(D_FF), _whole(), row(D_MODEL), vec, vec, row(D_MODEL)],
        out_specs=[row(D_MODEL), row(D_MODEL), _const((SUBLANES, LANES)), _const((SUBLANES, D_MODEL))],
        out_shape=[_sds((s_len, D_MODEL), BF16), _sds((s_len, D_MODEL), F32), _sds((SUBLANES, LANES), F32),
                   _sds((SUBLANES, D_MODEL), F32)],
        scratch=[], args=(act, w_down, x1, gt_f, g_post, target))
    return outs


def _ffn_conv(up_pre, prev8, cw_ref, cb):
    up = cb + cw_ref[FFN_CONV_K - 1:FFN_CONV_K, :] * up_pre
    taps = []
    for k in range(FFN_CONV_K - 1):
        tap = _shift_down(up_pre, prev8, FFN_CONV_K - 1 - k)
        taps.append(tap)
        up = up + cw_ref[k:k + 1, :] * tap
    return up, taps


def _ffn_bwd(d_y2, up_pre, up, ffn_cw, w_down, carry=None):
    s_len = d_y2.shape[0]
    tt = min(TT_BIG, s_len)
    nt = s_len // tt
    cw = FF_CW
    nc = D_FF // cw

    def body(dy2_ref, up_ref, upc_ref, cwg_ref, cwv_ref, wd_ref, dup_ref, cs_ref, nxt, cs_acc):
        i = pl.program_id(0)
        c = pl.program_id(1)

        @pl.when(i == 0)
        def _():
            nxt[c] = jnp.zeros((2, SUBLANES, cw), F32)
            cs_acc[c] = jnp.zeros((2, SUBLANES, cw), F32)

        pw = 2 * LANES
        for piece in range(cw // pw):
            cols = slice(piece * pw, (piece + 1) * pw)
            d_act = _dot_nt(dy2_ref[...], wd_ref[pl.ds(pl.multiple_of(c * cw + piece * pw, pw), pw), :])
            uv = upc_ref[1, :, cols]
            gl, dgl = _gelu_and_grad(upc_ref[0, :, cols])
            d_ug = d_act * uv * dgl
            d_uv = d_act * gl
            for half, (d_u, cw_ref) in enumerate(((d_ug, cwg_ref), (d_uv, cwv_ref))):
                nx = nxt[c, half, :, cols]
                x_in = up_ref[half, :, cols].astype(F32)
                d_pre = cw_ref[FFN_CONV_K - 1:FFN_CONV_K, cols] * d_u
                sums = [None] * (FFN_CONV_K + 1)
                sums[FFN_CONV_K - 1] = _colsum(d_u * x_in)
                for k in range(FFN_CONV_K - 1):
                    ahead = _shift_up(d_u, nx, FFN_CONV_K - 1 - k)
                    d_pre = d_pre + cw_ref[k:k + 1, cols] * ahead
                    sums[k] = _colsum(ahead * x_in)
                sums[FFN_CONV_K] = _colsum(d_u)
                pad = jnp.zeros((SUBLANES - FFN_CONV_K - 1, pw), F32)
                cs_acc[c, half, :, cols] += jnp.concatenate(sums + [pad], axis=0)
                nxt[c, half, :, cols] = d_u[0:SUBLANES]
                dup_ref[half, :, cols] = d_pre.astype(BF16)

        for cc in range(nc):
            @pl.when((i == nt - 1) & (c == cc))
            def _():
                cs_ref[:, cc * cw:(cc + 1) * cw] = cs_acc[cc, 0]
                cs_ref[:, D_FF + cc * cw:D_FF + (cc + 1) * cw] = cs_acc[cc, 1]

    row = pl.BlockSpec((tt, D_MODEL), lambda i, c: (nt - 1 - i, 0))
    blk = pl.BlockSpec((2, tt, cw), lambda i, c: (0, nt - 1 - i, c))
    return _call(
        body, "ffn_bwd", (nt, nc),
        in_specs=[row, blk, blk,
                  pl.BlockSpec((FFN_CONV_K, cw), lambda i, c: (0, c)),
                  pl.BlockSpec((FFN_CONV_K, cw), lambda i, c: (0, c + nc)),
                  _whole()],
        out_specs=[blk, _const((SUBLANES, 2 * D_FF))],
        out_shape=[_sds((2, s_len, D_FF), BF16), _sds((SUBLANES, 2 * D_FF), F32)],
        scratch=[pltpu.VMEM((nc, 2, SUBLANES, cw), F32), pltpu.VMEM((nc, 2, SUBLANES, cw), F32)],
        args=(d_y2, up_pre, up, ffn_cw, ffn_cw, w_down), carry=carry)


def _up_bwd(d_up, w_up3, x1, dout, y, w_out, g_pre, sc_f, g_post, gt_m, carry=None):
    s_len = x1.shape[0]
    tt = min(TT_BIG, s_len)

    def body(du_ref, wu_ref, x1_ref, do_ref, y_ref, wo_ref, g2_ref, sc_ref, gp_ref, gt_ref,
             dx1_ref, dy_ref, dyc_ref, vs_ref):
        @pl.when(pl.program_id(0) == 0)
        def _():
            vs_ref[...] = jnp.zeros_like(vs_ref)

        for rows in _row_pieces(tt):
            d_h2 = jnp.zeros((rows.stop - rows.start, D_MODEL), F32)
            for half in range(2):
                for ch in range(FF_CHUNKS):
                    d_h2 = d_h2 + _dot_nt(du_ref[half, rows, ch * FF_CHUNK_W:(ch + 1) * FF_CHUNK_W],
                                          wu_ref[half * FF_CHUNKS + ch])
            n1, r1 = _rms(x1_ref[rows, :])
            ng = n1 * g2_ref[...]
            vs_ref[0:1, :] += _colsum(d_h2)
            vs_ref[1:2, :] += _colsum(d_h2 * ng)
            d_ng = d_h2 * (1.0 + sc_ref[...])
            vs_ref[2:3, :] += _colsum(d_ng * n1)
            d_x1 = do_ref[rows, :] + _rms_bwd(d_ng * g2_ref[...], n1, r1)
            dx1_ref[rows, :] = d_x1
            n_y, r_y = _rms(y_ref[rows, :])
            vs_ref[3:4, :] += _colsum(d_x1 * n_y * gp_ref[...])
            d_on = d_x1 * gt_ref[...]
            vs_ref[4:5, :] += _colsum(d_on * n_y)
            d_y = _rms_bwd(d_on * gp_ref[...], n_y, r_y).astype(BF16)
            dy_ref[rows, :] = d_y
            dyc_ref[rows, :] = _dot_nt(d_y, wo_ref[...])

    row = lambda c: pl.BlockSpec((tt, c), lambda i: (i, 0))
    vec = _const((1, D_MODEL))
    return _call(
        body, "up_bwd", (s_len // tt,),
        in_specs=[pl.BlockSpec((2, tt, D_FF), lambda i: (0, i, 0)), _whole(), row(D_MODEL), row(D_MODEL), row(D_MODEL),
                  _whole(), vec, vec, vec, vec],
        out_specs=[row(D_MODEL), row(D_MODEL), row(LRU_W + GMLP_W), _const((SUBLANES, D_MODEL))],
        out_shape=[_sds((s_len, D_MODEL), F32), _sds((s_len, D_MODEL), BF16), _sds((s_len, LRU_W + GMLP_W), F32),
                   _sds((SUBLANES, D_MODEL), F32)],
        scratch=[], args=(d_up, w_up3, x1, dout, y, w_out, g_pre, sc_f, g_post, gt_m), carry=carry)


def _head_pair_block(hd):
    return (slice((hd // 2) * HEAD_DIM, (hd // 2 + 1) * HEAD_DIM), slice((hd % 2) * HEAD_DIM, (hd % 2 + 1) * HEAD_DIM))


def _mix_bwd(d_ycat, z, hl, conv_w, conv_b, wr_bd, wi_bd, b_r, b_i, lru_a, vn_g, vn_b, w_sp, w_sp_t, b_sp_t,
             g_lru, g_gmlp, carry=None):
    s_len = z.shape[0]
    tt = min(TT_MIX, s_len)
    nt = s_len // tt
    nblk = tt // POS_BLOCK
    hb = tt // SUBLANES

    def body(dyc_ref, z_ref, zh_ref, hl_ref, hh_ref, cw_ref, cb_ref, wr_ref, wi_ref, br_ref, bi_ref, la_ref,
             vg_ref, vb_ref, ws_ref, wst_ref, bst_ref, gl_ref, gg_ref,
             dz_ref, vs_ref, dcw_ref, dwrb_ref, dwib_ref, dws_ref, dbs_ref, nxt_dxc, nxt_a, nxt_lam, dwr_ref, dwi_ref):
        i = pl.program_id(0)
        first_tile = i == nt - 1

        @pl.when(i == 0)
        def _():
            for ref in (vs_ref, dcw_ref, dwr_ref, dwi_ref, dws_ref, dbs_ref, nxt_dxc, nxt_a, nxt_lam):
                ref[...] = jnp.zeros_like(ref)

        lx = z_ref[:, 0:LRU_W]
        gate = z_ref[:, LRU_W:2 * LRU_W]
        gu = z_ref[:, 2 * LRU_W:2 * LRU_W + GMLP_W]
        gv = z_ref[:, 2 * LRU_W + GMLP_W:]
        prev8 = jnp.where(first_tile, 0.0, zh_ref[...])
        hprev8 = jnp.where(first_tile, 0.0, hh_ref[...])

        xc, taps = _lru_conv(lx, prev8, cw_ref, cb_ref[...])
        a_par = la_ref[...]
        sp_a = _softplus(-a_par)
        r, ig, a, mult = _lru_gates(xc, wr_ref, wi_ref, br_ref[...], bi_ref[...], sp_a)
        hl = hl_ref[...]
        h_prev = _shift_down(hl, hprev8, 1)
        ggate, dggate = _gelu_and_grad(gate)
        y_lru = hl * ggate
        n_l, r_l = _rms(y_lru)
        d_nl = dyc_ref[:, 0:LRU_W]
        vs_ref[6:7, :] += _colsum(d_nl * n_l)
        d_yl = _rms_bwd(d_nl * gl_ref[...], n_l, r_l)
        d_hl = d_yl * ggate
        d_gate = d_yl * hl * dggate
        a_up = _shift_up(a, nxt_a[...], 1)
        lam = _scan_rev(a_up, d_hl, nxt_lam[0:1, :])
        nxt_a[...] = jnp.broadcast_to(a[0:1, :], nxt_a.shape)
        nxt_lam[...] = jnp.broadcast_to(lam[0:1, :], nxt_lam.shape)
        ixc = ig * xc
        d_la = lam * h_prev * a - lam * ixc * (a * a) / mult
        d_i = lam * mult * xc
        d_xc = lam * mult * ig
        vs_ref[3:4, :] += _colsum(d_la * r) * (LRU_C * _sigmoid(-a_par))
        d_pr = d_la * (-LRU_C * sp_a) * r * (1.0 - r)
        d_pi = d_i * ig * (1.0 - ig)
        vs_ref[1:2, :] += _colsum(d_pr)
        vs_ref[2:3, :] += _colsum(d_pi)
        dwr_ref[...] += _dot_tn(xc, d_pr)
        dwi_ref[...] += _dot_tn(xc, d_pi)
        d_xc = d_xc + _dot_nt(d_pr, wr_ref[...]) + _dot_nt(d_pi, wi_ref[...])
        vs_ref[0:1, :] += _colsum(d_xc)
        nx = nxt_dxc[...]
        d_lx = cw_ref[LRU_CONV_K - 1:LRU_CONV_K, :] * d_xc
        dcw_ref[LRU_CONV_K - 1:LRU_CONV_K, :] += _colsum(d_xc * lx)
        for k in range(LRU_CONV_K - 1):
            d_lx = d_lx + cw_ref[k:k + 1, :] * _shift_up(d_xc, nx, LRU_CONV_K - 1 - k)
            dcw_ref[k:k + 1, :] += _colsum(d_xc * taps[k])
        nxt_dxc[...] = d_xc[0:SUBLANES]
        dz_ref[:, 0:LRU_W] = d_lx.astype(BF16)
        dz_ref[:, LRU_W:2 * LRU_W] = d_gate.astype(BF16)

        u, du = _gelu_and_grad(gu)
        v, vhat, rs, dav = _gmlp_v(gv, vg_ref[...], vb_ref[...])
        mask = _ws_mask()
        sp_parts = []
        for nb in range(nblk):
            rowp = []
            for g in range(N_GROUPS):
                wsm = jnp.where(mask, ws_ref[g], 0.0)
                vblk = v[nb * POS_BLOCK:(nb + 1) * POS_BLOCK, g * LANES:(g + 1) * LANES]
                rowp.append(_dot(wsm, vblk) + bst_ref[:, g:g + 1])
            sp_parts.append(jnp.concatenate(rowp, axis=1))
        sp = jnp.concatenate(sp_parts, axis=0) if nblk > 1 else sp_parts[0]
        y_g = u * sp
        n_g, r_g = _rms(y_g)
        d_ng = dyc_ref[:, LRU_W:]
        vs_ref[7:8, :] += _colsum(d_ng * n_g)
        d_yg = _rms_bwd(d_ng * gg_ref[...], n_g, r_g)
        d_gu = d_yg * sp * du
        d_sp = d_yg * u
        mask_t = _ws_mask(transposed=True)
        ones8 = jnp.ones((SUBLANES, LANES), F32)
        dv_parts = []
        for nb in range(nblk):
            rowp = []
            for g in range(N_GROUPS):
                rs_, cs_ = slice(nb * POS_BLOCK, (nb + 1) * POS_BLOCK), slice(g * LANES, (g + 1) * LANES)
                dsp_blk = d_sp[rs_, cs_]
                dbs_ref[g:g + 1, :] += lax.dot_general(
                    ones8, dsp_blk, (((1,), (1,)), ((), ())), preferred_element_type=F32,
                    precision=lax.Precision.HIGHEST)[0:1, :]
                dws_ref[g] += _dot_nt(dsp_blk, v[rs_, cs_])
                wsm_t = jnp.where(mask_t, wst_ref[g], 0.0)
                rowp.append(_dot(wsm_t, dsp_blk))
            dv_parts.append(jnp.concatenate(rowp, axis=1))
        d_v = jnp.concatenate(dv_parts, axis=0) if nblk > 1 else dv_parts[0]
        vs_ref[4:5, :] += _colsum(d_v * vhat)
        vs_ref[5:6, :] += _colsum(d_v)
        d_vh = d_v * vg_ref[...]
        d_av = rs * (d_vh - jnp.mean(d_vh, axis=-1, keepdims=True)
                     - vhat * jnp.mean(d_vh * vhat, axis=-1, keepdims=True))
        dz_ref[:, 2 * LRU_W:2 * LRU_W + GMLP_W] = d_gu.astype(BF16)
        dz_ref[:, 2 * LRU_W + GMLP_W:] = (d_av * dav).astype(BF16)

        @pl.when(i == nt - 1)
        def _():
            for hd in range(N_HEADS):
                blk = slice(hd * HEAD_DIM, (hd + 1) * HEAD_DIM)
                dwrb_ref[_head_pair_block(hd)] = dwr_ref[blk, blk]
                dwib_ref[_head_pair_block(hd)] = dwi_ref[blk, blk]
            for g in range(N_GROUPS):
                dws_ref[g] = jnp.where(mask, dws_ref[g], 0.0)

    rev = lambda c: pl.BlockSpec((tt, c), lambda i: (nt - 1 - i, 0))
    halo = pl.BlockSpec((SUBLANES, LRU_W), lambda i: (jnp.maximum((nt - 1 - i) * hb - 1, 0), 0))
    v512 = _const((1, LRU_W))
    return _call(
        body, "mix_bwd", (nt,),
        in_specs=[rev(LRU_W + GMLP_W), rev(IN_COLS), halo, rev(LRU_W), halo,
                  _const((LRU_CONV_K, LRU_W)), v512, _whole(), _whole(), v512, v512, v512, v512, v512,
                  _whole(), _whole(), _whole(), v512, v512],
        out_specs=[rev(IN_COLS), _const((SUBLANES, LRU_W)), _const((SUBLANES, LRU_W)),
                   _const((LRU_W // 2, 2 * HEAD_DIM)), _const((LRU_W // 2, 2 * HEAD_DIM)),
                   _const((N_GROUPS, POS_BLOCK, POS_BLOCK)), _const((SUBLANES, POS_BLOCK))],
        out_shape=[_sds((s_len, IN_COLS), BF16), _sds((SUBLANES, LRU_W), F32), _sds((SUBLANES, LRU_W), F32),
                   _sds((LRU_W // 2, 2 * HEAD_DIM), F32), _sds((LRU_W // 2, 2 * HEAD_DIM), F32),
                   _sds((N_GROUPS, POS_BLOCK, POS_BLOCK), F32), _sds((SUBLANES, POS_BLOCK), F32)],
        scratch=[pltpu.VMEM((SUBLANES, LRU_W), F32), pltpu.VMEM((SUBLANES, LRU_W), F32),
                 pltpu.VMEM((SUBLANES, LRU_W), F32), pltpu.VMEM((LRU_W, LRU_W), F32), pltpu.VMEM((LRU_W, LRU_W), F32)],
        args=(d_ycat, z, z, hl, hl, conv_w, conv_b, wr_bd, wi_bd, b_r, b_i, lru_a, vn_g, vn_b, w_sp, w_sp_t, b_sp_t,
              g_lru, g_gmlp), carry=carry)


def _in_bwd(d_z, w_in, x, d_x1, g, sc, carry=None):
    s_len = x.shape[0]
    tt = min(TT_BIG, s_len)

    def body(dz_ref, w_ref, x_ref, dx1_ref, g_ref, sc_ref, gx_ref, vs_ref):
        @pl.when(pl.program_id(0) == 0)
        def _():
            vs_ref[...] = jnp.zeros_like(vs_ref)

        for rows in _row_pieces(tt):
            d_h = _dot_nt(dz_ref[rows, :], w_ref[...])
            n, r = _rms(x_ref[rows, :])
            vs_ref[0:1, :] += _colsum(d_h)
            vs_ref[1:2, :] += _colsum(d_h * n * g_ref[...])
            d_ng = d_h * (1.0 + sc_ref[...])
            vs_ref[2:3, :] += _colsum(d_ng * n)
            gx_ref[rows, :] = dx1_ref[rows, :] + _rms_bwd(d_ng * g_ref[...], n, r)

    row = lambda c: pl.BlockSpec((tt, c), lambda i: (i, 0))
    vec = _const((1, D_MODEL))
    return _call(
        body, "in_bwd", (s_len // tt,),
        in_specs=[row(IN_COLS), _whole(), row(D_MODEL), row(D_MODEL), vec, vec],
        out_specs=[row(D_MODEL), _const((SUBLANES, D_MODEL))],
        out_shape=[_sds((s_len, D_MODEL), F32), _sds((SUBLANES, D_MODEL), F32)],
        scratch=[], args=(d_z, w_in, x, d_x1, g, sc), carry=carry)


def _wgrad(a, b, name, by_rows=False, carry=None):
    s_len, k_dim = a.shape
    halves = b.ndim == 3
    n_dim = b.shape[-1] * (2 if halves else 1)

    def body(a_ref, b_ref, ob_ref, own_ref):
        out = _dot_tn(a_ref[...], b_ref[0] if halves else b_ref[...])
        ob_ref[...] = out.astype(BF16)

        @pl.when(pl.program_id(0) == _dev_index(_my_pos()))
        def _():
            own_ref[...] = out

    if by_rows:
        tile = k_dim // N_DEV
        a_spec = pl.BlockSpec((s_len, tile), lambda j: (0, j))
        b_spec = pl.BlockSpec((s_len, n_dim), lambda j: (0, 0))
        o_spec = pl.BlockSpec((tile, n_dim), lambda j: (j, 0))
        own_shape = (tile, n_dim)
    else:
        tile = n_dim // N_DEV
        a_spec = pl.BlockSpec((s_len, k_dim), lambda j: (0, 0))
        if halves:
            per_half = N_DEV // 2
            b_spec = pl.BlockSpec((1, s_len, tile), lambda j: (j // per_half, 0, j % per_half))
        else:
            b_spec = pl.BlockSpec((s_len, tile), lambda j: (0, j))
        o_spec = pl.BlockSpec((k_dim, tile), lambda j: (0, j))
        own_shape = (k_dim, tile)
    return _call(
        body, name, (N_DEV,), in_specs=[a_spec, b_spec], out_specs=[o_spec, _const(own_shape)],
        out_shape=[_sds((k_dim, n_dim), BF16), _sds(own_shape, F32)],
        scratch=[], args=(a, b), carry=carry)


def _adam_math(w, g, m, v):
    m = ADAM_B1 * m + (1.0 - ADAM_B1) * g
    v = ADAM_B2 * v + (1.0 - ADAM_B2) * (g * g)
    m_hat = m / (1.0 - ADAM_B1 ** ADAM_STEP)
    v_hat = v / (1.0 - ADAM_B2 ** ADAM_STEP)
    delta = -ADAM_LR * (m_hat / (jnp.sqrt(v_hat) + ADAM_EPS) + ADAM_WD * w)
    return delta, m, v


def _row_tile(rows, cols, n_f32_arrays):
    budget = VMEM_LIMIT // 2
    tr = rows
    while tr % 2 == 0 and tr // 2 >= SUBLANES and (tr // 2) % SUBLANES == 0 and tr * cols * 4 * n_f32_arrays * 2 > budget:
        tr //= 2
    return tr


def _adamw_sum_block(w_ref, g_ref, r_refs, m_ref, v_ref, go_ref, d_ref, mo_ref, vo_ref):
    g = g_ref[...]
    for r_ref in r_refs:
        for k in range(r_ref.shape[0]):
            g = g + r_ref[k].astype(F32)
    go_ref[0] = g
    d_ref[0], mo_ref[0], vo_ref[0] = _adam_math(w_ref[0], g, m_ref[0], v_ref[0])


def _adamw_rider(parts, steps):
    inputs, in_specs, out_shape, out_specs, n_recvs = [], [], [], [], []
    for w, g_own, recv, m, v in parts:
        _, rows, cols = w.shape
        tr = rows // steps
        blk = pl.BlockSpec((1, tr, cols), lambda i: (0, i, 0))
        inputs += [w, g_own, *recv, m, v]
        in_specs += ([blk, pl.BlockSpec((tr, cols), lambda i: (i, 0))]
                     + [pl.BlockSpec((r.shape[0], tr, cols), lambda i: (0, i, 0)) for r in recv] + [blk, blk])
        out_shape += [_sds((1, rows, cols), F32)] * 4
        out_specs += [blk] * 4
        n_recvs.append(len(recv))

    def each(ins, outs, scr):
        for n_recv in n_recvs:
            _adamw_sum_block(ins[0], ins[1], ins[2:2 + n_recv], ins[2 + n_recv], ins[3 + n_recv], *outs[:4])
            ins, outs = ins[4 + n_recv:], outs[4:]

    return _Carry(inputs=inputs, in_specs=in_specs, out_shape=out_shape, out_specs=out_specs, scratch=[], each=each)


def _adamw_sum(w, g_own, recv, m, v, name):
    _, rows, cols = w.shape
    n_recv = len(recv)
    tr = min(_row_tile(rows, cols, 10), rows // N_DEV)
    nb = rows // tr

    def body(w_ref, g_ref, *rest):
        _adamw_sum_block(w_ref, g_ref, rest[:n_recv], *rest[n_recv:])

    blk = pl.BlockSpec((1, tr, cols), lambda i: (0, i, 0))
    return pl.pallas_call(
        body, name=name, grid=(nb,),
        in_specs=[blk, pl.BlockSpec((tr, cols), lambda i: (i, 0))]
        + [pl.BlockSpec((r.shape[0], tr, cols), lambda i: (0, i, 0)) for r in recv] + [blk, blk],
        out_specs=[blk] * 4, out_shape=[_sds((1, rows, cols), F32)] * 4,
        compiler_params=_cparams(("arbitrary",)),
    )(w, g_own, *recv, m, v)


def _row_of_each(ref, row):
    cols = ref.shape[1]
    rows = _rows((N_DEV, cols))
    out = jnp.zeros((N_DEV, cols), F32)
    for d in range(N_DEV):
        picked = ref[d * SUBLANES + row:d * SUBLANES + row + 1, :]
        out = jnp.where(rows == d, jnp.broadcast_to(picked, (N_DEV, cols)), out)
    return out


def _my_columns(full, width, me):
    out = jnp.zeros(full.shape[:-1] + (width,), F32)
    for d in range(N_DEV):
        out = out + jnp.where(me == d, full[:, d * width:(d + 1) * width], 0.0)
    return out


def _adamw_wada(c_all, vs_in_all, vs_up_all, vs_ffn_all, w, m, v):
    _, rows, cols = w.shape

    def body(c_ref, vi_ref, vu_ref, vf_ref, w_ref, m_ref, v_ref, go_ref, d_ref, mo_ref, vo_ref):
        me = _dev_index(_my_pos())
        cv = _row_of_each(c_ref, 0)
        ca = cv * _sigmoid(cv)
        dmod = jnp.concatenate([_row_of_each(vi_ref, 0), _row_of_each(vi_ref, 1), _row_of_each(vu_ref, 3),
                                _row_of_each(vu_ref, 0), _row_of_each(vu_ref, 1), _row_of_each(vf_ref, 0)], axis=1)
        dm = _my_columns(dmod, cols, me)
        g = lax.dot_general(ca, dm, (((0,), (0,)), ((), ())), preferred_element_type=F32,
                            precision=lax.Precision.HIGHEST)
        go_ref[0] = g
        d_ref[0], mo_ref[0], vo_ref[0] = _adam_math(w_ref[0], g, m_ref[0], v_ref[0])

    tr = rows // N_DEV
    blk = pl.BlockSpec((1, tr, cols), lambda i: (0, i, 0))
    full = _const((N_DEV * SUBLANES, D_MODEL))
    return pl.pallas_call(
        body, name="adamw_w_ada", grid=(rows // tr,), out_shape=[_sds((1, rows, cols), F32)] * 4,
        in_specs=[pl.BlockSpec((N_DEV * SUBLANES, tr), lambda i: (0, i)), full, full, full, blk, blk, blk],
        out_specs=[blk] * 4, compiler_params=_cparams(("arbitrary",)),
    )(c_all, vs_in_all, vs_up_all, vs_ffn_all, w, m, v)


def _adamw_small(gathered, reduced, params, conv_params):
    names = list(params) + list(conv_params)
    allp = {**params, **conv_params}
    n_g = len(gathered) + len(reduced)

    def body(*refs):
        g_refs = refs[:n_g]
        p_refs = refs[n_g:n_g + 3 * len(names)]
        o_refs = refs[n_g + 3 * len(names):]
        me = _dev_index(_my_pos())

        def total(ref):
            s = ref[0:SUBLANES, :]
            for d in range(1, N_DEV):
                s = s + ref[d * SUBLANES:(d + 1) * SUBLANES, :]
            return s

        vs_in, vs_up, vs_ffn, loss = [total(r) for r in g_refs[:4]]
        cs, vs_mix, dcw, dwr, dwi, dws, dbs = [r[...] for r in g_refs[4:]]
        o_refs[-1][...] = loss[0:1, 0:1]
        mine = lambda full, width: _my_columns(full, width, me)

        all_ = (slice(None), slice(None))
        heads = lambda row: [((0, slice(h, h + 1), slice(None)), row[:, h * HEAD_DIM:(h + 1) * HEAD_DIM])
                             for h in range(N_HEADS)]
        blocks = lambda pairs: [((0, h), pairs[_head_pair_block(h)]) for h in range(N_HEADS)]
        pieces = {
            "b_ada": [((slice(None), slice(k * D_MODEL, (k + 1) * D_MODEL)), row) for k, row in enumerate(
                (vs_in[0:1], vs_in[1:2], vs_up[3:4], vs_up[0:1], vs_up[1:2], vs_ffn[0:1]))],
            "g_mix_pre": [(all_, vs_in[2:3])], "g_mix_post": [(all_, vs_up[4:5])],
            "g_ffn_pre": [(all_, vs_up[2:3])], "g_ffn_post": [(all_, vs_ffn[1:2])],
            "conv_b": [(all_, vs_mix[0:1])], "b_rgate": heads(vs_mix[1:2]), "b_igate": heads(vs_mix[2:3]),
            "lru_a": [(all_, vs_mix[3:4])], "v_norm_g": [(all_, vs_mix[4:5])], "v_norm_b": [(all_, vs_mix[5:6])],
            "g_lru_out": [(all_, vs_mix[6:7])], "g_gmlp_out": [(all_, vs_mix[7:8])],
            "w_rgate": blocks(dwr), "w_igate": blocks(dwi),
            "w_spatial": [((0, g), dws[g * POS_BLOCK:(g + 1) * POS_BLOCK, :]) for g in range(N_GROUPS)],
            "b_spatial": [((0,), dbs[0:N_GROUPS])],
            "ffn_conv_b": [(all_, cs[FFN_CONV_K:FFN_CONV_K + 1])],
            "conv_w": [((0,), mine(dcw[0:LRU_CONV_K], LRU_W // N_DEV))],
        }
        ffn_cw_rows = mine(cs[0:FFN_CONV_K], 2 * D_FF // N_DEV)
        pieces["ffn_conv_w"] = [((k,), ffn_cw_rows[k:k + 1]) for k in range(FFN_CONV_K)]
        for n_i, name in enumerate(names):
            w_ref, m_ref, v_ref = p_refs[3 * n_i:3 * n_i + 3]
            go_ref, d_ref, mo_ref, vo_ref = o_refs[4 * n_i:4 * n_i + 4]
            for idx, g in pieces[name]:
                go_ref[idx] = g
                d_ref[idx], mo_ref[idx], vo_ref[idx] = _adam_math(w_ref[idx], g, m_ref[idx], v_ref[idx])

    flat_params = [a for n in names for a in allp[n]]
    out_shape = [_sds(allp[n][0].shape, F32) for n in names for _ in range(4)] + [_sds((1, 1), F32)]
    outs = pl.pallas_call(
        body, name="adamw_small", out_shape=out_shape,
        in_specs=[_whole()] * (n_g + len(flat_params)), out_specs=[_whole()] * len(out_shape),
        compiler_params=_cparams(),
    )(*gathered, *reduced, *flat_params)
    return {n: outs[4 * i:4 * i + 4] for i, n in enumerate(names)}, outs[-1]


def _my_pos():
    return lax.axis_index("x"), lax.axis_index("y"), lax.axis_index("c")


def _flip(pos, k):
    x, y, c = pos
    return (1 - x if k & 4 else x, 1 - y if k & 2 else y, 1 - c if k & 1 else c)


def _dev_index(pos):
    x, y, c = pos
    return 4 * x + 2 * y + c


def _all_gather_small(ins, outs, send_sems, recv_sems, meanwhile=None):
    n = len(ins)
    me = _my_pos()

    def slot(a, pos):
        rows = ins[a].shape[0]
        return outs[a].at[pl.ds(pl.multiple_of(_dev_index(pos) * rows, SUBLANES), rows), :]

    def copy(a, k, block):
        return pltpu.make_async_remote_copy(
            src_ref=ins[a], dst_ref=slot(a, block), send_sem=send_sems.at[a, k - 1], recv_sem=recv_sems.at[a, k - 1],
            device_id=_flip(me, k), device_id_type=MESH)

    sends = [copy(a, k, me) for a in range(n) for k in range(1, N_DEV)]
    for cp in sends:
        cp.start()
    for a in range(n):
        rows = ins[a].shape[0]
        outs[a][pl.ds(pl.multiple_of(_dev_index(me) * rows, SUBLANES), rows), :] = ins[a][...]
    if meanwhile:
        meanwhile()
    for a in range(n):
        for k in range(1, N_DEV):
            copy(a, k, _flip(me, k)).wait_recv()
    for cp in sends:
        cp.wait_send()


def _prologue(c, cw, fcw, w_ada, b_ada, w_rgate, w_igate, b_rgate, b_igate, carry):
    cols = w_ada.shape[1]
    cw_w, fcw_w = cw.shape[-1], fcw.shape[-1]
    step = math.gcd(cols, D_MODEL)

    def body(c_ref, cw_ref, fcw_ref, w_ref, b_ref, wr_ref, wi_ref, br_ref, bi_ref,
             call_ref, cwf_ref, fcwf_ref, wrbd_ref, wibd_ref, brow_ref, birow_ref, *rest, start_carry):
        mod_refs, (mod_scr, c8, cw8, fcw8, cwall, fcwall, modall, s1, r1, s2, r2) = rest[:N_MOD], rest[N_MOD:]
        me = _dev_index(_my_pos())
        c8[...] = jnp.broadcast_to(c_ref[...], c8.shape)
        cw8[...] = jnp.zeros(cw8.shape, F32)
        cw8[0:LRU_CONV_K, :] = cw_ref[...]
        fcw8[...] = jnp.zeros(fcw8.shape, F32)
        for k in range(FFN_CONV_K):
            fcw8[k:k + 1, :] = fcw_ref[k]

        def block_diagonals():
            for bd_ref, hb_ref in ((wrbd_ref, wr_ref), (wibd_ref, wi_ref)):
                bd_ref[...] = jnp.zeros(bd_ref.shape, BF16)
                for h in range(N_HEADS):
                    span = slice(h * HEAD_DIM, (h + 1) * HEAD_DIM)
                    bd_ref[span, span] = hb_ref[h].astype(BF16)
            for row_ref, hb_ref in ((brow_ref, br_ref), (birow_ref, bi_ref)):
                for h in range(N_HEADS):
                    row_ref[:, h * HEAD_DIM:(h + 1) * HEAD_DIM] = hb_ref[h:h + 1, :]

        def conv_weights():
            for d in range(N_DEV):
                cwf_ref[:, d * cw_w:(d + 1) * cw_w] = cwall[d * SUBLANES:d * SUBLANES + LRU_CONV_K, :]
                fcwf_ref[:, d * fcw_w:(d + 1) * fcw_w] = fcwall[d * SUBLANES:d * SUBLANES + FFN_CONV_K, :]

        _all_gather_small([c8, cw8, fcw8], [call_ref, cwall, fcwall], s1, r1, meanwhile=block_diagonals)
        start_carry()
        cv = _row_of_each(call_ref, 0)
        ca = cv * _sigmoid(cv)
        b_cols = _my_columns(b_ref[...], cols, me)
        mod_scr[...] = jnp.dot(ca, w_ref[...], preferred_element_type=F32, precision=lax.Precision.HIGHEST) + b_cols
        _all_gather_small([mod_scr], [modall], s2, r2, meanwhile=conv_weights)
        mine = _rows((N_DEV, cols)) == me
        for d in range(N_DEV):
            piece = jnp.sum(jnp.where(mine, modall[d * N_DEV:(d + 1) * N_DEV, :], 0.0), axis=0, keepdims=True)
            for t in range(cols // step):
                at = d * cols + t * step
                mod_refs[at // D_MODEL][:, at % D_MODEL:at % D_MODEL + step] = piece[:, t * step:(t + 1) * step]

    sem = lambda n: pltpu.SemaphoreType.DMA((n, N_DEV - 1))
    gate = N_HEADS * HEAD_DIM
    return _call(
        body, "prologue", (1,), in_specs=[_whole()] * 9, out_specs=[_whole()] * (7 + N_MOD),
        out_shape=[_sds((N_DEV * SUBLANES, c.shape[-1]), F32), _sds((LRU_CONV_K, N_DEV * cw_w), F32),
                   _sds((FFN_CONV_K, N_DEV * fcw_w), F32), _sds((gate, gate), BF16), _sds((gate, gate), BF16),
                   _sds((1, gate), F32), _sds((1, gate), F32)] + [_sds((1, D_MODEL), F32)] * N_MOD,
        scratch=[pltpu.VMEM((N_DEV, cols), F32)] + [pltpu.VMEM((SUBLANES, a.shape[-1]), F32) for a in (c, cw, fcw)]
        + [pltpu.VMEM((N_DEV * SUBLANES, cw_w), F32), pltpu.VMEM((N_DEV * SUBLANES, fcw_w), F32),
           pltpu.VMEM((N_DEV * N_DEV, cols), F32), sem(3), sem(3), sem(1), sem(1)],
        args=(c, cw, fcw, w_ada, b_ada, w_rgate, w_igate, b_rgate, b_igate), carry=carry, body_starts_carry=True)


def _reduce_small(gath, red, carry=None):
    n_g, n_r = len(gath), len(red)
    chip_flips = CHIP_FLIPS

    def body(*refs, start_carry):
        g_in, r_in = refs[:n_g], refs[n_g:n_g + n_r]
        g_out, r_out = refs[n_g + n_r:2 * n_g + n_r], refs[2 * n_g + n_r:2 * (n_g + n_r)]
        scr = refs[2 * (n_g + n_r):]
        sib, land = scr[:n_r], scr[n_r:2 * n_r]
        g_send, g_recv, s_send, s_recv, i_send, i_recv, f_send, f_recv = scr[2 * n_r:]
        me = _my_pos()
        c = me[2]
        sibling = _flip(me, 1)

        def slot(a, pos):
            return g_out[a].at[pl.ds(pl.multiple_of(_dev_index(pos) * SUBLANES, SUBLANES), SUBLANES), :]

        def gcopy(a, k):
            return pltpu.make_async_remote_copy(
                src_ref=g_in[a], dst_ref=slot(a, me), send_sem=g_send.at[a, k - 1], recv_sem=g_recv.at[a, k - 1],
                device_id=_flip(me, k), device_id_type=MESH)

        def scopy(a):
            return pltpu.make_async_remote_copy(
                src_ref=r_in[a], dst_ref=sib[a], send_sem=s_send.at[a], recv_sem=s_recv.at[a],
                device_id=sibling, device_id_type=MESH)

        def icopy(a, j):
            return pltpu.make_async_remote_copy(
                src_ref=r_out[a], dst_ref=land[a].at[j], send_sem=i_send.at[a, j], recv_sem=i_recv.at[a, j],
                device_id=_flip(me, chip_flips[j]), device_id_type=MESH)

        def fcopy(a, j):
            return pltpu.make_async_remote_copy(
                src_ref=land[a].at[j], dst_ref=land[a].at[j], send_sem=f_send.at[a, j], recv_sem=f_recv.at[a, j],
                device_id=sibling, device_id_type=MESH)

        gathers = [gcopy(a, k) for a in range(n_g) for k in range(1, N_DEV)]
        swaps = [scopy(a) for a in range(n_r)]
        for cp in gathers + swaps:
            cp.start()
        for a in range(n_g):
            g_out[a][pl.ds(pl.multiple_of(_dev_index(me) * SUBLANES, SUBLANES), SUBLANES), :] = g_in[a][...]
        for a in range(n_r):
            swaps[a].wait_recv()
            r_out[a][...] = r_in[a][...] + sib[a][...]

        for core in range(2):
            @pl.when(c == core)
            def _():
                for a in range(core, n_r, 2):
                    for j in range(3):
                        icopy(a, j).start()

        start_carry()

        for core in range(2):
            mine = [a for a in range(n_r) if a % 2 == core]
            theirs = [a for a in range(n_r) if a % 2 != core]

            @pl.when(c == core)
            def _():
                out = [icopy(a, j) for a in mine for j in range(3)]
                fwd = []
                for a in mine:
                    for j in range(3):
                        icopy(a, j).wait_recv()
                        cp = fcopy(a, j)
                        cp.start()
                        fwd.append(cp)
                for a in theirs:
                    for j in range(3):
                        fcopy(a, j).wait_recv()
                for cp in out + fwd:
                    cp.wait_send()

        for a in range(n_r):
            r_out[a][...] = (r_out[a][...] + land[a][1]) + (land[a][0] + land[a][2])
        for a in range(n_g):
            for k in range(1, N_DEV):
                pltpu.make_async_remote_copy(
                    src_ref=g_in[a], dst_ref=slot(a, _flip(me, k)), send_sem=g_send.at[a, k - 1],
                    recv_sem=g_recv.at[a, k - 1], device_id=_flip(me, k), device_id_type=MESH).wait_recv()
        for cp in gathers + swaps:
            cp.wait_send()

    shapes = [tuple(a.shape) for a in red]
    outs, carried = _call(
        body, "reduce_small", (1,), in_specs=[_whole()] * (n_g + n_r), out_specs=[_whole()] * (n_g + n_r),
        out_shape=[_sds((N_DEV * SUBLANES, a.shape[1]), F32) for a in gath] + [_sds(s, F32) for s in shapes],
        scratch=[pltpu.VMEM(s, F32) for s in shapes] + [pltpu.VMEM((3,) + s, F32) for s in shapes]
        + [pltpu.SemaphoreType.DMA((n_g, N_DEV - 1)), pltpu.SemaphoreType.DMA((n_g, N_DEV - 1)),
           pltpu.SemaphoreType.DMA((n_r,)), pltpu.SemaphoreType.DMA((n_r,)),
           pltpu.SemaphoreType.DMA((n_r, 3)), pltpu.SemaphoreType.DMA((n_r, 3)),
           pltpu.SemaphoreType.DMA((n_r, 3)), pltpu.SemaphoreType.DMA((n_r, 3))],
        args=tuple(gath) + tuple(red), carry=carry, body_starts_carry=True)
    return (outs[:n_g], outs[n_g:]), carried


STACKED = "stacked"


def _region(ref, shard_shape, col_sharded, pos):
    r, cdim = shard_shape
    d = _dev_index(pos)
    if col_sharded == STACKED:
        return ref.at[d]
    if col_sharded:
        return ref.at[:, pl.ds(pl.multiple_of(d * cdim, LANES), cdim)]
    return ref.at[pl.ds(pl.multiple_of(d * r, 2 * SUBLANES), r), :]


def _gather_carry(shards, col_sharded):
    n_w = len(shards)
    shapes = [tuple(s.shape) for s in shards]
    full_shapes = [(N_DEV,) + s if cs == STACKED else (s[0], s[1] * N_DEV) if cs else (s[0] * N_DEV, s[1])
                   for s, cs in zip(shapes, col_sharded)]

    def tools(out_refs, scr):
        send_sems, recv_sems = scr[n_w], scr[n_w + 1]
        me = _my_pos()
        x, y, c = me
        sibling = (x, y, 1 - c)
        chips = [(1 - x, y), (x, 1 - y), (1 - x, 1 - y)]

        def region(w, pos):
            return _region(out_refs[w], shapes[w], col_sharded[w], pos)

        def copy(w, k, block, to, src=None):
            return pltpu.make_async_remote_copy(
                src_ref=region(w, block) if src is None else src, dst_ref=region(w, block),
                send_sem=send_sems.at[w, k], recv_sem=recv_sems.at[w, k], device_id=to, device_id_type=MESH)

        def first(w):
            return [copy(w, 0, me, sibling, src=scr[w])] + [
                copy(w, 1 + j, me, (*chip, c), src=scr[w]) for j, chip in enumerate(chips)]

        def mine(w):
            return pltpu.make_async_copy(scr[w], region(w, me), scr[n_w + 2].at[w])

        return me, c, sibling, chips, copy, first, mine

    def start(ins, outs, scr):
        _, _, _, _, _, first, mine = tools(outs, scr)
        for w in range(n_w):
            scr[w][...] = ins[w][...].astype(BF16)
            for cp in first(w) + [mine(w)]:
                cp.start()

    def finish(ins, outs, scr):
        me, c, sibling, chips, copy, first, mine = tools(outs, scr)
        passed = []
        for w in range(n_w):
            for j, chip in enumerate(chips):
                copy(w, 1 + j, (*chip, c), me).wait_recv()
                fwd = copy(w, 4 + j, (*chip, c), sibling)
                fwd.start()
                passed.append(fwd)
        for w in range(n_w):
            copy(w, 0, sibling, me).wait_recv()
            for j, chip in enumerate(chips):
                copy(w, 4 + j, (*chip, 1 - c), me).wait_recv()
        for w in range(n_w):
            for cp in first(w):
                cp.wait_send()
            mine(w).wait()
        for cp in passed:
            cp.wait_send()

    return _Carry(
        inputs=list(shards), in_specs=[_whole()] * n_w,
        out_shape=[_sds(s, BF16) for s in full_shapes], out_specs=[_any()] * n_w,
        scratch=[pltpu.VMEM(s, BF16) for s in shapes]
        + [pltpu.SemaphoreType.DMA((n_w, N_DEV - 1)), pltpu.SemaphoreType.DMA((n_w, N_DEV - 1)),
           pltpu.SemaphoreType.DMA((n_w,))],
        start=start, finish=finish)


CHIP_FLIPS = (4, 2, 6)


def _two_level_scatter_carry(g_bf, g_own, col_sharded, mid_step):
    shape = tuple(g_own.shape)
    n = len(CHIP_FLIPS)

    def pair_copies(ins, scr):
        mine, sib, _, send_sems, recv_sems, local_sems = scr[:6]
        me = _my_pos()
        sibling = _flip(me, 1)

        def region(pos):
            return _region(ins[0], shape, col_sharded, pos)

        local = [pltpu.make_async_copy(region(_flip(me, f)), mine.at[s], local_sems.at[s])
                 for s, f in enumerate(CHIP_FLIPS)]
        sends = [pltpu.make_async_remote_copy(
            src_ref=region(_flip(sibling, f)), dst_ref=sib.at[s], send_sem=send_sems.at[s], recv_sem=recv_sems.at[s],
            device_id=sibling, device_id_type=MESH) for s, f in enumerate((0,) + CHIP_FLIPS)]
        return local, sends

    def chip_copies(outs, scr):
        h_out, chip_send, chip_recv = scr[2], scr[6], scr[7]
        me = _my_pos()
        return [pltpu.make_async_remote_copy(
            src_ref=h_out.at[j], dst_ref=outs[1].at[j], send_sem=chip_send.at[j], recv_sem=chip_recv.at[j],
            device_id=_flip(me, CHIP_FLIPS[j]), device_id_type=MESH) for j in range(n)]

    def start(ins, outs, scr):
        local, sends = pair_copies(ins, scr)
        for cp in local + sends:
            cp.start()

    def mid(ins, outs, scr):
        mine, sib, h_out = scr[:3]
        local, sends = pair_copies(ins, scr)
        for cp in local:
            cp.wait()
        for cp in sends:
            cp.wait_recv()
        outs[0][...] = ins[1][...] + sib[0].astype(F32)
        for s in range(n):
            h_out[s] = (mine[s].astype(F32) + sib[s + 1].astype(F32)).astype(BF16)
        for cp in chip_copies(outs, scr):
            cp.start()

    def finish(ins, outs, scr):
        cps = chip_copies(outs, scr)
        for cp in cps:
            cp.wait_recv()
        for cp in cps + pair_copies(ins, scr)[1]:
            cp.wait_send()

    return _Carry(
        inputs=[g_bf, g_own], in_specs=[_any(), _whole()],
        out_shape=[_sds(shape, F32), _sds((n,) + shape, BF16)], out_specs=[_whole(), _any()],
        scratch=[pltpu.VMEM((n,) + shape, BF16), pltpu.VMEM((n + 1,) + shape, BF16), pltpu.VMEM((n,) + shape, BF16),
                 pltpu.SemaphoreType.DMA((n + 1,)), pltpu.SemaphoreType.DMA((n + 1,)), pltpu.SemaphoreType.DMA((n,)),
                 pltpu.SemaphoreType.DMA((n,)), pltpu.SemaphoreType.DMA((n,))],
        start=start, finish=finish, mid=(mid_step, mid))


def _scatter_carry(grads_bf, shard_shapes, col_sharded, relations):
    n_w = len(grads_bf)
    shapes = [tuple(s) for s in shard_shapes]

    def copies(ins, outs, scr):
        send_sems, recv_sems = scr
        me = _my_pos()
        out = []
        for w in range(n_w):
            for i, k in enumerate(relations[w]):
                peer = _flip(me, k)
                out.append(pltpu.make_async_remote_copy(
                    src_ref=_region(ins[w], shapes[w], col_sharded[w], peer), dst_ref=outs[w].at[i],
                    send_sem=send_sems.at[w, i], recv_sem=recv_sems.at[w, i],
                    device_id=peer, device_id_type=MESH))
        return out

    def start(ins, outs, scr):
        for cp in copies(ins, outs, scr):
            cp.start()

    def finish(ins, outs, scr):
        cps = copies(ins, outs, scr)
        for cp in cps:
            cp.wait_recv()
        for cp in cps:
            cp.wait_send()

    return _Carry(
        inputs=list(grads_bf), in_specs=[_any()] * n_w,
        out_shape=[_sds((len(r),) + s, BF16) for r, s in zip(relations, shapes)], out_specs=[_any()] * n_w,
        scratch=[pltpu.SemaphoreType.DMA((n_w, N_DEV - 1)), pltpu.SemaphoreType.DMA((n_w, N_DEV - 1))],
        start=start, finish=finish)


def _block_diag(w):
    eye = jnp.eye(N_HEADS, dtype=w.dtype)
    return (eye[:, None, :, None] * w[:, :, None, :]).reshape(N_HEADS * HEAD_DIM, N_HEADS * HEAD_DIM)


def _local_step(x2, target, mod, w_in_f, w_full, conv_w_full, ffn_cw_full,
                g_mix_pre, g_mix_post, conv_b, w_rgate, b_rgate, w_igate, b_igate, lru_a, v_norm_g, v_norm_b,
                w_spatial, b_spatial, g_lru_out, g_gmlp_out, g_ffn_pre, g_ffn_post, ffn_conv_b,
                gather=None, scatter=None, adam=None, gate_bd=None):
    sh_m, sc_m, gt_m, sh_f, sc_f, gt_f = [mod[k] for k in range(N_MOD)]
    if gate_bd:
        wr_bd, wi_bd, b_r, b_i = gate_bd
    else:
        wr_bd, wi_bd = [_block_diag(w[0]).astype(BF16) for w in (w_rgate, w_igate)]
        b_r, b_i = b_rgate.reshape(1, LRU_W), b_igate.reshape(1, LRU_W)
    b_sp_t = b_spatial[0].T
    w_sp_t = jnp.swapaxes(w_spatial[0], 1, 2)

    def arriving(*names):
        return gather(*names) if gather else None

    near, far = (1, 2, 3, 4, 5), (6, 7)

    def leaving(*parts):
        return scatter(parts) if scatter else None

    def received(recv, parts, outs):
        for (name, _, _), out in zip(parts, outs):
            recv.setdefault(name, []).append(out)

    mix_params = (conv_w_full, conv_b, wr_bd, wi_bd, b_r, b_i, lru_a, v_norm_g, v_norm_b)
    w_out_f = w_full["w_out"]
    (z, h, ycat, hl, y, x1, h2), got = _mix_fwd(
        x2, sh_m, sc_m, g_mix_pre, w_in_f, *mix_params, w_spatial[0], b_sp_t, g_lru_out, g_gmlp_out,
        w_out_f, g_mix_post, gt_m, g_ffn_pre, sc_f, sh_f, carry=arriving("w_up"))
    w_up_f = got[0] if gather else w_full["w_up"]
    (up_pre, up, act), got = _ffn_fwd(h2, w_up_f, ffn_cw_full, ffn_conv_b, carry=arriving("w_down"))
    w_down_f = got[0] if gather else w_full["w_down"]
    d_y2, dout, loss_acc, vs_ffn = _ffn_tail(act, w_down_f, x1, gt_f, g_ffn_post, target)

    recv, updated = {}, {}

    def updating(grads):
        if not adam:
            return None
        return _adamw_rider([(adam[n][0], g[1], recv[n], adam[n][1], adam[n][2]) for n, g in grads.items()], N_DEV)

    def updates(grads, outs):
        for j, n in enumerate(grads):
            updated[n] = tuple(outs[4 * j:4 * j + 4])

    gw_down, _ = _wgrad(act, d_y2, "wgrad_down", by_rows=True)
    parts = [("w_down", gw_down[0], near + far)]
    (d_up, cs_ffn), got = _ffn_bwd(d_y2, up_pre, up, ffn_cw_full, w_down_f, carry=leaving(*parts))
    received(recv, parts, got)
    gw_up, got = _wgrad(h2, d_up, "wgrad_up", carry=updating(dict(w_down=gw_down)))
    updates(dict(w_down=gw_down), got)
    parts = [("w_up", gw_up[0], near)]
    (d_x1, d_y, d_ycat, vs_up), got = _up_bwd(
        d_up, w_up_f, x1, dout, y, w_out_f, g_ffn_pre, sc_f, g_mix_post, gt_m, carry=leaving(*parts))
    received(recv, parts, got)
    gw_out, _ = _wgrad(ycat, d_y, "wgrad_out", by_rows=True)
    parts = [("w_up", gw_up[0], far), ("w_out", gw_out[0], near + far)]
    (d_z, vs_mix, dcw, d_wr, d_wi, d_ws, d_bs), got = _mix_bwd(
        d_ycat, z, hl, *mix_params, w_spatial[0], w_sp_t, b_sp_t, g_lru_out, g_gmlp_out, carry=leaving(*parts))
    received(recv, parts, got)
    gw_in, got = _wgrad(h, d_z, "wgrad_in", carry=updating(dict(w_up=gw_up, w_out=gw_out)))
    updates(dict(w_up=gw_up, w_out=gw_out), got)
    in_bwd_steps = x2.shape[0] // min(TT_BIG, x2.shape[0])
    two_level = _two_level_scatter_carry(gw_in[0], gw_in[1], True, min(1, in_bwd_steps - 1)) if scatter else None
    (grad_x, vs_in), got = _in_bwd(d_z, w_in_f, x2, d_x1, g_mix_pre, sc_m, carry=two_level)
    if scatter:
        gw_in = (gw_in[0], got[0])
    recv["w_in"] = list(got[1:])

    gath = [vs_in, vs_up, vs_ffn, loss_acc]
    red = [cs_ffn, vs_mix, dcw, d_wr, d_wi, d_ws.reshape(N_GROUPS * POS_BLOCK, POS_BLOCK), d_bs]
    return dict(grad_x=grad_x, gath=gath, red=red, recv=recv, updated=updated,
                w_in=gw_in, w_out=gw_out, w_up=gw_up, w_down=gw_down)


def kernel(x, c, w_ada, b_ada, g_mix_pre, g_mix_post, w_in, conv_w, conv_b, w_rgate, b_rgate, w_igate, b_igate, lru_a, v_norm_g, v_norm_b, w_spatial, b_spatial, g_lru_out, g_gmlp_out, w_out, g_ffn_pre, g_ffn_post, w_up, ffn_conv_w, ffn_conv_b, w_down, loss_target, m_w_ada, m_b_ada, m_g_mix_pre, m_g_mix_post, m_w_in, m_conv_w, m_conv_b, m_w_rgate, m_b_rgate, m_w_igate, m_b_igate, m_lru_a, m_v_norm_g, m_v_norm_b, m_w_spatial, m_b_spatial, m_g_lru_out, m_g_gmlp_out, m_w_out, m_g_ffn_pre, m_g_ffn_post, m_w_up, m_ffn_conv_w, m_ffn_conv_b, m_w_down, v_w_ada, v_b_ada, v_g_mix_pre, v_g_mix_post, v_w_in, v_conv_w, v_conv_b, v_w_rgate, v_b_rgate, v_w_igate, v_b_igate, v_lru_a, v_v_norm_g, v_v_norm_b, v_w_spatial, v_b_spatial, v_g_lru_out, v_g_gmlp_out, v_w_out, v_g_ffn_pre, v_g_ffn_post, v_w_up, v_ffn_conv_w, v_ffn_conv_b, v_w_down):
    big_w = dict(w_in=(w_in, m_w_in, v_w_in, True), w_out=(w_out, m_w_out, v_w_out, False),
                 w_up=(w_up, m_w_up, v_w_up, True), w_down=(w_down, m_w_down, v_w_down, False))

    def gather(*names):
        return _gather_carry([big_w[n][0][0] for n in names], [STACKED if n == "w_up" else big_w[n][3] for n in names])

    def scatter(parts):
        return _scatter_carry([g for _, g, _ in parts], [big_w[n][0].shape[1:] for n, _, _ in parts],
                              [big_w[n][3] for n, _, _ in parts], [rel for _, _, rel in parts])

    ffn_cw_taps = tuple(a.reshape(FFN_CONV_K, 1, -1) for a in (ffn_conv_w, m_ffn_conv_w, v_ffn_conv_w))
    (c_all, conv_w_full, ffn_cw_full, *gate_bd, sh_m, sc_m, gt_m, sh_f, sc_f, gt_f), (w_in_f, w_out_f) = _prologue(
        c, conv_w[0], ffn_cw_taps[0], w_ada[0], b_ada, w_rgate[0], w_igate[0], b_rgate[0], b_igate[0],
        carry=gather("w_in", "w_out"))
    mod = (sh_m, sc_m, gt_m, sh_f, sc_f, gt_f)

    loc = _local_step(x[0], loss_target[0], mod, w_in_f, dict(w_out=w_out_f), conv_w_full, ffn_cw_full,
                      g_mix_pre, g_mix_post, conv_b, w_rgate, b_rgate, w_igate, b_igate, lru_a, v_norm_g, v_norm_b,
                      w_spatial, b_spatial, g_lru_out, g_gmlp_out, g_ffn_pre, g_ffn_post, ffn_conv_b,
                      gather=gather, scatter=scatter,
                      adam={n: big_w[n][:3] for n in ("w_out", "w_up", "w_down")}, gate_bd=gate_bd)
    grad_x = loc["grad_x"]

    (gathered, reduced), _ = _reduce_small(loc["gath"], loc["red"])

    results = dict(loc["updated"])
    w_, m_, v_, _ = big_w["w_in"]
    results["w_in"] = _adamw_sum(w_, loc["w_in"][1], loc["recv"]["w_in"], m_, v_, "adamw_w_in")

    params = dict(
        b_ada=(b_ada, m_b_ada, v_b_ada), g_mix_pre=(g_mix_pre, m_g_mix_pre, v_g_mix_pre),
        g_mix_post=(g_mix_post, m_g_mix_post, v_g_mix_post), conv_b=(conv_b, m_conv_b, v_conv_b),
        w_rgate=(w_rgate, m_w_rgate, v_w_rgate), b_rgate=(b_rgate, m_b_rgate, v_b_rgate),
        w_igate=(w_igate, m_w_igate, v_w_igate), b_igate=(b_igate, m_b_igate, v_b_igate),
        lru_a=(lru_a, m_lru_a, v_lru_a), v_norm_g=(v_norm_g, m_v_norm_g, v_v_norm_g),
        v_norm_b=(v_norm_b, m_v_norm_b, v_v_norm_b), w_spatial=(w_spatial, m_w_spatial, v_w_spatial),
        b_spatial=(b_spatial, m_b_spatial, v_b_spatial), g_lru_out=(g_lru_out, m_g_lru_out, v_g_lru_out),
        g_gmlp_out=(g_gmlp_out, m_g_gmlp_out, v_g_gmlp_out), g_ffn_pre=(g_ffn_pre, m_g_ffn_pre, v_g_ffn_pre),
        g_ffn_post=(g_ffn_post, m_g_ffn_post, v_g_ffn_post), ffn_conv_b=(ffn_conv_b, m_ffn_conv_b, v_ffn_conv_b))
    conv_params = dict(conv_w=(conv_w, m_conv_w, v_conv_w), ffn_conv_w=ffn_cw_taps)
    small_results, loss = _adamw_small(gathered, reduced, params, conv_params)
    results.update(small_results)
    results["ffn_conv_w"] = tuple(a.reshape(ffn_conv_w.shape) for a in results["ffn_conv_w"])
    loss = loss.reshape(())

    results["w_ada"] = _adamw_wada(c_all, gathered[0], gathered[1], gathered[2], w_ada, m_w_ada, v_w_ada)

    order = ["w_ada", "b_ada", "g_mix_pre", "g_mix_post", "w_in", "conv_w", "conv_b", "w_rgate", "b_rgate", "w_igate",
             "b_igate", "lru_a", "v_norm_g", "v_norm_b", "w_spatial", "b_spatial", "g_lru_out", "g_gmlp_out", "w_out",
             "g_ffn_pre", "g_ffn_post", "w_up", "ffn_conv_w", "ffn_conv_b", "w_down"]
    outs = [loss, grad_x[None]]
    for kind in range(4):
        outs += [results[n][kind] for n in order]
    return tuple(outs)
```

```python
import functools
import math

import jax
import jax.numpy as jnp
from jax import lax
from jax.experimental import pallas as pl
from jax.experimental.pallas import tpu as pltpu

F32 = jnp.float32
BF16 = jnp.bfloat16

D_MODEL = 1024
LRU_W = 512
GMLP_W = 512
N_HEADS = 8
HEAD_DIM = 64
N_GROUPS = 4
POS_BLOCK = 128
CHUNK = 64
IN_COLS = 2048
D_FF = 3072
N_MOD = 6
N_DEV = 8
EPS = 1e-6
LRU_C = 8.0
LRU_CONV_K = 4
FFN_CONV_K = 3

ADAM_LR = 0.001
ADAM_B1 = 0.9
ADAM_B2 = 0.999
ADAM_EPS = 1e-08
ADAM_WD = 0.01
ADAM_STEP = 10

LANES = 128
SUBLANES = 8
TT_BIG = 512
TT_MIX = 256
FF_CW = 1024
VMEM_LIMIT = 56 * 1024 * 1024

MESH = pl.DeviceIdType.MESH


def _sds(shape, dtype):
    return jax.ShapeDtypeStruct(shape, dtype)


def _cparams(sem=None):
    return pltpu.CompilerParams(dimension_semantics=sem, vmem_limit_bytes=VMEM_LIMIT)


def _whole():
    return pl.BlockSpec(memory_space=pltpu.VMEM)


def _const(shape):
    nd = len(shape)
    return pl.BlockSpec(shape, lambda *_: (0,) * nd)


def _any():
    return pl.BlockSpec(memory_space=pl.ANY)


class _Carry:
    def __init__(self, inputs, in_specs, out_shape, out_specs, scratch, start=None, finish=None, each=None, mid=None):
        self.inputs, self.in_specs, self.out_shape, self.out_specs = inputs, in_specs, out_shape, out_specs
        self.scratch, self.start, self.finish, self.each, self.mid = scratch, start, finish, each, mid


def _call(body, name, grid, in_specs, out_specs, out_shape, scratch, args, carry=None, body_starts_carry=False):
    n_in, n_out, n_scr = len(in_specs), len(out_specs), len(scratch)
    c_in = len(carry.in_specs) if carry else 0
    c_out = len(carry.out_specs) if carry else 0

    def full_body(*refs):
        ins = refs[:n_in]
        c_ins = refs[n_in:n_in + c_in]
        outs = refs[n_in + c_in:n_in + c_in + n_out]
        c_outs = refs[n_in + c_in + n_out:n_in + c_in + n_out + c_out]
        scr = refs[n_in + c_in + n_out + c_out:n_in + c_in + n_out + c_out + n_scr]
        c_scr = refs[n_in + c_in + n_out + c_out + n_scr:]
        if carry:
            first = functools.reduce(lambda a, b: a & b, [pl.program_id(d) == 0 for d in range(len(grid))])
            last = functools.reduce(lambda a, b: a & b, [pl.program_id(d) == g - 1 for d, g in enumerate(grid)])

        if carry and carry.start and not body_starts_carry:
            @pl.when(first)
            def _():
                carry.start(c_ins, c_outs, c_scr)

        if carry and carry.mid:
            @pl.when(pl.program_id(0) == carry.mid[0])
            def _():
                carry.mid[1](c_ins, c_outs, c_scr)

        if body_starts_carry:
            body(*ins, *outs, *scr, start_carry=(lambda: carry.start(c_ins, c_outs, c_scr)) if carry else (lambda: None))
        else:
            body(*ins, *outs, *scr)
        if carry and carry.each:
            carry.each(c_ins, c_outs, c_scr)
        if carry and carry.finish:
            @pl.when(last)
            def _():
                carry.finish(c_ins, c_outs, c_scr)

    res = pl.pallas_call(
        full_body, name=name, grid=grid,
        in_specs=list(in_specs) + (list(carry.in_specs) if carry else []),
        out_specs=list(out_specs) + (list(carry.out_specs) if carry else []),
        out_shape=list(out_shape) + (list(carry.out_shape) if carry else []),
        scratch_shapes=list(scratch) + (list(carry.scratch) if carry else []),
        compiler_params=_cparams(("arbitrary",) * len(grid)),
    )(*args, *(carry.inputs if carry else []))
    return res[:n_out], res[n_out:]


GELU_C0 = 0.7978845608028654
GELU_C1 = GELU_C0 * 0.044715


def _gelu(x):
    t = jnp.tanh(x * (GELU_C0 + GELU_C1 * (x * x)))
    hx = 0.5 * x
    return hx + hx * t


def _gelu_and_grad(x):
    x2 = x * x
    t = jnp.tanh(x * (GELU_C0 + GELU_C1 * x2))
    hx = 0.5 * x
    g = hx + hx * t
    dg = (0.5 + 0.5 * t) + hx * (1.0 - t * t) * (GELU_C0 + 3.0 * GELU_C1 * x2)
    return g, dg


def _sigmoid(x):
    return 1.0 / (1.0 + jnp.exp(-x))


def _softplus(x):
    return jnp.maximum(x, 0.0) + jnp.log1p(jnp.exp(-jnp.abs(x)))


def _neg_expm1(x):
    series = -x * (1.0 + x * (0.5 + x * (1.0 / 6.0 + x * (1.0 / 24.0 + x * (1.0 / 120.0)))))
    return jnp.where(x > -0.1, series, 1.0 - jnp.exp(x))


def _dot(a, b):
    return jnp.dot(a.astype(BF16), b.astype(BF16), preferred_element_type=F32)


def _dot_nt(a, b):
    return lax.dot_general(a.astype(BF16), b.astype(BF16), (((1,), (1,)), ((), ())), preferred_element_type=F32)


def _dot_tn(a, b):
    return lax.dot_general(a.astype(BF16), b.astype(BF16), (((0,), (0,)), ((), ())), preferred_element_type=F32)


def _rows(shape):
    return lax.broadcasted_iota(jnp.int32, shape, 0)


def _shift_down(cur, prev8, s):
    if s == 0:
        return cur
    n = cur.shape[0]
    r = pltpu.roll(cur, s, 0)
    p = pltpu.roll(prev8, s, 0)
    top = jnp.where(_rows(p.shape) < s, p, r[0:SUBLANES])
    if n == SUBLANES:
        return top
    return jnp.concatenate([top, r[SUBLANES:]], axis=0)


def _shift_up(cur, next8, s):
    if s == 0:
        return cur
    n = cur.shape[0]
    r = pltpu.roll(cur, n - s, 0)
    q = pltpu.roll(next8, SUBLANES - s, 0)
    bot = jnp.where(_rows(q.shape) >= SUBLANES - s, q, r[n - SUBLANES:])
    if n == SUBLANES:
        return bot
    return jnp.concatenate([r[:n - SUBLANES], bot], axis=0)


def _scan_fwd(a, b, h_in):
    n = a.shape[0]
    in_group = _rows(a.shape) & (SUBLANES - 1)
    s = 1
    while s < SUBLANES:
        a_s = pltpu.roll(a, s, 0)
        b_s = pltpu.roll(b, s, 0)
        m = in_group >= s
        b = jnp.where(m, a * b_s + b, b)
        a = jnp.where(m, a * a_s, a)
        s *= 2
    out, carry = [], h_in
    for g in range(n // SUBLANES):
        rows = slice(g * SUBLANES, (g + 1) * SUBLANES)
        h_g = a[rows] * carry + b[rows]
        out.append(h_g)
        carry = h_g[SUBLANES - 1:SUBLANES, :]
    return jnp.concatenate(out, axis=0)


def _scan_rev(a, b, l_in):
    n = a.shape[0]
    in_group = _rows(a.shape) & (SUBLANES - 1)
    s = 1
    while s < SUBLANES:
        a_s = pltpu.roll(a, n - s, 0)
        b_s = pltpu.roll(b, n - s, 0)
        m = in_group < SUBLANES - s
        b = jnp.where(m, b + a * b_s, b)
        a = jnp.where(m, a * a_s, a)
        s *= 2
    out, carry = [], l_in
    for g in reversed(range(n // SUBLANES)):
        rows = slice(g * SUBLANES, (g + 1) * SUBLANES)
        l_g = b[rows] + a[rows] * carry
        out.append(l_g)
        carry = l_g[0:1, :]
    return jnp.concatenate(out[::-1], axis=0)


def _rms(x):
    r = lax.rsqrt(jnp.mean(x * x, axis=-1, keepdims=True) + EPS)
    return x * r, r


def _rms_bwd(d_n, n, r):
    return r * (d_n - n * jnp.mean(d_n * n, axis=-1, keepdims=True))


def _colsum(x):
    return jnp.sum(x, axis=0, keepdims=True)


ROW_PIECE = 256


def _row_pieces(tt):
    return [slice(r, r + min(ROW_PIECE, tt)) for r in range(0, tt, min(ROW_PIECE, tt))]


def _lru_gates(xc, wr_ref, wi_ref, br, bi, sp_a):
    r = _sigmoid(_dot(xc, wr_ref[...]) + br)
    i = _sigmoid(_dot(xc, wi_ref[...]) + bi)
    la = -LRU_C * r * sp_a
    a = jnp.exp(la)
    mult = jnp.sqrt(_neg_expm1(2.0 * la))
    return r, i, a, mult


def _lru_conv(lx, prev8, cw_ref, cb):
    xc = cb + cw_ref[LRU_CONV_K - 1:LRU_CONV_K, :] * lx
    taps = []
    for k in range(LRU_CONV_K - 1):
        tap = _shift_down(lx, prev8, LRU_CONV_K - 1 - k)
        taps.append(tap)
        xc = xc + cw_ref[k:k + 1, :] * tap
    return xc, taps


def _ws_mask(transposed=False):
    i = lax.broadcasted_iota(jnp.int32, (POS_BLOCK, POS_BLOCK), 0)
    j = lax.broadcasted_iota(jnp.int32, (POS_BLOCK, POS_BLOCK), 1)
    if transposed:
        i, j = j, i
    return (j // CHUNK) <= (i // CHUNK)


def _gmlp_v(gv, vg, vb):
    av, dav = _gelu_and_grad(gv)
    mu = jnp.mean(av, axis=-1, keepdims=True)
    cen = av - mu
    rs = lax.rsqrt(jnp.mean(cen * cen, axis=-1, keepdims=True) + EPS)
    vhat = cen * rs
    return vhat * vg + vb, vhat, rs, dav


def _mix_fwd(x, sh, sc, g_pre, w_in, conv_w, conv_b, wr_bd, wi_bd, b_r, b_i, lru_a, vn_g, vn_b, w_sp, b_sp_t,
             g_lru, g_gmlp, w_out, g_post, gt_m, g_ffn_pre, sc_f, sh_f, carry=None):
    s_len = x.shape[0]
    tt = min(TT_MIX, s_len)
    nblk = tt // POS_BLOCK

    def body(x_ref, sh_ref, sc_ref, g_ref, w_ref, cw_ref, cb_ref, wr_ref, wi_ref, br_ref, bi_ref, la_ref, vg_ref,
             vb_ref, ws_ref, bst_ref, gl_ref, gg_ref, wo_ref, gp_ref, gtm_ref, g2_ref, scf_ref, shf_ref,
             z_ref, h_ref, y_ref, hl_ref, yo_ref, x1_ref, h2_ref, prev8, hcar):
        i = pl.program_id(0)

        @pl.when(i == 0)
        def _():
            prev8[...] = jnp.zeros_like(prev8)
            hcar[...] = jnp.zeros_like(hcar)

        n_x, _ = _rms(x_ref[...])
        h = (n_x * g_ref[...] * (1.0 + sc_ref[...]) + sh_ref[...]).astype(BF16)
        h_ref[...] = h
        z_ref[...] = jnp.dot(h, w_ref[...], preferred_element_type=F32)

        lx = z_ref[:, 0:LRU_W]
        gate = z_ref[:, LRU_W:2 * LRU_W]
        gu = z_ref[:, 2 * LRU_W:2 * LRU_W + GMLP_W]
        gv = z_ref[:, 2 * LRU_W + GMLP_W:]

        xc, _ = _lru_conv(lx, prev8[...], cw_ref, cb_ref[...])
        prev8[...] = lx[tt - SUBLANES:]
        sp_a = _softplus(-la_ref[...])
        _, ig, a, mult = _lru_gates(xc, wr_ref, wi_ref, br_ref[...], bi_ref[...], sp_a)
        bx = mult * (ig * xc)
        hl = _scan_fwd(a, bx, hcar[0:1, :])
        hcar[...] = jnp.broadcast_to(hl[tt - 1:tt, :], hcar.shape)
        hl_ref[...] = hl
        y_lru = hl * _gelu(gate)
        n_l, _ = _rms(y_lru)
        y_ref[:, 0:LRU_W] = (n_l * gl_ref[...]).astype(BF16)

        u = _gelu(gu)
        v, _, _, _ = _gmlp_v(gv, vg_ref[...], vb_ref[...])
        mask = _ws_mask()
        sp_parts = []
        for nb in range(nblk):
            row = []
            for g in range(N_GROUPS):
                wsm = jnp.where(mask, ws_ref[g], 0.0)
                vblk = v[nb * POS_BLOCK:(nb + 1) * POS_BLOCK, g * LANES:(g + 1) * LANES]
                row.append(_dot(wsm, vblk) + bst_ref[:, g:g + 1])
            sp_parts.append(jnp.concatenate(row, axis=1))
        sp = jnp.concatenate(sp_parts, axis=0) if nblk > 1 else sp_parts[0]
        n_g, _ = _rms(u * sp)
        y_ref[:, LRU_W:] = (n_g * gg_ref[...]).astype(BF16)

        y = jnp.dot(y_ref[...], wo_ref[...], preferred_element_type=F32)
        yo_ref[...] = y
        n_y, _ = _rms(y)
        x1 = x_ref[...] + gtm_ref[...] * (n_y * gp_ref[...])
        x1_ref[...] = x1
        n1, _ = _rms(x1)
        h2_ref[...] = (n1 * g2_ref[...] * (1.0 + scf_ref[...]) + shf_ref[...]).astype(BF16)

    row = lambda c: pl.BlockSpec((tt, c), lambda i: (i, 0))
    v512 = _const((1, LRU_W))
    vec = _const((1, D_MODEL))
    return _call(
        body, "mix_fwd", (s_len // tt,),
        in_specs=[row(D_MODEL), vec, vec, vec, _whole(),
                  _const((LRU_CONV_K, LRU_W)), v512, _whole(), _whole(), v512, v512, v512, v512, v512,
                  _whole(), _whole(), v512, v512, _whole(), vec, vec, vec, vec, vec],
        out_specs=[row(IN_COLS), row(D_MODEL), row(LRU_W + GMLP_W), row(LRU_W), row(D_MODEL), row(D_MODEL),
                   row(D_MODEL)],
        out_shape=[_sds((s_len, IN_COLS), F32), _sds((s_len, D_MODEL), BF16),
                   _sds((s_len, LRU_W + GMLP_W), BF16), _sds((s_len, LRU_W), F32),
                   _sds((s_len, D_MODEL), F32), _sds((s_len, D_MODEL), F32), _sds((s_len, D_MODEL), BF16)],
        scratch=[pltpu.VMEM((SUBLANES, LRU_W), F32), pltpu.VMEM((SUBLANES, LRU_W), F32)],
        args=(x, sh, sc, g_pre, w_in, conv_w, conv_b, wr_bd, wi_bd, b_r, b_i, lru_a, vn_g, vn_b, w_sp, b_sp_t,
              g_lru, g_gmlp, w_out, g_post, gt_m, g_ffn_pre, sc_f, sh_f), carry=carry)


FF_CHUNKS = N_DEV // 2
FF_CHUNK_W = D_FF // FF_CHUNKS


def _ffn_fwd(h2, w_up3, ffn_cw, ffn_cb, carry=None):
    s_len = h2.shape[0]
    tt = min(TT_BIG, s_len)
    nc, cw = FF_CHUNKS, FF_CHUNK_W

    def body(h2_ref, wu_hbm, cwg_ref, cwv_ref, cbg_ref, cbv_ref, up_ref, upc_ref, act_ref, prev, wu_ref, w_sems):
        i = pl.program_id(0)
        c = pl.program_id(1)

        def stage(k):
            return pltpu.make_async_copy(wu_hbm.at[k], wu_ref.at[k], w_sems.at[k])

        @pl.when((i == 0) & (c == 0))
        def _():
            for k in range(nc):
                stage(k).start()
                stage(nc + k).start()

        @pl.when(i == 0)
        def _():
            prev[c] = jnp.zeros((2, SUBLANES, cw), F32)
            stage(c).wait()
            stage(nc + c).wait()

        h2 = h2_ref[...]
        ug_pre = jnp.dot(h2, wu_ref[c], preferred_element_type=F32)
        uv_pre = jnp.dot(h2, wu_ref[nc + c], preferred_element_type=F32)
        up_ref[0] = ug_pre.astype(BF16)
        up_ref[1] = uv_pre.astype(BF16)
        ug, _ = _ffn_conv(ug_pre, prev[c, 0], cwg_ref, cbg_ref[...])
        uv, _ = _ffn_conv(uv_pre, prev[c, 1], cwv_ref, cbv_ref[...])
        prev[c, 0] = ug_pre[tt - SUBLANES:, :]
        prev[c, 1] = uv_pre[tt - SUBLANES:, :]
        upc_ref[0] = ug
        upc_ref[1] = uv
        act_ref[...] = (_gelu(ug) * uv).astype(BF16)

    chunk2 = pl.BlockSpec((2, tt, cw), lambda i, c: (0, i, c))
    ffn_cb2 = ffn_cb.reshape(1, 2 * D_FF)
    return _call(
        body, "ffn_fwd", (s_len // tt, nc),
        in_specs=[pl.BlockSpec((tt, D_MODEL), lambda i, c: (i, 0)), _any(),
                  pl.BlockSpec((FFN_CONV_K, cw), lambda i, c: (0, c)),
                  pl.BlockSpec((FFN_CONV_K, cw), lambda i, c: (0, c + nc)),
                  pl.BlockSpec((1, cw), lambda i, c: (0, c)),
                  pl.BlockSpec((1, cw), lambda i, c: (0, c + nc))],
        out_specs=[chunk2, chunk2, pl.BlockSpec((tt, cw), lambda i, c: (i, c))],
        out_shape=[_sds((2, s_len, D_FF), BF16), _sds((2, s_len, D_FF), F32), _sds((s_len, D_FF), BF16)],
        scratch=[pltpu.VMEM((nc, 2, SUBLANES, cw), F32), pltpu.VMEM(tuple(w_up3.shape), BF16),
                 pltpu.SemaphoreType.DMA((2 * nc,))],
        args=(h2, w_up3, ffn_cw, ffn_cw, ffn_cb2, ffn_cb2), carry=carry)


def _ffn_tail(act, w_down, x1, gt_f, g_post, target):
    s_len = x1.shape[0]
    tt = min(TT_BIG, s_len)

    def body(act_ref, wd_ref, x1_ref, gtf_ref, gp_ref, tg_ref, dy2_ref, dout_ref, loss_ref, vs_ref):
        @pl.when(pl.program_id(0) == 0)
        def _():
            loss_ref[...] = jnp.zeros_like(loss_ref)
            vs_ref[...] = jnp.zeros_like(vs_ref)

        for rows in _row_pieces(tt):
            n2, r2 = _rms(jnp.dot(act_ref[rows, :], wd_ref[...], preferred_element_type=F32))
            out = x1_ref[rows, :] + gtf_ref[...] * (n2 * gp_ref[...])
            err = out - tg_ref[rows, :]
            do = err * (1.0 / D_MODEL)
            dout_ref[rows, :] = do
            loss_ref[...] += jnp.broadcast_to(0.5 * jnp.sum(err * err, keepdims=True) * (1.0 / D_MODEL),
                                              loss_ref.shape)
            vs_ref[0:1, :] += _colsum(do * n2 * gp_ref[...])
            vs_ref[1:2, :] += _colsum(do * gtf_ref[...] * n2)
            dy2_ref[rows, :] = _rms_bwd(do * gtf_ref[...] * gp_ref[...], n2, r2).astype(BF16)

    row = lambda c: pl.BlockSpec((tt, c), lambda i: (i, 0))
    vec = _const((1, D_MODEL))
    outs, _ = _call(
        body, "ffn_tail", (s_len // tt,),
        in_specs=[row(D_FF), _whole(), row(D_MODEL), vec, vec, row(D_MODEL)],
        out_specs=[row(D_MODEL), row(D_MODEL), _const((SUBLANES, LANES)), _const((SUBLANES, D_MODEL))],
        out_shape=[_sds((s_len, D_MODEL), BF16), _sds((s_len, D_MODEL), F32), _sds((SUBLANES, LANES), F32),
                   _sds((SUBLANES, D_MODEL), F32)],
        scratch=[], args=(act, w_down, x1, gt_f, g_post, target))
    return outs


def _ffn_conv(up_pre, prev8, cw_ref, cb):
    up = cb + cw_ref[FFN_CONV_K - 1:FFN_CONV_K, :] * up_pre
    taps = []
    for k in range(FFN_CONV_K - 1):
        tap = _shift_down(up_pre, prev8, FFN_CONV_K - 1 - k)
        taps.append(tap)
        up = up + cw_ref[k:k + 1, :] * tap
    return up, taps


def _ffn_bwd(d_y2, up_pre, up, ffn_cw, w_down, carry=None):
    s_len = d_y2.shape[0]
    tt = min(TT_BIG, s_len)
    nt = s_len // tt
    cw = FF_CW
    nc = D_FF // cw

    def body(dy2_ref, up_ref, upc_ref, cwg_ref, cwv_ref, wd_hbm, dup_ref, cs_ref, nxt, cs_acc, wd_ref, w_sems):
        i = pl.program_id(0)
        c = pl.program_id(1)

        def stage(k):
            rows = pl.ds(k * cw if isinstance(k, int) else pl.multiple_of(k * cw, cw), cw)
            return pltpu.make_async_copy(wd_hbm.at[rows, :], wd_ref.at[rows, :], w_sems.at[k])

        @pl.when((i == 0) & (c == 0))
        def _():
            for k in range(nc):
                stage(k).start()

        @pl.when(i == 0)
        def _():
            nxt[c] = jnp.zeros((2, SUBLANES, cw), F32)
            cs_acc[c] = jnp.zeros((2, SUBLANES, cw), F32)
            stage(c).wait()

        pw = 2 * LANES
        for piece in range(cw // pw):
            cols = slice(piece * pw, (piece + 1) * pw)
            d_act = _dot_nt(dy2_ref[...], wd_ref[pl.ds(pl.multiple_of(c * cw + piece * pw, pw), pw), :])
            uv = upc_ref[1, :, cols]
            gl, dgl = _gelu_and_grad(upc_ref[0, :, cols])
            d_ug = d_act * uv * dgl
            d_uv = d_act * gl
            for half, (d_u, cw_ref) in enumerate(((d_ug, cwg_ref), (d_uv, cwv_ref))):
                nx = nxt[c, half, :, cols]
                x_in = up_ref[half, :, cols].astype(F32)
                d_pre = cw_ref[FFN_CONV_K - 1:FFN_CONV_K, cols] * d_u
                sums = [None] * (FFN_CONV_K + 1)
                sums[FFN_CONV_K - 1] = _colsum(d_u * x_in)
                for k in range(FFN_CONV_K - 1):
                    ahead = _shift_up(d_u, nx, FFN_CONV_K - 1 - k)
                    d_pre = d_pre + cw_ref[k:k + 1, cols] * ahead
                    sums[k] = _colsum(ahead * x_in)
                sums[FFN_CONV_K] = _colsum(d_u)
                pad = jnp.zeros((SUBLANES - FFN_CONV_K - 1, pw), F32)
                cs_acc[c, half, :, cols] += jnp.concatenate(sums + [pad], axis=0)
                nxt[c, half, :, cols] = d_u[0:SUBLANES]
                dup_ref[half, :, cols] = d_pre.astype(BF16)

        for cc in range(nc):
            @pl.when((i == nt - 1) & (c == cc))
            def _():
                cs_ref[:, cc * cw:(cc + 1) * cw] = cs_acc[cc, 0]
                cs_ref[:, D_FF + cc * cw:D_FF + (cc + 1) * cw] = cs_acc[cc, 1]

    row = pl.BlockSpec((tt, D_MODEL), lambda i, c: (nt - 1 - i, 0))
    blk = pl.BlockSpec((2, tt, cw), lambda i, c: (0, nt - 1 - i, c))
    return _call(
        body, "ffn_bwd", (nt, nc),
        in_specs=[row, blk, blk,
                  pl.BlockSpec((FFN_CONV_K, cw), lambda i, c: (0, c)),
                  pl.BlockSpec((FFN_CONV_K, cw), lambda i, c: (0, c + nc)),
                  _any()],
        out_specs=[blk, _const((SUBLANES, 2 * D_FF))],
        out_shape=[_sds((2, s_len, D_FF), BF16), _sds((SUBLANES, 2 * D_FF), F32)],
        scratch=[pltpu.VMEM((nc, 2, SUBLANES, cw), F32), pltpu.VMEM((nc, 2, SUBLANES, cw), F32),
                 pltpu.VMEM(tuple(w_down.shape), BF16), pltpu.SemaphoreType.DMA((nc,))],
        args=(d_y2, up_pre, up, ffn_cw, ffn_cw, w_down), carry=carry)


def _up_bwd(d_up, w_up3, x1, dout, y, w_out, g_pre, sc_f, g_post, gt_m, carry=None):
    s_len = x1.shape[0]
    tt = min(TT_BIG, s_len)

    def body(du_ref, wu_ref, x1_ref, do_ref, y_ref, wo_ref, g2_ref, sc_ref, gp_ref, gt_ref,
             dx1_ref, dy_ref, dyc_ref, vs_ref):
        @pl.when(pl.program_id(0) == 0)
        def _():
            vs_ref[...] = jnp.zeros_like(vs_ref)

        for rows in _row_pieces(tt):
            d_h2 = jnp.zeros((rows.stop - rows.start, D_MODEL), F32)
            for half in range(2):
                for ch in range(FF_CHUNKS):
                    d_h2 = d_h2 + _dot_nt(du_ref[half, rows, ch * FF_CHUNK_W:(ch + 1) * FF_CHUNK_W],
                                          wu_ref[half * FF_CHUNKS + ch])
            n1, r1 = _rms(x1_ref[rows, :])
            ng = n1 * g2_ref[...]
            vs_ref[0:1, :] += _colsum(d_h2)
            vs_ref[1:2, :] += _colsum(d_h2 * ng)
            d_ng = d_h2 * (1.0 + sc_ref[...])
            vs_ref[2:3, :] += _colsum(d_ng * n1)
            d_x1 = do_ref[rows, :] + _rms_bwd(d_ng * g2_ref[...], n1, r1)
            dx1_ref[rows, :] = d_x1
            n_y, r_y = _rms(y_ref[rows, :])
            vs_ref[3:4, :] += _colsum(d_x1 * n_y * gp_ref[...])
            d_on = d_x1 * gt_ref[...]
            vs_ref[4:5, :] += _colsum(d_on * n_y)
            d_y = _rms_bwd(d_on * gp_ref[...], n_y, r_y).astype(BF16)
            dy_ref[rows, :] = d_y
            dyc_ref[rows, :] = _dot_nt(d_y, wo_ref[...])

    row = lambda c: pl.BlockSpec((tt, c), lambda i: (i, 0))
    vec = _const((1, D_MODEL))
    return _call(
        body, "up_bwd", (s_len // tt,),
        in_specs=[pl.BlockSpec((2, tt, D_FF), lambda i: (0, i, 0)), _whole(), row(D_MODEL), row(D_MODEL), row(D_MODEL),
                  _whole(), vec, vec, vec, vec],
        out_specs=[row(D_MODEL), row(D_MODEL), row(LRU_W + GMLP_W), _const((SUBLANES, D_MODEL))],
        out_shape=[_sds((s_len, D_MODEL), F32), _sds((s_len, D_MODEL), BF16), _sds((s_len, LRU_W + GMLP_W), F32),
                   _sds((SUBLANES, D_MODEL), F32)],
        scratch=[], args=(d_up, w_up3, x1, dout, y, w_out, g_pre, sc_f, g_post, gt_m), carry=carry)


def _head_pair_block(hd):
    return (slice((hd // 2) * HEAD_DIM, (hd // 2 + 1) * HEAD_DIM), slice((hd % 2) * HEAD_DIM, (hd % 2 + 1) * HEAD_DIM))


def _mix_bwd(d_ycat, z, hl, conv_w, conv_b, wr_bd, wi_bd, b_r, b_i, lru_a, vn_g, vn_b, w_sp, w_sp_t, b_sp_t,
             g_lru, g_gmlp, carry=None):
    s_len = z.shape[0]
    tt = min(TT_MIX, s_len)
    nt = s_len // tt
    nblk = tt // POS_BLOCK
    hb = tt // SUBLANES

    def body(dyc_ref, z_ref, zh_ref, hl_ref, hh_ref, cw_ref, cb_ref, wr_ref, wi_ref, br_ref, bi_ref, la_ref,
             vg_ref, vb_ref, ws_ref, wst_ref, bst_ref, gl_ref, gg_ref,
             dz_ref, vs_ref, dcw_ref, dwrb_ref, dwib_ref, dws_ref, dbs_ref, nxt_dxc, nxt_a, nxt_lam, dwr_ref, dwi_ref):
        i = pl.program_id(0)
        first_tile = i == nt - 1

        @pl.when(i == 0)
        def _():
            for ref in (vs_ref, dcw_ref, dwr_ref, dwi_ref, dws_ref, dbs_ref, nxt_dxc, nxt_a, nxt_lam):
                ref[...] = jnp.zeros_like(ref)

        lx = z_ref[:, 0:LRU_W]
        gate = z_ref[:, LRU_W:2 * LRU_W]
        gu = z_ref[:, 2 * LRU_W:2 * LRU_W + GMLP_W]
        gv = z_ref[:, 2 * LRU_W + GMLP_W:]
        prev8 = jnp.where(first_tile, 0.0, zh_ref[...])
        hprev8 = jnp.where(first_tile, 0.0, hh_ref[...])

        xc, taps = _lru_conv(lx, prev8, cw_ref, cb_ref[...])
        a_par = la_ref[...]
        sp_a = _softplus(-a_par)
        r, ig, a, mult = _lru_gates(xc, wr_ref, wi_ref, br_ref[...], bi_ref[...], sp_a)
        hl = hl_ref[...]
        h_prev = _shift_down(hl, hprev8, 1)
        ggate, dggate = _gelu_and_grad(gate)
        y_lru = hl * ggate
        n_l, r_l = _rms(y_lru)
        d_nl = dyc_ref[:, 0:LRU_W]
        vs_ref[6:7, :] += _colsum(d_nl * n_l)
        d_yl = _rms_bwd(d_nl * gl_ref[...], n_l, r_l)
        d_hl = d_yl * ggate
        d_gate = d_yl * hl * dggate
        a_up = _shift_up(a, nxt_a[...], 1)
        lam = _scan_rev(a_up, d_hl, nxt_lam[0:1, :])
        nxt_a[...] = jnp.broadcast_to(a[0:1, :], nxt_a.shape)
        nxt_lam[...] = jnp.broadcast_to(lam[0:1, :], nxt_lam.shape)
        ixc = ig * xc
        d_la = lam * h_prev * a - lam * ixc * (a * a) / mult
        d_i = lam * mult * xc
        d_xc = lam * mult * ig
        vs_ref[3:4, :] += _colsum(d_la * r) * (LRU_C * _sigmoid(-a_par))
        d_pr = d_la * (-LRU_C * sp_a) * r * (1.0 - r)
        d_pi = d_i * ig * (1.0 - ig)
        vs_ref[1:2, :] += _colsum(d_pr)
        vs_ref[2:3, :] += _colsum(d_pi)
        dwr_ref[...] += _dot_tn(xc, d_pr)
        dwi_ref[...] += _dot_tn(xc, d_pi)
        d_xc = d_xc + _dot_nt(d_pr, wr_ref[...]) + _dot_nt(d_pi, wi_ref[...])
        vs_ref[0:1, :] += _colsum(d_xc)
        nx = nxt_dxc[...]
        d_lx = cw_ref[LRU_CONV_K - 1:LRU_CONV_K, :] * d_xc
        dcw_ref[LRU_CONV_K - 1:LRU_CONV_K, :] += _colsum(d_xc * lx)
        for k in range(LRU_CONV_K - 1):
            d_lx = d_lx + cw_ref[k:k + 1, :] * _shift_up(d_xc, nx, LRU_CONV_K - 1 - k)
            dcw_ref[k:k + 1, :] += _colsum(d_xc * taps[k])
        nxt_dxc[...] = d_xc[0:SUBLANES]
        dz_ref[:, 0:LRU_W] = d_lx.astype(BF16)
        dz_ref[:, LRU_W:2 * LRU_W] = d_gate.astype(BF16)

        u, du = _gelu_and_grad(gu)
        v, vhat, rs, dav = _gmlp_v(gv, vg_ref[...], vb_ref[...])
        mask = _ws_mask()
        sp_parts = []
        for nb in range(nblk):
            rowp = []
            for g in range(N_GROUPS):
                wsm = jnp.where(mask, ws_ref[g], 0.0)
                vblk = v[nb * POS_BLOCK:(nb + 1) * POS_BLOCK, g * LANES:(g + 1) * LANES]
                rowp.append(_dot(wsm, vblk) + bst_ref[:, g:g + 1])
            sp_parts.append(jnp.concatenate(rowp, axis=1))
        sp = jnp.concatenate(sp_parts, axis=0) if nblk > 1 else sp_parts[0]
        y_g = u * sp
        n_g, r_g = _rms(y_g)
        d_ng = dyc_ref[:, LRU_W:]
        vs_ref[7:8, :] += _colsum(d_ng * n_g)
        d_yg = _rms_bwd(d_ng * gg_ref[...], n_g, r_g)
        d_gu = d_yg * sp * du
        d_sp = d_yg * u
        mask_t = _ws_mask(transposed=True)
        ones8 = jnp.ones((SUBLANES, LANES), F32)
        dv_parts = []
        for nb in range(nblk):
            rowp = []
            for g in range(N_GROUPS):
                rs_, cs_ = slice(nb * POS_BLOCK, (nb + 1) * POS_BLOCK), slice(g * LANES, (g + 1) * LANES)
                dsp_blk = d_sp[rs_, cs_]
                dbs_ref[g:g + 1, :] += lax.dot_general(
                    ones8, dsp_blk, (((1,), (1,)), ((), ())), preferred_element_type=F32,
                    precision=lax.Precision.HIGHEST)[0:1, :]
                dws_ref[g] += _dot_nt(dsp_blk, v[rs_, cs_])
                wsm_t = jnp.where(mask_t, wst_ref[g], 0.0)
                rowp.append(_dot(wsm_t, dsp_blk))
            dv_parts.append(jnp.concatenate(rowp, axis=1))
        d_v = jnp.concatenate(dv_parts, axis=0) if nblk > 1 else dv_parts[0]
        vs_ref[4:5, :] += _colsum(d_v * vhat)
        vs_ref[5:6, :] += _colsum(d_v)
        d_vh = d_v * vg_ref[...]
        d_av = rs * (d_vh - jnp.mean(d_vh, axis=-1, keepdims=True)
                     - vhat * jnp.mean(d_vh * vhat, axis=-1, keepdims=True))
        dz_ref[:, 2 * LRU_W:2 * LRU_W + GMLP_W] = d_gu.astype(BF16)
        dz_ref[:, 2 * LRU_W + GMLP_W:] = (d_av * dav).astype(BF16)

        @pl.when(i == nt - 1)
        def _():
            for hd in range(N_HEADS):
                blk = slice(hd * HEAD_DIM, (hd + 1) * HEAD_DIM)
                dwrb_ref[_head_pair_block(hd)] = dwr_ref[blk, blk]
                dwib_ref[_head_pair_block(hd)] = dwi_ref[blk, blk]
            for g in range(N_GROUPS):
                dws_ref[g] = jnp.where(mask, dws_ref[g], 0.0)

    rev = lambda c: pl.BlockSpec((tt, c), lambda i: (nt - 1 - i, 0))
    halo = pl.BlockSpec((SUBLANES, LRU_W), lambda i: (jnp.maximum((nt - 1 - i) * hb - 1, 0), 0))
    v512 = _const((1, LRU_W))
    return _call(
        body, "mix_bwd", (nt,),
        in_specs=[rev(LRU_W + GMLP_W), rev(IN_COLS), halo, rev(LRU_W), halo,
                  _const((LRU_CONV_K, LRU_W)), v512, _whole(), _whole(), v512, v512, v512, v512, v512,
                  _whole(), _whole(), _whole(), v512, v512],
        out_specs=[rev(IN_COLS), _const((SUBLANES, LRU_W)), _const((SUBLANES, LRU_W)),
                   _const((LRU_W // 2, 2 * HEAD_DIM)), _const((LRU_W // 2, 2 * HEAD_DIM)),
                   _const((N_GROUPS, POS_BLOCK, POS_BLOCK)), _const((SUBLANES, POS_BLOCK))],
        out_shape=[_sds((s_len, IN_COLS), BF16), _sds((SUBLANES, LRU_W), F32), _sds((SUBLANES, LRU_W), F32),
                   _sds((LRU_W // 2, 2 * HEAD_DIM), F32), _sds((LRU_W // 2, 2 * HEAD_DIM), F32),
                   _sds((N_GROUPS, POS_BLOCK, POS_BLOCK), F32), _sds((SUBLANES, POS_BLOCK), F32)],
        scratch=[pltpu.VMEM((SUBLANES, LRU_W), F32), pltpu.VMEM((SUBLANES, LRU_W), F32),
                 pltpu.VMEM((SUBLANES, LRU_W), F32), pltpu.VMEM((LRU_W, LRU_W), F32), pltpu.VMEM((LRU_W, LRU_W), F32)],
        args=(d_ycat, z, z, hl, hl, conv_w, conv_b, wr_bd, wi_bd, b_r, b_i, lru_a, vn_g, vn_b, w_sp, w_sp_t, b_sp_t,
              g_lru, g_gmlp), carry=carry)


def _in_bwd(d_z, w_in, x, d_x1, g, sc, carry=None):
    s_len = x.shape[0]
    tt = min(TT_BIG, s_len)

    def body(dz_ref, w_ref, x_ref, dx1_ref, g_ref, sc_ref, gx_ref, vs_ref):
        @pl.when(pl.program_id(0) == 0)
        def _():
            vs_ref[...] = jnp.zeros_like(vs_ref)

        for rows in _row_pieces(tt):
            d_h = _dot_nt(dz_ref[rows, :], w_ref[...])
            n, r = _rms(x_ref[rows, :])
            vs_ref[0:1, :] += _colsum(d_h)
            vs_ref[1:2, :] += _colsum(d_h * n * g_ref[...])
            d_ng = d_h * (1.0 + sc_ref[...])
            vs_ref[2:3, :] += _colsum(d_ng * n)
            gx_ref[rows, :] = dx1_ref[rows, :] + _rms_bwd(d_ng * g_ref[...], n, r)

    row = lambda c: pl.BlockSpec((tt, c), lambda i: (i, 0))
    vec = _const((1, D_MODEL))
    return _call(
        body, "in_bwd", (s_len // tt,),
        in_specs=[row(IN_COLS), _whole(), row(D_MODEL), row(D_MODEL), vec, vec],
        out_specs=[row(D_MODEL), _const((SUBLANES, D_MODEL))],
        out_shape=[_sds((s_len, D_MODEL), F32), _sds((SUBLANES, D_MODEL), F32)],
        scratch=[], args=(d_z, w_in, x, d_x1, g, sc), carry=carry)


def _wgrad(a, b, name, by_rows=False, carry=None):
    s_len, k_dim = a.shape
    halves = b.ndim == 3
    n_dim = b.shape[-1] * (2 if halves else 1)

    def body(a_ref, b_ref, ob_ref, own_ref):
        out = _dot_tn(a_ref[...], b_ref[0] if halves else b_ref[...])
        ob_ref[...] = out.astype(BF16)

        @pl.when(pl.program_id(0) == _dev_index(_my_pos()))
        def _():
            own_ref[...] = out

    if by_rows:
        tile = k_dim // N_DEV
        a_spec = pl.BlockSpec((s_len, tile), lambda j: (0, j))
        b_spec = pl.BlockSpec((s_len, n_dim), lambda j: (0, 0))
        o_spec = pl.BlockSpec((tile, n_dim), lambda j: (j, 0))
        own_shape = (tile, n_dim)
    else:
        tile = n_dim // N_DEV
        a_spec = pl.BlockSpec((s_len, k_dim), lambda j: (0, 0))
        if halves:
            per_half = N_DEV // 2
            b_spec = pl.BlockSpec((1, s_len, tile), lambda j: (j // per_half, 0, j % per_half))
        else:
            b_spec = pl.BlockSpec((s_len, tile), lambda j: (0, j))
        o_spec = pl.BlockSpec((k_dim, tile), lambda j: (0, j))
        own_shape = (k_dim, tile)
    return _call(
        body, name, (N_DEV,), in_specs=[a_spec, b_spec], out_specs=[o_spec, _const(own_shape)],
        out_shape=[_sds((k_dim, n_dim), BF16), _sds(own_shape, F32)],
        scratch=[], args=(a, b), carry=carry)


def _adam_math(w, g, m, v):
    m = ADAM_B1 * m + (1.0 - ADAM_B1) * g
    v = ADAM_B2 * v + (1.0 - ADAM_B2) * (g * g)
    m_hat = m / (1.0 - ADAM_B1 ** ADAM_STEP)
    v_hat = v / (1.0 - ADAM_B2 ** ADAM_STEP)
    delta = -ADAM_LR * (m_hat / (jnp.sqrt(v_hat) + ADAM_EPS) + ADAM_WD * w)
    return delta, m, v


def _row_tile(rows, cols, n_f32_arrays):
    budget = VMEM_LIMIT // 2
    tr = rows
    while tr % 2 == 0 and tr // 2 >= SUBLANES and (tr // 2) % SUBLANES == 0 and tr * cols * 4 * n_f32_arrays * 2 > budget:
        tr //= 2
    return tr


def _adamw_sum_block(w_ref, g_ref, r_refs, m_ref, v_ref, go_ref, d_ref, mo_ref, vo_ref):
    g = g_ref[...]
    for r_ref in r_refs:
        for k in range(r_ref.shape[0]):
            g = g + r_ref[k].astype(F32)
    go_ref[0] = g
    d_ref[0], mo_ref[0], vo_ref[0] = _adam_math(w_ref[0], g, m_ref[0], v_ref[0])


def _adamw_rider(parts, steps):
    inputs, in_specs, out_shape, out_specs, n_recvs = [], [], [], [], []
    for w, g_own, recv, m, v in parts:
        _, rows, cols = w.shape
        tr = rows // steps
        blk = pl.BlockSpec((1, tr, cols), lambda i: (0, i, 0))
        inputs += [w, g_own, *recv, m, v]
        in_specs += ([blk, pl.BlockSpec((tr, cols), lambda i: (i, 0))]
                     + [pl.BlockSpec((r.shape[0], tr, cols), lambda i: (0, i, 0)) for r in recv] + [blk, blk])
        out_shape += [_sds((1, rows, cols), F32)] * 4
        out_specs += [blk] * 4
        n_recvs.append(len(recv))

    def each(ins, outs, scr):
        for n_recv in n_recvs:
            _adamw_sum_block(ins[0], ins[1], ins[2:2 + n_recv], ins[2 + n_recv], ins[3 + n_recv], *outs[:4])
            ins, outs = ins[4 + n_recv:], outs[4:]

    return _Carry(inputs=inputs, in_specs=in_specs, out_shape=out_shape, out_specs=out_specs, scratch=[], each=each)


def _adamw_sum(w, g_own, recv, m, v, name):
    _, rows, cols = w.shape
    n_recv = len(recv)
    tr = _row_tile(rows, cols, 10)
    nb = rows // tr

    def body(w_ref, g_ref, *rest):
        _adamw_sum_block(w_ref, g_ref, rest[:n_recv], *rest[n_recv:])

    blk = pl.BlockSpec((1, tr, cols), lambda i: (0, i, 0))
    return pl.pallas_call(
        body, name=name, grid=(nb,),
        in_specs=[blk, pl.BlockSpec((tr, cols), lambda i: (i, 0))]
        + [pl.BlockSpec((r.shape[0], tr, cols), lambda i: (0, i, 0)) for r in recv] + [blk, blk],
        out_specs=[blk] * 4, out_shape=[_sds((1, rows, cols), F32)] * 4,
        compiler_params=_cparams(("arbitrary",)),
    )(w, g_own, *recv, m, v)


def _row_of_each(ref, row):
    cols = ref.shape[1]
    rows = _rows((N_DEV, cols))
    out = jnp.zeros((N_DEV, cols), F32)
    for d in range(N_DEV):
        picked = ref[d * SUBLANES + row:d * SUBLANES + row + 1, :]
        out = jnp.where(rows == d, jnp.broadcast_to(picked, (N_DEV, cols)), out)
    return out


def _my_columns(full, width, me):
    out = jnp.zeros(full.shape[:-1] + (width,), F32)
    for d in range(N_DEV):
        out = out + jnp.where(me == d, full[:, d * width:(d + 1) * width], 0.0)
    return out


def _adamw_wada(c_all, vs_in_all, vs_up_all, vs_ffn_all, w, m, v):
    _, rows, cols = w.shape

    def body(c_ref, vi_ref, vu_ref, vf_ref, w_ref, m_ref, v_ref, go_ref, d_ref, mo_ref, vo_ref):
        me = _dev_index(_my_pos())
        cv = _row_of_each(c_ref, 0)
        ca = cv * _sigmoid(cv)
        dmod = jnp.concatenate([_row_of_each(vi_ref, 0), _row_of_each(vi_ref, 1), _row_of_each(vu_ref, 3),
                                _row_of_each(vu_ref, 0), _row_of_each(vu_ref, 1), _row_of_each(vf_ref, 0)], axis=1)
        dm = _my_columns(dmod, cols, me)
        g = lax.dot_general(ca, dm, (((0,), (0,)), ((), ())), preferred_element_type=F32,
                            precision=lax.Precision.HIGHEST)
        go_ref[0] = g
        d_ref[0], mo_ref[0], vo_ref[0] = _adam_math(w_ref[0], g, m_ref[0], v_ref[0])

    return pl.pallas_call(
        body, name="adamw_w_ada", out_shape=[_sds((1, rows, cols), F32)] * 4,
        in_specs=[_whole()] * 7, out_specs=[_whole()] * 4,
        compiler_params=_cparams(),
    )(c_all, vs_in_all, vs_up_all, vs_ffn_all, w, m, v)


def _adamw_small(gathered, reduced, params, conv_params):
    names = list(params) + list(conv_params)
    allp = {**params, **conv_params}
    n_g = len(gathered) + len(reduced)

    def body(*refs):
        g_refs = refs[:n_g]
        p_refs = refs[n_g:n_g + 3 * len(names)]
        o_refs = refs[n_g + 3 * len(names):]
        me = _dev_index(_my_pos())

        def total(ref):
            s = ref[0:SUBLANES, :]
            for d in range(1, N_DEV):
                s = s + ref[d * SUBLANES:(d + 1) * SUBLANES, :]
            return s

        vs_in, vs_up, vs_ffn, loss = [total(r) for r in g_refs[:4]]
        cs, vs_mix, dcw, dwr, dwi, dws, dbs = [r[...] for r in g_refs[4:]]
        o_refs[-1][...] = loss[0:1, 0:1]
        mine = lambda full, width: _my_columns(full, width, me)

        all_ = (slice(None), slice(None))
        heads = lambda row: [((0, slice(h, h + 1), slice(None)), row[:, h * HEAD_DIM:(h + 1) * HEAD_DIM])
                             for h in range(N_HEADS)]
        blocks = lambda pairs: [((0, h), pairs[_head_pair_block(h)]) for h in range(N_HEADS)]
        pieces = {
            "b_ada": [((slice(None), slice(k * D_MODEL, (k + 1) * D_MODEL)), row) for k, row in enumerate(
                (vs_in[0:1], vs_in[1:2], vs_up[3:4], vs_up[0:1], vs_up[1:2], vs_ffn[0:1]))],
            "g_mix_pre": [(all_, vs_in[2:3])], "g_mix_post": [(all_, vs_up[4:5])],
            "g_ffn_pre": [(all_, vs_up[2:3])], "g_ffn_post": [(all_, vs_ffn[1:2])],
            "conv_b": [(all_, vs_mix[0:1])], "b_rgate": heads(vs_mix[1:2]), "b_igate": heads(vs_mix[2:3]),
            "lru_a": [(all_, vs_mix[3:4])], "v_norm_g": [(all_, vs_mix[4:5])], "v_norm_b": [(all_, vs_mix[5:6])],
            "g_lru_out": [(all_, vs_mix[6:7])], "g_gmlp_out": [(all_, vs_mix[7:8])],
            "w_rgate": blocks(dwr), "w_igate": blocks(dwi),
            "w_spatial": [((0, g), dws[g * POS_BLOCK:(g + 1) * POS_BLOCK, :]) for g in range(N_GROUPS)],
            "b_spatial": [((0,), dbs[0:N_GROUPS])],
            "ffn_conv_b": [(all_, cs[FFN_CONV_K:FFN_CONV_K + 1])],
            "conv_w": [((0,), mine(dcw[0:LRU_CONV_K], LRU_W // N_DEV))],
        }
        ffn_cw_rows = mine(cs[0:FFN_CONV_K], 2 * D_FF // N_DEV)
        pieces["ffn_conv_w"] = [((k,), ffn_cw_rows[k:k + 1]) for k in range(FFN_CONV_K)]
        for n_i, name in enumerate(names):
            w_ref, m_ref, v_ref = p_refs[3 * n_i:3 * n_i + 3]
            go_ref, d_ref, mo_ref, vo_ref = o_refs[4 * n_i:4 * n_i + 4]
            for idx, g in pieces[name]:
                go_ref[idx] = g
                d_ref[idx], mo_ref[idx], vo_ref[idx] = _adam_math(w_ref[idx], g, m_ref[idx], v_ref[idx])

    flat_params = [a for n in names for a in allp[n]]
    out_shape = [_sds(allp[n][0].shape, F32) for n in names for _ in range(4)] + [_sds((1, 1), F32)]
    outs = pl.pallas_call(
        body, name="adamw_small", out_shape=out_shape,
        in_specs=[_whole()] * (n_g + len(flat_params)), out_specs=[_whole()] * len(out_shape),
        compiler_params=_cparams(),
    )(*gathered, *reduced, *flat_params)
    return {n: outs[4 * i:4 * i + 4] for i, n in enumerate(names)}, outs[-1]


def _my_pos():
    return lax.axis_index("x"), lax.axis_index("y"), lax.axis_index("c")


def _flip(pos, k):
    x, y, c = pos
    return (1 - x if k & 4 else x, 1 - y if k & 2 else y, 1 - c if k & 1 else c)


def _dev_index(pos):
    x, y, c = pos
    return 4 * x + 2 * y + c


def _all_gather_small(ins, outs, send_sems, recv_sems, meanwhile=None):
    n = len(ins)
    me = _my_pos()

    def slot(a, pos):
        rows = ins[a].shape[0]
        return outs[a].at[pl.ds(pl.multiple_of(_dev_index(pos) * rows, SUBLANES), rows), :]

    def copy(a, k, block):
        return pltpu.make_async_remote_copy(
            src_ref=ins[a], dst_ref=slot(a, block), send_sem=send_sems.at[a, k - 1], recv_sem=recv_sems.at[a, k - 1],
            device_id=_flip(me, k), device_id_type=MESH)

    sends = [copy(a, k, me) for a in range(n) for k in range(1, N_DEV)]
    for cp in sends:
        cp.start()
    for a in range(n):
        rows = ins[a].shape[0]
        outs[a][pl.ds(pl.multiple_of(_dev_index(me) * rows, SUBLANES), rows), :] = ins[a][...]
    if meanwhile:
        meanwhile()
    for a in range(n):
        for k in range(1, N_DEV):
            copy(a, k, _flip(me, k)).wait_recv()
    for cp in sends:
        cp.wait_send()


def _prologue(c, cw, fcw, w_ada, b_ada, w_rgate, w_igate, b_rgate, b_igate, carry):
    cols = w_ada.shape[1]
    cw_w, fcw_w = cw.shape[-1], fcw.shape[-1]
    step = math.gcd(cols, D_MODEL)

    def body(c_ref, cw_ref, fcw_ref, w_ref, b_ref, wr_ref, wi_ref, br_ref, bi_ref,
             call_ref, cwf_ref, fcwf_ref, wrbd_ref, wibd_ref, brow_ref, birow_ref, *rest, start_carry):
        mod_refs, (mod_scr, c8, cw8, fcw8, cwall, fcwall, modall, s1, r1, s2, r2) = rest[:N_MOD], rest[N_MOD:]
        me = _dev_index(_my_pos())
        c8[...] = jnp.broadcast_to(c_ref[...], c8.shape)
        cw8[...] = jnp.zeros(cw8.shape, F32)
        cw8[0:LRU_CONV_K, :] = cw_ref[...]
        fcw8[...] = jnp.zeros(fcw8.shape, F32)
        for k in range(FFN_CONV_K):
            fcw8[k:k + 1, :] = fcw_ref[k]

        def block_diagonals():
            for bd_ref, hb_ref in ((wrbd_ref, wr_ref), (wibd_ref, wi_ref)):
                bd_ref[...] = jnp.zeros(bd_ref.shape, BF16)
                for h in range(N_HEADS):
                    span = slice(h * HEAD_DIM, (h + 1) * HEAD_DIM)
                    bd_ref[span, span] = hb_ref[h].astype(BF16)
            for row_ref, hb_ref in ((brow_ref, br_ref), (birow_ref, bi_ref)):
                for h in range(N_HEADS):
                    row_ref[:, h * HEAD_DIM:(h + 1) * HEAD_DIM] = hb_ref[h:h + 1, :]

        def conv_weights():
            for d in range(N_DEV):
                cwf_ref[:, d * cw_w:(d + 1) * cw_w] = cwall[d * SUBLANES:d * SUBLANES + LRU_CONV_K, :]
                fcwf_ref[:, d * fcw_w:(d + 1) * fcw_w] = fcwall[d * SUBLANES:d * SUBLANES + FFN_CONV_K, :]

        _all_gather_small([c8, cw8, fcw8], [call_ref, cwall, fcwall], s1, r1, meanwhile=block_diagonals)
        start_carry()
        cv = _row_of_each(call_ref, 0)
        ca = cv * _sigmoid(cv)
        b_cols = _my_columns(b_ref[...], cols, me)
        mod_scr[...] = jnp.dot(ca, w_ref[...], preferred_element_type=F32, precision=lax.Precision.HIGHEST) + b_cols
        _all_gather_small([mod_scr], [modall], s2, r2, meanwhile=conv_weights)
        mine = _rows((N_DEV, cols)) == me
        for d in range(N_DEV):
            piece = jnp.sum(jnp.where(mine, modall[d * N_DEV:(d + 1) * N_DEV, :], 0.0), axis=0, keepdims=True)
            for t in range(cols // step):
                at = d * cols + t * step
                mod_refs[at // D_MODEL][:, at % D_MODEL:at % D_MODEL + step] = piece[:, t * step:(t + 1) * step]

    sem = lambda n: pltpu.SemaphoreType.DMA((n, N_DEV - 1))
    gate = N_HEADS * HEAD_DIM
    return _call(
        body, "prologue", (1,), in_specs=[_whole()] * 9, out_specs=[_whole()] * (7 + N_MOD),
        out_shape=[_sds((N_DEV * SUBLANES, c.shape[-1]), F32), _sds((LRU_CONV_K, N_DEV * cw_w), F32),
                   _sds((FFN_CONV_K, N_DEV * fcw_w), F32), _sds((gate, gate), BF16), _sds((gate, gate), BF16),
                   _sds((1, gate), F32), _sds((1, gate), F32)] + [_sds((1, D_MODEL), F32)] * N_MOD,
        scratch=[pltpu.VMEM((N_DEV, cols), F32)] + [pltpu.VMEM((SUBLANES, a.shape[-1]), F32) for a in (c, cw, fcw)]
        + [pltpu.VMEM((N_DEV * SUBLANES, cw_w), F32), pltpu.VMEM((N_DEV * SUBLANES, fcw_w), F32),
           pltpu.VMEM((N_DEV * N_DEV, cols), F32), sem(3), sem(3), sem(1), sem(1)],
        args=(c, cw, fcw, w_ada, b_ada, w_rgate, w_igate, b_rgate, b_igate), carry=carry, body_starts_carry=True)


def _reduce_small(gath, red, carry=None):
    n_g, n_r = len(gath), len(red)
    chip_flips = CHIP_FLIPS

    def body(*refs, start_carry):
        g_in, r_in = refs[:n_g], refs[n_g:n_g + n_r]
        g_out, r_out = refs[n_g + n_r:2 * n_g + n_r], refs[2 * n_g + n_r:2 * (n_g + n_r)]
        scr = refs[2 * (n_g + n_r):]
        sib, land = scr[:n_r], scr[n_r:2 * n_r]
        g_send, g_recv, s_send, s_recv, i_send, i_recv, f_send, f_recv = scr[2 * n_r:]
        me = _my_pos()
        c = me[2]
        sibling = _flip(me, 1)

        def slot(a, pos):
            return g_out[a].at[pl.ds(pl.multiple_of(_dev_index(pos) * SUBLANES, SUBLANES), SUBLANES), :]

        def gcopy(a, k):
            return pltpu.make_async_remote_copy(
                src_ref=g_in[a], dst_ref=slot(a, me), send_sem=g_send.at[a, k - 1], recv_sem=g_recv.at[a, k - 1],
                device_id=_flip(me, k), device_id_type=MESH)

        def scopy(a):
            return pltpu.make_async_remote_copy(
                src_ref=r_in[a], dst_ref=sib[a], send_sem=s_send.at[a], recv_sem=s_recv.at[a],
                device_id=sibling, device_id_type=MESH)

        def icopy(a, j):
            return pltpu.make_async_remote_copy(
                src_ref=r_out[a], dst_ref=land[a].at[j], send_sem=i_send.at[a, j], recv_sem=i_recv.at[a, j],
                device_id=_flip(me, chip_flips[j]), device_id_type=MESH)

        def fcopy(a, j):
            return pltpu.make_async_remote_copy(
                src_ref=land[a].at[j], dst_ref=land[a].at[j], send_sem=f_send.at[a, j], recv_sem=f_recv.at[a, j],
                device_id=sibling, device_id_type=MESH)

        gathers = [gcopy(a, k) for a in range(n_g) for k in range(1, N_DEV)]
        swaps = [scopy(a) for a in range(n_r)]
        for cp in gathers + swaps:
            cp.start()
        for a in range(n_g):
            g_out[a][pl.ds(pl.multiple_of(_dev_index(me) * SUBLANES, SUBLANES), SUBLANES), :] = g_in[a][...]
        for a in range(n_r):
            swaps[a].wait_recv()
            r_out[a][...] = r_in[a][...] + sib[a][...]

        for core in range(2):
            @pl.when(c == core)
            def _():
                for a in range(core, n_r, 2):
                    for j in range(3):
                        icopy(a, j).start()

        start_carry()

        for core in range(2):
            mine = [a for a in range(n_r) if a % 2 == core]
            theirs = [a for a in range(n_r) if a % 2 != core]

            @pl.when(c == core)
            def _():
                out = [icopy(a, j) for a in mine for j in range(3)]
                fwd = []
                for a in mine:
                    for j in range(3):
                        icopy(a, j).wait_recv()
                        cp = fcopy(a, j)
                        cp.start()
                        fwd.append(cp)
                for a in theirs:
                    for j in range(3):
                        fcopy(a, j).wait_recv()
                for cp in out + fwd:
                    cp.wait_send()

        for a in range(n_r):
            r_out[a][...] = (r_out[a][...] + land[a][1]) + (land[a][0] + land[a][2])
        for a in range(n_g):
            for k in range(1, N_DEV):
                pltpu.make_async_remote_copy(
                    src_ref=g_in[a], dst_ref=slot(a, _flip(me, k)), send_sem=g_send.at[a, k - 1],
                    recv_sem=g_recv.at[a, k - 1], device_id=_flip(me, k), device_id_type=MESH).wait_recv()
        for cp in gathers + swaps:
            cp.wait_send()

    shapes = [tuple(a.shape) for a in red]
    outs, carried = _call(
        body, "reduce_small", (1,), in_specs=[_whole()] * (n_g + n_r), out_specs=[_whole()] * (n_g + n_r),
        out_shape=[_sds((N_DEV * SUBLANES, a.shape[1]), F32) for a in gath] + [_sds(s, F32) for s in shapes],
        scratch=[pltpu.VMEM(s, F32) for s in shapes] + [pltpu.VMEM((3,) + s, F32) for s in shapes]
        + [pltpu.SemaphoreType.DMA((n_g, N_DEV - 1)), pltpu.SemaphoreType.DMA((n_g, N_DEV - 1)),
           pltpu.SemaphoreType.DMA((n_r,)), pltpu.SemaphoreType.DMA((n_r,)),
           pltpu.SemaphoreType.DMA((n_r, 3)), pltpu.SemaphoreType.DMA((n_r, 3)),
           pltpu.SemaphoreType.DMA((n_r, 3)), pltpu.SemaphoreType.DMA((n_r, 3))],
        args=tuple(gath) + tuple(red), carry=carry, body_starts_carry=True)
    return (outs[:n_g], outs[n_g:]), carried


STACKED = "stacked"


def _region(ref, shard_shape, col_sharded, pos):
    r, cdim = shard_shape
    d = _dev_index(pos)
    if col_sharded == STACKED:
        return ref.at[d]
    if col_sharded:
        return ref.at[:, pl.ds(pl.multiple_of(d * cdim, LANES), cdim)]
    return ref.at[pl.ds(pl.multiple_of(d * r, 2 * SUBLANES), r), :]


def _gather_carry(shards, col_sharded):
    n_w = len(shards)
    shapes = [tuple(s.shape) for s in shards]
    full_shapes = [(N_DEV,) + s if cs == STACKED else (s[0], s[1] * N_DEV) if cs else (s[0] * N_DEV, s[1])
                   for s, cs in zip(shapes, col_sharded)]

    def tools(out_refs, scr):
        send_sems, recv_sems = scr[n_w], scr[n_w + 1]
        me = _my_pos()
        x, y, c = me
        sibling = (x, y, 1 - c)
        chips = [(1 - x, y), (x, 1 - y), (1 - x, 1 - y)]

        def region(w, pos):
            return _region(out_refs[w], shapes[w], col_sharded[w], pos)

        def copy(w, k, block, to, src=None):
            return pltpu.make_async_remote_copy(
                src_ref=region(w, block) if src is None else src, dst_ref=region(w, block),
                send_sem=send_sems.at[w, k], recv_sem=recv_sems.at[w, k], device_id=to, device_id_type=MESH)

        def first(w):
            return [copy(w, 0, me, sibling, src=scr[w])] + [
                copy(w, 1 + j, me, (*chip, c), src=scr[w]) for j, chip in enumerate(chips)]

        def mine(w):
            return pltpu.make_async_copy(scr[w], region(w, me), scr[n_w + 2].at[w])

        return me, c, sibling, chips, copy, first, mine

    def start(ins, outs, scr):
        _, _, _, _, _, first, mine = tools(outs, scr)
        for w in range(n_w):
            scr[w][...] = ins[w][...].astype(BF16)
            for cp in first(w) + [mine(w)]:
                cp.start()

    def finish(ins, outs, scr):
        me, c, sibling, chips, copy, first, mine = tools(outs, scr)
        passed = []
        for w in range(n_w):
            for j, chip in enumerate(chips):
                copy(w, 1 + j, (*chip, c), me).wait_recv()
                fwd = copy(w, 4 + j, (*chip, c), sibling)
                fwd.start()
                passed.append(fwd)
        for w in range(n_w):
            copy(w, 0, sibling, me).wait_recv()
            for j, chip in enumerate(chips):
                copy(w, 4 + j, (*chip, 1 - c), me).wait_recv()
        for w in range(n_w):
            for cp in first(w):
                cp.wait_send()
            mine(w).wait()
        for cp in passed:
            cp.wait_send()

    return _Carry(
        inputs=list(shards), in_specs=[_whole()] * n_w,
        out_shape=[_sds(s, BF16) for s in full_shapes], out_specs=[_any()] * n_w,
        scratch=[pltpu.VMEM(s, BF16) for s in shapes]
        + [pltpu.SemaphoreType.DMA((n_w, N_DEV - 1)), pltpu.SemaphoreType.DMA((n_w, N_DEV - 1)),
           pltpu.SemaphoreType.DMA((n_w,))],
        start=start, finish=finish)


CHIP_FLIPS = (4, 2, 6)


def _two_level_scatter_carry(g_bf, g_own, col_sharded, mid_step):
    shape = tuple(g_own.shape)
    n = len(CHIP_FLIPS)

    def pair_copies(ins, scr):
        mine, sib, _, send_sems, recv_sems, local_sems = scr[:6]
        me = _my_pos()
        sibling = _flip(me, 1)

        def region(pos):
            return _region(ins[0], shape, col_sharded, pos)

        local = [pltpu.make_async_copy(region(_flip(me, f)), mine.at[s], local_sems.at[s])
                 for s, f in enumerate(CHIP_FLIPS)]
        sends = [pltpu.make_async_remote_copy(
            src_ref=region(_flip(sibling, f)), dst_ref=sib.at[s], send_sem=send_sems.at[s], recv_sem=recv_sems.at[s],
            device_id=sibling, device_id_type=MESH) for s, f in enumerate((0,) + CHIP_FLIPS)]
        return local, sends

    def chip_copies(outs, scr):
        h_out, chip_send, chip_recv = scr[2], scr[6], scr[7]
        me = _my_pos()
        return [pltpu.make_async_remote_copy(
            src_ref=h_out.at[j], dst_ref=outs[1].at[j], send_sem=chip_send.at[j], recv_sem=chip_recv.at[j],
            device_id=_flip(me, CHIP_FLIPS[j]), device_id_type=MESH) for j in range(n)]

    def start(ins, outs, scr):
        local, sends = pair_copies(ins, scr)
        for cp in local + sends:
            cp.start()

    def mid(ins, outs, scr):
        mine, sib, h_out = scr[:3]
        local, sends = pair_copies(ins, scr)
        for cp in local:
            cp.wait()
        for cp in sends:
            cp.wait_recv()
        outs[0][...] = ins[1][...] + sib[0].astype(F32)
        for s in range(n):
            h_out[s] = (mine[s].astype(F32) + sib[s + 1].astype(F32)).astype(BF16)
        for cp in chip_copies(outs, scr):
            cp.start()

    def finish(ins, outs, scr):
        cps = chip_copies(outs, scr)
        for cp in cps:
            cp.wait_recv()
        for cp in cps + pair_copies(ins, scr)[1]:
            cp.wait_send()

    return _Carry(
        inputs=[g_bf, g_own], in_specs=[_any(), _whole()],
        out_shape=[_sds(shape, F32), _sds((n,) + shape, BF16)], out_specs=[_whole(), _any()],
        scratch=[pltpu.VMEM((n,) + shape, BF16), pltpu.VMEM((n + 1,) + shape, BF16), pltpu.VMEM((n,) + shape, BF16),
                 pltpu.SemaphoreType.DMA((n + 1,)), pltpu.SemaphoreType.DMA((n + 1,)), pltpu.SemaphoreType.DMA((n,)),
                 pltpu.SemaphoreType.DMA((n,)), pltpu.SemaphoreType.DMA((n,))],
        start=start, finish=finish, mid=(mid_step, mid))


def _scatter_carry(grads_bf, shard_shapes, col_sharded, relations):
    n_w = len(grads_bf)
    shapes = [tuple(s) for s in shard_shapes]

    def copies(ins, outs, scr):
        send_sems, recv_sems = scr
        me = _my_pos()
        out = []
        for w in range(n_w):
            for i, k in enumerate(relations[w]):
                peer = _flip(me, k)
                out.append(pltpu.make_async_remote_copy(
                    src_ref=_region(ins[w], shapes[w], col_sharded[w], peer), dst_ref=outs[w].at[i],
                    send_sem=send_sems.at[w, i], recv_sem=recv_sems.at[w, i],
                    device_id=peer, device_id_type=MESH))
        return out

    def start(ins, outs, scr):
        for cp in copies(ins, outs, scr):
            cp.start()

    def finish(ins, outs, scr):
        cps = copies(ins, outs, scr)
        for cp in cps:
            cp.wait_recv()
        for cp in cps:
            cp.wait_send()

    return _Carry(
        inputs=list(grads_bf), in_specs=[_any()] * n_w,
        out_shape=[_sds((len(r),) + s, BF16) for r, s in zip(relations, shapes)], out_specs=[_any()] * n_w,
        scratch=[pltpu.SemaphoreType.DMA((n_w, N_DEV - 1)), pltpu.SemaphoreType.DMA((n_w, N_DEV - 1))],
        start=start, finish=finish)


def _block_diag(w):
    eye = jnp.eye(N_HEADS, dtype=w.dtype)
    return (eye[:, None, :, None] * w[:, :, None, :]).reshape(N_HEADS * HEAD_DIM, N_HEADS * HEAD_DIM)


def _local_step(x2, target, mod, w_in_f, w_full, conv_w_full, ffn_cw_full,
                g_mix_pre, g_mix_post, conv_b, w_rgate, b_rgate, w_igate, b_igate, lru_a, v_norm_g, v_norm_b,
                w_spatial, b_spatial, g_lru_out, g_gmlp_out, g_ffn_pre, g_ffn_post, ffn_conv_b,
                gather=None, scatter=None, adam=None, gate_bd=None):
    sh_m, sc_m, gt_m, sh_f, sc_f, gt_f = [mod[k] for k in range(N_MOD)]
    if gate_bd:
        wr_bd, wi_bd, b_r, b_i = gate_bd
    else:
        wr_bd, wi_bd = [_block_diag(w[0]).astype(BF16) for w in (w_rgate, w_igate)]
        b_r, b_i = b_rgate.reshape(1, LRU_W), b_igate.reshape(1, LRU_W)
    b_sp_t = b_spatial[0].T
    w_sp_t = jnp.swapaxes(w_spatial[0], 1, 2)

    def arriving(*names):
        return gather(*names) if gather else None

    near, far = (1, 2, 3, 4, 5), (6, 7)

    def leaving(*parts):
        return scatter(parts) if scatter else None

    def received(recv, parts, outs):
        for (name, _, _), out in zip(parts, outs):
            recv.setdefault(name, []).append(out)

    mix_params = (conv_w_full, conv_b, wr_bd, wi_bd, b_r, b_i, lru_a, v_norm_g, v_norm_b)
    w_out_f = w_full["w_out"]
    (z, h, ycat, hl, y, x1, h2), got = _mix_fwd(
        x2, sh_m, sc_m, g_mix_pre, w_in_f, *mix_params, w_spatial[0], b_sp_t, g_lru_out, g_gmlp_out,
        w_out_f, g_mix_post, gt_m, g_ffn_pre, sc_f, sh_f, carry=arriving("w_up"))
    w_up_f = got[0] if gather else w_full["w_up"]
    (up_pre, up, act), got = _ffn_fwd(h2, w_up_f, ffn_cw_full, ffn_conv_b, carry=arriving("w_down"))
    w_down_f = got[0] if gather else w_full["w_down"]
    d_y2, dout, loss_acc, vs_ffn = _ffn_tail(act, w_down_f, x1, gt_f, g_ffn_post, target)

    recv, updated = {}, {}

    def updating(grads):
        if not adam:
            return None
        return _adamw_rider([(adam[n][0], g[1], recv[n], adam[n][1], adam[n][2]) for n, g in grads.items()], N_DEV)

    def updates(grads, outs):
        for j, n in enumerate(grads):
            updated[n] = tuple(outs[4 * j:4 * j + 4])

    gw_down, _ = _wgrad(act, d_y2, "wgrad_down", by_rows=True)
    parts = [("w_down", gw_down[0], near + far)]
    (d_up, cs_ffn), got = _ffn_bwd(d_y2, up_pre, up, ffn_cw_full, w_down_f, carry=leaving(*parts))
    received(recv, parts, got)
    gw_up, got = _wgrad(h2, d_up, "wgrad_up", carry=updating(dict(w_down=gw_down)))
    updates(dict(w_down=gw_down), got)
    parts = [("w_up", gw_up[0], near)]
    (d_x1, d_y, d_ycat, vs_up), got = _up_bwd(
        d_up, w_up_f, x1, dout, y, w_out_f, g_ffn_pre, sc_f, g_mix_post, gt_m, carry=leaving(*parts))
    received(recv, parts, got)
    gw_out, _ = _wgrad(ycat, d_y, "wgrad_out", by_rows=True)
    parts = [("w_up", gw_up[0], far), ("w_out", gw_out[0], near + far)]
    (d_z, vs_mix, dcw, d_wr, d_wi, d_ws, d_bs), got = _mix_bwd(
        d_ycat, z, hl, *mix_params, w_spatial[0], w_sp_t, b_sp_t, g_lru_out, g_gmlp_out, carry=leaving(*parts))
    received(recv, parts, got)
    gw_in, got = _wgrad(h, d_z, "wgrad_in", carry=updating(dict(w_up=gw_up, w_out=gw_out)))
    updates(dict(w_up=gw_up, w_out=gw_out), got)
    in_bwd_steps = x2.shape[0] // min(TT_BIG, x2.shape[0])
    two_level = _two_level_scatter_carry(gw_in[0], gw_in[1], True, min(1, in_bwd_steps - 1)) if scatter else None
    (grad_x, vs_in), got = _in_bwd(d_z, w_in_f, x2, d_x1, g_mix_pre, sc_m, carry=two_level)
    if scatter:
        gw_in = (gw_in[0], got[0])
    recv["w_in"] = list(got[1:])

    gath = [vs_in, vs_up, vs_ffn, loss_acc]
    red = [cs_ffn, vs_mix, dcw, d_wr, d_wi, d_ws.reshape(N_GROUPS * POS_BLOCK, POS_BLOCK), d_bs]
    return dict(grad_x=grad_x, gath=gath, red=red, recv=recv, updated=updated,
                w_in=gw_in, w_out=gw_out, w_up=gw_up, w_down=gw_down)


def kernel(x, c, w_ada, b_ada, g_mix_pre, g_mix_post, w_in, conv_w, conv_b, w_rgate, b_rgate, w_igate, b_igate, lru_a, v_norm_g, v_norm_b, w_spatial, b_spatial, g_lru_out, g_gmlp_out, w_out, g_ffn_pre, g_ffn_post, w_up, ffn_conv_w, ffn_conv_b, w_down, loss_target, m_w_ada, m_b_ada, m_g_mix_pre, m_g_mix_post, m_w_in, m_conv_w, m_conv_b, m_w_rgate, m_b_rgate, m_w_igate, m_b_igate, m_lru_a, m_v_norm_g, m_v_norm_b, m_w_spatial, m_b_spatial, m_g_lru_out, m_g_gmlp_out, m_w_out, m_g_ffn_pre, m_g_ffn_post, m_w_up, m_ffn_conv_w, m_ffn_conv_b, m_w_down, v_w_ada, v_b_ada, v_g_mix_pre, v_g_mix_post, v_w_in, v_conv_w, v_conv_b, v_w_rgate, v_b_rgate, v_w_igate, v_b_igate, v_lru_a, v_v_norm_g, v_v_norm_b, v_w_spatial, v_b_spatial, v_g_lru_out, v_g_gmlp_out, v_w_out, v_g_ffn_pre, v_g_ffn_post, v_w_up, v_ffn_conv_w, v_ffn_conv_b, v_w_down):
    big_w = dict(w_in=(w_in, m_w_in, v_w_in, True), w_out=(w_out, m_w_out, v_w_out, False),
                 w_up=(w_up, m_w_up, v_w_up, True), w_down=(w_down, m_w_down, v_w_down, False))

    def gather(*names):
        return _gather_carry([big_w[n][0][0] for n in names], [STACKED if n == "w_up" else big_w[n][3] for n in names])

    def scatter(parts):
        return _scatter_carry([g for _, g, _ in parts], [big_w[n][0].shape[1:] for n, _, _ in parts],
                              [big_w[n][3] for n, _, _ in parts], [rel for _, _, rel in parts])

    ffn_cw_taps = tuple(a.reshape(FFN_CONV_K, 1, -1) for a in (ffn_conv_w, m_ffn_conv_w, v_ffn_conv_w))
    (c_all, conv_w_full, ffn_cw_full, *gate_bd, sh_m, sc_m, gt_m, sh_f, sc_f, gt_f), (w_in_f, w_out_f) = _prologue(
        c, conv_w[0], ffn_cw_taps[0], w_ada[0], b_ada, w_rgate[0], w_igate[0], b_rgate[0], b_igate[0],
        carry=gather("w_in", "w_out"))
    mod = (sh_m, sc_m, gt_m, sh_f, sc_f, gt_f)

    loc = _local_step(x[0], loss_target[0], mod, w_in_f, dict(w_out=w_out_f), conv_w_full, ffn_cw_full,
                      g_mix_pre, g_mix_post, conv_b, w_rgate, b_rgate, w_igate, b_igate, lru_a, v_norm_g, v_norm_b,
                      w_spatial, b_spatial, g_lru_out, g_gmlp_out, g_ffn_pre, g_ffn_post, ffn_conv_b,
                      gather=gather, scatter=scatter,
                      adam={n: big_w[n][:3] for n in ("w_out", "w_up", "w_down")}, gate_bd=gate_bd)
    grad_x = loc["grad_x"]

    (gathered, reduced), _ = _reduce_small(loc["gath"], loc["red"])

    results = dict(loc["updated"])
    w_, m_, v_, _ = big_w["w_in"]
    results["w_in"] = _adamw_sum(w_, loc["w_in"][1], loc["recv"]["w_in"], m_, v_, "adamw_w_in")

    params = dict(
        b_ada=(b_ada, m_b_ada, v_b_ada), g_mix_pre=(g_mix_pre, m_g_mix_pre, v_g_mix_pre),
        g_mix_post=(g_mix_post, m_g_mix_post, v_g_mix_post), conv_b=(conv_b, m_conv_b, v_conv_b),
        w_rgate=(w_rgate, m_w_rgate, v_w_rgate), b_rgate=(b_rgate, m_b_rgate, v_b_rgate),
        w_igate=(w_igate, m_w_igate, v_w_igate), b_igate=(b_igate, m_b_igate, v_b_igate),
        lru_a=(lru_a, m_lru_a, v_lru_a), v_norm_g=(v_norm_g, m_v_norm_g, v_v_norm_g),
        v_norm_b=(v_norm_b, m_v_norm_b, v_v_norm_b), w_spatial=(w_spatial, m_w_spatial, v_w_spatial),
        b_spatial=(b_spatial, m_b_spatial, v_b_spatial), g_lru_out=(g_lru_out, m_g_lru_out, v_g_lru_out),
        g_gmlp_out=(g_gmlp_out, m_g_gmlp_out, v_g_gmlp_out), g_ffn_pre=(g_ffn_pre, m_g_ffn_pre, v_g_ffn_pre),
        g_ffn_post=(g_ffn_post, m_g_ffn_post, v_g_ffn_post), ffn_conv_b=(ffn_conv_b, m_ffn_conv_b, v_ffn_conv_b))
    conv_params = dict(conv_w=(conv_w, m_conv_w, v_conv_w), ffn_conv_w=ffn_cw_taps)
    small_results, loss = _adamw_small(gathered, reduced, params, conv_params)
    results.update(small_results)
    results["ffn_conv_w"] = tuple(a.reshape(ffn_conv_w.shape) for a in results["ffn_conv_w"])
    loss = loss.reshape(())

    results["w_ada"] = _adamw_wada(c_all, gathered[0], gathered[1], gathered[2], w_ada, m_w_ada, v_w_ada)

    order = ["w_ada", "b_ada", "g_mix_pre", "g_mix_post", "w_in", "conv_w", "conv_b", "w_rgate", "b_rgate", "w_igate",
             "b_igate", "lru_a", "v_norm_g", "v_norm_b", "w_spatial", "b_spatial", "g_lru_out", "g_gmlp_out", "w_out",
             "g_ffn_pre", "g_ffn_post", "w_up", "ffn_conv_w", "ffn_conv_b", "w_down"]
    outs = [loss, grad_x[None]]
    for kind in range(4):
        outs += [results[n][kind] for n in order]
    return tuple(outs)
```
